```python
import jax, jax.numpy as jnp
from jax import lax
import numpy as np

D_MODEL = 1024
BATCH = 16
SEQ = 2048
DEPTH = 1

HEAD_DIM = 64
ATTN_WIDTH = D_MODEL // 2
CONV_WIDTH = D_MODEL - ATTN_WIDTH
N_Q_HEADS = ATTN_WIDTH // HEAD_DIM
N_KV_HEADS = 2
GQA_GROUP = N_Q_HEADS // N_KV_HEADS
KV_WIDTH = N_KV_HEADS * HEAD_DIM
WINDOW = 128
BLOCK = 128
ROT_DIM = HEAD_DIM // 4
ROPE_THETA = 500000.0
CONV_W = 3
D_FF = 2816
N_MOD = 9
LN_EPS = 1e-5
DN_ALPHA = (2.0 * DEPTH) ** 0.25
DN_BETA = (8.0 * DEPTH) ** -0.25
IN_WIDTH = ATTN_WIDTH + 2 * KV_WIDTH + 3 * CONV_WIDTH

kernel_name = "hybrid_swa_sink_shortconv_macaron_deepnorm_adaln"


def layer_norm(x, g, b):
    xf = x.astype(jnp.float32)
    mu = jnp.mean(xf, axis=-1, keepdims=True)
    var = jnp.mean(jnp.square(xf - mu), axis=-1, keepdims=True)
    y = (xf - mu) * lax.rsqrt(var + LN_EPS) * g.astype(jnp.float32) + b.astype(jnp.float32)
    return y.astype(x.dtype)


def swiglu(h, w_gate_up, w_down):
    gu = h @ w_gate_up
    g, u = jnp.split(gu, 2, axis=-1)
    return (jax.nn.silu(g) * u) @ w_down


def partial_rope(x, positions):
    half = ROT_DIM // 2
    inv_freq = jnp.power(jnp.float32(ROPE_THETA), -jnp.arange(0, ROT_DIM, 2, dtype=jnp.float32) / ROT_DIM)
    ang = positions.astype(jnp.float32)[..., None] * inv_freq
    cos = jnp.cos(ang)[:, :, None, :].astype(x.dtype)
    sin = jnp.sin(ang)[:, :, None, :].astype(x.dtype)
    x1 = x[..., :half]
    x2 = x[..., half:ROT_DIM]
    rest = x[..., ROT_DIM:]
    return jnp.concatenate([x1 * cos - x2 * sin, x2 * cos + x1 * sin, rest], axis=-1)


def sliding_window_sink_attention(q, k, v, sinks):
    bsz, seq = q.shape[0], q.shape[1]
    nb = seq // BLOCK
    qb = q.reshape(bsz, nb, BLOCK, N_KV_HEADS, GQA_GROUP, HEAD_DIM)
    pad = ((0, 0), (BLOCK, 0), (0, 0), (0, 0))
    kp = jnp.pad(k, pad).reshape(bsz, nb + 1, BLOCK, N_KV_HEADS, HEAD_DIM)
    vp = jnp.pad(v, pad).reshape(bsz, nb + 1, BLOCK, N_KV_HEADS, HEAD_DIM)
    kb = jnp.concatenate([kp[:, :-1], kp[:, 1:]], axis=2)
    vb = jnp.concatenate([vp[:, :-1], vp[:, 1:]], axis=2)
    scale = HEAD_DIM ** -0.5
    scores = jnp.einsum('bnqhgd,bnkhd->bnhgqk', qb, kb).astype(jnp.float32) * scale
    blk = jnp.arange(nb)[:, None, None]
    qi = jnp.arange(BLOCK)[None, :, None]
    ki = jnp.arange(2 * BLOCK)[None, None, :]
    diff = qi + BLOCK - ki
    key_pos = (blk - 1) * BLOCK + ki
    valid = (diff >= 0) & (diff < WINDOW) & (key_pos >= 0)
    scores = jnp.where(valid[None, :, None, None], scores, -jnp.inf)
    sink = jnp.broadcast_to(
        sinks.astype(jnp.float32).reshape(N_KV_HEADS, GQA_GROUP)[None, None, :, :, None, None],
        scores.shape[:-1] + (1,))
    probs = jax.nn.softmax(jnp.concatenate([scores, sink], axis=-1), axis=-1)[..., :-1]
    out = jnp.einsum('bnhgqk,bnkhd->bnqhgd', probs.astype(v.dtype), vb)
    return out.reshape(bsz, seq, N_Q_HEADS * HEAD_DIM)


def short_gated_conv(u, b_gate, c_gate, conv_w):
    seq = u.shape[1]
    z = c_gate * u
    zp = jnp.pad(z, ((0, 0), (CONV_W - 1, 0), (0, 0)))
    y = conv_w[0] * zp[:, 0:seq]
    for tap in range(1, CONV_W):
        y = y + conv_w[tap] * zp[:, tap:tap + seq]
    return b_gate * y


def _fwd_setup_inputs(seed: int = 0) -> dict:
    key = jax.random.key(seed)
    ks = jax.random.split(key, 20)
    nrm = lambda k, shape, s: jax.random.normal(k, shape, jnp.float32) * s
    L, D = DEPTH, D_MODEL
    x = jax.random.normal(ks[0], (BATCH, SEQ, D), jnp.float32)
    c = jax.random.normal(ks[1], (BATCH, D), jnp.float32)
    offsets = jax.random.randint(ks[2], (BATCH, 1), 0, 1024, dtype=jnp.int32)
    positions = offsets + jnp.arange(SEQ, dtype=jnp.int32)[None, :]
    return {
        "x": x,
        "c": c,
        "positions": positions,
        "w_ada": nrm(ks[3], (L, D, N_MOD * D), 0.1 * D ** -0.5),
        "b_ada": nrm(ks[4], (L, N_MOD * D), 0.01),
        "ffn1_w_gate_up": nrm(ks[5], (L, D, 2 * D_FF), D ** -0.5),
        "ffn1_w_down": nrm(ks[6], (L, D_FF, D), DN_BETA * D_FF ** -0.5),
        "ln1_g": 1.0 + nrm(ks[7], (L, D), 0.02),
        "ln1_b": nrm(ks[8], (L, D), 0.02),
        "w_in": nrm(ks[9], (L, D, IN_WIDTH), D ** -0.5),
        "conv_w": nrm(ks[10], (L, CONV_W, CONV_WIDTH), CONV_W ** -0.5),
        "attn_sinks": nrm(ks[11], (L, N_Q_HEADS), 1.0),
        "w_out": nrm(ks[12], (L, D, D), DN_BETA * D ** -0.5),
        "ln2_g": 1.0 + nrm(ks[13], (L, D), 0.02),
        "ln2_b": nrm(ks[14], (L, D), 0.02),
        "ffn2_w_gate_up": nrm(ks[15], (L, D, 2 * D_FF), D ** -0.5),
        "ffn2_w_down": nrm(ks[16], (L, D_FF, D), DN_BETA * D_FF ** -0.5),
        "ln3_g": 1.0 + nrm(ks[17], (L, D), 0.02),
        "ln3_b": nrm(ks[18], (L, D), 0.02),
    }


def _fwd_reference(x, c, positions, w_ada, b_ada, ffn1_w_gate_up, ffn1_w_down, ln1_g, ln1_b,
              w_in, conv_w, attn_sinks, w_out, ln2_g, ln2_b,
              ffn2_w_gate_up, ffn2_w_down, ln3_g, ln3_b):
    bsz, seq = x.shape[0], x.shape[1]
    split_at = [ATTN_WIDTH, ATTN_WIDTH + KV_WIDTH, ATTN_WIDTH + 2 * KV_WIDTH,
                ATTN_WIDTH + 2 * KV_WIDTH + CONV_WIDTH, ATTN_WIDTH + 2 * KV_WIDTH + 2 * CONV_WIDTH]
    cond = jax.nn.silu(c)
    for l in range(DEPTH):
        mod = (cond @ w_ada[l] + b_ada[l])[:, None, :]
        sh1, sc1, g1, sh2, sc2, g2, sh3, sc3, g3 = jnp.split(mod, N_MOD, axis=-1)

        h = x * (1 + sc1) + sh1
        x = layer_norm(DN_ALPHA * x + 0.5 * (1 + g1) * swiglu(h, ffn1_w_gate_up[l], ffn1_w_down[l]),
                       ln1_g[l], ln1_b[l])

        h = x * (1 + sc2) + sh2
        proj = h @ w_in[l]
        q, k, v, u, b_gate, c_gate = jnp.split(proj, split_at, axis=-1)
        q = partial_rope(q.reshape(bsz, seq, N_Q_HEADS, HEAD_DIM), positions)
        k = partial_rope(k.reshape(bsz, seq, N_KV_HEADS, HEAD_DIM), positions)
        v = v.reshape(bsz, seq, N_KV_HEADS, HEAD_DIM)
        attn_out = sliding_window_sink_attention(q, k, v, attn_sinks[l])
        conv_out = short_gated_conv(u, b_gate, c_gate, conv_w[l])
        mix = jnp.concatenate([attn_out, conv_out], axis=-1) @ w_out[l]
        x = layer_norm(DN_ALPHA * x + (1 + g2) * mix, ln2_g[l], ln2_b[l])

        h = x * (1 + sc3) + sh3
        x = layer_norm(DN_ALPHA * x + 0.5 * (1 + g3) * swiglu(h, ffn2_w_gate_up[l], ffn2_w_down[l]),
                       ln3_g[l], ln3_b[l])
    return x


import jax as _jax
import jax.numpy as _jnp

TWIN_FORMAT = 'train_step'
FWD_PARAMS = ['x', 'c', 'positions', 'w_ada', 'b_ada', 'ffn1_w_gate_up', 'ffn1_w_down', 'ln1_g', 'ln1_b', 'w_in', 'conv_w', 'attn_sinks', 'w_out', 'ln2_g', 'ln2_b', 'ffn2_w_gate_up', 'ffn2_w_down', 'ln3_g', 'ln3_b']
TWIN_WEIGHTS = ['w_ada', 'b_ada', 'ffn1_w_gate_up', 'ffn1_w_down', 'ln1_g', 'ln1_b', 'w_in', 'conv_w', 'attn_sinks', 'w_out', 'ln2_g', 'ln2_b', 'ffn2_w_gate_up', 'ffn2_w_down', 'ln3_g', 'ln3_b']
TWIN_DIFF_INPUT = 'x'
TWIN_INPUTS = ['x', 'c', 'positions', 'w_ada', 'b_ada', 'ffn1_w_gate_up', 'ffn1_w_down', 'ln1_g', 'ln1_b', 'w_in', 'conv_w', 'attn_sinks', 'w_out', 'ln2_g', 'ln2_b', 'ffn2_w_gate_up', 'ffn2_w_down', 'ln3_g', 'ln3_b', 'loss_target', 'm_w_ada', 'm_b_ada', 'm_ffn1_w_gate_up', 'm_ffn1_w_down', 'm_ln1_g', 'm_ln1_b', 'm_w_in', 'm_conv_w', 'm_attn_sinks', 'm_w_out', 'm_ln2_g', 'm_ln2_b', 'm_ffn2_w_gate_up', 'm_ffn2_w_down', 'm_ln3_g', 'm_ln3_b', 'v_w_ada', 'v_b_ada', 'v_ffn1_w_gate_up', 'v_ffn1_w_down', 'v_ln1_g', 'v_ln1_b', 'v_w_in', 'v_conv_w', 'v_attn_sinks', 'v_w_out', 'v_ln2_g', 'v_ln2_b', 'v_ffn2_w_gate_up', 'v_ffn2_w_down', 'v_ln3_g', 'v_ln3_b']
TWIN_OUTPUTS = ['loss', 'grad_x', 'grad_w_ada', 'grad_b_ada', 'grad_ffn1_w_gate_up', 'grad_ffn1_w_down', 'grad_ln1_g', 'grad_ln1_b', 'grad_w_in', 'grad_conv_w', 'grad_attn_sinks', 'grad_w_out', 'grad_ln2_g', 'grad_ln2_b', 'grad_ffn2_w_gate_up', 'grad_ffn2_w_down', 'grad_ln3_g', 'grad_ln3_b', 'delta_w_ada', 'delta_b_ada', 'delta_ffn1_w_gate_up', 'delta_ffn1_w_down', 'delta_ln1_g', 'delta_ln1_b', 'delta_w_in', 'delta_conv_w', 'delta_attn_sinks', 'delta_w_out', 'delta_ln2_g', 'delta_ln2_b', 'delta_ffn2_w_gate_up', 'delta_ffn2_w_down', 'delta_ln3_g', 'delta_ln3_b', 'new_m_w_ada', 'new_m_b_ada', 'new_m_ffn1_w_gate_up', 'new_m_ffn1_w_down', 'new_m_ln1_g', 'new_m_ln1_b', 'new_m_w_in', 'new_m_conv_w', 'new_m_attn_sinks', 'new_m_w_out', 'new_m_ln2_g', 'new_m_ln2_b', 'new_m_ffn2_w_gate_up', 'new_m_ffn2_w_down', 'new_m_ln3_g', 'new_m_ln3_b', 'new_v_w_ada', 'new_v_b_ada', 'new_v_ffn1_w_gate_up', 'new_v_ffn1_w_down', 'new_v_ln1_g', 'new_v_ln1_b', 'new_v_w_in', 'new_v_conv_w', 'new_v_attn_sinks', 'new_v_w_out', 'new_v_ln2_g', 'new_v_ln2_b', 'new_v_ffn2_w_gate_up', 'new_v_ffn2_w_down', 'new_v_ln3_g', 'new_v_ln3_b']
TWIN_LEAF_KINDS = {'loss': 'loss', 'grad_x': 'grad_x', 'grad_w_ada': 'grad_w', 'grad_b_ada': 'grad_w', 'grad_ffn1_w_gate_up': 'grad_w', 'grad_ffn1_w_down': 'grad_w', 'grad_ln1_g': 'grad_w', 'grad_ln1_b': 'grad_w', 'grad_w_in': 'grad_w', 'grad_conv_w': 'grad_w', 'grad_attn_sinks': 'grad_w', 'grad_w_out': 'grad_w', 'grad_ln2_g': 'grad_w', 'grad_ln2_b': 'grad_w', 'grad_ffn2_w_gate_up': 'grad_w', 'grad_ffn2_w_down': 'grad_w', 'grad_ln3_g': 'grad_w', 'grad_ln3_b': 'grad_w', 'delta_w_ada': 'delta_w', 'delta_b_ada': 'delta_w', 'delta_ffn1_w_gate_up': 'delta_w', 'delta_ffn1_w_down': 'delta_w', 'delta_ln1_g': 'delta_w', 'delta_ln1_b': 'delta_w', 'delta_w_in': 'delta_w', 'delta_conv_w': 'delta_w', 'delta_attn_sinks': 'delta_w', 'delta_w_out': 'delta_w', 'delta_ln2_g': 'delta_w', 'delta_ln2_b': 'delta_w', 'delta_ffn2_w_gate_up': 'delta_w', 'delta_ffn2_w_down': 'delta_w', 'delta_ln3_g': 'delta_w', 'delta_ln3_b': 'delta_w', 'new_m_w_ada': 'new_m', 'new_m_b_ada': 'new_m', 'new_m_ffn1_w_gate_up': 'new_m', 'new_m_ffn1_w_down': 'new_m', 'new_m_ln1_g': 'new_m', 'new_m_ln1_b': 'new_m', 'new_m_w_in': 'new_m', 'new_m_conv_w': 'new_m', 'new_m_attn_sinks': 'new_m', 'new_m_w_out': 'new_m', 'new_m_ln2_g': 'new_m', 'new_m_ln2_b': 'new_m', 'new_m_ffn2_w_gate_up': 'new_m', 'new_m_ffn2_w_down': 'new_m', 'new_m_ln3_g': 'new_m', 'new_m_ln3_b': 'new_m', 'new_v_w_ada': 'new_v', 'new_v_b_ada': 'new_v', 'new_v_ffn1_w_gate_up': 'new_v', 'new_v_ffn1_w_down': 'new_v', 'new_v_ln1_g': 'new_v', 'new_v_ln1_b': 'new_v', 'new_v_w_in': 'new_v', 'new_v_conv_w': 'new_v', 'new_v_attn_sinks': 'new_v', 'new_v_w_out': 'new_v', 'new_v_ln2_g': 'new_v', 'new_v_ln2_b': 'new_v', 'new_v_ffn2_w_gate_up': 'new_v', 'new_v_ffn2_w_down': 'new_v', 'new_v_ln3_g': 'new_v', 'new_v_ln3_b': 'new_v'}


def _forward(args):
    return _fwd_reference(*[args[k] for k in FWD_PARAMS])


def _output_shape():
    out = _jax.eval_shape(lambda: _forward(_fwd_setup_inputs(0)))
    return out.shape, out.dtype

N_MICROBATCH = 1
ADAM_LR = 0.001
ADAM_B1 = 0.9
ADAM_B2 = 0.999
ADAM_EPS = 1e-08
ADAM_WD = 0.01
ADAM_STEP = 10
PER_EXAMPLE_BATCH_AXIS = {'x': 0, 'c': 0, 'positions': 0, 'loss_target': 0}
SHARED_INPUTS = []
_WEIGHT_DTYPES = {'w_ada': _jnp.float32, 'b_ada': _jnp.float32, 'ffn1_w_gate_up': _jnp.float32, 'ffn1_w_down': _jnp.float32, 'ln1_g': _jnp.float32, 'ln1_b': _jnp.float32, 'w_in': _jnp.float32, 'conv_w': _jnp.float32, 'attn_sinks': _jnp.float32, 'w_out': _jnp.float32, 'ln2_g': _jnp.float32, 'ln2_b': _jnp.float32, 'ffn2_w_gate_up': _jnp.float32, 'ffn2_w_down': _jnp.float32, 'ln3_g': _jnp.float32, 'ln3_b': _jnp.float32}
MOMENT_SCALE = {'w_ada': 4.328922e-02, 'b_ada': 7.845390e-02, 'ffn1_w_gate_up': 1.802523e-02, 'ffn1_w_down': 4.950756e-02, 'ln1_g': 8.372484e-01, 'ln1_b': 4.238668e-01, 'w_in': 7.010049e-02, 'conv_w': 8.652634e-02, 'attn_sinks': 1.502222e-02, 'w_out': 1.030868e-01, 'ln2_g': 1.014116e+00, 'ln2_b': 4.497578e-01, 'ffn2_w_gate_up': 1.619725e-02, 'ffn2_w_down': 4.448512e-02, 'ln3_g': 3.204666e+01, 'ln3_b': 8.520681e-01}


def _to_microbatches(a, axis):
    t = _jnp.moveaxis(a, axis, 0)
    t = t.reshape((N_MICROBATCH, t.shape[0] // N_MICROBATCH) + t.shape[1:])
    return _jnp.moveaxis(t, 1, axis + 1)


def setup_inputs(seed: int = 0) -> dict:
    inp = _fwd_setup_inputs(seed)
    key = _jax.random.fold_in(_jax.random.key(seed), 7919)
    shape, _ = _output_shape()
    out = dict(inp)
    out["loss_target"] = _jax.random.normal(_jax.random.fold_in(key, 0), shape, _jnp.float32)
    for i, name in enumerate(TWIN_WEIGHTS):
        w = inp[name].astype(_jnp.float32)
        if MOMENT_SCALE is None:
            s = _jnp.sqrt(_jnp.mean(_jnp.square(w)) + 1e-30)
        else:
            s = MOMENT_SCALE[name]
        km, kv = _jax.random.split(_jax.random.fold_in(key, i + 1))
        out[name] = w
        out["m_" + name] = s * _jax.random.normal(km, w.shape, _jnp.float32)
        out["v_" + name] = (s * s) * _jax.random.uniform(kv, w.shape, _jnp.float32, 0.5, 1.5)
    if N_MICROBATCH > 1:
        for name, axis in PER_EXAMPLE_BATCH_AXIS.items():
            out[name] = _to_microbatches(out[name], axis)
    return {'x': out['x'], 'c': out['c'], 'positions': out['positions'], 'w_ada': out['w_ada'], 'b_ada': out['b_ada'], 'ffn1_w_gate_up': out['ffn1_w_gate_up'], 'ffn1_w_down': out['ffn1_w_down'], 'ln1_g': out['ln1_g'], 'ln1_b': out['ln1_b'], 'w_in': out['w_in'], 'conv_w': out['conv_w'], 'attn_sinks': out['attn_sinks'], 'w_out': out['w_out'], 'ln2_g': out['ln2_g'], 'ln2_b': out['ln2_b'], 'ffn2_w_gate_up': out['ffn2_w_gate_up'], 'ffn2_w_down': out['ffn2_w_down'], 'ln3_g': out['ln3_g'], 'ln3_b': out['ln3_b'], 'loss_target': out['loss_target'], 'm_w_ada': out['m_w_ada'], 'm_b_ada': out['m_b_ada'], 'm_ffn1_w_gate_up': out['m_ffn1_w_gate_up'], 'm_ffn1_w_down': out['m_ffn1_w_down'], 'm_ln1_g': out['m_ln1_g'], 'm_ln1_b': out['m_ln1_b'], 'm_w_in': out['m_w_in'], 'm_conv_w': out['m_conv_w'], 'm_attn_sinks': out['m_attn_sinks'], 'm_w_out': out['m_w_out'], 'm_ln2_g': out['m_ln2_g'], 'm_ln2_b': out['m_ln2_b'], 'm_ffn2_w_gate_up': out['m_ffn2_w_gate_up'], 'm_ffn2_w_down': out['m_ffn2_w_down'], 'm_ln3_g': out['m_ln3_g'], 'm_ln3_b': out['m_ln3_b'], 'v_w_ada': out['v_w_ada'], 'v_b_ada': out['v_b_ada'], 'v_ffn1_w_gate_up': out['v_ffn1_w_gate_up'], 'v_ffn1_w_down': out['v_ffn1_w_down'], 'v_ln1_g': out['v_ln1_g'], 'v_ln1_b': out['v_ln1_b'], 'v_w_in': out['v_w_in'], 'v_conv_w': out['v_conv_w'], 'v_attn_sinks': out['v_attn_sinks'], 'v_w_out': out['v_w_out'], 'v_ln2_g': out['v_ln2_g'], 'v_ln2_b': out['v_ln2_b'], 'v_ffn2_w_gate_up': out['v_ffn2_w_gate_up'], 'v_ffn2_w_down': out['v_ffn2_w_down'], 'v_ln3_g': out['v_ln3_g'], 'v_ln3_b': out['v_ln3_b']}


def _loss(weights, diff, rest, loss_target):
    with _jax.named_scope("forward"):
        args = {**rest, TWIN_DIFF_INPUT: diff, **{k: w.astype(_WEIGHT_DTYPES[k]) for k, w in weights.items()}}
        y = _forward(args)
    with _jax.named_scope("loss_head"):
        err = _jnp.square(y.astype(_jnp.float32) - loss_target)
        return 0.5 * _jnp.sum(_jnp.mean(err, axis=-1)) if err.ndim else 0.5 * err


def _adamw(w, g, m, v):
    m = ADAM_B1 * m + (1.0 - ADAM_B1) * g
    v = ADAM_B2 * v + (1.0 - ADAM_B2) * _jnp.square(g)
    m_hat = m / (1.0 - ADAM_B1 ** ADAM_STEP)
    v_hat = v / (1.0 - ADAM_B2 ** ADAM_STEP)
    delta = -ADAM_LR * (m_hat / (_jnp.sqrt(v_hat) + ADAM_EPS) + ADAM_WD * w)
    return delta, m, v


def reference(x, c, positions, w_ada, b_ada, ffn1_w_gate_up, ffn1_w_down, ln1_g, ln1_b, w_in, conv_w, attn_sinks, w_out, ln2_g, ln2_b, ffn2_w_gate_up, ffn2_w_down, ln3_g, ln3_b, loss_target, m_w_ada, m_b_ada, m_ffn1_w_gate_up, m_ffn1_w_down, m_ln1_g, m_ln1_b, m_w_in, m_conv_w, m_attn_sinks, m_w_out, m_ln2_g, m_ln2_b, m_ffn2_w_gate_up, m_ffn2_w_down, m_ln3_g, m_ln3_b, v_w_ada, v_b_ada, v_ffn1_w_gate_up, v_ffn1_w_down, v_ln1_g, v_ln1_b, v_w_in, v_conv_w, v_attn_sinks, v_w_out, v_ln2_g, v_ln2_b, v_ffn2_w_gate_up, v_ffn2_w_down, v_ln3_g, v_ln3_b):
    given = dict(x=x, c=c, positions=positions, w_ada=w_ada, b_ada=b_ada, ffn1_w_gate_up=ffn1_w_gate_up, ffn1_w_down=ffn1_w_down, ln1_g=ln1_g, ln1_b=ln1_b, w_in=w_in, conv_w=conv_w, attn_sinks=attn_sinks, w_out=w_out, ln2_g=ln2_g, ln2_b=ln2_b, ffn2_w_gate_up=ffn2_w_gate_up, ffn2_w_down=ffn2_w_down, ln3_g=ln3_g, ln3_b=ln3_b, loss_target=loss_target, m_w_ada=m_w_ada, m_b_ada=m_b_ada, m_ffn1_w_gate_up=m_ffn1_w_gate_up, m_ffn1_w_down=m_ffn1_w_down, m_ln1_g=m_ln1_g, m_ln1_b=m_ln1_b, m_w_in=m_w_in, m_conv_w=m_conv_w, m_attn_sinks=m_attn_sinks, m_w_out=m_w_out, m_ln2_g=m_ln2_g, m_ln2_b=m_ln2_b, m_ffn2_w_gate_up=m_ffn2_w_gate_up, m_ffn2_w_down=m_ffn2_w_down, m_ln3_g=m_ln3_g, m_ln3_b=m_ln3_b, v_w_ada=v_w_ada, v_b_ada=v_b_ada, v_ffn1_w_gate_up=v_ffn1_w_gate_up, v_ffn1_w_down=v_ffn1_w_down, v_ln1_g=v_ln1_g, v_ln1_b=v_ln1_b, v_w_in=v_w_in, v_conv_w=v_conv_w, v_attn_sinks=v_attn_sinks, v_w_out=v_w_out, v_ln2_g=v_ln2_g, v_ln2_b=v_ln2_b, v_ffn2_w_gate_up=v_ffn2_w_gate_up, v_ffn2_w_down=v_ffn2_w_down, v_ln3_g=v_ln3_g, v_ln3_b=v_ln3_b)
    weights = {n: given[n] for n in TWIN_WEIGHTS}
    shared = {n: given[n] for n in SHARED_INPUTS}
    per_example = {n: given[n] for n in ['x', 'c', 'positions']}
    grad_fn = _jax.value_and_grad(_loss, argnums=(0, 1))

    def one_microbatch(ex, loss_target):
        ex = dict(ex)
        diff = ex.pop(TWIN_DIFF_INPUT)
        return grad_fn(weights, diff, {**shared, **ex}, loss_target)

    if N_MICROBATCH == 1:
        loss, (grad_w, grad_x) = one_microbatch(per_example, given["loss_target"])
    else:
        def body(carry, xs):
            loss_sum, grad_sum = carry
            l_k, (gw_k, gx_k) = one_microbatch(xs[0], xs[1])
            with _jax.named_scope("update"):
                return (loss_sum + l_k, _jax.tree.map(_jnp.add, grad_sum, gw_k)), gx_k

        init = (_jnp.zeros((), _jnp.float32), _jax.tree.map(_jnp.zeros_like, weights))
        (loss, grad_w), grad_x = _jax.lax.scan(body, init, (per_example, given["loss_target"]))
    with _jax.named_scope("update"):
        delta_w, new_m, new_v = {}, {}, {}
        for n in TWIN_WEIGHTS:
            delta_w[n], new_m[n], new_v[n] = _adamw(weights[n], grad_w[n], given["m_" + n], given["v_" + n])
    return (loss, grad_x, *[grad_w[n] for n in TWIN_WEIGHTS], *[delta_w[n] for n in TWIN_WEIGHTS],
            *[new_m[n] for n in TWIN_WEIGHTS], *[new_v[n] for n in TWIN_WEIGHTS])
```

```python
import jax
import jax.numpy as jnp
from jax import lax
from jax.experimental import pallas as pl
from jax.experimental.pallas import tpu as pltpu

F32 = jnp.float32
BF16 = jnp.bfloat16
MESH = pl.DeviceIdType.MESH

N_DEV = 8
N_CHIP = 4
HEAD_DIM = 64
N_Q_HEADS = 8
N_KV_HEADS = 2
GQA_GROUP = N_Q_HEADS // N_KV_HEADS
ATTN_BLOCK = 128
ROT_DIM = 16
ROPE_THETA = 500000.0
CONV_TAPS = 3
LN_EPS = 1e-5
DN_ALPHA = 2.0 ** 0.25
ADAM_LR = 0.001
ADAM_B1 = 0.9
ADAM_B2 = 0.999
ADAM_EPS = 1e-08
ADAM_WD = 0.01
ADAM_STEP = 10
NEG_BIG = -1e30

VMEM_LIMIT = 56 * 1024 * 1024
TOKEN_TILE = 256
TN_TOKEN_TILE = 512


def _params(semantics=None, vmem=VMEM_LIMIT):
    return pltpu.CompilerParams(dimension_semantics=semantics, vmem_limit_bytes=vmem)


def _dot(a, b):
    return jnp.dot(a, b, preferred_element_type=F32)


def _dot_nt(a, b):
    return lax.dot_general(a, b, (((1,), (1,)), ((), ())), preferred_element_type=F32)


def _dot_tn(a, b):
    return lax.dot_general(a, b, (((0,), (0,)), ((), ())), preferred_element_type=F32)


def _sigmoid(x):
    return 1.0 / (1.0 + jnp.exp(-x))


def _ln_stats(r):
    mu = jnp.mean(r, axis=-1, keepdims=True)
    d = r - mu
    var = jnp.mean(d * d, axis=-1, keepdims=True)
    rstd = lax.rsqrt(var + LN_EPS)
    return d * rstd, rstd


def _ln_bwd(dy, r, g):
    xhat, rstd = _ln_stats(r)
    dxhat = dy * g
    c1 = jnp.mean(dxhat, axis=-1, keepdims=True)
    c2 = jnp.mean(dxhat * xhat, axis=-1, keepdims=True)
    dr = rstd * (dxhat - c1 - xhat * c2)
    return dr, jnp.sum(dy * xhat, axis=0, keepdims=True), jnp.sum(dy, axis=0, keepdims=True)


def _const_spec(shape):
    nd = len(shape)
    return pl.BlockSpec(shape, lambda *_: (0,) * nd, pipeline_mode=pl.Buffered(1))


def all_gather(arrs, name):
    n = len(arrs)

    def body(*refs):
        ins, outs = refs[:n], refs[n:2 * n]
        send_sems, recv_sems, local_sems = refs[2 * n:]
        x, y, c = lax.axis_index("x"), lax.axis_index("y"), lax.axis_index("c")
        me, sibling = (x, y, c), (x, y, 1 - c)
        chips = [(1 - x, y), (x, 1 - y), (1 - x, 1 - y)]

        def slot(i, p):
            return outs[i].at[4 * p[0] + 2 * p[1] + p[2]]

        def copy(i, k, block, to, src=None):
            return pltpu.make_async_remote_copy(
                src_ref=slot(i, block) if src is None else src, dst_ref=slot(i, block),
                send_sem=send_sems.at[i, k], recv_sem=recv_sems.at[i, k],
                device_id=to, device_id_type=MESH)

        mine = [pltpu.make_async_copy(ins[i], slot(i, me), local_sems.at[i]) for i in range(n)]
        for cp in mine:
            cp.start()
        first = []
        for i in range(n):
            first.append(copy(i, 0, me, sibling, src=ins[i]))
            first += [copy(i, 1 + j, me, (*chip, c), src=ins[i]) for j, chip in enumerate(chips)]
        for cp in first:
            cp.start()
        passed = []
        for j, chip in enumerate(chips):
            for i in range(n):
                copy(i, 1 + j, (*chip, c), me).wait_recv()
                cp = copy(i, 4 + j, (*chip, c), sibling)
                cp.start()
                passed.append(cp)
        for i in range(n):
            copy(i, 0, sibling, me).wait_recv()
            for j, chip in enumerate(chips):
                copy(i, 4 + j, (*chip, 1 - c), me).wait_recv()
        for cp in first + passed:
            cp.wait_send()
        for cp in mine:
            cp.wait()

    any_spec = pl.BlockSpec(memory_space=pl.ANY)
    return pl.pallas_call(
        body, name=name,
        out_shape=[jax.ShapeDtypeStruct((N_DEV, *a.shape), a.dtype) for a in arrs],
        in_specs=[any_spec] * n, out_specs=[any_spec] * n,
        scratch_shapes=[pltpu.SemaphoreType.DMA((n, 7)), pltpu.SemaphoreType.DMA((n, 7)),
                        pltpu.SemaphoreType.DMA((n,))],
    )(*arrs)


RS_ROWS = 32


def reduce_scatter(g, name):
    _, rows, cols = g.shape
    nblk = rows // RS_ROWS
    assert nblk * RS_ROWS == rows

    def body(g_ref, out_ref, r1_ref, p_ref, r2_ref, send_sems, recv_sems):
        x, y, c = lax.axis_index("x"), lax.axis_index("y"), lax.axis_index("c")
        sibling = (x, y, 1 - c)
        q_me = 2 * x + y
        swaps = []
        for q in range(N_CHIP):
            cp = pltpu.make_async_remote_copy(
                src_ref=g_ref.at[2 * q + (1 - c)], dst_ref=r1_ref.at[q],
                send_sem=send_sems.at[q], recv_sem=recv_sems.at[q], device_id=sibling, device_id_type=MESH)
            cp.start()
            swaps.append(cp)
        for cp in swaps:
            cp.wait_recv()

        def pair_sum(i, carry):
            r = pl.ds(pl.multiple_of(i * RS_ROWS, RS_ROWS), RS_ROWS)
            for q in range(N_CHIP):
                p_ref[q, r, :] = (g_ref[2 * q + c, r, :].astype(F32) + r1_ref[q, r, :].astype(F32)).astype(BF16)
            return carry

        lax.fori_loop(0, nblk, pair_sum, 0)
        chips = [(1 - x, y), (x, 1 - y), (1 - x, 1 - y)]
        sends = []
        for k, chip in enumerate(chips):
            cp = pltpu.make_async_remote_copy(
                src_ref=p_ref.at[2 * chip[0] + chip[1]], dst_ref=r2_ref.at[k],
                send_sem=send_sems.at[N_CHIP + k], recv_sem=recv_sems.at[N_CHIP + k],
                device_id=(*chip, c), device_id_type=MESH)
            cp.start()
            sends.append(cp)
        for cp in sends:
            cp.wait_recv()

        def total(i, carry):
            r = pl.ds(pl.multiple_of(i * RS_ROWS, RS_ROWS), RS_ROWS)
            acc = g_ref[2 * q_me + c, r, :].astype(F32) + r1_ref[q_me, r, :].astype(F32)
            for k in range(3):
                acc = acc + r2_ref[k, r, :].astype(F32)
            out_ref[r, :] = acc
            return carry

        lax.fori_loop(0, nblk, total, 0)
        for cp in swaps + sends:
            cp.wait_send()

    vmem = pl.BlockSpec(memory_space=pltpu.VMEM)
    return pl.pallas_call(
        body, name=name,
        out_shape=jax.ShapeDtypeStruct((rows, cols), F32),
        in_specs=[vmem], out_specs=vmem,
        scratch_shapes=[pltpu.VMEM((N_CHIP, rows, cols), BF16), pltpu.VMEM((N_CHIP, rows, cols), BF16),
                        pltpu.VMEM((3, rows, cols), BF16),
                        pltpu.SemaphoreType.DMA((N_CHIP + 3,)), pltpu.SemaphoreType.DMA((N_CHIP + 3,))],
        compiler_params=_params(),
    )(g)


def sum_devices(a, name):
    def body(a_ref, o_ref):
        acc = a_ref[0]
        for d in range(1, N_DEV):
            acc = acc + a_ref[d]
        o_ref[...] = acc

    return pl.pallas_call(body, name=name, out_shape=jax.ShapeDtypeStruct(a.shape[1:], F32))(a)


def adamw(w, g, m, v, name):
    rows, cols = w.shape
    rb = rows
    for cand in (256, 128, 64, 32, 16, 8):
        if rows % cand == 0:
            rb = cand
            break

    def body(w_ref, g_ref, m_ref, v_ref, d_ref, nm_ref, nv_ref):
        gg = g_ref[...]
        nm = ADAM_B1 * m_ref[...] + (1.0 - ADAM_B1) * gg
        nv = ADAM_B2 * v_ref[...] + (1.0 - ADAM_B2) * (gg * gg)
        m_hat = nm / (1.0 - ADAM_B1 ** ADAM_STEP)
        v_hat = nv / (1.0 - ADAM_B2 ** ADAM_STEP)
        d_ref[...] = -ADAM_LR * (m_hat / (jnp.sqrt(v_hat) + ADAM_EPS) + ADAM_WD * w_ref[...])
        nm_ref[...] = nm
        nv_ref[...] = nv

    spec = pl.BlockSpec((rb, cols), lambda i: (i, 0))
    out = jax.ShapeDtypeStruct((rows, cols), F32)
    return pl.pallas_call(
        body, name=name, grid=(rows // rb,), in_specs=[spec] * 4, out_specs=[spec] * 3,
        out_shape=[out, out, out], compiler_params=_params(("parallel",)),
    )(w, g, m, v)


def ada_fwd(c_all, w_cols, b_cols, name):
    def body(c_ref, w_ref, b_ref, cond_ref, mod_ref):
        cc = c_ref[...]
        cond = (cc * _sigmoid(cc)).astype(BF16)
        cond_ref[...] = cond
        mod_ref[...] = _dot(cond, w_ref[...].astype(BF16)) + b_ref[...]

    n, cols = c_all.shape[0], w_cols.shape[1]
    return pl.pallas_call(
        body, name=name,
        out_shape=[jax.ShapeDtypeStruct(c_all.shape, BF16), jax.ShapeDtypeStruct((n, cols), F32)],
        compiler_params=_params(),
    )(c_all, w_cols, b_cols)


def ada_bwd(cond_all, dmod_cols, name):
    def body(c_ref, d_ref, gw_ref, gb_ref):
        d = d_ref[...]
        gw_ref[...] = _dot_tn(c_ref[...], d.astype(BF16))
        gb_ref[...] = jnp.sum(d, axis=0, keepdims=True)

    dm, cols = cond_all.shape[1], dmod_cols.shape[1]
    return pl.pallas_call(
        body, name=name,
        out_shape=[jax.ShapeDtypeStruct((dm, cols), F32), jax.ShapeDtypeStruct((1, cols), F32)],
        compiler_params=_params(),
    )(cond_all, dmod_cols)


def _mod_spec(tiles_per_seq, dm):
    return pl.BlockSpec((1, 1, dm), lambda i: (i // tiles_per_seq, 0, 0))


def ffn_fwd(x, sh, sc, gt, wgu, wd, ln_g, ln_b, seq, name, target=None):
    tokens, dm = x.shape
    fc = wgu.shape[2]
    tm = min(TOKEN_TILE, seq)
    tiles_per_seq = seq // tm
    with_loss = target is not None

    def body(*refs):
        if with_loss:
            (x_ref, sh_ref, sc_ref, gt_ref, wgu_ref, wd_ref, lg_ref, lb_ref, t_ref,
             xo_ref, loss_ref, r_ref, gu_ref, f_ref) = refs
        else:
            (x_ref, sh_ref, sc_ref, gt_ref, wgu_ref, wd_ref, lg_ref, lb_ref,
             xo_ref, r_ref, gu_ref, f_ref) = refs
        xx = x_ref[...]
        h = (xx * (1.0 + sc_ref[0]) + sh_ref[0]).astype(BF16)
        acc = jnp.zeros((tm, dm), F32)
        for k in range(4):
            gk = _dot(h, wgu_ref[k])
            uk = _dot(h, wgu_ref[k + 4])
            gu_ref[k] = gk.astype(BF16)
            gu_ref[k + 4] = uk.astype(BF16)
            a = (gk * _sigmoid(gk) * uk).astype(BF16)
            acc = acc + _dot(a, wd_ref[k])
        f_ref[...] = acc.astype(BF16)
        r = DN_ALPHA * xx + (0.5 * (1.0 + gt_ref[0])) * acc
        r_ref[...] = r
        xhat, _ = _ln_stats(r)
        yy = xhat * lg_ref[...] + lb_ref[...]
        if with_loss:
            err = yy - t_ref[...]
            xo_ref[...] = err * (1.0 / dm)

            @pl.when(pl.program_id(0) == 0)
            def _():
                loss_ref[...] = jnp.zeros_like(loss_ref)

            loss_ref[...] += jnp.full((1, 128), (0.5 / dm) * jnp.sum(err * err), F32)
        else:
            xo_ref[...] = yy

    tile = pl.BlockSpec((tm, dm), lambda i: (i, 0))
    mod = _mod_spec(tiles_per_seq, dm)
    in_specs = [tile, mod, mod, mod, _const_spec(wgu.shape), _const_spec(wd.shape),
                _const_spec((1, dm)), _const_spec((1, dm))]
    args = [x, sh, sc, gt, wgu, wd, ln_g, ln_b]
    out_specs = [tile]
    out_shape = [jax.ShapeDtypeStruct((tokens, dm), F32)]
    if with_loss:
        in_specs.append(tile)
        args.append(target)
        out_specs.append(pl.BlockSpec((1, 128), lambda i: (0, 0)))
        out_shape.append(jax.ShapeDtypeStruct((1, 128), F32))
    out_specs += [tile, pl.BlockSpec((8, tm, fc), lambda i: (0, i, 0)), tile]
    out_shape += [jax.ShapeDtypeStruct((tokens, dm), F32), jax.ShapeDtypeStruct((8, tokens, fc), BF16),
                  jax.ShapeDtypeStruct((tokens, dm), BF16)]
    return pl.pallas_call(
        body, name=name, grid=(tokens // tm,), in_specs=in_specs, out_specs=out_specs, out_shape=out_shape,
        compiler_params=_params(("arbitrary",)),
    )(*args)


def ffn_bwd(dy, r, x, f, gu, sh, sc, gt, wgu, wd, ln_g, seq, name):
    tokens, dm = x.shape
    fc = wgu.shape[2]
    tm = min(TOKEN_TILE, seq)
    tiles_per_seq = seq // tm
    nseq = tokens // seq

    def body(dy_ref, r_ref, x_ref, f_ref, gu_ref, sh_ref, sc_ref, gt_ref, wgu_ref, wd_ref, lg_ref,
             dx_ref, dgu_ref, df_ref, a_ref, h_ref, dln_ref, dmod_ref):
        i = pl.program_id(0)
        dr, dgain, dbias = _ln_bwd(dy_ref[...], r_ref[...], lg_ref[...])

        @pl.when(i == 0)
        def _():
            dln_ref[...] = jnp.zeros_like(dln_ref)

        @pl.when(i % tiles_per_seq == 0)
        def _():
            dmod_ref[...] = jnp.zeros_like(dmod_ref)

        dln_ref[0:1, :] += dgain
        dln_ref[1:2, :] += dbias
        df32 = (0.5 * (1.0 + gt_ref[0])) * dr
        df = df32.astype(BF16)
        df_ref[...] = df
        dgate = jnp.sum(dr * (0.5 * f_ref[...].astype(F32)), axis=0, keepdims=True)
        xx = x_ref[...]
        one_sc = 1.0 + sc_ref[0]
        h = (xx * one_sc + sh_ref[0]).astype(BF16)
        h_ref[...] = h
        dh = jnp.zeros((tm, dm), F32)
        for k in range(4):
            da = _dot_nt(df, wd_ref[k])
            gk = gu_ref[k].astype(F32)
            uk = gu_ref[k + 4].astype(F32)
            sg = _sigmoid(gk)
            sil = gk * sg
            a_ref[k] = (sil * uk).astype(BF16)
            du = (da * sil).astype(BF16)
            dg = (da * uk * (sg * (1.0 + gk * (1.0 - sg)))).astype(BF16)
            dgu_ref[k] = dg
            dgu_ref[k + 4] = du
            dh = dh + _dot_nt(dg, wgu_ref[k]) + _dot_nt(du, wgu_ref[k + 4])
        dx_ref[...] = DN_ALPHA * dr + dh * one_sc
        dmod_ref[0, 0:1, :] += jnp.sum(dh, axis=0, keepdims=True)
        dmod_ref[0, 1:2, :] += jnp.sum(dh * xx, axis=0, keepdims=True)
        dmod_ref[0, 2:3, :] += dgate

    tile = pl.BlockSpec((tm, dm), lambda i: (i, 0))
    mod = _mod_spec(tiles_per_seq, dm)
    gu_spec = pl.BlockSpec((8, tm, fc), lambda i: (0, i, 0))
    return pl.pallas_call(
        body, name=name, grid=(tokens // tm,),
        in_specs=[tile, tile, tile, tile, gu_spec, mod, mod, mod, _const_spec(wgu.shape), _const_spec(wd.shape),
                  _const_spec((1, dm))],
        out_specs=[tile, gu_spec, tile, pl.BlockSpec((4, tm, fc), lambda i: (0, i, 0)), tile,
                   pl.BlockSpec((2, dm), lambda i: (0, 0)),
                   pl.BlockSpec((1, 3, dm), lambda i: (i // tiles_per_seq, 0, 0))],
        out_shape=[jax.ShapeDtypeStruct((tokens, dm), F32), jax.ShapeDtypeStruct((8, tokens, fc), BF16),
                   jax.ShapeDtypeStruct((tokens, dm), BF16), jax.ShapeDtypeStruct((4, tokens, fc), BF16),
                   jax.ShapeDtypeStruct((tokens, dm), BF16), jax.ShapeDtypeStruct((2, dm), F32),
                   jax.ShapeDtypeStruct((nseq, 3, dm), F32)],
        compiler_params=_params(("arbitrary",)),
    )(dy, r, x, f, gu, sh, sc, gt, wgu, wd, ln_g)


def tn_matmul(a, b, name):
    na, tokens, kk = a.shape
    nb, _, cc = b.shape
    tt = min(TN_TOKEN_TILE, tokens)
    steps = tokens // tt

    def body(a_ref, b_ref, o_ref, acc_ref):
        t = pl.program_id(2)

        @pl.when(t == 0)
        def _():
            acc_ref[...] = jnp.zeros_like(acc_ref)

        acc_ref[...] += _dot_tn(a_ref[0], b_ref[0])

        @pl.when(t == steps - 1)
        def _():
            o_ref[0, 0] = acc_ref[...].astype(BF16)

    return pl.pallas_call(
        body, name=name, grid=(na, nb, steps),
        in_specs=[pl.BlockSpec((1, tt, kk), lambda i, j, t: (i, t, 0)),
                  pl.BlockSpec((1, tt, cc), lambda i, j, t: (j, t, 0))],
        out_specs=pl.BlockSpec((1, 1, kk, cc), lambda i, j, t: (i, j, 0, 0)),
        out_shape=jax.ShapeDtypeStruct((na, nb, kk, cc), BF16),
        scratch_shapes=[pltpu.VMEM((kk, cc), F32)],
        compiler_params=_params(("parallel", "parallel", "arbitrary")),
    )(a, b)


def proj_fwd(x1, sh, sc, w_in, seq, name):
    tokens, dm = x1.shape
    tm = min(TOKEN_TILE, seq)
    tiles_per_seq = seq // tm
    widths = [N_Q_HEADS * HEAD_DIM, N_KV_HEADS * HEAD_DIM, N_KV_HEADS * HEAD_DIM, 512, 512, 512]
    assert sum(widths) == w_in.shape[1]

    def body(x_ref, sh_ref, sc_ref, w_ref, *outs):
        h = (x_ref[...] * (1.0 + sc_ref[0]) + sh_ref[0]).astype(BF16)
        proj = _dot(h, w_ref[...])
        at = 0
        for o_ref, wdt in zip(outs, widths):
            o_ref[...] = proj[:, at:at + wdt]
            at += wdt

    tile = pl.BlockSpec((tm, dm), lambda i: (i, 0))
    mod = _mod_spec(tiles_per_seq, dm)
    return pl.pallas_call(
        body, name=name, grid=(tokens // tm,),
        in_specs=[tile, mod, mod, _const_spec(w_in.shape)],
        out_specs=[pl.BlockSpec((tm, wdt), lambda i: (i, 0)) for wdt in widths],
        out_shape=[jax.ShapeDtypeStruct((tokens, wdt), F32) for wdt in widths],
        compiler_params=_params(("parallel",)),
    )(x1, sh, sc, w_in)


def _rope_swap_matrix():
    half = ROT_DIM // 2
    i = lax.broadcasted_iota(jnp.int32, (HEAD_DIM, HEAD_DIM), 0)
    j = lax.broadcasted_iota(jnp.int32, (HEAD_DIM, HEAD_DIM), 1)
    hit = ((j < half) & (i == j + half)) | ((j >= half) & (j < ROT_DIM) & (i == j - half))
    return jnp.where(hit, 1.0, 0.0).astype(BF16)


def _swap_halves(v, perm):
    hi = v.astype(BF16)
    rest = v - hi.astype(F32)
    mid = rest.astype(BF16)
    lo = (rest - mid.astype(F32)).astype(BF16)
    return _dot(hi, perm) + _dot(mid, perm) + _dot(lo, perm)


def _rope(v, cos_t, sin_t, perm):
    return v * cos_t + _swap_halves(v, perm) * sin_t


def _unrope(dv, cos_t, sin_t, perm):
    return dv * cos_t + _swap_halves(dv * sin_t, perm)


def _window_mask(rows, first):
    ncol = ATTN_BLOCK if first else 2 * ATTN_BLOCK
    qi = lax.broadcasted_iota(jnp.int32, (rows, ncol), 0) % ATTN_BLOCK
    ki = lax.broadcasted_iota(jnp.int32, (rows, ncol), 1)
    diff = qi - ki if first else qi + ATTN_BLOCK - ki
    return (diff >= 0) & (diff < ATTN_BLOCK)


def _attn_specs(seq):
    q_spec = pl.BlockSpec((1, GQA_GROUP, seq, HEAD_DIM), lambda b, g: (b, g, 0, 0))
    kv_spec = pl.BlockSpec((1, 1, seq, HEAD_DIM), lambda b, g: (b, g, 0, 0))
    rope_spec = pl.BlockSpec((1, seq, HEAD_DIM), lambda b, g: (b, 0, 0))
    sink_spec = pl.BlockSpec((1, GQA_GROUP * ATTN_BLOCK, 1), lambda b, g: (g, 0, 0))
    return q_spec, kv_spec, rope_spec, sink_spec


def _sink_columns(sinks):
    return jnp.repeat(sinks.reshape(N_KV_HEADS, GQA_GROUP), ATTN_BLOCK, axis=1)[:, :, None]


def attn_fwd(q, k, v, cos_t, sin_t, sinks, name):
    nseq, _, seq, _ = q.shape
    nblk = seq // ATTN_BLOCK
    rows = GQA_GROUP * ATTN_BLOCK
    scale = HEAD_DIM ** -0.5

    def body(q_ref, k_ref, v_ref, cos_ref, sin_ref, sink_ref, o_ref, kr_ref):
        perm = _rope_swap_matrix()
        kr_ref[...] = _rope(k_ref[0, 0], cos_ref[0], sin_ref[0], perm).astype(BF16)
        sink = sink_ref[0]
        for n in range(nblk):
            lo = n * ATTN_BLOCK
            first = n == 0
            win = pl.ds(lo, ATTN_BLOCK) if first else pl.ds(lo - ATTN_BLOCK, 2 * ATTN_BLOCK)
            blk = pl.ds(lo, ATTN_BLOCK)
            cs = jnp.tile(cos_ref[0, blk, :], (GQA_GROUP, 1))
            sn = jnp.tile(sin_ref[0, blk, :], (GQA_GROUP, 1))
            qg = jnp.concatenate([q_ref[0, hh, blk, :] for hh in range(GQA_GROUP)], axis=0)
            qr = _rope(qg, cs, sn, perm).astype(BF16)
            s = _dot_nt(qr, kr_ref[win, :]) * scale
            s = jnp.where(_window_mask(rows, first), s, NEG_BIG)
            m = jnp.maximum(jnp.max(s, axis=-1, keepdims=True), sink)
            p = jnp.exp(s - m)
            denom = jnp.sum(p, axis=-1, keepdims=True) + jnp.exp(sink - m)
            out = _dot((p / denom).astype(BF16), v_ref[0, 0, win, :].astype(BF16))
            for hh in range(GQA_GROUP):
                o_ref[0, hh, blk, :] = out[hh * ATTN_BLOCK:(hh + 1) * ATTN_BLOCK]

    q_spec, kv_spec, rope_spec, sink_spec = _attn_specs(seq)
    return pl.pallas_call(
        body, name=name, grid=(nseq, N_KV_HEADS),
        in_specs=[q_spec, kv_spec, kv_spec, rope_spec, rope_spec, sink_spec],
        out_specs=q_spec, out_shape=jax.ShapeDtypeStruct(q.shape, F32),
        scratch_shapes=[pltpu.VMEM((seq, HEAD_DIM), BF16)],
        compiler_params=_params(("parallel", "parallel")),
    )(q, k, v, cos_t, sin_t, _sink_columns(sinks))


def attn_bwd(q, k, v, do, cos_t, sin_t, sinks, name):
    nseq, _, seq, _ = q.shape
    nblk = seq // ATTN_BLOCK
    rows = GQA_GROUP * ATTN_BLOCK
    scale = HEAD_DIM ** -0.5

    def body(q_ref, k_ref, v_ref, do_ref, cos_ref, sin_ref, sink_ref, dq_ref, dk_ref, dv_ref, ds_ref,
             kr_ref, dkr_ref):
        perm = _rope_swap_matrix()
        kr_ref[...] = _rope(k_ref[0, 0], cos_ref[0], sin_ref[0], perm).astype(BF16)
        dkr_ref[...] = jnp.zeros_like(dkr_ref)
        dv_ref[...] = jnp.zeros_like(dv_ref)
        sink = sink_ref[0]
        dsink = jnp.zeros((rows, 1), F32)
        for n in range(nblk):
            lo = n * ATTN_BLOCK
            first = n == 0
            win = pl.ds(lo, ATTN_BLOCK) if first else pl.ds(lo - ATTN_BLOCK, 2 * ATTN_BLOCK)
            blk = pl.ds(lo, ATTN_BLOCK)
            cs = jnp.tile(cos_ref[0, blk, :], (GQA_GROUP, 1))
            sn = jnp.tile(sin_ref[0, blk, :], (GQA_GROUP, 1))
            qg = jnp.concatenate([q_ref[0, hh, blk, :] for hh in range(GQA_GROUP)], axis=0)
            dog = jnp.concatenate([do_ref[0, hh, blk, :] for hh in range(GQA_GROUP)], axis=0).astype(BF16)
            qr = _rope(qg, cs, sn, perm).astype(BF16)
            kw = kr_ref[win, :]
            vw = v_ref[0, 0, win, :].astype(BF16)
            s = _dot_nt(qr, kw) * scale
            s = jnp.where(_window_mask(rows, first), s, NEG_BIG)
            m = jnp.maximum(jnp.max(s, axis=-1, keepdims=True), sink)
            p = jnp.exp(s - m)
            e_sink = jnp.exp(sink - m)
            inv = 1.0 / (jnp.sum(p, axis=-1, keepdims=True) + e_sink)
            pn = p * inv
            pn16 = pn.astype(BF16)
            dv_ref[0, 0, win, :] += _dot_tn(pn16, dog)
            dp = _dot_nt(dog, vw)
            delta = jnp.sum(dp * pn, axis=-1, keepdims=True)
            dsink = dsink - e_sink * inv * delta
            ds = (pn * (dp - delta)).astype(BF16)
            dqr = _dot(ds, kw) * scale
            dkr_ref[win, :] += _dot_tn(ds, qr) * scale
            dqg = _unrope(dqr, cs, sn, perm)
            for hh in range(GQA_GROUP):
                dq_ref[0, hh, blk, :] = dqg[hh * ATTN_BLOCK:(hh + 1) * ATTN_BLOCK]
        dk_ref[0, 0] = _unrope(dkr_ref[...], cos_ref[0], sin_ref[0], perm)
        ds_ref[0, 0] = dsink

    q_spec, kv_spec, rope_spec, sink_spec = _attn_specs(seq)
    return pl.pallas_call(
        body, name=name, grid=(nseq, N_KV_HEADS),
        in_specs=[q_spec, kv_spec, kv_spec, q_spec, rope_spec, rope_spec, sink_spec],
        out_specs=[q_spec, kv_spec, kv_spec, pl.BlockSpec((1, 1, rows, 1), lambda b, g: (b, g, 0, 0))],
        out_shape=[jax.ShapeDtypeStruct(q.shape, F32), jax.ShapeDtypeStruct(k.shape, F32),
                   jax.ShapeDtypeStruct(k.shape, F32), jax.ShapeDtypeStruct((nseq, N_KV_HEADS, rows, 1), F32)],
        scratch_shapes=[pltpu.VMEM((seq, HEAD_DIM), BF16), pltpu.VMEM((seq, HEAD_DIM), F32)],
        compiler_params=_params(("parallel", "parallel")),
    )(q, k, v, do, cos_t, sin_t, _sink_columns(sinks))


CONV_COLS = 128


def _shift_down(z, by):
    t = lax.broadcasted_iota(jnp.int32, z.shape, 0)
    return jnp.where(t >= by, pltpu.roll(z, by, 0), 0.0)


def _shift_up(z, by):
    n = z.shape[0]
    t = lax.broadcasted_iota(jnp.int32, z.shape, 0)
    return jnp.where(t < n - by, pltpu.roll(z, n - by, 0), 0.0)


def conv_fwd(u, bg, cg, conv_w, seq, name):
    tokens, width = u.shape

    def body(u_ref, bg_ref, cg_ref, w_ref, o_ref):
        z = cg_ref[...] * u_ref[...]
        yy = w_ref[2:3, :] * z + w_ref[1:2, :] * _shift_down(z, 1) + w_ref[0:1, :] * _shift_down(z, 2)
        o_ref[...] = bg_ref[...] * yy

    col = pl.BlockSpec((seq, CONV_COLS), lambda j, b: (b, j))
    return pl.pallas_call(
        body, name=name, grid=(width // CONV_COLS, tokens // seq),
        in_specs=[col, col, col, pl.BlockSpec((CONV_TAPS, CONV_COLS), lambda j, b: (0, j))],
        out_specs=col, out_shape=jax.ShapeDtypeStruct((tokens, width), F32),
        compiler_params=_params(("parallel", "parallel")),
    )(u, bg, cg, conv_w)


def conv_bwd(dout, u, bg, cg, conv_w, seq, name):
    tokens, width = u.shape

    def body(do_ref, u_ref, bg_ref, cg_ref, w_ref, du_ref, dbg_ref, dcg_ref, dw_ref):
        uu, cg_v, do = u_ref[...], cg_ref[...], do_ref[...]
        z = cg_v * uu
        z1, z2 = _shift_down(z, 1), _shift_down(z, 2)
        yy = w_ref[2:3, :] * z + w_ref[1:2, :] * z1 + w_ref[0:1, :] * z2
        dbg_ref[...] = do * yy
        dyy = do * bg_ref[...]
        dz = w_ref[2:3, :] * dyy + w_ref[1:2, :] * _shift_up(dyy, 1) + w_ref[0:1, :] * _shift_up(dyy, 2)
        du_ref[...] = dz * cg_v
        dcg_ref[...] = dz * uu

        @pl.when(pl.program_id(1) == 0)
        def _():
            dw_ref[...] = jnp.zeros_like(dw_ref)

        dw_ref[0:1, :] += jnp.sum(dyy * z2, axis=0, keepdims=True)
        dw_ref[1:2, :] += jnp.sum(dyy * z1, axis=0, keepdims=True)
        dw_ref[2:3, :] += jnp.sum(dyy * z, axis=0, keepdims=True)

    col = pl.BlockSpec((seq, CONV_COLS), lambda j, b: (b, j))
    w_spec = pl.BlockSpec((CONV_TAPS, CONV_COLS), lambda j, b: (0, j))
    act = jax.ShapeDtypeStruct((tokens, width), F32)
    return pl.pallas_call(
        body, name=name, grid=(width // CONV_COLS, tokens // seq),
        in_specs=[col, col, col, col, w_spec], out_specs=[col, col, col, w_spec],
        out_shape=[act, act, act, jax.ShapeDtypeStruct((CONV_TAPS, width), F32)],
        compiler_params=_params(("parallel", "arbitrary")),
    )(dout, u, bg, cg, conv_w)


def out_fwd(x1, attn, conv, gt, w_out, ln_g, ln_b, seq, name):
    tokens, dm = x1.shape
    half = attn.shape[1]
    tm = min(TOKEN_TILE, seq)
    tiles_per_seq = seq // tm

    def body(x_ref, a_ref, c_ref, gt_ref, w_ref, lg_ref, lb_ref, xo_ref, r_ref, mi_ref, mix_ref):
        mixin = jnp.concatenate([a_ref[...], c_ref[...]], axis=1).astype(BF16)
        mi_ref[...] = mixin
        mix = _dot(mixin, w_ref[...])
        mix_ref[...] = mix.astype(BF16)
        r = DN_ALPHA * x_ref[...] + (1.0 + gt_ref[0]) * mix
        r_ref[...] = r
        xhat, _ = _ln_stats(r)
        xo_ref[...] = xhat * lg_ref[...] + lb_ref[...]

    tile = pl.BlockSpec((tm, dm), lambda i: (i, 0))
    htile = pl.BlockSpec((tm, half), lambda i: (i, 0))
    return pl.pallas_call(
        body, name=name, grid=(tokens // tm,),
        in_specs=[tile, htile, htile, _mod_spec(tiles_per_seq, dm), _const_spec(w_out.shape),
                  _const_spec((1, dm)), _const_spec((1, dm))],
        out_specs=[tile, tile, tile, tile],
        out_shape=[jax.ShapeDtypeStruct((tokens, dm), F32), jax.ShapeDtypeStruct((tokens, dm), F32),
                   jax.ShapeDtypeStruct((tokens, dm), BF16), jax.ShapeDtypeStruct((tokens, dm), BF16)],
        compiler_params=_params(("parallel",)),
    )(x1, attn, conv, gt, w_out, ln_g, ln_b)


def out_bwd(dy, r, mix, gt, w_out, ln_g, seq, name):
    tokens, dm = r.shape
    half = dm // 2
    tm = min(TOKEN_TILE, seq)
    tiles_per_seq = seq // tm
    nseq = tokens // seq

    def body(dy_ref, r_ref, mix_ref, gt_ref, w_ref, lg_ref, dres_ref, da_ref, dc_ref, dmix_ref, dln_ref, dgt_ref):
        i = pl.program_id(0)
        dr, dgain, dbias = _ln_bwd(dy_ref[...], r_ref[...], lg_ref[...])

        @pl.when(i == 0)
        def _():
            dln_ref[...] = jnp.zeros_like(dln_ref)

        @pl.when(i % tiles_per_seq == 0)
        def _():
            dgt_ref[...] = jnp.zeros_like(dgt_ref)

        dln_ref[0:1, :] += dgain
        dln_ref[1:2, :] += dbias
        dgt_ref[0] += jnp.sum(dr * mix_ref[...].astype(F32), axis=0, keepdims=True)
        dres_ref[...] = DN_ALPHA * dr
        dmix = ((1.0 + gt_ref[0]) * dr).astype(BF16)
        dmix_ref[...] = dmix
        dmixin = _dot_nt(dmix, w_ref[...])
        da_ref[...] = dmixin[:, :half]
        dc_ref[...] = dmixin[:, half:]

    tile = pl.BlockSpec((tm, dm), lambda i: (i, 0))
    htile = pl.BlockSpec((tm, half), lambda i: (i, 0))
    return pl.pallas_call(
        body, name=name, grid=(tokens // tm,),
        in_specs=[tile, tile, tile, _mod_spec(tiles_per_seq, dm), _const_spec(w_out.shape), _const_spec((1, dm))],
        out_specs=[tile, htile, htile, tile, pl.BlockSpec((2, dm), lambda i: (0, 0)),
                   pl.BlockSpec((1, 1, dm), lambda i: (i // tiles_per_seq, 0, 0))],
        out_shape=[jax.ShapeDtypeStruct((tokens, dm), F32), jax.ShapeDtypeStruct((tokens, half), F32),
                   jax.ShapeDtypeStruct((tokens, half), F32), jax.ShapeDtypeStruct((tokens, dm), BF16),
                   jax.ShapeDtypeStruct((2, dm), F32), jax.ShapeDtypeStruct((nseq, 1, dm), F32)],
        compiler_params=_params(("arbitrary",)),
    )(dy, r, mix, gt, w_out, ln_g)


def proj_bwd(parts, dres, x1, sh, sc, w_in, seq, name):
    tokens, dm = x1.shape
    tm = min(TOKEN_TILE, seq)
    tiles_per_seq = seq // tm
    nseq = tokens // seq
    widths = [p.shape[1] for p in parts]
    total = sum(widths)

    def body(*refs):
        part_refs = refs[:6]
        dres_ref, x_ref, sh_ref, sc_ref, w_ref, dx_ref, dproj_ref, h_ref, dmod_ref = refs[6:]
        dproj = jnp.concatenate([p[...] for p in part_refs], axis=1).astype(BF16)
        dproj_ref[...] = dproj
        dh = _dot_nt(dproj, w_ref[...])
        xx = x_ref[...]
        one_sc = 1.0 + sc_ref[0]
        h_ref[...] = (xx * one_sc + sh_ref[0]).astype(BF16)
        dx_ref[...] = dres_ref[...] + dh * one_sc

        @pl.when(pl.program_id(0) % tiles_per_seq == 0)
        def _():
            dmod_ref[...] = jnp.zeros_like(dmod_ref)

        dmod_ref[0, 0:1, :] += jnp.sum(dh, axis=0, keepdims=True)
        dmod_ref[0, 1:2, :] += jnp.sum(dh * xx, axis=0, keepdims=True)

    tile = pl.BlockSpec((tm, dm), lambda i: (i, 0))
    mod = _mod_spec(tiles_per_seq, dm)
    return pl.pallas_call(
        body, name=name, grid=(tokens // tm,),
        in_specs=[pl.BlockSpec((tm, wdt), lambda i: (i, 0)) for wdt in widths]
        + [tile, tile, mod, mod, _const_spec(w_in.shape)],
        out_specs=[tile, pl.BlockSpec((tm, total), lambda i: (i, 0)), tile,
                   pl.BlockSpec((1, 2, dm), lambda i: (i // tiles_per_seq, 0, 0))],
        out_shape=[jax.ShapeDtypeStruct((tokens, dm), F32), jax.ShapeDtypeStruct((tokens, total), BF16),
                   jax.ShapeDtypeStruct((tokens, dm), BF16), jax.ShapeDtypeStruct((nseq, 2, dm), F32)],
        compiler_params=_params(("arbitrary",)),
    )(*parts, dres, x1, sh, sc, w_in)


def _heads(t, nseq, seq, nheads):
    return t.reshape(nseq, seq, nheads, HEAD_DIM).transpose(0, 2, 1, 3)


def _unheads(t):
    nseq, nheads, seq, _ = t.shape
    return t.transpose(0, 2, 1, 3).reshape(nseq * seq, nheads * HEAD_DIM)


def _rope_tables(positions):
    inv_freq = jnp.power(jnp.float32(ROPE_THETA), -jnp.arange(0, ROT_DIM, 2, dtype=F32) / ROT_DIM)
    ang = positions.astype(F32)[..., None] * inv_freq
    cos, sin = jnp.cos(ang), jnp.sin(ang)
    rest = positions.shape + (HEAD_DIM - ROT_DIM,)
    cos_t = jnp.concatenate([cos, cos, jnp.ones(rest, F32)], axis=-1)
    sin_t = jnp.concatenate([-sin, sin, jnp.zeros(rest, F32)], axis=-1)
    return cos_t, sin_t


def kernel(x, c, positions, w_ada, b_ada, ffn1_w_gate_up, ffn1_w_down, ln1_g, ln1_b, w_in, conv_w, attn_sinks, w_out, ln2_g, ln2_b, ffn2_w_gate_up, ffn2_w_down, ln3_g, ln3_b, loss_target, m_w_ada, m_b_ada, m_ffn1_w_gate_up, m_ffn1_w_down, m_ln1_g, m_ln1_b, m_w_in, m_conv_w, m_attn_sinks, m_w_out, m_ln2_g, m_ln2_b, m_ffn2_w_gate_up, m_ffn2_w_down, m_ln3_g, m_ln3_b, v_w_ada, v_b_ada, v_ffn1_w_gate_up, v_ffn1_w_down, v_ln1_g, v_ln1_b, v_w_in, v_conv_w, v_attn_sinks, v_w_out, v_ln2_g, v_ln2_b, v_ffn2_w_gate_up, v_ffn2_w_down, v_ln3_g, v_ln3_b):
    nseq, seq, dm = x.shape
    tokens = nseq * seq
    dev = 4 * lax.axis_index("x") + 2 * lax.axis_index("y") + lax.axis_index("c")
    ada_cols = w_ada.shape[2]
    ff = ffn1_w_down.shape[1] * N_DEV
    fc = ff // 4
    in_cols = w_in.shape[2]
    conv_cols = conv_w.shape[2]

    shards = [ffn1_w_gate_up[0], ffn1_w_down[0], w_in[0], w_out[0], ffn2_w_gate_up[0], ffn2_w_down[0]]
    c_all, convw_all = all_gather([c, conv_w[0]], "gather_cond")
    wgu1, wd1, win, wout, wgu2, wd2 = all_gather([s.astype(BF16) for s in shards], "gather_weights")
    c_all = c_all.reshape(N_DEV * nseq, dm)
    convw_full = convw_all.transpose(1, 0, 2).reshape(CONV_TAPS, N_DEV * conv_cols)
    wd1 = wd1.reshape(4, fc, dm)
    wd2 = wd2.reshape(4, fc, dm)
    win = win.transpose(1, 0, 2).reshape(dm, N_DEV * in_cols)
    wout = wout.reshape(dm, dm)

    b_cols = lax.dynamic_slice(b_ada, (0, dev * ada_cols), (1, ada_cols))
    cond_all, mod_cols = ada_fwd(c_all, w_ada[0], b_cols, "ada_fwd")
    (mod_all,) = all_gather([mod_cols], "gather_mod")
    mod = lax.dynamic_slice(mod_all, (0, dev * nseq, 0), (N_DEV, nseq, ada_cols))
    mod = mod.transpose(1, 0, 2).reshape(nseq, 9, 1, dm)
    sh1, sc1, g1, sh2, sc2, g2, sh3, sc3, g3 = [mod[:, i] for i in range(9)]

    x0 = x.reshape(tokens, dm)
    x1, r1, gu1, f1 = ffn_fwd(x0, sh1, sc1, g1, wgu1, wd1, ln1_g, ln1_b, seq, "ffn1_fwd")
    q, k, v, u, bg, cg = proj_fwd(x1, sh2, sc2, win, seq, "proj_fwd")
    cos_t, sin_t = _rope_tables(positions)
    qh, kh, vh = _heads(q, nseq, seq, N_Q_HEADS), _heads(k, nseq, seq, N_KV_HEADS), _heads(v, nseq, seq, N_KV_HEADS)
    sinks = attn_sinks[0]
    attn = _unheads(attn_fwd(qh, kh, vh, cos_t, sin_t, sinks, "attn_fwd"))
    conv = conv_fwd(u, bg, cg, convw_full, seq, "conv_fwd")
    x2, r2, mixin, mix = out_fwd(x1, attn, conv, g2, wout, ln2_g, ln2_b, seq, "out_fwd")
    target = loss_target.reshape(tokens, dm)
    dy3, loss_part, r3, gu3, f3 = ffn_fwd(x2, sh3, sc3, g3, wgu2, wd2, ln3_g, ln3_b, seq, "ffn2_fwd", target=target)

    dx2, dgu3, df3, a3, h3, dln3, dmod3 = ffn_bwd(dy3, r3, x2, f3, gu3, sh3, sc3, g3, wgu2, wd2, ln3_g, seq, "ffn2_bwd")
    g_wgu2 = tn_matmul(h3[None], dgu3, "ffn2_dwgu").reshape(N_DEV, dm, fc)
    g_wd2 = tn_matmul(a3, df3[None], "ffn2_dwd").reshape(N_DEV, ff // N_DEV, dm)
    dres2, dattn, dconv, dmix, dln2, dg2 = out_bwd(dx2, r2, mix, g2, wout, ln2_g, seq, "out_bwd")
    g_wout = tn_matmul(mixin[None], dmix[None], "dwout").reshape(N_DEV, dm // N_DEV, dm)
    du, dbg, dcg, dconvw = conv_bwd(dconv, u, bg, cg, convw_full, seq, "conv_bwd")
    dqh, dkh, dvh, dsink_rows = attn_bwd(qh, kh, vh, _heads(dattn, nseq, seq, N_Q_HEADS), cos_t, sin_t, sinks, "attn_bwd")
    parts = [_unheads(dqh), _unheads(dkh), _unheads(dvh), du, dbg, dcg]
    dx1, dproj, h2, dmod2 = proj_bwd(parts, dres2, x1, sh2, sc2, win, seq, "proj_bwd")
    g_win = tn_matmul(h2[None], dproj[None], "dwin").reshape(dm, N_DEV, in_cols).transpose(1, 0, 2)
    dx0, dgu1, df1, a1, h1, dln1, dmod1 = ffn_bwd(dx1, r1, x0, f1, gu1, sh1, sc1, g1, wgu1, wd1, ln1_g, seq, "ffn1_bwd")
    g_wgu1 = tn_matmul(h1[None], dgu1, "ffn1_dwgu").reshape(N_DEV, dm, fc)
    g_wd1 = tn_matmul(a1, df1[None], "ffn1_dwd").reshape(N_DEV, ff // N_DEV, dm)

    grads = {
        "ffn1_w_gate_up": reduce_scatter(g_wgu1, "rs_wgu1"), "ffn1_w_down": reduce_scatter(g_wd1, "rs_wd1"),
        "w_in": reduce_scatter(g_win, "rs_win"), "w_out": reduce_scatter(g_wout, "rs_wout"),
        "ffn2_w_gate_up": reduce_scatter(g_wgu2, "rs_wgu2"), "ffn2_w_down": reduce_scatter(g_wd2, "rs_wd2"),
    }

    dmod = jnp.concatenate([dmod1, dmod2, dg2, dmod3], axis=1).reshape(nseq, 9 * dm)
    (dmod_all,) = all_gather([dmod], "gather_dmod")
    dmod_cols = lax.dynamic_slice(dmod_all.reshape(N_DEV * nseq, 9 * dm), (0, dev * ada_cols), (N_DEV * nseq, ada_cols))
    grads["w_ada"], gb_cols = ada_bwd(cond_all, dmod_cols, "ada_bwd")

    dsinks = jnp.sum(dsink_rows.reshape(nseq, N_Q_HEADS, ATTN_BLOCK), axis=(0, 2))
    small = jnp.zeros((8, dm), F32)
    small = small.at[0:2].set(dln1).at[2:4].set(dln2).at[4:6].set(dln3)
    small = small.at[6, 0:N_Q_HEADS].set(dsinks).at[7, 0].set(loss_part[0, 0])
    small_all, dconvw_all, gb_all = all_gather([small, dconvw, gb_cols], "gather_small")
    small_sum = sum_devices(small_all, "sum_small")
    dconvw_sum = sum_devices(dconvw_all, "sum_convw")
    loss = small_sum[7, 0]
    grads["b_ada"] = gb_all.reshape(1, N_DEV * ada_cols)
    grads["conv_w"] = lax.dynamic_slice(dconvw_sum, (0, dev * conv_cols), (CONV_TAPS, conv_cols))
    grads["attn_sinks"] = small_sum[6:7, 0:N_Q_HEADS]
    for i, nm in enumerate(["ln1_g", "ln1_b", "ln2_g", "ln2_b", "ln3_g", "ln3_b"]):
        grads[nm] = small_sum[i:i + 1]

    given = dict(w_ada=(w_ada, m_w_ada, v_w_ada), b_ada=(b_ada, m_b_ada, v_b_ada),
                 ffn1_w_gate_up=(ffn1_w_gate_up, m_ffn1_w_gate_up, v_ffn1_w_gate_up),
                 ffn1_w_down=(ffn1_w_down, m_ffn1_w_down, v_ffn1_w_down),
                 ln1_g=(ln1_g, m_ln1_g, v_ln1_g), ln1_b=(ln1_b, m_ln1_b, v_ln1_b),
                 w_in=(w_in, m_w_in, v_w_in), conv_w=(conv_w, m_conv_w, v_conv_w),
                 attn_sinks=(attn_sinks, m_attn_sinks, v_attn_sinks), w_out=(w_out, m_w_out, v_w_out),
                 ln2_g=(ln2_g, m_ln2_g, v_ln2_g), ln2_b=(ln2_b, m_ln2_b, v_ln2_b),
                 ffn2_w_gate_up=(ffn2_w_gate_up, m_ffn2_w_gate_up, v_ffn2_w_gate_up),
                 ffn2_w_down=(ffn2_w_down, m_ffn2_w_down, v_ffn2_w_down),
                 ln3_g=(ln3_g, m_ln3_g, v_ln3_g), ln3_b=(ln3_b, m_ln3_b, v_ln3_b))
    order = ["w_ada", "b_ada", "ffn1_w_gate_up", "ffn1_w_down", "ln1_g", "ln1_b", "w_in", "conv_w", "attn_sinks",
             "w_out", "ln2_g", "ln2_b", "ffn2_w_gate_up", "ffn2_w_down", "ln3_g", "ln3_b"]
    out_g, out_d, out_m, out_v = [], [], [], []
    for nm in order:
        w, m, v = given[nm]
        shape = w.shape
        two_d = (shape[-2], shape[-1])
        g2d = grads[nm].reshape(two_d)
        d, nm_new, nv_new = adamw(w.reshape(two_d), g2d, m.reshape(two_d), v.reshape(two_d), "adamw_" + nm)
        out_g.append(g2d.reshape(shape))
        out_d.append(d.reshape(shape))
        out_m.append(nm_new.reshape(shape))
        out_v.append(nv_new.reshape(shape))
    grad_x = dx0.reshape(nseq, seq, dm)
    return (loss, grad_x, *out_g, *out_d, *out_m, *out_v)
```

```python
import functools

import jax
import jax.numpy as jnp
from jax import lax
from jax.experimental import pallas as pl
from jax.experimental.pallas import tpu as pltpu

F32 = jnp.float32
BF16 = jnp.bfloat16
MESH = pl.DeviceIdType.MESH

N_DEV = 8
N_CHIP = 4
HEAD_DIM = 64
N_Q_HEADS = 8
N_KV_HEADS = 2
GQA_GROUP = N_Q_HEADS // N_KV_HEADS
ATTN_BLOCK = 128
ROT_DIM = 16
ROPE_THETA = 500000.0
CONV_TAPS = 3
LN_EPS = 1e-5
DN_ALPHA = 2.0 ** 0.25
ADAM_LR = 0.001
ADAM_B1 = 0.9
ADAM_B2 = 0.999
ADAM_EPS = 1e-08
ADAM_WD = 0.01
ADAM_STEP = 10
NEG_BIG = -1e30

VMEM_LIMIT = 56 * 1024 * 1024
TOKEN_TILE = 256
TN_TOKEN_TILE = 512


def _params(semantics=None, vmem=VMEM_LIMIT):
    return pltpu.CompilerParams(dimension_semantics=semantics, vmem_limit_bytes=vmem)


def _dot(a, b):
    return jnp.dot(a, b, preferred_element_type=F32)


def _dot_nt(a, b):
    return lax.dot_general(a, b, (((1,), (1,)), ((), ())), preferred_element_type=F32)


def _dot_tn(a, b):
    return lax.dot_general(a, b, (((0,), (0,)), ((), ())), preferred_element_type=F32)


def _sigmoid(x):
    return 1.0 / (1.0 + jnp.exp(-x))


def _ln_stats(r):
    mu = jnp.mean(r, axis=-1, keepdims=True)
    d = r - mu
    var = jnp.mean(d * d, axis=-1, keepdims=True)
    rstd = lax.rsqrt(var + LN_EPS)
    return d * rstd, rstd


def _ln_bwd(dy, r, g):
    xhat, rstd = _ln_stats(r)
    dxhat = dy * g
    c1 = jnp.mean(dxhat, axis=-1, keepdims=True)
    c2 = jnp.mean(dxhat * xhat, axis=-1, keepdims=True)
    dr = rstd * (dxhat - c1 - xhat * c2)
    return dr, jnp.sum(dy * xhat, axis=0, keepdims=True), jnp.sum(dy, axis=0, keepdims=True)


def _const_spec(shape):
    nd = len(shape)
    return pl.BlockSpec(shape, lambda *_: (0,) * nd, pipeline_mode=pl.Buffered(1))


def all_gather(arrs, name):
    n = len(arrs)

    def body(*refs):
        ins, outs = refs[:n], refs[n:2 * n]
        send_sems, recv_sems, local_sems = refs[2 * n:]
        x, y, c = lax.axis_index("x"), lax.axis_index("y"), lax.axis_index("c")
        me, sibling = (x, y, c), (x, y, 1 - c)
        chips = [(1 - x, y), (x, 1 - y), (1 - x, 1 - y)]

        def slot(i, p):
            return outs[i].at[4 * p[0] + 2 * p[1] + p[2]]

        def copy(i, k, block, to, src=None):
            return pltpu.make_async_remote_copy(
                src_ref=slot(i, block) if src is None else src, dst_ref=slot(i, block),
                send_sem=send_sems.at[i, k], recv_sem=recv_sems.at[i, k],
                device_id=to, device_id_type=MESH)

        mine = [pltpu.make_async_copy(ins[i], slot(i, me), local_sems.at[i]) for i in range(n)]
        for cp in mine:
            cp.start()
        first = []
        for i in range(n):
            first.append(copy(i, 0, me, sibling, src=ins[i]))
            first += [copy(i, 1 + j, me, (*chip, c), src=ins[i]) for j, chip in enumerate(chips)]
        for cp in first:
            cp.start()
        passed = []
        for j, chip in enumerate(chips):
            for i in range(n):
                copy(i, 1 + j, (*chip, c), me).wait_recv()
                cp = copy(i, 4 + j, (*chip, c), sibling)
                cp.start()
                passed.append(cp)
        for i in range(n):
            copy(i, 0, sibling, me).wait_recv()
            for j, chip in enumerate(chips):
                copy(i, 4 + j, (*chip, 1 - c), me).wait_recv()
        for cp in first + passed:
            cp.wait_send()
        for cp in mine:
            cp.wait()

    any_spec = pl.BlockSpec(memory_space=pl.ANY)
    return pl.pallas_call(
        body, name=name,
        out_shape=[jax.ShapeDtypeStruct((N_DEV, *a.shape), a.dtype) for a in arrs],
        in_specs=[any_spec] * n, out_specs=[any_spec] * n,
        scratch_shapes=[pltpu.SemaphoreType.DMA((n, 7)), pltpu.SemaphoreType.DMA((n, 7)),
                        pltpu.SemaphoreType.DMA((n,))],
    )(*arrs)


RS_ROWS = 32


def reduce_scatter(g, name):
    _, rows, cols = g.shape
    nblk = rows // RS_ROWS
    assert nblk * RS_ROWS == rows

    def body(g_ref, out_ref, r1_ref, p_ref, r2_ref, send_sems, recv_sems):
        x, y, c = lax.axis_index("x"), lax.axis_index("y"), lax.axis_index("c")
        sibling = (x, y, 1 - c)
        q_me = 2 * x + y
        swaps = []
        for q in range(N_CHIP):
            cp = pltpu.make_async_remote_copy(
                src_ref=g_ref.at[2 * q + (1 - c)], dst_ref=r1_ref.at[q],
                send_sem=send_sems.at[q], recv_sem=recv_sems.at[q], device_id=sibling, device_id_type=MESH)
            cp.start()
            swaps.append(cp)
        for cp in swaps:
            cp.wait_recv()

        def pair_sum(i, carry):
            r = pl.ds(pl.multiple_of(i * RS_ROWS, RS_ROWS), RS_ROWS)
            for q in range(N_CHIP):
                p_ref[q, r, :] = (g_ref[2 * q + c, r, :].astype(F32) + r1_ref[q, r, :].astype(F32)).astype(BF16)
            return carry

        lax.fori_loop(0, nblk, pair_sum, 0)
        chips = [(1 - x, y), (x, 1 - y), (1 - x, 1 - y)]
        sends = []
        for k, chip in enumerate(chips):
            cp = pltpu.make_async_remote_copy(
                src_ref=p_ref.at[2 * chip[0] + chip[1]], dst_ref=r2_ref.at[k],
                send_sem=send_sems.at[N_CHIP + k], recv_sem=recv_sems.at[N_CHIP + k],
                device_id=(*chip, c), device_id_type=MESH)
            cp.start()
            sends.append(cp)
        for cp in sends:
            cp.wait_recv()

        def total(i, carry):
            r = pl.ds(pl.multiple_of(i * RS_ROWS, RS_ROWS), RS_ROWS)
            acc = g_ref[2 * q_me + c, r, :].astype(F32) + r1_ref[q_me, r, :].astype(F32)
            for k in range(3):
                acc = acc + r2_ref[k, r, :].astype(F32)
            out_ref[r, :] = acc
            return carry

        lax.fori_loop(0, nblk, total, 0)
        for cp in swaps + sends:
            cp.wait_send()

    vmem = pl.BlockSpec(memory_space=pltpu.VMEM)
    return pl.pallas_call(
        body, name=name,
        out_shape=jax.ShapeDtypeStruct((rows, cols), F32),
        in_specs=[vmem], out_specs=vmem,
        scratch_shapes=[pltpu.VMEM((N_CHIP, rows, cols), BF16), pltpu.VMEM((N_CHIP, rows, cols), BF16),
                        pltpu.VMEM((3, rows, cols), BF16),
                        pltpu.SemaphoreType.DMA((N_CHIP + 3,)), pltpu.SemaphoreType.DMA((N_CHIP + 3,))],
        compiler_params=_params(),
    )(g)


def _place():
    x, y, c = lax.axis_index("x"), lax.axis_index("y"), lax.axis_index("c")
    return x, y, c, [(1 - x, y), (x, 1 - y), (1 - x, 1 - y)]


def _slot(p):
    return 4 * p[0] + 2 * p[1] + p[2]


class _Job:
    def __init__(self, ins, outs, nsem, copies, aliases=None, local=None):
        self.ins, self.outs, self.nsem, self.copies = list(ins), list(outs), nsem, copies
        self.aliases = aliases or {}
        self.local = local

    def scratch(self):
        s = [pltpu.SemaphoreType.DMA(self.nsem), pltpu.SemaphoreType.DMA(self.nsem)]
        if self.local is not None:
            s.append(pltpu.SemaphoreType.DMA((len(self.ins),)))
        return s

    def start(self, ins, outs, sems):
        if self.local is not None:
            for cp in self.local(ins, outs, sems[2]):
                cp.start()
        for cp in self.copies(ins, outs, sems[0], sems[1])[0]:
            cp.start()

    def finish(self, ins, outs, sems):
        started, awaited = self.copies(ins, outs, sems[0], sems[1])
        for cp in awaited:
            cp.wait_recv()
        for cp in started:
            cp.wait_send()
        if self.local is not None:
            for cp in self.local(ins, outs, sems[2]):
                cp.wait()


def _remote(src, dst, send, recv, idx, to):
    return pltpu.make_async_remote_copy(src_ref=src, dst_ref=dst, send_sem=send.at[idx], recv_sem=recv.at[idx],
                                        device_id=to, device_id_type=MESH)


def gather_spread_job(shards):
    def copies(ins, outs, send, recv):
        x, y, c, chips = _place()
        me = (x, y, c)
        peers = [(x, y, 1 - c)] + [(*chip, c) for chip in chips]
        started, awaited = [], []
        for i, (src, dst) in enumerate(zip(ins, outs)):
            for k, peer in enumerate(peers):
                started.append(_remote(src, dst.at[_slot(me)], send, recv, (i, k), peer))
                awaited.append(_remote(src, dst.at[_slot(peer)], send, recv, (i, k), peer))
        return started, awaited

    def local(ins, outs, sems):
        x, y, c, _ = _place()
        return [pltpu.make_async_copy(src, dst.at[_slot((x, y, c))], sems.at[i])
                for i, (src, dst) in enumerate(zip(ins, outs))]

    outs = [jax.ShapeDtypeStruct((N_DEV, *a.shape), a.dtype) for a in shards]
    return _Job(shards, outs, (len(shards), 4), copies, local=local)


def gather_forward_job(fulls):
    def copies(ins, outs, send, recv):
        x, y, c, chips = _place()
        started, awaited = [], []
        for i, buf in enumerate(outs):
            for j, chip in enumerate(chips):
                mine, theirs = buf.at[_slot((*chip, c))], buf.at[_slot((*chip, 1 - c))]
                started.append(_remote(mine, mine, send, recv, (i, j), (x, y, 1 - c)))
                awaited.append(_remote(theirs, theirs, send, recv, (i, j), (x, y, 1 - c)))
        return started, awaited

    outs = [jax.ShapeDtypeStruct(a.shape, a.dtype) for a in fulls]
    return _Job(fulls, outs, (len(fulls), 3), copies, aliases={i: i for i in range(len(fulls))})


def swap_job(gs):
    def copies(ins, outs, send, recv):
        x, y, c, _ = _place()
        started, awaited = [], []
        for i, (g, r1) in enumerate(zip(ins, outs)):
            for q in range(N_CHIP):
                started.append(_remote(g.at[2 * q + (1 - c)], r1.at[q], send, recv, (i, q), (x, y, 1 - c)))
                awaited.append(_remote(g.at[2 * q + c], r1.at[q], send, recv, (i, q), (x, y, 1 - c)))
        return started, awaited

    outs = [jax.ShapeDtypeStruct((N_CHIP, *g.shape[1:]), g.dtype) for g in gs]
    return _Job(gs, outs, (len(gs), N_CHIP), copies)


def chip_exchange_job(ps):
    def copies(ins, outs, send, recv):
        x, y, c, chips = _place()
        started, awaited = [], []
        for i, (p, r2) in enumerate(zip(ins, outs)):
            for k, chip in enumerate(chips):
                started.append(_remote(p.at[2 * chip[0] + chip[1]], r2.at[k], send, recv, (i, k), (*chip, c)))
                awaited.append(_remote(p.at[2 * x + y], r2.at[k], send, recv, (i, k), (*chip, c)))
        return started, awaited

    outs = [jax.ShapeDtypeStruct((3, *p.shape[1:]), p.dtype) for p in ps]
    return _Job(ps, outs, (len(ps), 3), copies)


def _call(body, job, *, name, grid, in_specs, out_specs, out_shape, args, scratch_shapes=(), vmem=VMEM_LIMIT):
    if job is None:
        res = pl.pallas_call(
            body, name=name, grid=grid, in_specs=in_specs, out_specs=out_specs, out_shape=out_shape,
            scratch_shapes=list(scratch_shapes), compiler_params=_params(("arbitrary",) * len(grid), vmem),
        )(*args)
        return res, []
    n_in, n_out, n_scr = len(in_specs), len(out_specs), len(scratch_shapes)
    j_in, j_out = len(job.ins), len(job.outs)

    def with_copies(*refs):
        at = 0
        ins = refs[at:at + n_in]; at += n_in
        jins = refs[at:at + j_in]; at += j_in
        outs = refs[at:at + n_out]; at += n_out
        jouts = refs[at:at + j_out]; at += j_out
        scr = refs[at:at + n_scr]; at += n_scr
        sems = refs[at:]
        ids = [pl.program_id(d) for d in range(len(grid))]
        first = functools.reduce(jnp.logical_and, [i == 0 for i in ids])
        last = functools.reduce(jnp.logical_and, [i == n - 1 for i, n in zip(ids, grid)])

        @pl.when(first)
        def _():
            job.start(jins, jouts, sems)

        body(*ins, *outs, *scr)

        @pl.when(last)
        def _():
            job.finish(jins, jouts, sems)

    any_spec = pl.BlockSpec(memory_space=pl.ANY)
    res = pl.pallas_call(
        with_copies, name=name, grid=grid,
        in_specs=list(in_specs) + [any_spec] * j_in, out_specs=list(out_specs) + [any_spec] * j_out,
        out_shape=list(out_shape) + list(job.outs),
        input_output_aliases={n_in + i: n_out + o for i, o in job.aliases.items()},
        scratch_shapes=list(scratch_shapes) + job.scratch(),
        compiler_params=_params(("arbitrary",) * len(grid), vmem),
    )(*args, *job.ins)
    return res[:n_out], res[n_out:]


def run_job(job, name):
    def body(*refs):
        j_in, j_out = len(job.ins), len(job.outs)
        ins, outs, sems = refs[:j_in], refs[j_in:j_in + j_out], refs[j_in + j_out:]
        job.start(ins, outs, sems)
        job.finish(ins, outs, sems)

    any_spec = pl.BlockSpec(memory_space=pl.ANY)
    return pl.pallas_call(
        body, name=name, in_specs=[any_spec] * len(job.ins), out_specs=[any_spec] * len(job.outs),
        out_shape=list(job.outs), input_output_aliases=dict(job.aliases), scratch_shapes=job.scratch(),
    )(*job.ins)


def pair_sum(g, r1, name):
    _, rows, cols = g.shape
    rb = rows if rows <= 512 else 256
    assert rows % rb == 0

    def body(g_ref, r1_ref, p_ref, own_ref):
        x, y, c, _ = _place()
        s = g_ref[c].astype(F32) + r1_ref[0].astype(F32)
        p_ref[0] = s.astype(BF16)

        @pl.when(pl.program_id(1) == 2 * x + y)
        def _():
            own_ref[...] = s

    return pl.pallas_call(
        body, name=name, grid=(rows // rb, N_CHIP),
        in_specs=[pl.BlockSpec((2, rb, cols), lambda i, q: (q, i, 0)), pl.BlockSpec((1, rb, cols), lambda i, q: (q, i, 0))],
        out_specs=[pl.BlockSpec((1, rb, cols), lambda i, q: (q, i, 0)), pl.BlockSpec((rb, cols), lambda i, q: (i, 0))],
        out_shape=[jax.ShapeDtypeStruct((N_CHIP, rows, cols), BF16), jax.ShapeDtypeStruct((rows, cols), F32)],
        compiler_params=_params(("arbitrary", "arbitrary")),
    )(g, r1)


def sum_devices(a, name):
    def body(a_ref, o_ref):
        acc = a_ref[0]
        for d in range(1, N_DEV):
            acc = acc + a_ref[d]
        o_ref[...] = acc

    return pl.pallas_call(body, name=name, out_shape=jax.ShapeDtypeStruct(a.shape[1:], F32))(a)


def adamw(w, g, m, v, name, others=None):
    rows, cols = w.shape
    rb = rows
    for cand in (256, 128, 64, 32, 16, 8):
        if rows % cand == 0:
            rb = cand
            break

    def body(*refs):
        if others is None:
            w_ref, g_ref, m_ref, v_ref, go_ref, d_ref, nm_ref, nv_ref = refs
            gg = g_ref[...]
        else:
            w_ref, g_ref, m_ref, v_ref, r2_ref, go_ref, d_ref, nm_ref, nv_ref = refs
            gg = g_ref[...]
            for k in range(3):
                gg = gg + r2_ref[k].astype(F32)
        go_ref[...] = gg
        nm = ADAM_B1 * m_ref[...] + (1.0 - ADAM_B1) * gg
        nv = ADAM_B2 * v_ref[...] + (1.0 - ADAM_B2) * (gg * gg)
        m_hat = nm / (1.0 - ADAM_B1 ** ADAM_STEP)
        v_hat = nv / (1.0 - ADAM_B2 ** ADAM_STEP)
        d_ref[...] = -ADAM_LR * (m_hat / (jnp.sqrt(v_hat) + ADAM_EPS) + ADAM_WD * w_ref[...])
        nm_ref[...] = nm
        nv_ref[...] = nv

    spec = pl.BlockSpec((rb, cols), lambda i: (i, 0))
    out = jax.ShapeDtypeStruct((rows, cols), F32)
    in_specs, args = [spec] * 4, [w, g, m, v]
    if others is not None:
        in_specs.append(pl.BlockSpec((3, rb, cols), lambda i: (0, i, 0)))
        args.append(others)
    return pl.pallas_call(
        body, name=name, grid=(rows // rb,), in_specs=in_specs, out_specs=[spec] * 4,
        out_shape=[out, out, out, out], compiler_params=_params(("parallel",)),
    )(*args)


def ada_fwd(c_all, w_cols, b_cols, name):
    def body(c_ref, w_ref, b_ref, cond_ref, mod_ref):
        cc = c_ref[...]
        cond = (cc * _sigmoid(cc)).astype(BF16)
        cond_ref[...] = cond
        mod_ref[...] = _dot(cond, w_ref[...].astype(BF16)) + b_ref[...]

    n, cols = c_all.shape[0], w_cols.shape[1]
    return pl.pallas_call(
        body, name=name,
        out_shape=[jax.ShapeDtypeStruct(c_all.shape, BF16), jax.ShapeDtypeStruct((n, cols), F32)],
        compiler_params=_params(),
    )(c_all, w_cols, b_cols)


def ada_bwd(cond_all, dmod_cols, name):
    def body(c_ref, d_ref, gw_ref, gb_ref):
        d = d_ref[...]
        gw_ref[...] = _dot_tn(c_ref[...], d.astype(BF16))
        gb_ref[...] = jnp.sum(d, axis=0, keepdims=True)

    dm, cols = cond_all.shape[1], dmod_cols.shape[1]
    return pl.pallas_call(
        body, name=name,
        out_shape=[jax.ShapeDtypeStruct((dm, cols), F32), jax.ShapeDtypeStruct((1, cols), F32)],
        compiler_params=_params(),
    )(cond_all, dmod_cols)


def _mod_spec(tiles_per_seq, dm):
    return pl.BlockSpec((1, 1, dm), lambda i: (i // tiles_per_seq, 0, 0))


def ffn_fwd(x, sh, sc, gt, wgu, wd, ln_g, ln_b, seq, name, target=None, job=None):
    tokens, dm = x.shape
    fc = wgu.shape[2]
    tm = min(TOKEN_TILE, seq)
    tiles_per_seq = seq // tm
    with_loss = target is not None

    def body(*refs):
        if with_loss:
            (x_ref, sh_ref, sc_ref, gt_ref, wgu_ref, wd_ref, lg_ref, lb_ref, t_ref,
             xo_ref, loss_ref, r_ref, gu_ref, f_ref) = refs
        else:
            (x_ref, sh_ref, sc_ref, gt_ref, wgu_ref, wd_ref, lg_ref, lb_ref,
             xo_ref, r_ref, gu_ref, f_ref) = refs
        xx = x_ref[...]
        h = (xx * (1.0 + sc_ref[0]) + sh_ref[0]).astype(BF16)
        acc = jnp.zeros((tm, dm), F32)
        for k in range(4):
            gk = _dot(h, wgu_ref[k])
            uk = _dot(h, wgu_ref[k + 4])
            gu_ref[k] = gk.astype(BF16)
            gu_ref[k + 4] = uk.astype(BF16)
            a = (gk * _sigmoid(gk) * uk).astype(BF16)
            acc = acc + _dot(a, wd_ref[k])
        f_ref[...] = acc.astype(BF16)
        r = DN_ALPHA * xx + (0.5 * (1.0 + gt_ref[0])) * acc
        r_ref[...] = r
        xhat, _ = _ln_stats(r)
        yy = xhat * lg_ref[...] + lb_ref[...]
        if with_loss:
            err = yy - t_ref[...]
            xo_ref[...] = err * (1.0 / dm)

            @pl.when(pl.program_id(0) == 0)
            def _():
                loss_ref[...] = jnp.zeros_like(loss_ref)

            loss_ref[...] += jnp.full((1, 128), (0.5 / dm) * jnp.sum(err * err), F32)
        else:
            xo_ref[...] = yy

    tile = pl.BlockSpec((tm, dm), lambda i: (i, 0))
    mod = _mod_spec(tiles_per_seq, dm)
    in_specs = [tile, mod, mod, mod, _const_spec(wgu.shape), _const_spec(wd.shape),
                _const_spec((1, dm)), _const_spec((1, dm))]
    args = [x, sh, sc, gt, wgu, wd, ln_g, ln_b]
    out_specs = [tile]
    out_shape = [jax.ShapeDtypeStruct((tokens, dm), F32)]
    if with_loss:
        in_specs.append(tile)
        args.append(target)
        out_specs.append(pl.BlockSpec((1, 128), lambda i: (0, 0)))
        out_shape.append(jax.ShapeDtypeStruct((1, 128), F32))
    out_specs += [tile, pl.BlockSpec((8, tm, fc), lambda i: (0, i, 0)), tile]
    out_shape += [jax.ShapeDtypeStruct((tokens, dm), F32), jax.ShapeDtypeStruct((8, tokens, fc), BF16),
                  jax.ShapeDtypeStruct((tokens, dm), BF16)]
    return _call(body, job, name=name, grid=(tokens // tm,), in_specs=in_specs, out_specs=out_specs,
                 out_shape=out_shape, args=args)


def ffn_bwd(dy, r, x, f, gu, sh, sc, gt, wgu, wd, ln_g, seq, name, job=None):
    tokens, dm = x.shape
    fc = wgu.shape[2]
    tm = min(TOKEN_TILE, seq)
    tiles_per_seq = seq // tm
    nseq = tokens // seq

    def body(dy_ref, r_ref, x_ref, f_ref, gu_ref, sh_ref, sc_ref, gt_ref, wgu_ref, wd_ref, lg_ref,
             dx_ref, dgu_ref, df_ref, a_ref, h_ref, dln_ref, dmod_ref):
        i = pl.program_id(0)
        dr, dgain, dbias = _ln_bwd(dy_ref[...], r_ref[...], lg_ref[...])

        @pl.when(i == 0)
        def _():
            dln_ref[...] = jnp.zeros_like(dln_ref)

        @pl.when(i % tiles_per_seq == 0)
        def _():
            dmod_ref[...] = jnp.zeros_like(dmod_ref)

        dln_ref[0:1, :] += dgain
        dln_ref[1:2, :] += dbias
        df32 = (0.5 * (1.0 + gt_ref[0])) * dr
        df = df32.astype(BF16)
        df_ref[...] = df
        dgate = jnp.sum(dr * (0.5 * f_ref[...].astype(F32)), axis=0, keepdims=True)
        xx = x_ref[...]
        one_sc = 1.0 + sc_ref[0]
        h = (xx * one_sc + sh_ref[0]).astype(BF16)
        h_ref[...] = h
        dh = jnp.zeros((tm, dm), F32)
        for k in range(4):
            da = _dot_nt(df, wd_ref[k])
            gk = gu_ref[k].astype(F32)
            uk = gu_ref[k + 4].astype(F32)
            sg = _sigmoid(gk)
            sil = gk * sg
            a_ref[k] = (sil * uk).astype(BF16)
            du = (da * sil).astype(BF16)
            dg = (da * uk * (sg * (1.0 + gk * (1.0 - sg)))).astype(BF16)
            dgu_ref[k] = dg
            dgu_ref[k + 4] = du
            dh = dh + _dot_nt(dg, wgu_ref[k]) + _dot_nt(du, wgu_ref[k + 4])
        dx_ref[...] = DN_ALPHA * dr + dh * one_sc
        dmod_ref[0, 0:1, :] += jnp.sum(dh, axis=0, keepdims=True)
        dmod_ref[0, 1:2, :] += jnp.sum(dh * xx, axis=0, keepdims=True)
        dmod_ref[0, 2:3, :] += dgate

    tile = pl.BlockSpec((tm, dm), lambda i: (i, 0))
    mod = _mod_spec(tiles_per_seq, dm)
    gu_spec = pl.BlockSpec((8, tm, fc), lambda i: (0, i, 0))
    return _call(
        body, job, name=name, grid=(tokens // tm,),
        in_specs=[tile, tile, tile, tile, gu_spec, mod, mod, mod, _const_spec(wgu.shape), _const_spec(wd.shape),
                  _const_spec((1, dm))],
        out_specs=[tile, gu_spec, tile, pl.BlockSpec((4, tm, fc), lambda i: (0, i, 0)), tile,
                   pl.BlockSpec((2, dm), lambda i: (0, 0)),
                   pl.BlockSpec((1, 3, dm), lambda i: (i // tiles_per_seq, 0, 0))],
        out_shape=[jax.ShapeDtypeStruct((tokens, dm), F32), jax.ShapeDtypeStruct((8, tokens, fc), BF16),
                   jax.ShapeDtypeStruct((tokens, dm), BF16), jax.ShapeDtypeStruct((4, tokens, fc), BF16),
                   jax.ShapeDtypeStruct((tokens, dm), BF16), jax.ShapeDtypeStruct((2, dm), F32),
                   jax.ShapeDtypeStruct((nseq, 3, dm), F32)],
        args=(dy, r, x, f, gu, sh, sc, gt, wgu, wd, ln_g))


def tn_matmul(a, b, name, job=None):
    na, tokens, kk = a.shape
    nb, _, cc = b.shape
    tt = min(TN_TOKEN_TILE, tokens)
    steps = tokens // tt

    def body(a_ref, b_ref, o_ref, acc_ref):
        t = pl.program_id(2)

        @pl.when(t == 0)
        def _():
            acc_ref[...] = jnp.zeros_like(acc_ref)

        acc_ref[...] += _dot_tn(a_ref[0], b_ref[0])

        @pl.when(t == steps - 1)
        def _():
            o_ref[0, 0] = acc_ref[...].astype(BF16)

    return _call(
        body, job, name=name, grid=(na, nb, steps),
        in_specs=[pl.BlockSpec((1, tt, kk), lambda i, j, t: (i, t, 0)),
                  pl.BlockSpec((1, tt, cc), lambda i, j, t: (j, t, 0))],
        out_specs=[pl.BlockSpec((1, 1, kk, cc), lambda i, j, t: (i, j, 0, 0))],
        out_shape=[jax.ShapeDtypeStruct((na, nb, kk, cc), BF16)],
        scratch_shapes=[pltpu.VMEM((kk, cc), F32)], args=(a, b))


def proj_fwd(x1, sh, sc, w_in, seq, name, job=None):
    tokens, dm = x1.shape
    tm = min(TOKEN_TILE, seq)
    tiles_per_seq = seq // tm
    widths = [N_Q_HEADS * HEAD_DIM, N_KV_HEADS * HEAD_DIM, N_KV_HEADS * HEAD_DIM, 512, 512, 512]
    assert sum(widths) == w_in.shape[1]

    def body(x_ref, sh_ref, sc_ref, w_ref, *outs):
        h = (x_ref[...] * (1.0 + sc_ref[0]) + sh_ref[0]).astype(BF16)
        proj = _dot(h, w_ref[...])
        at = 0
        for o_ref, wdt in zip(outs, widths):
            o_ref[...] = proj[:, at:at + wdt]
            at += wdt

    tile = pl.BlockSpec((tm, dm), lambda i: (i, 0))
    mod = _mod_spec(tiles_per_seq, dm)
    return _call(
        body, job, name=name, grid=(tokens // tm,),
        in_specs=[tile, mod, mod, _const_spec(w_in.shape)],
        out_specs=[pl.BlockSpec((tm, wdt), lambda i: (i, 0)) for wdt in widths],
        out_shape=[jax.ShapeDtypeStruct((tokens, wdt), F32) for wdt in widths],
        args=(x1, sh, sc, w_in))


def _rope_swap_matrix():
    half = ROT_DIM // 2
    i = lax.broadcasted_iota(jnp.int32, (HEAD_DIM, HEAD_DIM), 0)
    j = lax.broadcasted_iota(jnp.int32, (HEAD_DIM, HEAD_DIM), 1)
    hit = ((j < half) & (i == j + half)) | ((j >= half) & (j < ROT_DIM) & (i == j - half))
    return jnp.where(hit, 1.0, 0.0).astype(BF16)


def _swap_halves(v, perm):
    hi = v.astype(BF16)
    rest = v - hi.astype(F32)
    mid = rest.astype(BF16)
    lo = (rest - mid.astype(F32)).astype(BF16)
    return _dot(hi, perm) + _dot(mid, perm) + _dot(lo, perm)


def _rope(v, cos_t, sin_t, perm):
    return v * cos_t + _swap_halves(v, perm) * sin_t


def _unrope(dv, cos_t, sin_t, perm):
    return dv * cos_t + _swap_halves(dv * sin_t, perm)


def _window_mask(rows, first):
    ncol = ATTN_BLOCK if first else 2 * ATTN_BLOCK
    qi = lax.broadcasted_iota(jnp.int32, (rows, ncol), 0) % ATTN_BLOCK
    ki = lax.broadcasted_iota(jnp.int32, (rows, ncol), 1)
    diff = qi - ki if first else qi + ATTN_BLOCK - ki
    return (diff >= 0) & (diff < ATTN_BLOCK)


def _attn_specs(seq):
    q_spec = pl.BlockSpec((1, GQA_GROUP, seq, HEAD_DIM), lambda b, g: (b, g, 0, 0))
    kv_spec = pl.BlockSpec((1, 1, seq, HEAD_DIM), lambda b, g: (b, g, 0, 0))
    rope_spec = pl.BlockSpec((1, seq, HEAD_DIM), lambda b, g: (b, 0, 0))
    sink_spec = pl.BlockSpec((1, GQA_GROUP * ATTN_BLOCK, 1), lambda b, g: (g, 0, 0))
    return q_spec, kv_spec, rope_spec, sink_spec


def _sink_columns(sinks):
    return jnp.repeat(sinks.reshape(N_KV_HEADS, GQA_GROUP), ATTN_BLOCK, axis=1)[:, :, None]


def attn_fwd(q, k, v, cos_t, sin_t, sinks, name, job=None):
    nseq, _, seq, _ = q.shape
    nblk = seq // ATTN_BLOCK
    rows = GQA_GROUP * ATTN_BLOCK
    scale = HEAD_DIM ** -0.5

    def body(q_ref, k_ref, v_ref, cos_ref, sin_ref, sink_ref, o_ref, kr_ref):
        perm = _rope_swap_matrix()
        kr_ref[...] = _rope(k_ref[0, 0], cos_ref[0], sin_ref[0], perm).astype(BF16)
        sink = sink_ref[0]
        for n in range(nblk):
            lo = n * ATTN_BLOCK
            first = n == 0
            win = pl.ds(lo, ATTN_BLOCK) if first else pl.ds(lo - ATTN_BLOCK, 2 * ATTN_BLOCK)
            blk = pl.ds(lo, ATTN_BLOCK)
            cs = jnp.tile(cos_ref[0, blk, :], (GQA_GROUP, 1))
            sn = jnp.tile(sin_ref[0, blk, :], (GQA_GROUP, 1))
            qg = jnp.concatenate([q_ref[0, hh, blk, :] for hh in range(GQA_GROUP)], axis=0)
            qr = _rope(qg, cs, sn, perm).astype(BF16)
            s = _dot_nt(qr, kr_ref[win, :]) * scale
            s = jnp.where(_window_mask(rows, first), s, NEG_BIG)
            m = jnp.maximum(jnp.max(s, axis=-1, keepdims=True), sink)
            p = jnp.exp(s - m)
            denom = jnp.sum(p, axis=-1, keepdims=True) + jnp.exp(sink - m)
            out = _dot((p / denom).astype(BF16), v_ref[0, 0, win, :].astype(BF16))
            for hh in range(GQA_GROUP):
                o_ref[0, hh, blk, :] = out[hh * ATTN_BLOCK:(hh + 1) * ATTN_BLOCK]

    q_spec, kv_spec, rope_spec, sink_spec = _attn_specs(seq)
    return _call(
        body, job, name=name, grid=(nseq, N_KV_HEADS),
        in_specs=[q_spec, kv_spec, kv_spec, rope_spec, rope_spec, sink_spec],
        out_specs=[q_spec], out_shape=[jax.ShapeDtypeStruct(q.shape, F32)],
        scratch_shapes=[pltpu.VMEM((seq, HEAD_DIM), BF16)],
        args=(q, k, v, cos_t, sin_t, _sink_columns(sinks)))


def attn_bwd(q, k, v, do, cos_t, sin_t, sinks, name, job=None):
    nseq, _, seq, _ = q.shape
    nblk = seq // ATTN_BLOCK
    rows = GQA_GROUP * ATTN_BLOCK
    scale = HEAD_DIM ** -0.5

    def body(q_ref, k_ref, v_ref, do_ref, cos_ref, sin_ref, sink_ref, dq_ref, dk_ref, dv_ref, ds_ref,
             kr_ref, dkr_ref):
        perm = _rope_swap_matrix()
        kr_ref[...] = _rope(k_ref[0, 0], cos_ref[0], sin_ref[0], perm).astype(BF16)
        dkr_ref[...] = jnp.zeros_like(dkr_ref)
        dv_ref[...] = jnp.zeros_like(dv_ref)
        sink = sink_ref[0]
        dsink = jnp.zeros((rows, 1), F32)
        for n in range(nblk):
            lo = n * ATTN_BLOCK
            first = n == 0
            win = pl.ds(lo, ATTN_BLOCK) if first else pl.ds(lo - ATTN_BLOCK, 2 * ATTN_BLOCK)
            blk = pl.ds(lo, ATTN_BLOCK)
            cs = jnp.tile(cos_ref[0, blk, :], (GQA_GROUP, 1))
            sn = jnp.tile(sin_ref[0, blk, :], (GQA_GROUP, 1))
            qg = jnp.concatenate([q_ref[0, hh, blk, :] for hh in range(GQA_GROUP)], axis=0)
            dog = jnp.concatenate([do_ref[0, hh, blk, :] for hh in range(GQA_GROUP)], axis=0).astype(BF16)
            qr = _rope(qg, cs, sn, perm).astype(BF16)
            kw = kr_ref[win, :]
            vw = v_ref[0, 0, win, :].astype(BF16)
            s = _dot_nt(qr, kw) * scale
            s = jnp.where(_window_mask(rows, first), s, NEG_BIG)
            m = jnp.maximum(jnp.max(s, axis=-1, keepdims=True), sink)
            p = jnp.exp(s - m)
            e_sink = jnp.exp(sink - m)
            inv = 1.0 / (jnp.sum(p, axis=-1, keepdims=True) + e_sink)
            pn = p * inv
            pn16 = pn.astype(BF16)
            dv_ref[0, 0, win, :] += _dot_tn(pn16, dog)
            dp = _dot_nt(dog, vw)
            delta = jnp.sum(dp * pn, axis=-1, keepdims=True)
            dsink = dsink - e_sink * inv * delta
            ds = (pn * (dp - delta)).astype(BF16)
            dqr = _dot(ds, kw) * scale
            dkr_ref[win, :] += _dot_tn(ds, qr) * scale
            dqg = _unrope(dqr, cs, sn, perm)
            for hh in range(GQA_GROUP):
                dq_ref[0, hh, blk, :] = dqg[hh * ATTN_BLOCK:(hh + 1) * ATTN_BLOCK]
        dk_ref[0, 0] = _unrope(dkr_ref[...], cos_ref[0], sin_ref[0], perm)
        ds_ref[0, 0] = dsink

    q_spec, kv_spec, rope_spec, sink_spec = _attn_specs(seq)
    return _call(
        body, job, name=name, grid=(nseq, N_KV_HEADS),
        in_specs=[q_spec, kv_spec, kv_spec, q_spec, rope_spec, rope_spec, sink_spec],
        out_specs=[q_spec, kv_spec, kv_spec, pl.BlockSpec((1, 1, rows, 1), lambda b, g: (b, g, 0, 0))],
        out_shape=[jax.ShapeDtypeStruct(q.shape, F32), jax.ShapeDtypeStruct(k.shape, F32),
                   jax.ShapeDtypeStruct(k.shape, F32), jax.ShapeDtypeStruct((nseq, N_KV_HEADS, rows, 1), F32)],
        scratch_shapes=[pltpu.VMEM((seq, HEAD_DIM), BF16), pltpu.VMEM((seq, HEAD_DIM), F32)],
        args=(q, k, v, do, cos_t, sin_t, _sink_columns(sinks)))


CONV_COLS = 128


def _shift_down(z, by):
    t = lax.broadcasted_iota(jnp.int32, z.shape, 0)
    return jnp.where(t >= by, pltpu.roll(z, by, 0), 0.0)


def _shift_up(z, by):
    n = z.shape[0]
    t = lax.broadcasted_iota(jnp.int32, z.shape, 0)
    return jnp.where(t < n - by, pltpu.roll(z, n - by, 0), 0.0)


def conv_fwd(u, bg, cg, conv_w, seq, name):
    tokens, width = u.shape

    def body(u_ref, bg_ref, cg_ref, w_ref, o_ref):
        z = cg_ref[...] * u_ref[...]
        yy = w_ref[2:3, :] * z + w_ref[1:2, :] * _shift_down(z, 1) + w_ref[0:1, :] * _shift_down(z, 2)
        o_ref[...] = bg_ref[...] * yy

    col = pl.BlockSpec((seq, CONV_COLS), lambda j, b: (b, j))
    return pl.pallas_call(
        body, name=name, grid=(width // CONV_COLS, tokens // seq),
        in_specs=[col, col, col, pl.BlockSpec((CONV_TAPS, CONV_COLS), lambda j, b: (0, j))],
        out_specs=col, out_shape=jax.ShapeDtypeStruct((tokens, width), F32),
        compiler_params=_params(("parallel", "parallel")),
    )(u, bg, cg, conv_w)


def conv_bwd(dout, u, bg, cg, conv_w, seq, name):
    tokens, width = u.shape

    def body(do_ref, u_ref, bg_ref, cg_ref, w_ref, du_ref, dbg_ref, dcg_ref, dw_ref):
        uu, cg_v, do = u_ref[...], cg_ref[...], do_ref[...]
        z = cg_v * uu
        z1, z2 = _shift_down(z, 1), _shift_down(z, 2)
        yy = w_ref[2:3, :] * z + w_ref[1:2, :] * z1 + w_ref[0:1, :] * z2
        dbg_ref[...] = do * yy
        dyy = do * bg_ref[...]
        dz = w_ref[2:3, :] * dyy + w_ref[1:2, :] * _shift_up(dyy, 1) + w_ref[0:1, :] * _shift_up(dyy, 2)
        du_ref[...] = dz * cg_v
        dcg_ref[...] = dz * uu

        @pl.when(pl.program_id(1) == 0)
        def _():
            dw_ref[...] = jnp.zeros_like(dw_ref)

        dw_ref[0:1, :] += jnp.sum(dyy * z2, axis=0, keepdims=True)
        dw_ref[1:2, :] += jnp.sum(dyy * z1, axis=0, keepdims=True)
        dw_ref[2:3, :] += jnp.sum(dyy * z, axis=0, keepdims=True)

    col = pl.BlockSpec((seq, CONV_COLS), lambda j, b: (b, j))
    w_spec = pl.BlockSpec((CONV_TAPS, CONV_COLS), lambda j, b: (0, j))
    act = jax.ShapeDtypeStruct((tokens, width), F32)
    return pl.pallas_call(
        body, name=name, grid=(width // CONV_COLS, tokens // seq),
        in_specs=[col, col, col, col, w_spec], out_specs=[col, col, col, w_spec],
        out_shape=[act, act, act, jax.ShapeDtypeStruct((CONV_TAPS, width), F32)],
        compiler_params=_params(("parallel", "arbitrary")),
    )(dout, u, bg, cg, conv_w)


def out_fwd(x1, attn, conv, gt, w_out, ln_g, ln_b, seq, name, job=None):
    tokens, dm = x1.shape
    half = attn.shape[1]
    tm = min(TOKEN_TILE, seq)
    tiles_per_seq = seq // tm

    def body(x_ref, a_ref, c_ref, gt_ref, w_ref, lg_ref, lb_ref, xo_ref, r_ref, mi_ref, mix_ref):
        mixin = jnp.concatenate([a_ref[...], c_ref[...]], axis=1).astype(BF16)
        mi_ref[...] = mixin
        mix = _dot(mixin, w_ref[...])
        mix_ref[...] = mix.astype(BF16)
        r = DN_ALPHA * x_ref[...] + (1.0 + gt_ref[0]) * mix
        r_ref[...] = r
        xhat, _ = _ln_stats(r)
        xo_ref[...] = xhat * lg_ref[...] + lb_ref[...]

    tile = pl.BlockSpec((tm, dm), lambda i: (i, 0))
    htile = pl.BlockSpec((tm, half), lambda i: (i, 0))
    return _call(
        body, job, name=name, grid=(tokens // tm,),
        in_specs=[tile, htile, htile, _mod_spec(tiles_per_seq, dm), _const_spec(w_out.shape),
                  _const_spec((1, dm)), _const_spec((1, dm))],
        out_specs=[tile, tile, tile, tile],
        out_shape=[jax.ShapeDtypeStruct((tokens, dm), F32), jax.ShapeDtypeStruct((tokens, dm), F32),
                   jax.ShapeDtypeStruct((tokens, dm), BF16), jax.ShapeDtypeStruct((tokens, dm), BF16)],
        args=(x1, attn, conv, gt, w_out, ln_g, ln_b))


def out_bwd(dy, r, mix, gt, w_out, ln_g, seq, name, job=None):
    tokens, dm = r.shape
    half = dm // 2
    tm = min(TOKEN_TILE, seq)
    tiles_per_seq = seq // tm
    nseq = tokens // seq

    def body(dy_ref, r_ref, mix_ref, gt_ref, w_ref, lg_ref, dres_ref, da_ref, dc_ref, dmix_ref, dln_ref, dgt_ref):
        i = pl.program_id(0)
        dr, dgain, dbias = _ln_bwd(dy_ref[...], r_ref[...], lg_ref[...])

        @pl.when(i == 0)
        def _():
            dln_ref[...] = jnp.zeros_like(dln_ref)

        @pl.when(i % tiles_per_seq == 0)
        def _():
            dgt_ref[...] = jnp.zeros_like(dgt_ref)

        dln_ref[0:1, :] += dgain
        dln_ref[1:2, :] += dbias
        dgt_ref[0] += jnp.sum(dr * mix_ref[...].astype(F32), axis=0, keepdims=True)
        dres_ref[...] = DN_ALPHA * dr
        dmix = ((1.0 + gt_ref[0]) * dr).astype(BF16)
        dmix_ref[...] = dmix
        dmixin = _dot_nt(dmix, w_ref[...])
        da_ref[...] = dmixin[:, :half]
        dc_ref[...] = dmixin[:, half:]

    tile = pl.BlockSpec((tm, dm), lambda i: (i, 0))
    htile = pl.BlockSpec((tm, half), lambda i: (i, 0))
    return _call(
        body, job, name=name, grid=(tokens // tm,),
        in_specs=[tile, tile, tile, _mod_spec(tiles_per_seq, dm), _const_spec(w_out.shape), _const_spec((1, dm))],
        out_specs=[tile, htile, htile, tile, pl.BlockSpec((2, dm), lambda i: (0, 0)),
                   pl.BlockSpec((1, 1, dm), lambda i: (i // tiles_per_seq, 0, 0))],
        out_shape=[jax.ShapeDtypeStruct((tokens, dm), F32), jax.ShapeDtypeStruct((tokens, half), F32),
                   jax.ShapeDtypeStruct((tokens, half), F32), jax.ShapeDtypeStruct((tokens, dm), BF16),
                   jax.ShapeDtypeStruct((2, dm), F32), jax.ShapeDtypeStruct((nseq, 1, dm), F32)],
        args=(dy, r, mix, gt, w_out, ln_g))


def proj_bwd(parts, dres, x1, sh, sc, w_in, seq, name):
    tokens, dm = x1.shape
    tm = min(TOKEN_TILE, seq)
    tiles_per_seq = seq // tm
    nseq = tokens // seq
    widths = [p.shape[1] for p in parts]
    total = sum(widths)

    def body(*refs):
        part_refs = refs[:6]
        dres_ref, x_ref, sh_ref, sc_ref, w_ref, dx_ref, dproj_ref, h_ref, dmod_ref = refs[6:]
        dproj = jnp.concatenate([p[...] for p in part_refs], axis=1).astype(BF16)
        dproj_ref[...] = dproj
        dh = _dot_nt(dproj, w_ref[...])
        xx = x_ref[...]
        one_sc = 1.0 + sc_ref[0]
        h_ref[...] = (xx * one_sc + sh_ref[0]).astype(BF16)
        dx_ref[...] = dres_ref[...] + dh * one_sc

        @pl.when(pl.program_id(0) % tiles_per_seq == 0)
        def _():
            dmod_ref[...] = jnp.zeros_like(dmod_ref)

        dmod_ref[0, 0:1, :] += jnp.sum(dh, axis=0, keepdims=True)
        dmod_ref[0, 1:2, :] += jnp.sum(dh * xx, axis=0, keepdims=True)

    tile = pl.BlockSpec((tm, dm), lambda i: (i, 0))
    mod = _mod_spec(tiles_per_seq, dm)
    return pl.pallas_call(
        body, name=name, grid=(tokens // tm,),
        in_specs=[pl.BlockSpec((tm, wdt), lambda i: (i, 0)) for wdt in widths]
        + [tile, tile, mod, mod, _const_spec(w_in.shape)],
        out_specs=[tile, pl.BlockSpec((tm, total), lambda i: (i, 0)), tile,
                   pl.BlockSpec((1, 2, dm), lambda i: (i // tiles_per_seq, 0, 0))],
        out_shape=[jax.ShapeDtypeStruct((tokens, dm), F32), jax.ShapeDtypeStruct((tokens, total), BF16),
                   jax.ShapeDtypeStruct((tokens, dm), BF16), jax.ShapeDtypeStruct((nseq, 2, dm), F32)],
        compiler_params=_params(("arbitrary",)),
    )(*parts, dres, x1, sh, sc, w_in)


def _heads(t, nseq, seq, nheads):
    return t.reshape(nseq, seq, nheads, HEAD_DIM).transpose(0, 2, 1, 3)


def _unheads(t):
    nseq, nheads, seq, _ = t.shape
    return t.transpose(0, 2, 1, 3).reshape(nseq * seq, nheads * HEAD_DIM)


def _rope_tables(positions):
    inv_freq = jnp.power(jnp.float32(ROPE_THETA), -jnp.arange(0, ROT_DIM, 2, dtype=F32) / ROT_DIM)
    ang = positions.astype(F32)[..., None] * inv_freq
    cos, sin = jnp.cos(ang), jnp.sin(ang)
    rest = positions.shape + (HEAD_DIM - ROT_DIM,)
    cos_t = jnp.concatenate([cos, cos, jnp.ones(rest, F32)], axis=-1)
    sin_t = jnp.concatenate([-sin, sin, jnp.zeros(rest, F32)], axis=-1)
    return cos_t, sin_t


def kernel(x, c, positions, w_ada, b_ada, ffn1_w_gate_up, ffn1_w_down, ln1_g, ln1_b, w_in, conv_w, attn_sinks, w_out, ln2_g, ln2_b, ffn2_w_gate_up, ffn2_w_down, ln3_g, ln3_b, loss_target, m_w_ada, m_b_ada, m_ffn1_w_gate_up, m_ffn1_w_down, m_ln1_g, m_ln1_b, m_w_in, m_conv_w, m_attn_sinks, m_w_out, m_ln2_g, m_ln2_b, m_ffn2_w_gate_up, m_ffn2_w_down, m_ln3_g, m_ln3_b, v_w_ada, v_b_ada, v_ffn1_w_gate_up, v_ffn1_w_down, v_ln1_g, v_ln1_b, v_w_in, v_conv_w, v_attn_sinks, v_w_out, v_ln2_g, v_ln2_b, v_ffn2_w_gate_up, v_ffn2_w_down, v_ln3_g, v_ln3_b):
    nseq, seq, dm = x.shape
    tokens = nseq * seq
    dev = 4 * lax.axis_index("x") + 2 * lax.axis_index("y") + lax.axis_index("c")
    ada_cols = w_ada.shape[2]
    ff = ffn1_w_down.shape[1] * N_DEV
    fc = ff // 4
    in_cols = w_in.shape[2]
    conv_cols = conv_w.shape[2]

    c_all, convw_all = all_gather([c, conv_w[0]], "gather_cond")
    wgu1, wd1 = all_gather([ffn1_w_gate_up[0].astype(BF16), ffn1_w_down[0].astype(BF16)], "gather_ffn1")
    c_all = c_all.reshape(N_DEV * nseq, dm)
    convw_full = convw_all.transpose(1, 0, 2).reshape(CONV_TAPS, N_DEV * conv_cols)
    wd1 = wd1.reshape(4, fc, dm)

    b_cols = lax.dynamic_slice(b_ada, (0, dev * ada_cols), (1, ada_cols))
    cond_all, mod_cols = ada_fwd(c_all, w_ada[0], b_cols, "ada_fwd")
    (mod_all,) = all_gather([mod_cols], "gather_mod")
    mod = lax.dynamic_slice(mod_all, (0, dev * nseq, 0), (N_DEV, nseq, ada_cols))
    mod = mod.transpose(1, 0, 2).reshape(nseq, 9, 1, dm)
    sh1, sc1, g1, sh2, sc2, g2, sh3, sc3, g3 = [mod[:, i] for i in range(9)]

    x0 = x.reshape(tokens, dm)
    spread = gather_spread_job([w_in[0].astype(BF16), w_out[0].astype(BF16), ffn2_w_down[0].astype(BF16)])
    (x1, r1, gu1, f1), spread = ffn_fwd(x0, sh1, sc1, g1, wgu1, wd1, ln1_g, ln1_b, seq, "ffn1_fwd", job=spread)
    win, wout = run_job(gather_forward_job(spread[:2]), "gather_mix_forward")
    win = win.transpose(1, 0, 2).reshape(dm, N_DEV * in_cols)
    wout = wout.reshape(dm, dm)
    (q, k, v, u, bg, cg), (wd2,) = proj_fwd(x1, sh2, sc2, win, seq, "proj_fwd", job=gather_forward_job(spread[2:]))
    wd2 = wd2.reshape(4, fc, dm)
    cos_t, sin_t = _rope_tables(positions)
    qh, kh, vh = _heads(q, nseq, seq, N_Q_HEADS), _heads(k, nseq, seq, N_KV_HEADS), _heads(v, nseq, seq, N_KV_HEADS)
    sinks = attn_sinks[0]
    (attn_h,), spread = attn_fwd(qh, kh, vh, cos_t, sin_t, sinks, "attn_fwd",
                                 job=gather_spread_job([ffn2_w_gate_up[0].astype(BF16)]))
    attn = _unheads(attn_h)
    conv = conv_fwd(u, bg, cg, convw_full, seq, "conv_fwd")
    (x2, r2, mixin, mix), (wgu2,) = out_fwd(x1, attn, conv, g2, wout, ln2_g, ln2_b, seq, "out_fwd",
                                            job=gather_forward_job(spread))
    target = loss_target.reshape(tokens, dm)
    (dy3, loss_part, r3, gu3, f3), _ = ffn_fwd(x2, sh3, sc3, g3, wgu2, wd2, ln3_g, ln3_b, seq, "ffn2_fwd", target=target)

    (dx2, dgu3, df3, a3, h3, dln3, dmod3), _ = ffn_bwd(dy3, r3, x2, f3, gu3, sh3, sc3, g3, wgu2, wd2, ln3_g, seq, "ffn2_bwd")
    g_wd2 = tn_matmul(a3, df3[None], "ffn2_dwd")[0][0].reshape(N_DEV, ff // N_DEV, dm)
    g_wgu2 = tn_matmul(h3[None], dgu3, "ffn2_dwgu")[0][0].reshape(N_DEV, dm, fc)
    (dres2, dattn, dconv, dmix, dln2, dg2), swapped = out_bwd(dx2, r2, mix, g2, wout, ln2_g, seq, "out_bwd",
                                                              job=swap_job([g_wgu2, g_wd2]))
    p_wgu2, own_wgu2 = pair_sum(g_wgu2, swapped[0], "pair_wgu2")
    p_wd2, own_wd2 = pair_sum(g_wd2, swapped[1], "pair_wd2")
    g_wout = tn_matmul(mixin[None], dmix[None], "dwout")[0][0].reshape(N_DEV, dm // N_DEV, dm)
    du, dbg, dcg, dconvw = conv_bwd(dconv, u, bg, cg, convw_full, seq, "conv_bwd")
    (dqh, dkh, dvh, dsink_rows), (far_wgu2, far_wd2) = attn_bwd(
        qh, kh, vh, _heads(dattn, nseq, seq, N_Q_HEADS), cos_t, sin_t, sinks, "attn_bwd",
        job=chip_exchange_job([p_wgu2, p_wd2]))
    parts = [_unheads(dqh), _unheads(dkh), _unheads(dvh), du, dbg, dcg]
    dx1, dproj, h2, dmod2 = proj_bwd(parts, dres2, x1, sh2, sc2, win, seq, "proj_bwd")
    g_win = tn_matmul(h2[None], dproj[None], "dwin")[0][0].reshape(dm, N_DEV, in_cols).transpose(1, 0, 2)
    (dx0, dgu1, df1, a1, h1, dln1, dmod1), swapped = ffn_bwd(
        dx1, r1, x0, f1, gu1, sh1, sc1, g1, wgu1, wd1, ln1_g, seq, "ffn1_bwd", job=swap_job([g_wout, g_win]))
    p_wout, own_wout = pair_sum(g_wout, swapped[0], "pair_wout")
    p_win, own_win = pair_sum(g_win, swapped[1], "pair_win")
    (g_wd1,), (far_wout, far_win) = tn_matmul(a1, df1[None], "ffn1_dwd", job=chip_exchange_job([p_wout, p_win]))
    g_wd1 = g_wd1.reshape(N_DEV, ff // N_DEV, dm)
    g_wgu1 = tn_matmul(h1[None], dgu1, "ffn1_dwgu")[0][0].reshape(N_DEV, dm, fc)

    grads = {
        "ffn1_w_gate_up": reduce_scatter(g_wgu1, "rs_wgu1"), "ffn1_w_down": reduce_scatter(g_wd1, "rs_wd1"),
        "w_in": own_win, "w_out": own_wout, "ffn2_w_gate_up": own_wgu2, "ffn2_w_down": own_wd2,
    }
    others = {"w_in": far_win, "w_out": far_wout, "ffn2_w_gate_up": far_wgu2, "ffn2_w_down": far_wd2}

    dmod = jnp.concatenate([dmod1, dmod2, dg2, dmod3], axis=1).reshape(nseq, 9 * dm)
    (dmod_all,) = all_gather([dmod], "gather_dmod")
    dmod_cols = lax.dynamic_slice(dmod_all.reshape(N_DEV * nseq, 9 * dm), (0, dev * ada_cols), (N_DEV * nseq, ada_cols))
    grads["w_ada"], gb_cols = ada_bwd(cond_all, dmod_cols, "ada_bwd")

    dsinks = jnp.sum(dsink_rows.reshape(nseq, N_Q_HEADS, ATTN_BLOCK), axis=(0, 2))
    small = jnp.zeros((8, dm), F32)
    small = small.at[0:2].set(dln1).at[2:4].set(dln2).at[4:6].set(dln3)
    small = small.at[6, 0:N_Q_HEADS].set(dsinks).at[7, 0].set(loss_part[0, 0])
    small_all, dconvw_all, gb_all = all_gather([small, dconvw, gb_cols], "gather_small")
    small_sum = sum_devices(small_all, "sum_small")
    dconvw_sum = sum_devices(dconvw_all, "sum_convw")
    loss = small_sum[7, 0]
    grads["b_ada"] = gb_all.reshape(1, N_DEV * ada_cols)
    grads["conv_w"] = lax.dynamic_slice(dconvw_sum, (0, dev * conv_cols), (CONV_TAPS, conv_cols))
    grads["attn_sinks"] = small_sum[6:7, 0:N_Q_HEADS]
    for i, nm in enumerate(["ln1_g", "ln1_b", "ln2_g", "ln2_b", "ln3_g", "ln3_b"]):
        grads[nm] = small_sum[i:i + 1]

    given = dict(w_ada=(w_ada, m_w_ada, v_w_ada), b_ada=(b_ada, m_b_ada, v_b_ada),
                 ffn1_w_gate_up=(ffn1_w_gate_up, m_ffn1_w_gate_up, v_ffn1_w_gate_up),
                 ffn1_w_down=(ffn1_w_down, m_ffn1_w_down, v_ffn1_w_down),
                 ln1_g=(ln1_g, m_ln1_g, v_ln1_g), ln1_b=(ln1_b, m_ln1_b, v_ln1_b),
                 w_in=(w_in, m_w_in, v_w_in), conv_w=(conv_w, m_conv_w, v_conv_w),
                 attn_sinks=(attn_sinks, m_attn_sinks, v_attn_sinks), w_out=(w_out, m_w_out, v_w_out),
                 ln2_g=(ln2_g, m_ln2_g, v_ln2_g), ln2_b=(ln2_b, m_ln2_b, v_ln2_b),
                 ffn2_w_gate_up=(ffn2_w_gate_up, m_ffn2_w_gate_up, v_ffn2_w_gate_up),
                 ffn2_w_down=(ffn2_w_down, m_ffn2_w_down, v_ffn2_w_down),
                 ln3_g=(ln3_g, m_ln3_g, v_ln3_g), ln3_b=(ln3_b, m_ln3_b, v_ln3_b))
    order = ["w_ada", "b_ada", "ffn1_w_gate_up", "ffn1_w_down", "ln1_g", "ln1_b", "w_in", "conv_w", "attn_sinks",
             "w_out", "ln2_g", "ln2_b", "ffn2_w_gate_up", "ffn2_w_down", "ln3_g", "ln3_b"]
    out_g, out_d, out_m, out_v = [], [], [], []
    for nm in order:
        w, m, v = given[nm]
        shape = w.shape
        two_d = (shape[-2], shape[-1])
        g2d, d, nm_new, nv_new = adamw(w.reshape(two_d), grads[nm].reshape(two_d), m.reshape(two_d), v.reshape(two_d),
                                       "adamw_" + nm, others=others.get(nm))
        out_g.append(g2d.reshape(shape))
        out_d.append(d.reshape(shape))
        out_m.append(nm_new.reshape(shape))
        out_v.append(nv_new.reshape(shape))
    grad_x = dx0.reshape(nseq, seq, dm)
    return (loss, grad_x, *out_g, *out_d, *out_m, *out_v)
```

```python
import functools

import jax
import jax.numpy as jnp
from jax import lax
from jax.experimental import pallas as pl
from jax.experimental.pallas import tpu as pltpu

F32 = jnp.float32
BF16 = jnp.bfloat16
MESH = pl.DeviceIdType.MESH

N_DEV = 8
N_CHIP = 4
HEAD_DIM = 64
N_Q_HEADS = 8
N_KV_HEADS = 2
GQA_GROUP = N_Q_HEADS // N_KV_HEADS
ATTN_BLOCK = 128
ROT_DIM = 16
ROPE_THETA = 500000.0
CONV_TAPS = 3
LN_EPS = 1e-5
DN_ALPHA = 2.0 ** 0.25
ADAM_LR = 0.001
ADAM_B1 = 0.9
ADAM_B2 = 0.999
ADAM_EPS = 1e-08
ADAM_WD = 0.01
ADAM_STEP = 10
NEG_BIG = -1e30

VMEM_LIMIT = 56 * 1024 * 1024
TOKEN_TILE = 256
FFN_FWD_TILE = 512
TN_VMEM_BUDGET = 36 * 1024 * 1024


def _params(semantics=None, vmem=VMEM_LIMIT):
    return pltpu.CompilerParams(dimension_semantics=semantics, vmem_limit_bytes=vmem)


def _dot(a, b):
    return jnp.dot(a, b, preferred_element_type=F32)


def _dot_nt(a, b):
    return lax.dot_general(a, b, (((1,), (1,)), ((), ())), preferred_element_type=F32)


def _dot_tn(a, b):
    return lax.dot_general(a, b, (((0,), (0,)), ((), ())), preferred_element_type=F32)


def _sigmoid(x):
    return 1.0 / (1.0 + jnp.exp(-x))


def _ln_stats(r):
    mu = jnp.mean(r, axis=-1, keepdims=True)
    d = r - mu
    var = jnp.mean(d * d, axis=-1, keepdims=True)
    rstd = lax.rsqrt(var + LN_EPS)
    return d * rstd, rstd


def _ln_bwd(dy, r, g):
    xhat, rstd = _ln_stats(r)
    dxhat = dy * g
    c1 = jnp.mean(dxhat, axis=-1, keepdims=True)
    c2 = jnp.mean(dxhat * xhat, axis=-1, keepdims=True)
    dr = rstd * (dxhat - c1 - xhat * c2)
    return dr, jnp.sum(dy * xhat, axis=0, keepdims=True), jnp.sum(dy, axis=0, keepdims=True)


def _const_spec(shape):
    nd = len(shape)
    return pl.BlockSpec(shape, lambda *_: (0,) * nd, pipeline_mode=pl.Buffered(1))


def all_gather(arrs, name):
    n = len(arrs)

    def body(*refs):
        ins, outs = refs[:n], refs[n:2 * n]
        send_sems, recv_sems, local_sems = refs[2 * n:]
        x, y, c = lax.axis_index("x"), lax.axis_index("y"), lax.axis_index("c")
        me, sibling = (x, y, c), (x, y, 1 - c)
        chips = [(1 - x, y), (x, 1 - y), (1 - x, 1 - y)]

        def slot(i, p):
            return outs[i].at[4 * p[0] + 2 * p[1] + p[2]]

        def copy(i, k, block, to, src=None):
            return pltpu.make_async_remote_copy(
                src_ref=slot(i, block) if src is None else src, dst_ref=slot(i, block),
                send_sem=send_sems.at[i, k], recv_sem=recv_sems.at[i, k],
                device_id=to, device_id_type=MESH)

        mine = [pltpu.make_async_copy(ins[i], slot(i, me), local_sems.at[i]) for i in range(n)]
        for cp in mine:
            cp.start()
        first = []
        for i in range(n):
            first.append(copy(i, 0, me, sibling, src=ins[i]))
            first += [copy(i, 1 + j, me, (*chip, c), src=ins[i]) for j, chip in enumerate(chips)]
        for cp in first:
            cp.start()
        passed = []
        for j, chip in enumerate(chips):
            for i in range(n):
                copy(i, 1 + j, (*chip, c), me).wait_recv()
                cp = copy(i, 4 + j, (*chip, c), sibling)
                cp.start()
                passed.append(cp)
        for i in range(n):
            copy(i, 0, sibling, me).wait_recv()
            for j, chip in enumerate(chips):
                copy(i, 4 + j, (*chip, 1 - c), me).wait_recv()
        for cp in first + passed:
            cp.wait_send()
        for cp in mine:
            cp.wait()

    any_spec = pl.BlockSpec(memory_space=pl.ANY)
    return pl.pallas_call(
        body, name=name,
        out_shape=[jax.ShapeDtypeStruct((N_DEV, *a.shape), a.dtype) for a in arrs],
        in_specs=[any_spec] * n, out_specs=[any_spec] * n,
        scratch_shapes=[pltpu.SemaphoreType.DMA((n, 7)), pltpu.SemaphoreType.DMA((n, 7)),
                        pltpu.SemaphoreType.DMA((n,))],
    )(*arrs)


RS_ROWS = 32


def reduce_scatter(g, name):
    _, rows, cols = g.shape
    nblk = rows // RS_ROWS
    assert nblk * RS_ROWS == rows

    def body(g_ref, out_ref, r1_ref, p_ref, r2_ref, send_sems, recv_sems):
        x, y, c = lax.axis_index("x"), lax.axis_index("y"), lax.axis_index("c")
        sibling = (x, y, 1 - c)
        q_me = 2 * x + y
        swaps = []
        for q in range(N_CHIP):
            cp = pltpu.make_async_remote_copy(
                src_ref=g_ref.at[2 * q + (1 - c)], dst_ref=r1_ref.at[q],
                send_sem=send_sems.at[q], recv_sem=recv_sems.at[q], device_id=sibling, device_id_type=MESH)
            cp.start()
            swaps.append(cp)
        for cp in swaps:
            cp.wait_recv()

        def pair_sum(i, carry):
            r = pl.ds(pl.multiple_of(i * RS_ROWS, RS_ROWS), RS_ROWS)
            for q in range(N_CHIP):
                p_ref[q, r, :] = (g_ref[2 * q + c, r, :].astype(F32) + r1_ref[q, r, :].astype(F32)).astype(BF16)
            return carry

        lax.fori_loop(0, nblk, pair_sum, 0)
        chips = [(1 - x, y), (x, 1 - y), (1 - x, 1 - y)]
        sends = []
        for k, chip in enumerate(chips):
            cp = pltpu.make_async_remote_copy(
                src_ref=p_ref.at[2 * chip[0] + chip[1]], dst_ref=r2_ref.at[k],
                send_sem=send_sems.at[N_CHIP + k], recv_sem=recv_sems.at[N_CHIP + k],
                device_id=(*chip, c), device_id_type=MESH)
            cp.start()
            sends.append(cp)
        for cp in sends:
            cp.wait_recv()

        def total(i, carry):
            r = pl.ds(pl.multiple_of(i * RS_ROWS, RS_ROWS), RS_ROWS)
            acc = g_ref[2 * q_me + c, r, :].astype(F32) + r1_ref[q_me, r, :].astype(F32)
            for k in range(3):
                acc = acc + r2_ref[k, r, :].astype(F32)
            out_ref[r, :] = acc
            return carry

        lax.fori_loop(0, nblk, total, 0)
        for cp in swaps + sends:
            cp.wait_send()

    vmem = pl.BlockSpec(memory_space=pltpu.VMEM)
    return pl.pallas_call(
        body, name=name,
        out_shape=jax.ShapeDtypeStruct((rows, cols), F32),
        in_specs=[vmem], out_specs=vmem,
        scratch_shapes=[pltpu.VMEM((N_CHIP, rows, cols), BF16), pltpu.VMEM((N_CHIP, rows, cols), BF16),
                        pltpu.VMEM((3, rows, cols), BF16),
                        pltpu.SemaphoreType.DMA((N_CHIP + 3,)), pltpu.SemaphoreType.DMA((N_CHIP + 3,))],
        compiler_params=_params(),
    )(g)


def _place():
    x, y, c = lax.axis_index("x"), lax.axis_index("y"), lax.axis_index("c")
    return x, y, c, [(1 - x, y), (x, 1 - y), (1 - x, 1 - y)]


def _slot(p):
    return 4 * p[0] + 2 * p[1] + p[2]


class _Job:
    def __init__(self, ins, outs, nsem, copies, aliases=None, local=None):
        self.ins, self.outs, self.nsem, self.copies = list(ins), list(outs), nsem, copies
        self.aliases = aliases or {}
        self.local = local

    def scratch(self):
        s = [pltpu.SemaphoreType.DMA(self.nsem), pltpu.SemaphoreType.DMA(self.nsem)]
        if self.local is not None:
            s.append(pltpu.SemaphoreType.DMA((len(self.ins),)))
        return s

    def start(self, ins, outs, sems):
        if self.local is not None:
            for cp in self.local(ins, outs, sems[2]):
                cp.start()
        for cp in self.copies(ins, outs, sems[0], sems[1])[0]:
            cp.start()

    def finish(self, ins, outs, sems):
        started, awaited = self.copies(ins, outs, sems[0], sems[1])
        for cp in awaited:
            cp.wait_recv()
        for cp in started:
            cp.wait_send()
        if self.local is not None:
            for cp in self.local(ins, outs, sems[2]):
                cp.wait()


def _remote(src, dst, send, recv, idx, to):
    return pltpu.make_async_remote_copy(src_ref=src, dst_ref=dst, send_sem=send.at[idx], recv_sem=recv.at[idx],
                                        device_id=to, device_id_type=MESH)


def gather_spread_job(shards):
    def copies(ins, outs, send, recv):
        x, y, c, chips = _place()
        me = (x, y, c)
        peers = [(x, y, 1 - c)] + [(*chip, c) for chip in chips]
        started, awaited = [], []
        for i, (src, dst) in enumerate(zip(ins, outs)):
            for k, peer in enumerate(peers):
                started.append(_remote(src, dst.at[_slot(me)], send, recv, (i, k), peer))
                awaited.append(_remote(src, dst.at[_slot(peer)], send, recv, (i, k), peer))
        return started, awaited

    def local(ins, outs, sems):
        x, y, c, _ = _place()
        return [pltpu.make_async_copy(src, dst.at[_slot((x, y, c))], sems.at[i])
                for i, (src, dst) in enumerate(zip(ins, outs))]

    outs = [jax.ShapeDtypeStruct((N_DEV, *a.shape), a.dtype) for a in shards]
    return _Job(shards, outs, (len(shards), 4), copies, local=local)


def gather_forward_job(fulls):
    def copies(ins, outs, send, recv):
        x, y, c, chips = _place()
        started, awaited = [], []
        for i, buf in enumerate(outs):
            for j, chip in enumerate(chips):
                mine, theirs = buf.at[_slot((*chip, c))], buf.at[_slot((*chip, 1 - c))]
                started.append(_remote(mine, mine, send, recv, (i, j), (x, y, 1 - c)))
                awaited.append(_remote(theirs, theirs, send, recv, (i, j), (x, y, 1 - c)))
        return started, awaited

    outs = [jax.ShapeDtypeStruct(a.shape, a.dtype) for a in fulls]
    return _Job(fulls, outs, (len(fulls), 3), copies, aliases={i: i for i in range(len(fulls))})


def swap_job(gs):
    def copies(ins, outs, send, recv):
        x, y, c, _ = _place()
        started, awaited = [], []
        for i, (g, r1) in enumerate(zip(ins, outs)):
            for q in range(N_CHIP):
                started.append(_remote(g.at[2 * q + (1 - c)], r1.at[q], send, recv, (i, q), (x, y, 1 - c)))
                awaited.append(_remote(g.at[2 * q + c], r1.at[q], send, recv, (i, q), (x, y, 1 - c)))
        return started, awaited

    outs = [jax.ShapeDtypeStruct((N_CHIP, *g.shape[1:]), g.dtype) for g in gs]
    return _Job(gs, outs, (len(gs), N_CHIP), copies)


def chip_exchange_job(ps):
    def copies(ins, outs, send, recv):
        x, y, c, chips = _place()
        started, awaited = [], []
        for i, (p, r2) in enumerate(zip(ins, outs)):
            for k, chip in enumerate(chips):
                started.append(_remote(p.at[2 * chip[0] + chip[1]], r2.at[k], send, recv, (i, k), (*chip, c)))
                awaited.append(_remote(p.at[2 * x + y], r2.at[k], send, recv, (i, k), (*chip, c)))
        return started, awaited

    outs = [jax.ShapeDtypeStruct((3, *p.shape[1:]), p.dtype) for p in ps]
    return _Job(ps, outs, (len(ps), 3), copies)


def _call(body, job, *, name, grid, in_specs, out_specs, out_shape, args, scratch_shapes=(), vmem=VMEM_LIMIT):
    if job is None:
        res = pl.pallas_call(
            body, name=name, grid=grid, in_specs=in_specs, out_specs=out_specs, out_shape=out_shape,
            scratch_shapes=list(scratch_shapes), compiler_params=_params(("arbitrary",) * len(grid), vmem),
        )(*args)
        return res, []
    n_in, n_out, n_scr = len(in_specs), len(out_specs), len(scratch_shapes)
    j_in, j_out = len(job.ins), len(job.outs)

    def with_copies(*refs):
        at = 0
        ins = refs[at:at + n_in]; at += n_in
        jins = refs[at:at + j_in]; at += j_in
        outs = refs[at:at + n_out]; at += n_out
        jouts = refs[at:at + j_out]; at += j_out
        scr = refs[at:at + n_scr]; at += n_scr
        sems = refs[at:]
        ids = [pl.program_id(d) for d in range(len(grid))]
        first = functools.reduce(jnp.logical_and, [i == 0 for i in ids])
        last = functools.reduce(jnp.logical_and, [i == n - 1 for i, n in zip(ids, grid)])

        @pl.when(first)
        def _():
            job.start(jins, jouts, sems)

        body(*ins, *outs, *scr)

        @pl.when(last)
        def _():
            job.finish(jins, jouts, sems)

    any_spec = pl.BlockSpec(memory_space=pl.ANY)
    res = pl.pallas_call(
        with_copies, name=name, grid=grid,
        in_specs=list(in_specs) + [any_spec] * j_in, out_specs=list(out_specs) + [any_spec] * j_out,
        out_shape=list(out_shape) + list(job.outs),
        input_output_aliases={n_in + i: n_out + o for i, o in job.aliases.items()},
        scratch_shapes=list(scratch_shapes) + job.scratch(),
        compiler_params=_params(("arbitrary",) * len(grid), vmem),
    )(*args, *job.ins)
    return res[:n_out], res[n_out:]


def run_job(job, name):
    def body(*refs):
        j_in, j_out = len(job.ins), len(job.outs)
        ins, outs, sems = refs[:j_in], refs[j_in:j_in + j_out], refs[j_in + j_out:]
        job.start(ins, outs, sems)
        job.finish(ins, outs, sems)

    any_spec = pl.BlockSpec(memory_space=pl.ANY)
    return pl.pallas_call(
        body, name=name, in_specs=[any_spec] * len(job.ins), out_specs=[any_spec] * len(job.outs),
        out_shape=list(job.outs), input_output_aliases=dict(job.aliases), scratch_shapes=job.scratch(),
    )(*job.ins)


def pair_sum(g, r1, name):
    _, rows, cols = g.shape
    rb = rows if rows <= 512 else 256
    assert rows % rb == 0

    def body(g_ref, r1_ref, p_ref, own_ref):
        x, y, c, _ = _place()
        s = g_ref[c].astype(F32) + r1_ref[0].astype(F32)
        p_ref[0] = s.astype(BF16)

        @pl.when(pl.program_id(1) == 2 * x + y)
        def _():
            own_ref[...] = s

    return pl.pallas_call(
        body, name=name, grid=(rows // rb, N_CHIP),
        in_specs=[pl.BlockSpec((2, rb, cols), lambda i, q: (q, i, 0)), pl.BlockSpec((1, rb, cols), lambda i, q: (q, i, 0))],
        out_specs=[pl.BlockSpec((1, rb, cols), lambda i, q: (q, i, 0)), pl.BlockSpec((rb, cols), lambda i, q: (i, 0))],
        out_shape=[jax.ShapeDtypeStruct((N_CHIP, rows, cols), BF16), jax.ShapeDtypeStruct((rows, cols), F32)],
        compiler_params=_params(("arbitrary", "arbitrary")),
    )(g, r1)


def sum_devices(a, name):
    def body(a_ref, o_ref):
        acc = a_ref[0]
        for d in range(1, N_DEV):
            acc = acc + a_ref[d]
        o_ref[...] = acc

    return pl.pallas_call(body, name=name, out_shape=jax.ShapeDtypeStruct(a.shape[1:], F32))(a)


def adamw(w, g, m, v, name, others=None):
    rows, cols = w.shape
    rb = rows
    for cand in (256, 128, 64, 32, 16, 8):
        if rows % cand == 0:
            rb = cand
            break

    def body(*refs):
        if others is None:
            w_ref, g_ref, m_ref, v_ref, d_ref, nm_ref, nv_ref = refs
            gg = g_ref[...]
        else:
            w_ref, g_ref, m_ref, v_ref, r2_ref, go_ref, d_ref, nm_ref, nv_ref = refs
            gg = g_ref[...]
            for k in range(3):
                gg = gg + r2_ref[k].astype(F32)
            go_ref[...] = gg
        nm = ADAM_B1 * m_ref[...] + (1.0 - ADAM_B1) * gg
        nv = ADAM_B2 * v_ref[...] + (1.0 - ADAM_B2) * (gg * gg)
        m_hat = nm / (1.0 - ADAM_B1 ** ADAM_STEP)
        v_hat = nv / (1.0 - ADAM_B2 ** ADAM_STEP)
        d_ref[...] = -ADAM_LR * (m_hat / (jnp.sqrt(v_hat) + ADAM_EPS) + ADAM_WD * w_ref[...])
        nm_ref[...] = nm
        nv_ref[...] = nv

    spec = pl.BlockSpec((rb, cols), lambda i: (i, 0))
    out = jax.ShapeDtypeStruct((rows, cols), F32)
    in_specs, args = [spec] * 4, [w, g, m, v]
    if others is not None:
        in_specs.append(pl.BlockSpec((3, rb, cols), lambda i: (0, i, 0)))
        args.append(others)
    n_out = 3 if others is None else 4
    res = pl.pallas_call(
        body, name=name, grid=(rows // rb,), in_specs=in_specs, out_specs=[spec] * n_out,
        out_shape=[out] * n_out, compiler_params=_params(("parallel",)),
    )(*args)
    return (g, *res) if others is None else tuple(res)


def ada_fwd(c_all, w_cols, b_cols, name):
    def body(c_ref, w_ref, b_ref, cond_ref, mod_ref):
        cc = c_ref[...]
        cond = (cc * _sigmoid(cc)).astype(BF16)
        cond_ref[...] = cond
        mod_ref[...] = _dot(cond, w_ref[...].astype(BF16)) + b_ref[...]

    n, cols = c_all.shape[0], w_cols.shape[1]
    return pl.pallas_call(
        body, name=name,
        out_shape=[jax.ShapeDtypeStruct(c_all.shape, BF16), jax.ShapeDtypeStruct((n, cols), F32)],
        compiler_params=_params(),
    )(c_all, w_cols, b_cols)


def ada_bwd(cond_all, dmod_cols, name):
    def body(c_ref, d_ref, gw_ref, gb_ref):
        d = d_ref[...]
        gw_ref[...] = _dot_tn(c_ref[...], d.astype(BF16))
        gb_ref[...] = jnp.sum(d, axis=0, keepdims=True)

    dm, cols = cond_all.shape[1], dmod_cols.shape[1]
    return pl.pallas_call(
        body, name=name,
        out_shape=[jax.ShapeDtypeStruct((dm, cols), F32), jax.ShapeDtypeStruct((1, cols), F32)],
        compiler_params=_params(),
    )(cond_all, dmod_cols)


def _mod_spec(tiles_per_seq, dm):
    return pl.BlockSpec((1, 1, dm), lambda i: (i // tiles_per_seq, 0, 0))


def ffn_fwd(x, sh, sc, gt, wgu, wd, ln_g, ln_b, seq, name, target=None, job=None):
    tokens, dm = x.shape
    fc = wgu.shape[2]
    tm = min(FFN_FWD_TILE, seq)
    tiles_per_seq = seq // tm
    with_loss = target is not None

    def body(*refs):
        if with_loss:
            (x_ref, sh_ref, sc_ref, gt_ref, wgu_ref, wd_ref, lg_ref, lb_ref, t_ref,
             xo_ref, loss_ref, r_ref, gu_ref, f_ref) = refs
        else:
            (x_ref, sh_ref, sc_ref, gt_ref, wgu_ref, wd_ref, lg_ref, lb_ref,
             xo_ref, r_ref, gu_ref, f_ref) = refs
        xx = x_ref[...]
        h = (xx * (1.0 + sc_ref[0]) + sh_ref[0]).astype(BF16)
        acc = jnp.zeros((tm, dm), F32)
        for k in range(4):
            gk = _dot(h, wgu_ref[k])
            uk = _dot(h, wgu_ref[k + 4])
            gu_ref[k] = gk.astype(BF16)
            gu_ref[k + 4] = uk.astype(BF16)
            a = (gk * _sigmoid(gk) * uk).astype(BF16)
            acc = acc + _dot(a, wd_ref[k])
        f_ref[...] = acc.astype(BF16)
        r = DN_ALPHA * xx + (0.5 * (1.0 + gt_ref[0])) * acc
        r_ref[...] = r
        xhat, _ = _ln_stats(r)
        yy = xhat * lg_ref[...] + lb_ref[...]
        if with_loss:
            err = yy - t_ref[...]
            xo_ref[...] = err * (1.0 / dm)

            @pl.when(pl.program_id(0) == 0)
            def _():
                loss_ref[...] = jnp.zeros_like(loss_ref)

            loss_ref[...] += jnp.full((1, 128), (0.5 / dm) * jnp.sum(err * err), F32)
        else:
            xo_ref[...] = yy

    tile = pl.BlockSpec((tm, dm), lambda i: (i, 0))
    mod = _mod_spec(tiles_per_seq, dm)
    in_specs = [tile, mod, mod, mod, _const_spec(wgu.shape), _const_spec(wd.shape),
                _const_spec((1, dm)), _const_spec((1, dm))]
    args = [x, sh, sc, gt, wgu, wd, ln_g, ln_b]
    out_specs = [tile]
    out_shape = [jax.ShapeDtypeStruct((tokens, dm), F32)]
    if with_loss:
        in_specs.append(tile)
        args.append(target)
        out_specs.append(pl.BlockSpec((1, 128), lambda i: (0, 0)))
        out_shape.append(jax.ShapeDtypeStruct((1, 128), F32))
    out_specs += [tile, pl.BlockSpec((8, tm, fc), lambda i: (0, i, 0)), tile]
    out_shape += [jax.ShapeDtypeStruct((tokens, dm), F32), jax.ShapeDtypeStruct((8, tokens, fc), BF16),
                  jax.ShapeDtypeStruct((tokens, dm), BF16)]
    return _call(body, job, name=name, grid=(tokens // tm,), in_specs=in_specs, out_specs=out_specs,
                 out_shape=out_shape, args=args)


def ffn_bwd(dy, r, x, f, gu, sh, sc, gt, wgu, wd, ln_g, seq, name, job=None):
    tokens, dm = x.shape
    fc = wgu.shape[2]
    tm = min(TOKEN_TILE, seq)
    tiles_per_seq = seq // tm
    nseq = tokens // seq

    def body(dy_ref, r_ref, x_ref, f_ref, gu_ref, sh_ref, sc_ref, gt_ref, wgu_ref, wd_ref, lg_ref,
             dx_ref, dgu_ref, df_ref, a_ref, h_ref, dln_ref, dmod_ref):
        i = pl.program_id(0)
        dr, dgain, dbias = _ln_bwd(dy_ref[...], r_ref[...], lg_ref[...])

        @pl.when(i == 0)
        def _():
            dln_ref[...] = jnp.zeros_like(dln_ref)

        @pl.when(i % tiles_per_seq == 0)
        def _():
            dmod_ref[...] = jnp.zeros_like(dmod_ref)

        dln_ref[0:1, :] += dgain
        dln_ref[1:2, :] += dbias
        df32 = (0.5 * (1.0 + gt_ref[0])) * dr
        df = df32.astype(BF16)
        df_ref[...] = df
        dgate = jnp.sum(dr * (0.5 * f_ref[...].astype(F32)), axis=0, keepdims=True)
        xx = x_ref[...]
        one_sc = 1.0 + sc_ref[0]
        h = (xx * one_sc + sh_ref[0]).astype(BF16)
        h_ref[...] = h
        dh = jnp.zeros((tm, dm), F32)
        for k in range(4):
            da = _dot_nt(df, wd_ref[k])
            gk = gu_ref[k].astype(F32)
            uk = gu_ref[k + 4].astype(F32)
            sg = _sigmoid(gk)
            sil = gk * sg
            a_ref[k] = (sil * uk).astype(BF16)
            du = (da * sil).astype(BF16)
            dg = (da * uk * (sg * (1.0 + gk * (1.0 - sg)))).astype(BF16)
            dgu_ref[k] = dg
            dgu_ref[k + 4] = du
            dh = dh + _dot_nt(dg, wgu_ref[k]) + _dot_nt(du, wgu_ref[k + 4])
        dx_ref[...] = DN_ALPHA * dr + dh * one_sc
        dmod_ref[0, 0:1, :] += jnp.sum(dh, axis=0, keepdims=True)
        dmod_ref[0, 1:2, :] += jnp.sum(dh * xx, axis=0, keepdims=True)
        dmod_ref[0, 2:3, :] += dgate

    tile = pl.BlockSpec((tm, dm), lambda i: (i, 0))
    mod = _mod_spec(tiles_per_seq, dm)
    gu_spec = pl.BlockSpec((8, tm, fc), lambda i: (0, i, 0))
    return _call(
        body, job, name=name, grid=(tokens // tm,),
        in_specs=[tile, tile, tile, tile, gu_spec, mod, mod, mod, _const_spec(wgu.shape), _const_spec(wd.shape),
                  _const_spec((1, dm))],
        out_specs=[tile, gu_spec, tile, pl.BlockSpec((4, tm, fc), lambda i: (0, i, 0)), tile,
                   pl.BlockSpec((2, dm), lambda i: (0, 0)),
                   pl.BlockSpec((1, 3, dm), lambda i: (i // tiles_per_seq, 0, 0))],
        out_shape=[jax.ShapeDtypeStruct((tokens, dm), F32), jax.ShapeDtypeStruct((8, tokens, fc), BF16),
                   jax.ShapeDtypeStruct((tokens, dm), BF16), jax.ShapeDtypeStruct((4, tokens, fc), BF16),
                   jax.ShapeDtypeStruct((tokens, dm), BF16), jax.ShapeDtypeStruct((2, dm), F32),
                   jax.ShapeDtypeStruct((nseq, 3, dm), F32)],
        args=(dy, r, x, f, gu, sh, sc, gt, wgu, wd, ln_g))


def tn_matmul(a, b, name, job=None):
    na, tokens, kk = a.shape
    nb, _, cc = b.shape
    tt = tokens
    while 4 * tt * (kk + cc) + 8 * kk * cc > TN_VMEM_BUDGET and tt % 2 == 0 and tt > 256:
        tt //= 2
    steps = tokens // tt

    def body(a_ref, b_ref, o_ref, acc_ref):
        t = pl.program_id(2)

        @pl.when(t == 0)
        def _():
            acc_ref[...] = jnp.zeros_like(acc_ref)

        acc_ref[...] += _dot_tn(a_ref[0], b_ref[0])

        @pl.when(t == steps - 1)
        def _():
            o_ref[0, 0] = acc_ref[...].astype(BF16)

    return _call(
        body, job, name=name, grid=(na, nb, steps),
        in_specs=[pl.BlockSpec((1, tt, kk), lambda i, j, t: (i, t, 0)),
                  pl.BlockSpec((1, tt, cc), lambda i, j, t: (j, t, 0))],
        out_specs=[pl.BlockSpec((1, 1, kk, cc), lambda i, j, t: (i, j, 0, 0))],
        out_shape=[jax.ShapeDtypeStruct((na, nb, kk, cc), BF16)],
        scratch_shapes=[pltpu.VMEM((kk, cc), F32)], args=(a, b))


def proj_fwd(x1, sh, sc, w_in, seq, name, job=None):
    tokens, dm = x1.shape
    tm = min(TOKEN_TILE, seq)
    tiles_per_seq = seq // tm
    widths = [N_Q_HEADS * HEAD_DIM, N_KV_HEADS * HEAD_DIM, N_KV_HEADS * HEAD_DIM, 512, 512, 512]
    assert sum(widths) == w_in.shape[1]

    def body(x_ref, sh_ref, sc_ref, w_ref, *outs):
        h = (x_ref[...] * (1.0 + sc_ref[0]) + sh_ref[0]).astype(BF16)
        proj = _dot(h, w_ref[...])
        at = 0
        for o_ref, wdt in zip(outs, widths):
            o_ref[...] = proj[:, at:at + wdt]
            at += wdt

    tile = pl.BlockSpec((tm, dm), lambda i: (i, 0))
    mod = _mod_spec(tiles_per_seq, dm)
    return _call(
        body, job, name=name, grid=(tokens // tm,),
        in_specs=[tile, mod, mod, _const_spec(w_in.shape)],
        out_specs=[pl.BlockSpec((tm, wdt), lambda i: (i, 0)) for wdt in widths],
        out_shape=[jax.ShapeDtypeStruct((tokens, wdt), F32) for wdt in widths],
        args=(x1, sh, sc, w_in))


def _rope_swap_matrix():
    half = ROT_DIM // 2
    i = lax.broadcasted_iota(jnp.int32, (HEAD_DIM, HEAD_DIM), 0)
    j = lax.broadcasted_iota(jnp.int32, (HEAD_DIM, HEAD_DIM), 1)
    hit = ((j < half) & (i == j + half)) | ((j >= half) & (j < ROT_DIM) & (i == j - half))
    return jnp.where(hit, 1.0, 0.0).astype(BF16)


def _swap_halves(v, perm):
    hi = v.astype(BF16)
    rest = v - hi.astype(F32)
    mid = rest.astype(BF16)
    lo = (rest - mid.astype(F32)).astype(BF16)
    return _dot(hi, perm) + _dot(mid, perm) + _dot(lo, perm)


def _rope(v, cos_t, sin_t, perm):
    return v * cos_t + _swap_halves(v, perm) * sin_t


def _unrope(dv, cos_t, sin_t, perm):
    return dv * cos_t + _swap_halves(dv * sin_t, perm)


def _window_mask(rows, first):
    ncol = ATTN_BLOCK if first else 2 * ATTN_BLOCK
    qi = lax.broadcasted_iota(jnp.int32, (rows, ncol), 0) % ATTN_BLOCK
    ki = lax.broadcasted_iota(jnp.int32, (rows, ncol), 1)
    diff = qi - ki if first else qi + ATTN_BLOCK - ki
    return (diff >= 0) & (diff < ATTN_BLOCK)


def _attn_specs(seq):
    q_spec = pl.BlockSpec((1, GQA_GROUP, seq, HEAD_DIM), lambda b, g: (b, g, 0, 0))
    kv_spec = pl.BlockSpec((1, 1, seq, HEAD_DIM), lambda b, g: (b, g, 0, 0))
    rope_spec = pl.BlockSpec((1, seq, HEAD_DIM), lambda b, g: (b, 0, 0))
    sink_spec = pl.BlockSpec((1, GQA_GROUP * ATTN_BLOCK, 1), lambda b, g: (g, 0, 0))
    return q_spec, kv_spec, rope_spec, sink_spec


def _sink_columns(sinks):
    return jnp.repeat(sinks.reshape(N_KV_HEADS, GQA_GROUP), ATTN_BLOCK, axis=1)[:, :, None]


def attn_fwd(q, k, v, cos_t, sin_t, sinks, name, job=None):
    nseq, _, seq, _ = q.shape
    nblk = seq // ATTN_BLOCK
    rows = GQA_GROUP * ATTN_BLOCK
    scale = HEAD_DIM ** -0.5

    def body(q_ref, k_ref, v_ref, cos_ref, sin_ref, sink_ref, o_ref, kr_ref):
        perm = _rope_swap_matrix()
        kr_ref[...] = _rope(k_ref[0, 0], cos_ref[0], sin_ref[0], perm).astype(BF16)
        sink = sink_ref[0]
        for n in range(nblk):
            lo = n * ATTN_BLOCK
            first = n == 0
            win = pl.ds(lo, ATTN_BLOCK) if first else pl.ds(lo - ATTN_BLOCK, 2 * ATTN_BLOCK)
            blk = pl.ds(lo, ATTN_BLOCK)
            cs = jnp.tile(cos_ref[0, blk, :], (GQA_GROUP, 1))
            sn = jnp.tile(sin_ref[0, blk, :], (GQA_GROUP, 1))
            qg = jnp.concatenate([q_ref[0, hh, blk, :] for hh in range(GQA_GROUP)], axis=0)
            qr = _rope(qg, cs, sn, perm).astype(BF16)
            s = _dot_nt(qr, kr_ref[win, :]) * scale
            s = jnp.where(_window_mask(rows, first), s, NEG_BIG)
            m = jnp.maximum(jnp.max(s, axis=-1, keepdims=True), sink)
            p = jnp.exp(s - m)
            denom = jnp.sum(p, axis=-1, keepdims=True) + jnp.exp(sink - m)
            out = _dot((p / denom).astype(BF16), v_ref[0, 0, win, :].astype(BF16))
            for hh in range(GQA_GROUP):
                o_ref[0, hh, blk, :] = out[hh * ATTN_BLOCK:(hh + 1) * ATTN_BLOCK]

    q_spec, kv_spec, rope_spec, sink_spec = _attn_specs(seq)
    return _call(
        body, job, name=name, grid=(nseq, N_KV_HEADS),
        in_specs=[q_spec, kv_spec, kv_spec, rope_spec, rope_spec, sink_spec],
        out_specs=[q_spec], out_shape=[jax.ShapeDtypeStruct(q.shape, F32)],
        scratch_shapes=[pltpu.VMEM((seq, HEAD_DIM), BF16)],
        args=(q, k, v, cos_t, sin_t, _sink_columns(sinks)))


def attn_bwd(q, k, v, do, cos_t, sin_t, sinks, name, job=None):
    nseq, _, seq, _ = q.shape
    nblk = seq // ATTN_BLOCK
    rows = GQA_GROUP * ATTN_BLOCK
    scale = HEAD_DIM ** -0.5

    def body(q_ref, k_ref, v_ref, do_ref, cos_ref, sin_ref, sink_ref, dq_ref, dk_ref, dv_ref, ds_ref,
             kr_ref, dkr_ref):
        perm = _rope_swap_matrix()
        kr_ref[...] = _rope(k_ref[0, 0], cos_ref[0], sin_ref[0], perm).astype(BF16)
        dkr_ref[...] = jnp.zeros_like(dkr_ref)
        dv_ref[...] = jnp.zeros_like(dv_ref)
        sink = sink_ref[0]
        dsink = jnp.zeros((rows, 1), F32)
        for n in range(nblk):
            lo = n * ATTN_BLOCK
            first = n == 0
            win = pl.ds(lo, ATTN_BLOCK) if first else pl.ds(lo - ATTN_BLOCK, 2 * ATTN_BLOCK)
            blk = pl.ds(lo, ATTN_BLOCK)
            cs = jnp.tile(cos_ref[0, blk, :], (GQA_GROUP, 1))
            sn = jnp.tile(sin_ref[0, blk, :], (GQA_GROUP, 1))
            qg = jnp.concatenate([q_ref[0, hh, blk, :] for hh in range(GQA_GROUP)], axis=0)
            dog = jnp.concatenate([do_ref[0, hh, blk, :] for hh in range(GQA_GROUP)], axis=0).astype(BF16)
            qr = _rope(qg, cs, sn, perm).astype(BF16)
            kw = kr_ref[win, :]
            vw = v_ref[0, 0, win, :].astype(BF16)
            s = _dot_nt(qr, kw) * scale
            s = jnp.where(_window_mask(rows, first), s, NEG_BIG)
            m = jnp.maximum(jnp.max(s, axis=-1, keepdims=True), sink)
            p = jnp.exp(s - m)
            e_sink = jnp.exp(sink - m)
            inv = 1.0 / (jnp.sum(p, axis=-1, keepdims=True) + e_sink)
            pn = p * inv
            pn16 = pn.astype(BF16)
            dv_ref[0, 0, win, :] += _dot_tn(pn16, dog)
            dp = _dot_nt(dog, vw)
            delta = jnp.sum(dp * pn, axis=-1, keepdims=True)
            dsink = dsink - e_sink * inv * delta
            ds = (pn * (dp - delta)).astype(BF16)
            dqr = _dot(ds, kw) * scale
            dkr_ref[win, :] += _dot_tn(ds, qr) * scale
            dqg = _unrope(dqr, cs, sn, perm)
            for hh in range(GQA_GROUP):
                dq_ref[0, hh, blk, :] = dqg[hh * ATTN_BLOCK:(hh + 1) * ATTN_BLOCK]
        dk_ref[0, 0] = _unrope(dkr_ref[...], cos_ref[0], sin_ref[0], perm)
        ds_ref[0, 0] = dsink

    q_spec, kv_spec, rope_spec, sink_spec = _attn_specs(seq)
    return _call(
        body, job, name=name, grid=(nseq, N_KV_HEADS),
        in_specs=[q_spec, kv_spec, kv_spec, q_spec, rope_spec, rope_spec, sink_spec],
        out_specs=[q_spec, kv_spec, kv_spec, pl.BlockSpec((1, 1, rows, 1), lambda b, g: (b, g, 0, 0))],
        out_shape=[jax.ShapeDtypeStruct(q.shape, F32), jax.ShapeDtypeStruct(k.shape, F32),
                   jax.ShapeDtypeStruct(k.shape, F32), jax.ShapeDtypeStruct((nseq, N_KV_HEADS, rows, 1), F32)],
        scratch_shapes=[pltpu.VMEM((seq, HEAD_DIM), BF16), pltpu.VMEM((seq, HEAD_DIM), F32)],
        args=(q, k, v, do, cos_t, sin_t, _sink_columns(sinks)))


CONV_COLS = 128


def _shift_down(z, by):
    t = lax.broadcasted_iota(jnp.int32, z.shape, 0)
    return jnp.where(t >= by, pltpu.roll(z, by, 0), 0.0)


def _shift_up(z, by):
    n = z.shape[0]
    t = lax.broadcasted_iota(jnp.int32, z.shape, 0)
    return jnp.where(t < n - by, pltpu.roll(z, n - by, 0), 0.0)


def conv_fwd(u, bg, cg, conv_w, seq, name):
    tokens, width = u.shape

    def body(u_ref, bg_ref, cg_ref, w_ref, o_ref):
        z = cg_ref[...] * u_ref[...]
        yy = w_ref[2:3, :] * z + w_ref[1:2, :] * _shift_down(z, 1) + w_ref[0:1, :] * _shift_down(z, 2)
        o_ref[...] = bg_ref[...] * yy

    col = pl.BlockSpec((seq, CONV_COLS), lambda j, b: (b, j))
    return pl.pallas_call(
        body, name=name, grid=(width // CONV_COLS, tokens // seq),
        in_specs=[col, col, col, pl.BlockSpec((CONV_TAPS, CONV_COLS), lambda j, b: (0, j))],
        out_specs=col, out_shape=jax.ShapeDtypeStruct((tokens, width), F32),
        compiler_params=_params(("parallel", "parallel")),
    )(u, bg, cg, conv_w)


def conv_bwd(dout, u, bg, cg, conv_w, seq, name):
    tokens, width = u.shape

    def body(do_ref, u_ref, bg_ref, cg_ref, w_ref, du_ref, dbg_ref, dcg_ref, dw_ref):
        uu, cg_v, do = u_ref[...], cg_ref[...], do_ref[...]
        z = cg_v * uu
        z1, z2 = _shift_down(z, 1), _shift_down(z, 2)
        yy = w_ref[2:3, :] * z + w_ref[1:2, :] * z1 + w_ref[0:1, :] * z2
        dbg_ref[...] = do * yy
        dyy = do * bg_ref[...]
        dz = w_ref[2:3, :] * dyy + w_ref[1:2, :] * _shift_up(dyy, 1) + w_ref[0:1, :] * _shift_up(dyy, 2)
        du_ref[...] = dz * cg_v
        dcg_ref[...] = dz * uu

        @pl.when(pl.program_id(1) == 0)
        def _():
            dw_ref[...] = jnp.zeros_like(dw_ref)

        dw_ref[0:1, :] += jnp.sum(dyy * z2, axis=0, keepdims=True)
        dw_ref[1:2, :] += jnp.sum(dyy * z1, axis=0, keepdims=True)
        dw_ref[2:3, :] += jnp.sum(dyy * z, axis=0, keepdims=True)

    col = pl.BlockSpec((seq, CONV_COLS), lambda j, b: (b, j))
    w_spec = pl.BlockSpec((CONV_TAPS, CONV_COLS), lambda j, b: (0, j))
    act = jax.ShapeDtypeStruct((tokens, width), F32)
    return pl.pallas_call(
        body, name=name, grid=(width // CONV_COLS, tokens // seq),
        in_specs=[col, col, col, col, w_spec], out_specs=[col, col, col, w_spec],
        out_shape=[act, act, act, jax.ShapeDtypeStruct((CONV_TAPS, width), F32)],
        compiler_params=_params(("parallel", "arbitrary")),
    )(dout, u, bg, cg, conv_w)


def out_fwd(x1, attn, conv, gt, w_out, ln_g, ln_b, seq, name, job=None):
    tokens, dm = x1.shape
    half = attn.shape[1]
    tm = min(TOKEN_TILE, seq)
    tiles_per_seq = seq // tm

    def body(x_ref, a_ref, c_ref, gt_ref, w_ref, lg_ref, lb_ref, xo_ref, r_ref, mi_ref, mix_ref):
        mixin = jnp.concatenate([a_ref[...], c_ref[...]], axis=1).astype(BF16)
        mi_ref[...] = mixin
        mix = _dot(mixin, w_ref[...])
        mix_ref[...] = mix.astype(BF16)
        r = DN_ALPHA * x_ref[...] + (1.0 + gt_ref[0]) * mix
        r_ref[...] = r
        xhat, _ = _ln_stats(r)
        xo_ref[...] = xhat * lg_ref[...] + lb_ref[...]

    tile = pl.BlockSpec((tm, dm), lambda i: (i, 0))
    htile = pl.BlockSpec((tm, half), lambda i: (i, 0))
    return _call(
        body, job, name=name, grid=(tokens // tm,),
        in_specs=[tile, htile, htile, _mod_spec(tiles_per_seq, dm), _const_spec(w_out.shape),
                  _const_spec((1, dm)), _const_spec((1, dm))],
        out_specs=[tile, tile, tile, tile],
        out_shape=[jax.ShapeDtypeStruct((tokens, dm), F32), jax.ShapeDtypeStruct((tokens, dm), F32),
                   jax.ShapeDtypeStruct((tokens, dm), BF16), jax.ShapeDtypeStruct((tokens, dm), BF16)],
        args=(x1, attn, conv, gt, w_out, ln_g, ln_b))


def out_bwd(dy, r, mix, gt, w_out, ln_g, seq, name, job=None):
    tokens, dm = r.shape
    half = dm // 2
    tm = min(TOKEN_TILE, seq)
    tiles_per_seq = seq // tm
    nseq = tokens // seq

    def body(dy_ref, r_ref, mix_ref, gt_ref, w_ref, lg_ref, dres_ref, da_ref, dc_ref, dmix_ref, dln_ref, dgt_ref):
        i = pl.program_id(0)
        dr, dgain, dbias = _ln_bwd(dy_ref[...], r_ref[...], lg_ref[...])

        @pl.when(i == 0)
        def _():
            dln_ref[...] = jnp.zeros_like(dln_ref)

        @pl.when(i % tiles_per_seq == 0)
        def _():
            dgt_ref[...] = jnp.zeros_like(dgt_ref)

        dln_ref[0:1, :] += dgain
        dln_ref[1:2, :] += dbias
        dgt_ref[0] += jnp.sum(dr * mix_ref[...].astype(F32), axis=0, keepdims=True)
        dres_ref[...] = DN_ALPHA * dr
        dmix = ((1.0 + gt_ref[0]) * dr).astype(BF16)
        dmix_ref[...] = dmix
        dmixin = _dot_nt(dmix, w_ref[...])
        da_ref[...] = dmixin[:, :half]
        dc_ref[...] = dmixin[:, half:]

    tile = pl.BlockSpec((tm, dm), lambda i: (i, 0))
    htile = pl.BlockSpec((tm, half), lambda i: (i, 0))
    return _call(
        body, job, name=name, grid=(tokens // tm,),
        in_specs=[tile, tile, tile, _mod_spec(tiles_per_seq, dm), _const_spec(w_out.shape), _const_spec((1, dm))],
        out_specs=[tile, htile, htile, tile, pl.BlockSpec((2, dm), lambda i: (0, 0)),
                   pl.BlockSpec((1, 1, dm), lambda i: (i // tiles_per_seq, 0, 0))],
        out_shape=[jax.ShapeDtypeStruct((tokens, dm), F32), jax.ShapeDtypeStruct((tokens, half), F32),
                   jax.ShapeDtypeStruct((tokens, half), F32), jax.ShapeDtypeStruct((tokens, dm), BF16),
                   jax.ShapeDtypeStruct((2, dm), F32), jax.ShapeDtypeStruct((nseq, 1, dm), F32)],
        args=(dy, r, mix, gt, w_out, ln_g))


def proj_bwd(parts, dres, x1, sh, sc, w_in, seq, name):
    tokens, dm = x1.shape
    tm = min(TOKEN_TILE, seq)
    tiles_per_seq = seq // tm
    nseq = tokens // seq
    widths = [p.shape[1] for p in parts]
    total = sum(widths)

    def body(*refs):
        part_refs = refs[:6]
        dres_ref, x_ref, sh_ref, sc_ref, w_ref, dx_ref, dproj_ref, h_ref, dmod_ref = refs[6:]
        dproj = jnp.concatenate([p[...] for p in part_refs], axis=1).astype(BF16)
        dproj_ref[...] = dproj
        dh = _dot_nt(dproj, w_ref[...])
        xx = x_ref[...]
        one_sc = 1.0 + sc_ref[0]
        h_ref[...] = (xx * one_sc + sh_ref[0]).astype(BF16)
        dx_ref[...] = dres_ref[...] + dh * one_sc

        @pl.when(pl.program_id(0) % tiles_per_seq == 0)
        def _():
            dmod_ref[...] = jnp.zeros_like(dmod_ref)

        dmod_ref[0, 0:1, :] += jnp.sum(dh, axis=0, keepdims=True)
        dmod_ref[0, 1:2, :] += jnp.sum(dh * xx, axis=0, keepdims=True)

    tile = pl.BlockSpec((tm, dm), lambda i: (i, 0))
    mod = _mod_spec(tiles_per_seq, dm)
    return pl.pallas_call(
        body, name=name, grid=(tokens // tm,),
        in_specs=[pl.BlockSpec((tm, wdt), lambda i: (i, 0)) for wdt in widths]
        + [tile, tile, mod, mod, _const_spec(w_in.shape)],
        out_specs=[tile, pl.BlockSpec((tm, total), lambda i: (i, 0)), tile,
                   pl.BlockSpec((1, 2, dm), lambda i: (i // tiles_per_seq, 0, 0))],
        out_shape=[jax.ShapeDtypeStruct((tokens, dm), F32), jax.ShapeDtypeStruct((tokens, total), BF16),
                   jax.ShapeDtypeStruct((tokens, dm), BF16), jax.ShapeDtypeStruct((nseq, 2, dm), F32)],
        compiler_params=_params(("arbitrary",)),
    )(*parts, dres, x1, sh, sc, w_in)


def _heads(t, nseq, seq, nheads):
    return t.reshape(nseq, seq, nheads, HEAD_DIM).transpose(0, 2, 1, 3)


def _unheads(t):
    nseq, nheads, seq, _ = t.shape
    return t.transpose(0, 2, 1, 3).reshape(nseq * seq, nheads * HEAD_DIM)


def _rope_tables(positions):
    inv_freq = jnp.power(jnp.float32(ROPE_THETA), -jnp.arange(0, ROT_DIM, 2, dtype=F32) / ROT_DIM)
    ang = positions.astype(F32)[..., None] * inv_freq
    cos, sin = jnp.cos(ang), jnp.sin(ang)
    rest = positions.shape + (HEAD_DIM - ROT_DIM,)
    cos_t = jnp.concatenate([cos, cos, jnp.ones(rest, F32)], axis=-1)
    sin_t = jnp.concatenate([-sin, sin, jnp.zeros(rest, F32)], axis=-1)
    return cos_t, sin_t


def kernel(x, c, positions, w_ada, b_ada, ffn1_w_gate_up, ffn1_w_down, ln1_g, ln1_b, w_in, conv_w, attn_sinks, w_out, ln2_g, ln2_b, ffn2_w_gate_up, ffn2_w_down, ln3_g, ln3_b, loss_target, m_w_ada, m_b_ada, m_ffn1_w_gate_up, m_ffn1_w_down, m_ln1_g, m_ln1_b, m_w_in, m_conv_w, m_attn_sinks, m_w_out, m_ln2_g, m_ln2_b, m_ffn2_w_gate_up, m_ffn2_w_down, m_ln3_g, m_ln3_b, v_w_ada, v_b_ada, v_ffn1_w_gate_up, v_ffn1_w_down, v_ln1_g, v_ln1_b, v_w_in, v_conv_w, v_attn_sinks, v_w_out, v_ln2_g, v_ln2_b, v_ffn2_w_gate_up, v_ffn2_w_down, v_ln3_g, v_ln3_b):
    nseq, seq, dm = x.shape
    tokens = nseq * seq
    dev = 4 * lax.axis_index("x") + 2 * lax.axis_index("y") + lax.axis_index("c")
    ada_cols = w_ada.shape[2]
    ff = ffn1_w_down.shape[1] * N_DEV
    fc = ff // 4
    in_cols = w_in.shape[2]
    conv_cols = conv_w.shape[2]

    c_all, convw_all = all_gather([c, conv_w[0]], "gather_cond")
    wgu1, wd1 = all_gather([ffn1_w_gate_up[0].astype(BF16), ffn1_w_down[0].astype(BF16)], "gather_ffn1")
    c_all = c_all.reshape(N_DEV * nseq, dm)
    convw_full = convw_all.transpose(1, 0, 2).reshape(CONV_TAPS, N_DEV * conv_cols)
    wd1 = wd1.reshape(4, fc, dm)

    b_cols = lax.dynamic_slice(b_ada, (0, dev * ada_cols), (1, ada_cols))
    cond_all, mod_cols = ada_fwd(c_all, w_ada[0], b_cols, "ada_fwd")
    (mod_all,) = all_gather([mod_cols], "gather_mod")
    mod = lax.dynamic_slice(mod_all, (0, dev * nseq, 0), (N_DEV, nseq, ada_cols))
    mod = mod.transpose(1, 0, 2).reshape(nseq, 9, 1, dm)
    sh1, sc1, g1, sh2, sc2, g2, sh3, sc3, g3 = [mod[:, i] for i in range(9)]

    x0 = x.reshape(tokens, dm)
    spread = gather_spread_job([w_in[0].astype(BF16), w_out[0].astype(BF16), ffn2_w_down[0].astype(BF16)])
    (x1, r1, gu1, f1), spread = ffn_fwd(x0, sh1, sc1, g1, wgu1, wd1, ln1_g, ln1_b, seq, "ffn1_fwd", job=spread)
    win, wout = run_job(gather_forward_job(spread[:2]), "gather_mix_forward")
    win = win.transpose(1, 0, 2).reshape(dm, N_DEV * in_cols)
    wout = wout.reshape(dm, dm)
    (q, k, v, u, bg, cg), (wd2,) = proj_fwd(x1, sh2, sc2, win, seq, "proj_fwd", job=gather_forward_job(spread[2:]))
    wd2 = wd2.reshape(4, fc, dm)
    cos_t, sin_t = _rope_tables(positions)
    qh, kh, vh = _heads(q, nseq, seq, N_Q_HEADS), _heads(k, nseq, seq, N_KV_HEADS), _heads(v, nseq, seq, N_KV_HEADS)
    sinks = attn_sinks[0]
    (attn_h,), spread = attn_fwd(qh, kh, vh, cos_t, sin_t, sinks, "attn_fwd",
                                 job=gather_spread_job([ffn2_w_gate_up[0].astype(BF16)]))
    attn = _unheads(attn_h)
    conv = conv_fwd(u, bg, cg, convw_full, seq, "conv_fwd")
    (x2, r2, mixin, mix), (wgu2,) = out_fwd(x1, attn, conv, g2, wout, ln2_g, ln2_b, seq, "out_fwd",
                                            job=gather_forward_job(spread))
    target = loss_target.reshape(tokens, dm)
    (dy3, loss_part, r3, gu3, f3), _ = ffn_fwd(x2, sh3, sc3, g3, wgu2, wd2, ln3_g, ln3_b, seq, "ffn2_fwd", target=target)

    (dx2, dgu3, df3, a3, h3, dln3, dmod3), _ = ffn_bwd(dy3, r3, x2, f3, gu3, sh3, sc3, g3, wgu2, wd2, ln3_g, seq, "ffn2_bwd")
    g_wd2 = tn_matmul(a3, df3[None], "ffn2_dwd")[0][0].reshape(N_DEV, ff // N_DEV, dm)
    g_wgu2 = tn_matmul(h3[None], dgu3, "ffn2_dwgu")[0][0].reshape(N_DEV, dm, fc)
    (dres2, dattn, dconv, dmix, dln2, dg2), swapped = out_bwd(dx2, r2, mix, g2, wout, ln2_g, seq, "out_bwd",
                                                              job=swap_job([g_wgu2, g_wd2]))
    p_wgu2, own_wgu2 = pair_sum(g_wgu2, swapped[0], "pair_wgu2")
    p_wd2, own_wd2 = pair_sum(g_wd2, swapped[1], "pair_wd2")
    g_wout = tn_matmul(mixin[None], dmix[None], "dwout")[0][0].reshape(N_DEV, dm // N_DEV, dm)
    du, dbg, dcg, dconvw = conv_bwd(dconv, u, bg, cg, convw_full, seq, "conv_bwd")
    (dqh, dkh, dvh, dsink_rows), (far_wgu2, far_wd2) = attn_bwd(
        qh, kh, vh, _heads(dattn, nseq, seq, N_Q_HEADS), cos_t, sin_t, sinks, "attn_bwd",
        job=chip_exchange_job([p_wgu2, p_wd2]))
    parts = [_unheads(dqh), _unheads(dkh), _unheads(dvh), du, dbg, dcg]
    dx1, dproj, h2, dmod2 = proj_bwd(parts, dres2, x1, sh2, sc2, win, seq, "proj_bwd")
    g_win = tn_matmul(h2[None], dproj[None], "dwin")[0][0].reshape(dm, N_DEV, in_cols).transpose(1, 0, 2)
    (dx0, dgu1, df1, a1, h1, dln1, dmod1), swapped = ffn_bwd(
        dx1, r1, x0, f1, gu1, sh1, sc1, g1, wgu1, wd1, ln1_g, seq, "ffn1_bwd", job=swap_job([g_wout, g_win]))
    p_wout, own_wout = pair_sum(g_wout, swapped[0], "pair_wout")
    p_win, own_win = pair_sum(g_win, swapped[1], "pair_win")
    (g_wd1,), (far_wout, far_win) = tn_matmul(a1, df1[None], "ffn1_dwd", job=chip_exchange_job([p_wout, p_win]))
    g_wd1 = g_wd1.reshape(N_DEV, ff // N_DEV, dm)
    g_wgu1 = tn_matmul(h1[None], dgu1, "ffn1_dwgu")[0][0].reshape(N_DEV, dm, fc)

    grads = {
        "ffn1_w_gate_up": reduce_scatter(g_wgu1, "rs_wgu1"), "ffn1_w_down": reduce_scatter(g_wd1, "rs_wd1"),
        "w_in": own_win, "w_out": own_wout, "ffn2_w_gate_up": own_wgu2, "ffn2_w_down": own_wd2,
    }
    others = {"w_in": far_win, "w_out": far_wout, "ffn2_w_gate_up": far_wgu2, "ffn2_w_down": far_wd2}

    dmod = jnp.concatenate([dmod1, dmod2, dg2, dmod3], axis=1).reshape(nseq, 9 * dm)
    (dmod_all,) = all_gather([dmod], "gather_dmod")
    dmod_cols = lax.dynamic_slice(dmod_all.reshape(N_DEV * nseq, 9 * dm), (0, dev * ada_cols), (N_DEV * nseq, ada_cols))
    grads["w_ada"], gb_cols = ada_bwd(cond_all, dmod_cols, "ada_bwd")

    dsinks = jnp.sum(dsink_rows.reshape(nseq, N_Q_HEADS, ATTN_BLOCK), axis=(0, 2))
    small = jnp.zeros((8, dm), F32)
    small = small.at[0:2].set(dln1).at[2:4].set(dln2).at[4:6].set(dln3)
    small = small.at[6, 0:N_Q_HEADS].set(dsinks).at[7, 0].set(loss_part[0, 0])
    small_all, dconvw_all, gb_all = all_gather([small, dconvw, gb_cols], "gather_small")
    small_sum = sum_devices(small_all, "sum_small")
    dconvw_sum = sum_devices(dconvw_all, "sum_convw")
    loss = small_sum[7, 0]
    grads["b_ada"] = gb_all.reshape(1, N_DEV * ada_cols)
    grads["conv_w"] = lax.dynamic_slice(dconvw_sum, (0, dev * conv_cols), (CONV_TAPS, conv_cols))
    grads["attn_sinks"] = small_sum[6:7, 0:N_Q_HEADS]
    for i, nm in enumerate(["ln1_g", "ln1_b", "ln2_g", "ln2_b", "ln3_g", "ln3_b"]):
        grads[nm] = small_sum[i:i + 1]

    given = dict(w_ada=(w_ada, m_w_ada, v_w_ada), b_ada=(b_ada, m_b_ada, v_b_ada),
                 ffn1_w_gate_up=(ffn1_w_gate_up, m_ffn1_w_gate_up, v_ffn1_w_gate_up),
                 ffn1_w_down=(ffn1_w_down, m_ffn1_w_down, v_ffn1_w_down),
                 ln1_g=(ln1_g, m_ln1_g, v_ln1_g), ln1_b=(ln1_b, m_ln1_b, v_ln1_b),
                 w_in=(w_in, m_w_in, v_w_in), conv_w=(conv_w, m_conv_w, v_conv_w),
                 attn_sinks=(attn_sinks, m_attn_sinks, v_attn_sinks), w_out=(w_out, m_w_out, v_w_out),
                 ln2_g=(ln2_g, m_ln2_g, v_ln2_g), ln2_b=(ln2_b, m_ln2_b, v_ln2_b),
                 ffn2_w_gate_up=(ffn2_w_gate_up, m_ffn2_w_gate_up, v_ffn2_w_gate_up),
                 ffn2_w_down=(ffn2_w_down, m_ffn2_w_down, v_ffn2_w_down),
                 ln3_g=(ln3_g, m_ln3_g, v_ln3_g), ln3_b=(ln3_b, m_ln3_b, v_ln3_b))
    order = ["w_ada", "b_ada", "ffn1_w_gate_up", "ffn1_w_down", "ln1_g", "ln1_b", "w_in", "conv_w", "attn_sinks",
             "w_out", "ln2_g", "ln2_b", "ffn2_w_gate_up", "ffn2_w_down", "ln3_g", "ln3_b"]
    out_g, out_d, out_m, out_v = [], [], [], []
    for nm in order:
        w, m, v = given[nm]
        shape = w.shape
        two_d = (shape[-2], shape[-1])
        g2d, d, nm_new, nv_new = adamw(w.reshape(two_d), grads[nm].reshape(two_d), m.reshape(two_d), v.reshape(two_d),
                                       "adamw_" + nm, others=others.get(nm))
        out_g.append(g2d.reshape(shape))
        out_d.append(d.reshape(shape))
        out_m.append(nm_new.reshape(shape))
        out_v.append(nv_new.reshape(shape))
    grad_x = dx0.reshape(nseq, seq, dm)
    return (loss, grad_x, *out_g, *out_d, *out_m, *out_v)
```

```python
import functools

import jax
import jax.numpy as jnp
from jax import lax
from jax.experimental import pallas as pl
from jax.experimental.pallas import tpu as pltpu

F32 = jnp.float32
BF16 = jnp.bfloat16
MESH = pl.DeviceIdType.MESH

N_DEV = 8
N_CHIP = 4
HEAD_DIM = 64
N_Q_HEADS = 8
N_KV_HEADS = 2
GQA_GROUP = N_Q_HEADS // N_KV_HEADS
ATTN_BLOCK = 128
ROT_DIM = 16
ROPE_THETA = 500000.0
CONV_TAPS = 3
LN_EPS = 1e-5
DN_ALPHA = 2.0 ** 0.25
ADAM_LR = 0.001
ADAM_B1 = 0.9
ADAM_B2 = 0.999
ADAM_EPS = 1e-08
ADAM_WD = 0.01
ADAM_STEP = 10
NEG_BIG = -1e30

VMEM_LIMIT = 56 * 1024 * 1024
TOKEN_TILE = 256
FFN_FWD_TILE = 512
TN_VMEM_BUDGET = 36 * 1024 * 1024


def _params(semantics=None, vmem=VMEM_LIMIT):
    return pltpu.CompilerParams(dimension_semantics=semantics, vmem_limit_bytes=vmem)


def _dot(a, b):
    return jnp.dot(a, b, preferred_element_type=F32)


def _dot_nt(a, b):
    return lax.dot_general(a, b, (((1,), (1,)), ((), ())), preferred_element_type=F32)


def _dot_tn(a, b):
    return lax.dot_general(a, b, (((0,), (0,)), ((), ())), preferred_element_type=F32)


def _sigmoid(x):
    return 1.0 / (1.0 + jnp.exp(-x))


def _ln_stats(r):
    mu = jnp.mean(r, axis=-1, keepdims=True)
    d = r - mu
    var = jnp.mean(d * d, axis=-1, keepdims=True)
    rstd = lax.rsqrt(var + LN_EPS)
    return d * rstd, rstd


def _ln_bwd(dy, r, g):
    xhat, rstd = _ln_stats(r)
    dxhat = dy * g
    c1 = jnp.mean(dxhat, axis=-1, keepdims=True)
    c2 = jnp.mean(dxhat * xhat, axis=-1, keepdims=True)
    dr = rstd * (dxhat - c1 - xhat * c2)
    return dr, jnp.sum(dy * xhat, axis=0, keepdims=True), jnp.sum(dy, axis=0, keepdims=True)


def _const_spec(shape):
    nd = len(shape)
    return pl.BlockSpec(shape, lambda *_: (0,) * nd, pipeline_mode=pl.Buffered(1))


def all_gather(arrs, name):
    n = len(arrs)

    def body(*refs):
        ins, outs = refs[:n], refs[n:2 * n]
        send_sems, recv_sems, local_sems = refs[2 * n:]
        x, y, c = lax.axis_index("x"), lax.axis_index("y"), lax.axis_index("c")
        me, sibling = (x, y, c), (x, y, 1 - c)
        chips = [(1 - x, y), (x, 1 - y), (1 - x, 1 - y)]

        def slot(i, p):
            return outs[i].at[4 * p[0] + 2 * p[1] + p[2]]

        def copy(i, k, block, to, src=None):
            return pltpu.make_async_remote_copy(
                src_ref=slot(i, block) if src is None else src, dst_ref=slot(i, block),
                send_sem=send_sems.at[i, k], recv_sem=recv_sems.at[i, k],
                device_id=to, device_id_type=MESH)

        mine = [pltpu.make_async_copy(ins[i], slot(i, me), local_sems.at[i]) for i in range(n)]
        for cp in mine:
            cp.start()
        first = []
        for i in range(n):
            first.append(copy(i, 0, me, sibling, src=ins[i]))
            first += [copy(i, 1 + j, me, (*chip, c), src=ins[i]) for j, chip in enumerate(chips)]
        for cp in first:
            cp.start()
        passed = []
        for j, chip in enumerate(chips):
            for i in range(n):
                copy(i, 1 + j, (*chip, c), me).wait_recv()
                cp = copy(i, 4 + j, (*chip, c), sibling)
                cp.start()
                passed.append(cp)
        for i in range(n):
            copy(i, 0, sibling, me).wait_recv()
            for j, chip in enumerate(chips):
                copy(i, 4 + j, (*chip, 1 - c), me).wait_recv()
        for cp in first + passed:
            cp.wait_send()
        for cp in mine:
            cp.wait()

    any_spec = pl.BlockSpec(memory_space=pl.ANY)
    return pl.pallas_call(
        body, name=name,
        out_shape=[jax.ShapeDtypeStruct((N_DEV, *a.shape), a.dtype) for a in arrs],
        in_specs=[any_spec] * n, out_specs=[any_spec] * n,
        scratch_shapes=[pltpu.SemaphoreType.DMA((n, 7)), pltpu.SemaphoreType.DMA((n, 7)),
                        pltpu.SemaphoreType.DMA((n,))],
    )(*arrs)


RS_ROWS = 32


def reduce_scatter(g, name):
    _, rows, cols = g.shape
    nblk = rows // RS_ROWS
    assert nblk * RS_ROWS == rows

    def body(g_ref, out_ref, r1_ref, p_ref, r2_ref, send_sems, recv_sems):
        x, y, c = lax.axis_index("x"), lax.axis_index("y"), lax.axis_index("c")
        sibling = (x, y, 1 - c)
        q_me = 2 * x + y
        swaps = []
        for q in range(N_CHIP):
            cp = pltpu.make_async_remote_copy(
                src_ref=g_ref.at[2 * q + (1 - c)], dst_ref=r1_ref.at[q],
                send_sem=send_sems.at[q], recv_sem=recv_sems.at[q], device_id=sibling, device_id_type=MESH)
            cp.start()
            swaps.append(cp)
        for cp in swaps:
            cp.wait_recv()

        def pair_sum(i, carry):
            r = pl.ds(pl.multiple_of(i * RS_ROWS, RS_ROWS), RS_ROWS)
            for q in range(N_CHIP):
                p_ref[q, r, :] = (g_ref[2 * q + c, r, :].astype(F32) + r1_ref[q, r, :].astype(F32)).astype(BF16)
            return carry

        lax.fori_loop(0, nblk, pair_sum, 0)
        chips = [(1 - x, y), (x, 1 - y), (1 - x, 1 - y)]
        sends = []
        for k, chip in enumerate(chips):
            cp = pltpu.make_async_remote_copy(
                src_ref=p_ref.at[2 * chip[0] + chip[1]], dst_ref=r2_ref.at[k],
                send_sem=send_sems.at[N_CHIP + k], recv_sem=recv_sems.at[N_CHIP + k],
                device_id=(*chip, c), device_id_type=MESH)
            cp.start()
            sends.append(cp)
        for cp in sends:
            cp.wait_recv()

        def total(i, carry):
            r = pl.ds(pl.multiple_of(i * RS_ROWS, RS_ROWS), RS_ROWS)
            acc = g_ref[2 * q_me + c, r, :].astype(F32) + r1_ref[q_me, r, :].astype(F32)
            for k in range(3):
                acc = acc + r2_ref[k, r, :].astype(F32)
            out_ref[r, :] = acc
            return carry

        lax.fori_loop(0, nblk, total, 0)
        for cp in swaps + sends:
            cp.wait_send()

    vmem = pl.BlockSpec(memory_space=pltpu.VMEM)
    return pl.pallas_call(
        body, name=name,
        out_shape=jax.ShapeDtypeStruct((rows, cols), F32),
        in_specs=[vmem], out_specs=vmem,
        scratch_shapes=[pltpu.VMEM((N_CHIP, rows, cols), BF16), pltpu.VMEM((N_CHIP, rows, cols), BF16),
                        pltpu.VMEM((3, rows, cols), BF16),
                        pltpu.SemaphoreType.DMA((N_CHIP + 3,)), pltpu.SemaphoreType.DMA((N_CHIP + 3,))],
        compiler_params=_params(),
    )(g)


def _place():
    x, y, c = lax.axis_index("x"), lax.axis_index("y"), lax.axis_index("c")
    return x, y, c, [(1 - x, y), (x, 1 - y), (1 - x, 1 - y)]


def _slot(p):
    return 4 * p[0] + 2 * p[1] + p[2]


class _Job:
    def __init__(self, ins, outs, nsem, copies, aliases=None, local=None):
        self.ins, self.outs, self.nsem, self.copies = list(ins), list(outs), nsem, copies
        self.aliases = aliases or {}
        self.local = local

    def scratch(self):
        s = [pltpu.SemaphoreType.DMA(self.nsem), pltpu.SemaphoreType.DMA(self.nsem)]
        if self.local is not None:
            s.append(pltpu.SemaphoreType.DMA((len(self.ins),)))
        return s

    def start(self, ins, outs, sems):
        if self.local is not None:
            for cp in self.local(ins, outs, sems[2]):
                cp.start()
        for cp in self.copies(ins, outs, sems[0], sems[1])[0]:
            cp.start()

    def finish(self, ins, outs, sems):
        started, awaited = self.copies(ins, outs, sems[0], sems[1])
        for cp in awaited:
            cp.wait_recv()
        for cp in started:
            cp.wait_send()
        if self.local is not None:
            for cp in self.local(ins, outs, sems[2]):
                cp.wait()


def _remote(src, dst, send, recv, idx, to):
    return pltpu.make_async_remote_copy(src_ref=src, dst_ref=dst, send_sem=send.at[idx], recv_sem=recv.at[idx],
                                        device_id=to, device_id_type=MESH)


def gather_spread_job(shards):
    def copies(ins, outs, send, recv):
        x, y, c, chips = _place()
        me = (x, y, c)
        peers = [(x, y, 1 - c)] + [(*chip, c) for chip in chips]
        started, awaited = [], []
        for i, (src, dst) in enumerate(zip(ins, outs)):
            for k, peer in enumerate(peers):
                started.append(_remote(src, dst.at[_slot(me)], send, recv, (i, k), peer))
                awaited.append(_remote(src, dst.at[_slot(peer)], send, recv, (i, k), peer))
        return started, awaited

    def local(ins, outs, sems):
        x, y, c, _ = _place()
        return [pltpu.make_async_copy(src, dst.at[_slot((x, y, c))], sems.at[i])
                for i, (src, dst) in enumerate(zip(ins, outs))]

    outs = [jax.ShapeDtypeStruct((N_DEV, *a.shape), a.dtype) for a in shards]
    return _Job(shards, outs, (len(shards), 4), copies, local=local)


def gather_forward_job(fulls):
    def copies(ins, outs, send, recv):
        x, y, c, chips = _place()
        started, awaited = [], []
        for i, buf in enumerate(outs):
            for j, chip in enumerate(chips):
                mine, theirs = buf.at[_slot((*chip, c))], buf.at[_slot((*chip, 1 - c))]
                started.append(_remote(mine, mine, send, recv, (i, j), (x, y, 1 - c)))
                awaited.append(_remote(theirs, theirs, send, recv, (i, j), (x, y, 1 - c)))
        return started, awaited

    outs = [jax.ShapeDtypeStruct(a.shape, a.dtype) for a in fulls]
    return _Job(fulls, outs, (len(fulls), 3), copies, aliases={i: i for i in range(len(fulls))})


def swap_job(gs):
    def copies(ins, outs, send, recv):
        x, y, c, _ = _place()
        started, awaited = [], []
        for i, (g, r1) in enumerate(zip(ins, outs)):
            for q in range(N_CHIP):
                started.append(_remote(g.at[2 * q + (1 - c)], r1.at[q], send, recv, (i, q), (x, y, 1 - c)))
                awaited.append(_remote(g.at[2 * q + c], r1.at[q], send, recv, (i, q), (x, y, 1 - c)))
        return started, awaited

    outs = [jax.ShapeDtypeStruct((N_CHIP, *g.shape[1:]), g.dtype) for g in gs]
    return _Job(gs, outs, (len(gs), N_CHIP), copies)


def chip_exchange_job(ps):
    def copies(ins, outs, send, recv):
        x, y, c, chips = _place()
        started, awaited = [], []
        for i, (p, r2) in enumerate(zip(ins, outs)):
            for k, chip in enumerate(chips):
                started.append(_remote(p.at[2 * chip[0] + chip[1]], r2.at[k], send, recv, (i, k), (*chip, c)))
                awaited.append(_remote(p.at[2 * x + y], r2.at[k], send, recv, (i, k), (*chip, c)))
        return started, awaited

    outs = [jax.ShapeDtypeStruct((3, *p.shape[1:]), p.dtype) for p in ps]
    return _Job(ps, outs, (len(ps), 3), copies)


def _call(body, job, *, name, grid, in_specs, out_specs, out_shape, args, scratch_shapes=(), vmem=VMEM_LIMIT):
    if job is None:
        res = pl.pallas_call(
            body, name=name, grid=grid, in_specs=in_specs, out_specs=out_specs, out_shape=out_shape,
            scratch_shapes=list(scratch_shapes), compiler_params=_params(("arbitrary",) * len(grid), vmem),
        )(*args)
        return res, []
    n_in, n_out, n_scr = len(in_specs), len(out_specs), len(scratch_shapes)
    j_in, j_out = len(job.ins), len(job.outs)

    def with_copies(*refs):
        at = 0
        ins = refs[at:at + n_in]; at += n_in
        jins = refs[at:at + j_in]; at += j_in
        outs = refs[at:at + n_out]; at += n_out
        jouts = refs[at:at + j_out]; at += j_out
        scr = refs[at:at + n_scr]; at += n_scr
        sems = refs[at:]
        ids = [pl.program_id(d) for d in range(len(grid))]
        first = functools.reduce(jnp.logical_and, [i == 0 for i in ids])
        last = functools.reduce(jnp.logical_and, [i == n - 1 for i, n in zip(ids, grid)])

        @pl.when(first)
        def _():
            job.start(jins, jouts, sems)

        body(*ins, *outs, *scr)

        @pl.when(last)
        def _():
            job.finish(jins, jouts, sems)

    any_spec = pl.BlockSpec(memory_space=pl.ANY)
    res = pl.pallas_call(
        with_copies, name=name, grid=grid,
        in_specs=list(in_specs) + [any_spec] * j_in, out_specs=list(out_specs) + [any_spec] * j_out,
        out_shape=list(out_shape) + list(job.outs),
        input_output_aliases={n_in + i: n_out + o for i, o in job.aliases.items()},
        scratch_shapes=list(scratch_shapes) + job.scratch(),
        compiler_params=_params(("arbitrary",) * len(grid), vmem),
    )(*args, *job.ins)
    return res[:n_out], res[n_out:]


def run_job(job, name):
    def body(*refs):
        j_in, j_out = len(job.ins), len(job.outs)
        ins, outs, sems = refs[:j_in], refs[j_in:j_in + j_out], refs[j_in + j_out:]
        job.start(ins, outs, sems)
        job.finish(ins, outs, sems)

    any_spec = pl.BlockSpec(memory_space=pl.ANY)
    return pl.pallas_call(
        body, name=name, in_specs=[any_spec] * len(job.ins), out_specs=[any_spec] * len(job.outs),
        out_shape=list(job.outs), input_output_aliases=dict(job.aliases), scratch_shapes=job.scratch(),
    )(*job.ins)


def pair_sum(g, r1, name):
    _, rows, cols = g.shape
    rb = rows if rows <= 512 else 256
    assert rows % rb == 0

    def body(g_ref, r1_ref, p_ref, own_ref):
        x, y, c, _ = _place()
        s = g_ref[c].astype(F32) + r1_ref[0].astype(F32)
        p_ref[0] = s.astype(BF16)

        @pl.when(pl.program_id(1) == 2 * x + y)
        def _():
            own_ref[...] = s

    return pl.pallas_call(
        body, name=name, grid=(rows // rb, N_CHIP),
        in_specs=[pl.BlockSpec((2, rb, cols), lambda i, q: (q, i, 0)), pl.BlockSpec((1, rb, cols), lambda i, q: (q, i, 0))],
        out_specs=[pl.BlockSpec((1, rb, cols), lambda i, q: (q, i, 0)), pl.BlockSpec((rb, cols), lambda i, q: (i, 0))],
        out_shape=[jax.ShapeDtypeStruct((N_CHIP, rows, cols), BF16), jax.ShapeDtypeStruct((rows, cols), F32)],
        compiler_params=_params(("arbitrary", "arbitrary")),
    )(g, r1)


def sum_devices(a, name):
    def body(a_ref, o_ref):
        acc = a_ref[0]
        for d in range(1, N_DEV):
            acc = acc + a_ref[d]
        o_ref[...] = acc

    return pl.pallas_call(body, name=name, out_shape=jax.ShapeDtypeStruct(a.shape[1:], F32))(a)


def adamw(w, g, m, v, name, others=None):
    rows, cols = w.shape
    rb = rows
    for cand in (256, 128, 64, 32, 16, 8):
        if rows % cand == 0:
            rb = cand
            break

    def body(*refs):
        if others is None:
            w_ref, g_ref, m_ref, v_ref, d_ref, nm_ref, nv_ref = refs
            gg = g_ref[...]
        else:
            w_ref, g_ref, m_ref, v_ref, r2_ref, go_ref, d_ref, nm_ref, nv_ref = refs
            gg = g_ref[...]
            for k in range(3):
                gg = gg + r2_ref[k].astype(F32)
            go_ref[...] = gg
        nm = ADAM_B1 * m_ref[...] + (1.0 - ADAM_B1) * gg
        nv = ADAM_B2 * v_ref[...] + (1.0 - ADAM_B2) * (gg * gg)
        m_hat = nm / (1.0 - ADAM_B1 ** ADAM_STEP)
        v_hat = nv / (1.0 - ADAM_B2 ** ADAM_STEP)
        d_ref[...] = -ADAM_LR * (m_hat / (jnp.sqrt(v_hat) + ADAM_EPS) + ADAM_WD * w_ref[...])
        nm_ref[...] = nm
        nv_ref[...] = nv

    spec = pl.BlockSpec((rb, cols), lambda i: (i, 0))
    out = jax.ShapeDtypeStruct((rows, cols), F32)
    in_specs, args = [spec] * 4, [w, g, m, v]
    if others is not None:
        in_specs.append(pl.BlockSpec((3, rb, cols), lambda i: (0, i, 0)))
        args.append(others)
    n_out = 3 if others is None else 4
    res = pl.pallas_call(
        body, name=name, grid=(rows // rb,), in_specs=in_specs, out_specs=[spec] * n_out,
        out_shape=[out] * n_out, compiler_params=_params(("parallel",)),
    )(*args)
    return (g, *res) if others is None else tuple(res)


def ada_fwd(c_all, w_cols, b_cols, name):
    def body(c_ref, w_ref, b_ref, cond_ref, mod_ref):
        cc = c_ref[...]
        cond = (cc * _sigmoid(cc)).astype(BF16)
        cond_ref[...] = cond
        mod_ref[...] = _dot(cond, w_ref[...].astype(BF16)) + b_ref[...]

    n, cols = c_all.shape[0], w_cols.shape[1]
    return pl.pallas_call(
        body, name=name,
        out_shape=[jax.ShapeDtypeStruct(c_all.shape, BF16), jax.ShapeDtypeStruct((n, cols), F32)],
        compiler_params=_params(),
    )(c_all, w_cols, b_cols)


def ada_bwd(cond_all, dmod_cols, name):
    def body(c_ref, d_ref, gw_ref, gb_ref):
        d = d_ref[...]
        gw_ref[...] = _dot_tn(c_ref[...], d.astype(BF16))
        gb_ref[...] = jnp.sum(d, axis=0, keepdims=True)

    dm, cols = cond_all.shape[1], dmod_cols.shape[1]
    return pl.pallas_call(
        body, name=name,
        out_shape=[jax.ShapeDtypeStruct((dm, cols), F32), jax.ShapeDtypeStruct((1, cols), F32)],
        compiler_params=_params(),
    )(cond_all, dmod_cols)


def _mod_spec(tiles_per_seq, dm):
    return pl.BlockSpec((1, 1, dm), lambda i: (i // tiles_per_seq, 0, 0))


def ffn_fwd(x, sh, sc, gt, wgu, wd, ln_g, ln_b, seq, name, target=None, job=None):
    tokens, dm = x.shape
    fc = wgu.shape[2]
    tm = min(FFN_FWD_TILE, seq)
    tiles_per_seq = seq // tm
    with_loss = target is not None

    def body(*refs):
        if with_loss:
            (x_ref, sh_ref, sc_ref, gt_ref, wgu_ref, wd_ref, lg_ref, lb_ref, t_ref,
             xo_ref, loss_ref, r_ref, gu_ref, f_ref) = refs
        else:
            (x_ref, sh_ref, sc_ref, gt_ref, wgu_ref, wd_ref, lg_ref, lb_ref,
             xo_ref, r_ref, gu_ref, f_ref) = refs
        xx = x_ref[...]
        h = (xx * (1.0 + sc_ref[0]) + sh_ref[0]).astype(BF16)
        acc = jnp.zeros((tm, dm), F32)
        for k in range(4):
            gk = _dot(h, wgu_ref[k])
            uk = _dot(h, wgu_ref[k + 4])
            gu_ref[k] = gk.astype(BF16)
            gu_ref[k + 4] = uk.astype(BF16)
            a = (gk * _sigmoid(gk) * uk).astype(BF16)
            acc = acc + _dot(a, wd_ref[k])
        f_ref[...] = acc.astype(BF16)
        r = DN_ALPHA * xx + (0.5 * (1.0 + gt_ref[0])) * acc
        r_ref[...] = r
        xhat, _ = _ln_stats(r)
        yy = xhat * lg_ref[...] + lb_ref[...]
        if with_loss:
            err = yy - t_ref[...]
            xo_ref[...] = err * (1.0 / dm)

            @pl.when(pl.program_id(0) == 0)
            def _():
                loss_ref[...] = jnp.zeros_like(loss_ref)

            loss_ref[...] += jnp.full((1, 128), (0.5 / dm) * jnp.sum(err * err), F32)
        else:
            xo_ref[...] = yy

    tile = pl.BlockSpec((tm, dm), lambda i: (i, 0))
    mod = _mod_spec(tiles_per_seq, dm)
    in_specs = [tile, mod, mod, mod, _const_spec(wgu.shape), _const_spec(wd.shape),
                _const_spec((1, dm)), _const_spec((1, dm))]
    args = [x, sh, sc, gt, wgu, wd, ln_g, ln_b]
    out_specs = [tile]
    out_shape = [jax.ShapeDtypeStruct((tokens, dm), F32)]
    if with_loss:
        in_specs.append(tile)
        args.append(target)
        out_specs.append(pl.BlockSpec((1, 128), lambda i: (0, 0)))
        out_shape.append(jax.ShapeDtypeStruct((1, 128), F32))
    out_specs += [tile, pl.BlockSpec((8, tm, fc), lambda i: (0, i, 0)), tile]
    out_shape += [jax.ShapeDtypeStruct((tokens, dm), F32), jax.ShapeDtypeStruct((8, tokens, fc), BF16),
                  jax.ShapeDtypeStruct((tokens, dm), BF16)]
    return _call(body, job, name=name, grid=(tokens // tm,), in_specs=in_specs, out_specs=out_specs,
                 out_shape=out_shape, args=args)


def ffn_bwd(dy, r, x, f, gu, sh, sc, gt, wgu, wd, ln_g, seq, name, job=None):
    tokens, dm = x.shape
    fc = wgu.shape[2]
    tm = min(TOKEN_TILE, seq)
    tiles_per_seq = seq // tm
    nseq = tokens // seq

    def body(dy_ref, r_ref, x_ref, f_ref, gu_ref, sh_ref, sc_ref, gt_ref, wgu_ref, wd_ref, lg_ref,
             dx_ref, dgu_ref, df_ref, a_ref, h_ref, dln_ref, dmod_ref):
        i = pl.program_id(0)
        dr, dgain, dbias = _ln_bwd(dy_ref[...], r_ref[...], lg_ref[...])

        @pl.when(i == 0)
        def _():
            dln_ref[...] = jnp.zeros_like(dln_ref)

        @pl.when(i % tiles_per_seq == 0)
        def _():
            dmod_ref[...] = jnp.zeros_like(dmod_ref)

        dln_ref[0:1, :] += dgain
        dln_ref[1:2, :] += dbias
        df32 = (0.5 * (1.0 + gt_ref[0])) * dr
        df = df32.astype(BF16)
        df_ref[...] = df
        dgate = jnp.sum(dr * (0.5 * f_ref[...].astype(F32)), axis=0, keepdims=True)
        xx = x_ref[...]
        one_sc = 1.0 + sc_ref[0]
        h = (xx * one_sc + sh_ref[0]).astype(BF16)
        h_ref[...] = h
        dh = jnp.zeros((tm, dm), F32)
        for k in range(4):
            da = _dot_nt(df, wd_ref[k])
            gk = gu_ref[k].astype(F32)
            uk = gu_ref[k + 4].astype(F32)
            sg = _sigmoid(gk)
            sil = gk * sg
            a_ref[k] = (sil * uk).astype(BF16)
            du = (da * sil).astype(BF16)
            dg = (da * uk * (sg * (1.0 + gk * (1.0 - sg)))).astype(BF16)
            dgu_ref[k] = dg
            dgu_ref[k + 4] = du
            dh = dh + _dot_nt(dg, wgu_ref[k]) + _dot_nt(du, wgu_ref[k + 4])
        dx_ref[...] = DN_ALPHA * dr + dh * one_sc
        dmod_ref[0, 0:1, :] += jnp.sum(dh, axis=0, keepdims=True)
        dmod_ref[0, 1:2, :] += jnp.sum(dh * xx, axis=0, keepdims=True)
        dmod_ref[0, 2:3, :] += dgate

    tile = pl.BlockSpec((tm, dm), lambda i: (i, 0))
    mod = _mod_spec(tiles_per_seq, dm)
    gu_spec = pl.BlockSpec((8, tm, fc), lambda i: (0, i, 0))
    return _call(
        body, job, name=name, grid=(tokens // tm,),
        in_specs=[tile, tile, tile, tile, gu_spec, mod, mod, mod, _const_spec(wgu.shape), _const_spec(wd.shape),
                  _const_spec((1, dm))],
        out_specs=[tile, gu_spec, tile, pl.BlockSpec((4, tm, fc), lambda i: (0, i, 0)), tile,
                   pl.BlockSpec((2, dm), lambda i: (0, 0)),
                   pl.BlockSpec((1, 3, dm), lambda i: (i // tiles_per_seq, 0, 0))],
        out_shape=[jax.ShapeDtypeStruct((tokens, dm), F32), jax.ShapeDtypeStruct((8, tokens, fc), BF16),
                   jax.ShapeDtypeStruct((tokens, dm), BF16), jax.ShapeDtypeStruct((4, tokens, fc), BF16),
                   jax.ShapeDtypeStruct((tokens, dm), BF16), jax.ShapeDtypeStruct((2, dm), F32),
                   jax.ShapeDtypeStruct((nseq, 3, dm), F32)],
        args=(dy, r, x, f, gu, sh, sc, gt, wgu, wd, ln_g))


def tn_matmul(a, b, name, job=None):
    na, tokens, kk = a.shape
    nb, _, cc = b.shape
    tt = tokens
    while 4 * tt * (kk + cc) + 8 * kk * cc > TN_VMEM_BUDGET and tt % 2 == 0 and tt > 256:
        tt //= 2
    steps = tokens // tt

    def body(a_ref, b_ref, o_ref, acc_ref):
        t = pl.program_id(2)

        @pl.when(t == 0)
        def _():
            acc_ref[...] = jnp.zeros_like(acc_ref)

        acc_ref[...] += _dot_tn(a_ref[0], b_ref[0])

        @pl.when(t == steps - 1)
        def _():
            o_ref[0, 0] = acc_ref[...].astype(BF16)

    return _call(
        body, job, name=name, grid=(na, nb, steps),
        in_specs=[pl.BlockSpec((1, tt, kk), lambda i, j, t: (i, t, 0)),
                  pl.BlockSpec((1, tt, cc), lambda i, j, t: (j, t, 0))],
        out_specs=[pl.BlockSpec((1, 1, kk, cc), lambda i, j, t: (i, j, 0, 0))],
        out_shape=[jax.ShapeDtypeStruct((na, nb, kk, cc), BF16)],
        scratch_shapes=[pltpu.VMEM((kk, cc), F32)], args=(a, b))


def proj_fwd(x1, sh, sc, w_in, seq, name, job=None):
    tokens, dm = x1.shape
    tm = min(TOKEN_TILE, seq)
    tiles_per_seq = seq // tm
    widths = [N_Q_HEADS * HEAD_DIM, N_KV_HEADS * HEAD_DIM, N_KV_HEADS * HEAD_DIM, 512, 512, 512]
    assert sum(widths) == w_in.shape[1]

    def body(x_ref, sh_ref, sc_ref, w_ref, *outs):
        h = (x_ref[...] * (1.0 + sc_ref[0]) + sh_ref[0]).astype(BF16)
        proj = _dot(h, w_ref[...])
        at = 0
        for o_ref, wdt in zip(outs, widths):
            o_ref[...] = proj[:, at:at + wdt]
            at += wdt

    tile = pl.BlockSpec((tm, dm), lambda i: (i, 0))
    mod = _mod_spec(tiles_per_seq, dm)
    return _call(
        body, job, name=name, grid=(tokens // tm,),
        in_specs=[tile, mod, mod, _const_spec(w_in.shape)],
        out_specs=[pl.BlockSpec((tm, wdt), lambda i: (i, 0)) for wdt in widths],
        out_shape=[jax.ShapeDtypeStruct((tokens, wdt), F32) for wdt in widths],
        args=(x1, sh, sc, w_in))


LANES = 2 * HEAD_DIM


def _head_lane(shape):
    return lax.broadcasted_iota(jnp.int32, shape, 1) % HEAD_DIM


def _lane_half(shape):
    return lax.broadcasted_iota(jnp.int32, shape, 1) // HEAD_DIM


def _swap_rot(v):
    lane = _head_lane(v.shape)
    half = ROT_DIM // 2
    return jnp.where(lane < half, pltpu.roll(v, LANES - half, 1),
                     jnp.where(lane < ROT_DIM, pltpu.roll(v, half, 1), 0.0))


def _rope(v, cos_t, sin_t):
    return v * cos_t + _swap_rot(v) * sin_t


def _unrope(dv, cos_t, sin_t):
    return dv * cos_t + _swap_rot(dv * sin_t)


def _both_halves(t, g):
    return jnp.where(_lane_half(t.shape) == g, t, pltpu.roll(t, HEAD_DIM, 1))


def _fold_halves(t, g):
    return jnp.where(_lane_half(t.shape) == g, t + pltpu.roll(t, HEAD_DIM, 1), 0.0)


def _stack_heads(blocks):
    rows = []
    for blk in blocks:
        half = _lane_half(blk.shape)
        rows += [jnp.where(half == 0, blk, 0.0), jnp.where(half == 1, blk, 0.0)]
    return jnp.concatenate(rows, axis=0)


def _unstack_heads(t, j):
    lo = t[(2 * j) * ATTN_BLOCK:(2 * j + 1) * ATTN_BLOCK]
    hi = t[(2 * j + 1) * ATTN_BLOCK:(2 * j + 2) * ATTN_BLOCK]
    return jnp.where(_lane_half(lo.shape) == 0, lo, hi)


def _band_mask(q0, w0):
    rows, cols = GQA_GROUP * ATTN_BLOCK, 2 * ATTN_BLOCK
    qi = lax.broadcasted_iota(jnp.int32, (rows, cols), 0) % ATTN_BLOCK + q0
    ki = lax.broadcasted_iota(jnp.int32, (rows, cols), 1) + w0
    diff = qi - ki
    return (diff >= 0) & (diff < ATTN_BLOCK)


def _attn_specs(seq):
    q_spec = pl.BlockSpec((seq, GQA_GROUP * HEAD_DIM), lambda b, g: (b, g))
    kv_spec = pl.BlockSpec((seq, LANES), lambda b, g: (b, 0))
    sink_spec = pl.BlockSpec((1, GQA_GROUP * ATTN_BLOCK, 1), lambda b, g: (g, 0, 0))
    return q_spec, kv_spec, sink_spec


def _block_starts(n):
    q0 = pl.multiple_of(n * ATTN_BLOCK, ATTN_BLOCK)
    w0 = pl.multiple_of(jnp.maximum(n - 1, 0) * ATTN_BLOCK, ATTN_BLOCK)
    return q0, w0


def _stacked_queries(ref, rows, cos_b=None, sin_b=None):
    blocks = []
    for j in range(2):
        blk = ref[rows, j * LANES:(j + 1) * LANES]
        blocks.append(blk if cos_b is None else _rope(blk, cos_b, sin_b))
    return _stack_heads(blocks).astype(BF16)


def _sink_columns(sinks):
    return jnp.repeat(sinks.reshape(N_KV_HEADS, GQA_GROUP), ATTN_BLOCK, axis=1)[:, :, None]


def attn_fwd(q, k, v, cos_t, sin_t, sinks, seq, name, job=None):
    tokens = q.shape[0]
    nblk = seq // ATTN_BLOCK
    assert nblk >= 2
    scale = HEAD_DIM ** -0.5

    def body(q_ref, k_ref, v_ref, cos_ref, sin_ref, sink_ref, o_ref, kd_ref, vd_ref):
        g = pl.program_id(1)
        kd_ref[...] = _both_halves(_rope(k_ref[...], cos_ref[...], sin_ref[...]), g).astype(BF16)
        vd_ref[...] = _both_halves(v_ref[...], g).astype(BF16)
        sink = sink_ref[0]

        def block(n, carry):
            q0, w0 = _block_starts(n)
            rows, win = pl.ds(q0, ATTN_BLOCK), pl.ds(w0, 2 * ATTN_BLOCK)
            qs = _stacked_queries(q_ref, rows, cos_ref[rows, :], sin_ref[rows, :])
            s = _dot_nt(qs, kd_ref[win, :]) * scale
            s = jnp.where(_band_mask(q0, w0), s, NEG_BIG)
            m = jnp.maximum(jnp.max(s, axis=-1, keepdims=True), sink)
            p = jnp.exp(s - m)
            denom = jnp.sum(p, axis=-1, keepdims=True) + jnp.exp(sink - m)
            out = _dot((p / denom).astype(BF16), vd_ref[win, :])
            for j in range(2):
                o_ref[rows, j * LANES:(j + 1) * LANES] = _unstack_heads(out, j)
            return carry

        lax.fori_loop(0, nblk, block, 0)

    q_spec, kv_spec, sink_spec = _attn_specs(seq)
    return _call(
        body, job, name=name, grid=(tokens // seq, N_KV_HEADS),
        in_specs=[q_spec, kv_spec, kv_spec, kv_spec, kv_spec, sink_spec],
        out_specs=[q_spec], out_shape=[jax.ShapeDtypeStruct(q.shape, F32)],
        scratch_shapes=[pltpu.VMEM((seq, LANES), BF16), pltpu.VMEM((seq, LANES), BF16)],
        args=(q, k, v, cos_t, sin_t, _sink_columns(sinks)))


def attn_bwd(q, k, v, do, cos_t, sin_t, sinks, seq, name, job=None):
    tokens = q.shape[0]
    nseq = tokens // seq
    nblk = seq // ATTN_BLOCK
    assert nblk >= 2
    rows_stacked = GQA_GROUP * ATTN_BLOCK
    scale = HEAD_DIM ** -0.5

    def body(q_ref, k_ref, v_ref, do_ref, cos_ref, sin_ref, sink_ref, dq_ref, dk_ref, dv_ref, ds_ref,
             kd_ref, vd_ref, dkd_ref, dvd_ref):
        g = pl.program_id(1)
        kd_ref[...] = _both_halves(_rope(k_ref[...], cos_ref[...], sin_ref[...]), g).astype(BF16)
        vd_ref[...] = _both_halves(v_ref[...], g).astype(BF16)
        dkd_ref[...] = jnp.zeros_like(dkd_ref)
        dvd_ref[...] = jnp.zeros_like(dvd_ref)
        sink = sink_ref[0]

        def block(n, dsink):
            q0, w0 = _block_starts(n)
            rows, win = pl.ds(q0, ATTN_BLOCK), pl.ds(w0, 2 * ATTN_BLOCK)
            cos_b, sin_b = cos_ref[rows, :], sin_ref[rows, :]
            qs = _stacked_queries(q_ref, rows, cos_b, sin_b)
            dos = _stacked_queries(do_ref, rows)
            kw, vw = kd_ref[win, :], vd_ref[win, :]
            s = _dot_nt(qs, kw) * scale
            s = jnp.where(_band_mask(q0, w0), s, NEG_BIG)
            m = jnp.maximum(jnp.max(s, axis=-1, keepdims=True), sink)
            p = jnp.exp(s - m)
            e_sink = jnp.exp(sink - m)
            inv = 1.0 / (jnp.sum(p, axis=-1, keepdims=True) + e_sink)
            pn = p * inv
            dvd_ref[win, :] += _dot_tn(pn.astype(BF16), dos)
            dp = _dot_nt(dos, vw)
            delta = jnp.sum(dp * pn, axis=-1, keepdims=True)
            ds = (pn * (dp - delta)).astype(BF16)
            dqs = _dot(ds, kw) * scale
            dkd_ref[win, :] += _dot_tn(ds, qs) * scale
            for j in range(2):
                dq_ref[rows, j * LANES:(j + 1) * LANES] = _unrope(_unstack_heads(dqs, j), cos_b, sin_b)
            return dsink - e_sink * inv * delta

        ds_ref[0, 0] = lax.fori_loop(0, nblk, block, jnp.zeros((rows_stacked, 1), F32))
        dk_g = _unrope(_fold_halves(dkd_ref[...], g), cos_ref[...], sin_ref[...])
        dv_g = _fold_halves(dvd_ref[...], g)

        @pl.when(g == 0)
        def _():
            dk_ref[...] = dk_g
            dv_ref[...] = dv_g

        @pl.when(g != 0)
        def _():
            dk_ref[...] += dk_g
            dv_ref[...] += dv_g

    q_spec, kv_spec, sink_spec = _attn_specs(seq)
    return _call(
        body, job, name=name, grid=(nseq, N_KV_HEADS),
        in_specs=[q_spec, kv_spec, kv_spec, q_spec, kv_spec, kv_spec, sink_spec],
        out_specs=[q_spec, kv_spec, kv_spec, pl.BlockSpec((1, 1, rows_stacked, 1), lambda b, g: (b, g, 0, 0))],
        out_shape=[jax.ShapeDtypeStruct(q.shape, F32), jax.ShapeDtypeStruct(k.shape, F32),
                   jax.ShapeDtypeStruct(k.shape, F32), jax.ShapeDtypeStruct((nseq, N_KV_HEADS, rows_stacked, 1), F32)],
        scratch_shapes=[pltpu.VMEM((seq, LANES), BF16), pltpu.VMEM((seq, LANES), BF16),
                        pltpu.VMEM((seq, LANES), F32), pltpu.VMEM((seq, LANES), F32)],
        args=(q, k, v, do, cos_t, sin_t, _sink_columns(sinks)))


CONV_COLS = 128


def _shift_down(z, by):
    t = lax.broadcasted_iota(jnp.int32, z.shape, 0)
    return jnp.where(t >= by, pltpu.roll(z, by, 0), 0.0)


def _shift_up(z, by):
    n = z.shape[0]
    t = lax.broadcasted_iota(jnp.int32, z.shape, 0)
    return jnp.where(t < n - by, pltpu.roll(z, n - by, 0), 0.0)


def conv_fwd(u, bg, cg, conv_w, seq, name):
    tokens, width = u.shape

    def body(u_ref, bg_ref, cg_ref, w_ref, o_ref):
        z = cg_ref[...] * u_ref[...]
        yy = w_ref[2:3, :] * z + w_ref[1:2, :] * _shift_down(z, 1) + w_ref[0:1, :] * _shift_down(z, 2)
        o_ref[...] = bg_ref[...] * yy

    col = pl.BlockSpec((seq, CONV_COLS), lambda j, b: (b, j))
    return pl.pallas_call(
        body, name=name, grid=(width // CONV_COLS, tokens // seq),
        in_specs=[col, col, col, pl.BlockSpec((CONV_TAPS, CONV_COLS), lambda j, b: (0, j))],
        out_specs=col, out_shape=jax.ShapeDtypeStruct((tokens, width), F32),
        compiler_params=_params(("parallel", "parallel")),
    )(u, bg, cg, conv_w)


def conv_bwd(dout, u, bg, cg, conv_w, seq, name):
    tokens, width = u.shape

    def body(do_ref, u_ref, bg_ref, cg_ref, w_ref, du_ref, dbg_ref, dcg_ref, dw_ref):
        uu, cg_v, do = u_ref[...], cg_ref[...], do_ref[...]
        z = cg_v * uu
        z1, z2 = _shift_down(z, 1), _shift_down(z, 2)
        yy = w_ref[2:3, :] * z + w_ref[1:2, :] * z1 + w_ref[0:1, :] * z2
        dbg_ref[...] = do * yy
        dyy = do * bg_ref[...]
        dz = w_ref[2:3, :] * dyy + w_ref[1:2, :] * _shift_up(dyy, 1) + w_ref[0:1, :] * _shift_up(dyy, 2)
        du_ref[...] = dz * cg_v
        dcg_ref[...] = dz * uu

        @pl.when(pl.program_id(1) == 0)
        def _():
            dw_ref[...] = jnp.zeros_like(dw_ref)

        dw_ref[0:1, :] += jnp.sum(dyy * z2, axis=0, keepdims=True)
        dw_ref[1:2, :] += jnp.sum(dyy * z1, axis=0, keepdims=True)
        dw_ref[2:3, :] += jnp.sum(dyy * z, axis=0, keepdims=True)

    col = pl.BlockSpec((seq, CONV_COLS), lambda j, b: (b, j))
    w_spec = pl.BlockSpec((CONV_TAPS, CONV_COLS), lambda j, b: (0, j))
    act = jax.ShapeDtypeStruct((tokens, width), F32)
    return pl.pallas_call(
        body, name=name, grid=(width // CONV_COLS, tokens // seq),
        in_specs=[col, col, col, col, w_spec], out_specs=[col, col, col, w_spec],
        out_shape=[act, act, act, jax.ShapeDtypeStruct((CONV_TAPS, width), F32)],
        compiler_params=_params(("parallel", "arbitrary")),
    )(dout, u, bg, cg, conv_w)


def out_fwd(x1, attn, conv, gt, w_out, ln_g, ln_b, seq, name, job=None):
    tokens, dm = x1.shape
    half = attn.shape[1]
    tm = min(TOKEN_TILE, seq)
    tiles_per_seq = seq // tm

    def body(x_ref, a_ref, c_ref, gt_ref, w_ref, lg_ref, lb_ref, xo_ref, r_ref, mi_ref, mix_ref):
        mixin = jnp.concatenate([a_ref[...], c_ref[...]], axis=1).astype(BF16)
        mi_ref[...] = mixin
        mix = _dot(mixin, w_ref[...])
        mix_ref[...] = mix.astype(BF16)
        r = DN_ALPHA * x_ref[...] + (1.0 + gt_ref[0]) * mix
        r_ref[...] = r
        xhat, _ = _ln_stats(r)
        xo_ref[...] = xhat * lg_ref[...] + lb_ref[...]

    tile = pl.BlockSpec((tm, dm), lambda i: (i, 0))
    htile = pl.BlockSpec((tm, half), lambda i: (i, 0))
    return _call(
        body, job, name=name, grid=(tokens // tm,),
        in_specs=[tile, htile, htile, _mod_spec(tiles_per_seq, dm), _const_spec(w_out.shape),
                  _const_spec((1, dm)), _const_spec((1, dm))],
        out_specs=[tile, tile, tile, tile],
        out_shape=[jax.ShapeDtypeStruct((tokens, dm), F32), jax.ShapeDtypeStruct((tokens, dm), F32),
                   jax.ShapeDtypeStruct((tokens, dm), BF16), jax.ShapeDtypeStruct((tokens, dm), BF16)],
        args=(x1, attn, conv, gt, w_out, ln_g, ln_b))


def out_bwd(dy, r, mix, gt, w_out, ln_g, seq, name, job=None):
    tokens, dm = r.shape
    half = dm // 2
    tm = min(TOKEN_TILE, seq)
    tiles_per_seq = seq // tm
    nseq = tokens // seq

    def body(dy_ref, r_ref, mix_ref, gt_ref, w_ref, lg_ref, dres_ref, da_ref, dc_ref, dmix_ref, dln_ref, dgt_ref):
        i = pl.program_id(0)
        dr, dgain, dbias = _ln_bwd(dy_ref[...], r_ref[...], lg_ref[...])

        @pl.when(i == 0)
        def _():
            dln_ref[...] = jnp.zeros_like(dln_ref)

        @pl.when(i % tiles_per_seq == 0)
        def _():
            dgt_ref[...] = jnp.zeros_like(dgt_ref)

        dln_ref[0:1, :] += dgain
        dln_ref[1:2, :] += dbias
        dgt_ref[0] += jnp.sum(dr * mix_ref[...].astype(F32), axis=0, keepdims=True)
        dres_ref[...] = DN_ALPHA * dr
        dmix = ((1.0 + gt_ref[0]) * dr).astype(BF16)
        dmix_ref[...] = dmix
        dmixin = _dot_nt(dmix, w_ref[...])
        da_ref[...] = dmixin[:, :half]
        dc_ref[...] = dmixin[:, half:]

    tile = pl.BlockSpec((tm, dm), lambda i: (i, 0))
    htile = pl.BlockSpec((tm, half), lambda i: (i, 0))
    return _call(
        body, job, name=name, grid=(tokens // tm,),
        in_specs=[tile, tile, tile, _mod_spec(tiles_per_seq, dm), _const_spec(w_out.shape), _const_spec((1, dm))],
        out_specs=[tile, htile, htile, tile, pl.BlockSpec((2, dm), lambda i: (0, 0)),
                   pl.BlockSpec((1, 1, dm), lambda i: (i // tiles_per_seq, 0, 0))],
        out_shape=[jax.ShapeDtypeStruct((tokens, dm), F32), jax.ShapeDtypeStruct((tokens, half), F32),
                   jax.ShapeDtypeStruct((tokens, half), F32), jax.ShapeDtypeStruct((tokens, dm), BF16),
                   jax.ShapeDtypeStruct((2, dm), F32), jax.ShapeDtypeStruct((nseq, 1, dm), F32)],
        args=(dy, r, mix, gt, w_out, ln_g))


def proj_bwd(parts, dres, x1, sh, sc, w_in, seq, name):
    tokens, dm = x1.shape
    tm = min(TOKEN_TILE, seq)
    tiles_per_seq = seq // tm
    nseq = tokens // seq
    widths = [p.shape[1] for p in parts]
    total = sum(widths)

    def body(*refs):
        part_refs = refs[:6]
        dres_ref, x_ref, sh_ref, sc_ref, w_ref, dx_ref, dproj_ref, h_ref, dmod_ref = refs[6:]
        dproj = jnp.concatenate([p[...] for p in part_refs], axis=1).astype(BF16)
        dproj_ref[...] = dproj
        dh = _dot_nt(dproj, w_ref[...])
        xx = x_ref[...]
        one_sc = 1.0 + sc_ref[0]
        h_ref[...] = (xx * one_sc + sh_ref[0]).astype(BF16)
        dx_ref[...] = dres_ref[...] + dh * one_sc

        @pl.when(pl.program_id(0) % tiles_per_seq == 0)
        def _():
            dmod_ref[...] = jnp.zeros_like(dmod_ref)

        dmod_ref[0, 0:1, :] += jnp.sum(dh, axis=0, keepdims=True)
        dmod_ref[0, 1:2, :] += jnp.sum(dh * xx, axis=0, keepdims=True)

    tile = pl.BlockSpec((tm, dm), lambda i: (i, 0))
    mod = _mod_spec(tiles_per_seq, dm)
    return pl.pallas_call(
        body, name=name, grid=(tokens // tm,),
        in_specs=[pl.BlockSpec((tm, wdt), lambda i: (i, 0)) for wdt in widths]
        + [tile, tile, mod, mod, _const_spec(w_in.shape)],
        out_specs=[tile, pl.BlockSpec((tm, total), lambda i: (i, 0)), tile,
                   pl.BlockSpec((1, 2, dm), lambda i: (i // tiles_per_seq, 0, 0))],
        out_shape=[jax.ShapeDtypeStruct((tokens, dm), F32), jax.ShapeDtypeStruct((tokens, total), BF16),
                   jax.ShapeDtypeStruct((tokens, dm), BF16), jax.ShapeDtypeStruct((nseq, 2, dm), F32)],
        compiler_params=_params(("arbitrary",)),
    )(*parts, dres, x1, sh, sc, w_in)


def _rope_tables(positions):
    inv_freq = jnp.power(jnp.float32(ROPE_THETA), -jnp.arange(0, ROT_DIM, 2, dtype=F32) / ROT_DIM)
    ang = positions.astype(F32)[..., None] * inv_freq
    cos, sin = jnp.cos(ang), jnp.sin(ang)
    rest = positions.shape + (HEAD_DIM - ROT_DIM,)
    cos_t = jnp.concatenate([cos, cos, jnp.ones(rest, F32)], axis=-1)
    sin_t = jnp.concatenate([-sin, sin, jnp.zeros(rest, F32)], axis=-1)
    return jnp.concatenate([cos_t, cos_t], axis=-1), jnp.concatenate([sin_t, sin_t], axis=-1)


def kernel(x, c, positions, w_ada, b_ada, ffn1_w_gate_up, ffn1_w_down, ln1_g, ln1_b, w_in, conv_w, attn_sinks, w_out, ln2_g, ln2_b, ffn2_w_gate_up, ffn2_w_down, ln3_g, ln3_b, loss_target, m_w_ada, m_b_ada, m_ffn1_w_gate_up, m_ffn1_w_down, m_ln1_g, m_ln1_b, m_w_in, m_conv_w, m_attn_sinks, m_w_out, m_ln2_g, m_ln2_b, m_ffn2_w_gate_up, m_ffn2_w_down, m_ln3_g, m_ln3_b, v_w_ada, v_b_ada, v_ffn1_w_gate_up, v_ffn1_w_down, v_ln1_g, v_ln1_b, v_w_in, v_conv_w, v_attn_sinks, v_w_out, v_ln2_g, v_ln2_b, v_ffn2_w_gate_up, v_ffn2_w_down, v_ln3_g, v_ln3_b):
    nseq, seq, dm = x.shape
    tokens = nseq * seq
    dev = 4 * lax.axis_index("x") + 2 * lax.axis_index("y") + lax.axis_index("c")
    ada_cols = w_ada.shape[2]
    ff = ffn1_w_down.shape[1] * N_DEV
    fc = ff // 4
    in_cols = w_in.shape[2]
    conv_cols = conv_w.shape[2]

    c_all, convw_all = all_gather([c, conv_w[0]], "gather_cond")
    wgu1, wd1 = all_gather([ffn1_w_gate_up[0].astype(BF16), ffn1_w_down[0].astype(BF16)], "gather_ffn1")
    c_all = c_all.reshape(N_DEV * nseq, dm)
    convw_full = convw_all.transpose(1, 0, 2).reshape(CONV_TAPS, N_DEV * conv_cols)
    wd1 = wd1.reshape(4, fc, dm)

    b_cols = lax.dynamic_slice(b_ada, (0, dev * ada_cols), (1, ada_cols))
    cond_all, mod_cols = ada_fwd(c_all, w_ada[0], b_cols, "ada_fwd")
    (mod_all,) = all_gather([mod_cols], "gather_mod")
    mod = lax.dynamic_slice(mod_all, (0, dev * nseq, 0), (N_DEV, nseq, ada_cols))
    mod = mod.transpose(1, 0, 2).reshape(nseq, 9, 1, dm)
    sh1, sc1, g1, sh2, sc2, g2, sh3, sc3, g3 = [mod[:, i] for i in range(9)]

    x0 = x.reshape(tokens, dm)
    spread = gather_spread_job([w_in[0].astype(BF16), w_out[0].astype(BF16), ffn2_w_down[0].astype(BF16)])
    (x1, r1, gu1, f1), spread = ffn_fwd(x0, sh1, sc1, g1, wgu1, wd1, ln1_g, ln1_b, seq, "ffn1_fwd", job=spread)
    win, wout = run_job(gather_forward_job(spread[:2]), "gather_mix_forward")
    win = win.transpose(1, 0, 2).reshape(dm, N_DEV * in_cols)
    wout = wout.reshape(dm, dm)
    (q, k, v, u, bg, cg), (wd2,) = proj_fwd(x1, sh2, sc2, win, seq, "proj_fwd", job=gather_forward_job(spread[2:]))
    wd2 = wd2.reshape(4, fc, dm)
    cos_t, sin_t = _rope_tables(positions.reshape(tokens))
    sinks = attn_sinks[0]
    (attn,), spread = attn_fwd(q, k, v, cos_t, sin_t, sinks, seq, "attn_fwd",
                               job=gather_spread_job([ffn2_w_gate_up[0].astype(BF16)]))
    conv = conv_fwd(u, bg, cg, convw_full, seq, "conv_fwd")
    (x2, r2, mixin, mix), (wgu2,) = out_fwd(x1, attn, conv, g2, wout, ln2_g, ln2_b, seq, "out_fwd",
                                            job=gather_forward_job(spread))
    target = loss_target.reshape(tokens, dm)
    (dy3, loss_part, r3, gu3, f3), _ = ffn_fwd(x2, sh3, sc3, g3, wgu2, wd2, ln3_g, ln3_b, seq, "ffn2_fwd", target=target)

    (dx2, dgu3, df3, a3, h3, dln3, dmod3), _ = ffn_bwd(dy3, r3, x2, f3, gu3, sh3, sc3, g3, wgu2, wd2, ln3_g, seq, "ffn2_bwd")
    g_wd2 = tn_matmul(a3, df3[None], "ffn2_dwd")[0][0].reshape(N_DEV, ff // N_DEV, dm)
    g_wgu2 = tn_matmul(h3[None], dgu3, "ffn2_dwgu")[0][0].reshape(N_DEV, dm, fc)
    (dres2, dattn, dconv, dmix, dln2, dg2), swapped = out_bwd(dx2, r2, mix, g2, wout, ln2_g, seq, "out_bwd",
                                                              job=swap_job([g_wgu2, g_wd2]))
    p_wgu2, own_wgu2 = pair_sum(g_wgu2, swapped[0], "pair_wgu2")
    p_wd2, own_wd2 = pair_sum(g_wd2, swapped[1], "pair_wd2")
    g_wout = tn_matmul(mixin[None], dmix[None], "dwout")[0][0].reshape(N_DEV, dm // N_DEV, dm)
    du, dbg, dcg, dconvw = conv_bwd(dconv, u, bg, cg, convw_full, seq, "conv_bwd")
    (dq, dk, dv, dsink_rows), (far_wgu2, far_wd2) = attn_bwd(
        q, k, v, dattn, cos_t, sin_t, sinks, seq, "attn_bwd", job=chip_exchange_job([p_wgu2, p_wd2]))
    parts = [dq, dk, dv, du, dbg, dcg]
    dx1, dproj, h2, dmod2 = proj_bwd(parts, dres2, x1, sh2, sc2, win, seq, "proj_bwd")
    g_win = tn_matmul(h2[None], dproj[None], "dwin")[0][0].reshape(dm, N_DEV, in_cols).transpose(1, 0, 2)
    (dx0, dgu1, df1, a1, h1, dln1, dmod1), swapped = ffn_bwd(
        dx1, r1, x0, f1, gu1, sh1, sc1, g1, wgu1, wd1, ln1_g, seq, "ffn1_bwd", job=swap_job([g_wout, g_win]))
    p_wout, own_wout = pair_sum(g_wout, swapped[0], "pair_wout")
    p_win, own_win = pair_sum(g_win, swapped[1], "pair_win")
    (g_wd1,), (far_wout, far_win) = tn_matmul(a1, df1[None], "ffn1_dwd", job=chip_exchange_job([p_wout, p_win]))
    g_wd1 = g_wd1.reshape(N_DEV, ff // N_DEV, dm)
    g_wgu1 = tn_matmul(h1[None], dgu1, "ffn1_dwgu")[0][0].reshape(N_DEV, dm, fc)

    grads = {
        "ffn1_w_gate_up": reduce_scatter(g_wgu1, "rs_wgu1"), "ffn1_w_down": reduce_scatter(g_wd1, "rs_wd1"),
        "w_in": own_win, "w_out": own_wout, "ffn2_w_gate_up": own_wgu2, "ffn2_w_down": own_wd2,
    }
    others = {"w_in": far_win, "w_out": far_wout, "ffn2_w_gate_up": far_wgu2, "ffn2_w_down": far_wd2}

    dmod = jnp.concatenate([dmod1, dmod2, dg2, dmod3], axis=1).reshape(nseq, 9 * dm)
    (dmod_all,) = all_gather([dmod], "gather_dmod")
    dmod_cols = lax.dynamic_slice(dmod_all.reshape(N_DEV * nseq, 9 * dm), (0, dev * ada_cols), (N_DEV * nseq, ada_cols))
    grads["w_ada"], gb_cols = ada_bwd(cond_all, dmod_cols, "ada_bwd")

    dsinks = jnp.sum(dsink_rows.reshape(nseq, N_Q_HEADS, ATTN_BLOCK), axis=(0, 2))
    small = jnp.zeros((8, dm), F32)
    small = small.at[0:2].set(dln1).at[2:4].set(dln2).at[4:6].set(dln3)
    small = small.at[6, 0:N_Q_HEADS].set(dsinks).at[7, 0].set(loss_part[0, 0])
    small_all, dconvw_all, gb_all = all_gather([small, dconvw, gb_cols], "gather_small")
    small_sum = sum_devices(small_all, "sum_small")
    dconvw_sum = sum_devices(dconvw_all, "sum_convw")
    loss = small_sum[7, 0]
    grads["b_ada"] = gb_all.reshape(1, N_DEV * ada_cols)
    grads["conv_w"] = lax.dynamic_slice(dconvw_sum, (0, dev * conv_cols), (CONV_TAPS, conv_cols))
    grads["attn_sinks"] = small_sum[6:7, 0:N_Q_HEADS]
    for i, nm in enumerate(["ln1_g", "ln1_b", "ln2_g", "ln2_b", "ln3_g", "ln3_b"]):
        grads[nm] = small_sum[i:i + 1]

    given = dict(w_ada=(w_ada, m_w_ada, v_w_ada), b_ada=(b_ada, m_b_ada, v_b_ada),
                 ffn1_w_gate_up=(ffn1_w_gate_up, m_ffn1_w_gate_up, v_ffn1_w_gate_up),
                 ffn1_w_down=(ffn1_w_down, m_ffn1_w_down, v_ffn1_w_down),
                 ln1_g=(ln1_g, m_ln1_g, v_ln1_g), ln1_b=(ln1_b, m_ln1_b, v_ln1_b),
                 w_in=(w_in, m_w_in, v_w_in), conv_w=(conv_w, m_conv_w, v_conv_w),
                 attn_sinks=(attn_sinks, m_attn_sinks, v_attn_sinks), w_out=(w_out, m_w_out, v_w_out),
                 ln2_g=(ln2_g, m_ln2_g, v_ln2_g), ln2_b=(ln2_b, m_ln2_b, v_ln2_b),
                 ffn2_w_gate_up=(ffn2_w_gate_up, m_ffn2_w_gate_up, v_ffn2_w_gate_up),
                 ffn2_w_down=(ffn2_w_down, m_ffn2_w_down, v_ffn2_w_down),
                 ln3_g=(ln3_g, m_ln3_g, v_ln3_g), ln3_b=(ln3_b, m_ln3_b, v_ln3_b))
    order = ["w_ada", "b_ada", "ffn1_w_gate_up", "ffn1_w_down", "ln1_g", "ln1_b", "w_in", "conv_w", "attn_sinks",
             "w_out", "ln2_g", "ln2_b", "ffn2_w_gate_up", "ffn2_w_down", "ln3_g", "ln3_b"]
    out_g, out_d, out_m, out_v = [], [], [], []
    for nm in order:
        w, m, v = given[nm]
        shape = w.shape
        two_d = (shape[-2], shape[-1])
        g2d, d, nm_new, nv_new = adamw(w.reshape(two_d), grads[nm].reshape(two_d), m.reshape(two_d), v.reshape(two_d),
                                       "adamw_" + nm, others=others.get(nm))
        out_g.append(g2d.reshape(shape))
        out_d.append(d.reshape(shape))
        out_m.append(nm_new.reshape(shape))
        out_v.append(nv_new.reshape(shape))
    grad_x = dx0.reshape(nseq, seq, dm)
    return (loss, grad_x, *out_g, *out_d, *out_m, *out_v)
```

```python
import functools

import jax
import jax.numpy as jnp
from jax import lax
from jax.experimental import pallas as pl
from jax.experimental.pallas import tpu as pltpu

F32 = jnp.float32
BF16 = jnp.bfloat16
MESH = pl.DeviceIdType.MESH

N_DEV = 8
N_CHIP = 4
HEAD_DIM = 64
N_Q_HEADS = 8
N_KV_HEADS = 2
GQA_GROUP = N_Q_HEADS // N_KV_HEADS
ATTN_BLOCK = 128
ROT_DIM = 16
ROPE_THETA = 500000.0
CONV_TAPS = 3
LN_EPS = 1e-5
DN_ALPHA = 2.0 ** 0.25
ADAM_LR = 0.001
ADAM_B1 = 0.9
ADAM_B2 = 0.999
ADAM_EPS = 1e-08
ADAM_WD = 0.01
ADAM_STEP = 10
NEG_BIG = -1e30

VMEM_LIMIT = 56 * 1024 * 1024
TOKEN_TILE = 256
FFN_FWD_TILE = 512
TN_VMEM_BUDGET = 36 * 1024 * 1024


def _params(semantics=None, vmem=VMEM_LIMIT):
    return pltpu.CompilerParams(dimension_semantics=semantics, vmem_limit_bytes=vmem)


def _dot(a, b):
    return jnp.dot(a, b, preferred_element_type=F32)


def _dot_nt(a, b):
    return lax.dot_general(a, b, (((1,), (1,)), ((), ())), preferred_element_type=F32)


def _dot_tn(a, b):
    return lax.dot_general(a, b, (((0,), (0,)), ((), ())), preferred_element_type=F32)


def _sigmoid(x):
    return 1.0 / (1.0 + jnp.exp(-x))


def _ln_stats(r):
    mu = jnp.mean(r, axis=-1, keepdims=True)
    d = r - mu
    var = jnp.mean(d * d, axis=-1, keepdims=True)
    rstd = lax.rsqrt(var + LN_EPS)
    return d * rstd, rstd


def _ln_bwd(dy, r, g):
    xhat, rstd = _ln_stats(r)
    dxhat = dy * g
    c1 = jnp.mean(dxhat, axis=-1, keepdims=True)
    c2 = jnp.mean(dxhat * xhat, axis=-1, keepdims=True)
    dr = rstd * (dxhat - c1 - xhat * c2)
    return dr, jnp.sum(dy * xhat, axis=0, keepdims=True), jnp.sum(dy, axis=0, keepdims=True)


def _const_spec(shape):
    nd = len(shape)
    return pl.BlockSpec(shape, lambda *_: (0,) * nd, pipeline_mode=pl.Buffered(1))


def all_gather(arrs, name):
    n = len(arrs)

    def body(*refs):
        ins, outs = refs[:n], refs[n:2 * n]
        send_sems, recv_sems, local_sems = refs[2 * n:]
        x, y, c = lax.axis_index("x"), lax.axis_index("y"), lax.axis_index("c")
        me, sibling = (x, y, c), (x, y, 1 - c)
        chips = [(1 - x, y), (x, 1 - y), (1 - x, 1 - y)]

        def slot(i, p):
            return outs[i].at[4 * p[0] + 2 * p[1] + p[2]]

        def copy(i, k, block, to, src=None):
            return pltpu.make_async_remote_copy(
                src_ref=slot(i, block) if src is None else src, dst_ref=slot(i, block),
                send_sem=send_sems.at[i, k], recv_sem=recv_sems.at[i, k],
                device_id=to, device_id_type=MESH)

        mine = [pltpu.make_async_copy(ins[i], slot(i, me), local_sems.at[i]) for i in range(n)]
        for cp in mine:
            cp.start()
        first = []
        for i in range(n):
            first.append(copy(i, 0, me, sibling, src=ins[i]))
            first += [copy(i, 1 + j, me, (*chip, c), src=ins[i]) for j, chip in enumerate(chips)]
        for cp in first:
            cp.start()
        passed = []
        for j, chip in enumerate(chips):
            for i in range(n):
                copy(i, 1 + j, (*chip, c), me).wait_recv()
                cp = copy(i, 4 + j, (*chip, c), sibling)
                cp.start()
                passed.append(cp)
        for i in range(n):
            copy(i, 0, sibling, me).wait_recv()
            for j, chip in enumerate(chips):
                copy(i, 4 + j, (*chip, 1 - c), me).wait_recv()
        for cp in first + passed:
            cp.wait_send()
        for cp in mine:
            cp.wait()

    any_spec = pl.BlockSpec(memory_space=pl.ANY)
    return pl.pallas_call(
        body, name=name,
        out_shape=[jax.ShapeDtypeStruct((N_DEV, *a.shape), a.dtype) for a in arrs],
        in_specs=[any_spec] * n, out_specs=[any_spec] * n,
        scratch_shapes=[pltpu.SemaphoreType.DMA((n, 7)), pltpu.SemaphoreType.DMA((n, 7)),
                        pltpu.SemaphoreType.DMA((n,))],
    )(*arrs)


RS_ROWS = 32


def reduce_scatter(g, name):
    _, rows, cols = g.shape
    nblk = rows // RS_ROWS
    assert nblk * RS_ROWS == rows

    def body(g_ref, out_ref, r1_ref, p_ref, r2_ref, send_sems, recv_sems):
        x, y, c = lax.axis_index("x"), lax.axis_index("y"), lax.axis_index("c")
        sibling = (x, y, 1 - c)
        q_me = 2 * x + y
        swaps = []
        for q in range(N_CHIP):
            cp = pltpu.make_async_remote_copy(
                src_ref=g_ref.at[2 * q + (1 - c)], dst_ref=r1_ref.at[q],
                send_sem=send_sems.at[q], recv_sem=recv_sems.at[q], device_id=sibling, device_id_type=MESH)
            cp.start()
            swaps.append(cp)
        for cp in swaps:
            cp.wait_recv()

        def pair_sum(i, carry):
            r = pl.ds(pl.multiple_of(i * RS_ROWS, RS_ROWS), RS_ROWS)
            for q in range(N_CHIP):
                p_ref[q, r, :] = (g_ref[2 * q + c, r, :].astype(F32) + r1_ref[q, r, :].astype(F32)).astype(BF16)
            return carry

        lax.fori_loop(0, nblk, pair_sum, 0)
        chips = [(1 - x, y), (x, 1 - y), (1 - x, 1 - y)]
        sends = []
        for k, chip in enumerate(chips):
            cp = pltpu.make_async_remote_copy(
                src_ref=p_ref.at[2 * chip[0] + chip[1]], dst_ref=r2_ref.at[k],
                send_sem=send_sems.at[N_CHIP + k], recv_sem=recv_sems.at[N_CHIP + k],
                device_id=(*chip, c), device_id_type=MESH)
            cp.start()
            sends.append(cp)
        for cp in sends:
            cp.wait_recv()

        def total(i, carry):
            r = pl.ds(pl.multiple_of(i * RS_ROWS, RS_ROWS), RS_ROWS)
            acc = g_ref[2 * q_me + c, r, :].astype(F32) + r1_ref[q_me, r, :].astype(F32)
            for k in range(3):
                acc = acc + r2_ref[k, r, :].astype(F32)
            out_ref[r, :] = acc
            return carry

        lax.fori_loop(0, nblk, total, 0)
        for cp in swaps + sends:
            cp.wait_send()

    vmem = pl.BlockSpec(memory_space=pltpu.VMEM)
    return pl.pallas_call(
        body, name=name,
        out_shape=jax.ShapeDtypeStruct((rows, cols), F32),
        in_specs=[vmem], out_specs=vmem,
        scratch_shapes=[pltpu.VMEM((N_CHIP, rows, cols), BF16), pltpu.VMEM((N_CHIP, rows, cols), BF16),
                        pltpu.VMEM((3, rows, cols), BF16),
                        pltpu.SemaphoreType.DMA((N_CHIP + 3,)), pltpu.SemaphoreType.DMA((N_CHIP + 3,))],
        compiler_params=_params(),
    )(g)


def _place():
    x, y, c = lax.axis_index("x"), lax.axis_index("y"), lax.axis_index("c")
    return x, y, c, [(1 - x, y), (x, 1 - y), (1 - x, 1 - y)]


def _slot(p):
    return 4 * p[0] + 2 * p[1] + p[2]


class _Job:
    def __init__(self, ins, outs, nsem, copies, aliases=None, local=None):
        self.ins, self.outs, self.nsem, self.copies = list(ins), list(outs), nsem, copies
        self.aliases = aliases or {}
        self.local = local

    def scratch(self):
        s = [pltpu.SemaphoreType.DMA(self.nsem), pltpu.SemaphoreType.DMA(self.nsem)]
        if self.local is not None:
            s.append(pltpu.SemaphoreType.DMA((len(self.ins),)))
        return s

    def start(self, ins, outs, sems):
        if self.local is not None:
            for cp in self.local(ins, outs, sems[2]):
                cp.start()
        for cp in self.copies(ins, outs, sems[0], sems[1])[0]:
            cp.start()

    def finish(self, ins, outs, sems):
        started, awaited = self.copies(ins, outs, sems[0], sems[1])
        for cp in awaited:
            cp.wait_recv()
        for cp in started:
            cp.wait_send()
        if self.local is not None:
            for cp in self.local(ins, outs, sems[2]):
                cp.wait()


def _remote(src, dst, send, recv, idx, to):
    return pltpu.make_async_remote_copy(src_ref=src, dst_ref=dst, send_sem=send.at[idx], recv_sem=recv.at[idx],
                                        device_id=to, device_id_type=MESH)


def gather_spread_job(shards):
    def copies(ins, outs, send, recv):
        x, y, c, chips = _place()
        me = (x, y, c)
        peers = [(x, y, 1 - c)] + [(*chip, c) for chip in chips]
        started, awaited = [], []
        for i, (src, dst) in enumerate(zip(ins, outs)):
            for k, peer in enumerate(peers):
                started.append(_remote(src, dst.at[_slot(me)], send, recv, (i, k), peer))
                awaited.append(_remote(src, dst.at[_slot(peer)], send, recv, (i, k), peer))
        return started, awaited

    def local(ins, outs, sems):
        x, y, c, _ = _place()
        return [pltpu.make_async_copy(src, dst.at[_slot((x, y, c))], sems.at[i])
                for i, (src, dst) in enumerate(zip(ins, outs))]

    outs = [jax.ShapeDtypeStruct((N_DEV, *a.shape), a.dtype) for a in shards]
    return _Job(shards, outs, (len(shards), 4), copies, local=local)


def gather_forward_job(fulls):
    def copies(ins, outs, send, recv):
        x, y, c, chips = _place()
        started, awaited = [], []
        for i, buf in enumerate(outs):
            for j, chip in enumerate(chips):
                mine, theirs = buf.at[_slot((*chip, c))], buf.at[_slot((*chip, 1 - c))]
                started.append(_remote(mine, mine, send, recv, (i, j), (x, y, 1 - c)))
                awaited.append(_remote(theirs, theirs, send, recv, (i, j), (x, y, 1 - c)))
        return started, awaited

    outs = [jax.ShapeDtypeStruct(a.shape, a.dtype) for a in fulls]
    return _Job(fulls, outs, (len(fulls), 3), copies, aliases={i: i for i in range(len(fulls))})


def swap_job(gs):
    def copies(ins, outs, send, recv):
        x, y, c, _ = _place()
        started, awaited = [], []
        for i, (g, r1) in enumerate(zip(ins, outs)):
            for q in range(N_CHIP):
                started.append(_remote(g.at[2 * q + (1 - c)], r1.at[q], send, recv, (i, q), (x, y, 1 - c)))
                awaited.append(_remote(g.at[2 * q + c], r1.at[q], send, recv, (i, q), (x, y, 1 - c)))
        return started, awaited

    outs = [jax.ShapeDtypeStruct((N_CHIP, *g.shape[1:]), g.dtype) for g in gs]
    return _Job(gs, outs, (len(gs), N_CHIP), copies)


def chip_exchange_job(ps):
    def copies(ins, outs, send, recv):
        x, y, c, chips = _place()
        started, awaited = [], []
        for i, (p, r2) in enumerate(zip(ins, outs)):
            for k, chip in enumerate(chips):
                started.append(_remote(p.at[2 * chip[0] + chip[1]], r2.at[k], send, recv, (i, k), (*chip, c)))
                awaited.append(_remote(p.at[2 * x + y], r2.at[k], send, recv, (i, k), (*chip, c)))
        return started, awaited

    outs = [jax.ShapeDtypeStruct((3, *p.shape[1:]), p.dtype) for p in ps]
    return _Job(ps, outs, (len(ps), 3), copies)


def _call(body, job, *, name, grid, in_specs, out_specs, out_shape, args, scratch_shapes=(), vmem=VMEM_LIMIT):
    if job is None:
        res = pl.pallas_call(
            body, name=name, grid=grid, in_specs=in_specs, out_specs=out_specs, out_shape=out_shape,
            scratch_shapes=list(scratch_shapes), compiler_params=_params(("arbitrary",) * len(grid), vmem),
        )(*args)
        return res, []
    n_in, n_out, n_scr = len(in_specs), len(out_specs), len(scratch_shapes)
    j_in, j_out = len(job.ins), len(job.outs)

    def with_copies(*refs):
        at = 0
        ins = refs[at:at + n_in]; at += n_in
        jins = refs[at:at + j_in]; at += j_in
        outs = refs[at:at + n_out]; at += n_out
        jouts = refs[at:at + j_out]; at += j_out
        scr = refs[at:at + n_scr]; at += n_scr
        sems = refs[at:]
        ids = [pl.program_id(d) for d in range(len(grid))]
        first = functools.reduce(jnp.logical_and, [i == 0 for i in ids])
        last = functools.reduce(jnp.logical_and, [i == n - 1 for i, n in zip(ids, grid)])

        @pl.when(first)
        def _():
            job.start(jins, jouts, sems)

        body(*ins, *outs, *scr)

        @pl.when(last)
        def _():
            job.finish(jins, jouts, sems)

    any_spec = pl.BlockSpec(memory_space=pl.ANY)
    res = pl.pallas_call(
        with_copies, name=name, grid=grid,
        in_specs=list(in_specs) + [any_spec] * j_in, out_specs=list(out_specs) + [any_spec] * j_out,
        out_shape=list(out_shape) + list(job.outs),
        input_output_aliases={n_in + i: n_out + o for i, o in job.aliases.items()},
        scratch_shapes=list(scratch_shapes) + job.scratch(),
        compiler_params=_params(("arbitrary",) * len(grid), vmem),
    )(*args, *job.ins)
    return res[:n_out], res[n_out:]


def run_job(job, name):
    def body(*refs):
        j_in, j_out = len(job.ins), len(job.outs)
        ins, outs, sems = refs[:j_in], refs[j_in:j_in + j_out], refs[j_in + j_out:]
        job.start(ins, outs, sems)
        job.finish(ins, outs, sems)

    any_spec = pl.BlockSpec(memory_space=pl.ANY)
    return pl.pallas_call(
        body, name=name, in_specs=[any_spec] * len(job.ins), out_specs=[any_spec] * len(job.outs),
        out_shape=list(job.outs), input_output_aliases=dict(job.aliases), scratch_shapes=job.scratch(),
    )(*job.ins)


def pair_sum(g, r1, name):
    _, rows, cols = g.shape
    rb = next(cand for cand in range(min(rows, 512), 0, -16) if rows % cand == 0)

    def body(g_ref, r1_ref, p_ref, own_ref):
        x, y, c, _ = _place()
        s = g_ref[c].astype(F32) + r1_ref[0].astype(F32)
        p_ref[0] = s.astype(BF16)

        @pl.when(pl.program_id(1) == 2 * x + y)
        def _():
            own_ref[...] = s

    return pl.pallas_call(
        body, name=name, grid=(rows // rb, N_CHIP),
        in_specs=[pl.BlockSpec((2, rb, cols), lambda i, q: (q, i, 0)), pl.BlockSpec((1, rb, cols), lambda i, q: (q, i, 0))],
        out_specs=[pl.BlockSpec((1, rb, cols), lambda i, q: (q, i, 0)), pl.BlockSpec((rb, cols), lambda i, q: (i, 0))],
        out_shape=[jax.ShapeDtypeStruct((N_CHIP, rows, cols), BF16), jax.ShapeDtypeStruct((rows, cols), F32)],
        compiler_params=_params(("arbitrary", "arbitrary")),
    )(g, r1)


def sum_devices(a, name):
    def body(a_ref, o_ref):
        acc = a_ref[0]
        for d in range(1, N_DEV):
            acc = acc + a_ref[d]
        o_ref[...] = acc

    return pl.pallas_call(body, name=name, out_shape=jax.ShapeDtypeStruct(a.shape[1:], F32))(a)


def adamw(w, g, m, v, name, others=None):
    rows, cols = w.shape
    rb = rows
    for cand in range(min(rows, 512), 7, -8):
        if rows % cand == 0 and cand % 8 == 0:
            rb = cand
            break

    def body(*refs):
        if others is None:
            w_ref, g_ref, m_ref, v_ref, d_ref, nm_ref, nv_ref = refs
            gg = g_ref[...]
        else:
            w_ref, g_ref, m_ref, v_ref, r2_ref, go_ref, d_ref, nm_ref, nv_ref = refs
            gg = g_ref[...]
            for k in range(3):
                gg = gg + r2_ref[k].astype(F32)
            go_ref[...] = gg
        nm = ADAM_B1 * m_ref[...] + (1.0 - ADAM_B1) * gg
        nv = ADAM_B2 * v_ref[...] + (1.0 - ADAM_B2) * (gg * gg)
        m_hat = nm / (1.0 - ADAM_B1 ** ADAM_STEP)
        v_hat = nv / (1.0 - ADAM_B2 ** ADAM_STEP)
        d_ref[...] = -ADAM_LR * (m_hat / (jnp.sqrt(v_hat) + ADAM_EPS) + ADAM_WD * w_ref[...])
        nm_ref[...] = nm
        nv_ref[...] = nv

    spec = pl.BlockSpec((rb, cols), lambda i: (i, 0))
    out = jax.ShapeDtypeStruct((rows, cols), F32)
    in_specs, args = [spec] * 4, [w, g, m, v]
    if others is not None:
        in_specs.append(pl.BlockSpec((3, rb, cols), lambda i: (0, i, 0)))
        args.append(others)
    n_out = 3 if others is None else 4
    res = pl.pallas_call(
        body, name=name, grid=(rows // rb,), in_specs=in_specs, out_specs=[spec] * n_out,
        out_shape=[out] * n_out, compiler_params=_params(("parallel",)),
    )(*args)
    return (g, *res) if others is None else tuple(res)


def ada_fwd(c_all, w_cols, b_cols, name):
    def body(c_ref, w_ref, b_ref, cond_ref, mod_ref):
        cc = c_ref[...]
        cond = (cc * _sigmoid(cc)).astype(BF16)
        cond_ref[...] = cond
        mod_ref[...] = _dot(cond, w_ref[...].astype(BF16)) + b_ref[...]

    n, cols = c_all.shape[0], w_cols.shape[1]
    return pl.pallas_call(
        body, name=name,
        out_shape=[jax.ShapeDtypeStruct(c_all.shape, BF16), jax.ShapeDtypeStruct((n, cols), F32)],
        compiler_params=_params(),
    )(c_all, w_cols, b_cols)


def ada_bwd(cond_all, dmod_cols, name):
    def body(c_ref, d_ref, gw_ref, gb_ref):
        d = d_ref[...]
        gw_ref[...] = _dot_tn(c_ref[...], d.astype(BF16))
        gb_ref[...] = jnp.sum(d, axis=0, keepdims=True)

    dm, cols = cond_all.shape[1], dmod_cols.shape[1]
    return pl.pallas_call(
        body, name=name,
        out_shape=[jax.ShapeDtypeStruct((dm, cols), F32), jax.ShapeDtypeStruct((1, cols), F32)],
        compiler_params=_params(),
    )(cond_all, dmod_cols)


def _mod_spec(tiles_per_seq, dm):
    return pl.BlockSpec((1, 1, dm), lambda i: (i // tiles_per_seq, 0, 0))


def ffn_fwd(x, sh, sc, gt, wgu, wd, ln_g, ln_b, seq, name, target=None, job=None):
    tokens, dm = x.shape
    fc = wgu.shape[1]
    tm = min(FFN_FWD_TILE, seq)
    tiles_per_seq = seq // tm
    with_loss = target is not None

    def body(*refs):
        if with_loss:
            (x_ref, sh_ref, sc_ref, gt_ref, wgu_ref, wd_ref, lg_ref, lb_ref, t_ref,
             xo_ref, loss_ref, r_ref, gu_ref, f_ref) = refs
        else:
            (x_ref, sh_ref, sc_ref, gt_ref, wgu_ref, wd_ref, lg_ref, lb_ref,
             xo_ref, r_ref, gu_ref, f_ref) = refs
        xx = x_ref[...]
        h = (xx * (1.0 + sc_ref[0]) + sh_ref[0]).astype(BF16)
        acc = jnp.zeros((tm, dm), F32)
        for k in range(4):
            gk = _dot_nt(h, wgu_ref[k])
            uk = _dot_nt(h, wgu_ref[k + 4])
            gu_ref[k] = gk.astype(BF16)
            gu_ref[k + 4] = uk.astype(BF16)
            a = (gk * _sigmoid(gk) * uk).astype(BF16)
            acc = acc + _dot(a, wd_ref[k])
        f_ref[...] = acc.astype(BF16)
        r = DN_ALPHA * xx + (0.5 * (1.0 + gt_ref[0])) * acc
        r_ref[...] = r
        xhat, _ = _ln_stats(r)
        yy = xhat * lg_ref[...] + lb_ref[...]
        if with_loss:
            err = yy - t_ref[...]
            xo_ref[...] = err * (1.0 / dm)

            @pl.when(pl.program_id(0) == 0)
            def _():
                loss_ref[...] = jnp.zeros_like(loss_ref)

            loss_ref[...] += jnp.full((1, 128), (0.5 / dm) * jnp.sum(err * err), F32)
        else:
            xo_ref[...] = yy

    tile = pl.BlockSpec((tm, dm), lambda i: (i, 0))
    mod = _mod_spec(tiles_per_seq, dm)
    in_specs = [tile, mod, mod, mod, _const_spec(wgu.shape), _const_spec(wd.shape),
                _const_spec((1, dm)), _const_spec((1, dm))]
    args = [x, sh, sc, gt, wgu, wd, ln_g, ln_b]
    out_specs = [tile]
    out_shape = [jax.ShapeDtypeStruct((tokens, dm), F32)]
    if with_loss:
        in_specs.append(tile)
        args.append(target)
        out_specs.append(pl.BlockSpec((1, 128), lambda i: (0, 0)))
        out_shape.append(jax.ShapeDtypeStruct((1, 128), F32))
    out_specs += [tile, pl.BlockSpec((8, tm, fc), lambda i: (0, i, 0)), tile]
    out_shape += [jax.ShapeDtypeStruct((tokens, dm), F32), jax.ShapeDtypeStruct((8, tokens, fc), BF16),
                  jax.ShapeDtypeStruct((tokens, dm), BF16)]
    return _call(body, job, name=name, grid=(tokens // tm,), in_specs=in_specs, out_specs=out_specs,
                 out_shape=out_shape, args=args)


def ffn_bwd(dy, r, x, f, gu, sh, sc, gt, wgu, wd, ln_g, seq, name, job=None):
    tokens, dm = x.shape
    fc = wgu.shape[1]
    tm = min(TOKEN_TILE, seq)
    tiles_per_seq = seq // tm
    nseq = tokens // seq

    def body(dy_ref, r_ref, x_ref, f_ref, gu_ref, sh_ref, sc_ref, gt_ref, wgu_ref, wd_ref, lg_ref,
             dx_ref, dgu_ref, df_ref, a_ref, h_ref, dln_ref, dmod_ref):
        i = pl.program_id(0)
        dr, dgain, dbias = _ln_bwd(dy_ref[...], r_ref[...], lg_ref[...])

        @pl.when(i == 0)
        def _():
            dln_ref[...] = jnp.zeros_like(dln_ref)

        @pl.when(i % tiles_per_seq == 0)
        def _():
            dmod_ref[...] = jnp.zeros_like(dmod_ref)

        dln_ref[0:1, :] += dgain
        dln_ref[1:2, :] += dbias
        df32 = (0.5 * (1.0 + gt_ref[0])) * dr
        df = df32.astype(BF16)
        df_ref[...] = df
        dgate = jnp.sum(dr * (0.5 * f_ref[...].astype(F32)), axis=0, keepdims=True)
        xx = x_ref[...]
        one_sc = 1.0 + sc_ref[0]
        h = (xx * one_sc + sh_ref[0]).astype(BF16)
        h_ref[...] = h
        dh = jnp.zeros((tm, dm), F32)
        for k in range(4):
            da = _dot_nt(df, wd_ref[k])
            gk = gu_ref[k].astype(F32)
            uk = gu_ref[k + 4].astype(F32)
            sg = _sigmoid(gk)
            sil = gk * sg
            a_ref[k] = (sil * uk).astype(BF16)
            du = (da * sil).astype(BF16)
            dg = (da * uk * (sg * (1.0 + gk * (1.0 - sg)))).astype(BF16)
            dgu_ref[k] = dg
            dgu_ref[k + 4] = du
            dh = dh + _dot(dg, wgu_ref[k]) + _dot(du, wgu_ref[k + 4])
        dx_ref[...] = DN_ALPHA * dr + dh * one_sc
        dmod_ref[0, 0:1, :] += jnp.sum(dh, axis=0, keepdims=True)
        dmod_ref[0, 1:2, :] += jnp.sum(dh * xx, axis=0, keepdims=True)
        dmod_ref[0, 2:3, :] += dgate

    tile = pl.BlockSpec((tm, dm), lambda i: (i, 0))
    mod = _mod_spec(tiles_per_seq, dm)
    gu_spec = pl.BlockSpec((8, tm, fc), lambda i: (0, i, 0))
    return _call(
        body, job, name=name, grid=(tokens // tm,),
        in_specs=[tile, tile, tile, tile, gu_spec, mod, mod, mod, _const_spec(wgu.shape), _const_spec(wd.shape),
                  _const_spec((1, dm))],
        out_specs=[tile, gu_spec, tile, pl.BlockSpec((4, tm, fc), lambda i: (0, i, 0)), tile,
                   pl.BlockSpec((2, dm), lambda i: (0, 0)),
                   pl.BlockSpec((1, 3, dm), lambda i: (i // tiles_per_seq, 0, 0))],
        out_shape=[jax.ShapeDtypeStruct((tokens, dm), F32), jax.ShapeDtypeStruct((8, tokens, fc), BF16),
                   jax.ShapeDtypeStruct((tokens, dm), BF16), jax.ShapeDtypeStruct((4, tokens, fc), BF16),
                   jax.ShapeDtypeStruct((tokens, dm), BF16), jax.ShapeDtypeStruct((2, dm), F32),
                   jax.ShapeDtypeStruct((nseq, 3, dm), F32)],
        args=(dy, r, x, f, gu, sh, sc, gt, wgu, wd, ln_g))


def tn_matmul(a, b, name, job=None):
    na, tokens, kk = a.shape
    nb, _, cc = b.shape
    tt = tokens
    while 4 * tt * (kk + cc) + 8 * kk * cc > TN_VMEM_BUDGET and tt % 2 == 0 and tt > 256:
        tt //= 2
    steps = tokens // tt

    def body(a_ref, b_ref, o_ref, acc_ref):
        t = pl.program_id(2)

        @pl.when(t == 0)
        def _():
            acc_ref[...] = jnp.zeros_like(acc_ref)

        acc_ref[...] += _dot_tn(a_ref[0], b_ref[0])

        @pl.when(t == steps - 1)
        def _():
            o_ref[0, 0] = acc_ref[...].astype(BF16)

    return _call(
        body, job, name=name, grid=(na, nb, steps),
        in_specs=[pl.BlockSpec((1, tt, kk), lambda i, j, t: (i, t, 0)),
                  pl.BlockSpec((1, tt, cc), lambda i, j, t: (j, t, 0))],
        out_specs=[pl.BlockSpec((1, 1, kk, cc), lambda i, j, t: (i, j, 0, 0))],
        out_shape=[jax.ShapeDtypeStruct((na, nb, kk, cc), BF16)],
        scratch_shapes=[pltpu.VMEM((kk, cc), F32)], args=(a, b))


def proj_fwd(x1, sh, sc, w_in, seq, name, job=None):
    tokens, dm = x1.shape
    tm = min(TOKEN_TILE, seq)
    tiles_per_seq = seq // tm
    widths = [N_Q_HEADS * HEAD_DIM, N_KV_HEADS * HEAD_DIM, N_KV_HEADS * HEAD_DIM, 512, 512, 512]
    assert sum(widths) == w_in.shape[0]

    def body(x_ref, sh_ref, sc_ref, w_ref, *outs):
        h = (x_ref[...] * (1.0 + sc_ref[0]) + sh_ref[0]).astype(BF16)
        proj = _dot_nt(h, w_ref[...])
        at = 0
        for o_ref, wdt in zip(outs, widths):
            o_ref[...] = proj[:, at:at + wdt]
            at += wdt

    tile = pl.BlockSpec((tm, dm), lambda i: (i, 0))
    mod = _mod_spec(tiles_per_seq, dm)
    return _call(
        body, job, name=name, grid=(tokens // tm,),
        in_specs=[tile, mod, mod, _const_spec(w_in.shape)],
        out_specs=[pl.BlockSpec((tm, wdt), lambda i: (i, 0)) for wdt in widths],
        out_shape=[jax.ShapeDtypeStruct((tokens, wdt), F32) for wdt in widths],
        args=(x1, sh, sc, w_in))


LANES = 2 * HEAD_DIM


def _head_lane(shape):
    return lax.broadcasted_iota(jnp.int32, shape, 1) % HEAD_DIM


def _lane_half(shape):
    return lax.broadcasted_iota(jnp.int32, shape, 1) // HEAD_DIM


def _swap_rot(v):
    lane = _head_lane(v.shape)
    half = ROT_DIM // 2
    return jnp.where(lane < half, pltpu.roll(v, LANES - half, 1),
                     jnp.where(lane < ROT_DIM, pltpu.roll(v, half, 1), 0.0))


def _rope(v, cos_t, sin_t):
    return v * cos_t + _swap_rot(v) * sin_t


def _unrope(dv, cos_t, sin_t):
    return dv * cos_t + _swap_rot(dv * sin_t)


def _both_halves(t, g):
    return jnp.where(_lane_half(t.shape) == g, t, pltpu.roll(t, HEAD_DIM, 1))


def _fold_halves(t, g):
    return jnp.where(_lane_half(t.shape) == g, t + pltpu.roll(t, HEAD_DIM, 1), 0.0)


def _stack_heads(blocks):
    rows = []
    for blk in blocks:
        half = _lane_half(blk.shape)
        rows += [jnp.where(half == 0, blk, 0.0), jnp.where(half == 1, blk, 0.0)]
    return jnp.concatenate(rows, axis=0)


def _unstack_heads(t, j):
    lo = t[(2 * j) * ATTN_BLOCK:(2 * j + 1) * ATTN_BLOCK]
    hi = t[(2 * j + 1) * ATTN_BLOCK:(2 * j + 2) * ATTN_BLOCK]
    return jnp.where(_lane_half(lo.shape) == 0, lo, hi)


def _band_mask(q0, w0):
    rows, cols = GQA_GROUP * ATTN_BLOCK, 2 * ATTN_BLOCK
    qi = lax.broadcasted_iota(jnp.int32, (rows, cols), 0) % ATTN_BLOCK + q0
    ki = lax.broadcasted_iota(jnp.int32, (rows, cols), 1) + w0
    diff = qi - ki
    return (diff >= 0) & (diff < ATTN_BLOCK)


def _attn_specs(seq):
    q_spec = pl.BlockSpec((seq, GQA_GROUP * HEAD_DIM), lambda b, g: (b, g))
    kv_spec = pl.BlockSpec((seq, LANES), lambda b, g: (b, 0))
    sink_spec = pl.BlockSpec((1, GQA_GROUP * ATTN_BLOCK, 1), lambda b, g: (g, 0, 0))
    return q_spec, kv_spec, sink_spec


def _block_starts(n):
    q0 = pl.multiple_of(n * ATTN_BLOCK, ATTN_BLOCK)
    w0 = pl.multiple_of(jnp.maximum(n - 1, 0) * ATTN_BLOCK, ATTN_BLOCK)
    return q0, w0


def _stacked_queries(ref, rows, cos_b=None, sin_b=None):
    blocks = []
    for j in range(2):
        blk = ref[rows, j * LANES:(j + 1) * LANES]
        blocks.append(blk if cos_b is None else _rope(blk, cos_b, sin_b))
    return _stack_heads(blocks).astype(BF16)


def _sink_columns(sinks):
    return jnp.repeat(sinks.reshape(N_KV_HEADS, GQA_GROUP), ATTN_BLOCK, axis=1)[:, :, None]


def attn_fwd(q, k, v, cos_t, sin_t, sinks, seq, name, job=None):
    tokens = q.shape[0]
    nblk = seq // ATTN_BLOCK
    assert nblk >= 2
    scale = HEAD_DIM ** -0.5

    def body(q_ref, k_ref, v_ref, cos_ref, sin_ref, sink_ref, o_ref, kd_ref, vd_ref):
        g = pl.program_id(1)
        kd_ref[...] = _both_halves(_rope(k_ref[...], cos_ref[...], sin_ref[...]), g).astype(BF16)
        vd_ref[...] = _both_halves(v_ref[...], g).astype(BF16)
        sink = sink_ref[0]

        def block(n, carry):
            q0, w0 = _block_starts(n)
            rows, win = pl.ds(q0, ATTN_BLOCK), pl.ds(w0, 2 * ATTN_BLOCK)
            qs = _stacked_queries(q_ref, rows, cos_ref[rows, :], sin_ref[rows, :])
            s = _dot_nt(qs, kd_ref[win, :]) * scale
            s = jnp.where(_band_mask(q0, w0), s, NEG_BIG)
            m = jnp.maximum(jnp.max(s, axis=-1, keepdims=True), sink)
            p = jnp.exp(s - m)
            denom = jnp.sum(p, axis=-1, keepdims=True) + jnp.exp(sink - m)
            out = _dot((p / denom).astype(BF16), vd_ref[win, :])
            for j in range(2):
                o_ref[rows, j * LANES:(j + 1) * LANES] = _unstack_heads(out, j)
            return carry

        lax.fori_loop(0, nblk, block, 0, unroll=2)

    q_spec, kv_spec, sink_spec = _attn_specs(seq)
    return _call(
        body, job, name=name, grid=(tokens // seq, N_KV_HEADS),
        in_specs=[q_spec, kv_spec, kv_spec, kv_spec, kv_spec, sink_spec],
        out_specs=[q_spec], out_shape=[jax.ShapeDtypeStruct(q.shape, F32)],
        scratch_shapes=[pltpu.VMEM((seq, LANES), BF16), pltpu.VMEM((seq, LANES), BF16)],
        args=(q, k, v, cos_t, sin_t, _sink_columns(sinks)))


def attn_bwd(q, k, v, do, cos_t, sin_t, sinks, seq, name, job=None):
    tokens = q.shape[0]
    nseq = tokens // seq
    nblk = seq // ATTN_BLOCK
    assert nblk >= 2
    rows_stacked = GQA_GROUP * ATTN_BLOCK
    scale = HEAD_DIM ** -0.5

    def body(q_ref, k_ref, v_ref, do_ref, cos_ref, sin_ref, sink_ref, dq_ref, dk_ref, dv_ref, ds_ref,
             kd_ref, vd_ref, dkd_ref, dvd_ref):
        g = pl.program_id(1)
        kd_ref[...] = _both_halves(_rope(k_ref[...], cos_ref[...], sin_ref[...]), g).astype(BF16)
        vd_ref[...] = _both_halves(v_ref[...], g).astype(BF16)
        dkd_ref[...] = jnp.zeros_like(dkd_ref)
        dvd_ref[...] = jnp.zeros_like(dvd_ref)
        sink = sink_ref[0]

        def block(n, dsink):
            q0, w0 = _block_starts(n)
            rows, win = pl.ds(q0, ATTN_BLOCK), pl.ds(w0, 2 * ATTN_BLOCK)
            cos_b, sin_b = cos_ref[rows, :], sin_ref[rows, :]
            qs = _stacked_queries(q_ref, rows, cos_b, sin_b)
            dos = _stacked_queries(do_ref, rows)
            kw, vw = kd_ref[win, :], vd_ref[win, :]
            s = _dot_nt(qs, kw) * scale
            s = jnp.where(_band_mask(q0, w0), s, NEG_BIG)
            m = jnp.maximum(jnp.max(s, axis=-1, keepdims=True), sink)
            p = jnp.exp(s - m)
            e_sink = jnp.exp(sink - m)
            inv = 1.0 / (jnp.sum(p, axis=-1, keepdims=True) + e_sink)
            pn = p * inv
            dvd_ref[win, :] += _dot_tn(pn.astype(BF16), dos)
            dp = _dot_nt(dos, vw)
            delta = jnp.sum(dp * pn, axis=-1, keepdims=True)
            ds = (pn * (dp - delta)).astype(BF16)
            dqs = _dot(ds, kw) * scale
            dkd_ref[win, :] += _dot_tn(ds, qs) * scale
            for j in range(2):
                dq_ref[rows, j * LANES:(j + 1) * LANES] = _unrope(_unstack_heads(dqs, j), cos_b, sin_b)
            return dsink - e_sink * inv * delta

        ds_ref[0, 0] = lax.fori_loop(0, nblk // 2, lambda i, acc: block(2 * i + 1, block(2 * i, acc)),
                                     jnp.zeros((rows_stacked, 1), F32))
        dk_g = _unrope(_fold_halves(dkd_ref[...], g), cos_ref[...], sin_ref[...])
        dv_g = _fold_halves(dvd_ref[...], g)

        @pl.when(g == 0)
        def _():
            dk_ref[...] = dk_g
            dv_ref[...] = dv_g

        @pl.when(g != 0)
        def _():
            dk_ref[...] += dk_g
            dv_ref[...] += dv_g

    q_spec, kv_spec, sink_spec = _attn_specs(seq)
    return _call(
        body, job, name=name, grid=(nseq, N_KV_HEADS),
        in_specs=[q_spec, kv_spec, kv_spec, q_spec, kv_spec, kv_spec, sink_spec],
        out_specs=[q_spec, kv_spec, kv_spec, pl.BlockSpec((1, 1, rows_stacked, 1), lambda b, g: (b, g, 0, 0))],
        out_shape=[jax.ShapeDtypeStruct(q.shape, F32), jax.ShapeDtypeStruct(k.shape, F32),
                   jax.ShapeDtypeStruct(k.shape, F32), jax.ShapeDtypeStruct((nseq, N_KV_HEADS, rows_stacked, 1), F32)],
        scratch_shapes=[pltpu.VMEM((seq, LANES), BF16), pltpu.VMEM((seq, LANES), BF16),
                        pltpu.VMEM((seq, LANES), F32), pltpu.VMEM((seq, LANES), F32)],
        args=(q, k, v, do, cos_t, sin_t, _sink_columns(sinks)))


CONV_COLS = 128


def _shift_down(z, by):
    t = lax.broadcasted_iota(jnp.int32, z.shape, 0)
    return jnp.where(t >= by, pltpu.roll(z, by, 0), 0.0)


def _shift_up(z, by):
    n = z.shape[0]
    t = lax.broadcasted_iota(jnp.int32, z.shape, 0)
    return jnp.where(t < n - by, pltpu.roll(z, n - by, 0), 0.0)


def conv_fwd(u, bg, cg, conv_w, seq, name):
    tokens, width = u.shape

    def body(u_ref, bg_ref, cg_ref, w_ref, o_ref):
        z = cg_ref[...] * u_ref[...]
        yy = w_ref[2:3, :] * z + w_ref[1:2, :] * _shift_down(z, 1) + w_ref[0:1, :] * _shift_down(z, 2)
        o_ref[...] = bg_ref[...] * yy

    col = pl.BlockSpec((seq, CONV_COLS), lambda j, b: (b, j))
    return pl.pallas_call(
        body, name=name, grid=(width // CONV_COLS, tokens // seq),
        in_specs=[col, col, col, pl.BlockSpec((CONV_TAPS, CONV_COLS), lambda j, b: (0, j))],
        out_specs=col, out_shape=jax.ShapeDtypeStruct((tokens, width), F32),
        compiler_params=_params(("parallel", "parallel")),
    )(u, bg, cg, conv_w)


def conv_bwd(dout, u, bg, cg, conv_w, seq, name):
    tokens, width = u.shape

    def body(do_ref, u_ref, bg_ref, cg_ref, w_ref, du_ref, dbg_ref, dcg_ref, dw_ref):
        uu, cg_v, do = u_ref[...], cg_ref[...], do_ref[...]
        z = cg_v * uu
        z1, z2 = _shift_down(z, 1), _shift_down(z, 2)
        yy = w_ref[2:3, :] * z + w_ref[1:2, :] * z1 + w_ref[0:1, :] * z2
        dbg_ref[...] = do * yy
        dyy = do * bg_ref[...]
        dz = w_ref[2:3, :] * dyy + w_ref[1:2, :] * _shift_up(dyy, 1) + w_ref[0:1, :] * _shift_up(dyy, 2)
        du_ref[...] = dz * cg_v
        dcg_ref[...] = dz * uu

        @pl.when(pl.program_id(1) == 0)
        def _():
            dw_ref[...] = jnp.zeros_like(dw_ref)

        dw_ref[0:1, :] += jnp.sum(dyy * z2, axis=0, keepdims=True)
        dw_ref[1:2, :] += jnp.sum(dyy * z1, axis=0, keepdims=True)
        dw_ref[2:3, :] += jnp.sum(dyy * z, axis=0, keepdims=True)

    col = pl.BlockSpec((seq, CONV_COLS), lambda j, b: (b, j))
    w_spec = pl.BlockSpec((CONV_TAPS, CONV_COLS), lambda j, b: (0, j))
    act = jax.ShapeDtypeStruct((tokens, width), F32)
    return pl.pallas_call(
        body, name=name, grid=(width // CONV_COLS, tokens // seq),
        in_specs=[col, col, col, col, w_spec], out_specs=[col, col, col, w_spec],
        out_shape=[act, act, act, jax.ShapeDtypeStruct((CONV_TAPS, width), F32)],
        compiler_params=_params(("parallel", "arbitrary")),
    )(dout, u, bg, cg, conv_w)


def out_fwd(x1, attn, conv, gt, w_out, ln_g, ln_b, seq, name, job=None):
    tokens, dm = x1.shape
    half = attn.shape[1]
    tm = min(TOKEN_TILE, seq)
    tiles_per_seq = seq // tm

    def body(x_ref, a_ref, c_ref, gt_ref, w_ref, lg_ref, lb_ref, xo_ref, r_ref, mi_ref, mix_ref):
        mixin = jnp.concatenate([a_ref[...], c_ref[...]], axis=1).astype(BF16)
        mi_ref[...] = mixin
        mix = _dot(mixin, w_ref[...])
        mix_ref[...] = mix.astype(BF16)
        r = DN_ALPHA * x_ref[...] + (1.0 + gt_ref[0]) * mix
        r_ref[...] = r
        xhat, _ = _ln_stats(r)
        xo_ref[...] = xhat * lg_ref[...] + lb_ref[...]

    tile = pl.BlockSpec((tm, dm), lambda i: (i, 0))
    htile = pl.BlockSpec((tm, half), lambda i: (i, 0))
    return _call(
        body, job, name=name, grid=(tokens // tm,),
        in_specs=[tile, htile, htile, _mod_spec(tiles_per_seq, dm), _const_spec(w_out.shape),
                  _const_spec((1, dm)), _const_spec((1, dm))],
        out_specs=[tile, tile, tile, tile],
        out_shape=[jax.ShapeDtypeStruct((tokens, dm), F32), jax.ShapeDtypeStruct((tokens, dm), F32),
                   jax.ShapeDtypeStruct((tokens, dm), BF16), jax.ShapeDtypeStruct((tokens, dm), BF16)],
        args=(x1, attn, conv, gt, w_out, ln_g, ln_b))


def out_bwd(dy, r, mix, gt, w_out, ln_g, seq, name, job=None):
    tokens, dm = r.shape
    half = dm // 2
    tm = min(TOKEN_TILE, seq)
    tiles_per_seq = seq // tm
    nseq = tokens // seq

    def body(dy_ref, r_ref, mix_ref, gt_ref, w_ref, lg_ref, dres_ref, da_ref, dc_ref, dmix_ref, dln_ref, dgt_ref):
        i = pl.program_id(0)
        dr, dgain, dbias = _ln_bwd(dy_ref[...], r_ref[...], lg_ref[...])

        @pl.when(i == 0)
        def _():
            dln_ref[...] = jnp.zeros_like(dln_ref)

        @pl.when(i % tiles_per_seq == 0)
        def _():
            dgt_ref[...] = jnp.zeros_like(dgt_ref)

        dln_ref[0:1, :] += dgain
        dln_ref[1:2, :] += dbias
        dgt_ref[0] += jnp.sum(dr * mix_ref[...].astype(F32), axis=0, keepdims=True)
        dres_ref[...] = DN_ALPHA * dr
        dmix = ((1.0 + gt_ref[0]) * dr).astype(BF16)
        dmix_ref[...] = dmix
        dmixin = _dot_nt(dmix, w_ref[...])
        da_ref[...] = dmixin[:, :half]
        dc_ref[...] = dmixin[:, half:]

    tile = pl.BlockSpec((tm, dm), lambda i: (i, 0))
    htile = pl.BlockSpec((tm, half), lambda i: (i, 0))
    return _call(
        body, job, name=name, grid=(tokens // tm,),
        in_specs=[tile, tile, tile, _mod_spec(tiles_per_seq, dm), _const_spec(w_out.shape), _const_spec((1, dm))],
        out_specs=[tile, htile, htile, tile, pl.BlockSpec((2, dm), lambda i: (0, 0)),
                   pl.BlockSpec((1, 1, dm), lambda i: (i // tiles_per_seq, 0, 0))],
        out_shape=[jax.ShapeDtypeStruct((tokens, dm), F32), jax.ShapeDtypeStruct((tokens, half), F32),
                   jax.ShapeDtypeStruct((tokens, half), F32), jax.ShapeDtypeStruct((tokens, dm), BF16),
                   jax.ShapeDtypeStruct((2, dm), F32), jax.ShapeDtypeStruct((nseq, 1, dm), F32)],
        args=(dy, r, mix, gt, w_out, ln_g))


def proj_bwd(parts, dres, x1, sh, sc, w_in, seq, name):
    tokens, dm = x1.shape
    tm = min(TOKEN_TILE, seq)
    tiles_per_seq = seq // tm
    nseq = tokens // seq
    widths = [p.shape[1] for p in parts]
    total = sum(widths)

    def body(*refs):
        part_refs = refs[:6]
        dres_ref, x_ref, sh_ref, sc_ref, w_ref, dx_ref, dproj_ref, h_ref, dmod_ref = refs[6:]
        dproj = jnp.concatenate([p[...] for p in part_refs], axis=1).astype(BF16)
        dproj_ref[...] = dproj
        dh = _dot(dproj, w_ref[...])
        xx = x_ref[...]
        one_sc = 1.0 + sc_ref[0]
        h_ref[...] = (xx * one_sc + sh_ref[0]).astype(BF16)
        dx_ref[...] = dres_ref[...] + dh * one_sc

        @pl.when(pl.program_id(0) % tiles_per_seq == 0)
        def _():
            dmod_ref[...] = jnp.zeros_like(dmod_ref)

        dmod_ref[0, 0:1, :] += jnp.sum(dh, axis=0, keepdims=True)
        dmod_ref[0, 1:2, :] += jnp.sum(dh * xx, axis=0, keepdims=True)

    tile = pl.BlockSpec((tm, dm), lambda i: (i, 0))
    mod = _mod_spec(tiles_per_seq, dm)
    return pl.pallas_call(
        body, name=name, grid=(tokens // tm,),
        in_specs=[pl.BlockSpec((tm, wdt), lambda i: (i, 0)) for wdt in widths]
        + [tile, tile, mod, mod, _const_spec(w_in.shape)],
        out_specs=[tile, pl.BlockSpec((tm, total), lambda i: (i, 0)), tile,
                   pl.BlockSpec((1, 2, dm), lambda i: (i // tiles_per_seq, 0, 0))],
        out_shape=[jax.ShapeDtypeStruct((tokens, dm), F32), jax.ShapeDtypeStruct((tokens, total), BF16),
                   jax.ShapeDtypeStruct((tokens, dm), BF16), jax.ShapeDtypeStruct((nseq, 2, dm), F32)],
        compiler_params=_params(("arbitrary",)),
    )(*parts, dres, x1, sh, sc, w_in)


def _rope_tables(positions):
    half = ROT_DIM // 2
    inv_freq = jnp.power(jnp.float32(ROPE_THETA), -jnp.arange(0, ROT_DIM, 2, dtype=F32) / ROT_DIM)
    lane = jnp.arange(LANES) % HEAD_DIM
    freq = jnp.where(lane < ROT_DIM, inv_freq[lane % half], 0.0)
    sign = jnp.where(lane < half, -1.0, 1.0).astype(F32)
    ang = positions.astype(F32)[:, None] * freq[None, :]
    return jnp.cos(ang), sign[None, :] * jnp.sin(ang)


def kernel(x, c, positions, w_ada, b_ada, ffn1_w_gate_up, ffn1_w_down, ln1_g, ln1_b, w_in, conv_w, attn_sinks, w_out, ln2_g, ln2_b, ffn2_w_gate_up, ffn2_w_down, ln3_g, ln3_b, loss_target, m_w_ada, m_b_ada, m_ffn1_w_gate_up, m_ffn1_w_down, m_ln1_g, m_ln1_b, m_w_in, m_conv_w, m_attn_sinks, m_w_out, m_ln2_g, m_ln2_b, m_ffn2_w_gate_up, m_ffn2_w_down, m_ln3_g, m_ln3_b, v_w_ada, v_b_ada, v_ffn1_w_gate_up, v_ffn1_w_down, v_ln1_g, v_ln1_b, v_w_in, v_conv_w, v_attn_sinks, v_w_out, v_ln2_g, v_ln2_b, v_ffn2_w_gate_up, v_ffn2_w_down, v_ln3_g, v_ln3_b):
    nseq, seq, dm = x.shape
    tokens = nseq * seq
    dev = 4 * lax.axis_index("x") + 2 * lax.axis_index("y") + lax.axis_index("c")
    ada_cols = w_ada.shape[2]
    ff = ffn1_w_down.shape[1] * N_DEV
    fc = ff // 4
    in_cols = w_in.shape[2]
    conv_cols = conv_w.shape[2]

    def t_bf16(w):
        return w[0].T.astype(BF16)

    c_all, convw_all = all_gather([c, conv_w[0]], "gather_cond")
    wgu1, wd1 = all_gather([t_bf16(ffn1_w_gate_up), ffn1_w_down[0].astype(BF16)], "gather_ffn1")
    c_all = c_all.reshape(N_DEV * nseq, dm)
    convw_full = convw_all.transpose(1, 0, 2).reshape(CONV_TAPS, N_DEV * conv_cols)
    wd1 = wd1.reshape(4, fc, dm)

    b_cols = lax.dynamic_slice(b_ada, (0, dev * ada_cols), (1, ada_cols))
    cond_all, mod_cols = ada_fwd(c_all, w_ada[0], b_cols, "ada_fwd")
    (mod_all,) = all_gather([mod_cols], "gather_mod")
    mod = lax.dynamic_slice(mod_all, (0, dev * nseq, 0), (N_DEV, nseq, ada_cols))
    mod = mod.transpose(1, 0, 2).reshape(nseq, 9, 1, dm)
    sh1, sc1, g1, sh2, sc2, g2, sh3, sc3, g3 = [mod[:, i] for i in range(9)]

    x0 = x.reshape(tokens, dm)
    spread = gather_spread_job([t_bf16(w_in), w_out[0].astype(BF16), ffn2_w_down[0].astype(BF16)])
    (x1, r1, gu1, f1), spread = ffn_fwd(x0, sh1, sc1, g1, wgu1, wd1, ln1_g, ln1_b, seq, "ffn1_fwd", job=spread)
    win, wout = run_job(gather_forward_job(spread[:2]), "gather_mix_forward")
    win = win.reshape(N_DEV * in_cols, dm)
    wout = wout.reshape(dm, dm)
    (q, k, v, u, bg, cg), (wd2,) = proj_fwd(x1, sh2, sc2, win, seq, "proj_fwd", job=gather_forward_job(spread[2:]))
    wd2 = wd2.reshape(4, fc, dm)
    cos_t, sin_t = _rope_tables(positions.reshape(tokens))
    sinks = attn_sinks[0]
    (attn,), spread = attn_fwd(q, k, v, cos_t, sin_t, sinks, seq, "attn_fwd",
                               job=gather_spread_job([t_bf16(ffn2_w_gate_up)]))
    conv = conv_fwd(u, bg, cg, convw_full, seq, "conv_fwd")
    (x2, r2, mixin, mix), (wgu2,) = out_fwd(x1, attn, conv, g2, wout, ln2_g, ln2_b, seq, "out_fwd",
                                            job=gather_forward_job(spread))
    target = loss_target.reshape(tokens, dm)
    (dy3, loss_part, r3, gu3, f3), _ = ffn_fwd(x2, sh3, sc3, g3, wgu2, wd2, ln3_g, ln3_b, seq, "ffn2_fwd", target=target)

    (dx2, dgu3, df3, a3, h3, dln3, dmod3), _ = ffn_bwd(dy3, r3, x2, f3, gu3, sh3, sc3, g3, wgu2, wd2, ln3_g, seq, "ffn2_bwd")
    g_wd2 = tn_matmul(a3, df3[None], "ffn2_dwd")[0][0].reshape(N_DEV, ff // N_DEV, dm)
    g_wgu2 = tn_matmul(dgu3, h3[None], "ffn2_dwgu")[0][0].reshape(N_DEV, fc, dm)
    (dres2, dattn, dconv, dmix, dln2, dg2), swapped = out_bwd(dx2, r2, mix, g2, wout, ln2_g, seq, "out_bwd",
                                                              job=swap_job([g_wgu2, g_wd2]))
    p_wgu2, own_wgu2 = pair_sum(g_wgu2, swapped[0], "pair_wgu2")
    p_wd2, own_wd2 = pair_sum(g_wd2, swapped[1], "pair_wd2")
    g_wout = tn_matmul(mixin[None], dmix[None], "dwout")[0][0].reshape(N_DEV, dm // N_DEV, dm)
    du, dbg, dcg, dconvw = conv_bwd(dconv, u, bg, cg, convw_full, seq, "conv_bwd")
    (dq, dk, dv, dsink_rows), (far_wgu2, far_wd2) = attn_bwd(
        q, k, v, dattn, cos_t, sin_t, sinks, seq, "attn_bwd", job=chip_exchange_job([p_wgu2, p_wd2]))
    parts = [dq, dk, dv, du, dbg, dcg]
    dx1, dproj, h2, dmod2 = proj_bwd(parts, dres2, x1, sh2, sc2, win, seq, "proj_bwd")
    g_win = tn_matmul(dproj[None], h2[None], "dwin")[0][0].reshape(N_DEV, in_cols, dm)
    (dx0, dgu1, df1, a1, h1, dln1, dmod1), _ = ffn_bwd(
        dx1, r1, x0, f1, gu1, sh1, sc1, g1, wgu1, wd1, ln1_g, seq, "ffn1_bwd")
    (g_wd1,), swapped = tn_matmul(a1, df1[None], "ffn1_dwd", job=swap_job([g_wout, g_win]))
    g_wd1 = g_wd1.reshape(N_DEV, ff // N_DEV, dm)
    p_wout, own_wout = pair_sum(g_wout, swapped[0], "pair_wout")
    p_win, own_win = pair_sum(g_win, swapped[1], "pair_win")
    (g_wgu1,), (far_wout, far_win) = tn_matmul(dgu1, h1[None], "ffn1_dwgu", job=chip_exchange_job([p_wout, p_win]))
    g_wgu1 = g_wgu1.reshape(N_DEV, fc, dm)

    grads = {
        "ffn1_w_gate_up": reduce_scatter(g_wgu1, "rs_wgu1"), "ffn1_w_down": reduce_scatter(g_wd1, "rs_wd1"),
        "w_in": own_win, "w_out": own_wout, "ffn2_w_gate_up": own_wgu2, "ffn2_w_down": own_wd2,
    }
    others = {"w_in": far_win, "w_out": far_wout, "ffn2_w_gate_up": far_wgu2, "ffn2_w_down": far_wd2}

    dmod = jnp.concatenate([dmod1, dmod2, dg2, dmod3], axis=1).reshape(nseq, 9 * dm)
    (dmod_all,) = all_gather([dmod], "gather_dmod")
    dmod_cols = lax.dynamic_slice(dmod_all.reshape(N_DEV * nseq, 9 * dm), (0, dev * ada_cols), (N_DEV * nseq, ada_cols))
    grads["w_ada"], gb_cols = ada_bwd(cond_all, dmod_cols, "ada_bwd")

    dsinks = jnp.sum(dsink_rows.reshape(nseq, N_Q_HEADS, ATTN_BLOCK), axis=(0, 2))
    small = jnp.zeros((8, dm), F32)
    small = small.at[0:2].set(dln1).at[2:4].set(dln2).at[4:6].set(dln3)
    small = small.at[6, 0:N_Q_HEADS].set(dsinks).at[7, 0].set(loss_part[0, 0])
    small_all, dconvw_all, gb_all = all_gather([small, dconvw, gb_cols], "gather_small")
    small_sum = sum_devices(small_all, "sum_small")
    dconvw_sum = sum_devices(dconvw_all, "sum_convw")
    loss = small_sum[7, 0]
    grads["b_ada"] = gb_all.reshape(1, N_DEV * ada_cols)
    grads["conv_w"] = lax.dynamic_slice(dconvw_sum, (0, dev * conv_cols), (CONV_TAPS, conv_cols))
    grads["attn_sinks"] = small_sum[6:7, 0:N_Q_HEADS]
    for i, nm in enumerate(["ln1_g", "ln1_b", "ln2_g", "ln2_b", "ln3_g", "ln3_b"]):
        grads[nm] = small_sum[i:i + 1]

    given = dict(w_ada=(w_ada, m_w_ada, v_w_ada), b_ada=(b_ada, m_b_ada, v_b_ada),
                 ffn1_w_gate_up=(ffn1_w_gate_up, m_ffn1_w_gate_up, v_ffn1_w_gate_up),
                 ffn1_w_down=(ffn1_w_down, m_ffn1_w_down, v_ffn1_w_down),
                 ln1_g=(ln1_g, m_ln1_g, v_ln1_g), ln1_b=(ln1_b, m_ln1_b, v_ln1_b),
                 w_in=(w_in, m_w_in, v_w_in), conv_w=(conv_w, m_conv_w, v_conv_w),
                 attn_sinks=(attn_sinks, m_attn_sinks, v_attn_sinks), w_out=(w_out, m_w_out, v_w_out),
                 ln2_g=(ln2_g, m_ln2_g, v_ln2_g), ln2_b=(ln2_b, m_ln2_b, v_ln2_b),
                 ffn2_w_gate_up=(ffn2_w_gate_up, m_ffn2_w_gate_up, v_ffn2_w_gate_up),
                 ffn2_w_down=(ffn2_w_down, m_ffn2_w_down, v_ffn2_w_down),
                 ln3_g=(ln3_g, m_ln3_g, v_ln3_g), ln3_b=(ln3_b, m_ln3_b, v_ln3_b))
    order = ["w_ada", "b_ada", "ffn1_w_gate_up", "ffn1_w_down", "ln1_g", "ln1_b", "w_in", "conv_w", "attn_sinks",
             "w_out", "ln2_g", "ln2_b", "ffn2_w_gate_up", "ffn2_w_down", "ln3_g", "ln3_b"]
    transposed = ("ffn1_w_gate_up", "ffn2_w_gate_up", "w_in")
    out_g, out_d, out_m, out_v = [], [], [], []
    for nm in order:
        shape = given[nm][0].shape
        two_d = (shape[-2], shape[-1])
        if nm in transposed:
            w2, m2, v2 = [t[0].T for t in given[nm]]
            back = lambda t: t.T[None]
        else:
            w2, m2, v2 = [t.reshape(two_d) for t in given[nm]]
            back = lambda t, shape=shape: t.reshape(shape)
        res = adamw(w2, grads[nm].reshape(w2.shape), m2, v2, "adamw_" + nm, others=others.get(nm))
        for lst, t in zip((out_g, out_d, out_m, out_v), res):
            lst.append(back(t))
    grad_x = dx0.reshape(nseq, seq, dm)
    return (loss, grad_x, *out_g, *out_d, *out_m, *out_v)
```

```python
import functools

import jax
import jax.numpy as jnp
from jax import lax
from jax.experimental import pallas as pl
from jax.experimental.pallas import tpu as pltpu

F32 = jnp.float32
BF16 = jnp.bfloat16
MESH = pl.DeviceIdType.MESH

N_DEV = 8
N_CHIP = 4
HEAD_DIM = 64
N_Q_HEADS = 8
N_KV_HEADS = 2
GQA_GROUP = N_Q_HEADS // N_KV_HEADS
ATTN_BLOCK = 128
ROT_DIM = 16
ROPE_THETA = 500000.0
CONV_TAPS = 3
LN_EPS = 1e-5
DN_ALPHA = 2.0 ** 0.25
ADAM_LR = 0.001
ADAM_B1 = 0.9
ADAM_B2 = 0.999
ADAM_EPS = 1e-08
ADAM_WD = 0.01
ADAM_STEP = 10
NEG_BIG = -1e30

VMEM_LIMIT = 56 * 1024 * 1024
TOKEN_TILE = 256
FFN_FWD_TILE = 512
TN_VMEM_BUDGET = 36 * 1024 * 1024


def _params(semantics=None, vmem=VMEM_LIMIT):
    return pltpu.CompilerParams(dimension_semantics=semantics, vmem_limit_bytes=vmem)


def _dot(a, b):
    return jnp.dot(a, b, preferred_element_type=F32)


def _dot_nt(a, b):
    return lax.dot_general(a, b, (((1,), (1,)), ((), ())), preferred_element_type=F32)


def _dot_tn(a, b):
    return lax.dot_general(a, b, (((0,), (0,)), ((), ())), preferred_element_type=F32)


def _sigmoid(x):
    return pl.reciprocal(1.0 + jnp.exp(-x), approx=True)


def _ln_stats(r):
    mu = jnp.mean(r, axis=-1, keepdims=True)
    d = r - mu
    var = jnp.mean(d * d, axis=-1, keepdims=True)
    rstd = lax.rsqrt(var + LN_EPS)
    return d * rstd, rstd


def _ln_bwd(dy, r, g):
    xhat, rstd = _ln_stats(r)
    dxhat = dy * g
    c1 = jnp.mean(dxhat, axis=-1, keepdims=True)
    c2 = jnp.mean(dxhat * xhat, axis=-1, keepdims=True)
    dr = rstd * (dxhat - c1 - xhat * c2)
    return dr, jnp.sum(dy * xhat, axis=0, keepdims=True), jnp.sum(dy, axis=0, keepdims=True)


def _const_spec(shape):
    nd = len(shape)
    return pl.BlockSpec(shape, lambda *_: (0,) * nd, pipeline_mode=pl.Buffered(1))


def all_gather(arrs, name):
    n = len(arrs)

    def body(*refs):
        ins, outs = refs[:n], refs[n:2 * n]
        send_sems, recv_sems, local_sems = refs[2 * n:]
        x, y, c = lax.axis_index("x"), lax.axis_index("y"), lax.axis_index("c")
        me, sibling = (x, y, c), (x, y, 1 - c)
        chips = [(1 - x, y), (x, 1 - y), (1 - x, 1 - y)]

        def slot(i, p):
            return outs[i].at[4 * p[0] + 2 * p[1] + p[2]]

        def copy(i, k, block, to, src=None):
            return pltpu.make_async_remote_copy(
                src_ref=slot(i, block) if src is None else src, dst_ref=slot(i, block),
                send_sem=send_sems.at[i, k], recv_sem=recv_sems.at[i, k],
                device_id=to, device_id_type=MESH)

        mine = [pltpu.make_async_copy(ins[i], slot(i, me), local_sems.at[i]) for i in range(n)]
        for cp in mine:
            cp.start()
        first = []
        for i in range(n):
            first.append(copy(i, 0, me, sibling, src=ins[i]))
            first += [copy(i, 1 + j, me, (*chip, c), src=ins[i]) for j, chip in enumerate(chips)]
        for cp in first:
            cp.start()
        passed = []
        for j, chip in enumerate(chips):
            for i in range(n):
                copy(i, 1 + j, (*chip, c), me).wait_recv()
                cp = copy(i, 4 + j, (*chip, c), sibling)
                cp.start()
                passed.append(cp)
        for i in range(n):
            copy(i, 0, sibling, me).wait_recv()
            for j, chip in enumerate(chips):
                copy(i, 4 + j, (*chip, 1 - c), me).wait_recv()
        for cp in first + passed:
            cp.wait_send()
        for cp in mine:
            cp.wait()

    any_spec = pl.BlockSpec(memory_space=pl.ANY)
    return pl.pallas_call(
        body, name=name,
        out_shape=[jax.ShapeDtypeStruct((N_DEV, *a.shape), a.dtype) for a in arrs],
        in_specs=[any_spec] * n, out_specs=[any_spec] * n,
        scratch_shapes=[pltpu.SemaphoreType.DMA((n, 7)), pltpu.SemaphoreType.DMA((n, 7)),
                        pltpu.SemaphoreType.DMA((n,))],
    )(*arrs)


RS_ROWS = 32


def reduce_scatter(g, name):
    _, rows, cols = g.shape
    nblk = rows // RS_ROWS
    assert nblk * RS_ROWS == rows

    def body(g_ref, out_ref, r1_ref, p_ref, r2_ref, send_sems, recv_sems):
        x, y, c = lax.axis_index("x"), lax.axis_index("y"), lax.axis_index("c")
        sibling = (x, y, 1 - c)
        q_me = 2 * x + y
        swaps = []
        for q in range(N_CHIP):
            cp = pltpu.make_async_remote_copy(
                src_ref=g_ref.at[2 * q + (1 - c)], dst_ref=r1_ref.at[q],
                send_sem=send_sems.at[q], recv_sem=recv_sems.at[q], device_id=sibling, device_id_type=MESH)
            cp.start()
            swaps.append(cp)
        for cp in swaps:
            cp.wait_recv()

        def pair_sum(i, carry):
            r = pl.ds(pl.multiple_of(i * RS_ROWS, RS_ROWS), RS_ROWS)
            for q in range(N_CHIP):
                p_ref[q, r, :] = (g_ref[2 * q + c, r, :].astype(F32) + r1_ref[q, r, :].astype(F32)).astype(BF16)
            return carry

        lax.fori_loop(0, nblk, pair_sum, 0)
        chips = [(1 - x, y), (x, 1 - y), (1 - x, 1 - y)]
        sends = []
        for k, chip in enumerate(chips):
            cp = pltpu.make_async_remote_copy(
                src_ref=p_ref.at[2 * chip[0] + chip[1]], dst_ref=r2_ref.at[k],
                send_sem=send_sems.at[N_CHIP + k], recv_sem=recv_sems.at[N_CHIP + k],
                device_id=(*chip, c), device_id_type=MESH)
            cp.start()
            sends.append(cp)
        for cp in sends:
            cp.wait_recv()

        def total(i, carry):
            r = pl.ds(pl.multiple_of(i * RS_ROWS, RS_ROWS), RS_ROWS)
            acc = g_ref[2 * q_me + c, r, :].astype(F32) + r1_ref[q_me, r, :].astype(F32)
            for k in range(3):
                acc = acc + r2_ref[k, r, :].astype(F32)
            out_ref[r, :] = acc
            return carry

        lax.fori_loop(0, nblk, total, 0)
        for cp in swaps + sends:
            cp.wait_send()

    vmem = pl.BlockSpec(memory_space=pltpu.VMEM)
    return pl.pallas_call(
        body, name=name,
        out_shape=jax.ShapeDtypeStruct((rows, cols), F32),
        in_specs=[vmem], out_specs=vmem,
        scratch_shapes=[pltpu.VMEM((N_CHIP, rows, cols), BF16), pltpu.VMEM((N_CHIP, rows, cols), BF16),
                        pltpu.VMEM((3, rows, cols), BF16),
                        pltpu.SemaphoreType.DMA((N_CHIP + 3,)), pltpu.SemaphoreType.DMA((N_CHIP + 3,))],
        compiler_params=_params(),
    )(g)


def _place():
    x, y, c = lax.axis_index("x"), lax.axis_index("y"), lax.axis_index("c")
    return x, y, c, [(1 - x, y), (x, 1 - y), (1 - x, 1 - y)]


def _slot(p):
    return 4 * p[0] + 2 * p[1] + p[2]


class _Job:
    def __init__(self, ins, outs, nsem, copies, aliases=None, local=None):
        self.ins, self.outs, self.nsem, self.copies = list(ins), list(outs), nsem, copies
        self.aliases = aliases or {}
        self.local = local

    def scratch(self):
        s = [pltpu.SemaphoreType.DMA(self.nsem), pltpu.SemaphoreType.DMA(self.nsem)]
        if self.local is not None:
            s.append(pltpu.SemaphoreType.DMA((len(self.ins),)))
        return s

    def start(self, ins, outs, sems):
        if self.local is not None:
            for cp in self.local(ins, outs, sems[2]):
                cp.start()
        for cp in self.copies(ins, outs, sems[0], sems[1])[0]:
            cp.start()

    def finish(self, ins, outs, sems):
        started, awaited = self.copies(ins, outs, sems[0], sems[1])
        for cp in awaited:
            cp.wait_recv()
        for cp in started:
            cp.wait_send()
        if self.local is not None:
            for cp in self.local(ins, outs, sems[2]):
                cp.wait()


class _Jobs:
    def __init__(self, jobs):
        self.jobs = jobs
        self.ins = [a for j in jobs for a in j.ins]
        self.outs = [o for j in jobs for o in j.outs]
        self.aliases = {}
        at_in = at_out = 0
        for j in jobs:
            self.aliases.update({at_in + i: at_out + o for i, o in j.aliases.items()})
            at_in, at_out = at_in + len(j.ins), at_out + len(j.outs)

    def scratch(self):
        return [s for j in self.jobs for s in j.scratch()]

    def _each(self, ins, outs, sems):
        at_in = at_out = at_sem = 0
        for j in self.jobs:
            n_in, n_out, n_sem = len(j.ins), len(j.outs), len(j.scratch())
            yield j, ins[at_in:at_in + n_in], outs[at_out:at_out + n_out], sems[at_sem:at_sem + n_sem]
            at_in, at_out, at_sem = at_in + n_in, at_out + n_out, at_sem + n_sem

    def start(self, ins, outs, sems):
        for j, i, o, s in self._each(ins, outs, sems):
            j.start(i, o, s)

    def finish(self, ins, outs, sems):
        for j, i, o, s in self._each(ins, outs, sems):
            j.finish(i, o, s)

    def split(self, results):
        at, parts = 0, []
        for j in self.jobs:
            parts.append(results[at:at + len(j.outs)])
            at += len(j.outs)
        return parts


def _remote(src, dst, send, recv, idx, to):
    return pltpu.make_async_remote_copy(src_ref=src, dst_ref=dst, send_sem=send.at[idx], recv_sem=recv.at[idx],
                                        device_id=to, device_id_type=MESH)


def gather_spread_job(shards):
    def copies(ins, outs, send, recv):
        x, y, c, chips = _place()
        me = (x, y, c)
        peers = [(x, y, 1 - c)] + [(*chip, c) for chip in chips]
        started, awaited = [], []
        for i, (src, dst) in enumerate(zip(ins, outs)):
            for k, peer in enumerate(peers):
                started.append(_remote(src, dst.at[_slot(me)], send, recv, (i, k), peer))
                awaited.append(_remote(src, dst.at[_slot(peer)], send, recv, (i, k), peer))
        return started, awaited

    def local(ins, outs, sems):
        x, y, c, _ = _place()
        return [pltpu.make_async_copy(src, dst.at[_slot((x, y, c))], sems.at[i])
                for i, (src, dst) in enumerate(zip(ins, outs))]

    outs = [jax.ShapeDtypeStruct((N_DEV, *a.shape), a.dtype) for a in shards]
    return _Job(shards, outs, (len(shards), 4), copies, local=local)


def gather_forward_job(fulls):
    def copies(ins, outs, send, recv):
        x, y, c, chips = _place()
        started, awaited = [], []
        for i, buf in enumerate(outs):
            for j, chip in enumerate(chips):
                mine, theirs = buf.at[_slot((*chip, c))], buf.at[_slot((*chip, 1 - c))]
                started.append(_remote(mine, mine, send, recv, (i, j), (x, y, 1 - c)))
                awaited.append(_remote(theirs, theirs, send, recv, (i, j), (x, y, 1 - c)))
        return started, awaited

    outs = [jax.ShapeDtypeStruct(a.shape, a.dtype) for a in fulls]
    return _Job(fulls, outs, (len(fulls), 3), copies, aliases={i: i for i in range(len(fulls))})


def swap_job(gs):
    def copies(ins, outs, send, recv):
        x, y, c, _ = _place()
        started, awaited = [], []
        for i, (g, r1) in enumerate(zip(ins, outs)):
            for q in range(N_CHIP):
                started.append(_remote(g.at[2 * q + (1 - c)], r1.at[q], send, recv, (i, q), (x, y, 1 - c)))
                awaited.append(_remote(g.at[2 * q + c], r1.at[q], send, recv, (i, q), (x, y, 1 - c)))
        return started, awaited

    outs = [jax.ShapeDtypeStruct((N_CHIP, *g.shape[1:]), g.dtype) for g in gs]
    return _Job(gs, outs, (len(gs), N_CHIP), copies)


def chip_exchange_job(ps):
    def copies(ins, outs, send, recv):
        x, y, c, chips = _place()
        started, awaited = [], []
        for i, (p, r2) in enumerate(zip(ins, outs)):
            for k, chip in enumerate(chips):
                started.append(_remote(p.at[2 * chip[0] + chip[1]], r2.at[k], send, recv, (i, k), (*chip, c)))
                awaited.append(_remote(p.at[2 * x + y], r2.at[k], send, recv, (i, k), (*chip, c)))
        return started, awaited

    outs = [jax.ShapeDtypeStruct((3, *p.shape[1:]), p.dtype) for p in ps]
    return _Job(ps, outs, (len(ps), 3), copies)


def _call(body, job, *, name, grid, in_specs, out_specs, out_shape, args, scratch_shapes=(), vmem=VMEM_LIMIT):
    if job is None:
        res = pl.pallas_call(
            body, name=name, grid=grid, in_specs=in_specs, out_specs=out_specs, out_shape=out_shape,
            scratch_shapes=list(scratch_shapes), compiler_params=_params(("arbitrary",) * len(grid), vmem),
        )(*args)
        return res, []
    n_in, n_out, n_scr = len(in_specs), len(out_specs), len(scratch_shapes)
    j_in, j_out = len(job.ins), len(job.outs)

    def with_copies(*refs):
        at = 0
        ins = refs[at:at + n_in]; at += n_in
        jins = refs[at:at + j_in]; at += j_in
        outs = refs[at:at + n_out]; at += n_out
        jouts = refs[at:at + j_out]; at += j_out
        scr = refs[at:at + n_scr]; at += n_scr
        sems = refs[at:]
        ids = [pl.program_id(d) for d in range(len(grid))]
        first = functools.reduce(jnp.logical_and, [i == 0 for i in ids])
        last = functools.reduce(jnp.logical_and, [i == n - 1 for i, n in zip(ids, grid)])

        @pl.when(first)
        def _():
            job.start(jins, jouts, sems)

        body(*ins, *outs, *scr)

        @pl.when(last)
        def _():
            job.finish(jins, jouts, sems)

    any_spec = pl.BlockSpec(memory_space=pl.ANY)
    res = pl.pallas_call(
        with_copies, name=name, grid=grid,
        in_specs=list(in_specs) + [any_spec] * j_in, out_specs=list(out_specs) + [any_spec] * j_out,
        out_shape=list(out_shape) + list(job.outs),
        input_output_aliases={n_in + i: n_out + o for i, o in job.aliases.items()},
        scratch_shapes=list(scratch_shapes) + job.scratch(),
        compiler_params=_params(("arbitrary",) * len(grid), vmem),
    )(*args, *job.ins)
    return res[:n_out], res[n_out:]


def run_job(job, name):
    def body(*refs):
        j_in, j_out = len(job.ins), len(job.outs)
        ins, outs, sems = refs[:j_in], refs[j_in:j_in + j_out], refs[j_in + j_out:]
        job.start(ins, outs, sems)
        job.finish(ins, outs, sems)

    any_spec = pl.BlockSpec(memory_space=pl.ANY)
    return pl.pallas_call(
        body, name=name, in_specs=[any_spec] * len(job.ins), out_specs=[any_spec] * len(job.outs),
        out_shape=list(job.outs), input_output_aliases=dict(job.aliases), scratch_shapes=job.scratch(),
    )(*job.ins)


def pair_sum(g, r1, name):
    _, rows, cols = g.shape
    rb = next(cand for cand in range(min(rows, 512), 0, -16) if rows % cand == 0)

    def body(g_ref, r1_ref, p_ref, own_ref):
        x, y, c, _ = _place()
        s = g_ref[c].astype(F32) + r1_ref[0].astype(F32)
        p_ref[0] = s.astype(BF16)

        @pl.when(pl.program_id(1) == 2 * x + y)
        def _():
            own_ref[...] = s

    return pl.pallas_call(
        body, name=name, grid=(rows // rb, N_CHIP),
        in_specs=[pl.BlockSpec((2, rb, cols), lambda i, q: (q, i, 0)), pl.BlockSpec((1, rb, cols), lambda i, q: (q, i, 0))],
        out_specs=[pl.BlockSpec((1, rb, cols), lambda i, q: (q, i, 0)), pl.BlockSpec((rb, cols), lambda i, q: (i, 0))],
        out_shape=[jax.ShapeDtypeStruct((N_CHIP, rows, cols), BF16), jax.ShapeDtypeStruct((rows, cols), F32)],
        compiler_params=_params(("arbitrary", "arbitrary")),
    )(g, r1)


def sum_devices(a, name):
    def body(a_ref, o_ref):
        acc = a_ref[0]
        for d in range(1, N_DEV):
            acc = acc + a_ref[d]
        o_ref[...] = acc

    return pl.pallas_call(body, name=name, out_shape=jax.ShapeDtypeStruct(a.shape[1:], F32))(a)


def adamw(w, g, m, v, name, others=None):
    rows, cols = w.shape
    rb = rows
    for cand in range(min(rows, 512), 7, -8):
        if rows % cand == 0 and cand % 8 == 0:
            rb = cand
            break

    def body(*refs):
        if others is None:
            w_ref, g_ref, m_ref, v_ref, d_ref, nm_ref, nv_ref = refs
            gg = g_ref[...]
        else:
            w_ref, g_ref, m_ref, v_ref, r2_ref, go_ref, d_ref, nm_ref, nv_ref = refs
            gg = g_ref[...]
            for k in range(3):
                gg = gg + r2_ref[k].astype(F32)
            go_ref[...] = gg
        nm = ADAM_B1 * m_ref[...] + (1.0 - ADAM_B1) * gg
        nv = ADAM_B2 * v_ref[...] + (1.0 - ADAM_B2) * (gg * gg)
        m_hat = nm / (1.0 - ADAM_B1 ** ADAM_STEP)
        v_hat = nv / (1.0 - ADAM_B2 ** ADAM_STEP)
        d_ref[...] = -ADAM_LR * (m_hat / (jnp.sqrt(v_hat) + ADAM_EPS) + ADAM_WD * w_ref[...])
        nm_ref[...] = nm
        nv_ref[...] = nv

    spec = pl.BlockSpec((rb, cols), lambda i: (i, 0))
    out = jax.ShapeDtypeStruct((rows, cols), F32)
    in_specs, args = [spec] * 4, [w, g, m, v]
    if others is not None:
        in_specs.append(pl.BlockSpec((3, rb, cols), lambda i: (0, i, 0)))
        args.append(others)
    n_out = 3 if others is None else 4
    res = pl.pallas_call(
        body, name=name, grid=(rows // rb,), in_specs=in_specs, out_specs=[spec] * n_out,
        out_shape=[out] * n_out, compiler_params=_params(("parallel",)),
    )(*args)
    return (g, *res) if others is None else tuple(res)


def ada_fwd(c_all, w_cols, b_cols, name):
    def body(c_ref, w_ref, b_ref, cond_ref, mod_ref):
        cc = c_ref[...]
        cond = (cc * _sigmoid(cc)).astype(BF16)
        cond_ref[...] = cond
        mod_ref[...] = _dot(cond, w_ref[...].astype(BF16)) + b_ref[...]

    n, cols = c_all.shape[0], w_cols.shape[1]
    return pl.pallas_call(
        body, name=name,
        out_shape=[jax.ShapeDtypeStruct(c_all.shape, BF16), jax.ShapeDtypeStruct((n, cols), F32)],
        compiler_params=_params(),
    )(c_all, w_cols, b_cols)


def ada_bwd(cond_all, dmod_cols, name):
    def body(c_ref, d_ref, gw_ref, gb_ref):
        d = d_ref[...]
        gw_ref[...] = _dot_tn(c_ref[...], d.astype(BF16))
        gb_ref[...] = jnp.sum(d, axis=0, keepdims=True)

    dm, cols = cond_all.shape[1], dmod_cols.shape[1]
    return pl.pallas_call(
        body, name=name,
        out_shape=[jax.ShapeDtypeStruct((dm, cols), F32), jax.ShapeDtypeStruct((1, cols), F32)],
        compiler_params=_params(),
    )(cond_all, dmod_cols)


def _mod_spec(tiles_per_seq, dm):
    return pl.BlockSpec((1, 1, dm), lambda i: (i // tiles_per_seq, 0, 0))


def ffn_fwd(x, sh, sc, gt, wgu, wd, ln_g, ln_b, seq, name, target=None, job=None):
    tokens, dm = x.shape
    fc = wgu.shape[1]
    tm = min(FFN_FWD_TILE, seq)
    tiles_per_seq = seq // tm
    with_loss = target is not None

    def body(*refs):
        if with_loss:
            (x_ref, sh_ref, sc_ref, gt_ref, wgu_ref, wd_ref, lg_ref, lb_ref, t_ref,
             xo_ref, loss_ref, r_ref, gu_ref, f_ref) = refs
        else:
            (x_ref, sh_ref, sc_ref, gt_ref, wgu_ref, wd_ref, lg_ref, lb_ref,
             xo_ref, r_ref, gu_ref, f_ref) = refs
        xx = x_ref[...]
        h = (xx * (1.0 + sc_ref[0]) + sh_ref[0]).astype(BF16)
        acc = jnp.zeros((tm, dm), F32)
        for k in range(4):
            gk = _dot_nt(h, wgu_ref[k])
            uk = _dot_nt(h, wgu_ref[k + 4])
            gu_ref[k] = gk.astype(BF16)
            gu_ref[k + 4] = uk.astype(BF16)
            a = (gk * _sigmoid(gk) * uk).astype(BF16)
            acc = acc + _dot(a, wd_ref[k])
        f_ref[...] = acc.astype(BF16)
        r = DN_ALPHA * xx + (0.5 * (1.0 + gt_ref[0])) * acc
        r_ref[...] = r
        xhat, _ = _ln_stats(r)
        yy = xhat * lg_ref[...] + lb_ref[...]
        if with_loss:
            err = yy - t_ref[...]
            xo_ref[...] = err * (1.0 / dm)

            @pl.when(pl.program_id(0) == 0)
            def _():
                loss_ref[...] = jnp.zeros_like(loss_ref)

            loss_ref[...] += jnp.full((1, 128), (0.5 / dm) * jnp.sum(err * err), F32)
        else:
            xo_ref[...] = yy

    tile = pl.BlockSpec((tm, dm), lambda i: (i, 0))
    mod = _mod_spec(tiles_per_seq, dm)
    in_specs = [tile, mod, mod, mod, _const_spec(wgu.shape), _const_spec(wd.shape),
                _const_spec((1, dm)), _const_spec((1, dm))]
    args = [x, sh, sc, gt, wgu, wd, ln_g, ln_b]
    out_specs = [tile]
    out_shape = [jax.ShapeDtypeStruct((tokens, dm), F32)]
    if with_loss:
        in_specs.append(tile)
        args.append(target)
        out_specs.append(pl.BlockSpec((1, 128), lambda i: (0, 0)))
        out_shape.append(jax.ShapeDtypeStruct((1, 128), F32))
    out_specs += [tile, pl.BlockSpec((8, tm, fc), lambda i: (0, i, 0)), tile]
    out_shape += [jax.ShapeDtypeStruct((tokens, dm), F32), jax.ShapeDtypeStruct((8, tokens, fc), BF16),
                  jax.ShapeDtypeStruct((tokens, dm), BF16)]
    return _call(body, job, name=name, grid=(tokens // tm,), in_specs=in_specs, out_specs=out_specs,
                 out_shape=out_shape, args=args)


def ffn_bwd(dy, r, x, f, gu, sh, sc, gt, wgu, wd, ln_g, seq, name, job=None):
    tokens, dm = x.shape
    fc = wgu.shape[1]
    tm = min(TOKEN_TILE, seq)
    tiles_per_seq = seq // tm
    nseq = tokens // seq

    def body(dy_ref, r_ref, x_ref, f_ref, gu_ref, sh_ref, sc_ref, gt_ref, wgu_ref, wd_ref, lg_ref,
             dx_ref, dgu_ref, df_ref, a_ref, h_ref, dln_ref, dmod_ref):
        i = pl.program_id(0)
        dr, dgain, dbias = _ln_bwd(dy_ref[...], r_ref[...], lg_ref[...])

        @pl.when(i == 0)
        def _():
            dln_ref[...] = jnp.zeros_like(dln_ref)

        @pl.when(i % tiles_per_seq == 0)
        def _():
            dmod_ref[...] = jnp.zeros_like(dmod_ref)

        dln_ref[0:1, :] += dgain
        dln_ref[1:2, :] += dbias
        df32 = (0.5 * (1.0 + gt_ref[0])) * dr
        df = df32.astype(BF16)
        df_ref[...] = df
        dgate = jnp.sum(dr * (0.5 * f_ref[...].astype(F32)), axis=0, keepdims=True)
        xx = x_ref[...]
        one_sc = 1.0 + sc_ref[0]
        h = (xx * one_sc + sh_ref[0]).astype(BF16)
        h_ref[...] = h
        dh = jnp.zeros((tm, dm), F32)
        for k in range(4):
            da = _dot_nt(df, wd_ref[k])
            gk = gu_ref[k].astype(F32)
            uk = gu_ref[k + 4].astype(F32)
            sg = _sigmoid(gk)
            sil = gk * sg
            a_ref[k] = (sil * uk).astype(BF16)
            du = (da * sil).astype(BF16)
            dg = (da * uk * (sg * (1.0 + gk * (1.0 - sg)))).astype(BF16)
            dgu_ref[k] = dg
            dgu_ref[k + 4] = du
            dh = dh + _dot(dg, wgu_ref[k]) + _dot(du, wgu_ref[k + 4])
        dx_ref[...] = DN_ALPHA * dr + dh * one_sc
        dmod_ref[0, 0:1, :] += jnp.sum(dh, axis=0, keepdims=True)
        dmod_ref[0, 1:2, :] += jnp.sum(dh * xx, axis=0, keepdims=True)
        dmod_ref[0, 2:3, :] += dgate

    tile = pl.BlockSpec((tm, dm), lambda i: (i, 0))
    mod = _mod_spec(tiles_per_seq, dm)
    gu_spec = pl.BlockSpec((8, tm, fc), lambda i: (0, i, 0))
    return _call(
        body, job, name=name, grid=(tokens // tm,),
        in_specs=[tile, tile, tile, tile, gu_spec, mod, mod, mod, _const_spec(wgu.shape), _const_spec(wd.shape),
                  _const_spec((1, dm))],
        out_specs=[tile, gu_spec, tile, pl.BlockSpec((4, tm, fc), lambda i: (0, i, 0)), tile,
                   pl.BlockSpec((2, dm), lambda i: (0, 0)),
                   pl.BlockSpec((1, 3, dm), lambda i: (i // tiles_per_seq, 0, 0))],
        out_shape=[jax.ShapeDtypeStruct((tokens, dm), F32), jax.ShapeDtypeStruct((8, tokens, fc), BF16),
                   jax.ShapeDtypeStruct((tokens, dm), BF16), jax.ShapeDtypeStruct((4, tokens, fc), BF16),
                   jax.ShapeDtypeStruct((tokens, dm), BF16), jax.ShapeDtypeStruct((2, dm), F32),
                   jax.ShapeDtypeStruct((nseq, 3, dm), F32)],
        args=(dy, r, x, f, gu, sh, sc, gt, wgu, wd, ln_g))


def tn_matmul(a, b, name, job=None):
    na, tokens, kk = a.shape
    nb, _, cc = b.shape
    tt = tokens
    while 4 * tt * (kk + cc) + 8 * kk * cc > TN_VMEM_BUDGET and tt % 2 == 0 and tt > 256:
        tt //= 2
    steps = tokens // tt

    def body(a_ref, b_ref, o_ref, acc_ref):
        t = pl.program_id(2)

        @pl.when(t == 0)
        def _():
            acc_ref[...] = jnp.zeros_like(acc_ref)

        acc_ref[...] += _dot_tn(a_ref[0], b_ref[0])

        @pl.when(t == steps - 1)
        def _():
            o_ref[0, 0] = acc_ref[...].astype(BF16)

    return _call(
        body, job, name=name, grid=(na, nb, steps),
        in_specs=[pl.BlockSpec((1, tt, kk), lambda i, j, t: (i, t, 0)),
                  pl.BlockSpec((1, tt, cc), lambda i, j, t: (j, t, 0))],
        out_specs=[pl.BlockSpec((1, 1, kk, cc), lambda i, j, t: (i, j, 0, 0))],
        out_shape=[jax.ShapeDtypeStruct((na, nb, kk, cc), BF16)],
        scratch_shapes=[pltpu.VMEM((kk, cc), F32)], args=(a, b))


def proj_fwd(x1, sh, sc, w_in, seq, name, job=None):
    tokens, dm = x1.shape
    tm = min(TOKEN_TILE, seq)
    tiles_per_seq = seq // tm
    widths = [N_Q_HEADS * HEAD_DIM, N_KV_HEADS * HEAD_DIM, N_KV_HEADS * HEAD_DIM, 512, 512, 512]
    assert sum(widths) == w_in.shape[0]

    def body(x_ref, sh_ref, sc_ref, w_ref, *outs):
        h = (x_ref[...] * (1.0 + sc_ref[0]) + sh_ref[0]).astype(BF16)
        proj = _dot_nt(h, w_ref[...])
        at = 0
        for o_ref, wdt in zip(outs, widths):
            o_ref[...] = proj[:, at:at + wdt]
            at += wdt

    tile = pl.BlockSpec((tm, dm), lambda i: (i, 0))
    mod = _mod_spec(tiles_per_seq, dm)
    return _call(
        body, job, name=name, grid=(tokens // tm,),
        in_specs=[tile, mod, mod, _const_spec(w_in.shape)],
        out_specs=[pl.BlockSpec((tm, wdt), lambda i: (i, 0)) for wdt in widths],
        out_shape=[jax.ShapeDtypeStruct((tokens, wdt), F32) for wdt in widths],
        args=(x1, sh, sc, w_in))


LANES = 2 * HEAD_DIM


def _head_lane(shape):
    return lax.broadcasted_iota(jnp.int32, shape, 1) % HEAD_DIM


def _lane_half(shape):
    return lax.broadcasted_iota(jnp.int32, shape, 1) // HEAD_DIM


def _swap_rot(v):
    lane = _head_lane(v.shape)
    half = ROT_DIM // 2
    return jnp.where(lane < half, pltpu.roll(v, LANES - half, 1),
                     jnp.where(lane < ROT_DIM, pltpu.roll(v, half, 1), 0.0))


def _rope(v, cos_t, sin_t):
    return v * cos_t + _swap_rot(v) * sin_t


def _unrope(dv, cos_t, sin_t):
    return dv * cos_t + _swap_rot(dv * sin_t)


def _both_halves(t, g):
    return jnp.where(_lane_half(t.shape) == g, t, pltpu.roll(t, HEAD_DIM, 1))


def _fold_halves(t, g):
    return jnp.where(_lane_half(t.shape) == g, t + pltpu.roll(t, HEAD_DIM, 1), 0.0)


def _stack_heads(blocks):
    rows = []
    for blk in blocks:
        half = _lane_half(blk.shape)
        rows += [jnp.where(half == 0, blk, 0.0), jnp.where(half == 1, blk, 0.0)]
    return jnp.concatenate(rows, axis=0)


def _unstack_heads(t, j):
    lo = t[(2 * j) * ATTN_BLOCK:(2 * j + 1) * ATTN_BLOCK]
    hi = t[(2 * j + 1) * ATTN_BLOCK:(2 * j + 2) * ATTN_BLOCK]
    return jnp.where(_lane_half(lo.shape) == 0, lo, hi)


def _band_mask(q0, w0):
    rows, cols = GQA_GROUP * ATTN_BLOCK, 2 * ATTN_BLOCK
    qi = lax.broadcasted_iota(jnp.int32, (rows, cols), 0) % ATTN_BLOCK + q0
    ki = lax.broadcasted_iota(jnp.int32, (rows, cols), 1) + w0
    diff = qi - ki
    return (diff >= 0) & (diff < ATTN_BLOCK)


def _attn_specs(seq):
    q_spec = pl.BlockSpec((seq, GQA_GROUP * HEAD_DIM), lambda b, g: (b, g))
    kv_spec = pl.BlockSpec((seq, LANES), lambda b, g: (b, 0))
    sink_spec = pl.BlockSpec((1, GQA_GROUP * ATTN_BLOCK, 1), lambda b, g: (g, 0, 0))
    return q_spec, kv_spec, sink_spec


def _block_starts(n):
    q0 = pl.multiple_of(n * ATTN_BLOCK, ATTN_BLOCK)
    w0 = pl.multiple_of(jnp.maximum(n - 1, 0) * ATTN_BLOCK, ATTN_BLOCK)
    return q0, w0


def _stacked_queries(ref, rows, cos_b=None, sin_b=None):
    blocks = []
    for j in range(2):
        blk = ref[rows, j * LANES:(j + 1) * LANES]
        blocks.append(blk if cos_b is None else _rope(blk, cos_b, sin_b))
    return _stack_heads(blocks).astype(BF16)


def _sink_columns(sinks):
    return jnp.repeat(sinks.reshape(N_KV_HEADS, GQA_GROUP), ATTN_BLOCK, axis=1)[:, :, None]


def attn_fwd(q, k, v, cos_t, sin_t, sinks, seq, name, job=None):
    tokens = q.shape[0]
    nblk = seq // ATTN_BLOCK
    assert nblk >= 2
    scale = HEAD_DIM ** -0.5

    def body(q_ref, k_ref, v_ref, cos_ref, sin_ref, sink_ref, o_ref, kd_ref, vd_ref):
        g = pl.program_id(1)
        kd_ref[...] = _both_halves(_rope(k_ref[...], cos_ref[...], sin_ref[...]), g).astype(BF16)
        vd_ref[...] = _both_halves(v_ref[...], g).astype(BF16)
        sink = sink_ref[0]

        def block(n, carry):
            q0, w0 = _block_starts(n)
            rows, win = pl.ds(q0, ATTN_BLOCK), pl.ds(w0, 2 * ATTN_BLOCK)
            qs = _stacked_queries(q_ref, rows, cos_ref[rows, :], sin_ref[rows, :])
            s = _dot_nt(qs, kd_ref[win, :]) * scale
            s = jnp.where(_band_mask(q0, w0), s, NEG_BIG)
            m = jnp.maximum(jnp.max(s, axis=-1, keepdims=True), sink)
            p = jnp.exp(s - m)
            denom = jnp.sum(p, axis=-1, keepdims=True) + jnp.exp(sink - m)
            out = _dot((p * pl.reciprocal(denom, approx=True)).astype(BF16), vd_ref[win, :])
            for j in range(2):
                o_ref[rows, j * LANES:(j + 1) * LANES] = _unstack_heads(out, j).astype(o_ref.dtype)
            return carry

        lax.fori_loop(0, nblk, block, 0, unroll=2)

    q_spec, kv_spec, sink_spec = _attn_specs(seq)
    return _call(
        body, job, name=name, grid=(tokens // seq, N_KV_HEADS),
        in_specs=[q_spec, kv_spec, kv_spec, kv_spec, kv_spec, sink_spec],
        out_specs=[q_spec], out_shape=[jax.ShapeDtypeStruct(q.shape, BF16)],
        scratch_shapes=[pltpu.VMEM((seq, LANES), BF16), pltpu.VMEM((seq, LANES), BF16)],
        args=(q, k, v, cos_t, sin_t, _sink_columns(sinks)))


def attn_bwd(q, k, v, do, cos_t, sin_t, sinks, seq, name, job=None):
    tokens = q.shape[0]
    nseq = tokens // seq
    nblk = seq // ATTN_BLOCK
    assert nblk >= 2
    rows_stacked = GQA_GROUP * ATTN_BLOCK
    scale = HEAD_DIM ** -0.5

    def body(q_ref, k_ref, v_ref, do_ref, cos_ref, sin_ref, sink_ref, dq_ref, dk_ref, dv_ref, ds_ref,
             kd_ref, vd_ref, dkd_ref, dvd_ref):
        g = pl.program_id(1)
        kd_ref[...] = _both_halves(_rope(k_ref[...], cos_ref[...], sin_ref[...]), g).astype(BF16)
        vd_ref[...] = _both_halves(v_ref[...], g).astype(BF16)
        dkd_ref[...] = jnp.zeros_like(dkd_ref)
        dvd_ref[...] = jnp.zeros_like(dvd_ref)
        sink = sink_ref[0]

        def block(n, dsink):
            q0, w0 = _block_starts(n)
            rows, win = pl.ds(q0, ATTN_BLOCK), pl.ds(w0, 2 * ATTN_BLOCK)
            cos_b, sin_b = cos_ref[rows, :], sin_ref[rows, :]
            qs = _stacked_queries(q_ref, rows, cos_b, sin_b)
            dos = _stacked_queries(do_ref, rows)
            kw, vw = kd_ref[win, :], vd_ref[win, :]
            s = _dot_nt(qs, kw) * scale
            s = jnp.where(_band_mask(q0, w0), s, NEG_BIG)
            m = jnp.maximum(jnp.max(s, axis=-1, keepdims=True), sink)
            p = jnp.exp(s - m)
            e_sink = jnp.exp(sink - m)
            inv = pl.reciprocal(jnp.sum(p, axis=-1, keepdims=True) + e_sink, approx=True)
            pn = p * inv
            dvd_ref[win, :] += _dot_tn(pn.astype(BF16), dos)
            dp = _dot_nt(dos, vw)
            delta = jnp.sum(dp * pn, axis=-1, keepdims=True)
            ds = (pn * (dp - delta)).astype(BF16)
            dqs = _dot(ds, kw) * scale
            dkd_ref[win, :] += _dot_tn(ds, qs) * scale
            for j in range(2):
                dq_ref[rows, j * LANES:(j + 1) * LANES] = _unrope(_unstack_heads(dqs, j), cos_b, sin_b).astype(BF16)
            return dsink - e_sink * inv * delta

        ds_ref[0, 0] = lax.fori_loop(0, nblk // 2, lambda i, acc: block(2 * i + 1, block(2 * i, acc)),
                                     jnp.zeros((rows_stacked, 1), F32))
        dk_g = _unrope(_fold_halves(dkd_ref[...], g), cos_ref[...], sin_ref[...])
        dv_g = _fold_halves(dvd_ref[...], g)

        @pl.when(g == 0)
        def _():
            dk_ref[...] = dk_g
            dv_ref[...] = dv_g

        @pl.when(g != 0)
        def _():
            dk_ref[...] += dk_g
            dv_ref[...] += dv_g

    q_spec, kv_spec, sink_spec = _attn_specs(seq)
    return _call(
        body, job, name=name, grid=(nseq, N_KV_HEADS),
        in_specs=[q_spec, kv_spec, kv_spec, q_spec, kv_spec, kv_spec, sink_spec],
        out_specs=[q_spec, kv_spec, kv_spec, pl.BlockSpec((1, 1, rows_stacked, 1), lambda b, g: (b, g, 0, 0))],
        out_shape=[jax.ShapeDtypeStruct(q.shape, BF16), jax.ShapeDtypeStruct(k.shape, F32),
                   jax.ShapeDtypeStruct(k.shape, F32), jax.ShapeDtypeStruct((nseq, N_KV_HEADS, rows_stacked, 1), F32)],
        scratch_shapes=[pltpu.VMEM((seq, LANES), BF16), pltpu.VMEM((seq, LANES), BF16),
                        pltpu.VMEM((seq, LANES), F32), pltpu.VMEM((seq, LANES), F32)],
        args=(q, k, v, do, cos_t, sin_t, _sink_columns(sinks)))


CONV_COLS = 128


def _shift_down(z, by):
    t = lax.broadcasted_iota(jnp.int32, z.shape, 0)
    return jnp.where(t >= by, pltpu.roll(z, by, 0), 0.0)


def _shift_up(z, by):
    n = z.shape[0]
    t = lax.broadcasted_iota(jnp.int32, z.shape, 0)
    return jnp.where(t < n - by, pltpu.roll(z, n - by, 0), 0.0)


def conv_fwd(u, bg, cg, conv_w, seq, name):
    tokens, width = u.shape

    def body(u_ref, bg_ref, cg_ref, w_ref, o_ref):
        z = cg_ref[...] * u_ref[...]
        yy = w_ref[2:3, :] * z + w_ref[1:2, :] * _shift_down(z, 1) + w_ref[0:1, :] * _shift_down(z, 2)
        o_ref[...] = (bg_ref[...] * yy).astype(BF16)

    col = pl.BlockSpec((seq, CONV_COLS), lambda j, b: (b, j))
    return pl.pallas_call(
        body, name=name, grid=(width // CONV_COLS, tokens // seq),
        in_specs=[col, col, col, pl.BlockSpec((CONV_TAPS, CONV_COLS), lambda j, b: (0, j))],
        out_specs=col, out_shape=jax.ShapeDtypeStruct((tokens, width), BF16),
        compiler_params=_params(("parallel", "parallel")),
    )(u, bg, cg, conv_w)


def conv_bwd(dout, u, bg, cg, conv_w, seq, name):
    tokens, width = u.shape

    def body(do_ref, u_ref, bg_ref, cg_ref, w_ref, du_ref, dbg_ref, dcg_ref, dw_ref):
        uu, cg_v, do = u_ref[...], cg_ref[...], do_ref[...].astype(F32)
        z = cg_v * uu
        z1, z2 = _shift_down(z, 1), _shift_down(z, 2)
        yy = w_ref[2:3, :] * z + w_ref[1:2, :] * z1 + w_ref[0:1, :] * z2
        dbg_ref[...] = (do * yy).astype(BF16)
        dyy = do * bg_ref[...]
        dz = w_ref[2:3, :] * dyy + w_ref[1:2, :] * _shift_up(dyy, 1) + w_ref[0:1, :] * _shift_up(dyy, 2)
        du_ref[...] = (dz * cg_v).astype(BF16)
        dcg_ref[...] = (dz * uu).astype(BF16)

        @pl.when(pl.program_id(1) == 0)
        def _():
            dw_ref[...] = jnp.zeros_like(dw_ref)

        dw_ref[0:1, :] += jnp.sum(dyy * z2, axis=0, keepdims=True)
        dw_ref[1:2, :] += jnp.sum(dyy * z1, axis=0, keepdims=True)
        dw_ref[2:3, :] += jnp.sum(dyy * z, axis=0, keepdims=True)

    col = pl.BlockSpec((seq, CONV_COLS), lambda j, b: (b, j))
    w_spec = pl.BlockSpec((CONV_TAPS, CONV_COLS), lambda j, b: (0, j))
    act = jax.ShapeDtypeStruct((tokens, width), BF16)
    return pl.pallas_call(
        body, name=name, grid=(width // CONV_COLS, tokens // seq),
        in_specs=[col, col, col, col, w_spec], out_specs=[col, col, col, w_spec],
        out_shape=[act, act, act, jax.ShapeDtypeStruct((CONV_TAPS, width), F32)],
        compiler_params=_params(("parallel", "arbitrary")),
    )(dout, u, bg, cg, conv_w)


def out_fwd(x1, attn, conv, gt, w_out, ln_g, ln_b, seq, name, job=None):
    tokens, dm = x1.shape
    half = attn.shape[1]
    tm = min(TOKEN_TILE, seq)
    tiles_per_seq = seq // tm

    def body(x_ref, a_ref, c_ref, gt_ref, w_ref, lg_ref, lb_ref, xo_ref, r_ref, mi_ref, mix_ref):
        mixin = jnp.concatenate([a_ref[...], c_ref[...]], axis=1).astype(BF16)
        mi_ref[...] = mixin
        mix = _dot(mixin, w_ref[...])
        mix_ref[...] = mix.astype(BF16)
        r = DN_ALPHA * x_ref[...] + (1.0 + gt_ref[0]) * mix
        r_ref[...] = r
        xhat, _ = _ln_stats(r)
        xo_ref[...] = xhat * lg_ref[...] + lb_ref[...]

    tile = pl.BlockSpec((tm, dm), lambda i: (i, 0))
    htile = pl.BlockSpec((tm, half), lambda i: (i, 0))
    return _call(
        body, job, name=name, grid=(tokens // tm,),
        in_specs=[tile, htile, htile, _mod_spec(tiles_per_seq, dm), _const_spec(w_out.shape),
                  _const_spec((1, dm)), _const_spec((1, dm))],
        out_specs=[tile, tile, tile, tile],
        out_shape=[jax.ShapeDtypeStruct((tokens, dm), F32), jax.ShapeDtypeStruct((tokens, dm), F32),
                   jax.ShapeDtypeStruct((tokens, dm), BF16), jax.ShapeDtypeStruct((tokens, dm), BF16)],
        args=(x1, attn, conv, gt, w_out, ln_g, ln_b))


def out_bwd(dy, r, mix, gt, w_out, ln_g, seq, name, job=None):
    tokens, dm = r.shape
    half = dm // 2
    tm = min(TOKEN_TILE, seq)
    tiles_per_seq = seq // tm
    nseq = tokens // seq

    def body(dy_ref, r_ref, mix_ref, gt_ref, w_ref, lg_ref, dres_ref, da_ref, dc_ref, dmix_ref, dln_ref, dgt_ref):
        i = pl.program_id(0)
        dr, dgain, dbias = _ln_bwd(dy_ref[...], r_ref[...], lg_ref[...])

        @pl.when(i == 0)
        def _():
            dln_ref[...] = jnp.zeros_like(dln_ref)

        @pl.when(i % tiles_per_seq == 0)
        def _():
            dgt_ref[...] = jnp.zeros_like(dgt_ref)

        dln_ref[0:1, :] += dgain
        dln_ref[1:2, :] += dbias
        dgt_ref[0] += jnp.sum(dr * mix_ref[...].astype(F32), axis=0, keepdims=True)
        dres_ref[...] = DN_ALPHA * dr
        dmix = ((1.0 + gt_ref[0]) * dr).astype(BF16)
        dmix_ref[...] = dmix
        dmixin = _dot_nt(dmix, w_ref[...])
        da_ref[...] = dmixin[:, :half].astype(BF16)
        dc_ref[...] = dmixin[:, half:].astype(BF16)

    tile = pl.BlockSpec((tm, dm), lambda i: (i, 0))
    htile = pl.BlockSpec((tm, half), lambda i: (i, 0))
    return _call(
        body, job, name=name, grid=(tokens // tm,),
        in_specs=[tile, tile, tile, _mod_spec(tiles_per_seq, dm), _const_spec(w_out.shape), _const_spec((1, dm))],
        out_specs=[tile, htile, htile, tile, pl.BlockSpec((2, dm), lambda i: (0, 0)),
                   pl.BlockSpec((1, 1, dm), lambda i: (i // tiles_per_seq, 0, 0))],
        out_shape=[jax.ShapeDtypeStruct((tokens, dm), F32), jax.ShapeDtypeStruct((tokens, half), BF16),
                   jax.ShapeDtypeStruct((tokens, half), BF16), jax.ShapeDtypeStruct((tokens, dm), BF16),
                   jax.ShapeDtypeStruct((2, dm), F32), jax.ShapeDtypeStruct((nseq, 1, dm), F32)],
        args=(dy, r, mix, gt, w_out, ln_g))


def proj_bwd(parts, dres, x1, sh, sc, w_in, seq, name):
    tokens, dm = x1.shape
    tm = min(TOKEN_TILE, seq)
    tiles_per_seq = seq // tm
    nseq = tokens // seq
    widths = [p.shape[1] for p in parts]
    total = sum(widths)

    def body(*refs):
        part_refs = refs[:6]
        dres_ref, x_ref, sh_ref, sc_ref, w_ref, dx_ref, dproj_ref, h_ref, dmod_ref = refs[6:]
        dproj = jnp.concatenate([p[...].astype(BF16) for p in part_refs], axis=1)
        dproj_ref[...] = dproj
        dh = _dot(dproj, w_ref[...])
        xx = x_ref[...]
        one_sc = 1.0 + sc_ref[0]
        h_ref[...] = (xx * one_sc + sh_ref[0]).astype(BF16)
        dx_ref[...] = dres_ref[...] + dh * one_sc

        @pl.when(pl.program_id(0) % tiles_per_seq == 0)
        def _():
            dmod_ref[...] = jnp.zeros_like(dmod_ref)

        dmod_ref[0, 0:1, :] += jnp.sum(dh, axis=0, keepdims=True)
        dmod_ref[0, 1:2, :] += jnp.sum(dh * xx, axis=0, keepdims=True)

    tile = pl.BlockSpec((tm, dm), lambda i: (i, 0))
    mod = _mod_spec(tiles_per_seq, dm)
    return pl.pallas_call(
        body, name=name, grid=(tokens // tm,),
        in_specs=[pl.BlockSpec((tm, wdt), lambda i: (i, 0)) for wdt in widths]
        + [tile, tile, mod, mod, _const_spec(w_in.shape)],
        out_specs=[tile, pl.BlockSpec((tm, total), lambda i: (i, 0)), tile,
                   pl.BlockSpec((1, 2, dm), lambda i: (i // tiles_per_seq, 0, 0))],
        out_shape=[jax.ShapeDtypeStruct((tokens, dm), F32), jax.ShapeDtypeStruct((tokens, total), BF16),
                   jax.ShapeDtypeStruct((tokens, dm), BF16), jax.ShapeDtypeStruct((nseq, 2, dm), F32)],
        compiler_params=_params(("arbitrary",)),
    )(*parts, dres, x1, sh, sc, w_in)


def _rope_tables(positions):
    half = ROT_DIM // 2
    inv_freq = jnp.power(jnp.float32(ROPE_THETA), -jnp.arange(0, ROT_DIM, 2, dtype=F32) / ROT_DIM)
    lane = jnp.arange(LANES) % HEAD_DIM
    freq = jnp.where(lane < ROT_DIM, inv_freq[lane % half], 0.0)
    sign = jnp.where(lane < half, -1.0, 1.0).astype(F32)
    ang = positions.astype(F32)[:, None] * freq[None, :]
    return jnp.cos(ang), sign[None, :] * jnp.sin(ang)


def kernel(x, c, positions, w_ada, b_ada, ffn1_w_gate_up, ffn1_w_down, ln1_g, ln1_b, w_in, conv_w, attn_sinks, w_out, ln2_g, ln2_b, ffn2_w_gate_up, ffn2_w_down, ln3_g, ln3_b, loss_target, m_w_ada, m_b_ada, m_ffn1_w_gate_up, m_ffn1_w_down, m_ln1_g, m_ln1_b, m_w_in, m_conv_w, m_attn_sinks, m_w_out, m_ln2_g, m_ln2_b, m_ffn2_w_gate_up, m_ffn2_w_down, m_ln3_g, m_ln3_b, v_w_ada, v_b_ada, v_ffn1_w_gate_up, v_ffn1_w_down, v_ln1_g, v_ln1_b, v_w_in, v_conv_w, v_attn_sinks, v_w_out, v_ln2_g, v_ln2_b, v_ffn2_w_gate_up, v_ffn2_w_down, v_ln3_g, v_ln3_b):
    nseq, seq, dm = x.shape
    tokens = nseq * seq
    dev = 4 * lax.axis_index("x") + 2 * lax.axis_index("y") + lax.axis_index("c")
    ada_cols = w_ada.shape[2]
    ff = ffn1_w_down.shape[1] * N_DEV
    fc = ff // 4
    in_cols = w_in.shape[2]
    conv_cols = conv_w.shape[2]

    def t_bf16(w):
        return w[0].T.astype(BF16)

    c_all, convw_all = all_gather([c, conv_w[0]], "gather_cond")
    wgu1, wd1 = all_gather([t_bf16(ffn1_w_gate_up), ffn1_w_down[0].astype(BF16)], "gather_ffn1")
    c_all = c_all.reshape(N_DEV * nseq, dm)
    convw_full = convw_all.transpose(1, 0, 2).reshape(CONV_TAPS, N_DEV * conv_cols)
    wd1 = wd1.reshape(4, fc, dm)

    b_cols = lax.dynamic_slice(b_ada, (0, dev * ada_cols), (1, ada_cols))
    cond_all, mod_cols = ada_fwd(c_all, w_ada[0], b_cols, "ada_fwd")
    (mod_all,) = all_gather([mod_cols], "gather_mod")
    mod = lax.dynamic_slice(mod_all, (0, dev * nseq, 0), (N_DEV, nseq, ada_cols))
    mod = mod.transpose(1, 0, 2).reshape(nseq, 9, 1, dm)
    sh1, sc1, g1, sh2, sc2, g2, sh3, sc3, g3 = [mod[:, i] for i in range(9)]

    x0 = x.reshape(tokens, dm)
    spread = gather_spread_job([t_bf16(w_in), w_out[0].astype(BF16), ffn2_w_down[0].astype(BF16)])
    (x1, r1, gu1, f1), spread = ffn_fwd(x0, sh1, sc1, g1, wgu1, wd1, ln1_g, ln1_b, seq, "ffn1_fwd", job=spread)
    win, wout = run_job(gather_forward_job(spread[:2]), "gather_mix_forward")
    win = win.reshape(N_DEV * in_cols, dm)
    wout = wout.reshape(dm, dm)
    (q, k, v, u, bg, cg), (wd2,) = proj_fwd(x1, sh2, sc2, win, seq, "proj_fwd", job=gather_forward_job(spread[2:]))
    wd2 = wd2.reshape(4, fc, dm)
    cos_t, sin_t = _rope_tables(positions.reshape(tokens))
    sinks = attn_sinks[0]
    (attn,), spread = attn_fwd(q, k, v, cos_t, sin_t, sinks, seq, "attn_fwd",
                               job=gather_spread_job([t_bf16(ffn2_w_gate_up)]))
    conv = conv_fwd(u, bg, cg, convw_full, seq, "conv_fwd")
    (x2, r2, mixin, mix), (wgu2,) = out_fwd(x1, attn, conv, g2, wout, ln2_g, ln2_b, seq, "out_fwd",
                                            job=gather_forward_job(spread))
    target = loss_target.reshape(tokens, dm)
    (dy3, loss_part, r3, gu3, f3), _ = ffn_fwd(x2, sh3, sc3, g3, wgu2, wd2, ln3_g, ln3_b, seq, "ffn2_fwd", target=target)

    (dx2, dgu3, df3, a3, h3, dln3, dmod3), _ = ffn_bwd(dy3, r3, x2, f3, gu3, sh3, sc3, g3, wgu2, wd2, ln3_g, seq, "ffn2_bwd")
    g_wd2 = tn_matmul(a3, df3[None], "ffn2_dwd")[0][0].reshape(N_DEV, ff // N_DEV, dm)
    g_wgu2 = tn_matmul(dgu3, h3[None], "ffn2_dwgu")[0][0].reshape(N_DEV, fc, dm)
    (dres2, dattn, dconv, dmix, dln2, dg2), swapped = out_bwd(dx2, r2, mix, g2, wout, ln2_g, seq, "out_bwd",
                                                              job=swap_job([g_wgu2, g_wd2]))
    p_wgu2, own_wgu2 = pair_sum(g_wgu2, swapped[0], "pair_wgu2")
    p_wd2, own_wd2 = pair_sum(g_wd2, swapped[1], "pair_wd2")
    g_wout = tn_matmul(mixin[None], dmix[None], "dwout")[0][0].reshape(N_DEV, dm // N_DEV, dm)
    du, dbg, dcg, dconvw = conv_bwd(dconv, u, bg, cg, convw_full, seq, "conv_bwd")
    (dq, dk, dv, dsink_rows), (far_wgu2, far_wd2) = attn_bwd(
        q, k, v, dattn, cos_t, sin_t, sinks, seq, "attn_bwd", job=chip_exchange_job([p_wgu2, p_wd2]))
    parts = [dq, dk, dv, du, dbg, dcg]
    dx1, dproj, h2, dmod2 = proj_bwd(parts, dres2, x1, sh2, sc2, win, seq, "proj_bwd")
    g_win = tn_matmul(dproj[None], h2[None], "dwin")[0][0].reshape(N_DEV, in_cols, dm)
    (dx0, dgu1, df1, a1, h1, dln1, dmod1), _ = ffn_bwd(
        dx1, r1, x0, f1, gu1, sh1, sc1, g1, wgu1, wd1, ln1_g, seq, "ffn1_bwd")
    dmod = jnp.concatenate([dmod1, dmod2, dg2, dmod3], axis=1).reshape(nseq, 9 * dm)
    jobs = _Jobs([swap_job([g_wout, g_win]), gather_spread_job([dmod])])
    (g_wd1,), res = tn_matmul(a1, df1[None], "ffn1_dwd", job=jobs)
    swapped, dmod_spread = jobs.split(res)
    g_wd1 = g_wd1.reshape(N_DEV, ff // N_DEV, dm)
    p_wout, own_wout = pair_sum(g_wout, swapped[0], "pair_wout")
    p_win, own_win = pair_sum(g_win, swapped[1], "pair_win")
    jobs = _Jobs([chip_exchange_job([p_wout, p_win]), gather_forward_job(dmod_spread)])
    (g_wgu1,), res = tn_matmul(dgu1, h1[None], "ffn1_dwgu", job=jobs)
    (far_wout, far_win), (dmod_all,) = jobs.split(res)
    g_wgu1 = g_wgu1.reshape(N_DEV, fc, dm)

    grads = {
        "ffn1_w_gate_up": reduce_scatter(g_wgu1, "rs_wgu1"), "ffn1_w_down": reduce_scatter(g_wd1, "rs_wd1"),
        "w_in": own_win, "w_out": own_wout, "ffn2_w_gate_up": own_wgu2, "ffn2_w_down": own_wd2,
    }
    others = {"w_in": far_win, "w_out": far_wout, "ffn2_w_gate_up": far_wgu2, "ffn2_w_down": far_wd2}

    dmod_cols = lax.dynamic_slice(dmod_all.reshape(N_DEV * nseq, 9 * dm), (0, dev * ada_cols), (N_DEV * nseq, ada_cols))
    grads["w_ada"], gb_cols = ada_bwd(cond_all, dmod_cols, "ada_bwd")

    dsinks = jnp.sum(dsink_rows.reshape(nseq, N_Q_HEADS, ATTN_BLOCK), axis=(0, 2))
    small = jnp.zeros((8, dm), F32)
    small = small.at[0:2].set(dln1).at[2:4].set(dln2).at[4:6].set(dln3)
    small = small.at[6, 0:N_Q_HEADS].set(dsinks).at[7, 0].set(loss_part[0, 0])
    small_all, dconvw_all, gb_all = all_gather([small, dconvw, gb_cols], "gather_small")
    small_sum = sum_devices(small_all, "sum_small")
    dconvw_sum = sum_devices(dconvw_all, "sum_convw")
    loss = small_sum[7, 0]
    grads["b_ada"] = gb_all.reshape(1, N_DEV * ada_cols)
    grads["conv_w"] = lax.dynamic_slice(dconvw_sum, (0, dev * conv_cols), (CONV_TAPS, conv_cols))
    grads["attn_sinks"] = small_sum[6:7, 0:N_Q_HEADS]
    for i, nm in enumerate(["ln1_g", "ln1_b", "ln2_g", "ln2_b", "ln3_g", "ln3_b"]):
        grads[nm] = small_sum[i:i + 1]

    given = dict(w_ada=(w_ada, m_w_ada, v_w_ada), b_ada=(b_ada, m_b_ada, v_b_ada),
                 ffn1_w_gate_up=(ffn1_w_gate_up, m_ffn1_w_gate_up, v_ffn1_w_gate_up),
                 ffn1_w_down=(ffn1_w_down, m_ffn1_w_down, v_ffn1_w_down),
                 ln1_g=(ln1_g, m_ln1_g, v_ln1_g), ln1_b=(ln1_b, m_ln1_b, v_ln1_b),
                 w_in=(w_in, m_w_in, v_w_in), conv_w=(conv_w, m_conv_w, v_conv_w),
                 attn_sinks=(attn_sinks, m_attn_sinks, v_attn_sinks), w_out=(w_out, m_w_out, v_w_out),
                 ln2_g=(ln2_g, m_ln2_g, v_ln2_g), ln2_b=(ln2_b, m_ln2_b, v_ln2_b),
                 ffn2_w_gate_up=(ffn2_w_gate_up, m_ffn2_w_gate_up, v_ffn2_w_gate_up),
                 ffn2_w_down=(ffn2_w_down, m_ffn2_w_down, v_ffn2_w_down),
                 ln3_g=(ln3_g, m_ln3_g, v_ln3_g), ln3_b=(ln3_b, m_ln3_b, v_ln3_b))
    order = ["w_ada", "b_ada", "ffn1_w_gate_up", "ffn1_w_down", "ln1_g", "ln1_b", "w_in", "conv_w", "attn_sinks",
             "w_out", "ln2_g", "ln2_b", "ffn2_w_gate_up", "ffn2_w_down", "ln3_g", "ln3_b"]
    transposed = ("ffn1_w_gate_up", "ffn2_w_gate_up", "w_in")
    out_g, out_d, out_m, out_v = [], [], [], []
    for nm in order:
        shape = given[nm][0].shape
        two_d = (shape[-2], shape[-1])
        if nm in transposed:
            w2, m2, v2 = [t[0].T for t in given[nm]]
            back = lambda t: t.T[None]
        else:
            w2, m2, v2 = [t.reshape(two_d) for t in given[nm]]
            back = lambda t, shape=shape: t.reshape(shape)
        res = adamw(w2, grads[nm].reshape(w2.shape), m2, v2, "adamw_" + nm, others=others.get(nm))
        for lst, t in zip((out_g, out_d, out_m, out_v), res):
            lst.append(back(t))
    grad_x = dx0.reshape(nseq, seq, dm)
    return (loss, grad_x, *out_g, *out_d, *out_m, *out_v)
```

```python
import functools

import jax
import jax.numpy as jnp
from jax import lax
from jax.experimental import pallas as pl
from jax.experimental.pallas import tpu as pltpu

F32 = jnp.float32
BF16 = jnp.bfloat16
MESH = pl.DeviceIdType.MESH

N_DEV = 8
N_CHIP = 4
HEAD_DIM = 64
N_Q_HEADS = 8
N_KV_HEADS = 2
GQA_GROUP = N_Q_HEADS // N_KV_HEADS
ATTN_BLOCK = 128
ROT_DIM = 16
ROPE_THETA = 500000.0
CONV_TAPS = 3
LN_EPS = 1e-5
DN_ALPHA = 2.0 ** 0.25
ADAM_LR = 0.001
ADAM_B1 = 0.9
ADAM_B2 = 0.999
ADAM_EPS = 1e-08
ADAM_WD = 0.01
ADAM_STEP = 10
NEG_BIG = -1e30

VMEM_LIMIT = 56 * 1024 * 1024
TOKEN_TILE = 256
FFN_FWD_TILE = 512
TN_VMEM_BUDGET = 36 * 1024 * 1024


def _params(semantics=None, vmem=VMEM_LIMIT):
    return pltpu.CompilerParams(dimension_semantics=semantics, vmem_limit_bytes=vmem)


def _dot(a, b):
    return jnp.dot(a, b, preferred_element_type=F32)


def _dot_nt(a, b):
    return lax.dot_general(a, b, (((1,), (1,)), ((), ())), preferred_element_type=F32)


def _dot_tn(a, b):
    return lax.dot_general(a, b, (((0,), (0,)), ((), ())), preferred_element_type=F32)


def _sigmoid(x):
    return pl.reciprocal(1.0 + jnp.exp(-x), approx=True)


def _ln_stats(r):
    mu = jnp.mean(r, axis=-1, keepdims=True)
    d = r - mu
    var = jnp.mean(d * d, axis=-1, keepdims=True)
    rstd = lax.rsqrt(var + LN_EPS)
    return d * rstd, rstd


def _ln_bwd(dy, r, g):
    xhat, rstd = _ln_stats(r)
    dxhat = dy * g
    c1 = jnp.mean(dxhat, axis=-1, keepdims=True)
    c2 = jnp.mean(dxhat * xhat, axis=-1, keepdims=True)
    dr = rstd * (dxhat - c1 - xhat * c2)
    return dr, jnp.sum(dy * xhat, axis=0, keepdims=True), jnp.sum(dy, axis=0, keepdims=True)


def _const_spec(shape):
    nd = len(shape)
    return pl.BlockSpec(shape, lambda *_: (0,) * nd, pipeline_mode=pl.Buffered(1))


def all_gather(arrs, name):
    n = len(arrs)

    def body(*refs):
        ins, outs = refs[:n], refs[n:2 * n]
        send_sems, recv_sems, local_sems = refs[2 * n:]
        x, y, c = lax.axis_index("x"), lax.axis_index("y"), lax.axis_index("c")
        me, sibling = (x, y, c), (x, y, 1 - c)
        chips = [(1 - x, y), (x, 1 - y), (1 - x, 1 - y)]

        def slot(i, p):
            return outs[i].at[4 * p[0] + 2 * p[1] + p[2]]

        def copy(i, k, block, to, src=None):
            return pltpu.make_async_remote_copy(
                src_ref=slot(i, block) if src is None else src, dst_ref=slot(i, block),
                send_sem=send_sems.at[i, k], recv_sem=recv_sems.at[i, k],
                device_id=to, device_id_type=MESH)

        mine = [pltpu.make_async_copy(ins[i], slot(i, me), local_sems.at[i]) for i in range(n)]
        for cp in mine:
            cp.start()
        first = []
        for i in range(n):
            first.append(copy(i, 0, me, sibling, src=ins[i]))
            first += [copy(i, 1 + j, me, (*chip, c), src=ins[i]) for j, chip in enumerate(chips)]
        for cp in first:
            cp.start()
        passed = []
        for j, chip in enumerate(chips):
            for i in range(n):
                copy(i, 1 + j, (*chip, c), me).wait_recv()
                cp = copy(i, 4 + j, (*chip, c), sibling)
                cp.start()
                passed.append(cp)
        for i in range(n):
            copy(i, 0, sibling, me).wait_recv()
            for j, chip in enumerate(chips):
                copy(i, 4 + j, (*chip, 1 - c), me).wait_recv()
        for cp in first + passed:
            cp.wait_send()
        for cp in mine:
            cp.wait()

    any_spec = pl.BlockSpec(memory_space=pl.ANY)
    return pl.pallas_call(
        body, name=name,
        out_shape=[jax.ShapeDtypeStruct((N_DEV, *a.shape), a.dtype) for a in arrs],
        in_specs=[any_spec] * n, out_specs=[any_spec] * n,
        scratch_shapes=[pltpu.SemaphoreType.DMA((n, 7)), pltpu.SemaphoreType.DMA((n, 7)),
                        pltpu.SemaphoreType.DMA((n,))],
    )(*arrs)


RS_ROWS = 32


def reduce_scatter(g, name):
    _, rows, cols = g.shape
    nblk = rows // RS_ROWS
    assert nblk * RS_ROWS == rows

    def body(g_ref, out_ref, r1_ref, p_ref, r2_ref, send_sems, recv_sems):
        x, y, c = lax.axis_index("x"), lax.axis_index("y"), lax.axis_index("c")
        sibling = (x, y, 1 - c)
        q_me = 2 * x + y
        swaps = []
        for q in range(N_CHIP):
            cp = pltpu.make_async_remote_copy(
                src_ref=g_ref.at[2 * q + (1 - c)], dst_ref=r1_ref.at[q],
                send_sem=send_sems.at[q], recv_sem=recv_sems.at[q], device_id=sibling, device_id_type=MESH)
            cp.start()
            swaps.append(cp)
        for cp in swaps:
            cp.wait_recv()

        def pair_sum(i, carry):
            r = pl.ds(pl.multiple_of(i * RS_ROWS, RS_ROWS), RS_ROWS)
            for q in range(N_CHIP):
                p_ref[q, r, :] = (g_ref[2 * q + c, r, :].astype(F32) + r1_ref[q, r, :].astype(F32)).astype(BF16)
            return carry

        lax.fori_loop(0, nblk, pair_sum, 0)
        chips = [(1 - x, y), (x, 1 - y), (1 - x, 1 - y)]
        sends = []
        for k, chip in enumerate(chips):
            cp = pltpu.make_async_remote_copy(
                src_ref=p_ref.at[2 * chip[0] + chip[1]], dst_ref=r2_ref.at[k],
                send_sem=send_sems.at[N_CHIP + k], recv_sem=recv_sems.at[N_CHIP + k],
                device_id=(*chip, c), device_id_type=MESH)
            cp.start()
            sends.append(cp)
        for cp in sends:
            cp.wait_recv()

        def total(i, carry):
            r = pl.ds(pl.multiple_of(i * RS_ROWS, RS_ROWS), RS_ROWS)
            acc = g_ref[2 * q_me + c, r, :].astype(F32) + r1_ref[q_me, r, :].astype(F32)
            for k in range(3):
                acc = acc + r2_ref[k, r, :].astype(F32)
            out_ref[r, :] = acc
            return carry

        lax.fori_loop(0, nblk, total, 0)
        for cp in swaps + sends:
            cp.wait_send()

    vmem = pl.BlockSpec(memory_space=pltpu.VMEM)
    return pl.pallas_call(
        body, name=name,
        out_shape=jax.ShapeDtypeStruct((rows, cols), F32),
        in_specs=[vmem], out_specs=vmem,
        scratch_shapes=[pltpu.VMEM((N_CHIP, rows, cols), BF16), pltpu.VMEM((N_CHIP, rows, cols), BF16),
                        pltpu.VMEM((3, rows, cols), BF16),
                        pltpu.SemaphoreType.DMA((N_CHIP + 3,)), pltpu.SemaphoreType.DMA((N_CHIP + 3,))],
        compiler_params=_params(),
    )(g)


def _place():
    x, y, c = lax.axis_index("x"), lax.axis_index("y"), lax.axis_index("c")
    return x, y, c, [(1 - x, y), (x, 1 - y), (1 - x, 1 - y)]


def _slot(p):
    return 4 * p[0] + 2 * p[1] + p[2]


class _Job:
    def __init__(self, ins, outs, nsem, copies, aliases=None, local=None):
        self.ins, self.outs, self.nsem, self.copies = list(ins), list(outs), nsem, copies
        self.aliases = aliases or {}
        self.local = local

    def scratch(self):
        s = [pltpu.SemaphoreType.DMA(self.nsem), pltpu.SemaphoreType.DMA(self.nsem)]
        if self.local is not None:
            s.append(pltpu.SemaphoreType.DMA((len(self.ins),)))
        return s

    def start(self, ins, outs, sems):
        if self.local is not None:
            for cp in self.local(ins, outs, sems[2]):
                cp.start()
        for cp in self.copies(ins, outs, sems[0], sems[1])[0]:
            cp.start()

    def finish(self, ins, outs, sems):
        started, awaited = self.copies(ins, outs, sems[0], sems[1])
        for cp in awaited:
            cp.wait_recv()
        for cp in started:
            cp.wait_send()
        if self.local is not None:
            for cp in self.local(ins, outs, sems[2]):
                cp.wait()


class _Jobs:
    def __init__(self, jobs):
        self.jobs = jobs
        self.ins = [a for j in jobs for a in j.ins]
        self.outs = [o for j in jobs for o in j.outs]
        self.aliases = {}
        at_in = at_out = 0
        for j in jobs:
            self.aliases.update({at_in + i: at_out + o for i, o in j.aliases.items()})
            at_in, at_out = at_in + len(j.ins), at_out + len(j.outs)

    def scratch(self):
        return [s for j in self.jobs for s in j.scratch()]

    def _each(self, ins, outs, sems):
        at_in = at_out = at_sem = 0
        for j in self.jobs:
            n_in, n_out, n_sem = len(j.ins), len(j.outs), len(j.scratch())
            yield j, ins[at_in:at_in + n_in], outs[at_out:at_out + n_out], sems[at_sem:at_sem + n_sem]
            at_in, at_out, at_sem = at_in + n_in, at_out + n_out, at_sem + n_sem

    def start(self, ins, outs, sems):
        for j, i, o, s in self._each(ins, outs, sems):
            j.start(i, o, s)

    def finish(self, ins, outs, sems):
        for j, i, o, s in self._each(ins, outs, sems):
            j.finish(i, o, s)

    def split(self, results):
        at, parts = 0, []
        for j in self.jobs:
            parts.append(results[at:at + len(j.outs)])
            at += len(j.outs)
        return parts


def _remote(src, dst, send, recv, idx, to):
    return pltpu.make_async_remote_copy(src_ref=src, dst_ref=dst, send_sem=send.at[idx], recv_sem=recv.at[idx],
                                        device_id=to, device_id_type=MESH)


def gather_spread_job(shards):
    def copies(ins, outs, send, recv):
        x, y, c, chips = _place()
        me = (x, y, c)
        peers = [(x, y, 1 - c)] + [(*chip, c) for chip in chips]
        started, awaited = [], []
        for i, (src, dst) in enumerate(zip(ins, outs)):
            for k, peer in enumerate(peers):
                started.append(_remote(src, dst.at[_slot(me)], send, recv, (i, k), peer))
                awaited.append(_remote(src, dst.at[_slot(peer)], send, recv, (i, k), peer))
        return started, awaited

    def local(ins, outs, sems):
        x, y, c, _ = _place()
        return [pltpu.make_async_copy(src, dst.at[_slot((x, y, c))], sems.at[i])
                for i, (src, dst) in enumerate(zip(ins, outs))]

    outs = [jax.ShapeDtypeStruct((N_DEV, *a.shape), a.dtype) for a in shards]
    return _Job(shards, outs, (len(shards), 4), copies, local=local)


def gather_forward_job(fulls):
    def copies(ins, outs, send, recv):
        x, y, c, chips = _place()
        started, awaited = [], []
        for i, buf in enumerate(outs):
            for j, chip in enumerate(chips):
                mine, theirs = buf.at[_slot((*chip, c))], buf.at[_slot((*chip, 1 - c))]
                started.append(_remote(mine, mine, send, recv, (i, j), (x, y, 1 - c)))
                awaited.append(_remote(theirs, theirs, send, recv, (i, j), (x, y, 1 - c)))
        return started, awaited

    outs = [jax.ShapeDtypeStruct(a.shape, a.dtype) for a in fulls]
    return _Job(fulls, outs, (len(fulls), 3), copies, aliases={i: i for i in range(len(fulls))})


def swap_job(gs):
    def copies(ins, outs, send, recv):
        x, y, c, _ = _place()
        started, awaited = [], []
        for i, (g, r1) in enumerate(zip(ins, outs)):
            for q in range(N_CHIP):
                started.append(_remote(g.at[2 * q + (1 - c)], r1.at[q], send, recv, (i, q), (x, y, 1 - c)))
                awaited.append(_remote(g.at[2 * q + c], r1.at[q], send, recv, (i, q), (x, y, 1 - c)))
        return started, awaited

    outs = [jax.ShapeDtypeStruct((N_CHIP, *g.shape[1:]), g.dtype) for g in gs]
    return _Job(gs, outs, (len(gs), N_CHIP), copies)


def chip_exchange_job(ps):
    def copies(ins, outs, send, recv):
        x, y, c, chips = _place()
        started, awaited = [], []
        for i, (p, r2) in enumerate(zip(ins, outs)):
            for k, chip in enumerate(chips):
                started.append(_remote(p.at[2 * chip[0] + chip[1]], r2.at[k], send, recv, (i, k), (*chip, c)))
                awaited.append(_remote(p.at[2 * x + y], r2.at[k], send, recv, (i, k), (*chip, c)))
        return started, awaited

    outs = [jax.ShapeDtypeStruct((3, *p.shape[1:]), p.dtype) for p in ps]
    return _Job(ps, outs, (len(ps), 3), copies)


def _call(body, job, *, name, grid, in_specs, out_specs, out_shape, args, scratch_shapes=(), vmem=VMEM_LIMIT):
    if job is None:
        res = pl.pallas_call(
            body, name=name, grid=grid, in_specs=in_specs, out_specs=out_specs, out_shape=out_shape,
            scratch_shapes=list(scratch_shapes), compiler_params=_params(("arbitrary",) * len(grid), vmem),
        )(*args)
        return res, []
    n_in, n_out, n_scr = len(in_specs), len(out_specs), len(scratch_shapes)
    j_in, j_out = len(job.ins), len(job.outs)

    def with_copies(*refs):
        at = 0
        ins = refs[at:at + n_in]; at += n_in
        jins = refs[at:at + j_in]; at += j_in
        outs = refs[at:at + n_out]; at += n_out
        jouts = refs[at:at + j_out]; at += j_out
        scr = refs[at:at + n_scr]; at += n_scr
        sems = refs[at:]
        ids = [pl.program_id(d) for d in range(len(grid))]
        first = functools.reduce(jnp.logical_and, [i == 0 for i in ids])
        last = functools.reduce(jnp.logical_and, [i == n - 1 for i, n in zip(ids, grid)])

        @pl.when(first)
        def _():
            job.start(jins, jouts, sems)

        body(*ins, *outs, *scr)

        @pl.when(last)
        def _():
            job.finish(jins, jouts, sems)

    any_spec = pl.BlockSpec(memory_space=pl.ANY)
    res = pl.pallas_call(
        with_copies, name=name, grid=grid,
        in_specs=list(in_specs) + [any_spec] * j_in, out_specs=list(out_specs) + [any_spec] * j_out,
        out_shape=list(out_shape) + list(job.outs),
        input_output_aliases={n_in + i: n_out + o for i, o in job.aliases.items()},
        scratch_shapes=list(scratch_shapes) + job.scratch(),
        compiler_params=_params(("arbitrary",) * len(grid), vmem),
    )(*args, *job.ins)
    return res[:n_out], res[n_out:]


def run_job(job, name):
    def body(*refs):
        j_in, j_out = len(job.ins), len(job.outs)
        ins, outs, sems = refs[:j_in], refs[j_in:j_in + j_out], refs[j_in + j_out:]
        job.start(ins, outs, sems)
        job.finish(ins, outs, sems)

    any_spec = pl.BlockSpec(memory_space=pl.ANY)
    return pl.pallas_call(
        body, name=name, in_specs=[any_spec] * len(job.ins), out_specs=[any_spec] * len(job.outs),
        out_shape=list(job.outs), input_output_aliases=dict(job.aliases), scratch_shapes=job.scratch(),
    )(*job.ins)


def pair_sum(g, r1, name):
    _, rows, cols = g.shape
    rb = next(cand for cand in range(min(rows, 512), 0, -16) if rows % cand == 0)

    def body(g_ref, r1_ref, p_ref, own_ref):
        x, y, c, _ = _place()
        s = g_ref[c].astype(F32) + r1_ref[0].astype(F32)
        p_ref[0] = s.astype(BF16)

        @pl.when(pl.program_id(1) == 2 * x + y)
        def _():
            own_ref[...] = s

    return pl.pallas_call(
        body, name=name, grid=(rows // rb, N_CHIP),
        in_specs=[pl.BlockSpec((2, rb, cols), lambda i, q: (q, i, 0)), pl.BlockSpec((1, rb, cols), lambda i, q: (q, i, 0))],
        out_specs=[pl.BlockSpec((1, rb, cols), lambda i, q: (q, i, 0)), pl.BlockSpec((rb, cols), lambda i, q: (i, 0))],
        out_shape=[jax.ShapeDtypeStruct((N_CHIP, rows, cols), BF16), jax.ShapeDtypeStruct((rows, cols), F32)],
        compiler_params=_params(("arbitrary", "arbitrary")),
    )(g, r1)


def sum_devices(a, name):
    def body(a_ref, o_ref):
        acc = a_ref[0]
        for d in range(1, N_DEV):
            acc = acc + a_ref[d]
        o_ref[...] = acc

    return pl.pallas_call(body, name=name, out_shape=jax.ShapeDtypeStruct(a.shape[1:], F32))(a)


def adamw(w, g, m, v, name, others=None):
    rows, cols = w.shape
    rb = rows
    for cand in range(min(rows, 512), 7, -8):
        if rows % cand == 0 and cand % 8 == 0:
            rb = cand
            break

    def body(*refs):
        if others is None:
            w_ref, g_ref, m_ref, v_ref, d_ref, nm_ref, nv_ref = refs
            gg = g_ref[...]
        else:
            w_ref, g_ref, m_ref, v_ref, r2_ref, go_ref, d_ref, nm_ref, nv_ref = refs
            gg = g_ref[...]
            for k in range(3):
                gg = gg + r2_ref[k].astype(F32)
            go_ref[...] = gg
        nm = ADAM_B1 * m_ref[...] + (1.0 - ADAM_B1) * gg
        nv = ADAM_B2 * v_ref[...] + (1.0 - ADAM_B2) * (gg * gg)
        m_hat = nm / (1.0 - ADAM_B1 ** ADAM_STEP)
        v_hat = nv / (1.0 - ADAM_B2 ** ADAM_STEP)
        d_ref[...] = -ADAM_LR * (m_hat / (jnp.sqrt(v_hat) + ADAM_EPS) + ADAM_WD * w_ref[...])
        nm_ref[...] = nm
        nv_ref[...] = nv

    spec = pl.BlockSpec((rb, cols), lambda i: (i, 0))
    out = jax.ShapeDtypeStruct((rows, cols), F32)
    in_specs, args = [spec] * 4, [w, g, m, v]
    if others is not None:
        in_specs.append(pl.BlockSpec((3, rb, cols), lambda i: (0, i, 0)))
        args.append(others)
    n_out = 3 if others is None else 4
    res = pl.pallas_call(
        body, name=name, grid=(rows // rb,), in_specs=in_specs, out_specs=[spec] * n_out,
        out_shape=[out] * n_out, compiler_params=_params(("parallel",)),
    )(*args)
    return (g, *res) if others is None else tuple(res)


def ada_fwd(c_all, w_cols, b_cols, name):
    def body(c_ref, w_ref, b_ref, cond_ref, mod_ref):
        cc = c_ref[...]
        cond = (cc * _sigmoid(cc)).astype(BF16)
        cond_ref[...] = cond
        mod_ref[...] = _dot(cond, w_ref[...].astype(BF16)) + b_ref[...]

    n, cols = c_all.shape[0], w_cols.shape[1]
    return pl.pallas_call(
        body, name=name,
        out_shape=[jax.ShapeDtypeStruct(c_all.shape, BF16), jax.ShapeDtypeStruct((n, cols), F32)],
        compiler_params=_params(),
    )(c_all, w_cols, b_cols)


def ada_bwd(cond_all, dmod_cols, name):
    def body(c_ref, d_ref, gw_ref, gb_ref):
        d = d_ref[...]
        gw_ref[...] = _dot_tn(c_ref[...], d.astype(BF16))
        gb_ref[...] = jnp.sum(d, axis=0, keepdims=True)

    dm, cols = cond_all.shape[1], dmod_cols.shape[1]
    return pl.pallas_call(
        body, name=name,
        out_shape=[jax.ShapeDtypeStruct((dm, cols), F32), jax.ShapeDtypeStruct((1, cols), F32)],
        compiler_params=_params(),
    )(cond_all, dmod_cols)


def _mod_spec(tiles_per_seq, dm):
    return pl.BlockSpec((1, 1, dm), lambda i: (i // tiles_per_seq, 0, 0))


def ffn_fwd(x, sh, sc, gt, wgu, wd, ln_g, ln_b, seq, name, target=None, job=None):
    tokens, dm = x.shape
    fc = wgu.shape[1]
    tm = min(FFN_FWD_TILE, seq)
    tiles_per_seq = seq // tm
    with_loss = target is not None

    def body(*refs):
        if with_loss:
            (x_ref, sh_ref, sc_ref, gt_ref, wgu_ref, wd_ref, lg_ref, lb_ref, t_ref,
             xo_ref, loss_ref, r_ref, gu_ref, f_ref) = refs
        else:
            (x_ref, sh_ref, sc_ref, gt_ref, wgu_ref, wd_ref, lg_ref, lb_ref,
             xo_ref, r_ref, gu_ref, f_ref) = refs
        xx = x_ref[...]
        h = (xx * (1.0 + sc_ref[0]) + sh_ref[0]).astype(BF16)
        acc = jnp.zeros((tm, dm), F32)
        for k in range(4):
            gk = _dot_nt(h, wgu_ref[k])
            uk = _dot_nt(h, wgu_ref[k + 4])
            gu_ref[k] = gk.astype(BF16)
            gu_ref[k + 4] = uk.astype(BF16)
            a = (gk * _sigmoid(gk) * uk).astype(BF16)
            acc = acc + _dot(a, wd_ref[k])
        f_ref[...] = acc.astype(BF16)
        r = DN_ALPHA * xx + (0.5 * (1.0 + gt_ref[0])) * acc
        r_ref[...] = r
        xhat, _ = _ln_stats(r)
        yy = xhat * lg_ref[...] + lb_ref[...]
        if with_loss:
            err = yy - t_ref[...]
            xo_ref[...] = err * (1.0 / dm)

            @pl.when(pl.program_id(0) == 0)
            def _():
                loss_ref[...] = jnp.zeros_like(loss_ref)

            loss_ref[...] += jnp.full((1, 128), (0.5 / dm) * jnp.sum(err * err), F32)
        else:
            xo_ref[...] = yy

    tile = pl.BlockSpec((tm, dm), lambda i: (i, 0))
    mod = _mod_spec(tiles_per_seq, dm)
    in_specs = [tile, mod, mod, mod, _const_spec(wgu.shape), _const_spec(wd.shape),
                _const_spec((1, dm)), _const_spec((1, dm))]
    args = [x, sh, sc, gt, wgu, wd, ln_g, ln_b]
    out_specs = [tile]
    out_shape = [jax.ShapeDtypeStruct((tokens, dm), F32)]
    if with_loss:
        in_specs.append(tile)
        args.append(target)
        out_specs.append(pl.BlockSpec((1, 128), lambda i: (0, 0)))
        out_shape.append(jax.ShapeDtypeStruct((1, 128), F32))
    out_specs += [tile, pl.BlockSpec((8, tm, fc), lambda i: (0, i, 0)), tile]
    out_shape += [jax.ShapeDtypeStruct((tokens, dm), F32), jax.ShapeDtypeStruct((8, tokens, fc), BF16),
                  jax.ShapeDtypeStruct((tokens, dm), BF16)]
    return _call(body, job, name=name, grid=(tokens // tm,), in_specs=in_specs, out_specs=out_specs,
                 out_shape=out_shape, args=args)


def ffn_bwd(dy, r, x, f, gu, sh, sc, gt, wgu, wd, ln_g, seq, name, job=None):
    tokens, dm = x.shape
    fc = wgu.shape[1]
    tm = min(TOKEN_TILE, seq)
    tiles_per_seq = seq // tm
    nseq = tokens // seq

    def body(dy_ref, r_ref, x_ref, f_ref, gu_ref, sh_ref, sc_ref, gt_ref, wgu_ref, wd_ref, lg_ref,
             dx_ref, dgu_ref, df_ref, a_ref, h_ref, dln_ref, dmod_ref):
        i = pl.program_id(0)
        dr, dgain, dbias = _ln_bwd(dy_ref[...], r_ref[...], lg_ref[...])

        @pl.when(i == 0)
        def _():
            dln_ref[...] = jnp.zeros_like(dln_ref)

        @pl.when(i % tiles_per_seq == 0)
        def _():
            dmod_ref[...] = jnp.zeros_like(dmod_ref)

        dln_ref[0:1, :] += dgain
        dln_ref[1:2, :] += dbias
        df32 = (0.5 * (1.0 + gt_ref[0])) * dr
        df = df32.astype(BF16)
        df_ref[...] = df
        dgate = jnp.sum(dr * (0.5 * f_ref[...].astype(F32)), axis=0, keepdims=True)
        xx = x_ref[...]
        one_sc = 1.0 + sc_ref[0]
        h = (xx * one_sc + sh_ref[0]).astype(BF16)
        h_ref[...] = h
        dh = jnp.zeros((tm, dm), F32)
        for k in range(4):
            da = _dot_nt(df, wd_ref[k])
            gk = gu_ref[k].astype(F32)
            uk = gu_ref[k + 4].astype(F32)
            sg = _sigmoid(gk)
            sil = gk * sg
            a_ref[k] = (sil * uk).astype(BF16)
            du = (da * sil).astype(BF16)
            dg = (da * uk * (sg * (1.0 + gk * (1.0 - sg)))).astype(BF16)
            dgu_ref[k] = dg
            dgu_ref[k + 4] = du
            dh = dh + _dot(dg, wgu_ref[k]) + _dot(du, wgu_ref[k + 4])
        dx_ref[...] = DN_ALPHA * dr + dh * one_sc
        dmod_ref[0, 0:1, :] += jnp.sum(dh, axis=0, keepdims=True)
        dmod_ref[0, 1:2, :] += jnp.sum(dh * xx, axis=0, keepdims=True)
        dmod_ref[0, 2:3, :] += dgate

    tile = pl.BlockSpec((tm, dm), lambda i: (i, 0))
    mod = _mod_spec(tiles_per_seq, dm)
    gu_spec = pl.BlockSpec((8, tm, fc), lambda i: (0, i, 0))
    return _call(
        body, job, name=name, grid=(tokens // tm,),
        in_specs=[tile, tile, tile, tile, gu_spec, mod, mod, mod, _const_spec(wgu.shape), _const_spec(wd.shape),
                  _const_spec((1, dm))],
        out_specs=[tile, gu_spec, tile, pl.BlockSpec((4, tm, fc), lambda i: (0, i, 0)), tile,
                   pl.BlockSpec((2, dm), lambda i: (0, 0)),
                   pl.BlockSpec((1, 3, dm), lambda i: (i // tiles_per_seq, 0, 0))],
        out_shape=[jax.ShapeDtypeStruct((tokens, dm), F32), jax.ShapeDtypeStruct((8, tokens, fc), BF16),
                   jax.ShapeDtypeStruct((tokens, dm), BF16), jax.ShapeDtypeStruct((4, tokens, fc), BF16),
                   jax.ShapeDtypeStruct((tokens, dm), BF16), jax.ShapeDtypeStruct((2, dm), F32),
                   jax.ShapeDtypeStruct((nseq, 3, dm), F32)],
        args=(dy, r, x, f, gu, sh, sc, gt, wgu, wd, ln_g))


def tn_matmul(a, b, name, job=None, b_cols=None):
    na, tokens, kk = a.shape
    nb, _, cc = b.shape
    col = 0
    if b_cols is not None:
        col, cc = b_cols
    tt = tokens
    while 4 * tt * (kk + cc) + 8 * kk * cc > TN_VMEM_BUDGET and tt % 2 == 0 and tt > 256:
        tt //= 2
    steps = tokens // tt

    def body(a_ref, b_ref, o_ref, *acc):
        if steps == 1:
            o_ref[0, 0] = _dot_tn(a_ref[0], b_ref[0]).astype(BF16)
            return
        acc_ref, = acc
        t = pl.program_id(2)

        @pl.when(t == 0)
        def _():
            acc_ref[...] = jnp.zeros_like(acc_ref)

        acc_ref[...] += _dot_tn(a_ref[0], b_ref[0])

        @pl.when(t == steps - 1)
        def _():
            o_ref[0, 0] = acc_ref[...].astype(BF16)

    return _call(
        body, job, name=name, grid=(na, nb, steps),
        in_specs=[pl.BlockSpec((1, tt, kk), lambda i, j, t: (i, t, 0)),
                  pl.BlockSpec((1, tt, cc), lambda i, j, t: (j, t, col))],
        out_specs=[pl.BlockSpec((1, 1, kk, cc), lambda i, j, t: (i, j, 0, 0))],
        out_shape=[jax.ShapeDtypeStruct((na, nb, kk, cc), BF16)],
        scratch_shapes=[] if steps == 1 else [pltpu.VMEM((kk, cc), F32)], args=(a, b))


def proj_fwd(x1, sh, sc, w_in, seq, name, job=None):
    tokens, dm = x1.shape
    tm = min(TOKEN_TILE, seq)
    tiles_per_seq = seq // tm
    widths = [N_Q_HEADS * HEAD_DIM, N_KV_HEADS * HEAD_DIM, N_KV_HEADS * HEAD_DIM, 512, 512, 512]
    assert sum(widths) == w_in.shape[0]

    def body(x_ref, sh_ref, sc_ref, w_ref, *outs):
        h = (x_ref[...] * (1.0 + sc_ref[0]) + sh_ref[0]).astype(BF16)
        proj = _dot_nt(h, w_ref[...])
        at = 0
        for o_ref, wdt in zip(outs, widths):
            o_ref[...] = proj[:, at:at + wdt]
            at += wdt

    tile = pl.BlockSpec((tm, dm), lambda i: (i, 0))
    mod = _mod_spec(tiles_per_seq, dm)
    return _call(
        body, job, name=name, grid=(tokens // tm,),
        in_specs=[tile, mod, mod, _const_spec(w_in.shape)],
        out_specs=[pl.BlockSpec((tm, wdt), lambda i: (i, 0)) for wdt in widths],
        out_shape=[jax.ShapeDtypeStruct((tokens, wdt), F32) for wdt in widths],
        args=(x1, sh, sc, w_in))


LANES = 2 * HEAD_DIM


def _head_lane(shape):
    return lax.broadcasted_iota(jnp.int32, shape, 1) % HEAD_DIM


def _lane_half(shape):
    return lax.broadcasted_iota(jnp.int32, shape, 1) // HEAD_DIM


def _swap_rot(v):
    lane = _head_lane(v.shape)
    half = ROT_DIM // 2
    return jnp.where(lane < half, pltpu.roll(v, LANES - half, 1),
                     jnp.where(lane < ROT_DIM, pltpu.roll(v, half, 1), 0.0))


def _rope(v, cos_t, sin_t):
    return v * cos_t + _swap_rot(v) * sin_t


def _unrope(dv, cos_t, sin_t):
    return dv * cos_t + _swap_rot(dv * sin_t)


def _both_halves(t, g):
    return jnp.where(_lane_half(t.shape) == g, t, pltpu.roll(t, HEAD_DIM, 1))


def _fold_halves(t, g):
    return jnp.where(_lane_half(t.shape) == g, t + pltpu.roll(t, HEAD_DIM, 1), 0.0)


def _stack_heads(blocks):
    rows = []
    for blk in blocks:
        half = _lane_half(blk.shape)
        rows += [jnp.where(half == 0, blk, 0.0), jnp.where(half == 1, blk, 0.0)]
    return jnp.concatenate(rows, axis=0)


def _unstack_heads(t, j):
    lo = t[(2 * j) * ATTN_BLOCK:(2 * j + 1) * ATTN_BLOCK]
    hi = t[(2 * j + 1) * ATTN_BLOCK:(2 * j + 2) * ATTN_BLOCK]
    return jnp.where(_lane_half(lo.shape) == 0, lo, hi)


def _band_mask(q0, w0):
    rows, cols = GQA_GROUP * ATTN_BLOCK, 2 * ATTN_BLOCK
    qi = lax.broadcasted_iota(jnp.int32, (rows, cols), 0) % ATTN_BLOCK + q0
    ki = lax.broadcasted_iota(jnp.int32, (rows, cols), 1) + w0
    diff = qi - ki
    return (diff >= 0) & (diff < ATTN_BLOCK)


def _attn_specs(seq):
    q_spec = pl.BlockSpec((seq, GQA_GROUP * HEAD_DIM), lambda b, g: (b, g))
    kv_spec = pl.BlockSpec((seq, LANES), lambda b, g: (b, 0))
    sink_spec = pl.BlockSpec((1, GQA_GROUP * ATTN_BLOCK, 1), lambda b, g: (g, 0, 0))
    return q_spec, kv_spec, sink_spec


def _block_starts(n):
    q0 = pl.multiple_of(n * ATTN_BLOCK, ATTN_BLOCK)
    w0 = pl.multiple_of(jnp.maximum(n - 1, 0) * ATTN_BLOCK, ATTN_BLOCK)
    return q0, w0


def _stacked_queries(ref, rows, cos_b=None, sin_b=None):
    blocks = []
    for j in range(2):
        blk = ref[rows, j * LANES:(j + 1) * LANES]
        blocks.append(blk if cos_b is None else _rope(blk, cos_b, sin_b))
    return _stack_heads(blocks).astype(BF16)


def _sink_columns(sinks):
    return jnp.repeat(sinks.reshape(N_KV_HEADS, GQA_GROUP), ATTN_BLOCK, axis=1)[:, :, None]


def attn_fwd(q, k, v, cos_t, sin_t, sinks, seq, name, job=None):
    tokens = q.shape[0]
    nblk = seq // ATTN_BLOCK
    assert nblk >= 2
    scale = HEAD_DIM ** -0.5

    def body(q_ref, k_ref, v_ref, cos_ref, sin_ref, sink_ref, o_ref, kd_ref, vd_ref):
        g = pl.program_id(1)
        kd_ref[...] = _both_halves(_rope(k_ref[...], cos_ref[...], sin_ref[...]), g).astype(BF16)
        vd_ref[...] = _both_halves(v_ref[...], g).astype(BF16)
        sink = sink_ref[0]

        def block(n, carry):
            q0, w0 = _block_starts(n)
            rows, win = pl.ds(q0, ATTN_BLOCK), pl.ds(w0, 2 * ATTN_BLOCK)
            qs = _stacked_queries(q_ref, rows, cos_ref[rows, :], sin_ref[rows, :])
            s = _dot_nt(qs, kd_ref[win, :]) * scale
            s = jnp.where(_band_mask(q0, w0), s, NEG_BIG)
            m = jnp.maximum(jnp.max(s, axis=-1, keepdims=True), sink)
            p = jnp.exp(s - m)
            denom = jnp.sum(p, axis=-1, keepdims=True) + jnp.exp(sink - m)
            out = _dot((p * pl.reciprocal(denom, approx=True)).astype(BF16), vd_ref[win, :])
            for j in range(2):
                o_ref[rows, j * LANES:(j + 1) * LANES] = _unstack_heads(out, j).astype(o_ref.dtype)
            return carry

        lax.fori_loop(0, nblk, block, 0, unroll=2)

    q_spec, kv_spec, sink_spec = _attn_specs(seq)
    return _call(
        body, job, name=name, grid=(tokens // seq, N_KV_HEADS),
        in_specs=[q_spec, kv_spec, kv_spec, kv_spec, kv_spec, sink_spec],
        out_specs=[q_spec], out_shape=[jax.ShapeDtypeStruct(q.shape, BF16)],
        scratch_shapes=[pltpu.VMEM((seq, LANES), BF16), pltpu.VMEM((seq, LANES), BF16)],
        args=(q, k, v, cos_t, sin_t, _sink_columns(sinks)))


def attn_bwd(q, k, v, do, cos_t, sin_t, sinks, seq, name, job=None):
    tokens = q.shape[0]
    nseq = tokens // seq
    nblk = seq // ATTN_BLOCK
    assert nblk >= 2
    rows_stacked = GQA_GROUP * ATTN_BLOCK
    scale = HEAD_DIM ** -0.5

    def body(q_ref, k_ref, v_ref, do_ref, cos_ref, sin_ref, sink_ref, dq_ref, dk_ref, dv_ref, ds_ref,
             kd_ref, vd_ref, dkd_ref, dvd_ref):
        g = pl.program_id(1)
        kd_ref[...] = _both_halves(_rope(k_ref[...], cos_ref[...], sin_ref[...]), g).astype(BF16)
        vd_ref[...] = _both_halves(v_ref[...], g).astype(BF16)
        dkd_ref[...] = jnp.zeros_like(dkd_ref)
        dvd_ref[...] = jnp.zeros_like(dvd_ref)
        sink = sink_ref[0]

        def block(n, dsink):
            q0, w0 = _block_starts(n)
            rows, win = pl.ds(q0, ATTN_BLOCK), pl.ds(w0, 2 * ATTN_BLOCK)
            cos_b, sin_b = cos_ref[rows, :], sin_ref[rows, :]
            qs = _stacked_queries(q_ref, rows, cos_b, sin_b)
            dos = _stacked_queries(do_ref, rows)
            kw, vw = kd_ref[win, :], vd_ref[win, :]
            s = _dot_nt(qs, kw) * scale
            s = jnp.where(_band_mask(q0, w0), s, NEG_BIG)
            m = jnp.maximum(jnp.max(s, axis=-1, keepdims=True), sink)
            p = jnp.exp(s - m)
            e_sink = jnp.exp(sink - m)
            inv = pl.reciprocal(jnp.sum(p, axis=-1, keepdims=True) + e_sink, approx=True)
            pn = p * inv
            dvd_ref[win, :] += _dot_tn(pn.astype(BF16), dos)
            dp = _dot_nt(dos, vw)
            delta = jnp.sum(dp * pn, axis=-1, keepdims=True)
            ds = (pn * (dp - delta)).astype(BF16)
            dqs = _dot(ds, kw) * scale
            dkd_ref[win, :] += _dot_tn(ds, qs) * scale
            for j in range(2):
                dq_ref[rows, j * LANES:(j + 1) * LANES] = _unrope(_unstack_heads(dqs, j), cos_b, sin_b).astype(BF16)
            return dsink - e_sink * inv * delta

        ds_ref[0, 0] = lax.fori_loop(0, nblk // 2, lambda i, acc: block(2 * i + 1, block(2 * i, acc)),
                                     jnp.zeros((rows_stacked, 1), F32))
        dk_g = _unrope(_fold_halves(dkd_ref[...], g), cos_ref[...], sin_ref[...])
        dv_g = _fold_halves(dvd_ref[...], g)

        @pl.when(g == 0)
        def _():
            dk_ref[...] = dk_g
            dv_ref[...] = dv_g

        @pl.when(g != 0)
        def _():
            dk_ref[...] += dk_g
            dv_ref[...] += dv_g

    q_spec, kv_spec, sink_spec = _attn_specs(seq)
    return _call(
        body, job, name=name, grid=(nseq, N_KV_HEADS),
        in_specs=[q_spec, kv_spec, kv_spec, q_spec, kv_spec, kv_spec, sink_spec],
        out_specs=[q_spec, kv_spec, kv_spec, pl.BlockSpec((1, 1, rows_stacked, 1), lambda b, g: (b, g, 0, 0))],
        out_shape=[jax.ShapeDtypeStruct(q.shape, BF16), jax.ShapeDtypeStruct(k.shape, F32),
                   jax.ShapeDtypeStruct(k.shape, F32), jax.ShapeDtypeStruct((nseq, N_KV_HEADS, rows_stacked, 1), F32)],
        scratch_shapes=[pltpu.VMEM((seq, LANES), BF16), pltpu.VMEM((seq, LANES), BF16),
                        pltpu.VMEM((seq, LANES), F32), pltpu.VMEM((seq, LANES), F32)],
        args=(q, k, v, do, cos_t, sin_t, _sink_columns(sinks)))


CONV_COLS = 128


def _shift_down(z, by):
    t = lax.broadcasted_iota(jnp.int32, z.shape, 0)
    return jnp.where(t >= by, pltpu.roll(z, by, 0), 0.0)


def _shift_up(z, by):
    n = z.shape[0]
    t = lax.broadcasted_iota(jnp.int32, z.shape, 0)
    return jnp.where(t < n - by, pltpu.roll(z, n - by, 0), 0.0)


def conv_fwd(u, bg, cg, conv_w, seq, name):
    tokens, width = u.shape

    def body(u_ref, bg_ref, cg_ref, w_ref, o_ref):
        z = cg_ref[...] * u_ref[...]
        yy = w_ref[2:3, :] * z + w_ref[1:2, :] * _shift_down(z, 1) + w_ref[0:1, :] * _shift_down(z, 2)
        o_ref[...] = (bg_ref[...] * yy).astype(BF16)

    col = pl.BlockSpec((seq, CONV_COLS), lambda j, b: (b, j))
    return pl.pallas_call(
        body, name=name, grid=(width // CONV_COLS, tokens // seq),
        in_specs=[col, col, col, pl.BlockSpec((CONV_TAPS, CONV_COLS), lambda j, b: (0, j))],
        out_specs=col, out_shape=jax.ShapeDtypeStruct((tokens, width), BF16),
        compiler_params=_params(("parallel", "parallel")),
    )(u, bg, cg, conv_w)


def conv_bwd(dout, u, bg, cg, conv_w, seq, name):
    tokens, width = u.shape

    def body(do_ref, u_ref, bg_ref, cg_ref, w_ref, du_ref, dbg_ref, dcg_ref, dw_ref):
        uu, cg_v, do = u_ref[...], cg_ref[...], do_ref[...].astype(F32)
        z = cg_v * uu
        z1, z2 = _shift_down(z, 1), _shift_down(z, 2)
        yy = w_ref[2:3, :] * z + w_ref[1:2, :] * z1 + w_ref[0:1, :] * z2
        dbg_ref[...] = (do * yy).astype(BF16)
        dyy = do * bg_ref[...]
        dz = w_ref[2:3, :] * dyy + w_ref[1:2, :] * _shift_up(dyy, 1) + w_ref[0:1, :] * _shift_up(dyy, 2)
        du_ref[...] = (dz * cg_v).astype(BF16)
        dcg_ref[...] = (dz * uu).astype(BF16)

        @pl.when(pl.program_id(1) == 0)
        def _():
            dw_ref[...] = jnp.zeros_like(dw_ref)

        dw_ref[0:1, :] += jnp.sum(dyy * z2, axis=0, keepdims=True)
        dw_ref[1:2, :] += jnp.sum(dyy * z1, axis=0, keepdims=True)
        dw_ref[2:3, :] += jnp.sum(dyy * z, axis=0, keepdims=True)

    col = pl.BlockSpec((seq, CONV_COLS), lambda j, b: (b, j))
    w_spec = pl.BlockSpec((CONV_TAPS, CONV_COLS), lambda j, b: (0, j))
    act = jax.ShapeDtypeStruct((tokens, width), BF16)
    return pl.pallas_call(
        body, name=name, grid=(width // CONV_COLS, tokens // seq),
        in_specs=[col, col, col, col, w_spec], out_specs=[col, col, col, w_spec],
        out_shape=[act, act, act, jax.ShapeDtypeStruct((CONV_TAPS, width), F32)],
        compiler_params=_params(("parallel", "arbitrary")),
    )(dout, u, bg, cg, conv_w)


def out_fwd(x1, attn, conv, gt, w_out, ln_g, ln_b, seq, name, job=None):
    tokens, dm = x1.shape
    half = attn.shape[1]
    tm = min(TOKEN_TILE, seq)
    tiles_per_seq = seq // tm

    def body(x_ref, a_ref, c_ref, gt_ref, w_ref, lg_ref, lb_ref, xo_ref, r_ref, mi_ref, mix_ref):
        mixin = jnp.concatenate([a_ref[...], c_ref[...]], axis=1).astype(BF16)
        mi_ref[...] = mixin
        mix = _dot(mixin, w_ref[...])
        mix_ref[...] = mix.astype(BF16)
        r = DN_ALPHA * x_ref[...] + (1.0 + gt_ref[0]) * mix
        r_ref[...] = r
        xhat, _ = _ln_stats(r)
        xo_ref[...] = xhat * lg_ref[...] + lb_ref[...]

    tile = pl.BlockSpec((tm, dm), lambda i: (i, 0))
    htile = pl.BlockSpec((tm, half), lambda i: (i, 0))
    return _call(
        body, job, name=name, grid=(tokens // tm,),
        in_specs=[tile, htile, htile, _mod_spec(tiles_per_seq, dm), _const_spec(w_out.shape),
                  _const_spec((1, dm)), _const_spec((1, dm))],
        out_specs=[tile, tile, tile, tile],
        out_shape=[jax.ShapeDtypeStruct((tokens, dm), F32), jax.ShapeDtypeStruct((tokens, dm), F32),
                   jax.ShapeDtypeStruct((tokens, dm), BF16), jax.ShapeDtypeStruct((tokens, dm), BF16)],
        args=(x1, attn, conv, gt, w_out, ln_g, ln_b))


def out_bwd(dy, r, mix, gt, w_out, ln_g, seq, name, job=None):
    tokens, dm = r.shape
    half = dm // 2
    tm = min(TOKEN_TILE, seq)
    tiles_per_seq = seq // tm
    nseq = tokens // seq

    def body(dy_ref, r_ref, mix_ref, gt_ref, w_ref, lg_ref, dres_ref, da_ref, dc_ref, dmix_ref, dln_ref, dgt_ref):
        i = pl.program_id(0)
        dr, dgain, dbias = _ln_bwd(dy_ref[...], r_ref[...], lg_ref[...])

        @pl.when(i == 0)
        def _():
            dln_ref[...] = jnp.zeros_like(dln_ref)

        @pl.when(i % tiles_per_seq == 0)
        def _():
            dgt_ref[...] = jnp.zeros_like(dgt_ref)

        dln_ref[0:1, :] += dgain
        dln_ref[1:2, :] += dbias
        dgt_ref[0] += jnp.sum(dr * mix_ref[...].astype(F32), axis=0, keepdims=True)
        dres_ref[...] = DN_ALPHA * dr
        dmix = ((1.0 + gt_ref[0]) * dr).astype(BF16)
        dmix_ref[...] = dmix
        dmixin = _dot_nt(dmix, w_ref[...])
        da_ref[...] = dmixin[:, :half].astype(BF16)
        dc_ref[...] = dmixin[:, half:].astype(BF16)

    tile = pl.BlockSpec((tm, dm), lambda i: (i, 0))
    htile = pl.BlockSpec((tm, half), lambda i: (i, 0))
    return _call(
        body, job, name=name, grid=(tokens // tm,),
        in_specs=[tile, tile, tile, _mod_spec(tiles_per_seq, dm), _const_spec(w_out.shape), _const_spec((1, dm))],
        out_specs=[tile, htile, htile, tile, pl.BlockSpec((2, dm), lambda i: (0, 0)),
                   pl.BlockSpec((1, 1, dm), lambda i: (i // tiles_per_seq, 0, 0))],
        out_shape=[jax.ShapeDtypeStruct((tokens, dm), F32), jax.ShapeDtypeStruct((tokens, half), BF16),
                   jax.ShapeDtypeStruct((tokens, half), BF16), jax.ShapeDtypeStruct((tokens, dm), BF16),
                   jax.ShapeDtypeStruct((2, dm), F32), jax.ShapeDtypeStruct((nseq, 1, dm), F32)],
        args=(dy, r, mix, gt, w_out, ln_g))


def proj_bwd(parts, dres, x1, sh, sc, w_in, seq, name):
    tokens, dm = x1.shape
    tm = min(TOKEN_TILE, seq)
    tiles_per_seq = seq // tm
    nseq = tokens // seq
    widths = [p.shape[1] for p in parts]
    total = sum(widths)

    def body(*refs):
        part_refs = refs[:6]
        dres_ref, x_ref, sh_ref, sc_ref, w_ref, dx_ref, dproj_ref, h_ref, dmod_ref = refs[6:]
        dproj = jnp.concatenate([p[...].astype(BF16) for p in part_refs], axis=1)
        dproj_ref[...] = dproj
        dh = _dot(dproj, w_ref[...])
        xx = x_ref[...]
        one_sc = 1.0 + sc_ref[0]
        h_ref[...] = (xx * one_sc + sh_ref[0]).astype(BF16)
        dx_ref[...] = dres_ref[...] + dh * one_sc

        @pl.when(pl.program_id(0) % tiles_per_seq == 0)
        def _():
            dmod_ref[...] = jnp.zeros_like(dmod_ref)

        dmod_ref[0, 0:1, :] += jnp.sum(dh, axis=0, keepdims=True)
        dmod_ref[0, 1:2, :] += jnp.sum(dh * xx, axis=0, keepdims=True)

    tile = pl.BlockSpec((tm, dm), lambda i: (i, 0))
    mod = _mod_spec(tiles_per_seq, dm)
    return pl.pallas_call(
        body, name=name, grid=(tokens // tm,),
        in_specs=[pl.BlockSpec((tm, wdt), lambda i: (i, 0)) for wdt in widths]
        + [tile, tile, mod, mod, _const_spec(w_in.shape)],
        out_specs=[tile, pl.BlockSpec((tm, total), lambda i: (i, 0)), tile,
                   pl.BlockSpec((1, 2, dm), lambda i: (i // tiles_per_seq, 0, 0))],
        out_shape=[jax.ShapeDtypeStruct((tokens, dm), F32), jax.ShapeDtypeStruct((tokens, total), BF16),
                   jax.ShapeDtypeStruct((tokens, dm), BF16), jax.ShapeDtypeStruct((nseq, 2, dm), F32)],
        compiler_params=_params(("arbitrary",)),
    )(*parts, dres, x1, sh, sc, w_in)


def _rope_tables(positions):
    half = ROT_DIM // 2
    inv_freq = jnp.power(jnp.float32(ROPE_THETA), -jnp.arange(0, ROT_DIM, 2, dtype=F32) / ROT_DIM)
    lane = jnp.arange(LANES) % HEAD_DIM
    freq = jnp.where(lane < ROT_DIM, inv_freq[lane % half], 0.0)
    sign = jnp.where(lane < half, -1.0, 1.0).astype(F32)
    ang = positions.astype(F32)[:, None] * freq[None, :]
    return jnp.cos(ang), sign[None, :] * jnp.sin(ang)


def kernel(x, c, positions, w_ada, b_ada, ffn1_w_gate_up, ffn1_w_down, ln1_g, ln1_b, w_in, conv_w, attn_sinks, w_out, ln2_g, ln2_b, ffn2_w_gate_up, ffn2_w_down, ln3_g, ln3_b, loss_target, m_w_ada, m_b_ada, m_ffn1_w_gate_up, m_ffn1_w_down, m_ln1_g, m_ln1_b, m_w_in, m_conv_w, m_attn_sinks, m_w_out, m_ln2_g, m_ln2_b, m_ffn2_w_gate_up, m_ffn2_w_down, m_ln3_g, m_ln3_b, v_w_ada, v_b_ada, v_ffn1_w_gate_up, v_ffn1_w_down, v_ln1_g, v_ln1_b, v_w_in, v_conv_w, v_attn_sinks, v_w_out, v_ln2_g, v_ln2_b, v_ffn2_w_gate_up, v_ffn2_w_down, v_ln3_g, v_ln3_b):
    nseq, seq, dm = x.shape
    tokens = nseq * seq
    dev = 4 * lax.axis_index("x") + 2 * lax.axis_index("y") + lax.axis_index("c")
    ada_cols = w_ada.shape[2]
    ff = ffn1_w_down.shape[1] * N_DEV
    fc = ff // 4
    in_cols = w_in.shape[2]
    conv_cols = conv_w.shape[2]

    def t_bf16(w):
        return w[0].T.astype(BF16)

    c_all, convw_all = all_gather([c, conv_w[0]], "gather_cond")
    wgu1, wd1 = all_gather([t_bf16(ffn1_w_gate_up), ffn1_w_down[0].astype(BF16)], "gather_ffn1")
    c_all = c_all.reshape(N_DEV * nseq, dm)
    convw_full = convw_all.transpose(1, 0, 2).reshape(CONV_TAPS, N_DEV * conv_cols)
    wd1 = wd1.reshape(4, fc, dm)

    b_cols = lax.dynamic_slice(b_ada, (0, dev * ada_cols), (1, ada_cols))
    cond_all, mod_cols = ada_fwd(c_all, w_ada[0], b_cols, "ada_fwd")
    (mod_all,) = all_gather([mod_cols], "gather_mod")
    mod = lax.dynamic_slice(mod_all, (0, dev * nseq, 0), (N_DEV, nseq, ada_cols))
    mod = mod.transpose(1, 0, 2).reshape(nseq, 9, 1, dm)
    sh1, sc1, g1, sh2, sc2, g2, sh3, sc3, g3 = [mod[:, i] for i in range(9)]

    x0 = x.reshape(tokens, dm)
    spread = gather_spread_job([t_bf16(w_in), w_out[0].astype(BF16), ffn2_w_down[0].astype(BF16)])
    (x1, r1, gu1, f1), spread = ffn_fwd(x0, sh1, sc1, g1, wgu1, wd1, ln1_g, ln1_b, seq, "ffn1_fwd", job=spread)
    win, wout = run_job(gather_forward_job(spread[:2]), "gather_mix_forward")
    win = win.reshape(N_DEV * in_cols, dm)
    wout = wout.reshape(dm, dm)
    (q, k, v, u, bg, cg), (wd2,) = proj_fwd(x1, sh2, sc2, win, seq, "proj_fwd", job=gather_forward_job(spread[2:]))
    wd2 = wd2.reshape(4, fc, dm)
    cos_t, sin_t = _rope_tables(positions.reshape(tokens))
    sinks = attn_sinks[0]
    (attn,), spread = attn_fwd(q, k, v, cos_t, sin_t, sinks, seq, "attn_fwd",
                               job=gather_spread_job([t_bf16(ffn2_w_gate_up)]))
    conv = conv_fwd(u, bg, cg, convw_full, seq, "conv_fwd")
    (x2, r2, mixin, mix), (wgu2,) = out_fwd(x1, attn, conv, g2, wout, ln2_g, ln2_b, seq, "out_fwd",
                                            job=gather_forward_job(spread))
    target = loss_target.reshape(tokens, dm)
    (dy3, loss_part, r3, gu3, f3), _ = ffn_fwd(x2, sh3, sc3, g3, wgu2, wd2, ln3_g, ln3_b, seq, "ffn2_fwd", target=target)

    (dx2, dgu3, df3, a3, h3, dln3, dmod3), _ = ffn_bwd(dy3, r3, x2, f3, gu3, sh3, sc3, g3, wgu2, wd2, ln3_g, seq, "ffn2_bwd")
    g_wd2 = tn_matmul(a3, df3[None], "ffn2_dwd")[0][0].reshape(N_DEV, ff // N_DEV, dm)
    g_wgu2 = tn_matmul(dgu3, h3[None], "ffn2_dwgu")[0][0].reshape(N_DEV, fc, dm)
    (dres2, dattn, dconv, dmix, dln2, dg2), swapped = out_bwd(dx2, r2, mix, g2, wout, ln2_g, seq, "out_bwd",
                                                              job=swap_job([g_wgu2, g_wd2]))
    p_wgu2, own_wgu2 = pair_sum(g_wgu2, swapped[0], "pair_wgu2")
    p_wd2, own_wd2 = pair_sum(g_wd2, swapped[1], "pair_wd2")
    du, dbg, dcg, dconvw = conv_bwd(dconv, u, bg, cg, convw_full, seq, "conv_bwd")
    (dq, dk, dv, dsink_rows), (far_wgu2, far_wd2) = attn_bwd(
        q, k, v, dattn, cos_t, sin_t, sinks, seq, "attn_bwd", job=chip_exchange_job([p_wgu2, p_wd2]))
    parts = [dq, dk, dv, du, dbg, dcg]
    dx1, dproj, h2, dmod2 = proj_bwd(parts, dres2, x1, sh2, sc2, win, seq, "proj_bwd")
    (dx0, dgu1, df1, a1, h1, dln1, dmod1), _ = ffn_bwd(
        dx1, r1, x0, f1, gu1, sh1, sc1, g1, wgu1, wd1, ln1_g, seq, "ffn1_bwd")

    dmod = jnp.concatenate([dmod1, dmod2, dg2, dmod3], axis=1).reshape(nseq, 9 * dm)
    half = dm // 2
    jobs = _Jobs([gather_spread_job([dmod])])
    (g_wd1,), res = tn_matmul(a1, df1[None], "ffn1_dwd", job=jobs)
    (dmod_spread,) = jobs.split(res)
    g_wd1 = g_wd1.reshape(N_DEV, ff // N_DEV, dm)
    jobs = _Jobs([swap_job([g_wd1]), gather_forward_job(dmod_spread)])
    (g_l,), res = tn_matmul(dgu1, h1[None], "ffn1_dwgu_l", job=jobs, b_cols=(0, half))
    (sw_wd1,), (dmod_all,) = jobs.split(res)
    g_l = g_l.reshape(N_DEV, fc, half)
    p_wd1, own_wd1 = pair_sum(g_wd1, sw_wd1, "pair_wd1")
    jobs = _Jobs([chip_exchange_job([p_wd1]), swap_job([g_l])])
    (g_r,), res = tn_matmul(dgu1, h1[None], "ffn1_dwgu_r", job=jobs, b_cols=(1, half))
    (far_wd1,), (sw_l,) = jobs.split(res)
    g_r = g_r.reshape(N_DEV, fc, half)
    p_l, own_l = pair_sum(g_l, sw_l, "pair_wgu1_l")
    jobs = _Jobs([chip_exchange_job([p_l]), swap_job([g_r])])
    (g_win,), res = tn_matmul(dproj[None], h2[None], "dwin", job=jobs)
    (far_l,), (sw_r,) = jobs.split(res)
    g_win = g_win.reshape(N_DEV, in_cols, dm)
    p_r, own_r = pair_sum(g_r, sw_r, "pair_wgu1_r")
    jobs = _Jobs([chip_exchange_job([p_r]), swap_job([g_win])])
    (g_wout,), res = tn_matmul(mixin[None], dmix[None], "dwout", job=jobs)
    (far_r,), (sw_win,) = jobs.split(res)
    g_wout = g_wout.reshape(N_DEV, dm // N_DEV, dm)
    p_win, own_win = pair_sum(g_win, sw_win, "pair_win")
    jobs = _Jobs([chip_exchange_job([p_win]), swap_job([g_wout])])
    (far_win,), (sw_wout,) = jobs.split(run_job(jobs, "rs_tail_win"))
    p_wout, own_wout = pair_sum(g_wout, sw_wout, "pair_wout")
    (far_wout,) = run_job(chip_exchange_job([p_wout]), "rs_tail_wout")

    grads = {
        "ffn1_w_gate_up": jnp.concatenate([own_l, own_r], axis=1), "ffn1_w_down": own_wd1,
        "w_in": own_win, "w_out": own_wout, "ffn2_w_gate_up": own_wgu2, "ffn2_w_down": own_wd2,
    }
    others = {"ffn1_w_gate_up": jnp.concatenate([far_l, far_r], axis=2), "ffn1_w_down": far_wd1,
              "w_in": far_win, "w_out": far_wout, "ffn2_w_gate_up": far_wgu2, "ffn2_w_down": far_wd2}

    dmod_cols = lax.dynamic_slice(dmod_all.reshape(N_DEV * nseq, 9 * dm), (0, dev * ada_cols), (N_DEV * nseq, ada_cols))
    grads["w_ada"], gb_cols = ada_bwd(cond_all, dmod_cols, "ada_bwd")

    dsinks = jnp.sum(dsink_rows.reshape(nseq, N_Q_HEADS, ATTN_BLOCK), axis=(0, 2))
    small = jnp.zeros((8, dm), F32)
    small = small.at[0:2].set(dln1).at[2:4].set(dln2).at[4:6].set(dln3)
    small = small.at[6, 0:N_Q_HEADS].set(dsinks).at[7, 0].set(loss_part[0, 0])
    small_all, dconvw_all, gb_all = all_gather([small, dconvw, gb_cols], "gather_small")
    small_sum = sum_devices(small_all, "sum_small")
    dconvw_sum = sum_devices(dconvw_all, "sum_convw")
    loss = small_sum[7, 0]
    grads["b_ada"] = gb_all.reshape(1, N_DEV * ada_cols)
    grads["conv_w"] = lax.dynamic_slice(dconvw_sum, (0, dev * conv_cols), (CONV_TAPS, conv_cols))
    grads["attn_sinks"] = small_sum[6:7, 0:N_Q_HEADS]
    for i, nm in enumerate(["ln1_g", "ln1_b", "ln2_g", "ln2_b", "ln3_g", "ln3_b"]):
        grads[nm] = small_sum[i:i + 1]

    given = dict(w_ada=(w_ada, m_w_ada, v_w_ada), b_ada=(b_ada, m_b_ada, v_b_ada),
                 ffn1_w_gate_up=(ffn1_w_gate_up, m_ffn1_w_gate_up, v_ffn1_w_gate_up),
                 ffn1_w_down=(ffn1_w_down, m_ffn1_w_down, v_ffn1_w_down),
                 ln1_g=(ln1_g, m_ln1_g, v_ln1_g), ln1_b=(ln1_b, m_ln1_b, v_ln1_b),
                 w_in=(w_in, m_w_in, v_w_in), conv_w=(conv_w, m_conv_w, v_conv_w),
                 attn_sinks=(attn_sinks, m_attn_sinks, v_attn_sinks), w_out=(w_out, m_w_out, v_w_out),
                 ln2_g=(ln2_g, m_ln2_g, v_ln2_g), ln2_b=(ln2_b, m_ln2_b, v_ln2_b),
                 ffn2_w_gate_up=(ffn2_w_gate_up, m_ffn2_w_gate_up, v_ffn2_w_gate_up),
                 ffn2_w_down=(ffn2_w_down, m_ffn2_w_down, v_ffn2_w_down),
                 ln3_g=(ln3_g, m_ln3_g, v_ln3_g), ln3_b=(ln3_b, m_ln3_b, v_ln3_b))
    order = ["w_ada", "b_ada", "ffn1_w_gate_up", "ffn1_w_down", "ln1_g", "ln1_b", "w_in", "conv_w", "attn_sinks",
             "w_out", "ln2_g", "ln2_b", "ffn2_w_gate_up", "ffn2_w_down", "ln3_g", "ln3_b"]
    transposed = ("ffn1_w_gate_up", "ffn2_w_gate_up", "w_in")
    out_g, out_d, out_m, out_v = [], [], [], []
    for nm in order:
        shape = given[nm][0].shape
        two_d = (shape[-2], shape[-1])
        if nm in transposed:
            w2, m2, v2 = [t[0].T for t in given[nm]]
            back = lambda t: t.T[None]
        else:
            w2, m2, v2 = [t.reshape(two_d) for t in given[nm]]
            back = lambda t, shape=shape: t.reshape(shape)
        res = adamw(w2, grads[nm].reshape(w2.shape), m2, v2, "adamw_" + nm, others=others.get(nm))
        for lst, t in zip((out_g, out_d, out_m, out_v), res):
            lst.append(back(t))
    grad_x = dx0.reshape(nseq, seq, dm)
    return (loss, grad_x, *out_g, *out_d, *out_m, *out_v)
```

```python
import functools

import jax
import jax.numpy as jnp
from jax import lax
from jax.experimental import pallas as pl
from jax.experimental.pallas import tpu as pltpu

F32 = jnp.float32
BF16 = jnp.bfloat16
MESH = pl.DeviceIdType.MESH

N_DEV = 8
N_CHIP = 4
HEAD_DIM = 64
N_Q_HEADS = 8
N_KV_HEADS = 2
GQA_GROUP = N_Q_HEADS // N_KV_HEADS
ATTN_BLOCK = 128
ROT_DIM = 16
ROPE_THETA = 500000.0
CONV_TAPS = 3
LN_EPS = 1e-5
DN_ALPHA = 2.0 ** 0.25
ADAM_LR = 0.001
ADAM_B1 = 0.9
ADAM_B2 = 0.999
ADAM_EPS = 1e-08
ADAM_WD = 0.01
ADAM_STEP = 10
NEG_BIG = -1e30

VMEM_LIMIT = 56 * 1024 * 1024
TOKEN_TILE = 256
FFN_FWD_TILE = 512
TN_VMEM_BUDGET = 36 * 1024 * 1024


def _params(semantics=None, vmem=VMEM_LIMIT):
    return pltpu.CompilerParams(dimension_semantics=semantics, vmem_limit_bytes=vmem)


def _dot(a, b):
    return jnp.dot(a, b, preferred_element_type=F32)


def _dot_nt(a, b):
    return lax.dot_general(a, b, (((1,), (1,)), ((), ())), preferred_element_type=F32)


def _dot_tn(a, b):
    return lax.dot_general(a, b, (((0,), (0,)), ((), ())), preferred_element_type=F32)


def _sigmoid(x):
    return pl.reciprocal(1.0 + jnp.exp(-x), approx=True)


def _ln_stats(r):
    mu = jnp.mean(r, axis=-1, keepdims=True)
    d = r - mu
    var = jnp.mean(d * d, axis=-1, keepdims=True)
    rstd = lax.rsqrt(var + LN_EPS)
    return d * rstd, rstd


def _ln_bwd(dy, r, g):
    xhat, rstd = _ln_stats(r)
    dxhat = dy * g
    c1 = jnp.mean(dxhat, axis=-1, keepdims=True)
    c2 = jnp.mean(dxhat * xhat, axis=-1, keepdims=True)
    dr = rstd * (dxhat - c1 - xhat * c2)
    return dr, jnp.sum(dy * xhat, axis=0, keepdims=True), jnp.sum(dy, axis=0, keepdims=True)


def _const_spec(shape):
    nd = len(shape)
    return pl.BlockSpec(shape, lambda *_: (0,) * nd, pipeline_mode=pl.Buffered(1))


def all_gather(arrs, name):
    n = len(arrs)

    def body(*refs):
        ins, outs = refs[:n], refs[n:2 * n]
        send_sems, recv_sems, local_sems = refs[2 * n:]
        x, y, c = lax.axis_index("x"), lax.axis_index("y"), lax.axis_index("c")
        me, sibling = (x, y, c), (x, y, 1 - c)
        chips = [(1 - x, y), (x, 1 - y), (1 - x, 1 - y)]

        def slot(i, p):
            return outs[i].at[4 * p[0] + 2 * p[1] + p[2]]

        def copy(i, k, block, to, src=None):
            return pltpu.make_async_remote_copy(
                src_ref=slot(i, block) if src is None else src, dst_ref=slot(i, block),
                send_sem=send_sems.at[i, k], recv_sem=recv_sems.at[i, k],
                device_id=to, device_id_type=MESH)

        mine = [pltpu.make_async_copy(ins[i], slot(i, me), local_sems.at[i]) for i in range(n)]
        for cp in mine:
            cp.start()
        first = []
        for i in range(n):
            first.append(copy(i, 0, me, sibling, src=ins[i]))
            first += [copy(i, 1 + j, me, (*chip, c), src=ins[i]) for j, chip in enumerate(chips)]
        for cp in first:
            cp.start()
        passed = []
        for j, chip in enumerate(chips):
            for i in range(n):
                copy(i, 1 + j, (*chip, c), me).wait_recv()
                cp = copy(i, 4 + j, (*chip, c), sibling)
                cp.start()
                passed.append(cp)
        for i in range(n):
            copy(i, 0, sibling, me).wait_recv()
            for j, chip in enumerate(chips):
                copy(i, 4 + j, (*chip, 1 - c), me).wait_recv()
        for cp in first + passed:
            cp.wait_send()
        for cp in mine:
            cp.wait()

    any_spec = pl.BlockSpec(memory_space=pl.ANY)
    return pl.pallas_call(
        body, name=name,
        out_shape=[jax.ShapeDtypeStruct((N_DEV, *a.shape), a.dtype) for a in arrs],
        in_specs=[any_spec] * n, out_specs=[any_spec] * n,
        scratch_shapes=[pltpu.SemaphoreType.DMA((n, 7)), pltpu.SemaphoreType.DMA((n, 7)),
                        pltpu.SemaphoreType.DMA((n,))],
    )(*arrs)


RS_ROWS = 32


def reduce_scatter(g, name):
    _, rows, cols = g.shape
    nblk = rows // RS_ROWS
    assert nblk * RS_ROWS == rows

    def body(g_ref, out_ref, r1_ref, p_ref, r2_ref, send_sems, recv_sems):
        x, y, c = lax.axis_index("x"), lax.axis_index("y"), lax.axis_index("c")
        sibling = (x, y, 1 - c)
        q_me = 2 * x + y
        swaps = []
        for q in range(N_CHIP):
            cp = pltpu.make_async_remote_copy(
                src_ref=g_ref.at[2 * q + (1 - c)], dst_ref=r1_ref.at[q],
                send_sem=send_sems.at[q], recv_sem=recv_sems.at[q], device_id=sibling, device_id_type=MESH)
            cp.start()
            swaps.append(cp)
        for cp in swaps:
            cp.wait_recv()

        def pair_sum(i, carry):
            r = pl.ds(pl.multiple_of(i * RS_ROWS, RS_ROWS), RS_ROWS)
            for q in range(N_CHIP):
                p_ref[q, r, :] = (g_ref[2 * q + c, r, :].astype(F32) + r1_ref[q, r, :].astype(F32)).astype(BF16)
            return carry

        lax.fori_loop(0, nblk, pair_sum, 0)
        chips = [(1 - x, y), (x, 1 - y), (1 - x, 1 - y)]
        sends = []
        for k, chip in enumerate(chips):
            cp = pltpu.make_async_remote_copy(
                src_ref=p_ref.at[2 * chip[0] + chip[1]], dst_ref=r2_ref.at[k],
                send_sem=send_sems.at[N_CHIP + k], recv_sem=recv_sems.at[N_CHIP + k],
                device_id=(*chip, c), device_id_type=MESH)
            cp.start()
            sends.append(cp)
        for cp in sends:
            cp.wait_recv()

        def total(i, carry):
            r = pl.ds(pl.multiple_of(i * RS_ROWS, RS_ROWS), RS_ROWS)
            acc = g_ref[2 * q_me + c, r, :].astype(F32) + r1_ref[q_me, r, :].astype(F32)
            for k in range(3):
                acc = acc + r2_ref[k, r, :].astype(F32)
            out_ref[r, :] = acc
            return carry

        lax.fori_loop(0, nblk, total, 0)
        for cp in swaps + sends:
            cp.wait_send()

    vmem = pl.BlockSpec(memory_space=pltpu.VMEM)
    return pl.pallas_call(
        body, name=name,
        out_shape=jax.ShapeDtypeStruct((rows, cols), F32),
        in_specs=[vmem], out_specs=vmem,
        scratch_shapes=[pltpu.VMEM((N_CHIP, rows, cols), BF16), pltpu.VMEM((N_CHIP, rows, cols), BF16),
                        pltpu.VMEM((3, rows, cols), BF16),
                        pltpu.SemaphoreType.DMA((N_CHIP + 3,)), pltpu.SemaphoreType.DMA((N_CHIP + 3,))],
        compiler_params=_params(),
    )(g)


def _place():
    x, y, c = lax.axis_index("x"), lax.axis_index("y"), lax.axis_index("c")
    return x, y, c, [(1 - x, y), (x, 1 - y), (1 - x, 1 - y)]


def _slot(p):
    return 4 * p[0] + 2 * p[1] + p[2]


class _Job:
    def __init__(self, ins, outs, nsem, copies, aliases=None, local=None):
        self.ins, self.outs, self.nsem, self.copies = list(ins), list(outs), nsem, copies
        self.aliases = aliases or {}
        self.local = local

    def scratch(self):
        s = [pltpu.SemaphoreType.DMA(self.nsem), pltpu.SemaphoreType.DMA(self.nsem)]
        if self.local is not None:
            s.append(pltpu.SemaphoreType.DMA((len(self.ins),)))
        return s

    def start(self, ins, outs, sems):
        if self.local is not None:
            for cp in self.local(ins, outs, sems[2]):
                cp.start()
        for cp in self.copies(ins, outs, sems[0], sems[1])[0]:
            cp.start()

    def finish(self, ins, outs, sems):
        started, awaited = self.copies(ins, outs, sems[0], sems[1])
        for cp in awaited:
            cp.wait_recv()
        for cp in started:
            cp.wait_send()
        if self.local is not None:
            for cp in self.local(ins, outs, sems[2]):
                cp.wait()


class _Jobs:
    def __init__(self, jobs):
        self.jobs = jobs
        self.ins = [a for j in jobs for a in j.ins]
        self.outs = [o for j in jobs for o in j.outs]
        self.aliases = {}
        at_in = at_out = 0
        for j in jobs:
            self.aliases.update({at_in + i: at_out + o for i, o in j.aliases.items()})
            at_in, at_out = at_in + len(j.ins), at_out + len(j.outs)

    def scratch(self):
        return [s for j in self.jobs for s in j.scratch()]

    def _each(self, ins, outs, sems):
        at_in = at_out = at_sem = 0
        for j in self.jobs:
            n_in, n_out, n_sem = len(j.ins), len(j.outs), len(j.scratch())
            yield j, ins[at_in:at_in + n_in], outs[at_out:at_out + n_out], sems[at_sem:at_sem + n_sem]
            at_in, at_out, at_sem = at_in + n_in, at_out + n_out, at_sem + n_sem

    def start(self, ins, outs, sems):
        for j, i, o, s in self._each(ins, outs, sems):
            j.start(i, o, s)

    def finish(self, ins, outs, sems):
        for j, i, o, s in self._each(ins, outs, sems):
            j.finish(i, o, s)

    def split(self, results):
        at, parts = 0, []
        for j in self.jobs:
            parts.append(results[at:at + len(j.outs)])
            at += len(j.outs)
        return parts


def _remote(src, dst, send, recv, idx, to):
    return pltpu.make_async_remote_copy(src_ref=src, dst_ref=dst, send_sem=send.at[idx], recv_sem=recv.at[idx],
                                        device_id=to, device_id_type=MESH)


def gather_spread_job(shards):
    def copies(ins, outs, send, recv):
        x, y, c, chips = _place()
        me = (x, y, c)
        peers = [(x, y, 1 - c)] + [(*chip, c) for chip in chips]
        started, awaited = [], []
        for i, (src, dst) in enumerate(zip(ins, outs)):
            for k, peer in enumerate(peers):
                started.append(_remote(src, dst.at[_slot(me)], send, recv, (i, k), peer))
                awaited.append(_remote(src, dst.at[_slot(peer)], send, recv, (i, k), peer))
        return started, awaited

    def local(ins, outs, sems):
        x, y, c, _ = _place()
        return [pltpu.make_async_copy(src, dst.at[_slot((x, y, c))], sems.at[i])
                for i, (src, dst) in enumerate(zip(ins, outs))]

    outs = [jax.ShapeDtypeStruct((N_DEV, *a.shape), a.dtype) for a in shards]
    return _Job(shards, outs, (len(shards), 4), copies, local=local)


def gather_forward_job(fulls):
    def copies(ins, outs, send, recv):
        x, y, c, chips = _place()
        started, awaited = [], []
        for i, buf in enumerate(outs):
            for j, chip in enumerate(chips):
                mine, theirs = buf.at[_slot((*chip, c))], buf.at[_slot((*chip, 1 - c))]
                started.append(_remote(mine, mine, send, recv, (i, j), (x, y, 1 - c)))
                awaited.append(_remote(theirs, theirs, send, recv, (i, j), (x, y, 1 - c)))
        return started, awaited

    outs = [jax.ShapeDtypeStruct(a.shape, a.dtype) for a in fulls]
    return _Job(fulls, outs, (len(fulls), 3), copies, aliases={i: i for i in range(len(fulls))})


def swap_job(gs):
    def copies(ins, outs, send, recv):
        x, y, c, _ = _place()
        started, awaited = [], []
        for i, (g, r1) in enumerate(zip(ins, outs)):
            for q in range(N_CHIP):
                started.append(_remote(g.at[2 * q + (1 - c)], r1.at[q], send, recv, (i, q), (x, y, 1 - c)))
                awaited.append(_remote(g.at[2 * q + c], r1.at[q], send, recv, (i, q), (x, y, 1 - c)))
        return started, awaited

    outs = [jax.ShapeDtypeStruct((N_CHIP, *g.shape[1:]), g.dtype) for g in gs]
    return _Job(gs, outs, (len(gs), N_CHIP), copies)


def chip_exchange_job(ps):
    def copies(ins, outs, send, recv):
        x, y, c, chips = _place()
        started, awaited = [], []
        for i, (p, r2) in enumerate(zip(ins, outs)):
            for k, chip in enumerate(chips):
                started.append(_remote(p.at[2 * chip[0] + chip[1]], r2.at[k], send, recv, (i, k), (*chip, c)))
                awaited.append(_remote(p.at[2 * x + y], r2.at[k], send, recv, (i, k), (*chip, c)))
        return started, awaited

    outs = [jax.ShapeDtypeStruct((3, *p.shape[1:]), p.dtype) for p in ps]
    return _Job(ps, outs, (len(ps), 3), copies)


def _call(body, job, *, name, grid, in_specs, out_specs, out_shape, args, scratch_shapes=(), vmem=VMEM_LIMIT):
    if job is None:
        res = pl.pallas_call(
            body, name=name, grid=grid, in_specs=in_specs, out_specs=out_specs, out_shape=out_shape,
            scratch_shapes=list(scratch_shapes), compiler_params=_params(("arbitrary",) * len(grid), vmem),
        )(*args)
        return res, []
    n_in, n_out, n_scr = len(in_specs), len(out_specs), len(scratch_shapes)
    j_in, j_out = len(job.ins), len(job.outs)

    def with_copies(*refs):
        at = 0
        ins = refs[at:at + n_in]; at += n_in
        jins = refs[at:at + j_in]; at += j_in
        outs = refs[at:at + n_out]; at += n_out
        jouts = refs[at:at + j_out]; at += j_out
        scr = refs[at:at + n_scr]; at += n_scr
        sems = refs[at:]
        ids = [pl.program_id(d) for d in range(len(grid))]
        first = functools.reduce(jnp.logical_and, [i == 0 for i in ids])
        last = functools.reduce(jnp.logical_and, [i == n - 1 for i, n in zip(ids, grid)])

        @pl.when(first)
        def _():
            job.start(jins, jouts, sems)

        body(*ins, *outs, *scr)

        @pl.when(last)
        def _():
            job.finish(jins, jouts, sems)

    any_spec = pl.BlockSpec(memory_space=pl.ANY)
    res = pl.pallas_call(
        with_copies, name=name, grid=grid,
        in_specs=list(in_specs) + [any_spec] * j_in, out_specs=list(out_specs) + [any_spec] * j_out,
        out_shape=list(out_shape) + list(job.outs),
        input_output_aliases={n_in + i: n_out + o for i, o in job.aliases.items()},
        scratch_shapes=list(scratch_shapes) + job.scratch(),
        compiler_params=_params(("arbitrary",) * len(grid), vmem),
    )(*args, *job.ins)
    return res[:n_out], res[n_out:]


def run_job(job, name):
    def body(*refs):
        j_in, j_out = len(job.ins), len(job.outs)
        ins, outs, sems = refs[:j_in], refs[j_in:j_in + j_out], refs[j_in + j_out:]
        job.start(ins, outs, sems)
        job.finish(ins, outs, sems)

    any_spec = pl.BlockSpec(memory_space=pl.ANY)
    return pl.pallas_call(
        body, name=name, in_specs=[any_spec] * len(job.ins), out_specs=[any_spec] * len(job.outs),
        out_shape=list(job.outs), input_output_aliases=dict(job.aliases), scratch_shapes=job.scratch(),
    )(*job.ins)


def pair_sum(g, r1, name):
    _, rows, cols = g.shape
    rb = next(cand for cand in range(min(rows, 512), 0, -16) if rows % cand == 0)

    def body(g_ref, r1_ref, p_ref, own_ref):
        x, y, c, _ = _place()
        s = g_ref[c].astype(F32) + r1_ref[0].astype(F32)
        p_ref[0] = s.astype(BF16)

        @pl.when(pl.program_id(1) == 2 * x + y)
        def _():
            own_ref[...] = s

    return pl.pallas_call(
        body, name=name, grid=(rows // rb, N_CHIP),
        in_specs=[pl.BlockSpec((2, rb, cols), lambda i, q: (q, i, 0)), pl.BlockSpec((1, rb, cols), lambda i, q: (q, i, 0))],
        out_specs=[pl.BlockSpec((1, rb, cols), lambda i, q: (q, i, 0)), pl.BlockSpec((rb, cols), lambda i, q: (i, 0))],
        out_shape=[jax.ShapeDtypeStruct((N_CHIP, rows, cols), BF16), jax.ShapeDtypeStruct((rows, cols), F32)],
        compiler_params=_params(("arbitrary", "arbitrary")),
    )(g, r1)


def sum_devices(a, name):
    def body(a_ref, o_ref):
        acc = a_ref[0]
        for d in range(1, N_DEV):
            acc = acc + a_ref[d]
        o_ref[...] = acc

    return pl.pallas_call(body, name=name, out_shape=jax.ShapeDtypeStruct(a.shape[1:], F32))(a)


def adamw(w, g, m, v, name, others=None):
    rows, cols = w.shape
    rb = rows
    for cand in range(min(rows, 512), 7, -8):
        if rows % cand == 0 and cand % 8 == 0:
            rb = cand
            break

    def body(*refs):
        if others is None:
            w_ref, g_ref, m_ref, v_ref, d_ref, nm_ref, nv_ref = refs
            gg = g_ref[...]
        else:
            w_ref, g_ref, m_ref, v_ref, r2_ref, go_ref, d_ref, nm_ref, nv_ref = refs
            gg = g_ref[...]
            for k in range(3):
                gg = gg + r2_ref[k].astype(F32)
            go_ref[...] = gg
        nm = ADAM_B1 * m_ref[...] + (1.0 - ADAM_B1) * gg
        nv = ADAM_B2 * v_ref[...] + (1.0 - ADAM_B2) * (gg * gg)
        m_hat = nm / (1.0 - ADAM_B1 ** ADAM_STEP)
        v_hat = nv / (1.0 - ADAM_B2 ** ADAM_STEP)
        d_ref[...] = -ADAM_LR * (m_hat / (jnp.sqrt(v_hat) + ADAM_EPS) + ADAM_WD * w_ref[...])
        nm_ref[...] = nm
        nv_ref[...] = nv

    spec = pl.BlockSpec((rb, cols), lambda i: (i, 0))
    out = jax.ShapeDtypeStruct((rows, cols), F32)
    in_specs, args = [spec] * 4, [w, g, m, v]
    if others is not None:
        in_specs.append(pl.BlockSpec((3, rb, cols), lambda i: (0, i, 0)))
        args.append(others)
    n_out = 3 if others is None else 4
    res = pl.pallas_call(
        body, name=name, grid=(rows // rb,), in_specs=in_specs, out_specs=[spec] * n_out,
        out_shape=[out] * n_out, compiler_params=_params(("parallel",)),
    )(*args)
    return (g, *res) if others is None else tuple(res)


def ada_fwd(c_all, w_cols, b_cols, name):
    def body(c_ref, w_ref, b_ref, cond_ref, mod_ref):
        cc = c_ref[...]
        cond = (cc * _sigmoid(cc)).astype(BF16)
        cond_ref[...] = cond
        mod_ref[...] = _dot(cond, w_ref[...].astype(BF16)) + b_ref[...]

    n, cols = c_all.shape[0], w_cols.shape[1]
    return pl.pallas_call(
        body, name=name,
        out_shape=[jax.ShapeDtypeStruct(c_all.shape, BF16), jax.ShapeDtypeStruct((n, cols), F32)],
        compiler_params=_params(),
    )(c_all, w_cols, b_cols)


def ada_bwd(cond_all, dmod_cols, name):
    def body(c_ref, d_ref, gw_ref, gb_ref):
        d = d_ref[...]
        gw_ref[...] = _dot_tn(c_ref[...], d.astype(BF16))
        gb_ref[...] = jnp.sum(d, axis=0, keepdims=True)

    dm, cols = cond_all.shape[1], dmod_cols.shape[1]
    return pl.pallas_call(
        body, name=name,
        out_shape=[jax.ShapeDtypeStruct((dm, cols), F32), jax.ShapeDtypeStruct((1, cols), F32)],
        compiler_params=_params(),
    )(cond_all, dmod_cols)


def _mod_spec(tiles_per_seq, dm):
    return pl.BlockSpec((1, 1, dm), lambda i: (i // tiles_per_seq, 0, 0))


def ffn_fwd(x, sh, sc, gt, wgu, wd, ln_g, ln_b, seq, name, target=None, job=None):
    tokens, dm = x.shape
    fc = wgu.shape[1]
    tm = min(FFN_FWD_TILE, seq)
    tiles_per_seq = seq // tm
    with_loss = target is not None

    def body(*refs):
        if with_loss:
            (x_ref, sh_ref, sc_ref, gt_ref, wgu_ref, wd_ref, lg_ref, lb_ref, t_ref,
             xo_ref, loss_ref, r_ref, gu_ref, f_ref) = refs
        else:
            (x_ref, sh_ref, sc_ref, gt_ref, wgu_ref, wd_ref, lg_ref, lb_ref,
             xo_ref, r_ref, gu_ref, f_ref) = refs
        xx = x_ref[...]
        h = (xx * (1.0 + sc_ref[0]) + sh_ref[0]).astype(BF16)
        acc = jnp.zeros((tm, dm), F32)
        for k in range(4):
            gk = _dot_nt(h, wgu_ref[k])
            uk = _dot_nt(h, wgu_ref[k + 4])
            gu_ref[k] = gk.astype(BF16)
            gu_ref[k + 4] = uk.astype(BF16)
            a = (gk * _sigmoid(gk) * uk).astype(BF16)
            acc = acc + _dot(a, wd_ref[k])
        f_ref[...] = acc.astype(BF16)
        r = DN_ALPHA * xx + (0.5 * (1.0 + gt_ref[0])) * acc
        r_ref[...] = r
        xhat, _ = _ln_stats(r)
        yy = xhat * lg_ref[...] + lb_ref[...]
        if with_loss:
            err = yy - t_ref[...]
            xo_ref[...] = err * (1.0 / dm)

            @pl.when(pl.program_id(0) == 0)
            def _():
                loss_ref[...] = jnp.zeros_like(loss_ref)

            loss_ref[...] += jnp.full((1, 128), (0.5 / dm) * jnp.sum(err * err), F32)
        else:
            xo_ref[...] = yy

    tile = pl.BlockSpec((tm, dm), lambda i: (i, 0))
    mod = _mod_spec(tiles_per_seq, dm)
    in_specs = [tile, mod, mod, mod, _const_spec(wgu.shape), _const_spec(wd.shape),
                _const_spec((1, dm)), _const_spec((1, dm))]
    args = [x, sh, sc, gt, wgu, wd, ln_g, ln_b]
    out_specs = [tile]
    out_shape = [jax.ShapeDtypeStruct((tokens, dm), F32)]
    if with_loss:
        in_specs.append(tile)
        args.append(target)
        out_specs.append(pl.BlockSpec((1, 128), lambda i: (0, 0)))
        out_shape.append(jax.ShapeDtypeStruct((1, 128), F32))
    out_specs += [tile, pl.BlockSpec((8, tm, fc), lambda i: (0, i, 0)), tile]
    out_shape += [jax.ShapeDtypeStruct((tokens, dm), F32), jax.ShapeDtypeStruct((8, tokens, fc), BF16),
                  jax.ShapeDtypeStruct((tokens, dm), BF16)]
    return _call(body, job, name=name, grid=(tokens // tm,), in_specs=in_specs, out_specs=out_specs,
                 out_shape=out_shape, args=args)


def ffn_bwd(dy, r, x, f, gu, sh, sc, gt, wgu, wd, ln_g, seq, name, job=None):
    tokens, dm = x.shape
    fc = wgu.shape[1]
    tm = min(TOKEN_TILE, seq)
    tiles_per_seq = seq // tm
    nseq = tokens // seq

    def body(dy_ref, r_ref, x_ref, f_ref, gu_ref, sh_ref, sc_ref, gt_ref, wgu_ref, wd_ref, lg_ref,
             dx_ref, dgu_ref, df_ref, a_ref, h_ref, dln_ref, dmod_ref):
        i = pl.program_id(0)
        dr, dgain, dbias = _ln_bwd(dy_ref[...], r_ref[...], lg_ref[...])

        @pl.when(i == 0)
        def _():
            dln_ref[...] = jnp.zeros_like(dln_ref)

        @pl.when(i % tiles_per_seq == 0)
        def _():
            dmod_ref[...] = jnp.zeros_like(dmod_ref)

        dln_ref[0:1, :] += dgain
        dln_ref[1:2, :] += dbias
        df32 = (0.5 * (1.0 + gt_ref[0])) * dr
        df = df32.astype(BF16)
        df_ref[...] = df
        dgate = jnp.sum(dr * (0.5 * f_ref[...].astype(F32)), axis=0, keepdims=True)
        xx = x_ref[...]
        one_sc = 1.0 + sc_ref[0]
        h = (xx * one_sc + sh_ref[0]).astype(BF16)
        h_ref[...] = h
        dh = jnp.zeros((tm, dm), F32)
        for k in range(4):
            da = _dot_nt(df, wd_ref[k])
            gk = gu_ref[k].astype(F32)
            uk = gu_ref[k + 4].astype(F32)
            sg = _sigmoid(gk)
            sil = gk * sg
            a_ref[k] = (sil * uk).astype(BF16)
            du = (da * sil).astype(BF16)
            dg = (da * uk * (sg * (1.0 + gk * (1.0 - sg)))).astype(BF16)
            dgu_ref[k] = dg
            dgu_ref[k + 4] = du
            dh = dh + _dot(dg, wgu_ref[k]) + _dot(du, wgu_ref[k + 4])
        dx_ref[...] = DN_ALPHA * dr + dh * one_sc
        dmod_ref[0, 0:1, :] += jnp.sum(dh, axis=0, keepdims=True)
        dmod_ref[0, 1:2, :] += jnp.sum(dh * xx, axis=0, keepdims=True)
        dmod_ref[0, 2:3, :] += dgate

    tile = pl.BlockSpec((tm, dm), lambda i: (i, 0))
    mod = _mod_spec(tiles_per_seq, dm)
    gu_spec = pl.BlockSpec((8, tm, fc), lambda i: (0, i, 0))
    return _call(
        body, job, name=name, grid=(tokens // tm,),
        in_specs=[tile, tile, tile, tile, gu_spec, mod, mod, mod, _const_spec(wgu.shape), _const_spec(wd.shape),
                  _const_spec((1, dm))],
        out_specs=[tile, gu_spec, tile, pl.BlockSpec((4, tm, fc), lambda i: (0, i, 0)), tile,
                   pl.BlockSpec((2, dm), lambda i: (0, 0)),
                   pl.BlockSpec((1, 3, dm), lambda i: (i // tiles_per_seq, 0, 0))],
        out_shape=[jax.ShapeDtypeStruct((tokens, dm), F32), jax.ShapeDtypeStruct((8, tokens, fc), BF16),
                   jax.ShapeDtypeStruct((tokens, dm), BF16), jax.ShapeDtypeStruct((4, tokens, fc), BF16),
                   jax.ShapeDtypeStruct((tokens, dm), BF16), jax.ShapeDtypeStruct((2, dm), F32),
                   jax.ShapeDtypeStruct((nseq, 3, dm), F32)],
        args=(dy, r, x, f, gu, sh, sc, gt, wgu, wd, ln_g))


def tn_matmul(a, b, name, job=None, b_cols=None):
    na, tokens, kk = a.shape
    nb, _, cc = b.shape
    col = 0
    if b_cols is not None:
        col, cc = b_cols
    tt = tokens
    while 4 * tt * (kk + cc) + 8 * kk * cc > TN_VMEM_BUDGET and tt % 2 == 0 and tt > 256:
        tt //= 2
    steps = tokens // tt

    def body(a_ref, b_ref, o_ref, *acc):
        if steps == 1:
            o_ref[0, 0] = _dot_tn(a_ref[0], b_ref[0]).astype(BF16)
            return
        acc_ref, = acc
        t = pl.program_id(2)

        @pl.when(t == 0)
        def _():
            acc_ref[...] = jnp.zeros_like(acc_ref)

        acc_ref[...] += _dot_tn(a_ref[0], b_ref[0])

        @pl.when(t == steps - 1)
        def _():
            o_ref[0, 0] = acc_ref[...].astype(BF16)

    return _call(
        body, job, name=name, grid=(na, nb, steps),
        in_specs=[pl.BlockSpec((1, tt, kk), lambda i, j, t: (i, t, 0)),
                  pl.BlockSpec((1, tt, cc), lambda i, j, t: (j, t, col))],
        out_specs=[pl.BlockSpec((1, 1, kk, cc), lambda i, j, t: (i, j, 0, 0))],
        out_shape=[jax.ShapeDtypeStruct((na, nb, kk, cc), BF16)],
        scratch_shapes=[] if steps == 1 else [pltpu.VMEM((kk, cc), F32)], args=(a, b))


def proj_fwd(x1, sh, sc, w_in, seq, name, job=None):
    tokens, dm = x1.shape
    tm = min(TOKEN_TILE, seq)
    tiles_per_seq = seq // tm
    widths = [N_Q_HEADS * HEAD_DIM, N_KV_HEADS * HEAD_DIM, N_KV_HEADS * HEAD_DIM, 512, 512, 512]
    assert sum(widths) == w_in.shape[0]

    def body(x_ref, sh_ref, sc_ref, w_ref, *outs):
        h = (x_ref[...] * (1.0 + sc_ref[0]) + sh_ref[0]).astype(BF16)
        proj = _dot_nt(h, w_ref[...])
        at = 0
        for o_ref, wdt in zip(outs, widths):
            o_ref[...] = proj[:, at:at + wdt]
            at += wdt

    tile = pl.BlockSpec((tm, dm), lambda i: (i, 0))
    mod = _mod_spec(tiles_per_seq, dm)
    return _call(
        body, job, name=name, grid=(tokens // tm,),
        in_specs=[tile, mod, mod, _const_spec(w_in.shape)],
        out_specs=[pl.BlockSpec((tm, wdt), lambda i: (i, 0)) for wdt in widths],
        out_shape=[jax.ShapeDtypeStruct((tokens, wdt), F32) for wdt in widths],
        args=(x1, sh, sc, w_in))


LANES = 2 * HEAD_DIM


def _head_lane(shape):
    return lax.broadcasted_iota(jnp.int32, shape, 1) % HEAD_DIM


def _lane_half(shape):
    return lax.broadcasted_iota(jnp.int32, shape, 1) // HEAD_DIM


def _swap_rot(v):
    lane = _head_lane(v.shape)
    half = ROT_DIM // 2
    return jnp.where(lane < half, pltpu.roll(v, LANES - half, 1),
                     jnp.where(lane < ROT_DIM, pltpu.roll(v, half, 1), 0.0))


def _rope(v, cos_t, sin_t):
    return v * cos_t + _swap_rot(v) * sin_t


def _unrope(dv, cos_t, sin_t):
    return dv * cos_t + _swap_rot(dv * sin_t)


def _both_halves(t, g):
    return jnp.where(_lane_half(t.shape) == g, t, pltpu.roll(t, HEAD_DIM, 1))


def _fold_halves(t, g):
    return jnp.where(_lane_half(t.shape) == g, t + pltpu.roll(t, HEAD_DIM, 1), 0.0)


def _stack_heads(blocks):
    rows = []
    for blk in blocks:
        half = _lane_half(blk.shape)
        rows += [jnp.where(half == 0, blk, 0.0), jnp.where(half == 1, blk, 0.0)]
    return jnp.concatenate(rows, axis=0)


def _unstack_heads(t, j):
    lo = t[(2 * j) * ATTN_BLOCK:(2 * j + 1) * ATTN_BLOCK]
    hi = t[(2 * j + 1) * ATTN_BLOCK:(2 * j + 2) * ATTN_BLOCK]
    return jnp.where(_lane_half(lo.shape) == 0, lo, hi)


def _band_mask(q0, w0):
    rows, cols = GQA_GROUP * ATTN_BLOCK, 2 * ATTN_BLOCK
    qi = lax.broadcasted_iota(jnp.int32, (rows, cols), 0) % ATTN_BLOCK + q0
    ki = lax.broadcasted_iota(jnp.int32, (rows, cols), 1) + w0
    diff = qi - ki
    return (diff >= 0) & (diff < ATTN_BLOCK)


def _attn_specs(seq):
    q_spec = pl.BlockSpec((seq, GQA_GROUP * HEAD_DIM), lambda b, g: (b, g))
    kv_spec = pl.BlockSpec((seq, LANES), lambda b, g: (b, 0))
    sink_spec = pl.BlockSpec((1, GQA_GROUP * ATTN_BLOCK, 1), lambda b, g: (g, 0, 0))
    return q_spec, kv_spec, sink_spec


def _block_starts(n):
    q0 = pl.multiple_of(n * ATTN_BLOCK, ATTN_BLOCK)
    w0 = pl.multiple_of(jnp.maximum(n - 1, 0) * ATTN_BLOCK, ATTN_BLOCK)
    return q0, w0


def _stacked_queries(ref, rows, cos_b=None, sin_b=None):
    blocks = []
    for j in range(2):
        blk = ref[rows, j * LANES:(j + 1) * LANES]
        blocks.append(blk if cos_b is None else _rope(blk, cos_b, sin_b))
    return _stack_heads(blocks).astype(BF16)


def _sink_columns(sinks):
    return jnp.repeat(sinks.reshape(N_KV_HEADS, GQA_GROUP), ATTN_BLOCK, axis=1)[:, :, None]


def _probs_spec(nblk):
    return pl.BlockSpec((1, 1, nblk, GQA_GROUP * ATTN_BLOCK, 2 * ATTN_BLOCK), lambda b, g: (b, g, 0, 0, 0))


def _sink_probs_spec():
    return pl.BlockSpec((1, 1, GQA_GROUP * ATTN_BLOCK, LANES), lambda b, g: (b, g, 0, 0))


def attn_fwd(q, k, v, cos_t, sin_t, sinks, seq, name, job=None):
    tokens = q.shape[0]
    nblk = seq // ATTN_BLOCK
    assert nblk >= 2
    scale = HEAD_DIM ** -0.5

    nseq = tokens // seq
    rows_stacked = GQA_GROUP * ATTN_BLOCK
    assert nblk <= LANES

    def body(q_ref, k_ref, v_ref, cos_ref, sin_ref, sink_ref, o_ref, qr_ref, p_ref, ps_ref, kd_ref, vd_ref):
        g = pl.program_id(1)
        kd_ref[...] = _both_halves(_rope(k_ref[...], cos_ref[...], sin_ref[...]), g).astype(BF16)
        vd_ref[...] = _both_halves(v_ref[...], g).astype(BF16)
        sink = sink_ref[0]
        lane = lax.broadcasted_iota(jnp.int32, (rows_stacked, LANES), 1)

        def block(n, p_sinks):
            q0, w0 = _block_starts(n)
            rows, win = pl.ds(q0, ATTN_BLOCK), pl.ds(w0, 2 * ATTN_BLOCK)
            blocks = []
            for j in range(2):
                qr = _rope(q_ref[rows, j * LANES:(j + 1) * LANES], cos_ref[rows, :], sin_ref[rows, :]).astype(BF16)
                qr_ref[rows, j * LANES:(j + 1) * LANES] = qr
                blocks.append(qr)
            qs = _stack_heads(blocks)
            s = _dot_nt(qs, kd_ref[win, :]) * scale
            s = jnp.where(_band_mask(q0, w0), s, NEG_BIG)
            m = jnp.maximum(jnp.max(s, axis=-1, keepdims=True), sink)
            p = jnp.exp(s - m)
            e_sink = jnp.exp(sink - m)
            inv = pl.reciprocal(jnp.sum(p, axis=-1, keepdims=True) + e_sink, approx=True)
            pn = (p * inv).astype(BF16)
            p_ref[0, 0, n] = pn
            out = _dot(pn, vd_ref[win, :])
            for j in range(2):
                o_ref[rows, j * LANES:(j + 1) * LANES] = _unstack_heads(out, j).astype(o_ref.dtype)
            return jnp.where(lane == n, e_sink * inv, p_sinks)

        ps_ref[0, 0] = lax.fori_loop(0, nblk, block, jnp.zeros((rows_stacked, LANES), F32), unroll=2)

    q_spec, kv_spec, sink_spec = _attn_specs(seq)
    return _call(
        body, job, name=name, grid=(nseq, N_KV_HEADS),
        in_specs=[q_spec, kv_spec, kv_spec, kv_spec, kv_spec, sink_spec],
        out_specs=[q_spec, q_spec, _probs_spec(nblk), _sink_probs_spec()],
        out_shape=[jax.ShapeDtypeStruct(q.shape, BF16), jax.ShapeDtypeStruct(q.shape, BF16),
                   jax.ShapeDtypeStruct((nseq, N_KV_HEADS, nblk, rows_stacked, 2 * ATTN_BLOCK), BF16),
                   jax.ShapeDtypeStruct((nseq, N_KV_HEADS, rows_stacked, LANES), F32)],
        scratch_shapes=[pltpu.VMEM((seq, LANES), BF16), pltpu.VMEM((seq, LANES), BF16)],
        args=(q, k, v, cos_t, sin_t, _sink_columns(sinks)))


def attn_bwd(qr, k, v, do, probs, sink_probs, cos_t, sin_t, seq, name, job=None):
    tokens = qr.shape[0]
    nseq = tokens // seq
    nblk = seq // ATTN_BLOCK
    assert nblk >= 2
    rows_stacked = GQA_GROUP * ATTN_BLOCK
    scale = HEAD_DIM ** -0.5

    def body(q_ref, k_ref, v_ref, do_ref, p_ref, ps_ref, cos_ref, sin_ref, dq_ref, dk_ref, dv_ref, ds_ref,
             kd_ref, vd_ref, dkd_ref, dvd_ref):
        g = pl.program_id(1)
        kd_ref[...] = _both_halves(_rope(k_ref[...], cos_ref[...], sin_ref[...]), g).astype(BF16)
        vd_ref[...] = _both_halves(v_ref[...], g).astype(BF16)
        dkd_ref[...] = jnp.zeros_like(dkd_ref)
        dvd_ref[...] = jnp.zeros_like(dvd_ref)
        lane = lax.broadcasted_iota(jnp.int32, (rows_stacked, LANES), 1)
        p_sinks = ps_ref[0, 0]

        def block(n, acc):
            q0, w0 = _block_starts(n)
            rows, win = pl.ds(q0, ATTN_BLOCK), pl.ds(w0, 2 * ATTN_BLOCK)
            qs = _stacked_queries(q_ref, rows)
            dos = _stacked_queries(do_ref, rows)
            kw, vw = kd_ref[win, :], vd_ref[win, :]
            pn16 = p_ref[0, 0, n]
            pn = pn16.astype(F32)
            dvd_ref[win, :] += _dot_tn(pn16, dos)
            dp = _dot_nt(dos, vw)
            delta = jnp.sum(dp * pn, axis=-1, keepdims=True)
            ds = (pn * (dp - delta)).astype(BF16)
            dqs = _dot(ds, kw) * scale
            dkd_ref[win, :] += _dot_tn(ds, qs) * scale
            cos_b, sin_b = cos_ref[rows, :], sin_ref[rows, :]
            for j in range(2):
                dq_ref[rows, j * LANES:(j + 1) * LANES] = _unrope(_unstack_heads(dqs, j), cos_b, sin_b).astype(BF16)
            return acc + jnp.where(lane == n, p_sinks * delta, 0.0)

        acc = lax.fori_loop(0, nblk // 2, lambda i, acc: block(2 * i + 1, block(2 * i, acc)),
                            jnp.zeros((rows_stacked, LANES), F32))
        ds_ref[0, 0] = -jnp.sum(acc, axis=-1, keepdims=True)
        dk_g = _unrope(_fold_halves(dkd_ref[...], g), cos_ref[...], sin_ref[...])
        dv_g = _fold_halves(dvd_ref[...], g)

        @pl.when(g == 0)
        def _():
            dk_ref[...] = dk_g
            dv_ref[...] = dv_g

        @pl.when(g != 0)
        def _():
            dk_ref[...] += dk_g
            dv_ref[...] += dv_g

    q_spec, kv_spec, _ = _attn_specs(seq)
    return _call(
        body, job, name=name, grid=(nseq, N_KV_HEADS),
        in_specs=[q_spec, kv_spec, kv_spec, q_spec, _probs_spec(nblk), _sink_probs_spec(), kv_spec, kv_spec],
        out_specs=[q_spec, kv_spec, kv_spec, pl.BlockSpec((1, 1, rows_stacked, 1), lambda b, g: (b, g, 0, 0))],
        out_shape=[jax.ShapeDtypeStruct(qr.shape, BF16), jax.ShapeDtypeStruct(k.shape, F32),
                   jax.ShapeDtypeStruct(k.shape, F32), jax.ShapeDtypeStruct((nseq, N_KV_HEADS, rows_stacked, 1), F32)],
        scratch_shapes=[pltpu.VMEM((seq, LANES), BF16), pltpu.VMEM((seq, LANES), BF16),
                        pltpu.VMEM((seq, LANES), F32), pltpu.VMEM((seq, LANES), F32)],
        args=(qr, k, v, do, probs, sink_probs, cos_t, sin_t))


CONV_COLS = 128


def _shift_down(z, by):
    t = lax.broadcasted_iota(jnp.int32, z.shape, 0)
    return jnp.where(t >= by, pltpu.roll(z, by, 0), 0.0)


def _shift_up(z, by):
    n = z.shape[0]
    t = lax.broadcasted_iota(jnp.int32, z.shape, 0)
    return jnp.where(t < n - by, pltpu.roll(z, n - by, 0), 0.0)


def conv_fwd(u, bg, cg, conv_w, seq, name):
    tokens, width = u.shape

    def body(u_ref, bg_ref, cg_ref, w_ref, o_ref):
        z = cg_ref[...] * u_ref[...]
        yy = w_ref[2:3, :] * z + w_ref[1:2, :] * _shift_down(z, 1) + w_ref[0:1, :] * _shift_down(z, 2)
        o_ref[...] = (bg_ref[...] * yy).astype(BF16)

    col = pl.BlockSpec((seq, CONV_COLS), lambda j, b: (b, j))
    return pl.pallas_call(
        body, name=name, grid=(width // CONV_COLS, tokens // seq),
        in_specs=[col, col, col, pl.BlockSpec((CONV_TAPS, CONV_COLS), lambda j, b: (0, j))],
        out_specs=col, out_shape=jax.ShapeDtypeStruct((tokens, width), BF16),
        compiler_params=_params(("parallel", "parallel")),
    )(u, bg, cg, conv_w)


def conv_bwd(dout, u, bg, cg, conv_w, seq, name):
    tokens, width = u.shape

    def body(do_ref, u_ref, bg_ref, cg_ref, w_ref, du_ref, dbg_ref, dcg_ref, dw_ref):
        uu, cg_v, do = u_ref[...], cg_ref[...], do_ref[...].astype(F32)
        z = cg_v * uu
        z1, z2 = _shift_down(z, 1), _shift_down(z, 2)
        yy = w_ref[2:3, :] * z + w_ref[1:2, :] * z1 + w_ref[0:1, :] * z2
        dbg_ref[...] = (do * yy).astype(BF16)
        dyy = do * bg_ref[...]
        dz = w_ref[2:3, :] * dyy + w_ref[1:2, :] * _shift_up(dyy, 1) + w_ref[0:1, :] * _shift_up(dyy, 2)
        du_ref[...] = (dz * cg_v).astype(BF16)
        dcg_ref[...] = (dz * uu).astype(BF16)

        @pl.when(pl.program_id(1) == 0)
        def _():
            dw_ref[...] = jnp.zeros_like(dw_ref)

        dw_ref[0:1, :] += jnp.sum(dyy * z2, axis=0, keepdims=True)
        dw_ref[1:2, :] += jnp.sum(dyy * z1, axis=0, keepdims=True)
        dw_ref[2:3, :] += jnp.sum(dyy * z, axis=0, keepdims=True)

    col = pl.BlockSpec((seq, CONV_COLS), lambda j, b: (b, j))
    w_spec = pl.BlockSpec((CONV_TAPS, CONV_COLS), lambda j, b: (0, j))
    act = jax.ShapeDtypeStruct((tokens, width), BF16)
    return pl.pallas_call(
        body, name=name, grid=(width // CONV_COLS, tokens // seq),
        in_specs=[col, col, col, col, w_spec], out_specs=[col, col, col, w_spec],
        out_shape=[act, act, act, jax.ShapeDtypeStruct((CONV_TAPS, width), F32)],
        compiler_params=_params(("parallel", "arbitrary")),
    )(dout, u, bg, cg, conv_w)


def out_fwd(x1, attn, conv, gt, w_out, ln_g, ln_b, seq, name, job=None):
    tokens, dm = x1.shape
    half = attn.shape[1]
    tm = min(TOKEN_TILE, seq)
    tiles_per_seq = seq // tm

    def body(x_ref, a_ref, c_ref, gt_ref, w_ref, lg_ref, lb_ref, xo_ref, r_ref, mi_ref, mix_ref):
        mixin = jnp.concatenate([a_ref[...], c_ref[...]], axis=1).astype(BF16)
        mi_ref[...] = mixin
        mix = _dot(mixin, w_ref[...])
        mix_ref[...] = mix.astype(BF16)
        r = DN_ALPHA * x_ref[...] + (1.0 + gt_ref[0]) * mix
        r_ref[...] = r
        xhat, _ = _ln_stats(r)
        xo_ref[...] = xhat * lg_ref[...] + lb_ref[...]

    tile = pl.BlockSpec((tm, dm), lambda i: (i, 0))
    htile = pl.BlockSpec((tm, half), lambda i: (i, 0))
    return _call(
        body, job, name=name, grid=(tokens // tm,),
        in_specs=[tile, htile, htile, _mod_spec(tiles_per_seq, dm), _const_spec(w_out.shape),
                  _const_spec((1, dm)), _const_spec((1, dm))],
        out_specs=[tile, tile, tile, tile],
        out_shape=[jax.ShapeDtypeStruct((tokens, dm), F32), jax.ShapeDtypeStruct((tokens, dm), F32),
                   jax.ShapeDtypeStruct((tokens, dm), BF16), jax.ShapeDtypeStruct((tokens, dm), BF16)],
        args=(x1, attn, conv, gt, w_out, ln_g, ln_b))


def out_bwd(dy, r, mix, gt, w_out, ln_g, seq, name, job=None):
    tokens, dm = r.shape
    half = dm // 2
    tm = min(TOKEN_TILE, seq)
    tiles_per_seq = seq // tm
    nseq = tokens // seq

    def body(dy_ref, r_ref, mix_ref, gt_ref, w_ref, lg_ref, dres_ref, da_ref, dc_ref, dmix_ref, dln_ref, dgt_ref):
        i = pl.program_id(0)
        dr, dgain, dbias = _ln_bwd(dy_ref[...], r_ref[...], lg_ref[...])

        @pl.when(i == 0)
        def _():
            dln_ref[...] = jnp.zeros_like(dln_ref)

        @pl.when(i % tiles_per_seq == 0)
        def _():
            dgt_ref[...] = jnp.zeros_like(dgt_ref)

        dln_ref[0:1, :] += dgain
        dln_ref[1:2, :] += dbias
        dgt_ref[0] += jnp.sum(dr * mix_ref[...].astype(F32), axis=0, keepdims=True)
        dres_ref[...] = DN_ALPHA * dr
        dmix = ((1.0 + gt_ref[0]) * dr).astype(BF16)
        dmix_ref[...] = dmix
        dmixin = _dot_nt(dmix, w_ref[...])
        da_ref[...] = dmixin[:, :half].astype(BF16)
        dc_ref[...] = dmixin[:, half:].astype(BF16)

    tile = pl.BlockSpec((tm, dm), lambda i: (i, 0))
    htile = pl.BlockSpec((tm, half), lambda i: (i, 0))
    return _call(
        body, job, name=name, grid=(tokens // tm,),
        in_specs=[tile, tile, tile, _mod_spec(tiles_per_seq, dm), _const_spec(w_out.shape), _const_spec((1, dm))],
        out_specs=[tile, htile, htile, tile, pl.BlockSpec((2, dm), lambda i: (0, 0)),
                   pl.BlockSpec((1, 1, dm), lambda i: (i // tiles_per_seq, 0, 0))],
        out_shape=[jax.ShapeDtypeStruct((tokens, dm), F32), jax.ShapeDtypeStruct((tokens, half), BF16),
                   jax.ShapeDtypeStruct((tokens, half), BF16), jax.ShapeDtypeStruct((tokens, dm), BF16),
                   jax.ShapeDtypeStruct((2, dm), F32), jax.ShapeDtypeStruct((nseq, 1, dm), F32)],
        args=(dy, r, mix, gt, w_out, ln_g))


def proj_bwd(parts, dres, x1, sh, sc, w_in, seq, name):
    tokens, dm = x1.shape
    tm = min(TOKEN_TILE, seq)
    tiles_per_seq = seq // tm
    nseq = tokens // seq
    widths = [p.shape[1] for p in parts]
    total = sum(widths)

    def body(*refs):
        part_refs = refs[:6]
        dres_ref, x_ref, sh_ref, sc_ref, w_ref, dx_ref, dproj_ref, h_ref, dmod_ref = refs[6:]
        dproj = jnp.concatenate([p[...].astype(BF16) for p in part_refs], axis=1)
        dproj_ref[...] = dproj
        dh = _dot(dproj, w_ref[...])
        xx = x_ref[...]
        one_sc = 1.0 + sc_ref[0]
        h_ref[...] = (xx * one_sc + sh_ref[0]).astype(BF16)
        dx_ref[...] = dres_ref[...] + dh * one_sc

        @pl.when(pl.program_id(0) % tiles_per_seq == 0)
        def _():
            dmod_ref[...] = jnp.zeros_like(dmod_ref)

        dmod_ref[0, 0:1, :] += jnp.sum(dh, axis=0, keepdims=True)
        dmod_ref[0, 1:2, :] += jnp.sum(dh * xx, axis=0, keepdims=True)

    tile = pl.BlockSpec((tm, dm), lambda i: (i, 0))
    mod = _mod_spec(tiles_per_seq, dm)
    return pl.pallas_call(
        body, name=name, grid=(tokens // tm,),
        in_specs=[pl.BlockSpec((tm, wdt), lambda i: (i, 0)) for wdt in widths]
        + [tile, tile, mod, mod, _const_spec(w_in.shape)],
        out_specs=[tile, pl.BlockSpec((tm, total), lambda i: (i, 0)), tile,
                   pl.BlockSpec((1, 2, dm), lambda i: (i // tiles_per_seq, 0, 0))],
        out_shape=[jax.ShapeDtypeStruct((tokens, dm), F32), jax.ShapeDtypeStruct((tokens, total), BF16),
                   jax.ShapeDtypeStruct((tokens, dm), BF16), jax.ShapeDtypeStruct((nseq, 2, dm), F32)],
        compiler_params=_params(("arbitrary",)),
    )(*parts, dres, x1, sh, sc, w_in)


def _rope_tables(positions):
    half = ROT_DIM // 2
    inv_freq = jnp.power(jnp.float32(ROPE_THETA), -jnp.arange(0, ROT_DIM, 2, dtype=F32) / ROT_DIM)
    lane = jnp.arange(LANES) % HEAD_DIM
    freq = jnp.where(lane < ROT_DIM, inv_freq[lane % half], 0.0)
    sign = jnp.where(lane < half, -1.0, 1.0).astype(F32)
    ang = positions.astype(F32)[:, None] * freq[None, :]
    return jnp.cos(ang), sign[None, :] * jnp.sin(ang)


def kernel(x, c, positions, w_ada, b_ada, ffn1_w_gate_up, ffn1_w_down, ln1_g, ln1_b, w_in, conv_w, attn_sinks, w_out, ln2_g, ln2_b, ffn2_w_gate_up, ffn2_w_down, ln3_g, ln3_b, loss_target, m_w_ada, m_b_ada, m_ffn1_w_gate_up, m_ffn1_w_down, m_ln1_g, m_ln1_b, m_w_in, m_conv_w, m_attn_sinks, m_w_out, m_ln2_g, m_ln2_b, m_ffn2_w_gate_up, m_ffn2_w_down, m_ln3_g, m_ln3_b, v_w_ada, v_b_ada, v_ffn1_w_gate_up, v_ffn1_w_down, v_ln1_g, v_ln1_b, v_w_in, v_conv_w, v_attn_sinks, v_w_out, v_ln2_g, v_ln2_b, v_ffn2_w_gate_up, v_ffn2_w_down, v_ln3_g, v_ln3_b):
    nseq, seq, dm = x.shape
    tokens = nseq * seq
    dev = 4 * lax.axis_index("x") + 2 * lax.axis_index("y") + lax.axis_index("c")
    ada_cols = w_ada.shape[2]
    ff = ffn1_w_down.shape[1] * N_DEV
    fc = ff // 4
    in_cols = w_in.shape[2]
    conv_cols = conv_w.shape[2]

    def t_bf16(w):
        return w[0].T.astype(BF16)

    c_all, convw_all = all_gather([c, conv_w[0]], "gather_cond")
    wgu1, wd1 = all_gather([t_bf16(ffn1_w_gate_up), ffn1_w_down[0].astype(BF16)], "gather_ffn1")
    c_all = c_all.reshape(N_DEV * nseq, dm)
    convw_full = convw_all.transpose(1, 0, 2).reshape(CONV_TAPS, N_DEV * conv_cols)
    wd1 = wd1.reshape(4, fc, dm)

    b_cols = lax.dynamic_slice(b_ada, (0, dev * ada_cols), (1, ada_cols))
    cond_all, mod_cols = ada_fwd(c_all, w_ada[0], b_cols, "ada_fwd")
    (mod_all,) = all_gather([mod_cols], "gather_mod")
    mod = lax.dynamic_slice(mod_all, (0, dev * nseq, 0), (N_DEV, nseq, ada_cols))
    mod = mod.transpose(1, 0, 2).reshape(nseq, 9, 1, dm)
    sh1, sc1, g1, sh2, sc2, g2, sh3, sc3, g3 = [mod[:, i] for i in range(9)]

    x0 = x.reshape(tokens, dm)
    spread = gather_spread_job([t_bf16(w_in), w_out[0].astype(BF16), ffn2_w_down[0].astype(BF16)])
    (x1, r1, gu1, f1), spread = ffn_fwd(x0, sh1, sc1, g1, wgu1, wd1, ln1_g, ln1_b, seq, "ffn1_fwd", job=spread)
    win, wout = run_job(gather_forward_job(spread[:2]), "gather_mix_forward")
    win = win.reshape(N_DEV * in_cols, dm)
    wout = wout.reshape(dm, dm)
    (q, k, v, u, bg, cg), (wd2,) = proj_fwd(x1, sh2, sc2, win, seq, "proj_fwd", job=gather_forward_job(spread[2:]))
    wd2 = wd2.reshape(4, fc, dm)
    cos_t, sin_t = _rope_tables(positions.reshape(tokens))
    sinks = attn_sinks[0]
    (attn, q_rot, probs, sink_probs), spread = attn_fwd(q, k, v, cos_t, sin_t, sinks, seq, "attn_fwd",
                                                        job=gather_spread_job([t_bf16(ffn2_w_gate_up)]))
    conv = conv_fwd(u, bg, cg, convw_full, seq, "conv_fwd")
    (x2, r2, mixin, mix), (wgu2,) = out_fwd(x1, attn, conv, g2, wout, ln2_g, ln2_b, seq, "out_fwd",
                                            job=gather_forward_job(spread))
    target = loss_target.reshape(tokens, dm)
    (dy3, loss_part, r3, gu3, f3), _ = ffn_fwd(x2, sh3, sc3, g3, wgu2, wd2, ln3_g, ln3_b, seq, "ffn2_fwd", target=target)

    (dx2, dgu3, df3, a3, h3, dln3, dmod3), _ = ffn_bwd(dy3, r3, x2, f3, gu3, sh3, sc3, g3, wgu2, wd2, ln3_g, seq, "ffn2_bwd")
    g_wd2 = tn_matmul(a3, df3[None], "ffn2_dwd")[0][0].reshape(N_DEV, ff // N_DEV, dm)
    g_wgu2 = tn_matmul(dgu3, h3[None], "ffn2_dwgu")[0][0].reshape(N_DEV, fc, dm)
    (dres2, dattn, dconv, dmix, dln2, dg2), swapped = out_bwd(dx2, r2, mix, g2, wout, ln2_g, seq, "out_bwd",
                                                              job=swap_job([g_wgu2, g_wd2]))
    p_wgu2, own_wgu2 = pair_sum(g_wgu2, swapped[0], "pair_wgu2")
    p_wd2, own_wd2 = pair_sum(g_wd2, swapped[1], "pair_wd2")
    du, dbg, dcg, dconvw = conv_bwd(dconv, u, bg, cg, convw_full, seq, "conv_bwd")
    (dq, dk, dv, dsink_rows), (far_wgu2, far_wd2) = attn_bwd(
        q_rot, k, v, dattn, probs, sink_probs, cos_t, sin_t, seq, "attn_bwd", job=chip_exchange_job([p_wgu2, p_wd2]))
    parts = [dq, dk, dv, du, dbg, dcg]
    dx1, dproj, h2, dmod2 = proj_bwd(parts, dres2, x1, sh2, sc2, win, seq, "proj_bwd")
    (dx0, dgu1, df1, a1, h1, dln1, dmod1), _ = ffn_bwd(
        dx1, r1, x0, f1, gu1, sh1, sc1, g1, wgu1, wd1, ln1_g, seq, "ffn1_bwd")

    dmod = jnp.concatenate([dmod1, dmod2, dg2, dmod3], axis=1).reshape(nseq, 9 * dm)
    half = dm // 2
    jobs = _Jobs([gather_spread_job([dmod])])
    (g_wd1,), res = tn_matmul(a1, df1[None], "ffn1_dwd", job=jobs)
    (dmod_spread,) = jobs.split(res)
    g_wd1 = g_wd1.reshape(N_DEV, ff // N_DEV, dm)
    jobs = _Jobs([swap_job([g_wd1]), gather_forward_job(dmod_spread)])
    (g_l,), res = tn_matmul(dgu1, h1[None], "ffn1_dwgu_l", job=jobs, b_cols=(0, half))
    (sw_wd1,), (dmod_all,) = jobs.split(res)
    g_l = g_l.reshape(N_DEV, fc, half)
    p_wd1, own_wd1 = pair_sum(g_wd1, sw_wd1, "pair_wd1")
    jobs = _Jobs([chip_exchange_job([p_wd1]), swap_job([g_l])])
    (g_r,), res = tn_matmul(dgu1, h1[None], "ffn1_dwgu_r", job=jobs, b_cols=(1, half))
    (far_wd1,), (sw_l,) = jobs.split(res)
    g_r = g_r.reshape(N_DEV, fc, half)
    p_l, own_l = pair_sum(g_l, sw_l, "pair_wgu1_l")
    jobs = _Jobs([chip_exchange_job([p_l]), swap_job([g_r])])
    (g_win,), res = tn_matmul(dproj[None], h2[None], "dwin", job=jobs)
    (far_l,), (sw_r,) = jobs.split(res)
    g_win = g_win.reshape(N_DEV, in_cols, dm)
    p_r, own_r = pair_sum(g_r, sw_r, "pair_wgu1_r")
    jobs = _Jobs([chip_exchange_job([p_r]), swap_job([g_win])])
    (g_wout,), res = tn_matmul(mixin[None], dmix[None], "dwout", job=jobs)
    (far_r,), (sw_win,) = jobs.split(res)
    g_wout = g_wout.reshape(N_DEV, dm // N_DEV, dm)
    p_win, own_win = pair_sum(g_win, sw_win, "pair_win")
    jobs = _Jobs([chip_exchange_job([p_win]), swap_job([g_wout])])
    (far_win,), (sw_wout,) = jobs.split(run_job(jobs, "rs_tail_win"))
    p_wout, own_wout = pair_sum(g_wout, sw_wout, "pair_wout")
    (far_wout,) = run_job(chip_exchange_job([p_wout]), "rs_tail_wout")

    grads = {
        "ffn1_w_gate_up": jnp.concatenate([own_l, own_r], axis=1), "ffn1_w_down": own_wd1,
        "w_in": own_win, "w_out": own_wout, "ffn2_w_gate_up": own_wgu2, "ffn2_w_down": own_wd2,
    }
    others = {"ffn1_w_gate_up": jnp.concatenate([far_l, far_r], axis=2), "ffn1_w_down": far_wd1,
              "w_in": far_win, "w_out": far_wout, "ffn2_w_gate_up": far_wgu2, "ffn2_w_down": far_wd2}

    dmod_cols = lax.dynamic_slice(dmod_all.reshape(N_DEV * nseq, 9 * dm), (0, dev * ada_cols), (N_DEV * nseq, ada_cols))
    grads["w_ada"], gb_cols = ada_bwd(cond_all, dmod_cols, "ada_bwd")

    dsinks = jnp.sum(dsink_rows.reshape(nseq, N_Q_HEADS, ATTN_BLOCK), axis=(0, 2))
    small = jnp.zeros((8, dm), F32)
    small = small.at[0:2].set(dln1).at[2:4].set(dln2).at[4:6].set(dln3)
    small = small.at[6, 0:N_Q_HEADS].set(dsinks).at[7, 0].set(loss_part[0, 0])
    small_all, dconvw_all, gb_all = all_gather([small, dconvw, gb_cols], "gather_small")
    small_sum = sum_devices(small_all, "sum_small")
    dconvw_sum = sum_devices(dconvw_all, "sum_convw")
    loss = small_sum[7, 0]
    grads["b_ada"] = gb_all.reshape(1, N_DEV * ada_cols)
    grads["conv_w"] = lax.dynamic_slice(dconvw_sum, (0, dev * conv_cols), (CONV_TAPS, conv_cols))
    grads["attn_sinks"] = small_sum[6:7, 0:N_Q_HEADS]
    for i, nm in enumerate(["ln1_g", "ln1_b", "ln2_g", "ln2_b", "ln3_g", "ln3_b"]):
        grads[nm] = small_sum[i:i + 1]

    given = dict(w_ada=(w_ada, m_w_ada, v_w_ada), b_ada=(b_ada, m_b_ada, v_b_ada),
                 ffn1_w_gate_up=(ffn1_w_gate_up, m_ffn1_w_gate_up, v_ffn1_w_gate_up),
                 ffn1_w_down=(ffn1_w_down, m_ffn1_w_down, v_ffn1_w_down),
                 ln1_g=(ln1_g, m_ln1_g, v_ln1_g), ln1_b=(ln1_b, m_ln1_b, v_ln1_b),
                 w_in=(w_in, m_w_in, v_w_in), conv_w=(conv_w, m_conv_w, v_conv_w),
                 attn_sinks=(attn_sinks, m_attn_sinks, v_attn_sinks), w_out=(w_out, m_w_out, v_w_out),
                 ln2_g=(ln2_g, m_ln2_g, v_ln2_g), ln2_b=(ln2_b, m_ln2_b, v_ln2_b),
                 ffn2_w_gate_up=(ffn2_w_gate_up, m_ffn2_w_gate_up, v_ffn2_w_gate_up),
                 ffn2_w_down=(ffn2_w_down, m_ffn2_w_down, v_ffn2_w_down),
                 ln3_g=(ln3_g, m_ln3_g, v_ln3_g), ln3_b=(ln3_b, m_ln3_b, v_ln3_b))
    order = ["w_ada", "b_ada", "ffn1_w_gate_up", "ffn1_w_down", "ln1_g", "ln1_b", "w_in", "conv_w", "attn_sinks",
             "w_out", "ln2_g", "ln2_b", "ffn2_w_gate_up", "ffn2_w_down", "ln3_g", "ln3_b"]
    transposed = ("ffn1_w_gate_up", "ffn2_w_gate_up", "w_in")
    out_g, out_d, out_m, out_v = [], [], [], []
    for nm in order:
        shape = given[nm][0].shape
        two_d = (shape[-2], shape[-1])
        if nm in transposed:
            w2, m2, v2 = [t[0].T for t in given[nm]]
            back = lambda t: t.T[None]
        else:
            w2, m2, v2 = [t.reshape(two_d) for t in given[nm]]
            back = lambda t, shape=shape: t.reshape(shape)
        res = adamw(w2, grads[nm].reshape(w2.shape), m2, v2, "adamw_" + nm, others=others.get(nm))
        for lst, t in zip((out_g, out_d, out_m, out_v), res):
            lst.append(back(t))
    grad_x = dx0.reshape(nseq, seq, dm)
    return (loss, grad_x, *out_g, *out_d, *out_m, *out_v)
```

```python
import functools

import jax
import jax.numpy as jnp
from jax import lax
from jax.experimental import pallas as pl
from jax.experimental.pallas import tpu as pltpu

F32 = jnp.float32
BF16 = jnp.bfloat16
MESH = pl.DeviceIdType.MESH

N_DEV = 8
N_CHIP = 4
HEAD_DIM = 64
N_Q_HEADS = 8
N_KV_HEADS = 2
GQA_GROUP = N_Q_HEADS // N_KV_HEADS
ATTN_BLOCK = 128
ROT_DIM = 16
ROPE_THETA = 500000.0
CONV_TAPS = 3
LN_EPS = 1e-5
DN_ALPHA = 2.0 ** 0.25
ADAM_LR = 0.001
ADAM_B1 = 0.9
ADAM_B2 = 0.999
ADAM_EPS = 1e-08
ADAM_WD = 0.01
ADAM_STEP = 10
NEG_BIG = -1e30

VMEM_LIMIT = 56 * 1024 * 1024
TOKEN_TILE = 256
FFN_FWD_TILE = 512
TN_VMEM_BUDGET = 36 * 1024 * 1024


def _params(semantics=None, vmem=VMEM_LIMIT):
    return pltpu.CompilerParams(dimension_semantics=semantics, vmem_limit_bytes=vmem)


def _dot(a, b):
    return jnp.dot(a, b, preferred_element_type=F32)


def _dot_nt(a, b):
    return lax.dot_general(a, b, (((1,), (1,)), ((), ())), preferred_element_type=F32)


def _dot_tn(a, b):
    return lax.dot_general(a, b, (((0,), (0,)), ((), ())), preferred_element_type=F32)


def _sigmoid(x):
    return pl.reciprocal(1.0 + jnp.exp(-x), approx=True)


def _ln_stats(r):
    mu = jnp.mean(r, axis=-1, keepdims=True)
    d = r - mu
    var = jnp.mean(d * d, axis=-1, keepdims=True)
    rstd = lax.rsqrt(var + LN_EPS)
    return d * rstd, rstd


def _ln_bwd(dy, r, g):
    xhat, rstd = _ln_stats(r)
    dxhat = dy * g
    c1 = jnp.mean(dxhat, axis=-1, keepdims=True)
    c2 = jnp.mean(dxhat * xhat, axis=-1, keepdims=True)
    dr = rstd * (dxhat - c1 - xhat * c2)
    return dr, jnp.sum(dy * xhat, axis=0, keepdims=True), jnp.sum(dy, axis=0, keepdims=True)


def _const_spec(shape):
    nd = len(shape)
    return pl.BlockSpec(shape, lambda *_: (0,) * nd, pipeline_mode=pl.Buffered(1))


def all_gather(arrs, name):
    n = len(arrs)

    def body(*refs):
        ins, outs = refs[:n], refs[n:2 * n]
        send_sems, recv_sems, local_sems = refs[2 * n:]
        x, y, c = lax.axis_index("x"), lax.axis_index("y"), lax.axis_index("c")
        me, sibling = (x, y, c), (x, y, 1 - c)
        chips = [(1 - x, y), (x, 1 - y), (1 - x, 1 - y)]

        def slot(i, p):
            return outs[i].at[4 * p[0] + 2 * p[1] + p[2]]

        def copy(i, k, block, to, src=None):
            return pltpu.make_async_remote_copy(
                src_ref=slot(i, block) if src is None else src, dst_ref=slot(i, block),
                send_sem=send_sems.at[i, k], recv_sem=recv_sems.at[i, k],
                device_id=to, device_id_type=MESH)

        mine = [pltpu.make_async_copy(ins[i], slot(i, me), local_sems.at[i]) for i in range(n)]
        for cp in mine:
            cp.start()
        first = []
        for i in range(n):
            first.append(copy(i, 0, me, sibling, src=ins[i]))
            first += [copy(i, 1 + j, me, (*chip, c), src=ins[i]) for j, chip in enumerate(chips)]
        for cp in first:
            cp.start()
        passed = []
        for j, chip in enumerate(chips):
            for i in range(n):
                copy(i, 1 + j, (*chip, c), me).wait_recv()
                cp = copy(i, 4 + j, (*chip, c), sibling)
                cp.start()
                passed.append(cp)
        for i in range(n):
            copy(i, 0, sibling, me).wait_recv()
            for j, chip in enumerate(chips):
                copy(i, 4 + j, (*chip, 1 - c), me).wait_recv()
        for cp in first + passed:
            cp.wait_send()
        for cp in mine:
            cp.wait()

    any_spec = pl.BlockSpec(memory_space=pl.ANY)
    return pl.pallas_call(
        body, name=name,
        out_shape=[jax.ShapeDtypeStruct((N_DEV, *a.shape), a.dtype) for a in arrs],
        in_specs=[any_spec] * n, out_specs=[any_spec] * n,
        scratch_shapes=[pltpu.SemaphoreType.DMA((n, 7)), pltpu.SemaphoreType.DMA((n, 7)),
                        pltpu.SemaphoreType.DMA((n,))],
    )(*arrs)


RS_ROWS = 32


def reduce_scatter(g, name):
    _, rows, cols = g.shape
    nblk = rows // RS_ROWS
    assert nblk * RS_ROWS == rows

    def body(g_ref, out_ref, r1_ref, p_ref, r2_ref, send_sems, recv_sems):
        x, y, c = lax.axis_index("x"), lax.axis_index("y"), lax.axis_index("c")
        sibling = (x, y, 1 - c)
        q_me = 2 * x + y
        swaps = []
        for q in range(N_CHIP):
            cp = pltpu.make_async_remote_copy(
                src_ref=g_ref.at[2 * q + (1 - c)], dst_ref=r1_ref.at[q],
                send_sem=send_sems.at[q], recv_sem=recv_sems.at[q], device_id=sibling, device_id_type=MESH)
            cp.start()
            swaps.append(cp)
        for cp in swaps:
            cp.wait_recv()

        def pair_sum(i, carry):
            r = pl.ds(pl.multiple_of(i * RS_ROWS, RS_ROWS), RS_ROWS)
            for q in range(N_CHIP):
                p_ref[q, r, :] = (g_ref[2 * q + c, r, :].astype(F32) + r1_ref[q, r, :].astype(F32)).astype(BF16)
            return carry

        lax.fori_loop(0, nblk, pair_sum, 0)
        chips = [(1 - x, y), (x, 1 - y), (1 - x, 1 - y)]
        sends = []
        for k, chip in enumerate(chips):
            cp = pltpu.make_async_remote_copy(
                src_ref=p_ref.at[2 * chip[0] + chip[1]], dst_ref=r2_ref.at[k],
                send_sem=send_sems.at[N_CHIP + k], recv_sem=recv_sems.at[N_CHIP + k],
                device_id=(*chip, c), device_id_type=MESH)
            cp.start()
            sends.append(cp)
        for cp in sends:
            cp.wait_recv()

        def total(i, carry):
            r = pl.ds(pl.multiple_of(i * RS_ROWS, RS_ROWS), RS_ROWS)
            acc = g_ref[2 * q_me + c, r, :].astype(F32) + r1_ref[q_me, r, :].astype(F32)
            for k in range(3):
                acc = acc + r2_ref[k, r, :].astype(F32)
            out_ref[r, :] = acc
            return carry

        lax.fori_loop(0, nblk, total, 0)
        for cp in swaps + sends:
            cp.wait_send()

    vmem = pl.BlockSpec(memory_space=pltpu.VMEM)
    return pl.pallas_call(
        body, name=name,
        out_shape=jax.ShapeDtypeStruct((rows, cols), F32),
        in_specs=[vmem], out_specs=vmem,
        scratch_shapes=[pltpu.VMEM((N_CHIP, rows, cols), BF16), pltpu.VMEM((N_CHIP, rows, cols), BF16),
                        pltpu.VMEM((3, rows, cols), BF16),
                        pltpu.SemaphoreType.DMA((N_CHIP + 3,)), pltpu.SemaphoreType.DMA((N_CHIP + 3,))],
        compiler_params=_params(),
    )(g)


def _place():
    x, y, c = lax.axis_index("x"), lax.axis_index("y"), lax.axis_index("c")
    return x, y, c, [(1 - x, y), (x, 1 - y), (1 - x, 1 - y)]


def _slot(p):
    return 4 * p[0] + 2 * p[1] + p[2]


class _Job:
    def __init__(self, ins, outs, nsem, copies, aliases=None, local=None):
        self.ins, self.outs, self.nsem, self.copies = list(ins), list(outs), nsem, copies
        self.aliases = aliases or {}
        self.local = local

    def scratch(self):
        s = [pltpu.SemaphoreType.DMA(self.nsem), pltpu.SemaphoreType.DMA(self.nsem)]
        if self.local is not None:
            s.append(pltpu.SemaphoreType.DMA((len(self.ins),)))
        return s

    def start(self, ins, outs, sems):
        if self.local is not None:
            for cp in self.local(ins, outs, sems[2]):
                cp.start()
        for cp in self.copies(ins, outs, sems[0], sems[1])[0]:
            cp.start()

    def finish(self, ins, outs, sems):
        started, awaited = self.copies(ins, outs, sems[0], sems[1])
        for cp in awaited:
            cp.wait_recv()
        for cp in started:
            cp.wait_send()
        if self.local is not None:
            for cp in self.local(ins, outs, sems[2]):
                cp.wait()


class _Jobs:
    def __init__(self, jobs):
        self.jobs = jobs
        self.ins = [a for j in jobs for a in j.ins]
        self.outs = [o for j in jobs for o in j.outs]
        self.aliases = {}
        at_in = at_out = 0
        for j in jobs:
            self.aliases.update({at_in + i: at_out + o for i, o in j.aliases.items()})
            at_in, at_out = at_in + len(j.ins), at_out + len(j.outs)

    def scratch(self):
        return [s for j in self.jobs for s in j.scratch()]

    def _each(self, ins, outs, sems):
        at_in = at_out = at_sem = 0
        for j in self.jobs:
            n_in, n_out, n_sem = len(j.ins), len(j.outs), len(j.scratch())
            yield j, ins[at_in:at_in + n_in], outs[at_out:at_out + n_out], sems[at_sem:at_sem + n_sem]
            at_in, at_out, at_sem = at_in + n_in, at_out + n_out, at_sem + n_sem

    def start(self, ins, outs, sems):
        for j, i, o, s in self._each(ins, outs, sems):
            j.start(i, o, s)

    def finish(self, ins, outs, sems):
        for j, i, o, s in self._each(ins, outs, sems):
            j.finish(i, o, s)

    def split(self, results):
        at, parts = 0, []
        for j in self.jobs:
            parts.append(results[at:at + len(j.outs)])
            at += len(j.outs)
        return parts


def _remote(src, dst, send, recv, idx, to):
    return pltpu.make_async_remote_copy(src_ref=src, dst_ref=dst, send_sem=send.at[idx], recv_sem=recv.at[idx],
                                        device_id=to, device_id_type=MESH)


def gather_spread_job(shards):
    def copies(ins, outs, send, recv):
        x, y, c, chips = _place()
        me = (x, y, c)
        peers = [(x, y, 1 - c)] + [(*chip, c) for chip in chips]
        started, awaited = [], []
        for i, (src, dst) in enumerate(zip(ins, outs)):
            for k, peer in enumerate(peers):
                started.append(_remote(src, dst.at[_slot(me)], send, recv, (i, k), peer))
                awaited.append(_remote(src, dst.at[_slot(peer)], send, recv, (i, k), peer))
        return started, awaited

    def local(ins, outs, sems):
        x, y, c, _ = _place()
        return [pltpu.make_async_copy(src, dst.at[_slot((x, y, c))], sems.at[i])
                for i, (src, dst) in enumerate(zip(ins, outs))]

    outs = [jax.ShapeDtypeStruct((N_DEV, *a.shape), a.dtype) for a in shards]
    return _Job(shards, outs, (len(shards), 4), copies, local=local)


def gather_forward_job(fulls):
    def copies(ins, outs, send, recv):
        x, y, c, chips = _place()
        started, awaited = [], []
        for i, buf in enumerate(outs):
            for j, chip in enumerate(chips):
                mine, theirs = buf.at[_slot((*chip, c))], buf.at[_slot((*chip, 1 - c))]
                started.append(_remote(mine, mine, send, recv, (i, j), (x, y, 1 - c)))
                awaited.append(_remote(theirs, theirs, send, recv, (i, j), (x, y, 1 - c)))
        return started, awaited

    outs = [jax.ShapeDtypeStruct(a.shape, a.dtype) for a in fulls]
    return _Job(fulls, outs, (len(fulls), 3), copies, aliases={i: i for i in range(len(fulls))})


def swap_job(gs):
    def copies(ins, outs, send, recv):
        x, y, c, _ = _place()
        started, awaited = [], []
        for i, (g, r1) in enumerate(zip(ins, outs)):
            for q in range(N_CHIP):
                started.append(_remote(g.at[2 * q + (1 - c)], r1.at[q], send, recv, (i, q), (x, y, 1 - c)))
                awaited.append(_remote(g.at[2 * q + c], r1.at[q], send, recv, (i, q), (x, y, 1 - c)))
        return started, awaited

    outs = [jax.ShapeDtypeStruct((N_CHIP, *g.shape[1:]), g.dtype) for g in gs]
    return _Job(gs, outs, (len(gs), N_CHIP), copies)


def chip_exchange_job(ps):
    def copies(ins, outs, send, recv):
        x, y, c, chips = _place()
        started, awaited = [], []
        for i, (p, r2) in enumerate(zip(ins, outs)):
            for k, chip in enumerate(chips):
                started.append(_remote(p.at[2 * chip[0] + chip[1]], r2.at[k], send, recv, (i, k), (*chip, c)))
                awaited.append(_remote(p.at[2 * x + y], r2.at[k], send, recv, (i, k), (*chip, c)))
        return started, awaited

    outs = [jax.ShapeDtypeStruct((3, *p.shape[1:]), p.dtype) for p in ps]
    return _Job(ps, outs, (len(ps), 3), copies)


def _call(body, job, *, name, grid, in_specs, out_specs, out_shape, args, scratch_shapes=(), vmem=VMEM_LIMIT):
    if job is None:
        res = pl.pallas_call(
            body, name=name, grid=grid, in_specs=in_specs, out_specs=out_specs, out_shape=out_shape,
            scratch_shapes=list(scratch_shapes), compiler_params=_params(("arbitrary",) * len(grid), vmem),
        )(*args)
        return res, []
    n_in, n_out, n_scr = len(in_specs), len(out_specs), len(scratch_shapes)
    j_in, j_out = len(job.ins), len(job.outs)

    def with_copies(*refs):
        at = 0
        ins = refs[at:at + n_in]; at += n_in
        jins = refs[at:at + j_in]; at += j_in
        outs = refs[at:at + n_out]; at += n_out
        jouts = refs[at:at + j_out]; at += j_out
        scr = refs[at:at + n_scr]; at += n_scr
        sems = refs[at:]
        ids = [pl.program_id(d) for d in range(len(grid))]
        first = functools.reduce(jnp.logical_and, [i == 0 for i in ids])
        last = functools.reduce(jnp.logical_and, [i == n - 1 for i, n in zip(ids, grid)])

        @pl.when(first)
        def _():
            job.start(jins, jouts, sems)

        body(*ins, *outs, *scr)

        @pl.when(last)
        def _():
            job.finish(jins, jouts, sems)

    any_spec = pl.BlockSpec(memory_space=pl.ANY)
    res = pl.pallas_call(
        with_copies, name=name, grid=grid,
        in_specs=list(in_specs) + [any_spec] * j_in, out_specs=list(out_specs) + [any_spec] * j_out,
        out_shape=list(out_shape) + list(job.outs),
        input_output_aliases={n_in + i: n_out + o for i, o in job.aliases.items()},
        scratch_shapes=list(scratch_shapes) + job.scratch(),
        compiler_params=_params(("arbitrary",) * len(grid), vmem),
    )(*args, *job.ins)
    return res[:n_out], res[n_out:]


def run_job(job, name):
    def body(*refs):
        j_in, j_out = len(job.ins), len(job.outs)
        ins, outs, sems = refs[:j_in], refs[j_in:j_in + j_out], refs[j_in + j_out:]
        job.start(ins, outs, sems)
        job.finish(ins, outs, sems)

    any_spec = pl.BlockSpec(memory_space=pl.ANY)
    return pl.pallas_call(
        body, name=name, in_specs=[any_spec] * len(job.ins), out_specs=[any_spec] * len(job.outs),
        out_shape=list(job.outs), input_output_aliases=dict(job.aliases), scratch_shapes=job.scratch(),
    )(*job.ins)


def pair_sum(g, r1, name):
    _, rows, cols = g.shape
    rb = next(cand for cand in range(min(rows, 512), 0, -16) if rows % cand == 0)

    def body(g_ref, r1_ref, p_ref, own_ref):
        x, y, c, _ = _place()
        s = g_ref[c].astype(F32) + r1_ref[0].astype(F32)
        p_ref[0] = s.astype(BF16)

        @pl.when(pl.program_id(1) == 2 * x + y)
        def _():
            own_ref[...] = s

    return pl.pallas_call(
        body, name=name, grid=(rows // rb, N_CHIP),
        in_specs=[pl.BlockSpec((2, rb, cols), lambda i, q: (q, i, 0)), pl.BlockSpec((1, rb, cols), lambda i, q: (q, i, 0))],
        out_specs=[pl.BlockSpec((1, rb, cols), lambda i, q: (q, i, 0)), pl.BlockSpec((rb, cols), lambda i, q: (i, 0))],
        out_shape=[jax.ShapeDtypeStruct((N_CHIP, rows, cols), BF16), jax.ShapeDtypeStruct((rows, cols), F32)],
        compiler_params=_params(("arbitrary", "arbitrary")),
    )(g, r1)


def sum_devices(a, name):
    def body(a_ref, o_ref):
        acc = a_ref[0]
        for d in range(1, N_DEV):
            acc = acc + a_ref[d]
        o_ref[...] = acc

    return pl.pallas_call(body, name=name, out_shape=jax.ShapeDtypeStruct(a.shape[1:], F32))(a)


def adamw(w, g, m, v, name, others=None):
    rows, cols = w.shape
    rb = rows
    for cand in range(min(rows, 512), 7, -8):
        if rows % cand == 0 and cand % 8 == 0:
            rb = cand
            break

    def body(*refs):
        if others is None:
            w_ref, g_ref, m_ref, v_ref, d_ref, nm_ref, nv_ref = refs
            gg = g_ref[...]
        else:
            w_ref, g_ref, m_ref, v_ref, r2_ref, go_ref, d_ref, nm_ref, nv_ref = refs
            gg = g_ref[...]
            for k in range(3):
                gg = gg + r2_ref[k].astype(F32)
            go_ref[...] = gg
        nm = ADAM_B1 * m_ref[...] + (1.0 - ADAM_B1) * gg
        nv = ADAM_B2 * v_ref[...] + (1.0 - ADAM_B2) * (gg * gg)
        m_hat = nm / (1.0 - ADAM_B1 ** ADAM_STEP)
        v_hat = nv / (1.0 - ADAM_B2 ** ADAM_STEP)
        d_ref[...] = -ADAM_LR * (m_hat / (jnp.sqrt(v_hat) + ADAM_EPS) + ADAM_WD * w_ref[...])
        nm_ref[...] = nm
        nv_ref[...] = nv

    spec = pl.BlockSpec((rb, cols), lambda i: (i, 0))
    out = jax.ShapeDtypeStruct((rows, cols), F32)
    in_specs, args = [spec] * 4, [w, g, m, v]
    if others is not None:
        in_specs.append(pl.BlockSpec((3, rb, cols), lambda i: (0, i, 0)))
        args.append(others)
    n_out = 3 if others is None else 4
    res = pl.pallas_call(
        body, name=name, grid=(rows // rb,), in_specs=in_specs, out_specs=[spec] * n_out,
        out_shape=[out] * n_out, compiler_params=_params(("parallel",)),
    )(*args)
    return (g, *res) if others is None else tuple(res)


def ada_fwd(c_all, w_cols, b_cols, name):
    def body(c_ref, w_ref, b_ref, cond_ref, mod_ref):
        cc = c_ref[...]
        cond = (cc * _sigmoid(cc)).astype(BF16)
        cond_ref[...] = cond
        mod_ref[...] = _dot(cond, w_ref[...].astype(BF16)) + b_ref[...]

    n, cols = c_all.shape[0], w_cols.shape[1]
    return pl.pallas_call(
        body, name=name,
        out_shape=[jax.ShapeDtypeStruct(c_all.shape, BF16), jax.ShapeDtypeStruct((n, cols), F32)],
        compiler_params=_params(),
    )(c_all, w_cols, b_cols)


def ada_bwd(cond_all, dmod_cols, name):
    def body(c_ref, d_ref, gw_ref, gb_ref):
        d = d_ref[...]
        gw_ref[...] = _dot_tn(c_ref[...], d.astype(BF16))
        gb_ref[...] = jnp.sum(d, axis=0, keepdims=True)

    dm, cols = cond_all.shape[1], dmod_cols.shape[1]
    return pl.pallas_call(
        body, name=name,
        out_shape=[jax.ShapeDtypeStruct((dm, cols), F32), jax.ShapeDtypeStruct((1, cols), F32)],
        compiler_params=_params(),
    )(cond_all, dmod_cols)


def _mod_spec(tiles_per_seq, dm):
    return pl.BlockSpec((1, 1, dm), lambda i: (i // tiles_per_seq, 0, 0))


def ffn_fwd(x, sh, sc, gt, wgu, wd, ln_g, ln_b, seq, name, target=None, job=None):
    tokens, dm = x.shape
    fc = wgu.shape[1]
    tm = min(FFN_FWD_TILE, seq)
    tiles_per_seq = seq // tm
    with_loss = target is not None

    def body(*refs):
        if with_loss:
            (x_ref, sh_ref, sc_ref, gt_ref, wgu_ref, wd_ref, lg_ref, lb_ref, t_ref,
             xo_ref, loss_ref, r_ref, gu_ref, f_ref) = refs
        else:
            (x_ref, sh_ref, sc_ref, gt_ref, wgu_ref, wd_ref, lg_ref, lb_ref,
             xo_ref, r_ref, gu_ref, f_ref) = refs
        xx = x_ref[...]
        h = (xx * (1.0 + sc_ref[0]) + sh_ref[0]).astype(BF16)
        acc = jnp.zeros((tm, dm), F32)
        for k in range(4):
            gk = _dot_nt(h, wgu_ref[k])
            uk = _dot_nt(h, wgu_ref[k + 4])
            gu_ref[k] = gk.astype(BF16)
            gu_ref[k + 4] = uk.astype(BF16)
            a = (gk * _sigmoid(gk) * uk).astype(BF16)
            acc = acc + _dot(a, wd_ref[k])
        f_ref[...] = acc.astype(BF16)
        r = DN_ALPHA * xx + (0.5 * (1.0 + gt_ref[0])) * acc
        r_ref[...] = r
        xhat, _ = _ln_stats(r)
        yy = xhat * lg_ref[...] + lb_ref[...]
        if with_loss:
            err = yy - t_ref[...]
            xo_ref[...] = err * (1.0 / dm)

            @pl.when(pl.program_id(0) == 0)
            def _():
                loss_ref[...] = jnp.zeros_like(loss_ref)

            loss_ref[...] += jnp.full((1, 128), (0.5 / dm) * jnp.sum(err * err), F32)
        else:
            xo_ref[...] = yy

    tile = pl.BlockSpec((tm, dm), lambda i: (i, 0))
    mod = _mod_spec(tiles_per_seq, dm)
    in_specs = [tile, mod, mod, mod, _const_spec(wgu.shape), _const_spec(wd.shape),
                _const_spec((1, dm)), _const_spec((1, dm))]
    args = [x, sh, sc, gt, wgu, wd, ln_g, ln_b]
    out_specs = [tile]
    out_shape = [jax.ShapeDtypeStruct((tokens, dm), F32)]
    if with_loss:
        in_specs.append(tile)
        args.append(target)
        out_specs.append(pl.BlockSpec((1, 128), lambda i: (0, 0)))
        out_shape.append(jax.ShapeDtypeStruct((1, 128), F32))
    out_specs += [tile, pl.BlockSpec((8, tm, fc), lambda i: (0, i, 0)), tile]
    out_shape += [jax.ShapeDtypeStruct((tokens, dm), F32), jax.ShapeDtypeStruct((8, tokens, fc), BF16),
                  jax.ShapeDtypeStruct((tokens, dm), BF16)]
    return _call(body, job, name=name, grid=(tokens // tm,), in_specs=in_specs, out_specs=out_specs,
                 out_shape=out_shape, args=args)


def ffn_bwd(dy, r, x, f, gu, sh, sc, gt, wgu, wd, ln_g, seq, name, job=None):
    tokens, dm = x.shape
    fc = wgu.shape[1]
    tm = min(TOKEN_TILE, seq)
    tiles_per_seq = seq // tm
    nseq = tokens // seq

    def body(dy_ref, r_ref, x_ref, f_ref, gu_ref, sh_ref, sc_ref, gt_ref, wgu_ref, wd_ref, lg_ref,
             dx_ref, dgu_ref, df_ref, a_ref, h_ref, dln_ref, dmod_ref):
        i = pl.program_id(0)
        dr, dgain, dbias = _ln_bwd(dy_ref[...], r_ref[...], lg_ref[...])

        @pl.when(i == 0)
        def _():
            dln_ref[...] = jnp.zeros_like(dln_ref)

        @pl.when(i % tiles_per_seq == 0)
        def _():
            dmod_ref[...] = jnp.zeros_like(dmod_ref)

        dln_ref[0:1, :] += dgain
        dln_ref[1:2, :] += dbias
        df32 = (0.5 * (1.0 + gt_ref[0])) * dr
        df = df32.astype(BF16)
        df_ref[...] = df
        dgate = jnp.sum(dr * (0.5 * f_ref[...].astype(F32)), axis=0, keepdims=True)
        xx = x_ref[...]
        one_sc = 1.0 + sc_ref[0]
        h = (xx * one_sc + sh_ref[0]).astype(BF16)
        h_ref[...] = h
        dh = jnp.zeros((tm, dm), F32)
        for k in range(4):
            da = _dot_nt(df, wd_ref[k])
            gk = gu_ref[k].astype(F32)
            uk = gu_ref[k + 4].astype(F32)
            sg = _sigmoid(gk)
            sil = gk * sg
            a_ref[k] = (sil * uk).astype(BF16)
            du = (da * sil).astype(BF16)
            dg = (da * uk * (sg * (1.0 + gk * (1.0 - sg)))).astype(BF16)
            dgu_ref[k] = dg
            dgu_ref[k + 4] = du
            dh = dh + _dot(dg, wgu_ref[k]) + _dot(du, wgu_ref[k + 4])
        dx_ref[...] = DN_ALPHA * dr + dh * one_sc
        dmod_ref[0, 0:1, :] += jnp.sum(dh, axis=0, keepdims=True)
        dmod_ref[0, 1:2, :] += jnp.sum(dh * xx, axis=0, keepdims=True)
        dmod_ref[0, 2:3, :] += dgate

    tile = pl.BlockSpec((tm, dm), lambda i: (i, 0))
    mod = _mod_spec(tiles_per_seq, dm)
    gu_spec = pl.BlockSpec((8, tm, fc), lambda i: (0, i, 0))
    return _call(
        body, job, name=name, grid=(tokens // tm,),
        in_specs=[tile, tile, tile, tile, gu_spec, mod, mod, mod, _const_spec(wgu.shape), _const_spec(wd.shape),
                  _const_spec((1, dm))],
        out_specs=[tile, gu_spec, tile, pl.BlockSpec((4, tm, fc), lambda i: (0, i, 0)), tile,
                   pl.BlockSpec((2, dm), lambda i: (0, 0)),
                   pl.BlockSpec((1, 3, dm), lambda i: (i // tiles_per_seq, 0, 0))],
        out_shape=[jax.ShapeDtypeStruct((tokens, dm), F32), jax.ShapeDtypeStruct((8, tokens, fc), BF16),
                   jax.ShapeDtypeStruct((tokens, dm), BF16), jax.ShapeDtypeStruct((4, tokens, fc), BF16),
                   jax.ShapeDtypeStruct((tokens, dm), BF16), jax.ShapeDtypeStruct((2, dm), F32),
                   jax.ShapeDtypeStruct((nseq, 3, dm), F32)],
        args=(dy, r, x, f, gu, sh, sc, gt, wgu, wd, ln_g))


def tn_matmul(a, b, name, job=None, b_cols=None):
    na, tokens, kk = a.shape
    nb, _, cc = b.shape
    col = 0
    if b_cols is not None:
        col, cc = b_cols
    tt = tokens
    while 4 * tt * (kk + cc) + 8 * kk * cc > TN_VMEM_BUDGET and tt % 2 == 0 and tt > 256:
        tt //= 2
    steps = tokens // tt

    def body(a_ref, b_ref, o_ref, *acc):
        if steps == 1:
            o_ref[0, 0] = _dot_tn(a_ref[0], b_ref[0]).astype(BF16)
            return
        acc_ref, = acc
        t = pl.program_id(2)

        @pl.when(t == 0)
        def _():
            acc_ref[...] = jnp.zeros_like(acc_ref)

        acc_ref[...] += _dot_tn(a_ref[0], b_ref[0])

        @pl.when(t == steps - 1)
        def _():
            o_ref[0, 0] = acc_ref[...].astype(BF16)

    return _call(
        body, job, name=name, grid=(na, nb, steps),
        in_specs=[pl.BlockSpec((1, tt, kk), lambda i, j, t: (i, t, 0)),
                  pl.BlockSpec((1, tt, cc), lambda i, j, t: (j, t, col))],
        out_specs=[pl.BlockSpec((1, 1, kk, cc), lambda i, j, t: (i, j, 0, 0))],
        out_shape=[jax.ShapeDtypeStruct((na, nb, kk, cc), BF16)],
        scratch_shapes=[] if steps == 1 else [pltpu.VMEM((kk, cc), F32)], args=(a, b))


def proj_fwd(x1, sh, sc, w_in, seq, name, job=None):
    tokens, dm = x1.shape
    tm = min(TOKEN_TILE, seq)
    tiles_per_seq = seq // tm
    widths = [N_Q_HEADS * HEAD_DIM, N_KV_HEADS * HEAD_DIM, N_KV_HEADS * HEAD_DIM, 512, 512, 512]
    assert sum(widths) == w_in.shape[0]

    def body(x_ref, sh_ref, sc_ref, w_ref, *outs):
        h = (x_ref[...] * (1.0 + sc_ref[0]) + sh_ref[0]).astype(BF16)
        proj = _dot_nt(h, w_ref[...])
        at = 0
        for o_ref, wdt in zip(outs, widths):
            o_ref[...] = proj[:, at:at + wdt]
            at += wdt

    tile = pl.BlockSpec((tm, dm), lambda i: (i, 0))
    mod = _mod_spec(tiles_per_seq, dm)
    return _call(
        body, job, name=name, grid=(tokens // tm,),
        in_specs=[tile, mod, mod, _const_spec(w_in.shape)],
        out_specs=[pl.BlockSpec((tm, wdt), lambda i: (i, 0)) for wdt in widths],
        out_shape=[jax.ShapeDtypeStruct((tokens, wdt), F32) for wdt in widths],
        args=(x1, sh, sc, w_in))


LANES = 2 * HEAD_DIM


def _head_lane(shape):
    return lax.broadcasted_iota(jnp.int32, shape, 1) % HEAD_DIM


def _lane_half(shape):
    return lax.broadcasted_iota(jnp.int32, shape, 1) // HEAD_DIM


def _swap_rot(v):
    lane = _head_lane(v.shape)
    half = ROT_DIM // 2
    return jnp.where(lane < half, pltpu.roll(v, LANES - half, 1),
                     jnp.where(lane < ROT_DIM, pltpu.roll(v, half, 1), 0.0))


def _rope(v, cos_t, sin_t):
    return v * cos_t + _swap_rot(v) * sin_t


def _unrope(dv, cos_t, sin_t):
    return dv * cos_t + _swap_rot(dv * sin_t)


def _both_halves(t, g):
    return jnp.where(_lane_half(t.shape) == g, t, pltpu.roll(t, HEAD_DIM, 1))


def _fold_halves(t, g):
    return jnp.where(_lane_half(t.shape) == g, t + pltpu.roll(t, HEAD_DIM, 1), 0.0)


def _stack_heads(blocks):
    rows = []
    for blk in blocks:
        half = _lane_half(blk.shape)
        rows += [jnp.where(half == 0, blk, 0.0), jnp.where(half == 1, blk, 0.0)]
    return jnp.concatenate(rows, axis=0)


def _unstack_heads(t, j):
    lo = t[(2 * j) * ATTN_BLOCK:(2 * j + 1) * ATTN_BLOCK]
    hi = t[(2 * j + 1) * ATTN_BLOCK:(2 * j + 2) * ATTN_BLOCK]
    return jnp.where(_lane_half(lo.shape) == 0, lo, hi)


def _band_mask(q0, w0):
    rows, cols = GQA_GROUP * ATTN_BLOCK, 2 * ATTN_BLOCK
    qi = lax.broadcasted_iota(jnp.int32, (rows, cols), 0) % ATTN_BLOCK + q0
    ki = lax.broadcasted_iota(jnp.int32, (rows, cols), 1) + w0
    diff = qi - ki
    return (diff >= 0) & (diff < ATTN_BLOCK)


def _attn_specs(seq):
    q_spec = pl.BlockSpec((seq, GQA_GROUP * HEAD_DIM), lambda b, g: (b, g))
    kv_spec = pl.BlockSpec((seq, LANES), lambda b, g: (b, 0))
    sink_spec = pl.BlockSpec((1, GQA_GROUP * ATTN_BLOCK, 1), lambda b, g: (g, 0, 0))
    return q_spec, kv_spec, sink_spec


def _block_starts(n):
    q0 = pl.multiple_of(n * ATTN_BLOCK, ATTN_BLOCK)
    w0 = pl.multiple_of(jnp.maximum(n - 1, 0) * ATTN_BLOCK, ATTN_BLOCK)
    return q0, w0


def _stacked_queries(ref, rows, cos_b=None, sin_b=None):
    blocks = []
    for j in range(2):
        blk = ref[rows, j * LANES:(j + 1) * LANES]
        blocks.append(blk if cos_b is None else _rope(blk, cos_b, sin_b))
    return _stack_heads(blocks).astype(BF16)


def _sink_columns(sinks):
    return jnp.repeat(sinks.reshape(N_KV_HEADS, GQA_GROUP), ATTN_BLOCK, axis=1)[:, :, None]


def _probs_spec(nblk):
    return pl.BlockSpec((1, 1, nblk, GQA_GROUP * ATTN_BLOCK, 2 * ATTN_BLOCK), lambda b, g: (b, g, 0, 0, 0))


def _sink_probs_spec():
    return pl.BlockSpec((1, 1, GQA_GROUP * ATTN_BLOCK, LANES), lambda b, g: (b, g, 0, 0))


def attn_fwd(q, k, v, cos_t, sin_t, sinks, seq, name, job=None):
    tokens = q.shape[0]
    nblk = seq // ATTN_BLOCK
    assert nblk >= 2
    scale = HEAD_DIM ** -0.5

    nseq = tokens // seq
    rows_stacked = GQA_GROUP * ATTN_BLOCK
    assert nblk <= LANES

    def body(q_ref, k_ref, v_ref, cos_ref, sin_ref, sink_ref, o_ref, qr_ref, p_ref, ps_ref, kd_ref, vd_ref):
        g = pl.program_id(1)
        kd_ref[...] = _both_halves(_rope(k_ref[...], cos_ref[...], sin_ref[...]), g).astype(BF16)
        vd_ref[...] = _both_halves(v_ref[...], g).astype(BF16)
        sink = sink_ref[0]
        lane = lax.broadcasted_iota(jnp.int32, (rows_stacked, LANES), 1)

        ps_ref[...] = jnp.zeros_like(ps_ref)

        def block(n, carry):
            q0, w0 = _block_starts(n)
            rows, win = pl.ds(q0, ATTN_BLOCK), pl.ds(w0, 2 * ATTN_BLOCK)
            blocks = []
            for j in range(2):
                qr = _rope(q_ref[rows, j * LANES:(j + 1) * LANES], cos_ref[rows, :], sin_ref[rows, :]).astype(BF16)
                qr_ref[rows, j * LANES:(j + 1) * LANES] = qr
                blocks.append(qr)
            qs = _stack_heads(blocks)
            s = _dot_nt(qs, kd_ref[win, :]) * scale
            s = jnp.where(_band_mask(q0, w0), s, NEG_BIG)
            m = jnp.maximum(jnp.max(s, axis=-1, keepdims=True), sink)
            p = jnp.exp(s - m)
            e_sink = jnp.exp(sink - m)
            inv = pl.reciprocal(jnp.sum(p, axis=-1, keepdims=True) + e_sink, approx=True)
            pn = (p * inv).astype(BF16)
            p_ref[0, 0, n] = pn
            out = _dot(pn, vd_ref[win, :])
            for j in range(2):
                o_ref[rows, j * LANES:(j + 1) * LANES] = _unstack_heads(out, j).astype(o_ref.dtype)
            ps_ref[0, 0] = jnp.where(lane == n, e_sink * inv, ps_ref[0, 0])
            return carry

        lax.fori_loop(0, nblk, block, 0, unroll=2)

    q_spec, kv_spec, sink_spec = _attn_specs(seq)
    return _call(
        body, job, name=name, grid=(nseq, N_KV_HEADS),
        in_specs=[q_spec, kv_spec, kv_spec, kv_spec, kv_spec, sink_spec],
        out_specs=[q_spec, q_spec, _probs_spec(nblk), _sink_probs_spec()],
        out_shape=[jax.ShapeDtypeStruct(q.shape, BF16), jax.ShapeDtypeStruct(q.shape, BF16),
                   jax.ShapeDtypeStruct((nseq, N_KV_HEADS, nblk, rows_stacked, 2 * ATTN_BLOCK), BF16),
                   jax.ShapeDtypeStruct((nseq, N_KV_HEADS, rows_stacked, LANES), F32)],
        scratch_shapes=[pltpu.VMEM((seq, LANES), BF16), pltpu.VMEM((seq, LANES), BF16)],
        args=(q, k, v, cos_t, sin_t, _sink_columns(sinks)))


def attn_bwd(qr, k, v, do, probs, sink_probs, cos_t, sin_t, seq, name, job=None):
    tokens = qr.shape[0]
    nseq = tokens // seq
    nblk = seq // ATTN_BLOCK
    assert nblk >= 2
    rows_stacked = GQA_GROUP * ATTN_BLOCK
    scale = HEAD_DIM ** -0.5

    def body(q_ref, k_ref, v_ref, do_ref, p_ref, ps_ref, cos_ref, sin_ref, dq_ref, dk_ref, dv_ref, ds_ref,
             kd_ref, vd_ref, dkd_ref, dvd_ref, acc_ref):
        g = pl.program_id(1)
        kd_ref[...] = _both_halves(_rope(k_ref[...], cos_ref[...], sin_ref[...]), g).astype(BF16)
        vd_ref[...] = _both_halves(v_ref[...], g).astype(BF16)
        dkd_ref[...] = jnp.zeros_like(dkd_ref)
        dvd_ref[...] = jnp.zeros_like(dvd_ref)
        acc_ref[...] = jnp.zeros_like(acc_ref)
        lane = lax.broadcasted_iota(jnp.int32, (rows_stacked, LANES), 1)

        def block(n, carry):
            q0, w0 = _block_starts(n)
            rows, win = pl.ds(q0, ATTN_BLOCK), pl.ds(w0, 2 * ATTN_BLOCK)
            qs = _stacked_queries(q_ref, rows)
            dos = _stacked_queries(do_ref, rows)
            kw, vw = kd_ref[win, :], vd_ref[win, :]
            pn16 = p_ref[0, 0, n]
            pn = pn16.astype(F32)
            dvd_ref[win, :] += _dot_tn(pn16, dos)
            dp = _dot_nt(dos, vw)
            delta = jnp.sum(dp * pn, axis=-1, keepdims=True)
            ds = (pn * (dp - delta)).astype(BF16)
            dqs = _dot(ds, kw) * scale
            dkd_ref[win, :] += _dot_tn(ds, qs) * scale
            cos_b, sin_b = cos_ref[rows, :], sin_ref[rows, :]
            for j in range(2):
                dq_ref[rows, j * LANES:(j + 1) * LANES] = _unrope(_unstack_heads(dqs, j), cos_b, sin_b).astype(BF16)
            acc_ref[...] += jnp.where(lane == n, ps_ref[0, 0] * delta, 0.0)
            return carry

        lax.fori_loop(0, nblk // 2, lambda i, carry: block(2 * i + 1, block(2 * i, carry)), 0)
        ds_ref[0, 0] = -jnp.sum(acc_ref[...], axis=-1, keepdims=True)
        dk_g = _unrope(_fold_halves(dkd_ref[...], g), cos_ref[...], sin_ref[...])
        dv_g = _fold_halves(dvd_ref[...], g)

        @pl.when(g == 0)
        def _():
            dk_ref[...] = dk_g
            dv_ref[...] = dv_g

        @pl.when(g != 0)
        def _():
            dk_ref[...] += dk_g
            dv_ref[...] += dv_g

    q_spec, kv_spec, _ = _attn_specs(seq)
    return _call(
        body, job, name=name, grid=(nseq, N_KV_HEADS),
        in_specs=[q_spec, kv_spec, kv_spec, q_spec, _probs_spec(nblk), _sink_probs_spec(), kv_spec, kv_spec],
        out_specs=[q_spec, kv_spec, kv_spec, pl.BlockSpec((1, 1, rows_stacked, 1), lambda b, g: (b, g, 0, 0))],
        out_shape=[jax.ShapeDtypeStruct(qr.shape, BF16), jax.ShapeDtypeStruct(k.shape, F32),
                   jax.ShapeDtypeStruct(k.shape, F32), jax.ShapeDtypeStruct((nseq, N_KV_HEADS, rows_stacked, 1), F32)],
        scratch_shapes=[pltpu.VMEM((seq, LANES), BF16), pltpu.VMEM((seq, LANES), BF16),
                        pltpu.VMEM((seq, LANES), F32), pltpu.VMEM((seq, LANES), F32),
                        pltpu.VMEM((rows_stacked, LANES), F32)],
        args=(qr, k, v, do, probs, sink_probs, cos_t, sin_t))


CONV_COLS = 128


def _shift_down(z, by):
    t = lax.broadcasted_iota(jnp.int32, z.shape, 0)
    return jnp.where(t >= by, pltpu.roll(z, by, 0), 0.0)


def _shift_up(z, by):
    n = z.shape[0]
    t = lax.broadcasted_iota(jnp.int32, z.shape, 0)
    return jnp.where(t < n - by, pltpu.roll(z, n - by, 0), 0.0)


def conv_fwd(u, bg, cg, conv_w, seq, name):
    tokens, width = u.shape

    def body(u_ref, bg_ref, cg_ref, w_ref, o_ref):
        z = cg_ref[...] * u_ref[...]
        yy = w_ref[2:3, :] * z + w_ref[1:2, :] * _shift_down(z, 1) + w_ref[0:1, :] * _shift_down(z, 2)
        o_ref[...] = (bg_ref[...] * yy).astype(BF16)

    col = pl.BlockSpec((seq, CONV_COLS), lambda j, b: (b, j))
    return pl.pallas_call(
        body, name=name, grid=(width // CONV_COLS, tokens // seq),
        in_specs=[col, col, col, pl.BlockSpec((CONV_TAPS, CONV_COLS), lambda j, b: (0, j))],
        out_specs=col, out_shape=jax.ShapeDtypeStruct((tokens, width), BF16),
        compiler_params=_params(("parallel", "parallel")),
    )(u, bg, cg, conv_w)


def conv_bwd(dout, u, bg, cg, conv_w, seq, name):
    tokens, width = u.shape

    def body(do_ref, u_ref, bg_ref, cg_ref, w_ref, du_ref, dbg_ref, dcg_ref, dw_ref):
        uu, cg_v, do = u_ref[...], cg_ref[...], do_ref[...].astype(F32)
        z = cg_v * uu
        z1, z2 = _shift_down(z, 1), _shift_down(z, 2)
        yy = w_ref[2:3, :] * z + w_ref[1:2, :] * z1 + w_ref[0:1, :] * z2
        dbg_ref[...] = (do * yy).astype(BF16)
        dyy = do * bg_ref[...]
        dz = w_ref[2:3, :] * dyy + w_ref[1:2, :] * _shift_up(dyy, 1) + w_ref[0:1, :] * _shift_up(dyy, 2)
        du_ref[...] = (dz * cg_v).astype(BF16)
        dcg_ref[...] = (dz * uu).astype(BF16)

        @pl.when(pl.program_id(1) == 0)
        def _():
            dw_ref[...] = jnp.zeros_like(dw_ref)

        dw_ref[0:1, :] += jnp.sum(dyy * z2, axis=0, keepdims=True)
        dw_ref[1:2, :] += jnp.sum(dyy * z1, axis=0, keepdims=True)
        dw_ref[2:3, :] += jnp.sum(dyy * z, axis=0, keepdims=True)

    col = pl.BlockSpec((seq, CONV_COLS), lambda j, b: (b, j))
    w_spec = pl.BlockSpec((CONV_TAPS, CONV_COLS), lambda j, b: (0, j))
    act = jax.ShapeDtypeStruct((tokens, width), BF16)
    return pl.pallas_call(
        body, name=name, grid=(width // CONV_COLS, tokens // seq),
        in_specs=[col, col, col, col, w_spec], out_specs=[col, col, col, w_spec],
        out_shape=[act, act, act, jax.ShapeDtypeStruct((CONV_TAPS, width), F32)],
        compiler_params=_params(("parallel", "arbitrary")),
    )(dout, u, bg, cg, conv_w)


def out_fwd(x1, attn, conv, gt, w_out, ln_g, ln_b, seq, name, job=None):
    tokens, dm = x1.shape
    half = attn.shape[1]
    tm = min(TOKEN_TILE, seq)
    tiles_per_seq = seq // tm

    def body(x_ref, a_ref, c_ref, gt_ref, w_ref, lg_ref, lb_ref, xo_ref, r_ref, mi_ref, mix_ref):
        mixin = jnp.concatenate([a_ref[...], c_ref[...]], axis=1).astype(BF16)
        mi_ref[...] = mixin
        mix = _dot(mixin, w_ref[...])
        mix_ref[...] = mix.astype(BF16)
        r = DN_ALPHA * x_ref[...] + (1.0 + gt_ref[0]) * mix
        r_ref[...] = r
        xhat, _ = _ln_stats(r)
        xo_ref[...] = xhat * lg_ref[...] + lb_ref[...]

    tile = pl.BlockSpec((tm, dm), lambda i: (i, 0))
    htile = pl.BlockSpec((tm, half), lambda i: (i, 0))
    return _call(
        body, job, name=name, grid=(tokens // tm,),
        in_specs=[tile, htile, htile, _mod_spec(tiles_per_seq, dm), _const_spec(w_out.shape),
                  _const_spec((1, dm)), _const_spec((1, dm))],
        out_specs=[tile, tile, tile, tile],
        out_shape=[jax.ShapeDtypeStruct((tokens, dm), F32), jax.ShapeDtypeStruct((tokens, dm), F32),
                   jax.ShapeDtypeStruct((tokens, dm), BF16), jax.ShapeDtypeStruct((tokens, dm), BF16)],
        args=(x1, attn, conv, gt, w_out, ln_g, ln_b))


def out_bwd(dy, r, mix, gt, w_out, ln_g, seq, name, job=None):
    tokens, dm = r.shape
    half = dm // 2
    tm = min(TOKEN_TILE, seq)
    tiles_per_seq = seq // tm
    nseq = tokens // seq

    def body(dy_ref, r_ref, mix_ref, gt_ref, w_ref, lg_ref, dres_ref, da_ref, dc_ref, dmix_ref, dln_ref, dgt_ref):
        i = pl.program_id(0)
        dr, dgain, dbias = _ln_bwd(dy_ref[...], r_ref[...], lg_ref[...])

        @pl.when(i == 0)
        def _():
            dln_ref[...] = jnp.zeros_like(dln_ref)

        @pl.when(i % tiles_per_seq == 0)
        def _():
            dgt_ref[...] = jnp.zeros_like(dgt_ref)

        dln_ref[0:1, :] += dgain
        dln_ref[1:2, :] += dbias
        dgt_ref[0] += jnp.sum(dr * mix_ref[...].astype(F32), axis=0, keepdims=True)
        dres_ref[...] = DN_ALPHA * dr
        dmix = ((1.0 + gt_ref[0]) * dr).astype(BF16)
        dmix_ref[...] = dmix
        dmixin = _dot_nt(dmix, w_ref[...])
        da_ref[...] = dmixin[:, :half].astype(BF16)
        dc_ref[...] = dmixin[:, half:].astype(BF16)

    tile = pl.BlockSpec((tm, dm), lambda i: (i, 0))
    htile = pl.BlockSpec((tm, half), lambda i: (i, 0))
    return _call(
        body, job, name=name, grid=(tokens // tm,),
        in_specs=[tile, tile, tile, _mod_spec(tiles_per_seq, dm), _const_spec(w_out.shape), _const_spec((1, dm))],
        out_specs=[tile, htile, htile, tile, pl.BlockSpec((2, dm), lambda i: (0, 0)),
                   pl.BlockSpec((1, 1, dm), lambda i: (i // tiles_per_seq, 0, 0))],
        out_shape=[jax.ShapeDtypeStruct((tokens, dm), F32), jax.ShapeDtypeStruct((tokens, half), BF16),
                   jax.ShapeDtypeStruct((tokens, half), BF16), jax.ShapeDtypeStruct((tokens, dm), BF16),
                   jax.ShapeDtypeStruct((2, dm), F32), jax.ShapeDtypeStruct((nseq, 1, dm), F32)],
        args=(dy, r, mix, gt, w_out, ln_g))


def proj_bwd(parts, dres, x1, sh, sc, w_in, seq, name):
    tokens, dm = x1.shape
    tm = min(TOKEN_TILE, seq)
    tiles_per_seq = seq // tm
    nseq = tokens // seq
    widths = [p.shape[1] for p in parts]
    total = sum(widths)

    def body(*refs):
        part_refs = refs[:6]
        dres_ref, x_ref, sh_ref, sc_ref, w_ref, dx_ref, dproj_ref, h_ref, dmod_ref = refs[6:]
        dproj = jnp.concatenate([p[...].astype(BF16) for p in part_refs], axis=1)
        dproj_ref[...] = dproj
        dh = _dot(dproj, w_ref[...])
        xx = x_ref[...]
        one_sc = 1.0 + sc_ref[0]
        h_ref[...] = (xx * one_sc + sh_ref[0]).astype(BF16)
        dx_ref[...] = dres_ref[...] + dh * one_sc

        @pl.when(pl.program_id(0) % tiles_per_seq == 0)
        def _():
            dmod_ref[...] = jnp.zeros_like(dmod_ref)

        dmod_ref[0, 0:1, :] += jnp.sum(dh, axis=0, keepdims=True)
        dmod_ref[0, 1:2, :] += jnp.sum(dh * xx, axis=0, keepdims=True)

    tile = pl.BlockSpec((tm, dm), lambda i: (i, 0))
    mod = _mod_spec(tiles_per_seq, dm)
    return pl.pallas_call(
        body, name=name, grid=(tokens // tm,),
        in_specs=[pl.BlockSpec((tm, wdt), lambda i: (i, 0)) for wdt in widths]
        + [tile, tile, mod, mod, _const_spec(w_in.shape)],
        out_specs=[tile, pl.BlockSpec((tm, total), lambda i: (i, 0)), tile,
                   pl.BlockSpec((1, 2, dm), lambda i: (i // tiles_per_seq, 0, 0))],
        out_shape=[jax.ShapeDtypeStruct((tokens, dm), F32), jax.ShapeDtypeStruct((tokens, total), BF16),
                   jax.ShapeDtypeStruct((tokens, dm), BF16), jax.ShapeDtypeStruct((nseq, 2, dm), F32)],
        compiler_params=_params(("arbitrary",)),
    )(*parts, dres, x1, sh, sc, w_in)


def _rope_tables(positions):
    half = ROT_DIM // 2
    inv_freq = jnp.power(jnp.float32(ROPE_THETA), -jnp.arange(0, ROT_DIM, 2, dtype=F32) / ROT_DIM)
    lane = jnp.arange(LANES) % HEAD_DIM
    freq = jnp.where(lane < ROT_DIM, inv_freq[lane % half], 0.0)
    sign = jnp.where(lane < half, -1.0, 1.0).astype(F32)
    ang = positions.astype(F32)[:, None] * freq[None, :]
    return jnp.cos(ang), sign[None, :] * jnp.sin(ang)


def kernel(x, c, positions, w_ada, b_ada, ffn1_w_gate_up, ffn1_w_down, ln1_g, ln1_b, w_in, conv_w, attn_sinks, w_out, ln2_g, ln2_b, ffn2_w_gate_up, ffn2_w_down, ln3_g, ln3_b, loss_target, m_w_ada, m_b_ada, m_ffn1_w_gate_up, m_ffn1_w_down, m_ln1_g, m_ln1_b, m_w_in, m_conv_w, m_attn_sinks, m_w_out, m_ln2_g, m_ln2_b, m_ffn2_w_gate_up, m_ffn2_w_down, m_ln3_g, m_ln3_b, v_w_ada, v_b_ada, v_ffn1_w_gate_up, v_ffn1_w_down, v_ln1_g, v_ln1_b, v_w_in, v_conv_w, v_attn_sinks, v_w_out, v_ln2_g, v_ln2_b, v_ffn2_w_gate_up, v_ffn2_w_down, v_ln3_g, v_ln3_b):
    nseq, seq, dm = x.shape
    tokens = nseq * seq
    dev = 4 * lax.axis_index("x") + 2 * lax.axis_index("y") + lax.axis_index("c")
    ada_cols = w_ada.shape[2]
    ff = ffn1_w_down.shape[1] * N_DEV
    fc = ff // 4
    in_cols = w_in.shape[2]
    conv_cols = conv_w.shape[2]

    def t_bf16(w):
        return w[0].T.astype(BF16)

    c_all, convw_all = all_gather([c, conv_w[0]], "gather_cond")
    wgu1, wd1 = all_gather([t_bf16(ffn1_w_gate_up), ffn1_w_down[0].astype(BF16)], "gather_ffn1")
    c_all = c_all.reshape(N_DEV * nseq, dm)
    convw_full = convw_all.transpose(1, 0, 2).reshape(CONV_TAPS, N_DEV * conv_cols)
    wd1 = wd1.reshape(4, fc, dm)

    b_cols = lax.dynamic_slice(b_ada, (0, dev * ada_cols), (1, ada_cols))
    cond_all, mod_cols = ada_fwd(c_all, w_ada[0], b_cols, "ada_fwd")
    (mod_all,) = all_gather([mod_cols], "gather_mod")
    mod = lax.dynamic_slice(mod_all, (0, dev * nseq, 0), (N_DEV, nseq, ada_cols))
    mod = mod.transpose(1, 0, 2).reshape(nseq, 9, 1, dm)
    sh1, sc1, g1, sh2, sc2, g2, sh3, sc3, g3 = [mod[:, i] for i in range(9)]

    x0 = x.reshape(tokens, dm)
    spread = gather_spread_job([t_bf16(w_in), w_out[0].astype(BF16), ffn2_w_down[0].astype(BF16)])
    (x1, r1, gu1, f1), spread = ffn_fwd(x0, sh1, sc1, g1, wgu1, wd1, ln1_g, ln1_b, seq, "ffn1_fwd", job=spread)
    win, wout = run_job(gather_forward_job(spread[:2]), "gather_mix_forward")
    win = win.reshape(N_DEV * in_cols, dm)
    wout = wout.reshape(dm, dm)
    (q, k, v, u, bg, cg), (wd2,) = proj_fwd(x1, sh2, sc2, win, seq, "proj_fwd", job=gather_forward_job(spread[2:]))
    wd2 = wd2.reshape(4, fc, dm)
    cos_t, sin_t = _rope_tables(positions.reshape(tokens))
    sinks = attn_sinks[0]
    (attn, q_rot, probs, sink_probs), spread = attn_fwd(q, k, v, cos_t, sin_t, sinks, seq, "attn_fwd",
                                                        job=gather_spread_job([t_bf16(ffn2_w_gate_up)]))
    conv = conv_fwd(u, bg, cg, convw_full, seq, "conv_fwd")
    (x2, r2, mixin, mix), (wgu2,) = out_fwd(x1, attn, conv, g2, wout, ln2_g, ln2_b, seq, "out_fwd",
                                            job=gather_forward_job(spread))
    target = loss_target.reshape(tokens, dm)
    (dy3, loss_part, r3, gu3, f3), _ = ffn_fwd(x2, sh3, sc3, g3, wgu2, wd2, ln3_g, ln3_b, seq, "ffn2_fwd", target=target)

    (dx2, dgu3, df3, a3, h3, dln3, dmod3), _ = ffn_bwd(dy3, r3, x2, f3, gu3, sh3, sc3, g3, wgu2, wd2, ln3_g, seq, "ffn2_bwd")
    g_wd2 = tn_matmul(a3, df3[None], "ffn2_dwd")[0][0].reshape(N_DEV, ff // N_DEV, dm)
    g_wgu2 = tn_matmul(dgu3, h3[None], "ffn2_dwgu")[0][0].reshape(N_DEV, fc, dm)
    (dres2, dattn, dconv, dmix, dln2, dg2), swapped = out_bwd(dx2, r2, mix, g2, wout, ln2_g, seq, "out_bwd",
                                                              job=swap_job([g_wgu2, g_wd2]))
    p_wgu2, own_wgu2 = pair_sum(g_wgu2, swapped[0], "pair_wgu2")
    p_wd2, own_wd2 = pair_sum(g_wd2, swapped[1], "pair_wd2")
    du, dbg, dcg, dconvw = conv_bwd(dconv, u, bg, cg, convw_full, seq, "conv_bwd")
    (dq, dk, dv, dsink_rows), (far_wgu2, far_wd2) = attn_bwd(
        q_rot, k, v, dattn, probs, sink_probs, cos_t, sin_t, seq, "attn_bwd", job=chip_exchange_job([p_wgu2, p_wd2]))
    parts = [dq, dk, dv, du, dbg, dcg]
    dx1, dproj, h2, dmod2 = proj_bwd(parts, dres2, x1, sh2, sc2, win, seq, "proj_bwd")
    (dx0, dgu1, df1, a1, h1, dln1, dmod1), _ = ffn_bwd(
        dx1, r1, x0, f1, gu1, sh1, sc1, g1, wgu1, wd1, ln1_g, seq, "ffn1_bwd")

    dmod = jnp.concatenate([dmod1, dmod2, dg2, dmod3], axis=1).reshape(nseq, 9 * dm)
    half = dm // 2
    jobs = _Jobs([gather_spread_job([dmod])])
    (g_wd1,), res = tn_matmul(a1, df1[None], "ffn1_dwd", job=jobs)
    (dmod_spread,) = jobs.split(res)
    g_wd1 = g_wd1.reshape(N_DEV, ff // N_DEV, dm)
    jobs = _Jobs([swap_job([g_wd1]), gather_forward_job(dmod_spread)])
    (g_l,), res = tn_matmul(dgu1, h1[None], "ffn1_dwgu_l", job=jobs, b_cols=(0, half))
    (sw_wd1,), (dmod_all,) = jobs.split(res)
    g_l = g_l.reshape(N_DEV, fc, half)
    p_wd1, own_wd1 = pair_sum(g_wd1, sw_wd1, "pair_wd1")
    jobs = _Jobs([chip_exchange_job([p_wd1]), swap_job([g_l])])
    (g_r,), res = tn_matmul(dgu1, h1[None], "ffn1_dwgu_r", job=jobs, b_cols=(1, half))
    (far_wd1,), (sw_l,) = jobs.split(res)
    g_r = g_r.reshape(N_DEV, fc, half)
    p_l, own_l = pair_sum(g_l, sw_l, "pair_wgu1_l")
    jobs = _Jobs([chip_exchange_job([p_l]), swap_job([g_r])])
    (g_win,), res = tn_matmul(dproj[None], h2[None], "dwin", job=jobs)
    (far_l,), (sw_r,) = jobs.split(res)
    g_win = g_win.reshape(N_DEV, in_cols, dm)
    p_r, own_r = pair_sum(g_r, sw_r, "pair_wgu1_r")
    jobs = _Jobs([chip_exchange_job([p_r]), swap_job([g_win])])
    (g_wout,), res = tn_matmul(mixin[None], dmix[None], "dwout", job=jobs)
    (far_r,), (sw_win,) = jobs.split(res)
    g_wout = g_wout.reshape(N_DEV, dm // N_DEV, dm)
    p_win, own_win = pair_sum(g_win, sw_win, "pair_win")
    jobs = _Jobs([chip_exchange_job([p_win]), swap_job([g_wout])])
    (far_win,), (sw_wout,) = jobs.split(run_job(jobs, "rs_tail_win"))
    p_wout, own_wout = pair_sum(g_wout, sw_wout, "pair_wout")
    (far_wout,) = run_job(chip_exchange_job([p_wout]), "rs_tail_wout")

    grads = {
        "ffn1_w_gate_up": jnp.concatenate([own_l, own_r], axis=1), "ffn1_w_down": own_wd1,
        "w_in": own_win, "w_out": own_wout, "ffn2_w_gate_up": own_wgu2, "ffn2_w_down": own_wd2,
    }
    others = {"ffn1_w_gate_up": jnp.concatenate([far_l, far_r], axis=2), "ffn1_w_down": far_wd1,
              "w_in": far_win, "w_out": far_wout, "ffn2_w_gate_up": far_wgu2, "ffn2_w_down": far_wd2}

    dmod_cols = lax.dynamic_slice(dmod_all.reshape(N_DEV * nseq, 9 * dm), (0, dev * ada_cols), (N_DEV * nseq, ada_cols))
    grads["w_ada"], gb_cols = ada_bwd(cond_all, dmod_cols, "ada_bwd")

    dsinks = jnp.sum(dsink_rows.reshape(nseq, N_Q_HEADS, ATTN_BLOCK), axis=(0, 2))
    small = jnp.zeros((8, dm), F32)
    small = small.at[0:2].set(dln1).at[2:4].set(dln2).at[4:6].set(dln3)
    small = small.at[6, 0:N_Q_HEADS].set(dsinks).at[7, 0].set(loss_part[0, 0])
    small_all, dconvw_all, gb_all = all_gather([small, dconvw, gb_cols], "gather_small")
    small_sum = sum_devices(small_all, "sum_small")
    dconvw_sum = sum_devices(dconvw_all, "sum_convw")
    loss = small_sum[7, 0]
    grads["b_ada"] = gb_all.reshape(1, N_DEV * ada_cols)
    grads["conv_w"] = lax.dynamic_slice(dconvw_sum, (0, dev * conv_cols), (CONV_TAPS, conv_cols))
    grads["attn_sinks"] = small_sum[6:7, 0:N_Q_HEADS]
    for i, nm in enumerate(["ln1_g", "ln1_b", "ln2_g", "ln2_b", "ln3_g", "ln3_b"]):
        grads[nm] = small_sum[i:i + 1]

    given = dict(w_ada=(w_ada, m_w_ada, v_w_ada), b_ada=(b_ada, m_b_ada, v_b_ada),
                 ffn1_w_gate_up=(ffn1_w_gate_up, m_ffn1_w_gate_up, v_ffn1_w_gate_up),
                 ffn1_w_down=(ffn1_w_down, m_ffn1_w_down, v_ffn1_w_down),
                 ln1_g=(ln1_g, m_ln1_g, v_ln1_g), ln1_b=(ln1_b, m_ln1_b, v_ln1_b),
                 w_in=(w_in, m_w_in, v_w_in), conv_w=(conv_w, m_conv_w, v_conv_w),
                 attn_sinks=(attn_sinks, m_attn_sinks, v_attn_sinks), w_out=(w_out, m_w_out, v_w_out),
                 ln2_g=(ln2_g, m_ln2_g, v_ln2_g), ln2_b=(ln2_b, m_ln2_b, v_ln2_b),
                 ffn2_w_gate_up=(ffn2_w_gate_up, m_ffn2_w_gate_up, v_ffn2_w_gate_up),
                 ffn2_w_down=(ffn2_w_down, m_ffn2_w_down, v_ffn2_w_down),
                 ln3_g=(ln3_g, m_ln3_g, v_ln3_g), ln3_b=(ln3_b, m_ln3_b, v_ln3_b))
    order = ["w_ada", "b_ada", "ffn1_w_gate_up", "ffn1_w_down", "ln1_g", "ln1_b", "w_in", "conv_w", "attn_sinks",
             "w_out", "ln2_g", "ln2_b", "ffn2_w_gate_up", "ffn2_w_down", "ln3_g", "ln3_b"]
    transposed = ("ffn1_w_gate_up", "ffn2_w_gate_up", "w_in")
    out_g, out_d, out_m, out_v = [], [], [], []
    for nm in order:
        shape = given[nm][0].shape
        two_d = (shape[-2], shape[-1])
        if nm in transposed:
            w2, m2, v2 = [t[0].T for t in given[nm]]
            back = lambda t: t.T[None]
        else:
            w2, m2, v2 = [t.reshape(two_d) for t in given[nm]]
            back = lambda t, shape=shape: t.reshape(shape)
        res = adamw(w2, grads[nm].reshape(w2.shape), m2, v2, "adamw_" + nm, others=others.get(nm))
        for lst, t in zip((out_g, out_d, out_m, out_v), res):
            lst.append(back(t))
    grad_x = dx0.reshape(nseq, seq, dm)
    return (loss, grad_x, *out_g, *out_d, *out_m, *out_v)
```

```python
import functools

import jax
import jax.numpy as jnp
from jax import lax
from jax.experimental import pallas as pl
from jax.experimental.pallas import tpu as pltpu

F32 = jnp.float32
BF16 = jnp.bfloat16
MESH = pl.DeviceIdType.MESH

N_DEV = 8
N_CHIP = 4
HEAD_DIM = 64
N_Q_HEADS = 8
N_KV_HEADS = 2
GQA_GROUP = N_Q_HEADS // N_KV_HEADS
ATTN_BLOCK = 128
ROT_DIM = 16
ROPE_THETA = 500000.0
CONV_TAPS = 3
LN_EPS = 1e-5
DN_ALPHA = 2.0 ** 0.25
ADAM_LR = 0.001
ADAM_B1 = 0.9
ADAM_B2 = 0.999
ADAM_EPS = 1e-08
ADAM_WD = 0.01
ADAM_STEP = 10
NEG_BIG = -1e30

VMEM_LIMIT = 56 * 1024 * 1024
TOKEN_TILE = 256
FFN_FWD_TILE = 512
TN_VMEM_BUDGET = 36 * 1024 * 1024


def _params(semantics=None, vmem=VMEM_LIMIT):
    return pltpu.CompilerParams(dimension_semantics=semantics, vmem_limit_bytes=vmem)


def _dot(a, b):
    return jnp.dot(a, b, preferred_element_type=F32)


def _dot_nt(a, b):
    return lax.dot_general(a, b, (((1,), (1,)), ((), ())), preferred_element_type=F32)


def _dot_tn(a, b):
    return lax.dot_general(a, b, (((0,), (0,)), ((), ())), preferred_element_type=F32)


def _sigmoid(x):
    return pl.reciprocal(1.0 + jnp.exp(-x), approx=True)


def _ln_stats(r):
    mu = jnp.mean(r, axis=-1, keepdims=True)
    d = r - mu
    var = jnp.mean(d * d, axis=-1, keepdims=True)
    rstd = lax.rsqrt(var + LN_EPS)
    return d * rstd, rstd


def _ln_bwd(dy, r, g):
    xhat, rstd = _ln_stats(r)
    dxhat = dy * g
    c1 = jnp.mean(dxhat, axis=-1, keepdims=True)
    c2 = jnp.mean(dxhat * xhat, axis=-1, keepdims=True)
    dr = rstd * (dxhat - c1 - xhat * c2)
    return dr, jnp.sum(dy * xhat, axis=0, keepdims=True), jnp.sum(dy, axis=0, keepdims=True)


def _const_spec(shape):
    nd = len(shape)
    return pl.BlockSpec(shape, lambda *_: (0,) * nd, pipeline_mode=pl.Buffered(1))


def all_gather(arrs, name):
    n = len(arrs)

    def body(*refs):
        ins, outs = refs[:n], refs[n:2 * n]
        send_sems, recv_sems, local_sems = refs[2 * n:]
        x, y, c = lax.axis_index("x"), lax.axis_index("y"), lax.axis_index("c")
        me, sibling = (x, y, c), (x, y, 1 - c)
        chips = [(1 - x, y), (x, 1 - y), (1 - x, 1 - y)]

        def slot(i, p):
            return outs[i].at[4 * p[0] + 2 * p[1] + p[2]]

        def copy(i, k, block, to, src=None):
            return pltpu.make_async_remote_copy(
                src_ref=slot(i, block) if src is None else src, dst_ref=slot(i, block),
                send_sem=send_sems.at[i, k], recv_sem=recv_sems.at[i, k],
                device_id=to, device_id_type=MESH)

        mine = [pltpu.make_async_copy(ins[i], slot(i, me), local_sems.at[i]) for i in range(n)]
        for cp in mine:
            cp.start()
        first = []
        for i in range(n):
            first.append(copy(i, 0, me, sibling, src=ins[i]))
            first += [copy(i, 1 + j, me, (*chip, c), src=ins[i]) for j, chip in enumerate(chips)]
        for cp in first:
            cp.start()
        passed = []
        for j, chip in enumerate(chips):
            for i in range(n):
                copy(i, 1 + j, (*chip, c), me).wait_recv()
                cp = copy(i, 4 + j, (*chip, c), sibling)
                cp.start()
                passed.append(cp)
        for i in range(n):
            copy(i, 0, sibling, me).wait_recv()
            for j, chip in enumerate(chips):
                copy(i, 4 + j, (*chip, 1 - c), me).wait_recv()
        for cp in first + passed:
            cp.wait_send()
        for cp in mine:
            cp.wait()

    any_spec = pl.BlockSpec(memory_space=pl.ANY)
    return pl.pallas_call(
        body, name=name,
        out_shape=[jax.ShapeDtypeStruct((N_DEV, *a.shape), a.dtype) for a in arrs],
        in_specs=[any_spec] * n, out_specs=[any_spec] * n,
        scratch_shapes=[pltpu.SemaphoreType.DMA((n, 7)), pltpu.SemaphoreType.DMA((n, 7)),
                        pltpu.SemaphoreType.DMA((n,))],
    )(*arrs)


RS_ROWS = 32


def reduce_scatter(g, name):
    _, rows, cols = g.shape
    nblk = rows // RS_ROWS
    assert nblk * RS_ROWS == rows

    def body(g_ref, out_ref, r1_ref, p_ref, r2_ref, send_sems, recv_sems):
        x, y, c = lax.axis_index("x"), lax.axis_index("y"), lax.axis_index("c")
        sibling = (x, y, 1 - c)
        q_me = 2 * x + y
        swaps = []
        for q in range(N_CHIP):
            cp = pltpu.make_async_remote_copy(
                src_ref=g_ref.at[2 * q + (1 - c)], dst_ref=r1_ref.at[q],
                send_sem=send_sems.at[q], recv_sem=recv_sems.at[q], device_id=sibling, device_id_type=MESH)
            cp.start()
            swaps.append(cp)
        for cp in swaps:
            cp.wait_recv()

        def pair_sum(i, carry):
            r = pl.ds(pl.multiple_of(i * RS_ROWS, RS_ROWS), RS_ROWS)
            for q in range(N_CHIP):
                p_ref[q, r, :] = (g_ref[2 * q + c, r, :].astype(F32) + r1_ref[q, r, :].astype(F32)).astype(BF16)
            return carry

        lax.fori_loop(0, nblk, pair_sum, 0)
        chips = [(1 - x, y), (x, 1 - y), (1 - x, 1 - y)]
        sends = []
        for k, chip in enumerate(chips):
            cp = pltpu.make_async_remote_copy(
                src_ref=p_ref.at[2 * chip[0] + chip[1]], dst_ref=r2_ref.at[k],
                send_sem=send_sems.at[N_CHIP + k], recv_sem=recv_sems.at[N_CHIP + k],
                device_id=(*chip, c), device_id_type=MESH)
            cp.start()
            sends.append(cp)
        for cp in sends:
            cp.wait_recv()

        def total(i, carry):
            r = pl.ds(pl.multiple_of(i * RS_ROWS, RS_ROWS), RS_ROWS)
            acc = g_ref[2 * q_me + c, r, :].astype(F32) + r1_ref[q_me, r, :].astype(F32)
            for k in range(3):
                acc = acc + r2_ref[k, r, :].astype(F32)
            out_ref[r, :] = acc
            return carry

        lax.fori_loop(0, nblk, total, 0)
        for cp in swaps + sends:
            cp.wait_send()

    vmem = pl.BlockSpec(memory_space=pltpu.VMEM)
    return pl.pallas_call(
        body, name=name,
        out_shape=jax.ShapeDtypeStruct((rows, cols), F32),
        in_specs=[vmem], out_specs=vmem,
        scratch_shapes=[pltpu.VMEM((N_CHIP, rows, cols), BF16), pltpu.VMEM((N_CHIP, rows, cols), BF16),
                        pltpu.VMEM((3, rows, cols), BF16),
                        pltpu.SemaphoreType.DMA((N_CHIP + 3,)), pltpu.SemaphoreType.DMA((N_CHIP + 3,))],
        compiler_params=_params(),
    )(g)


def _place():
    x, y, c = lax.axis_index("x"), lax.axis_index("y"), lax.axis_index("c")
    return x, y, c, [(1 - x, y), (x, 1 - y), (1 - x, 1 - y)]


def _slot(p):
    return 4 * p[0] + 2 * p[1] + p[2]


class _Job:
    def __init__(self, ins, outs, nsem, copies, aliases=None, local=None):
        self.ins, self.outs, self.nsem, self.copies = list(ins), list(outs), nsem, copies
        self.aliases = aliases or {}
        self.local = local

    def scratch(self):
        s = [pltpu.SemaphoreType.DMA(self.nsem), pltpu.SemaphoreType.DMA(self.nsem)]
        if self.local is not None:
            s.append(pltpu.SemaphoreType.DMA((len(self.ins),)))
        return s

    def start(self, ins, outs, sems):
        if self.local is not None:
            for cp in self.local(ins, outs, sems[2]):
                cp.start()
        for cp in self.copies(ins, outs, sems[0], sems[1])[0]:
            cp.start()

    def finish(self, ins, outs, sems):
        started, awaited = self.copies(ins, outs, sems[0], sems[1])
        for cp in awaited:
            cp.wait_recv()
        for cp in started:
            cp.wait_send()
        if self.local is not None:
            for cp in self.local(ins, outs, sems[2]):
                cp.wait()


class _Jobs:
    def __init__(self, jobs):
        self.jobs = jobs
        self.ins = [a for j in jobs for a in j.ins]
        self.outs = [o for j in jobs for o in j.outs]
        self.aliases = {}
        at_in = at_out = 0
        for j in jobs:
            self.aliases.update({at_in + i: at_out + o for i, o in j.aliases.items()})
            at_in, at_out = at_in + len(j.ins), at_out + len(j.outs)

    def scratch(self):
        return [s for j in self.jobs for s in j.scratch()]

    def _each(self, ins, outs, sems):
        at_in = at_out = at_sem = 0
        for j in self.jobs:
            n_in, n_out, n_sem = len(j.ins), len(j.outs), len(j.scratch())
            yield j, ins[at_in:at_in + n_in], outs[at_out:at_out + n_out], sems[at_sem:at_sem + n_sem]
            at_in, at_out, at_sem = at_in + n_in, at_out + n_out, at_sem + n_sem

    def start(self, ins, outs, sems):
        for j, i, o, s in self._each(ins, outs, sems):
            j.start(i, o, s)

    def finish(self, ins, outs, sems):
        for j, i, o, s in self._each(ins, outs, sems):
            j.finish(i, o, s)

    def split(self, results):
        at, parts = 0, []
        for j in self.jobs:
            parts.append(results[at:at + len(j.outs)])
            at += len(j.outs)
        return parts


def _remote(src, dst, send, recv, idx, to):
    return pltpu.make_async_remote_copy(src_ref=src, dst_ref=dst, send_sem=send.at[idx], recv_sem=recv.at[idx],
                                        device_id=to, device_id_type=MESH)


def gather_spread_job(shards):
    def copies(ins, outs, send, recv):
        x, y, c, chips = _place()
        me = (x, y, c)
        peers = [(x, y, 1 - c)] + [(*chip, c) for chip in chips]
        started, awaited = [], []
        for i, (src, dst) in enumerate(zip(ins, outs)):
            for k, peer in enumerate(peers):
                started.append(_remote(src, dst.at[_slot(me)], send, recv, (i, k), peer))
                awaited.append(_remote(src, dst.at[_slot(peer)], send, recv, (i, k), peer))
        return started, awaited

    def local(ins, outs, sems):
        x, y, c, _ = _place()
        return [pltpu.make_async_copy(src, dst.at[_slot((x, y, c))], sems.at[i])
                for i, (src, dst) in enumerate(zip(ins, outs))]

    outs = [jax.ShapeDtypeStruct((N_DEV, *a.shape), a.dtype) for a in shards]
    return _Job(shards, outs, (len(shards), 4), copies, local=local)


def gather_forward_job(fulls):
    def copies(ins, outs, send, recv):
        x, y, c, chips = _place()
        started, awaited = [], []
        for i, buf in enumerate(outs):
            for j, chip in enumerate(chips):
                mine, theirs = buf.at[_slot((*chip, c))], buf.at[_slot((*chip, 1 - c))]
                started.append(_remote(mine, mine, send, recv, (i, j), (x, y, 1 - c)))
                awaited.append(_remote(theirs, theirs, send, recv, (i, j), (x, y, 1 - c)))
        return started, awaited

    outs = [jax.ShapeDtypeStruct(a.shape, a.dtype) for a in fulls]
    return _Job(fulls, outs, (len(fulls), 3), copies, aliases={i: i for i in range(len(fulls))})


def swap_job(gs):
    def copies(ins, outs, send, recv):
        x, y, c, _ = _place()
        started, awaited = [], []
        for i, (g, r1) in enumerate(zip(ins, outs)):
            for q in range(N_CHIP):
                started.append(_remote(g.at[2 * q + (1 - c)], r1.at[q], send, recv, (i, q), (x, y, 1 - c)))
                awaited.append(_remote(g.at[2 * q + c], r1.at[q], send, recv, (i, q), (x, y, 1 - c)))
        return started, awaited

    outs = [jax.ShapeDtypeStruct((N_CHIP, *g.shape[1:]), g.dtype) for g in gs]
    return _Job(gs, outs, (len(gs), N_CHIP), copies)


def chip_exchange_job(ps, rows=None, into=None):
    n = len(ps)

    def copies(ins, outs, send, recv):
        x, y, c, chips = _place()
        started, awaited = [], []
        for i, (p, r2) in enumerate(zip(ins[:n], outs)):
            for k, chip in enumerate(chips):
                src, mine, dst = p.at[2 * chip[0] + chip[1]], p.at[2 * x + y], r2.at[k]
                if rows is not None:
                    src, mine, dst = (t.at[pl.ds(rows[0], rows[1])] for t in (src, mine, dst))
                started.append(_remote(src, dst, send, recv, (i, k), (*chip, c)))
                awaited.append(_remote(mine, dst, send, recv, (i, k), (*chip, c)))
        return started, awaited

    outs = [jax.ShapeDtypeStruct((3, *p.shape[1:]), p.dtype) for p in ps]
    if into is None:
        return _Job(ps, outs, (n, 3), copies)
    return _Job(list(ps) + list(into), outs, (n, 3), copies, aliases={n + i: i for i in range(n)})


def _call(body, job, *, name, grid, in_specs, out_specs, out_shape, args, scratch_shapes=(), vmem=VMEM_LIMIT):
    if job is None:
        res = pl.pallas_call(
            body, name=name, grid=grid, in_specs=in_specs, out_specs=out_specs, out_shape=out_shape,
            scratch_shapes=list(scratch_shapes), compiler_params=_params(("arbitrary",) * len(grid), vmem),
        )(*args)
        return res, []
    n_in, n_out, n_scr = len(in_specs), len(out_specs), len(scratch_shapes)
    j_in, j_out = len(job.ins), len(job.outs)

    def with_copies(*refs):
        at = 0
        ins = refs[at:at + n_in]; at += n_in
        jins = refs[at:at + j_in]; at += j_in
        outs = refs[at:at + n_out]; at += n_out
        jouts = refs[at:at + j_out]; at += j_out
        scr = refs[at:at + n_scr]; at += n_scr
        sems = refs[at:]
        ids = [pl.program_id(d) for d in range(len(grid))]
        first = functools.reduce(jnp.logical_and, [i == 0 for i in ids])
        last = functools.reduce(jnp.logical_and, [i == n - 1 for i, n in zip(ids, grid)])

        @pl.when(first)
        def _():
            job.start(jins, jouts, sems)

        body(*ins, *outs, *scr)

        @pl.when(last)
        def _():
            job.finish(jins, jouts, sems)

    any_spec = pl.BlockSpec(memory_space=pl.ANY)
    res = pl.pallas_call(
        with_copies, name=name, grid=grid,
        in_specs=list(in_specs) + [any_spec] * j_in, out_specs=list(out_specs) + [any_spec] * j_out,
        out_shape=list(out_shape) + list(job.outs),
        input_output_aliases={n_in + i: n_out + o for i, o in job.aliases.items()},
        scratch_shapes=list(scratch_shapes) + job.scratch(),
        compiler_params=_params(("arbitrary",) * len(grid), vmem),
    )(*args, *job.ins)
    return res[:n_out], res[n_out:]


def run_job(job, name):
    def body(*refs):
        j_in, j_out = len(job.ins), len(job.outs)
        ins, outs, sems = refs[:j_in], refs[j_in:j_in + j_out], refs[j_in + j_out:]
        job.start(ins, outs, sems)
        job.finish(ins, outs, sems)

    any_spec = pl.BlockSpec(memory_space=pl.ANY)
    return pl.pallas_call(
        body, name=name, in_specs=[any_spec] * len(job.ins), out_specs=[any_spec] * len(job.outs),
        out_shape=list(job.outs), input_output_aliases=dict(job.aliases), scratch_shapes=job.scratch(),
    )(*job.ins)


def pair_sum(g, r1, name):
    _, rows, cols = g.shape
    rb = next(cand for cand in range(min(rows, 512), 0, -16) if rows % cand == 0)

    def body(g_ref, r1_ref, p_ref, own_ref):
        x, y, c, _ = _place()
        s = g_ref[c].astype(F32) + r1_ref[0].astype(F32)
        p_ref[0] = s.astype(BF16)

        @pl.when(pl.program_id(1) == 2 * x + y)
        def _():
            own_ref[...] = s

    return pl.pallas_call(
        body, name=name, grid=(rows // rb, N_CHIP),
        in_specs=[pl.BlockSpec((2, rb, cols), lambda i, q: (q, i, 0)), pl.BlockSpec((1, rb, cols), lambda i, q: (q, i, 0))],
        out_specs=[pl.BlockSpec((1, rb, cols), lambda i, q: (q, i, 0)), pl.BlockSpec((rb, cols), lambda i, q: (i, 0))],
        out_shape=[jax.ShapeDtypeStruct((N_CHIP, rows, cols), BF16), jax.ShapeDtypeStruct((rows, cols), F32)],
        compiler_params=_params(("arbitrary", "arbitrary")),
    )(g, r1)


def sum_devices(a, name):
    def body(a_ref, o_ref):
        acc = a_ref[0]
        for d in range(1, N_DEV):
            acc = acc + a_ref[d]
        o_ref[...] = acc

    return pl.pallas_call(body, name=name, out_shape=jax.ShapeDtypeStruct(a.shape[1:], F32))(a)


def adamw(w, g, m, v, name, others=None):
    rows, cols = w.shape
    rb = rows
    for cand in range(min(rows, 512), 7, -8):
        if rows % cand == 0 and cand % 8 == 0:
            rb = cand
            break

    def body(*refs):
        if others is None:
            w_ref, g_ref, m_ref, v_ref, d_ref, nm_ref, nv_ref = refs
            gg = g_ref[...]
        else:
            w_ref, g_ref, m_ref, v_ref, r2_ref, go_ref, d_ref, nm_ref, nv_ref = refs
            gg = g_ref[...]
            for k in range(3):
                gg = gg + r2_ref[k].astype(F32)
            go_ref[...] = gg
        nm = ADAM_B1 * m_ref[...] + (1.0 - ADAM_B1) * gg
        nv = ADAM_B2 * v_ref[...] + (1.0 - ADAM_B2) * (gg * gg)
        m_hat = nm / (1.0 - ADAM_B1 ** ADAM_STEP)
        v_hat = nv / (1.0 - ADAM_B2 ** ADAM_STEP)
        d_ref[...] = -ADAM_LR * (m_hat / (jnp.sqrt(v_hat) + ADAM_EPS) + ADAM_WD * w_ref[...])
        nm_ref[...] = nm
        nv_ref[...] = nv

    spec = pl.BlockSpec((rb, cols), lambda i: (i, 0))
    out = jax.ShapeDtypeStruct((rows, cols), F32)
    in_specs, args = [spec] * 4, [w, g, m, v]
    if others is not None:
        in_specs.append(pl.BlockSpec((3, rb, cols), lambda i: (0, i, 0)))
        args.append(others)
    n_out = 3 if others is None else 4
    res = pl.pallas_call(
        body, name=name, grid=(rows // rb,), in_specs=in_specs, out_specs=[spec] * n_out,
        out_shape=[out] * n_out, compiler_params=_params(("parallel",)),
    )(*args)
    return (g, *res) if others is None else tuple(res)


def ada_fwd(c_all, w_cols, b_cols, name):
    def body(c_ref, w_ref, b_ref, cond_ref, mod_ref):
        cc = c_ref[...]
        cond = (cc * _sigmoid(cc)).astype(BF16)
        cond_ref[...] = cond
        mod_ref[...] = _dot(cond, w_ref[...].astype(BF16)) + b_ref[...]

    n, cols = c_all.shape[0], w_cols.shape[1]
    return pl.pallas_call(
        body, name=name,
        out_shape=[jax.ShapeDtypeStruct(c_all.shape, BF16), jax.ShapeDtypeStruct((n, cols), F32)],
        compiler_params=_params(),
    )(c_all, w_cols, b_cols)


def ada_bwd(cond_all, dmod_cols, name):
    def body(c_ref, d_ref, gw_ref, gb_ref):
        d = d_ref[...]
        gw_ref[...] = _dot_tn(c_ref[...], d.astype(BF16))
        gb_ref[...] = jnp.sum(d, axis=0, keepdims=True)

    dm, cols = cond_all.shape[1], dmod_cols.shape[1]
    return pl.pallas_call(
        body, name=name,
        out_shape=[jax.ShapeDtypeStruct((dm, cols), F32), jax.ShapeDtypeStruct((1, cols), F32)],
        compiler_params=_params(),
    )(cond_all, dmod_cols)


def _mod_spec(tiles_per_seq, dm):
    return pl.BlockSpec((1, 1, dm), lambda i: (i // tiles_per_seq, 0, 0))


def ffn_fwd(x, sh, sc, gt, wgu, wd, ln_g, ln_b, seq, name, target=None, job=None):
    tokens, dm = x.shape
    fc = wgu.shape[1]
    tm = min(FFN_FWD_TILE, seq)
    tiles_per_seq = seq // tm
    with_loss = target is not None

    def body(*refs):
        if with_loss:
            (x_ref, sh_ref, sc_ref, gt_ref, wgu_ref, wd_ref, lg_ref, lb_ref, t_ref,
             xo_ref, loss_ref, r_ref, gu_ref, f_ref) = refs
        else:
            (x_ref, sh_ref, sc_ref, gt_ref, wgu_ref, wd_ref, lg_ref, lb_ref,
             xo_ref, r_ref, gu_ref, f_ref) = refs
        xx = x_ref[...]
        h = (xx * (1.0 + sc_ref[0]) + sh_ref[0]).astype(BF16)
        acc = jnp.zeros((tm, dm), F32)
        for k in range(4):
            gk = _dot_nt(h, wgu_ref[k])
            uk = _dot_nt(h, wgu_ref[k + 4])
            gu_ref[k] = gk.astype(BF16)
            gu_ref[k + 4] = uk.astype(BF16)
            a = (gk * _sigmoid(gk) * uk).astype(BF16)
            acc = acc + _dot(a, wd_ref[k])
        f_ref[...] = acc.astype(BF16)
        r = DN_ALPHA * xx + (0.5 * (1.0 + gt_ref[0])) * acc
        r_ref[...] = r
        xhat, _ = _ln_stats(r)
        yy = xhat * lg_ref[...] + lb_ref[...]
        if with_loss:
            err = yy - t_ref[...]
            xo_ref[...] = err * (1.0 / dm)

            @pl.when(pl.program_id(0) == 0)
            def _():
                loss_ref[...] = jnp.zeros_like(loss_ref)

            loss_ref[...] += jnp.full((1, 128), (0.5 / dm) * jnp.sum(err * err), F32)
        else:
            xo_ref[...] = yy

    tile = pl.BlockSpec((tm, dm), lambda i: (i, 0))
    mod = _mod_spec(tiles_per_seq, dm)
    in_specs = [tile, mod, mod, mod, _const_spec(wgu.shape), _const_spec(wd.shape),
                _const_spec((1, dm)), _const_spec((1, dm))]
    args = [x, sh, sc, gt, wgu, wd, ln_g, ln_b]
    out_specs = [tile]
    out_shape = [jax.ShapeDtypeStruct((tokens, dm), F32)]
    if with_loss:
        in_specs.append(tile)
        args.append(target)
        out_specs.append(pl.BlockSpec((1, 128), lambda i: (0, 0)))
        out_shape.append(jax.ShapeDtypeStruct((1, 128), F32))
    out_specs += [tile, pl.BlockSpec((8, tm, fc), lambda i: (0, i, 0)), tile]
    out_shape += [jax.ShapeDtypeStruct((tokens, dm), F32), jax.ShapeDtypeStruct((8, tokens, fc), BF16),
                  jax.ShapeDtypeStruct((tokens, dm), BF16)]
    return _call(body, job, name=name, grid=(tokens // tm,), in_specs=in_specs, out_specs=out_specs,
                 out_shape=out_shape, args=args)


def ffn_bwd(dy, r, x, f, gu, sh, sc, gt, wgu, wd, ln_g, seq, name, job=None):
    tokens, dm = x.shape
    fc = wgu.shape[1]
    tm = min(TOKEN_TILE, seq)
    tiles_per_seq = seq // tm
    nseq = tokens // seq

    def body(dy_ref, r_ref, x_ref, f_ref, gu_ref, sh_ref, sc_ref, gt_ref, wgu_ref, wd_ref, lg_ref,
             dx_ref, dgu_ref, df_ref, a_ref, h_ref, dln_ref, dmod_ref):
        i = pl.program_id(0)
        dr, dgain, dbias = _ln_bwd(dy_ref[...], r_ref[...], lg_ref[...])

        @pl.when(i == 0)
        def _():
            dln_ref[...] = jnp.zeros_like(dln_ref)

        @pl.when(i % tiles_per_seq == 0)
        def _():
            dmod_ref[...] = jnp.zeros_like(dmod_ref)

        dln_ref[0:1, :] += dgain
        dln_ref[1:2, :] += dbias
        df32 = (0.5 * (1.0 + gt_ref[0])) * dr
        df = df32.astype(BF16)
        df_ref[...] = df
        dgate = jnp.sum(dr * (0.5 * f_ref[...].astype(F32)), axis=0, keepdims=True)
        xx = x_ref[...]
        one_sc = 1.0 + sc_ref[0]
        h = (xx * one_sc + sh_ref[0]).astype(BF16)
        h_ref[...] = h
        dh = jnp.zeros((tm, dm), F32)
        for k in range(4):
            da = _dot_nt(df, wd_ref[k])
            gk = gu_ref[k].astype(F32)
            uk = gu_ref[k + 4].astype(F32)
            sg = _sigmoid(gk)
            sil = gk * sg
            a_ref[k] = (sil * uk).astype(BF16)
            du = (da * sil).astype(BF16)
            dg = (da * uk * (sg * (1.0 + gk * (1.0 - sg)))).astype(BF16)
            dgu_ref[k] = dg
            dgu_ref[k + 4] = du
            dh = dh + _dot(dg, wgu_ref[k]) + _dot(du, wgu_ref[k + 4])
        dx_ref[...] = DN_ALPHA * dr + dh * one_sc
        dmod_ref[0, 0:1, :] += jnp.sum(dh, axis=0, keepdims=True)
        dmod_ref[0, 1:2, :] += jnp.sum(dh * xx, axis=0, keepdims=True)
        dmod_ref[0, 2:3, :] += dgate

    tile = pl.BlockSpec((tm, dm), lambda i: (i, 0))
    mod = _mod_spec(tiles_per_seq, dm)
    gu_spec = pl.BlockSpec((8, tm, fc), lambda i: (0, i, 0))
    return _call(
        body, job, name=name, grid=(tokens // tm,),
        in_specs=[tile, tile, tile, tile, gu_spec, mod, mod, mod, _const_spec(wgu.shape), _const_spec(wd.shape),
                  _const_spec((1, dm))],
        out_specs=[tile, gu_spec, tile, pl.BlockSpec((4, tm, fc), lambda i: (0, i, 0)), tile,
                   pl.BlockSpec((2, dm), lambda i: (0, 0)),
                   pl.BlockSpec((1, 3, dm), lambda i: (i // tiles_per_seq, 0, 0))],
        out_shape=[jax.ShapeDtypeStruct((tokens, dm), F32), jax.ShapeDtypeStruct((8, tokens, fc), BF16),
                   jax.ShapeDtypeStruct((tokens, dm), BF16), jax.ShapeDtypeStruct((4, tokens, fc), BF16),
                   jax.ShapeDtypeStruct((tokens, dm), BF16), jax.ShapeDtypeStruct((2, dm), F32),
                   jax.ShapeDtypeStruct((nseq, 3, dm), F32)],
        args=(dy, r, x, f, gu, sh, sc, gt, wgu, wd, ln_g))


def tn_matmul(a, b, name, job=None, b_cols=None):
    na, tokens, kk = a.shape
    nb, _, cc = b.shape
    col = 0
    if b_cols is not None:
        col, cc = b_cols
    tt = tokens
    while 4 * tt * (kk + cc) + 8 * kk * cc > TN_VMEM_BUDGET and tt % 2 == 0 and tt > 256:
        tt //= 2
    steps = tokens // tt

    def body(a_ref, b_ref, o_ref, *acc):
        if steps == 1:
            o_ref[0, 0] = _dot_tn(a_ref[0], b_ref[0]).astype(BF16)
            return
        acc_ref, = acc
        t = pl.program_id(2)

        @pl.when(t == 0)
        def _():
            acc_ref[...] = jnp.zeros_like(acc_ref)

        acc_ref[...] += _dot_tn(a_ref[0], b_ref[0])

        @pl.when(t == steps - 1)
        def _():
            o_ref[0, 0] = acc_ref[...].astype(BF16)

    return _call(
        body, job, name=name, grid=(na, nb, steps),
        in_specs=[pl.BlockSpec((1, tt, kk), lambda i, j, t: (i, t, 0)),
                  pl.BlockSpec((1, tt, cc), lambda i, j, t: (j, t, col))],
        out_specs=[pl.BlockSpec((1, 1, kk, cc), lambda i, j, t: (i, j, 0, 0))],
        out_shape=[jax.ShapeDtypeStruct((na, nb, kk, cc), BF16)],
        scratch_shapes=[] if steps == 1 else [pltpu.VMEM((kk, cc), F32)], args=(a, b))


def proj_fwd(x1, sh, sc, w_in, seq, name, job=None):
    tokens, dm = x1.shape
    tm = min(TOKEN_TILE, seq)
    tiles_per_seq = seq // tm
    widths = [N_Q_HEADS * HEAD_DIM, N_KV_HEADS * HEAD_DIM, N_KV_HEADS * HEAD_DIM, 512, 512, 512]
    assert sum(widths) == w_in.shape[0]

    def body(x_ref, sh_ref, sc_ref, w_ref, *outs):
        h = (x_ref[...] * (1.0 + sc_ref[0]) + sh_ref[0]).astype(BF16)
        proj = _dot_nt(h, w_ref[...])
        at = 0
        for o_ref, wdt in zip(outs, widths):
            o_ref[...] = proj[:, at:at + wdt]
            at += wdt

    tile = pl.BlockSpec((tm, dm), lambda i: (i, 0))
    mod = _mod_spec(tiles_per_seq, dm)
    return _call(
        body, job, name=name, grid=(tokens // tm,),
        in_specs=[tile, mod, mod, _const_spec(w_in.shape)],
        out_specs=[pl.BlockSpec((tm, wdt), lambda i: (i, 0)) for wdt in widths],
        out_shape=[jax.ShapeDtypeStruct((tokens, wdt), F32) for wdt in widths],
        args=(x1, sh, sc, w_in))


LANES = 2 * HEAD_DIM


def _head_lane(shape):
    return lax.broadcasted_iota(jnp.int32, shape, 1) % HEAD_DIM


def _lane_half(shape):
    return lax.broadcasted_iota(jnp.int32, shape, 1) // HEAD_DIM


def _swap_rot(v):
    lane = _head_lane(v.shape)
    half = ROT_DIM // 2
    return jnp.where(lane < half, pltpu.roll(v, LANES - half, 1),
                     jnp.where(lane < ROT_DIM, pltpu.roll(v, half, 1), 0.0))


def _rope(v, cos_t, sin_t):
    return v * cos_t + _swap_rot(v) * sin_t


def _unrope(dv, cos_t, sin_t):
    return dv * cos_t + _swap_rot(dv * sin_t)


def _both_halves(t, g):
    return jnp.where(_lane_half(t.shape) == g, t, pltpu.roll(t, HEAD_DIM, 1))


def _fold_halves(t, g):
    return jnp.where(_lane_half(t.shape) == g, t + pltpu.roll(t, HEAD_DIM, 1), 0.0)


def _stack_heads(blocks):
    rows = []
    for blk in blocks:
        half = _lane_half(blk.shape)
        rows += [jnp.where(half == 0, blk, 0.0), jnp.where(half == 1, blk, 0.0)]
    return jnp.concatenate(rows, axis=0)


def _unstack_heads(t, j):
    lo = t[(2 * j) * ATTN_BLOCK:(2 * j + 1) * ATTN_BLOCK]
    hi = t[(2 * j + 1) * ATTN_BLOCK:(2 * j + 2) * ATTN_BLOCK]
    return jnp.where(_lane_half(lo.shape) == 0, lo, hi)


def _band_mask(q0, w0):
    rows, cols = GQA_GROUP * ATTN_BLOCK, 2 * ATTN_BLOCK
    qi = lax.broadcasted_iota(jnp.int32, (rows, cols), 0) % ATTN_BLOCK + q0
    ki = lax.broadcasted_iota(jnp.int32, (rows, cols), 1) + w0
    diff = qi - ki
    return (diff >= 0) & (diff < ATTN_BLOCK)


def _attn_specs(seq):
    q_spec = pl.BlockSpec((seq, GQA_GROUP * HEAD_DIM), lambda b, g: (b, g))
    kv_spec = pl.BlockSpec((seq, LANES), lambda b, g: (b, 0))
    sink_spec = pl.BlockSpec((1, GQA_GROUP * ATTN_BLOCK, 1), lambda b, g: (g, 0, 0))
    return q_spec, kv_spec, sink_spec


def _block_starts(n):
    q0 = pl.multiple_of(n * ATTN_BLOCK, ATTN_BLOCK)
    w0 = pl.multiple_of(jnp.maximum(n - 1, 0) * ATTN_BLOCK, ATTN_BLOCK)
    return q0, w0


def _stacked_queries(ref, rows, cos_b=None, sin_b=None):
    blocks = []
    for j in range(2):
        blk = ref[rows, j * LANES:(j + 1) * LANES]
        blocks.append(blk if cos_b is None else _rope(blk, cos_b, sin_b))
    return _stack_heads(blocks).astype(BF16)


def _sink_columns(sinks):
    return jnp.repeat(sinks.reshape(N_KV_HEADS, GQA_GROUP), ATTN_BLOCK, axis=1)[:, :, None]


def _probs_spec(nblk):
    return pl.BlockSpec((1, 1, nblk, GQA_GROUP * ATTN_BLOCK, 2 * ATTN_BLOCK), lambda b, g: (b, g, 0, 0, 0))


def _sink_probs_spec():
    return pl.BlockSpec((1, 1, GQA_GROUP * ATTN_BLOCK, LANES), lambda b, g: (b, g, 0, 0))


def attn_fwd(q, k, v, cos_t, sin_t, sinks, seq, name, job=None):
    tokens = q.shape[0]
    nblk = seq // ATTN_BLOCK
    assert nblk >= 2
    scale = HEAD_DIM ** -0.5

    nseq = tokens // seq
    rows_stacked = GQA_GROUP * ATTN_BLOCK
    assert nblk <= LANES

    def body(q_ref, k_ref, v_ref, cos_ref, sin_ref, sink_ref, o_ref, qr_ref, p_ref, ps_ref, kd_ref, vd_ref):
        g = pl.program_id(1)
        kd_ref[...] = _both_halves(_rope(k_ref[...], cos_ref[...], sin_ref[...]), g).astype(BF16)
        vd_ref[...] = _both_halves(v_ref[...], g).astype(BF16)
        sink = sink_ref[0]
        lane = lax.broadcasted_iota(jnp.int32, (rows_stacked, LANES), 1)

        ps_ref[...] = jnp.zeros_like(ps_ref)

        def block(n, carry):
            q0, w0 = _block_starts(n)
            rows, win = pl.ds(q0, ATTN_BLOCK), pl.ds(w0, 2 * ATTN_BLOCK)
            blocks = []
            for j in range(2):
                qr = _rope(q_ref[rows, j * LANES:(j + 1) * LANES], cos_ref[rows, :], sin_ref[rows, :]).astype(BF16)
                qr_ref[rows, j * LANES:(j + 1) * LANES] = qr
                blocks.append(qr)
            qs = _stack_heads(blocks)
            s = _dot_nt(qs, kd_ref[win, :]) * scale
            s = jnp.where(_band_mask(q0, w0), s, NEG_BIG)
            m = jnp.maximum(jnp.max(s, axis=-1, keepdims=True), sink)
            p = jnp.exp(s - m)
            e_sink = jnp.exp(sink - m)
            inv = pl.reciprocal(jnp.sum(p, axis=-1, keepdims=True) + e_sink, approx=True)
            pn = (p * inv).astype(BF16)
            p_ref[0, 0, n] = pn
            out = _dot(pn, vd_ref[win, :])
            for j in range(2):
                o_ref[rows, j * LANES:(j + 1) * LANES] = _unstack_heads(out, j).astype(o_ref.dtype)
            ps_ref[0, 0] = jnp.where(lane == n, e_sink * inv, ps_ref[0, 0])
            return carry

        lax.fori_loop(0, nblk, block, 0, unroll=2)

    q_spec, kv_spec, sink_spec = _attn_specs(seq)
    return _call(
        body, job, name=name, grid=(nseq, N_KV_HEADS),
        in_specs=[q_spec, kv_spec, kv_spec, kv_spec, kv_spec, sink_spec],
        out_specs=[q_spec, q_spec, _probs_spec(nblk), _sink_probs_spec()],
        out_shape=[jax.ShapeDtypeStruct(q.shape, BF16), jax.ShapeDtypeStruct(q.shape, BF16),
                   jax.ShapeDtypeStruct((nseq, N_KV_HEADS, nblk, rows_stacked, 2 * ATTN_BLOCK), BF16),
                   jax.ShapeDtypeStruct((nseq, N_KV_HEADS, rows_stacked, LANES), F32)],
        scratch_shapes=[pltpu.VMEM((seq, LANES), BF16), pltpu.VMEM((seq, LANES), BF16)],
        args=(q, k, v, cos_t, sin_t, _sink_columns(sinks)))


def attn_bwd(qr, k, v, do, probs, sink_probs, cos_t, sin_t, seq, name, job=None):
    tokens = qr.shape[0]
    nseq = tokens // seq
    nblk = seq // ATTN_BLOCK
    assert nblk >= 2
    rows_stacked = GQA_GROUP * ATTN_BLOCK
    scale = HEAD_DIM ** -0.5

    def body(q_ref, k_ref, v_ref, do_ref, p_ref, ps_ref, cos_ref, sin_ref, dq_ref, dk_ref, dv_ref, ds_ref,
             kd_ref, vd_ref, dkd_ref, dvd_ref, acc_ref):
        g = pl.program_id(1)
        kd_ref[...] = _both_halves(_rope(k_ref[...], cos_ref[...], sin_ref[...]), g).astype(BF16)
        vd_ref[...] = _both_halves(v_ref[...], g).astype(BF16)
        dkd_ref[...] = jnp.zeros_like(dkd_ref)
        dvd_ref[...] = jnp.zeros_like(dvd_ref)
        acc_ref[...] = jnp.zeros_like(acc_ref)
        lane = lax.broadcasted_iota(jnp.int32, (rows_stacked, LANES), 1)

        def block(n, carry):
            q0, w0 = _block_starts(n)
            rows, win = pl.ds(q0, ATTN_BLOCK), pl.ds(w0, 2 * ATTN_BLOCK)
            qs = _stacked_queries(q_ref, rows)
            dos = _stacked_queries(do_ref, rows)
            kw, vw = kd_ref[win, :], vd_ref[win, :]
            pn16 = p_ref[0, 0, n]
            pn = pn16.astype(F32)
            dvd_ref[win, :] += _dot_tn(pn16, dos)
            dp = _dot_nt(dos, vw)
            delta = jnp.sum(dp * pn, axis=-1, keepdims=True)
            ds = (pn * (dp - delta)).astype(BF16)
            dqs = _dot(ds, kw) * scale
            dkd_ref[win, :] += _dot_tn(ds, qs) * scale
            cos_b, sin_b = cos_ref[rows, :], sin_ref[rows, :]
            for j in range(2):
                dq_ref[rows, j * LANES:(j + 1) * LANES] = _unrope(_unstack_heads(dqs, j), cos_b, sin_b).astype(BF16)
            acc_ref[...] += jnp.where(lane == n, ps_ref[0, 0] * delta, 0.0)
            return carry

        lax.fori_loop(0, nblk // 2, lambda i, carry: block(2 * i + 1, block(2 * i, carry)), 0)
        ds_ref[0, 0] = -jnp.sum(acc_ref[...], axis=-1, keepdims=True)
        dk_g = _unrope(_fold_halves(dkd_ref[...], g), cos_ref[...], sin_ref[...])
        dv_g = _fold_halves(dvd_ref[...], g)

        @pl.when(g == 0)
        def _():
            dk_ref[...] = dk_g
            dv_ref[...] = dv_g

        @pl.when(g != 0)
        def _():
            dk_ref[...] += dk_g
            dv_ref[...] += dv_g

    q_spec, kv_spec, _ = _attn_specs(seq)
    return _call(
        body, job, name=name, grid=(nseq, N_KV_HEADS),
        in_specs=[q_spec, kv_spec, kv_spec, q_spec, _probs_spec(nblk), _sink_probs_spec(), kv_spec, kv_spec],
        out_specs=[q_spec, kv_spec, kv_spec, pl.BlockSpec((1, 1, rows_stacked, 1), lambda b, g: (b, g, 0, 0))],
        out_shape=[jax.ShapeDtypeStruct(qr.shape, BF16), jax.ShapeDtypeStruct(k.shape, F32),
                   jax.ShapeDtypeStruct(k.shape, F32), jax.ShapeDtypeStruct((nseq, N_KV_HEADS, rows_stacked, 1), F32)],
        scratch_shapes=[pltpu.VMEM((seq, LANES), BF16), pltpu.VMEM((seq, LANES), BF16),
                        pltpu.VMEM((seq, LANES), F32), pltpu.VMEM((seq, LANES), F32),
                        pltpu.VMEM((rows_stacked, LANES), F32)],
        args=(qr, k, v, do, probs, sink_probs, cos_t, sin_t))


CONV_COLS = 128


def _shift_down(z, by):
    t = lax.broadcasted_iota(jnp.int32, z.shape, 0)
    return jnp.where(t >= by, pltpu.roll(z, by, 0), 0.0)


def _shift_up(z, by):
    n = z.shape[0]
    t = lax.broadcasted_iota(jnp.int32, z.shape, 0)
    return jnp.where(t < n - by, pltpu.roll(z, n - by, 0), 0.0)


def conv_fwd(u, bg, cg, conv_w, seq, name):
    tokens, width = u.shape

    def body(u_ref, bg_ref, cg_ref, w_ref, o_ref):
        z = cg_ref[...] * u_ref[...]
        yy = w_ref[2:3, :] * z + w_ref[1:2, :] * _shift_down(z, 1) + w_ref[0:1, :] * _shift_down(z, 2)
        o_ref[...] = (bg_ref[...] * yy).astype(BF16)

    col = pl.BlockSpec((seq, CONV_COLS), lambda j, b: (b, j))
    return pl.pallas_call(
        body, name=name, grid=(width // CONV_COLS, tokens // seq),
        in_specs=[col, col, col, pl.BlockSpec((CONV_TAPS, CONV_COLS), lambda j, b: (0, j))],
        out_specs=col, out_shape=jax.ShapeDtypeStruct((tokens, width), BF16),
        compiler_params=_params(("parallel", "parallel")),
    )(u, bg, cg, conv_w)


def conv_bwd(dout, u, bg, cg, conv_w, seq, name):
    tokens, width = u.shape

    def body(do_ref, u_ref, bg_ref, cg_ref, w_ref, du_ref, dbg_ref, dcg_ref, dw_ref):
        uu, cg_v, do = u_ref[...], cg_ref[...], do_ref[...].astype(F32)
        z = cg_v * uu
        z1, z2 = _shift_down(z, 1), _shift_down(z, 2)
        yy = w_ref[2:3, :] * z + w_ref[1:2, :] * z1 + w_ref[0:1, :] * z2
        dbg_ref[...] = (do * yy).astype(BF16)
        dyy = do * bg_ref[...]
        dz = w_ref[2:3, :] * dyy + w_ref[1:2, :] * _shift_up(dyy, 1) + w_ref[0:1, :] * _shift_up(dyy, 2)
        du_ref[...] = (dz * cg_v).astype(BF16)
        dcg_ref[...] = (dz * uu).astype(BF16)

        @pl.when(pl.program_id(1) == 0)
        def _():
            dw_ref[...] = jnp.zeros_like(dw_ref)

        dw_ref[0:1, :] += jnp.sum(dyy * z2, axis=0, keepdims=True)
        dw_ref[1:2, :] += jnp.sum(dyy * z1, axis=0, keepdims=True)
        dw_ref[2:3, :] += jnp.sum(dyy * z, axis=0, keepdims=True)

    col = pl.BlockSpec((seq, CONV_COLS), lambda j, b: (b, j))
    w_spec = pl.BlockSpec((CONV_TAPS, CONV_COLS), lambda j, b: (0, j))
    act = jax.ShapeDtypeStruct((tokens, width), BF16)
    return pl.pallas_call(
        body, name=name, grid=(width // CONV_COLS, tokens // seq),
        in_specs=[col, col, col, col, w_spec], out_specs=[col, col, col, w_spec],
        out_shape=[act, act, act, jax.ShapeDtypeStruct((CONV_TAPS, width), F32)],
        compiler_params=_params(("parallel", "arbitrary")),
    )(dout, u, bg, cg, conv_w)


def out_fwd(x1, attn, conv, gt, w_out, ln_g, ln_b, seq, name, job=None):
    tokens, dm = x1.shape
    half = attn.shape[1]
    tm = min(TOKEN_TILE, seq)
    tiles_per_seq = seq // tm

    def body(x_ref, a_ref, c_ref, gt_ref, w_ref, lg_ref, lb_ref, xo_ref, r_ref, mi_ref, mix_ref):
        mixin = jnp.concatenate([a_ref[...], c_ref[...]], axis=1).astype(BF16)
        mi_ref[...] = mixin
        mix = _dot(mixin, w_ref[...])
        mix_ref[...] = mix.astype(BF16)
        r = DN_ALPHA * x_ref[...] + (1.0 + gt_ref[0]) * mix
        r_ref[...] = r
        xhat, _ = _ln_stats(r)
        xo_ref[...] = xhat * lg_ref[...] + lb_ref[...]

    tile = pl.BlockSpec((tm, dm), lambda i: (i, 0))
    htile = pl.BlockSpec((tm, half), lambda i: (i, 0))
    return _call(
        body, job, name=name, grid=(tokens // tm,),
        in_specs=[tile, htile, htile, _mod_spec(tiles_per_seq, dm), _const_spec(w_out.shape),
                  _const_spec((1, dm)), _const_spec((1, dm))],
        out_specs=[tile, tile, tile, tile],
        out_shape=[jax.ShapeDtypeStruct((tokens, dm), F32), jax.ShapeDtypeStruct((tokens, dm), F32),
                   jax.ShapeDtypeStruct((tokens, dm), BF16), jax.ShapeDtypeStruct((tokens, dm), BF16)],
        args=(x1, attn, conv, gt, w_out, ln_g, ln_b))


def out_bwd(dy, r, mix, gt, w_out, ln_g, seq, name, job=None):
    tokens, dm = r.shape
    half = dm // 2
    tm = min(TOKEN_TILE, seq)
    tiles_per_seq = seq // tm
    nseq = tokens // seq

    def body(dy_ref, r_ref, mix_ref, gt_ref, w_ref, lg_ref, dres_ref, da_ref, dc_ref, dmix_ref, dln_ref, dgt_ref):
        i = pl.program_id(0)
        dr, dgain, dbias = _ln_bwd(dy_ref[...], r_ref[...], lg_ref[...])

        @pl.when(i == 0)
        def _():
            dln_ref[...] = jnp.zeros_like(dln_ref)

        @pl.when(i % tiles_per_seq == 0)
        def _():
            dgt_ref[...] = jnp.zeros_like(dgt_ref)

        dln_ref[0:1, :] += dgain
        dln_ref[1:2, :] += dbias
        dgt_ref[0] += jnp.sum(dr * mix_ref[...].astype(F32), axis=0, keepdims=True)
        dres_ref[...] = DN_ALPHA * dr
        dmix = ((1.0 + gt_ref[0]) * dr).astype(BF16)
        dmix_ref[...] = dmix
        dmixin = _dot_nt(dmix, w_ref[...])
        da_ref[...] = dmixin[:, :half].astype(BF16)
        dc_ref[...] = dmixin[:, half:].astype(BF16)

    tile = pl.BlockSpec((tm, dm), lambda i: (i, 0))
    htile = pl.BlockSpec((tm, half), lambda i: (i, 0))
    return _call(
        body, job, name=name, grid=(tokens // tm,),
        in_specs=[tile, tile, tile, _mod_spec(tiles_per_seq, dm), _const_spec(w_out.shape), _const_spec((1, dm))],
        out_specs=[tile, htile, htile, tile, pl.BlockSpec((2, dm), lambda i: (0, 0)),
                   pl.BlockSpec((1, 1, dm), lambda i: (i // tiles_per_seq, 0, 0))],
        out_shape=[jax.ShapeDtypeStruct((tokens, dm), F32), jax.ShapeDtypeStruct((tokens, half), BF16),
                   jax.ShapeDtypeStruct((tokens, half), BF16), jax.ShapeDtypeStruct((tokens, dm), BF16),
                   jax.ShapeDtypeStruct((2, dm), F32), jax.ShapeDtypeStruct((nseq, 1, dm), F32)],
        args=(dy, r, mix, gt, w_out, ln_g))


def proj_bwd(parts, dres, x1, sh, sc, w_in, seq, name, job=None):
    tokens, dm = x1.shape
    tm = min(TOKEN_TILE, seq)
    tiles_per_seq = seq // tm
    nseq = tokens // seq
    widths = [p.shape[1] for p in parts]
    total = sum(widths)

    def body(*refs):
        part_refs = refs[:6]
        dres_ref, x_ref, sh_ref, sc_ref, w_ref, dx_ref, dproj_ref, h_ref, dmod_ref = refs[6:]
        dproj = jnp.concatenate([p[...].astype(BF16) for p in part_refs], axis=1)
        dproj_ref[...] = dproj
        dh = _dot(dproj, w_ref[...])
        xx = x_ref[...]
        one_sc = 1.0 + sc_ref[0]
        h_ref[...] = (xx * one_sc + sh_ref[0]).astype(BF16)
        dx_ref[...] = dres_ref[...] + dh * one_sc

        @pl.when(pl.program_id(0) % tiles_per_seq == 0)
        def _():
            dmod_ref[...] = jnp.zeros_like(dmod_ref)

        dmod_ref[0, 0:1, :] += jnp.sum(dh, axis=0, keepdims=True)
        dmod_ref[0, 1:2, :] += jnp.sum(dh * xx, axis=0, keepdims=True)

    tile = pl.BlockSpec((tm, dm), lambda i: (i, 0))
    mod = _mod_spec(tiles_per_seq, dm)
    return _call(
        body, job, name=name, grid=(tokens // tm,),
        in_specs=[pl.BlockSpec((tm, wdt), lambda i: (i, 0)) for wdt in widths]
        + [tile, tile, mod, mod, _const_spec(w_in.shape)],
        out_specs=[tile, pl.BlockSpec((tm, total), lambda i: (i, 0)), tile,
                   pl.BlockSpec((1, 2, dm), lambda i: (i // tiles_per_seq, 0, 0))],
        out_shape=[jax.ShapeDtypeStruct((tokens, dm), F32), jax.ShapeDtypeStruct((tokens, total), BF16),
                   jax.ShapeDtypeStruct((tokens, dm), BF16), jax.ShapeDtypeStruct((nseq, 2, dm), F32)],
        args=(*parts, dres, x1, sh, sc, w_in))


def _rope_tables(positions):
    half = ROT_DIM // 2
    inv_freq = jnp.power(jnp.float32(ROPE_THETA), -jnp.arange(0, ROT_DIM, 2, dtype=F32) / ROT_DIM)
    lane = jnp.arange(LANES) % HEAD_DIM
    freq = jnp.where(lane < ROT_DIM, inv_freq[lane % half], 0.0)
    sign = jnp.where(lane < half, -1.0, 1.0).astype(F32)
    ang = positions.astype(F32)[:, None] * freq[None, :]
    return jnp.cos(ang), sign[None, :] * jnp.sin(ang)


def kernel(x, c, positions, w_ada, b_ada, ffn1_w_gate_up, ffn1_w_down, ln1_g, ln1_b, w_in, conv_w, attn_sinks, w_out, ln2_g, ln2_b, ffn2_w_gate_up, ffn2_w_down, ln3_g, ln3_b, loss_target, m_w_ada, m_b_ada, m_ffn1_w_gate_up, m_ffn1_w_down, m_ln1_g, m_ln1_b, m_w_in, m_conv_w, m_attn_sinks, m_w_out, m_ln2_g, m_ln2_b, m_ffn2_w_gate_up, m_ffn2_w_down, m_ln3_g, m_ln3_b, v_w_ada, v_b_ada, v_ffn1_w_gate_up, v_ffn1_w_down, v_ln1_g, v_ln1_b, v_w_in, v_conv_w, v_attn_sinks, v_w_out, v_ln2_g, v_ln2_b, v_ffn2_w_gate_up, v_ffn2_w_down, v_ln3_g, v_ln3_b):
    nseq, seq, dm = x.shape
    tokens = nseq * seq
    dev = 4 * lax.axis_index("x") + 2 * lax.axis_index("y") + lax.axis_index("c")
    ada_cols = w_ada.shape[2]
    ff = ffn1_w_down.shape[1] * N_DEV
    fc = ff // 4
    in_cols = w_in.shape[2]
    conv_cols = conv_w.shape[2]

    def t_bf16(w):
        return w[0].T.astype(BF16)

    c_all, convw_all = all_gather([c, conv_w[0]], "gather_cond")
    wgu1, wd1 = all_gather([t_bf16(ffn1_w_gate_up), ffn1_w_down[0].astype(BF16)], "gather_ffn1")
    c_all = c_all.reshape(N_DEV * nseq, dm)
    convw_full = convw_all.transpose(1, 0, 2).reshape(CONV_TAPS, N_DEV * conv_cols)
    wd1 = wd1.reshape(4, fc, dm)

    b_cols = lax.dynamic_slice(b_ada, (0, dev * ada_cols), (1, ada_cols))
    cond_all, mod_cols = ada_fwd(c_all, w_ada[0], b_cols, "ada_fwd")
    (mod_all,) = all_gather([mod_cols], "gather_mod")
    mod = lax.dynamic_slice(mod_all, (0, dev * nseq, 0), (N_DEV, nseq, ada_cols))
    mod = mod.transpose(1, 0, 2).reshape(nseq, 9, 1, dm)
    sh1, sc1, g1, sh2, sc2, g2, sh3, sc3, g3 = [mod[:, i] for i in range(9)]

    x0 = x.reshape(tokens, dm)
    spread = gather_spread_job([t_bf16(w_in), w_out[0].astype(BF16), ffn2_w_down[0].astype(BF16)])
    (x1, r1, gu1, f1), spread = ffn_fwd(x0, sh1, sc1, g1, wgu1, wd1, ln1_g, ln1_b, seq, "ffn1_fwd", job=spread)
    win, wout = run_job(gather_forward_job(spread[:2]), "gather_mix_forward")
    win = win.reshape(N_DEV * in_cols, dm)
    wout = wout.reshape(dm, dm)
    (q, k, v, u, bg, cg), (wd2,) = proj_fwd(x1, sh2, sc2, win, seq, "proj_fwd", job=gather_forward_job(spread[2:]))
    wd2 = wd2.reshape(4, fc, dm)
    cos_t, sin_t = _rope_tables(positions.reshape(tokens))
    sinks = attn_sinks[0]
    (attn, q_rot, probs, sink_probs), spread = attn_fwd(q, k, v, cos_t, sin_t, sinks, seq, "attn_fwd",
                                                        job=gather_spread_job([t_bf16(ffn2_w_gate_up)]))
    conv = conv_fwd(u, bg, cg, convw_full, seq, "conv_fwd")
    (x2, r2, mixin, mix), (wgu2,) = out_fwd(x1, attn, conv, g2, wout, ln2_g, ln2_b, seq, "out_fwd",
                                            job=gather_forward_job(spread))
    target = loss_target.reshape(tokens, dm)
    (dy3, loss_part, r3, gu3, f3), _ = ffn_fwd(x2, sh3, sc3, g3, wgu2, wd2, ln3_g, ln3_b, seq, "ffn2_fwd", target=target)

    (dx2, dgu3, df3, a3, h3, dln3, dmod3), _ = ffn_bwd(dy3, r3, x2, f3, gu3, sh3, sc3, g3, wgu2, wd2, ln3_g, seq, "ffn2_bwd")
    g_wd2 = tn_matmul(a3, df3[None], "ffn2_dwd")[0][0].reshape(N_DEV, ff // N_DEV, dm)
    g_wgu2 = tn_matmul(dgu3, h3[None], "ffn2_dwgu")[0][0].reshape(N_DEV, fc, dm)
    (dres2, dattn, dconv, dmix, dln2, dg2), swapped = out_bwd(dx2, r2, mix, g2, wout, ln2_g, seq, "out_bwd",
                                                              job=swap_job([g_wgu2, g_wd2]))
    p_wgu2, own_wgu2 = pair_sum(g_wgu2, swapped[0], "pair_wgu2")
    p_wd2, own_wd2 = pair_sum(g_wd2, swapped[1], "pair_wd2")
    du, dbg, dcg, dconvw = conv_bwd(dconv, u, bg, cg, convw_full, seq, "conv_bwd")
    (dq, dk, dv, dsink_rows), (far_wd2,) = attn_bwd(
        q_rot, k, v, dattn, probs, sink_probs, cos_t, sin_t, seq, "attn_bwd", job=chip_exchange_job([p_wd2]))
    parts = [dq, dk, dv, du, dbg, dcg]
    (dx1, dproj, h2, dmod2), far_top = proj_bwd(parts, dres2, x1, sh2, sc2, win, seq, "proj_bwd",
                                                job=chip_exchange_job([p_wgu2], rows=(0, fc // 2)))
    (dx0, dgu1, df1, a1, h1, dln1, dmod1), _ = ffn_bwd(
        dx1, r1, x0, f1, gu1, sh1, sc1, g1, wgu1, wd1, ln1_g, seq, "ffn1_bwd")

    dmod = jnp.concatenate([dmod1, dmod2, dg2, dmod3], axis=1).reshape(nseq, 9 * dm)
    half = dm // 2
    jobs = _Jobs([gather_spread_job([dmod]),
                  chip_exchange_job([p_wgu2], rows=(fc // 2, fc // 2), into=far_top)])
    (g_wd1,), res = tn_matmul(a1, df1[None], "ffn1_dwd", job=jobs)
    dmod_spread, (far_wgu2,) = jobs.split(res)
    g_wd1 = g_wd1.reshape(N_DEV, ff // N_DEV, dm)
    jobs = _Jobs([swap_job([g_wd1]), gather_forward_job(dmod_spread)])
    (g_l,), res = tn_matmul(dgu1, h1[None], "ffn1_dwgu_l", job=jobs, b_cols=(0, half))
    (sw_wd1,), (dmod_all,) = jobs.split(res)
    g_l = g_l.reshape(N_DEV, fc, half)
    p_wd1, own_wd1 = pair_sum(g_wd1, sw_wd1, "pair_wd1")
    jobs = _Jobs([chip_exchange_job([p_wd1]), swap_job([g_l])])
    (g_r,), res = tn_matmul(dgu1, h1[None], "ffn1_dwgu_r", job=jobs, b_cols=(1, half))
    (far_wd1,), (sw_l,) = jobs.split(res)
    g_r = g_r.reshape(N_DEV, fc, half)
    p_l, own_l = pair_sum(g_l, sw_l, "pair_wgu1_l")
    jobs = _Jobs([chip_exchange_job([p_l]), swap_job([g_r])])
    (g_win,), res = tn_matmul(dproj[None], h2[None], "dwin", job=jobs)
    (far_l,), (sw_r,) = jobs.split(res)
    g_win = g_win.reshape(N_DEV, in_cols, dm)
    p_r, own_r = pair_sum(g_r, sw_r, "pair_wgu1_r")
    jobs = _Jobs([chip_exchange_job([p_r]), swap_job([g_win])])
    (g_wout,), res = tn_matmul(mixin[None], dmix[None], "dwout", job=jobs)
    (far_r,), (sw_win,) = jobs.split(res)
    g_wout = g_wout.reshape(N_DEV, dm // N_DEV, dm)
    p_win, own_win = pair_sum(g_win, sw_win, "pair_win")
    jobs = _Jobs([chip_exchange_job([p_win]), swap_job([g_wout])])
    (far_win,), (sw_wout,) = jobs.split(run_job(jobs, "rs_tail_win"))
    p_wout, own_wout = pair_sum(g_wout, sw_wout, "pair_wout")
    (far_wout,) = run_job(chip_exchange_job([p_wout]), "rs_tail_wout")

    grads = {
        "ffn1_w_gate_up": jnp.concatenate([own_l, own_r], axis=1), "ffn1_w_down": own_wd1,
        "w_in": own_win, "w_out": own_wout, "ffn2_w_gate_up": own_wgu2, "ffn2_w_down": own_wd2,
    }
    others = {"ffn1_w_gate_up": jnp.concatenate([far_l, far_r], axis=2), "ffn1_w_down": far_wd1,
              "w_in": far_win, "w_out": far_wout, "ffn2_w_gate_up": far_wgu2, "ffn2_w_down": far_wd2}

    dmod_cols = lax.dynamic_slice(dmod_all.reshape(N_DEV * nseq, 9 * dm), (0, dev * ada_cols), (N_DEV * nseq, ada_cols))
    grads["w_ada"], gb_cols = ada_bwd(cond_all, dmod_cols, "ada_bwd")

    dsinks = jnp.sum(dsink_rows.reshape(nseq, N_Q_HEADS, ATTN_BLOCK), axis=(0, 2))
    small = jnp.zeros((8, dm), F32)
    small = small.at[0:2].set(dln1).at[2:4].set(dln2).at[4:6].set(dln3)
    small = small.at[6, 0:N_Q_HEADS].set(dsinks).at[7, 0].set(loss_part[0, 0])
    small_all, dconvw_all, gb_all = all_gather([small, dconvw, gb_cols], "gather_small")
    small_sum = sum_devices(small_all, "sum_small")
    dconvw_sum = sum_devices(dconvw_all, "sum_convw")
    loss = small_sum[7, 0]
    grads["b_ada"] = gb_all.reshape(1, N_DEV * ada_cols)
    grads["conv_w"] = lax.dynamic_slice(dconvw_sum, (0, dev * conv_cols), (CONV_TAPS, conv_cols))
    grads["attn_sinks"] = small_sum[6:7, 0:N_Q_HEADS]
    for i, nm in enumerate(["ln1_g", "ln1_b", "ln2_g", "ln2_b", "ln3_g", "ln3_b"]):
        grads[nm] = small_sum[i:i + 1]

    given = dict(w_ada=(w_ada, m_w_ada, v_w_ada), b_ada=(b_ada, m_b_ada, v_b_ada),
                 ffn1_w_gate_up=(ffn1_w_gate_up, m_ffn1_w_gate_up, v_ffn1_w_gate_up),
                 ffn1_w_down=(ffn1_w_down, m_ffn1_w_down, v_ffn1_w_down),
                 ln1_g=(ln1_g, m_ln1_g, v_ln1_g), ln1_b=(ln1_b, m_ln1_b, v_ln1_b),
                 w_in=(w_in, m_w_in, v_w_in), conv_w=(conv_w, m_conv_w, v_conv_w),
                 attn_sinks=(attn_sinks, m_attn_sinks, v_attn_sinks), w_out=(w_out, m_w_out, v_w_out),
                 ln2_g=(ln2_g, m_ln2_g, v_ln2_g), ln2_b=(ln2_b, m_ln2_b, v_ln2_b),
                 ffn2_w_gate_up=(ffn2_w_gate_up, m_ffn2_w_gate_up, v_ffn2_w_gate_up),
                 ffn2_w_down=(ffn2_w_down, m_ffn2_w_down, v_ffn2_w_down),
                 ln3_g=(ln3_g, m_ln3_g, v_ln3_g), ln3_b=(ln3_b, m_ln3_b, v_ln3_b))
    order = ["w_ada", "b_ada", "ffn1_w_gate_up", "ffn1_w_down", "ln1_g", "ln1_b", "w_in", "conv_w", "attn_sinks",
             "w_out", "ln2_g", "ln2_b", "ffn2_w_gate_up", "ffn2_w_down", "ln3_g", "ln3_b"]
    transposed = ("ffn1_w_gate_up", "ffn2_w_gate_up", "w_in")
    out_g, out_d, out_m, out_v = [], [], [], []
    for nm in order:
        shape = given[nm][0].shape
        two_d = (shape[-2], shape[-1])
        if nm in transposed:
            w2, m2, v2 = [t[0].T for t in given[nm]]
            back = lambda t: t.T[None]
        else:
            w2, m2, v2 = [t.reshape(two_d) for t in given[nm]]
            back = lambda t, shape=shape: t.reshape(shape)
        res = adamw(w2, grads[nm].reshape(w2.shape), m2, v2, "adamw_" + nm, others=others.get(nm))
        for lst, t in zip((out_g, out_d, out_m, out_v), res):
            lst.append(back(t))
    grad_x = dx0.reshape(nseq, seq, dm)
    return (loss, grad_x, *out_g, *out_d, *out_m, *out_v)
```

```python
import functools

import jax
import jax.numpy as jnp
from jax import lax
from jax.experimental import pallas as pl
from jax.experimental.pallas import tpu as pltpu

F32 = jnp.float32
BF16 = jnp.bfloat16
MESH = pl.DeviceIdType.MESH

N_DEV = 8
N_CHIP = 4
HEAD_DIM = 64
N_Q_HEADS = 8
N_KV_HEADS = 2
GQA_GROUP = N_Q_HEADS // N_KV_HEADS
ATTN_BLOCK = 128
ROT_DIM = 16
ROPE_THETA = 500000.0
CONV_TAPS = 3
LN_EPS = 1e-5
DN_ALPHA = 2.0 ** 0.25
ADAM_LR = 0.001
ADAM_B1 = 0.9
ADAM_B2 = 0.999
ADAM_EPS = 1e-08
ADAM_WD = 0.01
ADAM_STEP = 10
NEG_BIG = -1e30

VMEM_LIMIT = 56 * 1024 * 1024
TOKEN_TILE = 256
FFN_FWD_TILE = 512
TN_VMEM_BUDGET = 36 * 1024 * 1024


def _params(semantics=None, vmem=VMEM_LIMIT):
    return pltpu.CompilerParams(dimension_semantics=semantics, vmem_limit_bytes=vmem)


def _dot(a, b):
    return jnp.dot(a, b, preferred_element_type=F32)


def _dot_nt(a, b):
    return lax.dot_general(a, b, (((1,), (1,)), ((), ())), preferred_element_type=F32)


def _dot_tn(a, b):
    return lax.dot_general(a, b, (((0,), (0,)), ((), ())), preferred_element_type=F32)


def _sigmoid(x):
    return pl.reciprocal(1.0 + jnp.exp(-x), approx=True)


def _ln_stats(r):
    mu = jnp.mean(r, axis=-1, keepdims=True)
    d = r - mu
    var = jnp.mean(d * d, axis=-1, keepdims=True)
    rstd = lax.rsqrt(var + LN_EPS)
    return d * rstd, rstd


def _ln_bwd(dy, r, g):
    xhat, rstd = _ln_stats(r)
    dxhat = dy * g
    c1 = jnp.mean(dxhat, axis=-1, keepdims=True)
    c2 = jnp.mean(dxhat * xhat, axis=-1, keepdims=True)
    dr = rstd * (dxhat - c1 - xhat * c2)
    return dr, jnp.sum(dy * xhat, axis=0, keepdims=True), jnp.sum(dy, axis=0, keepdims=True)


def _const_spec(shape):
    nd = len(shape)
    return pl.BlockSpec(shape, lambda *_: (0,) * nd, pipeline_mode=pl.Buffered(1))


def all_gather(arrs, name):
    n = len(arrs)

    def body(*refs):
        ins, outs = refs[:n], refs[n:2 * n]
        send_sems, recv_sems, local_sems = refs[2 * n:]
        x, y, c = lax.axis_index("x"), lax.axis_index("y"), lax.axis_index("c")
        me, sibling = (x, y, c), (x, y, 1 - c)
        chips = [(1 - x, y), (x, 1 - y), (1 - x, 1 - y)]

        def slot(i, p):
            return outs[i].at[4 * p[0] + 2 * p[1] + p[2]]

        def copy(i, k, block, to, src=None):
            return pltpu.make_async_remote_copy(
                src_ref=slot(i, block) if src is None else src, dst_ref=slot(i, block),
                send_sem=send_sems.at[i, k], recv_sem=recv_sems.at[i, k],
                device_id=to, device_id_type=MESH)

        mine = [pltpu.make_async_copy(ins[i], slot(i, me), local_sems.at[i]) for i in range(n)]
        for cp in mine:
            cp.start()
        first = []
        for i in range(n):
            first.append(copy(i, 0, me, sibling, src=ins[i]))
            first += [copy(i, 1 + j, me, (*chip, c), src=ins[i]) for j, chip in enumerate(chips)]
        for cp in first:
            cp.start()
        passed = []
        for j, chip in enumerate(chips):
            for i in range(n):
                copy(i, 1 + j, (*chip, c), me).wait_recv()
                cp = copy(i, 4 + j, (*chip, c), sibling)
                cp.start()
                passed.append(cp)
        for i in range(n):
            copy(i, 0, sibling, me).wait_recv()
            for j, chip in enumerate(chips):
                copy(i, 4 + j, (*chip, 1 - c), me).wait_recv()
        for cp in first + passed:
            cp.wait_send()
        for cp in mine:
            cp.wait()

    any_spec = pl.BlockSpec(memory_space=pl.ANY)
    return pl.pallas_call(
        body, name=name,
        out_shape=[jax.ShapeDtypeStruct((N_DEV, *a.shape), a.dtype) for a in arrs],
        in_specs=[any_spec] * n, out_specs=[any_spec] * n,
        scratch_shapes=[pltpu.SemaphoreType.DMA((n, 7)), pltpu.SemaphoreType.DMA((n, 7)),
                        pltpu.SemaphoreType.DMA((n,))],
    )(*arrs)


RS_ROWS = 32


def reduce_scatter(g, name):
    _, rows, cols = g.shape
    nblk = rows // RS_ROWS
    assert nblk * RS_ROWS == rows

    def body(g_ref, out_ref, r1_ref, p_ref, r2_ref, send_sems, recv_sems):
        x, y, c = lax.axis_index("x"), lax.axis_index("y"), lax.axis_index("c")
        sibling = (x, y, 1 - c)
        q_me = 2 * x + y
        swaps = []
        for q in range(N_CHIP):
            cp = pltpu.make_async_remote_copy(
                src_ref=g_ref.at[2 * q + (1 - c)], dst_ref=r1_ref.at[q],
                send_sem=send_sems.at[q], recv_sem=recv_sems.at[q], device_id=sibling, device_id_type=MESH)
            cp.start()
            swaps.append(cp)
        for cp in swaps:
            cp.wait_recv()

        def pair_sum(i, carry):
            r = pl.ds(pl.multiple_of(i * RS_ROWS, RS_ROWS), RS_ROWS)
            for q in range(N_CHIP):
                p_ref[q, r, :] = (g_ref[2 * q + c, r, :].astype(F32) + r1_ref[q, r, :].astype(F32)).astype(BF16)
            return carry

        lax.fori_loop(0, nblk, pair_sum, 0)
        chips = [(1 - x, y), (x, 1 - y), (1 - x, 1 - y)]
        sends = []
        for k, chip in enumerate(chips):
            cp = pltpu.make_async_remote_copy(
                src_ref=p_ref.at[2 * chip[0] + chip[1]], dst_ref=r2_ref.at[k],
                send_sem=send_sems.at[N_CHIP + k], recv_sem=recv_sems.at[N_CHIP + k],
                device_id=(*chip, c), device_id_type=MESH)
            cp.start()
            sends.append(cp)
        for cp in sends:
            cp.wait_recv()

        def total(i, carry):
            r = pl.ds(pl.multiple_of(i * RS_ROWS, RS_ROWS), RS_ROWS)
            acc = g_ref[2 * q_me + c, r, :].astype(F32) + r1_ref[q_me, r, :].astype(F32)
            for k in range(3):
                acc = acc + r2_ref[k, r, :].astype(F32)
            out_ref[r, :] = acc
            return carry

        lax.fori_loop(0, nblk, total, 0)
        for cp in swaps + sends:
            cp.wait_send()

    vmem = pl.BlockSpec(memory_space=pltpu.VMEM)
    return pl.pallas_call(
        body, name=name,
        out_shape=jax.ShapeDtypeStruct((rows, cols), F32),
        in_specs=[vmem], out_specs=vmem,
        scratch_shapes=[pltpu.VMEM((N_CHIP, rows, cols), BF16), pltpu.VMEM((N_CHIP, rows, cols), BF16),
                        pltpu.VMEM((3, rows, cols), BF16),
                        pltpu.SemaphoreType.DMA((N_CHIP + 3,)), pltpu.SemaphoreType.DMA((N_CHIP + 3,))],
        compiler_params=_params(),
    )(g)


def _place():
    x, y, c = lax.axis_index("x"), lax.axis_index("y"), lax.axis_index("c")
    return x, y, c, [(1 - x, y), (x, 1 - y), (1 - x, 1 - y)]


def _slot(p):
    return 4 * p[0] + 2 * p[1] + p[2]


class _Job:
    def __init__(self, ins, outs, nsem, copies, aliases=None, local=None):
        self.ins, self.outs, self.nsem, self.copies = list(ins), list(outs), nsem, copies
        self.aliases = aliases or {}
        self.local = local

    def scratch(self):
        s = [pltpu.SemaphoreType.DMA(self.nsem), pltpu.SemaphoreType.DMA(self.nsem)]
        if self.local is not None:
            s.append(pltpu.SemaphoreType.DMA((len(self.ins),)))
        return s

    def start(self, ins, outs, sems):
        if self.local is not None:
            for cp in self.local(ins, outs, sems[2]):
                cp.start()
        for cp in self.copies(ins, outs, sems[0], sems[1])[0]:
            cp.start()

    def finish(self, ins, outs, sems):
        started, awaited = self.copies(ins, outs, sems[0], sems[1])
        for cp in awaited:
            cp.wait_recv()
        for cp in started:
            cp.wait_send()
        if self.local is not None:
            for cp in self.local(ins, outs, sems[2]):
                cp.wait()


class _Jobs:
    def __init__(self, jobs):
        self.jobs = jobs
        self.ins = [a for j in jobs for a in j.ins]
        self.outs = [o for j in jobs for o in j.outs]
        self.aliases = {}
        at_in = at_out = 0
        for j in jobs:
            self.aliases.update({at_in + i: at_out + o for i, o in j.aliases.items()})
            at_in, at_out = at_in + len(j.ins), at_out + len(j.outs)

    def scratch(self):
        return [s for j in self.jobs for s in j.scratch()]

    def _each(self, ins, outs, sems):
        at_in = at_out = at_sem = 0
        for j in self.jobs:
            n_in, n_out, n_sem = len(j.ins), len(j.outs), len(j.scratch())
            yield j, ins[at_in:at_in + n_in], outs[at_out:at_out + n_out], sems[at_sem:at_sem + n_sem]
            at_in, at_out, at_sem = at_in + n_in, at_out + n_out, at_sem + n_sem

    def start(self, ins, outs, sems):
        for j, i, o, s in self._each(ins, outs, sems):
            j.start(i, o, s)

    def finish(self, ins, outs, sems):
        for j, i, o, s in self._each(ins, outs, sems):
            j.finish(i, o, s)

    def split(self, results):
        at, parts = 0, []
        for j in self.jobs:
            parts.append(results[at:at + len(j.outs)])
            at += len(j.outs)
        return parts


def _remote(src, dst, send, recv, idx, to):
    return pltpu.make_async_remote_copy(src_ref=src, dst_ref=dst, send_sem=send.at[idx], recv_sem=recv.at[idx],
                                        device_id=to, device_id_type=MESH)


def _spread_copies(ins, outs, send, recv, base=0):
    x, y, c, chips = _place()
    me = (x, y, c)
    peers = [(x, y, 1 - c)] + [(*chip, c) for chip in chips]
    started, awaited = [], []
    for i, (src, dst) in enumerate(zip(ins, outs)):
        for k, peer in enumerate(peers):
            started.append(_remote(src, dst.at[_slot(me)], send, recv, (base + i, k), peer))
            awaited.append(_remote(src, dst.at[_slot(peer)], send, recv, (base + i, k), peer))
    return started, awaited


def _forward_copies(ins, outs, send, recv, base=0):
    x, y, c, chips = _place()
    started, awaited = [], []
    for i, buf in enumerate(outs):
        for j, chip in enumerate(chips):
            mine, theirs = buf.at[_slot((*chip, c))], buf.at[_slot((*chip, 1 - c))]
            started.append(_remote(mine, mine, send, recv, (base + i, j), (x, y, 1 - c)))
            awaited.append(_remote(theirs, theirs, send, recv, (base + i, j), (x, y, 1 - c)))
    return started, awaited


def _own_block_copies(ins, outs, sems):
    x, y, c, _ = _place()
    return [pltpu.make_async_copy(src, dst.at[_slot((x, y, c))], sems.at[i])
            for i, (src, dst) in enumerate(zip(ins, outs))]


def gather_spread_job(shards):
    outs = [jax.ShapeDtypeStruct((N_DEV, *a.shape), a.dtype) for a in shards]
    return _Job(shards, outs, (len(shards), 4), _spread_copies, local=_own_block_copies)


def gather_forward_job(fulls):
    outs = [jax.ShapeDtypeStruct(a.shape, a.dtype) for a in fulls]
    return _Job(fulls, outs, (len(fulls), 3), _forward_copies, aliases={i: i for i in range(len(fulls))})


def swap_job(gs):
    def copies(ins, outs, send, recv):
        x, y, c, _ = _place()
        started, awaited = [], []
        for i, (g, r1) in enumerate(zip(ins, outs)):
            for q in range(N_CHIP):
                started.append(_remote(g.at[2 * q + (1 - c)], r1.at[q], send, recv, (i, q), (x, y, 1 - c)))
                awaited.append(_remote(g.at[2 * q + c], r1.at[q], send, recv, (i, q), (x, y, 1 - c)))
        return started, awaited

    outs = [jax.ShapeDtypeStruct((N_CHIP, *g.shape[1:]), g.dtype) for g in gs]
    return _Job(gs, outs, (len(gs), N_CHIP), copies)


def chip_exchange_job(ps, rows=None, into=None):
    n = len(ps)

    def copies(ins, outs, send, recv):
        x, y, c, chips = _place()
        started, awaited = [], []
        for i, (p, r2) in enumerate(zip(ins[:n], outs)):
            for k, chip in enumerate(chips):
                src, mine, dst = p.at[2 * chip[0] + chip[1]], p.at[2 * x + y], r2.at[k]
                if rows is not None:
                    src, mine, dst = (t.at[pl.ds(rows[0], rows[1])] for t in (src, mine, dst))
                started.append(_remote(src, dst, send, recv, (i, k), (*chip, c)))
                awaited.append(_remote(mine, dst, send, recv, (i, k), (*chip, c)))
        return started, awaited

    outs = [jax.ShapeDtypeStruct((3, *p.shape[1:]), p.dtype) for p in ps]
    if into is None:
        return _Job(ps, outs, (n, 3), copies)
    return _Job(list(ps) + list(into), outs, (n, 3), copies, aliases={n + i: i for i in range(n)})


def _call(body, job, *, name, grid, in_specs, out_specs, out_shape, args, scratch_shapes=(), vmem=VMEM_LIMIT):
    if job is None:
        res = pl.pallas_call(
            body, name=name, grid=grid, in_specs=in_specs, out_specs=out_specs, out_shape=out_shape,
            scratch_shapes=list(scratch_shapes), compiler_params=_params(("arbitrary",) * len(grid), vmem),
        )(*args)
        return res, []
    n_in, n_out, n_scr = len(in_specs), len(out_specs), len(scratch_shapes)
    j_in, j_out = len(job.ins), len(job.outs)

    def with_copies(*refs):
        at = 0
        ins = refs[at:at + n_in]; at += n_in
        jins = refs[at:at + j_in]; at += j_in
        outs = refs[at:at + n_out]; at += n_out
        jouts = refs[at:at + j_out]; at += j_out
        scr = refs[at:at + n_scr]; at += n_scr
        sems = refs[at:]
        ids = [pl.program_id(d) for d in range(len(grid))]
        first = functools.reduce(jnp.logical_and, [i == 0 for i in ids])
        last = functools.reduce(jnp.logical_and, [i == n - 1 for i, n in zip(ids, grid)])

        @pl.when(first)
        def _():
            job.start(jins, jouts, sems)

        body(*ins, *outs, *scr)

        @pl.when(last)
        def _():
            job.finish(jins, jouts, sems)

    any_spec = pl.BlockSpec(memory_space=pl.ANY)
    res = pl.pallas_call(
        with_copies, name=name, grid=grid,
        in_specs=list(in_specs) + [any_spec] * j_in, out_specs=list(out_specs) + [any_spec] * j_out,
        out_shape=list(out_shape) + list(job.outs),
        input_output_aliases={n_in + i: n_out + o for i, o in job.aliases.items()},
        scratch_shapes=list(scratch_shapes) + job.scratch(),
        compiler_params=_params(("arbitrary",) * len(grid), vmem),
    )(*args, *job.ins)
    return res[:n_out], res[n_out:]


def run_job(job, name):
    def body(*refs):
        j_in, j_out = len(job.ins), len(job.outs)
        ins, outs, sems = refs[:j_in], refs[j_in:j_in + j_out], refs[j_in + j_out:]
        job.start(ins, outs, sems)
        job.finish(ins, outs, sems)

    any_spec = pl.BlockSpec(memory_space=pl.ANY)
    return pl.pallas_call(
        body, name=name, in_specs=[any_spec] * len(job.ins), out_specs=[any_spec] * len(job.outs),
        out_shape=list(job.outs), input_output_aliases=dict(job.aliases), scratch_shapes=job.scratch(),
    )(*job.ins)


def pair_sum(g, r1, name):
    _, rows, cols = g.shape
    rb = next(cand for cand in range(min(rows, 512), 0, -16) if rows % cand == 0)

    def body(g_ref, r1_ref, p_ref, own_ref):
        x, y, c, _ = _place()
        s = g_ref[c].astype(F32) + r1_ref[0].astype(F32)
        p_ref[0] = s.astype(BF16)

        @pl.when(pl.program_id(1) == 2 * x + y)
        def _():
            own_ref[...] = s

    return pl.pallas_call(
        body, name=name, grid=(rows // rb, N_CHIP),
        in_specs=[pl.BlockSpec((2, rb, cols), lambda i, q: (q, i, 0)), pl.BlockSpec((1, rb, cols), lambda i, q: (q, i, 0))],
        out_specs=[pl.BlockSpec((1, rb, cols), lambda i, q: (q, i, 0)), pl.BlockSpec((rb, cols), lambda i, q: (i, 0))],
        out_shape=[jax.ShapeDtypeStruct((N_CHIP, rows, cols), BF16), jax.ShapeDtypeStruct((rows, cols), F32)],
        compiler_params=_params(("arbitrary", "arbitrary")),
    )(g, r1)


def sum_devices(a, name):
    def body(a_ref, o_ref):
        acc = a_ref[0]
        for d in range(1, N_DEV):
            acc = acc + a_ref[d]
        o_ref[...] = acc

    return pl.pallas_call(body, name=name, out_shape=jax.ShapeDtypeStruct(a.shape[1:], F32))(a)


def adamw(w, g, m, v, name, others=None):
    rows, cols = w.shape
    rb = rows
    for cand in range(min(rows, 512), 7, -8):
        if rows % cand == 0 and cand % 8 == 0:
            rb = cand
            break

    def body(*refs):
        if others is None:
            w_ref, g_ref, m_ref, v_ref, d_ref, nm_ref, nv_ref = refs
            gg = g_ref[...]
        else:
            w_ref, g_ref, m_ref, v_ref, r2_ref, go_ref, d_ref, nm_ref, nv_ref = refs
            gg = g_ref[...]
            for k in range(3):
                gg = gg + r2_ref[k].astype(F32)
            go_ref[...] = gg
        nm = ADAM_B1 * m_ref[...] + (1.0 - ADAM_B1) * gg
        nv = ADAM_B2 * v_ref[...] + (1.0 - ADAM_B2) * (gg * gg)
        m_hat = nm / (1.0 - ADAM_B1 ** ADAM_STEP)
        v_hat = nv / (1.0 - ADAM_B2 ** ADAM_STEP)
        d_ref[...] = -ADAM_LR * (m_hat / (jnp.sqrt(v_hat) + ADAM_EPS) + ADAM_WD * w_ref[...])
        nm_ref[...] = nm
        nv_ref[...] = nv

    spec = pl.BlockSpec((rb, cols), lambda i: (i, 0))
    out = jax.ShapeDtypeStruct((rows, cols), F32)
    in_specs, args = [spec] * 4, [w, g, m, v]
    if others is not None:
        in_specs.append(pl.BlockSpec((3, rb, cols), lambda i: (0, i, 0)))
        args.append(others)
    n_out = 3 if others is None else 4
    res = pl.pallas_call(
        body, name=name, grid=(rows // rb,), in_specs=in_specs, out_specs=[spec] * n_out,
        out_shape=[out] * n_out, compiler_params=_params(("parallel",)),
    )(*args)
    return (g, *res) if others is None else tuple(res)


def ada_fwd(c_all, w_cols, b_cols, name):
    def body(c_ref, w_ref, b_ref, cond_ref, mod_ref):
        cc = c_ref[...]
        cond = (cc * _sigmoid(cc)).astype(BF16)
        cond_ref[...] = cond
        mod_ref[...] = _dot(cond, w_ref[...].astype(BF16)) + b_ref[...]

    n, cols = c_all.shape[0], w_cols.shape[1]
    return pl.pallas_call(
        body, name=name,
        out_shape=[jax.ShapeDtypeStruct(c_all.shape, BF16), jax.ShapeDtypeStruct((n, cols), F32)],
        compiler_params=_params(),
    )(c_all, w_cols, b_cols)


COND_ROWS = 8


def prologue(c_pad, conv_w, w_cols, b_cols, big_shards, name):
    nbig = len(big_shards)
    dm, cols = w_cols.shape

    def body(c_ref, cw_ref, wa_ref, b_ref, *refs):
        bigs, refs = refs[:nbig], refs[nbig:]
        call_ref, cwall_ref, cond_ref, modall_ref = refs[:4]
        big_outs, refs = refs[4:4 + nbig], refs[4 + nbig:]
        s_send, s_recv, f_send, f_recv, bs_send, bs_recv, bf_send, bf_recv, b_local = refs
        x, y, c, _ = _place()
        me = _slot((x, y, c))

        def two_level(ins, outs, base):
            started, awaited = _spread_copies(ins, outs, s_send, s_recv, base)
            for cp in started:
                cp.start()
            for cp in awaited:
                cp.wait_recv()
            fwd, arriving = _forward_copies(outs, outs, f_send, f_recv, base)
            for cp in fwd:
                cp.start()
            for cp in arriving:
                cp.wait_recv()
            for cp in started + fwd:
                cp.wait_send()

        big_local = _own_block_copies(bigs, big_outs, b_local)
        big_started, big_awaited = _spread_copies(bigs, big_outs, bs_send, bs_recv)
        for cp in big_local + big_started:
            cp.start()
        call_ref[me] = c_ref[...]
        cwall_ref[me] = cw_ref[...]
        two_level([call_ref.at[me], cwall_ref.at[me]], [call_ref, cwall_ref], 0)
        cc = call_ref[...].reshape(N_DEV * COND_ROWS, dm)
        cond = (cc * _sigmoid(cc)).astype(BF16)
        cond_ref[...] = cond
        modall_ref[me] = _dot(cond, wa_ref[...].astype(BF16)) + b_ref[...]
        two_level([modall_ref.at[me]], [modall_ref], 2)
        for cp in big_awaited:
            cp.wait_recv()
        big_fwd, big_arriving = _forward_copies(big_outs, big_outs, bf_send, bf_recv)
        for cp in big_fwd:
            cp.start()
        for cp in big_arriving:
            cp.wait_recv()
        for cp in big_started + big_fwd:
            cp.wait_send()
        for cp in big_local:
            cp.wait()

    vmem, any_spec = pl.BlockSpec(memory_space=pltpu.VMEM), pl.BlockSpec(memory_space=pl.ANY)
    return pl.pallas_call(
        body, name=name,
        in_specs=[vmem] * 4 + [any_spec] * nbig, out_specs=[vmem] * 4 + [any_spec] * nbig,
        out_shape=[jax.ShapeDtypeStruct((N_DEV, *c_pad.shape), F32), jax.ShapeDtypeStruct((N_DEV, *conv_w.shape), F32),
                   jax.ShapeDtypeStruct((N_DEV * COND_ROWS, dm), BF16),
                   jax.ShapeDtypeStruct((N_DEV, N_DEV * COND_ROWS, cols), F32)]
        + [jax.ShapeDtypeStruct((N_DEV, *a.shape), a.dtype) for a in big_shards],
        scratch_shapes=[pltpu.SemaphoreType.DMA((3, 4)), pltpu.SemaphoreType.DMA((3, 4)),
                        pltpu.SemaphoreType.DMA((3, 3)), pltpu.SemaphoreType.DMA((3, 3)),
                        pltpu.SemaphoreType.DMA((nbig, 4)), pltpu.SemaphoreType.DMA((nbig, 4)),
                        pltpu.SemaphoreType.DMA((nbig, 3)), pltpu.SemaphoreType.DMA((nbig, 3)),
                        pltpu.SemaphoreType.DMA((nbig,))],
        compiler_params=_params(),
    )(c_pad, conv_w, w_cols, b_cols, *big_shards)


def ada_bwd(cond_all, dmod_cols, name):
    def body(c_ref, d_ref, gw_ref, gb_ref):
        d = d_ref[...]
        gw_ref[...] = _dot_tn(c_ref[...], d.astype(BF16))
        gb_ref[...] = jnp.sum(d, axis=0, keepdims=True)

    dm, cols = cond_all.shape[1], dmod_cols.shape[1]
    return pl.pallas_call(
        body, name=name,
        out_shape=[jax.ShapeDtypeStruct((dm, cols), F32), jax.ShapeDtypeStruct((1, cols), F32)],
        compiler_params=_params(),
    )(cond_all, dmod_cols)


def _mod_spec(tiles_per_seq, dm):
    return pl.BlockSpec((1, 1, dm), lambda i: (i // tiles_per_seq, 0, 0))


def ffn_fwd(x, sh, sc, gt, wgu, wd, ln_g, ln_b, seq, name, target=None, job=None):
    tokens, dm = x.shape
    fc = wgu.shape[1]
    tm = min(FFN_FWD_TILE, seq)
    tiles_per_seq = seq // tm
    with_loss = target is not None

    def body(*refs):
        if with_loss:
            (x_ref, sh_ref, sc_ref, gt_ref, wgu_ref, wd_ref, lg_ref, lb_ref, t_ref,
             xo_ref, loss_ref, r_ref, gu_ref, f_ref) = refs
        else:
            (x_ref, sh_ref, sc_ref, gt_ref, wgu_ref, wd_ref, lg_ref, lb_ref,
             xo_ref, r_ref, gu_ref, f_ref) = refs
        xx = x_ref[...]
        h = (xx * (1.0 + sc_ref[0]) + sh_ref[0]).astype(BF16)
        acc = jnp.zeros((tm, dm), F32)
        for k in range(4):
            gk = _dot_nt(h, wgu_ref[k])
            uk = _dot_nt(h, wgu_ref[k + 4])
            gu_ref[k] = gk.astype(BF16)
            gu_ref[k + 4] = uk.astype(BF16)
            a = (gk * _sigmoid(gk) * uk).astype(BF16)
            acc = acc + _dot(a, wd_ref[k])
        f_ref[...] = acc.astype(BF16)
        r = DN_ALPHA * xx + (0.5 * (1.0 + gt_ref[0])) * acc
        r_ref[...] = r
        xhat, _ = _ln_stats(r)
        yy = xhat * lg_ref[...] + lb_ref[...]
        if with_loss:
            err = yy - t_ref[...]
            xo_ref[...] = err * (1.0 / dm)

            @pl.when(pl.program_id(0) == 0)
            def _():
                loss_ref[...] = jnp.zeros_like(loss_ref)

            loss_ref[...] += jnp.full((1, 128), (0.5 / dm) * jnp.sum(err * err), F32)
        else:
            xo_ref[...] = yy

    tile = pl.BlockSpec((tm, dm), lambda i: (i, 0))
    mod = _mod_spec(tiles_per_seq, dm)
    in_specs = [tile, mod, mod, mod, _const_spec(wgu.shape), _const_spec(wd.shape),
                _const_spec((1, dm)), _const_spec((1, dm))]
    args = [x, sh, sc, gt, wgu, wd, ln_g, ln_b]
    out_specs = [tile]
    out_shape = [jax.ShapeDtypeStruct((tokens, dm), F32)]
    if with_loss:
        in_specs.append(tile)
        args.append(target)
        out_specs.append(pl.BlockSpec((1, 128), lambda i: (0, 0)))
        out_shape.append(jax.ShapeDtypeStruct((1, 128), F32))
    out_specs += [tile, pl.BlockSpec((8, tm, fc), lambda i: (0, i, 0)), tile]
    out_shape += [jax.ShapeDtypeStruct((tokens, dm), F32), jax.ShapeDtypeStruct((8, tokens, fc), BF16),
                  jax.ShapeDtypeStruct((tokens, dm), BF16)]
    return _call(body, job, name=name, grid=(tokens // tm,), in_specs=in_specs, out_specs=out_specs,
                 out_shape=out_shape, args=args)


def ffn_bwd(dy, r, x, f, gu, sh, sc, gt, wgu, wd, ln_g, seq, name, job=None):
    tokens, dm = x.shape
    fc = wgu.shape[1]
    tm = min(TOKEN_TILE, seq)
    tiles_per_seq = seq // tm
    nseq = tokens // seq

    def body(dy_ref, r_ref, x_ref, f_ref, gu_ref, sh_ref, sc_ref, gt_ref, wgu_ref, wd_ref, lg_ref,
             dx_ref, dgu_ref, df_ref, a_ref, h_ref, dln_ref, dmod_ref):
        i = pl.program_id(0)
        dr, dgain, dbias = _ln_bwd(dy_ref[...], r_ref[...], lg_ref[...])

        @pl.when(i == 0)
        def _():
            dln_ref[...] = jnp.zeros_like(dln_ref)

        @pl.when(i % tiles_per_seq == 0)
        def _():
            dmod_ref[...] = jnp.zeros_like(dmod_ref)

        dln_ref[0:1, :] += dgain
        dln_ref[1:2, :] += dbias
        df32 = (0.5 * (1.0 + gt_ref[0])) * dr
        df = df32.astype(BF16)
        df_ref[...] = df
        dgate = jnp.sum(dr * (0.5 * f_ref[...].astype(F32)), axis=0, keepdims=True)
        xx = x_ref[...]
        one_sc = 1.0 + sc_ref[0]
        h = (xx * one_sc + sh_ref[0]).astype(BF16)
        h_ref[...] = h
        dh = jnp.zeros((tm, dm), F32)
        for k in range(4):
            da = _dot_nt(df, wd_ref[k])
            gk = gu_ref[k].astype(F32)
            uk = gu_ref[k + 4].astype(F32)
            sg = _sigmoid(gk)
            sil = gk * sg
            a_ref[k] = (sil * uk).astype(BF16)
            du = (da * sil).astype(BF16)
            dg = (da * uk * (sg * (1.0 + gk * (1.0 - sg)))).astype(BF16)
            dgu_ref[k] = dg
            dgu_ref[k + 4] = du
            dh = dh + _dot(dg, wgu_ref[k]) + _dot(du, wgu_ref[k + 4])
        dx_ref[...] = DN_ALPHA * dr + dh * one_sc
        dmod_ref[0, 0:1, :] += jnp.sum(dh, axis=0, keepdims=True)
        dmod_ref[0, 1:2, :] += jnp.sum(dh * xx, axis=0, keepdims=True)
        dmod_ref[0, 2:3, :] += dgate

    tile = pl.BlockSpec((tm, dm), lambda i: (i, 0))
    mod = _mod_spec(tiles_per_seq, dm)
    gu_spec = pl.BlockSpec((8, tm, fc), lambda i: (0, i, 0))
    return _call(
        body, job, name=name, grid=(tokens // tm,),
        in_specs=[tile, tile, tile, tile, gu_spec, mod, mod, mod, _const_spec(wgu.shape), _const_spec(wd.shape),
                  _const_spec((1, dm))],
        out_specs=[tile, gu_spec, tile, pl.BlockSpec((4, tm, fc), lambda i: (0, i, 0)), tile,
                   pl.BlockSpec((2, dm), lambda i: (0, 0)),
                   pl.BlockSpec((1, 3, dm), lambda i: (i // tiles_per_seq, 0, 0))],
        out_shape=[jax.ShapeDtypeStruct((tokens, dm), F32), jax.ShapeDtypeStruct((8, tokens, fc), BF16),
                   jax.ShapeDtypeStruct((tokens, dm), BF16), jax.ShapeDtypeStruct((4, tokens, fc), BF16),
                   jax.ShapeDtypeStruct((tokens, dm), BF16), jax.ShapeDtypeStruct((2, dm), F32),
                   jax.ShapeDtypeStruct((nseq, 3, dm), F32)],
        args=(dy, r, x, f, gu, sh, sc, gt, wgu, wd, ln_g))


def tn_matmul(a, b, name, job=None, b_cols=None):
    na, tokens, kk = a.shape
    nb, _, cc = b.shape
    col = 0
    if b_cols is not None:
        col, cc = b_cols
    tt = tokens
    while 4 * tt * (kk + cc) + 8 * kk * cc > TN_VMEM_BUDGET and tt % 2 == 0 and tt > 256:
        tt //= 2
    steps = tokens // tt

    def body(a_ref, b_ref, o_ref, *acc):
        if steps == 1:
            o_ref[0, 0] = _dot_tn(a_ref[0], b_ref[0]).astype(BF16)
            return
        acc_ref, = acc
        t = pl.program_id(2)

        @pl.when(t == 0)
        def _():
            acc_ref[...] = jnp.zeros_like(acc_ref)

        acc_ref[...] += _dot_tn(a_ref[0], b_ref[0])

        @pl.when(t == steps - 1)
        def _():
            o_ref[0, 0] = acc_ref[...].astype(BF16)

    return _call(
        body, job, name=name, grid=(na, nb, steps),
        in_specs=[pl.BlockSpec((1, tt, kk), lambda i, j, t: (i, t, 0)),
                  pl.BlockSpec((1, tt, cc), lambda i, j, t: (j, t, col))],
        out_specs=[pl.BlockSpec((1, 1, kk, cc), lambda i, j, t: (i, j, 0, 0))],
        out_shape=[jax.ShapeDtypeStruct((na, nb, kk, cc), BF16)],
        scratch_shapes=[] if steps == 1 else [pltpu.VMEM((kk, cc), F32)], args=(a, b))


def proj_fwd(x1, sh, sc, w_in, seq, name, job=None):
    tokens, dm = x1.shape
    tm = min(TOKEN_TILE, seq)
    tiles_per_seq = seq // tm
    widths = [N_Q_HEADS * HEAD_DIM, N_KV_HEADS * HEAD_DIM, N_KV_HEADS * HEAD_DIM, 512, 512, 512]
    assert sum(widths) == w_in.shape[0]

    def body(x_ref, sh_ref, sc_ref, w_ref, *outs):
        h = (x_ref[...] * (1.0 + sc_ref[0]) + sh_ref[0]).astype(BF16)
        proj = _dot_nt(h, w_ref[...])
        at = 0
        for o_ref, wdt in zip(outs, widths):
            o_ref[...] = proj[:, at:at + wdt]
            at += wdt

    tile = pl.BlockSpec((tm, dm), lambda i: (i, 0))
    mod = _mod_spec(tiles_per_seq, dm)
    return _call(
        body, job, name=name, grid=(tokens // tm,),
        in_specs=[tile, mod, mod, _const_spec(w_in.shape)],
        out_specs=[pl.BlockSpec((tm, wdt), lambda i: (i, 0)) for wdt in widths],
        out_shape=[jax.ShapeDtypeStruct((tokens, wdt), F32) for wdt in widths],
        args=(x1, sh, sc, w_in))


LANES = 2 * HEAD_DIM


def _head_lane(shape):
    return lax.broadcasted_iota(jnp.int32, shape, 1) % HEAD_DIM


def _lane_half(shape):
    return lax.broadcasted_iota(jnp.int32, shape, 1) // HEAD_DIM


def _swap_rot(v):
    lane = _head_lane(v.shape)
    half = ROT_DIM // 2
    return jnp.where(lane < half, pltpu.roll(v, LANES - half, 1),
                     jnp.where(lane < ROT_DIM, pltpu.roll(v, half, 1), 0.0))


def _rope(v, cos_t, sin_t):
    return v * cos_t + _swap_rot(v) * sin_t


def _unrope(dv, cos_t, sin_t):
    return dv * cos_t + _swap_rot(dv * sin_t)


def _both_halves(t, g):
    return jnp.where(_lane_half(t.shape) == g, t, pltpu.roll(t, HEAD_DIM, 1))


def _fold_halves(t, g):
    return jnp.where(_lane_half(t.shape) == g, t + pltpu.roll(t, HEAD_DIM, 1), 0.0)


def _stack_heads(blocks):
    rows = []
    for blk in blocks:
        half = _lane_half(blk.shape)
        rows += [jnp.where(half == 0, blk, 0.0), jnp.where(half == 1, blk, 0.0)]
    return jnp.concatenate(rows, axis=0)


def _unstack_heads(t, j):
    lo = t[(2 * j) * ATTN_BLOCK:(2 * j + 1) * ATTN_BLOCK]
    hi = t[(2 * j + 1) * ATTN_BLOCK:(2 * j + 2) * ATTN_BLOCK]
    return jnp.where(_lane_half(lo.shape) == 0, lo, hi)


def _band_mask(q0, w0):
    rows, cols = GQA_GROUP * ATTN_BLOCK, 2 * ATTN_BLOCK
    qi = lax.broadcasted_iota(jnp.int32, (rows, cols), 0) % ATTN_BLOCK + q0
    ki = lax.broadcasted_iota(jnp.int32, (rows, cols), 1) + w0
    diff = qi - ki
    return (diff >= 0) & (diff < ATTN_BLOCK)


def _attn_specs(seq):
    q_spec = pl.BlockSpec((seq, GQA_GROUP * HEAD_DIM), lambda b, g: (b, g))
    kv_spec = pl.BlockSpec((seq, LANES), lambda b, g: (b, 0))
    sink_spec = pl.BlockSpec((1, GQA_GROUP * ATTN_BLOCK, 1), lambda b, g: (g, 0, 0))
    return q_spec, kv_spec, sink_spec


def _block_starts(n):
    q0 = pl.multiple_of(n * ATTN_BLOCK, ATTN_BLOCK)
    w0 = pl.multiple_of(jnp.maximum(n - 1, 0) * ATTN_BLOCK, ATTN_BLOCK)
    return q0, w0


def _stacked_queries(ref, rows, cos_b=None, sin_b=None):
    blocks = []
    for j in range(2):
        blk = ref[rows, j * LANES:(j + 1) * LANES]
        blocks.append(blk if cos_b is None else _rope(blk, cos_b, sin_b))
    return _stack_heads(blocks).astype(BF16)


def _sink_columns(sinks):
    return jnp.repeat(sinks.reshape(N_KV_HEADS, GQA_GROUP), ATTN_BLOCK, axis=1)[:, :, None]


def _probs_spec(nblk):
    return pl.BlockSpec((1, 1, nblk, GQA_GROUP * ATTN_BLOCK, 2 * ATTN_BLOCK), lambda b, g: (b, g, 0, 0, 0))


def _sink_probs_spec():
    return pl.BlockSpec((1, 1, GQA_GROUP * ATTN_BLOCK, LANES), lambda b, g: (b, g, 0, 0))


def attn_fwd(q, k, v, cos_t, sin_t, sinks, seq, name, job=None):
    tokens = q.shape[0]
    nblk = seq // ATTN_BLOCK
    assert nblk >= 2
    scale = HEAD_DIM ** -0.5

    nseq = tokens // seq
    rows_stacked = GQA_GROUP * ATTN_BLOCK
    assert nblk <= LANES

    def body(q_ref, k_ref, v_ref, cos_ref, sin_ref, sink_ref, o_ref, qr_ref, p_ref, ps_ref, kd_ref, vd_ref):
        g = pl.program_id(1)
        kd_ref[...] = _both_halves(_rope(k_ref[...], cos_ref[...], sin_ref[...]), g).astype(BF16)
        vd_ref[...] = _both_halves(v_ref[...], g).astype(BF16)
        sink = sink_ref[0]
        lane = lax.broadcasted_iota(jnp.int32, (rows_stacked, LANES), 1)

        ps_ref[...] = jnp.zeros_like(ps_ref)

        def block(n, carry):
            q0, w0 = _block_starts(n)
            rows, win = pl.ds(q0, ATTN_BLOCK), pl.ds(w0, 2 * ATTN_BLOCK)
            blocks = []
            for j in range(2):
                qr = _rope(q_ref[rows, j * LANES:(j + 1) * LANES], cos_ref[rows, :], sin_ref[rows, :]).astype(BF16)
                qr_ref[rows, j * LANES:(j + 1) * LANES] = qr
                blocks.append(qr)
            qs = _stack_heads(blocks)
            s = _dot_nt(qs, kd_ref[win, :]) * scale
            s = jnp.where(_band_mask(q0, w0), s, NEG_BIG)
            m = jnp.maximum(jnp.max(s, axis=-1, keepdims=True), sink)
            p = jnp.exp(s - m)
            e_sink = jnp.exp(sink - m)
            inv = pl.reciprocal(jnp.sum(p, axis=-1, keepdims=True) + e_sink, approx=True)
            pn = (p * inv).astype(BF16)
            p_ref[0, 0, n] = pn
            out = _dot(pn, vd_ref[win, :])
            for j in range(2):
                o_ref[rows, j * LANES:(j + 1) * LANES] = _unstack_heads(out, j).astype(o_ref.dtype)
            ps_ref[0, 0] = jnp.where(lane == n, e_sink * inv, ps_ref[0, 0])
            return carry

        lax.fori_loop(0, nblk, block, 0, unroll=2)

    q_spec, kv_spec, sink_spec = _attn_specs(seq)
    return _call(
        body, job, name=name, grid=(nseq, N_KV_HEADS),
        in_specs=[q_spec, kv_spec, kv_spec, kv_spec, kv_spec, sink_spec],
        out_specs=[q_spec, q_spec, _probs_spec(nblk), _sink_probs_spec()],
        out_shape=[jax.ShapeDtypeStruct(q.shape, BF16), jax.ShapeDtypeStruct(q.shape, BF16),
                   jax.ShapeDtypeStruct((nseq, N_KV_HEADS, nblk, rows_stacked, 2 * ATTN_BLOCK), BF16),
                   jax.ShapeDtypeStruct((nseq, N_KV_HEADS, rows_stacked, LANES), F32)],
        scratch_shapes=[pltpu.VMEM((seq, LANES), BF16), pltpu.VMEM((seq, LANES), BF16)],
        args=(q, k, v, cos_t, sin_t, _sink_columns(sinks)))


def attn_bwd(qr, k, v, do, probs, sink_probs, cos_t, sin_t, seq, name, job=None):
    tokens = qr.shape[0]
    nseq = tokens // seq
    nblk = seq // ATTN_BLOCK
    assert nblk >= 2
    rows_stacked = GQA_GROUP * ATTN_BLOCK
    scale = HEAD_DIM ** -0.5

    def body(q_ref, k_ref, v_ref, do_ref, p_ref, ps_ref, cos_ref, sin_ref, dq_ref, dk_ref, dv_ref, ds_ref,
             kd_ref, vd_ref, dkd_ref, dvd_ref, acc_ref):
        g = pl.program_id(1)
        kd_ref[...] = _both_halves(_rope(k_ref[...], cos_ref[...], sin_ref[...]), g).astype(BF16)
        vd_ref[...] = _both_halves(v_ref[...], g).astype(BF16)
        dkd_ref[...] = jnp.zeros_like(dkd_ref)
        dvd_ref[...] = jnp.zeros_like(dvd_ref)
        acc_ref[...] = jnp.zeros_like(acc_ref)
        lane = lax.broadcasted_iota(jnp.int32, (rows_stacked, LANES), 1)

        def block(n, carry):
            q0, w0 = _block_starts(n)
            rows, win = pl.ds(q0, ATTN_BLOCK), pl.ds(w0, 2 * ATTN_BLOCK)
            qs = _stacked_queries(q_ref, rows)
            dos = _stacked_queries(do_ref, rows)
            kw, vw = kd_ref[win, :], vd_ref[win, :]
            pn16 = p_ref[0, 0, n]
            pn = pn16.astype(F32)
            dvd_ref[win, :] += _dot_tn(pn16, dos)
            dp = _dot_nt(dos, vw)
            delta = jnp.sum(dp * pn, axis=-1, keepdims=True)
            ds = (pn * (dp - delta)).astype(BF16)
            dqs = _dot(ds, kw) * scale
            dkd_ref[win, :] += _dot_tn(ds, qs) * scale
            cos_b, sin_b = cos_ref[rows, :], sin_ref[rows, :]
            for j in range(2):
                dq_ref[rows, j * LANES:(j + 1) * LANES] = _unrope(_unstack_heads(dqs, j), cos_b, sin_b).astype(BF16)
            acc_ref[...] += jnp.where(lane == n, ps_ref[0, 0] * delta, 0.0)
            return carry

        lax.fori_loop(0, nblk // 2, lambda i, carry: block(2 * i + 1, block(2 * i, carry)), 0)
        ds_ref[0, 0] = -jnp.sum(acc_ref[...], axis=-1, keepdims=True)
        dk_g = _unrope(_fold_halves(dkd_ref[...], g), cos_ref[...], sin_ref[...])
        dv_g = _fold_halves(dvd_ref[...], g)

        @pl.when(g == 0)
        def _():
            dk_ref[...] = dk_g
            dv_ref[...] = dv_g

        @pl.when(g != 0)
        def _():
            dk_ref[...] += dk_g
            dv_ref[...] += dv_g

    q_spec, kv_spec, _ = _attn_specs(seq)
    return _call(
        body, job, name=name, grid=(nseq, N_KV_HEADS),
        in_specs=[q_spec, kv_spec, kv_spec, q_spec, _probs_spec(nblk), _sink_probs_spec(), kv_spec, kv_spec],
        out_specs=[q_spec, kv_spec, kv_spec, pl.BlockSpec((1, 1, rows_stacked, 1), lambda b, g: (b, g, 0, 0))],
        out_shape=[jax.ShapeDtypeStruct(qr.shape, BF16), jax.ShapeDtypeStruct(k.shape, F32),
                   jax.ShapeDtypeStruct(k.shape, F32), jax.ShapeDtypeStruct((nseq, N_KV_HEADS, rows_stacked, 1), F32)],
        scratch_shapes=[pltpu.VMEM((seq, LANES), BF16), pltpu.VMEM((seq, LANES), BF16),
                        pltpu.VMEM((seq, LANES), F32), pltpu.VMEM((seq, LANES), F32),
                        pltpu.VMEM((rows_stacked, LANES), F32)],
        args=(qr, k, v, do, probs, sink_probs, cos_t, sin_t))


CONV_COLS = 128


def _shift_down(z, by):
    t = lax.broadcasted_iota(jnp.int32, z.shape, 0)
    return jnp.where(t >= by, pltpu.roll(z, by, 0), 0.0)


def _shift_up(z, by):
    n = z.shape[0]
    t = lax.broadcasted_iota(jnp.int32, z.shape, 0)
    return jnp.where(t < n - by, pltpu.roll(z, n - by, 0), 0.0)


def conv_fwd(u, bg, cg, conv_w, seq, name):
    tokens, width = u.shape

    def body(u_ref, bg_ref, cg_ref, w_ref, o_ref):
        z = cg_ref[...] * u_ref[...]
        yy = w_ref[2:3, :] * z + w_ref[1:2, :] * _shift_down(z, 1) + w_ref[0:1, :] * _shift_down(z, 2)
        o_ref[...] = (bg_ref[...] * yy).astype(BF16)

    col = pl.BlockSpec((seq, CONV_COLS), lambda j, b: (b, j))
    return pl.pallas_call(
        body, name=name, grid=(width // CONV_COLS, tokens // seq),
        in_specs=[col, col, col, pl.BlockSpec((CONV_TAPS, CONV_COLS), lambda j, b: (0, j))],
        out_specs=col, out_shape=jax.ShapeDtypeStruct((tokens, width), BF16),
        compiler_params=_params(("parallel", "parallel")),
    )(u, bg, cg, conv_w)


def conv_bwd(dout, u, bg, cg, conv_w, seq, name):
    tokens, width = u.shape

    def body(do_ref, u_ref, bg_ref, cg_ref, w_ref, du_ref, dbg_ref, dcg_ref, dw_ref):
        uu, cg_v, do = u_ref[...], cg_ref[...], do_ref[...].astype(F32)
        z = cg_v * uu
        z1, z2 = _shift_down(z, 1), _shift_down(z, 2)
        yy = w_ref[2:3, :] * z + w_ref[1:2, :] * z1 + w_ref[0:1, :] * z2
        dbg_ref[...] = (do * yy).astype(BF16)
        dyy = do * bg_ref[...]
        dz = w_ref[2:3, :] * dyy + w_ref[1:2, :] * _shift_up(dyy, 1) + w_ref[0:1, :] * _shift_up(dyy, 2)
        du_ref[...] = (dz * cg_v).astype(BF16)
        dcg_ref[...] = (dz * uu).astype(BF16)

        @pl.when(pl.program_id(1) == 0)
        def _():
            dw_ref[...] = jnp.zeros_like(dw_ref)

        dw_ref[0:1, :] += jnp.sum(dyy * z2, axis=0, keepdims=True)
        dw_ref[1:2, :] += jnp.sum(dyy * z1, axis=0, keepdims=True)
        dw_ref[2:3, :] += jnp.sum(dyy * z, axis=0, keepdims=True)

    col = pl.BlockSpec((seq, CONV_COLS), lambda j, b: (b, j))
    w_spec = pl.BlockSpec((CONV_TAPS, CONV_COLS), lambda j, b: (0, j))
    act = jax.ShapeDtypeStruct((tokens, width), BF16)
    return pl.pallas_call(
        body, name=name, grid=(width // CONV_COLS, tokens // seq),
        in_specs=[col, col, col, col, w_spec], out_specs=[col, col, col, w_spec],
        out_shape=[act, act, act, jax.ShapeDtypeStruct((CONV_TAPS, width), F32)],
        compiler_params=_params(("parallel", "arbitrary")),
    )(dout, u, bg, cg, conv_w)


def out_fwd(x1, attn, conv, gt, w_out, ln_g, ln_b, seq, name, job=None):
    tokens, dm = x1.shape
    half = attn.shape[1]
    tm = min(TOKEN_TILE, seq)
    tiles_per_seq = seq // tm

    def body(x_ref, a_ref, c_ref, gt_ref, w_ref, lg_ref, lb_ref, xo_ref, r_ref, mi_ref, mix_ref):
        mixin = jnp.concatenate([a_ref[...], c_ref[...]], axis=1).astype(BF16)
        mi_ref[...] = mixin
        mix = _dot(mixin, w_ref[...])
        mix_ref[...] = mix.astype(BF16)
        r = DN_ALPHA * x_ref[...] + (1.0 + gt_ref[0]) * mix
        r_ref[...] = r
        xhat, _ = _ln_stats(r)
        xo_ref[...] = xhat * lg_ref[...] + lb_ref[...]

    tile = pl.BlockSpec((tm, dm), lambda i: (i, 0))
    htile = pl.BlockSpec((tm, half), lambda i: (i, 0))
    return _call(
        body, job, name=name, grid=(tokens // tm,),
        in_specs=[tile, htile, htile, _mod_spec(tiles_per_seq, dm), _const_spec(w_out.shape),
                  _const_spec((1, dm)), _const_spec((1, dm))],
        out_specs=[tile, tile, tile, tile],
        out_shape=[jax.ShapeDtypeStruct((tokens, dm), F32), jax.ShapeDtypeStruct((tokens, dm), F32),
                   jax.ShapeDtypeStruct((tokens, dm), BF16), jax.ShapeDtypeStruct((tokens, dm), BF16)],
        args=(x1, attn, conv, gt, w_out, ln_g, ln_b))


def out_bwd(dy, r, mix, gt, w_out, ln_g, seq, name, job=None):
    tokens, dm = r.shape
    half = dm // 2
    tm = min(TOKEN_TILE, seq)
    tiles_per_seq = seq // tm
    nseq = tokens // seq

    def body(dy_ref, r_ref, mix_ref, gt_ref, w_ref, lg_ref, dres_ref, da_ref, dc_ref, dmix_ref, dln_ref, dgt_ref):
        i = pl.program_id(0)
        dr, dgain, dbias = _ln_bwd(dy_ref[...], r_ref[...], lg_ref[...])

        @pl.when(i == 0)
        def _():
            dln_ref[...] = jnp.zeros_like(dln_ref)

        @pl.when(i % tiles_per_seq == 0)
        def _():
            dgt_ref[...] = jnp.zeros_like(dgt_ref)

        dln_ref[0:1, :] += dgain
        dln_ref[1:2, :] += dbias
        dgt_ref[0] += jnp.sum(dr * mix_ref[...].astype(F32), axis=0, keepdims=True)
        dres_ref[...] = DN_ALPHA * dr
        dmix = ((1.0 + gt_ref[0]) * dr).astype(BF16)
        dmix_ref[...] = dmix
        dmixin = _dot_nt(dmix, w_ref[...])
        da_ref[...] = dmixin[:, :half].astype(BF16)
        dc_ref[...] = dmixin[:, half:].astype(BF16)

    tile = pl.BlockSpec((tm, dm), lambda i: (i, 0))
    htile = pl.BlockSpec((tm, half), lambda i: (i, 0))
    return _call(
        body, job, name=name, grid=(tokens // tm,),
        in_specs=[tile, tile, tile, _mod_spec(tiles_per_seq, dm), _const_spec(w_out.shape), _const_spec((1, dm))],
        out_specs=[tile, htile, htile, tile, pl.BlockSpec((2, dm), lambda i: (0, 0)),
                   pl.BlockSpec((1, 1, dm), lambda i: (i // tiles_per_seq, 0, 0))],
        out_shape=[jax.ShapeDtypeStruct((tokens, dm), F32), jax.ShapeDtypeStruct((tokens, half), BF16),
                   jax.ShapeDtypeStruct((tokens, half), BF16), jax.ShapeDtypeStruct((tokens, dm), BF16),
                   jax.ShapeDtypeStruct((2, dm), F32), jax.ShapeDtypeStruct((nseq, 1, dm), F32)],
        args=(dy, r, mix, gt, w_out, ln_g))


def proj_bwd(parts, dres, x1, sh, sc, w_in, seq, name, job=None):
    tokens, dm = x1.shape
    tm = min(TOKEN_TILE, seq)
    tiles_per_seq = seq // tm
    nseq = tokens // seq
    widths = [p.shape[1] for p in parts]
    total = sum(widths)

    def body(*refs):
        part_refs = refs[:6]
        dres_ref, x_ref, sh_ref, sc_ref, w_ref, dx_ref, dproj_ref, h_ref, dmod_ref = refs[6:]
        dproj = jnp.concatenate([p[...].astype(BF16) for p in part_refs], axis=1)
        dproj_ref[...] = dproj
        dh = _dot(dproj, w_ref[...])
        xx = x_ref[...]
        one_sc = 1.0 + sc_ref[0]
        h_ref[...] = (xx * one_sc + sh_ref[0]).astype(BF16)
        dx_ref[...] = dres_ref[...] + dh * one_sc

        @pl.when(pl.program_id(0) % tiles_per_seq == 0)
        def _():
            dmod_ref[...] = jnp.zeros_like(dmod_ref)

        dmod_ref[0, 0:1, :] += jnp.sum(dh, axis=0, keepdims=True)
        dmod_ref[0, 1:2, :] += jnp.sum(dh * xx, axis=0, keepdims=True)

    tile = pl.BlockSpec((tm, dm), lambda i: (i, 0))
    mod = _mod_spec(tiles_per_seq, dm)
    return _call(
        body, job, name=name, grid=(tokens // tm,),
        in_specs=[pl.BlockSpec((tm, wdt), lambda i: (i, 0)) for wdt in widths]
        + [tile, tile, mod, mod, _const_spec(w_in.shape)],
        out_specs=[tile, pl.BlockSpec((tm, total), lambda i: (i, 0)), tile,
                   pl.BlockSpec((1, 2, dm), lambda i: (i // tiles_per_seq, 0, 0))],
        out_shape=[jax.ShapeDtypeStruct((tokens, dm), F32), jax.ShapeDtypeStruct((tokens, total), BF16),
                   jax.ShapeDtypeStruct((tokens, dm), BF16), jax.ShapeDtypeStruct((nseq, 2, dm), F32)],
        args=(*parts, dres, x1, sh, sc, w_in))


def _rope_tables(positions):
    half = ROT_DIM // 2
    inv_freq = jnp.power(jnp.float32(ROPE_THETA), -jnp.arange(0, ROT_DIM, 2, dtype=F32) / ROT_DIM)
    lane = jnp.arange(LANES) % HEAD_DIM
    freq = jnp.where(lane < ROT_DIM, inv_freq[lane % half], 0.0)
    sign = jnp.where(lane < half, -1.0, 1.0).astype(F32)
    ang = positions.astype(F32)[:, None] * freq[None, :]
    return jnp.cos(ang), sign[None, :] * jnp.sin(ang)


def kernel(x, c, positions, w_ada, b_ada, ffn1_w_gate_up, ffn1_w_down, ln1_g, ln1_b, w_in, conv_w, attn_sinks, w_out, ln2_g, ln2_b, ffn2_w_gate_up, ffn2_w_down, ln3_g, ln3_b, loss_target, m_w_ada, m_b_ada, m_ffn1_w_gate_up, m_ffn1_w_down, m_ln1_g, m_ln1_b, m_w_in, m_conv_w, m_attn_sinks, m_w_out, m_ln2_g, m_ln2_b, m_ffn2_w_gate_up, m_ffn2_w_down, m_ln3_g, m_ln3_b, v_w_ada, v_b_ada, v_ffn1_w_gate_up, v_ffn1_w_down, v_ln1_g, v_ln1_b, v_w_in, v_conv_w, v_attn_sinks, v_w_out, v_ln2_g, v_ln2_b, v_ffn2_w_gate_up, v_ffn2_w_down, v_ln3_g, v_ln3_b):
    nseq, seq, dm = x.shape
    tokens = nseq * seq
    dev = 4 * lax.axis_index("x") + 2 * lax.axis_index("y") + lax.axis_index("c")
    ada_cols = w_ada.shape[2]
    ff = ffn1_w_down.shape[1] * N_DEV
    fc = ff // 4
    in_cols = w_in.shape[2]
    conv_cols = conv_w.shape[2]

    def t_bf16(w):
        return w[0].T.astype(BF16)

    c_pad = jnp.pad(c, ((0, COND_ROWS - nseq), (0, 0)))
    b_cols = lax.dynamic_slice(b_ada, (0, dev * ada_cols), (1, ada_cols))
    _, convw_all, cond_all, mod_all, wgu1, wd1 = prologue(
        c_pad, conv_w[0], w_ada[0], b_cols, [t_bf16(ffn1_w_gate_up), ffn1_w_down[0].astype(BF16)], "prologue")
    convw_full = convw_all.transpose(1, 0, 2).reshape(CONV_TAPS, N_DEV * conv_cols)
    wd1 = wd1.reshape(4, fc, dm)
    cond_all = cond_all.reshape(N_DEV, COND_ROWS, dm)[:, :nseq].reshape(N_DEV * nseq, dm)
    mod = lax.dynamic_slice(mod_all, (0, dev * COND_ROWS, 0), (N_DEV, nseq, ada_cols))
    mod = mod.transpose(1, 0, 2).reshape(nseq, 9, 1, dm)
    sh1, sc1, g1, sh2, sc2, g2, sh3, sc3, g3 = [mod[:, i] for i in range(9)]

    x0 = x.reshape(tokens, dm)
    spread = gather_spread_job([t_bf16(w_in), w_out[0].astype(BF16), ffn2_w_down[0].astype(BF16)])
    (x1, r1, gu1, f1), spread = ffn_fwd(x0, sh1, sc1, g1, wgu1, wd1, ln1_g, ln1_b, seq, "ffn1_fwd", job=spread)
    win, wout = run_job(gather_forward_job(spread[:2]), "gather_mix_forward")
    win = win.reshape(N_DEV * in_cols, dm)
    wout = wout.reshape(dm, dm)
    (q, k, v, u, bg, cg), (wd2,) = proj_fwd(x1, sh2, sc2, win, seq, "proj_fwd", job=gather_forward_job(spread[2:]))
    wd2 = wd2.reshape(4, fc, dm)
    cos_t, sin_t = _rope_tables(positions.reshape(tokens))
    sinks = attn_sinks[0]
    (attn, q_rot, probs, sink_probs), spread = attn_fwd(q, k, v, cos_t, sin_t, sinks, seq, "attn_fwd",
                                                        job=gather_spread_job([t_bf16(ffn2_w_gate_up)]))
    conv = conv_fwd(u, bg, cg, convw_full, seq, "conv_fwd")
    (x2, r2, mixin, mix), (wgu2,) = out_fwd(x1, attn, conv, g2, wout, ln2_g, ln2_b, seq, "out_fwd",
                                            job=gather_forward_job(spread))
    target = loss_target.reshape(tokens, dm)
    (dy3, loss_part, r3, gu3, f3), _ = ffn_fwd(x2, sh3, sc3, g3, wgu2, wd2, ln3_g, ln3_b, seq, "ffn2_fwd", target=target)

    (dx2, dgu3, df3, a3, h3, dln3, dmod3), _ = ffn_bwd(dy3, r3, x2, f3, gu3, sh3, sc3, g3, wgu2, wd2, ln3_g, seq, "ffn2_bwd")
    g_wd2 = tn_matmul(a3, df3[None], "ffn2_dwd")[0][0].reshape(N_DEV, ff // N_DEV, dm)
    g_wgu2 = tn_matmul(dgu3, h3[None], "ffn2_dwgu")[0][0].reshape(N_DEV, fc, dm)
    (dres2, dattn, dconv, dmix, dln2, dg2), swapped = out_bwd(dx2, r2, mix, g2, wout, ln2_g, seq, "out_bwd",
                                                              job=swap_job([g_wgu2, g_wd2]))
    p_wgu2, own_wgu2 = pair_sum(g_wgu2, swapped[0], "pair_wgu2")
    p_wd2, own_wd2 = pair_sum(g_wd2, swapped[1], "pair_wd2")
    du, dbg, dcg, dconvw = conv_bwd(dconv, u, bg, cg, convw_full, seq, "conv_bwd")
    (dq, dk, dv, dsink_rows), (far_wd2,) = attn_bwd(
        q_rot, k, v, dattn, probs, sink_probs, cos_t, sin_t, seq, "attn_bwd", job=chip_exchange_job([p_wd2]))
    parts = [dq, dk, dv, du, dbg, dcg]
    (dx1, dproj, h2, dmod2), far_top = proj_bwd(parts, dres2, x1, sh2, sc2, win, seq, "proj_bwd",
                                                job=chip_exchange_job([p_wgu2], rows=(0, fc // 2)))
    (dx0, dgu1, df1, a1, h1, dln1, dmod1), _ = ffn_bwd(
        dx1, r1, x0, f1, gu1, sh1, sc1, g1, wgu1, wd1, ln1_g, seq, "ffn1_bwd")

    dmod = jnp.concatenate([dmod1, dmod2, dg2, dmod3], axis=1).reshape(nseq, 9 * dm)
    half = dm // 2
    jobs = _Jobs([gather_spread_job([dmod]),
                  chip_exchange_job([p_wgu2], rows=(fc // 2, fc // 2), into=far_top)])
    (g_wd1,), res = tn_matmul(a1, df1[None], "ffn1_dwd", job=jobs)
    dmod_spread, (far_wgu2,) = jobs.split(res)
    g_wd1 = g_wd1.reshape(N_DEV, ff // N_DEV, dm)
    jobs = _Jobs([swap_job([g_wd1]), gather_forward_job(dmod_spread)])
    (g_l,), res = tn_matmul(dgu1, h1[None], "ffn1_dwgu_l", job=jobs, b_cols=(0, half))
    (sw_wd1,), (dmod_all,) = jobs.split(res)
    g_l = g_l.reshape(N_DEV, fc, half)
    p_wd1, own_wd1 = pair_sum(g_wd1, sw_wd1, "pair_wd1")
    jobs = _Jobs([chip_exchange_job([p_wd1]), swap_job([g_l])])
    (g_r,), res = tn_matmul(dgu1, h1[None], "ffn1_dwgu_r", job=jobs, b_cols=(1, half))
    (far_wd1,), (sw_l,) = jobs.split(res)
    g_r = g_r.reshape(N_DEV, fc, half)
    p_l, own_l = pair_sum(g_l, sw_l, "pair_wgu1_l")
    jobs = _Jobs([chip_exchange_job([p_l]), swap_job([g_r])])
    (g_win,), res = tn_matmul(dproj[None], h2[None], "dwin", job=jobs)
    (far_l,), (sw_r,) = jobs.split(res)
    g_win = g_win.reshape(N_DEV, in_cols, dm)
    p_r, own_r = pair_sum(g_r, sw_r, "pair_wgu1_r")
    jobs = _Jobs([chip_exchange_job([p_r]), swap_job([g_win])])
    (g_wout,), res = tn_matmul(mixin[None], dmix[None], "dwout", job=jobs)
    (far_r,), (sw_win,) = jobs.split(res)
    g_wout = g_wout.reshape(N_DEV, dm // N_DEV, dm)
    p_win, own_win = pair_sum(g_win, sw_win, "pair_win")
    jobs = _Jobs([chip_exchange_job([p_win]), swap_job([g_wout])])
    (far_win,), (sw_wout,) = jobs.split(run_job(jobs, "rs_tail_win"))
    p_wout, own_wout = pair_sum(g_wout, sw_wout, "pair_wout")
    (far_wout,) = run_job(chip_exchange_job([p_wout]), "rs_tail_wout")

    grads = {
        "ffn1_w_gate_up": jnp.concatenate([own_l, own_r], axis=1), "ffn1_w_down": own_wd1,
        "w_in": own_win, "w_out": own_wout, "ffn2_w_gate_up": own_wgu2, "ffn2_w_down": own_wd2,
    }
    others = {"ffn1_w_gate_up": jnp.concatenate([far_l, far_r], axis=2), "ffn1_w_down": far_wd1,
              "w_in": far_win, "w_out": far_wout, "ffn2_w_gate_up": far_wgu2, "ffn2_w_down": far_wd2}

    dmod_cols = lax.dynamic_slice(dmod_all.reshape(N_DEV * nseq, 9 * dm), (0, dev * ada_cols), (N_DEV * nseq, ada_cols))
    grads["w_ada"], gb_cols = ada_bwd(cond_all, dmod_cols, "ada_bwd")

    dsinks = jnp.sum(dsink_rows.reshape(nseq, N_Q_HEADS, ATTN_BLOCK), axis=(0, 2))
    small = jnp.zeros((8, dm), F32)
    small = small.at[0:2].set(dln1).at[2:4].set(dln2).at[4:6].set(dln3)
    small = small.at[6, 0:N_Q_HEADS].set(dsinks).at[7, 0].set(loss_part[0, 0])
    small_all, dconvw_all, gb_all = all_gather([small, dconvw, gb_cols], "gather_small")
    small_sum = sum_devices(small_all, "sum_small")
    dconvw_sum = sum_devices(dconvw_all, "sum_convw")
    loss = small_sum[7, 0]
    grads["b_ada"] = gb_all.reshape(1, N_DEV * ada_cols)
    grads["conv_w"] = lax.dynamic_slice(dconvw_sum, (0, dev * conv_cols), (CONV_TAPS, conv_cols))
    grads["attn_sinks"] = small_sum[6:7, 0:N_Q_HEADS]
    for i, nm in enumerate(["ln1_g", "ln1_b", "ln2_g", "ln2_b", "ln3_g", "ln3_b"]):
        grads[nm] = small_sum[i:i + 1]

    given = dict(w_ada=(w_ada, m_w_ada, v_w_ada), b_ada=(b_ada, m_b_ada, v_b_ada),
                 ffn1_w_gate_up=(ffn1_w_gate_up, m_ffn1_w_gate_up, v_ffn1_w_gate_up),
                 ffn1_w_down=(ffn1_w_down, m_ffn1_w_down, v_ffn1_w_down),
                 ln1_g=(ln1_g, m_ln1_g, v_ln1_g), ln1_b=(ln1_b, m_ln1_b, v_ln1_b),
                 w_in=(w_in, m_w_in, v_w_in), conv_w=(conv_w, m_conv_w, v_conv_w),
                 attn_sinks=(attn_sinks, m_attn_sinks, v_attn_sinks), w_out=(w_out, m_w_out, v_w_out),
                 ln2_g=(ln2_g, m_ln2_g, v_ln2_g), ln2_b=(ln2_b, m_ln2_b, v_ln2_b),
                 ffn2_w_gate_up=(ffn2_w_gate_up, m_ffn2_w_gate_up, v_ffn2_w_gate_up),
                 ffn2_w_down=(ffn2_w_down, m_ffn2_w_down, v_ffn2_w_down),
                 ln3_g=(ln3_g, m_ln3_g, v_ln3_g), ln3_b=(ln3_b, m_ln3_b, v_ln3_b))
    order = ["w_ada", "b_ada", "ffn1_w_gate_up", "ffn1_w_down", "ln1_g", "ln1_b", "w_in", "conv_w", "attn_sinks",
             "w_out", "ln2_g", "ln2_b", "ffn2_w_gate_up", "ffn2_w_down", "ln3_g", "ln3_b"]
    transposed = ("ffn1_w_gate_up", "ffn2_w_gate_up", "w_in")
    out_g, out_d, out_m, out_v = [], [], [], []
    for nm in order:
        shape = given[nm][0].shape
        two_d = (shape[-2], shape[-1])
        if nm in transposed:
            w2, m2, v2 = [t[0].T for t in given[nm]]
            back = lambda t: t.T[None]
        else:
            w2, m2, v2 = [t.reshape(two_d) for t in given[nm]]
            back = lambda t, shape=shape: t.reshape(shape)
        res = adamw(w2, grads[nm].reshape(w2.shape), m2, v2, "adamw_" + nm, others=others.get(nm))
        for lst, t in zip((out_g, out_d, out_m, out_v), res):
            lst.append(back(t))
    grad_x = dx0.reshape(nseq, seq, dm)
    return (loss, grad_x, *out_g, *out_d, *out_m, *out_v)
```

```python
import functools

import jax
import jax.numpy as jnp
from jax import lax
from jax.experimental import pallas as pl
from jax.experimental.pallas import tpu as pltpu

F32 = jnp.float32
BF16 = jnp.bfloat16
MESH = pl.DeviceIdType.MESH

N_DEV = 8
N_CHIP = 4
HEAD_DIM = 64
N_Q_HEADS = 8
N_KV_HEADS = 2
GQA_GROUP = N_Q_HEADS // N_KV_HEADS
ATTN_BLOCK = 128
ROT_DIM = 16
ROPE_THETA = 500000.0
CONV_TAPS = 3
LN_EPS = 1e-5
DN_ALPHA = 2.0 ** 0.25
ADAM_LR = 0.001
ADAM_B1 = 0.9
ADAM_B2 = 0.999
ADAM_EPS = 1e-08
ADAM_WD = 0.01
ADAM_STEP = 10
NEG_BIG = -1e30

VMEM_LIMIT = 56 * 1024 * 1024
TOKEN_TILE = 256
FFN_FWD_TILE = 512
MIX_TILE = 512
TN_VMEM_BUDGET = 36 * 1024 * 1024


def _params(semantics=None, vmem=VMEM_LIMIT):
    return pltpu.CompilerParams(dimension_semantics=semantics, vmem_limit_bytes=vmem)


def _dot(a, b):
    return jnp.dot(a, b, preferred_element_type=F32)


def _dot_nt(a, b):
    return lax.dot_general(a, b, (((1,), (1,)), ((), ())), preferred_element_type=F32)


def _dot_tn(a, b):
    return lax.dot_general(a, b, (((0,), (0,)), ((), ())), preferred_element_type=F32)


def _sigmoid(x):
    return pl.reciprocal(1.0 + jnp.exp(-x), approx=True)


def _ln_stats(r):
    mu = jnp.mean(r, axis=-1, keepdims=True)
    d = r - mu
    var = jnp.mean(d * d, axis=-1, keepdims=True)
    rstd = lax.rsqrt(var + LN_EPS)
    return d * rstd, rstd


def _ln_bwd(dy, r, g):
    xhat, rstd = _ln_stats(r)
    dxhat = dy * g
    c1 = jnp.mean(dxhat, axis=-1, keepdims=True)
    c2 = jnp.mean(dxhat * xhat, axis=-1, keepdims=True)
    dr = rstd * (dxhat - c1 - xhat * c2)
    return dr, jnp.sum(dy * xhat, axis=0, keepdims=True), jnp.sum(dy, axis=0, keepdims=True)


def _const_spec(shape):
    nd = len(shape)
    return pl.BlockSpec(shape, lambda *_: (0,) * nd, pipeline_mode=pl.Buffered(1))


def all_gather(arrs, name):
    n = len(arrs)

    def body(*refs):
        ins, outs = refs[:n], refs[n:2 * n]
        send_sems, recv_sems, local_sems = refs[2 * n:]
        x, y, c = lax.axis_index("x"), lax.axis_index("y"), lax.axis_index("c")
        me, sibling = (x, y, c), (x, y, 1 - c)
        chips = [(1 - x, y), (x, 1 - y), (1 - x, 1 - y)]

        def slot(i, p):
            return outs[i].at[4 * p[0] + 2 * p[1] + p[2]]

        def copy(i, k, block, to, src=None):
            return pltpu.make_async_remote_copy(
                src_ref=slot(i, block) if src is None else src, dst_ref=slot(i, block),
                send_sem=send_sems.at[i, k], recv_sem=recv_sems.at[i, k],
                device_id=to, device_id_type=MESH)

        mine = [pltpu.make_async_copy(ins[i], slot(i, me), local_sems.at[i]) for i in range(n)]
        for cp in mine:
            cp.start()
        first = []
        for i in range(n):
            first.append(copy(i, 0, me, sibling, src=ins[i]))
            first += [copy(i, 1 + j, me, (*chip, c), src=ins[i]) for j, chip in enumerate(chips)]
        for cp in first:
            cp.start()
        passed = []
        for i in range(n):
            for j, chip in enumerate(chips):
                copy(i, 1 + j, (*chip, c), me).wait_recv()
                cp = copy(i, 4 + j, (*chip, c), sibling)
                cp.start()
                passed.append(cp)
        for i in range(n):
            copy(i, 0, sibling, me).wait_recv()
            for j, chip in enumerate(chips):
                copy(i, 4 + j, (*chip, 1 - c), me).wait_recv()
        for cp in first + passed:
            cp.wait_send()
        for cp in mine:
            cp.wait()

    any_spec = pl.BlockSpec(memory_space=pl.ANY)
    return pl.pallas_call(
        body, name=name,
        out_shape=[jax.ShapeDtypeStruct((N_DEV, *a.shape), a.dtype) for a in arrs],
        in_specs=[any_spec] * n, out_specs=[any_spec] * n,
        scratch_shapes=[pltpu.SemaphoreType.DMA((n, 7)), pltpu.SemaphoreType.DMA((n, 7)),
                        pltpu.SemaphoreType.DMA((n,))],
    )(*arrs)


def _place():
    x, y, c = lax.axis_index("x"), lax.axis_index("y"), lax.axis_index("c")
    return x, y, c, [(1 - x, y), (x, 1 - y), (1 - x, 1 - y)]


def _slot(p):
    return 4 * p[0] + 2 * p[1] + p[2]


class _Job:
    def __init__(self, ins, outs, nsem, copies, aliases=None, local=None):
        self.ins, self.outs, self.nsem, self.copies = list(ins), list(outs), nsem, copies
        self.aliases = aliases or {}
        self.local = local

    def scratch(self):
        s = [pltpu.SemaphoreType.DMA(self.nsem), pltpu.SemaphoreType.DMA(self.nsem)]
        if self.local is not None:
            s.append(pltpu.SemaphoreType.DMA((len(self.ins),)))
        return s

    def start(self, ins, outs, sems):
        if self.local is not None:
            for cp in self.local(ins, outs, sems[2]):
                cp.start()
        for cp in self.copies(ins, outs, sems[0], sems[1])[0]:
            cp.start()

    def finish(self, ins, outs, sems):
        started, awaited = self.copies(ins, outs, sems[0], sems[1])
        for cp in awaited:
            cp.wait_recv()
        for cp in started:
            cp.wait_send()
        if self.local is not None:
            for cp in self.local(ins, outs, sems[2]):
                cp.wait()


class _Jobs:
    def __init__(self, jobs):
        self.jobs = jobs
        self.ins = [a for j in jobs for a in j.ins]
        self.outs = [o for j in jobs for o in j.outs]
        self.aliases = {}
        at_in = at_out = 0
        for j in jobs:
            self.aliases.update({at_in + i: at_out + o for i, o in j.aliases.items()})
            at_in, at_out = at_in + len(j.ins), at_out + len(j.outs)

    def scratch(self):
        return [s for j in self.jobs for s in j.scratch()]

    def _each(self, ins, outs, sems):
        at_in = at_out = at_sem = 0
        for j in self.jobs:
            n_in, n_out, n_sem = len(j.ins), len(j.outs), len(j.scratch())
            yield j, ins[at_in:at_in + n_in], outs[at_out:at_out + n_out], sems[at_sem:at_sem + n_sem]
            at_in, at_out, at_sem = at_in + n_in, at_out + n_out, at_sem + n_sem

    def start(self, ins, outs, sems):
        for j, i, o, s in self._each(ins, outs, sems):
            j.start(i, o, s)

    def finish(self, ins, outs, sems):
        for j, i, o, s in self._each(ins, outs, sems):
            j.finish(i, o, s)

    def split(self, results):
        at, parts = 0, []
        for j in self.jobs:
            parts.append(results[at:at + len(j.outs)])
            at += len(j.outs)
        return parts


def _remote(src, dst, send, recv, idx, to):
    return pltpu.make_async_remote_copy(src_ref=src, dst_ref=dst, send_sem=send.at[idx], recv_sem=recv.at[idx],
                                        device_id=to, device_id_type=MESH)


def _spread_copies(ins, outs, send, recv, base=0):
    x, y, c, chips = _place()
    me = (x, y, c)
    peers = [(x, y, 1 - c)] + [(*chip, c) for chip in chips]
    started, awaited = [], []
    for i, (src, dst) in enumerate(zip(ins, outs)):
        for k, peer in enumerate(peers):
            started.append(_remote(src, dst.at[_slot(me)], send, recv, (base + i, k), peer))
            awaited.append(_remote(src, dst.at[_slot(peer)], send, recv, (base + i, k), peer))
    return started, awaited


def _forward_copies(ins, outs, send, recv, base=0):
    x, y, c, chips = _place()
    started, awaited = [], []
    for i, buf in enumerate(outs):
        for j, chip in enumerate(chips):
            mine, theirs = buf.at[_slot((*chip, c))], buf.at[_slot((*chip, 1 - c))]
            started.append(_remote(mine, mine, send, recv, (base + i, j), (x, y, 1 - c)))
            awaited.append(_remote(theirs, theirs, send, recv, (base + i, j), (x, y, 1 - c)))
    return started, awaited


def _own_block_copies(ins, outs, sems):
    x, y, c, _ = _place()
    return [pltpu.make_async_copy(src, dst.at[_slot((x, y, c))], sems.at[i])
            for i, (src, dst) in enumerate(zip(ins, outs))]


def gather_spread_job(shards):
    outs = [jax.ShapeDtypeStruct((N_DEV, *a.shape), a.dtype) for a in shards]
    return _Job(shards, outs, (len(shards), 4), _spread_copies, local=_own_block_copies)


def gather_forward_job(fulls):
    outs = [jax.ShapeDtypeStruct(a.shape, a.dtype) for a in fulls]
    return _Job(fulls, outs, (len(fulls), 3), _forward_copies, aliases={i: i for i in range(len(fulls))})


def swap_job(gs):
    def copies(ins, outs, send, recv):
        x, y, c, _ = _place()
        started, awaited = [], []
        for i, (g, r1) in enumerate(zip(ins, outs)):
            for q in range(N_CHIP):
                started.append(_remote(g.at[2 * q + (1 - c)], r1.at[q], send, recv, (i, q), (x, y, 1 - c)))
                awaited.append(_remote(g.at[2 * q + c], r1.at[q], send, recv, (i, q), (x, y, 1 - c)))
        return started, awaited

    outs = [jax.ShapeDtypeStruct((N_CHIP, *g.shape[1:]), g.dtype) for g in gs]
    return _Job(gs, outs, (len(gs), N_CHIP), copies)


def chip_exchange_job(ps, rows=None, into=None):
    n = len(ps)

    def copies(ins, outs, send, recv):
        x, y, c, chips = _place()
        started, awaited = [], []
        for i, (p, r2) in enumerate(zip(ins[:n], outs)):
            for k, chip in enumerate(chips):
                src, mine, dst = p.at[2 * chip[0] + chip[1]], p.at[2 * x + y], r2.at[k]
                if rows is not None:
                    src, mine, dst = (t.at[pl.ds(rows[0], rows[1])] for t in (src, mine, dst))
                started.append(_remote(src, dst, send, recv, (i, k), (*chip, c)))
                awaited.append(_remote(mine, dst, send, recv, (i, k), (*chip, c)))
        return started, awaited

    outs = [jax.ShapeDtypeStruct((3, *p.shape[1:]), p.dtype) for p in ps]
    if into is None:
        return _Job(ps, outs, (n, 3), copies)
    return _Job(list(ps) + list(into), outs, (n, 3), copies, aliases={n + i: i for i in range(n)})


def _call(body, job, *, name, grid, in_specs, out_specs, out_shape, args, scratch_shapes=(), vmem=VMEM_LIMIT):
    if job is None:
        res = pl.pallas_call(
            body, name=name, grid=grid, in_specs=in_specs, out_specs=out_specs, out_shape=out_shape,
            scratch_shapes=list(scratch_shapes), compiler_params=_params(("arbitrary",) * len(grid), vmem),
        )(*args)
        return res, []
    n_in, n_out, n_scr = len(in_specs), len(out_specs), len(scratch_shapes)
    j_in, j_out = len(job.ins), len(job.outs)

    def with_copies(*refs):
        at = 0
        ins = refs[at:at + n_in]; at += n_in
        jins = refs[at:at + j_in]; at += j_in
        outs = refs[at:at + n_out]; at += n_out
        jouts = refs[at:at + j_out]; at += j_out
        scr = refs[at:at + n_scr]; at += n_scr
        sems = refs[at:]
        ids = [pl.program_id(d) for d in range(len(grid))]
        first = functools.reduce(jnp.logical_and, [i == 0 for i in ids])
        last = functools.reduce(jnp.logical_and, [i == n - 1 for i, n in zip(ids, grid)])

        @pl.when(first)
        def _():
            job.start(jins, jouts, sems)

        body(*ins, *outs, *scr)

        @pl.when(last)
        def _():
            job.finish(jins, jouts, sems)

    any_spec = pl.BlockSpec(memory_space=pl.ANY)
    res = pl.pallas_call(
        with_copies, name=name, grid=grid,
        in_specs=list(in_specs) + [any_spec] * j_in, out_specs=list(out_specs) + [any_spec] * j_out,
        out_shape=list(out_shape) + list(job.outs),
        input_output_aliases={n_in + i: n_out + o for i, o in job.aliases.items()},
        scratch_shapes=list(scratch_shapes) + job.scratch(),
        compiler_params=_params(("arbitrary",) * len(grid), vmem),
    )(*args, *job.ins)
    return res[:n_out], res[n_out:]


def run_job(job, name):
    def body(*refs):
        j_in, j_out = len(job.ins), len(job.outs)
        ins, outs, sems = refs[:j_in], refs[j_in:j_in + j_out], refs[j_in + j_out:]
        job.start(ins, outs, sems)
        job.finish(ins, outs, sems)

    any_spec = pl.BlockSpec(memory_space=pl.ANY)
    return pl.pallas_call(
        body, name=name, in_specs=[any_spec] * len(job.ins), out_specs=[any_spec] * len(job.outs),
        out_shape=list(job.outs), input_output_aliases=dict(job.aliases), scratch_shapes=job.scratch(),
    )(*job.ins)


def pair_sum(g, r1, name):
    _, rows, cols = g.shape
    rb = next(cand for cand in range(min(rows, 512), 0, -16) if rows % cand == 0)

    def body(g_ref, r1_ref, p_ref, own_ref):
        x, y, c, _ = _place()
        s = g_ref[c].astype(F32) + r1_ref[0].astype(F32)
        p_ref[0] = s.astype(BF16)

        @pl.when(pl.program_id(1) == 2 * x + y)
        def _():
            own_ref[...] = s

    return pl.pallas_call(
        body, name=name, grid=(rows // rb, N_CHIP),
        in_specs=[pl.BlockSpec((2, rb, cols), lambda i, q: (q, i, 0)), pl.BlockSpec((1, rb, cols), lambda i, q: (q, i, 0))],
        out_specs=[pl.BlockSpec((1, rb, cols), lambda i, q: (q, i, 0)), pl.BlockSpec((rb, cols), lambda i, q: (i, 0))],
        out_shape=[jax.ShapeDtypeStruct((N_CHIP, rows, cols), BF16), jax.ShapeDtypeStruct((rows, cols), F32)],
        compiler_params=_params(("arbitrary", "arbitrary")),
    )(g, r1)


def sum_devices(a, name):
    def body(a_ref, o_ref):
        acc = a_ref[0]
        for d in range(1, N_DEV):
            acc = acc + a_ref[d]
        o_ref[...] = acc

    return pl.pallas_call(body, name=name, out_shape=jax.ShapeDtypeStruct(a.shape[1:], F32))(a)


def adamw(w, g, m, v, name, others=None):
    rows, cols = w.shape
    rb = rows
    for cand in range(min(rows, 512), 7, -8):
        if rows % cand == 0 and cand % 8 == 0:
            rb = cand
            break

    def body(*refs):
        if others is None:
            w_ref, g_ref, m_ref, v_ref, d_ref, nm_ref, nv_ref = refs
            gg = g_ref[...]
        else:
            w_ref, g_ref, m_ref, v_ref, r2_ref, go_ref, d_ref, nm_ref, nv_ref = refs
            gg = g_ref[...]
            for k in range(3):
                gg = gg + r2_ref[k].astype(F32)
            go_ref[...] = gg
        nm = ADAM_B1 * m_ref[...] + (1.0 - ADAM_B1) * gg
        nv = ADAM_B2 * v_ref[...] + (1.0 - ADAM_B2) * (gg * gg)
        m_hat = nm / (1.0 - ADAM_B1 ** ADAM_STEP)
        v_hat = nv / (1.0 - ADAM_B2 ** ADAM_STEP)
        d_ref[...] = -ADAM_LR * (m_hat / (jnp.sqrt(v_hat) + ADAM_EPS) + ADAM_WD * w_ref[...])
        nm_ref[...] = nm
        nv_ref[...] = nv

    spec = pl.BlockSpec((rb, cols), lambda i: (i, 0))
    out = jax.ShapeDtypeStruct((rows, cols), F32)
    in_specs, args = [spec] * 4, [w, g, m, v]
    if others is not None:
        in_specs.append(pl.BlockSpec((3, rb, cols), lambda i: (0, i, 0)))
        args.append(others)
    n_out = 3 if others is None else 4
    res = pl.pallas_call(
        body, name=name, grid=(rows // rb,), in_specs=in_specs, out_specs=[spec] * n_out,
        out_shape=[out] * n_out, compiler_params=_params(("parallel",)),
    )(*args)
    return (g, *res) if others is None else tuple(res)


def ada_fwd(c_all, w_cols, b_cols, name):
    def body(c_ref, w_ref, b_ref, cond_ref, mod_ref):
        cc = c_ref[...]
        cond = (cc * _sigmoid(cc)).astype(BF16)
        cond_ref[...] = cond
        mod_ref[...] = _dot(cond, w_ref[...].astype(BF16)) + b_ref[...]

    n, cols = c_all.shape[0], w_cols.shape[1]
    return pl.pallas_call(
        body, name=name,
        out_shape=[jax.ShapeDtypeStruct(c_all.shape, BF16), jax.ShapeDtypeStruct((n, cols), F32)],
        compiler_params=_params(),
    )(c_all, w_cols, b_cols)


def ada_bwd(cond_all, dmod_cols, name):
    def body(c_ref, d_ref, gw_ref, gb_ref):
        d = d_ref[...]
        gw_ref[...] = _dot_tn(c_ref[...], d.astype(BF16))
        gb_ref[...] = jnp.sum(d, axis=0, keepdims=True)

    dm, cols = cond_all.shape[1], dmod_cols.shape[1]
    return pl.pallas_call(
        body, name=name,
        out_shape=[jax.ShapeDtypeStruct((dm, cols), F32), jax.ShapeDtypeStruct((1, cols), F32)],
        compiler_params=_params(),
    )(cond_all, dmod_cols)


def _mod_spec(tiles_per_seq, dm):
    return pl.BlockSpec((1, 1, dm), lambda i: (i // tiles_per_seq, 0, 0))


def ffn_fwd(x, sh, sc, gt, wgu, wd, ln_g, ln_b, seq, name, target=None, job=None):
    tokens, dm = x.shape
    fc = wgu.shape[1]
    tm = min(FFN_FWD_TILE, seq)
    tiles_per_seq = seq // tm
    with_loss = target is not None

    def body(*refs):
        if with_loss:
            (x_ref, sh_ref, sc_ref, gt_ref, wgu_ref, wd_ref, lg_ref, lb_ref, t_ref,
             xo_ref, loss_ref, r_ref, gu_ref, f_ref) = refs
        else:
            (x_ref, sh_ref, sc_ref, gt_ref, wgu_ref, wd_ref, lg_ref, lb_ref,
             xo_ref, r_ref, gu_ref, f_ref) = refs
        xx = x_ref[...]
        h = (xx * (1.0 + sc_ref[0]) + sh_ref[0]).astype(BF16)
        acc = jnp.zeros((tm, dm), F32)
        for k in range(4):
            gk = _dot_nt(h, wgu_ref[k])
            uk = _dot_nt(h, wgu_ref[k + 4])
            gu_ref[k] = gk.astype(BF16)
            gu_ref[k + 4] = uk.astype(BF16)
            a = (gk * _sigmoid(gk) * uk).astype(BF16)
            acc = acc + _dot(a, wd_ref[k])
        f_ref[...] = acc.astype(BF16)
        r = DN_ALPHA * xx + (0.5 * (1.0 + gt_ref[0])) * acc
        r_ref[...] = r
        xhat, _ = _ln_stats(r)
        yy = xhat * lg_ref[...] + lb_ref[...]
        if with_loss:
            err = yy - t_ref[...]
            xo_ref[...] = err * (1.0 / dm)

            @pl.when(pl.program_id(0) == 0)
            def _():
                loss_ref[...] = jnp.zeros_like(loss_ref)

            loss_ref[...] += jnp.full((1, 128), (0.5 / dm) * jnp.sum(err * err), F32)
        else:
            xo_ref[...] = yy

    tile = pl.BlockSpec((tm, dm), lambda i: (i, 0))
    mod = _mod_spec(tiles_per_seq, dm)
    in_specs = [tile, mod, mod, mod, _const_spec(wgu.shape), _const_spec(wd.shape),
                _const_spec((1, dm)), _const_spec((1, dm))]
    args = [x, sh, sc, gt, wgu, wd, ln_g, ln_b]
    out_specs = [tile]
    out_shape = [jax.ShapeDtypeStruct((tokens, dm), F32)]
    if with_loss:
        in_specs.append(tile)
        args.append(target)
        out_specs.append(pl.BlockSpec((1, 128), lambda i: (0, 0)))
        out_shape.append(jax.ShapeDtypeStruct((1, 128), F32))
    out_specs += [tile, pl.BlockSpec((8, tm, fc), lambda i: (0, i, 0)), tile]
    out_shape += [jax.ShapeDtypeStruct((tokens, dm), F32), jax.ShapeDtypeStruct((8, tokens, fc), BF16),
                  jax.ShapeDtypeStruct((tokens, dm), BF16)]
    return _call(body, job, name=name, grid=(tokens // tm,), in_specs=in_specs, out_specs=out_specs,
                 out_shape=out_shape, args=args)


def ffn_bwd(dy, r, x, f, gu, sh, sc, gt, wgu, wd, ln_g, seq, name, job=None):
    tokens, dm = x.shape
    fc = wgu.shape[1]
    tm = min(TOKEN_TILE, seq)
    tiles_per_seq = seq // tm
    nseq = tokens // seq

    def body(dy_ref, r_ref, x_ref, f_ref, gu_ref, sh_ref, sc_ref, gt_ref, wgu_ref, wd_ref, lg_ref,
             dx_ref, dgu_ref, df_ref, a_ref, h_ref, dln_ref, dmod_ref):
        i = pl.program_id(0)
        dr, dgain, dbias = _ln_bwd(dy_ref[...], r_ref[...], lg_ref[...])

        @pl.when(i == 0)
        def _():
            dln_ref[...] = jnp.zeros_like(dln_ref)

        @pl.when(i % tiles_per_seq == 0)
        def _():
            dmod_ref[...] = jnp.zeros_like(dmod_ref)

        dln_ref[0:1, :] += dgain
        dln_ref[1:2, :] += dbias
        df32 = (0.5 * (1.0 + gt_ref[0])) * dr
        df = df32.astype(BF16)
        df_ref[...] = df
        dgate = jnp.sum(dr * (0.5 * f_ref[...].astype(F32)), axis=0, keepdims=True)
        xx = x_ref[...]
        one_sc = 1.0 + sc_ref[0]
        h = (xx * one_sc + sh_ref[0]).astype(BF16)
        h_ref[...] = h
        dh = jnp.zeros((tm, dm), F32)
        for k in range(4):
            da = _dot_nt(df, wd_ref[k])
            gk = gu_ref[k].astype(F32)
            uk = gu_ref[k + 4].astype(F32)
            sg = _sigmoid(gk)
            sil = gk * sg
            a_ref[k] = (sil * uk).astype(BF16)
            du = (da * sil).astype(BF16)
            dg = (da * uk * (sg * (1.0 + gk * (1.0 - sg)))).astype(BF16)
            dgu_ref[k] = dg
            dgu_ref[k + 4] = du
            dh = dh + _dot(dg, wgu_ref[k]) + _dot(du, wgu_ref[k + 4])
        dx_ref[...] = DN_ALPHA * dr + dh * one_sc
        dmod_ref[0, 0:1, :] += jnp.sum(dh, axis=0, keepdims=True)
        dmod_ref[0, 1:2, :] += jnp.sum(dh * xx, axis=0, keepdims=True)
        dmod_ref[0, 2:3, :] += dgate

    tile = pl.BlockSpec((tm, dm), lambda i: (i, 0))
    mod = _mod_spec(tiles_per_seq, dm)
    gu_spec = pl.BlockSpec((8, tm, fc), lambda i: (0, i, 0))
    return _call(
        body, job, name=name, grid=(tokens // tm,),
        in_specs=[tile, tile, tile, tile, gu_spec, mod, mod, mod, _const_spec(wgu.shape), _const_spec(wd.shape),
                  _const_spec((1, dm))],
        out_specs=[tile, gu_spec, tile, pl.BlockSpec((4, tm, fc), lambda i: (0, i, 0)), tile,
                   pl.BlockSpec((2, dm), lambda i: (0, 0)),
                   pl.BlockSpec((1, 3, dm), lambda i: (i // tiles_per_seq, 0, 0))],
        out_shape=[jax.ShapeDtypeStruct((tokens, dm), F32), jax.ShapeDtypeStruct((8, tokens, fc), BF16),
                   jax.ShapeDtypeStruct((tokens, dm), BF16), jax.ShapeDtypeStruct((4, tokens, fc), BF16),
                   jax.ShapeDtypeStruct((tokens, dm), BF16), jax.ShapeDtypeStruct((2, dm), F32),
                   jax.ShapeDtypeStruct((nseq, 3, dm), F32)],
        args=(dy, r, x, f, gu, sh, sc, gt, wgu, wd, ln_g))


def tn_matmul(a, b, name, job=None, b_cols=None):
    na, tokens, kk = a.shape
    nb, _, cc = b.shape
    col = 0
    if b_cols is not None:
        col, cc = b_cols
    tt = tokens
    while 4 * tt * (kk + cc) + 8 * kk * cc > TN_VMEM_BUDGET and tt % 2 == 0 and tt > 256:
        tt //= 2
    steps = tokens // tt

    def body(a_ref, b_ref, o_ref, *acc):
        if steps == 1:
            o_ref[0, 0] = _dot_tn(a_ref[0], b_ref[0]).astype(BF16)
            return
        acc_ref, = acc
        t = pl.program_id(2)

        @pl.when(t == 0)
        def _():
            acc_ref[...] = jnp.zeros_like(acc_ref)

        acc_ref[...] += _dot_tn(a_ref[0], b_ref[0])

        @pl.when(t == steps - 1)
        def _():
            o_ref[0, 0] = acc_ref[...].astype(BF16)

    return _call(
        body, job, name=name, grid=(na, nb, steps),
        in_specs=[pl.BlockSpec((1, tt, kk), lambda i, j, t: (i, t, 0)),
                  pl.BlockSpec((1, tt, cc), lambda i, j, t: (j, t, col))],
        out_specs=[pl.BlockSpec((1, 1, kk, cc), lambda i, j, t: (i, j, 0, 0))],
        out_shape=[jax.ShapeDtypeStruct((na, nb, kk, cc), BF16)],
        scratch_shapes=[] if steps == 1 else [pltpu.VMEM((kk, cc), F32)], args=(a, b))


def proj_fwd(x1, sh, sc, w_in, seq, name, job=None):
    tokens, dm = x1.shape
    tm = min(MIX_TILE, seq)
    tiles_per_seq = seq // tm
    widths = [N_Q_HEADS * HEAD_DIM, N_KV_HEADS * HEAD_DIM, N_KV_HEADS * HEAD_DIM, 512, 512, 512]
    assert sum(widths) == w_in.shape[0]

    def body(x_ref, sh_ref, sc_ref, w_ref, *outs):
        h = (x_ref[...] * (1.0 + sc_ref[0]) + sh_ref[0]).astype(BF16)
        proj = _dot_nt(h, w_ref[...])
        at = 0
        for o_ref, wdt in zip(outs, widths):
            o_ref[...] = proj[:, at:at + wdt]
            at += wdt

    tile = pl.BlockSpec((tm, dm), lambda i: (i, 0))
    mod = _mod_spec(tiles_per_seq, dm)
    return _call(
        body, job, name=name, grid=(tokens // tm,),
        in_specs=[tile, mod, mod, _const_spec(w_in.shape)],
        out_specs=[pl.BlockSpec((tm, wdt), lambda i: (i, 0)) for wdt in widths],
        out_shape=[jax.ShapeDtypeStruct((tokens, wdt), F32) for wdt in widths],
        args=(x1, sh, sc, w_in))


LANES = 2 * HEAD_DIM


def _head_lane(shape):
    return lax.broadcasted_iota(jnp.int32, shape, 1) % HEAD_DIM


def _lane_half(shape):
    return lax.broadcasted_iota(jnp.int32, shape, 1) // HEAD_DIM


def _swap_rot(v):
    lane = _head_lane(v.shape)
    half = ROT_DIM // 2
    return jnp.where(lane < half, pltpu.roll(v, LANES - half, 1),
                     jnp.where(lane < ROT_DIM, pltpu.roll(v, half, 1), 0.0))


def _rope(v, cos_t, sin_t):
    return v * cos_t + _swap_rot(v) * sin_t


def _unrope(dv, cos_t, sin_t):
    return dv * cos_t + _swap_rot(dv * sin_t)


def _both_halves(t, g):
    return jnp.where(_lane_half(t.shape) == g, t, pltpu.roll(t, HEAD_DIM, 1))


def _fold_halves(t, g):
    return jnp.where(_lane_half(t.shape) == g, t + pltpu.roll(t, HEAD_DIM, 1), 0.0)


def _stack_heads(blocks):
    rows = []
    for blk in blocks:
        half = _lane_half(blk.shape)
        rows += [jnp.where(half == 0, blk, 0.0), jnp.where(half == 1, blk, 0.0)]
    return jnp.concatenate(rows, axis=0)


def _unstack_heads(t, j):
    lo = t[(2 * j) * ATTN_BLOCK:(2 * j + 1) * ATTN_BLOCK]
    hi = t[(2 * j + 1) * ATTN_BLOCK:(2 * j + 2) * ATTN_BLOCK]
    return jnp.where(_lane_half(lo.shape) == 0, lo, hi)


def _band_mask(q0, w0):
    rows, cols = GQA_GROUP * ATTN_BLOCK, 2 * ATTN_BLOCK
    qi = lax.broadcasted_iota(jnp.int32, (rows, cols), 0) % ATTN_BLOCK + q0
    ki = lax.broadcasted_iota(jnp.int32, (rows, cols), 1) + w0
    diff = qi - ki
    return (diff >= 0) & (diff < ATTN_BLOCK)


def _attn_specs(seq):
    q_spec = pl.BlockSpec((seq, GQA_GROUP * HEAD_DIM), lambda b, g: (b, g))
    kv_spec = pl.BlockSpec((seq, LANES), lambda b, g: (b, 0))
    sink_spec = pl.BlockSpec((1, GQA_GROUP * ATTN_BLOCK, 1), lambda b, g: (g, 0, 0))
    return q_spec, kv_spec, sink_spec


def _block_starts(n):
    q0 = pl.multiple_of(n * ATTN_BLOCK, ATTN_BLOCK)
    w0 = pl.multiple_of(jnp.maximum(n - 1, 0) * ATTN_BLOCK, ATTN_BLOCK)
    return q0, w0


def _stacked_queries(ref, rows):
    return _stack_heads([ref[rows, j * LANES:(j + 1) * LANES] for j in range(2)]).astype(BF16)


def _sink_columns(sinks):
    return jnp.repeat(sinks.reshape(N_KV_HEADS, GQA_GROUP), ATTN_BLOCK, axis=1)[:, :, None]


def _probs_spec(nblk):
    return pl.BlockSpec((1, 1, nblk, GQA_GROUP * ATTN_BLOCK, 2 * ATTN_BLOCK), lambda b, g: (b, g, 0, 0, 0))


def _sink_probs_spec():
    return pl.BlockSpec((1, 1, GQA_GROUP * ATTN_BLOCK, LANES), lambda b, g: (b, g, 0, 0))


def attn_fwd(q, k, v, cos_t, sin_t, sinks, seq, name, job=None):
    tokens = q.shape[0]
    nblk = seq // ATTN_BLOCK
    assert nblk >= 2
    scale = HEAD_DIM ** -0.5

    nseq = tokens // seq
    rows_stacked = GQA_GROUP * ATTN_BLOCK
    assert nblk <= LANES

    def body(q_ref, k_ref, v_ref, cos_ref, sin_ref, sink_ref, o_ref, qr_ref, p_ref, ps_ref, kd_ref, vd_ref):
        g = pl.program_id(1)
        kd_ref[...] = _both_halves(_rope(k_ref[...], cos_ref[...], sin_ref[...]), g).astype(BF16)
        vd_ref[...] = _both_halves(v_ref[...], g).astype(BF16)
        sink = sink_ref[0]
        lane = lax.broadcasted_iota(jnp.int32, (rows_stacked, LANES), 1)

        ps_ref[...] = jnp.zeros_like(ps_ref)

        def block(n, carry):
            q0, w0 = _block_starts(n)
            rows, win = pl.ds(q0, ATTN_BLOCK), pl.ds(w0, 2 * ATTN_BLOCK)
            blocks = []
            for j in range(2):
                qr = _rope(q_ref[rows, j * LANES:(j + 1) * LANES], cos_ref[rows, :], sin_ref[rows, :]).astype(BF16)
                qr_ref[rows, j * LANES:(j + 1) * LANES] = qr
                blocks.append(qr)
            qs = _stack_heads(blocks)
            s = _dot_nt(qs, kd_ref[win, :]) * scale
            s = jnp.where(_band_mask(q0, w0), s, NEG_BIG)
            m = jnp.maximum(jnp.max(s, axis=-1, keepdims=True), sink)
            p = jnp.exp(s - m)
            e_sink = jnp.exp(sink - m)
            inv = pl.reciprocal(jnp.sum(p, axis=-1, keepdims=True) + e_sink, approx=True)
            pn = (p * inv).astype(BF16)
            p_ref[0, 0, n] = pn
            out = _dot(pn, vd_ref[win, :])
            for j in range(2):
                o_ref[rows, j * LANES:(j + 1) * LANES] = _unstack_heads(out, j).astype(o_ref.dtype)
            ps_ref[0, 0] = jnp.where(lane == n, e_sink * inv, ps_ref[0, 0])
            return carry

        lax.fori_loop(0, nblk, block, 0, unroll=2)

    q_spec, kv_spec, sink_spec = _attn_specs(seq)
    return _call(
        body, job, name=name, grid=(nseq, N_KV_HEADS),
        in_specs=[q_spec, kv_spec, kv_spec, kv_spec, kv_spec, sink_spec],
        out_specs=[q_spec, q_spec, _probs_spec(nblk), _sink_probs_spec()],
        out_shape=[jax.ShapeDtypeStruct(q.shape, BF16), jax.ShapeDtypeStruct(q.shape, BF16),
                   jax.ShapeDtypeStruct((nseq, N_KV_HEADS, nblk, rows_stacked, 2 * ATTN_BLOCK), BF16),
                   jax.ShapeDtypeStruct((nseq, N_KV_HEADS, rows_stacked, LANES), F32)],
        scratch_shapes=[pltpu.VMEM((seq, LANES), BF16), pltpu.VMEM((seq, LANES), BF16)],
        args=(q, k, v, cos_t, sin_t, _sink_columns(sinks)))


def attn_bwd(qr, k, v, do, probs, sink_probs, cos_t, sin_t, seq, name, job=None):
    tokens = qr.shape[0]
    nseq = tokens // seq
    nblk = seq // ATTN_BLOCK
    assert nblk >= 2
    rows_stacked = GQA_GROUP * ATTN_BLOCK
    scale = HEAD_DIM ** -0.5

    def body(q_ref, k_ref, v_ref, do_ref, p_ref, ps_ref, cos_ref, sin_ref, dq_ref, dk_ref, dv_ref, ds_ref,
             kd_ref, vd_ref, dkd_ref, dvd_ref, acc_ref):
        g = pl.program_id(1)
        kd_ref[...] = _both_halves(_rope(k_ref[...], cos_ref[...], sin_ref[...]), g).astype(BF16)
        vd_ref[...] = _both_halves(v_ref[...], g).astype(BF16)
        dkd_ref[...] = jnp.zeros_like(dkd_ref)
        dvd_ref[...] = jnp.zeros_like(dvd_ref)
        acc_ref[...] = jnp.zeros_like(acc_ref)
        lane = lax.broadcasted_iota(jnp.int32, (rows_stacked, LANES), 1)

        def block(n, carry):
            q0, w0 = _block_starts(n)
            rows, win = pl.ds(q0, ATTN_BLOCK), pl.ds(w0, 2 * ATTN_BLOCK)
            qs = _stacked_queries(q_ref, rows)
            dos = _stacked_queries(do_ref, rows)
            kw, vw = kd_ref[win, :], vd_ref[win, :]
            pn16 = p_ref[0, 0, n]
            pn = pn16.astype(F32)
            dvd_ref[win, :] += _dot_tn(pn16, dos)
            dp = _dot_nt(dos, vw)
            delta = jnp.sum(dp * pn, axis=-1, keepdims=True)
            ds = (pn * (dp - delta)).astype(BF16)
            dqs = _dot(ds, kw) * scale
            dkd_ref[win, :] += _dot_tn(ds, qs) * scale
            cos_b, sin_b = cos_ref[rows, :], sin_ref[rows, :]
            for j in range(2):
                dq_ref[rows, j * LANES:(j + 1) * LANES] = _unrope(_unstack_heads(dqs, j), cos_b, sin_b).astype(BF16)
            acc_ref[...] += jnp.where(lane == n, ps_ref[0, 0] * delta, 0.0)
            return carry

        lax.fori_loop(0, nblk // 2, lambda i, carry: block(2 * i + 1, block(2 * i, carry)), 0)
        ds_ref[0, 0] = -jnp.sum(acc_ref[...], axis=-1, keepdims=True)
        dk_g = _unrope(_fold_halves(dkd_ref[...], g), cos_ref[...], sin_ref[...])
        dv_g = _fold_halves(dvd_ref[...], g)

        @pl.when(g == 0)
        def _():
            dk_ref[...] = dk_g
            dv_ref[...] = dv_g

        @pl.when(g != 0)
        def _():
            dk_ref[...] += dk_g
            dv_ref[...] += dv_g

    q_spec, kv_spec, _ = _attn_specs(seq)
    return _call(
        body, job, name=name, grid=(nseq, N_KV_HEADS),
        in_specs=[q_spec, kv_spec, kv_spec, q_spec, _probs_spec(nblk), _sink_probs_spec(), kv_spec, kv_spec],
        out_specs=[q_spec, kv_spec, kv_spec, pl.BlockSpec((1, 1, rows_stacked, 1), lambda b, g: (b, g, 0, 0))],
        out_shape=[jax.ShapeDtypeStruct(qr.shape, BF16), jax.ShapeDtypeStruct(k.shape, F32),
                   jax.ShapeDtypeStruct(k.shape, F32), jax.ShapeDtypeStruct((nseq, N_KV_HEADS, rows_stacked, 1), F32)],
        scratch_shapes=[pltpu.VMEM((seq, LANES), BF16), pltpu.VMEM((seq, LANES), BF16),
                        pltpu.VMEM((seq, LANES), F32), pltpu.VMEM((seq, LANES), F32),
                        pltpu.VMEM((rows_stacked, LANES), F32)],
        args=(qr, k, v, do, probs, sink_probs, cos_t, sin_t))


CONV_COLS = 128


def _shift_down(z, by):
    t = lax.broadcasted_iota(jnp.int32, z.shape, 0)
    return jnp.where(t >= by, pltpu.roll(z, by, 0), 0.0)


def _shift_up(z, by):
    n = z.shape[0]
    t = lax.broadcasted_iota(jnp.int32, z.shape, 0)
    return jnp.where(t < n - by, pltpu.roll(z, n - by, 0), 0.0)


def conv_fwd(u, bg, cg, conv_w, seq, name):
    tokens, width = u.shape

    def body(u_ref, bg_ref, cg_ref, w_ref, o_ref):
        z = cg_ref[...] * u_ref[...]
        yy = w_ref[2:3, :] * z + w_ref[1:2, :] * _shift_down(z, 1) + w_ref[0:1, :] * _shift_down(z, 2)
        o_ref[...] = (bg_ref[...] * yy).astype(BF16)

    col = pl.BlockSpec((seq, CONV_COLS), lambda j, b: (b, j))
    return pl.pallas_call(
        body, name=name, grid=(width // CONV_COLS, tokens // seq),
        in_specs=[col, col, col, pl.BlockSpec((CONV_TAPS, CONV_COLS), lambda j, b: (0, j))],
        out_specs=col, out_shape=jax.ShapeDtypeStruct((tokens, width), BF16),
        compiler_params=_params(("parallel", "parallel")),
    )(u, bg, cg, conv_w)


def conv_bwd(dout, u, bg, cg, conv_w, seq, name):
    tokens, width = u.shape

    def body(do_ref, u_ref, bg_ref, cg_ref, w_ref, du_ref, dbg_ref, dcg_ref, dw_ref):
        uu, cg_v, do = u_ref[...], cg_ref[...], do_ref[...].astype(F32)
        z = cg_v * uu
        z1, z2 = _shift_down(z, 1), _shift_down(z, 2)
        yy = w_ref[2:3, :] * z + w_ref[1:2, :] * z1 + w_ref[0:1, :] * z2
        dbg_ref[...] = (do * yy).astype(BF16)
        dyy = do * bg_ref[...]
        dz = w_ref[2:3, :] * dyy + w_ref[1:2, :] * _shift_up(dyy, 1) + w_ref[0:1, :] * _shift_up(dyy, 2)
        du_ref[...] = (dz * cg_v).astype(BF16)
        dcg_ref[...] = (dz * uu).astype(BF16)

        @pl.when(pl.program_id(1) == 0)
        def _():
            dw_ref[...] = jnp.zeros_like(dw_ref)

        dw_ref[0:1, :] += jnp.sum(dyy * z2, axis=0, keepdims=True)
        dw_ref[1:2, :] += jnp.sum(dyy * z1, axis=0, keepdims=True)
        dw_ref[2:3, :] += jnp.sum(dyy * z, axis=0, keepdims=True)

    col = pl.BlockSpec((seq, CONV_COLS), lambda j, b: (b, j))
    w_spec = pl.BlockSpec((CONV_TAPS, CONV_COLS), lambda j, b: (0, j))
    act = jax.ShapeDtypeStruct((tokens, width), BF16)
    return pl.pallas_call(
        body, name=name, grid=(width // CONV_COLS, tokens // seq),
        in_specs=[col, col, col, col, w_spec], out_specs=[col, col, col, w_spec],
        out_shape=[act, act, act, jax.ShapeDtypeStruct((CONV_TAPS, width), F32)],
        compiler_params=_params(("parallel", "arbitrary")),
    )(dout, u, bg, cg, conv_w)


def out_fwd(x1, attn, conv, gt, w_out, ln_g, ln_b, seq, name, job=None):
    tokens, dm = x1.shape
    half = attn.shape[1]
    tm = min(MIX_TILE, seq)
    tiles_per_seq = seq // tm

    def body(x_ref, a_ref, c_ref, gt_ref, w_ref, lg_ref, lb_ref, xo_ref, r_ref, mi_ref, mix_ref):
        mixin = jnp.concatenate([a_ref[...], c_ref[...]], axis=1).astype(BF16)
        mi_ref[...] = mixin
        mix = _dot(mixin, w_ref[...])
        mix_ref[...] = mix.astype(BF16)
        r = DN_ALPHA * x_ref[...] + (1.0 + gt_ref[0]) * mix
        r_ref[...] = r
        xhat, _ = _ln_stats(r)
        xo_ref[...] = xhat * lg_ref[...] + lb_ref[...]

    tile = pl.BlockSpec((tm, dm), lambda i: (i, 0))
    htile = pl.BlockSpec((tm, half), lambda i: (i, 0))
    return _call(
        body, job, name=name, grid=(tokens // tm,),
        in_specs=[tile, htile, htile, _mod_spec(tiles_per_seq, dm), _const_spec(w_out.shape),
                  _const_spec((1, dm)), _const_spec((1, dm))],
        out_specs=[tile, tile, tile, tile],
        out_shape=[jax.ShapeDtypeStruct((tokens, dm), F32), jax.ShapeDtypeStruct((tokens, dm), F32),
                   jax.ShapeDtypeStruct((tokens, dm), BF16), jax.ShapeDtypeStruct((tokens, dm), BF16)],
        args=(x1, attn, conv, gt, w_out, ln_g, ln_b))


def out_bwd(dy, r, mix, gt, w_out, ln_g, seq, name, job=None):
    tokens, dm = r.shape
    half = dm // 2
    tm = min(MIX_TILE, seq)
    tiles_per_seq = seq // tm
    nseq = tokens // seq

    def body(dy_ref, r_ref, mix_ref, gt_ref, w_ref, lg_ref, dres_ref, da_ref, dc_ref, dmix_ref, dln_ref, dgt_ref):
        i = pl.program_id(0)
        dr, dgain, dbias = _ln_bwd(dy_ref[...], r_ref[...], lg_ref[...])

        @pl.when(i == 0)
        def _():
            dln_ref[...] = jnp.zeros_like(dln_ref)

        @pl.when(i % tiles_per_seq == 0)
        def _():
            dgt_ref[...] = jnp.zeros_like(dgt_ref)

        dln_ref[0:1, :] += dgain
        dln_ref[1:2, :] += dbias
        dgt_ref[0] += jnp.sum(dr * mix_ref[...].astype(F32), axis=0, keepdims=True)
        dres_ref[...] = DN_ALPHA * dr
        dmix = ((1.0 + gt_ref[0]) * dr).astype(BF16)
        dmix_ref[...] = dmix
        dmixin = _dot_nt(dmix, w_ref[...])
        da_ref[...] = dmixin[:, :half].astype(BF16)
        dc_ref[...] = dmixin[:, half:].astype(BF16)

    tile = pl.BlockSpec((tm, dm), lambda i: (i, 0))
    htile = pl.BlockSpec((tm, half), lambda i: (i, 0))
    return _call(
        body, job, name=name, grid=(tokens // tm,),
        in_specs=[tile, tile, tile, _mod_spec(tiles_per_seq, dm), _const_spec(w_out.shape), _const_spec((1, dm))],
        out_specs=[tile, htile, htile, tile, pl.BlockSpec((2, dm), lambda i: (0, 0)),
                   pl.BlockSpec((1, 1, dm), lambda i: (i // tiles_per_seq, 0, 0))],
        out_shape=[jax.ShapeDtypeStruct((tokens, dm), F32), jax.ShapeDtypeStruct((tokens, half), BF16),
                   jax.ShapeDtypeStruct((tokens, half), BF16), jax.ShapeDtypeStruct((tokens, dm), BF16),
                   jax.ShapeDtypeStruct((2, dm), F32), jax.ShapeDtypeStruct((nseq, 1, dm), F32)],
        args=(dy, r, mix, gt, w_out, ln_g))


def proj_bwd(parts, dres, x1, sh, sc, w_in, seq, name, job=None):
    tokens, dm = x1.shape
    tm = min(MIX_TILE, seq)
    tiles_per_seq = seq // tm
    nseq = tokens // seq
    widths = [p.shape[1] for p in parts]
    total = sum(widths)

    def body(*refs):
        part_refs = refs[:6]
        dres_ref, x_ref, sh_ref, sc_ref, w_ref, dx_ref, dproj_ref, h_ref, dmod_ref = refs[6:]
        dproj = jnp.concatenate([p[...].astype(BF16) for p in part_refs], axis=1)
        dproj_ref[...] = dproj
        dh = _dot(dproj, w_ref[...])
        xx = x_ref[...]
        one_sc = 1.0 + sc_ref[0]
        h_ref[...] = (xx * one_sc + sh_ref[0]).astype(BF16)
        dx_ref[...] = dres_ref[...] + dh * one_sc

        @pl.when(pl.program_id(0) % tiles_per_seq == 0)
        def _():
            dmod_ref[...] = jnp.zeros_like(dmod_ref)

        dmod_ref[0, 0:1, :] += jnp.sum(dh, axis=0, keepdims=True)
        dmod_ref[0, 1:2, :] += jnp.sum(dh * xx, axis=0, keepdims=True)

    tile = pl.BlockSpec((tm, dm), lambda i: (i, 0))
    mod = _mod_spec(tiles_per_seq, dm)
    return _call(
        body, job, name=name, grid=(tokens // tm,),
        in_specs=[pl.BlockSpec((tm, wdt), lambda i: (i, 0)) for wdt in widths]
        + [tile, tile, mod, mod, _const_spec(w_in.shape)],
        out_specs=[tile, pl.BlockSpec((tm, total), lambda i: (i, 0)), tile,
                   pl.BlockSpec((1, 2, dm), lambda i: (i // tiles_per_seq, 0, 0))],
        out_shape=[jax.ShapeDtypeStruct((tokens, dm), F32), jax.ShapeDtypeStruct((tokens, total), BF16),
                   jax.ShapeDtypeStruct((tokens, dm), BF16), jax.ShapeDtypeStruct((nseq, 2, dm), F32)],
        args=(*parts, dres, x1, sh, sc, w_in))


def _rope_tables(positions):
    half = ROT_DIM // 2
    inv_freq = jnp.power(jnp.float32(ROPE_THETA), -jnp.arange(0, ROT_DIM, 2, dtype=F32) / ROT_DIM)
    lane = jnp.arange(LANES) % HEAD_DIM
    freq = jnp.where(lane < ROT_DIM, inv_freq[lane % half], 0.0)
    sign = jnp.where(lane < half, -1.0, 1.0).astype(F32)
    ang = positions.astype(F32)[:, None] * freq[None, :]
    return jnp.cos(ang), sign[None, :] * jnp.sin(ang)


def kernel(x, c, positions, w_ada, b_ada, ffn1_w_gate_up, ffn1_w_down, ln1_g, ln1_b, w_in, conv_w, attn_sinks, w_out, ln2_g, ln2_b, ffn2_w_gate_up, ffn2_w_down, ln3_g, ln3_b, loss_target, m_w_ada, m_b_ada, m_ffn1_w_gate_up, m_ffn1_w_down, m_ln1_g, m_ln1_b, m_w_in, m_conv_w, m_attn_sinks, m_w_out, m_ln2_g, m_ln2_b, m_ffn2_w_gate_up, m_ffn2_w_down, m_ln3_g, m_ln3_b, v_w_ada, v_b_ada, v_ffn1_w_gate_up, v_ffn1_w_down, v_ln1_g, v_ln1_b, v_w_in, v_conv_w, v_attn_sinks, v_w_out, v_ln2_g, v_ln2_b, v_ffn2_w_gate_up, v_ffn2_w_down, v_ln3_g, v_ln3_b):
    nseq, seq, dm = x.shape
    tokens = nseq * seq
    dev = 4 * lax.axis_index("x") + 2 * lax.axis_index("y") + lax.axis_index("c")
    ada_cols = w_ada.shape[2]
    ff = ffn1_w_down.shape[1] * N_DEV
    fc = ff // 4
    in_cols = w_in.shape[2]
    conv_cols = conv_w.shape[2]

    def t_bf16(w):
        return w[0].T.astype(BF16)

    c_all, convw_all = all_gather([c, conv_w[0]], "gather_cond")
    wgu1, wd1 = all_gather([t_bf16(ffn1_w_gate_up), ffn1_w_down[0].astype(BF16)], "gather_ffn1")
    c_all = c_all.reshape(N_DEV * nseq, dm)
    convw_full = convw_all.transpose(1, 0, 2).reshape(CONV_TAPS, N_DEV * conv_cols)
    wd1 = wd1.reshape(4, fc, dm)

    b_cols = lax.dynamic_slice(b_ada, (0, dev * ada_cols), (1, ada_cols))
    cond_all, mod_cols = ada_fwd(c_all, w_ada[0], b_cols, "ada_fwd")
    (mod_all,) = all_gather([mod_cols], "gather_mod")
    mod = lax.dynamic_slice(mod_all, (0, dev * nseq, 0), (N_DEV, nseq, ada_cols))
    mod = mod.transpose(1, 0, 2).reshape(nseq, 9, 1, dm)
    sh1, sc1, g1, sh2, sc2, g2, sh3, sc3, g3 = [mod[:, i] for i in range(9)]

    x0 = x.reshape(tokens, dm)
    spread = gather_spread_job([t_bf16(w_in), w_out[0].astype(BF16), ffn2_w_down[0].astype(BF16)])
    (x1, r1, gu1, f1), spread = ffn_fwd(x0, sh1, sc1, g1, wgu1, wd1, ln1_g, ln1_b, seq, "ffn1_fwd", job=spread)
    win, wout = run_job(gather_forward_job(spread[:2]), "gather_mix_forward")
    win = win.reshape(N_DEV * in_cols, dm)
    wout = wout.reshape(dm, dm)
    (q, k, v, u, bg, cg), (wd2,) = proj_fwd(x1, sh2, sc2, win, seq, "proj_fwd", job=gather_forward_job(spread[2:]))
    wd2 = wd2.reshape(4, fc, dm)
    cos_t, sin_t = _rope_tables(positions.reshape(tokens))
    sinks = attn_sinks[0]
    (attn, q_rot, probs, sink_probs), spread = attn_fwd(q, k, v, cos_t, sin_t, sinks, seq, "attn_fwd",
                                                        job=gather_spread_job([t_bf16(ffn2_w_gate_up)]))
    conv = conv_fwd(u, bg, cg, convw_full, seq, "conv_fwd")
    (x2, r2, mixin, mix), (wgu2,) = out_fwd(x1, attn, conv, g2, wout, ln2_g, ln2_b, seq, "out_fwd",
                                            job=gather_forward_job(spread))
    target = loss_target.reshape(tokens, dm)
    (dy3, loss_part, r3, gu3, f3), _ = ffn_fwd(x2, sh3, sc3, g3, wgu2, wd2, ln3_g, ln3_b, seq, "ffn2_fwd", target=target)

    (dx2, dgu3, df3, a3, h3, dln3, dmod3), _ = ffn_bwd(dy3, r3, x2, f3, gu3, sh3, sc3, g3, wgu2, wd2, ln3_g, seq, "ffn2_bwd")
    g_wd2 = tn_matmul(a3, df3[None], "ffn2_dwd")[0][0].reshape(N_DEV, ff // N_DEV, dm)
    g_wgu2 = tn_matmul(dgu3, h3[None], "ffn2_dwgu")[0][0].reshape(N_DEV, fc, dm)
    (dres2, dattn, dconv, dmix, dln2, dg2), swapped = out_bwd(dx2, r2, mix, g2, wout, ln2_g, seq, "out_bwd",
                                                              job=swap_job([g_wgu2, g_wd2]))
    p_wgu2, own_wgu2 = pair_sum(g_wgu2, swapped[0], "pair_wgu2")
    p_wd2, own_wd2 = pair_sum(g_wd2, swapped[1], "pair_wd2")
    du, dbg, dcg, dconvw = conv_bwd(dconv, u, bg, cg, convw_full, seq, "conv_bwd")
    (dq, dk, dv, dsink_rows), (far_wd2,) = attn_bwd(
        q_rot, k, v, dattn, probs, sink_probs, cos_t, sin_t, seq, "attn_bwd", job=chip_exchange_job([p_wd2]))
    parts = [dq, dk, dv, du, dbg, dcg]
    (dx1, dproj, h2, dmod2), far_top = proj_bwd(parts, dres2, x1, sh2, sc2, win, seq, "proj_bwd",
                                                job=chip_exchange_job([p_wgu2], rows=(0, fc // 2)))
    (dx0, dgu1, df1, a1, h1, dln1, dmod1), _ = ffn_bwd(
        dx1, r1, x0, f1, gu1, sh1, sc1, g1, wgu1, wd1, ln1_g, seq, "ffn1_bwd")

    dmod = jnp.concatenate([dmod1, dmod2, dg2, dmod3], axis=1).reshape(nseq, 9 * dm)
    half = dm // 2
    jobs = _Jobs([gather_spread_job([dmod]),
                  chip_exchange_job([p_wgu2], rows=(fc // 2, fc // 2), into=far_top)])
    (g_wd1,), res = tn_matmul(a1, df1[None], "ffn1_dwd", job=jobs)
    dmod_spread, (far_wgu2,) = jobs.split(res)
    g_wd1 = g_wd1.reshape(N_DEV, ff // N_DEV, dm)
    jobs = _Jobs([swap_job([g_wd1]), gather_forward_job(dmod_spread)])
    (g_l,), res = tn_matmul(dgu1, h1[None], "ffn1_dwgu_l", job=jobs, b_cols=(0, half))
    (sw_wd1,), (dmod_all,) = jobs.split(res)
    g_l = g_l.reshape(N_DEV, fc, half)
    p_wd1, own_wd1 = pair_sum(g_wd1, sw_wd1, "pair_wd1")
    jobs = _Jobs([chip_exchange_job([p_wd1]), swap_job([g_l])])
    (g_r,), res = tn_matmul(dgu1, h1[None], "ffn1_dwgu_r", job=jobs, b_cols=(1, half))
    (far_wd1,), (sw_l,) = jobs.split(res)
    g_r = g_r.reshape(N_DEV, fc, half)
    p_l, own_l = pair_sum(g_l, sw_l, "pair_wgu1_l")
    jobs = _Jobs([chip_exchange_job([p_l]), swap_job([g_r])])
    (g_win,), res = tn_matmul(dproj[None], h2[None], "dwin", job=jobs)
    (far_l,), (sw_r,) = jobs.split(res)
    g_win = g_win.reshape(N_DEV, in_cols, dm)
    p_r, own_r = pair_sum(g_r, sw_r, "pair_wgu1_r")
    jobs = _Jobs([chip_exchange_job([p_r]), swap_job([g_win])])
    (g_wout,), res = tn_matmul(mixin[None], dmix[None], "dwout", job=jobs)
    (far_r,), (sw_win,) = jobs.split(res)
    g_wout = g_wout.reshape(N_DEV, dm // N_DEV, dm)
    p_win, own_win = pair_sum(g_win, sw_win, "pair_win")
    jobs = _Jobs([chip_exchange_job([p_win]), swap_job([g_wout])])
    (far_win,), (sw_wout,) = jobs.split(run_job(jobs, "rs_tail_win"))
    p_wout, own_wout = pair_sum(g_wout, sw_wout, "pair_wout")
    (far_wout,) = run_job(chip_exchange_job([p_wout]), "rs_tail_wout")

    grads = {
        "ffn1_w_gate_up": jnp.concatenate([own_l, own_r], axis=1), "ffn1_w_down": own_wd1,
        "w_in": own_win, "w_out": own_wout, "ffn2_w_gate_up": own_wgu2, "ffn2_w_down": own_wd2,
    }
    others = {"ffn1_w_gate_up": jnp.concatenate([far_l, far_r], axis=2), "ffn1_w_down": far_wd1,
              "w_in": far_win, "w_out": far_wout, "ffn2_w_gate_up": far_wgu2, "ffn2_w_down": far_wd2}

    dmod_cols = lax.dynamic_slice(dmod_all.reshape(N_DEV * nseq, 9 * dm), (0, dev * ada_cols), (N_DEV * nseq, ada_cols))
    grads["w_ada"], gb_cols = ada_bwd(cond_all, dmod_cols, "ada_bwd")

    dsinks = jnp.sum(dsink_rows.reshape(nseq, N_Q_HEADS, ATTN_BLOCK), axis=(0, 2))
    small = jnp.zeros((8, dm), F32)
    small = small.at[0:2].set(dln1).at[2:4].set(dln2).at[4:6].set(dln3)
    small = small.at[6, 0:N_Q_HEADS].set(dsinks).at[7, 0].set(loss_part[0, 0])
    small_all, dconvw_all, gb_all = all_gather([small, dconvw, gb_cols], "gather_small")
    small_sum = sum_devices(small_all, "sum_small")
    dconvw_sum = sum_devices(dconvw_all, "sum_convw")
    loss = small_sum[7, 0]
    grads["b_ada"] = gb_all.reshape(1, N_DEV * ada_cols)
    grads["conv_w"] = lax.dynamic_slice(dconvw_sum, (0, dev * conv_cols), (CONV_TAPS, conv_cols))
    grads["attn_sinks"] = small_sum[6:7, 0:N_Q_HEADS]
    for i, nm in enumerate(["ln1_g", "ln1_b", "ln2_g", "ln2_b", "ln3_g", "ln3_b"]):
        grads[nm] = small_sum[i:i + 1]

    given = dict(w_ada=(w_ada, m_w_ada, v_w_ada), b_ada=(b_ada, m_b_ada, v_b_ada),
                 ffn1_w_gate_up=(ffn1_w_gate_up, m_ffn1_w_gate_up, v_ffn1_w_gate_up),
                 ffn1_w_down=(ffn1_w_down, m_ffn1_w_down, v_ffn1_w_down),
                 ln1_g=(ln1_g, m_ln1_g, v_ln1_g), ln1_b=(ln1_b, m_ln1_b, v_ln1_b),
                 w_in=(w_in, m_w_in, v_w_in), conv_w=(conv_w, m_conv_w, v_conv_w),
                 attn_sinks=(attn_sinks, m_attn_sinks, v_attn_sinks), w_out=(w_out, m_w_out, v_w_out),
                 ln2_g=(ln2_g, m_ln2_g, v_ln2_g), ln2_b=(ln2_b, m_ln2_b, v_ln2_b),
                 ffn2_w_gate_up=(ffn2_w_gate_up, m_ffn2_w_gate_up, v_ffn2_w_gate_up),
                 ffn2_w_down=(ffn2_w_down, m_ffn2_w_down, v_ffn2_w_down),
                 ln3_g=(ln3_g, m_ln3_g, v_ln3_g), ln3_b=(ln3_b, m_ln3_b, v_ln3_b))
    order = ["w_ada", "b_ada", "ffn1_w_gate_up", "ffn1_w_down", "ln1_g", "ln1_b", "w_in", "conv_w", "attn_sinks",
             "w_out", "ln2_g", "ln2_b", "ffn2_w_gate_up", "ffn2_w_down", "ln3_g", "ln3_b"]
    transposed = ("ffn1_w_gate_up", "ffn2_w_gate_up", "w_in")
    out_g, out_d, out_m, out_v = [], [], [], []
    for nm in order:
        shape = given[nm][0].shape
        two_d = (shape[-2], shape[-1])
        if nm in transposed:
            w2, m2, v2 = [t[0].T for t in given[nm]]
            back = lambda t: t.T[None]
        else:
            w2, m2, v2 = [t.reshape(two_d) for t in given[nm]]
            back = lambda t, shape=shape: t.reshape(shape)
        res = adamw(w2, grads[nm].reshape(w2.shape), m2, v2, "adamw_" + nm, others=others.get(nm))
        for lst, t in zip((out_g, out_d, out_m, out_v), res):
            lst.append(back(t))
    grad_x = dx0.reshape(nseq, seq, dm)
    return (loss, grad_x, *out_g, *out_d, *out_m, *out_v)
```

```python
import functools

import jax
import jax.numpy as jnp
from jax import lax
from jax.experimental import pallas as pl
from jax.experimental.pallas import tpu as pltpu

F32 = jnp.float32
BF16 = jnp.bfloat16
MESH = pl.DeviceIdType.MESH

N_DEV = 8
N_CHIP = 4
HEAD_DIM = 64
N_Q_HEADS = 8
N_KV_HEADS = 2
GQA_GROUP = N_Q_HEADS // N_KV_HEADS
ATTN_BLOCK = 128
ROT_DIM = 16
ROPE_THETA = 500000.0
CONV_TAPS = 3
LN_EPS = 1e-5
DN_ALPHA = 2.0 ** 0.25
ADAM_LR = 0.001
ADAM_B1 = 0.9
ADAM_B2 = 0.999
ADAM_EPS = 1e-08
ADAM_WD = 0.01
ADAM_STEP = 10
NEG_BIG = -1e30

VMEM_LIMIT = 56 * 1024 * 1024
TOKEN_TILE = 256
FFN_FWD_TILE = 512
MIX_TILE = 512
TN_VMEM_BUDGET = 36 * 1024 * 1024


def _params(semantics=None, vmem=VMEM_LIMIT):
    return pltpu.CompilerParams(dimension_semantics=semantics, vmem_limit_bytes=vmem)


def _dot(a, b):
    return jnp.dot(a, b, preferred_element_type=F32)


def _dot_nt(a, b):
    return lax.dot_general(a, b, (((1,), (1,)), ((), ())), preferred_element_type=F32)


def _dot_tn(a, b):
    return lax.dot_general(a, b, (((0,), (0,)), ((), ())), preferred_element_type=F32)


def _sigmoid(x):
    return pl.reciprocal(1.0 + jnp.exp(-x), approx=True)


def _ln_stats(r):
    mu = jnp.mean(r, axis=-1, keepdims=True)
    d = r - mu
    var = jnp.mean(d * d, axis=-1, keepdims=True)
    rstd = lax.rsqrt(var + LN_EPS)
    return d * rstd, rstd


def _ln_bwd(dy, r, g):
    xhat, rstd = _ln_stats(r)
    dxhat = dy * g
    c1 = jnp.mean(dxhat, axis=-1, keepdims=True)
    c2 = jnp.mean(dxhat * xhat, axis=-1, keepdims=True)
    dr = rstd * (dxhat - c1 - xhat * c2)
    return dr, jnp.sum(dy * xhat, axis=0, keepdims=True), jnp.sum(dy, axis=0, keepdims=True)


def _const_spec(shape):
    nd = len(shape)
    return pl.BlockSpec(shape, lambda *_: (0,) * nd, pipeline_mode=pl.Buffered(1))


def all_gather(arrs, name):
    n = len(arrs)

    def body(*refs):
        ins, outs = refs[:n], refs[n:2 * n]
        send_sems, recv_sems, local_sems = refs[2 * n:]
        x, y, c = lax.axis_index("x"), lax.axis_index("y"), lax.axis_index("c")
        me, sibling = (x, y, c), (x, y, 1 - c)
        chips = [(1 - x, y), (x, 1 - y), (1 - x, 1 - y)]

        def slot(i, p):
            return outs[i].at[4 * p[0] + 2 * p[1] + p[2]]

        def copy(i, k, block, to, src=None):
            return pltpu.make_async_remote_copy(
                src_ref=slot(i, block) if src is None else src, dst_ref=slot(i, block),
                send_sem=send_sems.at[i, k], recv_sem=recv_sems.at[i, k],
                device_id=to, device_id_type=MESH)

        mine = [pltpu.make_async_copy(ins[i], slot(i, me), local_sems.at[i]) for i in range(n)]
        for cp in mine:
            cp.start()
        first = []
        for i in range(n):
            first.append(copy(i, 0, me, sibling, src=ins[i]))
            first += [copy(i, 1 + j, me, (*chip, c), src=ins[i]) for j, chip in enumerate(chips)]
        for cp in first:
            cp.start()
        passed = []
        for i in range(n):
            for j, chip in enumerate(chips):
                copy(i, 1 + j, (*chip, c), me).wait_recv()
                cp = copy(i, 4 + j, (*chip, c), sibling)
                cp.start()
                passed.append(cp)
        for i in range(n):
            copy(i, 0, sibling, me).wait_recv()
            for j, chip in enumerate(chips):
                copy(i, 4 + j, (*chip, 1 - c), me).wait_recv()
        for cp in first + passed:
            cp.wait_send()
        for cp in mine:
            cp.wait()

    any_spec = pl.BlockSpec(memory_space=pl.ANY)
    return pl.pallas_call(
        body, name=name,
        out_shape=[jax.ShapeDtypeStruct((N_DEV, *a.shape), a.dtype) for a in arrs],
        in_specs=[any_spec] * n, out_specs=[any_spec] * n,
        scratch_shapes=[pltpu.SemaphoreType.DMA((n, 7)), pltpu.SemaphoreType.DMA((n, 7)),
                        pltpu.SemaphoreType.DMA((n,))],
    )(*arrs)


def _place():
    x, y, c = lax.axis_index("x"), lax.axis_index("y"), lax.axis_index("c")
    return x, y, c, [(1 - x, y), (x, 1 - y), (1 - x, 1 - y)]


def _slot(p):
    return 4 * p[0] + 2 * p[1] + p[2]


class _Job:
    def __init__(self, ins, outs, nsem, copies, aliases=None, local=None):
        self.ins, self.outs, self.nsem, self.copies = list(ins), list(outs), nsem, copies
        self.aliases = aliases or {}
        self.local = local

    def scratch(self):
        s = [pltpu.SemaphoreType.DMA(self.nsem), pltpu.SemaphoreType.DMA(self.nsem)]
        if self.local is not None:
            s.append(pltpu.SemaphoreType.DMA((len(self.ins),)))
        return s

    def start(self, ins, outs, sems):
        if self.local is not None:
            for cp in self.local(ins, outs, sems[2]):
                cp.start()
        for cp in self.copies(ins, outs, sems[0], sems[1])[0]:
            cp.start()

    def finish(self, ins, outs, sems):
        started, awaited = self.copies(ins, outs, sems[0], sems[1])
        for cp in awaited:
            cp.wait_recv()
        for cp in started:
            cp.wait_send()
        if self.local is not None:
            for cp in self.local(ins, outs, sems[2]):
                cp.wait()


class _Jobs:
    def __init__(self, jobs):
        self.jobs = jobs
        self.ins = [a for j in jobs for a in j.ins]
        self.outs = [o for j in jobs for o in j.outs]
        self.aliases = {}
        at_in = at_out = 0
        for j in jobs:
            self.aliases.update({at_in + i: at_out + o for i, o in j.aliases.items()})
            at_in, at_out = at_in + len(j.ins), at_out + len(j.outs)

    def scratch(self):
        return [s for j in self.jobs for s in j.scratch()]

    def _each(self, ins, outs, sems):
        at_in = at_out = at_sem = 0
        for j in self.jobs:
            n_in, n_out, n_sem = len(j.ins), len(j.outs), len(j.scratch())
            yield j, ins[at_in:at_in + n_in], outs[at_out:at_out + n_out], sems[at_sem:at_sem + n_sem]
            at_in, at_out, at_sem = at_in + n_in, at_out + n_out, at_sem + n_sem

    def start(self, ins, outs, sems):
        for j, i, o, s in self._each(ins, outs, sems):
            j.start(i, o, s)

    def finish(self, ins, outs, sems):
        for j, i, o, s in self._each(ins, outs, sems):
            j.finish(i, o, s)

    def split(self, results):
        at, parts = 0, []
        for j in self.jobs:
            parts.append(results[at:at + len(j.outs)])
            at += len(j.outs)
        return parts


def _remote(src, dst, send, recv, idx, to):
    return pltpu.make_async_remote_copy(src_ref=src, dst_ref=dst, send_sem=send.at[idx], recv_sem=recv.at[idx],
                                        device_id=to, device_id_type=MESH)


def _spread_copies(ins, outs, send, recv, base=0):
    x, y, c, chips = _place()
    me = (x, y, c)
    peers = [(x, y, 1 - c)] + [(*chip, c) for chip in chips]
    started, awaited = [], []
    for i, (src, dst) in enumerate(zip(ins, outs)):
        for k, peer in enumerate(peers):
            started.append(_remote(src, dst.at[_slot(me)], send, recv, (base + i, k), peer))
            awaited.append(_remote(src, dst.at[_slot(peer)], send, recv, (base + i, k), peer))
    return started, awaited


def _forward_copies(ins, outs, send, recv, base=0):
    x, y, c, chips = _place()
    started, awaited = [], []
    for i, buf in enumerate(outs):
        for j, chip in enumerate(chips):
            mine, theirs = buf.at[_slot((*chip, c))], buf.at[_slot((*chip, 1 - c))]
            started.append(_remote(mine, mine, send, recv, (base + i, j), (x, y, 1 - c)))
            awaited.append(_remote(theirs, theirs, send, recv, (base + i, j), (x, y, 1 - c)))
    return started, awaited


def _own_block_copies(ins, outs, sems):
    x, y, c, _ = _place()
    return [pltpu.make_async_copy(src, dst.at[_slot((x, y, c))], sems.at[i])
            for i, (src, dst) in enumerate(zip(ins, outs))]


def gather_spread_job(shards):
    outs = [jax.ShapeDtypeStruct((N_DEV, *a.shape), a.dtype) for a in shards]
    return _Job(shards, outs, (len(shards), 4), _spread_copies, local=_own_block_copies)


def gather_forward_job(fulls):
    outs = [jax.ShapeDtypeStruct(a.shape, a.dtype) for a in fulls]
    return _Job(fulls, outs, (len(fulls), 3), _forward_copies, aliases={i: i for i in range(len(fulls))})


def swap_job(gs):
    def copies(ins, outs, send, recv):
        x, y, c, _ = _place()
        started, awaited = [], []
        for i, (g, r1) in enumerate(zip(ins, outs)):
            for q in range(N_CHIP):
                started.append(_remote(g.at[2 * q + (1 - c)], r1.at[q], send, recv, (i, q), (x, y, 1 - c)))
                awaited.append(_remote(g.at[2 * q + c], r1.at[q], send, recv, (i, q), (x, y, 1 - c)))
        return started, awaited

    outs = [jax.ShapeDtypeStruct((N_CHIP, *g.shape[1:]), g.dtype) for g in gs]
    return _Job(gs, outs, (len(gs), N_CHIP), copies)


def chip_exchange_job(ps, rows=None, into=None):
    n = len(ps)

    def copies(ins, outs, send, recv):
        x, y, c, chips = _place()
        started, awaited = [], []
        for i, (p, r2) in enumerate(zip(ins[:n], outs)):
            for k, chip in enumerate(chips):
                src, mine, dst = p.at[2 * chip[0] + chip[1]], p.at[2 * x + y], r2.at[k]
                if rows is not None:
                    src, mine, dst = (t.at[pl.ds(rows[0], rows[1])] for t in (src, mine, dst))
                started.append(_remote(src, dst, send, recv, (i, k), (*chip, c)))
                awaited.append(_remote(mine, dst, send, recv, (i, k), (*chip, c)))
        return started, awaited

    outs = [jax.ShapeDtypeStruct((3, *p.shape[1:]), p.dtype) for p in ps]
    if into is None:
        return _Job(ps, outs, (n, 3), copies)
    return _Job(list(ps) + list(into), outs, (n, 3), copies, aliases={n + i: i for i in range(n)})


def _call(body, job, *, name, grid, in_specs, out_specs, out_shape, args, scratch_shapes=(), vmem=VMEM_LIMIT):
    if job is None:
        res = pl.pallas_call(
            body, name=name, grid=grid, in_specs=in_specs, out_specs=out_specs, out_shape=out_shape,
            scratch_shapes=list(scratch_shapes), compiler_params=_params(("arbitrary",) * len(grid), vmem),
        )(*args)
        return res, []
    n_in, n_out, n_scr = len(in_specs), len(out_specs), len(scratch_shapes)
    j_in, j_out = len(job.ins), len(job.outs)

    def with_copies(*refs):
        at = 0
        ins = refs[at:at + n_in]; at += n_in
        jins = refs[at:at + j_in]; at += j_in
        outs = refs[at:at + n_out]; at += n_out
        jouts = refs[at:at + j_out]; at += j_out
        scr = refs[at:at + n_scr]; at += n_scr
        sems = refs[at:]
        ids = [pl.program_id(d) for d in range(len(grid))]
        first = functools.reduce(jnp.logical_and, [i == 0 for i in ids])
        last = functools.reduce(jnp.logical_and, [i == n - 1 for i, n in zip(ids, grid)])

        @pl.when(first)
        def _():
            job.start(jins, jouts, sems)

        body(*ins, *outs, *scr)

        @pl.when(last)
        def _():
            job.finish(jins, jouts, sems)

    any_spec = pl.BlockSpec(memory_space=pl.ANY)
    res = pl.pallas_call(
        with_copies, name=name, grid=grid,
        in_specs=list(in_specs) + [any_spec] * j_in, out_specs=list(out_specs) + [any_spec] * j_out,
        out_shape=list(out_shape) + list(job.outs),
        input_output_aliases={n_in + i: n_out + o for i, o in job.aliases.items()},
        scratch_shapes=list(scratch_shapes) + job.scratch(),
        compiler_params=_params(("arbitrary",) * len(grid), vmem),
    )(*args, *job.ins)
    return res[:n_out], res[n_out:]


def run_job(job, name):
    def body(*refs):
        j_in, j_out = len(job.ins), len(job.outs)
        ins, outs, sems = refs[:j_in], refs[j_in:j_in + j_out], refs[j_in + j_out:]
        job.start(ins, outs, sems)
        job.finish(ins, outs, sems)

    any_spec = pl.BlockSpec(memory_space=pl.ANY)
    return pl.pallas_call(
        body, name=name, in_specs=[any_spec] * len(job.ins), out_specs=[any_spec] * len(job.outs),
        out_shape=list(job.outs), input_output_aliases=dict(job.aliases), scratch_shapes=job.scratch(),
    )(*job.ins)


def pair_sum(g, r1, name):
    _, rows, cols = g.shape
    rb = next(cand for cand in range(min(rows, 512), 0, -16) if rows % cand == 0)

    def body(g_ref, r1_ref, p_ref, own_ref):
        x, y, c, _ = _place()
        s = g_ref[c].astype(F32) + r1_ref[0].astype(F32)
        p_ref[0] = s.astype(BF16)

        @pl.when(pl.program_id(1) == 2 * x + y)
        def _():
            own_ref[...] = s

    return pl.pallas_call(
        body, name=name, grid=(rows // rb, N_CHIP),
        in_specs=[pl.BlockSpec((2, rb, cols), lambda i, q: (q, i, 0)), pl.BlockSpec((1, rb, cols), lambda i, q: (q, i, 0))],
        out_specs=[pl.BlockSpec((1, rb, cols), lambda i, q: (q, i, 0)), pl.BlockSpec((rb, cols), lambda i, q: (i, 0))],
        out_shape=[jax.ShapeDtypeStruct((N_CHIP, rows, cols), BF16), jax.ShapeDtypeStruct((rows, cols), F32)],
        compiler_params=_params(("arbitrary", "arbitrary")),
    )(g, r1)


def sum_devices(a, name):
    def body(a_ref, o_ref):
        acc = a_ref[0]
        for d in range(1, N_DEV):
            acc = acc + a_ref[d]
        o_ref[...] = acc

    return pl.pallas_call(body, name=name, out_shape=jax.ShapeDtypeStruct(a.shape[1:], F32))(a)


def adamw(w, g, m, v, name, others=None):
    rows, cols = w.shape
    rb = rows
    for cand in range(min(rows, 512), 7, -8):
        if rows % cand == 0 and cand % 8 == 0:
            rb = cand
            break

    def body(*refs):
        if others is None:
            w_ref, g_ref, m_ref, v_ref, d_ref, nm_ref, nv_ref = refs
            gg = g_ref[...]
        else:
            w_ref, g_ref, m_ref, v_ref, r2_ref, go_ref, d_ref, nm_ref, nv_ref = refs
            gg = g_ref[...]
            for k in range(3):
                gg = gg + r2_ref[k].astype(F32)
            go_ref[...] = gg
        nm = ADAM_B1 * m_ref[...] + (1.0 - ADAM_B1) * gg
        nv = ADAM_B2 * v_ref[...] + (1.0 - ADAM_B2) * (gg * gg)
        m_hat = nm / (1.0 - ADAM_B1 ** ADAM_STEP)
        v_hat = nv / (1.0 - ADAM_B2 ** ADAM_STEP)
        d_ref[...] = -ADAM_LR * (m_hat / (jnp.sqrt(v_hat) + ADAM_EPS) + ADAM_WD * w_ref[...])
        nm_ref[...] = nm
        nv_ref[...] = nv

    spec = pl.BlockSpec((rb, cols), lambda i: (i, 0))
    out = jax.ShapeDtypeStruct((rows, cols), F32)
    in_specs, args = [spec] * 4, [w, g, m, v]
    if others is not None:
        in_specs.append(pl.BlockSpec((3, rb, cols), lambda i: (0, i, 0)))
        args.append(others)
    n_out = 3 if others is None else 4
    res = pl.pallas_call(
        body, name=name, grid=(rows // rb,), in_specs=in_specs, out_specs=[spec] * n_out,
        out_shape=[out] * n_out, compiler_params=_params(("parallel",)),
    )(*args)
    return (g, *res) if others is None else tuple(res)


def ada_fwd(c_all, w_cols, b_cols, name):
    def body(c_ref, w_ref, b_ref, cond_ref, mod_ref):
        cc = c_ref[...]
        cond = (cc * _sigmoid(cc)).astype(BF16)
        cond_ref[...] = cond
        mod_ref[...] = _dot(cond, w_ref[...].astype(BF16)) + b_ref[...]

    n, cols = c_all.shape[0], w_cols.shape[1]
    return pl.pallas_call(
        body, name=name,
        out_shape=[jax.ShapeDtypeStruct(c_all.shape, BF16), jax.ShapeDtypeStruct((n, cols), F32)],
        compiler_params=_params(),
    )(c_all, w_cols, b_cols)


def ada_bwd(cond_all, dmod_cols, name):
    def body(c_ref, d_ref, gw_ref, gb_ref):
        d = d_ref[...]
        gw_ref[...] = _dot_tn(c_ref[...], d.astype(BF16))
        gb_ref[...] = jnp.sum(d, axis=0, keepdims=True)

    dm, cols = cond_all.shape[1], dmod_cols.shape[1]
    return pl.pallas_call(
        body, name=name,
        out_shape=[jax.ShapeDtypeStruct((dm, cols), F32), jax.ShapeDtypeStruct((1, cols), F32)],
        compiler_params=_params(),
    )(cond_all, dmod_cols)


MXU_COLS = 256
FFN_CHUNK = 4 * MXU_COLS


def _hidden_chunks(ff):
    assert ff % MXU_COLS == 0
    return [(at, min(FFN_CHUNK, ff - at)) for at in range(0, ff, FFN_CHUNK)]


def _mod_spec(tiles_per_seq, dm):
    return pl.BlockSpec((1, 1, dm), lambda i: (i // tiles_per_seq, 0, 0))


def ffn_fwd(x, sh, sc, gt, wgu, wd, ln_g, ln_b, seq, name, target=None, job=None):
    tokens, dm = x.shape
    ff = wgu.shape[1]
    chunks = _hidden_chunks(ff)
    tm = min(FFN_FWD_TILE, seq)
    tiles_per_seq = seq // tm
    with_loss = target is not None

    def body(*refs):
        if with_loss:
            (x_ref, sh_ref, sc_ref, gt_ref, wgu_ref, wd_ref, lg_ref, lb_ref, t_ref,
             xo_ref, loss_ref, r_ref, gu_ref, f_ref) = refs
        else:
            (x_ref, sh_ref, sc_ref, gt_ref, wgu_ref, wd_ref, lg_ref, lb_ref,
             xo_ref, r_ref, gu_ref, f_ref) = refs
        xx = x_ref[...]
        h = (xx * (1.0 + sc_ref[0]) + sh_ref[0]).astype(BF16)
        acc = jnp.zeros((tm, dm), F32)
        for at, wdt in chunks:
            gk = _dot_nt(h, wgu_ref[0, at:at + wdt, :])
            uk = _dot_nt(h, wgu_ref[1, at:at + wdt, :])
            gu_ref[0, :, at:at + wdt] = gk.astype(BF16)
            gu_ref[1, :, at:at + wdt] = uk.astype(BF16)
            a = (gk * _sigmoid(gk) * uk).astype(BF16)
            acc = acc + _dot(a, wd_ref[at:at + wdt, :])
        f_ref[...] = acc.astype(BF16)
        r = DN_ALPHA * xx + (0.5 * (1.0 + gt_ref[0])) * acc
        r_ref[...] = r
        xhat, _ = _ln_stats(r)
        yy = xhat * lg_ref[...] + lb_ref[...]
        if with_loss:
            err = yy - t_ref[...]
            xo_ref[...] = err * (1.0 / dm)

            @pl.when(pl.program_id(0) == 0)
            def _():
                loss_ref[...] = jnp.zeros_like(loss_ref)

            loss_ref[...] += jnp.full((1, 128), (0.5 / dm) * jnp.sum(err * err), F32)
        else:
            xo_ref[...] = yy

    tile = pl.BlockSpec((tm, dm), lambda i: (i, 0))
    mod = _mod_spec(tiles_per_seq, dm)
    in_specs = [tile, mod, mod, mod, _const_spec(wgu.shape), _const_spec(wd.shape),
                _const_spec((1, dm)), _const_spec((1, dm))]
    args = [x, sh, sc, gt, wgu, wd, ln_g, ln_b]
    out_specs = [tile]
    out_shape = [jax.ShapeDtypeStruct((tokens, dm), F32)]
    if with_loss:
        in_specs.append(tile)
        args.append(target)
        out_specs.append(pl.BlockSpec((1, 128), lambda i: (0, 0)))
        out_shape.append(jax.ShapeDtypeStruct((1, 128), F32))
    out_specs += [tile, pl.BlockSpec((2, tm, ff), lambda i: (0, i, 0)), tile]
    out_shape += [jax.ShapeDtypeStruct((tokens, dm), F32), jax.ShapeDtypeStruct((2, tokens, ff), BF16),
                  jax.ShapeDtypeStruct((tokens, dm), BF16)]
    return _call(body, job, name=name, grid=(tokens // tm,), in_specs=in_specs, out_specs=out_specs,
                 out_shape=out_shape, args=args)


def ffn_bwd(dy, r, x, f, gu, sh, sc, gt, wgu, wd, ln_g, seq, name, job=None):
    tokens, dm = x.shape
    ff = wgu.shape[1]
    chunks = _hidden_chunks(ff)
    tm = min(TOKEN_TILE, seq)
    tiles_per_seq = seq // tm
    nseq = tokens // seq

    def body(dy_ref, r_ref, x_ref, f_ref, gu_ref, sh_ref, sc_ref, gt_ref, wgu_ref, wd_ref, lg_ref,
             dx_ref, dgu_ref, df_ref, a_ref, h_ref, dln_ref, dmod_ref):
        i = pl.program_id(0)
        dr, dgain, dbias = _ln_bwd(dy_ref[...], r_ref[...], lg_ref[...])

        @pl.when(i == 0)
        def _():
            dln_ref[...] = jnp.zeros_like(dln_ref)

        @pl.when(i % tiles_per_seq == 0)
        def _():
            dmod_ref[...] = jnp.zeros_like(dmod_ref)

        dln_ref[0:1, :] += dgain
        dln_ref[1:2, :] += dbias
        df32 = (0.5 * (1.0 + gt_ref[0])) * dr
        df = df32.astype(BF16)
        df_ref[...] = df
        dgate = jnp.sum(dr * (0.5 * f_ref[...].astype(F32)), axis=0, keepdims=True)
        xx = x_ref[...]
        one_sc = 1.0 + sc_ref[0]
        h = (xx * one_sc + sh_ref[0]).astype(BF16)
        h_ref[...] = h
        dh = jnp.zeros((tm, dm), F32)
        for at, wdt in chunks:
            cols = slice(at, at + wdt)
            da = _dot_nt(df, wd_ref[cols, :])
            gk = gu_ref[0, :, cols].astype(F32)
            uk = gu_ref[1, :, cols].astype(F32)
            sg = _sigmoid(gk)
            sil = gk * sg
            a_ref[:, cols] = (sil * uk).astype(BF16)
            du = (da * sil).astype(BF16)
            dg = (da * uk * (sg * (1.0 + gk * (1.0 - sg)))).astype(BF16)
            dgu_ref[0, :, cols] = dg
            dgu_ref[1, :, cols] = du
            dh = dh + _dot(dg, wgu_ref[0, cols, :]) + _dot(du, wgu_ref[1, cols, :])
        dx_ref[...] = DN_ALPHA * dr + dh * one_sc
        dmod_ref[0, 0:1, :] += jnp.sum(dh, axis=0, keepdims=True)
        dmod_ref[0, 1:2, :] += jnp.sum(dh * xx, axis=0, keepdims=True)
        dmod_ref[0, 2:3, :] += dgate

    tile = pl.BlockSpec((tm, dm), lambda i: (i, 0))
    mod = _mod_spec(tiles_per_seq, dm)
    gu_spec = pl.BlockSpec((2, tm, ff), lambda i: (0, i, 0))
    return _call(
        body, job, name=name, grid=(tokens // tm,),
        in_specs=[tile, tile, tile, tile, gu_spec, mod, mod, mod, _const_spec(wgu.shape), _const_spec(wd.shape),
                  _const_spec((1, dm))],
        out_specs=[tile, gu_spec, tile, pl.BlockSpec((tm, ff), lambda i: (i, 0)), tile,
                   pl.BlockSpec((2, dm), lambda i: (0, 0)),
                   pl.BlockSpec((1, 3, dm), lambda i: (i // tiles_per_seq, 0, 0))],
        out_shape=[jax.ShapeDtypeStruct((tokens, dm), F32), jax.ShapeDtypeStruct((2, tokens, ff), BF16),
                   jax.ShapeDtypeStruct((tokens, dm), BF16), jax.ShapeDtypeStruct((tokens, ff), BF16),
                   jax.ShapeDtypeStruct((tokens, dm), BF16), jax.ShapeDtypeStruct((2, dm), F32),
                   jax.ShapeDtypeStruct((nseq, 3, dm), F32)],
        args=(dy, r, x, f, gu, sh, sc, gt, wgu, wd, ln_g))


def tn_matmul(a, b, name, job=None, b_cols=None, a_width=None):
    na, tokens, k_all = a.shape
    kk = k_all if a_width is None else a_width
    nka = k_all // kk
    assert nka * kk == k_all
    nb, _, cc = b.shape
    col = 0
    if b_cols is not None:
        col, cc = b_cols
    tt = tokens
    while 4 * tt * (kk + cc) + 8 * kk * cc > TN_VMEM_BUDGET and tt % 2 == 0 and tt > 256:
        tt //= 2
    steps = tokens // tt

    def body(a_ref, b_ref, o_ref, *acc):
        if steps == 1:
            o_ref[0, 0, 0] = _dot_tn(a_ref[0], b_ref[0]).astype(BF16)
            return
        acc_ref, = acc
        t = pl.program_id(3)

        @pl.when(t == 0)
        def _():
            acc_ref[...] = jnp.zeros_like(acc_ref)

        acc_ref[...] += _dot_tn(a_ref[0], b_ref[0])

        @pl.when(t == steps - 1)
        def _():
            o_ref[0, 0, 0] = acc_ref[...].astype(BF16)

    return _call(
        body, job, name=name, grid=(na, nka, nb, steps),
        in_specs=[pl.BlockSpec((1, tt, kk), lambda i, s, j, t: (i, t, s)),
                  pl.BlockSpec((1, tt, cc), lambda i, s, j, t: (j, t, col))],
        out_specs=[pl.BlockSpec((1, 1, 1, kk, cc), lambda i, s, j, t: (i, s, j, 0, 0))],
        out_shape=[jax.ShapeDtypeStruct((na, nka, nb, kk, cc), BF16)],
        scratch_shapes=[] if steps == 1 else [pltpu.VMEM((kk, cc), F32)], args=(a, b))


def proj_fwd(x1, sh, sc, w_in, seq, name, job=None):
    tokens, dm = x1.shape
    tm = min(MIX_TILE, seq)
    tiles_per_seq = seq // tm
    widths = [N_Q_HEADS * HEAD_DIM, N_KV_HEADS * HEAD_DIM, N_KV_HEADS * HEAD_DIM, 512, 512, 512]
    assert sum(widths) == w_in.shape[0]

    def body(x_ref, sh_ref, sc_ref, w_ref, *outs):
        h = (x_ref[...] * (1.0 + sc_ref[0]) + sh_ref[0]).astype(BF16)
        proj = _dot_nt(h, w_ref[...])
        at = 0
        for o_ref, wdt in zip(outs, widths):
            o_ref[...] = proj[:, at:at + wdt]
            at += wdt

    tile = pl.BlockSpec((tm, dm), lambda i: (i, 0))
    mod = _mod_spec(tiles_per_seq, dm)
    return _call(
        body, job, name=name, grid=(tokens // tm,),
        in_specs=[tile, mod, mod, _const_spec(w_in.shape)],
        out_specs=[pl.BlockSpec((tm, wdt), lambda i: (i, 0)) for wdt in widths],
        out_shape=[jax.ShapeDtypeStruct((tokens, wdt), F32) for wdt in widths],
        args=(x1, sh, sc, w_in))


LANES = 2 * HEAD_DIM


def _head_lane(shape):
    return lax.broadcasted_iota(jnp.int32, shape, 1) % HEAD_DIM


def _lane_half(shape):
    return lax.broadcasted_iota(jnp.int32, shape, 1) // HEAD_DIM


def _swap_rot(v):
    lane = _head_lane(v.shape)
    half = ROT_DIM // 2
    return jnp.where(lane < half, pltpu.roll(v, LANES - half, 1),
                     jnp.where(lane < ROT_DIM, pltpu.roll(v, half, 1), 0.0))


def _rope(v, cos_t, sin_t):
    return v * cos_t + _swap_rot(v) * sin_t


def _unrope(dv, cos_t, sin_t):
    return dv * cos_t + _swap_rot(dv * sin_t)


def _both_halves(t, g):
    return jnp.where(_lane_half(t.shape) == g, t, pltpu.roll(t, HEAD_DIM, 1))


def _fold_halves(t, g):
    return jnp.where(_lane_half(t.shape) == g, t + pltpu.roll(t, HEAD_DIM, 1), 0.0)


def _stack_heads(blocks):
    rows = []
    for blk in blocks:
        half = _lane_half(blk.shape)
        rows += [jnp.where(half == 0, blk, 0.0), jnp.where(half == 1, blk, 0.0)]
    return jnp.concatenate(rows, axis=0)


def _unstack_heads(t, j):
    lo = t[(2 * j) * ATTN_BLOCK:(2 * j + 1) * ATTN_BLOCK]
    hi = t[(2 * j + 1) * ATTN_BLOCK:(2 * j + 2) * ATTN_BLOCK]
    return jnp.where(_lane_half(lo.shape) == 0, lo, hi)


def _band_mask(q0, w0):
    rows, cols = GQA_GROUP * ATTN_BLOCK, 2 * ATTN_BLOCK
    qi = lax.broadcasted_iota(jnp.int32, (rows, cols), 0) % ATTN_BLOCK + q0
    ki = lax.broadcasted_iota(jnp.int32, (rows, cols), 1) + w0
    diff = qi - ki
    return (diff >= 0) & (diff < ATTN_BLOCK)


def _attn_specs(seq):
    q_spec = pl.BlockSpec((seq, GQA_GROUP * HEAD_DIM), lambda b, g: (b, g))
    kv_spec = pl.BlockSpec((seq, LANES), lambda b, g: (b, 0))
    sink_spec = pl.BlockSpec((1, GQA_GROUP * ATTN_BLOCK, 1), lambda b, g: (g, 0, 0))
    return q_spec, kv_spec, sink_spec


def _block_starts(n):
    q0 = pl.multiple_of(n * ATTN_BLOCK, ATTN_BLOCK)
    w0 = pl.multiple_of(jnp.maximum(n - 1, 0) * ATTN_BLOCK, ATTN_BLOCK)
    return q0, w0


def _stacked_queries(ref, rows):
    return _stack_heads([ref[rows, j * LANES:(j + 1) * LANES] for j in range(2)]).astype(BF16)


def _sink_columns(sinks):
    return jnp.repeat(sinks.reshape(N_KV_HEADS, GQA_GROUP), ATTN_BLOCK, axis=1)[:, :, None]


def _probs_spec(nblk):
    return pl.BlockSpec((1, 1, nblk, GQA_GROUP * ATTN_BLOCK, 2 * ATTN_BLOCK), lambda b, g: (b, g, 0, 0, 0))


def _sink_probs_spec():
    return pl.BlockSpec((1, 1, GQA_GROUP * ATTN_BLOCK, LANES), lambda b, g: (b, g, 0, 0))


def attn_fwd(q, k, v, cos_t, sin_t, sinks, seq, name, job=None):
    tokens = q.shape[0]
    nblk = seq // ATTN_BLOCK
    assert nblk >= 2
    scale = HEAD_DIM ** -0.5

    nseq = tokens // seq
    rows_stacked = GQA_GROUP * ATTN_BLOCK
    assert nblk <= LANES

    def body(q_ref, k_ref, v_ref, cos_ref, sin_ref, sink_ref, o_ref, qr_ref, p_ref, ps_ref, kd_ref, vd_ref):
        g = pl.program_id(1)
        kd_ref[...] = _both_halves(_rope(k_ref[...], cos_ref[...], sin_ref[...]), g).astype(BF16)
        vd_ref[...] = _both_halves(v_ref[...], g).astype(BF16)
        sink = sink_ref[0]
        lane = lax.broadcasted_iota(jnp.int32, (rows_stacked, LANES), 1)

        ps_ref[...] = jnp.zeros_like(ps_ref)

        def block(n, carry):
            q0, w0 = _block_starts(n)
            rows, win = pl.ds(q0, ATTN_BLOCK), pl.ds(w0, 2 * ATTN_BLOCK)
            blocks = []
            for j in range(2):
                qr = _rope(q_ref[rows, j * LANES:(j + 1) * LANES], cos_ref[rows, :], sin_ref[rows, :]).astype(BF16)
                qr_ref[rows, j * LANES:(j + 1) * LANES] = qr
                blocks.append(qr)
            qs = _stack_heads(blocks)
            s = _dot_nt(qs, kd_ref[win, :]) * scale
            s = jnp.where(_band_mask(q0, w0), s, NEG_BIG)
            m = jnp.maximum(jnp.max(s, axis=-1, keepdims=True), sink)
            p = jnp.exp(s - m)
            e_sink = jnp.exp(sink - m)
            inv = pl.reciprocal(jnp.sum(p, axis=-1, keepdims=True) + e_sink, approx=True)
            pn = (p * inv).astype(BF16)
            p_ref[0, 0, n] = pn
            out = _dot(pn, vd_ref[win, :])
            for j in range(2):
                o_ref[rows, j * LANES:(j + 1) * LANES] = _unstack_heads(out, j).astype(o_ref.dtype)
            ps_ref[0, 0] = jnp.where(lane == n, e_sink * inv, ps_ref[0, 0])
            return carry

        lax.fori_loop(0, nblk, block, 0, unroll=2)

    q_spec, kv_spec, sink_spec = _attn_specs(seq)
    return _call(
        body, job, name=name, grid=(nseq, N_KV_HEADS),
        in_specs=[q_spec, kv_spec, kv_spec, kv_spec, kv_spec, sink_spec],
        out_specs=[q_spec, q_spec, _probs_spec(nblk), _sink_probs_spec()],
        out_shape=[jax.ShapeDtypeStruct(q.shape, BF16), jax.ShapeDtypeStruct(q.shape, BF16),
                   jax.ShapeDtypeStruct((nseq, N_KV_HEADS, nblk, rows_stacked, 2 * ATTN_BLOCK), BF16),
                   jax.ShapeDtypeStruct((nseq, N_KV_HEADS, rows_stacked, LANES), F32)],
        scratch_shapes=[pltpu.VMEM((seq, LANES), BF16), pltpu.VMEM((seq, LANES), BF16)],
        args=(q, k, v, cos_t, sin_t, _sink_columns(sinks)))


def attn_bwd(qr, k, v, do, probs, sink_probs, cos_t, sin_t, seq, name, job=None):
    tokens = qr.shape[0]
    nseq = tokens // seq
    nblk = seq // ATTN_BLOCK
    assert nblk >= 2
    rows_stacked = GQA_GROUP * ATTN_BLOCK
    scale = HEAD_DIM ** -0.5

    def body(q_ref, k_ref, v_ref, do_ref, p_ref, ps_ref, cos_ref, sin_ref, dq_ref, dk_ref, dv_ref, ds_ref,
             kd_ref, vd_ref, dkd_ref, dvd_ref, acc_ref):
        g = pl.program_id(1)
        kd_ref[...] = _both_halves(_rope(k_ref[...], cos_ref[...], sin_ref[...]), g).astype(BF16)
        vd_ref[...] = _both_halves(v_ref[...], g).astype(BF16)
        dkd_ref[...] = jnp.zeros_like(dkd_ref)
        dvd_ref[...] = jnp.zeros_like(dvd_ref)
        acc_ref[...] = jnp.zeros_like(acc_ref)
        lane = lax.broadcasted_iota(jnp.int32, (rows_stacked, LANES), 1)

        def block(n, carry):
            q0, w0 = _block_starts(n)
            rows, win = pl.ds(q0, ATTN_BLOCK), pl.ds(w0, 2 * ATTN_BLOCK)
            qs = _stacked_queries(q_ref, rows)
            dos = _stacked_queries(do_ref, rows)
            kw, vw = kd_ref[win, :], vd_ref[win, :]
            pn16 = p_ref[0, 0, n]
            pn = pn16.astype(F32)
            dvd_ref[win, :] += _dot_tn(pn16, dos)
            dp = _dot_nt(dos, vw)
            delta = jnp.sum(dp * pn, axis=-1, keepdims=True)
            ds = (pn * (dp - delta)).astype(BF16)
            dqs = _dot(ds, kw) * scale
            dkd_ref[win, :] += _dot_tn(ds, qs) * scale
            cos_b, sin_b = cos_ref[rows, :], sin_ref[rows, :]
            for j in range(2):
                dq_ref[rows, j * LANES:(j + 1) * LANES] = _unrope(_unstack_heads(dqs, j), cos_b, sin_b).astype(BF16)
            acc_ref[...] += jnp.where(lane == n, ps_ref[0, 0] * delta, 0.0)
            return carry

        lax.fori_loop(0, nblk // 2, lambda i, carry: block(2 * i + 1, block(2 * i, carry)), 0)
        ds_ref[0, 0] = -jnp.sum(acc_ref[...], axis=-1, keepdims=True)
        dk_g = _unrope(_fold_halves(dkd_ref[...], g), cos_ref[...], sin_ref[...])
        dv_g = _fold_halves(dvd_ref[...], g)

        @pl.when(g == 0)
        def _():
            dk_ref[...] = dk_g
            dv_ref[...] = dv_g

        @pl.when(g != 0)
        def _():
            dk_ref[...] += dk_g
            dv_ref[...] += dv_g

    q_spec, kv_spec, _ = _attn_specs(seq)
    return _call(
        body, job, name=name, grid=(nseq, N_KV_HEADS),
        in_specs=[q_spec, kv_spec, kv_spec, q_spec, _probs_spec(nblk), _sink_probs_spec(), kv_spec, kv_spec],
        out_specs=[q_spec, kv_spec, kv_spec, pl.BlockSpec((1, 1, rows_stacked, 1), lambda b, g: (b, g, 0, 0))],
        out_shape=[jax.ShapeDtypeStruct(qr.shape, BF16), jax.ShapeDtypeStruct(k.shape, F32),
                   jax.ShapeDtypeStruct(k.shape, F32), jax.ShapeDtypeStruct((nseq, N_KV_HEADS, rows_stacked, 1), F32)],
        scratch_shapes=[pltpu.VMEM((seq, LANES), BF16), pltpu.VMEM((seq, LANES), BF16),
                        pltpu.VMEM((seq, LANES), F32), pltpu.VMEM((seq, LANES), F32),
                        pltpu.VMEM((rows_stacked, LANES), F32)],
        args=(qr, k, v, do, probs, sink_probs, cos_t, sin_t))


CONV_COLS = 128


def _shift_down(z, by):
    t = lax.broadcasted_iota(jnp.int32, z.shape, 0)
    return jnp.where(t >= by, pltpu.roll(z, by, 0), 0.0)


def _shift_up(z, by):
    n = z.shape[0]
    t = lax.broadcasted_iota(jnp.int32, z.shape, 0)
    return jnp.where(t < n - by, pltpu.roll(z, n - by, 0), 0.0)


def conv_fwd(u, bg, cg, conv_w, seq, name):
    tokens, width = u.shape

    def body(u_ref, bg_ref, cg_ref, w_ref, o_ref):
        z = cg_ref[...] * u_ref[...]
        yy = w_ref[2:3, :] * z + w_ref[1:2, :] * _shift_down(z, 1) + w_ref[0:1, :] * _shift_down(z, 2)
        o_ref[...] = (bg_ref[...] * yy).astype(BF16)

    col = pl.BlockSpec((seq, CONV_COLS), lambda j, b: (b, j))
    return pl.pallas_call(
        body, name=name, grid=(width // CONV_COLS, tokens // seq),
        in_specs=[col, col, col, pl.BlockSpec((CONV_TAPS, CONV_COLS), lambda j, b: (0, j))],
        out_specs=col, out_shape=jax.ShapeDtypeStruct((tokens, width), BF16),
        compiler_params=_params(("parallel", "parallel")),
    )(u, bg, cg, conv_w)


def conv_bwd(dout, u, bg, cg, conv_w, seq, name):
    tokens, width = u.shape

    def body(do_ref, u_ref, bg_ref, cg_ref, w_ref, du_ref, dbg_ref, dcg_ref, dw_ref):
        uu, cg_v, do = u_ref[...], cg_ref[...], do_ref[...].astype(F32)
        z = cg_v * uu
        z1, z2 = _shift_down(z, 1), _shift_down(z, 2)
        yy = w_ref[2:3, :] * z + w_ref[1:2, :] * z1 + w_ref[0:1, :] * z2
        dbg_ref[...] = (do * yy).astype(BF16)
        dyy = do * bg_ref[...]
        dz = w_ref[2:3, :] * dyy + w_ref[1:2, :] * _shift_up(dyy, 1) + w_ref[0:1, :] * _shift_up(dyy, 2)
        du_ref[...] = (dz * cg_v).astype(BF16)
        dcg_ref[...] = (dz * uu).astype(BF16)

        @pl.when(pl.program_id(1) == 0)
        def _():
            dw_ref[...] = jnp.zeros_like(dw_ref)

        dw_ref[0:1, :] += jnp.sum(dyy * z2, axis=0, keepdims=True)
        dw_ref[1:2, :] += jnp.sum(dyy * z1, axis=0, keepdims=True)
        dw_ref[2:3, :] += jnp.sum(dyy * z, axis=0, keepdims=True)

    col = pl.BlockSpec((seq, CONV_COLS), lambda j, b: (b, j))
    w_spec = pl.BlockSpec((CONV_TAPS, CONV_COLS), lambda j, b: (0, j))
    act = jax.ShapeDtypeStruct((tokens, width), BF16)
    return pl.pallas_call(
        body, name=name, grid=(width // CONV_COLS, tokens // seq),
        in_specs=[col, col, col, col, w_spec], out_specs=[col, col, col, w_spec],
        out_shape=[act, act, act, jax.ShapeDtypeStruct((CONV_TAPS, width), F32)],
        compiler_params=_params(("parallel", "arbitrary")),
    )(dout, u, bg, cg, conv_w)


def out_fwd(x1, attn, conv, gt, w_out, ln_g, ln_b, seq, name, job=None):
    tokens, dm = x1.shape
    half = attn.shape[1]
    tm = min(MIX_TILE, seq)
    tiles_per_seq = seq // tm

    def body(x_ref, a_ref, c_ref, gt_ref, w_ref, lg_ref, lb_ref, xo_ref, r_ref, mi_ref, mix_ref):
        mixin = jnp.concatenate([a_ref[...], c_ref[...]], axis=1).astype(BF16)
        mi_ref[...] = mixin
        mix = _dot(mixin, w_ref[...])
        mix_ref[...] = mix.astype(BF16)
        r = DN_ALPHA * x_ref[...] + (1.0 + gt_ref[0]) * mix
        r_ref[...] = r
        xhat, _ = _ln_stats(r)
        xo_ref[...] = xhat * lg_ref[...] + lb_ref[...]

    tile = pl.BlockSpec((tm, dm), lambda i: (i, 0))
    htile = pl.BlockSpec((tm, half), lambda i: (i, 0))
    return _call(
        body, job, name=name, grid=(tokens // tm,),
        in_specs=[tile, htile, htile, _mod_spec(tiles_per_seq, dm), _const_spec(w_out.shape),
                  _const_spec((1, dm)), _const_spec((1, dm))],
        out_specs=[tile, tile, tile, tile],
        out_shape=[jax.ShapeDtypeStruct((tokens, dm), F32), jax.ShapeDtypeStruct((tokens, dm), F32),
                   jax.ShapeDtypeStruct((tokens, dm), BF16), jax.ShapeDtypeStruct((tokens, dm), BF16)],
        args=(x1, attn, conv, gt, w_out, ln_g, ln_b))


def out_bwd(dy, r, mix, gt, w_out, ln_g, seq, name, job=None):
    tokens, dm = r.shape
    half = dm // 2
    tm = min(MIX_TILE, seq)
    tiles_per_seq = seq // tm
    nseq = tokens // seq

    def body(dy_ref, r_ref, mix_ref, gt_ref, w_ref, lg_ref, dres_ref, da_ref, dc_ref, dmix_ref, dln_ref, dgt_ref):
        i = pl.program_id(0)
        dr, dgain, dbias = _ln_bwd(dy_ref[...], r_ref[...], lg_ref[...])

        @pl.when(i == 0)
        def _():
            dln_ref[...] = jnp.zeros_like(dln_ref)

        @pl.when(i % tiles_per_seq == 0)
        def _():
            dgt_ref[...] = jnp.zeros_like(dgt_ref)

        dln_ref[0:1, :] += dgain
        dln_ref[1:2, :] += dbias
        dgt_ref[0] += jnp.sum(dr * mix_ref[...].astype(F32), axis=0, keepdims=True)
        dres_ref[...] = DN_ALPHA * dr
        dmix = ((1.0 + gt_ref[0]) * dr).astype(BF16)
        dmix_ref[...] = dmix
        dmixin = _dot_nt(dmix, w_ref[...])
        da_ref[...] = dmixin[:, :half].astype(BF16)
        dc_ref[...] = dmixin[:, half:].astype(BF16)

    tile = pl.BlockSpec((tm, dm), lambda i: (i, 0))
    htile = pl.BlockSpec((tm, half), lambda i: (i, 0))
    return _call(
        body, job, name=name, grid=(tokens // tm,),
        in_specs=[tile, tile, tile, _mod_spec(tiles_per_seq, dm), _const_spec(w_out.shape), _const_spec((1, dm))],
        out_specs=[tile, htile, htile, tile, pl.BlockSpec((2, dm), lambda i: (0, 0)),
                   pl.BlockSpec((1, 1, dm), lambda i: (i // tiles_per_seq, 0, 0))],
        out_shape=[jax.ShapeDtypeStruct((tokens, dm), F32), jax.ShapeDtypeStruct((tokens, half), BF16),
                   jax.ShapeDtypeStruct((tokens, half), BF16), jax.ShapeDtypeStruct((tokens, dm), BF16),
                   jax.ShapeDtypeStruct((2, dm), F32), jax.ShapeDtypeStruct((nseq, 1, dm), F32)],
        args=(dy, r, mix, gt, w_out, ln_g))


def proj_bwd(parts, dres, x1, sh, sc, w_in, seq, name, job=None):
    tokens, dm = x1.shape
    tm = min(MIX_TILE, seq)
    tiles_per_seq = seq // tm
    nseq = tokens // seq
    widths = [p.shape[1] for p in parts]
    total = sum(widths)

    def body(*refs):
        part_refs = refs[:6]
        dres_ref, x_ref, sh_ref, sc_ref, w_ref, dx_ref, dproj_ref, h_ref, dmod_ref = refs[6:]
        dproj = jnp.concatenate([p[...].astype(BF16) for p in part_refs], axis=1)
        dproj_ref[...] = dproj
        dh = _dot(dproj, w_ref[...])
        xx = x_ref[...]
        one_sc = 1.0 + sc_ref[0]
        h_ref[...] = (xx * one_sc + sh_ref[0]).astype(BF16)
        dx_ref[...] = dres_ref[...] + dh * one_sc

        @pl.when(pl.program_id(0) % tiles_per_seq == 0)
        def _():
            dmod_ref[...] = jnp.zeros_like(dmod_ref)

        dmod_ref[0, 0:1, :] += jnp.sum(dh, axis=0, keepdims=True)
        dmod_ref[0, 1:2, :] += jnp.sum(dh * xx, axis=0, keepdims=True)

    tile = pl.BlockSpec((tm, dm), lambda i: (i, 0))
    mod = _mod_spec(tiles_per_seq, dm)
    return _call(
        body, job, name=name, grid=(tokens // tm,),
        in_specs=[pl.BlockSpec((tm, wdt), lambda i: (i, 0)) for wdt in widths]
        + [tile, tile, mod, mod, _const_spec(w_in.shape)],
        out_specs=[tile, pl.BlockSpec((tm, total), lambda i: (i, 0)), tile,
                   pl.BlockSpec((1, 2, dm), lambda i: (i // tiles_per_seq, 0, 0))],
        out_shape=[jax.ShapeDtypeStruct((tokens, dm), F32), jax.ShapeDtypeStruct((tokens, total), BF16),
                   jax.ShapeDtypeStruct((tokens, dm), BF16), jax.ShapeDtypeStruct((nseq, 2, dm), F32)],
        args=(*parts, dres, x1, sh, sc, w_in))


def _rope_tables(positions):
    half = ROT_DIM // 2
    inv_freq = jnp.power(jnp.float32(ROPE_THETA), -jnp.arange(0, ROT_DIM, 2, dtype=F32) / ROT_DIM)
    lane = jnp.arange(LANES) % HEAD_DIM
    freq = jnp.where(lane < ROT_DIM, inv_freq[lane % half], 0.0)
    sign = jnp.where(lane < half, -1.0, 1.0).astype(F32)
    ang = positions.astype(F32)[:, None] * freq[None, :]
    return jnp.cos(ang), sign[None, :] * jnp.sin(ang)


def kernel(x, c, positions, w_ada, b_ada, ffn1_w_gate_up, ffn1_w_down, ln1_g, ln1_b, w_in, conv_w, attn_sinks, w_out, ln2_g, ln2_b, ffn2_w_gate_up, ffn2_w_down, ln3_g, ln3_b, loss_target, m_w_ada, m_b_ada, m_ffn1_w_gate_up, m_ffn1_w_down, m_ln1_g, m_ln1_b, m_w_in, m_conv_w, m_attn_sinks, m_w_out, m_ln2_g, m_ln2_b, m_ffn2_w_gate_up, m_ffn2_w_down, m_ln3_g, m_ln3_b, v_w_ada, v_b_ada, v_ffn1_w_gate_up, v_ffn1_w_down, v_ln1_g, v_ln1_b, v_w_in, v_conv_w, v_attn_sinks, v_w_out, v_ln2_g, v_ln2_b, v_ffn2_w_gate_up, v_ffn2_w_down, v_ln3_g, v_ln3_b):
    nseq, seq, dm = x.shape
    tokens = nseq * seq
    dev = 4 * lax.axis_index("x") + 2 * lax.axis_index("y") + lax.axis_index("c")
    ada_cols = w_ada.shape[2]
    ff = ffn1_w_down.shape[1] * N_DEV
    fc = ff // 4
    in_cols = w_in.shape[2]
    conv_cols = conv_w.shape[2]

    def t_bf16(w):
        return w[0].T.astype(BF16)

    c_all, convw_all = all_gather([c, conv_w[0]], "gather_cond")
    wgu1, wd1 = all_gather([t_bf16(ffn1_w_gate_up), ffn1_w_down[0].astype(BF16)], "gather_ffn1")
    c_all = c_all.reshape(N_DEV * nseq, dm)
    convw_full = convw_all.transpose(1, 0, 2).reshape(CONV_TAPS, N_DEV * conv_cols)
    wgu1, wd1 = wgu1.reshape(2, ff, dm), wd1.reshape(ff, dm)

    b_cols = lax.dynamic_slice(b_ada, (0, dev * ada_cols), (1, ada_cols))
    cond_all, mod_cols = ada_fwd(c_all, w_ada[0], b_cols, "ada_fwd")
    (mod_all,) = all_gather([mod_cols], "gather_mod")
    mod = lax.dynamic_slice(mod_all, (0, dev * nseq, 0), (N_DEV, nseq, ada_cols))
    mod = mod.transpose(1, 0, 2).reshape(nseq, 9, 1, dm)
    sh1, sc1, g1, sh2, sc2, g2, sh3, sc3, g3 = [mod[:, i] for i in range(9)]

    x0 = x.reshape(tokens, dm)
    spread = gather_spread_job([t_bf16(w_in), w_out[0].astype(BF16), ffn2_w_down[0].astype(BF16)])
    (x1, r1, gu1, f1), spread = ffn_fwd(x0, sh1, sc1, g1, wgu1, wd1, ln1_g, ln1_b, seq, "ffn1_fwd", job=spread)
    win, wout = run_job(gather_forward_job(spread[:2]), "gather_mix_forward")
    win = win.reshape(N_DEV * in_cols, dm)
    wout = wout.reshape(dm, dm)
    (q, k, v, u, bg, cg), (wd2,) = proj_fwd(x1, sh2, sc2, win, seq, "proj_fwd", job=gather_forward_job(spread[2:]))
    wd2 = wd2.reshape(ff, dm)
    cos_t, sin_t = _rope_tables(positions.reshape(tokens))
    sinks = attn_sinks[0]
    (attn, q_rot, probs, sink_probs), spread = attn_fwd(q, k, v, cos_t, sin_t, sinks, seq, "attn_fwd",
                                                        job=gather_spread_job([t_bf16(ffn2_w_gate_up)]))
    conv = conv_fwd(u, bg, cg, convw_full, seq, "conv_fwd")
    (x2, r2, mixin, mix), (wgu2,) = out_fwd(x1, attn, conv, g2, wout, ln2_g, ln2_b, seq, "out_fwd",
                                            job=gather_forward_job(spread))
    wgu2 = wgu2.reshape(2, ff, dm)
    target = loss_target.reshape(tokens, dm)
    (dy3, loss_part, r3, gu3, f3), _ = ffn_fwd(x2, sh3, sc3, g3, wgu2, wd2, ln3_g, ln3_b, seq, "ffn2_fwd", target=target)

    (dx2, dgu3, df3, a3, h3, dln3, dmod3), _ = ffn_bwd(dy3, r3, x2, f3, gu3, sh3, sc3, g3, wgu2, wd2, ln3_g, seq, "ffn2_bwd")
    pair = 2 * fc
    g_wd2 = tn_matmul(a3[None], df3[None], "ffn2_dwd", a_width=pair)[0][0].reshape(N_DEV, ff // N_DEV, dm)
    g_wgu2 = tn_matmul(dgu3, h3[None], "ffn2_dwgu", a_width=pair)[0][0].reshape(N_DEV, fc, dm)
    (dres2, dattn, dconv, dmix, dln2, dg2), swapped = out_bwd(dx2, r2, mix, g2, wout, ln2_g, seq, "out_bwd",
                                                              job=swap_job([g_wgu2, g_wd2]))
    p_wgu2, own_wgu2 = pair_sum(g_wgu2, swapped[0], "pair_wgu2")
    p_wd2, own_wd2 = pair_sum(g_wd2, swapped[1], "pair_wd2")
    du, dbg, dcg, dconvw = conv_bwd(dconv, u, bg, cg, convw_full, seq, "conv_bwd")
    (dq, dk, dv, dsink_rows), (far_wd2,) = attn_bwd(
        q_rot, k, v, dattn, probs, sink_probs, cos_t, sin_t, seq, "attn_bwd", job=chip_exchange_job([p_wd2]))
    parts = [dq, dk, dv, du, dbg, dcg]
    (dx1, dproj, h2, dmod2), far_top = proj_bwd(parts, dres2, x1, sh2, sc2, win, seq, "proj_bwd",
                                                job=chip_exchange_job([p_wgu2], rows=(0, fc // 2)))
    (dx0, dgu1, df1, a1, h1, dln1, dmod1), _ = ffn_bwd(
        dx1, r1, x0, f1, gu1, sh1, sc1, g1, wgu1, wd1, ln1_g, seq, "ffn1_bwd")

    dmod = jnp.concatenate([dmod1, dmod2, dg2, dmod3], axis=1).reshape(nseq, 9 * dm)
    half = dm // 2
    jobs = _Jobs([gather_spread_job([dmod]),
                  chip_exchange_job([p_wgu2], rows=(fc // 2, fc // 2), into=far_top)])
    (g_wd1,), res = tn_matmul(a1[None], df1[None], "ffn1_dwd", job=jobs, a_width=pair)
    dmod_spread, (far_wgu2,) = jobs.split(res)
    g_wd1 = g_wd1.reshape(N_DEV, ff // N_DEV, dm)
    jobs = _Jobs([swap_job([g_wd1]), gather_forward_job(dmod_spread)])
    (g_l,), res = tn_matmul(dgu1, h1[None], "ffn1_dwgu_l", job=jobs, b_cols=(0, half), a_width=pair)
    (sw_wd1,), (dmod_all,) = jobs.split(res)
    g_l = g_l.reshape(N_DEV, fc, half)
    p_wd1, own_wd1 = pair_sum(g_wd1, sw_wd1, "pair_wd1")
    jobs = _Jobs([chip_exchange_job([p_wd1]), swap_job([g_l])])
    (g_r,), res = tn_matmul(dgu1, h1[None], "ffn1_dwgu_r", job=jobs, b_cols=(1, half), a_width=pair)
    (far_wd1,), (sw_l,) = jobs.split(res)
    g_r = g_r.reshape(N_DEV, fc, half)
    p_l, own_l = pair_sum(g_l, sw_l, "pair_wgu1_l")
    jobs = _Jobs([chip_exchange_job([p_l]), swap_job([g_r])])
    (g_win,), res = tn_matmul(dproj[None], h2[None], "dwin", job=jobs)
    (far_l,), (sw_r,) = jobs.split(res)
    g_win = g_win.reshape(N_DEV, in_cols, dm)
    p_r, own_r = pair_sum(g_r, sw_r, "pair_wgu1_r")
    jobs = _Jobs([chip_exchange_job([p_r]), swap_job([g_win])])
    (g_wout,), res = tn_matmul(mixin[None], dmix[None], "dwout", job=jobs)
    (far_r,), (sw_win,) = jobs.split(res)
    g_wout = g_wout.reshape(N_DEV, dm // N_DEV, dm)
    p_win, own_win = pair_sum(g_win, sw_win, "pair_win")
    jobs = _Jobs([chip_exchange_job([p_win]), swap_job([g_wout])])
    (far_win,), (sw_wout,) = jobs.split(run_job(jobs, "rs_tail_win"))
    p_wout, own_wout = pair_sum(g_wout, sw_wout, "pair_wout")
    (far_wout,) = run_job(chip_exchange_job([p_wout]), "rs_tail_wout")

    grads = {
        "ffn1_w_gate_up": jnp.concatenate([own_l, own_r], axis=1), "ffn1_w_down": own_wd1,
        "w_in": own_win, "w_out": own_wout, "ffn2_w_gate_up": own_wgu2, "ffn2_w_down": own_wd2,
    }
    others = {"ffn1_w_gate_up": jnp.concatenate([far_l, far_r], axis=2), "ffn1_w_down": far_wd1,
              "w_in": far_win, "w_out": far_wout, "ffn2_w_gate_up": far_wgu2, "ffn2_w_down": far_wd2}

    dmod_cols = lax.dynamic_slice(dmod_all.reshape(N_DEV * nseq, 9 * dm), (0, dev * ada_cols), (N_DEV * nseq, ada_cols))
    grads["w_ada"], gb_cols = ada_bwd(cond_all, dmod_cols, "ada_bwd")

    dsinks = jnp.sum(dsink_rows.reshape(nseq, N_Q_HEADS, ATTN_BLOCK), axis=(0, 2))
    small = jnp.zeros((8, dm), F32)
    small = small.at[0:2].set(dln1).at[2:4].set(dln2).at[4:6].set(dln3)
    small = small.at[6, 0:N_Q_HEADS].set(dsinks).at[7, 0].set(loss_part[0, 0])
    small_all, dconvw_all, gb_all = all_gather([small, dconvw, gb_cols], "gather_small")
    small_sum = sum_devices(small_all, "sum_small")
    dconvw_sum = sum_devices(dconvw_all, "sum_convw")
    loss = small_sum[7, 0]
    grads["b_ada"] = gb_all.reshape(1, N_DEV * ada_cols)
    grads["conv_w"] = lax.dynamic_slice(dconvw_sum, (0, dev * conv_cols), (CONV_TAPS, conv_cols))
    grads["attn_sinks"] = small_sum[6:7, 0:N_Q_HEADS]
    for i, nm in enumerate(["ln1_g", "ln1_b", "ln2_g", "ln2_b", "ln3_g", "ln3_b"]):
        grads[nm] = small_sum[i:i + 1]

    given = dict(w_ada=(w_ada, m_w_ada, v_w_ada), b_ada=(b_ada, m_b_ada, v_b_ada),
                 ffn1_w_gate_up=(ffn1_w_gate_up, m_ffn1_w_gate_up, v_ffn1_w_gate_up),
                 ffn1_w_down=(ffn1_w_down, m_ffn1_w_down, v_ffn1_w_down),
                 ln1_g=(ln1_g, m_ln1_g, v_ln1_g), ln1_b=(ln1_b, m_ln1_b, v_ln1_b),
                 w_in=(w_in, m_w_in, v_w_in), conv_w=(conv_w, m_conv_w, v_conv_w),
                 attn_sinks=(attn_sinks, m_attn_sinks, v_attn_sinks), w_out=(w_out, m_w_out, v_w_out),
                 ln2_g=(ln2_g, m_ln2_g, v_ln2_g), ln2_b=(ln2_b, m_ln2_b, v_ln2_b),
                 ffn2_w_gate_up=(ffn2_w_gate_up, m_ffn2_w_gate_up, v_ffn2_w_gate_up),
                 ffn2_w_down=(ffn2_w_down, m_ffn2_w_down, v_ffn2_w_down),
                 ln3_g=(ln3_g, m_ln3_g, v_ln3_g), ln3_b=(ln3_b, m_ln3_b, v_ln3_b))
    order = ["w_ada", "b_ada", "ffn1_w_gate_up", "ffn1_w_down", "ln1_g", "ln1_b", "w_in", "conv_w", "attn_sinks",
             "w_out", "ln2_g", "ln2_b", "ffn2_w_gate_up", "ffn2_w_down", "ln3_g", "ln3_b"]
    transposed = ("ffn1_w_gate_up", "ffn2_w_gate_up", "w_in")
    out_g, out_d, out_m, out_v = [], [], [], []
    for nm in order:
        shape = given[nm][0].shape
        two_d = (shape[-2], shape[-1])
        if nm in transposed:
            w2, m2, v2 = [t[0].T for t in given[nm]]
            back = lambda t: t.T[None]
        else:
            w2, m2, v2 = [t.reshape(two_d) for t in given[nm]]
            back = lambda t, shape=shape: t.reshape(shape)
        res = adamw(w2, grads[nm].reshape(w2.shape), m2, v2, "adamw_" + nm, others=others.get(nm))
        for lst, t in zip((out_g, out_d, out_m, out_v), res):
            lst.append(back(t))
    grad_x = dx0.reshape(nseq, seq, dm)
    return (loss, grad_x, *out_g, *out_d, *out_m, *out_v)
```

```python
import functools

import jax
import jax.numpy as jnp
from jax import lax
from jax.experimental import pallas as pl
from jax.experimental.pallas import tpu as pltpu

F32 = jnp.float32
BF16 = jnp.bfloat16
MESH = pl.DeviceIdType.MESH

N_DEV = 8
N_CHIP = 4
HEAD_DIM = 64
N_Q_HEADS = 8
N_KV_HEADS = 2
GQA_GROUP = N_Q_HEADS // N_KV_HEADS
ATTN_BLOCK = 128
ROT_DIM = 16
ROPE_THETA = 500000.0
CONV_TAPS = 3
LN_EPS = 1e-5
DN_ALPHA = 2.0 ** 0.25
ADAM_LR = 0.001
ADAM_B1 = 0.9
ADAM_B2 = 0.999
ADAM_EPS = 1e-08
ADAM_WD = 0.01
ADAM_STEP = 10
NEG_BIG = -1e30

VMEM_LIMIT = 56 * 1024 * 1024
TOKEN_TILE = 256
FFN_FWD_TILE = 512
MIX_TILE = 512
TN_VMEM_BUDGET = 36 * 1024 * 1024


def _params(semantics=None, vmem=VMEM_LIMIT):
    return pltpu.CompilerParams(dimension_semantics=semantics, vmem_limit_bytes=vmem)


def _dot(a, b):
    return jnp.dot(a, b, preferred_element_type=F32)


def _dot_nt(a, b):
    return lax.dot_general(a, b, (((1,), (1,)), ((), ())), preferred_element_type=F32)


def _dot_tn(a, b):
    return lax.dot_general(a, b, (((0,), (0,)), ((), ())), preferred_element_type=F32)


def _sigmoid(x):
    return pl.reciprocal(1.0 + jnp.exp(-x), approx=True)


def _ln_stats(r):
    mu = jnp.mean(r, axis=-1, keepdims=True)
    d = r - mu
    var = jnp.mean(d * d, axis=-1, keepdims=True)
    rstd = lax.rsqrt(var + LN_EPS)
    return d * rstd, rstd


def _ln_bwd(dy, r, g):
    xhat, rstd = _ln_stats(r)
    dxhat = dy * g
    c1 = jnp.mean(dxhat, axis=-1, keepdims=True)
    c2 = jnp.mean(dxhat * xhat, axis=-1, keepdims=True)
    dr = rstd * (dxhat - c1 - xhat * c2)
    return dr, jnp.sum(dy * xhat, axis=0, keepdims=True), jnp.sum(dy, axis=0, keepdims=True)


def _const_spec(shape):
    nd = len(shape)
    return pl.BlockSpec(shape, lambda *_: (0,) * nd, pipeline_mode=pl.Buffered(1))


def all_gather(arrs, name):
    n = len(arrs)

    def body(*refs):
        ins, outs = refs[:n], refs[n:2 * n]
        send_sems, recv_sems, local_sems = refs[2 * n:]
        x, y, c = lax.axis_index("x"), lax.axis_index("y"), lax.axis_index("c")
        me, sibling = (x, y, c), (x, y, 1 - c)
        chips = [(1 - x, y), (x, 1 - y), (1 - x, 1 - y)]

        def slot(i, p):
            return outs[i].at[4 * p[0] + 2 * p[1] + p[2]]

        def copy(i, k, block, to, src=None):
            return pltpu.make_async_remote_copy(
                src_ref=slot(i, block) if src is None else src, dst_ref=slot(i, block),
                send_sem=send_sems.at[i, k], recv_sem=recv_sems.at[i, k],
                device_id=to, device_id_type=MESH)

        mine = [pltpu.make_async_copy(ins[i], slot(i, me), local_sems.at[i]) for i in range(n)]
        for cp in mine:
            cp.start()
        first = []
        for i in range(n):
            first.append(copy(i, 0, me, sibling, src=ins[i]))
            first += [copy(i, 1 + j, me, (*chip, c), src=ins[i]) for j, chip in enumerate(chips)]
        for cp in first:
            cp.start()
        passed = []
        for i in range(n):
            for j, chip in enumerate(chips):
                copy(i, 1 + j, (*chip, c), me).wait_recv()
                cp = copy(i, 4 + j, (*chip, c), sibling)
                cp.start()
                passed.append(cp)
        for i in range(n):
            copy(i, 0, sibling, me).wait_recv()
            for j, chip in enumerate(chips):
                copy(i, 4 + j, (*chip, 1 - c), me).wait_recv()
        for cp in first + passed:
            cp.wait_send()
        for cp in mine:
            cp.wait()

    any_spec = pl.BlockSpec(memory_space=pl.ANY)
    return pl.pallas_call(
        body, name=name,
        out_shape=[jax.ShapeDtypeStruct((N_DEV, *a.shape), a.dtype) for a in arrs],
        in_specs=[any_spec] * n, out_specs=[any_spec] * n,
        scratch_shapes=[pltpu.SemaphoreType.DMA((n, 7)), pltpu.SemaphoreType.DMA((n, 7)),
                        pltpu.SemaphoreType.DMA((n,))],
    )(*arrs)


def _place():
    x, y, c = lax.axis_index("x"), lax.axis_index("y"), lax.axis_index("c")
    return x, y, c, [(1 - x, y), (x, 1 - y), (1 - x, 1 - y)]


def _slot(p):
    return 4 * p[0] + 2 * p[1] + p[2]


class _Job:
    def __init__(self, ins, outs, nsem, copies, aliases=None, local=None):
        self.ins, self.outs, self.nsem, self.copies = list(ins), list(outs), nsem, copies
        self.aliases = aliases or {}
        self.local = local

    def scratch(self):
        s = [pltpu.SemaphoreType.DMA(self.nsem), pltpu.SemaphoreType.DMA(self.nsem)]
        if self.local is not None:
            s.append(pltpu.SemaphoreType.DMA((len(self.ins),)))
        return s

    def start(self, ins, outs, sems):
        if self.local is not None:
            for cp in self.local(ins, outs, sems[2]):
                cp.start()
        for cp in self.copies(ins, outs, sems[0], sems[1])[0]:
            cp.start()

    def finish(self, ins, outs, sems):
        started, awaited = self.copies(ins, outs, sems[0], sems[1])
        for cp in awaited:
            cp.wait_recv()
        for cp in started:
            cp.wait_send()
        if self.local is not None:
            for cp in self.local(ins, outs, sems[2]):
                cp.wait()


class _Jobs:
    def __init__(self, jobs):
        self.jobs = jobs
        self.ins = [a for j in jobs for a in j.ins]
        self.outs = [o for j in jobs for o in j.outs]
        self.aliases = {}
        at_in = at_out = 0
        for j in jobs:
            self.aliases.update({at_in + i: at_out + o for i, o in j.aliases.items()})
            at_in, at_out = at_in + len(j.ins), at_out + len(j.outs)

    def scratch(self):
        return [s for j in self.jobs for s in j.scratch()]

    def _each(self, ins, outs, sems):
        at_in = at_out = at_sem = 0
        for j in self.jobs:
            n_in, n_out, n_sem = len(j.ins), len(j.outs), len(j.scratch())
            yield j, ins[at_in:at_in + n_in], outs[at_out:at_out + n_out], sems[at_sem:at_sem + n_sem]
            at_in, at_out, at_sem = at_in + n_in, at_out + n_out, at_sem + n_sem

    def start(self, ins, outs, sems):
        for j, i, o, s in self._each(ins, outs, sems):
            j.start(i, o, s)

    def finish(self, ins, outs, sems):
        for j, i, o, s in self._each(ins, outs, sems):
            j.finish(i, o, s)

    def split(self, results):
        at, parts = 0, []
        for j in self.jobs:
            parts.append(results[at:at + len(j.outs)])
            at += len(j.outs)
        return parts


def _remote(src, dst, send, recv, idx, to):
    return pltpu.make_async_remote_copy(src_ref=src, dst_ref=dst, send_sem=send.at[idx], recv_sem=recv.at[idx],
                                        device_id=to, device_id_type=MESH)


def _spread_copies(ins, outs, send, recv, base=0):
    x, y, c, chips = _place()
    me = (x, y, c)
    peers = [(x, y, 1 - c)] + [(*chip, c) for chip in chips]
    started, awaited = [], []
    for i, (src, dst) in enumerate(zip(ins, outs)):
        for k, peer in enumerate(peers):
            started.append(_remote(src, dst.at[_slot(me)], send, recv, (base + i, k), peer))
            awaited.append(_remote(src, dst.at[_slot(peer)], send, recv, (base + i, k), peer))
    return started, awaited


def _forward_copies(ins, outs, send, recv, base=0):
    x, y, c, chips = _place()
    started, awaited = [], []
    for i, buf in enumerate(outs):
        for j, chip in enumerate(chips):
            mine, theirs = buf.at[_slot((*chip, c))], buf.at[_slot((*chip, 1 - c))]
            started.append(_remote(mine, mine, send, recv, (base + i, j), (x, y, 1 - c)))
            awaited.append(_remote(theirs, theirs, send, recv, (base + i, j), (x, y, 1 - c)))
    return started, awaited


def _own_block_copies(ins, outs, sems):
    x, y, c, _ = _place()
    return [pltpu.make_async_copy(src, dst.at[_slot((x, y, c))], sems.at[i])
            for i, (src, dst) in enumerate(zip(ins, outs))]


def gather_spread_job(shards):
    outs = [jax.ShapeDtypeStruct((N_DEV, *a.shape), a.dtype) for a in shards]
    return _Job(shards, outs, (len(shards), 4), _spread_copies, local=_own_block_copies)


def gather_forward_job(fulls):
    outs = [jax.ShapeDtypeStruct(a.shape, a.dtype) for a in fulls]
    return _Job(fulls, outs, (len(fulls), 3), _forward_copies, aliases={i: i for i in range(len(fulls))})


class _GatherJob:
    two_phase = True

    def __init__(self, shards):
        self.ins = list(shards)
        self.outs = [jax.ShapeDtypeStruct((N_DEV, *a.shape), a.dtype) for a in shards]
        self.aliases = {}

    def scratch(self):
        n = len(self.ins)
        return [pltpu.SemaphoreType.DMA((n, 4)), pltpu.SemaphoreType.DMA((n, 4)),
                pltpu.SemaphoreType.DMA((n, 3)), pltpu.SemaphoreType.DMA((n, 3)), pltpu.SemaphoreType.DMA((n,))]

    def start(self, ins, outs, sems):
        for cp in _own_block_copies(ins, outs, sems[4]) + _spread_copies(ins, outs, sems[0], sems[1])[0]:
            cp.start()

    def turn(self, ins, outs, sems):
        for cp in _spread_copies(ins, outs, sems[0], sems[1])[1]:
            cp.wait_recv()
        for cp in _forward_copies(outs, outs, sems[2], sems[3])[0]:
            cp.start()

    def finish(self, ins, outs, sems):
        handed_on, arriving = _forward_copies(outs, outs, sems[2], sems[3])
        for cp in arriving:
            cp.wait_recv()
        for cp in _spread_copies(ins, outs, sems[0], sems[1])[0] + handed_on:
            cp.wait_send()
        for cp in _own_block_copies(ins, outs, sems[4]):
            cp.wait()


def swap_job(gs):
    def copies(ins, outs, send, recv):
        x, y, c, _ = _place()
        started, awaited = [], []
        for i, (g, r1) in enumerate(zip(ins, outs)):
            for q in range(N_CHIP):
                started.append(_remote(g.at[2 * q + (1 - c)], r1.at[q], send, recv, (i, q), (x, y, 1 - c)))
                awaited.append(_remote(g.at[2 * q + c], r1.at[q], send, recv, (i, q), (x, y, 1 - c)))
        return started, awaited

    outs = [jax.ShapeDtypeStruct((N_CHIP, *g.shape[1:]), g.dtype) for g in gs]
    return _Job(gs, outs, (len(gs), N_CHIP), copies)


def chip_exchange_job(ps, rows=None, into=None):
    n = len(ps)

    def copies(ins, outs, send, recv):
        x, y, c, chips = _place()
        started, awaited = [], []
        for i, (p, r2) in enumerate(zip(ins[:n], outs)):
            for k, chip in enumerate(chips):
                src, mine, dst = p.at[2 * chip[0] + chip[1]], p.at[2 * x + y], r2.at[k]
                if rows is not None:
                    src, mine, dst = (t.at[pl.ds(rows[0], rows[1])] for t in (src, mine, dst))
                started.append(_remote(src, dst, send, recv, (i, k), (*chip, c)))
                awaited.append(_remote(mine, dst, send, recv, (i, k), (*chip, c)))
        return started, awaited

    outs = [jax.ShapeDtypeStruct((3, *p.shape[1:]), p.dtype) for p in ps]
    if into is None:
        return _Job(ps, outs, (n, 3), copies)
    return _Job(list(ps) + list(into), outs, (n, 3), copies, aliases={n + i: i for i in range(n)})


def _call(body, job, *, name, grid, in_specs, out_specs, out_shape, args, scratch_shapes=(), vmem=VMEM_LIMIT):
    if job is None:
        res = pl.pallas_call(
            body, name=name, grid=grid, in_specs=in_specs, out_specs=out_specs, out_shape=out_shape,
            scratch_shapes=list(scratch_shapes), compiler_params=_params(("arbitrary",) * len(grid), vmem),
        )(*args)
        return res, []
    n_in, n_out, n_scr = len(in_specs), len(out_specs), len(scratch_shapes)
    j_in, j_out = len(job.ins), len(job.outs)

    def with_copies(*refs):
        at = 0
        ins = refs[at:at + n_in]; at += n_in
        jins = refs[at:at + j_in]; at += j_in
        outs = refs[at:at + n_out]; at += n_out
        jouts = refs[at:at + j_out]; at += j_out
        scr = refs[at:at + n_scr]; at += n_scr
        sems = refs[at:]
        ids = [pl.program_id(d) for d in range(len(grid))]
        first = functools.reduce(jnp.logical_and, [i == 0 for i in ids])
        last = functools.reduce(jnp.logical_and, [i == n - 1 for i, n in zip(ids, grid)])

        @pl.when(first)
        def _():
            job.start(jins, jouts, sems)

        if getattr(job, "two_phase", False):
            steps, at = 1, 0
            for i, n in zip(ids, grid):
                steps, at = steps * n, at * n + i

            @pl.when(at == (3 * steps) // 4)
            def _():
                job.turn(jins, jouts, sems)

        body(*ins, *outs, *scr)

        @pl.when(last)
        def _():
            job.finish(jins, jouts, sems)

    any_spec = pl.BlockSpec(memory_space=pl.ANY)
    res = pl.pallas_call(
        with_copies, name=name, grid=grid,
        in_specs=list(in_specs) + [any_spec] * j_in, out_specs=list(out_specs) + [any_spec] * j_out,
        out_shape=list(out_shape) + list(job.outs),
        input_output_aliases={n_in + i: n_out + o for i, o in job.aliases.items()},
        scratch_shapes=list(scratch_shapes) + job.scratch(),
        compiler_params=_params(("arbitrary",) * len(grid), vmem),
    )(*args, *job.ins)
    return res[:n_out], res[n_out:]


def run_job(job, name):
    def body(*refs):
        j_in, j_out = len(job.ins), len(job.outs)
        ins, outs, sems = refs[:j_in], refs[j_in:j_in + j_out], refs[j_in + j_out:]
        job.start(ins, outs, sems)
        job.finish(ins, outs, sems)

    any_spec = pl.BlockSpec(memory_space=pl.ANY)
    return pl.pallas_call(
        body, name=name, in_specs=[any_spec] * len(job.ins), out_specs=[any_spec] * len(job.outs),
        out_shape=list(job.outs), input_output_aliases=dict(job.aliases), scratch_shapes=job.scratch(),
    )(*job.ins)


def pair_sum(g, r1, name):
    _, rows, cols = g.shape
    rb = next(cand for cand in range(min(rows, 512), 0, -16) if rows % cand == 0)

    def body(g_ref, r1_ref, p_ref, own_ref):
        x, y, c, _ = _place()
        s = g_ref[c].astype(F32) + r1_ref[0].astype(F32)
        p_ref[0] = s.astype(BF16)

        @pl.when(pl.program_id(1) == 2 * x + y)
        def _():
            own_ref[...] = s

    return pl.pallas_call(
        body, name=name, grid=(rows // rb, N_CHIP),
        in_specs=[pl.BlockSpec((2, rb, cols), lambda i, q: (q, i, 0)), pl.BlockSpec((1, rb, cols), lambda i, q: (q, i, 0))],
        out_specs=[pl.BlockSpec((1, rb, cols), lambda i, q: (q, i, 0)), pl.BlockSpec((rb, cols), lambda i, q: (i, 0))],
        out_shape=[jax.ShapeDtypeStruct((N_CHIP, rows, cols), BF16), jax.ShapeDtypeStruct((rows, cols), F32)],
        compiler_params=_params(("arbitrary", "arbitrary")),
    )(g, r1)


def sum_devices(a, name):
    def body(a_ref, o_ref):
        acc = a_ref[0]
        for d in range(1, N_DEV):
            acc = acc + a_ref[d]
        o_ref[...] = acc

    return pl.pallas_call(body, name=name, out_shape=jax.ShapeDtypeStruct(a.shape[1:], F32))(a)


def _adam_update(w, g, m, v):
    nm = ADAM_B1 * m + (1.0 - ADAM_B1) * g
    nv = ADAM_B2 * v + (1.0 - ADAM_B2) * (g * g)
    m_hat = nm / (1.0 - ADAM_B1 ** ADAM_STEP)
    v_hat = nv / (1.0 - ADAM_B2 ** ADAM_STEP)
    return -ADAM_LR * (m_hat / (jnp.sqrt(v_hat) + ADAM_EPS) + ADAM_WD * w), nm, nv


def adamw(w, g, m, v, name, others=None):
    rows, cols = w.shape
    rb = rows
    for cand in range(min(rows, 512), 7, -8):
        if rows % cand == 0 and cand % 8 == 0:
            rb = cand
            break

    def body(*refs):
        if others is None:
            w_ref, g_ref, m_ref, v_ref, d_ref, nm_ref, nv_ref = refs
            gg = g_ref[...]
        else:
            w_ref, g_ref, m_ref, v_ref, r2_ref, go_ref, d_ref, nm_ref, nv_ref = refs
            gg = g_ref[...]
            for k in range(3):
                gg = gg + r2_ref[k].astype(F32)
            go_ref[...] = gg
        d_ref[...], nm_ref[...], nv_ref[...] = _adam_update(w_ref[...], gg, m_ref[...], v_ref[...])

    spec = pl.BlockSpec((rb, cols), lambda i: (i, 0))
    out = jax.ShapeDtypeStruct((rows, cols), F32)
    in_specs, args = [spec] * 4, [w, g, m, v]
    if others is not None:
        in_specs.append(pl.BlockSpec((3, rb, cols), lambda i: (0, i, 0)))
        args.append(others)
    n_out = 3 if others is None else 4
    res = pl.pallas_call(
        body, name=name, grid=(rows // rb,), in_specs=in_specs, out_specs=[spec] * n_out,
        out_shape=[out] * n_out, compiler_params=_params(("parallel",)),
    )(*args)
    return (g, *res) if others is None else tuple(res)


def adamw_small(items, name):
    n = len(items)

    def body(*refs):
        ins, outs = refs[:4 * n], refs[4 * n:]
        for i in range(n):
            w_ref, g_ref, m_ref, v_ref = ins[4 * i:4 * i + 4]
            d_ref, nm_ref, nv_ref = outs[3 * i:3 * i + 3]
            d_ref[...], nm_ref[...], nv_ref[...] = _adam_update(w_ref[...], g_ref[...], m_ref[...], v_ref[...])

    res = pl.pallas_call(
        body, name=name,
        out_shape=[jax.ShapeDtypeStruct(w.shape, F32) for w, _, _, _ in items for _ in range(3)],
    )(*[t for item in items for t in item])
    return [tuple(res[3 * i:3 * i + 3]) for i in range(n)]


def ada_fwd(c_all, w_cols, b_cols, name):
    def body(c_ref, w_ref, b_ref, cond_ref, mod_ref):
        cc = c_ref[...]
        cond = (cc * _sigmoid(cc)).astype(BF16)
        cond_ref[...] = cond
        mod_ref[...] = _dot(cond, w_ref[...].astype(BF16)) + b_ref[...]

    n, cols = c_all.shape[0], w_cols.shape[1]
    return pl.pallas_call(
        body, name=name,
        out_shape=[jax.ShapeDtypeStruct(c_all.shape, BF16), jax.ShapeDtypeStruct((n, cols), F32)],
        compiler_params=_params(),
    )(c_all, w_cols, b_cols)


def ada_bwd(cond_all, dmod_cols, name):
    def body(c_ref, d_ref, gw_ref, gb_ref):
        d = d_ref[...]
        gw_ref[...] = _dot_tn(c_ref[...], d.astype(BF16))
        gb_ref[...] = jnp.sum(d, axis=0, keepdims=True)

    dm, cols = cond_all.shape[1], dmod_cols.shape[1]
    return pl.pallas_call(
        body, name=name,
        out_shape=[jax.ShapeDtypeStruct((dm, cols), F32), jax.ShapeDtypeStruct((1, cols), F32)],
        compiler_params=_params(),
    )(cond_all, dmod_cols)


MXU_COLS = 256
FFN_CHUNK = 4 * MXU_COLS


def _hidden_chunks(ff):
    assert ff % MXU_COLS == 0
    return [(at, min(FFN_CHUNK, ff - at)) for at in range(0, ff, FFN_CHUNK)]


def _mod_spec(tiles_per_seq, dm):
    return pl.BlockSpec((1, 1, dm), lambda i: (i // tiles_per_seq, 0, 0))


def ffn_fwd(x, sh, sc, gt, wgu, wd, ln_g, ln_b, seq, name, target=None, job=None):
    tokens, dm = x.shape
    ff = wgu.shape[1]
    chunks = _hidden_chunks(ff)
    tm = min(FFN_FWD_TILE, seq)
    tiles_per_seq = seq // tm
    with_loss = target is not None

    def body(*refs):
        if with_loss:
            (x_ref, sh_ref, sc_ref, gt_ref, wgu_ref, wd_ref, lg_ref, lb_ref, t_ref,
             xo_ref, loss_ref, r_ref, gu_ref, f_ref) = refs
        else:
            (x_ref, sh_ref, sc_ref, gt_ref, wgu_ref, wd_ref, lg_ref, lb_ref,
             xo_ref, r_ref, gu_ref, f_ref) = refs
        xx = x_ref[...]
        h = (xx * (1.0 + sc_ref[0]) + sh_ref[0]).astype(BF16)
        acc = jnp.zeros((tm, dm), F32)
        for at, wdt in chunks:
            gk = _dot_nt(h, wgu_ref[0, at:at + wdt, :])
            uk = _dot_nt(h, wgu_ref[1, at:at + wdt, :])
            gu_ref[0, :, at:at + wdt] = gk.astype(BF16)
            gu_ref[1, :, at:at + wdt] = uk.astype(BF16)
            a = (gk * _sigmoid(gk) * uk).astype(BF16)
            acc = acc + _dot(a, wd_ref[at:at + wdt, :])
        f_ref[...] = acc.astype(BF16)
        r = DN_ALPHA * xx + (0.5 * (1.0 + gt_ref[0])) * acc
        r_ref[...] = r
        xhat, _ = _ln_stats(r)
        yy = xhat * lg_ref[...] + lb_ref[...]
        if with_loss:
            err = yy - t_ref[...]
            xo_ref[...] = err * (1.0 / dm)

            @pl.when(pl.program_id(0) == 0)
            def _():
                loss_ref[...] = jnp.zeros_like(loss_ref)

            loss_ref[...] += jnp.full((1, 128), (0.5 / dm) * jnp.sum(err * err), F32)
        else:
            xo_ref[...] = yy

    tile = pl.BlockSpec((tm, dm), lambda i: (i, 0))
    mod = _mod_spec(tiles_per_seq, dm)
    in_specs = [tile, mod, mod, mod, _const_spec(wgu.shape), _const_spec(wd.shape),
                _const_spec((1, dm)), _const_spec((1, dm))]
    args = [x, sh, sc, gt, wgu, wd, ln_g, ln_b]
    out_specs = [tile]
    out_shape = [jax.ShapeDtypeStruct((tokens, dm), F32)]
    if with_loss:
        in_specs.append(tile)
        args.append(target)
        out_specs.append(pl.BlockSpec((1, 128), lambda i: (0, 0)))
        out_shape.append(jax.ShapeDtypeStruct((1, 128), F32))
    out_specs += [tile, pl.BlockSpec((2, tm, ff), lambda i: (0, i, 0)), tile]
    out_shape += [jax.ShapeDtypeStruct((tokens, dm), F32), jax.ShapeDtypeStruct((2, tokens, ff), BF16),
                  jax.ShapeDtypeStruct((tokens, dm), BF16)]
    return _call(body, job, name=name, grid=(tokens // tm,), in_specs=in_specs, out_specs=out_specs,
                 out_shape=out_shape, args=args)


def ffn_bwd(dy, r, x, f, gu, sh, sc, gt, wgu, wd, ln_g, seq, name, job=None):
    tokens, dm = x.shape
    ff = wgu.shape[1]
    chunks = _hidden_chunks(ff)
    tm = min(TOKEN_TILE, seq)
    tiles_per_seq = seq // tm
    nseq = tokens // seq

    def body(dy_ref, r_ref, x_ref, f_ref, gu_ref, sh_ref, sc_ref, gt_ref, wgu_ref, wd_ref, lg_ref,
             dx_ref, dgu_ref, df_ref, a_ref, h_ref, dln_ref, dmod_ref):
        i = pl.program_id(0)
        dr, dgain, dbias = _ln_bwd(dy_ref[...], r_ref[...], lg_ref[...])

        @pl.when(i == 0)
        def _():
            dln_ref[...] = jnp.zeros_like(dln_ref)

        @pl.when(i % tiles_per_seq == 0)
        def _():
            dmod_ref[...] = jnp.zeros_like(dmod_ref)

        dln_ref[0:1, :] += dgain
        dln_ref[1:2, :] += dbias
        df32 = (0.5 * (1.0 + gt_ref[0])) * dr
        df = df32.astype(BF16)
        df_ref[...] = df
        dgate = jnp.sum(dr * (0.5 * f_ref[...].astype(F32)), axis=0, keepdims=True)
        xx = x_ref[...]
        one_sc = 1.0 + sc_ref[0]
        h = (xx * one_sc + sh_ref[0]).astype(BF16)
        h_ref[...] = h
        dh = jnp.zeros((tm, dm), F32)
        for at, wdt in chunks:
            cols = slice(at, at + wdt)
            da = _dot_nt(df, wd_ref[cols, :])
            gk = gu_ref[0, :, cols].astype(F32)
            uk = gu_ref[1, :, cols].astype(F32)
            sg = _sigmoid(gk)
            sil = gk * sg
            a_ref[:, cols] = (sil * uk).astype(BF16)
            du = (da * sil).astype(BF16)
            dg = (da * uk * (sg * (1.0 + gk * (1.0 - sg)))).astype(BF16)
            dgu_ref[0, :, cols] = dg
            dgu_ref[1, :, cols] = du
            dh = dh + _dot(dg, wgu_ref[0, cols, :]) + _dot(du, wgu_ref[1, cols, :])
        dx_ref[...] = DN_ALPHA * dr + dh * one_sc
        dmod_ref[0, 0:1, :] += jnp.sum(dh, axis=0, keepdims=True)
        dmod_ref[0, 1:2, :] += jnp.sum(dh * xx, axis=0, keepdims=True)
        dmod_ref[0, 2:3, :] += dgate

    tile = pl.BlockSpec((tm, dm), lambda i: (i, 0))
    mod = _mod_spec(tiles_per_seq, dm)
    gu_spec = pl.BlockSpec((2, tm, ff), lambda i: (0, i, 0))
    return _call(
        body, job, name=name, grid=(tokens // tm,),
        in_specs=[tile, tile, tile, tile, gu_spec, mod, mod, mod, _const_spec(wgu.shape), _const_spec(wd.shape),
                  _const_spec((1, dm))],
        out_specs=[tile, gu_spec, tile, pl.BlockSpec((tm, ff), lambda i: (i, 0)), tile,
                   pl.BlockSpec((2, dm), lambda i: (0, 0)),
                   pl.BlockSpec((1, 3, dm), lambda i: (i // tiles_per_seq, 0, 0))],
        out_shape=[jax.ShapeDtypeStruct((tokens, dm), F32), jax.ShapeDtypeStruct((2, tokens, ff), BF16),
                   jax.ShapeDtypeStruct((tokens, dm), BF16), jax.ShapeDtypeStruct((tokens, ff), BF16),
                   jax.ShapeDtypeStruct((tokens, dm), BF16), jax.ShapeDtypeStruct((2, dm), F32),
                   jax.ShapeDtypeStruct((nseq, 3, dm), F32)],
        args=(dy, r, x, f, gu, sh, sc, gt, wgu, wd, ln_g))


def tn_matmul(a, b, name, job=None, b_cols=None, a_width=None):
    na, tokens, k_all = a.shape
    kk = k_all if a_width is None else a_width
    nka = k_all // kk
    assert nka * kk == k_all
    nb, _, cc = b.shape
    col = 0
    if b_cols is not None:
        col, cc = b_cols
    tt = tokens
    while 4 * tt * (kk + cc) + 8 * kk * cc > TN_VMEM_BUDGET and tt % 2 == 0 and tt > 256:
        tt //= 2
    steps = tokens // tt

    def body(a_ref, b_ref, o_ref, *acc):
        if steps == 1:
            o_ref[0, 0, 0] = _dot_tn(a_ref[0], b_ref[0]).astype(BF16)
            return
        acc_ref, = acc
        t = pl.program_id(3)

        @pl.when(t == 0)
        def _():
            acc_ref[...] = jnp.zeros_like(acc_ref)

        acc_ref[...] += _dot_tn(a_ref[0], b_ref[0])

        @pl.when(t == steps - 1)
        def _():
            o_ref[0, 0, 0] = acc_ref[...].astype(BF16)

    return _call(
        body, job, name=name, grid=(na, nka, nb, steps),
        in_specs=[pl.BlockSpec((1, tt, kk), lambda i, s, j, t: (i, t, s)),
                  pl.BlockSpec((1, tt, cc), lambda i, s, j, t: (j, t, col))],
        out_specs=[pl.BlockSpec((1, 1, 1, kk, cc), lambda i, s, j, t: (i, s, j, 0, 0))],
        out_shape=[jax.ShapeDtypeStruct((na, nka, nb, kk, cc), BF16)],
        scratch_shapes=[] if steps == 1 else [pltpu.VMEM((kk, cc), F32)], args=(a, b))


def proj_fwd(x1, sh, sc, w_in, seq, name, job=None):
    tokens, dm = x1.shape
    tm = min(MIX_TILE, seq)
    tiles_per_seq = seq // tm
    widths = [N_Q_HEADS * HEAD_DIM, N_KV_HEADS * HEAD_DIM, N_KV_HEADS * HEAD_DIM, 512, 512, 512]
    assert sum(widths) == w_in.shape[0]

    def body(x_ref, sh_ref, sc_ref, w_ref, *outs):
        h = (x_ref[...] * (1.0 + sc_ref[0]) + sh_ref[0]).astype(BF16)
        proj = _dot_nt(h, w_ref[...])
        at = 0
        for o_ref, wdt in zip(outs, widths):
            o_ref[...] = proj[:, at:at + wdt]
            at += wdt

    tile = pl.BlockSpec((tm, dm), lambda i: (i, 0))
    mod = _mod_spec(tiles_per_seq, dm)
    return _call(
        body, job, name=name, grid=(tokens // tm,),
        in_specs=[tile, mod, mod, _const_spec(w_in.shape)],
        out_specs=[pl.BlockSpec((tm, wdt), lambda i: (i, 0)) for wdt in widths],
        out_shape=[jax.ShapeDtypeStruct((tokens, wdt), F32) for wdt in widths],
        args=(x1, sh, sc, w_in))


LANES = 2 * HEAD_DIM


def _head_lane(shape):
    return lax.broadcasted_iota(jnp.int32, shape, 1) % HEAD_DIM


def _lane_half(shape):
    return lax.broadcasted_iota(jnp.int32, shape, 1) // HEAD_DIM


def _swap_rot(v):
    lane = _head_lane(v.shape)
    half = ROT_DIM // 2
    return jnp.where(lane < half, pltpu.roll(v, LANES - half, 1),
                     jnp.where(lane < ROT_DIM, pltpu.roll(v, half, 1), 0.0))


def _rope(v, cos_t, sin_t):
    return v * cos_t + _swap_rot(v) * sin_t


def _unrope(dv, cos_t, sin_t):
    return dv * cos_t + _swap_rot(dv * sin_t)


def _both_halves(t, g):
    return jnp.where(_lane_half(t.shape) == g, t, pltpu.roll(t, HEAD_DIM, 1))


def _fold_halves(t, g):
    return jnp.where(_lane_half(t.shape) == g, t + pltpu.roll(t, HEAD_DIM, 1), 0.0)


def _stack_heads(blocks):
    rows = []
    for blk in blocks:
        half = _lane_half(blk.shape)
        rows += [jnp.where(half == 0, blk, 0.0), jnp.where(half == 1, blk, 0.0)]
    return jnp.concatenate(rows, axis=0)


def _unstack_heads(t, j):
    lo = t[(2 * j) * ATTN_BLOCK:(2 * j + 1) * ATTN_BLOCK]
    hi = t[(2 * j + 1) * ATTN_BLOCK:(2 * j + 2) * ATTN_BLOCK]
    return jnp.where(_lane_half(lo.shape) == 0, lo, hi)


def _band_mask(q0, w0):
    rows, cols = GQA_GROUP * ATTN_BLOCK, 2 * ATTN_BLOCK
    qi = lax.broadcasted_iota(jnp.int32, (rows, cols), 0) % ATTN_BLOCK + q0
    ki = lax.broadcasted_iota(jnp.int32, (rows, cols), 1) + w0
    diff = qi - ki
    return (diff >= 0) & (diff < ATTN_BLOCK)


def _attn_specs(seq):
    q_spec = pl.BlockSpec((seq, GQA_GROUP * HEAD_DIM), lambda b, g: (b, g))
    kv_spec = pl.BlockSpec((seq, LANES), lambda b, g: (b, 0))
    sink_spec = pl.BlockSpec((1, GQA_GROUP * ATTN_BLOCK, 1), lambda b, g: (g, 0, 0))
    return q_spec, kv_spec, sink_spec


def _block_starts(n):
    q0 = pl.multiple_of(n * ATTN_BLOCK, ATTN_BLOCK)
    w0 = pl.multiple_of(jnp.maximum(n - 1, 0) * ATTN_BLOCK, ATTN_BLOCK)
    return q0, w0


def _stacked_queries(ref, rows):
    return _stack_heads([ref[rows, j * LANES:(j + 1) * LANES] for j in range(2)]).astype(BF16)


def _sink_columns(sinks):
    return jnp.repeat(sinks.reshape(N_KV_HEADS, GQA_GROUP), ATTN_BLOCK, axis=1)[:, :, None]


def _probs_spec(nblk):
    return pl.BlockSpec((1, 1, nblk, GQA_GROUP * ATTN_BLOCK, 2 * ATTN_BLOCK), lambda b, g: (b, g, 0, 0, 0))


def _sink_probs_spec():
    return pl.BlockSpec((1, 1, GQA_GROUP * ATTN_BLOCK, LANES), lambda b, g: (b, g, 0, 0))


def attn_fwd(q, k, v, cos_t, sin_t, sinks, seq, name, job=None):
    tokens = q.shape[0]
    nblk = seq // ATTN_BLOCK
    assert nblk >= 2
    scale = HEAD_DIM ** -0.5

    nseq = tokens // seq
    rows_stacked = GQA_GROUP * ATTN_BLOCK
    assert nblk <= LANES

    def body(q_ref, k_ref, v_ref, cos_ref, sin_ref, sink_ref, o_ref, qr_ref, p_ref, ps_ref, kd_ref, vd_ref):
        g = pl.program_id(1)
        kd_ref[...] = _both_halves(_rope(k_ref[...], cos_ref[...], sin_ref[...]), g).astype(BF16)
        vd_ref[...] = _both_halves(v_ref[...], g).astype(BF16)
        sink = sink_ref[0]
        lane = lax.broadcasted_iota(jnp.int32, (rows_stacked, LANES), 1)

        ps_ref[...] = jnp.zeros_like(ps_ref)

        def block(n, carry):
            q0, w0 = _block_starts(n)
            rows, win = pl.ds(q0, ATTN_BLOCK), pl.ds(w0, 2 * ATTN_BLOCK)
            blocks = []
            for j in range(2):
                qr = _rope(q_ref[rows, j * LANES:(j + 1) * LANES], cos_ref[rows, :], sin_ref[rows, :]).astype(BF16)
                qr_ref[rows, j * LANES:(j + 1) * LANES] = qr
                blocks.append(qr)
            qs = _stack_heads(blocks)
            s = _dot_nt(qs, kd_ref[win, :]) * scale
            s = jnp.where(_band_mask(q0, w0), s, NEG_BIG)
            m = jnp.maximum(jnp.max(s, axis=-1, keepdims=True), sink)
            p = jnp.exp(s - m)
            e_sink = jnp.exp(sink - m)
            inv = pl.reciprocal(jnp.sum(p, axis=-1, keepdims=True) + e_sink, approx=True)
            pn = (p * inv).astype(BF16)
            p_ref[0, 0, n] = pn
            out = _dot(pn, vd_ref[win, :])
            for j in range(2):
                o_ref[rows, j * LANES:(j + 1) * LANES] = _unstack_heads(out, j).astype(o_ref.dtype)
            ps_ref[0, 0] = jnp.where(lane == n, e_sink * inv, ps_ref[0, 0])
            return carry

        lax.fori_loop(0, nblk, block, 0, unroll=2)

    q_spec, kv_spec, sink_spec = _attn_specs(seq)
    return _call(
        body, job, name=name, grid=(nseq, N_KV_HEADS),
        in_specs=[q_spec, kv_spec, kv_spec, kv_spec, kv_spec, sink_spec],
        out_specs=[q_spec, q_spec, _probs_spec(nblk), _sink_probs_spec()],
        out_shape=[jax.ShapeDtypeStruct(q.shape, BF16), jax.ShapeDtypeStruct(q.shape, BF16),
                   jax.ShapeDtypeStruct((nseq, N_KV_HEADS, nblk, rows_stacked, 2 * ATTN_BLOCK), BF16),
                   jax.ShapeDtypeStruct((nseq, N_KV_HEADS, rows_stacked, LANES), F32)],
        scratch_shapes=[pltpu.VMEM((seq, LANES), BF16), pltpu.VMEM((seq, LANES), BF16)],
        args=(q, k, v, cos_t, sin_t, _sink_columns(sinks)))


def attn_bwd(qr, k, v, do, probs, sink_probs, cos_t, sin_t, seq, name, job=None):
    tokens = qr.shape[0]
    nseq = tokens // seq
    nblk = seq // ATTN_BLOCK
    assert nblk >= 2
    rows_stacked = GQA_GROUP * ATTN_BLOCK
    scale = HEAD_DIM ** -0.5

    def body(q_ref, k_ref, v_ref, do_ref, p_ref, ps_ref, cos_ref, sin_ref, dq_ref, dk_ref, dv_ref, ds_ref,
             kd_ref, vd_ref, dkd_ref, dvd_ref, acc_ref):
        g = pl.program_id(1)
        kd_ref[...] = _both_halves(_rope(k_ref[...], cos_ref[...], sin_ref[...]), g).astype(BF16)
        vd_ref[...] = _both_halves(v_ref[...], g).astype(BF16)
        dkd_ref[...] = jnp.zeros_like(dkd_ref)
        dvd_ref[...] = jnp.zeros_like(dvd_ref)
        acc_ref[...] = jnp.zeros_like(acc_ref)
        lane = lax.broadcasted_iota(jnp.int32, (rows_stacked, LANES), 1)

        def block(n, carry):
            q0, w0 = _block_starts(n)
            rows, win = pl.ds(q0, ATTN_BLOCK), pl.ds(w0, 2 * ATTN_BLOCK)
            qs = _stacked_queries(q_ref, rows)
            dos = _stacked_queries(do_ref, rows)
            kw, vw = kd_ref[win, :], vd_ref[win, :]
            pn16 = p_ref[0, 0, n]
            pn = pn16.astype(F32)
            dvd_ref[win, :] += _dot_tn(pn16, dos)
            dp = _dot_nt(dos, vw)
            delta = jnp.sum(dp * pn, axis=-1, keepdims=True)
            ds = (pn * (dp - delta)).astype(BF16)
            dqs = _dot(ds, kw) * scale
            dkd_ref[win, :] += _dot_tn(ds, qs) * scale
            cos_b, sin_b = cos_ref[rows, :], sin_ref[rows, :]
            for j in range(2):
                dq_ref[rows, j * LANES:(j + 1) * LANES] = _unrope(_unstack_heads(dqs, j), cos_b, sin_b).astype(BF16)
            acc_ref[...] += jnp.where(lane == n, ps_ref[0, 0] * delta, 0.0)
            return carry

        lax.fori_loop(0, nblk // 2, lambda i, carry: block(2 * i + 1, block(2 * i, carry)), 0)
        ds_ref[0, 0] = -jnp.sum(acc_ref[...], axis=-1, keepdims=True)
        dk_g = _unrope(_fold_halves(dkd_ref[...], g), cos_ref[...], sin_ref[...])
        dv_g = _fold_halves(dvd_ref[...], g)

        @pl.when(g == 0)
        def _():
            dk_ref[...] = dk_g
            dv_ref[...] = dv_g

        @pl.when(g != 0)
        def _():
            dk_ref[...] += dk_g
            dv_ref[...] += dv_g

    q_spec, kv_spec, _ = _attn_specs(seq)
    return _call(
        body, job, name=name, grid=(nseq, N_KV_HEADS),
        in_specs=[q_spec, kv_spec, kv_spec, q_spec, _probs_spec(nblk), _sink_probs_spec(), kv_spec, kv_spec],
        out_specs=[q_spec, kv_spec, kv_spec, pl.BlockSpec((1, 1, rows_stacked, 1), lambda b, g: (b, g, 0, 0))],
        out_shape=[jax.ShapeDtypeStruct(qr.shape, BF16), jax.ShapeDtypeStruct(k.shape, F32),
                   jax.ShapeDtypeStruct(k.shape, F32), jax.ShapeDtypeStruct((nseq, N_KV_HEADS, rows_stacked, 1), F32)],
        scratch_shapes=[pltpu.VMEM((seq, LANES), BF16), pltpu.VMEM((seq, LANES), BF16),
                        pltpu.VMEM((seq, LANES), F32), pltpu.VMEM((seq, LANES), F32),
                        pltpu.VMEM((rows_stacked, LANES), F32)],
        args=(qr, k, v, do, probs, sink_probs, cos_t, sin_t))


CONV_COLS = 128


def _shift_down(z, by):
    t = lax.broadcasted_iota(jnp.int32, z.shape, 0)
    return jnp.where(t >= by, pltpu.roll(z, by, 0), 0.0)


def _shift_up(z, by):
    n = z.shape[0]
    t = lax.broadcasted_iota(jnp.int32, z.shape, 0)
    return jnp.where(t < n - by, pltpu.roll(z, n - by, 0), 0.0)


def conv_fwd(u, bg, cg, conv_w, seq, name):
    tokens, width = u.shape

    def body(u_ref, bg_ref, cg_ref, w_ref, o_ref):
        z = cg_ref[...] * u_ref[...]
        yy = w_ref[2:3, :] * z + w_ref[1:2, :] * _shift_down(z, 1) + w_ref[0:1, :] * _shift_down(z, 2)
        o_ref[...] = (bg_ref[...] * yy).astype(BF16)

    col = pl.BlockSpec((seq, CONV_COLS), lambda j, b: (b, j))
    return pl.pallas_call(
        body, name=name, grid=(width // CONV_COLS, tokens // seq),
        in_specs=[col, col, col, pl.BlockSpec((CONV_TAPS, CONV_COLS), lambda j, b: (0, j))],
        out_specs=col, out_shape=jax.ShapeDtypeStruct((tokens, width), BF16),
        compiler_params=_params(("parallel", "parallel")),
    )(u, bg, cg, conv_w)


def conv_bwd(dout, u, bg, cg, conv_w, seq, name):
    tokens, width = u.shape

    def body(do_ref, u_ref, bg_ref, cg_ref, w_ref, du_ref, dbg_ref, dcg_ref, dw_ref):
        uu, cg_v, do = u_ref[...], cg_ref[...], do_ref[...].astype(F32)
        z = cg_v * uu
        z1, z2 = _shift_down(z, 1), _shift_down(z, 2)
        yy = w_ref[2:3, :] * z + w_ref[1:2, :] * z1 + w_ref[0:1, :] * z2
        dbg_ref[...] = (do * yy).astype(BF16)
        dyy = do * bg_ref[...]
        dz = w_ref[2:3, :] * dyy + w_ref[1:2, :] * _shift_up(dyy, 1) + w_ref[0:1, :] * _shift_up(dyy, 2)
        du_ref[...] = (dz * cg_v).astype(BF16)
        dcg_ref[...] = (dz * uu).astype(BF16)

        @pl.when(pl.program_id(1) == 0)
        def _():
            dw_ref[...] = jnp.zeros_like(dw_ref)

        dw_ref[0:1, :] += jnp.sum(dyy * z2, axis=0, keepdims=True)
        dw_ref[1:2, :] += jnp.sum(dyy * z1, axis=0, keepdims=True)
        dw_ref[2:3, :] += jnp.sum(dyy * z, axis=0, keepdims=True)

    col = pl.BlockSpec((seq, CONV_COLS), lambda j, b: (b, j))
    w_spec = pl.BlockSpec((CONV_TAPS, CONV_COLS), lambda j, b: (0, j))
    act = jax.ShapeDtypeStruct((tokens, width), BF16)
    return pl.pallas_call(
        body, name=name, grid=(width // CONV_COLS, tokens // seq),
        in_specs=[col, col, col, col, w_spec], out_specs=[col, col, col, w_spec],
        out_shape=[act, act, act, jax.ShapeDtypeStruct((CONV_TAPS, width), F32)],
        compiler_params=_params(("parallel", "arbitrary")),
    )(dout, u, bg, cg, conv_w)


def out_fwd(x1, attn, conv, gt, w_out, ln_g, ln_b, seq, name, job=None):
    tokens, dm = x1.shape
    half = attn.shape[1]
    tm = min(MIX_TILE, seq)
    tiles_per_seq = seq // tm

    def body(x_ref, a_ref, c_ref, gt_ref, w_ref, lg_ref, lb_ref, xo_ref, r_ref, mi_ref, mix_ref):
        mixin = jnp.concatenate([a_ref[...], c_ref[...]], axis=1).astype(BF16)
        mi_ref[...] = mixin
        mix = _dot(mixin, w_ref[...])
        mix_ref[...] = mix.astype(BF16)
        r = DN_ALPHA * x_ref[...] + (1.0 + gt_ref[0]) * mix
        r_ref[...] = r
        xhat, _ = _ln_stats(r)
        xo_ref[...] = xhat * lg_ref[...] + lb_ref[...]

    tile = pl.BlockSpec((tm, dm), lambda i: (i, 0))
    htile = pl.BlockSpec((tm, half), lambda i: (i, 0))
    return _call(
        body, job, name=name, grid=(tokens // tm,),
        in_specs=[tile, htile, htile, _mod_spec(tiles_per_seq, dm), _const_spec(w_out.shape),
                  _const_spec((1, dm)), _const_spec((1, dm))],
        out_specs=[tile, tile, tile, tile],
        out_shape=[jax.ShapeDtypeStruct((tokens, dm), F32), jax.ShapeDtypeStruct((tokens, dm), F32),
                   jax.ShapeDtypeStruct((tokens, dm), BF16), jax.ShapeDtypeStruct((tokens, dm), BF16)],
        args=(x1, attn, conv, gt, w_out, ln_g, ln_b))


def out_bwd(dy, r, mix, gt, w_out, ln_g, seq, name, job=None):
    tokens, dm = r.shape
    half = dm // 2
    tm = min(MIX_TILE, seq)
    tiles_per_seq = seq // tm
    nseq = tokens // seq

    def body(dy_ref, r_ref, mix_ref, gt_ref, w_ref, lg_ref, dres_ref, da_ref, dc_ref, dmix_ref, dln_ref, dgt_ref):
        i = pl.program_id(0)
        dr, dgain, dbias = _ln_bwd(dy_ref[...], r_ref[...], lg_ref[...])

        @pl.when(i == 0)
        def _():
            dln_ref[...] = jnp.zeros_like(dln_ref)

        @pl.when(i % tiles_per_seq == 0)
        def _():
            dgt_ref[...] = jnp.zeros_like(dgt_ref)

        dln_ref[0:1, :] += dgain
        dln_ref[1:2, :] += dbias
        dgt_ref[0] += jnp.sum(dr * mix_ref[...].astype(F32), axis=0, keepdims=True)
        dres_ref[...] = DN_ALPHA * dr
        dmix = ((1.0 + gt_ref[0]) * dr).astype(BF16)
        dmix_ref[...] = dmix
        dmixin = _dot_nt(dmix, w_ref[...])
        da_ref[...] = dmixin[:, :half].astype(BF16)
        dc_ref[...] = dmixin[:, half:].astype(BF16)

    tile = pl.BlockSpec((tm, dm), lambda i: (i, 0))
    htile = pl.BlockSpec((tm, half), lambda i: (i, 0))
    return _call(
        body, job, name=name, grid=(tokens // tm,),
        in_specs=[tile, tile, tile, _mod_spec(tiles_per_seq, dm), _const_spec(w_out.shape), _const_spec((1, dm))],
        out_specs=[tile, htile, htile, tile, pl.BlockSpec((2, dm), lambda i: (0, 0)),
                   pl.BlockSpec((1, 1, dm), lambda i: (i // tiles_per_seq, 0, 0))],
        out_shape=[jax.ShapeDtypeStruct((tokens, dm), F32), jax.ShapeDtypeStruct((tokens, half), BF16),
                   jax.ShapeDtypeStruct((tokens, half), BF16), jax.ShapeDtypeStruct((tokens, dm), BF16),
                   jax.ShapeDtypeStruct((2, dm), F32), jax.ShapeDtypeStruct((nseq, 1, dm), F32)],
        args=(dy, r, mix, gt, w_out, ln_g))


def proj_bwd(parts, dres, x1, sh, sc, w_in, seq, name, job=None):
    tokens, dm = x1.shape
    tm = min(MIX_TILE, seq)
    tiles_per_seq = seq // tm
    nseq = tokens // seq
    widths = [p.shape[1] for p in parts]
    total = sum(widths)

    def body(*refs):
        part_refs = refs[:6]
        dres_ref, x_ref, sh_ref, sc_ref, w_ref, dx_ref, dproj_ref, h_ref, dmod_ref = refs[6:]
        dproj = jnp.concatenate([p[...].astype(BF16) for p in part_refs], axis=1)
        dproj_ref[...] = dproj
        dh = _dot(dproj, w_ref[...])
        xx = x_ref[...]
        one_sc = 1.0 + sc_ref[0]
        h_ref[...] = (xx * one_sc + sh_ref[0]).astype(BF16)
        dx_ref[...] = dres_ref[...] + dh * one_sc

        @pl.when(pl.program_id(0) % tiles_per_seq == 0)
        def _():
            dmod_ref[...] = jnp.zeros_like(dmod_ref)

        dmod_ref[0, 0:1, :] += jnp.sum(dh, axis=0, keepdims=True)
        dmod_ref[0, 1:2, :] += jnp.sum(dh * xx, axis=0, keepdims=True)

    tile = pl.BlockSpec((tm, dm), lambda i: (i, 0))
    mod = _mod_spec(tiles_per_seq, dm)
    return _call(
        body, job, name=name, grid=(tokens // tm,),
        in_specs=[pl.BlockSpec((tm, wdt), lambda i: (i, 0)) for wdt in widths]
        + [tile, tile, mod, mod, _const_spec(w_in.shape)],
        out_specs=[tile, pl.BlockSpec((tm, total), lambda i: (i, 0)), tile,
                   pl.BlockSpec((1, 2, dm), lambda i: (i // tiles_per_seq, 0, 0))],
        out_shape=[jax.ShapeDtypeStruct((tokens, dm), F32), jax.ShapeDtypeStruct((tokens, total), BF16),
                   jax.ShapeDtypeStruct((tokens, dm), BF16), jax.ShapeDtypeStruct((nseq, 2, dm), F32)],
        args=(*parts, dres, x1, sh, sc, w_in))


def _rope_tables(positions):
    half = ROT_DIM // 2
    inv_freq = jnp.power(jnp.float32(ROPE_THETA), -jnp.arange(0, ROT_DIM, 2, dtype=F32) / ROT_DIM)
    lane = jnp.arange(LANES) % HEAD_DIM
    freq = jnp.where(lane < ROT_DIM, inv_freq[lane % half], 0.0)
    sign = jnp.where(lane < half, -1.0, 1.0).astype(F32)
    ang = positions.astype(F32)[:, None] * freq[None, :]
    return jnp.cos(ang), sign[None, :] * jnp.sin(ang)


def kernel(x, c, positions, w_ada, b_ada, ffn1_w_gate_up, ffn1_w_down, ln1_g, ln1_b, w_in, conv_w, attn_sinks, w_out, ln2_g, ln2_b, ffn2_w_gate_up, ffn2_w_down, ln3_g, ln3_b, loss_target, m_w_ada, m_b_ada, m_ffn1_w_gate_up, m_ffn1_w_down, m_ln1_g, m_ln1_b, m_w_in, m_conv_w, m_attn_sinks, m_w_out, m_ln2_g, m_ln2_b, m_ffn2_w_gate_up, m_ffn2_w_down, m_ln3_g, m_ln3_b, v_w_ada, v_b_ada, v_ffn1_w_gate_up, v_ffn1_w_down, v_ln1_g, v_ln1_b, v_w_in, v_conv_w, v_attn_sinks, v_w_out, v_ln2_g, v_ln2_b, v_ffn2_w_gate_up, v_ffn2_w_down, v_ln3_g, v_ln3_b):
    nseq, seq, dm = x.shape
    tokens = nseq * seq
    dev = 4 * lax.axis_index("x") + 2 * lax.axis_index("y") + lax.axis_index("c")
    ada_cols = w_ada.shape[2]
    ff = ffn1_w_down.shape[1] * N_DEV
    fc = ff // 4
    in_cols = w_in.shape[2]
    conv_cols = conv_w.shape[2]

    def t_bf16(w):
        return w[0].T.astype(BF16)

    c_all, convw_all = all_gather([c, conv_w[0]], "gather_cond")
    c_all = c_all.reshape(N_DEV * nseq, dm)
    convw_full = convw_all.transpose(1, 0, 2).reshape(CONV_TAPS, N_DEV * conv_cols)

    b_cols = lax.dynamic_slice(b_ada, (0, dev * ada_cols), (1, ada_cols))
    cond_all, mod_cols = ada_fwd(c_all, w_ada[0], b_cols, "ada_fwd")
    wgu1, wd1, mod_all = all_gather([t_bf16(ffn1_w_gate_up), ffn1_w_down[0].astype(BF16), mod_cols], "gather_ffn1")
    wgu1, wd1 = wgu1.reshape(2, ff, dm), wd1.reshape(ff, dm)
    mod = lax.dynamic_slice(mod_all, (0, dev * nseq, 0), (N_DEV, nseq, ada_cols))
    mod = mod.transpose(1, 0, 2).reshape(nseq, 9, 1, dm)
    sh1, sc1, g1, sh2, sc2, g2, sh3, sc3, g3 = [mod[:, i] for i in range(9)]

    x0 = x.reshape(tokens, dm)
    gather = _GatherJob([t_bf16(w_in), w_out[0].astype(BF16), ffn2_w_down[0].astype(BF16)])
    (x1, r1, gu1, f1), (win, wout, wd2) = ffn_fwd(x0, sh1, sc1, g1, wgu1, wd1, ln1_g, ln1_b, seq, "ffn1_fwd", job=gather)
    win, wout, wd2 = win.reshape(N_DEV * in_cols, dm), wout.reshape(dm, dm), wd2.reshape(ff, dm)
    (q, k, v, u, bg, cg), _ = proj_fwd(x1, sh2, sc2, win, seq, "proj_fwd")
    cos_t, sin_t = _rope_tables(positions.reshape(tokens))
    sinks = attn_sinks[0]
    (attn, q_rot, probs, sink_probs), (wgu2,) = attn_fwd(q, k, v, cos_t, sin_t, sinks, seq, "attn_fwd",
                                                         job=_GatherJob([t_bf16(ffn2_w_gate_up)]))
    conv = conv_fwd(u, bg, cg, convw_full, seq, "conv_fwd")
    (x2, r2, mixin, mix), _ = out_fwd(x1, attn, conv, g2, wout, ln2_g, ln2_b, seq, "out_fwd")
    wgu2 = wgu2.reshape(2, ff, dm)
    target = loss_target.reshape(tokens, dm)
    (dy3, loss_part, r3, gu3, f3), _ = ffn_fwd(x2, sh3, sc3, g3, wgu2, wd2, ln3_g, ln3_b, seq, "ffn2_fwd", target=target)

    (dx2, dgu3, df3, a3, h3, dln3, dmod3), _ = ffn_bwd(dy3, r3, x2, f3, gu3, sh3, sc3, g3, wgu2, wd2, ln3_g, seq, "ffn2_bwd")
    pair = 2 * fc
    g_wd2 = tn_matmul(a3[None], df3[None], "ffn2_dwd", a_width=pair)[0][0].reshape(N_DEV, ff // N_DEV, dm)
    g_wgu2 = tn_matmul(dgu3, h3[None], "ffn2_dwgu", a_width=pair)[0][0].reshape(N_DEV, fc, dm)
    (dres2, dattn, dconv, dmix, dln2, dg2), swapped = out_bwd(dx2, r2, mix, g2, wout, ln2_g, seq, "out_bwd",
                                                              job=swap_job([g_wgu2, g_wd2]))
    p_wgu2, own_wgu2 = pair_sum(g_wgu2, swapped[0], "pair_wgu2")
    p_wd2, own_wd2 = pair_sum(g_wd2, swapped[1], "pair_wd2")
    du, dbg, dcg, dconvw = conv_bwd(dconv, u, bg, cg, convw_full, seq, "conv_bwd")
    (dq, dk, dv, dsink_rows), (far_wd2,) = attn_bwd(
        q_rot, k, v, dattn, probs, sink_probs, cos_t, sin_t, seq, "attn_bwd", job=chip_exchange_job([p_wd2]))
    parts = [dq, dk, dv, du, dbg, dcg]
    (dx1, dproj, h2, dmod2), far_top = proj_bwd(parts, dres2, x1, sh2, sc2, win, seq, "proj_bwd",
                                                job=chip_exchange_job([p_wgu2], rows=(0, fc // 2)))
    (dx0, dgu1, df1, a1, h1, dln1, dmod1), _ = ffn_bwd(
        dx1, r1, x0, f1, gu1, sh1, sc1, g1, wgu1, wd1, ln1_g, seq, "ffn1_bwd")

    dmod = jnp.concatenate([dmod1, dmod2, dg2, dmod3], axis=1).reshape(nseq, 9 * dm)
    half = dm // 2
    jobs = _Jobs([gather_spread_job([dmod]),
                  chip_exchange_job([p_wgu2], rows=(fc // 2, fc // 2), into=far_top)])
    (g_wd1,), res = tn_matmul(a1[None], df1[None], "ffn1_dwd", job=jobs, a_width=pair)
    dmod_spread, (far_wgu2,) = jobs.split(res)
    g_wd1 = g_wd1.reshape(N_DEV, ff // N_DEV, dm)
    jobs = _Jobs([swap_job([g_wd1]), gather_forward_job(dmod_spread)])
    (g_l,), res = tn_matmul(dgu1, h1[None], "ffn1_dwgu_l", job=jobs, b_cols=(0, half), a_width=pair)
    (sw_wd1,), (dmod_all,) = jobs.split(res)
    g_l = g_l.reshape(N_DEV, fc, half)
    p_wd1, own_wd1 = pair_sum(g_wd1, sw_wd1, "pair_wd1")
    jobs = _Jobs([chip_exchange_job([p_wd1]), swap_job([g_l])])
    (g_r,), res = tn_matmul(dgu1, h1[None], "ffn1_dwgu_r", job=jobs, b_cols=(1, half), a_width=pair)
    (far_wd1,), (sw_l,) = jobs.split(res)
    g_r = g_r.reshape(N_DEV, fc, half)
    p_l, own_l = pair_sum(g_l, sw_l, "pair_wgu1_l")

    dmod_cols = lax.dynamic_slice(dmod_all.reshape(N_DEV * nseq, 9 * dm), (0, dev * ada_cols), (N_DEV * nseq, ada_cols))
    grad_w_ada, gb_cols = ada_bwd(cond_all, dmod_cols, "ada_bwd")
    dsinks = jnp.sum(dsink_rows.reshape(nseq, N_Q_HEADS, ATTN_BLOCK), axis=(0, 2))
    small = jnp.zeros((8, dm), F32)
    small = small.at[0:2].set(dln1).at[2:4].set(dln2).at[4:6].set(dln3)
    small = small.at[6, 0:N_Q_HEADS].set(dsinks).at[7, 0].set(loss_part[0, 0])

    jobs = _Jobs([chip_exchange_job([p_l]), swap_job([g_r]), gather_spread_job([small, dconvw, gb_cols])])
    (g_win,), res = tn_matmul(dproj[None], h2[None], "dwin", job=jobs)
    (far_l,), (sw_r,), small_spread = jobs.split(res)
    g_win = g_win.reshape(N_DEV, in_cols, dm)
    p_r, own_r = pair_sum(g_r, sw_r, "pair_wgu1_r")
    jobs = _Jobs([chip_exchange_job([p_r]), swap_job([g_win]), gather_forward_job(small_spread)])
    (g_wout,), res = tn_matmul(mixin[None], dmix[None], "dwout", job=jobs)
    (far_r,), (sw_win,), (small_all, dconvw_all, gb_all) = jobs.split(res)
    g_wout = g_wout.reshape(N_DEV, dm // N_DEV, dm)
    p_win, own_win = pair_sum(g_win, sw_win, "pair_win")
    jobs = _Jobs([chip_exchange_job([p_win]), swap_job([g_wout])])
    (far_win,), (sw_wout,) = jobs.split(run_job(jobs, "rs_tail_win"))
    p_wout, own_wout = pair_sum(g_wout, sw_wout, "pair_wout")
    (far_wout,) = run_job(chip_exchange_job([p_wout]), "rs_tail_wout")

    grads = {
        "ffn1_w_gate_up": jnp.concatenate([own_l, own_r], axis=1), "ffn1_w_down": own_wd1,
        "w_in": own_win, "w_out": own_wout, "ffn2_w_gate_up": own_wgu2, "ffn2_w_down": own_wd2,
    }
    others = {"ffn1_w_gate_up": jnp.concatenate([far_l, far_r], axis=2), "ffn1_w_down": far_wd1,
              "w_in": far_win, "w_out": far_wout, "ffn2_w_gate_up": far_wgu2, "ffn2_w_down": far_wd2}

    grads["w_ada"] = grad_w_ada
    small_sum = sum_devices(small_all, "sum_small")
    dconvw_sum = sum_devices(dconvw_all, "sum_convw")
    loss = small_sum[7, 0]
    grads["b_ada"] = gb_all.reshape(1, N_DEV * ada_cols)
    grads["conv_w"] = lax.dynamic_slice(dconvw_sum, (0, dev * conv_cols), (CONV_TAPS, conv_cols))
    grads["attn_sinks"] = small_sum[6:7, 0:N_Q_HEADS]
    for i, nm in enumerate(["ln1_g", "ln1_b", "ln2_g", "ln2_b", "ln3_g", "ln3_b"]):
        grads[nm] = small_sum[i:i + 1]

    given = dict(w_ada=(w_ada, m_w_ada, v_w_ada), b_ada=(b_ada, m_b_ada, v_b_ada),
                 ffn1_w_gate_up=(ffn1_w_gate_up, m_ffn1_w_gate_up, v_ffn1_w_gate_up),
                 ffn1_w_down=(ffn1_w_down, m_ffn1_w_down, v_ffn1_w_down),
                 ln1_g=(ln1_g, m_ln1_g, v_ln1_g), ln1_b=(ln1_b, m_ln1_b, v_ln1_b),
                 w_in=(w_in, m_w_in, v_w_in), conv_w=(conv_w, m_conv_w, v_conv_w),
                 attn_sinks=(attn_sinks, m_attn_sinks, v_attn_sinks), w_out=(w_out, m_w_out, v_w_out),
                 ln2_g=(ln2_g, m_ln2_g, v_ln2_g), ln2_b=(ln2_b, m_ln2_b, v_ln2_b),
                 ffn2_w_gate_up=(ffn2_w_gate_up, m_ffn2_w_gate_up, v_ffn2_w_gate_up),
                 ffn2_w_down=(ffn2_w_down, m_ffn2_w_down, v_ffn2_w_down),
                 ln3_g=(ln3_g, m_ln3_g, v_ln3_g), ln3_b=(ln3_b, m_ln3_b, v_ln3_b))
    order = ["w_ada", "b_ada", "ffn1_w_gate_up", "ffn1_w_down", "ln1_g", "ln1_b", "w_in", "conv_w", "attn_sinks",
             "w_out", "ln2_g", "ln2_b", "ffn2_w_gate_up", "ffn2_w_down", "ln3_g", "ln3_b"]
    transposed = ("ffn1_w_gate_up", "ffn2_w_gate_up", "w_in")
    big = ("w_ada", "ffn1_w_gate_up", "ffn1_w_down", "w_in", "w_out", "ffn2_w_gate_up", "ffn2_w_down")
    results = {}
    for nm in big:
        if nm in transposed:
            w2, m2, v2 = [t[0].T for t in given[nm]]
            results[nm] = [t.T[None] for t in adamw(w2, grads[nm], m2, v2, "adamw_" + nm, others=others.get(nm))]
        else:
            w2, m2, v2 = [t[0] for t in given[nm]]
            results[nm] = [t[None] for t in adamw(w2, grads[nm], m2, v2, "adamw_" + nm, others=others.get(nm))]
    small_names = [nm for nm in order if nm not in big]
    items = []
    for nm in small_names:
        shape = given[nm][0].shape
        two_d = (shape[-2], shape[-1])
        items.append((given[nm][0].reshape(two_d), grads[nm].reshape(two_d), *[t.reshape(two_d) for t in given[nm][1:]]))
    for nm, res in zip(small_names, adamw_small(items, "adamw_small")):
        shape = given[nm][0].shape
        results[nm] = [grads[nm].reshape(shape)] + [t.reshape(shape) for t in res]
    grad_x = dx0.reshape(nseq, seq, dm)
    return (loss, grad_x, *[results[nm][i] for i in range(4) for nm in order])
```

```python
import functools

import jax
import jax.numpy as jnp
from jax import lax
from jax.experimental import pallas as pl
from jax.experimental.pallas import tpu as pltpu

F32 = jnp.float32
BF16 = jnp.bfloat16
MESH = pl.DeviceIdType.MESH

N_DEV = 8
N_CHIP = 4
HEAD_DIM = 64
N_Q_HEADS = 8
N_KV_HEADS = 2
GQA_GROUP = N_Q_HEADS // N_KV_HEADS
ATTN_BLOCK = 128
ROT_DIM = 16
ROPE_THETA = 500000.0
CONV_TAPS = 3
LN_EPS = 1e-5
DN_ALPHA = 2.0 ** 0.25
ADAM_LR = 0.001
ADAM_B1 = 0.9
ADAM_B2 = 0.999
ADAM_EPS = 1e-08
ADAM_WD = 0.01
ADAM_STEP = 10
NEG_BIG = -1e30

VMEM_LIMIT = 56 * 1024 * 1024
TOKEN_TILE = 256
FFN_FWD_TILE = 512
MIX_TILE = 512
TN_VMEM_BUDGET = 36 * 1024 * 1024


def _params(semantics=None, vmem=VMEM_LIMIT):
    return pltpu.CompilerParams(dimension_semantics=semantics, vmem_limit_bytes=vmem)


def _dot(a, b):
    return jnp.dot(a, b, preferred_element_type=F32)


def _dot_nt(a, b):
    return lax.dot_general(a, b, (((1,), (1,)), ((), ())), preferred_element_type=F32)


def _dot_tn(a, b):
    return lax.dot_general(a, b, (((0,), (0,)), ((), ())), preferred_element_type=F32)


def _sigmoid(x):
    return pl.reciprocal(1.0 + jnp.exp(-x), approx=True)


def _ln_stats(r):
    mu = jnp.mean(r, axis=-1, keepdims=True)
    d = r - mu
    var = jnp.mean(d * d, axis=-1, keepdims=True)
    rstd = lax.rsqrt(var + LN_EPS)
    return d * rstd, rstd


def _ln_bwd(dy, r, g):
    xhat, rstd = _ln_stats(r)
    dxhat = dy * g
    c1 = jnp.mean(dxhat, axis=-1, keepdims=True)
    c2 = jnp.mean(dxhat * xhat, axis=-1, keepdims=True)
    dr = rstd * (dxhat - c1 - xhat * c2)
    return dr, jnp.sum(dy * xhat, axis=0, keepdims=True), jnp.sum(dy, axis=0, keepdims=True)


def _const_spec(shape):
    nd = len(shape)
    return pl.BlockSpec(shape, lambda *_: (0,) * nd, pipeline_mode=pl.Buffered(1))


def all_gather(arrs, name):
    n = len(arrs)

    def body(*refs):
        ins, outs = refs[:n], refs[n:2 * n]
        send_sems, recv_sems, local_sems = refs[2 * n:]
        x, y, c = lax.axis_index("x"), lax.axis_index("y"), lax.axis_index("c")
        me, sibling = (x, y, c), (x, y, 1 - c)
        chips = [(1 - x, y), (x, 1 - y), (1 - x, 1 - y)]

        def slot(i, p):
            return outs[i].at[4 * p[0] + 2 * p[1] + p[2]]

        def copy(i, k, block, to, src=None):
            return pltpu.make_async_remote_copy(
                src_ref=slot(i, block) if src is None else src, dst_ref=slot(i, block),
                send_sem=send_sems.at[i, k], recv_sem=recv_sems.at[i, k],
                device_id=to, device_id_type=MESH)

        mine = [pltpu.make_async_copy(ins[i], slot(i, me), local_sems.at[i]) for i in range(n)]
        for cp in mine:
            cp.start()
        first = []
        for i in range(n):
            first.append(copy(i, 0, me, sibling, src=ins[i]))
            first += [copy(i, 1 + j, me, (*chip, c), src=ins[i]) for j, chip in enumerate(chips)]
        for cp in first:
            cp.start()
        passed = []
        for i in range(n):
            for j, chip in enumerate(chips):
                copy(i, 1 + j, (*chip, c), me).wait_recv()
                cp = copy(i, 4 + j, (*chip, c), sibling)
                cp.start()
                passed.append(cp)
        for i in range(n):
            copy(i, 0, sibling, me).wait_recv()
            for j, chip in enumerate(chips):
                copy(i, 4 + j, (*chip, 1 - c), me).wait_recv()
        for cp in first + passed:
            cp.wait_send()
        for cp in mine:
            cp.wait()

    any_spec = pl.BlockSpec(memory_space=pl.ANY)
    return pl.pallas_call(
        body, name=name,
        out_shape=[jax.ShapeDtypeStruct((N_DEV, *a.shape), a.dtype) for a in arrs],
        in_specs=[any_spec] * n, out_specs=[any_spec] * n,
        scratch_shapes=[pltpu.SemaphoreType.DMA((n, 7)), pltpu.SemaphoreType.DMA((n, 7)),
                        pltpu.SemaphoreType.DMA((n,))],
    )(*arrs)


def _place():
    x, y, c = lax.axis_index("x"), lax.axis_index("y"), lax.axis_index("c")
    return x, y, c, [(1 - x, y), (x, 1 - y), (1 - x, 1 - y)]


def _slot(p):
    return 4 * p[0] + 2 * p[1] + p[2]


class _Job:
    def __init__(self, ins, outs, nsem, copies, aliases=None, local=None):
        self.ins, self.outs, self.nsem, self.copies = list(ins), list(outs), nsem, copies
        self.aliases = aliases or {}
        self.local = local

    def scratch(self):
        s = [pltpu.SemaphoreType.DMA(self.nsem), pltpu.SemaphoreType.DMA(self.nsem)]
        if self.local is not None:
            s.append(pltpu.SemaphoreType.DMA((len(self.ins),)))
        return s

    def start(self, ins, outs, sems):
        if self.local is not None:
            for cp in self.local(ins, outs, sems[2]):
                cp.start()
        for cp in self.copies(ins, outs, sems[0], sems[1])[0]:
            cp.start()

    def finish(self, ins, outs, sems):
        started, awaited = self.copies(ins, outs, sems[0], sems[1])
        for cp in awaited:
            cp.wait_recv()
        for cp in started:
            cp.wait_send()
        if self.local is not None:
            for cp in self.local(ins, outs, sems[2]):
                cp.wait()


class _Jobs:
    def __init__(self, jobs):
        self.jobs = jobs
        self.ins = [a for j in jobs for a in j.ins]
        self.outs = [o for j in jobs for o in j.outs]
        self.two_phase = any(getattr(j, "two_phase", False) for j in jobs)
        self.aliases = {}
        at_in = at_out = 0
        for j in jobs:
            self.aliases.update({at_in + i: at_out + o for i, o in j.aliases.items()})
            at_in, at_out = at_in + len(j.ins), at_out + len(j.outs)

    def scratch(self):
        return [s for j in self.jobs for s in j.scratch()]

    def _each(self, ins, outs, sems):
        at_in = at_out = at_sem = 0
        for j in self.jobs:
            n_in, n_out, n_sem = len(j.ins), len(j.outs), len(j.scratch())
            yield j, ins[at_in:at_in + n_in], outs[at_out:at_out + n_out], sems[at_sem:at_sem + n_sem]
            at_in, at_out, at_sem = at_in + n_in, at_out + n_out, at_sem + n_sem

    def start(self, ins, outs, sems):
        for j, i, o, s in self._each(ins, outs, sems):
            j.start(i, o, s)

    def turn(self, ins, outs, sems):
        for j, i, o, s in self._each(ins, outs, sems):
            if getattr(j, "two_phase", False):
                j.turn(i, o, s)

    def finish(self, ins, outs, sems):
        for j, i, o, s in self._each(ins, outs, sems):
            j.finish(i, o, s)

    def split(self, results):
        at, parts = 0, []
        for j in self.jobs:
            parts.append(results[at:at + len(j.outs)])
            at += len(j.outs)
        return parts


def _remote(src, dst, send, recv, idx, to):
    return pltpu.make_async_remote_copy(src_ref=src, dst_ref=dst, send_sem=send.at[idx], recv_sem=recv.at[idx],
                                        device_id=to, device_id_type=MESH)


def _spread_copies(ins, outs, send, recv, base=0):
    x, y, c, chips = _place()
    me = (x, y, c)
    peers = [(x, y, 1 - c)] + [(*chip, c) for chip in chips]
    started, awaited = [], []
    for i, (src, dst) in enumerate(zip(ins, outs)):
        for k, peer in enumerate(peers):
            started.append(_remote(src, dst.at[_slot(me)], send, recv, (base + i, k), peer))
            awaited.append(_remote(src, dst.at[_slot(peer)], send, recv, (base + i, k), peer))
    return started, awaited


def _forward_copies(ins, outs, send, recv, base=0):
    x, y, c, chips = _place()
    started, awaited = [], []
    for i, buf in enumerate(outs):
        for j, chip in enumerate(chips):
            mine, theirs = buf.at[_slot((*chip, c))], buf.at[_slot((*chip, 1 - c))]
            started.append(_remote(mine, mine, send, recv, (base + i, j), (x, y, 1 - c)))
            awaited.append(_remote(theirs, theirs, send, recv, (base + i, j), (x, y, 1 - c)))
    return started, awaited


def _own_block_copies(ins, outs, sems):
    x, y, c, _ = _place()
    return [pltpu.make_async_copy(src, dst.at[_slot((x, y, c))], sems.at[i])
            for i, (src, dst) in enumerate(zip(ins, outs))]


def gather_spread_job(shards):
    outs = [jax.ShapeDtypeStruct((N_DEV, *a.shape), a.dtype) for a in shards]
    return _Job(shards, outs, (len(shards), 4), _spread_copies, local=_own_block_copies)


def gather_forward_job(fulls):
    outs = [jax.ShapeDtypeStruct(a.shape, a.dtype) for a in fulls]
    return _Job(fulls, outs, (len(fulls), 3), _forward_copies, aliases={i: i for i in range(len(fulls))})


TURN_EIGHTHS = 6


class _GatherJob:
    two_phase = True

    def __init__(self, shards):
        self.ins = list(shards)
        self.outs = [jax.ShapeDtypeStruct((N_DEV, *a.shape), a.dtype) for a in shards]
        self.aliases = {}

    def scratch(self):
        n = len(self.ins)
        return [pltpu.SemaphoreType.DMA((n, 4)), pltpu.SemaphoreType.DMA((n, 4)),
                pltpu.SemaphoreType.DMA((n, 3)), pltpu.SemaphoreType.DMA((n, 3)), pltpu.SemaphoreType.DMA((n,))]

    def start(self, ins, outs, sems):
        for cp in _own_block_copies(ins, outs, sems[4]) + _spread_copies(ins, outs, sems[0], sems[1])[0]:
            cp.start()

    def turn(self, ins, outs, sems):
        for cp in _spread_copies(ins, outs, sems[0], sems[1])[1]:
            cp.wait_recv()
        for cp in _forward_copies(outs, outs, sems[2], sems[3])[0]:
            cp.start()

    def finish(self, ins, outs, sems):
        handed_on, arriving = _forward_copies(outs, outs, sems[2], sems[3])
        for cp in arriving:
            cp.wait_recv()
        for cp in _spread_copies(ins, outs, sems[0], sems[1])[0] + handed_on:
            cp.wait_send()
        for cp in _own_block_copies(ins, outs, sems[4]):
            cp.wait()


def swap_job(gs):
    def copies(ins, outs, send, recv):
        x, y, c, _ = _place()
        started, awaited = [], []
        for i, (g, r1) in enumerate(zip(ins, outs)):
            for q in range(N_CHIP):
                started.append(_remote(g.at[2 * q + (1 - c)], r1.at[q], send, recv, (i, q), (x, y, 1 - c)))
                awaited.append(_remote(g.at[2 * q + c], r1.at[q], send, recv, (i, q), (x, y, 1 - c)))
        return started, awaited

    outs = [jax.ShapeDtypeStruct((N_CHIP, *g.shape[1:]), g.dtype) for g in gs]
    return _Job(gs, outs, (len(gs), N_CHIP), copies)


def chip_exchange_job(ps, rows=None, into=None):
    n = len(ps)

    def copies(ins, outs, send, recv):
        x, y, c, chips = _place()
        started, awaited = [], []
        for i, (p, r2) in enumerate(zip(ins[:n], outs)):
            for k, chip in enumerate(chips):
                src, mine, dst = p.at[2 * chip[0] + chip[1]], p.at[2 * x + y], r2.at[k]
                if rows is not None:
                    src, mine, dst = (t.at[pl.ds(rows[0], rows[1])] for t in (src, mine, dst))
                started.append(_remote(src, dst, send, recv, (i, k), (*chip, c)))
                awaited.append(_remote(mine, dst, send, recv, (i, k), (*chip, c)))
        return started, awaited

    outs = [jax.ShapeDtypeStruct((3, *p.shape[1:]), p.dtype) for p in ps]
    if into is None:
        return _Job(ps, outs, (n, 3), copies)
    return _Job(list(ps) + list(into), outs, (n, 3), copies, aliases={n + i: i for i in range(n)})


def _call(body, job, *, name, grid, in_specs, out_specs, out_shape, args, scratch_shapes=(), vmem=VMEM_LIMIT):
    if job is None:
        res = pl.pallas_call(
            body, name=name, grid=grid, in_specs=in_specs, out_specs=out_specs, out_shape=out_shape,
            scratch_shapes=list(scratch_shapes), compiler_params=_params(("arbitrary",) * len(grid), vmem),
        )(*args)
        return res, []
    n_in, n_out, n_scr = len(in_specs), len(out_specs), len(scratch_shapes)
    j_in, j_out = len(job.ins), len(job.outs)

    def with_copies(*refs):
        at = 0
        ins = refs[at:at + n_in]; at += n_in
        jins = refs[at:at + j_in]; at += j_in
        outs = refs[at:at + n_out]; at += n_out
        jouts = refs[at:at + j_out]; at += j_out
        scr = refs[at:at + n_scr]; at += n_scr
        sems = refs[at:]
        ids = [pl.program_id(d) for d in range(len(grid))]
        first = functools.reduce(jnp.logical_and, [i == 0 for i in ids])
        last = functools.reduce(jnp.logical_and, [i == n - 1 for i, n in zip(ids, grid)])

        @pl.when(first)
        def _():
            job.start(jins, jouts, sems)

        if getattr(job, "two_phase", False):
            steps, at = 1, 0
            for i, n in zip(ids, grid):
                steps, at = steps * n, at * n + i

            @pl.when(at == (TURN_EIGHTHS * steps) // 8)
            def _():
                job.turn(jins, jouts, sems)

        body(*ins, *outs, *scr)

        @pl.when(last)
        def _():
            job.finish(jins, jouts, sems)

    any_spec = pl.BlockSpec(memory_space=pl.ANY)
    res = pl.pallas_call(
        with_copies, name=name, grid=grid,
        in_specs=list(in_specs) + [any_spec] * j_in, out_specs=list(out_specs) + [any_spec] * j_out,
        out_shape=list(out_shape) + list(job.outs),
        input_output_aliases={n_in + i: n_out + o for i, o in job.aliases.items()},
        scratch_shapes=list(scratch_shapes) + job.scratch(),
        compiler_params=_params(("arbitrary",) * len(grid), vmem),
    )(*args, *job.ins)
    return res[:n_out], res[n_out:]


def run_job(job, name):
    def body(*refs):
        j_in, j_out = len(job.ins), len(job.outs)
        ins, outs, sems = refs[:j_in], refs[j_in:j_in + j_out], refs[j_in + j_out:]
        job.start(ins, outs, sems)
        job.finish(ins, outs, sems)

    any_spec = pl.BlockSpec(memory_space=pl.ANY)
    return pl.pallas_call(
        body, name=name, in_specs=[any_spec] * len(job.ins), out_specs=[any_spec] * len(job.outs),
        out_shape=list(job.outs), input_output_aliases=dict(job.aliases), scratch_shapes=job.scratch(),
    )(*job.ins)


def pair_sum(g, r1, name):
    _, rows, cols = g.shape
    rb = next(cand for cand in range(min(rows, 512), 0, -16) if rows % cand == 0)

    def body(g_ref, r1_ref, p_ref, own_ref):
        x, y, c, _ = _place()
        s = g_ref[c].astype(F32) + r1_ref[0].astype(F32)
        p_ref[0] = s.astype(BF16)

        @pl.when(pl.program_id(1) == 2 * x + y)
        def _():
            own_ref[...] = s

    return pl.pallas_call(
        body, name=name, grid=(rows // rb, N_CHIP),
        in_specs=[pl.BlockSpec((2, rb, cols), lambda i, q: (q, i, 0)), pl.BlockSpec((1, rb, cols), lambda i, q: (q, i, 0))],
        out_specs=[pl.BlockSpec((1, rb, cols), lambda i, q: (q, i, 0)), pl.BlockSpec((rb, cols), lambda i, q: (i, 0))],
        out_shape=[jax.ShapeDtypeStruct((N_CHIP, rows, cols), BF16), jax.ShapeDtypeStruct((rows, cols), F32)],
        compiler_params=_params(("arbitrary", "arbitrary")),
    )(g, r1)


def sum_devices(a, name):
    def body(a_ref, o_ref):
        acc = a_ref[0]
        for d in range(1, N_DEV):
            acc = acc + a_ref[d]
        o_ref[...] = acc

    return pl.pallas_call(body, name=name, out_shape=jax.ShapeDtypeStruct(a.shape[1:], F32))(a)


def _adam_update(w, g, m, v):
    nm = ADAM_B1 * m + (1.0 - ADAM_B1) * g
    nv = ADAM_B2 * v + (1.0 - ADAM_B2) * (g * g)
    m_hat = nm / (1.0 - ADAM_B1 ** ADAM_STEP)
    v_hat = nv / (1.0 - ADAM_B2 ** ADAM_STEP)
    return -ADAM_LR * (m_hat / (jnp.sqrt(v_hat) + ADAM_EPS) + ADAM_WD * w), nm, nv


def adamw(w, g, m, v, name, others=None):
    rows, cols = w.shape
    rb = rows
    for cand in range(min(rows, 512), 7, -8):
        if rows % cand == 0 and cand % 8 == 0:
            rb = cand
            break

    def body(*refs):
        if others is None:
            w_ref, g_ref, m_ref, v_ref, d_ref, nm_ref, nv_ref = refs
            gg = g_ref[...]
        else:
            w_ref, g_ref, m_ref, v_ref, r2_ref, go_ref, d_ref, nm_ref, nv_ref = refs
            gg = g_ref[...]
            for k in range(3):
                gg = gg + r2_ref[k].astype(F32)
            go_ref[...] = gg
        d_ref[...], nm_ref[...], nv_ref[...] = _adam_update(w_ref[...], gg, m_ref[...], v_ref[...])

    spec = pl.BlockSpec((rb, cols), lambda i: (i, 0))
    out = jax.ShapeDtypeStruct((rows, cols), F32)
    in_specs, args = [spec] * 4, [w, g, m, v]
    if others is not None:
        in_specs.append(pl.BlockSpec((3, rb, cols), lambda i: (0, i, 0)))
        args.append(others)
    n_out = 3 if others is None else 4
    res = pl.pallas_call(
        body, name=name, grid=(rows // rb,), in_specs=in_specs, out_specs=[spec] * n_out,
        out_shape=[out] * n_out, compiler_params=_params(("parallel",)),
    )(*args)
    return (g, *res) if others is None else tuple(res)


def adamw_small(items, name):
    n = len(items)

    def body(*refs):
        ins, outs = refs[:4 * n], refs[4 * n:]
        for i in range(n):
            w_ref, g_ref, m_ref, v_ref = ins[4 * i:4 * i + 4]
            d_ref, nm_ref, nv_ref = outs[3 * i:3 * i + 3]
            d_ref[...], nm_ref[...], nv_ref[...] = _adam_update(w_ref[...], g_ref[...], m_ref[...], v_ref[...])

    res = pl.pallas_call(
        body, name=name,
        out_shape=[jax.ShapeDtypeStruct(w.shape, F32) for w, _, _, _ in items for _ in range(3)],
    )(*[t for item in items for t in item])
    return [tuple(res[3 * i:3 * i + 3]) for i in range(n)]


def ada_fwd(c_all, w_cols, b_cols, name):
    def body(c_ref, w_ref, b_ref, cond_ref, mod_ref):
        cc = c_ref[...]
        cond = (cc * _sigmoid(cc)).astype(BF16)
        cond_ref[...] = cond
        mod_ref[...] = _dot(cond, w_ref[...].astype(BF16)) + b_ref[...]

    n, cols = c_all.shape[0], w_cols.shape[1]
    return pl.pallas_call(
        body, name=name,
        out_shape=[jax.ShapeDtypeStruct(c_all.shape, BF16), jax.ShapeDtypeStruct((n, cols), F32)],
        compiler_params=_params(),
    )(c_all, w_cols, b_cols)


def ada_bwd(cond_all, dmod_cols, name):
    def body(c_ref, d_ref, gw_ref, gb_ref):
        d = d_ref[...]
        gw_ref[...] = _dot_tn(c_ref[...], d.astype(BF16))
        gb_ref[...] = jnp.sum(d, axis=0, keepdims=True)

    dm, cols = cond_all.shape[1], dmod_cols.shape[1]
    return pl.pallas_call(
        body, name=name,
        out_shape=[jax.ShapeDtypeStruct((dm, cols), F32), jax.ShapeDtypeStruct((1, cols), F32)],
        compiler_params=_params(),
    )(cond_all, dmod_cols)


MXU_COLS = 256
FFN_CHUNK = 4 * MXU_COLS


def _hidden_chunks(ff):
    assert ff % MXU_COLS == 0
    return [(at, min(FFN_CHUNK, ff - at)) for at in range(0, ff, FFN_CHUNK)]


def _mod_spec(tiles_per_seq, dm):
    return pl.BlockSpec((1, 1, dm), lambda i: (i // tiles_per_seq, 0, 0))


def ffn_fwd(x, sh, sc, gt, wgu, wd, ln_g, ln_b, seq, name, target=None, job=None):
    tokens, dm = x.shape
    ff = wgu.shape[1]
    chunks = _hidden_chunks(ff)
    tm = min(FFN_FWD_TILE, seq)
    tiles_per_seq = seq // tm
    with_loss = target is not None

    def body(*refs):
        if with_loss:
            (x_ref, sh_ref, sc_ref, gt_ref, wgu_ref, wd_ref, lg_ref, lb_ref, t_ref,
             xo_ref, loss_ref, r_ref, gu_ref, f_ref) = refs
        else:
            (x_ref, sh_ref, sc_ref, gt_ref, wgu_ref, wd_ref, lg_ref, lb_ref,
             xo_ref, r_ref, gu_ref, f_ref) = refs
        xx = x_ref[...]
        h = (xx * (1.0 + sc_ref[0]) + sh_ref[0]).astype(BF16)
        acc = jnp.zeros((tm, dm), F32)
        for at, wdt in chunks:
            gk = _dot_nt(h, wgu_ref[0, at:at + wdt, :])
            uk = _dot_nt(h, wgu_ref[1, at:at + wdt, :])
            gu_ref[0, :, at:at + wdt] = gk.astype(BF16)
            gu_ref[1, :, at:at + wdt] = uk.astype(BF16)
            a = (gk * _sigmoid(gk) * uk).astype(BF16)
            acc = acc + _dot(a, wd_ref[at:at + wdt, :])
        f_ref[...] = acc.astype(BF16)
        r = DN_ALPHA * xx + (0.5 * (1.0 + gt_ref[0])) * acc
        r_ref[...] = r
        xhat, _ = _ln_stats(r)
        yy = xhat * lg_ref[...] + lb_ref[...]
        if with_loss:
            err = yy - t_ref[...]
            xo_ref[...] = err * (1.0 / dm)

            @pl.when(pl.program_id(0) == 0)
            def _():
                loss_ref[...] = jnp.zeros_like(loss_ref)

            loss_ref[...] += jnp.full((1, 128), (0.5 / dm) * jnp.sum(err * err), F32)
        else:
            xo_ref[...] = yy

    tile = pl.BlockSpec((tm, dm), lambda i: (i, 0))
    mod = _mod_spec(tiles_per_seq, dm)
    in_specs = [tile, mod, mod, mod, _const_spec(wgu.shape), _const_spec(wd.shape),
                _const_spec((1, dm)), _const_spec((1, dm))]
    args = [x, sh, sc, gt, wgu, wd, ln_g, ln_b]
    out_specs = [tile]
    out_shape = [jax.ShapeDtypeStruct((tokens, dm), F32)]
    if with_loss:
        in_specs.append(tile)
        args.append(target)
        out_specs.append(pl.BlockSpec((1, 128), lambda i: (0, 0)))
        out_shape.append(jax.ShapeDtypeStruct((1, 128), F32))
    out_specs += [tile, pl.BlockSpec((2, tm, ff), lambda i: (0, i, 0)), tile]
    out_shape += [jax.ShapeDtypeStruct((tokens, dm), F32), jax.ShapeDtypeStruct((2, tokens, ff), BF16),
                  jax.ShapeDtypeStruct((tokens, dm), BF16)]
    return _call(body, job, name=name, grid=(tokens // tm,), in_specs=in_specs, out_specs=out_specs,
                 out_shape=out_shape, args=args)


def ffn_up(x, sh, sc, wgu, seq, name, job=None):
    tokens, dm = x.shape
    ff = wgu.shape[1]
    chunks = _hidden_chunks(ff)
    tm = min(FFN_FWD_TILE, seq)

    def body(x_ref, sh_ref, sc_ref, wgu_ref, gu_ref, a_ref):
        h = (x_ref[...] * (1.0 + sc_ref[0]) + sh_ref[0]).astype(BF16)
        for at, wdt in chunks:
            gk = _dot_nt(h, wgu_ref[0, at:at + wdt, :])
            uk = _dot_nt(h, wgu_ref[1, at:at + wdt, :])
            gu_ref[0, :, at:at + wdt] = gk.astype(BF16)
            gu_ref[1, :, at:at + wdt] = uk.astype(BF16)
            a_ref[:, at:at + wdt] = (gk * _sigmoid(gk) * uk).astype(BF16)

    tile = pl.BlockSpec((tm, dm), lambda i: (i, 0))
    mod = _mod_spec(seq // tm, dm)
    return _call(
        body, job, name=name, grid=(tokens // tm,),
        in_specs=[tile, mod, mod, _const_spec(wgu.shape)],
        out_specs=[pl.BlockSpec((2, tm, ff), lambda i: (0, i, 0)), pl.BlockSpec((tm, ff), lambda i: (i, 0))],
        out_shape=[jax.ShapeDtypeStruct((2, tokens, ff), BF16), jax.ShapeDtypeStruct((tokens, ff), BF16)],
        args=(x, sh, sc, wgu))


def ffn_down(x, a, gt, wd, ln_g, ln_b, seq, name, job=None):
    tokens, dm = x.shape
    ff = wd.shape[0]
    chunks = _hidden_chunks(ff)
    tm = min(FFN_FWD_TILE, seq)

    def body(x_ref, a_ref, gt_ref, wd_ref, lg_ref, lb_ref, xo_ref, r_ref, f_ref):
        acc = jnp.zeros((tm, dm), F32)
        for at, wdt in chunks:
            acc = acc + _dot(a_ref[:, at:at + wdt], wd_ref[at:at + wdt, :])
        f_ref[...] = acc.astype(BF16)
        r = DN_ALPHA * x_ref[...] + (0.5 * (1.0 + gt_ref[0])) * acc
        r_ref[...] = r
        xhat, _ = _ln_stats(r)
        xo_ref[...] = xhat * lg_ref[...] + lb_ref[...]

    tile = pl.BlockSpec((tm, dm), lambda i: (i, 0))
    return _call(
        body, job, name=name, grid=(tokens // tm,),
        in_specs=[tile, pl.BlockSpec((tm, ff), lambda i: (i, 0)), _mod_spec(seq // tm, dm), _const_spec(wd.shape),
                  _const_spec((1, dm)), _const_spec((1, dm))],
        out_specs=[tile, tile, tile],
        out_shape=[jax.ShapeDtypeStruct((tokens, dm), F32), jax.ShapeDtypeStruct((tokens, dm), F32),
                   jax.ShapeDtypeStruct((tokens, dm), BF16)],
        args=(x, a, gt, wd, ln_g, ln_b))


def ffn_bwd(dy, r, x, f, gu, sh, sc, gt, wgu, wd, ln_g, seq, name, job=None):
    tokens, dm = x.shape
    ff = wgu.shape[1]
    chunks = _hidden_chunks(ff)
    tm = min(TOKEN_TILE, seq)
    tiles_per_seq = seq // tm
    nseq = tokens // seq

    def body(dy_ref, r_ref, x_ref, f_ref, gu_ref, sh_ref, sc_ref, gt_ref, wgu_ref, wd_ref, lg_ref,
             dx_ref, dgu_ref, df_ref, a_ref, h_ref, dln_ref, dmod_ref):
        i = pl.program_id(0)
        dr, dgain, dbias = _ln_bwd(dy_ref[...], r_ref[...], lg_ref[...])

        @pl.when(i == 0)
        def _():
            dln_ref[...] = jnp.zeros_like(dln_ref)

        @pl.when(i % tiles_per_seq == 0)
        def _():
            dmod_ref[...] = jnp.zeros_like(dmod_ref)

        dln_ref[0:1, :] += dgain
        dln_ref[1:2, :] += dbias
        df32 = (0.5 * (1.0 + gt_ref[0])) * dr
        df = df32.astype(BF16)
        df_ref[...] = df
        dgate = jnp.sum(dr * (0.5 * f_ref[...].astype(F32)), axis=0, keepdims=True)
        xx = x_ref[...]
        one_sc = 1.0 + sc_ref[0]
        h = (xx * one_sc + sh_ref[0]).astype(BF16)
        h_ref[...] = h
        dh = jnp.zeros((tm, dm), F32)
        for at, wdt in chunks:
            cols = slice(at, at + wdt)
            da = _dot_nt(df, wd_ref[cols, :])
            gk = gu_ref[0, :, cols].astype(F32)
            uk = gu_ref[1, :, cols].astype(F32)
            sg = _sigmoid(gk)
            sil = gk * sg
            a_ref[:, cols] = (sil * uk).astype(BF16)
            du = (da * sil).astype(BF16)
            dg = (da * uk * (sg * (1.0 + gk * (1.0 - sg)))).astype(BF16)
            dgu_ref[0, :, cols] = dg
            dgu_ref[1, :, cols] = du
            dh = dh + _dot(dg, wgu_ref[0, cols, :]) + _dot(du, wgu_ref[1, cols, :])
        dx_ref[...] = DN_ALPHA * dr + dh * one_sc
        dmod_ref[0, 0:1, :] += jnp.sum(dh, axis=0, keepdims=True)
        dmod_ref[0, 1:2, :] += jnp.sum(dh * xx, axis=0, keepdims=True)
        dmod_ref[0, 2:3, :] += dgate

    tile = pl.BlockSpec((tm, dm), lambda i: (i, 0))
    mod = _mod_spec(tiles_per_seq, dm)
    gu_spec = pl.BlockSpec((2, tm, ff), lambda i: (0, i, 0))
    return _call(
        body, job, name=name, grid=(tokens // tm,),
        in_specs=[tile, tile, tile, tile, gu_spec, mod, mod, mod, _const_spec(wgu.shape), _const_spec(wd.shape),
                  _const_spec((1, dm))],
        out_specs=[tile, gu_spec, tile, pl.BlockSpec((tm, ff), lambda i: (i, 0)), tile,
                   pl.BlockSpec((2, dm), lambda i: (0, 0)),
                   pl.BlockSpec((1, 3, dm), lambda i: (i // tiles_per_seq, 0, 0))],
        out_shape=[jax.ShapeDtypeStruct((tokens, dm), F32), jax.ShapeDtypeStruct((2, tokens, ff), BF16),
                   jax.ShapeDtypeStruct((tokens, dm), BF16), jax.ShapeDtypeStruct((tokens, ff), BF16),
                   jax.ShapeDtypeStruct((tokens, dm), BF16), jax.ShapeDtypeStruct((2, dm), F32),
                   jax.ShapeDtypeStruct((nseq, 3, dm), F32)],
        args=(dy, r, x, f, gu, sh, sc, gt, wgu, wd, ln_g))


def tn_matmul(a, b, name, job=None, b_cols=None, a_width=None):
    na, tokens, k_all = a.shape
    kk = k_all if a_width is None else a_width
    nka = k_all // kk
    assert nka * kk == k_all
    nb, _, cc = b.shape
    col = 0
    if b_cols is not None:
        col, cc = b_cols
    tt = tokens
    while 4 * tt * (kk + cc) + 8 * kk * cc > TN_VMEM_BUDGET and tt % 2 == 0 and tt > 256:
        tt //= 2
    steps = tokens // tt

    def body(a_ref, b_ref, o_ref, *acc):
        if steps == 1:
            o_ref[0, 0, 0] = _dot_tn(a_ref[0], b_ref[0]).astype(BF16)
            return
        acc_ref, = acc
        t = pl.program_id(3)

        @pl.when(t == 0)
        def _():
            acc_ref[...] = jnp.zeros_like(acc_ref)

        acc_ref[...] += _dot_tn(a_ref[0], b_ref[0])

        @pl.when(t == steps - 1)
        def _():
            o_ref[0, 0, 0] = acc_ref[...].astype(BF16)

    return _call(
        body, job, name=name, grid=(na, nka, nb, steps),
        in_specs=[pl.BlockSpec((1, tt, kk), lambda i, s, j, t: (i, t, s)),
                  pl.BlockSpec((1, tt, cc), lambda i, s, j, t: (j, t, col))],
        out_specs=[pl.BlockSpec((1, 1, 1, kk, cc), lambda i, s, j, t: (i, s, j, 0, 0))],
        out_shape=[jax.ShapeDtypeStruct((na, nka, nb, kk, cc), BF16)],
        scratch_shapes=[] if steps == 1 else [pltpu.VMEM((kk, cc), F32)], args=(a, b))


def proj_fwd(x1, sh, sc, w_in, seq, name, job=None):
    tokens, dm = x1.shape
    tm = min(MIX_TILE, seq)
    tiles_per_seq = seq // tm
    widths = [N_Q_HEADS * HEAD_DIM, N_KV_HEADS * HEAD_DIM, N_KV_HEADS * HEAD_DIM, 512, 512, 512]
    assert sum(widths) == w_in.shape[0]

    def body(x_ref, sh_ref, sc_ref, w_ref, *outs):
        h = (x_ref[...] * (1.0 + sc_ref[0]) + sh_ref[0]).astype(BF16)
        proj = _dot_nt(h, w_ref[...])
        at = 0
        for o_ref, wdt in zip(outs, widths):
            o_ref[...] = proj[:, at:at + wdt]
            at += wdt

    tile = pl.BlockSpec((tm, dm), lambda i: (i, 0))
    mod = _mod_spec(tiles_per_seq, dm)
    return _call(
        body, job, name=name, grid=(tokens // tm,),
        in_specs=[tile, mod, mod, _const_spec(w_in.shape)],
        out_specs=[pl.BlockSpec((tm, wdt), lambda i: (i, 0)) for wdt in widths],
        out_shape=[jax.ShapeDtypeStruct((tokens, wdt), F32) for wdt in widths],
        args=(x1, sh, sc, w_in))


LANES = 2 * HEAD_DIM


def _head_lane(shape):
    return lax.broadcasted_iota(jnp.int32, shape, 1) % HEAD_DIM


def _lane_half(shape):
    return lax.broadcasted_iota(jnp.int32, shape, 1) // HEAD_DIM


def _swap_rot(v):
    lane = _head_lane(v.shape)
    half = ROT_DIM // 2
    return jnp.where(lane < half, pltpu.roll(v, LANES - half, 1),
                     jnp.where(lane < ROT_DIM, pltpu.roll(v, half, 1), 0.0))


def _rope(v, cos_t, sin_t):
    return v * cos_t + _swap_rot(v) * sin_t


def _unrope(dv, cos_t, sin_t):
    return dv * cos_t + _swap_rot(dv * sin_t)


def _both_halves(t, g):
    return jnp.where(_lane_half(t.shape) == g, t, pltpu.roll(t, HEAD_DIM, 1))


def _fold_halves(t, g):
    return jnp.where(_lane_half(t.shape) == g, t + pltpu.roll(t, HEAD_DIM, 1), 0.0)


def _stack_heads(blocks):
    rows = []
    for blk in blocks:
        half = _lane_half(blk.shape)
        rows += [jnp.where(half == 0, blk, 0.0), jnp.where(half == 1, blk, 0.0)]
    return jnp.concatenate(rows, axis=0)


def _unstack_heads(t, j):
    lo = t[(2 * j) * ATTN_BLOCK:(2 * j + 1) * ATTN_BLOCK]
    hi = t[(2 * j + 1) * ATTN_BLOCK:(2 * j + 2) * ATTN_BLOCK]
    return jnp.where(_lane_half(lo.shape) == 0, lo, hi)


def _band_mask(q0, w0):
    rows, cols = GQA_GROUP * ATTN_BLOCK, 2 * ATTN_BLOCK
    qi = lax.broadcasted_iota(jnp.int32, (rows, cols), 0) % ATTN_BLOCK + q0
    ki = lax.broadcasted_iota(jnp.int32, (rows, cols), 1) + w0
    diff = qi - ki
    return (diff >= 0) & (diff < ATTN_BLOCK)


def _attn_specs(seq):
    q_spec = pl.BlockSpec((seq, GQA_GROUP * HEAD_DIM), lambda b, g: (b, g))
    kv_spec = pl.BlockSpec((seq, LANES), lambda b, g: (b, 0))
    sink_spec = pl.BlockSpec((1, GQA_GROUP * ATTN_BLOCK, 1), lambda b, g: (g, 0, 0))
    return q_spec, kv_spec, sink_spec


def _block_starts(n):
    q0 = pl.multiple_of(n * ATTN_BLOCK, ATTN_BLOCK)
    w0 = pl.multiple_of(jnp.maximum(n - 1, 0) * ATTN_BLOCK, ATTN_BLOCK)
    return q0, w0


def _stacked_queries(ref, rows):
    return _stack_heads([ref[rows, j * LANES:(j + 1) * LANES] for j in range(2)]).astype(BF16)


def _sink_columns(sinks):
    return jnp.repeat(sinks.reshape(N_KV_HEADS, GQA_GROUP), ATTN_BLOCK, axis=1)[:, :, None]


def _probs_spec(nblk):
    return pl.BlockSpec((1, 1, nblk, GQA_GROUP * ATTN_BLOCK, 2 * ATTN_BLOCK), lambda b, g: (b, g, 0, 0, 0))


def _sink_probs_spec():
    return pl.BlockSpec((1, 1, GQA_GROUP * ATTN_BLOCK, LANES), lambda b, g: (b, g, 0, 0))


def attn_fwd(q, k, v, cos_t, sin_t, sinks, seq, name, job=None):
    tokens = q.shape[0]
    nblk = seq // ATTN_BLOCK
    assert nblk >= 2
    scale = HEAD_DIM ** -0.5

    nseq = tokens // seq
    rows_stacked = GQA_GROUP * ATTN_BLOCK
    assert nblk <= LANES

    def body(q_ref, k_ref, v_ref, cos_ref, sin_ref, sink_ref, o_ref, qr_ref, p_ref, ps_ref, kd_ref, vd_ref):
        g = pl.program_id(1)
        kd_ref[...] = _both_halves(_rope(k_ref[...], cos_ref[...], sin_ref[...]), g).astype(BF16)
        vd_ref[...] = _both_halves(v_ref[...], g).astype(BF16)
        sink = sink_ref[0]
        lane = lax.broadcasted_iota(jnp.int32, (rows_stacked, LANES), 1)

        ps_ref[...] = jnp.zeros_like(ps_ref)

        def block(n, carry):
            q0, w0 = _block_starts(n)
            rows, win = pl.ds(q0, ATTN_BLOCK), pl.ds(w0, 2 * ATTN_BLOCK)
            blocks = []
            for j in range(2):
                qr = _rope(q_ref[rows, j * LANES:(j + 1) * LANES], cos_ref[rows, :], sin_ref[rows, :]).astype(BF16)
                qr_ref[rows, j * LANES:(j + 1) * LANES] = qr
                blocks.append(qr)
            qs = _stack_heads(blocks)
            s = _dot_nt(qs, kd_ref[win, :]) * scale
            s = jnp.where(_band_mask(q0, w0), s, NEG_BIG)
            m = jnp.maximum(jnp.max(s, axis=-1, keepdims=True), sink)
            p = jnp.exp(s - m)
            e_sink = jnp.exp(sink - m)
            inv = pl.reciprocal(jnp.sum(p, axis=-1, keepdims=True) + e_sink, approx=True)
            pn = (p * inv).astype(BF16)
            p_ref[0, 0, n] = pn
            out = _dot(pn, vd_ref[win, :])
            for j in range(2):
                o_ref[rows, j * LANES:(j + 1) * LANES] = _unstack_heads(out, j).astype(o_ref.dtype)
            ps_ref[0, 0] = jnp.where(lane == n, e_sink * inv, ps_ref[0, 0])
            return carry

        lax.fori_loop(0, nblk, block, 0, unroll=2)

    q_spec, kv_spec, sink_spec = _attn_specs(seq)
    return _call(
        body, job, name=name, grid=(nseq, N_KV_HEADS),
        in_specs=[q_spec, kv_spec, kv_spec, kv_spec, kv_spec, sink_spec],
        out_specs=[q_spec, q_spec, _probs_spec(nblk), _sink_probs_spec()],
        out_shape=[jax.ShapeDtypeStruct(q.shape, BF16), jax.ShapeDtypeStruct(q.shape, BF16),
                   jax.ShapeDtypeStruct((nseq, N_KV_HEADS, nblk, rows_stacked, 2 * ATTN_BLOCK), BF16),
                   jax.ShapeDtypeStruct((nseq, N_KV_HEADS, rows_stacked, LANES), F32)],
        scratch_shapes=[pltpu.VMEM((seq, LANES), BF16), pltpu.VMEM((seq, LANES), BF16)],
        args=(q, k, v, cos_t, sin_t, _sink_columns(sinks)))


def attn_bwd(qr, k, v, do, probs, sink_probs, cos_t, sin_t, seq, name, job=None):
    tokens = qr.shape[0]
    nseq = tokens // seq
    nblk = seq // ATTN_BLOCK
    assert nblk >= 2
    rows_stacked = GQA_GROUP * ATTN_BLOCK
    scale = HEAD_DIM ** -0.5

    def body(q_ref, k_ref, v_ref, do_ref, p_ref, ps_ref, cos_ref, sin_ref, dq_ref, dk_ref, dv_ref, ds_ref,
             kd_ref, vd_ref, dkd_ref, dvd_ref, acc_ref):
        g = pl.program_id(1)
        kd_ref[...] = _both_halves(_rope(k_ref[...], cos_ref[...], sin_ref[...]), g).astype(BF16)
        vd_ref[...] = _both_halves(v_ref[...], g).astype(BF16)
        dkd_ref[...] = jnp.zeros_like(dkd_ref)
        dvd_ref[...] = jnp.zeros_like(dvd_ref)
        acc_ref[...] = jnp.zeros_like(acc_ref)
        lane = lax.broadcasted_iota(jnp.int32, (rows_stacked, LANES), 1)

        def block(n, carry):
            q0, w0 = _block_starts(n)
            rows, win = pl.ds(q0, ATTN_BLOCK), pl.ds(w0, 2 * ATTN_BLOCK)
            qs = _stacked_queries(q_ref, rows)
            dos = _stacked_queries(do_ref, rows)
            kw, vw = kd_ref[win, :], vd_ref[win, :]
            pn16 = p_ref[0, 0, n]
            pn = pn16.astype(F32)
            dvd_ref[win, :] += _dot_tn(pn16, dos)
            dp = _dot_nt(dos, vw)
            delta = jnp.sum(dp * pn, axis=-1, keepdims=True)
            ds = (pn * (dp - delta)).astype(BF16)
            dqs = _dot(ds, kw) * scale
            dkd_ref[win, :] += _dot_tn(ds, qs) * scale
            cos_b, sin_b = cos_ref[rows, :], sin_ref[rows, :]
            for j in range(2):
                dq_ref[rows, j * LANES:(j + 1) * LANES] = _unrope(_unstack_heads(dqs, j), cos_b, sin_b).astype(BF16)
            acc_ref[...] += jnp.where(lane == n, ps_ref[0, 0] * delta, 0.0)
            return carry

        lax.fori_loop(0, nblk // 2, lambda i, carry: block(2 * i + 1, block(2 * i, carry)), 0)
        ds_ref[0, 0] = -jnp.sum(acc_ref[...], axis=-1, keepdims=True)
        dk_g = _unrope(_fold_halves(dkd_ref[...], g), cos_ref[...], sin_ref[...])
        dv_g = _fold_halves(dvd_ref[...], g)

        @pl.when(g == 0)
        def _():
            dk_ref[...] = dk_g
            dv_ref[...] = dv_g

        @pl.when(g != 0)
        def _():
            dk_ref[...] += dk_g
            dv_ref[...] += dv_g

    q_spec, kv_spec, _ = _attn_specs(seq)
    return _call(
        body, job, name=name, grid=(nseq, N_KV_HEADS),
        in_specs=[q_spec, kv_spec, kv_spec, q_spec, _probs_spec(nblk), _sink_probs_spec(), kv_spec, kv_spec],
        out_specs=[q_spec, kv_spec, kv_spec, pl.BlockSpec((1, 1, rows_stacked, 1), lambda b, g: (b, g, 0, 0))],
        out_shape=[jax.ShapeDtypeStruct(qr.shape, BF16), jax.ShapeDtypeStruct(k.shape, F32),
                   jax.ShapeDtypeStruct(k.shape, F32), jax.ShapeDtypeStruct((nseq, N_KV_HEADS, rows_stacked, 1), F32)],
        scratch_shapes=[pltpu.VMEM((seq, LANES), BF16), pltpu.VMEM((seq, LANES), BF16),
                        pltpu.VMEM((seq, LANES), F32), pltpu.VMEM((seq, LANES), F32),
                        pltpu.VMEM((rows_stacked, LANES), F32)],
        args=(qr, k, v, do, probs, sink_probs, cos_t, sin_t))


CONV_COLS = 128


def _shift_down(z, by):
    t = lax.broadcasted_iota(jnp.int32, z.shape, 0)
    return jnp.where(t >= by, pltpu.roll(z, by, 0), 0.0)


def _shift_up(z, by):
    n = z.shape[0]
    t = lax.broadcasted_iota(jnp.int32, z.shape, 0)
    return jnp.where(t < n - by, pltpu.roll(z, n - by, 0), 0.0)


def conv_fwd(u, bg, cg, conv_w, seq, name):
    tokens, width = u.shape

    def body(u_ref, bg_ref, cg_ref, w_ref, o_ref):
        z = cg_ref[...] * u_ref[...]
        yy = w_ref[2:3, :] * z + w_ref[1:2, :] * _shift_down(z, 1) + w_ref[0:1, :] * _shift_down(z, 2)
        o_ref[...] = (bg_ref[...] * yy).astype(BF16)

    col = pl.BlockSpec((seq, CONV_COLS), lambda j, b: (b, j))
    return pl.pallas_call(
        body, name=name, grid=(width // CONV_COLS, tokens // seq),
        in_specs=[col, col, col, pl.BlockSpec((CONV_TAPS, CONV_COLS), lambda j, b: (0, j))],
        out_specs=col, out_shape=jax.ShapeDtypeStruct((tokens, width), BF16),
        compiler_params=_params(("parallel", "parallel")),
    )(u, bg, cg, conv_w)


def conv_bwd(dout, u, bg, cg, conv_w, seq, name):
    tokens, width = u.shape

    def body(do_ref, u_ref, bg_ref, cg_ref, w_ref, du_ref, dbg_ref, dcg_ref, dw_ref):
        uu, cg_v, do = u_ref[...], cg_ref[...], do_ref[...].astype(F32)
        z = cg_v * uu
        z1, z2 = _shift_down(z, 1), _shift_down(z, 2)
        yy = w_ref[2:3, :] * z + w_ref[1:2, :] * z1 + w_ref[0:1, :] * z2
        dbg_ref[...] = (do * yy).astype(BF16)
        dyy = do * bg_ref[...]
        dz = w_ref[2:3, :] * dyy + w_ref[1:2, :] * _shift_up(dyy, 1) + w_ref[0:1, :] * _shift_up(dyy, 2)
        du_ref[...] = (dz * cg_v).astype(BF16)
        dcg_ref[...] = (dz * uu).astype(BF16)

        @pl.when(pl.program_id(1) == 0)
        def _():
            dw_ref[...] = jnp.zeros_like(dw_ref)

        dw_ref[0:1, :] += jnp.sum(dyy * z2, axis=0, keepdims=True)
        dw_ref[1:2, :] += jnp.sum(dyy * z1, axis=0, keepdims=True)
        dw_ref[2:3, :] += jnp.sum(dyy * z, axis=0, keepdims=True)

    col = pl.BlockSpec((seq, CONV_COLS), lambda j, b: (b, j))
    w_spec = pl.BlockSpec((CONV_TAPS, CONV_COLS), lambda j, b: (0, j))
    act = jax.ShapeDtypeStruct((tokens, width), BF16)
    return pl.pallas_call(
        body, name=name, grid=(width // CONV_COLS, tokens // seq),
        in_specs=[col, col, col, col, w_spec], out_specs=[col, col, col, w_spec],
        out_shape=[act, act, act, jax.ShapeDtypeStruct((CONV_TAPS, width), F32)],
        compiler_params=_params(("parallel", "arbitrary")),
    )(dout, u, bg, cg, conv_w)


def out_fwd(x1, attn, conv, gt, w_out, ln_g, ln_b, seq, name, job=None):
    tokens, dm = x1.shape
    half = attn.shape[1]
    tm = min(MIX_TILE, seq)
    tiles_per_seq = seq // tm

    def body(x_ref, a_ref, c_ref, gt_ref, w_ref, lg_ref, lb_ref, xo_ref, r_ref, mi_ref, mix_ref):
        mixin = jnp.concatenate([a_ref[...], c_ref[...]], axis=1).astype(BF16)
        mi_ref[...] = mixin
        mix = _dot(mixin, w_ref[...])
        mix_ref[...] = mix.astype(BF16)
        r = DN_ALPHA * x_ref[...] + (1.0 + gt_ref[0]) * mix
        r_ref[...] = r
        xhat, _ = _ln_stats(r)
        xo_ref[...] = xhat * lg_ref[...] + lb_ref[...]

    tile = pl.BlockSpec((tm, dm), lambda i: (i, 0))
    htile = pl.BlockSpec((tm, half), lambda i: (i, 0))
    return _call(
        body, job, name=name, grid=(tokens // tm,),
        in_specs=[tile, htile, htile, _mod_spec(tiles_per_seq, dm), _const_spec(w_out.shape),
                  _const_spec((1, dm)), _const_spec((1, dm))],
        out_specs=[tile, tile, tile, tile],
        out_shape=[jax.ShapeDtypeStruct((tokens, dm), F32), jax.ShapeDtypeStruct((tokens, dm), F32),
                   jax.ShapeDtypeStruct((tokens, dm), BF16), jax.ShapeDtypeStruct((tokens, dm), BF16)],
        args=(x1, attn, conv, gt, w_out, ln_g, ln_b))


def out_bwd(dy, r, mix, gt, w_out, ln_g, seq, name, job=None):
    tokens, dm = r.shape
    half = dm // 2
    tm = min(MIX_TILE, seq)
    tiles_per_seq = seq // tm
    nseq = tokens // seq

    def body(dy_ref, r_ref, mix_ref, gt_ref, w_ref, lg_ref, dres_ref, da_ref, dc_ref, dmix_ref, dln_ref, dgt_ref):
        i = pl.program_id(0)
        dr, dgain, dbias = _ln_bwd(dy_ref[...], r_ref[...], lg_ref[...])

        @pl.when(i == 0)
        def _():
            dln_ref[...] = jnp.zeros_like(dln_ref)

        @pl.when(i % tiles_per_seq == 0)
        def _():
            dgt_ref[...] = jnp.zeros_like(dgt_ref)

        dln_ref[0:1, :] += dgain
        dln_ref[1:2, :] += dbias
        dgt_ref[0] += jnp.sum(dr * mix_ref[...].astype(F32), axis=0, keepdims=True)
        dres_ref[...] = DN_ALPHA * dr
        dmix = ((1.0 + gt_ref[0]) * dr).astype(BF16)
        dmix_ref[...] = dmix
        dmixin = _dot_nt(dmix, w_ref[...])
        da_ref[...] = dmixin[:, :half].astype(BF16)
        dc_ref[...] = dmixin[:, half:].astype(BF16)

    tile = pl.BlockSpec((tm, dm), lambda i: (i, 0))
    htile = pl.BlockSpec((tm, half), lambda i: (i, 0))
    return _call(
        body, job, name=name, grid=(tokens // tm,),
        in_specs=[tile, tile, tile, _mod_spec(tiles_per_seq, dm), _const_spec(w_out.shape), _const_spec((1, dm))],
        out_specs=[tile, htile, htile, tile, pl.BlockSpec((2, dm), lambda i: (0, 0)),
                   pl.BlockSpec((1, 1, dm), lambda i: (i // tiles_per_seq, 0, 0))],
        out_shape=[jax.ShapeDtypeStruct((tokens, dm), F32), jax.ShapeDtypeStruct((tokens, half), BF16),
                   jax.ShapeDtypeStruct((tokens, half), BF16), jax.ShapeDtypeStruct((tokens, dm), BF16),
                   jax.ShapeDtypeStruct((2, dm), F32), jax.ShapeDtypeStruct((nseq, 1, dm), F32)],
        args=(dy, r, mix, gt, w_out, ln_g))


def proj_bwd(parts, dres, x1, sh, sc, w_in, seq, name, job=None):
    tokens, dm = x1.shape
    tm = min(MIX_TILE, seq)
    tiles_per_seq = seq // tm
    nseq = tokens // seq
    widths = [p.shape[1] for p in parts]
    total = sum(widths)

    def body(*refs):
        part_refs = refs[:6]
        dres_ref, x_ref, sh_ref, sc_ref, w_ref, dx_ref, dproj_ref, h_ref, dmod_ref = refs[6:]
        dproj = jnp.concatenate([p[...].astype(BF16) for p in part_refs], axis=1)
        dproj_ref[...] = dproj
        dh = _dot(dproj, w_ref[...])
        xx = x_ref[...]
        one_sc = 1.0 + sc_ref[0]
        h_ref[...] = (xx * one_sc + sh_ref[0]).astype(BF16)
        dx_ref[...] = dres_ref[...] + dh * one_sc

        @pl.when(pl.program_id(0) % tiles_per_seq == 0)
        def _():
            dmod_ref[...] = jnp.zeros_like(dmod_ref)

        dmod_ref[0, 0:1, :] += jnp.sum(dh, axis=0, keepdims=True)
        dmod_ref[0, 1:2, :] += jnp.sum(dh * xx, axis=0, keepdims=True)

    tile = pl.BlockSpec((tm, dm), lambda i: (i, 0))
    mod = _mod_spec(tiles_per_seq, dm)
    return _call(
        body, job, name=name, grid=(tokens // tm,),
        in_specs=[pl.BlockSpec((tm, wdt), lambda i: (i, 0)) for wdt in widths]
        + [tile, tile, mod, mod, _const_spec(w_in.shape)],
        out_specs=[tile, pl.BlockSpec((tm, total), lambda i: (i, 0)), tile,
                   pl.BlockSpec((1, 2, dm), lambda i: (i // tiles_per_seq, 0, 0))],
        out_shape=[jax.ShapeDtypeStruct((tokens, dm), F32), jax.ShapeDtypeStruct((tokens, total), BF16),
                   jax.ShapeDtypeStruct((tokens, dm), BF16), jax.ShapeDtypeStruct((nseq, 2, dm), F32)],
        args=(*parts, dres, x1, sh, sc, w_in))


def _rope_tables(positions):
    half = ROT_DIM // 2
    inv_freq = jnp.power(jnp.float32(ROPE_THETA), -jnp.arange(0, ROT_DIM, 2, dtype=F32) / ROT_DIM)
    lane = jnp.arange(LANES) % HEAD_DIM
    freq = jnp.where(lane < ROT_DIM, inv_freq[lane % half], 0.0)
    sign = jnp.where(lane < half, -1.0, 1.0).astype(F32)
    ang = positions.astype(F32)[:, None] * freq[None, :]
    return jnp.cos(ang), sign[None, :] * jnp.sin(ang)


def kernel(x, c, positions, w_ada, b_ada, ffn1_w_gate_up, ffn1_w_down, ln1_g, ln1_b, w_in, conv_w, attn_sinks, w_out, ln2_g, ln2_b, ffn2_w_gate_up, ffn2_w_down, ln3_g, ln3_b, loss_target, m_w_ada, m_b_ada, m_ffn1_w_gate_up, m_ffn1_w_down, m_ln1_g, m_ln1_b, m_w_in, m_conv_w, m_attn_sinks, m_w_out, m_ln2_g, m_ln2_b, m_ffn2_w_gate_up, m_ffn2_w_down, m_ln3_g, m_ln3_b, v_w_ada, v_b_ada, v_ffn1_w_gate_up, v_ffn1_w_down, v_ln1_g, v_ln1_b, v_w_in, v_conv_w, v_attn_sinks, v_w_out, v_ln2_g, v_ln2_b, v_ffn2_w_gate_up, v_ffn2_w_down, v_ln3_g, v_ln3_b):
    nseq, seq, dm = x.shape
    tokens = nseq * seq
    dev = 4 * lax.axis_index("x") + 2 * lax.axis_index("y") + lax.axis_index("c")
    ada_cols = w_ada.shape[2]
    ff = ffn1_w_down.shape[1] * N_DEV
    fc = ff // 4
    in_cols = w_in.shape[2]
    conv_cols = conv_w.shape[2]

    def t_bf16(w):
        return w[0].T.astype(BF16)

    c_all, convw_all = all_gather([c, conv_w[0]], "gather_cond")
    c_all = c_all.reshape(N_DEV * nseq, dm)
    convw_full = convw_all.transpose(1, 0, 2).reshape(CONV_TAPS, N_DEV * conv_cols)

    b_cols = lax.dynamic_slice(b_ada, (0, dev * ada_cols), (1, ada_cols))
    cond_all, mod_cols = ada_fwd(c_all, w_ada[0], b_cols, "ada_fwd")
    wgu1, mod_all = all_gather([t_bf16(ffn1_w_gate_up), mod_cols], "gather_ffn1")
    wgu1 = wgu1.reshape(2, ff, dm)
    mod = lax.dynamic_slice(mod_all, (0, dev * nseq, 0), (N_DEV, nseq, ada_cols))
    mod = mod.transpose(1, 0, 2).reshape(nseq, 9, 1, dm)
    sh1, sc1, g1, sh2, sc2, g2, sh3, sc3, g3 = [mod[:, i] for i in range(9)]

    x0 = x.reshape(tokens, dm)
    jobs = _Jobs([_GatherJob([ffn1_w_down[0].astype(BF16)]), gather_spread_job([ffn2_w_down[0].astype(BF16)])])
    (gu1, a1_fwd), res = ffn_up(x0, sh1, sc1, wgu1, seq, "ffn1_up", job=jobs)
    (wd1,), wd2_spread = jobs.split(res)
    wd1 = wd1.reshape(ff, dm)
    (x1, r1, f1), (win,) = ffn_down(x0, a1_fwd, g1, wd1, ln1_g, ln1_b, seq, "ffn1_down", job=_GatherJob([t_bf16(w_in)]))
    win = win.reshape(N_DEV * in_cols, dm)
    jobs = _Jobs([_GatherJob([w_out[0].astype(BF16)]), gather_forward_job(wd2_spread)])
    (q, k, v, u, bg, cg), res = proj_fwd(x1, sh2, sc2, win, seq, "proj_fwd", job=jobs)
    (wout,), (wd2,) = jobs.split(res)
    wout, wd2 = wout.reshape(dm, dm), wd2.reshape(ff, dm)
    cos_t, sin_t = _rope_tables(positions.reshape(tokens))
    sinks = attn_sinks[0]
    (attn, q_rot, probs, sink_probs), (wgu2,) = attn_fwd(q, k, v, cos_t, sin_t, sinks, seq, "attn_fwd",
                                                         job=_GatherJob([t_bf16(ffn2_w_gate_up)]))
    conv = conv_fwd(u, bg, cg, convw_full, seq, "conv_fwd")
    (x2, r2, mixin, mix), _ = out_fwd(x1, attn, conv, g2, wout, ln2_g, ln2_b, seq, "out_fwd")
    wgu2 = wgu2.reshape(2, ff, dm)
    target = loss_target.reshape(tokens, dm)
    (dy3, loss_part, r3, gu3, f3), _ = ffn_fwd(x2, sh3, sc3, g3, wgu2, wd2, ln3_g, ln3_b, seq, "ffn2_fwd", target=target)

    (dx2, dgu3, df3, a3, h3, dln3, dmod3), _ = ffn_bwd(dy3, r3, x2, f3, gu3, sh3, sc3, g3, wgu2, wd2, ln3_g, seq, "ffn2_bwd")
    pair = 2 * fc
    g_wd2 = tn_matmul(a3[None], df3[None], "ffn2_dwd", a_width=pair)[0][0].reshape(N_DEV, ff // N_DEV, dm)
    g_wgu2 = tn_matmul(dgu3, h3[None], "ffn2_dwgu", a_width=pair)[0][0].reshape(N_DEV, fc, dm)
    (dres2, dattn, dconv, dmix, dln2, dg2), swapped = out_bwd(dx2, r2, mix, g2, wout, ln2_g, seq, "out_bwd",
                                                              job=swap_job([g_wgu2, g_wd2]))
    p_wgu2, own_wgu2 = pair_sum(g_wgu2, swapped[0], "pair_wgu2")
    p_wd2, own_wd2 = pair_sum(g_wd2, swapped[1], "pair_wd2")
    du, dbg, dcg, dconvw = conv_bwd(dconv, u, bg, cg, convw_full, seq, "conv_bwd")
    (dq, dk, dv, dsink_rows), (far_wd2,) = attn_bwd(
        q_rot, k, v, dattn, probs, sink_probs, cos_t, sin_t, seq, "attn_bwd", job=chip_exchange_job([p_wd2]))
    parts = [dq, dk, dv, du, dbg, dcg]
    (dx1, dproj, h2, dmod2), far_top = proj_bwd(parts, dres2, x1, sh2, sc2, win, seq, "proj_bwd",
                                                job=chip_exchange_job([p_wgu2], rows=(0, fc // 2)))
    (dx0, dgu1, df1, a1, h1, dln1, dmod1), _ = ffn_bwd(
        dx1, r1, x0, f1, gu1, sh1, sc1, g1, wgu1, wd1, ln1_g, seq, "ffn1_bwd")

    dmod = jnp.concatenate([dmod1, dmod2, dg2, dmod3], axis=1).reshape(nseq, 9 * dm)
    half = dm // 2
    jobs = _Jobs([gather_spread_job([dmod]),
                  chip_exchange_job([p_wgu2], rows=(fc // 2, fc // 2), into=far_top)])
    (g_wd1,), res = tn_matmul(a1[None], df1[None], "ffn1_dwd", job=jobs, a_width=pair)
    dmod_spread, (far_wgu2,) = jobs.split(res)
    g_wd1 = g_wd1.reshape(N_DEV, ff // N_DEV, dm)
    jobs = _Jobs([swap_job([g_wd1]), gather_forward_job(dmod_spread)])
    (g_l,), res = tn_matmul(dgu1, h1[None], "ffn1_dwgu_l", job=jobs, b_cols=(0, half), a_width=pair)
    (sw_wd1,), (dmod_all,) = jobs.split(res)
    g_l = g_l.reshape(N_DEV, fc, half)
    p_wd1, own_wd1 = pair_sum(g_wd1, sw_wd1, "pair_wd1")
    jobs = _Jobs([chip_exchange_job([p_wd1]), swap_job([g_l])])
    (g_r,), res = tn_matmul(dgu1, h1[None], "ffn1_dwgu_r", job=jobs, b_cols=(1, half), a_width=pair)
    (far_wd1,), (sw_l,) = jobs.split(res)
    g_r = g_r.reshape(N_DEV, fc, half)
    p_l, own_l = pair_sum(g_l, sw_l, "pair_wgu1_l")

    dmod_cols = lax.dynamic_slice(dmod_all.reshape(N_DEV * nseq, 9 * dm), (0, dev * ada_cols), (N_DEV * nseq, ada_cols))
    grad_w_ada, gb_cols = ada_bwd(cond_all, dmod_cols, "ada_bwd")
    dsinks = jnp.sum(dsink_rows.reshape(nseq, N_Q_HEADS, ATTN_BLOCK), axis=(0, 2))
    small = jnp.zeros((8, dm), F32)
    small = small.at[0:2].set(dln1).at[2:4].set(dln2).at[4:6].set(dln3)
    small = small.at[6, 0:N_Q_HEADS].set(dsinks).at[7, 0].set(loss_part[0, 0])

    jobs = _Jobs([chip_exchange_job([p_l]), swap_job([g_r]), gather_spread_job([small, dconvw, gb_cols])])
    (g_win,), res = tn_matmul(dproj[None], h2[None], "dwin", job=jobs)
    (far_l,), (sw_r,), small_spread = jobs.split(res)
    g_win = g_win.reshape(N_DEV, in_cols, dm)
    p_r, own_r = pair_sum(g_r, sw_r, "pair_wgu1_r")
    jobs = _Jobs([chip_exchange_job([p_r]), swap_job([g_win]), gather_forward_job(small_spread)])
    (g_wout,), res = tn_matmul(mixin[None], dmix[None], "dwout", job=jobs)
    (far_r,), (sw_win,), (small_all, dconvw_all, gb_all) = jobs.split(res)
    g_wout = g_wout.reshape(N_DEV, dm // N_DEV, dm)
    p_win, own_win = pair_sum(g_win, sw_win, "pair_win")
    jobs = _Jobs([chip_exchange_job([p_win]), swap_job([g_wout])])
    (far_win,), (sw_wout,) = jobs.split(run_job(jobs, "rs_tail_win"))
    p_wout, own_wout = pair_sum(g_wout, sw_wout, "pair_wout")
    (far_wout,) = run_job(chip_exchange_job([p_wout]), "rs_tail_wout")

    grads = {
        "ffn1_w_gate_up": jnp.concatenate([own_l, own_r], axis=1), "ffn1_w_down": own_wd1,
        "w_in": own_win, "w_out": own_wout, "ffn2_w_gate_up": own_wgu2, "ffn2_w_down": own_wd2,
    }
    others = {"ffn1_w_gate_up": jnp.concatenate([far_l, far_r], axis=2), "ffn1_w_down": far_wd1,
              "w_in": far_win, "w_out": far_wout, "ffn2_w_gate_up": far_wgu2, "ffn2_w_down": far_wd2}

    grads["w_ada"] = grad_w_ada
    small_sum = sum_devices(small_all, "sum_small")
    dconvw_sum = sum_devices(dconvw_all, "sum_convw")
    loss = small_sum[7, 0]
    grads["b_ada"] = gb_all.reshape(1, N_DEV * ada_cols)
    grads["conv_w"] = lax.dynamic_slice(dconvw_sum, (0, dev * conv_cols), (CONV_TAPS, conv_cols))
    grads["attn_sinks"] = small_sum[6:7, 0:N_Q_HEADS]
    for i, nm in enumerate(["ln1_g", "ln1_b", "ln2_g", "ln2_b", "ln3_g", "ln3_b"]):
        grads[nm] = small_sum[i:i + 1]

    given = dict(w_ada=(w_ada, m_w_ada, v_w_ada), b_ada=(b_ada, m_b_ada, v_b_ada),
                 ffn1_w_gate_up=(ffn1_w_gate_up, m_ffn1_w_gate_up, v_ffn1_w_gate_up),
                 ffn1_w_down=(ffn1_w_down, m_ffn1_w_down, v_ffn1_w_down),
                 ln1_g=(ln1_g, m_ln1_g, v_ln1_g), ln1_b=(ln1_b, m_ln1_b, v_ln1_b),
                 w_in=(w_in, m_w_in, v_w_in), conv_w=(conv_w, m_conv_w, v_conv_w),
                 attn_sinks=(attn_sinks, m_attn_sinks, v_attn_sinks), w_out=(w_out, m_w_out, v_w_out),
                 ln2_g=(ln2_g, m_ln2_g, v_ln2_g), ln2_b=(ln2_b, m_ln2_b, v_ln2_b),
                 ffn2_w_gate_up=(ffn2_w_gate_up, m_ffn2_w_gate_up, v_ffn2_w_gate_up),
                 ffn2_w_down=(ffn2_w_down, m_ffn2_w_down, v_ffn2_w_down),
                 ln3_g=(ln3_g, m_ln3_g, v_ln3_g), ln3_b=(ln3_b, m_ln3_b, v_ln3_b))
    order = ["w_ada", "b_ada", "ffn1_w_gate_up", "ffn1_w_down", "ln1_g", "ln1_b", "w_in", "conv_w", "attn_sinks",
             "w_out", "ln2_g", "ln2_b", "ffn2_w_gate_up", "ffn2_w_down", "ln3_g", "ln3_b"]
    transposed = ("ffn1_w_gate_up", "ffn2_w_gate_up", "w_in")
    big = ("w_ada", "ffn1_w_gate_up", "ffn1_w_down", "w_in", "w_out", "ffn2_w_gate_up", "ffn2_w_down")
    results = {}
    for nm in big:
        if nm in transposed:
            w2, m2, v2 = [t[0].T for t in given[nm]]
            results[nm] = [t.T[None] for t in adamw(w2, grads[nm], m2, v2, "adamw_" + nm, others=others.get(nm))]
        else:
            w2, m2, v2 = [t[0] for t in given[nm]]
            results[nm] = [t[None] for t in adamw(w2, grads[nm], m2, v2, "adamw_" + nm, others=others.get(nm))]
    small_names = [nm for nm in order if nm not in big]
    items = []
    for nm in small_names:
        shape = given[nm][0].shape
        two_d = (shape[-2], shape[-1])
        items.append((given[nm][0].reshape(two_d), grads[nm].reshape(two_d), *[t.reshape(two_d) for t in given[nm][1:]]))
    for nm, res in zip(small_names, adamw_small(items, "adamw_small")):
        shape = given[nm][0].shape
        results[nm] = [grads[nm].reshape(shape)] + [t.reshape(shape) for t in res]
    grad_x = dx0.reshape(nseq, seq, dm)
    return (loss, grad_x, *[results[nm][i] for i in range(4) for nm in order])
```

```python
import functools

import jax
import jax.numpy as jnp
from jax import lax
from jax.experimental import pallas as pl
from jax.experimental.pallas import tpu as pltpu

F32 = jnp.float32
BF16 = jnp.bfloat16
MESH = pl.DeviceIdType.MESH

N_DEV = 8
N_CHIP = 4
HEAD_DIM = 64
N_Q_HEADS = 8
N_KV_HEADS = 2
GQA_GROUP = N_Q_HEADS // N_KV_HEADS
ATTN_BLOCK = 128
ROT_DIM = 16
ROPE_THETA = 500000.0
CONV_TAPS = 3
LN_EPS = 1e-5
DN_ALPHA = 2.0 ** 0.25
ADAM_LR = 0.001
ADAM_B1 = 0.9
ADAM_B2 = 0.999
ADAM_EPS = 1e-08
ADAM_WD = 0.01
ADAM_STEP = 10
NEG_BIG = -1e30

VMEM_LIMIT = 56 * 1024 * 1024
TOKEN_TILE = 256
FFN_FWD_TILE = 512
MIX_TILE = 512
TN_VMEM_BUDGET = 36 * 1024 * 1024


def _params(semantics=None, vmem=VMEM_LIMIT):
    return pltpu.CompilerParams(dimension_semantics=semantics, vmem_limit_bytes=vmem)


def _dot(a, b):
    return jnp.dot(a, b, preferred_element_type=F32)


def _dot_nt(a, b):
    return lax.dot_general(a, b, (((1,), (1,)), ((), ())), preferred_element_type=F32)


def _dot_tn(a, b):
    return lax.dot_general(a, b, (((0,), (0,)), ((), ())), preferred_element_type=F32)


def _sigmoid(x):
    return pl.reciprocal(1.0 + jnp.exp(-x), approx=True)


def _ln_stats(r):
    mu = jnp.mean(r, axis=-1, keepdims=True)
    d = r - mu
    var = jnp.mean(d * d, axis=-1, keepdims=True)
    rstd = lax.rsqrt(var + LN_EPS)
    return d * rstd, rstd


def _ln_bwd(dy, r, g):
    xhat, rstd = _ln_stats(r)
    dxhat = dy * g
    c1 = jnp.mean(dxhat, axis=-1, keepdims=True)
    c2 = jnp.mean(dxhat * xhat, axis=-1, keepdims=True)
    dr = rstd * (dxhat - c1 - xhat * c2)
    return dr, jnp.sum(dy * xhat, axis=0, keepdims=True), jnp.sum(dy, axis=0, keepdims=True)


def _const_spec(shape):
    nd = len(shape)
    return pl.BlockSpec(shape, lambda *_: (0,) * nd, pipeline_mode=pl.Buffered(1))


def all_gather(arrs, name):
    n = len(arrs)

    def body(*refs):
        ins, outs = refs[:n], refs[n:2 * n]
        send_sems, recv_sems, local_sems = refs[2 * n:]
        x, y, c = lax.axis_index("x"), lax.axis_index("y"), lax.axis_index("c")
        me, sibling = (x, y, c), (x, y, 1 - c)
        chips = [(1 - x, y), (x, 1 - y), (1 - x, 1 - y)]

        def slot(i, p):
            return outs[i].at[4 * p[0] + 2 * p[1] + p[2]]

        def copy(i, k, block, to, src=None):
            return pltpu.make_async_remote_copy(
                src_ref=slot(i, block) if src is None else src, dst_ref=slot(i, block),
                send_sem=send_sems.at[i, k], recv_sem=recv_sems.at[i, k],
                device_id=to, device_id_type=MESH)

        mine = [pltpu.make_async_copy(ins[i], slot(i, me), local_sems.at[i]) for i in range(n)]
        for cp in mine:
            cp.start()
        first = []
        for i in range(n):
            first.append(copy(i, 0, me, sibling, src=ins[i]))
            first += [copy(i, 1 + j, me, (*chip, c), src=ins[i]) for j, chip in enumerate(chips)]
        for cp in first:
            cp.start()
        passed = []
        for i in range(n):
            for j, chip in enumerate(chips):
                copy(i, 1 + j, (*chip, c), me).wait_recv()
                cp = copy(i, 4 + j, (*chip, c), sibling)
                cp.start()
                passed.append(cp)
        for i in range(n):
            copy(i, 0, sibling, me).wait_recv()
            for j, chip in enumerate(chips):
                copy(i, 4 + j, (*chip, 1 - c), me).wait_recv()
        for cp in first + passed:
            cp.wait_send()
        for cp in mine:
            cp.wait()

    any_spec = pl.BlockSpec(memory_space=pl.ANY)
    return pl.pallas_call(
        body, name=name,
        out_shape=[jax.ShapeDtypeStruct((N_DEV, *a.shape), a.dtype) for a in arrs],
        in_specs=[any_spec] * n, out_specs=[any_spec] * n,
        scratch_shapes=[pltpu.SemaphoreType.DMA((n, 7)), pltpu.SemaphoreType.DMA((n, 7)),
                        pltpu.SemaphoreType.DMA((n,))],
    )(*arrs)


def _place():
    x, y, c = lax.axis_index("x"), lax.axis_index("y"), lax.axis_index("c")
    return x, y, c, [(1 - x, y), (x, 1 - y), (1 - x, 1 - y)]


def _slot(p):
    return 4 * p[0] + 2 * p[1] + p[2]


class _Job:
    def __init__(self, ins, outs, nsem, copies, aliases=None, local=None):
        self.ins, self.outs, self.nsem, self.copies = list(ins), list(outs), nsem, copies
        self.aliases = aliases or {}
        self.local = local

    def scratch(self):
        s = [pltpu.SemaphoreType.DMA(self.nsem), pltpu.SemaphoreType.DMA(self.nsem)]
        if self.local is not None:
            s.append(pltpu.SemaphoreType.DMA((len(self.ins),)))
        return s

    def start(self, ins, outs, sems):
        if self.local is not None:
            for cp in self.local(ins, outs, sems[2]):
                cp.start()
        for cp in self.copies(ins, outs, sems[0], sems[1])[0]:
            cp.start()

    def finish(self, ins, outs, sems):
        started, awaited = self.copies(ins, outs, sems[0], sems[1])
        for cp in awaited:
            cp.wait_recv()
        for cp in started:
            cp.wait_send()
        if self.local is not None:
            for cp in self.local(ins, outs, sems[2]):
                cp.wait()


class _Jobs:
    def __init__(self, jobs):
        self.jobs = jobs
        self.ins = [a for j in jobs for a in j.ins]
        self.outs = [o for j in jobs for o in j.outs]
        self.two_phase = any(getattr(j, "two_phase", False) for j in jobs)
        self.aliases = {}
        at_in = at_out = 0
        for j in jobs:
            self.aliases.update({at_in + i: at_out + o for i, o in j.aliases.items()})
            at_in, at_out = at_in + len(j.ins), at_out + len(j.outs)

    def scratch(self):
        return [s for j in self.jobs for s in j.scratch()]

    def _each(self, ins, outs, sems):
        at_in = at_out = at_sem = 0
        for j in self.jobs:
            n_in, n_out, n_sem = len(j.ins), len(j.outs), len(j.scratch())
            yield j, ins[at_in:at_in + n_in], outs[at_out:at_out + n_out], sems[at_sem:at_sem + n_sem]
            at_in, at_out, at_sem = at_in + n_in, at_out + n_out, at_sem + n_sem

    def start(self, ins, outs, sems):
        for j, i, o, s in self._each(ins, outs, sems):
            j.start(i, o, s)

    def turn(self, ins, outs, sems):
        for j, i, o, s in self._each(ins, outs, sems):
            if getattr(j, "two_phase", False):
                j.turn(i, o, s)

    def finish(self, ins, outs, sems):
        for j, i, o, s in self._each(ins, outs, sems):
            j.finish(i, o, s)

    def split(self, results):
        at, parts = 0, []
        for j in self.jobs:
            parts.append(results[at:at + len(j.outs)])
            at += len(j.outs)
        return parts


def _remote(src, dst, send, recv, idx, to):
    return pltpu.make_async_remote_copy(src_ref=src, dst_ref=dst, send_sem=send.at[idx], recv_sem=recv.at[idx],
                                        device_id=to, device_id_type=MESH)


def _spread_copies(ins, outs, send, recv, base=0):
    x, y, c, chips = _place()
    me = (x, y, c)
    peers = [(x, y, 1 - c)] + [(*chip, c) for chip in chips]
    started, awaited = [], []
    for i, (src, dst) in enumerate(zip(ins, outs)):
        for k, peer in enumerate(peers):
            started.append(_remote(src, dst.at[_slot(me)], send, recv, (base + i, k), peer))
            awaited.append(_remote(src, dst.at[_slot(peer)], send, recv, (base + i, k), peer))
    return started, awaited


def _forward_copies(ins, outs, send, recv, base=0):
    x, y, c, chips = _place()
    started, awaited = [], []
    for i, buf in enumerate(outs):
        for j, chip in enumerate(chips):
            mine, theirs = buf.at[_slot((*chip, c))], buf.at[_slot((*chip, 1 - c))]
            started.append(_remote(mine, mine, send, recv, (base + i, j), (x, y, 1 - c)))
            awaited.append(_remote(theirs, theirs, send, recv, (base + i, j), (x, y, 1 - c)))
    return started, awaited


def _own_block_copies(ins, outs, sems):
    x, y, c, _ = _place()
    return [pltpu.make_async_copy(src, dst.at[_slot((x, y, c))], sems.at[i])
            for i, (src, dst) in enumerate(zip(ins, outs))]


def gather_spread_job(shards):
    outs = [jax.ShapeDtypeStruct((N_DEV, *a.shape), a.dtype) for a in shards]
    return _Job(shards, outs, (len(shards), 4), _spread_copies, local=_own_block_copies)


def gather_forward_job(fulls):
    outs = [jax.ShapeDtypeStruct(a.shape, a.dtype) for a in fulls]
    return _Job(fulls, outs, (len(fulls), 3), _forward_copies, aliases={i: i for i in range(len(fulls))})


TURN_EIGHTHS = 6


class _GatherJob:
    two_phase = True

    def __init__(self, shards):
        self.ins = list(shards)
        self.outs = [jax.ShapeDtypeStruct((N_DEV, *a.shape), a.dtype) for a in shards]
        self.aliases = {}

    def scratch(self):
        n = len(self.ins)
        return [pltpu.SemaphoreType.DMA((n, 4)), pltpu.SemaphoreType.DMA((n, 4)),
                pltpu.SemaphoreType.DMA((n, 3)), pltpu.SemaphoreType.DMA((n, 3)), pltpu.SemaphoreType.DMA((n,))]

    def start(self, ins, outs, sems):
        for cp in _own_block_copies(ins, outs, sems[4]) + _spread_copies(ins, outs, sems[0], sems[1])[0]:
            cp.start()

    def turn(self, ins, outs, sems):
        for cp in _spread_copies(ins, outs, sems[0], sems[1])[1]:
            cp.wait_recv()
        for cp in _forward_copies(outs, outs, sems[2], sems[3])[0]:
            cp.start()

    def finish(self, ins, outs, sems):
        handed_on, arriving = _forward_copies(outs, outs, sems[2], sems[3])
        for cp in arriving:
            cp.wait_recv()
        for cp in _spread_copies(ins, outs, sems[0], sems[1])[0] + handed_on:
            cp.wait_send()
        for cp in _own_block_copies(ins, outs, sems[4]):
            cp.wait()


def swap_job(gs):
    def copies(ins, outs, send, recv):
        x, y, c, _ = _place()
        started, awaited = [], []
        for i, (g, r1) in enumerate(zip(ins, outs)):
            for q in range(N_CHIP):
                started.append(_remote(g.at[2 * q + (1 - c)], r1.at[q], send, recv, (i, q), (x, y, 1 - c)))
                awaited.append(_remote(g.at[2 * q + c], r1.at[q], send, recv, (i, q), (x, y, 1 - c)))
        return started, awaited

    outs = [jax.ShapeDtypeStruct((N_CHIP, *g.shape[1:]), g.dtype) for g in gs]
    return _Job(gs, outs, (len(gs), N_CHIP), copies)


def chip_exchange_job(ps, rows=None, into=None):
    n = len(ps)

    def copies(ins, outs, send, recv):
        x, y, c, chips = _place()
        started, awaited = [], []
        for i, (p, r2) in enumerate(zip(ins[:n], outs)):
            for k, chip in enumerate(chips):
                src, mine, dst = p.at[2 * chip[0] + chip[1]], p.at[2 * x + y], r2.at[k]
                if rows is not None:
                    src, mine, dst = (t.at[pl.ds(rows[0], rows[1])] for t in (src, mine, dst))
                started.append(_remote(src, dst, send, recv, (i, k), (*chip, c)))
                awaited.append(_remote(mine, dst, send, recv, (i, k), (*chip, c)))
        return started, awaited

    outs = [jax.ShapeDtypeStruct((3, *p.shape[1:]), p.dtype) for p in ps]
    if into is None:
        return _Job(ps, outs, (n, 3), copies)
    return _Job(list(ps) + list(into), outs, (n, 3), copies, aliases={n + i: i for i in range(n)})


def _call(body, job, *, name, grid, in_specs, out_specs, out_shape, args, scratch_shapes=(), vmem=VMEM_LIMIT):
    if job is None:
        res = pl.pallas_call(
            body, name=name, grid=grid, in_specs=in_specs, out_specs=out_specs, out_shape=out_shape,
            scratch_shapes=list(scratch_shapes), compiler_params=_params(("arbitrary",) * len(grid), vmem),
        )(*args)
        return res, []
    n_in, n_out, n_scr = len(in_specs), len(out_specs), len(scratch_shapes)
    j_in, j_out = len(job.ins), len(job.outs)

    def with_copies(*refs):
        at = 0
        ins = refs[at:at + n_in]; at += n_in
        jins = refs[at:at + j_in]; at += j_in
        outs = refs[at:at + n_out]; at += n_out
        jouts = refs[at:at + j_out]; at += j_out
        scr = refs[at:at + n_scr]; at += n_scr
        sems = refs[at:]
        ids = [pl.program_id(d) for d in range(len(grid))]
        first = functools.reduce(jnp.logical_and, [i == 0 for i in ids])
        last = functools.reduce(jnp.logical_and, [i == n - 1 for i, n in zip(ids, grid)])

        @pl.when(first)
        def _():
            job.start(jins, jouts, sems)

        if getattr(job, "two_phase", False):
            steps, at = 1, 0
            for i, n in zip(ids, grid):
                steps, at = steps * n, at * n + i

            @pl.when(at == (TURN_EIGHTHS * steps) // 8)
            def _():
                job.turn(jins, jouts, sems)

        body(*ins, *outs, *scr)

        @pl.when(last)
        def _():
            job.finish(jins, jouts, sems)

    any_spec = pl.BlockSpec(memory_space=pl.ANY)
    res = pl.pallas_call(
        with_copies, name=name, grid=grid,
        in_specs=list(in_specs) + [any_spec] * j_in, out_specs=list(out_specs) + [any_spec] * j_out,
        out_shape=list(out_shape) + list(job.outs),
        input_output_aliases={n_in + i: n_out + o for i, o in job.aliases.items()},
        scratch_shapes=list(scratch_shapes) + job.scratch(),
        compiler_params=_params(("arbitrary",) * len(grid), vmem),
    )(*args, *job.ins)
    return res[:n_out], res[n_out:]


def run_job(job, name):
    def body(*refs):
        j_in, j_out = len(job.ins), len(job.outs)
        ins, outs, sems = refs[:j_in], refs[j_in:j_in + j_out], refs[j_in + j_out:]
        job.start(ins, outs, sems)
        job.finish(ins, outs, sems)

    any_spec = pl.BlockSpec(memory_space=pl.ANY)
    return pl.pallas_call(
        body, name=name, in_specs=[any_spec] * len(job.ins), out_specs=[any_spec] * len(job.outs),
        out_shape=list(job.outs), input_output_aliases=dict(job.aliases), scratch_shapes=job.scratch(),
    )(*job.ins)


def pair_sum(g, r1, name):
    _, rows, cols = g.shape
    rb = next(cand for cand in range(min(rows, 512), 0, -16) if rows % cand == 0)

    def body(g_ref, r1_ref, p_ref, own_ref):
        x, y, c, _ = _place()
        s = g_ref[c].astype(F32) + r1_ref[0].astype(F32)
        p_ref[0] = s.astype(BF16)

        @pl.when(pl.program_id(1) == 2 * x + y)
        def _():
            own_ref[...] = s

    return pl.pallas_call(
        body, name=name, grid=(rows // rb, N_CHIP),
        in_specs=[pl.BlockSpec((2, rb, cols), lambda i, q: (q, i, 0)), pl.BlockSpec((1, rb, cols), lambda i, q: (q, i, 0))],
        out_specs=[pl.BlockSpec((1, rb, cols), lambda i, q: (q, i, 0)), pl.BlockSpec((rb, cols), lambda i, q: (i, 0))],
        out_shape=[jax.ShapeDtypeStruct((N_CHIP, rows, cols), BF16), jax.ShapeDtypeStruct((rows, cols), F32)],
        compiler_params=_params(("arbitrary", "arbitrary")),
    )(g, r1)


def sum_devices(a, name):
    def body(a_ref, o_ref):
        acc = a_ref[0]
        for d in range(1, N_DEV):
            acc = acc + a_ref[d]
        o_ref[...] = acc

    return pl.pallas_call(body, name=name, out_shape=jax.ShapeDtypeStruct(a.shape[1:], F32))(a)


def _adam_update(w, g, m, v):
    nm = ADAM_B1 * m + (1.0 - ADAM_B1) * g
    nv = ADAM_B2 * v + (1.0 - ADAM_B2) * (g * g)
    m_hat = nm / (1.0 - ADAM_B1 ** ADAM_STEP)
    v_hat = nv / (1.0 - ADAM_B2 ** ADAM_STEP)
    return -ADAM_LR * (m_hat / (jnp.sqrt(v_hat) + ADAM_EPS) + ADAM_WD * w), nm, nv


def adamw(w, g, m, v, name, others=None):
    rows, cols = w.shape
    rb = rows
    for cand in range(min(rows, 512), 7, -8):
        if rows % cand == 0 and cand % 8 == 0:
            rb = cand
            break

    def body(*refs):
        if others is None:
            w_ref, g_ref, m_ref, v_ref, d_ref, nm_ref, nv_ref = refs
            gg = g_ref[...]
        else:
            w_ref, g_ref, m_ref, v_ref, r2_ref, go_ref, d_ref, nm_ref, nv_ref = refs
            gg = g_ref[...]
            for k in range(3):
                gg = gg + r2_ref[k].astype(F32)
            go_ref[...] = gg
        d_ref[...], nm_ref[...], nv_ref[...] = _adam_update(w_ref[...], gg, m_ref[...], v_ref[...])

    spec = pl.BlockSpec((rb, cols), lambda i: (i, 0))
    out = jax.ShapeDtypeStruct((rows, cols), F32)
    in_specs, args = [spec] * 4, [w, g, m, v]
    if others is not None:
        in_specs.append(pl.BlockSpec((3, rb, cols), lambda i: (0, i, 0)))
        args.append(others)
    n_out = 3 if others is None else 4
    res = pl.pallas_call(
        body, name=name, grid=(rows // rb,), in_specs=in_specs, out_specs=[spec] * n_out,
        out_shape=[out] * n_out, compiler_params=_params(("parallel",)),
    )(*args)
    return (g, *res) if others is None else tuple(res)


def adamw_small(items, name):
    n = len(items)

    def body(*refs):
        ins, outs = refs[:4 * n], refs[4 * n:]
        for i in range(n):
            w_ref, g_ref, m_ref, v_ref = ins[4 * i:4 * i + 4]
            d_ref, nm_ref, nv_ref = outs[3 * i:3 * i + 3]
            d_ref[...], nm_ref[...], nv_ref[...] = _adam_update(w_ref[...], g_ref[...], m_ref[...], v_ref[...])

    res = pl.pallas_call(
        body, name=name,
        out_shape=[jax.ShapeDtypeStruct(w.shape, F32) for w, _, _, _ in items for _ in range(3)],
    )(*[t for item in items for t in item])
    return [tuple(res[3 * i:3 * i + 3]) for i in range(n)]


def ada_fwd(c_all, w_cols, b_cols, name):
    def body(c_ref, w_ref, b_ref, cond_ref, mod_ref):
        cc = c_ref[...]
        cond = (cc * _sigmoid(cc)).astype(BF16)
        cond_ref[...] = cond
        mod_ref[...] = _dot(cond, w_ref[...].astype(BF16)) + b_ref[...]

    n, cols = c_all.shape[0], w_cols.shape[1]
    return pl.pallas_call(
        body, name=name,
        out_shape=[jax.ShapeDtypeStruct(c_all.shape, BF16), jax.ShapeDtypeStruct((n, cols), F32)],
        compiler_params=_params(),
    )(c_all, w_cols, b_cols)


def ada_bwd(cond_all, dmod_cols, name):
    def body(c_ref, d_ref, gw_ref, gb_ref):
        d = d_ref[...]
        gw_ref[...] = _dot_tn(c_ref[...], d.astype(BF16))
        gb_ref[...] = jnp.sum(d, axis=0, keepdims=True)

    dm, cols = cond_all.shape[1], dmod_cols.shape[1]
    return pl.pallas_call(
        body, name=name,
        out_shape=[jax.ShapeDtypeStruct((dm, cols), F32), jax.ShapeDtypeStruct((1, cols), F32)],
        compiler_params=_params(),
    )(cond_all, dmod_cols)


MXU_COLS = 256
FFN_CHUNK = 4 * MXU_COLS


def _hidden_chunks(ff):
    assert ff % MXU_COLS == 0
    return [(at, min(FFN_CHUNK, ff - at)) for at in range(0, ff, FFN_CHUNK)]


def _mod_spec(tiles_per_seq, dm):
    return pl.BlockSpec((1, 1, dm), lambda i: (i // tiles_per_seq, 0, 0))


def ffn_fwd(x, sh, sc, gt, wgu, wd, ln_g, ln_b, seq, name, target=None, job=None):
    tokens, dm = x.shape
    ff = wgu.shape[1]
    chunks = _hidden_chunks(ff)
    tm = min(FFN_FWD_TILE, seq)
    tiles_per_seq = seq // tm
    with_loss = target is not None

    def body(*refs):
        if with_loss:
            (x_ref, sh_ref, sc_ref, gt_ref, wgu_ref, wd_ref, lg_ref, lb_ref, t_ref,
             xo_ref, loss_ref, r_ref, gu_ref, f_ref) = refs
        else:
            (x_ref, sh_ref, sc_ref, gt_ref, wgu_ref, wd_ref, lg_ref, lb_ref,
             xo_ref, r_ref, gu_ref, f_ref) = refs
        xx = x_ref[...]
        h = (xx * (1.0 + sc_ref[0]) + sh_ref[0]).astype(BF16)
        acc = jnp.zeros((tm, dm), F32)
        for at, wdt in chunks:
            gk = _dot_nt(h, wgu_ref[0, at:at + wdt, :])
            uk = _dot_nt(h, wgu_ref[1, at:at + wdt, :])
            gu_ref[0, :, at:at + wdt] = gk.astype(BF16)
            gu_ref[1, :, at:at + wdt] = uk.astype(BF16)
            a = (gk * _sigmoid(gk) * uk).astype(BF16)
            acc = acc + _dot(a, wd_ref[at:at + wdt, :])
        f_ref[...] = acc.astype(BF16)
        r = DN_ALPHA * xx + (0.5 * (1.0 + gt_ref[0])) * acc
        r_ref[...] = r
        xhat, _ = _ln_stats(r)
        yy = xhat * lg_ref[...] + lb_ref[...]
        if with_loss:
            err = yy - t_ref[...]
            xo_ref[...] = err * (1.0 / dm)

            @pl.when(pl.program_id(0) == 0)
            def _():
                loss_ref[...] = jnp.zeros_like(loss_ref)

            loss_ref[...] += jnp.full((1, 128), (0.5 / dm) * jnp.sum(err * err), F32)
        else:
            xo_ref[...] = yy

    tile = pl.BlockSpec((tm, dm), lambda i: (i, 0))
    mod = _mod_spec(tiles_per_seq, dm)
    in_specs = [tile, mod, mod, mod, _const_spec(wgu.shape), _const_spec(wd.shape),
                _const_spec((1, dm)), _const_spec((1, dm))]
    args = [x, sh, sc, gt, wgu, wd, ln_g, ln_b]
    out_specs = [tile]
    out_shape = [jax.ShapeDtypeStruct((tokens, dm), F32)]
    if with_loss:
        in_specs.append(tile)
        args.append(target)
        out_specs.append(pl.BlockSpec((1, 128), lambda i: (0, 0)))
        out_shape.append(jax.ShapeDtypeStruct((1, 128), F32))
    out_specs += [tile, pl.BlockSpec((2, tm, ff), lambda i: (0, i, 0)), tile]
    out_shape += [jax.ShapeDtypeStruct((tokens, dm), F32), jax.ShapeDtypeStruct((2, tokens, ff), BF16),
                  jax.ShapeDtypeStruct((tokens, dm), BF16)]
    return _call(body, job, name=name, grid=(tokens // tm,), in_specs=in_specs, out_specs=out_specs,
                 out_shape=out_shape, args=args)


def ffn_up(x, sh, sc, wgu, seq, name, job=None):
    tokens, dm = x.shape
    ff = wgu.shape[1]
    chunks = _hidden_chunks(ff)
    tm = min(FFN_FWD_TILE, seq)

    def body(x_ref, sh_ref, sc_ref, wgu_ref, gu_ref, a_ref):
        h = (x_ref[...] * (1.0 + sc_ref[0]) + sh_ref[0]).astype(BF16)
        for at, wdt in chunks:
            gk = _dot_nt(h, wgu_ref[0, at:at + wdt, :])
            uk = _dot_nt(h, wgu_ref[1, at:at + wdt, :])
            gu_ref[0, :, at:at + wdt] = gk.astype(BF16)
            gu_ref[1, :, at:at + wdt] = uk.astype(BF16)
            a_ref[:, at:at + wdt] = (gk * _sigmoid(gk) * uk).astype(BF16)

    tile = pl.BlockSpec((tm, dm), lambda i: (i, 0))
    mod = _mod_spec(seq // tm, dm)
    return _call(
        body, job, name=name, grid=(tokens // tm,),
        in_specs=[tile, mod, mod, _const_spec(wgu.shape)],
        out_specs=[pl.BlockSpec((2, tm, ff), lambda i: (0, i, 0)), pl.BlockSpec((tm, ff), lambda i: (i, 0))],
        out_shape=[jax.ShapeDtypeStruct((2, tokens, ff), BF16), jax.ShapeDtypeStruct((tokens, ff), BF16)],
        args=(x, sh, sc, wgu))


def ffn_down(x, a, gt, wd, ln_g, ln_b, seq, name, job=None):
    tokens, dm = x.shape
    ff = wd.shape[0]
    chunks = _hidden_chunks(ff)
    tm = min(FFN_FWD_TILE, seq)

    def body(x_ref, a_ref, gt_ref, wd_ref, lg_ref, lb_ref, xo_ref, r_ref, f_ref):
        acc = jnp.zeros((tm, dm), F32)
        for at, wdt in chunks:
            acc = acc + _dot(a_ref[:, at:at + wdt], wd_ref[at:at + wdt, :])
        f_ref[...] = acc.astype(BF16)
        r = DN_ALPHA * x_ref[...] + (0.5 * (1.0 + gt_ref[0])) * acc
        r_ref[...] = r
        xhat, _ = _ln_stats(r)
        xo_ref[...] = xhat * lg_ref[...] + lb_ref[...]

    tile = pl.BlockSpec((tm, dm), lambda i: (i, 0))
    return _call(
        body, job, name=name, grid=(tokens // tm,),
        in_specs=[tile, pl.BlockSpec((tm, ff), lambda i: (i, 0)), _mod_spec(seq // tm, dm), _const_spec(wd.shape),
                  _const_spec((1, dm)), _const_spec((1, dm))],
        out_specs=[tile, tile, tile],
        out_shape=[jax.ShapeDtypeStruct((tokens, dm), F32), jax.ShapeDtypeStruct((tokens, dm), F32),
                   jax.ShapeDtypeStruct((tokens, dm), BF16)],
        args=(x, a, gt, wd, ln_g, ln_b))


def ffn_bwd(dy, r, x, f, gu, sh, sc, gt, wgu, wd, ln_g, seq, name, job=None):
    tokens, dm = x.shape
    ff = wgu.shape[1]
    chunks = _hidden_chunks(ff)
    tm = min(TOKEN_TILE, seq)
    tiles_per_seq = seq // tm
    nseq = tokens // seq

    def body(dy_ref, r_ref, x_ref, f_ref, gu_ref, sh_ref, sc_ref, gt_ref, wgu_ref, wd_ref, lg_ref,
             dx_ref, dgu_ref, df_ref, a_ref, h_ref, dln_ref, dmod_ref):
        i = pl.program_id(0)
        dr, dgain, dbias = _ln_bwd(dy_ref[...], r_ref[...], lg_ref[...])

        @pl.when(i == 0)
        def _():
            dln_ref[...] = jnp.zeros_like(dln_ref)

        @pl.when(i % tiles_per_seq == 0)
        def _():
            dmod_ref[...] = jnp.zeros_like(dmod_ref)

        dln_ref[0:1, :] += dgain
        dln_ref[1:2, :] += dbias
        df32 = (0.5 * (1.0 + gt_ref[0])) * dr
        df = df32.astype(BF16)
        df_ref[...] = df
        dgate = jnp.sum(dr * (0.5 * f_ref[...].astype(F32)), axis=0, keepdims=True)
        xx = x_ref[...]
        one_sc = 1.0 + sc_ref[0]
        h = (xx * one_sc + sh_ref[0]).astype(BF16)
        h_ref[...] = h
        dh = jnp.zeros((tm, dm), F32)
        for at, wdt in chunks:
            cols = slice(at, at + wdt)
            da = _dot_nt(df, wd_ref[cols, :])
            gk = gu_ref[0, :, cols].astype(F32)
            uk = gu_ref[1, :, cols].astype(F32)
            sg = _sigmoid(gk)
            sil = gk * sg
            a_ref[:, cols] = (sil * uk).astype(BF16)
            du = (da * sil).astype(BF16)
            dg = (da * uk * (sg * (1.0 + gk * (1.0 - sg)))).astype(BF16)
            dgu_ref[0, :, cols] = dg
            dgu_ref[1, :, cols] = du
            dh = dh + _dot(dg, wgu_ref[0, cols, :]) + _dot(du, wgu_ref[1, cols, :])
        dx_ref[...] = DN_ALPHA * dr + dh * one_sc
        dmod_ref[0, 0:1, :] += jnp.sum(dh, axis=0, keepdims=True)
        dmod_ref[0, 1:2, :] += jnp.sum(dh * xx, axis=0, keepdims=True)
        dmod_ref[0, 2:3, :] += dgate

    tile = pl.BlockSpec((tm, dm), lambda i: (i, 0))
    mod = _mod_spec(tiles_per_seq, dm)
    gu_spec = pl.BlockSpec((2, tm, ff), lambda i: (0, i, 0))
    return _call(
        body, job, name=name, grid=(tokens // tm,),
        in_specs=[tile, tile, tile, tile, gu_spec, mod, mod, mod, _const_spec(wgu.shape), _const_spec(wd.shape),
                  _const_spec((1, dm))],
        out_specs=[tile, gu_spec, tile, pl.BlockSpec((tm, ff), lambda i: (i, 0)), tile,
                   pl.BlockSpec((2, dm), lambda i: (0, 0)),
                   pl.BlockSpec((1, 3, dm), lambda i: (i // tiles_per_seq, 0, 0))],
        out_shape=[jax.ShapeDtypeStruct((tokens, dm), F32), jax.ShapeDtypeStruct((2, tokens, ff), BF16),
                   jax.ShapeDtypeStruct((tokens, dm), BF16), jax.ShapeDtypeStruct((tokens, ff), BF16),
                   jax.ShapeDtypeStruct((tokens, dm), BF16), jax.ShapeDtypeStruct((2, dm), F32),
                   jax.ShapeDtypeStruct((nseq, 3, dm), F32)],
        args=(dy, r, x, f, gu, sh, sc, gt, wgu, wd, ln_g))


def tn_matmul(a, b, name, job=None, b_cols=None, a_width=None):
    na, tokens, k_all = a.shape
    kk = k_all if a_width is None else a_width
    nka = k_all // kk
    assert nka * kk == k_all
    nb, _, cc = b.shape
    col = 0
    if b_cols is not None:
        col, cc = b_cols
    tt = tokens
    while 4 * tt * (kk + cc) + 8 * kk * cc > TN_VMEM_BUDGET and tt % 2 == 0 and tt > 256:
        tt //= 2
    steps = tokens // tt

    def body(a_ref, b_ref, o_ref, *acc):
        if steps == 1:
            o_ref[0, 0, 0] = _dot_tn(a_ref[0], b_ref[0]).astype(BF16)
            return
        acc_ref, = acc
        t = pl.program_id(3)

        @pl.when(t == 0)
        def _():
            acc_ref[...] = jnp.zeros_like(acc_ref)

        acc_ref[...] += _dot_tn(a_ref[0], b_ref[0])

        @pl.when(t == steps - 1)
        def _():
            o_ref[0, 0, 0] = acc_ref[...].astype(BF16)

    return _call(
        body, job, name=name, grid=(na, nka, nb, steps),
        in_specs=[pl.BlockSpec((1, tt, kk), lambda i, s, j, t: (i, t, s)),
                  pl.BlockSpec((1, tt, cc), lambda i, s, j, t: (j, t, col))],
        out_specs=[pl.BlockSpec((1, 1, 1, kk, cc), lambda i, s, j, t: (i, s, j, 0, 0))],
        out_shape=[jax.ShapeDtypeStruct((na, nka, nb, kk, cc), BF16)],
        scratch_shapes=[] if steps == 1 else [pltpu.VMEM((kk, cc), F32)], args=(a, b))


def proj_fwd(x1, sh, sc, w_in, seq, name, job=None):
    tokens, dm = x1.shape
    tm = min(MIX_TILE, seq)
    tiles_per_seq = seq // tm
    widths = [N_Q_HEADS * HEAD_DIM, N_KV_HEADS * HEAD_DIM, N_KV_HEADS * HEAD_DIM, 512, 512, 512]
    assert sum(widths) == w_in.shape[0]

    def body(x_ref, sh_ref, sc_ref, w_ref, *outs):
        h = (x_ref[...] * (1.0 + sc_ref[0]) + sh_ref[0]).astype(BF16)
        proj = _dot_nt(h, w_ref[...])
        at = 0
        for o_ref, wdt in zip(outs, widths):
            o_ref[...] = proj[:, at:at + wdt]
            at += wdt

    tile = pl.BlockSpec((tm, dm), lambda i: (i, 0))
    mod = _mod_spec(tiles_per_seq, dm)
    return _call(
        body, job, name=name, grid=(tokens // tm,),
        in_specs=[tile, mod, mod, _const_spec(w_in.shape)],
        out_specs=[pl.BlockSpec((tm, wdt), lambda i: (i, 0)) for wdt in widths],
        out_shape=[jax.ShapeDtypeStruct((tokens, wdt), F32) for wdt in widths],
        args=(x1, sh, sc, w_in))


LANES = 2 * HEAD_DIM


def _head_lane(shape):
    return lax.broadcasted_iota(jnp.int32, shape, 1) % HEAD_DIM


def _lane_half(shape):
    return lax.broadcasted_iota(jnp.int32, shape, 1) // HEAD_DIM


def _swap_rot(v):
    lane = _head_lane(v.shape)
    half = ROT_DIM // 2
    return jnp.where(lane < half, pltpu.roll(v, LANES - half, 1),
                     jnp.where(lane < ROT_DIM, pltpu.roll(v, half, 1), 0.0))


def _rope(v, cos_t, sin_t):
    return v * cos_t + _swap_rot(v) * sin_t


def _unrope(dv, cos_t, sin_t):
    return dv * cos_t + _swap_rot(dv * sin_t)


def _both_halves(t, g):
    return jnp.where(_lane_half(t.shape) == g, t, pltpu.roll(t, HEAD_DIM, 1))


def _fold_halves(t, g):
    return jnp.where(_lane_half(t.shape) == g, t + pltpu.roll(t, HEAD_DIM, 1), 0.0)


def _stack_heads(blocks):
    rows = []
    for blk in blocks:
        half = _lane_half(blk.shape)
        rows += [jnp.where(half == 0, blk, 0.0), jnp.where(half == 1, blk, 0.0)]
    return jnp.concatenate(rows, axis=0)


def _unstack_heads(t, j):
    lo = t[(2 * j) * ATTN_BLOCK:(2 * j + 1) * ATTN_BLOCK]
    hi = t[(2 * j + 1) * ATTN_BLOCK:(2 * j + 2) * ATTN_BLOCK]
    return jnp.where(_lane_half(lo.shape) == 0, lo, hi)


def _band_mask(q0, w0):
    rows, cols = GQA_GROUP * ATTN_BLOCK, 2 * ATTN_BLOCK
    qi = lax.broadcasted_iota(jnp.int32, (rows, cols), 0) % ATTN_BLOCK + q0
    ki = lax.broadcasted_iota(jnp.int32, (rows, cols), 1) + w0
    diff = qi - ki
    return (diff >= 0) & (diff < ATTN_BLOCK)


def _attn_specs(seq):
    q_spec = pl.BlockSpec((seq, GQA_GROUP * HEAD_DIM), lambda b, g: (b, g))
    kv_spec = pl.BlockSpec((seq, LANES), lambda b, g: (b, 0))
    sink_spec = pl.BlockSpec((1, GQA_GROUP * ATTN_BLOCK, 1), lambda b, g: (g, 0, 0))
    return q_spec, kv_spec, sink_spec


def _block_starts(n):
    q0 = pl.multiple_of(n * ATTN_BLOCK, ATTN_BLOCK)
    w0 = pl.multiple_of(jnp.maximum(n - 1, 0) * ATTN_BLOCK, ATTN_BLOCK)
    return q0, w0


def _stacked_queries(ref, rows):
    return _stack_heads([ref[rows, j * LANES:(j + 1) * LANES] for j in range(2)]).astype(BF16)


def _sink_columns(sinks):
    return jnp.repeat(sinks.reshape(N_KV_HEADS, GQA_GROUP), ATTN_BLOCK, axis=1)[:, :, None]


def _probs_spec(nblk):
    return pl.BlockSpec((1, 1, nblk, GQA_GROUP * ATTN_BLOCK, 2 * ATTN_BLOCK), lambda b, g: (b, g, 0, 0, 0))


def _sink_probs_spec():
    return pl.BlockSpec((1, 1, GQA_GROUP * ATTN_BLOCK, LANES), lambda b, g: (b, g, 0, 0))


def attn_fwd(q, k, v, cos_t, sin_t, sinks, seq, name, job=None):
    tokens = q.shape[0]
    nblk = seq // ATTN_BLOCK
    assert nblk >= 2
    scale = HEAD_DIM ** -0.5

    nseq = tokens // seq
    rows_stacked = GQA_GROUP * ATTN_BLOCK
    assert nblk <= LANES

    def body(q_ref, k_ref, v_ref, cos_ref, sin_ref, sink_ref, o_ref, qr_ref, p_ref, ps_ref, kd_ref, vd_ref):
        g = pl.program_id(1)
        kd_ref[...] = _both_halves(_rope(k_ref[...], cos_ref[...], sin_ref[...]), g).astype(BF16)
        vd_ref[...] = _both_halves(v_ref[...], g).astype(BF16)
        sink = sink_ref[0]
        lane = lax.broadcasted_iota(jnp.int32, (rows_stacked, LANES), 1)

        ps_ref[...] = jnp.zeros_like(ps_ref)

        def block(n, carry):
            q0, w0 = _block_starts(n)
            rows, win = pl.ds(q0, ATTN_BLOCK), pl.ds(w0, 2 * ATTN_BLOCK)
            blocks = []
            for j in range(2):
                qr = _rope(q_ref[rows, j * LANES:(j + 1) * LANES], cos_ref[rows, :], sin_ref[rows, :]).astype(BF16)
                qr_ref[rows, j * LANES:(j + 1) * LANES] = qr
                blocks.append(qr)
            qs = _stack_heads(blocks)
            s = _dot_nt(qs, kd_ref[win, :]) * scale
            s = jnp.where(_band_mask(q0, w0), s, NEG_BIG)
            m = jnp.maximum(jnp.max(s, axis=-1, keepdims=True), sink)
            p = jnp.exp(s - m)
            e_sink = jnp.exp(sink - m)
            inv = pl.reciprocal(jnp.sum(p, axis=-1, keepdims=True) + e_sink, approx=True)
            pn = (p * inv).astype(BF16)
            p_ref[0, 0, n] = pn
            out = _dot(pn, vd_ref[win, :])
            for j in range(2):
                o_ref[rows, j * LANES:(j + 1) * LANES] = _unstack_heads(out, j).astype(o_ref.dtype)
            ps_ref[0, 0] = jnp.where(lane == n, e_sink * inv, ps_ref[0, 0])
            return carry

        lax.fori_loop(0, nblk, block, 0, unroll=2)

    q_spec, kv_spec, sink_spec = _attn_specs(seq)
    return _call(
        body, job, name=name, grid=(nseq, N_KV_HEADS),
        in_specs=[q_spec, kv_spec, kv_spec, kv_spec, kv_spec, sink_spec],
        out_specs=[q_spec, q_spec, _probs_spec(nblk), _sink_probs_spec()],
        out_shape=[jax.ShapeDtypeStruct(q.shape, BF16), jax.ShapeDtypeStruct(q.shape, BF16),
                   jax.ShapeDtypeStruct((nseq, N_KV_HEADS, nblk, rows_stacked, 2 * ATTN_BLOCK), BF16),
                   jax.ShapeDtypeStruct((nseq, N_KV_HEADS, rows_stacked, LANES), F32)],
        scratch_shapes=[pltpu.VMEM((seq, LANES), BF16), pltpu.VMEM((seq, LANES), BF16)],
        args=(q, k, v, cos_t, sin_t, _sink_columns(sinks)))


def attn_bwd(qr, k, v, do, probs, sink_probs, cos_t, sin_t, seq, name, job=None):
    tokens = qr.shape[0]
    nseq = tokens // seq
    nblk = seq // ATTN_BLOCK
    assert nblk >= 2
    rows_stacked = GQA_GROUP * ATTN_BLOCK
    scale = HEAD_DIM ** -0.5

    def body(q_ref, k_ref, v_ref, do_ref, p_ref, ps_ref, cos_ref, sin_ref, dq_ref, dk_ref, dv_ref, ds_ref,
             kd_ref, vd_ref, dkd_ref, dvd_ref, acc_ref):
        g = pl.program_id(1)
        kd_ref[...] = _both_halves(_rope(k_ref[...], cos_ref[...], sin_ref[...]), g).astype(BF16)
        vd_ref[...] = _both_halves(v_ref[...], g).astype(BF16)
        dkd_ref[...] = jnp.zeros_like(dkd_ref)
        dvd_ref[...] = jnp.zeros_like(dvd_ref)
        acc_ref[...] = jnp.zeros_like(acc_ref)
        lane = lax.broadcasted_iota(jnp.int32, (rows_stacked, LANES), 1)

        def block(n, carry):
            q0, w0 = _block_starts(n)
            rows, win = pl.ds(q0, ATTN_BLOCK), pl.ds(w0, 2 * ATTN_BLOCK)
            qs = _stacked_queries(q_ref, rows)
            dos = _stacked_queries(do_ref, rows)
            kw, vw = kd_ref[win, :], vd_ref[win, :]
            pn16 = p_ref[0, 0, n]
            pn = pn16.astype(F32)
            dvd_ref[win, :] += _dot_tn(pn16, dos)
            dp = _dot_nt(dos, vw)
            delta = jnp.sum(dp * pn, axis=-1, keepdims=True)
            ds = (pn * (dp - delta)).astype(BF16)
            dqs = _dot(ds, kw) * scale
            dkd_ref[win, :] += _dot_tn(ds, qs) * scale
            cos_b, sin_b = cos_ref[rows, :], sin_ref[rows, :]
            for j in range(2):
                dq_ref[rows, j * LANES:(j + 1) * LANES] = _unrope(_unstack_heads(dqs, j), cos_b, sin_b).astype(BF16)
            acc_ref[...] += jnp.where(lane == n, ps_ref[0, 0] * delta, 0.0)
            return carry

        lax.fori_loop(0, nblk // 2, lambda i, carry: block(2 * i + 1, block(2 * i, carry)), 0)
        ds_ref[0, 0] = -jnp.sum(acc_ref[...], axis=-1, keepdims=True)
        dk_g = _unrope(_fold_halves(dkd_ref[...], g), cos_ref[...], sin_ref[...])
        dv_g = _fold_halves(dvd_ref[...], g)

        @pl.when(g == 0)
        def _():
            dk_ref[...] = dk_g
            dv_ref[...] = dv_g

        @pl.when(g != 0)
        def _():
            dk_ref[...] += dk_g
            dv_ref[...] += dv_g

    q_spec, kv_spec, _ = _attn_specs(seq)
    return _call(
        body, job, name=name, grid=(nseq, N_KV_HEADS),
        in_specs=[q_spec, kv_spec, kv_spec, q_spec, _probs_spec(nblk), _sink_probs_spec(), kv_spec, kv_spec],
        out_specs=[q_spec, kv_spec, kv_spec, pl.BlockSpec((1, 1, rows_stacked, 1), lambda b, g: (b, g, 0, 0))],
        out_shape=[jax.ShapeDtypeStruct(qr.shape, BF16), jax.ShapeDtypeStruct(k.shape, F32),
                   jax.ShapeDtypeStruct(k.shape, F32), jax.ShapeDtypeStruct((nseq, N_KV_HEADS, rows_stacked, 1), F32)],
        scratch_shapes=[pltpu.VMEM((seq, LANES), BF16), pltpu.VMEM((seq, LANES), BF16),
                        pltpu.VMEM((seq, LANES), F32), pltpu.VMEM((seq, LANES), F32),
                        pltpu.VMEM((rows_stacked, LANES), F32)],
        args=(qr, k, v, do, probs, sink_probs, cos_t, sin_t))


CONV_COLS = 128


def _shift_down(z, by):
    t = lax.broadcasted_iota(jnp.int32, z.shape, 0)
    return jnp.where(t >= by, pltpu.roll(z, by, 0), 0.0)


def _shift_up(z, by):
    n = z.shape[0]
    t = lax.broadcasted_iota(jnp.int32, z.shape, 0)
    return jnp.where(t < n - by, pltpu.roll(z, n - by, 0), 0.0)


def conv_fwd(u, bg, cg, conv_w, seq, name):
    tokens, width = u.shape

    def body(u_ref, bg_ref, cg_ref, w_ref, o_ref):
        z = cg_ref[...] * u_ref[...]
        yy = w_ref[2:3, :] * z + w_ref[1:2, :] * _shift_down(z, 1) + w_ref[0:1, :] * _shift_down(z, 2)
        o_ref[...] = (bg_ref[...] * yy).astype(BF16)

    col = pl.BlockSpec((seq, CONV_COLS), lambda j, b: (b, j))
    return pl.pallas_call(
        body, name=name, grid=(width // CONV_COLS, tokens // seq),
        in_specs=[col, col, col, pl.BlockSpec((CONV_TAPS, CONV_COLS), lambda j, b: (0, j))],
        out_specs=col, out_shape=jax.ShapeDtypeStruct((tokens, width), BF16),
        compiler_params=_params(("parallel", "parallel")),
    )(u, bg, cg, conv_w)


def conv_bwd(dout, u, bg, cg, conv_w, seq, name):
    tokens, width = u.shape

    def body(do_ref, u_ref, bg_ref, cg_ref, w_ref, du_ref, dbg_ref, dcg_ref, dw_ref):
        uu, cg_v, do = u_ref[...], cg_ref[...], do_ref[...].astype(F32)
        z = cg_v * uu
        z1, z2 = _shift_down(z, 1), _shift_down(z, 2)
        yy = w_ref[2:3, :] * z + w_ref[1:2, :] * z1 + w_ref[0:1, :] * z2
        dbg_ref[...] = (do * yy).astype(BF16)
        dyy = do * bg_ref[...]
        dz = w_ref[2:3, :] * dyy + w_ref[1:2, :] * _shift_up(dyy, 1) + w_ref[0:1, :] * _shift_up(dyy, 2)
        du_ref[...] = (dz * cg_v).astype(BF16)
        dcg_ref[...] = (dz * uu).astype(BF16)

        @pl.when(pl.program_id(1) == 0)
        def _():
            dw_ref[...] = jnp.zeros_like(dw_ref)

        dw_ref[0:1, :] += jnp.sum(dyy * z2, axis=0, keepdims=True)
        dw_ref[1:2, :] += jnp.sum(dyy * z1, axis=0, keepdims=True)
        dw_ref[2:3, :] += jnp.sum(dyy * z, axis=0, keepdims=True)

    col = pl.BlockSpec((seq, CONV_COLS), lambda j, b: (b, j))
    w_spec = pl.BlockSpec((CONV_TAPS, CONV_COLS), lambda j, b: (0, j))
    act = jax.ShapeDtypeStruct((tokens, width), BF16)
    return pl.pallas_call(
        body, name=name, grid=(width // CONV_COLS, tokens // seq),
        in_specs=[col, col, col, col, w_spec], out_specs=[col, col, col, w_spec],
        out_shape=[act, act, act, jax.ShapeDtypeStruct((CONV_TAPS, width), F32)],
        compiler_params=_params(("parallel", "arbitrary")),
    )(dout, u, bg, cg, conv_w)


def out_fwd(x1, attn, conv, gt, w_out, ln_g, ln_b, seq, name, job=None):
    tokens, dm = x1.shape
    half = attn.shape[1]
    tm = min(MIX_TILE, seq)
    tiles_per_seq = seq // tm

    def body(x_ref, a_ref, c_ref, gt_ref, w_ref, lg_ref, lb_ref, xo_ref, r_ref, mi_ref, mix_ref):
        mixin = jnp.concatenate([a_ref[...], c_ref[...]], axis=1).astype(BF16)
        mi_ref[...] = mixin
        mix = _dot(mixin, w_ref[...])
        mix_ref[...] = mix.astype(BF16)
        r = DN_ALPHA * x_ref[...] + (1.0 + gt_ref[0]) * mix
        r_ref[...] = r
        xhat, _ = _ln_stats(r)
        xo_ref[...] = xhat * lg_ref[...] + lb_ref[...]

    tile = pl.BlockSpec((tm, dm), lambda i: (i, 0))
    htile = pl.BlockSpec((tm, half), lambda i: (i, 0))
    return _call(
        body, job, name=name, grid=(tokens // tm,),
        in_specs=[tile, htile, htile, _mod_spec(tiles_per_seq, dm), _const_spec(w_out.shape),
                  _const_spec((1, dm)), _const_spec((1, dm))],
        out_specs=[tile, tile, tile, tile],
        out_shape=[jax.ShapeDtypeStruct((tokens, dm), F32), jax.ShapeDtypeStruct((tokens, dm), F32),
                   jax.ShapeDtypeStruct((tokens, dm), BF16), jax.ShapeDtypeStruct((tokens, dm), BF16)],
        args=(x1, attn, conv, gt, w_out, ln_g, ln_b))


def out_bwd(dy, r, mix, gt, w_out, ln_g, seq, name, job=None):
    tokens, dm = r.shape
    half = dm // 2
    tm = min(MIX_TILE, seq)
    tiles_per_seq = seq // tm
    nseq = tokens // seq

    def body(dy_ref, r_ref, mix_ref, gt_ref, w_ref, lg_ref, dres_ref, da_ref, dc_ref, dmix_ref, dln_ref, dgt_ref):
        i = pl.program_id(0)
        dr, dgain, dbias = _ln_bwd(dy_ref[...], r_ref[...], lg_ref[...])

        @pl.when(i == 0)
        def _():
            dln_ref[...] = jnp.zeros_like(dln_ref)

        @pl.when(i % tiles_per_seq == 0)
        def _():
            dgt_ref[...] = jnp.zeros_like(dgt_ref)

        dln_ref[0:1, :] += dgain
        dln_ref[1:2, :] += dbias
        dgt_ref[0] += jnp.sum(dr * mix_ref[...].astype(F32), axis=0, keepdims=True)
        dres_ref[...] = DN_ALPHA * dr
        dmix = ((1.0 + gt_ref[0]) * dr).astype(BF16)
        dmix_ref[...] = dmix
        dmixin = _dot_nt(dmix, w_ref[...])
        da_ref[...] = dmixin[:, :half].astype(BF16)
        dc_ref[...] = dmixin[:, half:].astype(BF16)

    tile = pl.BlockSpec((tm, dm), lambda i: (i, 0))
    htile = pl.BlockSpec((tm, half), lambda i: (i, 0))
    return _call(
        body, job, name=name, grid=(tokens // tm,),
        in_specs=[tile, tile, tile, _mod_spec(tiles_per_seq, dm), _const_spec(w_out.shape), _const_spec((1, dm))],
        out_specs=[tile, htile, htile, tile, pl.BlockSpec((2, dm), lambda i: (0, 0)),
                   pl.BlockSpec((1, 1, dm), lambda i: (i // tiles_per_seq, 0, 0))],
        out_shape=[jax.ShapeDtypeStruct((tokens, dm), F32), jax.ShapeDtypeStruct((tokens, half), BF16),
                   jax.ShapeDtypeStruct((tokens, half), BF16), jax.ShapeDtypeStruct((tokens, dm), BF16),
                   jax.ShapeDtypeStruct((2, dm), F32), jax.ShapeDtypeStruct((nseq, 1, dm), F32)],
        args=(dy, r, mix, gt, w_out, ln_g))


def proj_bwd(parts, dres, x1, sh, sc, w_in, seq, name, job=None):
    tokens, dm = x1.shape
    tm = min(MIX_TILE, seq)
    tiles_per_seq = seq // tm
    nseq = tokens // seq
    widths = [p.shape[1] for p in parts]
    total = sum(widths)

    def body(*refs):
        part_refs = refs[:6]
        dres_ref, x_ref, sh_ref, sc_ref, w_ref, dx_ref, dproj_ref, h_ref, dmod_ref = refs[6:]
        dproj = jnp.concatenate([p[...].astype(BF16) for p in part_refs], axis=1)
        dproj_ref[...] = dproj
        dh = _dot(dproj, w_ref[...])
        xx = x_ref[...]
        one_sc = 1.0 + sc_ref[0]
        h_ref[...] = (xx * one_sc + sh_ref[0]).astype(BF16)
        dx_ref[...] = dres_ref[...] + dh * one_sc

        @pl.when(pl.program_id(0) % tiles_per_seq == 0)
        def _():
            dmod_ref[...] = jnp.zeros_like(dmod_ref)

        dmod_ref[0, 0:1, :] += jnp.sum(dh, axis=0, keepdims=True)
        dmod_ref[0, 1:2, :] += jnp.sum(dh * xx, axis=0, keepdims=True)

    tile = pl.BlockSpec((tm, dm), lambda i: (i, 0))
    mod = _mod_spec(tiles_per_seq, dm)
    return _call(
        body, job, name=name, grid=(tokens // tm,),
        in_specs=[pl.BlockSpec((tm, wdt), lambda i: (i, 0)) for wdt in widths]
        + [tile, tile, mod, mod, _const_spec(w_in.shape)],
        out_specs=[tile, pl.BlockSpec((tm, total), lambda i: (i, 0)), tile,
                   pl.BlockSpec((1, 2, dm), lambda i: (i // tiles_per_seq, 0, 0))],
        out_shape=[jax.ShapeDtypeStruct((tokens, dm), F32), jax.ShapeDtypeStruct((tokens, total), BF16),
                   jax.ShapeDtypeStruct((tokens, dm), BF16), jax.ShapeDtypeStruct((nseq, 2, dm), F32)],
        args=(*parts, dres, x1, sh, sc, w_in))


def _rope_tables(positions):
    half = ROT_DIM // 2
    inv_freq = jnp.power(jnp.float32(ROPE_THETA), -jnp.arange(0, ROT_DIM, 2, dtype=F32) / ROT_DIM)
    lane = jnp.arange(LANES) % HEAD_DIM
    freq = jnp.where(lane < ROT_DIM, inv_freq[lane % half], 0.0)
    sign = jnp.where(lane < half, -1.0, 1.0).astype(F32)
    ang = positions.astype(F32)[:, None] * freq[None, :]
    return jnp.cos(ang), sign[None, :] * jnp.sin(ang)


def kernel(x, c, positions, w_ada, b_ada, ffn1_w_gate_up, ffn1_w_down, ln1_g, ln1_b, w_in, conv_w, attn_sinks, w_out, ln2_g, ln2_b, ffn2_w_gate_up, ffn2_w_down, ln3_g, ln3_b, loss_target, m_w_ada, m_b_ada, m_ffn1_w_gate_up, m_ffn1_w_down, m_ln1_g, m_ln1_b, m_w_in, m_conv_w, m_attn_sinks, m_w_out, m_ln2_g, m_ln2_b, m_ffn2_w_gate_up, m_ffn2_w_down, m_ln3_g, m_ln3_b, v_w_ada, v_b_ada, v_ffn1_w_gate_up, v_ffn1_w_down, v_ln1_g, v_ln1_b, v_w_in, v_conv_w, v_attn_sinks, v_w_out, v_ln2_g, v_ln2_b, v_ffn2_w_gate_up, v_ffn2_w_down, v_ln3_g, v_ln3_b):
    nseq, seq, dm = x.shape
    tokens = nseq * seq
    dev = 4 * lax.axis_index("x") + 2 * lax.axis_index("y") + lax.axis_index("c")
    ada_cols = w_ada.shape[2]
    ff = ffn1_w_down.shape[1] * N_DEV
    fc = ff // 4
    in_cols = w_in.shape[2]
    conv_cols = conv_w.shape[2]

    def t_bf16(w):
        return w[0].T.astype(BF16)

    c_all, convw_all = all_gather([c, conv_w[0]], "gather_cond")
    c_all = c_all.reshape(N_DEV * nseq, dm)
    convw_full = convw_all.transpose(1, 0, 2).reshape(CONV_TAPS, N_DEV * conv_cols)

    b_cols = lax.dynamic_slice(b_ada, (0, dev * ada_cols), (1, ada_cols))
    cond_all, mod_cols = ada_fwd(c_all, w_ada[0], b_cols, "ada_fwd")
    wgu1, mod_all = all_gather([t_bf16(ffn1_w_gate_up), mod_cols], "gather_ffn1")
    wgu1 = wgu1.reshape(2, ff, dm)
    mod = lax.dynamic_slice(mod_all, (0, dev * nseq, 0), (N_DEV, nseq, ada_cols))
    mod = mod.transpose(1, 0, 2).reshape(nseq, 9, 1, dm)
    sh1, sc1, g1, sh2, sc2, g2, sh3, sc3, g3 = [mod[:, i] for i in range(9)]

    x0 = x.reshape(tokens, dm)
    (gu1, a1_fwd), (wd1, wout) = ffn_up(x0, sh1, sc1, wgu1, seq, "ffn1_up",
                                        job=_GatherJob([ffn1_w_down[0].astype(BF16), w_out[0].astype(BF16)]))
    wd1, wout = wd1.reshape(ff, dm), wout.reshape(dm, dm)
    (x1, r1, f1), (win,) = ffn_down(x0, a1_fwd, g1, wd1, ln1_g, ln1_b, seq, "ffn1_down", job=_GatherJob([t_bf16(w_in)]))
    win = win.reshape(N_DEV * in_cols, dm)
    (q, k, v, u, bg, cg), wd2_spread = proj_fwd(x1, sh2, sc2, win, seq, "proj_fwd",
                                                job=gather_spread_job([ffn2_w_down[0].astype(BF16)]))
    cos_t, sin_t = _rope_tables(positions.reshape(tokens))
    sinks = attn_sinks[0]
    (attn, q_rot, probs, sink_probs), wgu2_spread = attn_fwd(q, k, v, cos_t, sin_t, sinks, seq, "attn_fwd",
                                                             job=gather_spread_job([t_bf16(ffn2_w_gate_up)]))
    conv = conv_fwd(u, bg, cg, convw_full, seq, "conv_fwd")
    (x2, r2, mixin, mix), (wd2, wgu2) = out_fwd(x1, attn, conv, g2, wout, ln2_g, ln2_b, seq, "out_fwd",
                                                job=gather_forward_job(wd2_spread + wgu2_spread))
    wd2, wgu2 = wd2.reshape(ff, dm), wgu2.reshape(2, ff, dm)
    target = loss_target.reshape(tokens, dm)
    (dy3, loss_part, r3, gu3, f3), _ = ffn_fwd(x2, sh3, sc3, g3, wgu2, wd2, ln3_g, ln3_b, seq, "ffn2_fwd", target=target)

    (dx2, dgu3, df3, a3, h3, dln3, dmod3), _ = ffn_bwd(dy3, r3, x2, f3, gu3, sh3, sc3, g3, wgu2, wd2, ln3_g, seq, "ffn2_bwd")
    pair = 2 * fc
    g_wd2 = tn_matmul(a3[None], df3[None], "ffn2_dwd", a_width=pair)[0][0].reshape(N_DEV, ff // N_DEV, dm)
    g_wgu2 = tn_matmul(dgu3, h3[None], "ffn2_dwgu", a_width=pair)[0][0].reshape(N_DEV, fc, dm)
    (dres2, dattn, dconv, dmix, dln2, dg2), swapped = out_bwd(dx2, r2, mix, g2, wout, ln2_g, seq, "out_bwd",
                                                              job=swap_job([g_wgu2, g_wd2]))
    p_wgu2, own_wgu2 = pair_sum(g_wgu2, swapped[0], "pair_wgu2")
    p_wd2, own_wd2 = pair_sum(g_wd2, swapped[1], "pair_wd2")
    du, dbg, dcg, dconvw = conv_bwd(dconv, u, bg, cg, convw_full, seq, "conv_bwd")
    (dq, dk, dv, dsink_rows), (far_wd2,) = attn_bwd(
        q_rot, k, v, dattn, probs, sink_probs, cos_t, sin_t, seq, "attn_bwd", job=chip_exchange_job([p_wd2]))
    parts = [dq, dk, dv, du, dbg, dcg]
    (dx1, dproj, h2, dmod2), far_top = proj_bwd(parts, dres2, x1, sh2, sc2, win, seq, "proj_bwd",
                                                job=chip_exchange_job([p_wgu2], rows=(0, fc // 2)))
    (dx0, dgu1, df1, a1, h1, dln1, dmod1), _ = ffn_bwd(
        dx1, r1, x0, f1, gu1, sh1, sc1, g1, wgu1, wd1, ln1_g, seq, "ffn1_bwd")

    dmod = jnp.concatenate([dmod1, dmod2, dg2, dmod3], axis=1).reshape(nseq, 9 * dm)
    half = dm // 2
    jobs = _Jobs([gather_spread_job([dmod]),
                  chip_exchange_job([p_wgu2], rows=(fc // 2, fc // 2), into=far_top)])
    (g_wd1,), res = tn_matmul(a1[None], df1[None], "ffn1_dwd", job=jobs, a_width=pair)
    dmod_spread, (far_wgu2,) = jobs.split(res)
    g_wd1 = g_wd1.reshape(N_DEV, ff // N_DEV, dm)
    jobs = _Jobs([swap_job([g_wd1]), gather_forward_job(dmod_spread)])
    (g_l,), res = tn_matmul(dgu1, h1[None], "ffn1_dwgu_l", job=jobs, b_cols=(0, half), a_width=pair)
    (sw_wd1,), (dmod_all,) = jobs.split(res)
    g_l = g_l.reshape(N_DEV, fc, half)
    p_wd1, own_wd1 = pair_sum(g_wd1, sw_wd1, "pair_wd1")
    jobs = _Jobs([chip_exchange_job([p_wd1]), swap_job([g_l])])
    (g_r,), res = tn_matmul(dgu1, h1[None], "ffn1_dwgu_r", job=jobs, b_cols=(1, half), a_width=pair)
    (far_wd1,), (sw_l,) = jobs.split(res)
    g_r = g_r.reshape(N_DEV, fc, half)
    p_l, own_l = pair_sum(g_l, sw_l, "pair_wgu1_l")

    dmod_cols = lax.dynamic_slice(dmod_all.reshape(N_DEV * nseq, 9 * dm), (0, dev * ada_cols), (N_DEV * nseq, ada_cols))
    grad_w_ada, gb_cols = ada_bwd(cond_all, dmod_cols, "ada_bwd")
    dsinks = jnp.sum(dsink_rows.reshape(nseq, N_Q_HEADS, ATTN_BLOCK), axis=(0, 2))
    small = jnp.zeros((8, dm), F32)
    small = small.at[0:2].set(dln1).at[2:4].set(dln2).at[4:6].set(dln3)
    small = small.at[6, 0:N_Q_HEADS].set(dsinks).at[7, 0].set(loss_part[0, 0])

    jobs = _Jobs([chip_exchange_job([p_l]), swap_job([g_r]), gather_spread_job([small, dconvw, gb_cols])])
    (g_win,), res = tn_matmul(dproj[None], h2[None], "dwin", job=jobs)
    (far_l,), (sw_r,), small_spread = jobs.split(res)
    g_win = g_win.reshape(N_DEV, in_cols, dm)
    p_r, own_r = pair_sum(g_r, sw_r, "pair_wgu1_r")
    jobs = _Jobs([chip_exchange_job([p_r]), swap_job([g_win]), gather_forward_job(small_spread)])
    (g_wout,), res = tn_matmul(mixin[None], dmix[None], "dwout", job=jobs)
    (far_r,), (sw_win,), (small_all, dconvw_all, gb_all) = jobs.split(res)
    g_wout = g_wout.reshape(N_DEV, dm // N_DEV, dm)
    p_win, own_win = pair_sum(g_win, sw_win, "pair_win")
    jobs = _Jobs([chip_exchange_job([p_win]), swap_job([g_wout])])
    (far_win,), (sw_wout,) = jobs.split(run_job(jobs, "rs_tail_win"))
    p_wout, own_wout = pair_sum(g_wout, sw_wout, "pair_wout")
    (far_wout,) = run_job(chip_exchange_job([p_wout]), "rs_tail_wout")

    grads = {
        "ffn1_w_gate_up": jnp.concatenate([own_l, own_r], axis=1), "ffn1_w_down": own_wd1,
        "w_in": own_win, "w_out": own_wout, "ffn2_w_gate_up": own_wgu2, "ffn2_w_down": own_wd2,
    }
    others = {"ffn1_w_gate_up": jnp.concatenate([far_l, far_r], axis=2), "ffn1_w_down": far_wd1,
              "w_in": far_win, "w_out": far_wout, "ffn2_w_gate_up": far_wgu2, "ffn2_w_down": far_wd2}

    grads["w_ada"] = grad_w_ada
    small_sum = sum_devices(small_all, "sum_small")
    dconvw_sum = sum_devices(dconvw_all, "sum_convw")
    loss = small_sum[7, 0]
    grads["b_ada"] = gb_all.reshape(1, N_DEV * ada_cols)
    grads["conv_w"] = lax.dynamic_slice(dconvw_sum, (0, dev * conv_cols), (CONV_TAPS, conv_cols))
    grads["attn_sinks"] = small_sum[6:7, 0:N_Q_HEADS]
    for i, nm in enumerate(["ln1_g", "ln1_b", "ln2_g", "ln2_b", "ln3_g", "ln3_b"]):
        grads[nm] = small_sum[i:i + 1]

    given = dict(w_ada=(w_ada, m_w_ada, v_w_ada), b_ada=(b_ada, m_b_ada, v_b_ada),
                 ffn1_w_gate_up=(ffn1_w_gate_up, m_ffn1_w_gate_up, v_ffn1_w_gate_up),
                 ffn1_w_down=(ffn1_w_down, m_ffn1_w_down, v_ffn1_w_down),
                 ln1_g=(ln1_g, m_ln1_g, v_ln1_g), ln1_b=(ln1_b, m_ln1_b, v_ln1_b),
                 w_in=(w_in, m_w_in, v_w_in), conv_w=(conv_w, m_conv_w, v_conv_w),
                 attn_sinks=(attn_sinks, m_attn_sinks, v_attn_sinks), w_out=(w_out, m_w_out, v_w_out),
                 ln2_g=(ln2_g, m_ln2_g, v_ln2_g), ln2_b=(ln2_b, m_ln2_b, v_ln2_b),
                 ffn2_w_gate_up=(ffn2_w_gate_up, m_ffn2_w_gate_up, v_ffn2_w_gate_up),
                 ffn2_w_down=(ffn2_w_down, m_ffn2_w_down, v_ffn2_w_down),
                 ln3_g=(ln3_g, m_ln3_g, v_ln3_g), ln3_b=(ln3_b, m_ln3_b, v_ln3_b))
    order = ["w_ada", "b_ada", "ffn1_w_gate_up", "ffn1_w_down", "ln1_g", "ln1_b", "w_in", "conv_w", "attn_sinks",
             "w_out", "ln2_g", "ln2_b", "ffn2_w_gate_up", "ffn2_w_down", "ln3_g", "ln3_b"]
    transposed = ("ffn1_w_gate_up", "ffn2_w_gate_up", "w_in")
    big = ("w_ada", "ffn1_w_gate_up", "ffn1_w_down", "w_in", "w_out", "ffn2_w_gate_up", "ffn2_w_down")
    results = {}
    for nm in big:
        if nm in transposed:
            w2, m2, v2 = [t[0].T for t in given[nm]]
            results[nm] = [t.T[None] for t in adamw(w2, grads[nm], m2, v2, "adamw_" + nm, others=others.get(nm))]
        else:
            w2, m2, v2 = [t[0] for t in given[nm]]
            results[nm] = [t[None] for t in adamw(w2, grads[nm], m2, v2, "adamw_" + nm, others=others.get(nm))]
    small_names = [nm for nm in order if nm not in big]
    items = []
    for nm in small_names:
        shape = given[nm][0].shape
        two_d = (shape[-2], shape[-1])
        items.append((given[nm][0].reshape(two_d), grads[nm].reshape(two_d), *[t.reshape(two_d) for t in given[nm][1:]]))
    for nm, res in zip(small_names, adamw_small(items, "adamw_small")):
        shape = given[nm][0].shape
        results[nm] = [grads[nm].reshape(shape)] + [t.reshape(shape) for t in res]
    grad_x = dx0.reshape(nseq, seq, dm)
    return (loss, grad_x, *[results[nm][i] for i in range(4) for nm in order])
```

```python
import functools

import jax
import jax.numpy as jnp
from jax import lax
from jax.experimental import pallas as pl
from jax.experimental.pallas import tpu as pltpu

F32 = jnp.float32
BF16 = jnp.bfloat16
MESH = pl.DeviceIdType.MESH

N_DEV = 8
N_CHIP = 4
HEAD_DIM = 64
N_Q_HEADS = 8
N_KV_HEADS = 2
GQA_GROUP = N_Q_HEADS // N_KV_HEADS
ATTN_BLOCK = 128
ROT_DIM = 16
ROPE_THETA = 500000.0
CONV_TAPS = 3
LN_EPS = 1e-5
DN_ALPHA = 2.0 ** 0.25
ADAM_LR = 0.001
ADAM_B1 = 0.9
ADAM_B2 = 0.999
ADAM_EPS = 1e-08
ADAM_WD = 0.01
ADAM_STEP = 10
NEG_BIG = -1e30

VMEM_LIMIT = 56 * 1024 * 1024
TOKEN_TILE = 256
FFN_FWD_TILE = 512
MIX_TILE = 512
TN_VMEM_BUDGET = 36 * 1024 * 1024


def _params(semantics=None, vmem=VMEM_LIMIT):
    return pltpu.CompilerParams(dimension_semantics=semantics, vmem_limit_bytes=vmem)


def _dot(a, b):
    return jnp.dot(a, b, preferred_element_type=F32)


def _dot_nt(a, b):
    return lax.dot_general(a, b, (((1,), (1,)), ((), ())), preferred_element_type=F32)


def _dot_tn(a, b):
    return lax.dot_general(a, b, (((0,), (0,)), ((), ())), preferred_element_type=F32)


def _sigmoid(x):
    return pl.reciprocal(1.0 + jnp.exp(-x), approx=True)


def _ln_stats(r):
    mu = jnp.mean(r, axis=-1, keepdims=True)
    d = r - mu
    var = jnp.mean(d * d, axis=-1, keepdims=True)
    rstd = lax.rsqrt(var + LN_EPS)
    return d * rstd, rstd


def _ln_bwd(dy, r, g):
    xhat, rstd = _ln_stats(r)
    dxhat = dy * g
    c1 = jnp.mean(dxhat, axis=-1, keepdims=True)
    c2 = jnp.mean(dxhat * xhat, axis=-1, keepdims=True)
    dr = rstd * (dxhat - c1 - xhat * c2)
    return dr, jnp.sum(dy * xhat, axis=0, keepdims=True), jnp.sum(dy, axis=0, keepdims=True)


def _const_spec(shape):
    nd = len(shape)
    return pl.BlockSpec(shape, lambda *_: (0,) * nd, pipeline_mode=pl.Buffered(1))


def all_gather(arrs, name):
    n = len(arrs)

    def body(*refs):
        ins, outs = refs[:n], refs[n:2 * n]
        send_sems, recv_sems, local_sems = refs[2 * n:]
        x, y, c = lax.axis_index("x"), lax.axis_index("y"), lax.axis_index("c")
        me, sibling = (x, y, c), (x, y, 1 - c)
        chips = [(1 - x, y), (x, 1 - y), (1 - x, 1 - y)]

        def slot(i, p):
            return outs[i].at[4 * p[0] + 2 * p[1] + p[2]]

        def copy(i, k, block, to, src=None):
            return pltpu.make_async_remote_copy(
                src_ref=slot(i, block) if src is None else src, dst_ref=slot(i, block),
                send_sem=send_sems.at[i, k], recv_sem=recv_sems.at[i, k],
                device_id=to, device_id_type=MESH)

        mine = [pltpu.make_async_copy(ins[i], slot(i, me), local_sems.at[i]) for i in range(n)]
        for cp in mine:
            cp.start()
        first = []
        for i in range(n):
            first.append(copy(i, 0, me, sibling, src=ins[i]))
            first += [copy(i, 1 + j, me, (*chip, c), src=ins[i]) for j, chip in enumerate(chips)]
        for cp in first:
            cp.start()
        passed = []
        for i in range(n):
            for j, chip in enumerate(chips):
                copy(i, 1 + j, (*chip, c), me).wait_recv()
                cp = copy(i, 4 + j, (*chip, c), sibling)
                cp.start()
                passed.append(cp)
        for i in range(n):
            copy(i, 0, sibling, me).wait_recv()
            for j, chip in enumerate(chips):
                copy(i, 4 + j, (*chip, 1 - c), me).wait_recv()
        for cp in first + passed:
            cp.wait_send()
        for cp in mine:
            cp.wait()

    any_spec = pl.BlockSpec(memory_space=pl.ANY)
    return pl.pallas_call(
        body, name=name,
        out_shape=[jax.ShapeDtypeStruct((N_DEV, *a.shape), a.dtype) for a in arrs],
        in_specs=[any_spec] * n, out_specs=[any_spec] * n,
        scratch_shapes=[pltpu.SemaphoreType.DMA((n, 7)), pltpu.SemaphoreType.DMA((n, 7)),
                        pltpu.SemaphoreType.DMA((n,))],
    )(*arrs)


def _place():
    x, y, c = lax.axis_index("x"), lax.axis_index("y"), lax.axis_index("c")
    return x, y, c, [(1 - x, y), (x, 1 - y), (1 - x, 1 - y)]


def _slot(p):
    return 4 * p[0] + 2 * p[1] + p[2]


class _Job:
    def __init__(self, ins, outs, nsem, copies, aliases=None, local=None):
        self.ins, self.outs, self.nsem, self.copies = list(ins), list(outs), nsem, copies
        self.aliases = aliases or {}
        self.local = local

    def scratch(self):
        s = [pltpu.SemaphoreType.DMA(self.nsem), pltpu.SemaphoreType.DMA(self.nsem)]
        if self.local is not None:
            s.append(pltpu.SemaphoreType.DMA((len(self.ins),)))
        return s

    def start(self, ins, outs, sems):
        if self.local is not None:
            for cp in self.local(ins, outs, sems[2]):
                cp.start()
        for cp in self.copies(ins, outs, sems[0], sems[1])[0]:
            cp.start()

    def finish(self, ins, outs, sems):
        started, awaited = self.copies(ins, outs, sems[0], sems[1])
        for cp in awaited:
            cp.wait_recv()
        for cp in started:
            cp.wait_send()
        if self.local is not None:
            for cp in self.local(ins, outs, sems[2]):
                cp.wait()


class _Jobs:
    def __init__(self, jobs):
        self.jobs = jobs
        self.ins = [a for j in jobs for a in j.ins]
        self.outs = [o for j in jobs for o in j.outs]
        self.two_phase = any(getattr(j, "two_phase", False) for j in jobs)
        self.aliases = {}
        at_in = at_out = 0
        for j in jobs:
            self.aliases.update({at_in + i: at_out + o for i, o in j.aliases.items()})
            at_in, at_out = at_in + len(j.ins), at_out + len(j.outs)

    def scratch(self):
        return [s for j in self.jobs for s in j.scratch()]

    def _each(self, ins, outs, sems):
        at_in = at_out = at_sem = 0
        for j in self.jobs:
            n_in, n_out, n_sem = len(j.ins), len(j.outs), len(j.scratch())
            yield j, ins[at_in:at_in + n_in], outs[at_out:at_out + n_out], sems[at_sem:at_sem + n_sem]
            at_in, at_out, at_sem = at_in + n_in, at_out + n_out, at_sem + n_sem

    def start(self, ins, outs, sems):
        for j, i, o, s in self._each(ins, outs, sems):
            j.start(i, o, s)

    def turn(self, ins, outs, sems):
        for j, i, o, s in self._each(ins, outs, sems):
            if getattr(j, "two_phase", False):
                j.turn(i, o, s)

    def finish(self, ins, outs, sems):
        for j, i, o, s in self._each(ins, outs, sems):
            j.finish(i, o, s)

    def split(self, results):
        at, parts = 0, []
        for j in self.jobs:
            parts.append(results[at:at + len(j.outs)])
            at += len(j.outs)
        return parts


def _remote(src, dst, send, recv, idx, to):
    return pltpu.make_async_remote_copy(src_ref=src, dst_ref=dst, send_sem=send.at[idx], recv_sem=recv.at[idx],
                                        device_id=to, device_id_type=MESH)


def _spread_copies(ins, outs, send, recv, base=0):
    x, y, c, chips = _place()
    me = (x, y, c)
    peers = [(x, y, 1 - c)] + [(*chip, c) for chip in chips]
    started, awaited = [], []
    for i, (src, dst) in enumerate(zip(ins, outs)):
        for k, peer in enumerate(peers):
            started.append(_remote(src, dst.at[_slot(me)], send, recv, (base + i, k), peer))
            awaited.append(_remote(src, dst.at[_slot(peer)], send, recv, (base + i, k), peer))
    return started, awaited


def _forward_copies(ins, outs, send, recv, base=0):
    x, y, c, chips = _place()
    started, awaited = [], []
    for i, buf in enumerate(outs):
        for j, chip in enumerate(chips):
            mine, theirs = buf.at[_slot((*chip, c))], buf.at[_slot((*chip, 1 - c))]
            started.append(_remote(mine, mine, send, recv, (base + i, j), (x, y, 1 - c)))
            awaited.append(_remote(theirs, theirs, send, recv, (base + i, j), (x, y, 1 - c)))
    return started, awaited


def _own_block_copies(ins, outs, sems):
    x, y, c, _ = _place()
    return [pltpu.make_async_copy(src, dst.at[_slot((x, y, c))], sems.at[i])
            for i, (src, dst) in enumerate(zip(ins, outs))]


def gather_spread_job(shards):
    outs = [jax.ShapeDtypeStruct((N_DEV, *a.shape), a.dtype) for a in shards]
    return _Job(shards, outs, (len(shards), 4), _spread_copies, local=_own_block_copies)


def gather_forward_job(fulls):
    outs = [jax.ShapeDtypeStruct(a.shape, a.dtype) for a in fulls]
    return _Job(fulls, outs, (len(fulls), 3), _forward_copies, aliases={i: i for i in range(len(fulls))})


TURN_EIGHTHS = 6


class _GatherJob:
    two_phase = True

    def __init__(self, shards):
        self.ins = list(shards)
        self.outs = [jax.ShapeDtypeStruct((N_DEV, *a.shape), a.dtype) for a in shards]
        self.aliases = {}

    def scratch(self):
        n = len(self.ins)
        return [pltpu.SemaphoreType.DMA((n, 4)), pltpu.SemaphoreType.DMA((n, 4)),
                pltpu.SemaphoreType.DMA((n, 3)), pltpu.SemaphoreType.DMA((n, 3)), pltpu.SemaphoreType.DMA((n,))]

    def start(self, ins, outs, sems):
        for cp in _own_block_copies(ins, outs, sems[4]) + _spread_copies(ins, outs, sems[0], sems[1])[0]:
            cp.start()

    def turn(self, ins, outs, sems):
        for cp in _spread_copies(ins, outs, sems[0], sems[1])[1]:
            cp.wait_recv()
        for cp in _forward_copies(outs, outs, sems[2], sems[3])[0]:
            cp.start()

    def finish(self, ins, outs, sems):
        handed_on, arriving = _forward_copies(outs, outs, sems[2], sems[3])
        for cp in arriving:
            cp.wait_recv()
        for cp in _spread_copies(ins, outs, sems[0], sems[1])[0] + handed_on:
            cp.wait_send()
        for cp in _own_block_copies(ins, outs, sems[4]):
            cp.wait()


def swap_job(gs):
    def copies(ins, outs, send, recv):
        x, y, c, _ = _place()
        started, awaited = [], []
        for i, (g, r1) in enumerate(zip(ins, outs)):
            for q in range(N_CHIP):
                started.append(_remote(g.at[2 * q + (1 - c)], r1.at[q], send, recv, (i, q), (x, y, 1 - c)))
                awaited.append(_remote(g.at[2 * q + c], r1.at[q], send, recv, (i, q), (x, y, 1 - c)))
        return started, awaited

    outs = [jax.ShapeDtypeStruct((N_CHIP, *g.shape[1:]), g.dtype) for g in gs]
    return _Job(gs, outs, (len(gs), N_CHIP), copies)


def chip_exchange_job(ps, rows=None, into=None):
    n = len(ps)

    def copies(ins, outs, send, recv):
        x, y, c, chips = _place()
        started, awaited = [], []
        for i, (p, r2) in enumerate(zip(ins[:n], outs)):
            for k, chip in enumerate(chips):
                src, mine, dst = p.at[2 * chip[0] + chip[1]], p.at[2 * x + y], r2.at[k]
                if rows is not None:
                    src, mine, dst = (t.at[pl.ds(rows[0], rows[1])] for t in (src, mine, dst))
                started.append(_remote(src, dst, send, recv, (i, k), (*chip, c)))
                awaited.append(_remote(mine, dst, send, recv, (i, k), (*chip, c)))
        return started, awaited

    outs = [jax.ShapeDtypeStruct((3, *p.shape[1:]), p.dtype) for p in ps]
    if into is None:
        return _Job(ps, outs, (n, 3), copies)
    return _Job(list(ps) + list(into), outs, (n, 3), copies, aliases={n + i: i for i in range(n)})


def _call(body, job, *, name, grid, in_specs, out_specs, out_shape, args, scratch_shapes=(), vmem=VMEM_LIMIT):
    if job is None:
        res = pl.pallas_call(
            body, name=name, grid=grid, in_specs=in_specs, out_specs=out_specs, out_shape=out_shape,
            scratch_shapes=list(scratch_shapes), compiler_params=_params(("arbitrary",) * len(grid), vmem),
        )(*args)
        return res, []
    n_in, n_out, n_scr = len(in_specs), len(out_specs), len(scratch_shapes)
    j_in, j_out = len(job.ins), len(job.outs)

    def with_copies(*refs):
        at = 0
        ins = refs[at:at + n_in]; at += n_in
        jins = refs[at:at + j_in]; at += j_in
        outs = refs[at:at + n_out]; at += n_out
        jouts = refs[at:at + j_out]; at += j_out
        scr = refs[at:at + n_scr]; at += n_scr
        sems = refs[at:]
        ids = [pl.program_id(d) for d in range(len(grid))]
        first = functools.reduce(jnp.logical_and, [i == 0 for i in ids])
        last = functools.reduce(jnp.logical_and, [i == n - 1 for i, n in zip(ids, grid)])

        @pl.when(first)
        def _():
            job.start(jins, jouts, sems)

        if getattr(job, "two_phase", False):
            steps, at = 1, 0
            for i, n in zip(ids, grid):
                steps, at = steps * n, at * n + i

            @pl.when(at == (TURN_EIGHTHS * steps) // 8)
            def _():
                job.turn(jins, jouts, sems)

        body(*ins, *outs, *scr)

        @pl.when(last)
        def _():
            job.finish(jins, jouts, sems)

    any_spec = pl.BlockSpec(memory_space=pl.ANY)
    res = pl.pallas_call(
        with_copies, name=name, grid=grid,
        in_specs=list(in_specs) + [any_spec] * j_in, out_specs=list(out_specs) + [any_spec] * j_out,
        out_shape=list(out_shape) + list(job.outs),
        input_output_aliases={n_in + i: n_out + o for i, o in job.aliases.items()},
        scratch_shapes=list(scratch_shapes) + job.scratch(),
        compiler_params=_params(("arbitrary",) * len(grid), vmem),
    )(*args, *job.ins)
    return res[:n_out], res[n_out:]


def run_job(job, name):
    def body(*refs):
        j_in, j_out = len(job.ins), len(job.outs)
        ins, outs, sems = refs[:j_in], refs[j_in:j_in + j_out], refs[j_in + j_out:]
        job.start(ins, outs, sems)
        job.finish(ins, outs, sems)

    any_spec = pl.BlockSpec(memory_space=pl.ANY)
    return pl.pallas_call(
        body, name=name, in_specs=[any_spec] * len(job.ins), out_specs=[any_spec] * len(job.outs),
        out_shape=list(job.outs), input_output_aliases=dict(job.aliases), scratch_shapes=job.scratch(),
    )(*job.ins)


def pair_sum(core, g, r1, name):
    _, rows, cols = g.shape
    rb = next(cand for cand in range(min(rows, 512), 0, -16) if rows % cand == 0)

    def body(core_ref, g_ref, r1_ref, p_ref, own_ref):
        del core_ref
        x, y, _, _ = _place()
        s = g_ref[0].astype(F32) + r1_ref[0].astype(F32)
        p_ref[0] = s.astype(BF16)

        @pl.when(pl.program_id(1) == 2 * x + y)
        def _():
            own_ref[...] = s

    chunk = (1, rb, cols)
    return pl.pallas_call(
        body, name=name,
        grid_spec=pltpu.PrefetchScalarGridSpec(
            num_scalar_prefetch=1, grid=(rows // rb, N_CHIP),
            in_specs=[pl.BlockSpec(chunk, lambda i, q, core_ref: (2 * q + core_ref[0], i, 0)),
                      pl.BlockSpec(chunk, lambda i, q, core_ref: (q, i, 0))],
            out_specs=[pl.BlockSpec(chunk, lambda i, q, core_ref: (q, i, 0)),
                       pl.BlockSpec((rb, cols), lambda i, q, core_ref: (i, 0))]),
        out_shape=[jax.ShapeDtypeStruct((N_CHIP, rows, cols), BF16), jax.ShapeDtypeStruct((rows, cols), F32)],
        compiler_params=_params(("arbitrary", "arbitrary")),
    )(core, g, r1)


def sum_devices(a, name):
    def body(a_ref, o_ref):
        acc = a_ref[0]
        for d in range(1, N_DEV):
            acc = acc + a_ref[d]
        o_ref[...] = acc

    return pl.pallas_call(body, name=name, out_shape=jax.ShapeDtypeStruct(a.shape[1:], F32))(a)


def _adam_update(w, g, m, v):
    nm = ADAM_B1 * m + (1.0 - ADAM_B1) * g
    nv = ADAM_B2 * v + (1.0 - ADAM_B2) * (g * g)
    m_hat = nm / (1.0 - ADAM_B1 ** ADAM_STEP)
    v_hat = nv / (1.0 - ADAM_B2 ** ADAM_STEP)
    return -ADAM_LR * (m_hat / (jnp.sqrt(v_hat) + ADAM_EPS) + ADAM_WD * w), nm, nv


def adamw(w, g, m, v, name, others=None):
    rows, cols = w.shape
    rb = rows
    for cand in range(min(rows, 512), 7, -8):
        if rows % cand == 0 and cand % 8 == 0:
            rb = cand
            break

    def body(*refs):
        if others is None:
            w_ref, g_ref, m_ref, v_ref, d_ref, nm_ref, nv_ref = refs
            gg = g_ref[...]
        else:
            w_ref, g_ref, m_ref, v_ref, r2_ref, go_ref, d_ref, nm_ref, nv_ref = refs
            gg = g_ref[...]
            for k in range(3):
                gg = gg + r2_ref[k].astype(F32)
            go_ref[...] = gg
        d_ref[...], nm_ref[...], nv_ref[...] = _adam_update(w_ref[...], gg, m_ref[...], v_ref[...])

    spec = pl.BlockSpec((rb, cols), lambda i: (i, 0))
    out = jax.ShapeDtypeStruct((rows, cols), F32)
    in_specs, args = [spec] * 4, [w, g, m, v]
    if others is not None:
        in_specs.append(pl.BlockSpec((3, rb, cols), lambda i: (0, i, 0)))
        args.append(others)
    n_out = 3 if others is None else 4
    res = pl.pallas_call(
        body, name=name, grid=(rows // rb,), in_specs=in_specs, out_specs=[spec] * n_out,
        out_shape=[out] * n_out, compiler_params=_params(("parallel",)),
    )(*args)
    return (g, *res) if others is None else tuple(res)


def adamw_small(items, name):
    n = len(items)

    def body(*refs):
        ins, outs = refs[:4 * n], refs[4 * n:]
        for i in range(n):
            w_ref, g_ref, m_ref, v_ref = ins[4 * i:4 * i + 4]
            d_ref, nm_ref, nv_ref = outs[3 * i:3 * i + 3]
            d_ref[...], nm_ref[...], nv_ref[...] = _adam_update(w_ref[...], g_ref[...], m_ref[...], v_ref[...])

    res = pl.pallas_call(
        body, name=name,
        out_shape=[jax.ShapeDtypeStruct(w.shape, F32) for w, _, _, _ in items for _ in range(3)],
    )(*[t for item in items for t in item])
    return [tuple(res[3 * i:3 * i + 3]) for i in range(n)]


def ada_fwd(c_all, w_cols, b_cols, name):
    def body(c_ref, w_ref, b_ref, cond_ref, mod_ref):
        cc = c_ref[...]
        cond = (cc * _sigmoid(cc)).astype(BF16)
        cond_ref[...] = cond
        mod_ref[...] = _dot(cond, w_ref[...].astype(BF16)) + b_ref[...]

    n, cols = c_all.shape[0], w_cols.shape[1]
    return pl.pallas_call(
        body, name=name,
        out_shape=[jax.ShapeDtypeStruct(c_all.shape, BF16), jax.ShapeDtypeStruct((n, cols), F32)],
        compiler_params=_params(),
    )(c_all, w_cols, b_cols)


def ada_bwd(cond_all, dmod_cols, name):
    def body(c_ref, d_ref, gw_ref, gb_ref):
        d = d_ref[...]
        gw_ref[...] = _dot_tn(c_ref[...], d.astype(BF16))
        gb_ref[...] = jnp.sum(d, axis=0, keepdims=True)

    dm, cols = cond_all.shape[1], dmod_cols.shape[1]
    return pl.pallas_call(
        body, name=name,
        out_shape=[jax.ShapeDtypeStruct((dm, cols), F32), jax.ShapeDtypeStruct((1, cols), F32)],
        compiler_params=_params(),
    )(cond_all, dmod_cols)


MXU_COLS = 256
FFN_CHUNK = 4 * MXU_COLS


def _hidden_chunks(ff):
    assert ff % MXU_COLS == 0
    return [(at, min(FFN_CHUNK, ff - at)) for at in range(0, ff, FFN_CHUNK)]


def _mod_spec(tiles_per_seq, dm):
    return pl.BlockSpec((1, 1, dm), lambda i: (i // tiles_per_seq, 0, 0))


def ffn_fwd(x, sh, sc, gt, wgu, wd, ln_g, ln_b, seq, name, target=None, job=None):
    tokens, dm = x.shape
    ff = wgu.shape[1]
    chunks = _hidden_chunks(ff)
    tm = min(FFN_FWD_TILE, seq)
    tiles_per_seq = seq // tm
    with_loss = target is not None

    def body(*refs):
        if with_loss:
            (x_ref, sh_ref, sc_ref, gt_ref, wgu_ref, wd_ref, lg_ref, lb_ref, t_ref,
             xo_ref, loss_ref, r_ref, gu_ref, f_ref) = refs
        else:
            (x_ref, sh_ref, sc_ref, gt_ref, wgu_ref, wd_ref, lg_ref, lb_ref,
             xo_ref, r_ref, gu_ref, f_ref) = refs
        xx = x_ref[...]
        h = (xx * (1.0 + sc_ref[0]) + sh_ref[0]).astype(BF16)
        acc = jnp.zeros((tm, dm), F32)
        for at, wdt in chunks:
            gk = _dot_nt(h, wgu_ref[0, at:at + wdt, :])
            uk = _dot_nt(h, wgu_ref[1, at:at + wdt, :])
            gu_ref[0, :, at:at + wdt] = gk.astype(BF16)
            gu_ref[1, :, at:at + wdt] = uk.astype(BF16)
            a = (gk * _sigmoid(gk) * uk).astype(BF16)
            acc = acc + _dot(a, wd_ref[at:at + wdt, :])
        f_ref[...] = acc.astype(BF16)
        r = DN_ALPHA * xx + (0.5 * (1.0 + gt_ref[0])) * acc
        r_ref[...] = r
        xhat, _ = _ln_stats(r)
        yy = xhat * lg_ref[...] + lb_ref[...]
        if with_loss:
            err = yy - t_ref[...]
            xo_ref[...] = err * (1.0 / dm)

            @pl.when(pl.program_id(0) == 0)
            def _():
                loss_ref[...] = jnp.zeros_like(loss_ref)

            loss_ref[...] += jnp.full((1, 128), (0.5 / dm) * jnp.sum(err * err), F32)
        else:
            xo_ref[...] = yy

    tile = pl.BlockSpec((tm, dm), lambda i: (i, 0))
    mod = _mod_spec(tiles_per_seq, dm)
    in_specs = [tile, mod, mod, mod, _const_spec(wgu.shape), _const_spec(wd.shape),
                _const_spec((1, dm)), _const_spec((1, dm))]
    args = [x, sh, sc, gt, wgu, wd, ln_g, ln_b]
    out_specs = [tile]
    out_shape = [jax.ShapeDtypeStruct((tokens, dm), F32)]
    if with_loss:
        in_specs.append(tile)
        args.append(target)
        out_specs.append(pl.BlockSpec((1, 128), lambda i: (0, 0)))
        out_shape.append(jax.ShapeDtypeStruct((1, 128), F32))
    out_specs += [tile, pl.BlockSpec((2, tm, ff), lambda i: (0, i, 0)), tile]
    out_shape += [jax.ShapeDtypeStruct((tokens, dm), F32), jax.ShapeDtypeStruct((2, tokens, ff), BF16),
                  jax.ShapeDtypeStruct((tokens, dm), BF16)]
    return _call(body, job, name=name, grid=(tokens // tm,), in_specs=in_specs, out_specs=out_specs,
                 out_shape=out_shape, args=args)


def ffn_up(x, sh, sc, wgu, seq, name, job=None):
    tokens, dm = x.shape
    ff = wgu.shape[1]
    chunks = _hidden_chunks(ff)
    tm = min(FFN_FWD_TILE, seq)

    def body(x_ref, sh_ref, sc_ref, wgu_ref, gu_ref, a_ref):
        h = (x_ref[...] * (1.0 + sc_ref[0]) + sh_ref[0]).astype(BF16)
        for at, wdt in chunks:
            gk = _dot_nt(h, wgu_ref[0, at:at + wdt, :])
            uk = _dot_nt(h, wgu_ref[1, at:at + wdt, :])
            gu_ref[0, :, at:at + wdt] = gk.astype(BF16)
            gu_ref[1, :, at:at + wdt] = uk.astype(BF16)
            a_ref[:, at:at + wdt] = (gk * _sigmoid(gk) * uk).astype(BF16)

    tile = pl.BlockSpec((tm, dm), lambda i: (i, 0))
    mod = _mod_spec(seq // tm, dm)
    return _call(
        body, job, name=name, grid=(tokens // tm,),
        in_specs=[tile, mod, mod, _const_spec(wgu.shape)],
        out_specs=[pl.BlockSpec((2, tm, ff), lambda i: (0, i, 0)), pl.BlockSpec((tm, ff), lambda i: (i, 0))],
        out_shape=[jax.ShapeDtypeStruct((2, tokens, ff), BF16), jax.ShapeDtypeStruct((tokens, ff), BF16)],
        args=(x, sh, sc, wgu))


def ffn_down(x, a, gt, wd, ln_g, ln_b, seq, name, job=None):
    tokens, dm = x.shape
    ff = wd.shape[0]
    chunks = _hidden_chunks(ff)
    tm = min(FFN_FWD_TILE, seq)

    def body(x_ref, a_ref, gt_ref, wd_ref, lg_ref, lb_ref, xo_ref, r_ref, f_ref):
        acc = jnp.zeros((tm, dm), F32)
        for at, wdt in chunks:
            acc = acc + _dot(a_ref[:, at:at + wdt], wd_ref[at:at + wdt, :])
        f_ref[...] = acc.astype(BF16)
        r = DN_ALPHA * x_ref[...] + (0.5 * (1.0 + gt_ref[0])) * acc
        r_ref[...] = r
        xhat, _ = _ln_stats(r)
        xo_ref[...] = xhat * lg_ref[...] + lb_ref[...]

    tile = pl.BlockSpec((tm, dm), lambda i: (i, 0))
    return _call(
        body, job, name=name, grid=(tokens // tm,),
        in_specs=[tile, pl.BlockSpec((tm, ff), lambda i: (i, 0)), _mod_spec(seq // tm, dm), _const_spec(wd.shape),
                  _const_spec((1, dm)), _const_spec((1, dm))],
        out_specs=[tile, tile, tile],
        out_shape=[jax.ShapeDtypeStruct((tokens, dm), F32), jax.ShapeDtypeStruct((tokens, dm), F32),
                   jax.ShapeDtypeStruct((tokens, dm), BF16)],
        args=(x, a, gt, wd, ln_g, ln_b))


def ffn_bwd(dy, r, x, f, gu, sh, sc, gt, wgu, wd, ln_g, seq, name, job=None):
    tokens, dm = x.shape
    ff = wgu.shape[1]
    chunks = _hidden_chunks(ff)
    tm = min(TOKEN_TILE, seq)
    tiles_per_seq = seq // tm
    nseq = tokens // seq

    def body(dy_ref, r_ref, x_ref, f_ref, gu_ref, sh_ref, sc_ref, gt_ref, wgu_ref, wd_ref, lg_ref,
             dx_ref, dgu_ref, df_ref, a_ref, h_ref, dln_ref, dmod_ref):
        i = pl.program_id(0)
        dr, dgain, dbias = _ln_bwd(dy_ref[...], r_ref[...], lg_ref[...])

        @pl.when(i == 0)
        def _():
            dln_ref[...] = jnp.zeros_like(dln_ref)

        @pl.when(i % tiles_per_seq == 0)
        def _():
            dmod_ref[...] = jnp.zeros_like(dmod_ref)

        dln_ref[0:1, :] += dgain
        dln_ref[1:2, :] += dbias
        df32 = (0.5 * (1.0 + gt_ref[0])) * dr
        df = df32.astype(BF16)
        df_ref[...] = df
        dgate = jnp.sum(dr * (0.5 * f_ref[...].astype(F32)), axis=0, keepdims=True)
        xx = x_ref[...]
        one_sc = 1.0 + sc_ref[0]
        h = (xx * one_sc + sh_ref[0]).astype(BF16)
        h_ref[...] = h
        dh = jnp.zeros((tm, dm), F32)
        for at, wdt in chunks:
            cols = slice(at, at + wdt)
            da = _dot_nt(df, wd_ref[cols, :])
            gk = gu_ref[0, :, cols].astype(F32)
            uk = gu_ref[1, :, cols].astype(F32)
            sg = _sigmoid(gk)
            sil = gk * sg
            a_ref[:, cols] = (sil * uk).astype(BF16)
            du = (da * sil).astype(BF16)
            dg = (da * uk * (sg * (1.0 + gk * (1.0 - sg)))).astype(BF16)
            dgu_ref[0, :, cols] = dg
            dgu_ref[1, :, cols] = du
            dh = dh + _dot(dg, wgu_ref[0, cols, :]) + _dot(du, wgu_ref[1, cols, :])
        dx_ref[...] = DN_ALPHA * dr + dh * one_sc
        dmod_ref[0, 0:1, :] += jnp.sum(dh, axis=0, keepdims=True)
        dmod_ref[0, 1:2, :] += jnp.sum(dh * xx, axis=0, keepdims=True)
        dmod_ref[0, 2:3, :] += dgate

    tile = pl.BlockSpec((tm, dm), lambda i: (i, 0))
    mod = _mod_spec(tiles_per_seq, dm)
    gu_spec = pl.BlockSpec((2, tm, ff), lambda i: (0, i, 0))
    return _call(
        body, job, name=name, grid=(tokens // tm,),
        in_specs=[tile, tile, tile, tile, gu_spec, mod, mod, mod, _const_spec(wgu.shape), _const_spec(wd.shape),
                  _const_spec((1, dm))],
        out_specs=[tile, gu_spec, tile, pl.BlockSpec((tm, ff), lambda i: (i, 0)), tile,
                   pl.BlockSpec((2, dm), lambda i: (0, 0)),
                   pl.BlockSpec((1, 3, dm), lambda i: (i // tiles_per_seq, 0, 0))],
        out_shape=[jax.ShapeDtypeStruct((tokens, dm), F32), jax.ShapeDtypeStruct((2, tokens, ff), BF16),
                   jax.ShapeDtypeStruct((tokens, dm), BF16), jax.ShapeDtypeStruct((tokens, ff), BF16),
                   jax.ShapeDtypeStruct((tokens, dm), BF16), jax.ShapeDtypeStruct((2, dm), F32),
                   jax.ShapeDtypeStruct((nseq, 3, dm), F32)],
        args=(dy, r, x, f, gu, sh, sc, gt, wgu, wd, ln_g))


def tn_matmul(a, b, name, job=None, b_cols=None, a_width=None):
    na, tokens, k_all = a.shape
    kk = k_all if a_width is None else a_width
    nka = k_all // kk
    assert nka * kk == k_all
    nb, _, cc = b.shape
    col = 0
    if b_cols is not None:
        col, cc = b_cols
    tt = tokens
    while 4 * tt * (kk + cc) + 8 * kk * cc > TN_VMEM_BUDGET and tt % 2 == 0 and tt > 256:
        tt //= 2
    steps = tokens // tt

    def body(a_ref, b_ref, o_ref, *acc):
        if steps == 1:
            o_ref[0, 0, 0] = _dot_tn(a_ref[0], b_ref[0]).astype(BF16)
            return
        acc_ref, = acc
        t = pl.program_id(3)

        @pl.when(t == 0)
        def _():
            acc_ref[...] = jnp.zeros_like(acc_ref)

        acc_ref[...] += _dot_tn(a_ref[0], b_ref[0])

        @pl.when(t == steps - 1)
        def _():
            o_ref[0, 0, 0] = acc_ref[...].astype(BF16)

    return _call(
        body, job, name=name, grid=(na, nka, nb, steps),
        in_specs=[pl.BlockSpec((1, tt, kk), lambda i, s, j, t: (i, t, s)),
                  pl.BlockSpec((1, tt, cc), lambda i, s, j, t: (j, t, col))],
        out_specs=[pl.BlockSpec((1, 1, 1, kk, cc), lambda i, s, j, t: (i, s, j, 0, 0))],
        out_shape=[jax.ShapeDtypeStruct((na, nka, nb, kk, cc), BF16)],
        scratch_shapes=[] if steps == 1 else [pltpu.VMEM((kk, cc), F32)], args=(a, b))


def proj_fwd(x1, sh, sc, w_in, seq, name, job=None):
    tokens, dm = x1.shape
    tm = min(MIX_TILE, seq)
    tiles_per_seq = seq // tm
    widths = [N_Q_HEADS * HEAD_DIM, N_KV_HEADS * HEAD_DIM, N_KV_HEADS * HEAD_DIM, 512, 512, 512]
    assert sum(widths) == w_in.shape[0]

    def body(x_ref, sh_ref, sc_ref, w_ref, *outs):
        h = (x_ref[...] * (1.0 + sc_ref[0]) + sh_ref[0]).astype(BF16)
        proj = _dot_nt(h, w_ref[...])
        at = 0
        for o_ref, wdt in zip(outs, widths):
            o_ref[...] = proj[:, at:at + wdt].astype(BF16)
            at += wdt

    tile = pl.BlockSpec((tm, dm), lambda i: (i, 0))
    mod = _mod_spec(tiles_per_seq, dm)
    return _call(
        body, job, name=name, grid=(tokens // tm,),
        in_specs=[tile, mod, mod, _const_spec(w_in.shape)],
        out_specs=[pl.BlockSpec((tm, wdt), lambda i: (i, 0)) for wdt in widths],
        out_shape=[jax.ShapeDtypeStruct((tokens, wdt), BF16) for wdt in widths],
        args=(x1, sh, sc, w_in))


LANES = 2 * HEAD_DIM


def _head_lane(shape):
    return lax.broadcasted_iota(jnp.int32, shape, 1) % HEAD_DIM


def _lane_half(shape):
    return lax.broadcasted_iota(jnp.int32, shape, 1) // HEAD_DIM


def _swap_rot(v):
    lane = _head_lane(v.shape)
    half = ROT_DIM // 2
    return jnp.where(lane < half, pltpu.roll(v, LANES - half, 1),
                     jnp.where(lane < ROT_DIM, pltpu.roll(v, half, 1), 0.0))


def _rope(v, cos_t, sin_t):
    return v * cos_t + _swap_rot(v) * sin_t


def _unrope(dv, cos_t, sin_t):
    return dv * cos_t + _swap_rot(dv * sin_t)


def _both_halves(t, g):
    return jnp.where(_lane_half(t.shape) == g, t, pltpu.roll(t, HEAD_DIM, 1))


def _fold_halves(t, g):
    return jnp.where(_lane_half(t.shape) == g, t + pltpu.roll(t, HEAD_DIM, 1), 0.0)


def _stack_heads(blocks):
    rows = []
    for blk in blocks:
        half = _lane_half(blk.shape)
        rows += [jnp.where(half == 0, blk, 0.0), jnp.where(half == 1, blk, 0.0)]
    return jnp.concatenate(rows, axis=0)


def _unstack_heads(t, j):
    lo = t[(2 * j) * ATTN_BLOCK:(2 * j + 1) * ATTN_BLOCK]
    hi = t[(2 * j + 1) * ATTN_BLOCK:(2 * j + 2) * ATTN_BLOCK]
    return jnp.where(_lane_half(lo.shape) == 0, lo, hi)


def _band_mask(q0, w0):
    rows, cols = GQA_GROUP * ATTN_BLOCK, 2 * ATTN_BLOCK
    qi = lax.broadcasted_iota(jnp.int32, (rows, cols), 0) % ATTN_BLOCK + q0
    ki = lax.broadcasted_iota(jnp.int32, (rows, cols), 1) + w0
    diff = qi - ki
    return (diff >= 0) & (diff < ATTN_BLOCK)


def _attn_specs(seq):
    q_spec = pl.BlockSpec((seq, GQA_GROUP * HEAD_DIM), lambda b, g: (b, g))
    kv_spec = pl.BlockSpec((seq, LANES), lambda b, g: (b, 0))
    sink_spec = pl.BlockSpec((1, GQA_GROUP * ATTN_BLOCK, 1), lambda b, g: (g, 0, 0))
    return q_spec, kv_spec, sink_spec


def _block_starts(n):
    q0 = pl.multiple_of(n * ATTN_BLOCK, ATTN_BLOCK)
    w0 = pl.multiple_of(jnp.maximum(n - 1, 0) * ATTN_BLOCK, ATTN_BLOCK)
    return q0, w0


def _stacked_queries(ref, rows):
    return _stack_heads([ref[rows, j * LANES:(j + 1) * LANES] for j in range(2)]).astype(BF16)


def _sink_columns(sinks):
    return jnp.repeat(sinks.reshape(N_KV_HEADS, GQA_GROUP), ATTN_BLOCK, axis=1)[:, :, None]


def _probs_spec(nblk):
    return pl.BlockSpec((1, 1, nblk, GQA_GROUP * ATTN_BLOCK, 2 * ATTN_BLOCK), lambda b, g: (b, g, 0, 0, 0))


def _sink_probs_spec():
    return pl.BlockSpec((1, 1, GQA_GROUP * ATTN_BLOCK, LANES), lambda b, g: (b, g, 0, 0))


def attn_fwd(q, k, v, cos_t, sin_t, sinks, seq, name, job=None):
    tokens = q.shape[0]
    nblk = seq // ATTN_BLOCK
    assert nblk >= 2
    scale = HEAD_DIM ** -0.5

    nseq = tokens // seq
    rows_stacked = GQA_GROUP * ATTN_BLOCK
    assert nblk <= LANES

    def body(q_ref, k_ref, v_ref, cos_ref, sin_ref, sink_ref, o_ref, qr_ref, p_ref, ps_ref, kd_ref, vd_ref):
        g = pl.program_id(1)
        kd_ref[...] = _both_halves(_rope(k_ref[...].astype(F32), cos_ref[...], sin_ref[...]), g).astype(BF16)
        vd_ref[...] = _both_halves(v_ref[...].astype(F32), g).astype(BF16)
        sink = sink_ref[0]
        lane = lax.broadcasted_iota(jnp.int32, (rows_stacked, LANES), 1)

        ps_ref[...] = jnp.zeros_like(ps_ref)

        def block(n, carry):
            q0, w0 = _block_starts(n)
            rows, win = pl.ds(q0, ATTN_BLOCK), pl.ds(w0, 2 * ATTN_BLOCK)
            blocks = []
            for j in range(2):
                qr = _rope(q_ref[rows, j * LANES:(j + 1) * LANES].astype(F32), cos_ref[rows, :], sin_ref[rows, :]).astype(BF16)
                qr_ref[rows, j * LANES:(j + 1) * LANES] = qr
                blocks.append(qr)
            qs = _stack_heads(blocks)
            s = _dot_nt(qs, kd_ref[win, :]) * scale
            s = jnp.where(_band_mask(q0, w0), s, NEG_BIG)
            m = jnp.maximum(jnp.max(s, axis=-1, keepdims=True), sink)
            p = jnp.exp(s - m)
            e_sink = jnp.exp(sink - m)
            inv = pl.reciprocal(jnp.sum(p, axis=-1, keepdims=True) + e_sink, approx=True)
            pn = (p * inv).astype(BF16)
            p_ref[0, 0, n] = pn
            out = _dot(pn, vd_ref[win, :])
            for j in range(2):
                o_ref[rows, j * LANES:(j + 1) * LANES] = _unstack_heads(out, j).astype(o_ref.dtype)
            ps_ref[0, 0] = jnp.where(lane == n, e_sink * inv, ps_ref[0, 0])
            return carry

        lax.fori_loop(0, nblk, block, 0, unroll=2)

    q_spec, kv_spec, sink_spec = _attn_specs(seq)
    return _call(
        body, job, name=name, grid=(nseq, N_KV_HEADS),
        in_specs=[q_spec, kv_spec, kv_spec, kv_spec, kv_spec, sink_spec],
        out_specs=[q_spec, q_spec, _probs_spec(nblk), _sink_probs_spec()],
        out_shape=[jax.ShapeDtypeStruct(q.shape, BF16), jax.ShapeDtypeStruct(q.shape, BF16),
                   jax.ShapeDtypeStruct((nseq, N_KV_HEADS, nblk, rows_stacked, 2 * ATTN_BLOCK), BF16),
                   jax.ShapeDtypeStruct((nseq, N_KV_HEADS, rows_stacked, LANES), F32)],
        scratch_shapes=[pltpu.VMEM((seq, LANES), BF16), pltpu.VMEM((seq, LANES), BF16)],
        args=(q, k, v, cos_t, sin_t, _sink_columns(sinks)))


def attn_bwd(qr, k, v, do, probs, sink_probs, cos_t, sin_t, seq, name, job=None):
    tokens = qr.shape[0]
    nseq = tokens // seq
    nblk = seq // ATTN_BLOCK
    assert nblk >= 2
    rows_stacked = GQA_GROUP * ATTN_BLOCK
    scale = HEAD_DIM ** -0.5

    def body(q_ref, k_ref, v_ref, do_ref, p_ref, ps_ref, cos_ref, sin_ref, dq_ref, dk_ref, dv_ref, ds_ref,
             kd_ref, vd_ref, dkd_ref, dvd_ref, acc_ref):
        g = pl.program_id(1)
        kd_ref[...] = _both_halves(_rope(k_ref[...].astype(F32), cos_ref[...], sin_ref[...]), g).astype(BF16)
        vd_ref[...] = _both_halves(v_ref[...].astype(F32), g).astype(BF16)
        dkd_ref[...] = jnp.zeros_like(dkd_ref)
        dvd_ref[...] = jnp.zeros_like(dvd_ref)
        acc_ref[...] = jnp.zeros_like(acc_ref)
        lane = lax.broadcasted_iota(jnp.int32, (rows_stacked, LANES), 1)

        def block(n, carry):
            q0, w0 = _block_starts(n)
            rows, win = pl.ds(q0, ATTN_BLOCK), pl.ds(w0, 2 * ATTN_BLOCK)
            qs = _stacked_queries(q_ref, rows)
            dos = _stacked_queries(do_ref, rows)
            kw, vw = kd_ref[win, :], vd_ref[win, :]
            pn16 = p_ref[0, 0, n]
            pn = pn16.astype(F32)
            dvd_ref[win, :] += _dot_tn(pn16, dos)
            dp = _dot_nt(dos, vw)
            delta = jnp.sum(dp * pn, axis=-1, keepdims=True)
            ds = (pn * (dp - delta)).astype(BF16)
            dqs = _dot(ds, kw) * scale
            dkd_ref[win, :] += _dot_tn(ds, qs) * scale
            cos_b, sin_b = cos_ref[rows, :], sin_ref[rows, :]
            for j in range(2):
                dq_ref[rows, j * LANES:(j + 1) * LANES] = _unrope(_unstack_heads(dqs, j), cos_b, sin_b).astype(BF16)
            acc_ref[...] += jnp.where(lane == n, ps_ref[0, 0] * delta, 0.0)
            return carry

        lax.fori_loop(0, nblk // 2, lambda i, carry: block(2 * i + 1, block(2 * i, carry)), 0)
        ds_ref[0, 0] = -jnp.sum(acc_ref[...], axis=-1, keepdims=True)
        dk_g = _unrope(_fold_halves(dkd_ref[...], g), cos_ref[...], sin_ref[...])
        dv_g = _fold_halves(dvd_ref[...], g)

        @pl.when(g == 0)
        def _():
            dk_ref[...] = dk_g
            dv_ref[...] = dv_g

        @pl.when(g != 0)
        def _():
            dk_ref[...] += dk_g
            dv_ref[...] += dv_g

    q_spec, kv_spec, _ = _attn_specs(seq)
    return _call(
        body, job, name=name, grid=(nseq, N_KV_HEADS),
        in_specs=[q_spec, kv_spec, kv_spec, q_spec, _probs_spec(nblk), _sink_probs_spec(), kv_spec, kv_spec],
        out_specs=[q_spec, kv_spec, kv_spec, pl.BlockSpec((1, 1, rows_stacked, 1), lambda b, g: (b, g, 0, 0))],
        out_shape=[jax.ShapeDtypeStruct(qr.shape, BF16), jax.ShapeDtypeStruct(k.shape, F32),
                   jax.ShapeDtypeStruct(k.shape, F32), jax.ShapeDtypeStruct((nseq, N_KV_HEADS, rows_stacked, 1), F32)],
        scratch_shapes=[pltpu.VMEM((seq, LANES), BF16), pltpu.VMEM((seq, LANES), BF16),
                        pltpu.VMEM((seq, LANES), F32), pltpu.VMEM((seq, LANES), F32),
                        pltpu.VMEM((rows_stacked, LANES), F32)],
        args=(qr, k, v, do, probs, sink_probs, cos_t, sin_t))


CONV_COLS = 128


def _shift_down(z, by):
    t = lax.broadcasted_iota(jnp.int32, z.shape, 0)
    return jnp.where(t >= by, pltpu.roll(z, by, 0), 0.0)


def _shift_up(z, by):
    n = z.shape[0]
    t = lax.broadcasted_iota(jnp.int32, z.shape, 0)
    return jnp.where(t < n - by, pltpu.roll(z, n - by, 0), 0.0)


def conv_fwd(u, bg, cg, conv_w, seq, name):
    tokens, width = u.shape

    def body(u_ref, bg_ref, cg_ref, w_ref, o_ref):
        z = cg_ref[...].astype(F32) * u_ref[...].astype(F32)
        yy = w_ref[2:3, :] * z + w_ref[1:2, :] * _shift_down(z, 1) + w_ref[0:1, :] * _shift_down(z, 2)
        o_ref[...] = (bg_ref[...].astype(F32) * yy).astype(BF16)

    col = pl.BlockSpec((seq, CONV_COLS), lambda j, b: (b, j))
    return pl.pallas_call(
        body, name=name, grid=(width // CONV_COLS, tokens // seq),
        in_specs=[col, col, col, pl.BlockSpec((CONV_TAPS, CONV_COLS), lambda j, b: (0, j))],
        out_specs=col, out_shape=jax.ShapeDtypeStruct((tokens, width), BF16),
        compiler_params=_params(("parallel", "parallel")),
    )(u, bg, cg, conv_w)


def conv_bwd(dout, u, bg, cg, conv_w, seq, name):
    tokens, width = u.shape

    def body(do_ref, u_ref, bg_ref, cg_ref, w_ref, du_ref, dbg_ref, dcg_ref, dw_ref):
        uu, cg_v, do = u_ref[...].astype(F32), cg_ref[...].astype(F32), do_ref[...].astype(F32)
        z = cg_v * uu
        z1, z2 = _shift_down(z, 1), _shift_down(z, 2)
        yy = w_ref[2:3, :] * z + w_ref[1:2, :] * z1 + w_ref[0:1, :] * z2
        dbg_ref[...] = (do * yy).astype(BF16)
        dyy = do * bg_ref[...].astype(F32)
        dz = w_ref[2:3, :] * dyy + w_ref[1:2, :] * _shift_up(dyy, 1) + w_ref[0:1, :] * _shift_up(dyy, 2)
        du_ref[...] = (dz * cg_v).astype(BF16)
        dcg_ref[...] = (dz * uu).astype(BF16)

        @pl.when(pl.program_id(1) == 0)
        def _():
            dw_ref[...] = jnp.zeros_like(dw_ref)

        dw_ref[0:1, :] += jnp.sum(dyy * z2, axis=0, keepdims=True)
        dw_ref[1:2, :] += jnp.sum(dyy * z1, axis=0, keepdims=True)
        dw_ref[2:3, :] += jnp.sum(dyy * z, axis=0, keepdims=True)

    col = pl.BlockSpec((seq, CONV_COLS), lambda j, b: (b, j))
    w_spec = pl.BlockSpec((CONV_TAPS, CONV_COLS), lambda j, b: (0, j))
    act = jax.ShapeDtypeStruct((tokens, width), BF16)
    return pl.pallas_call(
        body, name=name, grid=(width // CONV_COLS, tokens // seq),
        in_specs=[col, col, col, col, w_spec], out_specs=[col, col, col, w_spec],
        out_shape=[act, act, act, jax.ShapeDtypeStruct((CONV_TAPS, width), F32)],
        compiler_params=_params(("parallel", "arbitrary")),
    )(dout, u, bg, cg, conv_w)


def out_fwd(x1, attn, conv, gt, w_out, ln_g, ln_b, seq, name, job=None):
    tokens, dm = x1.shape
    half = attn.shape[1]
    tm = min(MIX_TILE, seq)
    tiles_per_seq = seq // tm

    def body(x_ref, a_ref, c_ref, gt_ref, w_ref, lg_ref, lb_ref, xo_ref, r_ref, mi_ref, mix_ref):
        mixin = jnp.concatenate([a_ref[...], c_ref[...]], axis=1).astype(BF16)
        mi_ref[...] = mixin
        mix = _dot(mixin, w_ref[...])
        mix_ref[...] = mix.astype(BF16)
        r = DN_ALPHA * x_ref[...] + (1.0 + gt_ref[0]) * mix
        r_ref[...] = r
        xhat, _ = _ln_stats(r)
        xo_ref[...] = xhat * lg_ref[...] + lb_ref[...]

    tile = pl.BlockSpec((tm, dm), lambda i: (i, 0))
    htile = pl.BlockSpec((tm, half), lambda i: (i, 0))
    return _call(
        body, job, name=name, grid=(tokens // tm,),
        in_specs=[tile, htile, htile, _mod_spec(tiles_per_seq, dm), _const_spec(w_out.shape),
                  _const_spec((1, dm)), _const_spec((1, dm))],
        out_specs=[tile, tile, tile, tile],
        out_shape=[jax.ShapeDtypeStruct((tokens, dm), F32), jax.ShapeDtypeStruct((tokens, dm), F32),
                   jax.ShapeDtypeStruct((tokens, dm), BF16), jax.ShapeDtypeStruct((tokens, dm), BF16)],
        args=(x1, attn, conv, gt, w_out, ln_g, ln_b))


def out_bwd(dy, r, mix, gt, w_out, ln_g, seq, name, job=None):
    tokens, dm = r.shape
    half = dm // 2
    tm = min(MIX_TILE, seq)
    tiles_per_seq = seq // tm
    nseq = tokens // seq

    def body(dy_ref, r_ref, mix_ref, gt_ref, w_ref, lg_ref, dres_ref, da_ref, dc_ref, dmix_ref, dln_ref, dgt_ref):
        i = pl.program_id(0)
        dr, dgain, dbias = _ln_bwd(dy_ref[...], r_ref[...], lg_ref[...])

        @pl.when(i == 0)
        def _():
            dln_ref[...] = jnp.zeros_like(dln_ref)

        @pl.when(i % tiles_per_seq == 0)
        def _():
            dgt_ref[...] = jnp.zeros_like(dgt_ref)

        dln_ref[0:1, :] += dgain
        dln_ref[1:2, :] += dbias
        dgt_ref[0] += jnp.sum(dr * mix_ref[...].astype(F32), axis=0, keepdims=True)
        dres_ref[...] = DN_ALPHA * dr
        dmix = ((1.0 + gt_ref[0]) * dr).astype(BF16)
        dmix_ref[...] = dmix
        dmixin = _dot_nt(dmix, w_ref[...])
        da_ref[...] = dmixin[:, :half].astype(BF16)
        dc_ref[...] = dmixin[:, half:].astype(BF16)

    tile = pl.BlockSpec((tm, dm), lambda i: (i, 0))
    htile = pl.BlockSpec((tm, half), lambda i: (i, 0))
    return _call(
        body, job, name=name, grid=(tokens // tm,),
        in_specs=[tile, tile, tile, _mod_spec(tiles_per_seq, dm), _const_spec(w_out.shape), _const_spec((1, dm))],
        out_specs=[tile, htile, htile, tile, pl.BlockSpec((2, dm), lambda i: (0, 0)),
                   pl.BlockSpec((1, 1, dm), lambda i: (i // tiles_per_seq, 0, 0))],
        out_shape=[jax.ShapeDtypeStruct((tokens, dm), F32), jax.ShapeDtypeStruct((tokens, half), BF16),
                   jax.ShapeDtypeStruct((tokens, half), BF16), jax.ShapeDtypeStruct((tokens, dm), BF16),
                   jax.ShapeDtypeStruct((2, dm), F32), jax.ShapeDtypeStruct((nseq, 1, dm), F32)],
        args=(dy, r, mix, gt, w_out, ln_g))


def proj_bwd(parts, dres, x1, sh, sc, w_in, seq, name, job=None):
    tokens, dm = x1.shape
    tm = min(MIX_TILE, seq)
    tiles_per_seq = seq // tm
    nseq = tokens // seq
    widths = [p.shape[1] for p in parts]
    total = sum(widths)

    def body(*refs):
        part_refs = refs[:6]
        dres_ref, x_ref, sh_ref, sc_ref, w_ref, dx_ref, dproj_ref, h_ref, dmod_ref = refs[6:]
        dproj = jnp.concatenate([p[...].astype(BF16) for p in part_refs], axis=1)
        dproj_ref[...] = dproj
        dh = _dot(dproj, w_ref[...])
        xx = x_ref[...]
        one_sc = 1.0 + sc_ref[0]
        h_ref[...] = (xx * one_sc + sh_ref[0]).astype(BF16)
        dx_ref[...] = dres_ref[...] + dh * one_sc

        @pl.when(pl.program_id(0) % tiles_per_seq == 0)
        def _():
            dmod_ref[...] = jnp.zeros_like(dmod_ref)

        dmod_ref[0, 0:1, :] += jnp.sum(dh, axis=0, keepdims=True)
        dmod_ref[0, 1:2, :] += jnp.sum(dh * xx, axis=0, keepdims=True)

    tile = pl.BlockSpec((tm, dm), lambda i: (i, 0))
    mod = _mod_spec(tiles_per_seq, dm)
    return _call(
        body, job, name=name, grid=(tokens // tm,),
        in_specs=[pl.BlockSpec((tm, wdt), lambda i: (i, 0)) for wdt in widths]
        + [tile, tile, mod, mod, _const_spec(w_in.shape)],
        out_specs=[tile, pl.BlockSpec((tm, total), lambda i: (i, 0)), tile,
                   pl.BlockSpec((1, 2, dm), lambda i: (i // tiles_per_seq, 0, 0))],
        out_shape=[jax.ShapeDtypeStruct((tokens, dm), F32), jax.ShapeDtypeStruct((tokens, total), BF16),
                   jax.ShapeDtypeStruct((tokens, dm), BF16), jax.ShapeDtypeStruct((nseq, 2, dm), F32)],
        args=(*parts, dres, x1, sh, sc, w_in))


def _rope_tables(positions):
    half = ROT_DIM // 2
    inv_freq = jnp.power(jnp.float32(ROPE_THETA), -jnp.arange(0, ROT_DIM, 2, dtype=F32) / ROT_DIM)
    lane = jnp.arange(LANES) % HEAD_DIM
    freq = jnp.where(lane < ROT_DIM, inv_freq[lane % half], 0.0)
    sign = jnp.where(lane < half, -1.0, 1.0).astype(F32)
    ang = positions.astype(F32)[:, None] * freq[None, :]
    return jnp.cos(ang), sign[None, :] * jnp.sin(ang)


def kernel(x, c, positions, w_ada, b_ada, ffn1_w_gate_up, ffn1_w_down, ln1_g, ln1_b, w_in, conv_w, attn_sinks, w_out, ln2_g, ln2_b, ffn2_w_gate_up, ffn2_w_down, ln3_g, ln3_b, loss_target, m_w_ada, m_b_ada, m_ffn1_w_gate_up, m_ffn1_w_down, m_ln1_g, m_ln1_b, m_w_in, m_conv_w, m_attn_sinks, m_w_out, m_ln2_g, m_ln2_b, m_ffn2_w_gate_up, m_ffn2_w_down, m_ln3_g, m_ln3_b, v_w_ada, v_b_ada, v_ffn1_w_gate_up, v_ffn1_w_down, v_ln1_g, v_ln1_b, v_w_in, v_conv_w, v_attn_sinks, v_w_out, v_ln2_g, v_ln2_b, v_ffn2_w_gate_up, v_ffn2_w_down, v_ln3_g, v_ln3_b):
    nseq, seq, dm = x.shape
    tokens = nseq * seq
    dev = 4 * lax.axis_index("x") + 2 * lax.axis_index("y") + lax.axis_index("c")
    core = lax.axis_index("c").astype(jnp.int32).reshape(1)
    ada_cols = w_ada.shape[2]
    ff = ffn1_w_down.shape[1] * N_DEV
    fc = ff // 4
    in_cols = w_in.shape[2]
    conv_cols = conv_w.shape[2]

    def t_bf16(w):
        return w[0].T.astype(BF16)

    c_all, convw_all = all_gather([c, conv_w[0]], "gather_cond")
    c_all = c_all.reshape(N_DEV * nseq, dm)
    convw_full = convw_all.transpose(1, 0, 2).reshape(CONV_TAPS, N_DEV * conv_cols)

    b_cols = lax.dynamic_slice(b_ada, (0, dev * ada_cols), (1, ada_cols))
    cond_all, mod_cols = ada_fwd(c_all, w_ada[0], b_cols, "ada_fwd")
    wgu1, mod_all = all_gather([t_bf16(ffn1_w_gate_up), mod_cols], "gather_ffn1")
    wgu1 = wgu1.reshape(2, ff, dm)
    mod = lax.dynamic_slice(mod_all, (0, dev * nseq, 0), (N_DEV, nseq, ada_cols))
    mod = mod.transpose(1, 0, 2).reshape(nseq, 9, 1, dm)
    sh1, sc1, g1, sh2, sc2, g2, sh3, sc3, g3 = [mod[:, i] for i in range(9)]

    x0 = x.reshape(tokens, dm)
    (gu1, a1_fwd), (wd1, wout) = ffn_up(x0, sh1, sc1, wgu1, seq, "ffn1_up",
                                        job=_GatherJob([ffn1_w_down[0].astype(BF16), w_out[0].astype(BF16)]))
    wd1, wout = wd1.reshape(ff, dm), wout.reshape(dm, dm)
    (x1, r1, f1), (win,) = ffn_down(x0, a1_fwd, g1, wd1, ln1_g, ln1_b, seq, "ffn1_down", job=_GatherJob([t_bf16(w_in)]))
    win = win.reshape(N_DEV * in_cols, dm)
    (q, k, v, u, bg, cg), wd2_spread = proj_fwd(x1, sh2, sc2, win, seq, "proj_fwd",
                                                job=gather_spread_job([ffn2_w_down[0].astype(BF16)]))
    cos_t, sin_t = _rope_tables(positions.reshape(tokens))
    sinks = attn_sinks[0]
    (attn, q_rot, probs, sink_probs), wgu2_spread = attn_fwd(q, k, v, cos_t, sin_t, sinks, seq, "attn_fwd",
                                                             job=gather_spread_job([t_bf16(ffn2_w_gate_up)]))
    conv = conv_fwd(u, bg, cg, convw_full, seq, "conv_fwd")
    (x2, r2, mixin, mix), (wd2, wgu2) = out_fwd(x1, attn, conv, g2, wout, ln2_g, ln2_b, seq, "out_fwd",
                                                job=gather_forward_job(wd2_spread + wgu2_spread))
    wd2, wgu2 = wd2.reshape(ff, dm), wgu2.reshape(2, ff, dm)
    target = loss_target.reshape(tokens, dm)
    (dy3, loss_part, r3, gu3, f3), _ = ffn_fwd(x2, sh3, sc3, g3, wgu2, wd2, ln3_g, ln3_b, seq, "ffn2_fwd", target=target)

    (dx2, dgu3, df3, a3, h3, dln3, dmod3), _ = ffn_bwd(dy3, r3, x2, f3, gu3, sh3, sc3, g3, wgu2, wd2, ln3_g, seq, "ffn2_bwd")
    pair = 2 * fc
    g_wd2 = tn_matmul(a3[None], df3[None], "ffn2_dwd", a_width=pair)[0][0].reshape(N_DEV, ff // N_DEV, dm)
    g_wgu2 = tn_matmul(dgu3, h3[None], "ffn2_dwgu", a_width=pair)[0][0].reshape(N_DEV, fc, dm)
    (dres2, dattn, dconv, dmix, dln2, dg2), swapped = out_bwd(dx2, r2, mix, g2, wout, ln2_g, seq, "out_bwd",
                                                              job=swap_job([g_wgu2, g_wd2]))
    p_wgu2, own_wgu2 = pair_sum(core, g_wgu2, swapped[0], "pair_wgu2")
    p_wd2, own_wd2 = pair_sum(core, g_wd2, swapped[1], "pair_wd2")
    du, dbg, dcg, dconvw = conv_bwd(dconv, u, bg, cg, convw_full, seq, "conv_bwd")
    (dq, dk, dv, dsink_rows), (far_wd2,) = attn_bwd(
        q_rot, k, v, dattn, probs, sink_probs, cos_t, sin_t, seq, "attn_bwd", job=chip_exchange_job([p_wd2]))
    parts = [dq, dk, dv, du, dbg, dcg]
    (dx1, dproj, h2, dmod2), far_top = proj_bwd(parts, dres2, x1, sh2, sc2, win, seq, "proj_bwd",
                                                job=chip_exchange_job([p_wgu2], rows=(0, fc // 2)))
    (dx0, dgu1, df1, a1, h1, dln1, dmod1), _ = ffn_bwd(
        dx1, r1, x0, f1, gu1, sh1, sc1, g1, wgu1, wd1, ln1_g, seq, "ffn1_bwd")

    dmod = jnp.concatenate([dmod1, dmod2, dg2, dmod3], axis=1).reshape(nseq, 9 * dm)
    half = dm // 2
    jobs = _Jobs([gather_spread_job([dmod]),
                  chip_exchange_job([p_wgu2], rows=(fc // 2, fc // 2), into=far_top)])
    (g_wd1,), res = tn_matmul(a1[None], df1[None], "ffn1_dwd", job=jobs, a_width=pair)
    dmod_spread, (far_wgu2,) = jobs.split(res)
    g_wd1 = g_wd1.reshape(N_DEV, ff // N_DEV, dm)
    jobs = _Jobs([swap_job([g_wd1]), gather_forward_job(dmod_spread)])
    (g_l,), res = tn_matmul(dgu1, h1[None], "ffn1_dwgu_l", job=jobs, b_cols=(0, half), a_width=pair)
    (sw_wd1,), (dmod_all,) = jobs.split(res)
    g_l = g_l.reshape(N_DEV, fc, half)
    p_wd1, own_wd1 = pair_sum(core, g_wd1, sw_wd1, "pair_wd1")
    jobs = _Jobs([chip_exchange_job([p_wd1]), swap_job([g_l])])
    (g_r,), res = tn_matmul(dgu1, h1[None], "ffn1_dwgu_r", job=jobs, b_cols=(1, half), a_width=pair)
    (far_wd1,), (sw_l,) = jobs.split(res)
    g_r = g_r.reshape(N_DEV, fc, half)
    p_l, own_l = pair_sum(core, g_l, sw_l, "pair_wgu1_l")

    dmod_cols = lax.dynamic_slice(dmod_all.reshape(N_DEV * nseq, 9 * dm), (0, dev * ada_cols), (N_DEV * nseq, ada_cols))
    grad_w_ada, gb_cols = ada_bwd(cond_all, dmod_cols, "ada_bwd")
    dsinks = jnp.sum(dsink_rows.reshape(nseq, N_Q_HEADS, ATTN_BLOCK), axis=(0, 2))
    small = jnp.zeros((8, dm), F32)
    small = small.at[0:2].set(dln1).at[2:4].set(dln2).at[4:6].set(dln3)
    small = small.at[6, 0:N_Q_HEADS].set(dsinks).at[7, 0].set(loss_part[0, 0])

    jobs = _Jobs([chip_exchange_job([p_l]), swap_job([g_r]), gather_spread_job([small, dconvw, gb_cols])])
    (g_win,), res = tn_matmul(dproj[None], h2[None], "dwin", job=jobs)
    (far_l,), (sw_r,), small_spread = jobs.split(res)
    g_win = g_win.reshape(N_DEV, in_cols, dm)
    p_r, own_r = pair_sum(core, g_r, sw_r, "pair_wgu1_r")
    jobs = _Jobs([chip_exchange_job([p_r]), swap_job([g_win]), gather_forward_job(small_spread)])
    (g_wout,), res = tn_matmul(mixin[None], dmix[None], "dwout", job=jobs)
    (far_r,), (sw_win,), (small_all, dconvw_all, gb_all) = jobs.split(res)
    g_wout = g_wout.reshape(N_DEV, dm // N_DEV, dm)
    p_win, own_win = pair_sum(core, g_win, sw_win, "pair_win")
    jobs = _Jobs([chip_exchange_job([p_win]), swap_job([g_wout])])
    (far_win,), (sw_wout,) = jobs.split(run_job(jobs, "rs_tail_win"))
    p_wout, own_wout = pair_sum(core, g_wout, sw_wout, "pair_wout")
    (far_wout,) = run_job(chip_exchange_job([p_wout]), "rs_tail_wout")

    grads = {
        "ffn1_w_gate_up": jnp.concatenate([own_l, own_r], axis=1), "ffn1_w_down": own_wd1,
        "w_in": own_win, "w_out": own_wout, "ffn2_w_gate_up": own_wgu2, "ffn2_w_down": own_wd2,
    }
    others = {"ffn1_w_gate_up": jnp.concatenate([far_l, far_r], axis=2), "ffn1_w_down": far_wd1,
              "w_in": far_win, "w_out": far_wout, "ffn2_w_gate_up": far_wgu2, "ffn2_w_down": far_wd2}

    grads["w_ada"] = grad_w_ada
    small_sum = sum_devices(small_all, "sum_small")
    dconvw_sum = sum_devices(dconvw_all, "sum_convw")
    loss = small_sum[7, 0]
    grads["b_ada"] = gb_all.reshape(1, N_DEV * ada_cols)
    grads["conv_w"] = lax.dynamic_slice(dconvw_sum, (0, dev * conv_cols), (CONV_TAPS, conv_cols))
    grads["attn_sinks"] = small_sum[6:7, 0:N_Q_HEADS]
    for i, nm in enumerate(["ln1_g", "ln1_b", "ln2_g", "ln2_b", "ln3_g", "ln3_b"]):
        grads[nm] = small_sum[i:i + 1]

    given = dict(w_ada=(w_ada, m_w_ada, v_w_ada), b_ada=(b_ada, m_b_ada, v_b_ada),
                 ffn1_w_gate_up=(ffn1_w_gate_up, m_ffn1_w_gate_up, v_ffn1_w_gate_up),
                 ffn1_w_down=(ffn1_w_down, m_ffn1_w_down, v_ffn1_w_down),
                 ln1_g=(ln1_g, m_ln1_g, v_ln1_g), ln1_b=(ln1_b, m_ln1_b, v_ln1_b),
                 w_in=(w_in, m_w_in, v_w_in), conv_w=(conv_w, m_conv_w, v_conv_w),
                 attn_sinks=(attn_sinks, m_attn_sinks, v_attn_sinks), w_out=(w_out, m_w_out, v_w_out),
                 ln2_g=(ln2_g, m_ln2_g, v_ln2_g), ln2_b=(ln2_b, m_ln2_b, v_ln2_b),
                 ffn2_w_gate_up=(ffn2_w_gate_up, m_ffn2_w_gate_up, v_ffn2_w_gate_up),
                 ffn2_w_down=(ffn2_w_down, m_ffn2_w_down, v_ffn2_w_down),
                 ln3_g=(ln3_g, m_ln3_g, v_ln3_g), ln3_b=(ln3_b, m_ln3_b, v_ln3_b))
    order = ["w_ada", "b_ada", "ffn1_w_gate_up", "ffn1_w_down", "ln1_g", "ln1_b", "w_in", "conv_w", "attn_sinks",
             "w_out", "ln2_g", "ln2_b", "ffn2_w_gate_up", "ffn2_w_down", "ln3_g", "ln3_b"]
    transposed = ("ffn1_w_gate_up", "ffn2_w_gate_up", "w_in")
    big = ("w_ada", "ffn1_w_gate_up", "ffn1_w_down", "w_in", "w_out", "ffn2_w_gate_up", "ffn2_w_down")
    results = {}
    for nm in big:
        if nm in transposed:
            w2, m2, v2 = [t[0].T for t in given[nm]]
            results[nm] = [t.T[None] for t in adamw(w2, grads[nm], m2, v2, "adamw_" + nm, others=others.get(nm))]
        else:
            w2, m2, v2 = [t[0] for t in given[nm]]
            results[nm] = [t[None] for t in adamw(w2, grads[nm], m2, v2, "adamw_" + nm, others=others.get(nm))]
    small_names = [nm for nm in order if nm not in big]
    items = []
    for nm in small_names:
        shape = given[nm][0].shape
        two_d = (shape[-2], shape[-1])
        items.append((given[nm][0].reshape(two_d), grads[nm].reshape(two_d), *[t.reshape(two_d) for t in given[nm][1:]]))
    for nm, res in zip(small_names, adamw_small(items, "adamw_small")):
        shape = given[nm][0].shape
        results[nm] = [grads[nm].reshape(shape)] + [t.reshape(shape) for t in res]
    grad_x = dx0.reshape(nseq, seq, dm)
    return (loss, grad_x, *[results[nm][i] for i in range(4) for nm in order])
```

```python
import functools

import jax
import jax.numpy as jnp
from jax import lax
from jax.experimental import pallas as pl
from jax.experimental.pallas import tpu as pltpu

F32 = jnp.float32
BF16 = jnp.bfloat16
MESH = pl.DeviceIdType.MESH

N_DEV = 8
N_CHIP = 4
HEAD_DIM = 64
N_Q_HEADS = 8
N_KV_HEADS = 2
GQA_GROUP = N_Q_HEADS // N_KV_HEADS
ATTN_BLOCK = 128
ROT_DIM = 16
ROPE_THETA = 500000.0
CONV_TAPS = 3
LN_EPS = 1e-5
DN_ALPHA = 2.0 ** 0.25
ADAM_LR = 0.001
ADAM_B1 = 0.9
ADAM_B2 = 0.999
ADAM_EPS = 1e-08
ADAM_WD = 0.01
ADAM_STEP = 10
NEG_BIG = -1e30

VMEM_LIMIT = 56 * 1024 * 1024
TOKEN_TILE = 256
FFN_FWD_TILE = 512
MIX_TILE = 512
TN_VMEM_BUDGET = 36 * 1024 * 1024


def _params(semantics=None, vmem=VMEM_LIMIT):
    return pltpu.CompilerParams(dimension_semantics=semantics, vmem_limit_bytes=vmem)


def _dot(a, b):
    return jnp.dot(a, b, preferred_element_type=F32)


def _dot_nt(a, b):
    return lax.dot_general(a, b, (((1,), (1,)), ((), ())), preferred_element_type=F32)


def _dot_tn(a, b):
    return lax.dot_general(a, b, (((0,), (0,)), ((), ())), preferred_element_type=F32)


def _sigmoid(x):
    return pl.reciprocal(1.0 + jnp.exp(-x), approx=True)


def _ln_stats(r):
    mu = jnp.mean(r, axis=-1, keepdims=True)
    d = r - mu
    var = jnp.mean(d * d, axis=-1, keepdims=True)
    rstd = lax.rsqrt(var + LN_EPS)
    return d * rstd, rstd


def _ln_bwd(dy, r, g):
    xhat, rstd = _ln_stats(r)
    dxhat = dy * g
    c1 = jnp.mean(dxhat, axis=-1, keepdims=True)
    c2 = jnp.mean(dxhat * xhat, axis=-1, keepdims=True)
    dr = rstd * (dxhat - c1 - xhat * c2)
    return dr, jnp.sum(dy * xhat, axis=0, keepdims=True), jnp.sum(dy, axis=0, keepdims=True)


def _const_spec(shape):
    nd = len(shape)
    return pl.BlockSpec(shape, lambda *_: (0,) * nd, pipeline_mode=pl.Buffered(1))


def all_gather(arrs, name):
    n = len(arrs)

    def body(*refs):
        ins, outs = refs[:n], refs[n:2 * n]
        send_sems, recv_sems, local_sems = refs[2 * n:]
        x, y, c = lax.axis_index("x"), lax.axis_index("y"), lax.axis_index("c")
        me, sibling = (x, y, c), (x, y, 1 - c)
        chips = [(1 - x, y), (x, 1 - y), (1 - x, 1 - y)]

        def slot(i, p):
            return outs[i].at[4 * p[0] + 2 * p[1] + p[2]]

        def copy(i, k, block, to, src=None):
            return pltpu.make_async_remote_copy(
                src_ref=slot(i, block) if src is None else src, dst_ref=slot(i, block),
                send_sem=send_sems.at[i, k], recv_sem=recv_sems.at[i, k],
                device_id=to, device_id_type=MESH)

        mine = [pltpu.make_async_copy(ins[i], slot(i, me), local_sems.at[i]) for i in range(n)]
        for cp in mine:
            cp.start()
        first = []
        for i in range(n):
            first.append(copy(i, 0, me, sibling, src=ins[i]))
            first += [copy(i, 1 + j, me, (*chip, c), src=ins[i]) for j, chip in enumerate(chips)]
        for cp in first:
            cp.start()
        passed = []
        for i in range(n):
            for j, chip in enumerate(chips):
                copy(i, 1 + j, (*chip, c), me).wait_recv()
                cp = copy(i, 4 + j, (*chip, c), sibling)
                cp.start()
                passed.append(cp)
        for i in range(n):
            copy(i, 0, sibling, me).wait_recv()
            for j, chip in enumerate(chips):
                copy(i, 4 + j, (*chip, 1 - c), me).wait_recv()
        for cp in first + passed:
            cp.wait_send()
        for cp in mine:
            cp.wait()

    any_spec = pl.BlockSpec(memory_space=pl.ANY)
    return pl.pallas_call(
        body, name=name,
        out_shape=[jax.ShapeDtypeStruct((N_DEV, *a.shape), a.dtype) for a in arrs],
        in_specs=[any_spec] * n, out_specs=[any_spec] * n,
        scratch_shapes=[pltpu.SemaphoreType.DMA((n, 7)), pltpu.SemaphoreType.DMA((n, 7)),
                        pltpu.SemaphoreType.DMA((n,))],
    )(*arrs)


def _place():
    x, y, c = lax.axis_index("x"), lax.axis_index("y"), lax.axis_index("c")
    return x, y, c, [(1 - x, y), (x, 1 - y), (1 - x, 1 - y)]


def _slot(p):
    return 4 * p[0] + 2 * p[1] + p[2]


class _Job:
    def __init__(self, ins, outs, nsem, copies, aliases=None, local=None):
        self.ins, self.outs, self.nsem, self.copies = list(ins), list(outs), nsem, copies
        self.aliases = aliases or {}
        self.local = local

    def scratch(self):
        s = [pltpu.SemaphoreType.DMA(self.nsem), pltpu.SemaphoreType.DMA(self.nsem)]
        if self.local is not None:
            s.append(pltpu.SemaphoreType.DMA((len(self.ins),)))
        return s

    def start(self, ins, outs, sems):
        if self.local is not None:
            for cp in self.local(ins, outs, sems[2]):
                cp.start()
        for cp in self.copies(ins, outs, sems[0], sems[1])[0]:
            cp.start()

    def finish(self, ins, outs, sems):
        started, awaited = self.copies(ins, outs, sems[0], sems[1])
        for cp in awaited:
            cp.wait_recv()
        for cp in started:
            cp.wait_send()
        if self.local is not None:
            for cp in self.local(ins, outs, sems[2]):
                cp.wait()


class _Jobs:
    def __init__(self, jobs):
        self.jobs = jobs
        self.ins = [a for j in jobs for a in j.ins]
        self.outs = [o for j in jobs for o in j.outs]
        self.two_phase = any(getattr(j, "two_phase", False) for j in jobs)
        self.aliases = {}
        at_in = at_out = 0
        for j in jobs:
            self.aliases.update({at_in + i: at_out + o for i, o in j.aliases.items()})
            at_in, at_out = at_in + len(j.ins), at_out + len(j.outs)

    def scratch(self):
        return [s for j in self.jobs for s in j.scratch()]

    def _each(self, ins, outs, sems):
        at_in = at_out = at_sem = 0
        for j in self.jobs:
            n_in, n_out, n_sem = len(j.ins), len(j.outs), len(j.scratch())
            yield j, ins[at_in:at_in + n_in], outs[at_out:at_out + n_out], sems[at_sem:at_sem + n_sem]
            at_in, at_out, at_sem = at_in + n_in, at_out + n_out, at_sem + n_sem

    def start(self, ins, outs, sems):
        for j, i, o, s in self._each(ins, outs, sems):
            j.start(i, o, s)

    def turn(self, ins, outs, sems):
        for j, i, o, s in self._each(ins, outs, sems):
            if getattr(j, "two_phase", False):
                j.turn(i, o, s)

    def finish(self, ins, outs, sems):
        for j, i, o, s in self._each(ins, outs, sems):
            j.finish(i, o, s)

    def split(self, results):
        at, parts = 0, []
        for j in self.jobs:
            parts.append(results[at:at + len(j.outs)])
            at += len(j.outs)
        return parts


def _remote(src, dst, send, recv, idx, to):
    return pltpu.make_async_remote_copy(src_ref=src, dst_ref=dst, send_sem=send.at[idx], recv_sem=recv.at[idx],
                                        device_id=to, device_id_type=MESH)


def _spread_copies(ins, outs, send, recv, base=0):
    x, y, c, chips = _place()
    me = (x, y, c)
    peers = [(x, y, 1 - c)] + [(*chip, c) for chip in chips]
    started, awaited = [], []
    for i, (src, dst) in enumerate(zip(ins, outs)):
        for k, peer in enumerate(peers):
            started.append(_remote(src, dst.at[_slot(me)], send, recv, (base + i, k), peer))
            awaited.append(_remote(src, dst.at[_slot(peer)], send, recv, (base + i, k), peer))
    return started, awaited


def _forward_copies(ins, outs, send, recv, base=0):
    x, y, c, chips = _place()
    started, awaited = [], []
    for i, buf in enumerate(outs):
        for j, chip in enumerate(chips):
            mine, theirs = buf.at[_slot((*chip, c))], buf.at[_slot((*chip, 1 - c))]
            started.append(_remote(mine, mine, send, recv, (base + i, j), (x, y, 1 - c)))
            awaited.append(_remote(theirs, theirs, send, recv, (base + i, j), (x, y, 1 - c)))
    return started, awaited


def _own_block_copies(ins, outs, sems):
    x, y, c, _ = _place()
    return [pltpu.make_async_copy(src, dst.at[_slot((x, y, c))], sems.at[i])
            for i, (src, dst) in enumerate(zip(ins, outs))]


def gather_spread_job(shards):
    outs = [jax.ShapeDtypeStruct((N_DEV, *a.shape), a.dtype) for a in shards]
    return _Job(shards, outs, (len(shards), 4), _spread_copies, local=_own_block_copies)


def gather_forward_job(fulls):
    outs = [jax.ShapeDtypeStruct(a.shape, a.dtype) for a in fulls]
    return _Job(fulls, outs, (len(fulls), 3), _forward_copies, aliases={i: i for i in range(len(fulls))})


TURN_EIGHTHS = 6


class _GatherJob:
    two_phase = True

    def __init__(self, shards):
        self.ins = list(shards)
        self.outs = [jax.ShapeDtypeStruct((N_DEV, *a.shape), a.dtype) for a in shards]
        self.aliases = {}

    def scratch(self):
        n = len(self.ins)
        return [pltpu.SemaphoreType.DMA((n, 4)), pltpu.SemaphoreType.DMA((n, 4)),
                pltpu.SemaphoreType.DMA((n, 3)), pltpu.SemaphoreType.DMA((n, 3)), pltpu.SemaphoreType.DMA((n,))]

    def start(self, ins, outs, sems):
        for cp in _own_block_copies(ins, outs, sems[4]) + _spread_copies(ins, outs, sems[0], sems[1])[0]:
            cp.start()

    def turn(self, ins, outs, sems):
        for cp in _spread_copies(ins, outs, sems[0], sems[1])[1]:
            cp.wait_recv()
        for cp in _forward_copies(outs, outs, sems[2], sems[3])[0]:
            cp.start()

    def finish(self, ins, outs, sems):
        handed_on, arriving = _forward_copies(outs, outs, sems[2], sems[3])
        for cp in arriving:
            cp.wait_recv()
        for cp in _spread_copies(ins, outs, sems[0], sems[1])[0] + handed_on:
            cp.wait_send()
        for cp in _own_block_copies(ins, outs, sems[4]):
            cp.wait()


def swap_job(gs):
    def copies(ins, outs, send, recv):
        x, y, c, _ = _place()
        started, awaited = [], []
        for i, (g, r1) in enumerate(zip(ins, outs)):
            for q in range(N_CHIP):
                started.append(_remote(g.at[2 * q + (1 - c)], r1.at[q], send, recv, (i, q), (x, y, 1 - c)))
                awaited.append(_remote(g.at[2 * q + c], r1.at[q], send, recv, (i, q), (x, y, 1 - c)))
        return started, awaited

    outs = [jax.ShapeDtypeStruct((N_CHIP, *g.shape[1:]), g.dtype) for g in gs]
    return _Job(gs, outs, (len(gs), N_CHIP), copies)


def chip_exchange_job(ps, rows=None, into=None):
    n = len(ps)

    def copies(ins, outs, send, recv):
        x, y, c, chips = _place()
        started, awaited = [], []
        for i, (p, r2) in enumerate(zip(ins[:n], outs)):
            for k, chip in enumerate(chips):
                src, mine, dst = p.at[2 * chip[0] + chip[1]], p.at[2 * x + y], r2.at[k]
                if rows is not None:
                    src, mine, dst = (t.at[pl.ds(rows[0], rows[1])] for t in (src, mine, dst))
                started.append(_remote(src, dst, send, recv, (i, k), (*chip, c)))
                awaited.append(_remote(mine, dst, send, recv, (i, k), (*chip, c)))
        return started, awaited

    outs = [jax.ShapeDtypeStruct((3, *p.shape[1:]), p.dtype) for p in ps]
    if into is None:
        return _Job(ps, outs, (n, 3), copies)
    return _Job(list(ps) + list(into), outs, (n, 3), copies, aliases={n + i: i for i in range(n)})


def _call(body, job, *, name, grid, in_specs, out_specs, out_shape, args, scratch_shapes=(), vmem=VMEM_LIMIT):
    if job is None:
        res = pl.pallas_call(
            body, name=name, grid=grid, in_specs=in_specs, out_specs=out_specs, out_shape=out_shape,
            scratch_shapes=list(scratch_shapes), compiler_params=_params(("arbitrary",) * len(grid), vmem),
        )(*args)
        return res, []
    n_in, n_out, n_scr = len(in_specs), len(out_specs), len(scratch_shapes)
    j_in, j_out = len(job.ins), len(job.outs)

    def with_copies(*refs):
        at = 0
        ins = refs[at:at + n_in]; at += n_in
        jins = refs[at:at + j_in]; at += j_in
        outs = refs[at:at + n_out]; at += n_out
        jouts = refs[at:at + j_out]; at += j_out
        scr = refs[at:at + n_scr]; at += n_scr
        sems = refs[at:]
        ids = [pl.program_id(d) for d in range(len(grid))]
        first = functools.reduce(jnp.logical_and, [i == 0 for i in ids])
        last = functools.reduce(jnp.logical_and, [i == n - 1 for i, n in zip(ids, grid)])

        @pl.when(first)
        def _():
            job.start(jins, jouts, sems)

        if getattr(job, "two_phase", False):
            steps, at = 1, 0
            for i, n in zip(ids, grid):
                steps, at = steps * n, at * n + i

            @pl.when(at == (TURN_EIGHTHS * steps) // 8)
            def _():
                job.turn(jins, jouts, sems)

        body(*ins, *outs, *scr)

        @pl.when(last)
        def _():
            job.finish(jins, jouts, sems)

    any_spec = pl.BlockSpec(memory_space=pl.ANY)
    res = pl.pallas_call(
        with_copies, name=name, grid=grid,
        in_specs=list(in_specs) + [any_spec] * j_in, out_specs=list(out_specs) + [any_spec] * j_out,
        out_shape=list(out_shape) + list(job.outs),
        input_output_aliases={n_in + i: n_out + o for i, o in job.aliases.items()},
        scratch_shapes=list(scratch_shapes) + job.scratch(),
        compiler_params=_params(("arbitrary",) * len(grid), vmem),
    )(*args, *job.ins)
    return res[:n_out], res[n_out:]


def run_job(job, name):
    def body(*refs):
        j_in, j_out = len(job.ins), len(job.outs)
        ins, outs, sems = refs[:j_in], refs[j_in:j_in + j_out], refs[j_in + j_out:]
        job.start(ins, outs, sems)
        job.finish(ins, outs, sems)

    any_spec = pl.BlockSpec(memory_space=pl.ANY)
    return pl.pallas_call(
        body, name=name, in_specs=[any_spec] * len(job.ins), out_specs=[any_spec] * len(job.outs),
        out_shape=list(job.outs), input_output_aliases=dict(job.aliases), scratch_shapes=job.scratch(),
    )(*job.ins)


def pair_sum(core, g, r1, name):
    _, rows, cols = g.shape
    rb = next(cand for cand in range(min(rows, 512), 0, -16) if rows % cand == 0)

    def body(core_ref, g_ref, r1_ref, p_ref, own_ref):
        del core_ref
        x, y, _, _ = _place()
        s = g_ref[0].astype(F32) + r1_ref[0].astype(F32)
        p_ref[0] = s.astype(BF16)

        @pl.when(pl.program_id(1) == 2 * x + y)
        def _():
            own_ref[...] = s

    chunk = (1, rb, cols)
    return pl.pallas_call(
        body, name=name,
        grid_spec=pltpu.PrefetchScalarGridSpec(
            num_scalar_prefetch=1, grid=(rows // rb, N_CHIP),
            in_specs=[pl.BlockSpec(chunk, lambda i, q, core_ref: (2 * q + core_ref[0], i, 0)),
                      pl.BlockSpec(chunk, lambda i, q, core_ref: (q, i, 0))],
            out_specs=[pl.BlockSpec(chunk, lambda i, q, core_ref: (q, i, 0)),
                       pl.BlockSpec((rb, cols), lambda i, q, core_ref: (i, 0))]),
        out_shape=[jax.ShapeDtypeStruct((N_CHIP, rows, cols), BF16), jax.ShapeDtypeStruct((rows, cols), F32)],
        compiler_params=_params(("arbitrary", "arbitrary")),
    )(core, g, r1)


def sum_devices(a, name):
    def body(a_ref, o_ref):
        acc = a_ref[0]
        for d in range(1, N_DEV):
            acc = acc + a_ref[d]
        o_ref[...] = acc

    return pl.pallas_call(body, name=name, out_shape=jax.ShapeDtypeStruct(a.shape[1:], F32))(a)


def _adam_update(w, g, m, v):
    nm = ADAM_B1 * m + (1.0 - ADAM_B1) * g
    nv = ADAM_B2 * v + (1.0 - ADAM_B2) * (g * g)
    m_hat = nm / (1.0 - ADAM_B1 ** ADAM_STEP)
    v_hat = nv / (1.0 - ADAM_B2 ** ADAM_STEP)
    return -ADAM_LR * (m_hat / (jnp.sqrt(v_hat) + ADAM_EPS) + ADAM_WD * w), nm, nv


def adamw(w, g, m, v, name, others=None):
    rows, cols = w.shape
    rb = rows
    for cand in range(min(rows, 512), 7, -8):
        if rows % cand == 0 and cand % 8 == 0:
            rb = cand
            break

    def body(*refs):
        if others is None:
            w_ref, g_ref, m_ref, v_ref, d_ref, nm_ref, nv_ref = refs
            gg = g_ref[...]
        else:
            w_ref, g_ref, m_ref, v_ref, r2_ref, go_ref, d_ref, nm_ref, nv_ref = refs
            gg = g_ref[...]
            for k in range(3):
                gg = gg + r2_ref[k].astype(F32)
            go_ref[...] = gg
        d_ref[...], nm_ref[...], nv_ref[...] = _adam_update(w_ref[...], gg, m_ref[...], v_ref[...])

    spec = pl.BlockSpec((rb, cols), lambda i: (i, 0))
    out = jax.ShapeDtypeStruct((rows, cols), F32)
    in_specs, args = [spec] * 4, [w, g, m, v]
    if others is not None:
        in_specs.append(pl.BlockSpec((3, rb, cols), lambda i: (0, i, 0)))
        args.append(others)
    n_out = 3 if others is None else 4
    res = pl.pallas_call(
        body, name=name, grid=(rows // rb,), in_specs=in_specs, out_specs=[spec] * n_out,
        out_shape=[out] * n_out, compiler_params=_params(("parallel",)),
    )(*args)
    return (g, *res) if others is None else tuple(res)


def adamw_small(items, name):
    n = len(items)

    def body(*refs):
        ins, outs = refs[:4 * n], refs[4 * n:]
        for i in range(n):
            w_ref, g_ref, m_ref, v_ref = ins[4 * i:4 * i + 4]
            d_ref, nm_ref, nv_ref = outs[3 * i:3 * i + 3]
            d_ref[...], nm_ref[...], nv_ref[...] = _adam_update(w_ref[...], g_ref[...], m_ref[...], v_ref[...])

    res = pl.pallas_call(
        body, name=name,
        out_shape=[jax.ShapeDtypeStruct(w.shape, F32) for w, _, _, _ in items for _ in range(3)],
    )(*[t for item in items for t in item])
    return [tuple(res[3 * i:3 * i + 3]) for i in range(n)]


def ada_fwd(c_all, w_cols, b_cols, name):
    def body(c_ref, w_ref, b_ref, cond_ref, mod_ref):
        cc = c_ref[...]
        cond = (cc * _sigmoid(cc)).astype(BF16)
        cond_ref[...] = cond
        mod_ref[...] = _dot(cond, w_ref[...].astype(BF16)) + b_ref[...]

    n, cols = c_all.shape[0], w_cols.shape[1]
    return pl.pallas_call(
        body, name=name,
        out_shape=[jax.ShapeDtypeStruct(c_all.shape, BF16), jax.ShapeDtypeStruct((n, cols), F32)],
        compiler_params=_params(),
    )(c_all, w_cols, b_cols)


def ada_bwd(cond_all, dmod_cols, name):
    def body(c_ref, d_ref, gw_ref, gb_ref):
        d = d_ref[...]
        gw_ref[...] = _dot_tn(c_ref[...], d.astype(BF16))
        gb_ref[...] = jnp.sum(d, axis=0, keepdims=True)

    dm, cols = cond_all.shape[1], dmod_cols.shape[1]
    return pl.pallas_call(
        body, name=name,
        out_shape=[jax.ShapeDtypeStruct((dm, cols), F32), jax.ShapeDtypeStruct((1, cols), F32)],
        compiler_params=_params(),
    )(cond_all, dmod_cols)


MXU_COLS = 256
FFN_CHUNK = 4 * MXU_COLS


def _hidden_chunks(ff):
    assert ff % MXU_COLS == 0
    return [(at, min(FFN_CHUNK, ff - at)) for at in range(0, ff, FFN_CHUNK)]


def _mod_spec(tiles_per_seq, dm):
    return pl.BlockSpec((1, 1, dm), lambda i: (i // tiles_per_seq, 0, 0))


def ffn_fwd(x, sh, sc, gt, wgu, wd, ln_g, ln_b, seq, name, target=None, job=None):
    tokens, dm = x.shape
    ff = wgu.shape[1]
    chunks = _hidden_chunks(ff)
    tm = min(FFN_FWD_TILE, seq)
    tiles_per_seq = seq // tm
    with_loss = target is not None

    def body(*refs):
        if with_loss:
            (x_ref, sh_ref, sc_ref, gt_ref, wgu_ref, wd_ref, lg_ref, lb_ref, t_ref,
             xo_ref, loss_ref, r_ref, gu_ref, f_ref) = refs
        else:
            (x_ref, sh_ref, sc_ref, gt_ref, wgu_ref, wd_ref, lg_ref, lb_ref,
             xo_ref, r_ref, gu_ref, f_ref) = refs
        xx = x_ref[...]
        h = (xx * (1.0 + sc_ref[0]) + sh_ref[0]).astype(BF16)
        acc = jnp.zeros((tm, dm), F32)
        for at, wdt in chunks:
            gk = _dot_nt(h, wgu_ref[0, at:at + wdt, :])
            uk = _dot_nt(h, wgu_ref[1, at:at + wdt, :])
            gu_ref[0, :, at:at + wdt] = gk.astype(BF16)
            gu_ref[1, :, at:at + wdt] = uk.astype(BF16)
            a = (gk * _sigmoid(gk) * uk).astype(BF16)
            acc = acc + _dot(a, wd_ref[at:at + wdt, :])
        f_ref[...] = acc.astype(BF16)
        r = DN_ALPHA * xx + (0.5 * (1.0 + gt_ref[0])) * acc
        r_ref[...] = r
        xhat, _ = _ln_stats(r)
        yy = xhat * lg_ref[...] + lb_ref[...]
        if with_loss:
            err = yy - t_ref[...]
            xo_ref[...] = err * (1.0 / dm)

            @pl.when(pl.program_id(0) == 0)
            def _():
                loss_ref[...] = jnp.zeros_like(loss_ref)

            loss_ref[...] += jnp.full((1, 128), (0.5 / dm) * jnp.sum(err * err), F32)
        else:
            xo_ref[...] = yy

    tile = pl.BlockSpec((tm, dm), lambda i: (i, 0))
    mod = _mod_spec(tiles_per_seq, dm)
    in_specs = [tile, mod, mod, mod, _const_spec(wgu.shape), _const_spec(wd.shape),
                _const_spec((1, dm)), _const_spec((1, dm))]
    args = [x, sh, sc, gt, wgu, wd, ln_g, ln_b]
    out_specs = [tile]
    out_shape = [jax.ShapeDtypeStruct((tokens, dm), F32)]
    if with_loss:
        in_specs.append(tile)
        args.append(target)
        out_specs.append(pl.BlockSpec((1, 128), lambda i: (0, 0)))
        out_shape.append(jax.ShapeDtypeStruct((1, 128), F32))
    out_specs += [tile, pl.BlockSpec((2, tm, ff), lambda i: (0, i, 0)), tile]
    out_shape += [jax.ShapeDtypeStruct((tokens, dm), F32), jax.ShapeDtypeStruct((2, tokens, ff), BF16),
                  jax.ShapeDtypeStruct((tokens, dm), BF16)]
    return _call(body, job, name=name, grid=(tokens // tm,), in_specs=in_specs, out_specs=out_specs,
                 out_shape=out_shape, args=args)


def ffn_up(x, sh, sc, wgu, seq, name, job=None):
    tokens, dm = x.shape
    ff = wgu.shape[1]
    chunks = _hidden_chunks(ff)
    tm = min(FFN_FWD_TILE, seq)

    def body(x_ref, sh_ref, sc_ref, wgu_ref, gu_ref, a_ref):
        h = (x_ref[...] * (1.0 + sc_ref[0]) + sh_ref[0]).astype(BF16)
        for at, wdt in chunks:
            gk = _dot_nt(h, wgu_ref[0, at:at + wdt, :])
            uk = _dot_nt(h, wgu_ref[1, at:at + wdt, :])
            gu_ref[0, :, at:at + wdt] = gk.astype(BF16)
            gu_ref[1, :, at:at + wdt] = uk.astype(BF16)
            a_ref[:, at:at + wdt] = (gk * _sigmoid(gk) * uk).astype(BF16)

    tile = pl.BlockSpec((tm, dm), lambda i: (i, 0))
    mod = _mod_spec(seq // tm, dm)
    return _call(
        body, job, name=name, grid=(tokens // tm,),
        in_specs=[tile, mod, mod, _const_spec(wgu.shape)],
        out_specs=[pl.BlockSpec((2, tm, ff), lambda i: (0, i, 0)), pl.BlockSpec((tm, ff), lambda i: (i, 0))],
        out_shape=[jax.ShapeDtypeStruct((2, tokens, ff), BF16), jax.ShapeDtypeStruct((tokens, ff), BF16)],
        args=(x, sh, sc, wgu))


def ffn_down(x, a, gt, wd, ln_g, ln_b, seq, name, job=None):
    tokens, dm = x.shape
    ff = wd.shape[0]
    chunks = _hidden_chunks(ff)
    tm = min(FFN_FWD_TILE, seq)

    def body(x_ref, a_ref, gt_ref, wd_ref, lg_ref, lb_ref, xo_ref, r_ref, f_ref):
        acc = jnp.zeros((tm, dm), F32)
        for at, wdt in chunks:
            acc = acc + _dot(a_ref[:, at:at + wdt], wd_ref[at:at + wdt, :])
        f_ref[...] = acc.astype(BF16)
        r = DN_ALPHA * x_ref[...] + (0.5 * (1.0 + gt_ref[0])) * acc
        r_ref[...] = r
        xhat, _ = _ln_stats(r)
        xo_ref[...] = xhat * lg_ref[...] + lb_ref[...]

    tile = pl.BlockSpec((tm, dm), lambda i: (i, 0))
    return _call(
        body, job, name=name, grid=(tokens // tm,),
        in_specs=[tile, pl.BlockSpec((tm, ff), lambda i: (i, 0)), _mod_spec(seq // tm, dm), _const_spec(wd.shape),
                  _const_spec((1, dm)), _const_spec((1, dm))],
        out_specs=[tile, tile, tile],
        out_shape=[jax.ShapeDtypeStruct((tokens, dm), F32), jax.ShapeDtypeStruct((tokens, dm), F32),
                   jax.ShapeDtypeStruct((tokens, dm), BF16)],
        args=(x, a, gt, wd, ln_g, ln_b))


def ffn_bwd(dy, r, x, f, gu, sh, sc, gt, wgu, wd, ln_g, seq, name, job=None):
    tokens, dm = x.shape
    ff = wgu.shape[1]
    chunks = _hidden_chunks(ff)
    tm = min(TOKEN_TILE, seq)
    tiles_per_seq = seq // tm
    nseq = tokens // seq

    def body(dy_ref, r_ref, x_ref, f_ref, gu_ref, sh_ref, sc_ref, gt_ref, wgu_ref, wd_ref, lg_ref,
             dx_ref, dgu_ref, df_ref, a_ref, h_ref, dln_ref, dmod_ref):
        i = pl.program_id(0)
        dr, dgain, dbias = _ln_bwd(dy_ref[...], r_ref[...], lg_ref[...])

        @pl.when(i == 0)
        def _():
            dln_ref[...] = jnp.zeros_like(dln_ref)

        @pl.when(i % tiles_per_seq == 0)
        def _():
            dmod_ref[...] = jnp.zeros_like(dmod_ref)

        dln_ref[0:1, :] += dgain
        dln_ref[1:2, :] += dbias
        df32 = (0.5 * (1.0 + gt_ref[0])) * dr
        df = df32.astype(BF16)
        df_ref[...] = df
        dgate = jnp.sum(dr * (0.5 * f_ref[...].astype(F32)), axis=0, keepdims=True)
        xx = x_ref[...]
        one_sc = 1.0 + sc_ref[0]
        h = (xx * one_sc + sh_ref[0]).astype(BF16)
        h_ref[...] = h
        dh = jnp.zeros((tm, dm), F32)
        for at, wdt in chunks:
            cols = slice(at, at + wdt)
            da = _dot_nt(df, wd_ref[cols, :])
            gk = gu_ref[0, :, cols].astype(F32)
            uk = gu_ref[1, :, cols].astype(F32)
            sg = _sigmoid(gk)
            sil = gk * sg
            a_ref[:, cols] = (sil * uk).astype(BF16)
            du = (da * sil).astype(BF16)
            dg = (da * uk * (sg * (1.0 + gk * (1.0 - sg)))).astype(BF16)
            dgu_ref[0, :, cols] = dg
            dgu_ref[1, :, cols] = du
            dh = dh + _dot(dg, wgu_ref[0, cols, :]) + _dot(du, wgu_ref[1, cols, :])
        dx_ref[...] = DN_ALPHA * dr + dh * one_sc
        dmod_ref[0, 0:1, :] += jnp.sum(dh, axis=0, keepdims=True)
        dmod_ref[0, 1:2, :] += jnp.sum(dh * xx, axis=0, keepdims=True)
        dmod_ref[0, 2:3, :] += dgate

    tile = pl.BlockSpec((tm, dm), lambda i: (i, 0))
    mod = _mod_spec(tiles_per_seq, dm)
    gu_spec = pl.BlockSpec((2, tm, ff), lambda i: (0, i, 0))
    return _call(
        body, job, name=name, grid=(tokens // tm,),
        in_specs=[tile, tile, tile, tile, gu_spec, mod, mod, mod, _const_spec(wgu.shape), _const_spec(wd.shape),
                  _const_spec((1, dm))],
        out_specs=[tile, gu_spec, tile, pl.BlockSpec((tm, ff), lambda i: (i, 0)), tile,
                   pl.BlockSpec((2, dm), lambda i: (0, 0)),
                   pl.BlockSpec((1, 3, dm), lambda i: (i // tiles_per_seq, 0, 0))],
        out_shape=[jax.ShapeDtypeStruct((tokens, dm), F32), jax.ShapeDtypeStruct((2, tokens, ff), BF16),
                   jax.ShapeDtypeStruct((tokens, dm), BF16), jax.ShapeDtypeStruct((tokens, ff), BF16),
                   jax.ShapeDtypeStruct((tokens, dm), BF16), jax.ShapeDtypeStruct((2, dm), F32),
                   jax.ShapeDtypeStruct((nseq, 3, dm), F32)],
        args=(dy, r, x, f, gu, sh, sc, gt, wgu, wd, ln_g))


def tn_matmul(a, b, name, job=None, b_cols=None, a_width=None):
    na, tokens, k_all = a.shape
    kk = k_all if a_width is None else a_width
    nka = k_all // kk
    assert nka * kk == k_all
    nb, _, cc = b.shape
    col = 0
    if b_cols is not None:
        col, cc = b_cols
    tt = tokens
    while 4 * tt * (kk + cc) + 8 * kk * cc > TN_VMEM_BUDGET and tt % 2 == 0 and tt > 256:
        tt //= 2
    steps = tokens // tt

    def body(a_ref, b_ref, o_ref, *acc):
        if steps == 1:
            o_ref[0, 0, 0] = _dot_tn(a_ref[0], b_ref[0]).astype(BF16)
            return
        acc_ref, = acc
        t = pl.program_id(3)

        @pl.when(t == 0)
        def _():
            acc_ref[...] = jnp.zeros_like(acc_ref)

        acc_ref[...] += _dot_tn(a_ref[0], b_ref[0])

        @pl.when(t == steps - 1)
        def _():
            o_ref[0, 0, 0] = acc_ref[...].astype(BF16)

    return _call(
        body, job, name=name, grid=(na, nka, nb, steps),
        in_specs=[pl.BlockSpec((1, tt, kk), lambda i, s, j, t: (i, t, s)),
                  pl.BlockSpec((1, tt, cc), lambda i, s, j, t: (j, t, col))],
        out_specs=[pl.BlockSpec((1, 1, 1, kk, cc), lambda i, s, j, t: (i, s, j, 0, 0))],
        out_shape=[jax.ShapeDtypeStruct((na, nka, nb, kk, cc), BF16)],
        scratch_shapes=[] if steps == 1 else [pltpu.VMEM((kk, cc), F32)], args=(a, b))


def proj_fwd(x1, sh, sc, w_in, seq, name, job=None):
    tokens, dm = x1.shape
    tm = min(MIX_TILE, seq)
    tiles_per_seq = seq // tm
    widths = [N_Q_HEADS * HEAD_DIM, N_KV_HEADS * HEAD_DIM, N_KV_HEADS * HEAD_DIM, 512, 512, 512]
    assert sum(widths) == w_in.shape[0]

    def body(x_ref, sh_ref, sc_ref, w_ref, *outs):
        h = (x_ref[...] * (1.0 + sc_ref[0]) + sh_ref[0]).astype(BF16)
        proj = _dot_nt(h, w_ref[...])
        at = 0
        for o_ref, wdt in zip(outs, widths):
            o_ref[...] = proj[:, at:at + wdt].astype(o_ref.dtype)
            at += wdt

    tile = pl.BlockSpec((tm, dm), lambda i: (i, 0))
    mod = _mod_spec(tiles_per_seq, dm)
    return _call(
        body, job, name=name, grid=(tokens // tm,),
        in_specs=[tile, mod, mod, _const_spec(w_in.shape)],
        out_specs=[pl.BlockSpec((tm, wdt), lambda i: (i, 0)) for wdt in widths],
        out_shape=[jax.ShapeDtypeStruct((tokens, wdt), F32 if i < 3 else BF16) for i, wdt in enumerate(widths)],
        args=(x1, sh, sc, w_in))


LANES = 2 * HEAD_DIM


def _head_lane(shape):
    return lax.broadcasted_iota(jnp.int32, shape, 1) % HEAD_DIM


def _lane_half(shape):
    return lax.broadcasted_iota(jnp.int32, shape, 1) // HEAD_DIM


def _swap_rot(v):
    lane = _head_lane(v.shape)
    half = ROT_DIM // 2
    return jnp.where(lane < half, pltpu.roll(v, LANES - half, 1),
                     jnp.where(lane < ROT_DIM, pltpu.roll(v, half, 1), 0.0))


def _rope(v, cos_t, sin_t):
    return v * cos_t + _swap_rot(v) * sin_t


def _unrope(dv, cos_t, sin_t):
    return dv * cos_t + _swap_rot(dv * sin_t)


def _both_halves(t, g):
    return jnp.where(_lane_half(t.shape) == g, t, pltpu.roll(t, HEAD_DIM, 1))


def _fold_halves(t, g):
    return jnp.where(_lane_half(t.shape) == g, t + pltpu.roll(t, HEAD_DIM, 1), 0.0)


def _stack_heads(blocks):
    rows = []
    for blk in blocks:
        half = _lane_half(blk.shape)
        rows += [jnp.where(half == 0, blk, 0.0), jnp.where(half == 1, blk, 0.0)]
    return jnp.concatenate(rows, axis=0)


def _unstack_heads(t, j):
    lo = t[(2 * j) * ATTN_BLOCK:(2 * j + 1) * ATTN_BLOCK]
    hi = t[(2 * j + 1) * ATTN_BLOCK:(2 * j + 2) * ATTN_BLOCK]
    return jnp.where(_lane_half(lo.shape) == 0, lo, hi)


def _band_mask(q0, w0):
    rows, cols = GQA_GROUP * ATTN_BLOCK, 2 * ATTN_BLOCK
    qi = lax.broadcasted_iota(jnp.int32, (rows, cols), 0) % ATTN_BLOCK + q0
    ki = lax.broadcasted_iota(jnp.int32, (rows, cols), 1) + w0
    diff = qi - ki
    return (diff >= 0) & (diff < ATTN_BLOCK)


def _attn_specs(seq):
    q_spec = pl.BlockSpec((seq, GQA_GROUP * HEAD_DIM), lambda b, g: (b, g))
    kv_spec = pl.BlockSpec((seq, LANES), lambda b, g: (b, 0))
    sink_spec = pl.BlockSpec((1, GQA_GROUP * ATTN_BLOCK, 1), lambda b, g: (g, 0, 0))
    return q_spec, kv_spec, sink_spec


def _block_starts(n):
    q0 = pl.multiple_of(n * ATTN_BLOCK, ATTN_BLOCK)
    w0 = pl.multiple_of(jnp.maximum(n - 1, 0) * ATTN_BLOCK, ATTN_BLOCK)
    return q0, w0


def _stacked_queries(ref, rows):
    return _stack_heads([ref[rows, j * LANES:(j + 1) * LANES] for j in range(2)]).astype(BF16)


def _sink_columns(sinks):
    return jnp.repeat(sinks.reshape(N_KV_HEADS, GQA_GROUP), ATTN_BLOCK, axis=1)[:, :, None]


def _probs_spec(nblk):
    return pl.BlockSpec((1, 1, nblk, GQA_GROUP * ATTN_BLOCK, 2 * ATTN_BLOCK), lambda b, g: (b, g, 0, 0, 0))


def _sink_probs_spec():
    return pl.BlockSpec((1, 1, GQA_GROUP * ATTN_BLOCK, LANES), lambda b, g: (b, g, 0, 0))


def attn_fwd(q, k, v, cos_t, sin_t, sinks, seq, name, job=None):
    tokens = q.shape[0]
    nblk = seq // ATTN_BLOCK
    assert nblk >= 2
    scale = HEAD_DIM ** -0.5

    nseq = tokens // seq
    rows_stacked = GQA_GROUP * ATTN_BLOCK
    assert nblk <= LANES

    def body(q_ref, k_ref, v_ref, cos_ref, sin_ref, sink_ref, o_ref, qr_ref, p_ref, ps_ref, kd_ref, vd_ref):
        g = pl.program_id(1)
        kd_ref[...] = _both_halves(_rope(k_ref[...].astype(F32), cos_ref[...], sin_ref[...]), g).astype(BF16)
        vd_ref[...] = _both_halves(v_ref[...].astype(F32), g).astype(BF16)
        sink = sink_ref[0]
        lane = lax.broadcasted_iota(jnp.int32, (rows_stacked, LANES), 1)

        ps_ref[...] = jnp.zeros_like(ps_ref)

        def block(n, carry):
            q0, w0 = _block_starts(n)
            rows, win = pl.ds(q0, ATTN_BLOCK), pl.ds(w0, 2 * ATTN_BLOCK)
            blocks = []
            for j in range(2):
                qr = _rope(q_ref[rows, j * LANES:(j + 1) * LANES].astype(F32), cos_ref[rows, :], sin_ref[rows, :]).astype(BF16)
                qr_ref[rows, j * LANES:(j + 1) * LANES] = qr
                blocks.append(qr)
            qs = _stack_heads(blocks)
            s = _dot_nt(qs, kd_ref[win, :]) * scale
            s = jnp.where(_band_mask(q0, w0), s, NEG_BIG)
            m = jnp.maximum(jnp.max(s, axis=-1, keepdims=True), sink)
            p = jnp.exp(s - m)
            e_sink = jnp.exp(sink - m)
            inv = pl.reciprocal(jnp.sum(p, axis=-1, keepdims=True) + e_sink, approx=True)
            pn = (p * inv).astype(BF16)
            p_ref[0, 0, n] = pn
            out = _dot(pn, vd_ref[win, :])
            for j in range(2):
                o_ref[rows, j * LANES:(j + 1) * LANES] = _unstack_heads(out, j).astype(o_ref.dtype)
            ps_ref[0, 0] = jnp.where(lane == n, e_sink * inv, ps_ref[0, 0])
            return carry

        lax.fori_loop(0, nblk, block, 0, unroll=2)

    q_spec, kv_spec, sink_spec = _attn_specs(seq)
    return _call(
        body, job, name=name, grid=(nseq, N_KV_HEADS),
        in_specs=[q_spec, kv_spec, kv_spec, kv_spec, kv_spec, sink_spec],
        out_specs=[q_spec, q_spec, _probs_spec(nblk), _sink_probs_spec()],
        out_shape=[jax.ShapeDtypeStruct(q.shape, BF16), jax.ShapeDtypeStruct(q.shape, BF16),
                   jax.ShapeDtypeStruct((nseq, N_KV_HEADS, nblk, rows_stacked, 2 * ATTN_BLOCK), BF16),
                   jax.ShapeDtypeStruct((nseq, N_KV_HEADS, rows_stacked, LANES), F32)],
        scratch_shapes=[pltpu.VMEM((seq, LANES), BF16), pltpu.VMEM((seq, LANES), BF16)],
        args=(q, k, v, cos_t, sin_t, _sink_columns(sinks)))


def attn_bwd(qr, k, v, do, probs, sink_probs, cos_t, sin_t, seq, name, job=None):
    tokens = qr.shape[0]
    nseq = tokens // seq
    nblk = seq // ATTN_BLOCK
    assert nblk >= 2
    rows_stacked = GQA_GROUP * ATTN_BLOCK
    scale = HEAD_DIM ** -0.5

    def body(q_ref, k_ref, v_ref, do_ref, p_ref, ps_ref, cos_ref, sin_ref, dq_ref, dk_ref, dv_ref, ds_ref,
             kd_ref, vd_ref, dkd_ref, dvd_ref, acc_ref):
        g = pl.program_id(1)
        kd_ref[...] = _both_halves(_rope(k_ref[...].astype(F32), cos_ref[...], sin_ref[...]), g).astype(BF16)
        vd_ref[...] = _both_halves(v_ref[...].astype(F32), g).astype(BF16)
        dkd_ref[...] = jnp.zeros_like(dkd_ref)
        dvd_ref[...] = jnp.zeros_like(dvd_ref)
        acc_ref[...] = jnp.zeros_like(acc_ref)
        lane = lax.broadcasted_iota(jnp.int32, (rows_stacked, LANES), 1)

        def block(n, carry):
            q0, w0 = _block_starts(n)
            rows, win = pl.ds(q0, ATTN_BLOCK), pl.ds(w0, 2 * ATTN_BLOCK)
            qs = _stacked_queries(q_ref, rows)
            dos = _stacked_queries(do_ref, rows)
            kw, vw = kd_ref[win, :], vd_ref[win, :]
            pn16 = p_ref[0, 0, n]
            pn = pn16.astype(F32)
            dvd_ref[win, :] += _dot_tn(pn16, dos)
            dp = _dot_nt(dos, vw)
            delta = jnp.sum(dp * pn, axis=-1, keepdims=True)
            ds = (pn * (dp - delta)).astype(BF16)
            dqs = _dot(ds, kw) * scale
            dkd_ref[win, :] += _dot_tn(ds, qs) * scale
            cos_b, sin_b = cos_ref[rows, :], sin_ref[rows, :]
            for j in range(2):
                dq_ref[rows, j * LANES:(j + 1) * LANES] = _unrope(_unstack_heads(dqs, j), cos_b, sin_b).astype(BF16)
            acc_ref[...] += jnp.where(lane == n, ps_ref[0, 0] * delta, 0.0)
            return carry

        lax.fori_loop(0, nblk // 2, lambda i, carry: block(2 * i + 1, block(2 * i, carry)), 0)
        ds_ref[0, 0] = -jnp.sum(acc_ref[...], axis=-1, keepdims=True)
        dk_g = _unrope(_fold_halves(dkd_ref[...], g), cos_ref[...], sin_ref[...])
        dv_g = _fold_halves(dvd_ref[...], g)

        @pl.when(g == 0)
        def _():
            dk_ref[...] = dk_g
            dv_ref[...] = dv_g

        @pl.when(g != 0)
        def _():
            dk_ref[...] += dk_g
            dv_ref[...] += dv_g

    q_spec, kv_spec, _ = _attn_specs(seq)
    return _call(
        body, job, name=name, grid=(nseq, N_KV_HEADS),
        in_specs=[q_spec, kv_spec, kv_spec, q_spec, _probs_spec(nblk), _sink_probs_spec(), kv_spec, kv_spec],
        out_specs=[q_spec, kv_spec, kv_spec, pl.BlockSpec((1, 1, rows_stacked, 1), lambda b, g: (b, g, 0, 0))],
        out_shape=[jax.ShapeDtypeStruct(qr.shape, BF16), jax.ShapeDtypeStruct(k.shape, F32),
                   jax.ShapeDtypeStruct(k.shape, F32), jax.ShapeDtypeStruct((nseq, N_KV_HEADS, rows_stacked, 1), F32)],
        scratch_shapes=[pltpu.VMEM((seq, LANES), BF16), pltpu.VMEM((seq, LANES), BF16),
                        pltpu.VMEM((seq, LANES), F32), pltpu.VMEM((seq, LANES), F32),
                        pltpu.VMEM((rows_stacked, LANES), F32)],
        args=(qr, k, v, do, probs, sink_probs, cos_t, sin_t))


CONV_COLS = 128


def _shift_down(z, by):
    t = lax.broadcasted_iota(jnp.int32, z.shape, 0)
    return jnp.where(t >= by, pltpu.roll(z, by, 0), 0.0)


def _shift_up(z, by):
    n = z.shape[0]
    t = lax.broadcasted_iota(jnp.int32, z.shape, 0)
    return jnp.where(t < n - by, pltpu.roll(z, n - by, 0), 0.0)


def conv_fwd(u, bg, cg, conv_w, seq, name):
    tokens, width = u.shape

    def body(u_ref, bg_ref, cg_ref, w_ref, o_ref):
        z = cg_ref[...].astype(F32) * u_ref[...].astype(F32)
        yy = w_ref[2:3, :] * z + w_ref[1:2, :] * _shift_down(z, 1) + w_ref[0:1, :] * _shift_down(z, 2)
        o_ref[...] = (bg_ref[...].astype(F32) * yy).astype(BF16)

    col = pl.BlockSpec((seq, CONV_COLS), lambda j, b: (b, j))
    return pl.pallas_call(
        body, name=name, grid=(width // CONV_COLS, tokens // seq),
        in_specs=[col, col, col, pl.BlockSpec((CONV_TAPS, CONV_COLS), lambda j, b: (0, j))],
        out_specs=col, out_shape=jax.ShapeDtypeStruct((tokens, width), BF16),
        compiler_params=_params(("parallel", "parallel")),
    )(u, bg, cg, conv_w)


def conv_bwd(dout, u, bg, cg, conv_w, seq, name):
    tokens, width = u.shape

    def body(do_ref, u_ref, bg_ref, cg_ref, w_ref, du_ref, dbg_ref, dcg_ref, dw_ref):
        uu, cg_v, do = u_ref[...].astype(F32), cg_ref[...].astype(F32), do_ref[...].astype(F32)
        z = cg_v * uu
        z1, z2 = _shift_down(z, 1), _shift_down(z, 2)
        yy = w_ref[2:3, :] * z + w_ref[1:2, :] * z1 + w_ref[0:1, :] * z2
        dbg_ref[...] = (do * yy).astype(BF16)
        dyy = do * bg_ref[...].astype(F32)
        dz = w_ref[2:3, :] * dyy + w_ref[1:2, :] * _shift_up(dyy, 1) + w_ref[0:1, :] * _shift_up(dyy, 2)
        du_ref[...] = (dz * cg_v).astype(BF16)
        dcg_ref[...] = (dz * uu).astype(BF16)

        @pl.when(pl.program_id(1) == 0)
        def _():
            dw_ref[...] = jnp.zeros_like(dw_ref)

        dw_ref[0:1, :] += jnp.sum(dyy * z2, axis=0, keepdims=True)
        dw_ref[1:2, :] += jnp.sum(dyy * z1, axis=0, keepdims=True)
        dw_ref[2:3, :] += jnp.sum(dyy * z, axis=0, keepdims=True)

    col = pl.BlockSpec((seq, CONV_COLS), lambda j, b: (b, j))
    w_spec = pl.BlockSpec((CONV_TAPS, CONV_COLS), lambda j, b: (0, j))
    act = jax.ShapeDtypeStruct((tokens, width), BF16)
    return pl.pallas_call(
        body, name=name, grid=(width // CONV_COLS, tokens // seq),
        in_specs=[col, col, col, col, w_spec], out_specs=[col, col, col, w_spec],
        out_shape=[act, act, act, jax.ShapeDtypeStruct((CONV_TAPS, width), F32)],
        compiler_params=_params(("parallel", "arbitrary")),
    )(dout, u, bg, cg, conv_w)


def out_fwd(x1, attn, conv, gt, w_out, ln_g, ln_b, seq, name, job=None):
    tokens, dm = x1.shape
    half = attn.shape[1]
    tm = min(MIX_TILE, seq)
    tiles_per_seq = seq // tm

    def body(x_ref, a_ref, c_ref, gt_ref, w_ref, lg_ref, lb_ref, xo_ref, r_ref, mi_ref, mix_ref):
        mixin = jnp.concatenate([a_ref[...], c_ref[...]], axis=1).astype(BF16)
        mi_ref[...] = mixin
        mix = _dot(mixin, w_ref[...])
        mix_ref[...] = mix.astype(BF16)
        r = DN_ALPHA * x_ref[...] + (1.0 + gt_ref[0]) * mix
        r_ref[...] = r
        xhat, _ = _ln_stats(r)
        xo_ref[...] = xhat * lg_ref[...] + lb_ref[...]

    tile = pl.BlockSpec((tm, dm), lambda i: (i, 0))
    htile = pl.BlockSpec((tm, half), lambda i: (i, 0))
    return _call(
        body, job, name=name, grid=(tokens // tm,),
        in_specs=[tile, htile, htile, _mod_spec(tiles_per_seq, dm), _const_spec(w_out.shape),
                  _const_spec((1, dm)), _const_spec((1, dm))],
        out_specs=[tile, tile, tile, tile],
        out_shape=[jax.ShapeDtypeStruct((tokens, dm), F32), jax.ShapeDtypeStruct((tokens, dm), F32),
                   jax.ShapeDtypeStruct((tokens, dm), BF16), jax.ShapeDtypeStruct((tokens, dm), BF16)],
        args=(x1, attn, conv, gt, w_out, ln_g, ln_b))


def out_bwd(dy, r, mix, gt, w_out, ln_g, seq, name, job=None):
    tokens, dm = r.shape
    half = dm // 2
    tm = min(MIX_TILE, seq)
    tiles_per_seq = seq // tm
    nseq = tokens // seq

    def body(dy_ref, r_ref, mix_ref, gt_ref, w_ref, lg_ref, dres_ref, da_ref, dc_ref, dmix_ref, dln_ref, dgt_ref):
        i = pl.program_id(0)
        dr, dgain, dbias = _ln_bwd(dy_ref[...], r_ref[...], lg_ref[...])

        @pl.when(i == 0)
        def _():
            dln_ref[...] = jnp.zeros_like(dln_ref)

        @pl.when(i % tiles_per_seq == 0)
        def _():
            dgt_ref[...] = jnp.zeros_like(dgt_ref)

        dln_ref[0:1, :] += dgain
        dln_ref[1:2, :] += dbias
        dgt_ref[0] += jnp.sum(dr * mix_ref[...].astype(F32), axis=0, keepdims=True)
        dres_ref[...] = DN_ALPHA * dr
        dmix = ((1.0 + gt_ref[0]) * dr).astype(BF16)
        dmix_ref[...] = dmix
        dmixin = _dot_nt(dmix, w_ref[...])
        da_ref[...] = dmixin[:, :half].astype(BF16)
        dc_ref[...] = dmixin[:, half:].astype(BF16)

    tile = pl.BlockSpec((tm, dm), lambda i: (i, 0))
    htile = pl.BlockSpec((tm, half), lambda i: (i, 0))
    return _call(
        body, job, name=name, grid=(tokens // tm,),
        in_specs=[tile, tile, tile, _mod_spec(tiles_per_seq, dm), _const_spec(w_out.shape), _const_spec((1, dm))],
        out_specs=[tile, htile, htile, tile, pl.BlockSpec((2, dm), lambda i: (0, 0)),
                   pl.BlockSpec((1, 1, dm), lambda i: (i // tiles_per_seq, 0, 0))],
        out_shape=[jax.ShapeDtypeStruct((tokens, dm), F32), jax.ShapeDtypeStruct((tokens, half), BF16),
                   jax.ShapeDtypeStruct((tokens, half), BF16), jax.ShapeDtypeStruct((tokens, dm), BF16),
                   jax.ShapeDtypeStruct((2, dm), F32), jax.ShapeDtypeStruct((nseq, 1, dm), F32)],
        args=(dy, r, mix, gt, w_out, ln_g))


def proj_bwd(parts, dres, x1, sh, sc, w_in, seq, name, job=None):
    tokens, dm = x1.shape
    tm = min(MIX_TILE, seq)
    tiles_per_seq = seq // tm
    nseq = tokens // seq
    widths = [p.shape[1] for p in parts]
    total = sum(widths)

    def body(*refs):
        part_refs = refs[:6]
        dres_ref, x_ref, sh_ref, sc_ref, w_ref, dx_ref, dproj_ref, h_ref, dmod_ref = refs[6:]
        dproj = jnp.concatenate([p[...].astype(BF16) for p in part_refs], axis=1)
        dproj_ref[...] = dproj
        dh = _dot(dproj, w_ref[...])
        xx = x_ref[...]
        one_sc = 1.0 + sc_ref[0]
        h_ref[...] = (xx * one_sc + sh_ref[0]).astype(BF16)
        dx_ref[...] = dres_ref[...] + dh * one_sc

        @pl.when(pl.program_id(0) % tiles_per_seq == 0)
        def _():
            dmod_ref[...] = jnp.zeros_like(dmod_ref)

        dmod_ref[0, 0:1, :] += jnp.sum(dh, axis=0, keepdims=True)
        dmod_ref[0, 1:2, :] += jnp.sum(dh * xx, axis=0, keepdims=True)

    tile = pl.BlockSpec((tm, dm), lambda i: (i, 0))
    mod = _mod_spec(tiles_per_seq, dm)
    return _call(
        body, job, name=name, grid=(tokens // tm,),
        in_specs=[pl.BlockSpec((tm, wdt), lambda i: (i, 0)) for wdt in widths]
        + [tile, tile, mod, mod, _const_spec(w_in.shape)],
        out_specs=[tile, pl.BlockSpec((tm, total), lambda i: (i, 0)), tile,
                   pl.BlockSpec((1, 2, dm), lambda i: (i // tiles_per_seq, 0, 0))],
        out_shape=[jax.ShapeDtypeStruct((tokens, dm), F32), jax.ShapeDtypeStruct((tokens, total), BF16),
                   jax.ShapeDtypeStruct((tokens, dm), BF16), jax.ShapeDtypeStruct((nseq, 2, dm), F32)],
        args=(*parts, dres, x1, sh, sc, w_in))


def _rope_tables(positions):
    half = ROT_DIM // 2
    inv_freq = jnp.power(jnp.float32(ROPE_THETA), -jnp.arange(0, ROT_DIM, 2, dtype=F32) / ROT_DIM)
    lane = jnp.arange(LANES) % HEAD_DIM
    freq = jnp.where(lane < ROT_DIM, inv_freq[lane % half], 0.0)
    sign = jnp.where(lane < half, -1.0, 1.0).astype(F32)
    ang = positions.astype(F32)[:, None] * freq[None, :]
    return jnp.cos(ang), sign[None, :] * jnp.sin(ang)


def kernel(x, c, positions, w_ada, b_ada, ffn1_w_gate_up, ffn1_w_down, ln1_g, ln1_b, w_in, conv_w, attn_sinks, w_out, ln2_g, ln2_b, ffn2_w_gate_up, ffn2_w_down, ln3_g, ln3_b, loss_target, m_w_ada, m_b_ada, m_ffn1_w_gate_up, m_ffn1_w_down, m_ln1_g, m_ln1_b, m_w_in, m_conv_w, m_attn_sinks, m_w_out, m_ln2_g, m_ln2_b, m_ffn2_w_gate_up, m_ffn2_w_down, m_ln3_g, m_ln3_b, v_w_ada, v_b_ada, v_ffn1_w_gate_up, v_ffn1_w_down, v_ln1_g, v_ln1_b, v_w_in, v_conv_w, v_attn_sinks, v_w_out, v_ln2_g, v_ln2_b, v_ffn2_w_gate_up, v_ffn2_w_down, v_ln3_g, v_ln3_b):
    nseq, seq, dm = x.shape
    tokens = nseq * seq
    dev = 4 * lax.axis_index("x") + 2 * lax.axis_index("y") + lax.axis_index("c")
    core = lax.axis_index("c").astype(jnp.int32).reshape(1)
    ada_cols = w_ada.shape[2]
    ff = ffn1_w_down.shape[1] * N_DEV
    fc = ff // 4
    in_cols = w_in.shape[2]
    conv_cols = conv_w.shape[2]

    def t_bf16(w):
        return w[0].T.astype(BF16)

    c_all, convw_all = all_gather([c, conv_w[0]], "gather_cond")
    c_all = c_all.reshape(N_DEV * nseq, dm)
    convw_full = convw_all.transpose(1, 0, 2).reshape(CONV_TAPS, N_DEV * conv_cols)

    b_cols = lax.dynamic_slice(b_ada, (0, dev * ada_cols), (1, ada_cols))
    cond_all, mod_cols = ada_fwd(c_all, w_ada[0], b_cols, "ada_fwd")
    wgu1, mod_all = all_gather([t_bf16(ffn1_w_gate_up), mod_cols], "gather_ffn1")
    wgu1 = wgu1.reshape(2, ff, dm)
    mod = lax.dynamic_slice(mod_all, (0, dev * nseq, 0), (N_DEV, nseq, ada_cols))
    mod = mod.transpose(1, 0, 2).reshape(nseq, 9, 1, dm)
    sh1, sc1, g1, sh2, sc2, g2, sh3, sc3, g3 = [mod[:, i] for i in range(9)]

    x0 = x.reshape(tokens, dm)
    (gu1, a1_fwd), (wd1, wout) = ffn_up(x0, sh1, sc1, wgu1, seq, "ffn1_up",
                                        job=_GatherJob([ffn1_w_down[0].astype(BF16), w_out[0].astype(BF16)]))
    wd1, wout = wd1.reshape(ff, dm), wout.reshape(dm, dm)
    (x1, r1, f1), (win,) = ffn_down(x0, a1_fwd, g1, wd1, ln1_g, ln1_b, seq, "ffn1_down", job=_GatherJob([t_bf16(w_in)]))
    win = win.reshape(N_DEV * in_cols, dm)
    (q, k, v, u, bg, cg), wd2_spread = proj_fwd(x1, sh2, sc2, win, seq, "proj_fwd",
                                                job=gather_spread_job([ffn2_w_down[0].astype(BF16)]))
    cos_t, sin_t = _rope_tables(positions.reshape(tokens))
    sinks = attn_sinks[0]
    (attn, q_rot, probs, sink_probs), wgu2_spread = attn_fwd(q, k, v, cos_t, sin_t, sinks, seq, "attn_fwd",
                                                             job=gather_spread_job([t_bf16(ffn2_w_gate_up)]))
    conv = conv_fwd(u, bg, cg, convw_full, seq, "conv_fwd")
    (x2, r2, mixin, mix), (wd2, wgu2) = out_fwd(x1, attn, conv, g2, wout, ln2_g, ln2_b, seq, "out_fwd",
                                                job=gather_forward_job(wd2_spread + wgu2_spread))
    wd2, wgu2 = wd2.reshape(ff, dm), wgu2.reshape(2, ff, dm)
    target = loss_target.reshape(tokens, dm)
    (dy3, loss_part, r3, gu3, f3), _ = ffn_fwd(x2, sh3, sc3, g3, wgu2, wd2, ln3_g, ln3_b, seq, "ffn2_fwd", target=target)

    (dx2, dgu3, df3, a3, h3, dln3, dmod3), _ = ffn_bwd(dy3, r3, x2, f3, gu3, sh3, sc3, g3, wgu2, wd2, ln3_g, seq, "ffn2_bwd")
    pair = 2 * fc
    g_wd2 = tn_matmul(a3[None], df3[None], "ffn2_dwd", a_width=pair)[0][0].reshape(N_DEV, ff // N_DEV, dm)
    g_wgu2 = tn_matmul(dgu3, h3[None], "ffn2_dwgu", a_width=pair)[0][0].reshape(N_DEV, fc, dm)
    (dres2, dattn, dconv, dmix, dln2, dg2), swapped = out_bwd(dx2, r2, mix, g2, wout, ln2_g, seq, "out_bwd",
                                                              job=swap_job([g_wgu2, g_wd2]))
    p_wgu2, own_wgu2 = pair_sum(core, g_wgu2, swapped[0], "pair_wgu2")
    p_wd2, own_wd2 = pair_sum(core, g_wd2, swapped[1], "pair_wd2")
    du, dbg, dcg, dconvw = conv_bwd(dconv, u, bg, cg, convw_full, seq, "conv_bwd")
    (dq, dk, dv, dsink_rows), (far_wd2,) = attn_bwd(
        q_rot, k, v, dattn, probs, sink_probs, cos_t, sin_t, seq, "attn_bwd", job=chip_exchange_job([p_wd2]))
    parts = [dq, dk, dv, du, dbg, dcg]
    (dx1, dproj, h2, dmod2), far_top = proj_bwd(parts, dres2, x1, sh2, sc2, win, seq, "proj_bwd",
                                                job=chip_exchange_job([p_wgu2], rows=(0, fc // 2)))
    (dx0, dgu1, df1, a1, h1, dln1, dmod1), _ = ffn_bwd(
        dx1, r1, x0, f1, gu1, sh1, sc1, g1, wgu1, wd1, ln1_g, seq, "ffn1_bwd")

    dmod = jnp.concatenate([dmod1, dmod2, dg2, dmod3], axis=1).reshape(nseq, 9 * dm)
    half = dm // 2
    jobs = _Jobs([gather_spread_job([dmod]),
                  chip_exchange_job([p_wgu2], rows=(fc // 2, fc // 2), into=far_top)])
    (g_wd1,), res = tn_matmul(a1[None], df1[None], "ffn1_dwd", job=jobs, a_width=pair)
    dmod_spread, (far_wgu2,) = jobs.split(res)
    g_wd1 = g_wd1.reshape(N_DEV, ff // N_DEV, dm)
    jobs = _Jobs([swap_job([g_wd1]), gather_forward_job(dmod_spread)])
    (g_l,), res = tn_matmul(dgu1, h1[None], "ffn1_dwgu_l", job=jobs, b_cols=(0, half), a_width=pair)
    (sw_wd1,), (dmod_all,) = jobs.split(res)
    g_l = g_l.reshape(N_DEV, fc, half)
    p_wd1, own_wd1 = pair_sum(core, g_wd1, sw_wd1, "pair_wd1")
    jobs = _Jobs([chip_exchange_job([p_wd1]), swap_job([g_l])])
    (g_r,), res = tn_matmul(dgu1, h1[None], "ffn1_dwgu_r", job=jobs, b_cols=(1, half), a_width=pair)
    (far_wd1,), (sw_l,) = jobs.split(res)
    g_r = g_r.reshape(N_DEV, fc, half)
    p_l, own_l = pair_sum(core, g_l, sw_l, "pair_wgu1_l")

    dmod_cols = lax.dynamic_slice(dmod_all.reshape(N_DEV * nseq, 9 * dm), (0, dev * ada_cols), (N_DEV * nseq, ada_cols))
    grad_w_ada, gb_cols = ada_bwd(cond_all, dmod_cols, "ada_bwd")
    dsinks = jnp.sum(dsink_rows.reshape(nseq, N_Q_HEADS, ATTN_BLOCK), axis=(0, 2))
    small = jnp.zeros((8, dm), F32)
    small = small.at[0:2].set(dln1).at[2:4].set(dln2).at[4:6].set(dln3)
    small = small.at[6, 0:N_Q_HEADS].set(dsinks).at[7, 0].set(loss_part[0, 0])

    jobs = _Jobs([chip_exchange_job([p_l]), swap_job([g_r]), gather_spread_job([small, dconvw, gb_cols])])
    (g_win,), res = tn_matmul(dproj[None], h2[None], "dwin", job=jobs)
    (far_l,), (sw_r,), small_spread = jobs.split(res)
    g_win = g_win.reshape(N_DEV, in_cols, dm)
    p_r, own_r = pair_sum(core, g_r, sw_r, "pair_wgu1_r")
    jobs = _Jobs([chip_exchange_job([p_r]), swap_job([g_win]), gather_forward_job(small_spread)])
    (g_wout,), res = tn_matmul(mixin[None], dmix[None], "dwout", job=jobs)
    (far_r,), (sw_win,), (small_all, dconvw_all, gb_all) = jobs.split(res)
    g_wout = g_wout.reshape(N_DEV, dm // N_DEV, dm)
    p_win, own_win = pair_sum(core, g_win, sw_win, "pair_win")
    jobs = _Jobs([chip_exchange_job([p_win]), swap_job([g_wout])])
    (far_win,), (sw_wout,) = jobs.split(run_job(jobs, "rs_tail_win"))
    p_wout, own_wout = pair_sum(core, g_wout, sw_wout, "pair_wout")
    (far_wout,) = run_job(chip_exchange_job([p_wout]), "rs_tail_wout")

    grads = {
        "ffn1_w_gate_up": jnp.concatenate([own_l, own_r], axis=1), "ffn1_w_down": own_wd1,
        "w_in": own_win, "w_out": own_wout, "ffn2_w_gate_up": own_wgu2, "ffn2_w_down": own_wd2,
    }
    others = {"ffn1_w_gate_up": jnp.concatenate([far_l, far_r], axis=2), "ffn1_w_down": far_wd1,
              "w_in": far_win, "w_out": far_wout, "ffn2_w_gate_up": far_wgu2, "ffn2_w_down": far_wd2}

    grads["w_ada"] = grad_w_ada
    small_sum = sum_devices(small_all, "sum_small")
    dconvw_sum = sum_devices(dconvw_all, "sum_convw")
    loss = small_sum[7, 0]
    grads["b_ada"] = gb_all.reshape(1, N_DEV * ada_cols)
    grads["conv_w"] = lax.dynamic_slice(dconvw_sum, (0, dev * conv_cols), (CONV_TAPS, conv_cols))
    grads["attn_sinks"] = small_sum[6:7, 0:N_Q_HEADS]
    for i, nm in enumerate(["ln1_g", "ln1_b", "ln2_g", "ln2_b", "ln3_g", "ln3_b"]):
        grads[nm] = small_sum[i:i + 1]

    given = dict(w_ada=(w_ada, m_w_ada, v_w_ada), b_ada=(b_ada, m_b_ada, v_b_ada),
                 ffn1_w_gate_up=(ffn1_w_gate_up, m_ffn1_w_gate_up, v_ffn1_w_gate_up),
                 ffn1_w_down=(ffn1_w_down, m_ffn1_w_down, v_ffn1_w_down),
                 ln1_g=(ln1_g, m_ln1_g, v_ln1_g), ln1_b=(ln1_b, m_ln1_b, v_ln1_b),
                 w_in=(w_in, m_w_in, v_w_in), conv_w=(conv_w, m_conv_w, v_conv_w),
                 attn_sinks=(attn_sinks, m_attn_sinks, v_attn_sinks), w_out=(w_out, m_w_out, v_w_out),
                 ln2_g=(ln2_g, m_ln2_g, v_ln2_g), ln2_b=(ln2_b, m_ln2_b, v_ln2_b),
                 ffn2_w_gate_up=(ffn2_w_gate_up, m_ffn2_w_gate_up, v_ffn2_w_gate_up),
                 ffn2_w_down=(ffn2_w_down, m_ffn2_w_down, v_ffn2_w_down),
                 ln3_g=(ln3_g, m_ln3_g, v_ln3_g), ln3_b=(ln3_b, m_ln3_b, v_ln3_b))
    order = ["w_ada", "b_ada", "ffn1_w_gate_up", "ffn1_w_down", "ln1_g", "ln1_b", "w_in", "conv_w", "attn_sinks",
             "w_out", "ln2_g", "ln2_b", "ffn2_w_gate_up", "ffn2_w_down", "ln3_g", "ln3_b"]
    transposed = ("ffn1_w_gate_up", "ffn2_w_gate_up", "w_in")
    big = ("w_ada", "ffn1_w_gate_up", "ffn1_w_down", "w_in", "w_out", "ffn2_w_gate_up", "ffn2_w_down")
    results = {}
    for nm in big:
        if nm in transposed:
            w2, m2, v2 = [t[0].T for t in given[nm]]
            results[nm] = [t.T[None] for t in adamw(w2, grads[nm], m2, v2, "adamw_" + nm, others=others.get(nm))]
        else:
            w2, m2, v2 = [t[0] for t in given[nm]]
            results[nm] = [t[None] for t in adamw(w2, grads[nm], m2, v2, "adamw_" + nm, others=others.get(nm))]
    small_names = [nm for nm in order if nm not in big]
    items = []
    for nm in small_names:
        shape = given[nm][0].shape
        two_d = (shape[-2], shape[-1])
        items.append((given[nm][0].reshape(two_d), grads[nm].reshape(two_d), *[t.reshape(two_d) for t in given[nm][1:]]))
    for nm, res in zip(small_names, adamw_small(items, "adamw_small")):
        shape = given[nm][0].shape
        results[nm] = [grads[nm].reshape(shape)] + [t.reshape(shape) for t in res]
    grad_x = dx0.reshape(nseq, seq, dm)
    return (loss, grad_x, *[results[nm][i] for i in range(4) for nm in order])
```

```python
import functools

import jax
import jax.numpy as jnp
from jax import lax
from jax.experimental import pallas as pl
from jax.experimental.pallas import tpu as pltpu

F32 = jnp.float32
BF16 = jnp.bfloat16
MESH = pl.DeviceIdType.MESH

N_DEV = 8
N_CHIP = 4
HEAD_DIM = 64
N_Q_HEADS = 8
N_KV_HEADS = 2
GQA_GROUP = N_Q_HEADS // N_KV_HEADS
ATTN_BLOCK = 128
ROT_DIM = 16
ROPE_THETA = 500000.0
CONV_TAPS = 3
LN_EPS = 1e-5
DN_ALPHA = 2.0 ** 0.25
ADAM_LR = 0.001
ADAM_B1 = 0.9
ADAM_B2 = 0.999
ADAM_EPS = 1e-08
ADAM_WD = 0.01
ADAM_STEP = 10
NEG_BIG = -1e30

VMEM_LIMIT = 56 * 1024 * 1024
TOKEN_TILE = 256
FFN_FWD_TILE = 512
MIX_TILE = 512
TN_VMEM_BUDGET = 36 * 1024 * 1024


def _params(semantics=None, vmem=VMEM_LIMIT):
    return pltpu.CompilerParams(dimension_semantics=semantics, vmem_limit_bytes=vmem)


def _dot(a, b):
    return jnp.dot(a, b, preferred_element_type=F32)


def _dot_nt(a, b):
    return lax.dot_general(a, b, (((1,), (1,)), ((), ())), preferred_element_type=F32)


def _dot_tn(a, b):
    return lax.dot_general(a, b, (((0,), (0,)), ((), ())), preferred_element_type=F32)


def _sigmoid(x):
    return pl.reciprocal(1.0 + jnp.exp(-x), approx=True)


def _ln_stats(r):
    mu = jnp.mean(r, axis=-1, keepdims=True)
    d = r - mu
    var = jnp.mean(d * d, axis=-1, keepdims=True)
    rstd = lax.rsqrt(var + LN_EPS)
    return d * rstd, rstd


def _ln_bwd(dy, r, g):
    return _ln_bwd_normalized(dy, *_ln_stats(r), g)


def _ln_bwd_normalized(dy, xhat, rstd, g):
    dxhat = dy * g
    c1 = jnp.mean(dxhat, axis=-1, keepdims=True)
    c2 = jnp.mean(dxhat * xhat, axis=-1, keepdims=True)
    dr = rstd * (dxhat - c1 - xhat * c2)
    return dr, jnp.sum(dy * xhat, axis=0, keepdims=True), jnp.sum(dy, axis=0, keepdims=True)


def _const_spec(shape):
    nd = len(shape)
    return pl.BlockSpec(shape, lambda *_: (0,) * nd, pipeline_mode=pl.Buffered(1))


def all_gather(arrs, name):
    n = len(arrs)

    def body(*refs):
        ins, outs = refs[:n], refs[n:2 * n]
        send_sems, recv_sems, local_sems = refs[2 * n:]
        x, y, c = lax.axis_index("x"), lax.axis_index("y"), lax.axis_index("c")
        me, sibling = (x, y, c), (x, y, 1 - c)
        chips = [(1 - x, y), (x, 1 - y), (1 - x, 1 - y)]

        def slot(i, p):
            return outs[i].at[4 * p[0] + 2 * p[1] + p[2]]

        def copy(i, k, block, to, src=None):
            return pltpu.make_async_remote_copy(
                src_ref=slot(i, block) if src is None else src, dst_ref=slot(i, block),
                send_sem=send_sems.at[i, k], recv_sem=recv_sems.at[i, k],
                device_id=to, device_id_type=MESH)

        mine = [pltpu.make_async_copy(ins[i], slot(i, me), local_sems.at[i]) for i in range(n)]
        for cp in mine:
            cp.start()
        first = []
        for i in range(n):
            first.append(copy(i, 0, me, sibling, src=ins[i]))
            first += [copy(i, 1 + j, me, (*chip, c), src=ins[i]) for j, chip in enumerate(chips)]
        for cp in first:
            cp.start()
        passed = []
        for i in range(n):
            for j, chip in enumerate(chips):
                copy(i, 1 + j, (*chip, c), me).wait_recv()
                cp = copy(i, 4 + j, (*chip, c), sibling)
                cp.start()
                passed.append(cp)
        for i in range(n):
            copy(i, 0, sibling, me).wait_recv()
            for j, chip in enumerate(chips):
                copy(i, 4 + j, (*chip, 1 - c), me).wait_recv()
        for cp in first + passed:
            cp.wait_send()
        for cp in mine:
            cp.wait()

    any_spec = pl.BlockSpec(memory_space=pl.ANY)
    return pl.pallas_call(
        body, name=name,
        out_shape=[jax.ShapeDtypeStruct((N_DEV, *a.shape), a.dtype) for a in arrs],
        in_specs=[any_spec] * n, out_specs=[any_spec] * n,
        scratch_shapes=[pltpu.SemaphoreType.DMA((n, 7)), pltpu.SemaphoreType.DMA((n, 7)),
                        pltpu.SemaphoreType.DMA((n,))],
    )(*arrs)


def _place():
    x, y, c = lax.axis_index("x"), lax.axis_index("y"), lax.axis_index("c")
    return x, y, c, [(1 - x, y), (x, 1 - y), (1 - x, 1 - y)]


def _slot(p):
    return 4 * p[0] + 2 * p[1] + p[2]


class _Job:
    def __init__(self, ins, outs, nsem, copies, aliases=None, local=None):
        self.ins, self.outs, self.nsem, self.copies = list(ins), list(outs), nsem, copies
        self.aliases = aliases or {}
        self.local = local

    def scratch(self):
        s = [pltpu.SemaphoreType.DMA(self.nsem), pltpu.SemaphoreType.DMA(self.nsem)]
        if self.local is not None:
            s.append(pltpu.SemaphoreType.DMA((len(self.ins),)))
        return s

    def start(self, ins, outs, sems):
        if self.local is not None:
            for cp in self.local(ins, outs, sems[2]):
                cp.start()
        for cp in self.copies(ins, outs, sems[0], sems[1])[0]:
            cp.start()

    def finish(self, ins, outs, sems):
        started, awaited = self.copies(ins, outs, sems[0], sems[1])
        for cp in awaited:
            cp.wait_recv()
        for cp in started:
            cp.wait_send()
        if self.local is not None:
            for cp in self.local(ins, outs, sems[2]):
                cp.wait()


class _Jobs:
    def __init__(self, jobs):
        self.jobs = jobs
        self.ins = [a for j in jobs for a in j.ins]
        self.outs = [o for j in jobs for o in j.outs]
        self.two_phase = any(getattr(j, "two_phase", False) for j in jobs)
        self.aliases = {}
        at_in = at_out = 0
        for j in jobs:
            self.aliases.update({at_in + i: at_out + o for i, o in j.aliases.items()})
            at_in, at_out = at_in + len(j.ins), at_out + len(j.outs)

    def scratch(self):
        return [s for j in self.jobs for s in j.scratch()]

    def _each(self, ins, outs, sems):
        at_in = at_out = at_sem = 0
        for j in self.jobs:
            n_in, n_out, n_sem = len(j.ins), len(j.outs), len(j.scratch())
            yield j, ins[at_in:at_in + n_in], outs[at_out:at_out + n_out], sems[at_sem:at_sem + n_sem]
            at_in, at_out, at_sem = at_in + n_in, at_out + n_out, at_sem + n_sem

    def start(self, ins, outs, sems):
        for j, i, o, s in self._each(ins, outs, sems):
            j.start(i, o, s)

    def turn(self, ins, outs, sems):
        for j, i, o, s in self._each(ins, outs, sems):
            if getattr(j, "two_phase", False):
                j.turn(i, o, s)

    def finish(self, ins, outs, sems):
        for j, i, o, s in self._each(ins, outs, sems):
            j.finish(i, o, s)

    def split(self, results):
        at, parts = 0, []
        for j in self.jobs:
            parts.append(results[at:at + len(j.outs)])
            at += len(j.outs)
        return parts


def _remote(src, dst, send, recv, idx, to):
    return pltpu.make_async_remote_copy(src_ref=src, dst_ref=dst, send_sem=send.at[idx], recv_sem=recv.at[idx],
                                        device_id=to, device_id_type=MESH)


def _spread_copies(ins, outs, send, recv, base=0):
    x, y, c, chips = _place()
    me = (x, y, c)
    peers = [(x, y, 1 - c)] + [(*chip, c) for chip in chips]
    started, awaited = [], []
    for i, (src, dst) in enumerate(zip(ins, outs)):
        for k, peer in enumerate(peers):
            started.append(_remote(src, dst.at[_slot(me)], send, recv, (base + i, k), peer))
            awaited.append(_remote(src, dst.at[_slot(peer)], send, recv, (base + i, k), peer))
    return started, awaited


def _forward_copies(ins, outs, send, recv, base=0):
    x, y, c, chips = _place()
    started, awaited = [], []
    for i, buf in enumerate(outs):
        for j, chip in enumerate(chips):
            mine, theirs = buf.at[_slot((*chip, c))], buf.at[_slot((*chip, 1 - c))]
            started.append(_remote(mine, mine, send, recv, (base + i, j), (x, y, 1 - c)))
            awaited.append(_remote(theirs, theirs, send, recv, (base + i, j), (x, y, 1 - c)))
    return started, awaited


def _own_block_copies(ins, outs, sems):
    x, y, c, _ = _place()
    return [pltpu.make_async_copy(src, dst.at[_slot((x, y, c))], sems.at[i])
            for i, (src, dst) in enumerate(zip(ins, outs))]


def gather_spread_job(shards):
    outs = [jax.ShapeDtypeStruct((N_DEV, *a.shape), a.dtype) for a in shards]
    return _Job(shards, outs, (len(shards), 4), _spread_copies, local=_own_block_copies)


def gather_forward_job(fulls):
    outs = [jax.ShapeDtypeStruct(a.shape, a.dtype) for a in fulls]
    return _Job(fulls, outs, (len(fulls), 3), _forward_copies, aliases={i: i for i in range(len(fulls))})


TURN_EIGHTHS = 6


class _GatherJob:
    two_phase = True

    def __init__(self, shards):
        self.ins = list(shards)
        self.outs = [jax.ShapeDtypeStruct((N_DEV, *a.shape), a.dtype) for a in shards]
        self.aliases = {}

    def scratch(self):
        n = len(self.ins)
        return [pltpu.SemaphoreType.DMA((n, 4)), pltpu.SemaphoreType.DMA((n, 4)),
                pltpu.SemaphoreType.DMA((n, 3)), pltpu.SemaphoreType.DMA((n, 3)), pltpu.SemaphoreType.DMA((n,))]

    def start(self, ins, outs, sems):
        for cp in _own_block_copies(ins, outs, sems[4]) + _spread_copies(ins, outs, sems[0], sems[1])[0]:
            cp.start()

    def turn(self, ins, outs, sems):
        for cp in _spread_copies(ins, outs, sems[0], sems[1])[1]:
            cp.wait_recv()
        for cp in _forward_copies(outs, outs, sems[2], sems[3])[0]:
            cp.start()

    def finish(self, ins, outs, sems):
        handed_on, arriving = _forward_copies(outs, outs, sems[2], sems[3])
        for cp in arriving:
            cp.wait_recv()
        for cp in _spread_copies(ins, outs, sems[0], sems[1])[0] + handed_on:
            cp.wait_send()
        for cp in _own_block_copies(ins, outs, sems[4]):
            cp.wait()


def swap_job(gs):
    def copies(ins, outs, send, recv):
        x, y, c, _ = _place()
        started, awaited = [], []
        for i, (g, r1) in enumerate(zip(ins, outs)):
            for q in range(N_CHIP):
                started.append(_remote(g.at[2 * q + (1 - c)], r1.at[q], send, recv, (i, q), (x, y, 1 - c)))
                awaited.append(_remote(g.at[2 * q + c], r1.at[q], send, recv, (i, q), (x, y, 1 - c)))
        return started, awaited

    outs = [jax.ShapeDtypeStruct((N_CHIP, *g.shape[1:]), g.dtype) for g in gs]
    return _Job(gs, outs, (len(gs), N_CHIP), copies)


def chip_exchange_job(ps, rows=None, into=None):
    n = len(ps)

    def copies(ins, outs, send, recv):
        x, y, c, chips = _place()
        started, awaited = [], []
        for i, (p, r2) in enumerate(zip(ins[:n], outs)):
            for k, chip in enumerate(chips):
                src, mine, dst = p.at[2 * chip[0] + chip[1]], p.at[2 * x + y], r2.at[k]
                if rows is not None:
                    src, mine, dst = (t.at[pl.ds(rows[0], rows[1])] for t in (src, mine, dst))
                started.append(_remote(src, dst, send, recv, (i, k), (*chip, c)))
                awaited.append(_remote(mine, dst, send, recv, (i, k), (*chip, c)))
        return started, awaited

    outs = [jax.ShapeDtypeStruct((3, *p.shape[1:]), p.dtype) for p in ps]
    if into is None:
        return _Job(ps, outs, (n, 3), copies)
    return _Job(list(ps) + list(into), outs, (n, 3), copies, aliases={n + i: i for i in range(n)})


def _call(body, job, *, name, grid, in_specs, out_specs, out_shape, args, scratch_shapes=(), vmem=VMEM_LIMIT):
    if job is None:
        res = pl.pallas_call(
            body, name=name, grid=grid, in_specs=in_specs, out_specs=out_specs, out_shape=out_shape,
            scratch_shapes=list(scratch_shapes), compiler_params=_params(("arbitrary",) * len(grid), vmem),
        )(*args)
        return res, []
    n_in, n_out, n_scr = len(in_specs), len(out_specs), len(scratch_shapes)
    j_in, j_out = len(job.ins), len(job.outs)

    def with_copies(*refs):
        at = 0
        ins = refs[at:at + n_in]; at += n_in
        jins = refs[at:at + j_in]; at += j_in
        outs = refs[at:at + n_out]; at += n_out
        jouts = refs[at:at + j_out]; at += j_out
        scr = refs[at:at + n_scr]; at += n_scr
        sems = refs[at:]
        ids = [pl.program_id(d) for d in range(len(grid))]
        first = functools.reduce(jnp.logical_and, [i == 0 for i in ids])
        last = functools.reduce(jnp.logical_and, [i == n - 1 for i, n in zip(ids, grid)])

        @pl.when(first)
        def _():
            job.start(jins, jouts, sems)

        if getattr(job, "two_phase", False):
            steps, at = 1, 0
            for i, n in zip(ids, grid):
                steps, at = steps * n, at * n + i

            @pl.when(at == (TURN_EIGHTHS * steps) // 8)
            def _():
                job.turn(jins, jouts, sems)

        body(*ins, *outs, *scr)

        @pl.when(last)
        def _():
            job.finish(jins, jouts, sems)

    any_spec = pl.BlockSpec(memory_space=pl.ANY)
    res = pl.pallas_call(
        with_copies, name=name, grid=grid,
        in_specs=list(in_specs) + [any_spec] * j_in, out_specs=list(out_specs) + [any_spec] * j_out,
        out_shape=list(out_shape) + list(job.outs),
        input_output_aliases={n_in + i: n_out + o for i, o in job.aliases.items()},
        scratch_shapes=list(scratch_shapes) + job.scratch(),
        compiler_params=_params(("arbitrary",) * len(grid), vmem),
    )(*args, *job.ins)
    return res[:n_out], res[n_out:]


def run_job(job, name):
    def body(*refs):
        j_in, j_out = len(job.ins), len(job.outs)
        ins, outs, sems = refs[:j_in], refs[j_in:j_in + j_out], refs[j_in + j_out:]
        job.start(ins, outs, sems)
        job.finish(ins, outs, sems)

    any_spec = pl.BlockSpec(memory_space=pl.ANY)
    return pl.pallas_call(
        body, name=name, in_specs=[any_spec] * len(job.ins), out_specs=[any_spec] * len(job.outs),
        out_shape=list(job.outs), input_output_aliases=dict(job.aliases), scratch_shapes=job.scratch(),
    )(*job.ins)


def pair_sum(core, g, r1, name):
    _, rows, cols = g.shape
    rb = next(cand for cand in range(min(rows, 512), 0, -16) if rows % cand == 0)

    def body(core_ref, g_ref, r1_ref, p_ref, own_ref):
        del core_ref
        x, y, _, _ = _place()
        s = g_ref[0].astype(F32) + r1_ref[0].astype(F32)
        p_ref[0] = s.astype(BF16)

        @pl.when(pl.program_id(1) == 2 * x + y)
        def _():
            own_ref[...] = s

    chunk = (1, rb, cols)
    return pl.pallas_call(
        body, name=name,
        grid_spec=pltpu.PrefetchScalarGridSpec(
            num_scalar_prefetch=1, grid=(rows // rb, N_CHIP),
            in_specs=[pl.BlockSpec(chunk, lambda i, q, core_ref: (2 * q + core_ref[0], i, 0)),
                      pl.BlockSpec(chunk, lambda i, q, core_ref: (q, i, 0))],
            out_specs=[pl.BlockSpec(chunk, lambda i, q, core_ref: (q, i, 0)),
                       pl.BlockSpec((rb, cols), lambda i, q, core_ref: (i, 0))]),
        out_shape=[jax.ShapeDtypeStruct((N_CHIP, rows, cols), BF16), jax.ShapeDtypeStruct((rows, cols), F32)],
        compiler_params=_params(("arbitrary", "arbitrary")),
    )(core, g, r1)


def sum_devices(a, name):
    def body(a_ref, o_ref):
        acc = a_ref[0]
        for d in range(1, N_DEV):
            acc = acc + a_ref[d]
        o_ref[...] = acc

    return pl.pallas_call(body, name=name, out_shape=jax.ShapeDtypeStruct(a.shape[1:], F32))(a)


def _adam_update(w, g, m, v):
    nm = ADAM_B1 * m + (1.0 - ADAM_B1) * g
    nv = ADAM_B2 * v + (1.0 - ADAM_B2) * (g * g)
    m_hat = nm / (1.0 - ADAM_B1 ** ADAM_STEP)
    v_hat = nv / (1.0 - ADAM_B2 ** ADAM_STEP)
    return -ADAM_LR * (m_hat / (jnp.sqrt(v_hat) + ADAM_EPS) + ADAM_WD * w), nm, nv


def adamw(w, g, m, v, name, others=None):
    rows, cols = w.shape
    rb = rows
    for cand in range(min(rows, 512), 7, -8):
        if rows % cand == 0 and cand % 8 == 0:
            rb = cand
            break

    def body(*refs):
        if others is None:
            w_ref, g_ref, m_ref, v_ref, d_ref, nm_ref, nv_ref = refs
            gg = g_ref[...]
        else:
            w_ref, g_ref, m_ref, v_ref, r2_ref, go_ref, d_ref, nm_ref, nv_ref = refs
            gg = g_ref[...]
            for k in range(3):
                gg = gg + r2_ref[k].astype(F32)
            go_ref[...] = gg
        d_ref[...], nm_ref[...], nv_ref[...] = _adam_update(w_ref[...], gg, m_ref[...], v_ref[...])

    spec = pl.BlockSpec((rb, cols), lambda i: (i, 0))
    out = jax.ShapeDtypeStruct((rows, cols), F32)
    in_specs, args = [spec] * 4, [w, g, m, v]
    if others is not None:
        in_specs.append(pl.BlockSpec((3, rb, cols), lambda i: (0, i, 0)))
        args.append(others)
    n_out = 3 if others is None else 4
    res = pl.pallas_call(
        body, name=name, grid=(rows // rb,), in_specs=in_specs, out_specs=[spec] * n_out,
        out_shape=[out] * n_out, compiler_params=_params(("parallel",)),
    )(*args)
    return (g, *res) if others is None else tuple(res)


def adamw_small(items, name):
    n = len(items)

    def body(*refs):
        ins, outs = refs[:4 * n], refs[4 * n:]
        for i in range(n):
            w_ref, g_ref, m_ref, v_ref = ins[4 * i:4 * i + 4]
            d_ref, nm_ref, nv_ref = outs[3 * i:3 * i + 3]
            d_ref[...], nm_ref[...], nv_ref[...] = _adam_update(w_ref[...], g_ref[...], m_ref[...], v_ref[...])

    res = pl.pallas_call(
        body, name=name,
        out_shape=[jax.ShapeDtypeStruct(w.shape, F32) for w, _, _, _ in items for _ in range(3)],
    )(*[t for item in items for t in item])
    return [tuple(res[3 * i:3 * i + 3]) for i in range(n)]


def ada_fwd(c_all, w_cols, b_cols, name):
    def body(c_ref, w_ref, b_ref, cond_ref, mod_ref):
        cc = c_ref[...]
        cond = (cc * _sigmoid(cc)).astype(BF16)
        cond_ref[...] = cond
        mod_ref[...] = _dot(cond, w_ref[...].astype(BF16)) + b_ref[...]

    n, cols = c_all.shape[0], w_cols.shape[1]
    return pl.pallas_call(
        body, name=name,
        out_shape=[jax.ShapeDtypeStruct(c_all.shape, BF16), jax.ShapeDtypeStruct((n, cols), F32)],
        compiler_params=_params(),
    )(c_all, w_cols, b_cols)


def ada_bwd(cond_all, dmod_cols, name):
    def body(c_ref, d_ref, gw_ref, gb_ref):
        d = d_ref[...]
        gw_ref[...] = _dot_tn(c_ref[...], d.astype(BF16))
        gb_ref[...] = jnp.sum(d, axis=0, keepdims=True)

    dm, cols = cond_all.shape[1], dmod_cols.shape[1]
    return pl.pallas_call(
        body, name=name,
        out_shape=[jax.ShapeDtypeStruct((dm, cols), F32), jax.ShapeDtypeStruct((1, cols), F32)],
        compiler_params=_params(),
    )(cond_all, dmod_cols)


MXU_COLS = 256
FFN_CHUNK = 4 * MXU_COLS


def _hidden_chunks(ff):
    assert ff % MXU_COLS == 0
    return [(at, min(FFN_CHUNK, ff - at)) for at in range(0, ff, FFN_CHUNK)]


def _mod_spec(tiles_per_seq, dm):
    return pl.BlockSpec((1, 1, dm), lambda i: (i // tiles_per_seq, 0, 0))


def ffn_loss(x, sh, sc, gt, wgu, wd, ln_g, ln_b, target, seq, name):
    tokens, dm = x.shape
    ff = wgu.shape[1]
    chunks = _hidden_chunks(ff)
    tm = min(FFN_FWD_TILE, seq)
    tiles_per_seq = seq // tm

    def body(x_ref, sh_ref, sc_ref, gt_ref, wgu_ref, wd_ref, lg_ref, lb_ref, t_ref,
             dr_ref, df_ref, gu_ref, loss_ref, dln_ref, dgt_ref):
        i = pl.program_id(0)
        xx = x_ref[...]
        h = (xx * (1.0 + sc_ref[0]) + sh_ref[0]).astype(BF16)
        acc = jnp.zeros((tm, dm), F32)
        for at, wdt in chunks:
            gk = _dot_nt(h, wgu_ref[0, at:at + wdt, :])
            uk = _dot_nt(h, wgu_ref[1, at:at + wdt, :])
            gu_ref[0, :, at:at + wdt] = gk.astype(BF16)
            gu_ref[1, :, at:at + wdt] = uk.astype(BF16)
            a = (gk * _sigmoid(gk) * uk).astype(BF16)
            acc = acc + _dot(a, wd_ref[at:at + wdt, :])
        half_gate = 0.5 * (1.0 + gt_ref[0])
        xhat, rstd = _ln_stats(DN_ALPHA * xx + half_gate * acc)
        err = xhat * lg_ref[...] + lb_ref[...] - t_ref[...]
        dr, dgain, dbias = _ln_bwd_normalized(err * (1.0 / dm), xhat, rstd, lg_ref[...])
        dr_ref[...] = dr
        df_ref[...] = (half_gate * dr).astype(BF16)

        @pl.when(i == 0)
        def _():
            loss_ref[...] = jnp.zeros_like(loss_ref)
            dln_ref[...] = jnp.zeros_like(dln_ref)

        @pl.when(i % tiles_per_seq == 0)
        def _():
            dgt_ref[...] = jnp.zeros_like(dgt_ref)

        loss_ref[...] += jnp.full((1, 128), (0.5 / dm) * jnp.sum(err * err), F32)
        dln_ref[0:1, :] += dgain
        dln_ref[1:2, :] += dbias
        dgt_ref[0] += jnp.sum(dr * (0.5 * acc), axis=0, keepdims=True)

    tile = pl.BlockSpec((tm, dm), lambda i: (i, 0))
    mod = _mod_spec(tiles_per_seq, dm)
    res, _ = _call(
        body, None, name=name, grid=(tokens // tm,),
        in_specs=[tile, mod, mod, mod, _const_spec(wgu.shape), _const_spec(wd.shape),
                  _const_spec((1, dm)), _const_spec((1, dm)), tile],
        out_specs=[tile, tile, pl.BlockSpec((2, tm, ff), lambda i: (0, i, 0)), pl.BlockSpec((1, 128), lambda i: (0, 0)),
                   pl.BlockSpec((2, dm), lambda i: (0, 0)), mod],
        out_shape=[jax.ShapeDtypeStruct((tokens, dm), F32), jax.ShapeDtypeStruct((tokens, dm), BF16),
                   jax.ShapeDtypeStruct((2, tokens, ff), BF16), jax.ShapeDtypeStruct((1, 128), F32),
                   jax.ShapeDtypeStruct((2, dm), F32), jax.ShapeDtypeStruct((tokens // seq, 1, dm), F32)],
        args=(x, sh, sc, gt, wgu, wd, ln_g, ln_b, target))
    return res


def ffn_up(x, sh, sc, wgu, seq, name, job=None):
    tokens, dm = x.shape
    ff = wgu.shape[1]
    chunks = _hidden_chunks(ff)
    tm = min(FFN_FWD_TILE, seq)

    def body(x_ref, sh_ref, sc_ref, wgu_ref, gu_ref, a_ref):
        h = (x_ref[...] * (1.0 + sc_ref[0]) + sh_ref[0]).astype(BF16)
        for at, wdt in chunks:
            gk = _dot_nt(h, wgu_ref[0, at:at + wdt, :])
            uk = _dot_nt(h, wgu_ref[1, at:at + wdt, :])
            gu_ref[0, :, at:at + wdt] = gk.astype(BF16)
            gu_ref[1, :, at:at + wdt] = uk.astype(BF16)
            a_ref[:, at:at + wdt] = (gk * _sigmoid(gk) * uk).astype(BF16)

    tile = pl.BlockSpec((tm, dm), lambda i: (i, 0))
    mod = _mod_spec(seq // tm, dm)
    return _call(
        body, job, name=name, grid=(tokens // tm,),
        in_specs=[tile, mod, mod, _const_spec(wgu.shape)],
        out_specs=[pl.BlockSpec((2, tm, ff), lambda i: (0, i, 0)), pl.BlockSpec((tm, ff), lambda i: (i, 0))],
        out_shape=[jax.ShapeDtypeStruct((2, tokens, ff), BF16), jax.ShapeDtypeStruct((tokens, ff), BF16)],
        args=(x, sh, sc, wgu))


def ffn_down(x, a, gt, wd, ln_g, ln_b, seq, name, job=None):
    tokens, dm = x.shape
    ff = wd.shape[0]
    chunks = _hidden_chunks(ff)
    tm = min(FFN_FWD_TILE, seq)

    def body(x_ref, a_ref, gt_ref, wd_ref, lg_ref, lb_ref, xo_ref, r_ref, f_ref):
        acc = jnp.zeros((tm, dm), F32)
        for at, wdt in chunks:
            acc = acc + _dot(a_ref[:, at:at + wdt], wd_ref[at:at + wdt, :])
        f_ref[...] = acc.astype(BF16)
        r = DN_ALPHA * x_ref[...] + (0.5 * (1.0 + gt_ref[0])) * acc
        r_ref[...] = r
        xhat, _ = _ln_stats(r)
        xo_ref[...] = xhat * lg_ref[...] + lb_ref[...]

    tile = pl.BlockSpec((tm, dm), lambda i: (i, 0))
    return _call(
        body, job, name=name, grid=(tokens // tm,),
        in_specs=[tile, pl.BlockSpec((tm, ff), lambda i: (i, 0)), _mod_spec(seq // tm, dm), _const_spec(wd.shape),
                  _const_spec((1, dm)), _const_spec((1, dm))],
        out_specs=[tile, tile, tile],
        out_shape=[jax.ShapeDtypeStruct((tokens, dm), F32), jax.ShapeDtypeStruct((tokens, dm), F32),
                   jax.ShapeDtypeStruct((tokens, dm), BF16)],
        args=(x, a, gt, wd, ln_g, ln_b))


def ffn_bwd(dr, df, x, gu, sh, sc, wgu, wd, seq, name, job=None):
    tokens, dm = x.shape
    ff = wgu.shape[1]
    chunks = _hidden_chunks(ff)
    tm = min(TOKEN_TILE, seq)
    tiles_per_seq = seq // tm
    nseq = tokens // seq

    def body(dr_ref, df_ref, x_ref, gu_ref, sh_ref, sc_ref, wgu_ref, wd_ref, dx_ref, dgu_ref, a_ref, h_ref, dmod_ref):
        @pl.when(pl.program_id(0) % tiles_per_seq == 0)
        def _():
            dmod_ref[...] = jnp.zeros_like(dmod_ref)

        df = df_ref[...]
        xx = x_ref[...]
        one_sc = 1.0 + sc_ref[0]
        h = (xx * one_sc + sh_ref[0]).astype(BF16)
        h_ref[...] = h
        dh = jnp.zeros((tm, dm), F32)
        for at, wdt in chunks:
            cols = slice(at, at + wdt)
            da = _dot_nt(df, wd_ref[cols, :])
            gk = gu_ref[0, :, cols].astype(F32)
            uk = gu_ref[1, :, cols].astype(F32)
            sg = _sigmoid(gk)
            sil = gk * sg
            a_ref[:, cols] = (sil * uk).astype(BF16)
            du = (da * sil).astype(BF16)
            dg = (da * uk * (sg * (1.0 + gk * (1.0 - sg)))).astype(BF16)
            dgu_ref[0, :, cols] = dg
            dgu_ref[1, :, cols] = du
            dh = dh + _dot(dg, wgu_ref[0, cols, :]) + _dot(du, wgu_ref[1, cols, :])
        dx_ref[...] = DN_ALPHA * dr_ref[...] + dh * one_sc
        dmod_ref[0, 0:1, :] += jnp.sum(dh, axis=0, keepdims=True)
        dmod_ref[0, 1:2, :] += jnp.sum(dh * xx, axis=0, keepdims=True)

    tile = pl.BlockSpec((tm, dm), lambda i: (i, 0))
    mod = _mod_spec(tiles_per_seq, dm)
    gu_spec = pl.BlockSpec((2, tm, ff), lambda i: (0, i, 0))
    return _call(
        body, job, name=name, grid=(tokens // tm,),
        in_specs=[tile, tile, tile, gu_spec, mod, mod, _const_spec(wgu.shape), _const_spec(wd.shape)],
        out_specs=[tile, gu_spec, pl.BlockSpec((tm, ff), lambda i: (i, 0)), tile,
                   pl.BlockSpec((1, 2, dm), lambda i: (i // tiles_per_seq, 0, 0))],
        out_shape=[jax.ShapeDtypeStruct((tokens, dm), F32), jax.ShapeDtypeStruct((2, tokens, ff), BF16),
                   jax.ShapeDtypeStruct((tokens, ff), BF16), jax.ShapeDtypeStruct((tokens, dm), BF16),
                   jax.ShapeDtypeStruct((nseq, 2, dm), F32)],
        args=(dr, df, x, gu, sh, sc, wgu, wd))


def tn_matmul(a, b, name, job=None, b_cols=None, a_width=None):
    na, tokens, k_all = a.shape
    kk = k_all if a_width is None else a_width
    nka = k_all // kk
    assert nka * kk == k_all
    nb, _, cc = b.shape
    col = 0
    if b_cols is not None:
        col, cc = b_cols
    tt = tokens
    while 4 * tt * (kk + cc) + 8 * kk * cc > TN_VMEM_BUDGET and tt % 2 == 0 and tt > 256:
        tt //= 2
    steps = tokens // tt

    def body(a_ref, b_ref, o_ref, *acc):
        if steps == 1:
            o_ref[0, 0, 0] = _dot_tn(a_ref[0], b_ref[0]).astype(BF16)
            return
        acc_ref, = acc
        t = pl.program_id(3)

        @pl.when(t == 0)
        def _():
            acc_ref[...] = jnp.zeros_like(acc_ref)

        acc_ref[...] += _dot_tn(a_ref[0], b_ref[0])

        @pl.when(t == steps - 1)
        def _():
            o_ref[0, 0, 0] = acc_ref[...].astype(BF16)

    return _call(
        body, job, name=name, grid=(na, nka, nb, steps),
        in_specs=[pl.BlockSpec((1, tt, kk), lambda i, s, j, t: (i, t, s)),
                  pl.BlockSpec((1, tt, cc), lambda i, s, j, t: (j, t, col))],
        out_specs=[pl.BlockSpec((1, 1, 1, kk, cc), lambda i, s, j, t: (i, s, j, 0, 0))],
        out_shape=[jax.ShapeDtypeStruct((na, nka, nb, kk, cc), BF16)],
        scratch_shapes=[] if steps == 1 else [pltpu.VMEM((kk, cc), F32)], args=(a, b))


def proj_fwd(x1, sh, sc, w_in, seq, name, job=None):
    tokens, dm = x1.shape
    tm = min(MIX_TILE, seq)
    tiles_per_seq = seq // tm
    widths = [N_Q_HEADS * HEAD_DIM, N_KV_HEADS * HEAD_DIM, N_KV_HEADS * HEAD_DIM, 512, 512, 512]
    assert sum(widths) == w_in.shape[0]

    def body(x_ref, sh_ref, sc_ref, w_ref, *outs):
        h = (x_ref[...] * (1.0 + sc_ref[0]) + sh_ref[0]).astype(BF16)
        proj = _dot_nt(h, w_ref[...])
        at = 0
        for o_ref, wdt in zip(outs, widths):
            o_ref[...] = proj[:, at:at + wdt].astype(o_ref.dtype)
            at += wdt

    tile = pl.BlockSpec((tm, dm), lambda i: (i, 0))
    mod = _mod_spec(tiles_per_seq, dm)
    return _call(
        body, job, name=name, grid=(tokens // tm,),
        in_specs=[tile, mod, mod, _const_spec(w_in.shape)],
        out_specs=[pl.BlockSpec((tm, wdt), lambda i: (i, 0)) for wdt in widths],
        out_shape=[jax.ShapeDtypeStruct((tokens, wdt), F32 if i < 3 else BF16) for i, wdt in enumerate(widths)],
        args=(x1, sh, sc, w_in))


LANES = 2 * HEAD_DIM


def _head_lane(shape):
    return lax.broadcasted_iota(jnp.int32, shape, 1) % HEAD_DIM


def _lane_half(shape):
    return lax.broadcasted_iota(jnp.int32, shape, 1) // HEAD_DIM


def _swap_rot(v):
    lane = _head_lane(v.shape)
    half = ROT_DIM // 2
    return jnp.where(lane < half, pltpu.roll(v, LANES - half, 1),
                     jnp.where(lane < ROT_DIM, pltpu.roll(v, half, 1), 0.0))


def _rope(v, cos_t, sin_t):
    return v * cos_t + _swap_rot(v) * sin_t


def _unrope(dv, cos_t, sin_t):
    return dv * cos_t + _swap_rot(dv * sin_t)


def _both_halves(t, g):
    return jnp.where(_lane_half(t.shape) == g, t, pltpu.roll(t, HEAD_DIM, 1))


def _fold_halves(t, g):
    return jnp.where(_lane_half(t.shape) == g, t + pltpu.roll(t, HEAD_DIM, 1), 0.0)


def _stack_heads(blocks):
    rows = []
    for blk in blocks:
        half = _lane_half(blk.shape)
        rows += [jnp.where(half == 0, blk, 0.0), jnp.where(half == 1, blk, 0.0)]
    return jnp.concatenate(rows, axis=0)


def _unstack_heads(t, j):
    lo = t[(2 * j) * ATTN_BLOCK:(2 * j + 1) * ATTN_BLOCK]
    hi = t[(2 * j + 1) * ATTN_BLOCK:(2 * j + 2) * ATTN_BLOCK]
    return jnp.where(_lane_half(lo.shape) == 0, lo, hi)


def _band_mask(q0, w0):
    rows, cols = GQA_GROUP * ATTN_BLOCK, 2 * ATTN_BLOCK
    qi = lax.broadcasted_iota(jnp.int32, (rows, cols), 0) % ATTN_BLOCK + q0
    ki = lax.broadcasted_iota(jnp.int32, (rows, cols), 1) + w0
    diff = qi - ki
    return (diff >= 0) & (diff < ATTN_BLOCK)


def _attn_specs(seq):
    q_spec = pl.BlockSpec((seq, GQA_GROUP * HEAD_DIM), lambda b, g: (b, g))
    kv_spec = pl.BlockSpec((seq, LANES), lambda b, g: (b, 0))
    sink_spec = pl.BlockSpec((1, GQA_GROUP * ATTN_BLOCK, 1), lambda b, g: (g, 0, 0))
    return q_spec, kv_spec, sink_spec


def _block_starts(n):
    q0 = pl.multiple_of(n * ATTN_BLOCK, ATTN_BLOCK)
    w0 = pl.multiple_of(jnp.maximum(n - 1, 0) * ATTN_BLOCK, ATTN_BLOCK)
    return q0, w0


def _stacked_queries(ref, rows):
    return _stack_heads([ref[rows, j * LANES:(j + 1) * LANES] for j in range(2)]).astype(BF16)


def _sink_columns(sinks):
    return jnp.repeat(sinks.reshape(N_KV_HEADS, GQA_GROUP), ATTN_BLOCK, axis=1)[:, :, None]


def _probs_spec(nblk):
    return pl.BlockSpec((1, 1, nblk, GQA_GROUP * ATTN_BLOCK, 2 * ATTN_BLOCK), lambda b, g: (b, g, 0, 0, 0))


def _sink_probs_spec():
    return pl.BlockSpec((1, 1, GQA_GROUP * ATTN_BLOCK, LANES), lambda b, g: (b, g, 0, 0))


def attn_fwd(q, k, v, cos_t, sin_t, sinks, seq, name, job=None):
    tokens = q.shape[0]
    nblk = seq // ATTN_BLOCK
    assert nblk >= 2
    scale = HEAD_DIM ** -0.5

    nseq = tokens // seq
    rows_stacked = GQA_GROUP * ATTN_BLOCK
    assert nblk <= LANES

    def body(q_ref, k_ref, v_ref, cos_ref, sin_ref, sink_ref, o_ref, qr_ref, p_ref, ps_ref, kd_ref, vd_ref):
        g = pl.program_id(1)
        kd_ref[...] = _both_halves(_rope(k_ref[...].astype(F32), cos_ref[...], sin_ref[...]), g).astype(BF16)
        vd_ref[...] = _both_halves(v_ref[...].astype(F32), g).astype(BF16)
        sink = sink_ref[0]
        lane = lax.broadcasted_iota(jnp.int32, (rows_stacked, LANES), 1)

        ps_ref[...] = jnp.zeros_like(ps_ref)

        def block(n, carry):
            q0, w0 = _block_starts(n)
            rows, win = pl.ds(q0, ATTN_BLOCK), pl.ds(w0, 2 * ATTN_BLOCK)
            blocks = []
            for j in range(2):
                qr = _rope(q_ref[rows, j * LANES:(j + 1) * LANES].astype(F32), cos_ref[rows, :], sin_ref[rows, :]).astype(BF16)
                qr_ref[rows, j * LANES:(j + 1) * LANES] = qr
                blocks.append(qr)
            qs = _stack_heads(blocks)
            s = _dot_nt(qs, kd_ref[win, :]) * scale
            s = jnp.where(_band_mask(q0, w0), s, NEG_BIG)
            m = jnp.maximum(jnp.max(s, axis=-1, keepdims=True), sink)
            p = jnp.exp(s - m)
            e_sink = jnp.exp(sink - m)
            inv = pl.reciprocal(jnp.sum(p, axis=-1, keepdims=True) + e_sink, approx=True)
            pn = (p * inv).astype(BF16)
            p_ref[0, 0, n] = pn
            out = _dot(pn, vd_ref[win, :])
            for j in range(2):
                o_ref[rows, j * LANES:(j + 1) * LANES] = _unstack_heads(out, j).astype(o_ref.dtype)
            ps_ref[0, 0] = jnp.where(lane == n, e_sink * inv, ps_ref[0, 0])
            return carry

        lax.fori_loop(0, nblk, block, 0, unroll=2)

    q_spec, kv_spec, sink_spec = _attn_specs(seq)
    return _call(
        body, job, name=name, grid=(nseq, N_KV_HEADS),
        in_specs=[q_spec, kv_spec, kv_spec, kv_spec, kv_spec, sink_spec],
        out_specs=[q_spec, q_spec, _probs_spec(nblk), _sink_probs_spec()],
        out_shape=[jax.ShapeDtypeStruct(q.shape, BF16), jax.ShapeDtypeStruct(q.shape, BF16),
                   jax.ShapeDtypeStruct((nseq, N_KV_HEADS, nblk, rows_stacked, 2 * ATTN_BLOCK), BF16),
                   jax.ShapeDtypeStruct((nseq, N_KV_HEADS, rows_stacked, LANES), F32)],
        scratch_shapes=[pltpu.VMEM((seq, LANES), BF16), pltpu.VMEM((seq, LANES), BF16)],
        args=(q, k, v, cos_t, sin_t, _sink_columns(sinks)))


def attn_bwd(qr, k, v, do, probs, sink_probs, cos_t, sin_t, seq, name, job=None):
    tokens = qr.shape[0]
    nseq = tokens // seq
    nblk = seq // ATTN_BLOCK
    assert nblk >= 2
    rows_stacked = GQA_GROUP * ATTN_BLOCK
    scale = HEAD_DIM ** -0.5

    def body(q_ref, k_ref, v_ref, do_ref, p_ref, ps_ref, cos_ref, sin_ref, dq_ref, dk_ref, dv_ref, ds_ref,
             kd_ref, vd_ref, dkd_ref, dvd_ref, acc_ref):
        g = pl.program_id(1)
        kd_ref[...] = _both_halves(_rope(k_ref[...].astype(F32), cos_ref[...], sin_ref[...]), g).astype(BF16)
        vd_ref[...] = _both_halves(v_ref[...].astype(F32), g).astype(BF16)
        dkd_ref[...] = jnp.zeros_like(dkd_ref)
        dvd_ref[...] = jnp.zeros_like(dvd_ref)
        acc_ref[...] = jnp.zeros_like(acc_ref)
        lane = lax.broadcasted_iota(jnp.int32, (rows_stacked, LANES), 1)

        def block(n, carry):
            q0, w0 = _block_starts(n)
            rows, win = pl.ds(q0, ATTN_BLOCK), pl.ds(w0, 2 * ATTN_BLOCK)
            qs = _stacked_queries(q_ref, rows)
            dos = _stacked_queries(do_ref, rows)
            kw, vw = kd_ref[win, :], vd_ref[win, :]
            pn16 = p_ref[0, 0, n]
            pn = pn16.astype(F32)
            dvd_ref[win, :] += _dot_tn(pn16, dos)
            dp = _dot_nt(dos, vw)
            delta = jnp.sum(dp * pn, axis=-1, keepdims=True)
            ds = (pn * (dp - delta)).astype(BF16)
            dqs = _dot(ds, kw) * scale
            dkd_ref[win, :] += _dot_tn(ds, qs) * scale
            cos_b, sin_b = cos_ref[rows, :], sin_ref[rows, :]
            for j in range(2):
                dq_ref[rows, j * LANES:(j + 1) * LANES] = _unrope(_unstack_heads(dqs, j), cos_b, sin_b).astype(BF16)
            acc_ref[...] += jnp.where(lane == n, ps_ref[0, 0] * delta, 0.0)
            return carry

        lax.fori_loop(0, nblk // 2, lambda i, carry: block(2 * i + 1, block(2 * i, carry)), 0)
        ds_ref[0, 0] = -jnp.sum(acc_ref[...], axis=-1, keepdims=True)
        dk_g = _unrope(_fold_halves(dkd_ref[...], g), cos_ref[...], sin_ref[...])
        dv_g = _fold_halves(dvd_ref[...], g)

        @pl.when(g == 0)
        def _():
            dk_ref[...] = dk_g
            dv_ref[...] = dv_g

        @pl.when(g != 0)
        def _():
            dk_ref[...] += dk_g
            dv_ref[...] += dv_g

    q_spec, kv_spec, _ = _attn_specs(seq)
    return _call(
        body, job, name=name, grid=(nseq, N_KV_HEADS),
        in_specs=[q_spec, kv_spec, kv_spec, q_spec, _probs_spec(nblk), _sink_probs_spec(), kv_spec, kv_spec],
        out_specs=[q_spec, kv_spec, kv_spec, pl.BlockSpec((1, 1, rows_stacked, 1), lambda b, g: (b, g, 0, 0))],
        out_shape=[jax.ShapeDtypeStruct(qr.shape, BF16), jax.ShapeDtypeStruct(k.shape, F32),
                   jax.ShapeDtypeStruct(k.shape, F32), jax.ShapeDtypeStruct((nseq, N_KV_HEADS, rows_stacked, 1), F32)],
        scratch_shapes=[pltpu.VMEM((seq, LANES), BF16), pltpu.VMEM((seq, LANES), BF16),
                        pltpu.VMEM((seq, LANES), F32), pltpu.VMEM((seq, LANES), F32),
                        pltpu.VMEM((rows_stacked, LANES), F32)],
        args=(qr, k, v, do, probs, sink_probs, cos_t, sin_t))


CONV_COLS = 128


def _shift_down(z, by):
    t = lax.broadcasted_iota(jnp.int32, z.shape, 0)
    return jnp.where(t >= by, pltpu.roll(z, by, 0), 0.0)


def _shift_up(z, by):
    n = z.shape[0]
    t = lax.broadcasted_iota(jnp.int32, z.shape, 0)
    return jnp.where(t < n - by, pltpu.roll(z, n - by, 0), 0.0)


def conv_fwd(u, bg, cg, conv_w, seq, name):
    tokens, width = u.shape

    def body(u_ref, bg_ref, cg_ref, w_ref, o_ref):
        z = cg_ref[...].astype(F32) * u_ref[...].astype(F32)
        yy = w_ref[2:3, :] * z + w_ref[1:2, :] * _shift_down(z, 1) + w_ref[0:1, :] * _shift_down(z, 2)
        o_ref[...] = (bg_ref[...].astype(F32) * yy).astype(BF16)

    col = pl.BlockSpec((seq, CONV_COLS), lambda j, b: (b, j))
    return pl.pallas_call(
        body, name=name, grid=(width // CONV_COLS, tokens // seq),
        in_specs=[col, col, col, pl.BlockSpec((CONV_TAPS, CONV_COLS), lambda j, b: (0, j))],
        out_specs=col, out_shape=jax.ShapeDtypeStruct((tokens, width), BF16),
        compiler_params=_params(("parallel", "parallel")),
    )(u, bg, cg, conv_w)


def conv_bwd(dout, u, bg, cg, conv_w, seq, name):
    tokens, width = u.shape

    def body(do_ref, u_ref, bg_ref, cg_ref, w_ref, du_ref, dbg_ref, dcg_ref, dw_ref):
        uu, cg_v, do = u_ref[...].astype(F32), cg_ref[...].astype(F32), do_ref[...].astype(F32)
        z = cg_v * uu
        z1, z2 = _shift_down(z, 1), _shift_down(z, 2)
        yy = w_ref[2:3, :] * z + w_ref[1:2, :] * z1 + w_ref[0:1, :] * z2
        dbg_ref[...] = (do * yy).astype(BF16)
        dyy = do * bg_ref[...].astype(F32)
        dz = w_ref[2:3, :] * dyy + w_ref[1:2, :] * _shift_up(dyy, 1) + w_ref[0:1, :] * _shift_up(dyy, 2)
        du_ref[...] = (dz * cg_v).astype(BF16)
        dcg_ref[...] = (dz * uu).astype(BF16)

        @pl.when(pl.program_id(1) == 0)
        def _():
            dw_ref[...] = jnp.zeros_like(dw_ref)

        dw_ref[0:1, :] += jnp.sum(dyy * z2, axis=0, keepdims=True)
        dw_ref[1:2, :] += jnp.sum(dyy * z1, axis=0, keepdims=True)
        dw_ref[2:3, :] += jnp.sum(dyy * z, axis=0, keepdims=True)

    col = pl.BlockSpec((seq, CONV_COLS), lambda j, b: (b, j))
    w_spec = pl.BlockSpec((CONV_TAPS, CONV_COLS), lambda j, b: (0, j))
    act = jax.ShapeDtypeStruct((tokens, width), BF16)
    return pl.pallas_call(
        body, name=name, grid=(width // CONV_COLS, tokens // seq),
        in_specs=[col, col, col, col, w_spec], out_specs=[col, col, col, w_spec],
        out_shape=[act, act, act, jax.ShapeDtypeStruct((CONV_TAPS, width), F32)],
        compiler_params=_params(("parallel", "arbitrary")),
    )(dout, u, bg, cg, conv_w)


def out_fwd(x1, attn, conv, gt, w_out, ln_g, ln_b, seq, name, job=None):
    tokens, dm = x1.shape
    half = attn.shape[1]
    tm = min(MIX_TILE, seq)
    tiles_per_seq = seq // tm

    def body(x_ref, a_ref, c_ref, gt_ref, w_ref, lg_ref, lb_ref, xo_ref, r_ref, mi_ref, mix_ref):
        mixin = jnp.concatenate([a_ref[...], c_ref[...]], axis=1).astype(BF16)
        mi_ref[...] = mixin
        mix = _dot(mixin, w_ref[...])
        mix_ref[...] = mix.astype(BF16)
        r = DN_ALPHA * x_ref[...] + (1.0 + gt_ref[0]) * mix
        r_ref[...] = r
        xhat, _ = _ln_stats(r)
        xo_ref[...] = xhat * lg_ref[...] + lb_ref[...]

    tile = pl.BlockSpec((tm, dm), lambda i: (i, 0))
    htile = pl.BlockSpec((tm, half), lambda i: (i, 0))
    return _call(
        body, job, name=name, grid=(tokens // tm,),
        in_specs=[tile, htile, htile, _mod_spec(tiles_per_seq, dm), _const_spec(w_out.shape),
                  _const_spec((1, dm)), _const_spec((1, dm))],
        out_specs=[tile, tile, tile, tile],
        out_shape=[jax.ShapeDtypeStruct((tokens, dm), F32), jax.ShapeDtypeStruct((tokens, dm), F32),
                   jax.ShapeDtypeStruct((tokens, dm), BF16), jax.ShapeDtypeStruct((tokens, dm), BF16)],
        args=(x1, attn, conv, gt, w_out, ln_g, ln_b))


def out_bwd(dy, r, mix, gt, w_out, ln_g, seq, name, job=None):
    tokens, dm = r.shape
    half = dm // 2
    tm = min(MIX_TILE, seq)
    tiles_per_seq = seq // tm
    nseq = tokens // seq

    def body(dy_ref, r_ref, mix_ref, gt_ref, w_ref, lg_ref, dres_ref, da_ref, dc_ref, dmix_ref, dln_ref, dgt_ref):
        i = pl.program_id(0)
        dr, dgain, dbias = _ln_bwd(dy_ref[...], r_ref[...], lg_ref[...])

        @pl.when(i == 0)
        def _():
            dln_ref[...] = jnp.zeros_like(dln_ref)

        @pl.when(i % tiles_per_seq == 0)
        def _():
            dgt_ref[...] = jnp.zeros_like(dgt_ref)

        dln_ref[0:1, :] += dgain
        dln_ref[1:2, :] += dbias
        dgt_ref[0] += jnp.sum(dr * mix_ref[...].astype(F32), axis=0, keepdims=True)
        dres_ref[...] = DN_ALPHA * dr
        dmix = ((1.0 + gt_ref[0]) * dr).astype(BF16)
        dmix_ref[...] = dmix
        dmixin = _dot_nt(dmix, w_ref[...])
        da_ref[...] = dmixin[:, :half].astype(BF16)
        dc_ref[...] = dmixin[:, half:].astype(BF16)

    tile = pl.BlockSpec((tm, dm), lambda i: (i, 0))
    htile = pl.BlockSpec((tm, half), lambda i: (i, 0))
    return _call(
        body, job, name=name, grid=(tokens // tm,),
        in_specs=[tile, tile, tile, _mod_spec(tiles_per_seq, dm), _const_spec(w_out.shape), _const_spec((1, dm))],
        out_specs=[tile, htile, htile, tile, pl.BlockSpec((2, dm), lambda i: (0, 0)),
                   pl.BlockSpec((1, 1, dm), lambda i: (i // tiles_per_seq, 0, 0))],
        out_shape=[jax.ShapeDtypeStruct((tokens, dm), F32), jax.ShapeDtypeStruct((tokens, half), BF16),
                   jax.ShapeDtypeStruct((tokens, half), BF16), jax.ShapeDtypeStruct((tokens, dm), BF16),
                   jax.ShapeDtypeStruct((2, dm), F32), jax.ShapeDtypeStruct((nseq, 1, dm), F32)],
        args=(dy, r, mix, gt, w_out, ln_g))


def proj_bwd(parts, dres, x1, sh, sc, w_in, r_prev, f_prev, gt_prev, ln_g_prev, seq, name, job=None):
    tokens, dm = x1.shape
    tm = min(MIX_TILE, seq)
    tiles_per_seq = seq // tm
    nseq = tokens // seq
    widths = [p.shape[1] for p in parts]
    total = sum(widths)

    def body(*refs):
        part_refs = refs[:6]
        (dres_ref, x_ref, sh_ref, sc_ref, w_ref, r_ref, f_ref, gt_ref, lg_ref,
         dr_ref, df_ref, dproj_ref, h_ref, dmod_ref, dln_ref, dgt_ref) = refs[6:]
        i = pl.program_id(0)
        dproj = jnp.concatenate([p[...].astype(BF16) for p in part_refs], axis=1)
        dproj_ref[...] = dproj
        dh = _dot(dproj, w_ref[...])
        xx = x_ref[...]
        one_sc = 1.0 + sc_ref[0]
        h_ref[...] = (xx * one_sc + sh_ref[0]).astype(BF16)
        dr, dgain, dbias = _ln_bwd(dres_ref[...] + dh * one_sc, r_ref[...], lg_ref[...])
        dr_ref[...] = dr
        df_ref[...] = ((0.5 * (1.0 + gt_ref[0])) * dr).astype(BF16)

        @pl.when(i == 0)
        def _():
            dln_ref[...] = jnp.zeros_like(dln_ref)

        @pl.when(i % tiles_per_seq == 0)
        def _():
            dmod_ref[...] = jnp.zeros_like(dmod_ref)
            dgt_ref[...] = jnp.zeros_like(dgt_ref)

        dmod_ref[0, 0:1, :] += jnp.sum(dh, axis=0, keepdims=True)
        dmod_ref[0, 1:2, :] += jnp.sum(dh * xx, axis=0, keepdims=True)
        dln_ref[0:1, :] += dgain
        dln_ref[1:2, :] += dbias
        dgt_ref[0] += jnp.sum(dr * (0.5 * f_ref[...].astype(F32)), axis=0, keepdims=True)

    tile = pl.BlockSpec((tm, dm), lambda i: (i, 0))
    mod = _mod_spec(tiles_per_seq, dm)
    return _call(
        body, job, name=name, grid=(tokens // tm,),
        in_specs=[pl.BlockSpec((tm, wdt), lambda i: (i, 0)) for wdt in widths]
        + [tile, tile, mod, mod, _const_spec(w_in.shape), tile, tile, mod, _const_spec((1, dm))],
        out_specs=[tile, tile, pl.BlockSpec((tm, total), lambda i: (i, 0)), tile,
                   pl.BlockSpec((1, 2, dm), lambda i: (i // tiles_per_seq, 0, 0)),
                   pl.BlockSpec((2, dm), lambda i: (0, 0)), mod],
        out_shape=[jax.ShapeDtypeStruct((tokens, dm), F32), jax.ShapeDtypeStruct((tokens, dm), BF16),
                   jax.ShapeDtypeStruct((tokens, total), BF16), jax.ShapeDtypeStruct((tokens, dm), BF16),
                   jax.ShapeDtypeStruct((nseq, 2, dm), F32), jax.ShapeDtypeStruct((2, dm), F32),
                   jax.ShapeDtypeStruct((nseq, 1, dm), F32)],
        args=(*parts, dres, x1, sh, sc, w_in, r_prev, f_prev, gt_prev, ln_g_prev))


def _rope_tables(positions):
    half = ROT_DIM // 2
    inv_freq = jnp.power(jnp.float32(ROPE_THETA), -jnp.arange(0, ROT_DIM, 2, dtype=F32) / ROT_DIM)
    lane = jnp.arange(LANES) % HEAD_DIM
    freq = jnp.where(lane < ROT_DIM, inv_freq[lane % half], 0.0)
    sign = jnp.where(lane < half, -1.0, 1.0).astype(F32)
    ang = positions.astype(F32)[:, None] * freq[None, :]
    return jnp.cos(ang), sign[None, :] * jnp.sin(ang)


def kernel(x, c, positions, w_ada, b_ada, ffn1_w_gate_up, ffn1_w_down, ln1_g, ln1_b, w_in, conv_w, attn_sinks, w_out, ln2_g, ln2_b, ffn2_w_gate_up, ffn2_w_down, ln3_g, ln3_b, loss_target, m_w_ada, m_b_ada, m_ffn1_w_gate_up, m_ffn1_w_down, m_ln1_g, m_ln1_b, m_w_in, m_conv_w, m_attn_sinks, m_w_out, m_ln2_g, m_ln2_b, m_ffn2_w_gate_up, m_ffn2_w_down, m_ln3_g, m_ln3_b, v_w_ada, v_b_ada, v_ffn1_w_gate_up, v_ffn1_w_down, v_ln1_g, v_ln1_b, v_w_in, v_conv_w, v_attn_sinks, v_w_out, v_ln2_g, v_ln2_b, v_ffn2_w_gate_up, v_ffn2_w_down, v_ln3_g, v_ln3_b):
    nseq, seq, dm = x.shape
    tokens = nseq * seq
    dev = 4 * lax.axis_index("x") + 2 * lax.axis_index("y") + lax.axis_index("c")
    core = lax.axis_index("c").astype(jnp.int32).reshape(1)
    ada_cols = w_ada.shape[2]
    ff = ffn1_w_down.shape[1] * N_DEV
    fc = ff // 4
    in_cols = w_in.shape[2]
    conv_cols = conv_w.shape[2]

    def t_bf16(w):
        return w[0].T.astype(BF16)

    c_all, convw_all = all_gather([c, conv_w[0]], "gather_cond")
    c_all = c_all.reshape(N_DEV * nseq, dm)
    convw_full = convw_all.transpose(1, 0, 2).reshape(CONV_TAPS, N_DEV * conv_cols)

    b_cols = lax.dynamic_slice(b_ada, (0, dev * ada_cols), (1, ada_cols))
    cond_all, mod_cols = ada_fwd(c_all, w_ada[0], b_cols, "ada_fwd")
    wgu1, mod_all = all_gather([t_bf16(ffn1_w_gate_up), mod_cols], "gather_ffn1")
    wgu1 = wgu1.reshape(2, ff, dm)
    mod = lax.dynamic_slice(mod_all, (0, dev * nseq, 0), (N_DEV, nseq, ada_cols))
    mod = mod.transpose(1, 0, 2).reshape(nseq, 9, 1, dm)
    sh1, sc1, g1, sh2, sc2, g2, sh3, sc3, g3 = [mod[:, i] for i in range(9)]

    x0 = x.reshape(tokens, dm)
    (gu1, a1_fwd), (wd1, wout) = ffn_up(x0, sh1, sc1, wgu1, seq, "ffn1_up",
                                        job=_GatherJob([ffn1_w_down[0].astype(BF16), w_out[0].astype(BF16)]))
    wd1, wout = wd1.reshape(ff, dm), wout.reshape(dm, dm)
    (x1, r1, f1), (win,) = ffn_down(x0, a1_fwd, g1, wd1, ln1_g, ln1_b, seq, "ffn1_down", job=_GatherJob([t_bf16(w_in)]))
    win = win.reshape(N_DEV * in_cols, dm)
    (q, k, v, u, bg, cg), wd2_spread = proj_fwd(x1, sh2, sc2, win, seq, "proj_fwd",
                                                job=gather_spread_job([ffn2_w_down[0].astype(BF16)]))
    cos_t, sin_t = _rope_tables(positions.reshape(tokens))
    sinks = attn_sinks[0]
    (attn, q_rot, probs, sink_probs), wgu2_spread = attn_fwd(q, k, v, cos_t, sin_t, sinks, seq, "attn_fwd",
                                                             job=gather_spread_job([t_bf16(ffn2_w_gate_up)]))
    conv = conv_fwd(u, bg, cg, convw_full, seq, "conv_fwd")
    (x2, r2, mixin, mix), (wd2, wgu2) = out_fwd(x1, attn, conv, g2, wout, ln2_g, ln2_b, seq, "out_fwd",
                                                job=gather_forward_job(wd2_spread + wgu2_spread))
    wd2, wgu2 = wd2.reshape(ff, dm), wgu2.reshape(2, ff, dm)
    target = loss_target.reshape(tokens, dm)
    dr3, df3, gu3, loss_part, dln3, dg3 = ffn_loss(x2, sh3, sc3, g3, wgu2, wd2, ln3_g, ln3_b, target, seq, "ffn2_fwd")

    (dx2, dgu3, a3, h3, dmod3), _ = ffn_bwd(dr3, df3, x2, gu3, sh3, sc3, wgu2, wd2, seq, "ffn2_bwd")
    pair = 2 * fc
    g_wd2 = tn_matmul(a3[None], df3[None], "ffn2_dwd", a_width=pair)[0][0].reshape(N_DEV, ff // N_DEV, dm)
    g_wgu2 = tn_matmul(dgu3, h3[None], "ffn2_dwgu", a_width=pair)[0][0].reshape(N_DEV, fc, dm)
    (dres2, dattn, dconv, dmix, dln2, dg2), swapped = out_bwd(dx2, r2, mix, g2, wout, ln2_g, seq, "out_bwd",
                                                              job=swap_job([g_wgu2, g_wd2]))
    p_wgu2, own_wgu2 = pair_sum(core, g_wgu2, swapped[0], "pair_wgu2")
    p_wd2, own_wd2 = pair_sum(core, g_wd2, swapped[1], "pair_wd2")
    du, dbg, dcg, dconvw = conv_bwd(dconv, u, bg, cg, convw_full, seq, "conv_bwd")
    (dq, dk, dv, dsink_rows), (far_wd2,) = attn_bwd(
        q_rot, k, v, dattn, probs, sink_probs, cos_t, sin_t, seq, "attn_bwd", job=chip_exchange_job([p_wd2]))
    parts = [dq, dk, dv, du, dbg, dcg]
    (dr1, df1, dproj, h2, dmod2, dln1, dg1), far_top = proj_bwd(
        parts, dres2, x1, sh2, sc2, win, r1, f1, g1, ln1_g, seq, "proj_bwd",
        job=chip_exchange_job([p_wgu2], rows=(0, fc // 2)))
    (dx0, dgu1, a1, h1, dmod1), _ = ffn_bwd(dr1, df1, x0, gu1, sh1, sc1, wgu1, wd1, seq, "ffn1_bwd")

    dmod = jnp.concatenate([dmod1, dg1, dmod2, dg2, dmod3, dg3], axis=1).reshape(nseq, 9 * dm)
    half = dm // 2
    jobs = _Jobs([gather_spread_job([dmod]),
                  chip_exchange_job([p_wgu2], rows=(fc // 2, fc // 2), into=far_top)])
    (g_wd1,), res = tn_matmul(a1[None], df1[None], "ffn1_dwd", job=jobs, a_width=pair)
    dmod_spread, (far_wgu2,) = jobs.split(res)
    g_wd1 = g_wd1.reshape(N_DEV, ff // N_DEV, dm)
    jobs = _Jobs([swap_job([g_wd1]), gather_forward_job(dmod_spread)])
    (g_l,), res = tn_matmul(dgu1, h1[None], "ffn1_dwgu_l", job=jobs, b_cols=(0, half), a_width=pair)
    (sw_wd1,), (dmod_all,) = jobs.split(res)
    g_l = g_l.reshape(N_DEV, fc, half)
    p_wd1, own_wd1 = pair_sum(core, g_wd1, sw_wd1, "pair_wd1")
    jobs = _Jobs([chip_exchange_job([p_wd1]), swap_job([g_l])])
    (g_r,), res = tn_matmul(dgu1, h1[None], "ffn1_dwgu_r", job=jobs, b_cols=(1, half), a_width=pair)
    (far_wd1,), (sw_l,) = jobs.split(res)
    g_r = g_r.reshape(N_DEV, fc, half)
    p_l, own_l = pair_sum(core, g_l, sw_l, "pair_wgu1_l")

    dmod_cols = lax.dynamic_slice(dmod_all.reshape(N_DEV * nseq, 9 * dm), (0, dev * ada_cols), (N_DEV * nseq, ada_cols))
    grad_w_ada, gb_cols = ada_bwd(cond_all, dmod_cols, "ada_bwd")
    dsinks = jnp.sum(dsink_rows.reshape(nseq, N_Q_HEADS, ATTN_BLOCK), axis=(0, 2))
    small = jnp.zeros((8, dm), F32)
    small = small.at[0:2].set(dln1).at[2:4].set(dln2).at[4:6].set(dln3)
    small = small.at[6, 0:N_Q_HEADS].set(dsinks).at[7, 0].set(loss_part[0, 0])

    jobs = _Jobs([chip_exchange_job([p_l]), swap_job([g_r]), gather_spread_job([small, dconvw, gb_cols])])
    (g_win,), res = tn_matmul(dproj[None], h2[None], "dwin", job=jobs)
    (far_l,), (sw_r,), small_spread = jobs.split(res)
    g_win = g_win.reshape(N_DEV, in_cols, dm)
    p_r, own_r = pair_sum(core, g_r, sw_r, "pair_wgu1_r")
    jobs = _Jobs([chip_exchange_job([p_r]), swap_job([g_win]), gather_forward_job(small_spread)])
    (g_wout,), res = tn_matmul(mixin[None], dmix[None], "dwout", job=jobs)
    (far_r,), (sw_win,), (small_all, dconvw_all, gb_all) = jobs.split(res)
    g_wout = g_wout.reshape(N_DEV, dm // N_DEV, dm)
    p_win, own_win = pair_sum(core, g_win, sw_win, "pair_win")
    jobs = _Jobs([chip_exchange_job([p_win]), swap_job([g_wout])])
    (far_win,), (sw_wout,) = jobs.split(run_job(jobs, "rs_tail_win"))
    p_wout, own_wout = pair_sum(core, g_wout, sw_wout, "pair_wout")
    (far_wout,) = run_job(chip_exchange_job([p_wout]), "rs_tail_wout")

    grads = {
        "ffn1_w_gate_up": jnp.concatenate([own_l, own_r], axis=1), "ffn1_w_down": own_wd1,
        "w_in": own_win, "w_out": own_wout, "ffn2_w_gate_up": own_wgu2, "ffn2_w_down": own_wd2,
    }
    others = {"ffn1_w_gate_up": jnp.concatenate([far_l, far_r], axis=2), "ffn1_w_down": far_wd1,
              "w_in": far_win, "w_out": far_wout, "ffn2_w_gate_up": far_wgu2, "ffn2_w_down": far_wd2}

    grads["w_ada"] = grad_w_ada
    small_sum = sum_devices(small_all, "sum_small")
    dconvw_sum = sum_devices(dconvw_all, "sum_convw")
    loss = small_sum[7, 0]
    grads["b_ada"] = gb_all.reshape(1, N_DEV * ada_cols)
    grads["conv_w"] = lax.dynamic_slice(dconvw_sum, (0, dev * conv_cols), (CONV_TAPS, conv_cols))
    grads["attn_sinks"] = small_sum[6:7, 0:N_Q_HEADS]
    for i, nm in enumerate(["ln1_g", "ln1_b", "ln2_g", "ln2_b", "ln3_g", "ln3_b"]):
        grads[nm] = small_sum[i:i + 1]

    given = dict(w_ada=(w_ada, m_w_ada, v_w_ada), b_ada=(b_ada, m_b_ada, v_b_ada),
                 ffn1_w_gate_up=(ffn1_w_gate_up, m_ffn1_w_gate_up, v_ffn1_w_gate_up),
                 ffn1_w_down=(ffn1_w_down, m_ffn1_w_down, v_ffn1_w_down),
                 ln1_g=(ln1_g, m_ln1_g, v_ln1_g), ln1_b=(ln1_b, m_ln1_b, v_ln1_b),
                 w_in=(w_in, m_w_in, v_w_in), conv_w=(conv_w, m_conv_w, v_conv_w),
                 attn_sinks=(attn_sinks, m_attn_sinks, v_attn_sinks), w_out=(w_out, m_w_out, v_w_out),
                 ln2_g=(ln2_g, m_ln2_g, v_ln2_g), ln2_b=(ln2_b, m_ln2_b, v_ln2_b),
                 ffn2_w_gate_up=(ffn2_w_gate_up, m_ffn2_w_gate_up, v_ffn2_w_gate_up),
                 ffn2_w_down=(ffn2_w_down, m_ffn2_w_down, v_ffn2_w_down),
                 ln3_g=(ln3_g, m_ln3_g, v_ln3_g), ln3_b=(ln3_b, m_ln3_b, v_ln3_b))
    order = ["w_ada", "b_ada", "ffn1_w_gate_up", "ffn1_w_down", "ln1_g", "ln1_b", "w_in", "conv_w", "attn_sinks",
             "w_out", "ln2_g", "ln2_b", "ffn2_w_gate_up", "ffn2_w_down", "ln3_g", "ln3_b"]
    transposed = ("ffn1_w_gate_up", "ffn2_w_gate_up", "w_in")
    big = ("w_ada", "ffn1_w_gate_up", "ffn1_w_down", "w_in", "w_out", "ffn2_w_gate_up", "ffn2_w_down")
    results = {}
    for nm in big:
        if nm in transposed:
            w2, m2, v2 = [t[0].T for t in given[nm]]
            results[nm] = [t.T[None] for t in adamw(w2, grads[nm], m2, v2, "adamw_" + nm, others=others.get(nm))]
        else:
            w2, m2, v2 = [t[0] for t in given[nm]]
            results[nm] = [t[None] for t in adamw(w2, grads[nm], m2, v2, "adamw_" + nm, others=others.get(nm))]
    small_names = [nm for nm in order if nm not in big]
    items = []
    for nm in small_names:
        shape = given[nm][0].shape
        two_d = (shape[-2], shape[-1])
        items.append((given[nm][0].reshape(two_d), grads[nm].reshape(two_d), *[t.reshape(two_d) for t in given[nm][1:]]))
    for nm, res in zip(small_names, adamw_small(items, "adamw_small")):
        shape = given[nm][0].shape
        results[nm] = [grads[nm].reshape(shape)] + [t.reshape(shape) for t in res]
    grad_x = dx0.reshape(nseq, seq, dm)
    return (loss, grad_x, *[results[nm][i] for i in range(4) for nm in order])
```

```python
import functools

import jax
import jax.numpy as jnp
from jax import lax
from jax.experimental import pallas as pl
from jax.experimental.pallas import tpu as pltpu

F32 = jnp.float32
BF16 = jnp.bfloat16
MESH = pl.DeviceIdType.MESH

N_DEV = 8
N_CHIP = 4
HEAD_DIM = 64
N_Q_HEADS = 8
N_KV_HEADS = 2
GQA_GROUP = N_Q_HEADS // N_KV_HEADS
ATTN_BLOCK = 128
ROT_DIM = 16
ROPE_THETA = 500000.0
CONV_TAPS = 3
LN_EPS = 1e-5
DN_ALPHA = 2.0 ** 0.25
ADAM_LR = 0.001
ADAM_B1 = 0.9
ADAM_B2 = 0.999
ADAM_EPS = 1e-08
ADAM_WD = 0.01
ADAM_STEP = 10
NEG_BIG = -1e30

VMEM_LIMIT = 56 * 1024 * 1024
TOKEN_TILE = 256
FFN_FWD_TILE = 512
MIX_TILE = 512
TN_VMEM_BUDGET = 36 * 1024 * 1024


def _params(semantics=None, vmem=VMEM_LIMIT):
    return pltpu.CompilerParams(dimension_semantics=semantics, vmem_limit_bytes=vmem)


def _dot(a, b):
    return jnp.dot(a, b, preferred_element_type=F32)


def _dot_nt(a, b):
    return lax.dot_general(a, b, (((1,), (1,)), ((), ())), preferred_element_type=F32)


def _dot_tn(a, b):
    return lax.dot_general(a, b, (((0,), (0,)), ((), ())), preferred_element_type=F32)


def _sigmoid(x):
    return pl.reciprocal(1.0 + jnp.exp(-x), approx=True)


def _ln_stats(r):
    mu = jnp.mean(r, axis=-1, keepdims=True)
    d = r - mu
    var = jnp.mean(d * d, axis=-1, keepdims=True)
    rstd = lax.rsqrt(var + LN_EPS)
    return d * rstd, rstd


def _ln_bwd(dy, r, g):
    return _ln_bwd_normalized(dy, *_ln_stats(r), g)


def _ln_bwd_normalized(dy, xhat, rstd, g):
    dxhat = dy * g
    c1 = jnp.mean(dxhat, axis=-1, keepdims=True)
    c2 = jnp.mean(dxhat * xhat, axis=-1, keepdims=True)
    dr = rstd * (dxhat - c1 - xhat * c2)
    return dr, jnp.sum(dy * xhat, axis=0, keepdims=True), jnp.sum(dy, axis=0, keepdims=True)


def _const_spec(shape):
    nd = len(shape)
    return pl.BlockSpec(shape, lambda *_: (0,) * nd, pipeline_mode=pl.Buffered(1))


def all_gather(arrs, name):
    n = len(arrs)

    def body(*refs):
        ins, outs = refs[:n], refs[n:2 * n]
        send_sems, recv_sems, local_sems = refs[2 * n:]
        x, y, c = lax.axis_index("x"), lax.axis_index("y"), lax.axis_index("c")
        me, sibling = (x, y, c), (x, y, 1 - c)
        chips = [(1 - x, y), (x, 1 - y), (1 - x, 1 - y)]

        def slot(i, p):
            return outs[i].at[4 * p[0] + 2 * p[1] + p[2]]

        def copy(i, k, block, to, src=None):
            return pltpu.make_async_remote_copy(
                src_ref=slot(i, block) if src is None else src, dst_ref=slot(i, block),
                send_sem=send_sems.at[i, k], recv_sem=recv_sems.at[i, k],
                device_id=to, device_id_type=MESH)

        mine = [pltpu.make_async_copy(ins[i], slot(i, me), local_sems.at[i]) for i in range(n)]
        for cp in mine:
            cp.start()
        first = []
        for i in range(n):
            first.append(copy(i, 0, me, sibling, src=ins[i]))
            first += [copy(i, 1 + j, me, (*chip, c), src=ins[i]) for j, chip in enumerate(chips)]
        for cp in first:
            cp.start()
        passed = []
        for i in range(n):
            for j, chip in enumerate(chips):
                copy(i, 1 + j, (*chip, c), me).wait_recv()
                cp = copy(i, 4 + j, (*chip, c), sibling)
                cp.start()
                passed.append(cp)
        for i in range(n):
            copy(i, 0, sibling, me).wait_recv()
            for j, chip in enumerate(chips):
                copy(i, 4 + j, (*chip, 1 - c), me).wait_recv()
        for cp in first + passed:
            cp.wait_send()
        for cp in mine:
            cp.wait()

    any_spec = pl.BlockSpec(memory_space=pl.ANY)
    return pl.pallas_call(
        body, name=name,
        out_shape=[jax.ShapeDtypeStruct((N_DEV, *a.shape), a.dtype) for a in arrs],
        in_specs=[any_spec] * n, out_specs=[any_spec] * n,
        scratch_shapes=[pltpu.SemaphoreType.DMA((n, 7)), pltpu.SemaphoreType.DMA((n, 7)),
                        pltpu.SemaphoreType.DMA((n,))],
    )(*arrs)


def _place():
    x, y, c = lax.axis_index("x"), lax.axis_index("y"), lax.axis_index("c")
    return x, y, c, [(1 - x, y), (x, 1 - y), (1 - x, 1 - y)]


def _slot(p):
    return 4 * p[0] + 2 * p[1] + p[2]


class _Job:
    def __init__(self, ins, outs, nsem, copies, aliases=None, local=None):
        self.ins, self.outs, self.nsem, self.copies = list(ins), list(outs), nsem, copies
        self.aliases = aliases or {}
        self.local = local

    def scratch(self):
        s = [pltpu.SemaphoreType.DMA(self.nsem), pltpu.SemaphoreType.DMA(self.nsem)]
        if self.local is not None:
            s.append(pltpu.SemaphoreType.DMA((len(self.ins),)))
        return s

    def start(self, ins, outs, sems):
        if self.local is not None:
            for cp in self.local(ins, outs, sems[2]):
                cp.start()
        for cp in self.copies(ins, outs, sems[0], sems[1])[0]:
            cp.start()

    def finish(self, ins, outs, sems):
        started, awaited = self.copies(ins, outs, sems[0], sems[1])
        for cp in awaited:
            cp.wait_recv()
        for cp in started:
            cp.wait_send()
        if self.local is not None:
            for cp in self.local(ins, outs, sems[2]):
                cp.wait()


class _Jobs:
    def __init__(self, jobs):
        self.jobs = jobs
        self.ins = [a for j in jobs for a in j.ins]
        self.outs = [o for j in jobs for o in j.outs]
        self.two_phase = any(getattr(j, "two_phase", False) for j in jobs)
        self.aliases = {}
        at_in = at_out = 0
        for j in jobs:
            self.aliases.update({at_in + i: at_out + o for i, o in j.aliases.items()})
            at_in, at_out = at_in + len(j.ins), at_out + len(j.outs)

    def scratch(self):
        return [s for j in self.jobs for s in j.scratch()]

    def _each(self, ins, outs, sems):
        at_in = at_out = at_sem = 0
        for j in self.jobs:
            n_in, n_out, n_sem = len(j.ins), len(j.outs), len(j.scratch())
            yield j, ins[at_in:at_in + n_in], outs[at_out:at_out + n_out], sems[at_sem:at_sem + n_sem]
            at_in, at_out, at_sem = at_in + n_in, at_out + n_out, at_sem + n_sem

    def start(self, ins, outs, sems):
        for j, i, o, s in self._each(ins, outs, sems):
            j.start(i, o, s)

    def turn(self, ins, outs, sems):
        for j, i, o, s in self._each(ins, outs, sems):
            if getattr(j, "two_phase", False):
                j.turn(i, o, s)

    def finish(self, ins, outs, sems):
        for j, i, o, s in self._each(ins, outs, sems):
            j.finish(i, o, s)

    def split(self, results):
        at, parts = 0, []
        for j in self.jobs:
            parts.append(results[at:at + len(j.outs)])
            at += len(j.outs)
        return parts


def _remote(src, dst, send, recv, idx, to):
    return pltpu.make_async_remote_copy(src_ref=src, dst_ref=dst, send_sem=send.at[idx], recv_sem=recv.at[idx],
                                        device_id=to, device_id_type=MESH)


def _spread_copies(ins, outs, send, recv, base=0):
    x, y, c, chips = _place()
    me = (x, y, c)
    peers = [(x, y, 1 - c)] + [(*chip, c) for chip in chips]
    started, awaited = [], []
    for i, (src, dst) in enumerate(zip(ins, outs)):
        for k, peer in enumerate(peers):
            started.append(_remote(src, dst.at[_slot(me)], send, recv, (base + i, k), peer))
            awaited.append(_remote(src, dst.at[_slot(peer)], send, recv, (base + i, k), peer))
    return started, awaited


def _forward_copies(ins, outs, send, recv, base=0):
    x, y, c, chips = _place()
    started, awaited = [], []
    for i, buf in enumerate(outs):
        for j, chip in enumerate(chips):
            mine, theirs = buf.at[_slot((*chip, c))], buf.at[_slot((*chip, 1 - c))]
            started.append(_remote(mine, mine, send, recv, (base + i, j), (x, y, 1 - c)))
            awaited.append(_remote(theirs, theirs, send, recv, (base + i, j), (x, y, 1 - c)))
    return started, awaited


def _own_block_copies(ins, outs, sems):
    x, y, c, _ = _place()
    return [pltpu.make_async_copy(src, dst.at[_slot((x, y, c))], sems.at[i])
            for i, (src, dst) in enumerate(zip(ins, outs))]


def gather_spread_job(shards):
    outs = [jax.ShapeDtypeStruct((N_DEV, *a.shape), a.dtype) for a in shards]
    return _Job(shards, outs, (len(shards), 4), _spread_copies, local=_own_block_copies)


def gather_forward_job(fulls):
    outs = [jax.ShapeDtypeStruct(a.shape, a.dtype) for a in fulls]
    return _Job(fulls, outs, (len(fulls), 3), _forward_copies, aliases={i: i for i in range(len(fulls))})


TURN_EIGHTHS = 6


class _GatherJob:
    two_phase = True

    def __init__(self, shards):
        self.ins = list(shards)
        self.outs = [jax.ShapeDtypeStruct((N_DEV, *a.shape), a.dtype) for a in shards]
        self.aliases = {}

    def scratch(self):
        n = len(self.ins)
        return [pltpu.SemaphoreType.DMA((n, 4)), pltpu.SemaphoreType.DMA((n, 4)),
                pltpu.SemaphoreType.DMA((n, 3)), pltpu.SemaphoreType.DMA((n, 3)), pltpu.SemaphoreType.DMA((n,))]

    def start(self, ins, outs, sems):
        for cp in _own_block_copies(ins, outs, sems[4]) + _spread_copies(ins, outs, sems[0], sems[1])[0]:
            cp.start()

    def turn(self, ins, outs, sems):
        for cp in _spread_copies(ins, outs, sems[0], sems[1])[1]:
            cp.wait_recv()
        for cp in _forward_copies(outs, outs, sems[2], sems[3])[0]:
            cp.start()

    def finish(self, ins, outs, sems):
        handed_on, arriving = _forward_copies(outs, outs, sems[2], sems[3])
        for cp in arriving:
            cp.wait_recv()
        for cp in _spread_copies(ins, outs, sems[0], sems[1])[0] + handed_on:
            cp.wait_send()
        for cp in _own_block_copies(ins, outs, sems[4]):
            cp.wait()


def swap_job(gs):
    def copies(ins, outs, send, recv):
        x, y, c, _ = _place()
        started, awaited = [], []
        for i, (g, r1) in enumerate(zip(ins, outs)):
            for q in range(N_CHIP):
                started.append(_remote(g.at[2 * q + (1 - c)], r1.at[q], send, recv, (i, q), (x, y, 1 - c)))
                awaited.append(_remote(g.at[2 * q + c], r1.at[q], send, recv, (i, q), (x, y, 1 - c)))
        return started, awaited

    outs = [jax.ShapeDtypeStruct((N_CHIP, *g.shape[1:]), g.dtype) for g in gs]
    return _Job(gs, outs, (len(gs), N_CHIP), copies)


def chip_exchange_job(ps, rows=None, into=None):
    n = len(ps)

    def copies(ins, outs, send, recv):
        x, y, c, chips = _place()
        started, awaited = [], []
        for i, (p, r2) in enumerate(zip(ins[:n], outs)):
            for k, chip in enumerate(chips):
                src, mine, dst = p.at[2 * chip[0] + chip[1]], p.at[2 * x + y], r2.at[k]
                if rows is not None:
                    src, mine, dst = (t.at[pl.ds(rows[0], rows[1])] for t in (src, mine, dst))
                started.append(_remote(src, dst, send, recv, (i, k), (*chip, c)))
                awaited.append(_remote(mine, dst, send, recv, (i, k), (*chip, c)))
        return started, awaited

    outs = [jax.ShapeDtypeStruct((3, *p.shape[1:]), p.dtype) for p in ps]
    if into is None:
        return _Job(ps, outs, (n, 3), copies)
    return _Job(list(ps) + list(into), outs, (n, 3), copies, aliases={n + i: i for i in range(n)})


def _call(body, job, *, name, grid, in_specs, out_specs, out_shape, args, scratch_shapes=(), vmem=VMEM_LIMIT):
    if job is None:
        res = pl.pallas_call(
            body, name=name, grid=grid, in_specs=in_specs, out_specs=out_specs, out_shape=out_shape,
            scratch_shapes=list(scratch_shapes), compiler_params=_params(("arbitrary",) * len(grid), vmem),
        )(*args)
        return res, []
    n_in, n_out, n_scr = len(in_specs), len(out_specs), len(scratch_shapes)
    j_in, j_out = len(job.ins), len(job.outs)

    def with_copies(*refs):
        at = 0
        ins = refs[at:at + n_in]; at += n_in
        jins = refs[at:at + j_in]; at += j_in
        outs = refs[at:at + n_out]; at += n_out
        jouts = refs[at:at + j_out]; at += j_out
        scr = refs[at:at + n_scr]; at += n_scr
        sems = refs[at:]
        ids = [pl.program_id(d) for d in range(len(grid))]
        first = functools.reduce(jnp.logical_and, [i == 0 for i in ids])
        last = functools.reduce(jnp.logical_and, [i == n - 1 for i, n in zip(ids, grid)])

        @pl.when(first)
        def _():
            job.start(jins, jouts, sems)

        if getattr(job, "two_phase", False):
            steps, at = 1, 0
            for i, n in zip(ids, grid):
                steps, at = steps * n, at * n + i

            @pl.when(at == (TURN_EIGHTHS * steps) // 8)
            def _():
                job.turn(jins, jouts, sems)

        body(*ins, *outs, *scr)

        @pl.when(last)
        def _():
            job.finish(jins, jouts, sems)

    any_spec = pl.BlockSpec(memory_space=pl.ANY)
    res = pl.pallas_call(
        with_copies, name=name, grid=grid,
        in_specs=list(in_specs) + [any_spec] * j_in, out_specs=list(out_specs) + [any_spec] * j_out,
        out_shape=list(out_shape) + list(job.outs),
        input_output_aliases={n_in + i: n_out + o for i, o in job.aliases.items()},
        scratch_shapes=list(scratch_shapes) + job.scratch(),
        compiler_params=_params(("arbitrary",) * len(grid), vmem),
    )(*args, *job.ins)
    return res[:n_out], res[n_out:]


def run_job(job, name):
    def body(*refs):
        j_in, j_out = len(job.ins), len(job.outs)
        ins, outs, sems = refs[:j_in], refs[j_in:j_in + j_out], refs[j_in + j_out:]
        job.start(ins, outs, sems)
        job.finish(ins, outs, sems)

    any_spec = pl.BlockSpec(memory_space=pl.ANY)
    return pl.pallas_call(
        body, name=name, in_specs=[any_spec] * len(job.ins), out_specs=[any_spec] * len(job.outs),
        out_shape=list(job.outs), input_output_aliases=dict(job.aliases), scratch_shapes=job.scratch(),
    )(*job.ins)


def pair_sum(core, g, r1, name):
    _, rows, cols = g.shape
    rb = next(cand for cand in range(min(rows, 512), 0, -16) if rows % cand == 0)

    def body(core_ref, g_ref, r1_ref, p_ref, own_ref):
        del core_ref
        x, y, _, _ = _place()
        s = g_ref[0].astype(F32) + r1_ref[0].astype(F32)
        p_ref[0] = s.astype(BF16)

        @pl.when(pl.program_id(1) == 2 * x + y)
        def _():
            own_ref[...] = s

    chunk = (1, rb, cols)
    return pl.pallas_call(
        body, name=name,
        grid_spec=pltpu.PrefetchScalarGridSpec(
            num_scalar_prefetch=1, grid=(rows // rb, N_CHIP),
            in_specs=[pl.BlockSpec(chunk, lambda i, q, core_ref: (2 * q + core_ref[0], i, 0)),
                      pl.BlockSpec(chunk, lambda i, q, core_ref: (q, i, 0))],
            out_specs=[pl.BlockSpec(chunk, lambda i, q, core_ref: (q, i, 0)),
                       pl.BlockSpec((rb, cols), lambda i, q, core_ref: (i, 0))]),
        out_shape=[jax.ShapeDtypeStruct((N_CHIP, rows, cols), BF16), jax.ShapeDtypeStruct((rows, cols), F32)],
        compiler_params=_params(("arbitrary", "arbitrary")),
    )(core, g, r1)


def sum_devices(a, name):
    def body(a_ref, o_ref):
        acc = a_ref[0]
        for d in range(1, N_DEV):
            acc = acc + a_ref[d]
        o_ref[...] = acc

    return pl.pallas_call(body, name=name, out_shape=jax.ShapeDtypeStruct(a.shape[1:], F32))(a)


def _adam_update(w, g, m, v):
    nm = ADAM_B1 * m + (1.0 - ADAM_B1) * g
    nv = ADAM_B2 * v + (1.0 - ADAM_B2) * (g * g)
    m_hat = nm / (1.0 - ADAM_B1 ** ADAM_STEP)
    v_hat = nv / (1.0 - ADAM_B2 ** ADAM_STEP)
    return -ADAM_LR * (m_hat / (jnp.sqrt(v_hat) + ADAM_EPS) + ADAM_WD * w), nm, nv


def adamw(w, g, m, v, name, others=None):
    rows, cols = w.shape
    rb = rows
    for cand in range(min(rows, 512), 7, -8):
        if rows % cand == 0 and cand % 8 == 0:
            rb = cand
            break

    def body(*refs):
        if others is None:
            w_ref, g_ref, m_ref, v_ref, d_ref, nm_ref, nv_ref = refs
            gg = g_ref[...]
        else:
            w_ref, g_ref, m_ref, v_ref, r2_ref, go_ref, d_ref, nm_ref, nv_ref = refs
            gg = g_ref[...]
            for k in range(3):
                gg = gg + r2_ref[k].astype(F32)
            go_ref[...] = gg
        d_ref[...], nm_ref[...], nv_ref[...] = _adam_update(w_ref[...], gg, m_ref[...], v_ref[...])

    spec = pl.BlockSpec((rb, cols), lambda i: (i, 0))
    out = jax.ShapeDtypeStruct((rows, cols), F32)
    in_specs, args = [spec] * 4, [w, g, m, v]
    if others is not None:
        in_specs.append(pl.BlockSpec((3, rb, cols), lambda i: (0, i, 0)))
        args.append(others)
    n_out = 3 if others is None else 4
    res = pl.pallas_call(
        body, name=name, grid=(rows // rb,), in_specs=in_specs, out_specs=[spec] * n_out,
        out_shape=[out] * n_out, compiler_params=_params(("parallel",)),
    )(*args)
    return (g, *res) if others is None else tuple(res)


def adamw_small(items, name):
    n = len(items)

    def body(*refs):
        ins, outs = refs[:4 * n], refs[4 * n:]
        for i in range(n):
            w_ref, g_ref, m_ref, v_ref = ins[4 * i:4 * i + 4]
            d_ref, nm_ref, nv_ref = outs[3 * i:3 * i + 3]
            d_ref[...], nm_ref[...], nv_ref[...] = _adam_update(w_ref[...], g_ref[...], m_ref[...], v_ref[...])

    res = pl.pallas_call(
        body, name=name,
        out_shape=[jax.ShapeDtypeStruct(w.shape, F32) for w, _, _, _ in items for _ in range(3)],
    )(*[t for item in items for t in item])
    return [tuple(res[3 * i:3 * i + 3]) for i in range(n)]


def ada_fwd(c_all, w_cols, b_cols, name):
    def body(c_ref, w_ref, b_ref, cond_ref, mod_ref):
        cc = c_ref[...]
        cond = (cc * _sigmoid(cc)).astype(BF16)
        cond_ref[...] = cond
        mod_ref[...] = _dot(cond, w_ref[...].astype(BF16)) + b_ref[...]

    n, cols = c_all.shape[0], w_cols.shape[1]
    return pl.pallas_call(
        body, name=name,
        out_shape=[jax.ShapeDtypeStruct(c_all.shape, BF16), jax.ShapeDtypeStruct((n, cols), F32)],
        compiler_params=_params(),
    )(c_all, w_cols, b_cols)


def ada_bwd(cond_all, dmod_cols, name):
    def body(c_ref, d_ref, gw_ref, gb_ref):
        d = d_ref[...]
        gw_ref[...] = _dot_tn(c_ref[...], d.astype(BF16))
        gb_ref[...] = jnp.sum(d, axis=0, keepdims=True)

    dm, cols = cond_all.shape[1], dmod_cols.shape[1]
    return pl.pallas_call(
        body, name=name,
        out_shape=[jax.ShapeDtypeStruct((dm, cols), F32), jax.ShapeDtypeStruct((1, cols), F32)],
        compiler_params=_params(),
    )(cond_all, dmod_cols)


MXU_COLS = 256
FFN_CHUNK = 4 * MXU_COLS


def _hidden_chunks(ff):
    assert ff % MXU_COLS == 0
    return [(at, min(FFN_CHUNK, ff - at)) for at in range(0, ff, FFN_CHUNK)]


def _mod_spec(tiles_per_seq, dm):
    return pl.BlockSpec((1, 1, dm), lambda i: (i // tiles_per_seq, 0, 0))


def ffn_loss(x, sh, sc, gt, wgu, wd, ln_g, ln_b, target, seq, name):
    tokens, dm = x.shape
    ff = wgu.shape[1]
    chunks = _hidden_chunks(ff)
    tm = min(TOKEN_TILE, seq)
    tiles_per_seq = seq // tm

    def body(x_ref, sh_ref, sc_ref, gt_ref, wgu_ref, wd_ref, lg_ref, lb_ref, t_ref,
             dr_ref, df_ref, gu_ref, a_ref, h_ref, loss_ref, dln_ref, dgt_ref):
        i = pl.program_id(0)
        xx = x_ref[...]
        h = (xx * (1.0 + sc_ref[0]) + sh_ref[0]).astype(BF16)
        h_ref[...] = h
        acc = jnp.zeros((tm, dm), F32)
        for at, wdt in chunks:
            gk = _dot_nt(h, wgu_ref[0, at:at + wdt, :])
            uk = _dot_nt(h, wgu_ref[1, at:at + wdt, :])
            gu_ref[0, :, at:at + wdt] = gk.astype(BF16)
            gu_ref[1, :, at:at + wdt] = uk.astype(BF16)
            a = (gk * _sigmoid(gk) * uk).astype(BF16)
            a_ref[:, at:at + wdt] = a
            acc = acc + _dot(a, wd_ref[at:at + wdt, :])
        half_gate = 0.5 * (1.0 + gt_ref[0])
        xhat, rstd = _ln_stats(DN_ALPHA * xx + half_gate * acc)
        err = xhat * lg_ref[...] + lb_ref[...] - t_ref[...]
        dr, dgain, dbias = _ln_bwd_normalized(err * (1.0 / dm), xhat, rstd, lg_ref[...])
        dr_ref[...] = dr
        df_ref[...] = (half_gate * dr).astype(BF16)

        @pl.when(i == 0)
        def _():
            loss_ref[...] = jnp.zeros_like(loss_ref)
            dln_ref[...] = jnp.zeros_like(dln_ref)

        @pl.when(i % tiles_per_seq == 0)
        def _():
            dgt_ref[...] = jnp.zeros_like(dgt_ref)

        loss_ref[...] += jnp.full((1, 128), (0.5 / dm) * jnp.sum(err * err), F32)
        dln_ref[0:1, :] += dgain
        dln_ref[1:2, :] += dbias
        dgt_ref[0] += jnp.sum(dr * (0.5 * acc), axis=0, keepdims=True)

    tile = pl.BlockSpec((tm, dm), lambda i: (i, 0))
    mod = _mod_spec(tiles_per_seq, dm)
    res, _ = _call(
        body, None, name=name, grid=(tokens // tm,),
        in_specs=[tile, mod, mod, mod, _const_spec(wgu.shape), _const_spec(wd.shape),
                  _const_spec((1, dm)), _const_spec((1, dm)), tile],
        out_specs=[tile, tile, pl.BlockSpec((2, tm, ff), lambda i: (0, i, 0)), pl.BlockSpec((tm, ff), lambda i: (i, 0)),
                   tile, pl.BlockSpec((1, 128), lambda i: (0, 0)), pl.BlockSpec((2, dm), lambda i: (0, 0)), mod],
        out_shape=[jax.ShapeDtypeStruct((tokens, dm), F32), jax.ShapeDtypeStruct((tokens, dm), BF16),
                   jax.ShapeDtypeStruct((2, tokens, ff), BF16), jax.ShapeDtypeStruct((tokens, ff), BF16),
                   jax.ShapeDtypeStruct((tokens, dm), BF16), jax.ShapeDtypeStruct((1, 128), F32),
                   jax.ShapeDtypeStruct((2, dm), F32), jax.ShapeDtypeStruct((tokens // seq, 1, dm), F32)],
        args=(x, sh, sc, gt, wgu, wd, ln_g, ln_b, target))
    return res


def ffn_up(x, sh, sc, wgu, seq, name, job=None):
    tokens, dm = x.shape
    ff = wgu.shape[1]
    chunks = _hidden_chunks(ff)
    tm = min(FFN_FWD_TILE, seq)

    def body(x_ref, sh_ref, sc_ref, wgu_ref, gu_ref, a_ref, h_ref):
        h = (x_ref[...] * (1.0 + sc_ref[0]) + sh_ref[0]).astype(BF16)
        h_ref[...] = h
        for at, wdt in chunks:
            gk = _dot_nt(h, wgu_ref[0, at:at + wdt, :])
            uk = _dot_nt(h, wgu_ref[1, at:at + wdt, :])
            gu_ref[0, :, at:at + wdt] = gk.astype(BF16)
            gu_ref[1, :, at:at + wdt] = uk.astype(BF16)
            a_ref[:, at:at + wdt] = (gk * _sigmoid(gk) * uk).astype(BF16)

    tile = pl.BlockSpec((tm, dm), lambda i: (i, 0))
    mod = _mod_spec(seq // tm, dm)
    return _call(
        body, job, name=name, grid=(tokens // tm,),
        in_specs=[tile, mod, mod, _const_spec(wgu.shape)],
        out_specs=[pl.BlockSpec((2, tm, ff), lambda i: (0, i, 0)), pl.BlockSpec((tm, ff), lambda i: (i, 0)), tile],
        out_shape=[jax.ShapeDtypeStruct((2, tokens, ff), BF16), jax.ShapeDtypeStruct((tokens, ff), BF16),
                   jax.ShapeDtypeStruct((tokens, dm), BF16)],
        args=(x, sh, sc, wgu))


def ffn_down(x, a, gt, wd, ln_g, ln_b, seq, name, job=None):
    tokens, dm = x.shape
    ff = wd.shape[0]
    chunks = _hidden_chunks(ff)
    tm = min(FFN_FWD_TILE, seq)

    def body(x_ref, a_ref, gt_ref, wd_ref, lg_ref, lb_ref, xo_ref, r_ref, f_ref):
        acc = jnp.zeros((tm, dm), F32)
        for at, wdt in chunks:
            acc = acc + _dot(a_ref[:, at:at + wdt], wd_ref[at:at + wdt, :])
        f_ref[...] = acc.astype(BF16)
        r = DN_ALPHA * x_ref[...] + (0.5 * (1.0 + gt_ref[0])) * acc
        r_ref[...] = r
        xhat, _ = _ln_stats(r)
        xo_ref[...] = xhat * lg_ref[...] + lb_ref[...]

    tile = pl.BlockSpec((tm, dm), lambda i: (i, 0))
    return _call(
        body, job, name=name, grid=(tokens // tm,),
        in_specs=[tile, pl.BlockSpec((tm, ff), lambda i: (i, 0)), _mod_spec(seq // tm, dm), _const_spec(wd.shape),
                  _const_spec((1, dm)), _const_spec((1, dm))],
        out_specs=[tile, tile, tile],
        out_shape=[jax.ShapeDtypeStruct((tokens, dm), F32), jax.ShapeDtypeStruct((tokens, dm), F32),
                   jax.ShapeDtypeStruct((tokens, dm), BF16)],
        args=(x, a, gt, wd, ln_g, ln_b))


def ffn_bwd(dr, df, x, gu, sc, wgu, wd, seq, name, job=None):
    tokens, dm = x.shape
    ff = wgu.shape[1]
    chunks = _hidden_chunks(ff)
    tm = min(TOKEN_TILE, seq)
    tiles_per_seq = seq // tm
    nseq = tokens // seq

    def body(dr_ref, df_ref, x_ref, gu_ref, sc_ref, wgu_ref, wd_ref, dx_ref, dgu_ref, dmod_ref):
        @pl.when(pl.program_id(0) % tiles_per_seq == 0)
        def _():
            dmod_ref[...] = jnp.zeros_like(dmod_ref)

        df = df_ref[...]
        dh = jnp.zeros((tm, dm), F32)
        for at, wdt in chunks:
            cols = slice(at, at + wdt)
            da = _dot_nt(df, wd_ref[cols, :])
            gk = gu_ref[0, :, cols].astype(F32)
            uk = gu_ref[1, :, cols].astype(F32)
            sg = _sigmoid(gk)
            sil = gk * sg
            du = (da * sil).astype(BF16)
            dg = (da * uk * (sg * (1.0 + gk * (1.0 - sg)))).astype(BF16)
            dgu_ref[0, :, cols] = dg
            dgu_ref[1, :, cols] = du
            dh = dh + _dot(dg, wgu_ref[0, cols, :]) + _dot(du, wgu_ref[1, cols, :])
        dx_ref[...] = DN_ALPHA * dr_ref[...] + dh * (1.0 + sc_ref[0])
        dmod_ref[0, 0:1, :] += jnp.sum(dh, axis=0, keepdims=True)
        dmod_ref[0, 1:2, :] += jnp.sum(dh * x_ref[...], axis=0, keepdims=True)

    tile = pl.BlockSpec((tm, dm), lambda i: (i, 0))
    gu_spec = pl.BlockSpec((2, tm, ff), lambda i: (0, i, 0))
    return _call(
        body, job, name=name, grid=(tokens // tm,),
        in_specs=[tile, tile, tile, gu_spec, _mod_spec(tiles_per_seq, dm), _const_spec(wgu.shape), _const_spec(wd.shape)],
        out_specs=[tile, gu_spec, pl.BlockSpec((1, 2, dm), lambda i: (i // tiles_per_seq, 0, 0))],
        out_shape=[jax.ShapeDtypeStruct((tokens, dm), F32), jax.ShapeDtypeStruct((2, tokens, ff), BF16),
                   jax.ShapeDtypeStruct((nseq, 2, dm), F32)],
        args=(dr, df, x, gu, sc, wgu, wd))


def tn_matmul(a, b, name, job=None, b_cols=None, a_width=None):
    na, tokens, k_all = a.shape
    kk = k_all if a_width is None else a_width
    nka = k_all // kk
    assert nka * kk == k_all
    nb, _, cc = b.shape
    col = 0
    if b_cols is not None:
        col, cc = b_cols
    tt = tokens
    while 4 * tt * (kk + cc) + 8 * kk * cc > TN_VMEM_BUDGET and tt % 2 == 0 and tt > 256:
        tt //= 2
    steps = tokens // tt

    def body(a_ref, b_ref, o_ref, *acc):
        if steps == 1:
            o_ref[0, 0, 0] = _dot_tn(a_ref[0], b_ref[0]).astype(BF16)
            return
        acc_ref, = acc
        t = pl.program_id(3)

        @pl.when(t == 0)
        def _():
            acc_ref[...] = jnp.zeros_like(acc_ref)

        acc_ref[...] += _dot_tn(a_ref[0], b_ref[0])

        @pl.when(t == steps - 1)
        def _():
            o_ref[0, 0, 0] = acc_ref[...].astype(BF16)

    return _call(
        body, job, name=name, grid=(na, nka, nb, steps),
        in_specs=[pl.BlockSpec((1, tt, kk), lambda i, s, j, t: (i, t, s)),
                  pl.BlockSpec((1, tt, cc), lambda i, s, j, t: (j, t, col))],
        out_specs=[pl.BlockSpec((1, 1, 1, kk, cc), lambda i, s, j, t: (i, s, j, 0, 0))],
        out_shape=[jax.ShapeDtypeStruct((na, nka, nb, kk, cc), BF16)],
        scratch_shapes=[] if steps == 1 else [pltpu.VMEM((kk, cc), F32)], args=(a, b))


def proj_fwd(x1, sh, sc, w_in, seq, name, job=None):
    tokens, dm = x1.shape
    tm = min(MIX_TILE, seq)
    tiles_per_seq = seq // tm
    widths = [N_Q_HEADS * HEAD_DIM, N_KV_HEADS * HEAD_DIM, N_KV_HEADS * HEAD_DIM, 512, 512, 512]
    assert sum(widths) == w_in.shape[0]

    def body(x_ref, sh_ref, sc_ref, w_ref, *outs):
        h = (x_ref[...] * (1.0 + sc_ref[0]) + sh_ref[0]).astype(BF16)
        proj = _dot_nt(h, w_ref[...])
        at = 0
        for o_ref, wdt in zip(outs, widths):
            o_ref[...] = proj[:, at:at + wdt].astype(o_ref.dtype)
            at += wdt

    tile = pl.BlockSpec((tm, dm), lambda i: (i, 0))
    mod = _mod_spec(tiles_per_seq, dm)
    return _call(
        body, job, name=name, grid=(tokens // tm,),
        in_specs=[tile, mod, mod, _const_spec(w_in.shape)],
        out_specs=[pl.BlockSpec((tm, wdt), lambda i: (i, 0)) for wdt in widths],
        out_shape=[jax.ShapeDtypeStruct((tokens, wdt), F32 if i < 3 else BF16) for i, wdt in enumerate(widths)],
        args=(x1, sh, sc, w_in))


LANES = 2 * HEAD_DIM


def _head_lane(shape):
    return lax.broadcasted_iota(jnp.int32, shape, 1) % HEAD_DIM


def _lane_half(shape):
    return lax.broadcasted_iota(jnp.int32, shape, 1) // HEAD_DIM


def _swap_rot(v):
    lane = _head_lane(v.shape)
    half = ROT_DIM // 2
    return jnp.where(lane < half, pltpu.roll(v, LANES - half, 1),
                     jnp.where(lane < ROT_DIM, pltpu.roll(v, half, 1), 0.0))


def _rope(v, cos_t, sin_t):
    return v * cos_t + _swap_rot(v) * sin_t


def _unrope(dv, cos_t, sin_t):
    return dv * cos_t + _swap_rot(dv * sin_t)


def _both_halves(t, g):
    return jnp.where(_lane_half(t.shape) == g, t, pltpu.roll(t, HEAD_DIM, 1))


def _fold_halves(t, g):
    return jnp.where(_lane_half(t.shape) == g, t + pltpu.roll(t, HEAD_DIM, 1), 0.0)


def _stack_heads(blocks):
    rows = []
    for blk in blocks:
        half = _lane_half(blk.shape)
        rows += [jnp.where(half == 0, blk, 0.0), jnp.where(half == 1, blk, 0.0)]
    return jnp.concatenate(rows, axis=0)


def _unstack_heads(t, j):
    lo = t[(2 * j) * ATTN_BLOCK:(2 * j + 1) * ATTN_BLOCK]
    hi = t[(2 * j + 1) * ATTN_BLOCK:(2 * j + 2) * ATTN_BLOCK]
    return jnp.where(_lane_half(lo.shape) == 0, lo, hi)


def _band_mask(q0, w0):
    rows, cols = GQA_GROUP * ATTN_BLOCK, 2 * ATTN_BLOCK
    qi = lax.broadcasted_iota(jnp.int32, (rows, cols), 0) % ATTN_BLOCK + q0
    ki = lax.broadcasted_iota(jnp.int32, (rows, cols), 1) + w0
    diff = qi - ki
    return (diff >= 0) & (diff < ATTN_BLOCK)


def _attn_specs(seq):
    q_spec = pl.BlockSpec((seq, GQA_GROUP * HEAD_DIM), lambda b, g: (b, g))
    kv_spec = pl.BlockSpec((seq, LANES), lambda b, g: (b, 0))
    sink_spec = pl.BlockSpec((1, GQA_GROUP * ATTN_BLOCK, 1), lambda b, g: (g, 0, 0))
    return q_spec, kv_spec, sink_spec


def _block_starts(n):
    q0 = pl.multiple_of(n * ATTN_BLOCK, ATTN_BLOCK)
    w0 = pl.multiple_of(jnp.maximum(n - 1, 0) * ATTN_BLOCK, ATTN_BLOCK)
    return q0, w0


def _stacked_queries(ref, rows):
    return _stack_heads([ref[rows, j * LANES:(j + 1) * LANES] for j in range(2)]).astype(BF16)


def _sink_columns(sinks):
    return jnp.repeat(sinks.reshape(N_KV_HEADS, GQA_GROUP), ATTN_BLOCK, axis=1)[:, :, None]


def _probs_spec(nblk):
    return pl.BlockSpec((1, 1, nblk, GQA_GROUP * ATTN_BLOCK, 2 * ATTN_BLOCK), lambda b, g: (b, g, 0, 0, 0))


def _sink_probs_spec():
    return pl.BlockSpec((1, 1, GQA_GROUP * ATTN_BLOCK, LANES), lambda b, g: (b, g, 0, 0))


def attn_fwd(q, k, v, cos_t, sin_t, sinks, seq, name, job=None):
    tokens = q.shape[0]
    nblk = seq // ATTN_BLOCK
    assert nblk >= 2
    scale = HEAD_DIM ** -0.5

    nseq = tokens // seq
    rows_stacked = GQA_GROUP * ATTN_BLOCK
    assert nblk <= LANES

    def body(q_ref, k_ref, v_ref, cos_ref, sin_ref, sink_ref, o_ref, qr_ref, p_ref, ps_ref, kd_ref, vd_ref):
        g = pl.program_id(1)
        kd_ref[...] = _both_halves(_rope(k_ref[...].astype(F32), cos_ref[...], sin_ref[...]), g).astype(BF16)
        vd_ref[...] = _both_halves(v_ref[...].astype(F32), g).astype(BF16)
        sink = sink_ref[0]
        lane = lax.broadcasted_iota(jnp.int32, (rows_stacked, LANES), 1)

        ps_ref[...] = jnp.zeros_like(ps_ref)

        def block(n, carry):
            q0, w0 = _block_starts(n)
            rows, win = pl.ds(q0, ATTN_BLOCK), pl.ds(w0, 2 * ATTN_BLOCK)
            blocks = []
            for j in range(2):
                qr = _rope(q_ref[rows, j * LANES:(j + 1) * LANES].astype(F32), cos_ref[rows, :], sin_ref[rows, :]).astype(BF16)
                qr_ref[rows, j * LANES:(j + 1) * LANES] = qr
                blocks.append(qr)
            qs = _stack_heads(blocks)
            s = _dot_nt(qs, kd_ref[win, :]) * scale
            s = jnp.where(_band_mask(q0, w0), s, NEG_BIG)
            m = jnp.maximum(jnp.max(s, axis=-1, keepdims=True), sink)
            p = jnp.exp(s - m)
            e_sink = jnp.exp(sink - m)
            inv = pl.reciprocal(jnp.sum(p, axis=-1, keepdims=True) + e_sink, approx=True)
            pn = (p * inv).astype(BF16)
            p_ref[0, 0, n] = pn
            out = _dot(pn, vd_ref[win, :])
            for j in range(2):
                o_ref[rows, j * LANES:(j + 1) * LANES] = _unstack_heads(out, j).astype(o_ref.dtype)
            ps_ref[0, 0] = jnp.where(lane == n, e_sink * inv, ps_ref[0, 0])
            return carry

        lax.fori_loop(0, nblk, block, 0, unroll=2)

    q_spec, kv_spec, sink_spec = _attn_specs(seq)
    return _call(
        body, job, name=name, grid=(nseq, N_KV_HEADS),
        in_specs=[q_spec, kv_spec, kv_spec, kv_spec, kv_spec, sink_spec],
        out_specs=[q_spec, q_spec, _probs_spec(nblk), _sink_probs_spec()],
        out_shape=[jax.ShapeDtypeStruct(q.shape, BF16), jax.ShapeDtypeStruct(q.shape, BF16),
                   jax.ShapeDtypeStruct((nseq, N_KV_HEADS, nblk, rows_stacked, 2 * ATTN_BLOCK), BF16),
                   jax.ShapeDtypeStruct((nseq, N_KV_HEADS, rows_stacked, LANES), F32)],
        scratch_shapes=[pltpu.VMEM((seq, LANES), BF16), pltpu.VMEM((seq, LANES), BF16)],
        args=(q, k, v, cos_t, sin_t, _sink_columns(sinks)))


def attn_bwd(qr, k, v, do, probs, sink_probs, cos_t, sin_t, seq, name, job=None):
    tokens = qr.shape[0]
    nseq = tokens // seq
    nblk = seq // ATTN_BLOCK
    assert nblk >= 2
    rows_stacked = GQA_GROUP * ATTN_BLOCK
    scale = HEAD_DIM ** -0.5

    def body(q_ref, k_ref, v_ref, do_ref, p_ref, ps_ref, cos_ref, sin_ref, dq_ref, dk_ref, dv_ref, ds_ref,
             kd_ref, vd_ref, dkd_ref, dvd_ref, acc_ref):
        g = pl.program_id(1)
        kd_ref[...] = _both_halves(_rope(k_ref[...].astype(F32), cos_ref[...], sin_ref[...]), g).astype(BF16)
        vd_ref[...] = _both_halves(v_ref[...].astype(F32), g).astype(BF16)
        dkd_ref[...] = jnp.zeros_like(dkd_ref)
        dvd_ref[...] = jnp.zeros_like(dvd_ref)
        acc_ref[...] = jnp.zeros_like(acc_ref)
        lane = lax.broadcasted_iota(jnp.int32, (rows_stacked, LANES), 1)

        def block(n, carry):
            q0, w0 = _block_starts(n)
            rows, win = pl.ds(q0, ATTN_BLOCK), pl.ds(w0, 2 * ATTN_BLOCK)
            qs = _stacked_queries(q_ref, rows)
            dos = _stacked_queries(do_ref, rows)
            kw, vw = kd_ref[win, :], vd_ref[win, :]
            pn16 = p_ref[0, 0, n]
            pn = pn16.astype(F32)
            dvd_ref[win, :] += _dot_tn(pn16, dos)
            dp = _dot_nt(dos, vw)
            delta = jnp.sum(dp * pn, axis=-1, keepdims=True)
            ds = (pn * (dp - delta)).astype(BF16)
            dqs = _dot(ds, kw) * scale
            dkd_ref[win, :] += _dot_tn(ds, qs) * scale
            cos_b, sin_b = cos_ref[rows, :], sin_ref[rows, :]
            for j in range(2):
                dq_ref[rows, j * LANES:(j + 1) * LANES] = _unrope(_unstack_heads(dqs, j), cos_b, sin_b).astype(BF16)
            acc_ref[...] += jnp.where(lane == n, ps_ref[0, 0] * delta, 0.0)
            return carry

        lax.fori_loop(0, nblk // 2, lambda i, carry: block(2 * i + 1, block(2 * i, carry)), 0)
        ds_ref[0, 0] = -jnp.sum(acc_ref[...], axis=-1, keepdims=True)
        dk_g = _unrope(_fold_halves(dkd_ref[...], g), cos_ref[...], sin_ref[...])
        dv_g = _fold_halves(dvd_ref[...], g)

        @pl.when(g == 0)
        def _():
            dk_ref[...] = dk_g
            dv_ref[...] = dv_g

        @pl.when(g != 0)
        def _():
            dk_ref[...] += dk_g
            dv_ref[...] += dv_g

    q_spec, kv_spec, _ = _attn_specs(seq)
    return _call(
        body, job, name=name, grid=(nseq, N_KV_HEADS),
        in_specs=[q_spec, kv_spec, kv_spec, q_spec, _probs_spec(nblk), _sink_probs_spec(), kv_spec, kv_spec],
        out_specs=[q_spec, kv_spec, kv_spec, pl.BlockSpec((1, 1, rows_stacked, 1), lambda b, g: (b, g, 0, 0))],
        out_shape=[jax.ShapeDtypeStruct(qr.shape, BF16), jax.ShapeDtypeStruct(k.shape, F32),
                   jax.ShapeDtypeStruct(k.shape, F32), jax.ShapeDtypeStruct((nseq, N_KV_HEADS, rows_stacked, 1), F32)],
        scratch_shapes=[pltpu.VMEM((seq, LANES), BF16), pltpu.VMEM((seq, LANES), BF16),
                        pltpu.VMEM((seq, LANES), F32), pltpu.VMEM((seq, LANES), F32),
                        pltpu.VMEM((rows_stacked, LANES), F32)],
        args=(qr, k, v, do, probs, sink_probs, cos_t, sin_t))


CONV_COLS = 128


def _shift_down(z, by):
    t = lax.broadcasted_iota(jnp.int32, z.shape, 0)
    return jnp.where(t >= by, pltpu.roll(z, by, 0), 0.0)


def _shift_up(z, by):
    n = z.shape[0]
    t = lax.broadcasted_iota(jnp.int32, z.shape, 0)
    return jnp.where(t < n - by, pltpu.roll(z, n - by, 0), 0.0)


def conv_fwd(u, bg, cg, conv_w, seq, name):
    tokens, width = u.shape

    def body(u_ref, bg_ref, cg_ref, w_ref, o_ref):
        z = cg_ref[...].astype(F32) * u_ref[...].astype(F32)
        yy = w_ref[2:3, :] * z + w_ref[1:2, :] * _shift_down(z, 1) + w_ref[0:1, :] * _shift_down(z, 2)
        o_ref[...] = (bg_ref[...].astype(F32) * yy).astype(BF16)

    col = pl.BlockSpec((seq, CONV_COLS), lambda j, b: (b, j))
    return pl.pallas_call(
        body, name=name, grid=(width // CONV_COLS, tokens // seq),
        in_specs=[col, col, col, pl.BlockSpec((CONV_TAPS, CONV_COLS), lambda j, b: (0, j))],
        out_specs=col, out_shape=jax.ShapeDtypeStruct((tokens, width), BF16),
        compiler_params=_params(("parallel", "parallel")),
    )(u, bg, cg, conv_w)


def conv_bwd(dout, u, bg, cg, conv_w, seq, name):
    tokens, width = u.shape

    def body(do_ref, u_ref, bg_ref, cg_ref, w_ref, du_ref, dbg_ref, dcg_ref, dw_ref):
        uu, cg_v, do = u_ref[...].astype(F32), cg_ref[...].astype(F32), do_ref[...].astype(F32)
        z = cg_v * uu
        z1, z2 = _shift_down(z, 1), _shift_down(z, 2)
        yy = w_ref[2:3, :] * z + w_ref[1:2, :] * z1 + w_ref[0:1, :] * z2
        dbg_ref[...] = (do * yy).astype(BF16)
        dyy = do * bg_ref[...].astype(F32)
        dz = w_ref[2:3, :] * dyy + w_ref[1:2, :] * _shift_up(dyy, 1) + w_ref[0:1, :] * _shift_up(dyy, 2)
        du_ref[...] = (dz * cg_v).astype(BF16)
        dcg_ref[...] = (dz * uu).astype(BF16)

        @pl.when(pl.program_id(1) == 0)
        def _():
            dw_ref[...] = jnp.zeros_like(dw_ref)

        dw_ref[0:1, :] += jnp.sum(dyy * z2, axis=0, keepdims=True)
        dw_ref[1:2, :] += jnp.sum(dyy * z1, axis=0, keepdims=True)
        dw_ref[2:3, :] += jnp.sum(dyy * z, axis=0, keepdims=True)

    col = pl.BlockSpec((seq, CONV_COLS), lambda j, b: (b, j))
    w_spec = pl.BlockSpec((CONV_TAPS, CONV_COLS), lambda j, b: (0, j))
    act = jax.ShapeDtypeStruct((tokens, width), BF16)
    return pl.pallas_call(
        body, name=name, grid=(width // CONV_COLS, tokens // seq),
        in_specs=[col, col, col, col, w_spec], out_specs=[col, col, col, w_spec],
        out_shape=[act, act, act, jax.ShapeDtypeStruct((CONV_TAPS, width), F32)],
        compiler_params=_params(("parallel", "arbitrary")),
    )(dout, u, bg, cg, conv_w)


def out_fwd(x1, attn, conv, gt, w_out, ln_g, ln_b, seq, name, job=None):
    tokens, dm = x1.shape
    half = attn.shape[1]
    tm = min(MIX_TILE, seq)
    tiles_per_seq = seq // tm

    def body(x_ref, a_ref, c_ref, gt_ref, w_ref, lg_ref, lb_ref, xo_ref, r_ref, mi_ref, mix_ref):
        mixin = jnp.concatenate([a_ref[...], c_ref[...]], axis=1).astype(BF16)
        mi_ref[...] = mixin
        mix = _dot(mixin, w_ref[...])
        mix_ref[...] = mix.astype(BF16)
        r = DN_ALPHA * x_ref[...] + (1.0 + gt_ref[0]) * mix
        r_ref[...] = r
        xhat, _ = _ln_stats(r)
        xo_ref[...] = xhat * lg_ref[...] + lb_ref[...]

    tile = pl.BlockSpec((tm, dm), lambda i: (i, 0))
    htile = pl.BlockSpec((tm, half), lambda i: (i, 0))
    return _call(
        body, job, name=name, grid=(tokens // tm,),
        in_specs=[tile, htile, htile, _mod_spec(tiles_per_seq, dm), _const_spec(w_out.shape),
                  _const_spec((1, dm)), _const_spec((1, dm))],
        out_specs=[tile, tile, tile, tile],
        out_shape=[jax.ShapeDtypeStruct((tokens, dm), F32), jax.ShapeDtypeStruct((tokens, dm), F32),
                   jax.ShapeDtypeStruct((tokens, dm), BF16), jax.ShapeDtypeStruct((tokens, dm), BF16)],
        args=(x1, attn, conv, gt, w_out, ln_g, ln_b))


def out_bwd(dy, r, mix, gt, w_out, ln_g, seq, name, job=None):
    tokens, dm = r.shape
    half = dm // 2
    tm = min(MIX_TILE, seq)
    tiles_per_seq = seq // tm
    nseq = tokens // seq

    def body(dy_ref, r_ref, mix_ref, gt_ref, w_ref, lg_ref, dres_ref, da_ref, dc_ref, dmix_ref, dln_ref, dgt_ref):
        i = pl.program_id(0)
        dr, dgain, dbias = _ln_bwd(dy_ref[...], r_ref[...], lg_ref[...])

        @pl.when(i == 0)
        def _():
            dln_ref[...] = jnp.zeros_like(dln_ref)

        @pl.when(i % tiles_per_seq == 0)
        def _():
            dgt_ref[...] = jnp.zeros_like(dgt_ref)

        dln_ref[0:1, :] += dgain
        dln_ref[1:2, :] += dbias
        dgt_ref[0] += jnp.sum(dr * mix_ref[...].astype(F32), axis=0, keepdims=True)
        dres_ref[...] = DN_ALPHA * dr
        dmix = ((1.0 + gt_ref[0]) * dr).astype(BF16)
        dmix_ref[...] = dmix
        dmixin = _dot_nt(dmix, w_ref[...])
        da_ref[...] = dmixin[:, :half].astype(BF16)
        dc_ref[...] = dmixin[:, half:].astype(BF16)

    tile = pl.BlockSpec((tm, dm), lambda i: (i, 0))
    htile = pl.BlockSpec((tm, half), lambda i: (i, 0))
    return _call(
        body, job, name=name, grid=(tokens // tm,),
        in_specs=[tile, tile, tile, _mod_spec(tiles_per_seq, dm), _const_spec(w_out.shape), _const_spec((1, dm))],
        out_specs=[tile, htile, htile, tile, pl.BlockSpec((2, dm), lambda i: (0, 0)),
                   pl.BlockSpec((1, 1, dm), lambda i: (i // tiles_per_seq, 0, 0))],
        out_shape=[jax.ShapeDtypeStruct((tokens, dm), F32), jax.ShapeDtypeStruct((tokens, half), BF16),
                   jax.ShapeDtypeStruct((tokens, half), BF16), jax.ShapeDtypeStruct((tokens, dm), BF16),
                   jax.ShapeDtypeStruct((2, dm), F32), jax.ShapeDtypeStruct((nseq, 1, dm), F32)],
        args=(dy, r, mix, gt, w_out, ln_g))


def proj_bwd(parts, dres, x1, sh, sc, w_in, r_prev, f_prev, gt_prev, ln_g_prev, seq, name, job=None):
    tokens, dm = x1.shape
    tm = min(MIX_TILE, seq)
    tiles_per_seq = seq // tm
    nseq = tokens // seq
    widths = [p.shape[1] for p in parts]
    total = sum(widths)

    def body(*refs):
        part_refs = refs[:6]
        (dres_ref, x_ref, sh_ref, sc_ref, w_ref, r_ref, f_ref, gt_ref, lg_ref,
         dr_ref, df_ref, dproj_ref, h_ref, dmod_ref, dln_ref, dgt_ref) = refs[6:]
        i = pl.program_id(0)
        dproj = jnp.concatenate([p[...].astype(BF16) for p in part_refs], axis=1)
        dproj_ref[...] = dproj
        dh = _dot(dproj, w_ref[...])
        xx = x_ref[...]
        one_sc = 1.0 + sc_ref[0]
        h_ref[...] = (xx * one_sc + sh_ref[0]).astype(BF16)
        dr, dgain, dbias = _ln_bwd(dres_ref[...] + dh * one_sc, r_ref[...], lg_ref[...])
        dr_ref[...] = dr
        df_ref[...] = ((0.5 * (1.0 + gt_ref[0])) * dr).astype(BF16)

        @pl.when(i == 0)
        def _():
            dln_ref[...] = jnp.zeros_like(dln_ref)

        @pl.when(i % tiles_per_seq == 0)
        def _():
            dmod_ref[...] = jnp.zeros_like(dmod_ref)
            dgt_ref[...] = jnp.zeros_like(dgt_ref)

        dmod_ref[0, 0:1, :] += jnp.sum(dh, axis=0, keepdims=True)
        dmod_ref[0, 1:2, :] += jnp.sum(dh * xx, axis=0, keepdims=True)
        dln_ref[0:1, :] += dgain
        dln_ref[1:2, :] += dbias
        dgt_ref[0] += jnp.sum(dr * (0.5 * f_ref[...].astype(F32)), axis=0, keepdims=True)

    tile = pl.BlockSpec((tm, dm), lambda i: (i, 0))
    mod = _mod_spec(tiles_per_seq, dm)
    return _call(
        body, job, name=name, grid=(tokens // tm,),
        in_specs=[pl.BlockSpec((tm, wdt), lambda i: (i, 0)) for wdt in widths]
        + [tile, tile, mod, mod, _const_spec(w_in.shape), tile, tile, mod, _const_spec((1, dm))],
        out_specs=[tile, tile, pl.BlockSpec((tm, total), lambda i: (i, 0)), tile,
                   pl.BlockSpec((1, 2, dm), lambda i: (i // tiles_per_seq, 0, 0)),
                   pl.BlockSpec((2, dm), lambda i: (0, 0)), mod],
        out_shape=[jax.ShapeDtypeStruct((tokens, dm), F32), jax.ShapeDtypeStruct((tokens, dm), BF16),
                   jax.ShapeDtypeStruct((tokens, total), BF16), jax.ShapeDtypeStruct((tokens, dm), BF16),
                   jax.ShapeDtypeStruct((nseq, 2, dm), F32), jax.ShapeDtypeStruct((2, dm), F32),
                   jax.ShapeDtypeStruct((nseq, 1, dm), F32)],
        args=(*parts, dres, x1, sh, sc, w_in, r_prev, f_prev, gt_prev, ln_g_prev))


def _rope_tables(positions):
    half = ROT_DIM // 2
    inv_freq = jnp.power(jnp.float32(ROPE_THETA), -jnp.arange(0, ROT_DIM, 2, dtype=F32) / ROT_DIM)
    lane = jnp.arange(LANES) % HEAD_DIM
    freq = jnp.where(lane < ROT_DIM, inv_freq[lane % half], 0.0)
    sign = jnp.where(lane < half, -1.0, 1.0).astype(F32)
    ang = positions.astype(F32)[:, None] * freq[None, :]
    return jnp.cos(ang), sign[None, :] * jnp.sin(ang)


def kernel(x, c, positions, w_ada, b_ada, ffn1_w_gate_up, ffn1_w_down, ln1_g, ln1_b, w_in, conv_w, attn_sinks, w_out, ln2_g, ln2_b, ffn2_w_gate_up, ffn2_w_down, ln3_g, ln3_b, loss_target, m_w_ada, m_b_ada, m_ffn1_w_gate_up, m_ffn1_w_down, m_ln1_g, m_ln1_b, m_w_in, m_conv_w, m_attn_sinks, m_w_out, m_ln2_g, m_ln2_b, m_ffn2_w_gate_up, m_ffn2_w_down, m_ln3_g, m_ln3_b, v_w_ada, v_b_ada, v_ffn1_w_gate_up, v_ffn1_w_down, v_ln1_g, v_ln1_b, v_w_in, v_conv_w, v_attn_sinks, v_w_out, v_ln2_g, v_ln2_b, v_ffn2_w_gate_up, v_ffn2_w_down, v_ln3_g, v_ln3_b):
    nseq, seq, dm = x.shape
    tokens = nseq * seq
    dev = 4 * lax.axis_index("x") + 2 * lax.axis_index("y") + lax.axis_index("c")
    core = lax.axis_index("c").astype(jnp.int32).reshape(1)
    ada_cols = w_ada.shape[2]
    ff = ffn1_w_down.shape[1] * N_DEV
    fc = ff // 4
    in_cols = w_in.shape[2]
    conv_cols = conv_w.shape[2]

    def t_bf16(w):
        return w[0].T.astype(BF16)

    c_all, convw_all = all_gather([c, conv_w[0]], "gather_cond")
    c_all = c_all.reshape(N_DEV * nseq, dm)
    convw_full = convw_all.transpose(1, 0, 2).reshape(CONV_TAPS, N_DEV * conv_cols)

    b_cols = lax.dynamic_slice(b_ada, (0, dev * ada_cols), (1, ada_cols))
    cond_all, mod_cols = ada_fwd(c_all, w_ada[0], b_cols, "ada_fwd")
    wgu1, mod_all = all_gather([t_bf16(ffn1_w_gate_up), mod_cols], "gather_ffn1")
    wgu1 = wgu1.reshape(2, ff, dm)
    mod = lax.dynamic_slice(mod_all, (0, dev * nseq, 0), (N_DEV, nseq, ada_cols))
    mod = mod.transpose(1, 0, 2).reshape(nseq, 9, 1, dm)
    sh1, sc1, g1, sh2, sc2, g2, sh3, sc3, g3 = [mod[:, i] for i in range(9)]

    x0 = x.reshape(tokens, dm)
    (gu1, a1, h1), (wd1, wout) = ffn_up(x0, sh1, sc1, wgu1, seq, "ffn1_up",
                                        job=_GatherJob([ffn1_w_down[0].astype(BF16), w_out[0].astype(BF16)]))
    wd1, wout = wd1.reshape(ff, dm), wout.reshape(dm, dm)
    (x1, r1, f1), (win,) = ffn_down(x0, a1, g1, wd1, ln1_g, ln1_b, seq, "ffn1_down", job=_GatherJob([t_bf16(w_in)]))
    win = win.reshape(N_DEV * in_cols, dm)
    (q, k, v, u, bg, cg), wd2_spread = proj_fwd(x1, sh2, sc2, win, seq, "proj_fwd",
                                                job=gather_spread_job([ffn2_w_down[0].astype(BF16)]))
    cos_t, sin_t = _rope_tables(positions.reshape(tokens))
    sinks = attn_sinks[0]
    (attn, q_rot, probs, sink_probs), wgu2_spread = attn_fwd(q, k, v, cos_t, sin_t, sinks, seq, "attn_fwd",
                                                             job=gather_spread_job([t_bf16(ffn2_w_gate_up)]))
    conv = conv_fwd(u, bg, cg, convw_full, seq, "conv_fwd")
    (x2, r2, mixin, mix), (wd2, wgu2) = out_fwd(x1, attn, conv, g2, wout, ln2_g, ln2_b, seq, "out_fwd",
                                                job=gather_forward_job(wd2_spread + wgu2_spread))
    wd2, wgu2 = wd2.reshape(ff, dm), wgu2.reshape(2, ff, dm)
    target = loss_target.reshape(tokens, dm)
    dr3, df3, gu3, a3, h3, loss_part, dln3, dg3 = ffn_loss(x2, sh3, sc3, g3, wgu2, wd2, ln3_g, ln3_b, target, seq, "ffn2_fwd")

    (dx2, dgu3, dmod3), _ = ffn_bwd(dr3, df3, x2, gu3, sc3, wgu2, wd2, seq, "ffn2_bwd")
    pair = 2 * fc
    g_wd2 = tn_matmul(a3[None], df3[None], "ffn2_dwd", a_width=pair)[0][0].reshape(N_DEV, ff // N_DEV, dm)
    g_wgu2 = tn_matmul(dgu3, h3[None], "ffn2_dwgu", a_width=pair)[0][0].reshape(N_DEV, fc, dm)
    (dres2, dattn, dconv, dmix, dln2, dg2), swapped = out_bwd(dx2, r2, mix, g2, wout, ln2_g, seq, "out_bwd",
                                                              job=swap_job([g_wgu2, g_wd2]))
    p_wgu2, own_wgu2 = pair_sum(core, g_wgu2, swapped[0], "pair_wgu2")
    p_wd2, own_wd2 = pair_sum(core, g_wd2, swapped[1], "pair_wd2")
    du, dbg, dcg, dconvw = conv_bwd(dconv, u, bg, cg, convw_full, seq, "conv_bwd")
    (dq, dk, dv, dsink_rows), (far_wd2,) = attn_bwd(
        q_rot, k, v, dattn, probs, sink_probs, cos_t, sin_t, seq, "attn_bwd", job=chip_exchange_job([p_wd2]))
    parts = [dq, dk, dv, du, dbg, dcg]
    (dr1, df1, dproj, h2, dmod2, dln1, dg1), far_top = proj_bwd(
        parts, dres2, x1, sh2, sc2, win, r1, f1, g1, ln1_g, seq, "proj_bwd",
        job=chip_exchange_job([p_wgu2], rows=(0, fc // 2)))
    (dx0, dgu1, dmod1), _ = ffn_bwd(dr1, df1, x0, gu1, sc1, wgu1, wd1, seq, "ffn1_bwd")

    dmod = jnp.concatenate([dmod1, dg1, dmod2, dg2, dmod3, dg3], axis=1).reshape(nseq, 9 * dm)
    half = dm // 2
    jobs = _Jobs([gather_spread_job([dmod]),
                  chip_exchange_job([p_wgu2], rows=(fc // 2, fc // 2), into=far_top)])
    (g_wd1,), res = tn_matmul(a1[None], df1[None], "ffn1_dwd", job=jobs, a_width=pair)
    dmod_spread, (far_wgu2,) = jobs.split(res)
    g_wd1 = g_wd1.reshape(N_DEV, ff // N_DEV, dm)
    jobs = _Jobs([swap_job([g_wd1]), gather_forward_job(dmod_spread)])
    (g_l,), res = tn_matmul(dgu1, h1[None], "ffn1_dwgu_l", job=jobs, b_cols=(0, half), a_width=pair)
    (sw_wd1,), (dmod_all,) = jobs.split(res)
    g_l = g_l.reshape(N_DEV, fc, half)
    p_wd1, own_wd1 = pair_sum(core, g_wd1, sw_wd1, "pair_wd1")
    jobs = _Jobs([chip_exchange_job([p_wd1]), swap_job([g_l])])
    (g_r,), res = tn_matmul(dgu1, h1[None], "ffn1_dwgu_r", job=jobs, b_cols=(1, half), a_width=pair)
    (far_wd1,), (sw_l,) = jobs.split(res)
    g_r = g_r.reshape(N_DEV, fc, half)
    p_l, own_l = pair_sum(core, g_l, sw_l, "pair_wgu1_l")

    dmod_cols = lax.dynamic_slice(dmod_all.reshape(N_DEV * nseq, 9 * dm), (0, dev * ada_cols), (N_DEV * nseq, ada_cols))
    grad_w_ada, gb_cols = ada_bwd(cond_all, dmod_cols, "ada_bwd")
    dsinks = jnp.sum(dsink_rows.reshape(nseq, N_Q_HEADS, ATTN_BLOCK), axis=(0, 2))
    small = jnp.zeros((8, dm), F32)
    small = small.at[0:2].set(dln1).at[2:4].set(dln2).at[4:6].set(dln3)
    small = small.at[6, 0:N_Q_HEADS].set(dsinks).at[7, 0].set(loss_part[0, 0])

    jobs = _Jobs([chip_exchange_job([p_l]), swap_job([g_r]), gather_spread_job([small, dconvw, gb_cols])])
    (g_win,), res = tn_matmul(dproj[None], h2[None], "dwin", job=jobs)
    (far_l,), (sw_r,), small_spread = jobs.split(res)
    g_win = g_win.reshape(N_DEV, in_cols, dm)
    p_r, own_r = pair_sum(core, g_r, sw_r, "pair_wgu1_r")
    jobs = _Jobs([chip_exchange_job([p_r]), swap_job([g_win]), gather_forward_job(small_spread)])
    (g_wout,), res = tn_matmul(mixin[None], dmix[None], "dwout", job=jobs)
    (far_r,), (sw_win,), (small_all, dconvw_all, gb_all) = jobs.split(res)
    g_wout = g_wout.reshape(N_DEV, dm // N_DEV, dm)
    p_win, own_win = pair_sum(core, g_win, sw_win, "pair_win")
    jobs = _Jobs([chip_exchange_job([p_win]), swap_job([g_wout])])
    (far_win,), (sw_wout,) = jobs.split(run_job(jobs, "rs_tail_win"))
    p_wout, own_wout = pair_sum(core, g_wout, sw_wout, "pair_wout")
    (far_wout,) = run_job(chip_exchange_job([p_wout]), "rs_tail_wout")

    grads = {
        "ffn1_w_gate_up": jnp.concatenate([own_l, own_r], axis=1), "ffn1_w_down": own_wd1,
        "w_in": own_win, "w_out": own_wout, "ffn2_w_gate_up": own_wgu2, "ffn2_w_down": own_wd2,
    }
    others = {"ffn1_w_gate_up": jnp.concatenate([far_l, far_r], axis=2), "ffn1_w_down": far_wd1,
              "w_in": far_win, "w_out": far_wout, "ffn2_w_gate_up": far_wgu2, "ffn2_w_down": far_wd2}

    grads["w_ada"] = grad_w_ada
    small_sum = sum_devices(small_all, "sum_small")
    dconvw_sum = sum_devices(dconvw_all, "sum_convw")
    loss = small_sum[7, 0]
    grads["b_ada"] = gb_all.reshape(1, N_DEV * ada_cols)
    grads["conv_w"] = lax.dynamic_slice(dconvw_sum, (0, dev * conv_cols), (CONV_TAPS, conv_cols))
    grads["attn_sinks"] = small_sum[6:7, 0:N_Q_HEADS]
    for i, nm in enumerate(["ln1_g", "ln1_b", "ln2_g", "ln2_b", "ln3_g", "ln3_b"]):
        grads[nm] = small_sum[i:i + 1]

    given = dict(w_ada=(w_ada, m_w_ada, v_w_ada), b_ada=(b_ada, m_b_ada, v_b_ada),
                 ffn1_w_gate_up=(ffn1_w_gate_up, m_ffn1_w_gate_up, v_ffn1_w_gate_up),
                 ffn1_w_down=(ffn1_w_down, m_ffn1_w_down, v_ffn1_w_down),
                 ln1_g=(ln1_g, m_ln1_g, v_ln1_g), ln1_b=(ln1_b, m_ln1_b, v_ln1_b),
                 w_in=(w_in, m_w_in, v_w_in), conv_w=(conv_w, m_conv_w, v_conv_w),
                 attn_sinks=(attn_sinks, m_attn_sinks, v_attn_sinks), w_out=(w_out, m_w_out, v_w_out),
                 ln2_g=(ln2_g, m_ln2_g, v_ln2_g), ln2_b=(ln2_b, m_ln2_b, v_ln2_b),
                 ffn2_w_gate_up=(ffn2_w_gate_up, m_ffn2_w_gate_up, v_ffn2_w_gate_up),
                 ffn2_w_down=(ffn2_w_down, m_ffn2_w_down, v_ffn2_w_down),
                 ln3_g=(ln3_g, m_ln3_g, v_ln3_g), ln3_b=(ln3_b, m_ln3_b, v_ln3_b))
    order = ["w_ada", "b_ada", "ffn1_w_gate_up", "ffn1_w_down", "ln1_g", "ln1_b", "w_in", "conv_w", "attn_sinks",
             "w_out", "ln2_g", "ln2_b", "ffn2_w_gate_up", "ffn2_w_down", "ln3_g", "ln3_b"]
    transposed = ("ffn1_w_gate_up", "ffn2_w_gate_up", "w_in")
    big = ("w_ada", "ffn1_w_gate_up", "ffn1_w_down", "w_in", "w_out", "ffn2_w_gate_up", "ffn2_w_down")
    results = {}
    for nm in big:
        if nm in transposed:
            w2, m2, v2 = [t[0].T for t in given[nm]]
            results[nm] = [t.T[None] for t in adamw(w2, grads[nm], m2, v2, "adamw_" + nm, others=others.get(nm))]
        else:
            w2, m2, v2 = [t[0] for t in given[nm]]
            results[nm] = [t[None] for t in adamw(w2, grads[nm], m2, v2, "adamw_" + nm, others=others.get(nm))]
    small_names = [nm for nm in order if nm not in big]
    items = []
    for nm in small_names:
        shape = given[nm][0].shape
        two_d = (shape[-2], shape[-1])
        items.append((given[nm][0].reshape(two_d), grads[nm].reshape(two_d), *[t.reshape(two_d) for t in given[nm][1:]]))
    for nm, res in zip(small_names, adamw_small(items, "adamw_small")):
        shape = given[nm][0].shape
        results[nm] = [grads[nm].reshape(shape)] + [t.reshape(shape) for t in res]
    grad_x = dx0.reshape(nseq, seq, dm)
    return (loss, grad_x, *[results[nm][i] for i in range(4) for nm in order])
```

```python
import functools

import jax
import jax.numpy as jnp
from jax import lax
from jax.experimental import pallas as pl
from jax.experimental.pallas import tpu as pltpu

F32 = jnp.float32
BF16 = jnp.bfloat16
MESH = pl.DeviceIdType.MESH

N_DEV = 8
N_CHIP = 4
HEAD_DIM = 64
N_Q_HEADS = 8
N_KV_HEADS = 2
GQA_GROUP = N_Q_HEADS // N_KV_HEADS
ATTN_BLOCK = 128
ROT_DIM = 16
ROPE_THETA = 500000.0
CONV_TAPS = 3
LN_EPS = 1e-5
DN_ALPHA = 2.0 ** 0.25
ADAM_LR = 0.001
ADAM_B1 = 0.9
ADAM_B2 = 0.999
ADAM_EPS = 1e-08
ADAM_WD = 0.01
ADAM_STEP = 10
NEG_BIG = -1e30

VMEM_LIMIT = 56 * 1024 * 1024
TOKEN_TILE = 256
FFN_FWD_TILE = 512
MIX_TILE = 512
TN_VMEM_BUDGET = 36 * 1024 * 1024


def _params(semantics=None, vmem=VMEM_LIMIT):
    return pltpu.CompilerParams(dimension_semantics=semantics, vmem_limit_bytes=vmem)


def _dot(a, b):
    return jnp.dot(a, b, preferred_element_type=F32)


def _dot_nt(a, b):
    return lax.dot_general(a, b, (((1,), (1,)), ((), ())), preferred_element_type=F32)


def _dot_tn(a, b):
    return lax.dot_general(a, b, (((0,), (0,)), ((), ())), preferred_element_type=F32)


def _sigmoid(x):
    return pl.reciprocal(1.0 + jnp.exp(-x), approx=True)


def _ln_stats(r):
    mu = jnp.mean(r, axis=-1, keepdims=True)
    d = r - mu
    var = jnp.mean(d * d, axis=-1, keepdims=True)
    rstd = lax.rsqrt(var + LN_EPS)
    return d * rstd, rstd


def _ln_bwd(dy, r, g):
    return _ln_bwd_normalized(dy, *_ln_stats(r), g)


def _ln_bwd_normalized(dy, xhat, rstd, g):
    dxhat = dy * g
    c1 = jnp.mean(dxhat, axis=-1, keepdims=True)
    c2 = jnp.mean(dxhat * xhat, axis=-1, keepdims=True)
    dr = rstd * (dxhat - c1 - xhat * c2)
    return dr, jnp.sum(dy * xhat, axis=0, keepdims=True), jnp.sum(dy, axis=0, keepdims=True)


def _const_spec(shape):
    nd = len(shape)
    return pl.BlockSpec(shape, lambda *_: (0,) * nd, pipeline_mode=pl.Buffered(1))


def all_gather(arrs, name):
    n = len(arrs)

    def body(*refs):
        ins, outs = refs[:n], refs[n:2 * n]
        send_sems, recv_sems, local_sems = refs[2 * n:]
        x, y, c = lax.axis_index("x"), lax.axis_index("y"), lax.axis_index("c")
        me, sibling = (x, y, c), (x, y, 1 - c)
        chips = [(1 - x, y), (x, 1 - y), (1 - x, 1 - y)]

        def slot(i, p):
            return outs[i].at[4 * p[0] + 2 * p[1] + p[2]]

        def copy(i, k, block, to, src=None):
            return pltpu.make_async_remote_copy(
                src_ref=slot(i, block) if src is None else src, dst_ref=slot(i, block),
                send_sem=send_sems.at[i, k], recv_sem=recv_sems.at[i, k],
                device_id=to, device_id_type=MESH)

        mine = [pltpu.make_async_copy(ins[i], slot(i, me), local_sems.at[i]) for i in range(n)]
        for cp in mine:
            cp.start()
        first = []
        for i in range(n):
            first.append(copy(i, 0, me, sibling, src=ins[i]))
            first += [copy(i, 1 + j, me, (*chip, c), src=ins[i]) for j, chip in enumerate(chips)]
        for cp in first:
            cp.start()
        passed = []
        for i in range(n):
            for j, chip in enumerate(chips):
                copy(i, 1 + j, (*chip, c), me).wait_recv()
                cp = copy(i, 4 + j, (*chip, c), sibling)
                cp.start()
                passed.append(cp)
        for i in range(n):
            copy(i, 0, sibling, me).wait_recv()
            for j, chip in enumerate(chips):
                copy(i, 4 + j, (*chip, 1 - c), me).wait_recv()
        for cp in first + passed:
            cp.wait_send()
        for cp in mine:
            cp.wait()

    any_spec = pl.BlockSpec(memory_space=pl.ANY)
    return pl.pallas_call(
        body, name=name,
        out_shape=[jax.ShapeDtypeStruct((N_DEV, *a.shape), a.dtype) for a in arrs],
        in_specs=[any_spec] * n, out_specs=[any_spec] * n,
        scratch_shapes=[pltpu.SemaphoreType.DMA((n, 7)), pltpu.SemaphoreType.DMA((n, 7)),
                        pltpu.SemaphoreType.DMA((n,))],
    )(*arrs)


def _place():
    x, y, c = lax.axis_index("x"), lax.axis_index("y"), lax.axis_index("c")
    return x, y, c, [(1 - x, y), (x, 1 - y), (1 - x, 1 - y)]


def _slot(p):
    return 4 * p[0] + 2 * p[1] + p[2]


class _Job:
    def __init__(self, ins, outs, nsem, copies, aliases=None, local=None):
        self.ins, self.outs, self.nsem, self.copies = list(ins), list(outs), nsem, copies
        self.aliases = aliases or {}
        self.local = local

    def scratch(self):
        s = [pltpu.SemaphoreType.DMA(self.nsem), pltpu.SemaphoreType.DMA(self.nsem)]
        if self.local is not None:
            s.append(pltpu.SemaphoreType.DMA((len(self.ins),)))
        return s

    def start(self, ins, outs, sems):
        if self.local is not None:
            for cp in self.local(ins, outs, sems[2]):
                cp.start()
        for cp in self.copies(ins, outs, sems[0], sems[1])[0]:
            cp.start()

    def finish(self, ins, outs, sems):
        started, awaited = self.copies(ins, outs, sems[0], sems[1])
        for cp in awaited:
            cp.wait_recv()
        for cp in started:
            cp.wait_send()
        if self.local is not None:
            for cp in self.local(ins, outs, sems[2]):
                cp.wait()


class _Jobs:
    def __init__(self, jobs):
        self.jobs = jobs
        self.ins = [a for j in jobs for a in j.ins]
        self.outs = [o for j in jobs for o in j.outs]
        self.two_phase = any(getattr(j, "two_phase", False) for j in jobs)
        self.aliases = {}
        at_in = at_out = 0
        for j in jobs:
            self.aliases.update({at_in + i: at_out + o for i, o in j.aliases.items()})
            at_in, at_out = at_in + len(j.ins), at_out + len(j.outs)

    def scratch(self):
        return [s for j in self.jobs for s in j.scratch()]

    def _each(self, ins, outs, sems):
        at_in = at_out = at_sem = 0
        for j in self.jobs:
            n_in, n_out, n_sem = len(j.ins), len(j.outs), len(j.scratch())
            yield j, ins[at_in:at_in + n_in], outs[at_out:at_out + n_out], sems[at_sem:at_sem + n_sem]
            at_in, at_out, at_sem = at_in + n_in, at_out + n_out, at_sem + n_sem

    def start(self, ins, outs, sems):
        for j, i, o, s in self._each(ins, outs, sems):
            j.start(i, o, s)

    def turn(self, ins, outs, sems):
        for j, i, o, s in self._each(ins, outs, sems):
            if getattr(j, "two_phase", False):
                j.turn(i, o, s)

    def finish(self, ins, outs, sems):
        for j, i, o, s in self._each(ins, outs, sems):
            j.finish(i, o, s)

    def split(self, results):
        at, parts = 0, []
        for j in self.jobs:
            parts.append(results[at:at + len(j.outs)])
            at += len(j.outs)
        return parts


def _remote(src, dst, send, recv, idx, to):
    return pltpu.make_async_remote_copy(src_ref=src, dst_ref=dst, send_sem=send.at[idx], recv_sem=recv.at[idx],
                                        device_id=to, device_id_type=MESH)


def _spread_copies(ins, outs, send, recv, base=0):
    x, y, c, chips = _place()
    me = (x, y, c)
    peers = [(x, y, 1 - c)] + [(*chip, c) for chip in chips]
    started, awaited = [], []
    for i, (src, dst) in enumerate(zip(ins, outs)):
        for k, peer in enumerate(peers):
            started.append(_remote(src, dst.at[_slot(me)], send, recv, (base + i, k), peer))
            awaited.append(_remote(src, dst.at[_slot(peer)], send, recv, (base + i, k), peer))
    return started, awaited


def _forward_copies(ins, outs, send, recv, base=0):
    x, y, c, chips = _place()
    started, awaited = [], []
    for i, buf in enumerate(outs):
        for j, chip in enumerate(chips):
            mine, theirs = buf.at[_slot((*chip, c))], buf.at[_slot((*chip, 1 - c))]
            started.append(_remote(mine, mine, send, recv, (base + i, j), (x, y, 1 - c)))
            awaited.append(_remote(theirs, theirs, send, recv, (base + i, j), (x, y, 1 - c)))
    return started, awaited


def _own_block_copies(ins, outs, sems):
    x, y, c, _ = _place()
    return [pltpu.make_async_copy(src, dst.at[_slot((x, y, c))], sems.at[i])
            for i, (src, dst) in enumerate(zip(ins, outs))]


def gather_spread_job(shards):
    outs = [jax.ShapeDtypeStruct((N_DEV, *a.shape), a.dtype) for a in shards]
    return _Job(shards, outs, (len(shards), 4), _spread_copies, local=_own_block_copies)


def gather_forward_job(fulls):
    outs = [jax.ShapeDtypeStruct(a.shape, a.dtype) for a in fulls]
    return _Job(fulls, outs, (len(fulls), 3), _forward_copies, aliases={i: i for i in range(len(fulls))})


TURN_EIGHTHS = 6


class _GatherJob:
    two_phase = True

    def __init__(self, shards):
        self.ins = list(shards)
        self.outs = [jax.ShapeDtypeStruct((N_DEV, *a.shape), a.dtype) for a in shards]
        self.aliases = {}

    def scratch(self):
        n = len(self.ins)
        return [pltpu.SemaphoreType.DMA((n, 4)), pltpu.SemaphoreType.DMA((n, 4)),
                pltpu.SemaphoreType.DMA((n, 3)), pltpu.SemaphoreType.DMA((n, 3)), pltpu.SemaphoreType.DMA((n,))]

    def start(self, ins, outs, sems):
        for cp in _own_block_copies(ins, outs, sems[4]) + _spread_copies(ins, outs, sems[0], sems[1])[0]:
            cp.start()

    def turn(self, ins, outs, sems):
        for cp in _spread_copies(ins, outs, sems[0], sems[1])[1]:
            cp.wait_recv()
        for cp in _forward_copies(outs, outs, sems[2], sems[3])[0]:
            cp.start()

    def finish(self, ins, outs, sems):
        handed_on, arriving = _forward_copies(outs, outs, sems[2], sems[3])
        for cp in arriving:
            cp.wait_recv()
        for cp in _spread_copies(ins, outs, sems[0], sems[1])[0] + handed_on:
            cp.wait_send()
        for cp in _own_block_copies(ins, outs, sems[4]):
            cp.wait()


def swap_job(gs):
    def copies(ins, outs, send, recv):
        x, y, c, _ = _place()
        started, awaited = [], []
        for i, (g, r1) in enumerate(zip(ins, outs)):
            for q in range(N_CHIP):
                started.append(_remote(g.at[2 * q + (1 - c)], r1.at[q], send, recv, (i, q), (x, y, 1 - c)))
                awaited.append(_remote(g.at[2 * q + c], r1.at[q], send, recv, (i, q), (x, y, 1 - c)))
        return started, awaited

    outs = [jax.ShapeDtypeStruct((N_CHIP, *g.shape[1:]), g.dtype) for g in gs]
    return _Job(gs, outs, (len(gs), N_CHIP), copies)


def chip_exchange_job(ps, rows=None, into=None):
    n = len(ps)

    def copies(ins, outs, send, recv):
        x, y, c, chips = _place()
        started, awaited = [], []
        for i, (p, r2) in enumerate(zip(ins[:n], outs)):
            for k, chip in enumerate(chips):
                src, mine, dst = p.at[2 * chip[0] + chip[1]], p.at[2 * x + y], r2.at[k]
                if rows is not None:
                    src, mine, dst = (t.at[pl.ds(rows[0], rows[1])] for t in (src, mine, dst))
                started.append(_remote(src, dst, send, recv, (i, k), (*chip, c)))
                awaited.append(_remote(mine, dst, send, recv, (i, k), (*chip, c)))
        return started, awaited

    outs = [jax.ShapeDtypeStruct((3, *p.shape[1:]), p.dtype) for p in ps]
    if into is None:
        return _Job(ps, outs, (n, 3), copies)
    return _Job(list(ps) + list(into), outs, (n, 3), copies, aliases={n + i: i for i in range(n)})


def _call(body, job, *, name, grid, in_specs, out_specs, out_shape, args, scratch_shapes=(), vmem=VMEM_LIMIT):
    if job is None:
        res = pl.pallas_call(
            body, name=name, grid=grid, in_specs=in_specs, out_specs=out_specs, out_shape=out_shape,
            scratch_shapes=list(scratch_shapes), compiler_params=_params(("arbitrary",) * len(grid), vmem),
        )(*args)
        return res, []
    n_in, n_out, n_scr = len(in_specs), len(out_specs), len(scratch_shapes)
    j_in, j_out = len(job.ins), len(job.outs)

    def with_copies(*refs):
        at = 0
        ins = refs[at:at + n_in]; at += n_in
        jins = refs[at:at + j_in]; at += j_in
        outs = refs[at:at + n_out]; at += n_out
        jouts = refs[at:at + j_out]; at += j_out
        scr = refs[at:at + n_scr]; at += n_scr
        sems = refs[at:]
        ids = [pl.program_id(d) for d in range(len(grid))]
        first = functools.reduce(jnp.logical_and, [i == 0 for i in ids])
        last = functools.reduce(jnp.logical_and, [i == n - 1 for i, n in zip(ids, grid)])

        @pl.when(first)
        def _():
            job.start(jins, jouts, sems)

        if getattr(job, "two_phase", False):
            steps, at = 1, 0
            for i, n in zip(ids, grid):
                steps, at = steps * n, at * n + i

            @pl.when(at == (TURN_EIGHTHS * steps) // 8)
            def _():
                job.turn(jins, jouts, sems)

        body(*ins, *outs, *scr)

        @pl.when(last)
        def _():
            job.finish(jins, jouts, sems)

    any_spec = pl.BlockSpec(memory_space=pl.ANY)
    res = pl.pallas_call(
        with_copies, name=name, grid=grid,
        in_specs=list(in_specs) + [any_spec] * j_in, out_specs=list(out_specs) + [any_spec] * j_out,
        out_shape=list(out_shape) + list(job.outs),
        input_output_aliases={n_in + i: n_out + o for i, o in job.aliases.items()},
        scratch_shapes=list(scratch_shapes) + job.scratch(),
        compiler_params=_params(("arbitrary",) * len(grid), vmem),
    )(*args, *job.ins)
    return res[:n_out], res[n_out:]


def run_job(job, name):
    def body(*refs):
        j_in, j_out = len(job.ins), len(job.outs)
        ins, outs, sems = refs[:j_in], refs[j_in:j_in + j_out], refs[j_in + j_out:]
        job.start(ins, outs, sems)
        job.finish(ins, outs, sems)

    any_spec = pl.BlockSpec(memory_space=pl.ANY)
    return pl.pallas_call(
        body, name=name, in_specs=[any_spec] * len(job.ins), out_specs=[any_spec] * len(job.outs),
        out_shape=list(job.outs), input_output_aliases=dict(job.aliases), scratch_shapes=job.scratch(),
    )(*job.ins)


def pair_sum(core, g, r1, name):
    _, rows, cols = g.shape
    rb = next(cand for cand in range(min(rows, 512), 0, -16) if rows % cand == 0)

    def body(core_ref, g_ref, r1_ref, p_ref, own_ref):
        del core_ref
        x, y, _, _ = _place()
        s = g_ref[0].astype(F32) + r1_ref[0].astype(F32)
        p_ref[0] = s.astype(BF16)

        @pl.when(pl.program_id(1) == 2 * x + y)
        def _():
            own_ref[...] = s

    chunk = (1, rb, cols)
    return pl.pallas_call(
        body, name=name,
        grid_spec=pltpu.PrefetchScalarGridSpec(
            num_scalar_prefetch=1, grid=(rows // rb, N_CHIP),
            in_specs=[pl.BlockSpec(chunk, lambda i, q, core_ref: (2 * q + core_ref[0], i, 0)),
                      pl.BlockSpec(chunk, lambda i, q, core_ref: (q, i, 0))],
            out_specs=[pl.BlockSpec(chunk, lambda i, q, core_ref: (q, i, 0)),
                       pl.BlockSpec((rb, cols), lambda i, q, core_ref: (i, 0))]),
        out_shape=[jax.ShapeDtypeStruct((N_CHIP, rows, cols), BF16), jax.ShapeDtypeStruct((rows, cols), F32)],
        compiler_params=_params(("arbitrary", "arbitrary")),
    )(core, g, r1)


def sum_devices(a, name):
    def body(a_ref, o_ref):
        acc = a_ref[0]
        for d in range(1, N_DEV):
            acc = acc + a_ref[d]
        o_ref[...] = acc

    return pl.pallas_call(body, name=name, out_shape=jax.ShapeDtypeStruct(a.shape[1:], F32))(a)


def _adam_update(w, g, m, v):
    nm = ADAM_B1 * m + (1.0 - ADAM_B1) * g
    nv = ADAM_B2 * v + (1.0 - ADAM_B2) * (g * g)
    m_hat = nm / (1.0 - ADAM_B1 ** ADAM_STEP)
    v_hat = nv / (1.0 - ADAM_B2 ** ADAM_STEP)
    return -ADAM_LR * (m_hat / (jnp.sqrt(v_hat) + ADAM_EPS) + ADAM_WD * w), nm, nv


def adamw(w, g, m, v, name, others=None, job=None):
    rows, cols = w.shape
    rb = rows
    for cand in range(min(rows, 512), 7, -8):
        if rows % cand == 0 and cand % 8 == 0:
            rb = cand
            break

    def body(*refs):
        if others is None:
            w_ref, g_ref, m_ref, v_ref, d_ref, nm_ref, nv_ref = refs
            gg = g_ref[...]
        else:
            w_ref, g_ref, m_ref, v_ref, r2_ref, go_ref, d_ref, nm_ref, nv_ref = refs
            gg = g_ref[...]
            for k in range(3):
                gg = gg + r2_ref[k].astype(F32)
            go_ref[...] = gg
        d_ref[...], nm_ref[...], nv_ref[...] = _adam_update(w_ref[...], gg, m_ref[...], v_ref[...])

    spec = pl.BlockSpec((rb, cols), lambda i: (i, 0))
    out = jax.ShapeDtypeStruct((rows, cols), F32)
    in_specs, args = [spec] * 4, [w, g, m, v]
    if others is not None:
        in_specs.append(pl.BlockSpec((3, rb, cols), lambda i: (0, i, 0)))
        args.append(others)
    n_out = 3 if others is None else 4
    res, job_res = _call(body, job, name=name, grid=(rows // rb,), in_specs=in_specs, out_specs=[spec] * n_out,
                         out_shape=[out] * n_out, args=args)
    return ((g, *res) if others is None else tuple(res)), job_res


def adamw_small(items, name):
    n = len(items)

    def body(*refs):
        ins, outs = refs[:4 * n], refs[4 * n:]
        for i in range(n):
            w_ref, g_ref, m_ref, v_ref = ins[4 * i:4 * i + 4]
            d_ref, nm_ref, nv_ref = outs[3 * i:3 * i + 3]
            d_ref[...], nm_ref[...], nv_ref[...] = _adam_update(w_ref[...], g_ref[...], m_ref[...], v_ref[...])

    res = pl.pallas_call(
        body, name=name,
        out_shape=[jax.ShapeDtypeStruct(w.shape, F32) for w, _, _, _ in items for _ in range(3)],
    )(*[t for item in items for t in item])
    return [tuple(res[3 * i:3 * i + 3]) for i in range(n)]


def ada_fwd(c_all, w_cols, b_cols, name):
    def body(c_ref, w_ref, b_ref, cond_ref, mod_ref):
        cc = c_ref[...]
        cond = (cc * _sigmoid(cc)).astype(BF16)
        cond_ref[...] = cond
        mod_ref[...] = _dot(cond, w_ref[...].astype(BF16)) + b_ref[...]

    n, cols = c_all.shape[0], w_cols.shape[1]
    return pl.pallas_call(
        body, name=name,
        out_shape=[jax.ShapeDtypeStruct(c_all.shape, BF16), jax.ShapeDtypeStruct((n, cols), F32)],
        compiler_params=_params(),
    )(c_all, w_cols, b_cols)


def ada_bwd(cond_all, dmod_cols, name):
    def body(c_ref, d_ref, gw_ref, gb_ref):
        d = d_ref[...]
        gw_ref[...] = _dot_tn(c_ref[...], d.astype(BF16))
        gb_ref[...] = jnp.sum(d, axis=0, keepdims=True)

    dm, cols = cond_all.shape[1], dmod_cols.shape[1]
    return pl.pallas_call(
        body, name=name,
        out_shape=[jax.ShapeDtypeStruct((dm, cols), F32), jax.ShapeDtypeStruct((1, cols), F32)],
        compiler_params=_params(),
    )(cond_all, dmod_cols)


MXU_COLS = 256
FFN_CHUNK = 4 * MXU_COLS


def _hidden_chunks(ff):
    assert ff % MXU_COLS == 0
    return [(at, min(FFN_CHUNK, ff - at)) for at in range(0, ff, FFN_CHUNK)]


def _mod_spec(tiles_per_seq, dm):
    return pl.BlockSpec((1, 1, dm), lambda i: (i // tiles_per_seq, 0, 0))


def ffn_loss(x, sh, sc, gt, wgu, wd, ln_g, ln_b, target, seq, name):
    tokens, dm = x.shape
    ff = wgu.shape[1]
    chunks = _hidden_chunks(ff)
    tm = min(TOKEN_TILE, seq)
    tiles_per_seq = seq // tm

    def body(x_ref, sh_ref, sc_ref, gt_ref, wgu_ref, wd_ref, lg_ref, lb_ref, t_ref,
             dr_ref, df_ref, gu_ref, a_ref, h_ref, loss_ref, dln_ref, dgt_ref):
        i = pl.program_id(0)
        xx = x_ref[...]
        h = (xx * (1.0 + sc_ref[0]) + sh_ref[0]).astype(BF16)
        h_ref[...] = h
        acc = jnp.zeros((tm, dm), F32)
        for at, wdt in chunks:
            gk = _dot_nt(h, wgu_ref[0, at:at + wdt, :])
            uk = _dot_nt(h, wgu_ref[1, at:at + wdt, :])
            gu_ref[0, :, at:at + wdt] = gk.astype(BF16)
            gu_ref[1, :, at:at + wdt] = uk.astype(BF16)
            a = (gk * _sigmoid(gk) * uk).astype(BF16)
            a_ref[:, at:at + wdt] = a
            acc = acc + _dot(a, wd_ref[at:at + wdt, :])
        half_gate = 0.5 * (1.0 + gt_ref[0])
        xhat, rstd = _ln_stats(DN_ALPHA * xx + half_gate * acc)
        err = xhat * lg_ref[...] + lb_ref[...] - t_ref[...]
        dr, dgain, dbias = _ln_bwd_normalized(err * (1.0 / dm), xhat, rstd, lg_ref[...])
        dr_ref[...] = dr
        df_ref[...] = (half_gate * dr).astype(BF16)

        @pl.when(i == 0)
        def _():
            loss_ref[...] = jnp.zeros_like(loss_ref)
            dln_ref[...] = jnp.zeros_like(dln_ref)

        @pl.when(i % tiles_per_seq == 0)
        def _():
            dgt_ref[...] = jnp.zeros_like(dgt_ref)

        loss_ref[...] += jnp.full((1, 128), (0.5 / dm) * jnp.sum(err * err), F32)
        dln_ref[0:1, :] += dgain
        dln_ref[1:2, :] += dbias
        dgt_ref[0] += jnp.sum(dr * (0.5 * acc), axis=0, keepdims=True)

    tile = pl.BlockSpec((tm, dm), lambda i: (i, 0))
    mod = _mod_spec(tiles_per_seq, dm)
    res, _ = _call(
        body, None, name=name, grid=(tokens // tm,),
        in_specs=[tile, mod, mod, mod, _const_spec(wgu.shape), _const_spec(wd.shape),
                  _const_spec((1, dm)), _const_spec((1, dm)), tile],
        out_specs=[tile, tile, pl.BlockSpec((2, tm, ff), lambda i: (0, i, 0)), pl.BlockSpec((tm, ff), lambda i: (i, 0)),
                   tile, pl.BlockSpec((1, 128), lambda i: (0, 0)), pl.BlockSpec((2, dm), lambda i: (0, 0)), mod],
        out_shape=[jax.ShapeDtypeStruct((tokens, dm), F32), jax.ShapeDtypeStruct((tokens, dm), BF16),
                   jax.ShapeDtypeStruct((2, tokens, ff), BF16), jax.ShapeDtypeStruct((tokens, ff), BF16),
                   jax.ShapeDtypeStruct((tokens, dm), BF16), jax.ShapeDtypeStruct((1, 128), F32),
                   jax.ShapeDtypeStruct((2, dm), F32), jax.ShapeDtypeStruct((tokens // seq, 1, dm), F32)],
        args=(x, sh, sc, gt, wgu, wd, ln_g, ln_b, target))
    return res


def ffn_up(x, sh, sc, wgu, seq, name, job=None):
    tokens, dm = x.shape
    ff = wgu.shape[1]
    chunks = _hidden_chunks(ff)
    tm = min(FFN_FWD_TILE, seq)

    def body(x_ref, sh_ref, sc_ref, wgu_ref, gu_ref, a_ref, h_ref):
        h = (x_ref[...] * (1.0 + sc_ref[0]) + sh_ref[0]).astype(BF16)
        h_ref[...] = h
        for at, wdt in chunks:
            gk = _dot_nt(h, wgu_ref[0, at:at + wdt, :])
            uk = _dot_nt(h, wgu_ref[1, at:at + wdt, :])
            gu_ref[0, :, at:at + wdt] = gk.astype(BF16)
            gu_ref[1, :, at:at + wdt] = uk.astype(BF16)
            a_ref[:, at:at + wdt] = (gk * _sigmoid(gk) * uk).astype(BF16)

    tile = pl.BlockSpec((tm, dm), lambda i: (i, 0))
    mod = _mod_spec(seq // tm, dm)
    return _call(
        body, job, name=name, grid=(tokens // tm,),
        in_specs=[tile, mod, mod, _const_spec(wgu.shape)],
        out_specs=[pl.BlockSpec((2, tm, ff), lambda i: (0, i, 0)), pl.BlockSpec((tm, ff), lambda i: (i, 0)), tile],
        out_shape=[jax.ShapeDtypeStruct((2, tokens, ff), BF16), jax.ShapeDtypeStruct((tokens, ff), BF16),
                   jax.ShapeDtypeStruct((tokens, dm), BF16)],
        args=(x, sh, sc, wgu))


def ffn_down(x, a, gt, wd, ln_g, ln_b, seq, name, job=None):
    tokens, dm = x.shape
    ff = wd.shape[0]
    chunks = _hidden_chunks(ff)
    tm = min(FFN_FWD_TILE, seq)

    def body(x_ref, a_ref, gt_ref, wd_ref, lg_ref, lb_ref, xo_ref, r_ref, f_ref):
        acc = jnp.zeros((tm, dm), F32)
        for at, wdt in chunks:
            acc = acc + _dot(a_ref[:, at:at + wdt], wd_ref[at:at + wdt, :])
        f_ref[...] = acc.astype(BF16)
        r = DN_ALPHA * x_ref[...] + (0.5 * (1.0 + gt_ref[0])) * acc
        r_ref[...] = r
        xhat, _ = _ln_stats(r)
        xo_ref[...] = xhat * lg_ref[...] + lb_ref[...]

    tile = pl.BlockSpec((tm, dm), lambda i: (i, 0))
    return _call(
        body, job, name=name, grid=(tokens // tm,),
        in_specs=[tile, pl.BlockSpec((tm, ff), lambda i: (i, 0)), _mod_spec(seq // tm, dm), _const_spec(wd.shape),
                  _const_spec((1, dm)), _const_spec((1, dm))],
        out_specs=[tile, tile, tile],
        out_shape=[jax.ShapeDtypeStruct((tokens, dm), F32), jax.ShapeDtypeStruct((tokens, dm), F32),
                   jax.ShapeDtypeStruct((tokens, dm), BF16)],
        args=(x, a, gt, wd, ln_g, ln_b))


def ffn_bwd(dr, df, x, gu, sc, wgu, wd, seq, name, job=None):
    tokens, dm = x.shape
    ff = wgu.shape[1]
    chunks = _hidden_chunks(ff)
    tm = min(TOKEN_TILE, seq)
    tiles_per_seq = seq // tm
    nseq = tokens // seq

    def body(dr_ref, df_ref, x_ref, gu_ref, sc_ref, wgu_ref, wd_ref, dx_ref, dgu_ref, dmod_ref):
        @pl.when(pl.program_id(0) % tiles_per_seq == 0)
        def _():
            dmod_ref[...] = jnp.zeros_like(dmod_ref)

        df = df_ref[...]
        dh = jnp.zeros((tm, dm), F32)
        for at, wdt in chunks:
            cols = slice(at, at + wdt)
            da = _dot_nt(df, wd_ref[cols, :])
            gk = gu_ref[0, :, cols].astype(F32)
            uk = gu_ref[1, :, cols].astype(F32)
            sg = _sigmoid(gk)
            sil = gk * sg
            du = (da * sil).astype(BF16)
            dg = (da * uk * (sg * (1.0 + gk * (1.0 - sg)))).astype(BF16)
            dgu_ref[0, :, cols] = dg
            dgu_ref[1, :, cols] = du
            dh = dh + _dot(dg, wgu_ref[0, cols, :]) + _dot(du, wgu_ref[1, cols, :])
        dx_ref[...] = DN_ALPHA * dr_ref[...] + dh * (1.0 + sc_ref[0])
        dmod_ref[0, 0:1, :] += jnp.sum(dh, axis=0, keepdims=True)
        dmod_ref[0, 1:2, :] += jnp.sum(dh * x_ref[...], axis=0, keepdims=True)

    tile = pl.BlockSpec((tm, dm), lambda i: (i, 0))
    gu_spec = pl.BlockSpec((2, tm, ff), lambda i: (0, i, 0))
    return _call(
        body, job, name=name, grid=(tokens // tm,),
        in_specs=[tile, tile, tile, gu_spec, _mod_spec(tiles_per_seq, dm), _const_spec(wgu.shape), _const_spec(wd.shape)],
        out_specs=[tile, gu_spec, pl.BlockSpec((1, 2, dm), lambda i: (i // tiles_per_seq, 0, 0))],
        out_shape=[jax.ShapeDtypeStruct((tokens, dm), F32), jax.ShapeDtypeStruct((2, tokens, ff), BF16),
                   jax.ShapeDtypeStruct((nseq, 2, dm), F32)],
        args=(dr, df, x, gu, sc, wgu, wd))


def tn_matmul(a, b, name, job=None, b_cols=None, a_width=None):
    na, tokens, k_all = a.shape
    kk = k_all if a_width is None else a_width
    nka = k_all // kk
    assert nka * kk == k_all
    nb, _, cc = b.shape
    col = 0
    if b_cols is not None:
        col, cc = b_cols
    tt = tokens
    while 4 * tt * (kk + cc) + 8 * kk * cc > TN_VMEM_BUDGET and tt % 2 == 0 and tt > 256:
        tt //= 2
    steps = tokens // tt

    def body(a_ref, b_ref, o_ref, *acc):
        if steps == 1:
            o_ref[0, 0, 0] = _dot_tn(a_ref[0], b_ref[0]).astype(BF16)
            return
        acc_ref, = acc
        t = pl.program_id(3)

        @pl.when(t == 0)
        def _():
            acc_ref[...] = jnp.zeros_like(acc_ref)

        acc_ref[...] += _dot_tn(a_ref[0], b_ref[0])

        @pl.when(t == steps - 1)
        def _():
            o_ref[0, 0, 0] = acc_ref[...].astype(BF16)

    return _call(
        body, job, name=name, grid=(na, nka, nb, steps),
        in_specs=[pl.BlockSpec((1, tt, kk), lambda i, s, j, t: (i, t, s)),
                  pl.BlockSpec((1, tt, cc), lambda i, s, j, t: (j, t, col))],
        out_specs=[pl.BlockSpec((1, 1, 1, kk, cc), lambda i, s, j, t: (i, s, j, 0, 0))],
        out_shape=[jax.ShapeDtypeStruct((na, nka, nb, kk, cc), BF16)],
        scratch_shapes=[] if steps == 1 else [pltpu.VMEM((kk, cc), F32)], args=(a, b))


def proj_fwd(x1, sh, sc, w_in, seq, name, job=None):
    tokens, dm = x1.shape
    tm = min(MIX_TILE, seq)
    tiles_per_seq = seq // tm
    widths = [N_Q_HEADS * HEAD_DIM, N_KV_HEADS * HEAD_DIM, N_KV_HEADS * HEAD_DIM, 512, 512, 512]
    assert sum(widths) == w_in.shape[0]

    def body(x_ref, sh_ref, sc_ref, w_ref, *outs):
        h = (x_ref[...] * (1.0 + sc_ref[0]) + sh_ref[0]).astype(BF16)
        proj = _dot_nt(h, w_ref[...])
        at = 0
        for o_ref, wdt in zip(outs, widths):
            o_ref[...] = proj[:, at:at + wdt].astype(o_ref.dtype)
            at += wdt

    tile = pl.BlockSpec((tm, dm), lambda i: (i, 0))
    mod = _mod_spec(tiles_per_seq, dm)
    return _call(
        body, job, name=name, grid=(tokens // tm,),
        in_specs=[tile, mod, mod, _const_spec(w_in.shape)],
        out_specs=[pl.BlockSpec((tm, wdt), lambda i: (i, 0)) for wdt in widths],
        out_shape=[jax.ShapeDtypeStruct((tokens, wdt), F32 if i < 3 else BF16) for i, wdt in enumerate(widths)],
        args=(x1, sh, sc, w_in))


LANES = 2 * HEAD_DIM


def _head_lane(shape):
    return lax.broadcasted_iota(jnp.int32, shape, 1) % HEAD_DIM


def _lane_half(shape):
    return lax.broadcasted_iota(jnp.int32, shape, 1) // HEAD_DIM


def _swap_rot(v):
    lane = _head_lane(v.shape)
    half = ROT_DIM // 2
    return jnp.where(lane < half, pltpu.roll(v, LANES - half, 1),
                     jnp.where(lane < ROT_DIM, pltpu.roll(v, half, 1), 0.0))


def _rope(v, cos_t, sin_t):
    return v * cos_t + _swap_rot(v) * sin_t


def _unrope(dv, cos_t, sin_t):
    return dv * cos_t + _swap_rot(dv * sin_t)


def _both_halves(t, g):
    return jnp.where(_lane_half(t.shape) == g, t, pltpu.roll(t, HEAD_DIM, 1))


def _fold_halves(t, g):
    return jnp.where(_lane_half(t.shape) == g, t + pltpu.roll(t, HEAD_DIM, 1), 0.0)


def _stack_heads(blocks):
    rows = []
    for blk in blocks:
        half = _lane_half(blk.shape)
        rows += [jnp.where(half == 0, blk, 0.0), jnp.where(half == 1, blk, 0.0)]
    return jnp.concatenate(rows, axis=0)


def _unstack_heads(t, j):
    lo = t[(2 * j) * ATTN_BLOCK:(2 * j + 1) * ATTN_BLOCK]
    hi = t[(2 * j + 1) * ATTN_BLOCK:(2 * j + 2) * ATTN_BLOCK]
    return jnp.where(_lane_half(lo.shape) == 0, lo, hi)


def _band_mask(q0, w0):
    rows, cols = GQA_GROUP * ATTN_BLOCK, 2 * ATTN_BLOCK
    qi = lax.broadcasted_iota(jnp.int32, (rows, cols), 0) % ATTN_BLOCK + q0
    ki = lax.broadcasted_iota(jnp.int32, (rows, cols), 1) + w0
    diff = qi - ki
    return (diff >= 0) & (diff < ATTN_BLOCK)


def _attn_specs(seq):
    q_spec = pl.BlockSpec((seq, GQA_GROUP * HEAD_DIM), lambda b, g: (b, g))
    kv_spec = pl.BlockSpec((seq, LANES), lambda b, g: (b, 0))
    sink_spec = pl.BlockSpec((1, GQA_GROUP * ATTN_BLOCK, 1), lambda b, g: (g, 0, 0))
    return q_spec, kv_spec, sink_spec


def _block_starts(n):
    q0 = pl.multiple_of(n * ATTN_BLOCK, ATTN_BLOCK)
    w0 = pl.multiple_of(jnp.maximum(n - 1, 0) * ATTN_BLOCK, ATTN_BLOCK)
    return q0, w0


def _stacked_queries(ref, rows):
    return _stack_heads([ref[rows, j * LANES:(j + 1) * LANES] for j in range(2)]).astype(BF16)


def _sink_columns(sinks):
    return jnp.repeat(sinks.reshape(N_KV_HEADS, GQA_GROUP), ATTN_BLOCK, axis=1)[:, :, None]


def _probs_spec(nblk):
    return pl.BlockSpec((1, 1, nblk, GQA_GROUP * ATTN_BLOCK, 2 * ATTN_BLOCK), lambda b, g: (b, g, 0, 0, 0))


def _sink_probs_spec():
    return pl.BlockSpec((1, 1, GQA_GROUP * ATTN_BLOCK, LANES), lambda b, g: (b, g, 0, 0))


def attn_fwd(q, k, v, cos_t, sin_t, sinks, seq, name, job=None):
    tokens = q.shape[0]
    nblk = seq // ATTN_BLOCK
    assert nblk >= 2
    scale = HEAD_DIM ** -0.5

    nseq = tokens // seq
    rows_stacked = GQA_GROUP * ATTN_BLOCK
    assert nblk <= LANES

    def body(q_ref, k_ref, v_ref, cos_ref, sin_ref, sink_ref, o_ref, qr_ref, p_ref, ps_ref, kd_ref, vd_ref):
        g = pl.program_id(1)
        kd_ref[...] = _both_halves(_rope(k_ref[...].astype(F32), cos_ref[...], sin_ref[...]), g).astype(BF16)
        vd_ref[...] = _both_halves(v_ref[...].astype(F32), g).astype(BF16)
        sink = sink_ref[0]
        lane = lax.broadcasted_iota(jnp.int32, (rows_stacked, LANES), 1)

        ps_ref[...] = jnp.zeros_like(ps_ref)

        def block(n, carry):
            q0, w0 = _block_starts(n)
            rows, win = pl.ds(q0, ATTN_BLOCK), pl.ds(w0, 2 * ATTN_BLOCK)
            blocks = []
            for j in range(2):
                qr = _rope(q_ref[rows, j * LANES:(j + 1) * LANES].astype(F32), cos_ref[rows, :], sin_ref[rows, :]).astype(BF16)
                qr_ref[rows, j * LANES:(j + 1) * LANES] = qr
                blocks.append(qr)
            qs = _stack_heads(blocks)
            s = _dot_nt(qs, kd_ref[win, :]) * scale
            s = jnp.where(_band_mask(q0, w0), s, NEG_BIG)
            m = jnp.maximum(jnp.max(s, axis=-1, keepdims=True), sink)
            p = jnp.exp(s - m)
            e_sink = jnp.exp(sink - m)
            inv = pl.reciprocal(jnp.sum(p, axis=-1, keepdims=True) + e_sink, approx=True)
            pn = (p * inv).astype(BF16)
            p_ref[0, 0, n] = pn
            out = _dot(pn, vd_ref[win, :])
            for j in range(2):
                o_ref[rows, j * LANES:(j + 1) * LANES] = _unstack_heads(out, j).astype(o_ref.dtype)
            ps_ref[0, 0] = jnp.where(lane == n, e_sink * inv, ps_ref[0, 0])
            return carry

        lax.fori_loop(0, nblk, block, 0, unroll=2)

    q_spec, kv_spec, sink_spec = _attn_specs(seq)
    return _call(
        body, job, name=name, grid=(nseq, N_KV_HEADS),
        in_specs=[q_spec, kv_spec, kv_spec, kv_spec, kv_spec, sink_spec],
        out_specs=[q_spec, q_spec, _probs_spec(nblk), _sink_probs_spec()],
        out_shape=[jax.ShapeDtypeStruct(q.shape, BF16), jax.ShapeDtypeStruct(q.shape, BF16),
                   jax.ShapeDtypeStruct((nseq, N_KV_HEADS, nblk, rows_stacked, 2 * ATTN_BLOCK), BF16),
                   jax.ShapeDtypeStruct((nseq, N_KV_HEADS, rows_stacked, LANES), F32)],
        scratch_shapes=[pltpu.VMEM((seq, LANES), BF16), pltpu.VMEM((seq, LANES), BF16)],
        args=(q, k, v, cos_t, sin_t, _sink_columns(sinks)))


def attn_bwd(qr, k, v, do, probs, sink_probs, cos_t, sin_t, seq, name, job=None):
    tokens = qr.shape[0]
    nseq = tokens // seq
    nblk = seq // ATTN_BLOCK
    assert nblk >= 2
    rows_stacked = GQA_GROUP * ATTN_BLOCK
    scale = HEAD_DIM ** -0.5

    def body(q_ref, k_ref, v_ref, do_ref, p_ref, ps_ref, cos_ref, sin_ref, dq_ref, dk_ref, dv_ref, ds_ref,
             kd_ref, vd_ref, dkd_ref, dvd_ref, acc_ref):
        g = pl.program_id(1)
        kd_ref[...] = _both_halves(_rope(k_ref[...].astype(F32), cos_ref[...], sin_ref[...]), g).astype(BF16)
        vd_ref[...] = _both_halves(v_ref[...].astype(F32), g).astype(BF16)
        dkd_ref[...] = jnp.zeros_like(dkd_ref)
        dvd_ref[...] = jnp.zeros_like(dvd_ref)
        acc_ref[...] = jnp.zeros_like(acc_ref)
        lane = lax.broadcasted_iota(jnp.int32, (rows_stacked, LANES), 1)

        def block(n, carry):
            q0, w0 = _block_starts(n)
            rows, win = pl.ds(q0, ATTN_BLOCK), pl.ds(w0, 2 * ATTN_BLOCK)
            qs = _stacked_queries(q_ref, rows)
            dos = _stacked_queries(do_ref, rows)
            kw, vw = kd_ref[win, :], vd_ref[win, :]
            pn16 = p_ref[0, 0, n]
            pn = pn16.astype(F32)
            dvd_ref[win, :] += _dot_tn(pn16, dos)
            dp = _dot_nt(dos, vw)
            delta = jnp.sum(dp * pn, axis=-1, keepdims=True)
            ds = (pn * (dp - delta)).astype(BF16)
            dqs = _dot(ds, kw) * scale
            dkd_ref[win, :] += _dot_tn(ds, qs) * scale
            cos_b, sin_b = cos_ref[rows, :], sin_ref[rows, :]
            for j in range(2):
                dq_ref[rows, j * LANES:(j + 1) * LANES] = _unrope(_unstack_heads(dqs, j), cos_b, sin_b).astype(BF16)
            acc_ref[...] += jnp.where(lane == n, ps_ref[0, 0] * delta, 0.0)
            return carry

        lax.fori_loop(0, nblk // 2, lambda i, carry: block(2 * i + 1, block(2 * i, carry)), 0)
        ds_ref[0, 0] = -jnp.sum(acc_ref[...], axis=-1, keepdims=True)
        dk_g = _unrope(_fold_halves(dkd_ref[...], g), cos_ref[...], sin_ref[...])
        dv_g = _fold_halves(dvd_ref[...], g)

        @pl.when(g == 0)
        def _():
            dk_ref[...] = dk_g
            dv_ref[...] = dv_g

        @pl.when(g != 0)
        def _():
            dk_ref[...] += dk_g
            dv_ref[...] += dv_g

    q_spec, kv_spec, _ = _attn_specs(seq)
    return _call(
        body, job, name=name, grid=(nseq, N_KV_HEADS),
        in_specs=[q_spec, kv_spec, kv_spec, q_spec, _probs_spec(nblk), _sink_probs_spec(), kv_spec, kv_spec],
        out_specs=[q_spec, kv_spec, kv_spec, pl.BlockSpec((1, 1, rows_stacked, 1), lambda b, g: (b, g, 0, 0))],
        out_shape=[jax.ShapeDtypeStruct(qr.shape, BF16), jax.ShapeDtypeStruct(k.shape, F32),
                   jax.ShapeDtypeStruct(k.shape, F32), jax.ShapeDtypeStruct((nseq, N_KV_HEADS, rows_stacked, 1), F32)],
        scratch_shapes=[pltpu.VMEM((seq, LANES), BF16), pltpu.VMEM((seq, LANES), BF16),
                        pltpu.VMEM((seq, LANES), F32), pltpu.VMEM((seq, LANES), F32),
                        pltpu.VMEM((rows_stacked, LANES), F32)],
        args=(qr, k, v, do, probs, sink_probs, cos_t, sin_t))


CONV_COLS = 128


def _shift_down(z, by):
    t = lax.broadcasted_iota(jnp.int32, z.shape, 0)
    return jnp.where(t >= by, pltpu.roll(z, by, 0), 0.0)


def _shift_up(z, by):
    n = z.shape[0]
    t = lax.broadcasted_iota(jnp.int32, z.shape, 0)
    return jnp.where(t < n - by, pltpu.roll(z, n - by, 0), 0.0)


def conv_fwd(u, bg, cg, conv_w, seq, name):
    tokens, width = u.shape

    def body(u_ref, bg_ref, cg_ref, w_ref, o_ref):
        z = cg_ref[...].astype(F32) * u_ref[...].astype(F32)
        yy = w_ref[2:3, :] * z + w_ref[1:2, :] * _shift_down(z, 1) + w_ref[0:1, :] * _shift_down(z, 2)
        o_ref[...] = (bg_ref[...].astype(F32) * yy).astype(BF16)

    col = pl.BlockSpec((seq, CONV_COLS), lambda j, b: (b, j))
    return pl.pallas_call(
        body, name=name, grid=(width // CONV_COLS, tokens // seq),
        in_specs=[col, col, col, pl.BlockSpec((CONV_TAPS, CONV_COLS), lambda j, b: (0, j))],
        out_specs=col, out_shape=jax.ShapeDtypeStruct((tokens, width), BF16),
        compiler_params=_params(("parallel", "parallel")),
    )(u, bg, cg, conv_w)


def conv_bwd(dout, u, bg, cg, conv_w, seq, name):
    tokens, width = u.shape

    def body(do_ref, u_ref, bg_ref, cg_ref, w_ref, du_ref, dbg_ref, dcg_ref, dw_ref):
        uu, cg_v, do = u_ref[...].astype(F32), cg_ref[...].astype(F32), do_ref[...].astype(F32)
        z = cg_v * uu
        z1, z2 = _shift_down(z, 1), _shift_down(z, 2)
        yy = w_ref[2:3, :] * z + w_ref[1:2, :] * z1 + w_ref[0:1, :] * z2
        dbg_ref[...] = (do * yy).astype(BF16)
        dyy = do * bg_ref[...].astype(F32)
        dz = w_ref[2:3, :] * dyy + w_ref[1:2, :] * _shift_up(dyy, 1) + w_ref[0:1, :] * _shift_up(dyy, 2)
        du_ref[...] = (dz * cg_v).astype(BF16)
        dcg_ref[...] = (dz * uu).astype(BF16)

        @pl.when(pl.program_id(1) == 0)
        def _():
            dw_ref[...] = jnp.zeros_like(dw_ref)

        dw_ref[0:1, :] += jnp.sum(dyy * z2, axis=0, keepdims=True)
        dw_ref[1:2, :] += jnp.sum(dyy * z1, axis=0, keepdims=True)
        dw_ref[2:3, :] += jnp.sum(dyy * z, axis=0, keepdims=True)

    col = pl.BlockSpec((seq, CONV_COLS), lambda j, b: (b, j))
    w_spec = pl.BlockSpec((CONV_TAPS, CONV_COLS), lambda j, b: (0, j))
    act = jax.ShapeDtypeStruct((tokens, width), BF16)
    return pl.pallas_call(
        body, name=name, grid=(width // CONV_COLS, tokens // seq),
        in_specs=[col, col, col, col, w_spec], out_specs=[col, col, col, w_spec],
        out_shape=[act, act, act, jax.ShapeDtypeStruct((CONV_TAPS, width), F32)],
        compiler_params=_params(("parallel", "arbitrary")),
    )(dout, u, bg, cg, conv_w)


def out_fwd(x1, attn, conv, gt, w_out, ln_g, ln_b, seq, name, job=None):
    tokens, dm = x1.shape
    half = attn.shape[1]
    tm = min(MIX_TILE, seq)
    tiles_per_seq = seq // tm

    def body(x_ref, a_ref, c_ref, gt_ref, w_ref, lg_ref, lb_ref, xo_ref, r_ref, mi_ref, mix_ref):
        mixin = jnp.concatenate([a_ref[...], c_ref[...]], axis=1).astype(BF16)
        mi_ref[...] = mixin
        mix = _dot(mixin, w_ref[...])
        mix_ref[...] = mix.astype(BF16)
        r = DN_ALPHA * x_ref[...] + (1.0 + gt_ref[0]) * mix
        r_ref[...] = r
        xhat, _ = _ln_stats(r)
        xo_ref[...] = xhat * lg_ref[...] + lb_ref[...]

    tile = pl.BlockSpec((tm, dm), lambda i: (i, 0))
    htile = pl.BlockSpec((tm, half), lambda i: (i, 0))
    return _call(
        body, job, name=name, grid=(tokens // tm,),
        in_specs=[tile, htile, htile, _mod_spec(tiles_per_seq, dm), _const_spec(w_out.shape),
                  _const_spec((1, dm)), _const_spec((1, dm))],
        out_specs=[tile, tile, tile, tile],
        out_shape=[jax.ShapeDtypeStruct((tokens, dm), F32), jax.ShapeDtypeStruct((tokens, dm), F32),
                   jax.ShapeDtypeStruct((tokens, dm), BF16), jax.ShapeDtypeStruct((tokens, dm), BF16)],
        args=(x1, attn, conv, gt, w_out, ln_g, ln_b))


def out_bwd(dy, r, mix, gt, w_out, ln_g, seq, name, job=None):
    tokens, dm = r.shape
    half = dm // 2
    tm = min(MIX_TILE, seq)
    tiles_per_seq = seq // tm
    nseq = tokens // seq

    def body(dy_ref, r_ref, mix_ref, gt_ref, w_ref, lg_ref, dres_ref, da_ref, dc_ref, dmix_ref, dln_ref, dgt_ref):
        i = pl.program_id(0)
        dr, dgain, dbias = _ln_bwd(dy_ref[...], r_ref[...], lg_ref[...])

        @pl.when(i == 0)
        def _():
            dln_ref[...] = jnp.zeros_like(dln_ref)

        @pl.when(i % tiles_per_seq == 0)
        def _():
            dgt_ref[...] = jnp.zeros_like(dgt_ref)

        dln_ref[0:1, :] += dgain
        dln_ref[1:2, :] += dbias
        dgt_ref[0] += jnp.sum(dr * mix_ref[...].astype(F32), axis=0, keepdims=True)
        dres_ref[...] = DN_ALPHA * dr
        dmix = ((1.0 + gt_ref[0]) * dr).astype(BF16)
        dmix_ref[...] = dmix
        dmixin = _dot_nt(dmix, w_ref[...])
        da_ref[...] = dmixin[:, :half].astype(BF16)
        dc_ref[...] = dmixin[:, half:].astype(BF16)

    tile = pl.BlockSpec((tm, dm), lambda i: (i, 0))
    htile = pl.BlockSpec((tm, half), lambda i: (i, 0))
    return _call(
        body, job, name=name, grid=(tokens // tm,),
        in_specs=[tile, tile, tile, _mod_spec(tiles_per_seq, dm), _const_spec(w_out.shape), _const_spec((1, dm))],
        out_specs=[tile, htile, htile, tile, pl.BlockSpec((2, dm), lambda i: (0, 0)),
                   pl.BlockSpec((1, 1, dm), lambda i: (i // tiles_per_seq, 0, 0))],
        out_shape=[jax.ShapeDtypeStruct((tokens, dm), F32), jax.ShapeDtypeStruct((tokens, half), BF16),
                   jax.ShapeDtypeStruct((tokens, half), BF16), jax.ShapeDtypeStruct((tokens, dm), BF16),
                   jax.ShapeDtypeStruct((2, dm), F32), jax.ShapeDtypeStruct((nseq, 1, dm), F32)],
        args=(dy, r, mix, gt, w_out, ln_g))


def proj_bwd(parts, dres, x1, sh, sc, w_in, r_prev, f_prev, gt_prev, ln_g_prev, seq, name, job=None):
    tokens, dm = x1.shape
    tm = min(MIX_TILE, seq)
    tiles_per_seq = seq // tm
    nseq = tokens // seq
    widths = [p.shape[1] for p in parts]
    total = sum(widths)

    def body(*refs):
        part_refs = refs[:6]
        (dres_ref, x_ref, sh_ref, sc_ref, w_ref, r_ref, f_ref, gt_ref, lg_ref,
         dr_ref, df_ref, dproj_ref, h_ref, dmod_ref, dln_ref, dgt_ref) = refs[6:]
        i = pl.program_id(0)
        dproj = jnp.concatenate([p[...].astype(BF16) for p in part_refs], axis=1)
        dproj_ref[...] = dproj
        dh = _dot(dproj, w_ref[...])
        xx = x_ref[...]
        one_sc = 1.0 + sc_ref[0]
        h_ref[...] = (xx * one_sc + sh_ref[0]).astype(BF16)
        dr, dgain, dbias = _ln_bwd(dres_ref[...] + dh * one_sc, r_ref[...], lg_ref[...])
        dr_ref[...] = dr
        df_ref[...] = ((0.5 * (1.0 + gt_ref[0])) * dr).astype(BF16)

        @pl.when(i == 0)
        def _():
            dln_ref[...] = jnp.zeros_like(dln_ref)

        @pl.when(i % tiles_per_seq == 0)
        def _():
            dmod_ref[...] = jnp.zeros_like(dmod_ref)
            dgt_ref[...] = jnp.zeros_like(dgt_ref)

        dmod_ref[0, 0:1, :] += jnp.sum(dh, axis=0, keepdims=True)
        dmod_ref[0, 1:2, :] += jnp.sum(dh * xx, axis=0, keepdims=True)
        dln_ref[0:1, :] += dgain
        dln_ref[1:2, :] += dbias
        dgt_ref[0] += jnp.sum(dr * (0.5 * f_ref[...].astype(F32)), axis=0, keepdims=True)

    tile = pl.BlockSpec((tm, dm), lambda i: (i, 0))
    mod = _mod_spec(tiles_per_seq, dm)
    return _call(
        body, job, name=name, grid=(tokens // tm,),
        in_specs=[pl.BlockSpec((tm, wdt), lambda i: (i, 0)) for wdt in widths]
        + [tile, tile, mod, mod, _const_spec(w_in.shape), tile, tile, mod, _const_spec((1, dm))],
        out_specs=[tile, tile, pl.BlockSpec((tm, total), lambda i: (i, 0)), tile,
                   pl.BlockSpec((1, 2, dm), lambda i: (i // tiles_per_seq, 0, 0)),
                   pl.BlockSpec((2, dm), lambda i: (0, 0)), mod],
        out_shape=[jax.ShapeDtypeStruct((tokens, dm), F32), jax.ShapeDtypeStruct((tokens, dm), BF16),
                   jax.ShapeDtypeStruct((tokens, total), BF16), jax.ShapeDtypeStruct((tokens, dm), BF16),
                   jax.ShapeDtypeStruct((nseq, 2, dm), F32), jax.ShapeDtypeStruct((2, dm), F32),
                   jax.ShapeDtypeStruct((nseq, 1, dm), F32)],
        args=(*parts, dres, x1, sh, sc, w_in, r_prev, f_prev, gt_prev, ln_g_prev))


def _rope_tables(positions):
    half = ROT_DIM // 2
    inv_freq = jnp.power(jnp.float32(ROPE_THETA), -jnp.arange(0, ROT_DIM, 2, dtype=F32) / ROT_DIM)
    lane = jnp.arange(LANES) % HEAD_DIM
    freq = jnp.where(lane < ROT_DIM, inv_freq[lane % half], 0.0)
    sign = jnp.where(lane < half, -1.0, 1.0).astype(F32)
    ang = positions.astype(F32)[:, None] * freq[None, :]
    return jnp.cos(ang), sign[None, :] * jnp.sin(ang)


def kernel(x, c, positions, w_ada, b_ada, ffn1_w_gate_up, ffn1_w_down, ln1_g, ln1_b, w_in, conv_w, attn_sinks, w_out, ln2_g, ln2_b, ffn2_w_gate_up, ffn2_w_down, ln3_g, ln3_b, loss_target, m_w_ada, m_b_ada, m_ffn1_w_gate_up, m_ffn1_w_down, m_ln1_g, m_ln1_b, m_w_in, m_conv_w, m_attn_sinks, m_w_out, m_ln2_g, m_ln2_b, m_ffn2_w_gate_up, m_ffn2_w_down, m_ln3_g, m_ln3_b, v_w_ada, v_b_ada, v_ffn1_w_gate_up, v_ffn1_w_down, v_ln1_g, v_ln1_b, v_w_in, v_conv_w, v_attn_sinks, v_w_out, v_ln2_g, v_ln2_b, v_ffn2_w_gate_up, v_ffn2_w_down, v_ln3_g, v_ln3_b):
    nseq, seq, dm = x.shape
    tokens = nseq * seq
    dev = 4 * lax.axis_index("x") + 2 * lax.axis_index("y") + lax.axis_index("c")
    core = lax.axis_index("c").astype(jnp.int32).reshape(1)
    ada_cols = w_ada.shape[2]
    ff = ffn1_w_down.shape[1] * N_DEV
    fc = ff // 4
    in_cols = w_in.shape[2]
    conv_cols = conv_w.shape[2]

    def t_bf16(w):
        return w[0].T.astype(BF16)

    c_all, convw_all = all_gather([c, conv_w[0]], "gather_cond")
    c_all = c_all.reshape(N_DEV * nseq, dm)
    convw_full = convw_all.transpose(1, 0, 2).reshape(CONV_TAPS, N_DEV * conv_cols)

    b_cols = lax.dynamic_slice(b_ada, (0, dev * ada_cols), (1, ada_cols))
    cond_all, mod_cols = ada_fwd(c_all, w_ada[0], b_cols, "ada_fwd")
    wgu1, mod_all = all_gather([t_bf16(ffn1_w_gate_up), mod_cols], "gather_ffn1")
    wgu1 = wgu1.reshape(2, ff, dm)
    mod = lax.dynamic_slice(mod_all, (0, dev * nseq, 0), (N_DEV, nseq, ada_cols))
    mod = mod.transpose(1, 0, 2).reshape(nseq, 9, 1, dm)
    sh1, sc1, g1, sh2, sc2, g2, sh3, sc3, g3 = [mod[:, i] for i in range(9)]

    x0 = x.reshape(tokens, dm)
    (gu1, a1, h1), (wd1, wout) = ffn_up(x0, sh1, sc1, wgu1, seq, "ffn1_up",
                                        job=_GatherJob([ffn1_w_down[0].astype(BF16), w_out[0].astype(BF16)]))
    wd1, wout = wd1.reshape(ff, dm), wout.reshape(dm, dm)
    (x1, r1, f1), (win,) = ffn_down(x0, a1, g1, wd1, ln1_g, ln1_b, seq, "ffn1_down", job=_GatherJob([t_bf16(w_in)]))
    win = win.reshape(N_DEV * in_cols, dm)
    (q, k, v, u, bg, cg), wd2_spread = proj_fwd(x1, sh2, sc2, win, seq, "proj_fwd",
                                                job=gather_spread_job([ffn2_w_down[0].astype(BF16)]))
    cos_t, sin_t = _rope_tables(positions.reshape(tokens))
    sinks = attn_sinks[0]
    (attn, q_rot, probs, sink_probs), wgu2_spread = attn_fwd(q, k, v, cos_t, sin_t, sinks, seq, "attn_fwd",
                                                             job=gather_spread_job([t_bf16(ffn2_w_gate_up)]))
    conv = conv_fwd(u, bg, cg, convw_full, seq, "conv_fwd")
    (x2, r2, mixin, mix), (wd2, wgu2) = out_fwd(x1, attn, conv, g2, wout, ln2_g, ln2_b, seq, "out_fwd",
                                                job=gather_forward_job(wd2_spread + wgu2_spread))
    wd2, wgu2 = wd2.reshape(ff, dm), wgu2.reshape(2, ff, dm)
    target = loss_target.reshape(tokens, dm)
    dr3, df3, gu3, a3, h3, loss_part, dln3, dg3 = ffn_loss(x2, sh3, sc3, g3, wgu2, wd2, ln3_g, ln3_b, target, seq, "ffn2_fwd")

    (dx2, dgu3, dmod3), _ = ffn_bwd(dr3, df3, x2, gu3, sc3, wgu2, wd2, seq, "ffn2_bwd")
    pair = 2 * fc
    g_wd2 = tn_matmul(a3[None], df3[None], "ffn2_dwd", a_width=pair)[0][0].reshape(N_DEV, ff // N_DEV, dm)
    g_wgu2 = tn_matmul(dgu3, h3[None], "ffn2_dwgu", a_width=pair)[0][0].reshape(N_DEV, fc, dm)
    (dres2, dattn, dconv, dmix, dln2, dg2), swapped = out_bwd(dx2, r2, mix, g2, wout, ln2_g, seq, "out_bwd",
                                                              job=swap_job([g_wgu2, g_wd2]))
    p_wgu2, own_wgu2 = pair_sum(core, g_wgu2, swapped[0], "pair_wgu2")
    p_wd2, own_wd2 = pair_sum(core, g_wd2, swapped[1], "pair_wd2")
    du, dbg, dcg, dconvw = conv_bwd(dconv, u, bg, cg, convw_full, seq, "conv_bwd")
    (dq, dk, dv, dsink_rows), (far_wd2,) = attn_bwd(
        q_rot, k, v, dattn, probs, sink_probs, cos_t, sin_t, seq, "attn_bwd", job=chip_exchange_job([p_wd2]))
    parts = [dq, dk, dv, du, dbg, dcg]
    (dr1, df1, dproj, h2, dmod2, dln1, dg1), far_top = proj_bwd(
        parts, dres2, x1, sh2, sc2, win, r1, f1, g1, ln1_g, seq, "proj_bwd",
        job=chip_exchange_job([p_wgu2], rows=(0, fc // 2)))
    (dx0, dgu1, dmod1), _ = ffn_bwd(dr1, df1, x0, gu1, sc1, wgu1, wd1, seq, "ffn1_bwd")

    dmod = jnp.concatenate([dmod1, dg1, dmod2, dg2, dmod3, dg3], axis=1).reshape(nseq, 9 * dm)
    half = dm // 2
    jobs = _Jobs([gather_spread_job([dmod]),
                  chip_exchange_job([p_wgu2], rows=(fc // 2, fc // 2), into=far_top)])
    (g_wd1,), res = tn_matmul(a1[None], df1[None], "ffn1_dwd", job=jobs, a_width=pair)
    dmod_spread, (far_wgu2,) = jobs.split(res)
    g_wd1 = g_wd1.reshape(N_DEV, ff // N_DEV, dm)
    jobs = _Jobs([swap_job([g_wd1]), gather_forward_job(dmod_spread)])
    (g_l,), res = tn_matmul(dgu1, h1[None], "ffn1_dwgu_l", job=jobs, b_cols=(0, half), a_width=pair)
    (sw_wd1,), (dmod_all,) = jobs.split(res)
    g_l = g_l.reshape(N_DEV, fc, half)
    p_wd1, own_wd1 = pair_sum(core, g_wd1, sw_wd1, "pair_wd1")
    jobs = _Jobs([chip_exchange_job([p_wd1]), swap_job([g_l])])
    (g_r,), res = tn_matmul(dgu1, h1[None], "ffn1_dwgu_r", job=jobs, b_cols=(1, half), a_width=pair)
    (far_wd1,), (sw_l,) = jobs.split(res)
    g_r = g_r.reshape(N_DEV, fc, half)
    p_l, own_l = pair_sum(core, g_l, sw_l, "pair_wgu1_l")

    dmod_cols = lax.dynamic_slice(dmod_all.reshape(N_DEV * nseq, 9 * dm), (0, dev * ada_cols), (N_DEV * nseq, ada_cols))
    grad_w_ada, gb_cols = ada_bwd(cond_all, dmod_cols, "ada_bwd")
    dsinks = jnp.sum(dsink_rows.reshape(nseq, N_Q_HEADS, ATTN_BLOCK), axis=(0, 2))
    small = jnp.zeros((8, dm), F32)
    small = small.at[0:2].set(dln1).at[2:4].set(dln2).at[4:6].set(dln3)
    small = small.at[6, 0:N_Q_HEADS].set(dsinks).at[7, 0].set(loss_part[0, 0])

    jobs = _Jobs([chip_exchange_job([p_l]), swap_job([g_r]), gather_spread_job([small, dconvw, gb_cols])])
    (g_win,), res = tn_matmul(dproj[None], h2[None], "dwin", job=jobs)
    (far_l,), (sw_r,), small_spread = jobs.split(res)
    g_win = g_win.reshape(N_DEV, in_cols, dm)
    p_r, own_r = pair_sum(core, g_r, sw_r, "pair_wgu1_r")
    jobs = _Jobs([chip_exchange_job([p_r]), swap_job([g_win]), gather_forward_job(small_spread)])
    (g_wout,), res = tn_matmul(mixin[None], dmix[None], "dwout", job=jobs)
    (far_r,), (sw_win,), (small_all, dconvw_all, gb_all) = jobs.split(res)
    g_wout = g_wout.reshape(N_DEV, dm // N_DEV, dm)
    p_win, own_win = pair_sum(core, g_win, sw_win, "pair_win")

    given = dict(w_ada=(w_ada, m_w_ada, v_w_ada), b_ada=(b_ada, m_b_ada, v_b_ada),
                 ffn1_w_gate_up=(ffn1_w_gate_up, m_ffn1_w_gate_up, v_ffn1_w_gate_up),
                 ffn1_w_down=(ffn1_w_down, m_ffn1_w_down, v_ffn1_w_down),
                 ln1_g=(ln1_g, m_ln1_g, v_ln1_g), ln1_b=(ln1_b, m_ln1_b, v_ln1_b),
                 w_in=(w_in, m_w_in, v_w_in), conv_w=(conv_w, m_conv_w, v_conv_w),
                 attn_sinks=(attn_sinks, m_attn_sinks, v_attn_sinks), w_out=(w_out, m_w_out, v_w_out),
                 ln2_g=(ln2_g, m_ln2_g, v_ln2_g), ln2_b=(ln2_b, m_ln2_b, v_ln2_b),
                 ffn2_w_gate_up=(ffn2_w_gate_up, m_ffn2_w_gate_up, v_ffn2_w_gate_up),
                 ffn2_w_down=(ffn2_w_down, m_ffn2_w_down, v_ffn2_w_down),
                 ln3_g=(ln3_g, m_ln3_g, v_ln3_g), ln3_b=(ln3_b, m_ln3_b, v_ln3_b))
    transposed = ("ffn1_w_gate_up", "ffn2_w_gate_up", "w_in")

    def big_adamw(nm, grad, far=None, job=None):
        flip = nm in transposed
        w2, m2, v2 = [t[0].T if flip else t[0] for t in given[nm]]
        res, job_res = adamw(w2, grad, m2, v2, "adamw_" + nm, others=far, job=job)
        return [t.T[None] if flip else t[None] for t in res], job_res

    results = {}
    jobs = _Jobs([chip_exchange_job([p_win]), swap_job([g_wout])])
    results["w_ada"], res = big_adamw("w_ada", grad_w_ada, job=jobs)
    (far_win,), (sw_wout,) = jobs.split(res)
    p_wout, own_wout = pair_sum(core, g_wout, sw_wout, "pair_wout")
    results["ffn2_w_gate_up"], (far_wout,) = big_adamw("ffn2_w_gate_up", own_wgu2, far_wgu2,
                                                       job=chip_exchange_job([p_wout]))

    grads = {
        "ffn1_w_gate_up": jnp.concatenate([own_l, own_r], axis=1), "ffn1_w_down": own_wd1,
        "w_in": own_win, "w_out": own_wout, "ffn2_w_down": own_wd2,
    }
    others = {"ffn1_w_gate_up": jnp.concatenate([far_l, far_r], axis=2), "ffn1_w_down": far_wd1,
              "w_in": far_win, "w_out": far_wout, "ffn2_w_down": far_wd2}
    for nm in grads:
        results[nm], _ = big_adamw(nm, grads[nm], others[nm])

    small_sum = sum_devices(small_all, "sum_small")
    dconvw_sum = sum_devices(dconvw_all, "sum_convw")
    loss = small_sum[7, 0]
    grads["b_ada"] = gb_all.reshape(1, N_DEV * ada_cols)
    grads["conv_w"] = lax.dynamic_slice(dconvw_sum, (0, dev * conv_cols), (CONV_TAPS, conv_cols))
    grads["attn_sinks"] = small_sum[6:7, 0:N_Q_HEADS]
    for i, nm in enumerate(["ln1_g", "ln1_b", "ln2_g", "ln2_b", "ln3_g", "ln3_b"]):
        grads[nm] = small_sum[i:i + 1]

    order = ["w_ada", "b_ada", "ffn1_w_gate_up", "ffn1_w_down", "ln1_g", "ln1_b", "w_in", "conv_w", "attn_sinks",
             "w_out", "ln2_g", "ln2_b", "ffn2_w_gate_up", "ffn2_w_down", "ln3_g", "ln3_b"]
    small_names = [nm for nm in order if nm not in results]
    items = []
    for nm in small_names:
        shape = given[nm][0].shape
        two_d = (shape[-2], shape[-1])
        items.append((given[nm][0].reshape(two_d), grads[nm].reshape(two_d), *[t.reshape(two_d) for t in given[nm][1:]]))
    for nm, res in zip(small_names, adamw_small(items, "adamw_small")):
        shape = given[nm][0].shape
        results[nm] = [grads[nm].reshape(shape)] + [t.reshape(shape) for t in res]
    grad_x = dx0.reshape(nseq, seq, dm)
    return (loss, grad_x, *[results[nm][i] for i in range(4) for nm in order])
```

```python
import functools

import jax
import jax.numpy as jnp
from jax import lax
from jax.experimental import pallas as pl
from jax.experimental.pallas import tpu as pltpu

F32 = jnp.float32
BF16 = jnp.bfloat16
MESH = pl.DeviceIdType.MESH

N_DEV = 8
N_CHIP = 4
HEAD_DIM = 64
N_Q_HEADS = 8
N_KV_HEADS = 2
GQA_GROUP = N_Q_HEADS // N_KV_HEADS
ATTN_BLOCK = 128
ROT_DIM = 16
ROPE_THETA = 500000.0
CONV_TAPS = 3
LN_EPS = 1e-5
DN_ALPHA = 2.0 ** 0.25
ADAM_LR = 0.001
ADAM_B1 = 0.9
ADAM_B2 = 0.999
ADAM_EPS = 1e-08
ADAM_WD = 0.01
ADAM_STEP = 10
NEG_BIG = -1e30

VMEM_LIMIT = 56 * 1024 * 1024
TOKEN_TILE = 256
FFN_FWD_TILE = 512
MIX_TILE = 512
TN_VMEM_BUDGET = 36 * 1024 * 1024


def _params(semantics=None, vmem=VMEM_LIMIT):
    return pltpu.CompilerParams(dimension_semantics=semantics, vmem_limit_bytes=vmem)


def _dot(a, b):
    return jnp.dot(a, b, preferred_element_type=F32)


def _dot_nt(a, b):
    return lax.dot_general(a, b, (((1,), (1,)), ((), ())), preferred_element_type=F32)


def _dot_tn(a, b):
    return lax.dot_general(a, b, (((0,), (0,)), ((), ())), preferred_element_type=F32)


def _sigmoid(x):
    return pl.reciprocal(1.0 + jnp.exp(-x), approx=True)


def _ln_stats(r):
    mu = jnp.mean(r, axis=-1, keepdims=True)
    d = r - mu
    var = jnp.mean(d * d, axis=-1, keepdims=True)
    rstd = lax.rsqrt(var + LN_EPS)
    return d * rstd, rstd


def _ln_bwd(dy, r, g):
    return _ln_bwd_normalized(dy, *_ln_stats(r), g)


def _ln_bwd_normalized(dy, xhat, rstd, g):
    dxhat = dy * g
    c1 = jnp.mean(dxhat, axis=-1, keepdims=True)
    c2 = jnp.mean(dxhat * xhat, axis=-1, keepdims=True)
    dr = rstd * (dxhat - c1 - xhat * c2)
    return dr, jnp.sum(dy * xhat, axis=0, keepdims=True), jnp.sum(dy, axis=0, keepdims=True)


def _const_spec(shape):
    nd = len(shape)
    return pl.BlockSpec(shape, lambda *_: (0,) * nd, pipeline_mode=pl.Buffered(1))


def all_gather(arrs, name):
    n = len(arrs)

    def body(*refs):
        ins, outs = refs[:n], refs[n:2 * n]
        send_sems, recv_sems, local_sems = refs[2 * n:]
        x, y, c = lax.axis_index("x"), lax.axis_index("y"), lax.axis_index("c")
        me, sibling = (x, y, c), (x, y, 1 - c)
        chips = [(1 - x, y), (x, 1 - y), (1 - x, 1 - y)]

        def slot(i, p):
            return outs[i].at[4 * p[0] + 2 * p[1] + p[2]]

        def copy(i, k, block, to, src=None):
            return pltpu.make_async_remote_copy(
                src_ref=slot(i, block) if src is None else src, dst_ref=slot(i, block),
                send_sem=send_sems.at[i, k], recv_sem=recv_sems.at[i, k],
                device_id=to, device_id_type=MESH)

        mine = [pltpu.make_async_copy(ins[i], slot(i, me), local_sems.at[i]) for i in range(n)]
        for cp in mine:
            cp.start()
        first = []
        for i in range(n):
            first.append(copy(i, 0, me, sibling, src=ins[i]))
            first += [copy(i, 1 + j, me, (*chip, c), src=ins[i]) for j, chip in enumerate(chips)]
        for cp in first:
            cp.start()
        passed = []
        for i in range(n):
            for j, chip in enumerate(chips):
                copy(i, 1 + j, (*chip, c), me).wait_recv()
                cp = copy(i, 4 + j, (*chip, c), sibling)
                cp.start()
                passed.append(cp)
        for i in range(n):
            copy(i, 0, sibling, me).wait_recv()
            for j, chip in enumerate(chips):
                copy(i, 4 + j, (*chip, 1 - c), me).wait_recv()
        for cp in first + passed:
            cp.wait_send()
        for cp in mine:
            cp.wait()

    any_spec = pl.BlockSpec(memory_space=pl.ANY)
    return pl.pallas_call(
        body, name=name,
        out_shape=[jax.ShapeDtypeStruct((N_DEV, *a.shape), a.dtype) for a in arrs],
        in_specs=[any_spec] * n, out_specs=[any_spec] * n,
        scratch_shapes=[pltpu.SemaphoreType.DMA((n, 7)), pltpu.SemaphoreType.DMA((n, 7)),
                        pltpu.SemaphoreType.DMA((n,))],
    )(*arrs)


def _place():
    x, y, c = lax.axis_index("x"), lax.axis_index("y"), lax.axis_index("c")
    return x, y, c, [(1 - x, y), (x, 1 - y), (1 - x, 1 - y)]


def _slot(p):
    return 4 * p[0] + 2 * p[1] + p[2]


class _Job:
    def __init__(self, ins, outs, nsem, copies, aliases=None, local=None):
        self.ins, self.outs, self.nsem, self.copies = list(ins), list(outs), nsem, copies
        self.aliases = aliases or {}
        self.local = local

    def scratch(self):
        s = [pltpu.SemaphoreType.DMA(self.nsem), pltpu.SemaphoreType.DMA(self.nsem)]
        if self.local is not None:
            s.append(pltpu.SemaphoreType.DMA((len(self.ins),)))
        return s

    def start(self, ins, outs, sems):
        if self.local is not None:
            for cp in self.local(ins, outs, sems[2]):
                cp.start()
        for cp in self.copies(ins, outs, sems[0], sems[1])[0]:
            cp.start()

    def finish(self, ins, outs, sems):
        started, awaited = self.copies(ins, outs, sems[0], sems[1])
        for cp in awaited:
            cp.wait_recv()
        for cp in started:
            cp.wait_send()
        if self.local is not None:
            for cp in self.local(ins, outs, sems[2]):
                cp.wait()


class _Jobs:
    def __init__(self, jobs):
        self.jobs = jobs
        self.ins = [a for j in jobs for a in j.ins]
        self.outs = [o for j in jobs for o in j.outs]
        self.two_phase = any(getattr(j, "two_phase", False) for j in jobs)
        self.aliases = {}
        at_in = at_out = 0
        for j in jobs:
            self.aliases.update({at_in + i: at_out + o for i, o in j.aliases.items()})
            at_in, at_out = at_in + len(j.ins), at_out + len(j.outs)

    def scratch(self):
        return [s for j in self.jobs for s in j.scratch()]

    def _each(self, ins, outs, sems):
        at_in = at_out = at_sem = 0
        for j in self.jobs:
            n_in, n_out, n_sem = len(j.ins), len(j.outs), len(j.scratch())
            yield j, ins[at_in:at_in + n_in], outs[at_out:at_out + n_out], sems[at_sem:at_sem + n_sem]
            at_in, at_out, at_sem = at_in + n_in, at_out + n_out, at_sem + n_sem

    def start(self, ins, outs, sems):
        for j, i, o, s in self._each(ins, outs, sems):
            j.start(i, o, s)

    def turn(self, ins, outs, sems):
        for j, i, o, s in self._each(ins, outs, sems):
            if getattr(j, "two_phase", False):
                j.turn(i, o, s)

    def finish(self, ins, outs, sems):
        for j, i, o, s in self._each(ins, outs, sems):
            j.finish(i, o, s)

    def split(self, results):
        at, parts = 0, []
        for j in self.jobs:
            parts.append(results[at:at + len(j.outs)])
            at += len(j.outs)
        return parts


def _remote(src, dst, send, recv, idx, to):
    return pltpu.make_async_remote_copy(src_ref=src, dst_ref=dst, send_sem=send.at[idx], recv_sem=recv.at[idx],
                                        device_id=to, device_id_type=MESH)


def _spread_copies(ins, outs, send, recv, base=0):
    x, y, c, chips = _place()
    me = (x, y, c)
    peers = [(x, y, 1 - c)] + [(*chip, c) for chip in chips]
    started, awaited = [], []
    for i, (src, dst) in enumerate(zip(ins, outs)):
        for k, peer in enumerate(peers):
            started.append(_remote(src, dst.at[_slot(me)], send, recv, (base + i, k), peer))
            awaited.append(_remote(src, dst.at[_slot(peer)], send, recv, (base + i, k), peer))
    return started, awaited


def _forward_copies(ins, outs, send, recv, base=0):
    x, y, c, chips = _place()
    started, awaited = [], []
    for i, buf in enumerate(outs):
        for j, chip in enumerate(chips):
            mine, theirs = buf.at[_slot((*chip, c))], buf.at[_slot((*chip, 1 - c))]
            started.append(_remote(mine, mine, send, recv, (base + i, j), (x, y, 1 - c)))
            awaited.append(_remote(theirs, theirs, send, recv, (base + i, j), (x, y, 1 - c)))
    return started, awaited


def _own_block_copies(ins, outs, sems):
    x, y, c, _ = _place()
    return [pltpu.make_async_copy(src, dst.at[_slot((x, y, c))], sems.at[i])
            for i, (src, dst) in enumerate(zip(ins, outs))]


def gather_spread_job(shards):
    outs = [jax.ShapeDtypeStruct((N_DEV, *a.shape), a.dtype) for a in shards]
    return _Job(shards, outs, (len(shards), 4), _spread_copies, local=_own_block_copies)


def gather_forward_job(fulls):
    outs = [jax.ShapeDtypeStruct(a.shape, a.dtype) for a in fulls]
    return _Job(fulls, outs, (len(fulls), 3), _forward_copies, aliases={i: i for i in range(len(fulls))})


TURN_EIGHTHS = 6


class _GatherJob:
    two_phase = True

    def __init__(self, shards):
        self.ins = list(shards)
        self.outs = [jax.ShapeDtypeStruct((N_DEV, *a.shape), a.dtype) for a in shards]
        self.aliases = {}

    def scratch(self):
        n = len(self.ins)
        return [pltpu.SemaphoreType.DMA((n, 4)), pltpu.SemaphoreType.DMA((n, 4)),
                pltpu.SemaphoreType.DMA((n, 3)), pltpu.SemaphoreType.DMA((n, 3)), pltpu.SemaphoreType.DMA((n,))]

    def start(self, ins, outs, sems):
        for cp in _own_block_copies(ins, outs, sems[4]) + _spread_copies(ins, outs, sems[0], sems[1])[0]:
            cp.start()

    def turn(self, ins, outs, sems):
        for cp in _spread_copies(ins, outs, sems[0], sems[1])[1]:
            cp.wait_recv()
        for cp in _forward_copies(outs, outs, sems[2], sems[3])[0]:
            cp.start()

    def finish(self, ins, outs, sems):
        handed_on, arriving = _forward_copies(outs, outs, sems[2], sems[3])
        for cp in arriving:
            cp.wait_recv()
        for cp in _spread_copies(ins, outs, sems[0], sems[1])[0] + handed_on:
            cp.wait_send()
        for cp in _own_block_copies(ins, outs, sems[4]):
            cp.wait()


def swap_job(gs):
    def copies(ins, outs, send, recv):
        x, y, c, _ = _place()
        started, awaited = [], []
        for i, (g, r1) in enumerate(zip(ins, outs)):
            for q in range(N_CHIP):
                started.append(_remote(g.at[2 * q + (1 - c)], r1.at[q], send, recv, (i, q), (x, y, 1 - c)))
                awaited.append(_remote(g.at[2 * q + c], r1.at[q], send, recv, (i, q), (x, y, 1 - c)))
        return started, awaited

    outs = [jax.ShapeDtypeStruct((N_CHIP, *g.shape[1:]), g.dtype) for g in gs]
    return _Job(gs, outs, (len(gs), N_CHIP), copies)


def chip_exchange_job(ps, rows=None, into=None):
    n = len(ps)

    def copies(ins, outs, send, recv):
        x, y, c, chips = _place()
        started, awaited = [], []
        for i, (p, r2) in enumerate(zip(ins[:n], outs)):
            for k, chip in enumerate(chips):
                src, mine, dst = p.at[2 * chip[0] + chip[1]], p.at[2 * x + y], r2.at[k]
                if rows is not None:
                    src, mine, dst = (t.at[pl.ds(rows[0], rows[1])] for t in (src, mine, dst))
                started.append(_remote(src, dst, send, recv, (i, k), (*chip, c)))
                awaited.append(_remote(mine, dst, send, recv, (i, k), (*chip, c)))
        return started, awaited

    outs = [jax.ShapeDtypeStruct((3, *p.shape[1:]), p.dtype) for p in ps]
    if into is None:
        return _Job(ps, outs, (n, 3), copies)
    return _Job(list(ps) + list(into), outs, (n, 3), copies, aliases={n + i: i for i in range(n)})


def _call(body, job, *, name, grid, in_specs, out_specs, out_shape, args, scratch_shapes=(), vmem=VMEM_LIMIT):
    if job is None:
        res = pl.pallas_call(
            body, name=name, grid=grid, in_specs=in_specs, out_specs=out_specs, out_shape=out_shape,
            scratch_shapes=list(scratch_shapes), compiler_params=_params(("arbitrary",) * len(grid), vmem),
        )(*args)
        return res, []
    n_in, n_out, n_scr = len(in_specs), len(out_specs), len(scratch_shapes)
    j_in, j_out = len(job.ins), len(job.outs)

    def with_copies(*refs):
        at = 0
        ins = refs[at:at + n_in]; at += n_in
        jins = refs[at:at + j_in]; at += j_in
        outs = refs[at:at + n_out]; at += n_out
        jouts = refs[at:at + j_out]; at += j_out
        scr = refs[at:at + n_scr]; at += n_scr
        sems = refs[at:]
        ids = [pl.program_id(d) for d in range(len(grid))]
        first = functools.reduce(jnp.logical_and, [i == 0 for i in ids])
        last = functools.reduce(jnp.logical_and, [i == n - 1 for i, n in zip(ids, grid)])

        @pl.when(first)
        def _():
            job.start(jins, jouts, sems)

        if getattr(job, "two_phase", False):
            steps, at = 1, 0
            for i, n in zip(ids, grid):
                steps, at = steps * n, at * n + i

            @pl.when(at == (TURN_EIGHTHS * steps) // 8)
            def _():
                job.turn(jins, jouts, sems)

        body(*ins, *outs, *scr)

        @pl.when(last)
        def _():
            job.finish(jins, jouts, sems)

    any_spec = pl.BlockSpec(memory_space=pl.ANY)
    res = pl.pallas_call(
        with_copies, name=name, grid=grid,
        in_specs=list(in_specs) + [any_spec] * j_in, out_specs=list(out_specs) + [any_spec] * j_out,
        out_shape=list(out_shape) + list(job.outs),
        input_output_aliases={n_in + i: n_out + o for i, o in job.aliases.items()},
        scratch_shapes=list(scratch_shapes) + job.scratch(),
        compiler_params=_params(("arbitrary",) * len(grid), vmem),
    )(*args, *job.ins)
    return res[:n_out], res[n_out:]


def run_job(job, name):
    def body(*refs):
        j_in, j_out = len(job.ins), len(job.outs)
        ins, outs, sems = refs[:j_in], refs[j_in:j_in + j_out], refs[j_in + j_out:]
        job.start(ins, outs, sems)
        job.finish(ins, outs, sems)

    any_spec = pl.BlockSpec(memory_space=pl.ANY)
    return pl.pallas_call(
        body, name=name, in_specs=[any_spec] * len(job.ins), out_specs=[any_spec] * len(job.outs),
        out_shape=list(job.outs), input_output_aliases=dict(job.aliases), scratch_shapes=job.scratch(),
    )(*job.ins)


def pair_sum(core, g, r1, name):
    _, rows, cols = g.shape
    rb = next(cand for cand in range(min(rows, 512), 0, -16) if rows % cand == 0)

    def body(core_ref, g_ref, r1_ref, p_ref, own_ref):
        del core_ref
        x, y, _, _ = _place()
        s = g_ref[0].astype(F32) + r1_ref[0].astype(F32)
        p_ref[0] = s.astype(BF16)

        @pl.when(pl.program_id(1) == 2 * x + y)
        def _():
            own_ref[...] = s

    chunk = (1, rb, cols)
    return pl.pallas_call(
        body, name=name,
        grid_spec=pltpu.PrefetchScalarGridSpec(
            num_scalar_prefetch=1, grid=(rows // rb, N_CHIP),
            in_specs=[pl.BlockSpec(chunk, lambda i, q, core_ref: (2 * q + core_ref[0], i, 0)),
                      pl.BlockSpec(chunk, lambda i, q, core_ref: (q, i, 0))],
            out_specs=[pl.BlockSpec(chunk, lambda i, q, core_ref: (q, i, 0)),
                       pl.BlockSpec((rb, cols), lambda i, q, core_ref: (i, 0))]),
        out_shape=[jax.ShapeDtypeStruct((N_CHIP, rows, cols), BF16), jax.ShapeDtypeStruct((rows, cols), F32)],
        compiler_params=_params(("arbitrary", "arbitrary")),
    )(core, g, r1)


def sum_devices(a, name):
    def body(a_ref, o_ref):
        acc = a_ref[0]
        for d in range(1, N_DEV):
            acc = acc + a_ref[d]
        o_ref[...] = acc

    return pl.pallas_call(body, name=name, out_shape=jax.ShapeDtypeStruct(a.shape[1:], F32))(a)


def _adam_update(w, g, m, v):
    nm = ADAM_B1 * m + (1.0 - ADAM_B1) * g
    nv = ADAM_B2 * v + (1.0 - ADAM_B2) * (g * g)
    m_hat = nm / (1.0 - ADAM_B1 ** ADAM_STEP)
    v_hat = nv / (1.0 - ADAM_B2 ** ADAM_STEP)
    return -ADAM_LR * (m_hat / (jnp.sqrt(v_hat) + ADAM_EPS) + ADAM_WD * w), nm, nv


def adamw(w, g, m, v, name, others=None):
    rows, cols = w.shape
    rb = rows
    for cand in range(min(rows, 512), 7, -8):
        if rows % cand == 0 and cand % 8 == 0:
            rb = cand
            break

    def body(*refs):
        if others is None:
            w_ref, g_ref, m_ref, v_ref, d_ref, nm_ref, nv_ref = refs
            gg = g_ref[...]
        else:
            w_ref, g_ref, m_ref, v_ref, r2_ref, go_ref, d_ref, nm_ref, nv_ref = refs
            gg = g_ref[...]
            for k in range(3):
                gg = gg + r2_ref[k].astype(F32)
            go_ref[...] = gg
        d_ref[...], nm_ref[...], nv_ref[...] = _adam_update(w_ref[...], gg, m_ref[...], v_ref[...])

    spec = pl.BlockSpec((rb, cols), lambda i: (i, 0))
    out = jax.ShapeDtypeStruct((rows, cols), F32)
    in_specs, args = [spec] * 4, [w, g, m, v]
    if others is not None:
        in_specs.append(pl.BlockSpec((3, rb, cols), lambda i: (0, i, 0)))
        args.append(others)
    n_out = 3 if others is None else 4
    res = pl.pallas_call(
        body, name=name, grid=(rows // rb,), in_specs=in_specs, out_specs=[spec] * n_out,
        out_shape=[out] * n_out, compiler_params=_params(("parallel",)),
    )(*args)
    return (g, *res) if others is None else tuple(res)


def adamw_small(items, name):
    n = len(items)

    def body(*refs):
        ins, outs = refs[:4 * n], refs[4 * n:]
        for i in range(n):
            w_ref, g_ref, m_ref, v_ref = ins[4 * i:4 * i + 4]
            d_ref, nm_ref, nv_ref = outs[3 * i:3 * i + 3]
            d_ref[...], nm_ref[...], nv_ref[...] = _adam_update(w_ref[...], g_ref[...], m_ref[...], v_ref[...])

    res = pl.pallas_call(
        body, name=name,
        out_shape=[jax.ShapeDtypeStruct(w.shape, F32) for w, _, _, _ in items for _ in range(3)],
    )(*[t for item in items for t in item])
    return [tuple(res[3 * i:3 * i + 3]) for i in range(n)]


def ada_fwd(c_all, w_cols, b_cols, name):
    def body(c_ref, w_ref, b_ref, cond_ref, mod_ref):
        cc = c_ref[...]
        cond = (cc * _sigmoid(cc)).astype(BF16)
        cond_ref[...] = cond
        mod_ref[...] = _dot(cond, w_ref[...].astype(BF16)) + b_ref[...]

    n, cols = c_all.shape[0], w_cols.shape[1]
    return pl.pallas_call(
        body, name=name,
        out_shape=[jax.ShapeDtypeStruct(c_all.shape, BF16), jax.ShapeDtypeStruct((n, cols), F32)],
        compiler_params=_params(),
    )(c_all, w_cols, b_cols)


def ada_bwd(cond_all, dmod_cols, name):
    def body(c_ref, d_ref, gw_ref, gb_ref):
        d = d_ref[...]
        gw_ref[...] = _dot_tn(c_ref[...], d.astype(BF16))
        gb_ref[...] = jnp.sum(d, axis=0, keepdims=True)

    dm, cols = cond_all.shape[1], dmod_cols.shape[1]
    return pl.pallas_call(
        body, name=name,
        out_shape=[jax.ShapeDtypeStruct((dm, cols), F32), jax.ShapeDtypeStruct((1, cols), F32)],
        compiler_params=_params(),
    )(cond_all, dmod_cols)


MXU_COLS = 256
FFN_CHUNK = 4 * MXU_COLS


def _hidden_chunks(ff):
    assert ff % MXU_COLS == 0
    return [(at, min(FFN_CHUNK, ff - at)) for at in range(0, ff, FFN_CHUNK)]


def _mod_spec(tiles_per_seq, dm):
    return pl.BlockSpec((1, 1, dm), lambda i: (i // tiles_per_seq, 0, 0))


def ffn_loss(x, sh, sc, gt, wgu, wd, ln_g, ln_b, target, seq, name):
    tokens, dm = x.shape
    ff = wgu.shape[1]
    chunks = _hidden_chunks(ff)
    tm = min(TOKEN_TILE, seq)
    tiles_per_seq = seq // tm

    def body(x_ref, sh_ref, sc_ref, gt_ref, wgu_ref, wd_ref, lg_ref, lb_ref, t_ref,
             dr_ref, df_ref, gu_ref, a_ref, h_ref, loss_ref, dln_ref, dgt_ref):
        i = pl.program_id(0)
        xx = x_ref[...]
        h = (xx * (1.0 + sc_ref[0]) + sh_ref[0]).astype(BF16)
        h_ref[...] = h
        acc = jnp.zeros((tm, dm), F32)
        for at, wdt in chunks:
            gk = _dot_nt(h, wgu_ref[0, at:at + wdt, :])
            uk = _dot_nt(h, wgu_ref[1, at:at + wdt, :])
            gu_ref[0, :, at:at + wdt] = gk.astype(BF16)
            gu_ref[1, :, at:at + wdt] = uk.astype(BF16)
            a = (gk * _sigmoid(gk) * uk).astype(BF16)
            a_ref[:, at:at + wdt] = a
            acc = acc + _dot(a, wd_ref[at:at + wdt, :])
        half_gate = 0.5 * (1.0 + gt_ref[0])
        xhat, rstd = _ln_stats(DN_ALPHA * xx + half_gate * acc)
        err = xhat * lg_ref[...] + lb_ref[...] - t_ref[...]
        dr, dgain, dbias = _ln_bwd_normalized(err * (1.0 / dm), xhat, rstd, lg_ref[...])
        dr_ref[...] = dr
        df_ref[...] = (half_gate * dr).astype(BF16)

        @pl.when(i == 0)
        def _():
            loss_ref[...] = jnp.zeros_like(loss_ref)
            dln_ref[...] = jnp.zeros_like(dln_ref)

        @pl.when(i % tiles_per_seq == 0)
        def _():
            dgt_ref[...] = jnp.zeros_like(dgt_ref)

        loss_ref[...] += jnp.full((1, 128), (0.5 / dm) * jnp.sum(err * err), F32)
        dln_ref[0:1, :] += dgain
        dln_ref[1:2, :] += dbias
        dgt_ref[0] += jnp.sum(dr * (0.5 * acc), axis=0, keepdims=True)

    tile = pl.BlockSpec((tm, dm), lambda i: (i, 0))
    mod = _mod_spec(tiles_per_seq, dm)
    res, _ = _call(
        body, None, name=name, grid=(tokens // tm,),
        in_specs=[tile, mod, mod, mod, _const_spec(wgu.shape), _const_spec(wd.shape),
                  _const_spec((1, dm)), _const_spec((1, dm)), tile],
        out_specs=[tile, tile, pl.BlockSpec((2, tm, ff), lambda i: (0, i, 0)), pl.BlockSpec((tm, ff), lambda i: (i, 0)),
                   tile, pl.BlockSpec((1, 128), lambda i: (0, 0)), pl.BlockSpec((2, dm), lambda i: (0, 0)), mod],
        out_shape=[jax.ShapeDtypeStruct((tokens, dm), F32), jax.ShapeDtypeStruct((tokens, dm), BF16),
                   jax.ShapeDtypeStruct((2, tokens, ff), BF16), jax.ShapeDtypeStruct((tokens, ff), BF16),
                   jax.ShapeDtypeStruct((tokens, dm), BF16), jax.ShapeDtypeStruct((1, 128), F32),
                   jax.ShapeDtypeStruct((2, dm), F32), jax.ShapeDtypeStruct((tokens // seq, 1, dm), F32)],
        args=(x, sh, sc, gt, wgu, wd, ln_g, ln_b, target))
    return res


def ffn_up(x, sh, sc, wgu, seq, name, job=None):
    tokens, dm = x.shape
    ff = wgu.shape[1]
    chunks = _hidden_chunks(ff)
    tm = min(FFN_FWD_TILE, seq)

    def body(x_ref, sh_ref, sc_ref, wgu_ref, gu_ref, a_ref, h_ref):
        h = (x_ref[...] * (1.0 + sc_ref[0]) + sh_ref[0]).astype(BF16)
        h_ref[...] = h
        for at, wdt in chunks:
            gk = _dot_nt(h, wgu_ref[0, at:at + wdt, :])
            uk = _dot_nt(h, wgu_ref[1, at:at + wdt, :])
            gu_ref[0, :, at:at + wdt] = gk.astype(BF16)
            gu_ref[1, :, at:at + wdt] = uk.astype(BF16)
            a_ref[:, at:at + wdt] = (gk * _sigmoid(gk) * uk).astype(BF16)

    tile = pl.BlockSpec((tm, dm), lambda i: (i, 0))
    mod = _mod_spec(seq // tm, dm)
    return _call(
        body, job, name=name, grid=(tokens // tm,),
        in_specs=[tile, mod, mod, _const_spec(wgu.shape)],
        out_specs=[pl.BlockSpec((2, tm, ff), lambda i: (0, i, 0)), pl.BlockSpec((tm, ff), lambda i: (i, 0)), tile],
        out_shape=[jax.ShapeDtypeStruct((2, tokens, ff), BF16), jax.ShapeDtypeStruct((tokens, ff), BF16),
                   jax.ShapeDtypeStruct((tokens, dm), BF16)],
        args=(x, sh, sc, wgu))


def ffn_down(x, a, gt, wd, ln_g, ln_b, seq, name, job=None):
    tokens, dm = x.shape
    ff = wd.shape[0]
    chunks = _hidden_chunks(ff)
    tm = min(FFN_FWD_TILE, seq)

    def body(x_ref, a_ref, gt_ref, wd_ref, lg_ref, lb_ref, xo_ref, r_ref, f_ref):
        acc = jnp.zeros((tm, dm), F32)
        for at, wdt in chunks:
            acc = acc + _dot(a_ref[:, at:at + wdt], wd_ref[at:at + wdt, :])
        f_ref[...] = acc.astype(BF16)
        r = DN_ALPHA * x_ref[...] + (0.5 * (1.0 + gt_ref[0])) * acc
        r_ref[...] = r
        xhat, _ = _ln_stats(r)
        xo_ref[...] = xhat * lg_ref[...] + lb_ref[...]

    tile = pl.BlockSpec((tm, dm), lambda i: (i, 0))
    return _call(
        body, job, name=name, grid=(tokens // tm,),
        in_specs=[tile, pl.BlockSpec((tm, ff), lambda i: (i, 0)), _mod_spec(seq // tm, dm), _const_spec(wd.shape),
                  _const_spec((1, dm)), _const_spec((1, dm))],
        out_specs=[tile, tile, tile],
        out_shape=[jax.ShapeDtypeStruct((tokens, dm), F32), jax.ShapeDtypeStruct((tokens, dm), F32),
                   jax.ShapeDtypeStruct((tokens, dm), BF16)],
        args=(x, a, gt, wd, ln_g, ln_b))


def ffn_bwd(dr, df, x, gu, sc, wgu, wd, seq, name, job=None):
    tokens, dm = x.shape
    ff = wgu.shape[1]
    chunks = _hidden_chunks(ff)
    tm = min(TOKEN_TILE, seq)
    tiles_per_seq = seq // tm
    nseq = tokens // seq

    def body(dr_ref, df_ref, x_ref, gu_ref, sc_ref, wgu_ref, wd_ref, dx_ref, dgu_ref, dmod_ref):
        @pl.when(pl.program_id(0) % tiles_per_seq == 0)
        def _():
            dmod_ref[...] = jnp.zeros_like(dmod_ref)

        df = df_ref[...]
        dh = jnp.zeros((tm, dm), F32)
        for at, wdt in chunks:
            cols = slice(at, at + wdt)
            da = _dot_nt(df, wd_ref[cols, :])
            gk = gu_ref[0, :, cols].astype(F32)
            uk = gu_ref[1, :, cols].astype(F32)
            sg = _sigmoid(gk)
            sil = gk * sg
            du = (da * sil).astype(BF16)
            dg = (da * uk * (sg * (1.0 + gk * (1.0 - sg)))).astype(BF16)
            dgu_ref[0, :, cols] = dg
            dgu_ref[1, :, cols] = du
            dh = dh + _dot(dg, wgu_ref[0, cols, :]) + _dot(du, wgu_ref[1, cols, :])
        dx_ref[...] = DN_ALPHA * dr_ref[...] + dh * (1.0 + sc_ref[0])
        dmod_ref[0, 0:1, :] += jnp.sum(dh, axis=0, keepdims=True)
        dmod_ref[0, 1:2, :] += jnp.sum(dh * x_ref[...], axis=0, keepdims=True)

    tile = pl.BlockSpec((tm, dm), lambda i: (i, 0))
    gu_spec = pl.BlockSpec((2, tm, ff), lambda i: (0, i, 0))
    return _call(
        body, job, name=name, grid=(tokens // tm,),
        in_specs=[tile, tile, tile, gu_spec, _mod_spec(tiles_per_seq, dm), _const_spec(wgu.shape), _const_spec(wd.shape)],
        out_specs=[tile, gu_spec, pl.BlockSpec((1, 2, dm), lambda i: (i // tiles_per_seq, 0, 0))],
        out_shape=[jax.ShapeDtypeStruct((tokens, dm), F32), jax.ShapeDtypeStruct((2, tokens, ff), BF16),
                   jax.ShapeDtypeStruct((nseq, 2, dm), F32)],
        args=(dr, df, x, gu, sc, wgu, wd))


def tn_matmul(a, b, name, job=None, b_cols=None, a_width=None):
    na, tokens, k_all = a.shape
    kk = k_all if a_width is None else a_width
    nka = k_all // kk
    assert nka * kk == k_all
    nb, _, cc = b.shape
    col = 0
    if b_cols is not None:
        col, cc = b_cols
    tt = tokens
    while 4 * tt * (kk + cc) + 8 * kk * cc > TN_VMEM_BUDGET and tt % 2 == 0 and tt > 256:
        tt //= 2
    steps = tokens // tt

    def body(a_ref, b_ref, o_ref, *acc):
        if steps == 1:
            o_ref[0, 0, 0] = _dot_tn(a_ref[0], b_ref[0]).astype(BF16)
            return
        acc_ref, = acc
        t = pl.program_id(3)

        @pl.when(t == 0)
        def _():
            acc_ref[...] = jnp.zeros_like(acc_ref)

        acc_ref[...] += _dot_tn(a_ref[0], b_ref[0])

        @pl.when(t == steps - 1)
        def _():
            o_ref[0, 0, 0] = acc_ref[...].astype(BF16)

    return _call(
        body, job, name=name, grid=(na, nka, nb, steps),
        in_specs=[pl.BlockSpec((1, tt, kk), lambda i, s, j, t: (i, t, s)),
                  pl.BlockSpec((1, tt, cc), lambda i, s, j, t: (j, t, col))],
        out_specs=[pl.BlockSpec((1, 1, 1, kk, cc), lambda i, s, j, t: (i, s, j, 0, 0))],
        out_shape=[jax.ShapeDtypeStruct((na, nka, nb, kk, cc), BF16)],
        scratch_shapes=[] if steps == 1 else [pltpu.VMEM((kk, cc), F32)], args=(a, b))


def proj_fwd(x1, sh, sc, w_in, seq, name, job=None):
    tokens, dm = x1.shape
    tm = min(MIX_TILE, seq)
    tiles_per_seq = seq // tm
    widths = [N_Q_HEADS * HEAD_DIM, N_KV_HEADS * HEAD_DIM, N_KV_HEADS * HEAD_DIM, 512, 512, 512]
    assert sum(widths) == w_in.shape[0]

    def body(x_ref, sh_ref, sc_ref, w_ref, *outs):
        h = (x_ref[...] * (1.0 + sc_ref[0]) + sh_ref[0]).astype(BF16)
        proj = _dot_nt(h, w_ref[...])
        at = 0
        for o_ref, wdt in zip(outs, widths):
            o_ref[...] = proj[:, at:at + wdt].astype(o_ref.dtype)
            at += wdt

    tile = pl.BlockSpec((tm, dm), lambda i: (i, 0))
    mod = _mod_spec(tiles_per_seq, dm)
    return _call(
        body, job, name=name, grid=(tokens // tm,),
        in_specs=[tile, mod, mod, _const_spec(w_in.shape)],
        out_specs=[pl.BlockSpec((tm, wdt), lambda i: (i, 0)) for wdt in widths],
        out_shape=[jax.ShapeDtypeStruct((tokens, wdt), F32 if i < 3 else BF16) for i, wdt in enumerate(widths)],
        args=(x1, sh, sc, w_in))


LANES = 2 * HEAD_DIM


def _head_lane(shape):
    return lax.broadcasted_iota(jnp.int32, shape, 1) % HEAD_DIM


def _lane_half(shape):
    return lax.broadcasted_iota(jnp.int32, shape, 1) // HEAD_DIM


def _swap_rot(v):
    lane = _head_lane(v.shape)
    half = ROT_DIM // 2
    return jnp.where(lane < half, pltpu.roll(v, LANES - half, 1),
                     jnp.where(lane < ROT_DIM, pltpu.roll(v, half, 1), 0.0))


def _rope(v, cos_t, sin_t):
    return v * cos_t + _swap_rot(v) * sin_t


def _unrope(dv, cos_t, sin_t):
    return dv * cos_t + _swap_rot(dv * sin_t)


def _both_halves(t, g):
    return jnp.where(_lane_half(t.shape) == g, t, pltpu.roll(t, HEAD_DIM, 1))


def _fold_halves(t, g):
    return jnp.where(_lane_half(t.shape) == g, t + pltpu.roll(t, HEAD_DIM, 1), 0.0)


def _stack_heads(blocks):
    rows = []
    for blk in blocks:
        half = _lane_half(blk.shape)
        rows += [jnp.where(half == 0, blk, 0.0), jnp.where(half == 1, blk, 0.0)]
    return jnp.concatenate(rows, axis=0)


def _unstack_heads(t, j):
    lo = t[(2 * j) * ATTN_BLOCK:(2 * j + 1) * ATTN_BLOCK]
    hi = t[(2 * j + 1) * ATTN_BLOCK:(2 * j + 2) * ATTN_BLOCK]
    return jnp.where(_lane_half(lo.shape) == 0, lo, hi)


def _band_mask(q0, w0):
    rows, cols = GQA_GROUP * ATTN_BLOCK, 2 * ATTN_BLOCK
    qi = lax.broadcasted_iota(jnp.int32, (rows, cols), 0) % ATTN_BLOCK + q0
    ki = lax.broadcasted_iota(jnp.int32, (rows, cols), 1) + w0
    diff = qi - ki
    return (diff >= 0) & (diff < ATTN_BLOCK)


def _attn_specs(seq):
    q_spec = pl.BlockSpec((seq, GQA_GROUP * HEAD_DIM), lambda b, g: (b, g))
    kv_spec = pl.BlockSpec((seq, LANES), lambda b, g: (b, 0))
    sink_spec = pl.BlockSpec((1, GQA_GROUP * ATTN_BLOCK, 1), lambda b, g: (g, 0, 0))
    return q_spec, kv_spec, sink_spec


def _block_starts(n):
    q0 = pl.multiple_of(n * ATTN_BLOCK, ATTN_BLOCK)
    w0 = pl.multiple_of(jnp.maximum(n - 1, 0) * ATTN_BLOCK, ATTN_BLOCK)
    return q0, w0


def _stacked_queries(ref, rows):
    return _stack_heads([ref[rows, j * LANES:(j + 1) * LANES] for j in range(2)]).astype(BF16)


def _sink_columns(sinks):
    return jnp.repeat(sinks.reshape(N_KV_HEADS, GQA_GROUP), ATTN_BLOCK, axis=1)[:, :, None]


def _probs_spec(nblk):
    return pl.BlockSpec((1, 1, nblk, GQA_GROUP * ATTN_BLOCK, 2 * ATTN_BLOCK), lambda b, g: (b, g, 0, 0, 0))


def _sink_probs_spec():
    return pl.BlockSpec((1, 1, GQA_GROUP * ATTN_BLOCK, LANES), lambda b, g: (b, g, 0, 0))


def attn_fwd(q, k, v, cos_t, sin_t, sinks, seq, name, job=None):
    tokens = q.shape[0]
    nblk = seq // ATTN_BLOCK
    assert nblk >= 2
    scale = HEAD_DIM ** -0.5

    nseq = tokens // seq
    rows_stacked = GQA_GROUP * ATTN_BLOCK
    assert nblk <= LANES

    def body(q_ref, k_ref, v_ref, cos_ref, sin_ref, sink_ref, o_ref, qr_ref, p_ref, ps_ref, kd_ref, vd_ref):
        g = pl.program_id(1)
        kd_ref[...] = _both_halves(_rope(k_ref[...].astype(F32), cos_ref[...], sin_ref[...]), g).astype(BF16)
        vd_ref[...] = _both_halves(v_ref[...].astype(F32), g).astype(BF16)
        sink = sink_ref[0]
        lane = lax.broadcasted_iota(jnp.int32, (rows_stacked, LANES), 1)

        ps_ref[...] = jnp.zeros_like(ps_ref)

        def block(n, carry):
            q0, w0 = _block_starts(n)
            rows, win = pl.ds(q0, ATTN_BLOCK), pl.ds(w0, 2 * ATTN_BLOCK)
            blocks = []
            for j in range(2):
                qr = _rope(q_ref[rows, j * LANES:(j + 1) * LANES].astype(F32), cos_ref[rows, :], sin_ref[rows, :]).astype(BF16)
                qr_ref[rows, j * LANES:(j + 1) * LANES] = qr
                blocks.append(qr)
            qs = _stack_heads(blocks)
            s = _dot_nt(qs, kd_ref[win, :]) * scale
            s = jnp.where(_band_mask(q0, w0), s, NEG_BIG)
            m = jnp.maximum(jnp.max(s, axis=-1, keepdims=True), sink)
            p = jnp.exp(s - m)
            e_sink = jnp.exp(sink - m)
            inv = pl.reciprocal(jnp.sum(p, axis=-1, keepdims=True) + e_sink, approx=True)
            pn = (p * inv).astype(BF16)
            p_ref[0, 0, n] = pn
            out = _dot(pn, vd_ref[win, :])
            for j in range(2):
                o_ref[rows, j * LANES:(j + 1) * LANES] = _unstack_heads(out, j).astype(o_ref.dtype)
            ps_ref[0, 0] = jnp.where(lane == n, e_sink * inv, ps_ref[0, 0])
            return carry

        lax.fori_loop(0, nblk, block, 0, unroll=2)

    q_spec, kv_spec, sink_spec = _attn_specs(seq)
    return _call(
        body, job, name=name, grid=(nseq, N_KV_HEADS),
        in_specs=[q_spec, kv_spec, kv_spec, kv_spec, kv_spec, sink_spec],
        out_specs=[q_spec, q_spec, _probs_spec(nblk), _sink_probs_spec()],
        out_shape=[jax.ShapeDtypeStruct(q.shape, BF16), jax.ShapeDtypeStruct(q.shape, BF16),
                   jax.ShapeDtypeStruct((nseq, N_KV_HEADS, nblk, rows_stacked, 2 * ATTN_BLOCK), BF16),
                   jax.ShapeDtypeStruct((nseq, N_KV_HEADS, rows_stacked, LANES), F32)],
        scratch_shapes=[pltpu.VMEM((seq, LANES), BF16), pltpu.VMEM((seq, LANES), BF16)],
        args=(q, k, v, cos_t, sin_t, _sink_columns(sinks)))


def attn_bwd(qr, k, v, do, probs, sink_probs, cos_t, sin_t, seq, name, job=None):
    tokens = qr.shape[0]
    nseq = tokens // seq
    nblk = seq // ATTN_BLOCK
    assert nblk >= 2
    rows_stacked = GQA_GROUP * ATTN_BLOCK
    scale = HEAD_DIM ** -0.5

    def body(q_ref, k_ref, v_ref, do_ref, p_ref, ps_ref, cos_ref, sin_ref, dq_ref, dk_ref, dv_ref, ds_ref,
             kd_ref, vd_ref, dkd_ref, dvd_ref, acc_ref):
        g = pl.program_id(1)
        kd_ref[...] = _both_halves(_rope(k_ref[...].astype(F32), cos_ref[...], sin_ref[...]), g).astype(BF16)
        vd_ref[...] = _both_halves(v_ref[...].astype(F32), g).astype(BF16)
        dkd_ref[...] = jnp.zeros_like(dkd_ref)
        dvd_ref[...] = jnp.zeros_like(dvd_ref)
        acc_ref[...] = jnp.zeros_like(acc_ref)
        lane = lax.broadcasted_iota(jnp.int32, (rows_stacked, LANES), 1)

        def block(n, carry):
            q0, w0 = _block_starts(n)
            rows, win = pl.ds(q0, ATTN_BLOCK), pl.ds(w0, 2 * ATTN_BLOCK)
            qs = _stacked_queries(q_ref, rows)
            dos = _stacked_queries(do_ref, rows)
            kw, vw = kd_ref[win, :], vd_ref[win, :]
            pn16 = p_ref[0, 0, n]
            pn = pn16.astype(F32)
            dvd_ref[win, :] += _dot_tn(pn16, dos)
            dp = _dot_nt(dos, vw)
            delta = jnp.sum(dp * pn, axis=-1, keepdims=True)
            ds = (pn * (dp - delta)).astype(BF16)
            dqs = _dot(ds, kw) * scale
            dkd_ref[win, :] += _dot_tn(ds, qs) * scale
            cos_b, sin_b = cos_ref[rows, :], sin_ref[rows, :]
            for j in range(2):
                dq_ref[rows, j * LANES:(j + 1) * LANES] = _unrope(_unstack_heads(dqs, j), cos_b, sin_b).astype(BF16)
            acc_ref[...] += jnp.where(lane == n, ps_ref[0, 0] * delta, 0.0)
            return carry

        lax.fori_loop(0, nblk // 2, lambda i, carry: block(2 * i + 1, block(2 * i, carry)), 0)
        ds_ref[0, 0] = -jnp.sum(acc_ref[...], axis=-1, keepdims=True)
        dk_g = _unrope(_fold_halves(dkd_ref[...], g), cos_ref[...], sin_ref[...])
        dv_g = _fold_halves(dvd_ref[...], g)

        @pl.when(g == 0)
        def _():
            dk_ref[...] = dk_g
            dv_ref[...] = dv_g

        @pl.when(g != 0)
        def _():
            dk_ref[...] += dk_g
            dv_ref[...] += dv_g

    q_spec, kv_spec, _ = _attn_specs(seq)
    return _call(
        body, job, name=name, grid=(nseq, N_KV_HEADS),
        in_specs=[q_spec, kv_spec, kv_spec, q_spec, _probs_spec(nblk), _sink_probs_spec(), kv_spec, kv_spec],
        out_specs=[q_spec, kv_spec, kv_spec, pl.BlockSpec((1, 1, rows_stacked, 1), lambda b, g: (b, g, 0, 0))],
        out_shape=[jax.ShapeDtypeStruct(qr.shape, BF16), jax.ShapeDtypeStruct(k.shape, F32),
                   jax.ShapeDtypeStruct(k.shape, F32), jax.ShapeDtypeStruct((nseq, N_KV_HEADS, rows_stacked, 1), F32)],
        scratch_shapes=[pltpu.VMEM((seq, LANES), BF16), pltpu.VMEM((seq, LANES), BF16),
                        pltpu.VMEM((seq, LANES), F32), pltpu.VMEM((seq, LANES), F32),
                        pltpu.VMEM((rows_stacked, LANES), F32)],
        args=(qr, k, v, do, probs, sink_probs, cos_t, sin_t))


CONV_COLS = 128


def _shift_down(z, by):
    t = lax.broadcasted_iota(jnp.int32, z.shape, 0)
    return jnp.where(t >= by, pltpu.roll(z, by, 0), 0.0)


def _shift_up(z, by):
    n = z.shape[0]
    t = lax.broadcasted_iota(jnp.int32, z.shape, 0)
    return jnp.where(t < n - by, pltpu.roll(z, n - by, 0), 0.0)


def conv_fwd(u, bg, cg, conv_w, seq, name):
    tokens, width = u.shape

    def body(u_ref, bg_ref, cg_ref, w_ref, o_ref):
        z = cg_ref[...].astype(F32) * u_ref[...].astype(F32)
        yy = w_ref[2:3, :] * z + w_ref[1:2, :] * _shift_down(z, 1) + w_ref[0:1, :] * _shift_down(z, 2)
        o_ref[...] = (bg_ref[...].astype(F32) * yy).astype(BF16)

    col = pl.BlockSpec((seq, CONV_COLS), lambda j, b: (b, j))
    return pl.pallas_call(
        body, name=name, grid=(width // CONV_COLS, tokens // seq),
        in_specs=[col, col, col, pl.BlockSpec((CONV_TAPS, CONV_COLS), lambda j, b: (0, j))],
        out_specs=col, out_shape=jax.ShapeDtypeStruct((tokens, width), BF16),
        compiler_params=_params(("parallel", "parallel")),
    )(u, bg, cg, conv_w)


def conv_bwd(dout, u, bg, cg, conv_w, seq, name):
    tokens, width = u.shape

    def body(do_ref, u_ref, bg_ref, cg_ref, w_ref, du_ref, dbg_ref, dcg_ref, dw_ref):
        uu, cg_v, do = u_ref[...].astype(F32), cg_ref[...].astype(F32), do_ref[...].astype(F32)
        z = cg_v * uu
        z1, z2 = _shift_down(z, 1), _shift_down(z, 2)
        yy = w_ref[2:3, :] * z + w_ref[1:2, :] * z1 + w_ref[0:1, :] * z2
        dbg_ref[...] = (do * yy).astype(BF16)
        dyy = do * bg_ref[...].astype(F32)
        dz = w_ref[2:3, :] * dyy + w_ref[1:2, :] * _shift_up(dyy, 1) + w_ref[0:1, :] * _shift_up(dyy, 2)
        du_ref[...] = (dz * cg_v).astype(BF16)
        dcg_ref[...] = (dz * uu).astype(BF16)

        @pl.when(pl.program_id(1) == 0)
        def _():
            dw_ref[...] = jnp.zeros_like(dw_ref)

        dw_ref[0:1, :] += jnp.sum(dyy * z2, axis=0, keepdims=True)
        dw_ref[1:2, :] += jnp.sum(dyy * z1, axis=0, keepdims=True)
        dw_ref[2:3, :] += jnp.sum(dyy * z, axis=0, keepdims=True)

    col = pl.BlockSpec((seq, CONV_COLS), lambda j, b: (b, j))
    w_spec = pl.BlockSpec((CONV_TAPS, CONV_COLS), lambda j, b: (0, j))
    act = jax.ShapeDtypeStruct((tokens, width), BF16)
    return pl.pallas_call(
        body, name=name, grid=(width // CONV_COLS, tokens // seq),
        in_specs=[col, col, col, col, w_spec], out_specs=[col, col, col, w_spec],
        out_shape=[act, act, act, jax.ShapeDtypeStruct((CONV_TAPS, width), F32)],
        compiler_params=_params(("parallel", "arbitrary")),
    )(dout, u, bg, cg, conv_w)


def out_fwd(x1, attn, conv, gt, w_out, ln_g, ln_b, seq, name, job=None):
    tokens, dm = x1.shape
    half = attn.shape[1]
    tm = min(MIX_TILE, seq)
    tiles_per_seq = seq // tm

    def body(x_ref, a_ref, c_ref, gt_ref, w_ref, lg_ref, lb_ref, xo_ref, r_ref, mi_ref, mix_ref):
        mixin = jnp.concatenate([a_ref[...], c_ref[...]], axis=1).astype(BF16)
        mi_ref[...] = mixin
        mix = _dot(mixin, w_ref[...])
        mix_ref[...] = mix.astype(BF16)
        r = DN_ALPHA * x_ref[...] + (1.0 + gt_ref[0]) * mix
        r_ref[...] = r
        xhat, _ = _ln_stats(r)
        xo_ref[...] = xhat * lg_ref[...] + lb_ref[...]

    tile = pl.BlockSpec((tm, dm), lambda i: (i, 0))
    htile = pl.BlockSpec((tm, half), lambda i: (i, 0))
    return _call(
        body, job, name=name, grid=(tokens // tm,),
        in_specs=[tile, htile, htile, _mod_spec(tiles_per_seq, dm), _const_spec(w_out.shape),
                  _const_spec((1, dm)), _const_spec((1, dm))],
        out_specs=[tile, tile, tile, tile],
        out_shape=[jax.ShapeDtypeStruct((tokens, dm), F32), jax.ShapeDtypeStruct((tokens, dm), F32),
                   jax.ShapeDtypeStruct((tokens, dm), BF16), jax.ShapeDtypeStruct((tokens, dm), BF16)],
        args=(x1, attn, conv, gt, w_out, ln_g, ln_b))


def out_bwd(dy, r, mix, gt, w_out, ln_g, seq, name, job=None):
    tokens, dm = r.shape
    half = dm // 2
    tm = min(MIX_TILE, seq)
    tiles_per_seq = seq // tm
    nseq = tokens // seq

    def body(dy_ref, r_ref, mix_ref, gt_ref, w_ref, lg_ref, dres_ref, da_ref, dc_ref, dmix_ref, dln_ref, dgt_ref):
        i = pl.program_id(0)
        dr, dgain, dbias = _ln_bwd(dy_ref[...], r_ref[...], lg_ref[...])

        @pl.when(i == 0)
        def _():
            dln_ref[...] = jnp.zeros_like(dln_ref)

        @pl.when(i % tiles_per_seq == 0)
        def _():
            dgt_ref[...] = jnp.zeros_like(dgt_ref)

        dln_ref[0:1, :] += dgain
        dln_ref[1:2, :] += dbias
        dgt_ref[0] += jnp.sum(dr * mix_ref[...].astype(F32), axis=0, keepdims=True)
        dres_ref[...] = DN_ALPHA * dr
        dmix = ((1.0 + gt_ref[0]) * dr).astype(BF16)
        dmix_ref[...] = dmix
        dmixin = _dot_nt(dmix, w_ref[...])
        da_ref[...] = dmixin[:, :half].astype(BF16)
        dc_ref[...] = dmixin[:, half:].astype(BF16)

    tile = pl.BlockSpec((tm, dm), lambda i: (i, 0))
    htile = pl.BlockSpec((tm, half), lambda i: (i, 0))
    return _call(
        body, job, name=name, grid=(tokens // tm,),
        in_specs=[tile, tile, tile, _mod_spec(tiles_per_seq, dm), _const_spec(w_out.shape), _const_spec((1, dm))],
        out_specs=[tile, htile, htile, tile, pl.BlockSpec((2, dm), lambda i: (0, 0)),
                   pl.BlockSpec((1, 1, dm), lambda i: (i // tiles_per_seq, 0, 0))],
        out_shape=[jax.ShapeDtypeStruct((tokens, dm), F32), jax.ShapeDtypeStruct((tokens, half), BF16),
                   jax.ShapeDtypeStruct((tokens, half), BF16), jax.ShapeDtypeStruct((tokens, dm), BF16),
                   jax.ShapeDtypeStruct((2, dm), F32), jax.ShapeDtypeStruct((nseq, 1, dm), F32)],
        args=(dy, r, mix, gt, w_out, ln_g))


def proj_bwd(parts, dres, x1, sh, sc, w_in, r_prev, f_prev, gt_prev, ln_g_prev, seq, name, job=None):
    tokens, dm = x1.shape
    tm = min(MIX_TILE, seq)
    tiles_per_seq = seq // tm
    nseq = tokens // seq
    widths = [p.shape[1] for p in parts]
    total = sum(widths)

    def body(*refs):
        part_refs = refs[:6]
        (dres_ref, x_ref, sh_ref, sc_ref, w_ref, r_ref, f_ref, gt_ref, lg_ref,
         dr_ref, df_ref, dproj_ref, h_ref, dmod_ref, dln_ref, dgt_ref) = refs[6:]
        i = pl.program_id(0)
        dproj = jnp.concatenate([p[...].astype(BF16) for p in part_refs], axis=1)
        dproj_ref[...] = dproj
        dh = _dot(dproj, w_ref[...])
        xx = x_ref[...]
        one_sc = 1.0 + sc_ref[0]
        h_ref[...] = (xx * one_sc + sh_ref[0]).astype(BF16)
        dr, dgain, dbias = _ln_bwd(dres_ref[...] + dh * one_sc, r_ref[...], lg_ref[...])
        dr_ref[...] = dr
        df_ref[...] = ((0.5 * (1.0 + gt_ref[0])) * dr).astype(BF16)

        @pl.when(i == 0)
        def _():
            dln_ref[...] = jnp.zeros_like(dln_ref)

        @pl.when(i % tiles_per_seq == 0)
        def _():
            dmod_ref[...] = jnp.zeros_like(dmod_ref)
            dgt_ref[...] = jnp.zeros_like(dgt_ref)

        dmod_ref[0, 0:1, :] += jnp.sum(dh, axis=0, keepdims=True)
        dmod_ref[0, 1:2, :] += jnp.sum(dh * xx, axis=0, keepdims=True)
        dln_ref[0:1, :] += dgain
        dln_ref[1:2, :] += dbias
        dgt_ref[0] += jnp.sum(dr * (0.5 * f_ref[...].astype(F32)), axis=0, keepdims=True)

    tile = pl.BlockSpec((tm, dm), lambda i: (i, 0))
    mod = _mod_spec(tiles_per_seq, dm)
    return _call(
        body, job, name=name, grid=(tokens // tm,),
        in_specs=[pl.BlockSpec((tm, wdt), lambda i: (i, 0)) for wdt in widths]
        + [tile, tile, mod, mod, _const_spec(w_in.shape), tile, tile, mod, _const_spec((1, dm))],
        out_specs=[tile, tile, pl.BlockSpec((tm, total), lambda i: (i, 0)), tile,
                   pl.BlockSpec((1, 2, dm), lambda i: (i // tiles_per_seq, 0, 0)),
                   pl.BlockSpec((2, dm), lambda i: (0, 0)), mod],
        out_shape=[jax.ShapeDtypeStruct((tokens, dm), F32), jax.ShapeDtypeStruct((tokens, dm), BF16),
                   jax.ShapeDtypeStruct((tokens, total), BF16), jax.ShapeDtypeStruct((tokens, dm), BF16),
                   jax.ShapeDtypeStruct((nseq, 2, dm), F32), jax.ShapeDtypeStruct((2, dm), F32),
                   jax.ShapeDtypeStruct((nseq, 1, dm), F32)],
        args=(*parts, dres, x1, sh, sc, w_in, r_prev, f_prev, gt_prev, ln_g_prev))


def _rope_tables(positions):
    half = ROT_DIM // 2
    inv_freq = jnp.power(jnp.float32(ROPE_THETA), -jnp.arange(0, ROT_DIM, 2, dtype=F32) / ROT_DIM)
    lane = jnp.arange(LANES) % HEAD_DIM
    freq = jnp.where(lane < ROT_DIM, inv_freq[lane % half], 0.0)
    sign = jnp.where(lane < half, -1.0, 1.0).astype(F32)
    ang = positions.astype(F32)[:, None] * freq[None, :]
    return jnp.cos(ang), sign[None, :] * jnp.sin(ang)


def kernel(x, c, positions, w_ada, b_ada, ffn1_w_gate_up, ffn1_w_down, ln1_g, ln1_b, w_in, conv_w, attn_sinks, w_out, ln2_g, ln2_b, ffn2_w_gate_up, ffn2_w_down, ln3_g, ln3_b, loss_target, m_w_ada, m_b_ada, m_ffn1_w_gate_up, m_ffn1_w_down, m_ln1_g, m_ln1_b, m_w_in, m_conv_w, m_attn_sinks, m_w_out, m_ln2_g, m_ln2_b, m_ffn2_w_gate_up, m_ffn2_w_down, m_ln3_g, m_ln3_b, v_w_ada, v_b_ada, v_ffn1_w_gate_up, v_ffn1_w_down, v_ln1_g, v_ln1_b, v_w_in, v_conv_w, v_attn_sinks, v_w_out, v_ln2_g, v_ln2_b, v_ffn2_w_gate_up, v_ffn2_w_down, v_ln3_g, v_ln3_b):
    nseq, seq, dm = x.shape
    tokens = nseq * seq
    dev = 4 * lax.axis_index("x") + 2 * lax.axis_index("y") + lax.axis_index("c")
    core = lax.axis_index("c").astype(jnp.int32).reshape(1)
    ada_cols = w_ada.shape[2]
    ff = ffn1_w_down.shape[1] * N_DEV
    fc = ff // 4
    in_cols = w_in.shape[2]
    conv_cols = conv_w.shape[2]

    def t_bf16(w):
        return w[0].T.astype(BF16)

    c_all, convw_all = all_gather([c, conv_w[0]], "gather_cond")
    c_all = c_all.reshape(N_DEV * nseq, dm)
    convw_full = convw_all.transpose(1, 0, 2).reshape(CONV_TAPS, N_DEV * conv_cols)

    b_cols = lax.dynamic_slice(b_ada, (0, dev * ada_cols), (1, ada_cols))
    cond_all, mod_cols = ada_fwd(c_all, w_ada[0], b_cols, "ada_fwd")
    wgu1, mod_all = all_gather([t_bf16(ffn1_w_gate_up), mod_cols], "gather_ffn1")
    wgu1 = wgu1.reshape(2, ff, dm)
    mod = lax.dynamic_slice(mod_all, (0, dev * nseq, 0), (N_DEV, nseq, ada_cols))
    mod = mod.transpose(1, 0, 2).reshape(nseq, 9, 1, dm)
    sh1, sc1, g1, sh2, sc2, g2, sh3, sc3, g3 = [mod[:, i] for i in range(9)]

    x0 = x.reshape(tokens, dm)
    (gu1, a1, h1), (wd1, wout) = ffn_up(x0, sh1, sc1, wgu1, seq, "ffn1_up",
                                        job=_GatherJob([ffn1_w_down[0].astype(BF16), w_out[0].astype(BF16)]))
    wd1, wout = wd1.reshape(ff, dm), wout.reshape(dm, dm)
    (x1, r1, f1), (win,) = ffn_down(x0, a1, g1, wd1, ln1_g, ln1_b, seq, "ffn1_down", job=_GatherJob([t_bf16(w_in)]))
    win = win.reshape(N_DEV * in_cols, dm)
    (q, k, v, u, bg, cg), wd2_spread = proj_fwd(x1, sh2, sc2, win, seq, "proj_fwd",
                                                job=gather_spread_job([ffn2_w_down[0].astype(BF16)]))
    cos_t, sin_t = _rope_tables(positions.reshape(tokens))
    sinks = attn_sinks[0]
    (attn, q_rot, probs, sink_probs), wgu2_spread = attn_fwd(q, k, v, cos_t, sin_t, sinks, seq, "attn_fwd",
                                                             job=gather_spread_job([t_bf16(ffn2_w_gate_up)]))
    conv = conv_fwd(u, bg, cg, convw_full, seq, "conv_fwd")
    (x2, r2, mixin, mix), (wd2, wgu2) = out_fwd(x1, attn, conv, g2, wout, ln2_g, ln2_b, seq, "out_fwd",
                                                job=gather_forward_job(wd2_spread + wgu2_spread))
    wd2, wgu2 = wd2.reshape(ff, dm), wgu2.reshape(2, ff, dm)
    target = loss_target.reshape(tokens, dm)
    dr3, df3, gu3, a3, h3, loss_part, dln3, dg3 = ffn_loss(x2, sh3, sc3, g3, wgu2, wd2, ln3_g, ln3_b, target, seq, "ffn2_fwd")

    (dx2, dgu3, dmod3), _ = ffn_bwd(dr3, df3, x2, gu3, sc3, wgu2, wd2, seq, "ffn2_bwd")
    pair = 2 * fc
    g_wd2 = tn_matmul(a3[None], df3[None], "ffn2_dwd", a_width=pair)[0][0].reshape(N_DEV, ff // N_DEV, dm)
    g_wgu2 = tn_matmul(dgu3, h3[None], "ffn2_dwgu", a_width=pair)[0][0].reshape(N_DEV, fc, dm)
    (dres2, dattn, dconv, dmix, dln2, dg2), swapped = out_bwd(dx2, r2, mix, g2, wout, ln2_g, seq, "out_bwd",
                                                              job=swap_job([g_wgu2, g_wd2]))
    p_wgu2, own_wgu2 = pair_sum(core, g_wgu2, swapped[0], "pair_wgu2")
    p_wd2, own_wd2 = pair_sum(core, g_wd2, swapped[1], "pair_wd2")
    du, dbg, dcg, dconvw = conv_bwd(dconv, u, bg, cg, convw_full, seq, "conv_bwd")
    (dq, dk, dv, dsink_rows), (far_wd2,) = attn_bwd(
        q_rot, k, v, dattn, probs, sink_probs, cos_t, sin_t, seq, "attn_bwd", job=chip_exchange_job([p_wd2]))
    parts = [dq, dk, dv, du, dbg, dcg]
    (dr1, df1, dproj, h2, dmod2, dln1, dg1), far_top = proj_bwd(
        parts, dres2, x1, sh2, sc2, win, r1, f1, g1, ln1_g, seq, "proj_bwd",
        job=chip_exchange_job([p_wgu2], rows=(0, fc // 2)))
    (dx0, dgu1, dmod1), _ = ffn_bwd(dr1, df1, x0, gu1, sc1, wgu1, wd1, seq, "ffn1_bwd")

    dmod = jnp.concatenate([dmod1, dg1, dmod2, dg2, dmod3, dg3], axis=1).reshape(nseq, 9 * dm)
    half = dm // 2
    jobs = _Jobs([gather_spread_job([dmod]),
                  chip_exchange_job([p_wgu2], rows=(fc // 2, fc // 2), into=far_top)])
    (g_wd1,), res = tn_matmul(a1[None], df1[None], "ffn1_dwd", job=jobs, a_width=pair)
    dmod_spread, (far_wgu2,) = jobs.split(res)
    g_wd1 = g_wd1.reshape(N_DEV, ff // N_DEV, dm)
    jobs = _Jobs([swap_job([g_wd1]), gather_forward_job(dmod_spread)])
    (g_l,), res = tn_matmul(dgu1, h1[None], "ffn1_dwgu_l", job=jobs, b_cols=(0, half), a_width=pair)
    (sw_wd1,), (dmod_all,) = jobs.split(res)
    g_l = g_l.reshape(N_DEV, fc, half)
    p_wd1, own_wd1 = pair_sum(core, g_wd1, sw_wd1, "pair_wd1")
    jobs = _Jobs([chip_exchange_job([p_wd1]), swap_job([g_l])])
    (g_r,), res = tn_matmul(dgu1, h1[None], "ffn1_dwgu_r", job=jobs, b_cols=(1, half), a_width=pair)
    (far_wd1,), (sw_l,) = jobs.split(res)
    g_r = g_r.reshape(N_DEV, fc, half)
    p_l, own_l = pair_sum(core, g_l, sw_l, "pair_wgu1_l")

    dmod_cols = lax.dynamic_slice(dmod_all.reshape(N_DEV * nseq, 9 * dm), (0, dev * ada_cols), (N_DEV * nseq, ada_cols))
    grad_w_ada, gb_cols = ada_bwd(cond_all, dmod_cols, "ada_bwd")
    dsinks = jnp.sum(dsink_rows.reshape(nseq, N_Q_HEADS, ATTN_BLOCK), axis=(0, 2))
    small = jnp.zeros((8, dm), F32)
    small = small.at[0:2].set(dln1).at[2:4].set(dln2).at[4:6].set(dln3)
    small = small.at[6, 0:N_Q_HEADS].set(dsinks).at[7, 0].set(loss_part[0, 0])

    jobs = _Jobs([chip_exchange_job([p_l]), swap_job([g_r]), gather_spread_job([small, dconvw, gb_cols])])
    (g_win,), res = tn_matmul(dproj[None], h2[None], "dwin", job=jobs)
    (far_l,), (sw_r,), small_spread = jobs.split(res)
    g_win = g_win.reshape(N_DEV, in_cols, dm)
    p_r, own_r = pair_sum(core, g_r, sw_r, "pair_wgu1_r")
    jobs = _Jobs([chip_exchange_job([p_r]), swap_job([g_win]), gather_forward_job(small_spread)])
    (g_wout,), res = tn_matmul(mixin[None], dmix[None], "dwout", job=jobs)
    (far_r,), (sw_win,), (small_all, dconvw_all, gb_all) = jobs.split(res)
    g_wout = g_wout.reshape(N_DEV, dm // N_DEV, dm)
    p_win, own_win = pair_sum(core, g_win, sw_win, "pair_win")

    given = dict(w_ada=(w_ada, m_w_ada, v_w_ada), b_ada=(b_ada, m_b_ada, v_b_ada),
                 ffn1_w_gate_up=(ffn1_w_gate_up, m_ffn1_w_gate_up, v_ffn1_w_gate_up),
                 ffn1_w_down=(ffn1_w_down, m_ffn1_w_down, v_ffn1_w_down),
                 ln1_g=(ln1_g, m_ln1_g, v_ln1_g), ln1_b=(ln1_b, m_ln1_b, v_ln1_b),
                 w_in=(w_in, m_w_in, v_w_in), conv_w=(conv_w, m_conv_w, v_conv_w),
                 attn_sinks=(attn_sinks, m_attn_sinks, v_attn_sinks), w_out=(w_out, m_w_out, v_w_out),
                 ln2_g=(ln2_g, m_ln2_g, v_ln2_g), ln2_b=(ln2_b, m_ln2_b, v_ln2_b),
                 ffn2_w_gate_up=(ffn2_w_gate_up, m_ffn2_w_gate_up, v_ffn2_w_gate_up),
                 ffn2_w_down=(ffn2_w_down, m_ffn2_w_down, v_ffn2_w_down),
                 ln3_g=(ln3_g, m_ln3_g, v_ln3_g), ln3_b=(ln3_b, m_ln3_b, v_ln3_b))
    transposed = ("ffn1_w_gate_up", "ffn2_w_gate_up", "w_in")

    def big_adamw(nm, grad, far=None):
        flip = nm in transposed
        w2, m2, v2 = [t[0].T if flip else t[0] for t in given[nm]]
        return [t.T[None] if flip else t[None] for t in adamw(w2, grad, m2, v2, "adamw_" + nm, others=far)]

    jobs = _Jobs([chip_exchange_job([p_win]), swap_job([g_wout])])
    (far_win,), (sw_wout,) = jobs.split(run_job(jobs, "rs_tail_win"))
    p_wout, own_wout = pair_sum(core, g_wout, sw_wout, "pair_wout")
    (far_wout,) = run_job(chip_exchange_job([p_wout]), "rs_tail_wout")

    grads = {
        "ffn1_w_gate_up": jnp.concatenate([own_l, own_r], axis=1), "ffn1_w_down": own_wd1,
        "w_in": own_win, "w_out": own_wout, "ffn2_w_gate_up": own_wgu2, "ffn2_w_down": own_wd2,
    }
    others = {"ffn1_w_gate_up": jnp.concatenate([far_l, far_r], axis=2), "ffn1_w_down": far_wd1,
              "w_in": far_win, "w_out": far_wout, "ffn2_w_gate_up": far_wgu2, "ffn2_w_down": far_wd2}
    results = {"w_ada": big_adamw("w_ada", grad_w_ada)}
    for nm in grads:
        results[nm] = big_adamw(nm, grads[nm], others[nm])

    small_sum = sum_devices(small_all, "sum_small")
    dconvw_sum = sum_devices(dconvw_all, "sum_convw")
    loss = small_sum[7, 0]
    grads["b_ada"] = gb_all.reshape(1, N_DEV * ada_cols)
    grads["conv_w"] = lax.dynamic_slice(dconvw_sum, (0, dev * conv_cols), (CONV_TAPS, conv_cols))
    grads["attn_sinks"] = small_sum[6:7, 0:N_Q_HEADS]
    for i, nm in enumerate(["ln1_g", "ln1_b", "ln2_g", "ln2_b", "ln3_g", "ln3_b"]):
        grads[nm] = small_sum[i:i + 1]

    order = ["w_ada", "b_ada", "ffn1_w_gate_up", "ffn1_w_down", "ln1_g", "ln1_b", "w_in", "conv_w", "attn_sinks",
             "w_out", "ln2_g", "ln2_b", "ffn2_w_gate_up", "ffn2_w_down", "ln3_g", "ln3_b"]
    small_names = [nm for nm in order if nm not in results]
    items = []
    for nm in small_names:
        shape = given[nm][0].shape
        two_d = (shape[-2], shape[-1])
        items.append((given[nm][0].reshape(two_d), grads[nm].reshape(two_d), *[t.reshape(two_d) for t in given[nm][1:]]))
    for nm, res in zip(small_names, adamw_small(items, "adamw_small")):
        shape = given[nm][0].shape
        results[nm] = [grads[nm].reshape(shape)] + [t.reshape(shape) for t in res]
    grad_x = dx0.reshape(nseq, seq, dm)
    return (loss, grad_x, *[results[nm][i] for i in range(4) for nm in order])
```

```python
import functools

import jax
import jax.numpy as jnp
from jax import lax
from jax.experimental import pallas as pl
from jax.experimental.pallas import tpu as pltpu

F32 = jnp.float32
BF16 = jnp.bfloat16
MESH = pl.DeviceIdType.MESH

N_DEV = 8
N_CHIP = 4
HEAD_DIM = 64
N_Q_HEADS = 8
N_KV_HEADS = 2
GQA_GROUP = N_Q_HEADS // N_KV_HEADS
ATTN_BLOCK = 128
ROT_DIM = 16
ROPE_THETA = 500000.0
CONV_TAPS = 3
LN_EPS = 1e-5
DN_ALPHA = 2.0 ** 0.25
ADAM_LR = 0.001
ADAM_B1 = 0.9
ADAM_B2 = 0.999
ADAM_EPS = 1e-08
ADAM_WD = 0.01
ADAM_STEP = 10
NEG_BIG = -1e30

VMEM_LIMIT = 56 * 1024 * 1024
TOKEN_TILE = 256
FFN_FWD_TILE = 512
MIX_TILE = 512
FFN_BWD_TILE = 512
FFN_BWD_VMEM = 62 * 1024 * 1024
TN_VMEM_BUDGET = 36 * 1024 * 1024


def _params(semantics=None, vmem=VMEM_LIMIT):
    return pltpu.CompilerParams(dimension_semantics=semantics, vmem_limit_bytes=vmem)


def _dot(a, b):
    return jnp.dot(a, b, preferred_element_type=F32)


def _dot_nt(a, b):
    return lax.dot_general(a, b, (((1,), (1,)), ((), ())), preferred_element_type=F32)


def _dot_tn(a, b):
    return lax.dot_general(a, b, (((0,), (0,)), ((), ())), preferred_element_type=F32)


def _sigmoid(x):
    return pl.reciprocal(1.0 + jnp.exp(-x), approx=True)


def _ln_stats(r):
    mu = jnp.mean(r, axis=-1, keepdims=True)
    d = r - mu
    var = jnp.mean(d * d, axis=-1, keepdims=True)
    rstd = lax.rsqrt(var + LN_EPS)
    return d * rstd, rstd


def _ln_bwd(dy, r, g):
    return _ln_bwd_normalized(dy, *_ln_stats(r), g)


def _ln_bwd_normalized(dy, xhat, rstd, g):
    dxhat = dy * g
    c1 = jnp.mean(dxhat, axis=-1, keepdims=True)
    c2 = jnp.mean(dxhat * xhat, axis=-1, keepdims=True)
    dr = rstd * (dxhat - c1 - xhat * c2)
    return dr, jnp.sum(dy * xhat, axis=0, keepdims=True), jnp.sum(dy, axis=0, keepdims=True)


def _const_spec(shape):
    nd = len(shape)
    return pl.BlockSpec(shape, lambda *_: (0,) * nd, pipeline_mode=pl.Buffered(1))


def all_gather(arrs, name):
    n = len(arrs)

    def body(*refs):
        ins, outs = refs[:n], refs[n:2 * n]
        send_sems, recv_sems, local_sems = refs[2 * n:]
        x, y, c = lax.axis_index("x"), lax.axis_index("y"), lax.axis_index("c")
        me, sibling = (x, y, c), (x, y, 1 - c)
        chips = [(1 - x, y), (x, 1 - y), (1 - x, 1 - y)]

        def slot(i, p):
            return outs[i].at[4 * p[0] + 2 * p[1] + p[2]]

        def copy(i, k, block, to, src=None):
            return pltpu.make_async_remote_copy(
                src_ref=slot(i, block) if src is None else src, dst_ref=slot(i, block),
                send_sem=send_sems.at[i, k], recv_sem=recv_sems.at[i, k],
                device_id=to, device_id_type=MESH)

        mine = [pltpu.make_async_copy(ins[i], slot(i, me), local_sems.at[i]) for i in range(n)]
        for cp in mine:
            cp.start()
        first = []
        for i in range(n):
            first.append(copy(i, 0, me, sibling, src=ins[i]))
            first += [copy(i, 1 + j, me, (*chip, c), src=ins[i]) for j, chip in enumerate(chips)]
        for cp in first:
            cp.start()
        passed = []
        for i in range(n):
            for j, chip in enumerate(chips):
                copy(i, 1 + j, (*chip, c), me).wait_recv()
                cp = copy(i, 4 + j, (*chip, c), sibling)
                cp.start()
                passed.append(cp)
        for i in range(n):
            copy(i, 0, sibling, me).wait_recv()
            for j, chip in enumerate(chips):
                copy(i, 4 + j, (*chip, 1 - c), me).wait_recv()
        for cp in first + passed:
            cp.wait_send()
        for cp in mine:
            cp.wait()

    any_spec = pl.BlockSpec(memory_space=pl.ANY)
    return pl.pallas_call(
        body, name=name,
        out_shape=[jax.ShapeDtypeStruct((N_DEV, *a.shape), a.dtype) for a in arrs],
        in_specs=[any_spec] * n, out_specs=[any_spec] * n,
        scratch_shapes=[pltpu.SemaphoreType.DMA((n, 7)), pltpu.SemaphoreType.DMA((n, 7)),
                        pltpu.SemaphoreType.DMA((n,))],
    )(*arrs)


def _place():
    x, y, c = lax.axis_index("x"), lax.axis_index("y"), lax.axis_index("c")
    return x, y, c, [(1 - x, y), (x, 1 - y), (1 - x, 1 - y)]


def _slot(p):
    return 4 * p[0] + 2 * p[1] + p[2]


class _Job:
    def __init__(self, ins, outs, nsem, copies, aliases=None, local=None):
        self.ins, self.outs, self.nsem, self.copies = list(ins), list(outs), nsem, copies
        self.aliases = aliases or {}
        self.local = local

    def scratch(self):
        s = [pltpu.SemaphoreType.DMA(self.nsem), pltpu.SemaphoreType.DMA(self.nsem)]
        if self.local is not None:
            s.append(pltpu.SemaphoreType.DMA((len(self.ins),)))
        return s

    def start(self, ins, outs, sems):
        if self.local is not None:
            for cp in self.local(ins, outs, sems[2]):
                cp.start()
        for cp in self.copies(ins, outs, sems[0], sems[1])[0]:
            cp.start()

    def finish(self, ins, outs, sems):
        started, awaited = self.copies(ins, outs, sems[0], sems[1])
        for cp in awaited:
            cp.wait_recv()
        for cp in started:
            cp.wait_send()
        if self.local is not None:
            for cp in self.local(ins, outs, sems[2]):
                cp.wait()


class _Jobs:
    def __init__(self, jobs):
        self.jobs = jobs
        self.ins = [a for j in jobs for a in j.ins]
        self.outs = [o for j in jobs for o in j.outs]
        self.two_phase = any(getattr(j, "two_phase", False) for j in jobs)
        self.aliases = {}
        at_in = at_out = 0
        for j in jobs:
            self.aliases.update({at_in + i: at_out + o for i, o in j.aliases.items()})
            at_in, at_out = at_in + len(j.ins), at_out + len(j.outs)

    def scratch(self):
        return [s for j in self.jobs for s in j.scratch()]

    def _each(self, ins, outs, sems):
        at_in = at_out = at_sem = 0
        for j in self.jobs:
            n_in, n_out, n_sem = len(j.ins), len(j.outs), len(j.scratch())
            yield j, ins[at_in:at_in + n_in], outs[at_out:at_out + n_out], sems[at_sem:at_sem + n_sem]
            at_in, at_out, at_sem = at_in + n_in, at_out + n_out, at_sem + n_sem

    def start(self, ins, outs, sems):
        for j, i, o, s in self._each(ins, outs, sems):
            j.start(i, o, s)

    def turn(self, ins, outs, sems):
        for j, i, o, s in self._each(ins, outs, sems):
            if getattr(j, "two_phase", False):
                j.turn(i, o, s)

    def finish(self, ins, outs, sems):
        for j, i, o, s in self._each(ins, outs, sems):
            j.finish(i, o, s)

    def split(self, results):
        at, parts = 0, []
        for j in self.jobs:
            parts.append(results[at:at + len(j.outs)])
            at += len(j.outs)
        return parts


def _remote(src, dst, send, recv, idx, to):
    return pltpu.make_async_remote_copy(src_ref=src, dst_ref=dst, send_sem=send.at[idx], recv_sem=recv.at[idx],
                                        device_id=to, device_id_type=MESH)


def _spread_copies(ins, outs, send, recv, base=0):
    x, y, c, chips = _place()
    me = (x, y, c)
    peers = [(x, y, 1 - c)] + [(*chip, c) for chip in chips]
    started, awaited = [], []
    for i, (src, dst) in enumerate(zip(ins, outs)):
        for k, peer in enumerate(peers):
            started.append(_remote(src, dst.at[_slot(me)], send, recv, (base + i, k), peer))
            awaited.append(_remote(src, dst.at[_slot(peer)], send, recv, (base + i, k), peer))
    return started, awaited


def _forward_copies(ins, outs, send, recv, base=0):
    x, y, c, chips = _place()
    started, awaited = [], []
    for i, buf in enumerate(outs):
        for j, chip in enumerate(chips):
            mine, theirs = buf.at[_slot((*chip, c))], buf.at[_slot((*chip, 1 - c))]
            started.append(_remote(mine, mine, send, recv, (base + i, j), (x, y, 1 - c)))
            awaited.append(_remote(theirs, theirs, send, recv, (base + i, j), (x, y, 1 - c)))
    return started, awaited


def _own_block_copies(ins, outs, sems):
    x, y, c, _ = _place()
    return [pltpu.make_async_copy(src, dst.at[_slot((x, y, c))], sems.at[i])
            for i, (src, dst) in enumerate(zip(ins, outs))]


def gather_spread_job(shards):
    outs = [jax.ShapeDtypeStruct((N_DEV, *a.shape), a.dtype) for a in shards]
    return _Job(shards, outs, (len(shards), 4), _spread_copies, local=_own_block_copies)


def gather_forward_job(fulls):
    outs = [jax.ShapeDtypeStruct(a.shape, a.dtype) for a in fulls]
    return _Job(fulls, outs, (len(fulls), 3), _forward_copies, aliases={i: i for i in range(len(fulls))})


TURN_EIGHTHS = 6


class _GatherJob:
    two_phase = True

    def __init__(self, shards):
        self.ins = list(shards)
        self.outs = [jax.ShapeDtypeStruct((N_DEV, *a.shape), a.dtype) for a in shards]
        self.aliases = {}

    def scratch(self):
        n = len(self.ins)
        return [pltpu.SemaphoreType.DMA((n, 4)), pltpu.SemaphoreType.DMA((n, 4)),
                pltpu.SemaphoreType.DMA((n, 3)), pltpu.SemaphoreType.DMA((n, 3)), pltpu.SemaphoreType.DMA((n,))]

    def start(self, ins, outs, sems):
        for cp in _own_block_copies(ins, outs, sems[4]) + _spread_copies(ins, outs, sems[0], sems[1])[0]:
            cp.start()

    def turn(self, ins, outs, sems):
        for cp in _spread_copies(ins, outs, sems[0], sems[1])[1]:
            cp.wait_recv()
        for cp in _forward_copies(outs, outs, sems[2], sems[3])[0]:
            cp.start()

    def finish(self, ins, outs, sems):
        handed_on, arriving = _forward_copies(outs, outs, sems[2], sems[3])
        for cp in arriving:
            cp.wait_recv()
        for cp in _spread_copies(ins, outs, sems[0], sems[1])[0] + handed_on:
            cp.wait_send()
        for cp in _own_block_copies(ins, outs, sems[4]):
            cp.wait()


def swap_job(gs):
    def copies(ins, outs, send, recv):
        x, y, c, _ = _place()
        started, awaited = [], []
        for i, (g, r1) in enumerate(zip(ins, outs)):
            for q in range(N_CHIP):
                started.append(_remote(g.at[2 * q + (1 - c)], r1.at[q], send, recv, (i, q), (x, y, 1 - c)))
                awaited.append(_remote(g.at[2 * q + c], r1.at[q], send, recv, (i, q), (x, y, 1 - c)))
        return started, awaited

    outs = [jax.ShapeDtypeStruct((N_CHIP, *g.shape[1:]), g.dtype) for g in gs]
    return _Job(gs, outs, (len(gs), N_CHIP), copies)


def chip_exchange_job(ps, rows=None, into=None):
    n = len(ps)

    def copies(ins, outs, send, recv):
        x, y, c, chips = _place()
        started, awaited = [], []
        for i, (p, r2) in enumerate(zip(ins[:n], outs)):
            for k, chip in enumerate(chips):
                src, mine, dst = p.at[2 * chip[0] + chip[1]], p.at[2 * x + y], r2.at[k]
                if rows is not None:
                    src, mine, dst = (t.at[pl.ds(rows[0], rows[1])] for t in (src, mine, dst))
                started.append(_remote(src, dst, send, recv, (i, k), (*chip, c)))
                awaited.append(_remote(mine, dst, send, recv, (i, k), (*chip, c)))
        return started, awaited

    outs = [jax.ShapeDtypeStruct((3, *p.shape[1:]), p.dtype) for p in ps]
    if into is None:
        return _Job(ps, outs, (n, 3), copies)
    return _Job(list(ps) + list(into), outs, (n, 3), copies, aliases={n + i: i for i in range(n)})


def _call(body, job, *, name, grid, in_specs, out_specs, out_shape, args, scratch_shapes=(), vmem=VMEM_LIMIT):
    if job is None:
        res = pl.pallas_call(
            body, name=name, grid=grid, in_specs=in_specs, out_specs=out_specs, out_shape=out_shape,
            scratch_shapes=list(scratch_shapes), compiler_params=_params(("arbitrary",) * len(grid), vmem),
        )(*args)
        return res, []
    n_in, n_out, n_scr = len(in_specs), len(out_specs), len(scratch_shapes)
    j_in, j_out = len(job.ins), len(job.outs)

    def with_copies(*refs):
        at = 0
        ins = refs[at:at + n_in]; at += n_in
        jins = refs[at:at + j_in]; at += j_in
        outs = refs[at:at + n_out]; at += n_out
        jouts = refs[at:at + j_out]; at += j_out
        scr = refs[at:at + n_scr]; at += n_scr
        sems = refs[at:]
        ids = [pl.program_id(d) for d in range(len(grid))]
        first = functools.reduce(jnp.logical_and, [i == 0 for i in ids])
        last = functools.reduce(jnp.logical_and, [i == n - 1 for i, n in zip(ids, grid)])

        @pl.when(first)
        def _():
            job.start(jins, jouts, sems)

        if getattr(job, "two_phase", False):
            steps, at = 1, 0
            for i, n in zip(ids, grid):
                steps, at = steps * n, at * n + i

            @pl.when(at == (TURN_EIGHTHS * steps) // 8)
            def _():
                job.turn(jins, jouts, sems)

        body(*ins, *outs, *scr)

        @pl.when(last)
        def _():
            job.finish(jins, jouts, sems)

    any_spec = pl.BlockSpec(memory_space=pl.ANY)
    res = pl.pallas_call(
        with_copies, name=name, grid=grid,
        in_specs=list(in_specs) + [any_spec] * j_in, out_specs=list(out_specs) + [any_spec] * j_out,
        out_shape=list(out_shape) + list(job.outs),
        input_output_aliases={n_in + i: n_out + o for i, o in job.aliases.items()},
        scratch_shapes=list(scratch_shapes) + job.scratch(),
        compiler_params=_params(("arbitrary",) * len(grid), vmem),
    )(*args, *job.ins)
    return res[:n_out], res[n_out:]


def run_job(job, name):
    def body(*refs):
        j_in, j_out = len(job.ins), len(job.outs)
        ins, outs, sems = refs[:j_in], refs[j_in:j_in + j_out], refs[j_in + j_out:]
        job.start(ins, outs, sems)
        job.finish(ins, outs, sems)

    any_spec = pl.BlockSpec(memory_space=pl.ANY)
    return pl.pallas_call(
        body, name=name, in_specs=[any_spec] * len(job.ins), out_specs=[any_spec] * len(job.outs),
        out_shape=list(job.outs), input_output_aliases=dict(job.aliases), scratch_shapes=job.scratch(),
    )(*job.ins)


def pair_sum(core, g, r1, name):
    _, rows, cols = g.shape
    rb = next(cand for cand in range(min(rows, 512), 0, -16) if rows % cand == 0)

    def body(core_ref, g_ref, r1_ref, p_ref, own_ref):
        del core_ref
        x, y, _, _ = _place()
        s = g_ref[0].astype(F32) + r1_ref[0].astype(F32)
        p_ref[0] = s.astype(BF16)

        @pl.when(pl.program_id(1) == 2 * x + y)
        def _():
            own_ref[...] = s

    chunk = (1, rb, cols)
    return pl.pallas_call(
        body, name=name,
        grid_spec=pltpu.PrefetchScalarGridSpec(
            num_scalar_prefetch=1, grid=(rows // rb, N_CHIP),
            in_specs=[pl.BlockSpec(chunk, lambda i, q, core_ref: (2 * q + core_ref[0], i, 0)),
                      pl.BlockSpec(chunk, lambda i, q, core_ref: (q, i, 0))],
            out_specs=[pl.BlockSpec(chunk, lambda i, q, core_ref: (q, i, 0)),
                       pl.BlockSpec((rb, cols), lambda i, q, core_ref: (i, 0))]),
        out_shape=[jax.ShapeDtypeStruct((N_CHIP, rows, cols), BF16), jax.ShapeDtypeStruct((rows, cols), F32)],
        compiler_params=_params(("arbitrary", "arbitrary")),
    )(core, g, r1)


def sum_devices(a, name):
    def body(a_ref, o_ref):
        acc = a_ref[0]
        for d in range(1, N_DEV):
            acc = acc + a_ref[d]
        o_ref[...] = acc

    return pl.pallas_call(body, name=name, out_shape=jax.ShapeDtypeStruct(a.shape[1:], F32))(a)


def _adam_update(w, g, m, v):
    nm = ADAM_B1 * m + (1.0 - ADAM_B1) * g
    nv = ADAM_B2 * v + (1.0 - ADAM_B2) * (g * g)
    m_hat = nm / (1.0 - ADAM_B1 ** ADAM_STEP)
    v_hat = nv / (1.0 - ADAM_B2 ** ADAM_STEP)
    return -ADAM_LR * (m_hat / (jnp.sqrt(v_hat) + ADAM_EPS) + ADAM_WD * w), nm, nv


def adamw(w, g, m, v, name, others=None):
    rows, cols = w.shape
    rb = rows
    for cand in range(min(rows, 512), 7, -8):
        if rows % cand == 0 and cand % 8 == 0:
            rb = cand
            break

    def body(*refs):
        if others is None:
            w_ref, g_ref, m_ref, v_ref, d_ref, nm_ref, nv_ref = refs
            gg = g_ref[...]
        else:
            w_ref, g_ref, m_ref, v_ref, r2_ref, go_ref, d_ref, nm_ref, nv_ref = refs
            gg = g_ref[...]
            for k in range(3):
                gg = gg + r2_ref[k].astype(F32)
            go_ref[...] = gg
        d_ref[...], nm_ref[...], nv_ref[...] = _adam_update(w_ref[...], gg, m_ref[...], v_ref[...])

    spec = pl.BlockSpec((rb, cols), lambda i: (i, 0))
    out = jax.ShapeDtypeStruct((rows, cols), F32)
    in_specs, args = [spec] * 4, [w, g, m, v]
    if others is not None:
        in_specs.append(pl.BlockSpec((3, rb, cols), lambda i: (0, i, 0)))
        args.append(others)
    n_out = 3 if others is None else 4
    res = pl.pallas_call(
        body, name=name, grid=(rows // rb,), in_specs=in_specs, out_specs=[spec] * n_out,
        out_shape=[out] * n_out, compiler_params=_params(("parallel",)),
    )(*args)
    return (g, *res) if others is None else tuple(res)


def adamw_small(items, name):
    n = len(items)

    def body(*refs):
        ins, outs = refs[:4 * n], refs[4 * n:]
        for i in range(n):
            w_ref, g_ref, m_ref, v_ref = ins[4 * i:4 * i + 4]
            d_ref, nm_ref, nv_ref = outs[3 * i:3 * i + 3]
            d_ref[...], nm_ref[...], nv_ref[...] = _adam_update(w_ref[...], g_ref[...], m_ref[...], v_ref[...])

    res = pl.pallas_call(
        body, name=name,
        out_shape=[jax.ShapeDtypeStruct(w.shape, F32) for w, _, _, _ in items for _ in range(3)],
    )(*[t for item in items for t in item])
    return [tuple(res[3 * i:3 * i + 3]) for i in range(n)]


def ada_fwd(c_all, w_cols, b_cols, name):
    def body(c_ref, w_ref, b_ref, cond_ref, mod_ref):
        cc = c_ref[...]
        cond = (cc * _sigmoid(cc)).astype(BF16)
        cond_ref[...] = cond
        mod_ref[...] = _dot(cond, w_ref[...].astype(BF16)) + b_ref[...]

    n, cols = c_all.shape[0], w_cols.shape[1]
    return pl.pallas_call(
        body, name=name,
        out_shape=[jax.ShapeDtypeStruct(c_all.shape, BF16), jax.ShapeDtypeStruct((n, cols), F32)],
        compiler_params=_params(),
    )(c_all, w_cols, b_cols)


def ada_bwd(cond_all, dmod_cols, name):
    def body(c_ref, d_ref, gw_ref, gb_ref):
        d = d_ref[...]
        gw_ref[...] = _dot_tn(c_ref[...], d.astype(BF16))
        gb_ref[...] = jnp.sum(d, axis=0, keepdims=True)

    dm, cols = cond_all.shape[1], dmod_cols.shape[1]
    return pl.pallas_call(
        body, name=name,
        out_shape=[jax.ShapeDtypeStruct((dm, cols), F32), jax.ShapeDtypeStruct((1, cols), F32)],
        compiler_params=_params(),
    )(cond_all, dmod_cols)


MXU_COLS = 256
FFN_CHUNK = 4 * MXU_COLS


def _hidden_chunks(ff):
    assert ff % MXU_COLS == 0
    return [(at, min(FFN_CHUNK, ff - at)) for at in range(0, ff, FFN_CHUNK)]


def _mod_spec(tiles_per_seq, dm):
    return pl.BlockSpec((1, 1, dm), lambda i: (i // tiles_per_seq, 0, 0))


def ffn_loss(x, sh, sc, gt, wgu, wd, ln_g, ln_b, target, seq, name):
    tokens, dm = x.shape
    ff = wgu.shape[1]
    chunks = _hidden_chunks(ff)
    tm = min(TOKEN_TILE, seq)
    tiles_per_seq = seq // tm

    def body(x_ref, sh_ref, sc_ref, gt_ref, wgu_ref, wd_ref, lg_ref, lb_ref, t_ref,
             dr_ref, df_ref, gu_ref, a_ref, h_ref, loss_ref, dln_ref, dgt_ref):
        i = pl.program_id(0)
        xx = x_ref[...]
        h = (xx * (1.0 + sc_ref[0]) + sh_ref[0]).astype(BF16)
        h_ref[...] = h
        acc = jnp.zeros((tm, dm), F32)
        for at, wdt in chunks:
            gk = _dot_nt(h, wgu_ref[0, at:at + wdt, :])
            uk = _dot_nt(h, wgu_ref[1, at:at + wdt, :])
            gu_ref[0, :, at:at + wdt] = gk.astype(BF16)
            gu_ref[1, :, at:at + wdt] = uk.astype(BF16)
            a = (gk * _sigmoid(gk) * uk).astype(BF16)
            a_ref[:, at:at + wdt] = a
            acc = acc + _dot(a, wd_ref[at:at + wdt, :])
        half_gate = 0.5 * (1.0 + gt_ref[0])
        xhat, rstd = _ln_stats(DN_ALPHA * xx + half_gate * acc)
        err = xhat * lg_ref[...] + lb_ref[...] - t_ref[...]
        dr, dgain, dbias = _ln_bwd_normalized(err * (1.0 / dm), xhat, rstd, lg_ref[...])
        dr_ref[...] = dr
        df_ref[...] = (half_gate * dr).astype(BF16)

        @pl.when(i == 0)
        def _():
            loss_ref[...] = jnp.zeros_like(loss_ref)
            dln_ref[...] = jnp.zeros_like(dln_ref)

        @pl.when(i % tiles_per_seq == 0)
        def _():
            dgt_ref[...] = jnp.zeros_like(dgt_ref)

        loss_ref[...] += jnp.full((1, 128), (0.5 / dm) * jnp.sum(err * err), F32)
        dln_ref[0:1, :] += dgain
        dln_ref[1:2, :] += dbias
        dgt_ref[0] += jnp.sum(dr * (0.5 * acc), axis=0, keepdims=True)

    tile = pl.BlockSpec((tm, dm), lambda i: (i, 0))
    mod = _mod_spec(tiles_per_seq, dm)
    res, _ = _call(
        body, None, name=name, grid=(tokens // tm,),
        in_specs=[tile, mod, mod, mod, _const_spec(wgu.shape), _const_spec(wd.shape),
                  _const_spec((1, dm)), _const_spec((1, dm)), tile],
        out_specs=[tile, tile, pl.BlockSpec((2, tm, ff), lambda i: (0, i, 0)), pl.BlockSpec((tm, ff), lambda i: (i, 0)),
                   tile, pl.BlockSpec((1, 128), lambda i: (0, 0)), pl.BlockSpec((2, dm), lambda i: (0, 0)), mod],
        out_shape=[jax.ShapeDtypeStruct((tokens, dm), F32), jax.ShapeDtypeStruct((tokens, dm), BF16),
                   jax.ShapeDtypeStruct((2, tokens, ff), BF16), jax.ShapeDtypeStruct((tokens, ff), BF16),
                   jax.ShapeDtypeStruct((tokens, dm), BF16), jax.ShapeDtypeStruct((1, 128), F32),
                   jax.ShapeDtypeStruct((2, dm), F32), jax.ShapeDtypeStruct((tokens // seq, 1, dm), F32)],
        args=(x, sh, sc, gt, wgu, wd, ln_g, ln_b, target))
    return res


def ffn_up(x, sh, sc, wgu, seq, name, job=None):
    tokens, dm = x.shape
    ff = wgu.shape[1]
    chunks = _hidden_chunks(ff)
    tm = min(FFN_FWD_TILE, seq)

    def body(x_ref, sh_ref, sc_ref, wgu_ref, gu_ref, a_ref, h_ref):
        h = (x_ref[...] * (1.0 + sc_ref[0]) + sh_ref[0]).astype(BF16)
        h_ref[...] = h
        for at, wdt in chunks:
            gk = _dot_nt(h, wgu_ref[0, at:at + wdt, :])
            uk = _dot_nt(h, wgu_ref[1, at:at + wdt, :])
            gu_ref[0, :, at:at + wdt] = gk.astype(BF16)
            gu_ref[1, :, at:at + wdt] = uk.astype(BF16)
            a_ref[:, at:at + wdt] = (gk * _sigmoid(gk) * uk).astype(BF16)

    tile = pl.BlockSpec((tm, dm), lambda i: (i, 0))
    mod = _mod_spec(seq // tm, dm)
    return _call(
        body, job, name=name, grid=(tokens // tm,),
        in_specs=[tile, mod, mod, _const_spec(wgu.shape)],
        out_specs=[pl.BlockSpec((2, tm, ff), lambda i: (0, i, 0)), pl.BlockSpec((tm, ff), lambda i: (i, 0)), tile],
        out_shape=[jax.ShapeDtypeStruct((2, tokens, ff), BF16), jax.ShapeDtypeStruct((tokens, ff), BF16),
                   jax.ShapeDtypeStruct((tokens, dm), BF16)],
        args=(x, sh, sc, wgu))


def ffn_down(x, a, gt, wd, ln_g, ln_b, seq, name, job=None):
    tokens, dm = x.shape
    ff = wd.shape[0]
    chunks = _hidden_chunks(ff)
    tm = min(FFN_FWD_TILE, seq)

    def body(x_ref, a_ref, gt_ref, wd_ref, lg_ref, lb_ref, xo_ref, r_ref, f_ref):
        acc = jnp.zeros((tm, dm), F32)
        for at, wdt in chunks:
            acc = acc + _dot(a_ref[:, at:at + wdt], wd_ref[at:at + wdt, :])
        f_ref[...] = acc.astype(BF16)
        r = DN_ALPHA * x_ref[...] + (0.5 * (1.0 + gt_ref[0])) * acc
        r_ref[...] = r
        xhat, _ = _ln_stats(r)
        xo_ref[...] = xhat * lg_ref[...] + lb_ref[...]

    tile = pl.BlockSpec((tm, dm), lambda i: (i, 0))
    return _call(
        body, job, name=name, grid=(tokens // tm,),
        in_specs=[tile, pl.BlockSpec((tm, ff), lambda i: (i, 0)), _mod_spec(seq // tm, dm), _const_spec(wd.shape),
                  _const_spec((1, dm)), _const_spec((1, dm))],
        out_specs=[tile, tile, tile],
        out_shape=[jax.ShapeDtypeStruct((tokens, dm), F32), jax.ShapeDtypeStruct((tokens, dm), F32),
                   jax.ShapeDtypeStruct((tokens, dm), BF16)],
        args=(x, a, gt, wd, ln_g, ln_b))


def ffn_bwd(dr, df, x, gu, sc, wgu, wd, seq, name, job=None):
    tokens, dm = x.shape
    ff = wgu.shape[1]
    chunks = _hidden_chunks(ff)
    tm = min(FFN_BWD_TILE, seq)
    tiles_per_seq = seq // tm
    nseq = tokens // seq

    def body(dr_ref, df_ref, x_ref, gu_ref, sc_ref, wgu_ref, wd_ref, dx_ref, dgu_ref, dmod_ref):
        @pl.when(pl.program_id(0) % tiles_per_seq == 0)
        def _():
            dmod_ref[...] = jnp.zeros_like(dmod_ref)

        df = df_ref[...]
        dh = jnp.zeros((tm, dm), F32)
        for at, wdt in chunks:
            cols = slice(at, at + wdt)
            da = _dot_nt(df, wd_ref[cols, :])
            gk = gu_ref[0, :, cols].astype(F32)
            uk = gu_ref[1, :, cols].astype(F32)
            sg = _sigmoid(gk)
            sil = gk * sg
            du = (da * sil).astype(BF16)
            dg = (da * uk * (sg * (1.0 + gk * (1.0 - sg)))).astype(BF16)
            dgu_ref[0, :, cols] = dg
            dgu_ref[1, :, cols] = du
            dh = dh + _dot(dg, wgu_ref[0, cols, :]) + _dot(du, wgu_ref[1, cols, :])
        dx_ref[...] = DN_ALPHA * dr_ref[...] + dh * (1.0 + sc_ref[0])
        dmod_ref[0, 0:1, :] += jnp.sum(dh, axis=0, keepdims=True)
        dmod_ref[0, 1:2, :] += jnp.sum(dh * x_ref[...], axis=0, keepdims=True)

    tile = pl.BlockSpec((tm, dm), lambda i: (i, 0))
    gu_spec = pl.BlockSpec((2, tm, ff), lambda i: (0, i, 0))
    return _call(
        body, job, name=name, grid=(tokens // tm,),
        in_specs=[tile, tile, tile, gu_spec, _mod_spec(tiles_per_seq, dm), _const_spec(wgu.shape), _const_spec(wd.shape)],
        out_specs=[tile, gu_spec, pl.BlockSpec((1, 2, dm), lambda i: (i // tiles_per_seq, 0, 0))],
        out_shape=[jax.ShapeDtypeStruct((tokens, dm), F32), jax.ShapeDtypeStruct((2, tokens, ff), BF16),
                   jax.ShapeDtypeStruct((nseq, 2, dm), F32)],
        args=(dr, df, x, gu, sc, wgu, wd), vmem=FFN_BWD_VMEM)


def tn_matmul(a, b, name, job=None, b_cols=None, a_width=None):
    na, tokens, k_all = a.shape
    kk = k_all if a_width is None else a_width
    nka = k_all // kk
    assert nka * kk == k_all
    nb, _, cc = b.shape
    col = 0
    if b_cols is not None:
        col, cc = b_cols
    tt = tokens
    while 4 * tt * (kk + cc) + 8 * kk * cc > TN_VMEM_BUDGET and tt % 2 == 0 and tt > 256:
        tt //= 2
    steps = tokens // tt

    def body(a_ref, b_ref, o_ref, *acc):
        if steps == 1:
            o_ref[0, 0, 0] = _dot_tn(a_ref[0], b_ref[0]).astype(BF16)
            return
        acc_ref, = acc
        t = pl.program_id(3)

        @pl.when(t == 0)
        def _():
            acc_ref[...] = jnp.zeros_like(acc_ref)

        acc_ref[...] += _dot_tn(a_ref[0], b_ref[0])

        @pl.when(t == steps - 1)
        def _():
            o_ref[0, 0, 0] = acc_ref[...].astype(BF16)

    return _call(
        body, job, name=name, grid=(na, nka, nb, steps),
        in_specs=[pl.BlockSpec((1, tt, kk), lambda i, s, j, t: (i, t, s)),
                  pl.BlockSpec((1, tt, cc), lambda i, s, j, t: (j, t, col))],
        out_specs=[pl.BlockSpec((1, 1, 1, kk, cc), lambda i, s, j, t: (i, s, j, 0, 0))],
        out_shape=[jax.ShapeDtypeStruct((na, nka, nb, kk, cc), BF16)],
        scratch_shapes=[] if steps == 1 else [pltpu.VMEM((kk, cc), F32)], args=(a, b))


def proj_fwd(x1, sh, sc, w_in, seq, name, job=None):
    tokens, dm = x1.shape
    tm = min(MIX_TILE, seq)
    tiles_per_seq = seq // tm
    widths = [N_Q_HEADS * HEAD_DIM, N_KV_HEADS * HEAD_DIM, N_KV_HEADS * HEAD_DIM, 512, 512, 512]
    assert sum(widths) == w_in.shape[0]

    def body(x_ref, sh_ref, sc_ref, w_ref, *outs):
        h = (x_ref[...] * (1.0 + sc_ref[0]) + sh_ref[0]).astype(BF16)
        proj = _dot_nt(h, w_ref[...])
        at = 0
        for o_ref, wdt in zip(outs, widths):
            o_ref[...] = proj[:, at:at + wdt].astype(o_ref.dtype)
            at += wdt

    tile = pl.BlockSpec((tm, dm), lambda i: (i, 0))
    mod = _mod_spec(tiles_per_seq, dm)
    return _call(
        body, job, name=name, grid=(tokens // tm,),
        in_specs=[tile, mod, mod, _const_spec(w_in.shape)],
        out_specs=[pl.BlockSpec((tm, wdt), lambda i: (i, 0)) for wdt in widths],
        out_shape=[jax.ShapeDtypeStruct((tokens, wdt), F32 if i < 3 else BF16) for i, wdt in enumerate(widths)],
        args=(x1, sh, sc, w_in))


LANES = 2 * HEAD_DIM


def _head_lane(shape):
    return lax.broadcasted_iota(jnp.int32, shape, 1) % HEAD_DIM


def _lane_half(shape):
    return lax.broadcasted_iota(jnp.int32, shape, 1) // HEAD_DIM


def _swap_rot(v):
    lane = _head_lane(v.shape)
    half = ROT_DIM // 2
    return jnp.where(lane < half, pltpu.roll(v, LANES - half, 1),
                     jnp.where(lane < ROT_DIM, pltpu.roll(v, half, 1), 0.0))


def _rope(v, cos_t, sin_t):
    return v * cos_t + _swap_rot(v) * sin_t


def _unrope(dv, cos_t, sin_t):
    return dv * cos_t + _swap_rot(dv * sin_t)


def _both_halves(t, g):
    return jnp.where(_lane_half(t.shape) == g, t, pltpu.roll(t, HEAD_DIM, 1))


def _fold_halves(t, g):
    return jnp.where(_lane_half(t.shape) == g, t + pltpu.roll(t, HEAD_DIM, 1), 0.0)


def _stack_heads(blocks):
    rows = []
    for blk in blocks:
        half = _lane_half(blk.shape)
        rows += [jnp.where(half == 0, blk, 0.0), jnp.where(half == 1, blk, 0.0)]
    return jnp.concatenate(rows, axis=0)


def _unstack_heads(t, j):
    lo = t[(2 * j) * ATTN_BLOCK:(2 * j + 1) * ATTN_BLOCK]
    hi = t[(2 * j + 1) * ATTN_BLOCK:(2 * j + 2) * ATTN_BLOCK]
    return jnp.where(_lane_half(lo.shape) == 0, lo, hi)


def _band_mask(q0, w0):
    rows, cols = GQA_GROUP * ATTN_BLOCK, 2 * ATTN_BLOCK
    qi = lax.broadcasted_iota(jnp.int32, (rows, cols), 0) % ATTN_BLOCK + q0
    ki = lax.broadcasted_iota(jnp.int32, (rows, cols), 1) + w0
    diff = qi - ki
    return (diff >= 0) & (diff < ATTN_BLOCK)


def _attn_specs(seq):
    q_spec = pl.BlockSpec((seq, GQA_GROUP * HEAD_DIM), lambda b, g: (b, g))
    kv_spec = pl.BlockSpec((seq, LANES), lambda b, g: (b, 0))
    sink_spec = pl.BlockSpec((1, GQA_GROUP * ATTN_BLOCK, 1), lambda b, g: (g, 0, 0))
    return q_spec, kv_spec, sink_spec


def _block_starts(n):
    q0 = pl.multiple_of(n * ATTN_BLOCK, ATTN_BLOCK)
    w0 = pl.multiple_of(jnp.maximum(n - 1, 0) * ATTN_BLOCK, ATTN_BLOCK)
    return q0, w0


def _stacked_queries(ref, rows):
    return _stack_heads([ref[rows, j * LANES:(j + 1) * LANES] for j in range(2)]).astype(BF16)


def _sink_columns(sinks):
    return jnp.repeat(sinks.reshape(N_KV_HEADS, GQA_GROUP), ATTN_BLOCK, axis=1)[:, :, None]


def _probs_spec(nblk):
    return pl.BlockSpec((1, 1, nblk, GQA_GROUP * ATTN_BLOCK, 2 * ATTN_BLOCK), lambda b, g: (b, g, 0, 0, 0))


def _sink_probs_spec():
    return pl.BlockSpec((1, 1, GQA_GROUP * ATTN_BLOCK, LANES), lambda b, g: (b, g, 0, 0))


def attn_fwd(q, k, v, cos_t, sin_t, sinks, seq, name, job=None):
    tokens = q.shape[0]
    nblk = seq // ATTN_BLOCK
    assert nblk >= 2
    scale = HEAD_DIM ** -0.5

    nseq = tokens // seq
    rows_stacked = GQA_GROUP * ATTN_BLOCK
    assert nblk <= LANES

    def body(q_ref, k_ref, v_ref, cos_ref, sin_ref, sink_ref, o_ref, qr_ref, p_ref, ps_ref, kd_ref, vd_ref):
        g = pl.program_id(1)
        kd_ref[...] = _both_halves(_rope(k_ref[...].astype(F32), cos_ref[...], sin_ref[...]), g).astype(BF16)
        vd_ref[...] = _both_halves(v_ref[...].astype(F32), g).astype(BF16)
        sink = sink_ref[0]
        lane = lax.broadcasted_iota(jnp.int32, (rows_stacked, LANES), 1)

        ps_ref[...] = jnp.zeros_like(ps_ref)

        def block(n, carry):
            q0, w0 = _block_starts(n)
            rows, win = pl.ds(q0, ATTN_BLOCK), pl.ds(w0, 2 * ATTN_BLOCK)
            blocks = []
            for j in range(2):
                qr = _rope(q_ref[rows, j * LANES:(j + 1) * LANES].astype(F32), cos_ref[rows, :], sin_ref[rows, :]).astype(BF16)
                qr_ref[rows, j * LANES:(j + 1) * LANES] = qr
                blocks.append(qr)
            qs = _stack_heads(blocks)
            s = _dot_nt(qs, kd_ref[win, :]) * scale
            s = jnp.where(_band_mask(q0, w0), s, NEG_BIG)
            m = jnp.maximum(jnp.max(s, axis=-1, keepdims=True), sink)
            p = jnp.exp(s - m)
            e_sink = jnp.exp(sink - m)
            inv = pl.reciprocal(jnp.sum(p, axis=-1, keepdims=True) + e_sink, approx=True)
            pn = (p * inv).astype(BF16)
            p_ref[0, 0, n] = pn
            out = _dot(pn, vd_ref[win, :])
            for j in range(2):
                o_ref[rows, j * LANES:(j + 1) * LANES] = _unstack_heads(out, j).astype(o_ref.dtype)
            ps_ref[0, 0] = jnp.where(lane == n, e_sink * inv, ps_ref[0, 0])
            return carry

        lax.fori_loop(0, nblk, block, 0, unroll=2)

    q_spec, kv_spec, sink_spec = _attn_specs(seq)
    return _call(
        body, job, name=name, grid=(nseq, N_KV_HEADS),
        in_specs=[q_spec, kv_spec, kv_spec, kv_spec, kv_spec, sink_spec],
        out_specs=[q_spec, q_spec, _probs_spec(nblk), _sink_probs_spec()],
        out_shape=[jax.ShapeDtypeStruct(q.shape, BF16), jax.ShapeDtypeStruct(q.shape, BF16),
                   jax.ShapeDtypeStruct((nseq, N_KV_HEADS, nblk, rows_stacked, 2 * ATTN_BLOCK), BF16),
                   jax.ShapeDtypeStruct((nseq, N_KV_HEADS, rows_stacked, LANES), F32)],
        scratch_shapes=[pltpu.VMEM((seq, LANES), BF16), pltpu.VMEM((seq, LANES), BF16)],
        args=(q, k, v, cos_t, sin_t, _sink_columns(sinks)))


def attn_bwd(qr, k, v, do, probs, sink_probs, cos_t, sin_t, seq, name, job=None):
    tokens = qr.shape[0]
    nseq = tokens // seq
    nblk = seq // ATTN_BLOCK
    assert nblk >= 2
    rows_stacked = GQA_GROUP * ATTN_BLOCK
    scale = HEAD_DIM ** -0.5

    def body(q_ref, k_ref, v_ref, do_ref, p_ref, ps_ref, cos_ref, sin_ref, dq_ref, dk_ref, dv_ref, ds_ref,
             kd_ref, vd_ref, dkd_ref, dvd_ref, acc_ref):
        g = pl.program_id(1)
        kd_ref[...] = _both_halves(_rope(k_ref[...].astype(F32), cos_ref[...], sin_ref[...]), g).astype(BF16)
        vd_ref[...] = _both_halves(v_ref[...].astype(F32), g).astype(BF16)
        dkd_ref[...] = jnp.zeros_like(dkd_ref)
        dvd_ref[...] = jnp.zeros_like(dvd_ref)
        acc_ref[...] = jnp.zeros_like(acc_ref)
        lane = lax.broadcasted_iota(jnp.int32, (rows_stacked, LANES), 1)

        def block(n, carry):
            q0, w0 = _block_starts(n)
            rows, win = pl.ds(q0, ATTN_BLOCK), pl.ds(w0, 2 * ATTN_BLOCK)
            qs = _stacked_queries(q_ref, rows)
            dos = _stacked_queries(do_ref, rows)
            kw, vw = kd_ref[win, :], vd_ref[win, :]
            pn16 = p_ref[0, 0, n]
            pn = pn16.astype(F32)
            dvd_ref[win, :] += _dot_tn(pn16, dos)
            dp = _dot_nt(dos, vw)
            delta = jnp.sum(dp * pn, axis=-1, keepdims=True)
            ds = (pn * (dp - delta)).astype(BF16)
            dqs = _dot(ds, kw) * scale
            dkd_ref[win, :] += _dot_tn(ds, qs) * scale
            cos_b, sin_b = cos_ref[rows, :], sin_ref[rows, :]
            for j in range(2):
                dq_ref[rows, j * LANES:(j + 1) * LANES] = _unrope(_unstack_heads(dqs, j), cos_b, sin_b).astype(BF16)
            acc_ref[...] += jnp.where(lane == n, ps_ref[0, 0] * delta, 0.0)
            return carry

        lax.fori_loop(0, nblk // 2, lambda i, carry: block(2 * i + 1, block(2 * i, carry)), 0)
        ds_ref[0, 0] = -jnp.sum(acc_ref[...], axis=-1, keepdims=True)
        dk_g = _unrope(_fold_halves(dkd_ref[...], g), cos_ref[...], sin_ref[...])
        dv_g = _fold_halves(dvd_ref[...], g)

        @pl.when(g == 0)
        def _():
            dk_ref[...] = dk_g
            dv_ref[...] = dv_g

        @pl.when(g != 0)
        def _():
            dk_ref[...] += dk_g
            dv_ref[...] += dv_g

    q_spec, kv_spec, _ = _attn_specs(seq)
    return _call(
        body, job, name=name, grid=(nseq, N_KV_HEADS),
        in_specs=[q_spec, kv_spec, kv_spec, q_spec, _probs_spec(nblk), _sink_probs_spec(), kv_spec, kv_spec],
        out_specs=[q_spec, kv_spec, kv_spec, pl.BlockSpec((1, 1, rows_stacked, 1), lambda b, g: (b, g, 0, 0))],
        out_shape=[jax.ShapeDtypeStruct(qr.shape, BF16), jax.ShapeDtypeStruct(k.shape, F32),
                   jax.ShapeDtypeStruct(k.shape, F32), jax.ShapeDtypeStruct((nseq, N_KV_HEADS, rows_stacked, 1), F32)],
        scratch_shapes=[pltpu.VMEM((seq, LANES), BF16), pltpu.VMEM((seq, LANES), BF16),
                        pltpu.VMEM((seq, LANES), F32), pltpu.VMEM((seq, LANES), F32),
                        pltpu.VMEM((rows_stacked, LANES), F32)],
        args=(qr, k, v, do, probs, sink_probs, cos_t, sin_t))


CONV_COLS = 128


def _shift_down(z, by):
    t = lax.broadcasted_iota(jnp.int32, z.shape, 0)
    return jnp.where(t >= by, pltpu.roll(z, by, 0), 0.0)


def _shift_up(z, by):
    n = z.shape[0]
    t = lax.broadcasted_iota(jnp.int32, z.shape, 0)
    return jnp.where(t < n - by, pltpu.roll(z, n - by, 0), 0.0)


def conv_fwd(u, bg, cg, conv_w, seq, name):
    tokens, width = u.shape

    def body(u_ref, bg_ref, cg_ref, w_ref, o_ref):
        z = cg_ref[...].astype(F32) * u_ref[...].astype(F32)
        yy = w_ref[2:3, :] * z + w_ref[1:2, :] * _shift_down(z, 1) + w_ref[0:1, :] * _shift_down(z, 2)
        o_ref[...] = (bg_ref[...].astype(F32) * yy).astype(BF16)

    col = pl.BlockSpec((seq, CONV_COLS), lambda j, b: (b, j))
    return pl.pallas_call(
        body, name=name, grid=(width // CONV_COLS, tokens // seq),
        in_specs=[col, col, col, pl.BlockSpec((CONV_TAPS, CONV_COLS), lambda j, b: (0, j))],
        out_specs=col, out_shape=jax.ShapeDtypeStruct((tokens, width), BF16),
        compiler_params=_params(("parallel", "parallel")),
    )(u, bg, cg, conv_w)


def conv_bwd(dout, u, bg, cg, conv_w, seq, name):
    tokens, width = u.shape

    def body(do_ref, u_ref, bg_ref, cg_ref, w_ref, du_ref, dbg_ref, dcg_ref, dw_ref):
        uu, cg_v, do = u_ref[...].astype(F32), cg_ref[...].astype(F32), do_ref[...].astype(F32)
        z = cg_v * uu
        z1, z2 = _shift_down(z, 1), _shift_down(z, 2)
        yy = w_ref[2:3, :] * z + w_ref[1:2, :] * z1 + w_ref[0:1, :] * z2
        dbg_ref[...] = (do * yy).astype(BF16)
        dyy = do * bg_ref[...].astype(F32)
        dz = w_ref[2:3, :] * dyy + w_ref[1:2, :] * _shift_up(dyy, 1) + w_ref[0:1, :] * _shift_up(dyy, 2)
        du_ref[...] = (dz * cg_v).astype(BF16)
        dcg_ref[...] = (dz * uu).astype(BF16)

        @pl.when(pl.program_id(1) == 0)
        def _():
            dw_ref[...] = jnp.zeros_like(dw_ref)

        dw_ref[0:1, :] += jnp.sum(dyy * z2, axis=0, keepdims=True)
        dw_ref[1:2, :] += jnp.sum(dyy * z1, axis=0, keepdims=True)
        dw_ref[2:3, :] += jnp.sum(dyy * z, axis=0, keepdims=True)

    col = pl.BlockSpec((seq, CONV_COLS), lambda j, b: (b, j))
    w_spec = pl.BlockSpec((CONV_TAPS, CONV_COLS), lambda j, b: (0, j))
    act = jax.ShapeDtypeStruct((tokens, width), BF16)
    return pl.pallas_call(
        body, name=name, grid=(width // CONV_COLS, tokens // seq),
        in_specs=[col, col, col, col, w_spec], out_specs=[col, col, col, w_spec],
        out_shape=[act, act, act, jax.ShapeDtypeStruct((CONV_TAPS, width), F32)],
        compiler_params=_params(("parallel", "arbitrary")),
    )(dout, u, bg, cg, conv_w)


def out_fwd(x1, attn, conv, gt, w_out, ln_g, ln_b, seq, name, job=None):
    tokens, dm = x1.shape
    half = attn.shape[1]
    tm = min(MIX_TILE, seq)
    tiles_per_seq = seq // tm

    def body(x_ref, a_ref, c_ref, gt_ref, w_ref, lg_ref, lb_ref, xo_ref, r_ref, mi_ref, mix_ref):
        mixin = jnp.concatenate([a_ref[...], c_ref[...]], axis=1).astype(BF16)
        mi_ref[...] = mixin
        mix = _dot(mixin, w_ref[...])
        mix_ref[...] = mix.astype(BF16)
        r = DN_ALPHA * x_ref[...] + (1.0 + gt_ref[0]) * mix
        r_ref[...] = r
        xhat, _ = _ln_stats(r)
        xo_ref[...] = xhat * lg_ref[...] + lb_ref[...]

    tile = pl.BlockSpec((tm, dm), lambda i: (i, 0))
    htile = pl.BlockSpec((tm, half), lambda i: (i, 0))
    return _call(
        body, job, name=name, grid=(tokens // tm,),
        in_specs=[tile, htile, htile, _mod_spec(tiles_per_seq, dm), _const_spec(w_out.shape),
                  _const_spec((1, dm)), _const_spec((1, dm))],
        out_specs=[tile, tile, tile, tile],
        out_shape=[jax.ShapeDtypeStruct((tokens, dm), F32), jax.ShapeDtypeStruct((tokens, dm), F32),
                   jax.ShapeDtypeStruct((tokens, dm), BF16), jax.ShapeDtypeStruct((tokens, dm), BF16)],
        args=(x1, attn, conv, gt, w_out, ln_g, ln_b))


def out_bwd(dy, r, mix, gt, w_out, ln_g, seq, name, job=None):
    tokens, dm = r.shape
    half = dm // 2
    tm = min(MIX_TILE, seq)
    tiles_per_seq = seq // tm
    nseq = tokens // seq

    def body(dy_ref, r_ref, mix_ref, gt_ref, w_ref, lg_ref, dres_ref, da_ref, dc_ref, dmix_ref, dln_ref, dgt_ref):
        i = pl.program_id(0)
        dr, dgain, dbias = _ln_bwd(dy_ref[...], r_ref[...], lg_ref[...])

        @pl.when(i == 0)
        def _():
            dln_ref[...] = jnp.zeros_like(dln_ref)

        @pl.when(i % tiles_per_seq == 0)
        def _():
            dgt_ref[...] = jnp.zeros_like(dgt_ref)

        dln_ref[0:1, :] += dgain
        dln_ref[1:2, :] += dbias
        dgt_ref[0] += jnp.sum(dr * mix_ref[...].astype(F32), axis=0, keepdims=True)
        dres_ref[...] = DN_ALPHA * dr
        dmix = ((1.0 + gt_ref[0]) * dr).astype(BF16)
        dmix_ref[...] = dmix
        dmixin = _dot_nt(dmix, w_ref[...])
        da_ref[...] = dmixin[:, :half].astype(BF16)
        dc_ref[...] = dmixin[:, half:].astype(BF16)

    tile = pl.BlockSpec((tm, dm), lambda i: (i, 0))
    htile = pl.BlockSpec((tm, half), lambda i: (i, 0))
    return _call(
        body, job, name=name, grid=(tokens // tm,),
        in_specs=[tile, tile, tile, _mod_spec(tiles_per_seq, dm), _const_spec(w_out.shape), _const_spec((1, dm))],
        out_specs=[tile, htile, htile, tile, pl.BlockSpec((2, dm), lambda i: (0, 0)),
                   pl.BlockSpec((1, 1, dm), lambda i: (i // tiles_per_seq, 0, 0))],
        out_shape=[jax.ShapeDtypeStruct((tokens, dm), F32), jax.ShapeDtypeStruct((tokens, half), BF16),
                   jax.ShapeDtypeStruct((tokens, half), BF16), jax.ShapeDtypeStruct((tokens, dm), BF16),
                   jax.ShapeDtypeStruct((2, dm), F32), jax.ShapeDtypeStruct((nseq, 1, dm), F32)],
        args=(dy, r, mix, gt, w_out, ln_g))


def proj_bwd(parts, dres, x1, sh, sc, w_in, r_prev, f_prev, gt_prev, ln_g_prev, seq, name, job=None):
    tokens, dm = x1.shape
    tm = min(MIX_TILE, seq)
    tiles_per_seq = seq // tm
    nseq = tokens // seq
    widths = [p.shape[1] for p in parts]
    total = sum(widths)

    def body(*refs):
        part_refs = refs[:6]
        (dres_ref, x_ref, sh_ref, sc_ref, w_ref, r_ref, f_ref, gt_ref, lg_ref,
         dr_ref, df_ref, dproj_ref, h_ref, dmod_ref, dln_ref, dgt_ref) = refs[6:]
        i = pl.program_id(0)
        dproj = jnp.concatenate([p[...].astype(BF16) for p in part_refs], axis=1)
        dproj_ref[...] = dproj
        dh = _dot(dproj, w_ref[...])
        xx = x_ref[...]
        one_sc = 1.0 + sc_ref[0]
        h_ref[...] = (xx * one_sc + sh_ref[0]).astype(BF16)
        dr, dgain, dbias = _ln_bwd(dres_ref[...] + dh * one_sc, r_ref[...], lg_ref[...])
        dr_ref[...] = dr
        df_ref[...] = ((0.5 * (1.0 + gt_ref[0])) * dr).astype(BF16)

        @pl.when(i == 0)
        def _():
            dln_ref[...] = jnp.zeros_like(dln_ref)

        @pl.when(i % tiles_per_seq == 0)
        def _():
            dmod_ref[...] = jnp.zeros_like(dmod_ref)
            dgt_ref[...] = jnp.zeros_like(dgt_ref)

        dmod_ref[0, 0:1, :] += jnp.sum(dh, axis=0, keepdims=True)
        dmod_ref[0, 1:2, :] += jnp.sum(dh * xx, axis=0, keepdims=True)
        dln_ref[0:1, :] += dgain
        dln_ref[1:2, :] += dbias
        dgt_ref[0] += jnp.sum(dr * (0.5 * f_ref[...].astype(F32)), axis=0, keepdims=True)

    tile = pl.BlockSpec((tm, dm), lambda i: (i, 0))
    mod = _mod_spec(tiles_per_seq, dm)
    return _call(
        body, job, name=name, grid=(tokens // tm,),
        in_specs=[pl.BlockSpec((tm, wdt), lambda i: (i, 0)) for wdt in widths]
        + [tile, tile, mod, mod, _const_spec(w_in.shape), tile, tile, mod, _const_spec((1, dm))],
        out_specs=[tile, tile, pl.BlockSpec((tm, total), lambda i: (i, 0)), tile,
                   pl.BlockSpec((1, 2, dm), lambda i: (i // tiles_per_seq, 0, 0)),
                   pl.BlockSpec((2, dm), lambda i: (0, 0)), mod],
        out_shape=[jax.ShapeDtypeStruct((tokens, dm), F32), jax.ShapeDtypeStruct((tokens, dm), BF16),
                   jax.ShapeDtypeStruct((tokens, total), BF16), jax.ShapeDtypeStruct((tokens, dm), BF16),
                   jax.ShapeDtypeStruct((nseq, 2, dm), F32), jax.ShapeDtypeStruct((2, dm), F32),
                   jax.ShapeDtypeStruct((nseq, 1, dm), F32)],
        args=(*parts, dres, x1, sh, sc, w_in, r_prev, f_prev, gt_prev, ln_g_prev))


def _rope_tables(positions):
    half = ROT_DIM // 2
    inv_freq = jnp.power(jnp.float32(ROPE_THETA), -jnp.arange(0, ROT_DIM, 2, dtype=F32) / ROT_DIM)
    lane = jnp.arange(LANES) % HEAD_DIM
    freq = jnp.where(lane < ROT_DIM, inv_freq[lane % half], 0.0)
    sign = jnp.where(lane < half, -1.0, 1.0).astype(F32)
    ang = positions.astype(F32)[:, None] * freq[None, :]
    return jnp.cos(ang), sign[None, :] * jnp.sin(ang)


def kernel(x, c, positions, w_ada, b_ada, ffn1_w_gate_up, ffn1_w_down, ln1_g, ln1_b, w_in, conv_w, attn_sinks, w_out, ln2_g, ln2_b, ffn2_w_gate_up, ffn2_w_down, ln3_g, ln3_b, loss_target, m_w_ada, m_b_ada, m_ffn1_w_gate_up, m_ffn1_w_down, m_ln1_g, m_ln1_b, m_w_in, m_conv_w, m_attn_sinks, m_w_out, m_ln2_g, m_ln2_b, m_ffn2_w_gate_up, m_ffn2_w_down, m_ln3_g, m_ln3_b, v_w_ada, v_b_ada, v_ffn1_w_gate_up, v_ffn1_w_down, v_ln1_g, v_ln1_b, v_w_in, v_conv_w, v_attn_sinks, v_w_out, v_ln2_g, v_ln2_b, v_ffn2_w_gate_up, v_ffn2_w_down, v_ln3_g, v_ln3_b):
    nseq, seq, dm = x.shape
    tokens = nseq * seq
    dev = 4 * lax.axis_index("x") + 2 * lax.axis_index("y") + lax.axis_index("c")
    core = lax.axis_index("c").astype(jnp.int32).reshape(1)
    ada_cols = w_ada.shape[2]
    ff = ffn1_w_down.shape[1] * N_DEV
    fc = ff // 4
    in_cols = w_in.shape[2]
    conv_cols = conv_w.shape[2]

    def t_bf16(w):
        return w[0].T.astype(BF16)

    c_all, convw_all = all_gather([c, conv_w[0]], "gather_cond")
    c_all = c_all.reshape(N_DEV * nseq, dm)
    convw_full = convw_all.transpose(1, 0, 2).reshape(CONV_TAPS, N_DEV * conv_cols)

    b_cols = lax.dynamic_slice(b_ada, (0, dev * ada_cols), (1, ada_cols))
    cond_all, mod_cols = ada_fwd(c_all, w_ada[0], b_cols, "ada_fwd")
    wgu1, mod_all = all_gather([t_bf16(ffn1_w_gate_up), mod_cols], "gather_ffn1")
    wgu1 = wgu1.reshape(2, ff, dm)
    mod = lax.dynamic_slice(mod_all, (0, dev * nseq, 0), (N_DEV, nseq, ada_cols))
    mod = mod.transpose(1, 0, 2).reshape(nseq, 9, 1, dm)
    sh1, sc1, g1, sh2, sc2, g2, sh3, sc3, g3 = [mod[:, i] for i in range(9)]

    x0 = x.reshape(tokens, dm)
    (gu1, a1, h1), (wd1, wout) = ffn_up(x0, sh1, sc1, wgu1, seq, "ffn1_up",
                                        job=_GatherJob([ffn1_w_down[0].astype(BF16), w_out[0].astype(BF16)]))
    wd1, wout = wd1.reshape(ff, dm), wout.reshape(dm, dm)
    (x1, r1, f1), (win,) = ffn_down(x0, a1, g1, wd1, ln1_g, ln1_b, seq, "ffn1_down", job=_GatherJob([t_bf16(w_in)]))
    win = win.reshape(N_DEV * in_cols, dm)
    (q, k, v, u, bg, cg), wd2_spread = proj_fwd(x1, sh2, sc2, win, seq, "proj_fwd",
                                                job=gather_spread_job([ffn2_w_down[0].astype(BF16)]))
    cos_t, sin_t = _rope_tables(positions.reshape(tokens))
    sinks = attn_sinks[0]
    (attn, q_rot, probs, sink_probs), wgu2_spread = attn_fwd(q, k, v, cos_t, sin_t, sinks, seq, "attn_fwd",
                                                             job=gather_spread_job([t_bf16(ffn2_w_gate_up)]))
    conv = conv_fwd(u, bg, cg, convw_full, seq, "conv_fwd")
    (x2, r2, mixin, mix), (wd2, wgu2) = out_fwd(x1, attn, conv, g2, wout, ln2_g, ln2_b, seq, "out_fwd",
                                                job=gather_forward_job(wd2_spread + wgu2_spread))
    wd2, wgu2 = wd2.reshape(ff, dm), wgu2.reshape(2, ff, dm)
    target = loss_target.reshape(tokens, dm)
    dr3, df3, gu3, a3, h3, loss_part, dln3, dg3 = ffn_loss(x2, sh3, sc3, g3, wgu2, wd2, ln3_g, ln3_b, target, seq, "ffn2_fwd")

    (dx2, dgu3, dmod3), _ = ffn_bwd(dr3, df3, x2, gu3, sc3, wgu2, wd2, seq, "ffn2_bwd")
    pair = 2 * fc
    g_wd2 = tn_matmul(a3[None], df3[None], "ffn2_dwd", a_width=pair)[0][0].reshape(N_DEV, ff // N_DEV, dm)
    g_wgu2 = tn_matmul(dgu3, h3[None], "ffn2_dwgu", a_width=pair)[0][0].reshape(N_DEV, fc, dm)
    (dres2, dattn, dconv, dmix, dln2, dg2), swapped = out_bwd(dx2, r2, mix, g2, wout, ln2_g, seq, "out_bwd",
                                                              job=swap_job([g_wgu2, g_wd2]))
    p_wgu2, own_wgu2 = pair_sum(core, g_wgu2, swapped[0], "pair_wgu2")
    p_wd2, own_wd2 = pair_sum(core, g_wd2, swapped[1], "pair_wd2")
    du, dbg, dcg, dconvw = conv_bwd(dconv, u, bg, cg, convw_full, seq, "conv_bwd")
    (dq, dk, dv, dsink_rows), (far_wd2,) = attn_bwd(
        q_rot, k, v, dattn, probs, sink_probs, cos_t, sin_t, seq, "attn_bwd", job=chip_exchange_job([p_wd2]))
    parts = [dq, dk, dv, du, dbg, dcg]
    (dr1, df1, dproj, h2, dmod2, dln1, dg1), far_top = proj_bwd(
        parts, dres2, x1, sh2, sc2, win, r1, f1, g1, ln1_g, seq, "proj_bwd",
        job=chip_exchange_job([p_wgu2], rows=(0, fc // 2)))
    (dx0, dgu1, dmod1), _ = ffn_bwd(dr1, df1, x0, gu1, sc1, wgu1, wd1, seq, "ffn1_bwd")

    dmod = jnp.concatenate([dmod1, dg1, dmod2, dg2, dmod3, dg3], axis=1).reshape(nseq, 9 * dm)
    half = dm // 2
    jobs = _Jobs([gather_spread_job([dmod]),
                  chip_exchange_job([p_wgu2], rows=(fc // 2, fc // 2), into=far_top)])
    (g_wd1,), res = tn_matmul(a1[None], df1[None], "ffn1_dwd", job=jobs, a_width=pair)
    dmod_spread, (far_wgu2,) = jobs.split(res)
    g_wd1 = g_wd1.reshape(N_DEV, ff // N_DEV, dm)
    jobs = _Jobs([swap_job([g_wd1]), gather_forward_job(dmod_spread)])
    (g_l,), res = tn_matmul(dgu1, h1[None], "ffn1_dwgu_l", job=jobs, b_cols=(0, half), a_width=pair)
    (sw_wd1,), (dmod_all,) = jobs.split(res)
    g_l = g_l.reshape(N_DEV, fc, half)
    p_wd1, own_wd1 = pair_sum(core, g_wd1, sw_wd1, "pair_wd1")
    jobs = _Jobs([chip_exchange_job([p_wd1]), swap_job([g_l])])
    (g_r,), res = tn_matmul(dgu1, h1[None], "ffn1_dwgu_r", job=jobs, b_cols=(1, half), a_width=pair)
    (far_wd1,), (sw_l,) = jobs.split(res)
    g_r = g_r.reshape(N_DEV, fc, half)
    p_l, own_l = pair_sum(core, g_l, sw_l, "pair_wgu1_l")

    dmod_cols = lax.dynamic_slice(dmod_all.reshape(N_DEV * nseq, 9 * dm), (0, dev * ada_cols), (N_DEV * nseq, ada_cols))
    grad_w_ada, gb_cols = ada_bwd(cond_all, dmod_cols, "ada_bwd")
    dsinks = jnp.sum(dsink_rows.reshape(nseq, N_Q_HEADS, ATTN_BLOCK), axis=(0, 2))
    small = jnp.zeros((8, dm), F32)
    small = small.at[0:2].set(dln1).at[2:4].set(dln2).at[4:6].set(dln3)
    small = small.at[6, 0:N_Q_HEADS].set(dsinks).at[7, 0].set(loss_part[0, 0])

    jobs = _Jobs([chip_exchange_job([p_l]), swap_job([g_r]), gather_spread_job([small, dconvw, gb_cols])])
    (g_win,), res = tn_matmul(dproj[None], h2[None], "dwin", job=jobs)
    (far_l,), (sw_r,), small_spread = jobs.split(res)
    g_win = g_win.reshape(N_DEV, in_cols, dm)
    p_r, own_r = pair_sum(core, g_r, sw_r, "pair_wgu1_r")
    jobs = _Jobs([chip_exchange_job([p_r]), swap_job([g_win]), gather_forward_job(small_spread)])
    (g_wout,), res = tn_matmul(mixin[None], dmix[None], "dwout", job=jobs)
    (far_r,), (sw_win,), (small_all, dconvw_all, gb_all) = jobs.split(res)
    g_wout = g_wout.reshape(N_DEV, dm // N_DEV, dm)
    p_win, own_win = pair_sum(core, g_win, sw_win, "pair_win")

    given = dict(w_ada=(w_ada, m_w_ada, v_w_ada), b_ada=(b_ada, m_b_ada, v_b_ada),
                 ffn1_w_gate_up=(ffn1_w_gate_up, m_ffn1_w_gate_up, v_ffn1_w_gate_up),
                 ffn1_w_down=(ffn1_w_down, m_ffn1_w_down, v_ffn1_w_down),
                 ln1_g=(ln1_g, m_ln1_g, v_ln1_g), ln1_b=(ln1_b, m_ln1_b, v_ln1_b),
                 w_in=(w_in, m_w_in, v_w_in), conv_w=(conv_w, m_conv_w, v_conv_w),
                 attn_sinks=(attn_sinks, m_attn_sinks, v_attn_sinks), w_out=(w_out, m_w_out, v_w_out),
                 ln2_g=(ln2_g, m_ln2_g, v_ln2_g), ln2_b=(ln2_b, m_ln2_b, v_ln2_b),
                 ffn2_w_gate_up=(ffn2_w_gate_up, m_ffn2_w_gate_up, v_ffn2_w_gate_up),
                 ffn2_w_down=(ffn2_w_down, m_ffn2_w_down, v_ffn2_w_down),
                 ln3_g=(ln3_g, m_ln3_g, v_ln3_g), ln3_b=(ln3_b, m_ln3_b, v_ln3_b))
    transposed = ("ffn1_w_gate_up", "ffn2_w_gate_up", "w_in")

    def big_adamw(nm, grad, far=None):
        flip = nm in transposed
        w2, m2, v2 = [t[0].T if flip else t[0] for t in given[nm]]
        return [t.T[None] if flip else t[None] for t in adamw(w2, grad, m2, v2, "adamw_" + nm, others=far)]

    jobs = _Jobs([chip_exchange_job([p_win]), swap_job([g_wout])])
    (far_win,), (sw_wout,) = jobs.split(run_job(jobs, "rs_tail_win"))
    p_wout, own_wout = pair_sum(core, g_wout, sw_wout, "pair_wout")
    (far_wout,) = run_job(chip_exchange_job([p_wout]), "rs_tail_wout")

    grads = {
        "ffn1_w_gate_up": jnp.concatenate([own_l, own_r], axis=1), "ffn1_w_down": own_wd1,
        "w_in": own_win, "w_out": own_wout, "ffn2_w_gate_up": own_wgu2, "ffn2_w_down": own_wd2,
    }
    others = {"ffn1_w_gate_up": jnp.concatenate([far_l, far_r], axis=2), "ffn1_w_down": far_wd1,
              "w_in": far_win, "w_out": far_wout, "ffn2_w_gate_up": far_wgu2, "ffn2_w_down": far_wd2}
    results = {"w_ada": big_adamw("w_ada", grad_w_ada)}
    for nm in grads:
        results[nm] = big_adamw(nm, grads[nm], others[nm])

    small_sum = sum_devices(small_all, "sum_small")
    dconvw_sum = sum_devices(dconvw_all, "sum_convw")
    loss = small_sum[7, 0]
    grads["b_ada"] = gb_all.reshape(1, N_DEV * ada_cols)
    grads["conv_w"] = lax.dynamic_slice(dconvw_sum, (0, dev * conv_cols), (CONV_TAPS, conv_cols))
    grads["attn_sinks"] = small_sum[6:7, 0:N_Q_HEADS]
    for i, nm in enumerate(["ln1_g", "ln1_b", "ln2_g", "ln2_b", "ln3_g", "ln3_b"]):
        grads[nm] = small_sum[i:i + 1]

    order = ["w_ada", "b_ada", "ffn1_w_gate_up", "ffn1_w_down", "ln1_g", "ln1_b", "w_in", "conv_w", "attn_sinks",
             "w_out", "ln2_g", "ln2_b", "ffn2_w_gate_up", "ffn2_w_down", "ln3_g", "ln3_b"]
    small_names = [nm for nm in order if nm not in results]
    items = []
    for nm in small_names:
        shape = given[nm][0].shape
        two_d = (shape[-2], shape[-1])
        items.append((given[nm][0].reshape(two_d), grads[nm].reshape(two_d), *[t.reshape(two_d) for t in given[nm][1:]]))
    for nm, res in zip(small_names, adamw_small(items, "adamw_small")):
        shape = given[nm][0].shape
        results[nm] = [grads[nm].reshape(shape)] + [t.reshape(shape) for t in res]
    grad_x = dx0.reshape(nseq, seq, dm)
    return (loss, grad_x, *[results[nm][i] for i in range(4) for nm in order])
```

```python
import functools

import jax
import jax.numpy as jnp
from jax import lax
from jax.experimental import pallas as pl
from jax.experimental.pallas import tpu as pltpu

F32 = jnp.float32
BF16 = jnp.bfloat16
MESH = pl.DeviceIdType.MESH

N_DEV = 8
N_CHIP = 4
HEAD_DIM = 64
N_Q_HEADS = 8
N_KV_HEADS = 2
GQA_GROUP = N_Q_HEADS // N_KV_HEADS
ATTN_BLOCK = 128
ROT_DIM = 16
ROPE_THETA = 500000.0
CONV_TAPS = 3
LN_EPS = 1e-5
DN_ALPHA = 2.0 ** 0.25
ADAM_LR = 0.001
ADAM_B1 = 0.9
ADAM_B2 = 0.999
ADAM_EPS = 1e-08
ADAM_WD = 0.01
ADAM_STEP = 10
NEG_BIG = -1e30

VMEM_LIMIT = 56 * 1024 * 1024
TOKEN_TILE = 256
FFN_FWD_TILE = 512
MIX_TILE = 512
FFN_WIDE_TILE = 512
FFN_WIDE_VMEM = 62 * 1024 * 1024
TN_VMEM_BUDGET = 36 * 1024 * 1024


def _params(semantics=None, vmem=VMEM_LIMIT):
    return pltpu.CompilerParams(dimension_semantics=semantics, vmem_limit_bytes=vmem)


def _dot(a, b):
    return jnp.dot(a, b, preferred_element_type=F32)


def _dot_nt(a, b):
    return lax.dot_general(a, b, (((1,), (1,)), ((), ())), preferred_element_type=F32)


def _dot_tn(a, b):
    return lax.dot_general(a, b, (((0,), (0,)), ((), ())), preferred_element_type=F32)


def _sigmoid(x):
    return pl.reciprocal(1.0 + jnp.exp(-x), approx=True)


def _ln_stats(r):
    mu = jnp.mean(r, axis=-1, keepdims=True)
    d = r - mu
    var = jnp.mean(d * d, axis=-1, keepdims=True)
    rstd = lax.rsqrt(var + LN_EPS)
    return d * rstd, rstd


def _ln_bwd(dy, r, g):
    return _ln_bwd_normalized(dy, *_ln_stats(r), g)


def _ln_bwd_normalized(dy, xhat, rstd, g):
    dxhat = dy * g
    c1 = jnp.mean(dxhat, axis=-1, keepdims=True)
    c2 = jnp.mean(dxhat * xhat, axis=-1, keepdims=True)
    dr = rstd * (dxhat - c1 - xhat * c2)
    return dr, jnp.sum(dy * xhat, axis=0, keepdims=True), jnp.sum(dy, axis=0, keepdims=True)


def _const_spec(shape):
    nd = len(shape)
    return pl.BlockSpec(shape, lambda *_: (0,) * nd, pipeline_mode=pl.Buffered(1))


def all_gather(arrs, name):
    n = len(arrs)

    def body(*refs):
        ins, outs = refs[:n], refs[n:2 * n]
        send_sems, recv_sems, local_sems = refs[2 * n:]
        x, y, c = lax.axis_index("x"), lax.axis_index("y"), lax.axis_index("c")
        me, sibling = (x, y, c), (x, y, 1 - c)
        chips = [(1 - x, y), (x, 1 - y), (1 - x, 1 - y)]

        def slot(i, p):
            return outs[i].at[4 * p[0] + 2 * p[1] + p[2]]

        def copy(i, k, block, to, src=None):
            return pltpu.make_async_remote_copy(
                src_ref=slot(i, block) if src is None else src, dst_ref=slot(i, block),
                send_sem=send_sems.at[i, k], recv_sem=recv_sems.at[i, k],
                device_id=to, device_id_type=MESH)

        mine = [pltpu.make_async_copy(ins[i], slot(i, me), local_sems.at[i]) for i in range(n)]
        for cp in mine:
            cp.start()
        first = []
        for i in range(n):
            first.append(copy(i, 0, me, sibling, src=ins[i]))
            first += [copy(i, 1 + j, me, (*chip, c), src=ins[i]) for j, chip in enumerate(chips)]
        for cp in first:
            cp.start()
        passed = []
        for i in range(n):
            for j, chip in enumerate(chips):
                copy(i, 1 + j, (*chip, c), me).wait_recv()
                cp = copy(i, 4 + j, (*chip, c), sibling)
                cp.start()
                passed.append(cp)
        for i in range(n):
            copy(i, 0, sibling, me).wait_recv()
            for j, chip in enumerate(chips):
                copy(i, 4 + j, (*chip, 1 - c), me).wait_recv()
        for cp in first + passed:
            cp.wait_send()
        for cp in mine:
            cp.wait()

    any_spec = pl.BlockSpec(memory_space=pl.ANY)
    return pl.pallas_call(
        body, name=name,
        out_shape=[jax.ShapeDtypeStruct((N_DEV, *a.shape), a.dtype) for a in arrs],
        in_specs=[any_spec] * n, out_specs=[any_spec] * n,
        scratch_shapes=[pltpu.SemaphoreType.DMA((n, 7)), pltpu.SemaphoreType.DMA((n, 7)),
                        pltpu.SemaphoreType.DMA((n,))],
    )(*arrs)


def _place():
    x, y, c = lax.axis_index("x"), lax.axis_index("y"), lax.axis_index("c")
    return x, y, c, [(1 - x, y), (x, 1 - y), (1 - x, 1 - y)]


def _slot(p):
    return 4 * p[0] + 2 * p[1] + p[2]


class _Job:
    def __init__(self, ins, outs, nsem, copies, aliases=None, local=None):
        self.ins, self.outs, self.nsem, self.copies = list(ins), list(outs), nsem, copies
        self.aliases = aliases or {}
        self.local = local

    def scratch(self):
        s = [pltpu.SemaphoreType.DMA(self.nsem), pltpu.SemaphoreType.DMA(self.nsem)]
        if self.local is not None:
            s.append(pltpu.SemaphoreType.DMA((len(self.ins),)))
        return s

    def start(self, ins, outs, sems):
        if self.local is not None:
            for cp in self.local(ins, outs, sems[2]):
                cp.start()
        for cp in self.copies(ins, outs, sems[0], sems[1])[0]:
            cp.start()

    def finish(self, ins, outs, sems):
        started, awaited = self.copies(ins, outs, sems[0], sems[1])
        for cp in awaited:
            cp.wait_recv()
        for cp in started:
            cp.wait_send()
        if self.local is not None:
            for cp in self.local(ins, outs, sems[2]):
                cp.wait()


class _Jobs:
    def __init__(self, jobs):
        self.jobs = jobs
        self.ins = [a for j in jobs for a in j.ins]
        self.outs = [o for j in jobs for o in j.outs]
        self.two_phase = any(getattr(j, "two_phase", False) for j in jobs)
        self.aliases = {}
        at_in = at_out = 0
        for j in jobs:
            self.aliases.update({at_in + i: at_out + o for i, o in j.aliases.items()})
            at_in, at_out = at_in + len(j.ins), at_out + len(j.outs)

    def scratch(self):
        return [s for j in self.jobs for s in j.scratch()]

    def _each(self, ins, outs, sems):
        at_in = at_out = at_sem = 0
        for j in self.jobs:
            n_in, n_out, n_sem = len(j.ins), len(j.outs), len(j.scratch())
            yield j, ins[at_in:at_in + n_in], outs[at_out:at_out + n_out], sems[at_sem:at_sem + n_sem]
            at_in, at_out, at_sem = at_in + n_in, at_out + n_out, at_sem + n_sem

    def start(self, ins, outs, sems):
        for j, i, o, s in self._each(ins, outs, sems):
            j.start(i, o, s)

    def turn(self, ins, outs, sems):
        for j, i, o, s in self._each(ins, outs, sems):
            if getattr(j, "two_phase", False):
                j.turn(i, o, s)

    def finish(self, ins, outs, sems):
        for j, i, o, s in self._each(ins, outs, sems):
            j.finish(i, o, s)

    def split(self, results):
        at, parts = 0, []
        for j in self.jobs:
            parts.append(results[at:at + len(j.outs)])
            at += len(j.outs)
        return parts


def _remote(src, dst, send, recv, idx, to):
    return pltpu.make_async_remote_copy(src_ref=src, dst_ref=dst, send_sem=send.at[idx], recv_sem=recv.at[idx],
                                        device_id=to, device_id_type=MESH)


def _spread_copies(ins, outs, send, recv, base=0):
    x, y, c, chips = _place()
    me = (x, y, c)
    peers = [(x, y, 1 - c)] + [(*chip, c) for chip in chips]
    started, awaited = [], []
    for i, (src, dst) in enumerate(zip(ins, outs)):
        for k, peer in enumerate(peers):
            started.append(_remote(src, dst.at[_slot(me)], send, recv, (base + i, k), peer))
            awaited.append(_remote(src, dst.at[_slot(peer)], send, recv, (base + i, k), peer))
    return started, awaited


def _forward_copies(ins, outs, send, recv, base=0):
    x, y, c, chips = _place()
    started, awaited = [], []
    for i, buf in enumerate(outs):
        for j, chip in enumerate(chips):
            mine, theirs = buf.at[_slot((*chip, c))], buf.at[_slot((*chip, 1 - c))]
            started.append(_remote(mine, mine, send, recv, (base + i, j), (x, y, 1 - c)))
            awaited.append(_remote(theirs, theirs, send, recv, (base + i, j), (x, y, 1 - c)))
    return started, awaited


def _own_block_copies(ins, outs, sems):
    x, y, c, _ = _place()
    return [pltpu.make_async_copy(src, dst.at[_slot((x, y, c))], sems.at[i])
            for i, (src, dst) in enumerate(zip(ins, outs))]


def gather_spread_job(shards):
    outs = [jax.ShapeDtypeStruct((N_DEV, *a.shape), a.dtype) for a in shards]
    return _Job(shards, outs, (len(shards), 4), _spread_copies, local=_own_block_copies)


def gather_forward_job(fulls):
    outs = [jax.ShapeDtypeStruct(a.shape, a.dtype) for a in fulls]
    return _Job(fulls, outs, (len(fulls), 3), _forward_copies, aliases={i: i for i in range(len(fulls))})


TURN_EIGHTHS = 6


class _GatherJob:
    two_phase = True

    def __init__(self, shards):
        self.ins = list(shards)
        self.outs = [jax.ShapeDtypeStruct((N_DEV, *a.shape), a.dtype) for a in shards]
        self.aliases = {}

    def scratch(self):
        n = len(self.ins)
        return [pltpu.SemaphoreType.DMA((n, 4)), pltpu.SemaphoreType.DMA((n, 4)),
                pltpu.SemaphoreType.DMA((n, 3)), pltpu.SemaphoreType.DMA((n, 3)), pltpu.SemaphoreType.DMA((n,))]

    def start(self, ins, outs, sems):
        for cp in _own_block_copies(ins, outs, sems[4]) + _spread_copies(ins, outs, sems[0], sems[1])[0]:
            cp.start()

    def turn(self, ins, outs, sems):
        for cp in _spread_copies(ins, outs, sems[0], sems[1])[1]:
            cp.wait_recv()
        for cp in _forward_copies(outs, outs, sems[2], sems[3])[0]:
            cp.start()

    def finish(self, ins, outs, sems):
        handed_on, arriving = _forward_copies(outs, outs, sems[2], sems[3])
        for cp in arriving:
            cp.wait_recv()
        for cp in _spread_copies(ins, outs, sems[0], sems[1])[0] + handed_on:
            cp.wait_send()
        for cp in _own_block_copies(ins, outs, sems[4]):
            cp.wait()


def swap_job(gs):
    def copies(ins, outs, send, recv):
        x, y, c, _ = _place()
        started, awaited = [], []
        for i, (g, r1) in enumerate(zip(ins, outs)):
            for q in range(N_CHIP):
                started.append(_remote(g.at[2 * q + (1 - c)], r1.at[q], send, recv, (i, q), (x, y, 1 - c)))
                awaited.append(_remote(g.at[2 * q + c], r1.at[q], send, recv, (i, q), (x, y, 1 - c)))
        return started, awaited

    outs = [jax.ShapeDtypeStruct((N_CHIP, *g.shape[1:]), g.dtype) for g in gs]
    return _Job(gs, outs, (len(gs), N_CHIP), copies)


def chip_exchange_job(ps, rows=None, into=None):
    n = len(ps)

    def copies(ins, outs, send, recv):
        x, y, c, chips = _place()
        started, awaited = [], []
        for i, (p, r2) in enumerate(zip(ins[:n], outs)):
            for k, chip in enumerate(chips):
                src, mine, dst = p.at[2 * chip[0] + chip[1]], p.at[2 * x + y], r2.at[k]
                if rows is not None:
                    src, mine, dst = (t.at[pl.ds(rows[0], rows[1])] for t in (src, mine, dst))
                started.append(_remote(src, dst, send, recv, (i, k), (*chip, c)))
                awaited.append(_remote(mine, dst, send, recv, (i, k), (*chip, c)))
        return started, awaited

    outs = [jax.ShapeDtypeStruct((3, *p.shape[1:]), p.dtype) for p in ps]
    if into is None:
        return _Job(ps, outs, (n, 3), copies)
    return _Job(list(ps) + list(into), outs, (n, 3), copies, aliases={n + i: i for i in range(n)})


def _call(body, job, *, name, grid, in_specs, out_specs, out_shape, args, scratch_shapes=(), vmem=VMEM_LIMIT):
    if job is None:
        res = pl.pallas_call(
            body, name=name, grid=grid, in_specs=in_specs, out_specs=out_specs, out_shape=out_shape,
            scratch_shapes=list(scratch_shapes), compiler_params=_params(("arbitrary",) * len(grid), vmem),
        )(*args)
        return res, []
    n_in, n_out, n_scr = len(in_specs), len(out_specs), len(scratch_shapes)
    j_in, j_out = len(job.ins), len(job.outs)

    def with_copies(*refs):
        at = 0
        ins = refs[at:at + n_in]; at += n_in
        jins = refs[at:at + j_in]; at += j_in
        outs = refs[at:at + n_out]; at += n_out
        jouts = refs[at:at + j_out]; at += j_out
        scr = refs[at:at + n_scr]; at += n_scr
        sems = refs[at:]
        ids = [pl.program_id(d) for d in range(len(grid))]
        first = functools.reduce(jnp.logical_and, [i == 0 for i in ids])
        last = functools.reduce(jnp.logical_and, [i == n - 1 for i, n in zip(ids, grid)])

        @pl.when(first)
        def _():
            job.start(jins, jouts, sems)

        if getattr(job, "two_phase", False):
            steps, at = 1, 0
            for i, n in zip(ids, grid):
                steps, at = steps * n, at * n + i

            @pl.when(at == (TURN_EIGHTHS * steps) // 8)
            def _():
                job.turn(jins, jouts, sems)

        body(*ins, *outs, *scr)

        @pl.when(last)
        def _():
            job.finish(jins, jouts, sems)

    any_spec = pl.BlockSpec(memory_space=pl.ANY)
    res = pl.pallas_call(
        with_copies, name=name, grid=grid,
        in_specs=list(in_specs) + [any_spec] * j_in, out_specs=list(out_specs) + [any_spec] * j_out,
        out_shape=list(out_shape) + list(job.outs),
        input_output_aliases={n_in + i: n_out + o for i, o in job.aliases.items()},
        scratch_shapes=list(scratch_shapes) + job.scratch(),
        compiler_params=_params(("arbitrary",) * len(grid), vmem),
    )(*args, *job.ins)
    return res[:n_out], res[n_out:]


def run_job(job, name):
    def body(*refs):
        j_in, j_out = len(job.ins), len(job.outs)
        ins, outs, sems = refs[:j_in], refs[j_in:j_in + j_out], refs[j_in + j_out:]
        job.start(ins, outs, sems)
        job.finish(ins, outs, sems)

    any_spec = pl.BlockSpec(memory_space=pl.ANY)
    return pl.pallas_call(
        body, name=name, in_specs=[any_spec] * len(job.ins), out_specs=[any_spec] * len(job.outs),
        out_shape=list(job.outs), input_output_aliases=dict(job.aliases), scratch_shapes=job.scratch(),
    )(*job.ins)


def pair_sum(core, g, r1, name):
    _, rows, cols = g.shape
    rb = next(cand for cand in range(min(rows, 512), 0, -16) if rows % cand == 0)

    def body(core_ref, g_ref, r1_ref, p_ref, own_ref):
        del core_ref
        x, y, _, _ = _place()
        s = g_ref[0].astype(F32) + r1_ref[0].astype(F32)
        p_ref[0] = s.astype(BF16)

        @pl.when(pl.program_id(1) == 2 * x + y)
        def _():
            own_ref[...] = s

    chunk = (1, rb, cols)
    return pl.pallas_call(
        body, name=name,
        grid_spec=pltpu.PrefetchScalarGridSpec(
            num_scalar_prefetch=1, grid=(rows // rb, N_CHIP),
            in_specs=[pl.BlockSpec(chunk, lambda i, q, core_ref: (2 * q + core_ref[0], i, 0)),
                      pl.BlockSpec(chunk, lambda i, q, core_ref: (q, i, 0))],
            out_specs=[pl.BlockSpec(chunk, lambda i, q, core_ref: (q, i, 0)),
                       pl.BlockSpec((rb, cols), lambda i, q, core_ref: (i, 0))]),
        out_shape=[jax.ShapeDtypeStruct((N_CHIP, rows, cols), BF16), jax.ShapeDtypeStruct((rows, cols), F32)],
        compiler_params=_params(("arbitrary", "arbitrary")),
    )(core, g, r1)


def sum_devices(a, name):
    def body(a_ref, o_ref):
        acc = a_ref[0]
        for d in range(1, N_DEV):
            acc = acc + a_ref[d]
        o_ref[...] = acc

    return pl.pallas_call(body, name=name, out_shape=jax.ShapeDtypeStruct(a.shape[1:], F32))(a)


def _adam_update(w, g, m, v):
    nm = ADAM_B1 * m + (1.0 - ADAM_B1) * g
    nv = ADAM_B2 * v + (1.0 - ADAM_B2) * (g * g)
    m_hat = nm / (1.0 - ADAM_B1 ** ADAM_STEP)
    v_hat = nv / (1.0 - ADAM_B2 ** ADAM_STEP)
    return -ADAM_LR * (m_hat / (jnp.sqrt(v_hat) + ADAM_EPS) + ADAM_WD * w), nm, nv


def adamw(w, g, m, v, name, others=None):
    rows, cols = w.shape
    rb = rows
    for cand in range(min(rows, 512), 7, -8):
        if rows % cand == 0 and cand % 8 == 0:
            rb = cand
            break

    def body(*refs):
        if others is None:
            w_ref, g_ref, m_ref, v_ref, d_ref, nm_ref, nv_ref = refs
            gg = g_ref[...]
        else:
            w_ref, g_ref, m_ref, v_ref, r2_ref, go_ref, d_ref, nm_ref, nv_ref = refs
            gg = g_ref[...]
            for k in range(3):
                gg = gg + r2_ref[k].astype(F32)
            go_ref[...] = gg
        d_ref[...], nm_ref[...], nv_ref[...] = _adam_update(w_ref[...], gg, m_ref[...], v_ref[...])

    spec = pl.BlockSpec((rb, cols), lambda i: (i, 0))
    out = jax.ShapeDtypeStruct((rows, cols), F32)
    in_specs, args = [spec] * 4, [w, g, m, v]
    if others is not None:
        in_specs.append(pl.BlockSpec((3, rb, cols), lambda i: (0, i, 0)))
        args.append(others)
    n_out = 3 if others is None else 4
    res = pl.pallas_call(
        body, name=name, grid=(rows // rb,), in_specs=in_specs, out_specs=[spec] * n_out,
        out_shape=[out] * n_out, compiler_params=_params(("parallel",)),
    )(*args)
    return (g, *res) if others is None else tuple(res)


def adamw_small(items, name):
    n = len(items)

    def body(*refs):
        ins, outs = refs[:4 * n], refs[4 * n:]
        for i in range(n):
            w_ref, g_ref, m_ref, v_ref = ins[4 * i:4 * i + 4]
            d_ref, nm_ref, nv_ref = outs[3 * i:3 * i + 3]
            d_ref[...], nm_ref[...], nv_ref[...] = _adam_update(w_ref[...], g_ref[...], m_ref[...], v_ref[...])

    res = pl.pallas_call(
        body, name=name,
        out_shape=[jax.ShapeDtypeStruct(w.shape, F32) for w, _, _, _ in items for _ in range(3)],
    )(*[t for item in items for t in item])
    return [tuple(res[3 * i:3 * i + 3]) for i in range(n)]


def ada_fwd(c_all, w_cols, b_cols, name):
    def body(c_ref, w_ref, b_ref, cond_ref, mod_ref):
        cc = c_ref[...]
        cond = (cc * _sigmoid(cc)).astype(BF16)
        cond_ref[...] = cond
        mod_ref[...] = _dot(cond, w_ref[...].astype(BF16)) + b_ref[...]

    n, cols = c_all.shape[0], w_cols.shape[1]
    return pl.pallas_call(
        body, name=name,
        out_shape=[jax.ShapeDtypeStruct(c_all.shape, BF16), jax.ShapeDtypeStruct((n, cols), F32)],
        compiler_params=_params(),
    )(c_all, w_cols, b_cols)


def ada_bwd(cond_all, dmod_cols, name):
    def body(c_ref, d_ref, gw_ref, gb_ref):
        d = d_ref[...]
        gw_ref[...] = _dot_tn(c_ref[...], d.astype(BF16))
        gb_ref[...] = jnp.sum(d, axis=0, keepdims=True)

    dm, cols = cond_all.shape[1], dmod_cols.shape[1]
    return pl.pallas_call(
        body, name=name,
        out_shape=[jax.ShapeDtypeStruct((dm, cols), F32), jax.ShapeDtypeStruct((1, cols), F32)],
        compiler_params=_params(),
    )(cond_all, dmod_cols)


MXU_COLS = 256
FFN_CHUNK = 4 * MXU_COLS


def _hidden_chunks(ff):
    assert ff % MXU_COLS == 0
    return [(at, min(FFN_CHUNK, ff - at)) for at in range(0, ff, FFN_CHUNK)]


def _mod_spec(tiles_per_seq, dm):
    return pl.BlockSpec((1, 1, dm), lambda i: (i // tiles_per_seq, 0, 0))


def ffn_loss(x, sh, sc, gt, wgu, wd, ln_g, ln_b, target, seq, name):
    tokens, dm = x.shape
    ff = wgu.shape[1]
    chunks = _hidden_chunks(ff)
    tm = min(FFN_WIDE_TILE, seq)
    tiles_per_seq = seq // tm

    def body(x_ref, sh_ref, sc_ref, gt_ref, wgu_ref, wd_ref, lg_ref, lb_ref, t_ref,
             dr_ref, df_ref, gu_ref, a_ref, h_ref, loss_ref, dln_ref, dgt_ref):
        i = pl.program_id(0)
        xx = x_ref[...]
        h = (xx * (1.0 + sc_ref[0]) + sh_ref[0]).astype(BF16)
        h_ref[...] = h
        acc = jnp.zeros((tm, dm), F32)
        for at, wdt in chunks:
            gk = _dot_nt(h, wgu_ref[0, at:at + wdt, :])
            uk = _dot_nt(h, wgu_ref[1, at:at + wdt, :])
            gu_ref[0, :, at:at + wdt] = gk.astype(BF16)
            gu_ref[1, :, at:at + wdt] = uk.astype(BF16)
            a = (gk * _sigmoid(gk) * uk).astype(BF16)
            a_ref[:, at:at + wdt] = a
            acc = acc + _dot(a, wd_ref[at:at + wdt, :])
        half_gate = 0.5 * (1.0 + gt_ref[0])
        xhat, rstd = _ln_stats(DN_ALPHA * xx + half_gate * acc)
        err = xhat * lg_ref[...] + lb_ref[...] - t_ref[...]
        dr, dgain, dbias = _ln_bwd_normalized(err * (1.0 / dm), xhat, rstd, lg_ref[...])
        dr_ref[...] = dr
        df_ref[...] = (half_gate * dr).astype(BF16)

        @pl.when(i == 0)
        def _():
            loss_ref[...] = jnp.zeros_like(loss_ref)
            dln_ref[...] = jnp.zeros_like(dln_ref)

        @pl.when(i % tiles_per_seq == 0)
        def _():
            dgt_ref[...] = jnp.zeros_like(dgt_ref)

        loss_ref[...] += jnp.full((1, 128), (0.5 / dm) * jnp.sum(err * err), F32)
        dln_ref[0:1, :] += dgain
        dln_ref[1:2, :] += dbias
        dgt_ref[0] += jnp.sum(dr * (0.5 * acc), axis=0, keepdims=True)

    tile = pl.BlockSpec((tm, dm), lambda i: (i, 0))
    mod = _mod_spec(tiles_per_seq, dm)
    res, _ = _call(
        body, None, name=name, grid=(tokens // tm,),
        in_specs=[tile, mod, mod, mod, _const_spec(wgu.shape), _const_spec(wd.shape),
                  _const_spec((1, dm)), _const_spec((1, dm)), tile],
        out_specs=[tile, tile, pl.BlockSpec((2, tm, ff), lambda i: (0, i, 0)), pl.BlockSpec((tm, ff), lambda i: (i, 0)),
                   tile, pl.BlockSpec((1, 128), lambda i: (0, 0)), pl.BlockSpec((2, dm), lambda i: (0, 0)), mod],
        out_shape=[jax.ShapeDtypeStruct((tokens, dm), F32), jax.ShapeDtypeStruct((tokens, dm), BF16),
                   jax.ShapeDtypeStruct((2, tokens, ff), BF16), jax.ShapeDtypeStruct((tokens, ff), BF16),
                   jax.ShapeDtypeStruct((tokens, dm), BF16), jax.ShapeDtypeStruct((1, 128), F32),
                   jax.ShapeDtypeStruct((2, dm), F32), jax.ShapeDtypeStruct((tokens // seq, 1, dm), F32)],
        args=(x, sh, sc, gt, wgu, wd, ln_g, ln_b, target), vmem=FFN_WIDE_VMEM)
    return res


def ffn_up(x, sh, sc, wgu, seq, name, job=None):
    tokens, dm = x.shape
    ff = wgu.shape[1]
    chunks = _hidden_chunks(ff)
    tm = min(FFN_FWD_TILE, seq)

    def body(x_ref, sh_ref, sc_ref, wgu_ref, gu_ref, a_ref, h_ref):
        h = (x_ref[...] * (1.0 + sc_ref[0]) + sh_ref[0]).astype(BF16)
        h_ref[...] = h
        for at, wdt in chunks:
            gk = _dot_nt(h, wgu_ref[0, at:at + wdt, :])
            uk = _dot_nt(h, wgu_ref[1, at:at + wdt, :])
            gu_ref[0, :, at:at + wdt] = gk.astype(BF16)
            gu_ref[1, :, at:at + wdt] = uk.astype(BF16)
            a_ref[:, at:at + wdt] = (gk * _sigmoid(gk) * uk).astype(BF16)

    tile = pl.BlockSpec((tm, dm), lambda i: (i, 0))
    mod = _mod_spec(seq // tm, dm)
    return _call(
        body, job, name=name, grid=(tokens // tm,),
        in_specs=[tile, mod, mod, _const_spec(wgu.shape)],
        out_specs=[pl.BlockSpec((2, tm, ff), lambda i: (0, i, 0)), pl.BlockSpec((tm, ff), lambda i: (i, 0)), tile],
        out_shape=[jax.ShapeDtypeStruct((2, tokens, ff), BF16), jax.ShapeDtypeStruct((tokens, ff), BF16),
                   jax.ShapeDtypeStruct((tokens, dm), BF16)],
        args=(x, sh, sc, wgu))


def ffn_down(x, a, gt, wd, ln_g, ln_b, seq, name, job=None):
    tokens, dm = x.shape
    ff = wd.shape[0]
    chunks = _hidden_chunks(ff)
    tm = min(FFN_FWD_TILE, seq)

    def body(x_ref, a_ref, gt_ref, wd_ref, lg_ref, lb_ref, xo_ref, r_ref, f_ref):
        acc = jnp.zeros((tm, dm), F32)
        for at, wdt in chunks:
            acc = acc + _dot(a_ref[:, at:at + wdt], wd_ref[at:at + wdt, :])
        f_ref[...] = acc.astype(BF16)
        r = DN_ALPHA * x_ref[...] + (0.5 * (1.0 + gt_ref[0])) * acc
        r_ref[...] = r
        xhat, _ = _ln_stats(r)
        xo_ref[...] = xhat * lg_ref[...] + lb_ref[...]

    tile = pl.BlockSpec((tm, dm), lambda i: (i, 0))
    return _call(
        body, job, name=name, grid=(tokens // tm,),
        in_specs=[tile, pl.BlockSpec((tm, ff), lambda i: (i, 0)), _mod_spec(seq // tm, dm), _const_spec(wd.shape),
                  _const_spec((1, dm)), _const_spec((1, dm))],
        out_specs=[tile, tile, tile],
        out_shape=[jax.ShapeDtypeStruct((tokens, dm), F32), jax.ShapeDtypeStruct((tokens, dm), F32),
                   jax.ShapeDtypeStruct((tokens, dm), BF16)],
        args=(x, a, gt, wd, ln_g, ln_b))


def ffn_bwd(dr, df, x, gu, sc, wgu, wd, seq, name, job=None):
    tokens, dm = x.shape
    ff = wgu.shape[1]
    chunks = _hidden_chunks(ff)
    tm = min(FFN_WIDE_TILE, seq)
    tiles_per_seq = seq // tm
    nseq = tokens // seq

    def body(dr_ref, df_ref, x_ref, gu_ref, sc_ref, wgu_ref, wd_ref, dx_ref, dgu_ref, dmod_ref):
        @pl.when(pl.program_id(0) % tiles_per_seq == 0)
        def _():
            dmod_ref[...] = jnp.zeros_like(dmod_ref)

        df = df_ref[...]
        dh = jnp.zeros((tm, dm), F32)
        for at, wdt in chunks:
            cols = slice(at, at + wdt)
            da = _dot_nt(df, wd_ref[cols, :])
            gk = gu_ref[0, :, cols].astype(F32)
            uk = gu_ref[1, :, cols].astype(F32)
            sg = _sigmoid(gk)
            sil = gk * sg
            du = (da * sil).astype(BF16)
            dg = (da * uk * (sg * (1.0 + gk * (1.0 - sg)))).astype(BF16)
            dgu_ref[0, :, cols] = dg
            dgu_ref[1, :, cols] = du
            dh = dh + _dot(dg, wgu_ref[0, cols, :]) + _dot(du, wgu_ref[1, cols, :])
        dx_ref[...] = DN_ALPHA * dr_ref[...] + dh * (1.0 + sc_ref[0])
        dmod_ref[0, 0:1, :] += jnp.sum(dh, axis=0, keepdims=True)
        dmod_ref[0, 1:2, :] += jnp.sum(dh * x_ref[...], axis=0, keepdims=True)

    tile = pl.BlockSpec((tm, dm), lambda i: (i, 0))
    gu_spec = pl.BlockSpec((2, tm, ff), lambda i: (0, i, 0))
    return _call(
        body, job, name=name, grid=(tokens // tm,),
        in_specs=[tile, tile, tile, gu_spec, _mod_spec(tiles_per_seq, dm), _const_spec(wgu.shape), _const_spec(wd.shape)],
        out_specs=[tile, gu_spec, pl.BlockSpec((1, 2, dm), lambda i: (i // tiles_per_seq, 0, 0))],
        out_shape=[jax.ShapeDtypeStruct((tokens, dm), F32), jax.ShapeDtypeStruct((2, tokens, ff), BF16),
                   jax.ShapeDtypeStruct((nseq, 2, dm), F32)],
        args=(dr, df, x, gu, sc, wgu, wd), vmem=FFN_WIDE_VMEM)


def tn_matmul(a, b, name, job=None, b_cols=None, a_width=None):
    na, tokens, k_all = a.shape
    kk = k_all if a_width is None else a_width
    nka = k_all // kk
    assert nka * kk == k_all
    nb, _, cc = b.shape
    col = 0
    if b_cols is not None:
        col, cc = b_cols
    tt = tokens
    while 4 * tt * (kk + cc) + 8 * kk * cc > TN_VMEM_BUDGET and tt % 2 == 0 and tt > 256:
        tt //= 2
    steps = tokens // tt

    def body(a_ref, b_ref, o_ref, *acc):
        if steps == 1:
            o_ref[0, 0, 0] = _dot_tn(a_ref[0], b_ref[0]).astype(BF16)
            return
        acc_ref, = acc
        t = pl.program_id(3)

        @pl.when(t == 0)
        def _():
            acc_ref[...] = jnp.zeros_like(acc_ref)

        acc_ref[...] += _dot_tn(a_ref[0], b_ref[0])

        @pl.when(t == steps - 1)
        def _():
            o_ref[0, 0, 0] = acc_ref[...].astype(BF16)

    return _call(
        body, job, name=name, grid=(na, nka, nb, steps),
        in_specs=[pl.BlockSpec((1, tt, kk), lambda i, s, j, t: (i, t, s)),
                  pl.BlockSpec((1, tt, cc), lambda i, s, j, t: (j, t, col))],
        out_specs=[pl.BlockSpec((1, 1, 1, kk, cc), lambda i, s, j, t: (i, s, j, 0, 0))],
        out_shape=[jax.ShapeDtypeStruct((na, nka, nb, kk, cc), BF16)],
        scratch_shapes=[] if steps == 1 else [pltpu.VMEM((kk, cc), F32)], args=(a, b))


def proj_fwd(x1, sh, sc, w_in, seq, name, job=None):
    tokens, dm = x1.shape
    tm = min(MIX_TILE, seq)
    tiles_per_seq = seq // tm
    widths = [N_Q_HEADS * HEAD_DIM, N_KV_HEADS * HEAD_DIM, N_KV_HEADS * HEAD_DIM, 512, 512, 512]
    assert sum(widths) == w_in.shape[0]

    def body(x_ref, sh_ref, sc_ref, w_ref, *outs):
        h = (x_ref[...] * (1.0 + sc_ref[0]) + sh_ref[0]).astype(BF16)
        proj = _dot_nt(h, w_ref[...])
        at = 0
        for o_ref, wdt in zip(outs, widths):
            o_ref[...] = proj[:, at:at + wdt].astype(o_ref.dtype)
            at += wdt

    tile = pl.BlockSpec((tm, dm), lambda i: (i, 0))
    mod = _mod_spec(tiles_per_seq, dm)
    return _call(
        body, job, name=name, grid=(tokens // tm,),
        in_specs=[tile, mod, mod, _const_spec(w_in.shape)],
        out_specs=[pl.BlockSpec((tm, wdt), lambda i: (i, 0)) for wdt in widths],
        out_shape=[jax.ShapeDtypeStruct((tokens, wdt), F32 if i < 3 else BF16) for i, wdt in enumerate(widths)],
        args=(x1, sh, sc, w_in))


LANES = 2 * HEAD_DIM


def _head_lane(shape):
    return lax.broadcasted_iota(jnp.int32, shape, 1) % HEAD_DIM


def _lane_half(shape):
    return lax.broadcasted_iota(jnp.int32, shape, 1) // HEAD_DIM


def _swap_rot(v):
    lane = _head_lane(v.shape)
    half = ROT_DIM // 2
    return jnp.where(lane < half, pltpu.roll(v, LANES - half, 1),
                     jnp.where(lane < ROT_DIM, pltpu.roll(v, half, 1), 0.0))


def _rope(v, cos_t, sin_t):
    return v * cos_t + _swap_rot(v) * sin_t


def _unrope(dv, cos_t, sin_t):
    return dv * cos_t + _swap_rot(dv * sin_t)


def _both_halves(t, g):
    return jnp.where(_lane_half(t.shape) == g, t, pltpu.roll(t, HEAD_DIM, 1))


def _fold_halves(t, g):
    return jnp.where(_lane_half(t.shape) == g, t + pltpu.roll(t, HEAD_DIM, 1), 0.0)


def _stack_heads(blocks):
    rows = []
    for blk in blocks:
        half = _lane_half(blk.shape)
        rows += [jnp.where(half == 0, blk, 0.0), jnp.where(half == 1, blk, 0.0)]
    return jnp.concatenate(rows, axis=0)


def _unstack_heads(t, j):
    lo = t[(2 * j) * ATTN_BLOCK:(2 * j + 1) * ATTN_BLOCK]
    hi = t[(2 * j + 1) * ATTN_BLOCK:(2 * j + 2) * ATTN_BLOCK]
    return jnp.where(_lane_half(lo.shape) == 0, lo, hi)


def _band_mask(q0, w0):
    rows, cols = GQA_GROUP * ATTN_BLOCK, 2 * ATTN_BLOCK
    qi = lax.broadcasted_iota(jnp.int32, (rows, cols), 0) % ATTN_BLOCK + q0
    ki = lax.broadcasted_iota(jnp.int32, (rows, cols), 1) + w0
    diff = qi - ki
    return (diff >= 0) & (diff < ATTN_BLOCK)


def _attn_specs(seq):
    q_spec = pl.BlockSpec((seq, GQA_GROUP * HEAD_DIM), lambda b, g: (b, g))
    kv_spec = pl.BlockSpec((seq, LANES), lambda b, g: (b, 0))
    sink_spec = pl.BlockSpec((1, GQA_GROUP * ATTN_BLOCK, 1), lambda b, g: (g, 0, 0))
    return q_spec, kv_spec, sink_spec


def _block_starts(n):
    q0 = pl.multiple_of(n * ATTN_BLOCK, ATTN_BLOCK)
    w0 = pl.multiple_of(jnp.maximum(n - 1, 0) * ATTN_BLOCK, ATTN_BLOCK)
    return q0, w0


def _stacked_queries(ref, rows):
    return _stack_heads([ref[rows, j * LANES:(j + 1) * LANES] for j in range(2)]).astype(BF16)


def _sink_columns(sinks):
    return jnp.repeat(sinks.reshape(N_KV_HEADS, GQA_GROUP), ATTN_BLOCK, axis=1)[:, :, None]


def _probs_spec(nblk):
    return pl.BlockSpec((1, 1, nblk, GQA_GROUP * ATTN_BLOCK, 2 * ATTN_BLOCK), lambda b, g: (b, g, 0, 0, 0))


def _sink_probs_spec():
    return pl.BlockSpec((1, 1, GQA_GROUP * ATTN_BLOCK, LANES), lambda b, g: (b, g, 0, 0))


def attn_fwd(q, k, v, cos_t, sin_t, sinks, seq, name, job=None):
    tokens = q.shape[0]
    nblk = seq // ATTN_BLOCK
    assert nblk >= 2
    scale = HEAD_DIM ** -0.5

    nseq = tokens // seq
    rows_stacked = GQA_GROUP * ATTN_BLOCK
    assert nblk <= LANES

    def body(q_ref, k_ref, v_ref, cos_ref, sin_ref, sink_ref, o_ref, qr_ref, p_ref, ps_ref, kd_ref, vd_ref):
        g = pl.program_id(1)
        kd_ref[...] = _both_halves(_rope(k_ref[...].astype(F32), cos_ref[...], sin_ref[...]), g).astype(BF16)
        vd_ref[...] = _both_halves(v_ref[...].astype(F32), g).astype(BF16)
        sink = sink_ref[0]
        lane = lax.broadcasted_iota(jnp.int32, (rows_stacked, LANES), 1)

        ps_ref[...] = jnp.zeros_like(ps_ref)

        def block(n, carry):
            q0, w0 = _block_starts(n)
            rows, win = pl.ds(q0, ATTN_BLOCK), pl.ds(w0, 2 * ATTN_BLOCK)
            blocks = []
            for j in range(2):
                qr = _rope(q_ref[rows, j * LANES:(j + 1) * LANES].astype(F32), cos_ref[rows, :], sin_ref[rows, :]).astype(BF16)
                qr_ref[rows, j * LANES:(j + 1) * LANES] = qr
                blocks.append(qr)
            qs = _stack_heads(blocks)
            s = _dot_nt(qs, kd_ref[win, :]) * scale
            s = jnp.where(_band_mask(q0, w0), s, NEG_BIG)
            m = jnp.maximum(jnp.max(s, axis=-1, keepdims=True), sink)
            p = jnp.exp(s - m)
            e_sink = jnp.exp(sink - m)
            inv = pl.reciprocal(jnp.sum(p, axis=-1, keepdims=True) + e_sink, approx=True)
            pn = (p * inv).astype(BF16)
            p_ref[0, 0, n] = pn
            out = _dot(pn, vd_ref[win, :])
            for j in range(2):
                o_ref[rows, j * LANES:(j + 1) * LANES] = _unstack_heads(out, j).astype(o_ref.dtype)
            ps_ref[0, 0] = jnp.where(lane == n, e_sink * inv, ps_ref[0, 0])
            return carry

        lax.fori_loop(0, nblk, block, 0, unroll=2)

    q_spec, kv_spec, sink_spec = _attn_specs(seq)
    return _call(
        body, job, name=name, grid=(nseq, N_KV_HEADS),
        in_specs=[q_spec, kv_spec, kv_spec, kv_spec, kv_spec, sink_spec],
        out_specs=[q_spec, q_spec, _probs_spec(nblk), _sink_probs_spec()],
        out_shape=[jax.ShapeDtypeStruct(q.shape, BF16), jax.ShapeDtypeStruct(q.shape, BF16),
                   jax.ShapeDtypeStruct((nseq, N_KV_HEADS, nblk, rows_stacked, 2 * ATTN_BLOCK), BF16),
                   jax.ShapeDtypeStruct((nseq, N_KV_HEADS, rows_stacked, LANES), F32)],
        scratch_shapes=[pltpu.VMEM((seq, LANES), BF16), pltpu.VMEM((seq, LANES), BF16)],
        args=(q, k, v, cos_t, sin_t, _sink_columns(sinks)))


def attn_bwd(qr, k, v, do, probs, sink_probs, cos_t, sin_t, seq, name, job=None):
    tokens = qr.shape[0]
    nseq = tokens // seq
    nblk = seq // ATTN_BLOCK
    assert nblk >= 2
    rows_stacked = GQA_GROUP * ATTN_BLOCK
    scale = HEAD_DIM ** -0.5

    def body(q_ref, k_ref, v_ref, do_ref, p_ref, ps_ref, cos_ref, sin_ref, dq_ref, dk_ref, dv_ref, ds_ref,
             kd_ref, vd_ref, dkd_ref, dvd_ref, acc_ref):
        g = pl.program_id(1)
        kd_ref[...] = _both_halves(_rope(k_ref[...].astype(F32), cos_ref[...], sin_ref[...]), g).astype(BF16)
        vd_ref[...] = _both_halves(v_ref[...].astype(F32), g).astype(BF16)
        dkd_ref[...] = jnp.zeros_like(dkd_ref)
        dvd_ref[...] = jnp.zeros_like(dvd_ref)
        acc_ref[...] = jnp.zeros_like(acc_ref)
        lane = lax.broadcasted_iota(jnp.int32, (rows_stacked, LANES), 1)

        def block(n, carry):
            q0, w0 = _block_starts(n)
            rows, win = pl.ds(q0, ATTN_BLOCK), pl.ds(w0, 2 * ATTN_BLOCK)
            qs = _stacked_queries(q_ref, rows)
            dos = _stacked_queries(do_ref, rows)
            kw, vw = kd_ref[win, :], vd_ref[win, :]
            pn16 = p_ref[0, 0, n]
            pn = pn16.astype(F32)
            dvd_ref[win, :] += _dot_tn(pn16, dos)
            dp = _dot_nt(dos, vw)
            delta = jnp.sum(dp * pn, axis=-1, keepdims=True)
            ds = (pn * (dp - delta)).astype(BF16)
            dqs = _dot(ds, kw) * scale
            dkd_ref[win, :] += _dot_tn(ds, qs) * scale
            cos_b, sin_b = cos_ref[rows, :], sin_ref[rows, :]
            for j in range(2):
                dq_ref[rows, j * LANES:(j + 1) * LANES] = _unrope(_unstack_heads(dqs, j), cos_b, sin_b).astype(BF16)
            acc_ref[...] += jnp.where(lane == n, ps_ref[0, 0] * delta, 0.0)
            return carry

        lax.fori_loop(0, nblk // 2, lambda i, carry: block(2 * i + 1, block(2 * i, carry)), 0)
        ds_ref[0, 0] = -jnp.sum(acc_ref[...], axis=-1, keepdims=True)
        dk_g = _unrope(_fold_halves(dkd_ref[...], g), cos_ref[...], sin_ref[...])
        dv_g = _fold_halves(dvd_ref[...], g)

        @pl.when(g == 0)
        def _():
            dk_ref[...] = dk_g
            dv_ref[...] = dv_g

        @pl.when(g != 0)
        def _():
            dk_ref[...] += dk_g
            dv_ref[...] += dv_g

    q_spec, kv_spec, _ = _attn_specs(seq)
    return _call(
        body, job, name=name, grid=(nseq, N_KV_HEADS),
        in_specs=[q_spec, kv_spec, kv_spec, q_spec, _probs_spec(nblk), _sink_probs_spec(), kv_spec, kv_spec],
        out_specs=[q_spec, kv_spec, kv_spec, pl.BlockSpec((1, 1, rows_stacked, 1), lambda b, g: (b, g, 0, 0))],
        out_shape=[jax.ShapeDtypeStruct(qr.shape, BF16), jax.ShapeDtypeStruct(k.shape, F32),
                   jax.ShapeDtypeStruct(k.shape, F32), jax.ShapeDtypeStruct((nseq, N_KV_HEADS, rows_stacked, 1), F32)],
        scratch_shapes=[pltpu.VMEM((seq, LANES), BF16), pltpu.VMEM((seq, LANES), BF16),
                        pltpu.VMEM((seq, LANES), F32), pltpu.VMEM((seq, LANES), F32),
                        pltpu.VMEM((rows_stacked, LANES), F32)],
        args=(qr, k, v, do, probs, sink_probs, cos_t, sin_t))


CONV_COLS = 128


def _shift_down(z, by):
    t = lax.broadcasted_iota(jnp.int32, z.shape, 0)
    return jnp.where(t >= by, pltpu.roll(z, by, 0), 0.0)


def _shift_up(z, by):
    n = z.shape[0]
    t = lax.broadcasted_iota(jnp.int32, z.shape, 0)
    return jnp.where(t < n - by, pltpu.roll(z, n - by, 0), 0.0)


def conv_fwd(u, bg, cg, conv_w, seq, name):
    tokens, width = u.shape

    def body(u_ref, bg_ref, cg_ref, w_ref, o_ref):
        z = cg_ref[...].astype(F32) * u_ref[...].astype(F32)
        yy = w_ref[2:3, :] * z + w_ref[1:2, :] * _shift_down(z, 1) + w_ref[0:1, :] * _shift_down(z, 2)
        o_ref[...] = (bg_ref[...].astype(F32) * yy).astype(BF16)

    col = pl.BlockSpec((seq, CONV_COLS), lambda j, b: (b, j))
    return pl.pallas_call(
        body, name=name, grid=(width // CONV_COLS, tokens // seq),
        in_specs=[col, col, col, pl.BlockSpec((CONV_TAPS, CONV_COLS), lambda j, b: (0, j))],
        out_specs=col, out_shape=jax.ShapeDtypeStruct((tokens, width), BF16),
        compiler_params=_params(("parallel", "parallel")),
    )(u, bg, cg, conv_w)


def conv_bwd(dout, u, bg, cg, conv_w, seq, name):
    tokens, width = u.shape

    def body(do_ref, u_ref, bg_ref, cg_ref, w_ref, du_ref, dbg_ref, dcg_ref, dw_ref):
        uu, cg_v, do = u_ref[...].astype(F32), cg_ref[...].astype(F32), do_ref[...].astype(F32)
        z = cg_v * uu
        z1, z2 = _shift_down(z, 1), _shift_down(z, 2)
        yy = w_ref[2:3, :] * z + w_ref[1:2, :] * z1 + w_ref[0:1, :] * z2
        dbg_ref[...] = (do * yy).astype(BF16)
        dyy = do * bg_ref[...].astype(F32)
        dz = w_ref[2:3, :] * dyy + w_ref[1:2, :] * _shift_up(dyy, 1) + w_ref[0:1, :] * _shift_up(dyy, 2)
        du_ref[...] = (dz * cg_v).astype(BF16)
        dcg_ref[...] = (dz * uu).astype(BF16)

        @pl.when(pl.program_id(1) == 0)
        def _():
            dw_ref[...] = jnp.zeros_like(dw_ref)

        dw_ref[0:1, :] += jnp.sum(dyy * z2, axis=0, keepdims=True)
        dw_ref[1:2, :] += jnp.sum(dyy * z1, axis=0, keepdims=True)
        dw_ref[2:3, :] += jnp.sum(dyy * z, axis=0, keepdims=True)

    col = pl.BlockSpec((seq, CONV_COLS), lambda j, b: (b, j))
    w_spec = pl.BlockSpec((CONV_TAPS, CONV_COLS), lambda j, b: (0, j))
    act = jax.ShapeDtypeStruct((tokens, width), BF16)
    return pl.pallas_call(
        body, name=name, grid=(width // CONV_COLS, tokens // seq),
        in_specs=[col, col, col, col, w_spec], out_specs=[col, col, col, w_spec],
        out_shape=[act, act, act, jax.ShapeDtypeStruct((CONV_TAPS, width), F32)],
        compiler_params=_params(("parallel", "arbitrary")),
    )(dout, u, bg, cg, conv_w)


def out_fwd(x1, attn, conv, gt, w_out, ln_g, ln_b, seq, name, job=None):
    tokens, dm = x1.shape
    half = attn.shape[1]
    tm = min(MIX_TILE, seq)
    tiles_per_seq = seq // tm

    def body(x_ref, a_ref, c_ref, gt_ref, w_ref, lg_ref, lb_ref, xo_ref, r_ref, mi_ref, mix_ref):
        mixin = jnp.concatenate([a_ref[...], c_ref[...]], axis=1).astype(BF16)
        mi_ref[...] = mixin
        mix = _dot(mixin, w_ref[...])
        mix_ref[...] = mix.astype(BF16)
        r = DN_ALPHA * x_ref[...] + (1.0 + gt_ref[0]) * mix
        r_ref[...] = r
        xhat, _ = _ln_stats(r)
        xo_ref[...] = xhat * lg_ref[...] + lb_ref[...]

    tile = pl.BlockSpec((tm, dm), lambda i: (i, 0))
    htile = pl.BlockSpec((tm, half), lambda i: (i, 0))
    return _call(
        body, job, name=name, grid=(tokens // tm,),
        in_specs=[tile, htile, htile, _mod_spec(tiles_per_seq, dm), _const_spec(w_out.shape),
                  _const_spec((1, dm)), _const_spec((1, dm))],
        out_specs=[tile, tile, tile, tile],
        out_shape=[jax.ShapeDtypeStruct((tokens, dm), F32), jax.ShapeDtypeStruct((tokens, dm), F32),
                   jax.ShapeDtypeStruct((tokens, dm), BF16), jax.ShapeDtypeStruct((tokens, dm), BF16)],
        args=(x1, attn, conv, gt, w_out, ln_g, ln_b))


def out_bwd(dy, r, mix, gt, w_out, ln_g, seq, name, job=None):
    tokens, dm = r.shape
    half = dm // 2
    tm = min(MIX_TILE, seq)
    tiles_per_seq = seq // tm
    nseq = tokens // seq

    def body(dy_ref, r_ref, mix_ref, gt_ref, w_ref, lg_ref, dres_ref, da_ref, dc_ref, dmix_ref, dln_ref, dgt_ref):
        i = pl.program_id(0)
        dr, dgain, dbias = _ln_bwd(dy_ref[...], r_ref[...], lg_ref[...])

        @pl.when(i == 0)
        def _():
            dln_ref[...] = jnp.zeros_like(dln_ref)

        @pl.when(i % tiles_per_seq == 0)
        def _():
            dgt_ref[...] = jnp.zeros_like(dgt_ref)

        dln_ref[0:1, :] += dgain
        dln_ref[1:2, :] += dbias
        dgt_ref[0] += jnp.sum(dr * mix_ref[...].astype(F32), axis=0, keepdims=True)
        dres_ref[...] = DN_ALPHA * dr
        dmix = ((1.0 + gt_ref[0]) * dr).astype(BF16)
        dmix_ref[...] = dmix
        dmixin = _dot_nt(dmix, w_ref[...])
        da_ref[...] = dmixin[:, :half].astype(BF16)
        dc_ref[...] = dmixin[:, half:].astype(BF16)

    tile = pl.BlockSpec((tm, dm), lambda i: (i, 0))
    htile = pl.BlockSpec((tm, half), lambda i: (i, 0))
    return _call(
        body, job, name=name, grid=(tokens // tm,),
        in_specs=[tile, tile, tile, _mod_spec(tiles_per_seq, dm), _const_spec(w_out.shape), _const_spec((1, dm))],
        out_specs=[tile, htile, htile, tile, pl.BlockSpec((2, dm), lambda i: (0, 0)),
                   pl.BlockSpec((1, 1, dm), lambda i: (i // tiles_per_seq, 0, 0))],
        out_shape=[jax.ShapeDtypeStruct((tokens, dm), F32), jax.ShapeDtypeStruct((tokens, half), BF16),
                   jax.ShapeDtypeStruct((tokens, half), BF16), jax.ShapeDtypeStruct((tokens, dm), BF16),
                   jax.ShapeDtypeStruct((2, dm), F32), jax.ShapeDtypeStruct((nseq, 1, dm), F32)],
        args=(dy, r, mix, gt, w_out, ln_g))


def proj_bwd(parts, dres, x1, sh, sc, w_in, r_prev, f_prev, gt_prev, ln_g_prev, seq, name, job=None):
    tokens, dm = x1.shape
    tm = min(MIX_TILE, seq)
    tiles_per_seq = seq // tm
    nseq = tokens // seq
    widths = [p.shape[1] for p in parts]
    total = sum(widths)

    def body(*refs):
        part_refs = refs[:6]
        (dres_ref, x_ref, sh_ref, sc_ref, w_ref, r_ref, f_ref, gt_ref, lg_ref,
         dr_ref, df_ref, dproj_ref, h_ref, dmod_ref, dln_ref, dgt_ref) = refs[6:]
        i = pl.program_id(0)
        dproj = jnp.concatenate([p[...].astype(BF16) for p in part_refs], axis=1)
        dproj_ref[...] = dproj
        dh = _dot(dproj, w_ref[...])
        xx = x_ref[...]
        one_sc = 1.0 + sc_ref[0]
        h_ref[...] = (xx * one_sc + sh_ref[0]).astype(BF16)
        dr, dgain, dbias = _ln_bwd(dres_ref[...] + dh * one_sc, r_ref[...], lg_ref[...])
        dr_ref[...] = dr
        df_ref[...] = ((0.5 * (1.0 + gt_ref[0])) * dr).astype(BF16)

        @pl.when(i == 0)
        def _():
            dln_ref[...] = jnp.zeros_like(dln_ref)

        @pl.when(i % tiles_per_seq == 0)
        def _():
            dmod_ref[...] = jnp.zeros_like(dmod_ref)
            dgt_ref[...] = jnp.zeros_like(dgt_ref)

        dmod_ref[0, 0:1, :] += jnp.sum(dh, axis=0, keepdims=True)
        dmod_ref[0, 1:2, :] += jnp.sum(dh * xx, axis=0, keepdims=True)
        dln_ref[0:1, :] += dgain
        dln_ref[1:2, :] += dbias
        dgt_ref[0] += jnp.sum(dr * (0.5 * f_ref[...].astype(F32)), axis=0, keepdims=True)

    tile = pl.BlockSpec((tm, dm), lambda i: (i, 0))
    mod = _mod_spec(tiles_per_seq, dm)
    return _call(
        body, job, name=name, grid=(tokens // tm,),
        in_specs=[pl.BlockSpec((tm, wdt), lambda i: (i, 0)) for wdt in widths]
        + [tile, tile, mod, mod, _const_spec(w_in.shape), tile, tile, mod, _const_spec((1, dm))],
        out_specs=[tile, tile, pl.BlockSpec((tm, total), lambda i: (i, 0)), tile,
                   pl.BlockSpec((1, 2, dm), lambda i: (i // tiles_per_seq, 0, 0)),
                   pl.BlockSpec((2, dm), lambda i: (0, 0)), mod],
        out_shape=[jax.ShapeDtypeStruct((tokens, dm), F32), jax.ShapeDtypeStruct((tokens, dm), BF16),
                   jax.ShapeDtypeStruct((tokens, total), BF16), jax.ShapeDtypeStruct((tokens, dm), BF16),
                   jax.ShapeDtypeStruct((nseq, 2, dm), F32), jax.ShapeDtypeStruct((2, dm), F32),
                   jax.ShapeDtypeStruct((nseq, 1, dm), F32)],
        args=(*parts, dres, x1, sh, sc, w_in, r_prev, f_prev, gt_prev, ln_g_prev))


def _rope_tables(positions):
    half = ROT_DIM // 2
    inv_freq = jnp.power(jnp.float32(ROPE_THETA), -jnp.arange(0, ROT_DIM, 2, dtype=F32) / ROT_DIM)
    lane = jnp.arange(LANES) % HEAD_DIM
    freq = jnp.where(lane < ROT_DIM, inv_freq[lane % half], 0.0)
    sign = jnp.where(lane < half, -1.0, 1.0).astype(F32)
    ang = positions.astype(F32)[:, None] * freq[None, :]
    return jnp.cos(ang), sign[None, :] * jnp.sin(ang)


def kernel(x, c, positions, w_ada, b_ada, ffn1_w_gate_up, ffn1_w_down, ln1_g, ln1_b, w_in, conv_w, attn_sinks, w_out, ln2_g, ln2_b, ffn2_w_gate_up, ffn2_w_down, ln3_g, ln3_b, loss_target, m_w_ada, m_b_ada, m_ffn1_w_gate_up, m_ffn1_w_down, m_ln1_g, m_ln1_b, m_w_in, m_conv_w, m_attn_sinks, m_w_out, m_ln2_g, m_ln2_b, m_ffn2_w_gate_up, m_ffn2_w_down, m_ln3_g, m_ln3_b, v_w_ada, v_b_ada, v_ffn1_w_gate_up, v_ffn1_w_down, v_ln1_g, v_ln1_b, v_w_in, v_conv_w, v_attn_sinks, v_w_out, v_ln2_g, v_ln2_b, v_ffn2_w_gate_up, v_ffn2_w_down, v_ln3_g, v_ln3_b):
    nseq, seq, dm = x.shape
    tokens = nseq * seq
    dev = 4 * lax.axis_index("x") + 2 * lax.axis_index("y") + lax.axis_index("c")
    core = lax.axis_index("c").astype(jnp.int32).reshape(1)
    ada_cols = w_ada.shape[2]
    ff = ffn1_w_down.shape[1] * N_DEV
    fc = ff // 4
    in_cols = w_in.shape[2]
    conv_cols = conv_w.shape[2]

    def t_bf16(w):
        return w[0].T.astype(BF16)

    c_all, convw_all = all_gather([c, conv_w[0]], "gather_cond")
    c_all = c_all.reshape(N_DEV * nseq, dm)
    convw_full = convw_all.transpose(1, 0, 2).reshape(CONV_TAPS, N_DEV * conv_cols)

    b_cols = lax.dynamic_slice(b_ada, (0, dev * ada_cols), (1, ada_cols))
    cond_all, mod_cols = ada_fwd(c_all, w_ada[0], b_cols, "ada_fwd")
    wgu1, mod_all = all_gather([t_bf16(ffn1_w_gate_up), mod_cols], "gather_ffn1")
    wgu1 = wgu1.reshape(2, ff, dm)
    mod = lax.dynamic_slice(mod_all, (0, dev * nseq, 0), (N_DEV, nseq, ada_cols))
    mod = mod.transpose(1, 0, 2).reshape(nseq, 9, 1, dm)
    sh1, sc1, g1, sh2, sc2, g2, sh3, sc3, g3 = [mod[:, i] for i in range(9)]

    x0 = x.reshape(tokens, dm)
    (gu1, a1, h1), (wd1, wout) = ffn_up(x0, sh1, sc1, wgu1, seq, "ffn1_up",
                                        job=_GatherJob([ffn1_w_down[0].astype(BF16), w_out[0].astype(BF16)]))
    wd1, wout = wd1.reshape(ff, dm), wout.reshape(dm, dm)
    (x1, r1, f1), (win,) = ffn_down(x0, a1, g1, wd1, ln1_g, ln1_b, seq, "ffn1_down", job=_GatherJob([t_bf16(w_in)]))
    win = win.reshape(N_DEV * in_cols, dm)
    (q, k, v, u, bg, cg), wd2_spread = proj_fwd(x1, sh2, sc2, win, seq, "proj_fwd",
                                                job=gather_spread_job([ffn2_w_down[0].astype(BF16)]))
    cos_t, sin_t = _rope_tables(positions.reshape(tokens))
    sinks = attn_sinks[0]
    (attn, q_rot, probs, sink_probs), wgu2_spread = attn_fwd(q, k, v, cos_t, sin_t, sinks, seq, "attn_fwd",
                                                             job=gather_spread_job([t_bf16(ffn2_w_gate_up)]))
    conv = conv_fwd(u, bg, cg, convw_full, seq, "conv_fwd")
    (x2, r2, mixin, mix), (wd2, wgu2) = out_fwd(x1, attn, conv, g2, wout, ln2_g, ln2_b, seq, "out_fwd",
                                                job=gather_forward_job(wd2_spread + wgu2_spread))
    wd2, wgu2 = wd2.reshape(ff, dm), wgu2.reshape(2, ff, dm)
    target = loss_target.reshape(tokens, dm)
    dr3, df3, gu3, a3, h3, loss_part, dln3, dg3 = ffn_loss(x2, sh3, sc3, g3, wgu2, wd2, ln3_g, ln3_b, target, seq, "ffn2_fwd")

    (dx2, dgu3, dmod3), _ = ffn_bwd(dr3, df3, x2, gu3, sc3, wgu2, wd2, seq, "ffn2_bwd")
    pair = 2 * fc
    g_wd2 = tn_matmul(a3[None], df3[None], "ffn2_dwd", a_width=pair)[0][0].reshape(N_DEV, ff // N_DEV, dm)
    g_wgu2 = tn_matmul(dgu3, h3[None], "ffn2_dwgu", a_width=pair)[0][0].reshape(N_DEV, fc, dm)
    (dres2, dattn, dconv, dmix, dln2, dg2), swapped = out_bwd(dx2, r2, mix, g2, wout, ln2_g, seq, "out_bwd",
                                                              job=swap_job([g_wgu2, g_wd2]))
    p_wgu2, own_wgu2 = pair_sum(core, g_wgu2, swapped[0], "pair_wgu2")
    p_wd2, own_wd2 = pair_sum(core, g_wd2, swapped[1], "pair_wd2")
    du, dbg, dcg, dconvw = conv_bwd(dconv, u, bg, cg, convw_full, seq, "conv_bwd")
    (dq, dk, dv, dsink_rows), (far_wd2,) = attn_bwd(
        q_rot, k, v, dattn, probs, sink_probs, cos_t, sin_t, seq, "attn_bwd", job=chip_exchange_job([p_wd2]))
    parts = [dq, dk, dv, du, dbg, dcg]
    (dr1, df1, dproj, h2, dmod2, dln1, dg1), far_top = proj_bwd(
        parts, dres2, x1, sh2, sc2, win, r1, f1, g1, ln1_g, seq, "proj_bwd",
        job=chip_exchange_job([p_wgu2], rows=(0, fc // 2)))
    (dx0, dgu1, dmod1), _ = ffn_bwd(dr1, df1, x0, gu1, sc1, wgu1, wd1, seq, "ffn1_bwd")

    dmod = jnp.concatenate([dmod1, dg1, dmod2, dg2, dmod3, dg3], axis=1).reshape(nseq, 9 * dm)
    half = dm // 2
    jobs = _Jobs([gather_spread_job([dmod]),
                  chip_exchange_job([p_wgu2], rows=(fc // 2, fc // 2), into=far_top)])
    (g_wd1,), res = tn_matmul(a1[None], df1[None], "ffn1_dwd", job=jobs, a_width=pair)
    dmod_spread, (far_wgu2,) = jobs.split(res)
    g_wd1 = g_wd1.reshape(N_DEV, ff // N_DEV, dm)
    jobs = _Jobs([swap_job([g_wd1]), gather_forward_job(dmod_spread)])
    (g_l,), res = tn_matmul(dgu1, h1[None], "ffn1_dwgu_l", job=jobs, b_cols=(0, half), a_width=pair)
    (sw_wd1,), (dmod_all,) = jobs.split(res)
    g_l = g_l.reshape(N_DEV, fc, half)
    p_wd1, own_wd1 = pair_sum(core, g_wd1, sw_wd1, "pair_wd1")
    jobs = _Jobs([chip_exchange_job([p_wd1]), swap_job([g_l])])
    (g_r,), res = tn_matmul(dgu1, h1[None], "ffn1_dwgu_r", job=jobs, b_cols=(1, half), a_width=pair)
    (far_wd1,), (sw_l,) = jobs.split(res)
    g_r = g_r.reshape(N_DEV, fc, half)
    p_l, own_l = pair_sum(core, g_l, sw_l, "pair_wgu1_l")

    dmod_cols = lax.dynamic_slice(dmod_all.reshape(N_DEV * nseq, 9 * dm), (0, dev * ada_cols), (N_DEV * nseq, ada_cols))
    grad_w_ada, gb_cols = ada_bwd(cond_all, dmod_cols, "ada_bwd")
    dsinks = jnp.sum(dsink_rows.reshape(nseq, N_Q_HEADS, ATTN_BLOCK), axis=(0, 2))
    small = jnp.zeros((8, dm), F32)
    small = small.at[0:2].set(dln1).at[2:4].set(dln2).at[4:6].set(dln3)
    small = small.at[6, 0:N_Q_HEADS].set(dsinks).at[7, 0].set(loss_part[0, 0])

    jobs = _Jobs([chip_exchange_job([p_l]), swap_job([g_r]), gather_spread_job([small, dconvw, gb_cols])])
    (g_win,), res = tn_matmul(dproj[None], h2[None], "dwin", job=jobs)
    (far_l,), (sw_r,), small_spread = jobs.split(res)
    g_win = g_win.reshape(N_DEV, in_cols, dm)
    p_r, own_r = pair_sum(core, g_r, sw_r, "pair_wgu1_r")
    jobs = _Jobs([chip_exchange_job([p_r]), swap_job([g_win]), gather_forward_job(small_spread)])
    (g_wout,), res = tn_matmul(mixin[None], dmix[None], "dwout", job=jobs)
    (far_r,), (sw_win,), (small_all, dconvw_all, gb_all) = jobs.split(res)
    g_wout = g_wout.reshape(N_DEV, dm // N_DEV, dm)
    p_win, own_win = pair_sum(core, g_win, sw_win, "pair_win")

    given = dict(w_ada=(w_ada, m_w_ada, v_w_ada), b_ada=(b_ada, m_b_ada, v_b_ada),
                 ffn1_w_gate_up=(ffn1_w_gate_up, m_ffn1_w_gate_up, v_ffn1_w_gate_up),
                 ffn1_w_down=(ffn1_w_down, m_ffn1_w_down, v_ffn1_w_down),
                 ln1_g=(ln1_g, m_ln1_g, v_ln1_g), ln1_b=(ln1_b, m_ln1_b, v_ln1_b),
                 w_in=(w_in, m_w_in, v_w_in), conv_w=(conv_w, m_conv_w, v_conv_w),
                 attn_sinks=(attn_sinks, m_attn_sinks, v_attn_sinks), w_out=(w_out, m_w_out, v_w_out),
                 ln2_g=(ln2_g, m_ln2_g, v_ln2_g), ln2_b=(ln2_b, m_ln2_b, v_ln2_b),
                 ffn2_w_gate_up=(ffn2_w_gate_up, m_ffn2_w_gate_up, v_ffn2_w_gate_up),
                 ffn2_w_down=(ffn2_w_down, m_ffn2_w_down, v_ffn2_w_down),
                 ln3_g=(ln3_g, m_ln3_g, v_ln3_g), ln3_b=(ln3_b, m_ln3_b, v_ln3_b))
    transposed = ("ffn1_w_gate_up", "ffn2_w_gate_up", "w_in")

    def big_adamw(nm, grad, far=None):
        flip = nm in transposed
        w2, m2, v2 = [t[0].T if flip else t[0] for t in given[nm]]
        return [t.T[None] if flip else t[None] for t in adamw(w2, grad, m2, v2, "adamw_" + nm, others=far)]

    jobs = _Jobs([chip_exchange_job([p_win]), swap_job([g_wout])])
    (far_win,), (sw_wout,) = jobs.split(run_job(jobs, "rs_tail_win"))
    p_wout, own_wout = pair_sum(core, g_wout, sw_wout, "pair_wout")
    (far_wout,) = run_job(chip_exchange_job([p_wout]), "rs_tail_wout")

    grads = {
        "ffn1_w_gate_up": jnp.concatenate([own_l, own_r], axis=1), "ffn1_w_down": own_wd1,
        "w_in": own_win, "w_out": own_wout, "ffn2_w_gate_up": own_wgu2, "ffn2_w_down": own_wd2,
    }
    others = {"ffn1_w_gate_up": jnp.concatenate([far_l, far_r], axis=2), "ffn1_w_down": far_wd1,
              "w_in": far_win, "w_out": far_wout, "ffn2_w_gate_up": far_wgu2, "ffn2_w_down": far_wd2}
    results = {"w_ada": big_adamw("w_ada", grad_w_ada)}
    for nm in grads:
        results[nm] = big_adamw(nm, grads[nm], others[nm])

    small_sum = sum_devices(small_all, "sum_small")
    dconvw_sum = sum_devices(dconvw_all, "sum_convw")
    loss = small_sum[7, 0]
    grads["b_ada"] = gb_all.reshape(1, N_DEV * ada_cols)
    grads["conv_w"] = lax.dynamic_slice(dconvw_sum, (0, dev * conv_cols), (CONV_TAPS, conv_cols))
    grads["attn_sinks"] = small_sum[6:7, 0:N_Q_HEADS]
    for i, nm in enumerate(["ln1_g", "ln1_b", "ln2_g", "ln2_b", "ln3_g", "ln3_b"]):
        grads[nm] = small_sum[i:i + 1]

    order = ["w_ada", "b_ada", "ffn1_w_gate_up", "ffn1_w_down", "ln1_g", "ln1_b", "w_in", "conv_w", "attn_sinks",
             "w_out", "ln2_g", "ln2_b", "ffn2_w_gate_up", "ffn2_w_down", "ln3_g", "ln3_b"]
    small_names = [nm for nm in order if nm not in results]
    items = []
    for nm in small_names:
        shape = given[nm][0].shape
        two_d = (shape[-2], shape[-1])
        items.append((given[nm][0].reshape(two_d), grads[nm].reshape(two_d), *[t.reshape(two_d) for t in given[nm][1:]]))
    for nm, res in zip(small_names, adamw_small(items, "adamw_small")):
        shape = given[nm][0].shape
        results[nm] = [grads[nm].reshape(shape)] + [t.reshape(shape) for t in res]
    grad_x = dx0.reshape(nseq, seq, dm)
    return (loss, grad_x, *[results[nm][i] for i in range(4) for nm in order])
```

```python
import functools

import jax
import jax.numpy as jnp
from jax import lax
from jax.experimental import pallas as pl
from jax.experimental.pallas import tpu as pltpu

F32 = jnp.float32
BF16 = jnp.bfloat16
MESH = pl.DeviceIdType.MESH

N_DEV = 8
N_CHIP = 4
HEAD_DIM = 64
N_Q_HEADS = 8
N_KV_HEADS = 2
GQA_GROUP = N_Q_HEADS // N_KV_HEADS
ATTN_BLOCK = 128
ROT_DIM = 16
ROPE_THETA = 500000.0
CONV_TAPS = 3
LN_EPS = 1e-5
DN_ALPHA = 2.0 ** 0.25
ADAM_LR = 0.001
ADAM_B1 = 0.9
ADAM_B2 = 0.999
ADAM_EPS = 1e-08
ADAM_WD = 0.01
ADAM_STEP = 10
NEG_BIG = -1e30

VMEM_LIMIT = 56 * 1024 * 1024
FFN_FWD_TILE = 512
MIX_TILE = 512
FFN_WIDE_TILE = 512
FFN_WIDE_VMEM = 62 * 1024 * 1024
TN_VMEM_BUDGET = 36 * 1024 * 1024


def _params(semantics=None, vmem=VMEM_LIMIT):
    return pltpu.CompilerParams(dimension_semantics=semantics, vmem_limit_bytes=vmem)


def _dot(a, b):
    return jnp.dot(a, b, preferred_element_type=F32)


def _dot_nt(a, b):
    return lax.dot_general(a, b, (((1,), (1,)), ((), ())), preferred_element_type=F32)


def _dot_tn(a, b):
    return lax.dot_general(a, b, (((0,), (0,)), ((), ())), preferred_element_type=F32)


def _sigmoid(x):
    return pl.reciprocal(1.0 + jnp.exp(-x), approx=True)


def _ln_stats(r):
    mu = jnp.mean(r, axis=-1, keepdims=True)
    d = r - mu
    var = jnp.mean(d * d, axis=-1, keepdims=True)
    rstd = lax.rsqrt(var + LN_EPS)
    return d * rstd, rstd


def _ln_bwd(dy, r, g):
    return _ln_bwd_normalized(dy, *_ln_stats(r), g)


def _ln_bwd_normalized(dy, xhat, rstd, g):
    dxhat = dy * g
    c1 = jnp.mean(dxhat, axis=-1, keepdims=True)
    c2 = jnp.mean(dxhat * xhat, axis=-1, keepdims=True)
    dr = rstd * (dxhat - c1 - xhat * c2)
    return dr, jnp.sum(dy * xhat, axis=0, keepdims=True), jnp.sum(dy, axis=0, keepdims=True)


def _const_spec(shape):
    nd = len(shape)
    return pl.BlockSpec(shape, lambda *_: (0,) * nd, pipeline_mode=pl.Buffered(1))


def all_gather(arrs, name):
    n = len(arrs)

    def body(*refs):
        ins, outs = refs[:n], refs[n:2 * n]
        send_sems, recv_sems, local_sems = refs[2 * n:]
        x, y, c = lax.axis_index("x"), lax.axis_index("y"), lax.axis_index("c")
        me, sibling = (x, y, c), (x, y, 1 - c)
        chips = [(1 - x, y), (x, 1 - y), (1 - x, 1 - y)]

        def slot(i, p):
            return outs[i].at[4 * p[0] + 2 * p[1] + p[2]]

        def copy(i, k, block, to, src=None):
            return pltpu.make_async_remote_copy(
                src_ref=slot(i, block) if src is None else src, dst_ref=slot(i, block),
                send_sem=send_sems.at[i, k], recv_sem=recv_sems.at[i, k],
                device_id=to, device_id_type=MESH)

        mine = [pltpu.make_async_copy(ins[i], slot(i, me), local_sems.at[i]) for i in range(n)]
        for cp in mine:
            cp.start()
        first = []
        for i in range(n):
            first.append(copy(i, 0, me, sibling, src=ins[i]))
            first += [copy(i, 1 + j, me, (*chip, c), src=ins[i]) for j, chip in enumerate(chips)]
        for cp in first:
            cp.start()
        passed = []
        for i in range(n):
            for j, chip in enumerate(chips):
                copy(i, 1 + j, (*chip, c), me).wait_recv()
                cp = copy(i, 4 + j, (*chip, c), sibling)
                cp.start()
                passed.append(cp)
        for i in range(n):
            copy(i, 0, sibling, me).wait_recv()
            for j, chip in enumerate(chips):
                copy(i, 4 + j, (*chip, 1 - c), me).wait_recv()
        for cp in first + passed:
            cp.wait_send()
        for cp in mine:
            cp.wait()

    any_spec = pl.BlockSpec(memory_space=pl.ANY)
    return pl.pallas_call(
        body, name=name,
        out_shape=[jax.ShapeDtypeStruct((N_DEV, *a.shape), a.dtype) for a in arrs],
        in_specs=[any_spec] * n, out_specs=[any_spec] * n,
        scratch_shapes=[pltpu.SemaphoreType.DMA((n, 7)), pltpu.SemaphoreType.DMA((n, 7)),
                        pltpu.SemaphoreType.DMA((n,))],
    )(*arrs)


def _place():
    x, y, c = lax.axis_index("x"), lax.axis_index("y"), lax.axis_index("c")
    return x, y, c, [(1 - x, y), (x, 1 - y), (1 - x, 1 - y)]


def _slot(p):
    return 4 * p[0] + 2 * p[1] + p[2]


class _Job:
    def __init__(self, ins, outs, nsem, copies, aliases=None, local=None):
        self.ins, self.outs, self.nsem, self.copies = list(ins), list(outs), nsem, copies
        self.aliases = aliases or {}
        self.local = local

    def scratch(self):
        s = [pltpu.SemaphoreType.DMA(self.nsem), pltpu.SemaphoreType.DMA(self.nsem)]
        if self.local is not None:
            s.append(pltpu.SemaphoreType.DMA((len(self.ins),)))
        return s

    def start(self, ins, outs, sems):
        if self.local is not None:
            for cp in self.local(ins, outs, sems[2]):
                cp.start()
        for cp in self.copies(ins, outs, sems[0], sems[1])[0]:
            cp.start()

    def finish(self, ins, outs, sems):
        started, awaited = self.copies(ins, outs, sems[0], sems[1])
        for cp in awaited:
            cp.wait_recv()
        for cp in started:
            cp.wait_send()
        if self.local is not None:
            for cp in self.local(ins, outs, sems[2]):
                cp.wait()


class _Jobs:
    def __init__(self, jobs):
        self.jobs = jobs
        self.ins = [a for j in jobs for a in j.ins]
        self.outs = [o for j in jobs for o in j.outs]
        self.two_phase = any(getattr(j, "two_phase", False) for j in jobs)
        self.aliases = {}
        at_in = at_out = 0
        for j in jobs:
            self.aliases.update({at_in + i: at_out + o for i, o in j.aliases.items()})
            at_in, at_out = at_in + len(j.ins), at_out + len(j.outs)

    def scratch(self):
        return [s for j in self.jobs for s in j.scratch()]

    def _each(self, ins, outs, sems):
        at_in = at_out = at_sem = 0
        for j in self.jobs:
            n_in, n_out, n_sem = len(j.ins), len(j.outs), len(j.scratch())
            yield j, ins[at_in:at_in + n_in], outs[at_out:at_out + n_out], sems[at_sem:at_sem + n_sem]
            at_in, at_out, at_sem = at_in + n_in, at_out + n_out, at_sem + n_sem

    def start(self, ins, outs, sems):
        for j, i, o, s in self._each(ins, outs, sems):
            j.start(i, o, s)

    def turn(self, ins, outs, sems):
        for j, i, o, s in self._each(ins, outs, sems):
            if getattr(j, "two_phase", False):
                j.turn(i, o, s)

    def finish(self, ins, outs, sems):
        for j, i, o, s in self._each(ins, outs, sems):
            j.finish(i, o, s)

    def split(self, results):
        at, parts = 0, []
        for j in self.jobs:
            parts.append(results[at:at + len(j.outs)])
            at += len(j.outs)
        return parts


def _remote(src, dst, send, recv, idx, to):
    return pltpu.make_async_remote_copy(src_ref=src, dst_ref=dst, send_sem=send.at[idx], recv_sem=recv.at[idx],
                                        device_id=to, device_id_type=MESH)


def _spread_copies(ins, outs, send, recv, base=0):
    x, y, c, chips = _place()
    me = (x, y, c)
    peers = [(x, y, 1 - c)] + [(*chip, c) for chip in chips]
    started, awaited = [], []
    for i, (src, dst) in enumerate(zip(ins, outs)):
        for k, peer in enumerate(peers):
            started.append(_remote(src, dst.at[_slot(me)], send, recv, (base + i, k), peer))
            awaited.append(_remote(src, dst.at[_slot(peer)], send, recv, (base + i, k), peer))
    return started, awaited


def _forward_copies(ins, outs, send, recv, base=0):
    x, y, c, chips = _place()
    started, awaited = [], []
    for i, buf in enumerate(outs):
        for j, chip in enumerate(chips):
            mine, theirs = buf.at[_slot((*chip, c))], buf.at[_slot((*chip, 1 - c))]
            started.append(_remote(mine, mine, send, recv, (base + i, j), (x, y, 1 - c)))
            awaited.append(_remote(theirs, theirs, send, recv, (base + i, j), (x, y, 1 - c)))
    return started, awaited


def _own_block_copies(ins, outs, sems):
    x, y, c, _ = _place()
    return [pltpu.make_async_copy(src, dst.at[_slot((x, y, c))], sems.at[i])
            for i, (src, dst) in enumerate(zip(ins, outs))]


def gather_spread_job(shards):
    outs = [jax.ShapeDtypeStruct((N_DEV, *a.shape), a.dtype) for a in shards]
    return _Job(shards, outs, (len(shards), 4), _spread_copies, local=_own_block_copies)


def gather_forward_job(fulls):
    outs = [jax.ShapeDtypeStruct(a.shape, a.dtype) for a in fulls]
    return _Job(fulls, outs, (len(fulls), 3), _forward_copies, aliases={i: i for i in range(len(fulls))})


TURN_EIGHTHS = 6


class _GatherJob:
    two_phase = True

    def __init__(self, shards):
        self.ins = list(shards)
        self.outs = [jax.ShapeDtypeStruct((N_DEV, *a.shape), a.dtype) for a in shards]
        self.aliases = {}

    def scratch(self):
        n = len(self.ins)
        return [pltpu.SemaphoreType.DMA((n, 4)), pltpu.SemaphoreType.DMA((n, 4)),
                pltpu.SemaphoreType.DMA((n, 3)), pltpu.SemaphoreType.DMA((n, 3)), pltpu.SemaphoreType.DMA((n,))]

    def start(self, ins, outs, sems):
        for cp in _own_block_copies(ins, outs, sems[4]) + _spread_copies(ins, outs, sems[0], sems[1])[0]:
            cp.start()

    def turn(self, ins, outs, sems):
        for cp in _spread_copies(ins, outs, sems[0], sems[1])[1]:
            cp.wait_recv()
        for cp in _forward_copies(outs, outs, sems[2], sems[3])[0]:
            cp.start()

    def finish(self, ins, outs, sems):
        handed_on, arriving = _forward_copies(outs, outs, sems[2], sems[3])
        for cp in arriving:
            cp.wait_recv()
        for cp in _spread_copies(ins, outs, sems[0], sems[1])[0] + handed_on:
            cp.wait_send()
        for cp in _own_block_copies(ins, outs, sems[4]):
            cp.wait()


def swap_job(gs):
    def copies(ins, outs, send, recv):
        x, y, c, _ = _place()
        started, awaited = [], []
        for i, (g, r1) in enumerate(zip(ins, outs)):
            for q in range(N_CHIP):
                started.append(_remote(g.at[2 * q + (1 - c)], r1.at[q], send, recv, (i, q), (x, y, 1 - c)))
                awaited.append(_remote(g.at[2 * q + c], r1.at[q], send, recv, (i, q), (x, y, 1 - c)))
        return started, awaited

    outs = [jax.ShapeDtypeStruct((N_CHIP, *g.shape[1:]), g.dtype) for g in gs]
    return _Job(gs, outs, (len(gs), N_CHIP), copies)


def chip_exchange_job(ps, rows=None, into=None):
    n = len(ps)

    def copies(ins, outs, send, recv):
        x, y, c, chips = _place()
        started, awaited = [], []
        for i, (p, r2) in enumerate(zip(ins[:n], outs)):
            for k, chip in enumerate(chips):
                src, mine, dst = p.at[2 * chip[0] + chip[1]], p.at[2 * x + y], r2.at[k]
                if rows is not None:
                    src, mine, dst = (t.at[pl.ds(rows[0], rows[1])] for t in (src, mine, dst))
                started.append(_remote(src, dst, send, recv, (i, k), (*chip, c)))
                awaited.append(_remote(mine, dst, send, recv, (i, k), (*chip, c)))
        return started, awaited

    outs = [jax.ShapeDtypeStruct((3, *p.shape[1:]), p.dtype) for p in ps]
    if into is None:
        return _Job(ps, outs, (n, 3), copies)
    return _Job(list(ps) + list(into), outs, (n, 3), copies, aliases={n + i: i for i in range(n)})


def _call(body, job, *, name, grid, in_specs, out_specs, out_shape, args, scratch_shapes=(), vmem=VMEM_LIMIT):
    if job is None:
        res = pl.pallas_call(
            body, name=name, grid=grid, in_specs=in_specs, out_specs=out_specs, out_shape=out_shape,
            scratch_shapes=list(scratch_shapes), compiler_params=_params(("arbitrary",) * len(grid), vmem),
        )(*args)
        return res, []
    n_in, n_out, n_scr = len(in_specs), len(out_specs), len(scratch_shapes)
    j_in, j_out = len(job.ins), len(job.outs)

    def with_copies(*refs):
        at = 0
        ins = refs[at:at + n_in]; at += n_in
        jins = refs[at:at + j_in]; at += j_in
        outs = refs[at:at + n_out]; at += n_out
        jouts = refs[at:at + j_out]; at += j_out
        scr = refs[at:at + n_scr]; at += n_scr
        sems = refs[at:]
        ids = [pl.program_id(d) for d in range(len(grid))]
        first = functools.reduce(jnp.logical_and, [i == 0 for i in ids])
        last = functools.reduce(jnp.logical_and, [i == n - 1 for i, n in zip(ids, grid)])

        @pl.when(first)
        def _():
            job.start(jins, jouts, sems)

        if getattr(job, "two_phase", False):
            steps, at = 1, 0
            for i, n in zip(ids, grid):
                steps, at = steps * n, at * n + i

            @pl.when(at == (TURN_EIGHTHS * steps) // 8)
            def _():
                job.turn(jins, jouts, sems)

        body(*ins, *outs, *scr)

        @pl.when(last)
        def _():
            job.finish(jins, jouts, sems)

    any_spec = pl.BlockSpec(memory_space=pl.ANY)
    res = pl.pallas_call(
        with_copies, name=name, grid=grid,
        in_specs=list(in_specs) + [any_spec] * j_in, out_specs=list(out_specs) + [any_spec] * j_out,
        out_shape=list(out_shape) + list(job.outs),
        input_output_aliases={n_in + i: n_out + o for i, o in job.aliases.items()},
        scratch_shapes=list(scratch_shapes) + job.scratch(),
        compiler_params=_params(("arbitrary",) * len(grid), vmem),
    )(*args, *job.ins)
    return res[:n_out], res[n_out:]


def run_job(job, name):
    def body(*refs):
        j_in, j_out = len(job.ins), len(job.outs)
        ins, outs, sems = refs[:j_in], refs[j_in:j_in + j_out], refs[j_in + j_out:]
        job.start(ins, outs, sems)
        job.finish(ins, outs, sems)

    any_spec = pl.BlockSpec(memory_space=pl.ANY)
    return pl.pallas_call(
        body, name=name, in_specs=[any_spec] * len(job.ins), out_specs=[any_spec] * len(job.outs),
        out_shape=list(job.outs), input_output_aliases=dict(job.aliases), scratch_shapes=job.scratch(),
    )(*job.ins)


def pair_sum(core, g, r1, name):
    _, rows, cols = g.shape
    rb = next(cand for cand in range(min(rows, 512), 0, -16) if rows % cand == 0)

    def body(core_ref, g_ref, r1_ref, p_ref, own_ref):
        del core_ref
        x, y, _, _ = _place()
        s = g_ref[0].astype(F32) + r1_ref[0].astype(F32)
        p_ref[0] = s.astype(BF16)

        @pl.when(pl.program_id(1) == 2 * x + y)
        def _():
            own_ref[...] = s

    chunk = (1, rb, cols)
    return pl.pallas_call(
        body, name=name,
        grid_spec=pltpu.PrefetchScalarGridSpec(
            num_scalar_prefetch=1, grid=(rows // rb, N_CHIP),
            in_specs=[pl.BlockSpec(chunk, lambda i, q, core_ref: (2 * q + core_ref[0], i, 0)),
                      pl.BlockSpec(chunk, lambda i, q, core_ref: (q, i, 0))],
            out_specs=[pl.BlockSpec(chunk, lambda i, q, core_ref: (q, i, 0)),
                       pl.BlockSpec((rb, cols), lambda i, q, core_ref: (i, 0))]),
        out_shape=[jax.ShapeDtypeStruct((N_CHIP, rows, cols), BF16), jax.ShapeDtypeStruct((rows, cols), F32)],
        compiler_params=_params(("arbitrary", "arbitrary")),
    )(core, g, r1)


def sum_devices(a, name):
    def body(a_ref, o_ref):
        acc = a_ref[0]
        for d in range(1, N_DEV):
            acc = acc + a_ref[d]
        o_ref[...] = acc

    return pl.pallas_call(body, name=name, out_shape=jax.ShapeDtypeStruct(a.shape[1:], F32))(a)


def _adam_update(w, g, m, v):
    nm = ADAM_B1 * m + (1.0 - ADAM_B1) * g
    nv = ADAM_B2 * v + (1.0 - ADAM_B2) * (g * g)
    m_hat = nm / (1.0 - ADAM_B1 ** ADAM_STEP)
    v_hat = nv / (1.0 - ADAM_B2 ** ADAM_STEP)
    return -ADAM_LR * (m_hat / (jnp.sqrt(v_hat) + ADAM_EPS) + ADAM_WD * w), nm, nv


def adamw(w, g, m, v, name, others=None):
    rows, cols = w.shape
    rb = rows
    for cand in range(min(rows, 512), 7, -8):
        if rows % cand == 0 and cand % 8 == 0:
            rb = cand
            break

    def body(*refs):
        if others is None:
            w_ref, g_ref, m_ref, v_ref, d_ref, nm_ref, nv_ref = refs
            gg = g_ref[...]
        else:
            w_ref, g_ref, m_ref, v_ref, r2_ref, go_ref, d_ref, nm_ref, nv_ref = refs
            gg = g_ref[...]
            for k in range(3):
                gg = gg + r2_ref[k].astype(F32)
            go_ref[...] = gg
        d_ref[...], nm_ref[...], nv_ref[...] = _adam_update(w_ref[...], gg, m_ref[...], v_ref[...])

    spec = pl.BlockSpec((rb, cols), lambda i: (i, 0))
    out = jax.ShapeDtypeStruct((rows, cols), F32)
    in_specs, args = [spec] * 4, [w, g, m, v]
    if others is not None:
        in_specs.append(pl.BlockSpec((3, rb, cols), lambda i: (0, i, 0)))
        args.append(others)
    n_out = 3 if others is None else 4
    res = pl.pallas_call(
        body, name=name, grid=(rows // rb,), in_specs=in_specs, out_specs=[spec] * n_out,
        out_shape=[out] * n_out, compiler_params=_params(("parallel",)),
    )(*args)
    return (g, *res) if others is None else tuple(res)


def adamw_small(items, name):
    n = len(items)

    def body(*refs):
        ins, outs = refs[:4 * n], refs[4 * n:]
        for i in range(n):
            w_ref, g_ref, m_ref, v_ref = ins[4 * i:4 * i + 4]
            d_ref, nm_ref, nv_ref = outs[3 * i:3 * i + 3]
            d_ref[...], nm_ref[...], nv_ref[...] = _adam_update(w_ref[...], g_ref[...], m_ref[...], v_ref[...])

    res = pl.pallas_call(
        body, name=name,
        out_shape=[jax.ShapeDtypeStruct(w.shape, F32) for w, _, _, _ in items for _ in range(3)],
    )(*[t for item in items for t in item])
    return [tuple(res[3 * i:3 * i + 3]) for i in range(n)]


def ada_fwd(c_all, w_cols, b_cols, name):
    def body(c_ref, w_ref, b_ref, cond_ref, mod_ref):
        cc = c_ref[...]
        cond = (cc * _sigmoid(cc)).astype(BF16)
        cond_ref[...] = cond
        mod_ref[...] = _dot(cond, w_ref[...].astype(BF16)) + b_ref[...]

    n, cols = c_all.shape[0], w_cols.shape[1]
    return pl.pallas_call(
        body, name=name,
        out_shape=[jax.ShapeDtypeStruct(c_all.shape, BF16), jax.ShapeDtypeStruct((n, cols), F32)],
        compiler_params=_params(),
    )(c_all, w_cols, b_cols)


def ada_bwd(cond_all, dmod_cols, name):
    def body(c_ref, d_ref, gw_ref, gb_ref):
        d = d_ref[...]
        gw_ref[...] = _dot_tn(c_ref[...], d.astype(BF16))
        gb_ref[...] = jnp.sum(d, axis=0, keepdims=True)

    dm, cols = cond_all.shape[1], dmod_cols.shape[1]
    return pl.pallas_call(
        body, name=name,
        out_shape=[jax.ShapeDtypeStruct((dm, cols), F32), jax.ShapeDtypeStruct((1, cols), F32)],
        compiler_params=_params(),
    )(cond_all, dmod_cols)


MXU_COLS = 256
FFN_CHUNK = 4 * MXU_COLS


def _hidden_chunks(ff):
    assert ff % MXU_COLS == 0
    return [(at, min(FFN_CHUNK, ff - at)) for at in range(0, ff, FFN_CHUNK)]


def _mod_spec(tiles_per_seq, dm):
    return pl.BlockSpec((1, 1, dm), lambda i: (i // tiles_per_seq, 0, 0))


def ffn_loss(x, sh, sc, gt, wgu, wd, ln_g, ln_b, target, seq, name):
    tokens, dm = x.shape
    ff = wgu.shape[1]
    chunks = _hidden_chunks(ff)
    tm = min(FFN_WIDE_TILE, seq)
    tiles_per_seq = seq // tm

    def body(x_ref, sh_ref, sc_ref, gt_ref, wgu_ref, wd_ref, lg_ref, lb_ref, t_ref,
             dr_ref, df_ref, gu_ref, a_ref, h_ref, loss_ref, dln_ref, dgt_ref):
        i = pl.program_id(0)
        xx = x_ref[...]
        h = (xx * (1.0 + sc_ref[0]) + sh_ref[0]).astype(BF16)
        h_ref[...] = h
        acc = jnp.zeros((tm, dm), F32)
        for at, wdt in chunks:
            gk = _dot_nt(h, wgu_ref[0, at:at + wdt, :])
            uk = _dot_nt(h, wgu_ref[1, at:at + wdt, :])
            gu_ref[0, :, at:at + wdt] = gk.astype(BF16)
            gu_ref[1, :, at:at + wdt] = uk.astype(BF16)
            a = (gk * _sigmoid(gk) * uk).astype(BF16)
            a_ref[:, at:at + wdt] = a
            acc = acc + _dot(a, wd_ref[at:at + wdt, :])
        half_gate = 0.5 * (1.0 + gt_ref[0])
        xhat, rstd = _ln_stats(DN_ALPHA * xx + half_gate * acc)
        err = xhat * lg_ref[...] + lb_ref[...] - t_ref[...]
        dr, dgain, dbias = _ln_bwd_normalized(err * (1.0 / dm), xhat, rstd, lg_ref[...])
        dr_ref[...] = dr
        df_ref[...] = (half_gate * dr).astype(BF16)

        @pl.when(i == 0)
        def _():
            loss_ref[...] = jnp.zeros_like(loss_ref)
            dln_ref[...] = jnp.zeros_like(dln_ref)

        @pl.when(i % tiles_per_seq == 0)
        def _():
            dgt_ref[...] = jnp.zeros_like(dgt_ref)

        loss_ref[...] += jnp.full((1, 128), (0.5 / dm) * jnp.sum(err * err), F32)
        dln_ref[0:1, :] += dgain
        dln_ref[1:2, :] += dbias
        dgt_ref[0] += jnp.sum(dr * (0.5 * acc), axis=0, keepdims=True)

    tile = pl.BlockSpec((tm, dm), lambda i: (i, 0))
    mod = _mod_spec(tiles_per_seq, dm)
    res, _ = _call(
        body, None, name=name, grid=(tokens // tm,),
        in_specs=[tile, mod, mod, mod, _const_spec(wgu.shape), _const_spec(wd.shape),
                  _const_spec((1, dm)), _const_spec((1, dm)), tile],
        out_specs=[tile, tile, pl.BlockSpec((2, tm, ff), lambda i: (0, i, 0)), pl.BlockSpec((tm, ff), lambda i: (i, 0)),
                   tile, pl.BlockSpec((1, 128), lambda i: (0, 0)), pl.BlockSpec((2, dm), lambda i: (0, 0)), mod],
        out_shape=[jax.ShapeDtypeStruct((tokens, dm), F32), jax.ShapeDtypeStruct((tokens, dm), BF16),
                   jax.ShapeDtypeStruct((2, tokens, ff), BF16), jax.ShapeDtypeStruct((tokens, ff), BF16),
                   jax.ShapeDtypeStruct((tokens, dm), BF16), jax.ShapeDtypeStruct((1, 128), F32),
                   jax.ShapeDtypeStruct((2, dm), F32), jax.ShapeDtypeStruct((tokens // seq, 1, dm), F32)],
        args=(x, sh, sc, gt, wgu, wd, ln_g, ln_b, target), vmem=FFN_WIDE_VMEM)
    return res


def ffn_up(x, sh, sc, wgu, seq, name, job=None):
    tokens, dm = x.shape
    ff = wgu.shape[1]
    chunks = _hidden_chunks(ff)
    tm = min(FFN_FWD_TILE, seq)

    def body(x_ref, sh_ref, sc_ref, wgu_ref, gu_ref, a_ref, h_ref):
        h = (x_ref[...] * (1.0 + sc_ref[0]) + sh_ref[0]).astype(BF16)
        h_ref[...] = h
        for at, wdt in chunks:
            gk = _dot_nt(h, wgu_ref[0, at:at + wdt, :])
            uk = _dot_nt(h, wgu_ref[1, at:at + wdt, :])
            gu_ref[0, :, at:at + wdt] = gk.astype(BF16)
            gu_ref[1, :, at:at + wdt] = uk.astype(BF16)
            a_ref[:, at:at + wdt] = (gk * _sigmoid(gk) * uk).astype(BF16)

    tile = pl.BlockSpec((tm, dm), lambda i: (i, 0))
    mod = _mod_spec(seq // tm, dm)
    return _call(
        body, job, name=name, grid=(tokens // tm,),
        in_specs=[tile, mod, mod, _const_spec(wgu.shape)],
        out_specs=[pl.BlockSpec((2, tm, ff), lambda i: (0, i, 0)), pl.BlockSpec((tm, ff), lambda i: (i, 0)), tile],
        out_shape=[jax.ShapeDtypeStruct((2, tokens, ff), BF16), jax.ShapeDtypeStruct((tokens, ff), BF16),
                   jax.ShapeDtypeStruct((tokens, dm), BF16)],
        args=(x, sh, sc, wgu))


def ffn_down(x, a, gt, wd, ln_g, ln_b, seq, name, job=None):
    tokens, dm = x.shape
    ff = wd.shape[0]
    chunks = _hidden_chunks(ff)
    tm = min(FFN_FWD_TILE, seq)

    def body(x_ref, a_ref, gt_ref, wd_ref, lg_ref, lb_ref, xo_ref, r_ref, f_ref):
        acc = jnp.zeros((tm, dm), F32)
        for at, wdt in chunks:
            acc = acc + _dot(a_ref[:, at:at + wdt], wd_ref[at:at + wdt, :])
        f_ref[...] = acc.astype(BF16)
        r = DN_ALPHA * x_ref[...] + (0.5 * (1.0 + gt_ref[0])) * acc
        r_ref[...] = r
        xhat, _ = _ln_stats(r)
        xo_ref[...] = xhat * lg_ref[...] + lb_ref[...]

    tile = pl.BlockSpec((tm, dm), lambda i: (i, 0))
    return _call(
        body, job, name=name, grid=(tokens // tm,),
        in_specs=[tile, pl.BlockSpec((tm, ff), lambda i: (i, 0)), _mod_spec(seq // tm, dm), _const_spec(wd.shape),
                  _const_spec((1, dm)), _const_spec((1, dm))],
        out_specs=[tile, tile, tile],
        out_shape=[jax.ShapeDtypeStruct((tokens, dm), F32), jax.ShapeDtypeStruct((tokens, dm), F32),
                   jax.ShapeDtypeStruct((tokens, dm), BF16)],
        args=(x, a, gt, wd, ln_g, ln_b))


def ffn_down_loss(x, a, gt, wd, ln_g, ln_b, target, seq, name):
    tokens, dm = x.shape
    ff = wd.shape[0]
    chunks = _hidden_chunks(ff)
    tm = min(FFN_FWD_TILE, seq)
    tiles_per_seq = seq // tm

    def body(x_ref, a_ref, gt_ref, wd_ref, lg_ref, lb_ref, t_ref, dr_ref, df_ref, loss_ref, dln_ref, dgt_ref):
        i = pl.program_id(0)
        acc = jnp.zeros((tm, dm), F32)
        for at, wdt in chunks:
            acc = acc + _dot(a_ref[:, at:at + wdt], wd_ref[at:at + wdt, :])
        half_gate = 0.5 * (1.0 + gt_ref[0])
        xhat, rstd = _ln_stats(DN_ALPHA * x_ref[...] + half_gate * acc)
        err = xhat * lg_ref[...] + lb_ref[...] - t_ref[...]
        dr, dgain, dbias = _ln_bwd_normalized(err * (1.0 / dm), xhat, rstd, lg_ref[...])
        dr_ref[...] = dr
        df_ref[...] = (half_gate * dr).astype(BF16)

        @pl.when(i == 0)
        def _():
            loss_ref[...] = jnp.zeros_like(loss_ref)
            dln_ref[...] = jnp.zeros_like(dln_ref)

        @pl.when(i % tiles_per_seq == 0)
        def _():
            dgt_ref[...] = jnp.zeros_like(dgt_ref)

        loss_ref[...] += jnp.full((1, 128), (0.5 / dm) * jnp.sum(err * err), F32)
        dln_ref[0:1, :] += dgain
        dln_ref[1:2, :] += dbias
        dgt_ref[0] += jnp.sum(dr * (0.5 * acc), axis=0, keepdims=True)

    tile = pl.BlockSpec((tm, dm), lambda i: (i, 0))
    mod = _mod_spec(tiles_per_seq, dm)
    res, _ = _call(
        body, None, name=name, grid=(tokens // tm,),
        in_specs=[tile, pl.BlockSpec((tm, ff), lambda i: (i, 0)), mod, _const_spec(wd.shape),
                  _const_spec((1, dm)), _const_spec((1, dm)), tile],
        out_specs=[tile, tile, pl.BlockSpec((1, 128), lambda i: (0, 0)), pl.BlockSpec((2, dm), lambda i: (0, 0)), mod],
        out_shape=[jax.ShapeDtypeStruct((tokens, dm), F32), jax.ShapeDtypeStruct((tokens, dm), BF16),
                   jax.ShapeDtypeStruct((1, 128), F32), jax.ShapeDtypeStruct((2, dm), F32),
                   jax.ShapeDtypeStruct((tokens // seq, 1, dm), F32)],
        args=(x, a, gt, wd, ln_g, ln_b, target))
    return res


def ffn_bwd(dr, df, x, gu, sc, wgu, wd, seq, name, job=None):
    tokens, dm = x.shape
    ff = wgu.shape[1]
    chunks = _hidden_chunks(ff)
    tm = min(FFN_WIDE_TILE, seq)
    tiles_per_seq = seq // tm
    nseq = tokens // seq

    def body(dr_ref, df_ref, x_ref, gu_ref, sc_ref, wgu_ref, wd_ref, dx_ref, dgu_ref, dmod_ref):
        @pl.when(pl.program_id(0) % tiles_per_seq == 0)
        def _():
            dmod_ref[...] = jnp.zeros_like(dmod_ref)

        df = df_ref[...]
        dh = jnp.zeros((tm, dm), F32)
        for at, wdt in chunks:
            cols = slice(at, at + wdt)
            da = _dot_nt(df, wd_ref[cols, :])
            gk = gu_ref[0, :, cols].astype(F32)
            uk = gu_ref[1, :, cols].astype(F32)
            sg = _sigmoid(gk)
            sil = gk * sg
            du = (da * sil).astype(BF16)
            dg = (da * uk * (sg * (1.0 + gk * (1.0 - sg)))).astype(BF16)
            dgu_ref[0, :, cols] = dg
            dgu_ref[1, :, cols] = du
            dh = dh + _dot(dg, wgu_ref[0, cols, :]) + _dot(du, wgu_ref[1, cols, :])
        dx_ref[...] = DN_ALPHA * dr_ref[...] + dh * (1.0 + sc_ref[0])
        dmod_ref[0, 0:1, :] += jnp.sum(dh, axis=0, keepdims=True)
        dmod_ref[0, 1:2, :] += jnp.sum(dh * x_ref[...], axis=0, keepdims=True)

    tile = pl.BlockSpec((tm, dm), lambda i: (i, 0))
    gu_spec = pl.BlockSpec((2, tm, ff), lambda i: (0, i, 0))
    return _call(
        body, job, name=name, grid=(tokens // tm,),
        in_specs=[tile, tile, tile, gu_spec, _mod_spec(tiles_per_seq, dm), _const_spec(wgu.shape), _const_spec(wd.shape)],
        out_specs=[tile, gu_spec, pl.BlockSpec((1, 2, dm), lambda i: (i // tiles_per_seq, 0, 0))],
        out_shape=[jax.ShapeDtypeStruct((tokens, dm), F32), jax.ShapeDtypeStruct((2, tokens, ff), BF16),
                   jax.ShapeDtypeStruct((nseq, 2, dm), F32)],
        args=(dr, df, x, gu, sc, wgu, wd), vmem=FFN_WIDE_VMEM)


def tn_matmul(a, b, name, job=None, b_cols=None, a_width=None):
    na, tokens, k_all = a.shape
    kk = k_all if a_width is None else a_width
    nka = k_all // kk
    assert nka * kk == k_all
    nb, _, cc = b.shape
    col = 0
    if b_cols is not None:
        col, cc = b_cols
    tt = tokens
    while 4 * tt * (kk + cc) + 8 * kk * cc > TN_VMEM_BUDGET and tt % 2 == 0 and tt > 256:
        tt //= 2
    steps = tokens // tt

    def body(a_ref, b_ref, o_ref, *acc):
        if steps == 1:
            o_ref[0, 0, 0] = _dot_tn(a_ref[0], b_ref[0]).astype(BF16)
            return
        acc_ref, = acc
        t = pl.program_id(3)

        @pl.when(t == 0)
        def _():
            acc_ref[...] = jnp.zeros_like(acc_ref)

        acc_ref[...] += _dot_tn(a_ref[0], b_ref[0])

        @pl.when(t == steps - 1)
        def _():
            o_ref[0, 0, 0] = acc_ref[...].astype(BF16)

    return _call(
        body, job, name=name, grid=(na, nka, nb, steps),
        in_specs=[pl.BlockSpec((1, tt, kk), lambda i, s, j, t: (i, t, s)),
                  pl.BlockSpec((1, tt, cc), lambda i, s, j, t: (j, t, col))],
        out_specs=[pl.BlockSpec((1, 1, 1, kk, cc), lambda i, s, j, t: (i, s, j, 0, 0))],
        out_shape=[jax.ShapeDtypeStruct((na, nka, nb, kk, cc), BF16)],
        scratch_shapes=[] if steps == 1 else [pltpu.VMEM((kk, cc), F32)], args=(a, b))


def proj_fwd(x1, sh, sc, w_in, seq, name, job=None):
    tokens, dm = x1.shape
    tm = min(MIX_TILE, seq)
    tiles_per_seq = seq // tm
    widths = [N_Q_HEADS * HEAD_DIM, N_KV_HEADS * HEAD_DIM, N_KV_HEADS * HEAD_DIM, 512, 512, 512]
    assert sum(widths) == w_in.shape[0]

    def body(x_ref, sh_ref, sc_ref, w_ref, *outs):
        h = (x_ref[...] * (1.0 + sc_ref[0]) + sh_ref[0]).astype(BF16)
        proj = _dot_nt(h, w_ref[...])
        at = 0
        for o_ref, wdt in zip(outs, widths):
            o_ref[...] = proj[:, at:at + wdt].astype(o_ref.dtype)
            at += wdt

    tile = pl.BlockSpec((tm, dm), lambda i: (i, 0))
    mod = _mod_spec(tiles_per_seq, dm)
    return _call(
        body, job, name=name, grid=(tokens // tm,),
        in_specs=[tile, mod, mod, _const_spec(w_in.shape)],
        out_specs=[pl.BlockSpec((tm, wdt), lambda i: (i, 0)) for wdt in widths],
        out_shape=[jax.ShapeDtypeStruct((tokens, wdt), F32 if i < 3 else BF16) for i, wdt in enumerate(widths)],
        args=(x1, sh, sc, w_in))


LANES = 2 * HEAD_DIM


def _head_lane(shape):
    return lax.broadcasted_iota(jnp.int32, shape, 1) % HEAD_DIM


def _lane_half(shape):
    return lax.broadcasted_iota(jnp.int32, shape, 1) // HEAD_DIM


def _swap_rot(v):
    lane = _head_lane(v.shape)
    half = ROT_DIM // 2
    return jnp.where(lane < half, pltpu.roll(v, LANES - half, 1),
                     jnp.where(lane < ROT_DIM, pltpu.roll(v, half, 1), 0.0))


def _rope(v, cos_t, sin_t):
    return v * cos_t + _swap_rot(v) * sin_t


def _unrope(dv, cos_t, sin_t):
    return dv * cos_t + _swap_rot(dv * sin_t)


def _both_halves(t, g):
    return jnp.where(_lane_half(t.shape) == g, t, pltpu.roll(t, HEAD_DIM, 1))


def _fold_halves(t, g):
    return jnp.where(_lane_half(t.shape) == g, t + pltpu.roll(t, HEAD_DIM, 1), 0.0)


def _stack_heads(blocks):
    rows = []
    for blk in blocks:
        half = _lane_half(blk.shape)
        rows += [jnp.where(half == 0, blk, 0.0), jnp.where(half == 1, blk, 0.0)]
    return jnp.concatenate(rows, axis=0)


def _unstack_heads(t, j):
    lo = t[(2 * j) * ATTN_BLOCK:(2 * j + 1) * ATTN_BLOCK]
    hi = t[(2 * j + 1) * ATTN_BLOCK:(2 * j + 2) * ATTN_BLOCK]
    return jnp.where(_lane_half(lo.shape) == 0, lo, hi)


def _band_mask(q0, w0):
    rows, cols = GQA_GROUP * ATTN_BLOCK, 2 * ATTN_BLOCK
    qi = lax.broadcasted_iota(jnp.int32, (rows, cols), 0) % ATTN_BLOCK + q0
    ki = lax.broadcasted_iota(jnp.int32, (rows, cols), 1) + w0
    diff = qi - ki
    return (diff >= 0) & (diff < ATTN_BLOCK)


def _attn_specs(seq):
    q_spec = pl.BlockSpec((seq, GQA_GROUP * HEAD_DIM), lambda b, g: (b, g))
    kv_spec = pl.BlockSpec((seq, LANES), lambda b, g: (b, 0))
    sink_spec = pl.BlockSpec((1, GQA_GROUP * ATTN_BLOCK, 1), lambda b, g: (g, 0, 0))
    return q_spec, kv_spec, sink_spec


def _block_starts(n):
    q0 = pl.multiple_of(n * ATTN_BLOCK, ATTN_BLOCK)
    w0 = pl.multiple_of(jnp.maximum(n - 1, 0) * ATTN_BLOCK, ATTN_BLOCK)
    return q0, w0


def _stacked_queries(ref, rows):
    return _stack_heads([ref[rows, j * LANES:(j + 1) * LANES] for j in range(2)]).astype(BF16)


def _sink_columns(sinks):
    return jnp.repeat(sinks.reshape(N_KV_HEADS, GQA_GROUP), ATTN_BLOCK, axis=1)[:, :, None]


def _probs_spec(nblk):
    return pl.BlockSpec((1, 1, nblk, GQA_GROUP * ATTN_BLOCK, 2 * ATTN_BLOCK), lambda b, g: (b, g, 0, 0, 0))


def _sink_probs_spec():
    return pl.BlockSpec((1, 1, GQA_GROUP * ATTN_BLOCK, LANES), lambda b, g: (b, g, 0, 0))


def attn_fwd(q, k, v, cos_t, sin_t, sinks, seq, name, job=None):
    tokens = q.shape[0]
    nblk = seq // ATTN_BLOCK
    assert nblk >= 2
    scale = HEAD_DIM ** -0.5

    nseq = tokens // seq
    rows_stacked = GQA_GROUP * ATTN_BLOCK
    assert nblk <= LANES

    def body(q_ref, k_ref, v_ref, cos_ref, sin_ref, sink_ref, o_ref, qr_ref, p_ref, ps_ref, kd_ref, vd_ref):
        g = pl.program_id(1)
        kd_ref[...] = _both_halves(_rope(k_ref[...].astype(F32), cos_ref[...], sin_ref[...]), g).astype(BF16)
        vd_ref[...] = _both_halves(v_ref[...].astype(F32), g).astype(BF16)
        sink = sink_ref[0]
        lane = lax.broadcasted_iota(jnp.int32, (rows_stacked, LANES), 1)

        ps_ref[...] = jnp.zeros_like(ps_ref)

        def block(n, carry):
            q0, w0 = _block_starts(n)
            rows, win = pl.ds(q0, ATTN_BLOCK), pl.ds(w0, 2 * ATTN_BLOCK)
            blocks = []
            for j in range(2):
                qr = _rope(q_ref[rows, j * LANES:(j + 1) * LANES].astype(F32), cos_ref[rows, :], sin_ref[rows, :]).astype(BF16)
                qr_ref[rows, j * LANES:(j + 1) * LANES] = qr
                blocks.append(qr)
            qs = _stack_heads(blocks)
            s = _dot_nt(qs, kd_ref[win, :]) * scale
            s = jnp.where(_band_mask(q0, w0), s, NEG_BIG)
            m = jnp.maximum(jnp.max(s, axis=-1, keepdims=True), sink)
            p = jnp.exp(s - m)
            e_sink = jnp.exp(sink - m)
            inv = pl.reciprocal(jnp.sum(p, axis=-1, keepdims=True) + e_sink, approx=True)
            pn = (p * inv).astype(BF16)
            p_ref[0, 0, n] = pn
            out = _dot(pn, vd_ref[win, :])
            for j in range(2):
                o_ref[rows, j * LANES:(j + 1) * LANES] = _unstack_heads(out, j).astype(o_ref.dtype)
            ps_ref[0, 0] = jnp.where(lane == n, e_sink * inv, ps_ref[0, 0])
            return carry

        lax.fori_loop(0, nblk, block, 0, unroll=2)

    q_spec, kv_spec, sink_spec = _attn_specs(seq)
    return _call(
        body, job, name=name, grid=(nseq, N_KV_HEADS),
        in_specs=[q_spec, kv_spec, kv_spec, kv_spec, kv_spec, sink_spec],
        out_specs=[q_spec, q_spec, _probs_spec(nblk), _sink_probs_spec()],
        out_shape=[jax.ShapeDtypeStruct(q.shape, BF16), jax.ShapeDtypeStruct(q.shape, BF16),
                   jax.ShapeDtypeStruct((nseq, N_KV_HEADS, nblk, rows_stacked, 2 * ATTN_BLOCK), BF16),
                   jax.ShapeDtypeStruct((nseq, N_KV_HEADS, rows_stacked, LANES), F32)],
        scratch_shapes=[pltpu.VMEM((seq, LANES), BF16), pltpu.VMEM((seq, LANES), BF16)],
        args=(q, k, v, cos_t, sin_t, _sink_columns(sinks)))


def attn_bwd(qr, k, v, do, probs, sink_probs, cos_t, sin_t, seq, name, job=None):
    tokens = qr.shape[0]
    nseq = tokens // seq
    nblk = seq // ATTN_BLOCK
    assert nblk >= 2
    rows_stacked = GQA_GROUP * ATTN_BLOCK
    scale = HEAD_DIM ** -0.5

    def body(q_ref, k_ref, v_ref, do_ref, p_ref, ps_ref, cos_ref, sin_ref, dq_ref, dk_ref, dv_ref, ds_ref,
             kd_ref, vd_ref, dkd_ref, dvd_ref, acc_ref):
        g = pl.program_id(1)
        kd_ref[...] = _both_halves(_rope(k_ref[...].astype(F32), cos_ref[...], sin_ref[...]), g).astype(BF16)
        vd_ref[...] = _both_halves(v_ref[...].astype(F32), g).astype(BF16)
        dkd_ref[...] = jnp.zeros_like(dkd_ref)
        dvd_ref[...] = jnp.zeros_like(dvd_ref)
        acc_ref[...] = jnp.zeros_like(acc_ref)
        lane = lax.broadcasted_iota(jnp.int32, (rows_stacked, LANES), 1)

        def block(n, carry):
            q0, w0 = _block_starts(n)
            rows, win = pl.ds(q0, ATTN_BLOCK), pl.ds(w0, 2 * ATTN_BLOCK)
            qs = _stacked_queries(q_ref, rows)
            dos = _stacked_queries(do_ref, rows)
            kw, vw = kd_ref[win, :], vd_ref[win, :]
            pn16 = p_ref[0, 0, n]
            pn = pn16.astype(F32)
            dvd_ref[win, :] += _dot_tn(pn16, dos)
            dp = _dot_nt(dos, vw)
            delta = jnp.sum(dp * pn, axis=-1, keepdims=True)
            ds = (pn * (dp - delta)).astype(BF16)
            dqs = _dot(ds, kw) * scale
            dkd_ref[win, :] += _dot_tn(ds, qs) * scale
            cos_b, sin_b = cos_ref[rows, :], sin_ref[rows, :]
            for j in range(2):
                dq_ref[rows, j * LANES:(j + 1) * LANES] = _unrope(_unstack_heads(dqs, j), cos_b, sin_b).astype(BF16)
            acc_ref[...] += jnp.where(lane == n, ps_ref[0, 0] * delta, 0.0)
            return carry

        lax.fori_loop(0, nblk // 2, lambda i, carry: block(2 * i + 1, block(2 * i, carry)), 0)
        ds_ref[0, 0] = -jnp.sum(acc_ref[...], axis=-1, keepdims=True)
        dk_g = _unrope(_fold_halves(dkd_ref[...], g), cos_ref[...], sin_ref[...])
        dv_g = _fold_halves(dvd_ref[...], g)

        @pl.when(g == 0)
        def _():
            dk_ref[...] = dk_g
            dv_ref[...] = dv_g

        @pl.when(g != 0)
        def _():
            dk_ref[...] += dk_g
            dv_ref[...] += dv_g

    q_spec, kv_spec, _ = _attn_specs(seq)
    return _call(
        body, job, name=name, grid=(nseq, N_KV_HEADS),
        in_specs=[q_spec, kv_spec, kv_spec, q_spec, _probs_spec(nblk), _sink_probs_spec(), kv_spec, kv_spec],
        out_specs=[q_spec, kv_spec, kv_spec, pl.BlockSpec((1, 1, rows_stacked, 1), lambda b, g: (b, g, 0, 0))],
        out_shape=[jax.ShapeDtypeStruct(qr.shape, BF16), jax.ShapeDtypeStruct(k.shape, F32),
                   jax.ShapeDtypeStruct(k.shape, F32), jax.ShapeDtypeStruct((nseq, N_KV_HEADS, rows_stacked, 1), F32)],
        scratch_shapes=[pltpu.VMEM((seq, LANES), BF16), pltpu.VMEM((seq, LANES), BF16),
                        pltpu.VMEM((seq, LANES), F32), pltpu.VMEM((seq, LANES), F32),
                        pltpu.VMEM((rows_stacked, LANES), F32)],
        args=(qr, k, v, do, probs, sink_probs, cos_t, sin_t))


CONV_COLS = 128


def _shift_down(z, by):
    t = lax.broadcasted_iota(jnp.int32, z.shape, 0)
    return jnp.where(t >= by, pltpu.roll(z, by, 0), 0.0)


def _shift_up(z, by):
    n = z.shape[0]
    t = lax.broadcasted_iota(jnp.int32, z.shape, 0)
    return jnp.where(t < n - by, pltpu.roll(z, n - by, 0), 0.0)


def conv_fwd(u, bg, cg, conv_w, seq, name):
    tokens, width = u.shape

    def body(u_ref, bg_ref, cg_ref, w_ref, o_ref):
        z = cg_ref[...].astype(F32) * u_ref[...].astype(F32)
        yy = w_ref[2:3, :] * z + w_ref[1:2, :] * _shift_down(z, 1) + w_ref[0:1, :] * _shift_down(z, 2)
        o_ref[...] = (bg_ref[...].astype(F32) * yy).astype(BF16)

    col = pl.BlockSpec((seq, CONV_COLS), lambda j, b: (b, j))
    return pl.pallas_call(
        body, name=name, grid=(width // CONV_COLS, tokens // seq),
        in_specs=[col, col, col, pl.BlockSpec((CONV_TAPS, CONV_COLS), lambda j, b: (0, j))],
        out_specs=col, out_shape=jax.ShapeDtypeStruct((tokens, width), BF16),
        compiler_params=_params(("parallel", "parallel")),
    )(u, bg, cg, conv_w)


def conv_bwd(dout, u, bg, cg, conv_w, seq, name):
    tokens, width = u.shape

    def body(do_ref, u_ref, bg_ref, cg_ref, w_ref, du_ref, dbg_ref, dcg_ref, dw_ref):
        uu, cg_v, do = u_ref[...].astype(F32), cg_ref[...].astype(F32), do_ref[...].astype(F32)
        z = cg_v * uu
        z1, z2 = _shift_down(z, 1), _shift_down(z, 2)
        yy = w_ref[2:3, :] * z + w_ref[1:2, :] * z1 + w_ref[0:1, :] * z2
        dbg_ref[...] = (do * yy).astype(BF16)
        dyy = do * bg_ref[...].astype(F32)
        dz = w_ref[2:3, :] * dyy + w_ref[1:2, :] * _shift_up(dyy, 1) + w_ref[0:1, :] * _shift_up(dyy, 2)
        du_ref[...] = (dz * cg_v).astype(BF16)
        dcg_ref[...] = (dz * uu).astype(BF16)

        @pl.when(pl.program_id(1) == 0)
        def _():
            dw_ref[...] = jnp.zeros_like(dw_ref)

        dw_ref[0:1, :] += jnp.sum(dyy * z2, axis=0, keepdims=True)
        dw_ref[1:2, :] += jnp.sum(dyy * z1, axis=0, keepdims=True)
        dw_ref[2:3, :] += jnp.sum(dyy * z, axis=0, keepdims=True)

    col = pl.BlockSpec((seq, CONV_COLS), lambda j, b: (b, j))
    w_spec = pl.BlockSpec((CONV_TAPS, CONV_COLS), lambda j, b: (0, j))
    act = jax.ShapeDtypeStruct((tokens, width), BF16)
    return pl.pallas_call(
        body, name=name, grid=(width // CONV_COLS, tokens // seq),
        in_specs=[col, col, col, col, w_spec], out_specs=[col, col, col, w_spec],
        out_shape=[act, act, act, jax.ShapeDtypeStruct((CONV_TAPS, width), F32)],
        compiler_params=_params(("parallel", "arbitrary")),
    )(dout, u, bg, cg, conv_w)


def out_fwd(x1, attn, conv, gt, w_out, ln_g, ln_b, seq, name, job=None):
    tokens, dm = x1.shape
    half = attn.shape[1]
    tm = min(MIX_TILE, seq)
    tiles_per_seq = seq // tm

    def body(x_ref, a_ref, c_ref, gt_ref, w_ref, lg_ref, lb_ref, xo_ref, r_ref, mi_ref, mix_ref):
        mixin = jnp.concatenate([a_ref[...], c_ref[...]], axis=1).astype(BF16)
        mi_ref[...] = mixin
        mix = _dot(mixin, w_ref[...])
        mix_ref[...] = mix.astype(BF16)
        r = DN_ALPHA * x_ref[...] + (1.0 + gt_ref[0]) * mix
        r_ref[...] = r
        xhat, _ = _ln_stats(r)
        xo_ref[...] = xhat * lg_ref[...] + lb_ref[...]

    tile = pl.BlockSpec((tm, dm), lambda i: (i, 0))
    htile = pl.BlockSpec((tm, half), lambda i: (i, 0))
    return _call(
        body, job, name=name, grid=(tokens // tm,),
        in_specs=[tile, htile, htile, _mod_spec(tiles_per_seq, dm), _const_spec(w_out.shape),
                  _const_spec((1, dm)), _const_spec((1, dm))],
        out_specs=[tile, tile, tile, tile],
        out_shape=[jax.ShapeDtypeStruct((tokens, dm), F32), jax.ShapeDtypeStruct((tokens, dm), F32),
                   jax.ShapeDtypeStruct((tokens, dm), BF16), jax.ShapeDtypeStruct((tokens, dm), BF16)],
        args=(x1, attn, conv, gt, w_out, ln_g, ln_b))


def out_bwd(dy, r, mix, gt, w_out, ln_g, seq, name, job=None):
    tokens, dm = r.shape
    half = dm // 2
    tm = min(MIX_TILE, seq)
    tiles_per_seq = seq // tm
    nseq = tokens // seq

    def body(dy_ref, r_ref, mix_ref, gt_ref, w_ref, lg_ref, dres_ref, da_ref, dc_ref, dmix_ref, dln_ref, dgt_ref):
        i = pl.program_id(0)
        dr, dgain, dbias = _ln_bwd(dy_ref[...], r_ref[...], lg_ref[...])

        @pl.when(i == 0)
        def _():
            dln_ref[...] = jnp.zeros_like(dln_ref)

        @pl.when(i % tiles_per_seq == 0)
        def _():
            dgt_ref[...] = jnp.zeros_like(dgt_ref)

        dln_ref[0:1, :] += dgain
        dln_ref[1:2, :] += dbias
        dgt_ref[0] += jnp.sum(dr * mix_ref[...].astype(F32), axis=0, keepdims=True)
        dres_ref[...] = DN_ALPHA * dr
        dmix = ((1.0 + gt_ref[0]) * dr).astype(BF16)
        dmix_ref[...] = dmix
        dmixin = _dot_nt(dmix, w_ref[...])
        da_ref[...] = dmixin[:, :half].astype(BF16)
        dc_ref[...] = dmixin[:, half:].astype(BF16)

    tile = pl.BlockSpec((tm, dm), lambda i: (i, 0))
    htile = pl.BlockSpec((tm, half), lambda i: (i, 0))
    return _call(
        body, job, name=name, grid=(tokens // tm,),
        in_specs=[tile, tile, tile, _mod_spec(tiles_per_seq, dm), _const_spec(w_out.shape), _const_spec((1, dm))],
        out_specs=[tile, htile, htile, tile, pl.BlockSpec((2, dm), lambda i: (0, 0)),
                   pl.BlockSpec((1, 1, dm), lambda i: (i // tiles_per_seq, 0, 0))],
        out_shape=[jax.ShapeDtypeStruct((tokens, dm), F32), jax.ShapeDtypeStruct((tokens, half), BF16),
                   jax.ShapeDtypeStruct((tokens, half), BF16), jax.ShapeDtypeStruct((tokens, dm), BF16),
                   jax.ShapeDtypeStruct((2, dm), F32), jax.ShapeDtypeStruct((nseq, 1, dm), F32)],
        args=(dy, r, mix, gt, w_out, ln_g))


def proj_bwd(parts, dres, x1, sh, sc, w_in, r_prev, f_prev, gt_prev, ln_g_prev, seq, name, job=None):
    tokens, dm = x1.shape
    tm = min(MIX_TILE, seq)
    tiles_per_seq = seq // tm
    nseq = tokens // seq
    widths = [p.shape[1] for p in parts]
    total = sum(widths)

    def body(*refs):
        part_refs = refs[:6]
        (dres_ref, x_ref, sh_ref, sc_ref, w_ref, r_ref, f_ref, gt_ref, lg_ref,
         dr_ref, df_ref, dproj_ref, h_ref, dmod_ref, dln_ref, dgt_ref) = refs[6:]
        i = pl.program_id(0)
        dproj = jnp.concatenate([p[...].astype(BF16) for p in part_refs], axis=1)
        dproj_ref[...] = dproj
        dh = _dot(dproj, w_ref[...])
        xx = x_ref[...]
        one_sc = 1.0 + sc_ref[0]
        h_ref[...] = (xx * one_sc + sh_ref[0]).astype(BF16)
        dr, dgain, dbias = _ln_bwd(dres_ref[...] + dh * one_sc, r_ref[...], lg_ref[...])
        dr_ref[...] = dr
        df_ref[...] = ((0.5 * (1.0 + gt_ref[0])) * dr).astype(BF16)

        @pl.when(i == 0)
        def _():
            dln_ref[...] = jnp.zeros_like(dln_ref)

        @pl.when(i % tiles_per_seq == 0)
        def _():
            dmod_ref[...] = jnp.zeros_like(dmod_ref)
            dgt_ref[...] = jnp.zeros_like(dgt_ref)

        dmod_ref[0, 0:1, :] += jnp.sum(dh, axis=0, keepdims=True)
        dmod_ref[0, 1:2, :] += jnp.sum(dh * xx, axis=0, keepdims=True)
        dln_ref[0:1, :] += dgain
        dln_ref[1:2, :] += dbias
        dgt_ref[0] += jnp.sum(dr * (0.5 * f_ref[...].astype(F32)), axis=0, keepdims=True)

    tile = pl.BlockSpec((tm, dm), lambda i: (i, 0))
    mod = _mod_spec(tiles_per_seq, dm)
    return _call(
        body, job, name=name, grid=(tokens // tm,),
        in_specs=[pl.BlockSpec((tm, wdt), lambda i: (i, 0)) for wdt in widths]
        + [tile, tile, mod, mod, _const_spec(w_in.shape), tile, tile, mod, _const_spec((1, dm))],
        out_specs=[tile, tile, pl.BlockSpec((tm, total), lambda i: (i, 0)), tile,
                   pl.BlockSpec((1, 2, dm), lambda i: (i // tiles_per_seq, 0, 0)),
                   pl.BlockSpec((2, dm), lambda i: (0, 0)), mod],
        out_shape=[jax.ShapeDtypeStruct((tokens, dm), F32), jax.ShapeDtypeStruct((tokens, dm), BF16),
                   jax.ShapeDtypeStruct((tokens, total), BF16), jax.ShapeDtypeStruct((tokens, dm), BF16),
                   jax.ShapeDtypeStruct((nseq, 2, dm), F32), jax.ShapeDtypeStruct((2, dm), F32),
                   jax.ShapeDtypeStruct((nseq, 1, dm), F32)],
        args=(*parts, dres, x1, sh, sc, w_in, r_prev, f_prev, gt_prev, ln_g_prev))


def _rope_tables(positions):
    half = ROT_DIM // 2
    inv_freq = jnp.power(jnp.float32(ROPE_THETA), -jnp.arange(0, ROT_DIM, 2, dtype=F32) / ROT_DIM)
    lane = jnp.arange(LANES) % HEAD_DIM
    freq = jnp.where(lane < ROT_DIM, inv_freq[lane % half], 0.0)
    sign = jnp.where(lane < half, -1.0, 1.0).astype(F32)
    ang = positions.astype(F32)[:, None] * freq[None, :]
    return jnp.cos(ang), sign[None, :] * jnp.sin(ang)


def kernel(x, c, positions, w_ada, b_ada, ffn1_w_gate_up, ffn1_w_down, ln1_g, ln1_b, w_in, conv_w, attn_sinks, w_out, ln2_g, ln2_b, ffn2_w_gate_up, ffn2_w_down, ln3_g, ln3_b, loss_target, m_w_ada, m_b_ada, m_ffn1_w_gate_up, m_ffn1_w_down, m_ln1_g, m_ln1_b, m_w_in, m_conv_w, m_attn_sinks, m_w_out, m_ln2_g, m_ln2_b, m_ffn2_w_gate_up, m_ffn2_w_down, m_ln3_g, m_ln3_b, v_w_ada, v_b_ada, v_ffn1_w_gate_up, v_ffn1_w_down, v_ln1_g, v_ln1_b, v_w_in, v_conv_w, v_attn_sinks, v_w_out, v_ln2_g, v_ln2_b, v_ffn2_w_gate_up, v_ffn2_w_down, v_ln3_g, v_ln3_b):
    nseq, seq, dm = x.shape
    tokens = nseq * seq
    dev = 4 * lax.axis_index("x") + 2 * lax.axis_index("y") + lax.axis_index("c")
    core = lax.axis_index("c").astype(jnp.int32).reshape(1)
    ada_cols = w_ada.shape[2]
    ff = ffn1_w_down.shape[1] * N_DEV
    fc = ff // 4
    in_cols = w_in.shape[2]
    conv_cols = conv_w.shape[2]

    def t_bf16(w):
        return w[0].T.astype(BF16)

    c_all, convw_all = all_gather([c, conv_w[0]], "gather_cond")
    c_all = c_all.reshape(N_DEV * nseq, dm)
    convw_full = convw_all.transpose(1, 0, 2).reshape(CONV_TAPS, N_DEV * conv_cols)

    b_cols = lax.dynamic_slice(b_ada, (0, dev * ada_cols), (1, ada_cols))
    cond_all, mod_cols = ada_fwd(c_all, w_ada[0], b_cols, "ada_fwd")
    wgu1, mod_all = all_gather([t_bf16(ffn1_w_gate_up), mod_cols], "gather_ffn1")
    wgu1 = wgu1.reshape(2, ff, dm)
    mod = lax.dynamic_slice(mod_all, (0, dev * nseq, 0), (N_DEV, nseq, ada_cols))
    mod = mod.transpose(1, 0, 2).reshape(nseq, 9, 1, dm)
    sh1, sc1, g1, sh2, sc2, g2, sh3, sc3, g3 = [mod[:, i] for i in range(9)]

    x0 = x.reshape(tokens, dm)
    (gu1, a1, h1), (wd1, wout) = ffn_up(x0, sh1, sc1, wgu1, seq, "ffn1_up",
                                        job=_GatherJob([ffn1_w_down[0].astype(BF16), w_out[0].astype(BF16)]))
    wd1, wout = wd1.reshape(ff, dm), wout.reshape(dm, dm)
    (x1, r1, f1), (win,) = ffn_down(x0, a1, g1, wd1, ln1_g, ln1_b, seq, "ffn1_down", job=_GatherJob([t_bf16(w_in)]))
    win = win.reshape(N_DEV * in_cols, dm)
    (q, k, v, u, bg, cg), wd2_spread = proj_fwd(x1, sh2, sc2, win, seq, "proj_fwd",
                                                job=gather_spread_job([ffn2_w_down[0].astype(BF16)]))
    cos_t, sin_t = _rope_tables(positions.reshape(tokens))
    sinks = attn_sinks[0]
    (attn, q_rot, probs, sink_probs), wgu2_spread = attn_fwd(q, k, v, cos_t, sin_t, sinks, seq, "attn_fwd",
                                                             job=gather_spread_job([t_bf16(ffn2_w_gate_up)]))
    conv = conv_fwd(u, bg, cg, convw_full, seq, "conv_fwd")
    (x2, r2, mixin, mix), (wd2, wgu2) = out_fwd(x1, attn, conv, g2, wout, ln2_g, ln2_b, seq, "out_fwd",
                                                job=gather_forward_job(wd2_spread + wgu2_spread))
    wd2, wgu2 = wd2.reshape(ff, dm), wgu2.reshape(2, ff, dm)
    target = loss_target.reshape(tokens, dm)
    (gu3, a3, h3), _ = ffn_up(x2, sh3, sc3, wgu2, seq, "ffn2_up")
    dr3, df3, loss_part, dln3, dg3 = ffn_down_loss(x2, a3, g3, wd2, ln3_g, ln3_b, target, seq, "ffn2_down")

    (dx2, dgu3, dmod3), _ = ffn_bwd(dr3, df3, x2, gu3, sc3, wgu2, wd2, seq, "ffn2_bwd")
    pair = 2 * fc
    g_wd2 = tn_matmul(a3[None], df3[None], "ffn2_dwd", a_width=pair)[0][0].reshape(N_DEV, ff // N_DEV, dm)
    g_wgu2 = tn_matmul(dgu3, h3[None], "ffn2_dwgu", a_width=pair)[0][0].reshape(N_DEV, fc, dm)
    (dres2, dattn, dconv, dmix, dln2, dg2), swapped = out_bwd(dx2, r2, mix, g2, wout, ln2_g, seq, "out_bwd",
                                                              job=swap_job([g_wgu2, g_wd2]))
    p_wgu2, own_wgu2 = pair_sum(core, g_wgu2, swapped[0], "pair_wgu2")
    p_wd2, own_wd2 = pair_sum(core, g_wd2, swapped[1], "pair_wd2")
    du, dbg, dcg, dconvw = conv_bwd(dconv, u, bg, cg, convw_full, seq, "conv_bwd")
    (dq, dk, dv, dsink_rows), (far_wd2,) = attn_bwd(
        q_rot, k, v, dattn, probs, sink_probs, cos_t, sin_t, seq, "attn_bwd", job=chip_exchange_job([p_wd2]))
    parts = [dq, dk, dv, du, dbg, dcg]
    (dr1, df1, dproj, h2, dmod2, dln1, dg1), far_top = proj_bwd(
        parts, dres2, x1, sh2, sc2, win, r1, f1, g1, ln1_g, seq, "proj_bwd",
        job=chip_exchange_job([p_wgu2], rows=(0, fc // 2)))
    (dx0, dgu1, dmod1), _ = ffn_bwd(dr1, df1, x0, gu1, sc1, wgu1, wd1, seq, "ffn1_bwd")

    dmod = jnp.concatenate([dmod1, dg1, dmod2, dg2, dmod3, dg3], axis=1).reshape(nseq, 9 * dm)
    half = dm // 2
    jobs = _Jobs([gather_spread_job([dmod]),
                  chip_exchange_job([p_wgu2], rows=(fc // 2, fc // 2), into=far_top)])
    (g_wd1,), res = tn_matmul(a1[None], df1[None], "ffn1_dwd", job=jobs, a_width=pair)
    dmod_spread, (far_wgu2,) = jobs.split(res)
    g_wd1 = g_wd1.reshape(N_DEV, ff // N_DEV, dm)
    jobs = _Jobs([swap_job([g_wd1]), gather_forward_job(dmod_spread)])
    (g_l,), res = tn_matmul(dgu1, h1[None], "ffn1_dwgu_l", job=jobs, b_cols=(0, half), a_width=pair)
    (sw_wd1,), (dmod_all,) = jobs.split(res)
    g_l = g_l.reshape(N_DEV, fc, half)
    p_wd1, own_wd1 = pair_sum(core, g_wd1, sw_wd1, "pair_wd1")
    jobs = _Jobs([chip_exchange_job([p_wd1]), swap_job([g_l])])
    (g_r,), res = tn_matmul(dgu1, h1[None], "ffn1_dwgu_r", job=jobs, b_cols=(1, half), a_width=pair)
    (far_wd1,), (sw_l,) = jobs.split(res)
    g_r = g_r.reshape(N_DEV, fc, half)
    p_l, own_l = pair_sum(core, g_l, sw_l, "pair_wgu1_l")

    dmod_cols = lax.dynamic_slice(dmod_all.reshape(N_DEV * nseq, 9 * dm), (0, dev * ada_cols), (N_DEV * nseq, ada_cols))
    grad_w_ada, gb_cols = ada_bwd(cond_all, dmod_cols, "ada_bwd")
    dsinks = jnp.sum(dsink_rows.reshape(nseq, N_Q_HEADS, ATTN_BLOCK), axis=(0, 2))
    small = jnp.zeros((8, dm), F32)
    small = small.at[0:2].set(dln1).at[2:4].set(dln2).at[4:6].set(dln3)
    small = small.at[6, 0:N_Q_HEADS].set(dsinks).at[7, 0].set(loss_part[0, 0])

    jobs = _Jobs([chip_exchange_job([p_l]), swap_job([g_r]), gather_spread_job([small, dconvw, gb_cols])])
    (g_win,), res = tn_matmul(dproj[None], h2[None], "dwin", job=jobs)
    (far_l,), (sw_r,), small_spread = jobs.split(res)
    g_win = g_win.reshape(N_DEV, in_cols, dm)
    p_r, own_r = pair_sum(core, g_r, sw_r, "pair_wgu1_r")
    jobs = _Jobs([chip_exchange_job([p_r]), swap_job([g_win]), gather_forward_job(small_spread)])
    (g_wout,), res = tn_matmul(mixin[None], dmix[None], "dwout", job=jobs)
    (far_r,), (sw_win,), (small_all, dconvw_all, gb_all) = jobs.split(res)
    g_wout = g_wout.reshape(N_DEV, dm // N_DEV, dm)
    p_win, own_win = pair_sum(core, g_win, sw_win, "pair_win")

    given = dict(w_ada=(w_ada, m_w_ada, v_w_ada), b_ada=(b_ada, m_b_ada, v_b_ada),
                 ffn1_w_gate_up=(ffn1_w_gate_up, m_ffn1_w_gate_up, v_ffn1_w_gate_up),
                 ffn1_w_down=(ffn1_w_down, m_ffn1_w_down, v_ffn1_w_down),
                 ln1_g=(ln1_g, m_ln1_g, v_ln1_g), ln1_b=(ln1_b, m_ln1_b, v_ln1_b),
                 w_in=(w_in, m_w_in, v_w_in), conv_w=(conv_w, m_conv_w, v_conv_w),
                 attn_sinks=(attn_sinks, m_attn_sinks, v_attn_sinks), w_out=(w_out, m_w_out, v_w_out),
                 ln2_g=(ln2_g, m_ln2_g, v_ln2_g), ln2_b=(ln2_b, m_ln2_b, v_ln2_b),
                 ffn2_w_gate_up=(ffn2_w_gate_up, m_ffn2_w_gate_up, v_ffn2_w_gate_up),
                 ffn2_w_down=(ffn2_w_down, m_ffn2_w_down, v_ffn2_w_down),
                 ln3_g=(ln3_g, m_ln3_g, v_ln3_g), ln3_b=(ln3_b, m_ln3_b, v_ln3_b))
    transposed = ("ffn1_w_gate_up", "ffn2_w_gate_up", "w_in")

    def big_adamw(nm, grad, far=None):
        flip = nm in transposed
        w2, m2, v2 = [t[0].T if flip else t[0] for t in given[nm]]
        return [t.T[None] if flip else t[None] for t in adamw(w2, grad, m2, v2, "adamw_" + nm, others=far)]

    jobs = _Jobs([chip_exchange_job([p_win]), swap_job([g_wout])])
    (far_win,), (sw_wout,) = jobs.split(run_job(jobs, "rs_tail_win"))
    p_wout, own_wout = pair_sum(core, g_wout, sw_wout, "pair_wout")
    (far_wout,) = run_job(chip_exchange_job([p_wout]), "rs_tail_wout")

    grads = {
        "ffn1_w_gate_up": jnp.concatenate([own_l, own_r], axis=1), "ffn1_w_down": own_wd1,
        "w_in": own_win, "w_out": own_wout, "ffn2_w_gate_up": own_wgu2, "ffn2_w_down": own_wd2,
    }
    others = {"ffn1_w_gate_up": jnp.concatenate([far_l, far_r], axis=2), "ffn1_w_down": far_wd1,
              "w_in": far_win, "w_out": far_wout, "ffn2_w_gate_up": far_wgu2, "ffn2_w_down": far_wd2}
    results = {"w_ada": big_adamw("w_ada", grad_w_ada)}
    for nm in grads:
        results[nm] = big_adamw(nm, grads[nm], others[nm])

    small_sum = sum_devices(small_all, "sum_small")
    dconvw_sum = sum_devices(dconvw_all, "sum_convw")
    loss = small_sum[7, 0]
    grads["b_ada"] = gb_all.reshape(1, N_DEV * ada_cols)
    grads["conv_w"] = lax.dynamic_slice(dconvw_sum, (0, dev * conv_cols), (CONV_TAPS, conv_cols))
    grads["attn_sinks"] = small_sum[6:7, 0:N_Q_HEADS]
    for i, nm in enumerate(["ln1_g", "ln1_b", "ln2_g", "ln2_b", "ln3_g", "ln3_b"]):
        grads[nm] = small_sum[i:i + 1]

    order = ["w_ada", "b_ada", "ffn1_w_gate_up", "ffn1_w_down", "ln1_g", "ln1_b", "w_in", "conv_w", "attn_sinks",
             "w_out", "ln2_g", "ln2_b", "ffn2_w_gate_up", "ffn2_w_down", "ln3_g", "ln3_b"]
    small_names = [nm for nm in order if nm not in results]
    items = []
    for nm in small_names:
        shape = given[nm][0].shape
        two_d = (shape[-2], shape[-1])
        items.append((given[nm][0].reshape(two_d), grads[nm].reshape(two_d), *[t.reshape(two_d) for t in given[nm][1:]]))
    for nm, res in zip(small_names, adamw_small(items, "adamw_small")):
        shape = given[nm][0].shape
        results[nm] = [grads[nm].reshape(shape)] + [t.reshape(shape) for t in res]
    grad_x = dx0.reshape(nseq, seq, dm)
    return (loss, grad_x, *[results[nm][i] for i in range(4) for nm in order])
```

```python
import functools

import jax
import jax.numpy as jnp
from jax import lax
from jax.experimental import pallas as pl
from jax.experimental.pallas import tpu as pltpu

F32 = jnp.float32
BF16 = jnp.bfloat16
MESH = pl.DeviceIdType.MESH

N_DEV = 8
N_CHIP = 4
HEAD_DIM = 64
N_Q_HEADS = 8
N_KV_HEADS = 2
GQA_GROUP = N_Q_HEADS // N_KV_HEADS
ATTN_BLOCK = 128
ROT_DIM = 16
ROPE_THETA = 500000.0
CONV_TAPS = 3
LN_EPS = 1e-5
DN_ALPHA = 2.0 ** 0.25
ADAM_LR = 0.001
ADAM_B1 = 0.9
ADAM_B2 = 0.999
ADAM_EPS = 1e-08
ADAM_WD = 0.01
ADAM_STEP = 10
NEG_BIG = -1e30

VMEM_LIMIT = 56 * 1024 * 1024
FFN_FWD_TILE = 512
MIX_TILE = 512
FFN_WIDE_TILE = 512
FFN_WIDE_VMEM = 62 * 1024 * 1024
TN_VMEM_BUDGET = 36 * 1024 * 1024
TN_WHOLE_BUDGET = 56 * 1024 * 1024


def _params(semantics=None, vmem=VMEM_LIMIT):
    return pltpu.CompilerParams(dimension_semantics=semantics, vmem_limit_bytes=vmem)


def _dot(a, b):
    return jnp.dot(a, b, preferred_element_type=F32)


def _dot_nt(a, b):
    return lax.dot_general(a, b, (((1,), (1,)), ((), ())), preferred_element_type=F32)


def _dot_tn(a, b):
    return lax.dot_general(a, b, (((0,), (0,)), ((), ())), preferred_element_type=F32)


def _sigmoid(x):
    return pl.reciprocal(1.0 + jnp.exp(-x), approx=True)


def _ln_stats(r):
    mu = jnp.mean(r, axis=-1, keepdims=True)
    d = r - mu
    var = jnp.mean(d * d, axis=-1, keepdims=True)
    rstd = lax.rsqrt(var + LN_EPS)
    return d * rstd, rstd


def _ln_bwd(dy, r, g):
    return _ln_bwd_normalized(dy, *_ln_stats(r), g)


def _ln_bwd_normalized(dy, xhat, rstd, g):
    dxhat = dy * g
    c1 = jnp.mean(dxhat, axis=-1, keepdims=True)
    c2 = jnp.mean(dxhat * xhat, axis=-1, keepdims=True)
    dr = rstd * (dxhat - c1 - xhat * c2)
    return dr, jnp.sum(dy * xhat, axis=0, keepdims=True), jnp.sum(dy, axis=0, keepdims=True)


def _const_spec(shape):
    nd = len(shape)
    return pl.BlockSpec(shape, lambda *_: (0,) * nd, pipeline_mode=pl.Buffered(1))


def all_gather(arrs, name):
    n = len(arrs)

    def body(*refs):
        ins, outs = refs[:n], refs[n:2 * n]
        send_sems, recv_sems, local_sems = refs[2 * n:]
        x, y, c = lax.axis_index("x"), lax.axis_index("y"), lax.axis_index("c")
        me, sibling = (x, y, c), (x, y, 1 - c)
        chips = [(1 - x, y), (x, 1 - y), (1 - x, 1 - y)]

        def slot(i, p):
            return outs[i].at[4 * p[0] + 2 * p[1] + p[2]]

        def copy(i, k, block, to, src=None):
            return pltpu.make_async_remote_copy(
                src_ref=slot(i, block) if src is None else src, dst_ref=slot(i, block),
                send_sem=send_sems.at[i, k], recv_sem=recv_sems.at[i, k],
                device_id=to, device_id_type=MESH)

        mine = [pltpu.make_async_copy(ins[i], slot(i, me), local_sems.at[i]) for i in range(n)]
        for cp in mine:
            cp.start()
        first = []
        for i in range(n):
            first.append(copy(i, 0, me, sibling, src=ins[i]))
            first += [copy(i, 1 + j, me, (*chip, c), src=ins[i]) for j, chip in enumerate(chips)]
        for cp in first:
            cp.start()
        passed = []
        for i in range(n):
            for j, chip in enumerate(chips):
                copy(i, 1 + j, (*chip, c), me).wait_recv()
                cp = copy(i, 4 + j, (*chip, c), sibling)
                cp.start()
                passed.append(cp)
        for i in range(n):
            copy(i, 0, sibling, me).wait_recv()
            for j, chip in enumerate(chips):
                copy(i, 4 + j, (*chip, 1 - c), me).wait_recv()
        for cp in first + passed:
            cp.wait_send()
        for cp in mine:
            cp.wait()

    any_spec = pl.BlockSpec(memory_space=pl.ANY)
    return pl.pallas_call(
        body, name=name,
        out_shape=[jax.ShapeDtypeStruct((N_DEV, *a.shape), a.dtype) for a in arrs],
        in_specs=[any_spec] * n, out_specs=[any_spec] * n,
        scratch_shapes=[pltpu.SemaphoreType.DMA((n, 7)), pltpu.SemaphoreType.DMA((n, 7)),
                        pltpu.SemaphoreType.DMA((n,))],
    )(*arrs)


def _place():
    x, y, c = lax.axis_index("x"), lax.axis_index("y"), lax.axis_index("c")
    return x, y, c, [(1 - x, y), (x, 1 - y), (1 - x, 1 - y)]


def _slot(p):
    return 4 * p[0] + 2 * p[1] + p[2]


class _Job:
    def __init__(self, ins, outs, nsem, copies, aliases=None, local=None):
        self.ins, self.outs, self.nsem, self.copies = list(ins), list(outs), nsem, copies
        self.aliases = aliases or {}
        self.local = local

    def scratch(self):
        s = [pltpu.SemaphoreType.DMA(self.nsem), pltpu.SemaphoreType.DMA(self.nsem)]
        if self.local is not None:
            s.append(pltpu.SemaphoreType.DMA((len(self.ins),)))
        return s

    def start(self, ins, outs, sems):
        if self.local is not None:
            for cp in self.local(ins, outs, sems[2]):
                cp.start()
        for cp in self.copies(ins, outs, sems[0], sems[1])[0]:
            cp.start()

    def finish(self, ins, outs, sems):
        started, awaited = self.copies(ins, outs, sems[0], sems[1])
        for cp in awaited:
            cp.wait_recv()
        for cp in started:
            cp.wait_send()
        if self.local is not None:
            for cp in self.local(ins, outs, sems[2]):
                cp.wait()


class _Jobs:
    def __init__(self, jobs):
        self.jobs = jobs
        self.ins = [a for j in jobs for a in j.ins]
        self.outs = [o for j in jobs for o in j.outs]
        self.two_phase = any(getattr(j, "two_phase", False) for j in jobs)
        self.aliases = {}
        at_in = at_out = 0
        for j in jobs:
            self.aliases.update({at_in + i: at_out + o for i, o in j.aliases.items()})
            at_in, at_out = at_in + len(j.ins), at_out + len(j.outs)

    def scratch(self):
        return [s for j in self.jobs for s in j.scratch()]

    def _each(self, ins, outs, sems):
        at_in = at_out = at_sem = 0
        for j in self.jobs:
            n_in, n_out, n_sem = len(j.ins), len(j.outs), len(j.scratch())
            yield j, ins[at_in:at_in + n_in], outs[at_out:at_out + n_out], sems[at_sem:at_sem + n_sem]
            at_in, at_out, at_sem = at_in + n_in, at_out + n_out, at_sem + n_sem

    def start(self, ins, outs, sems):
        for j, i, o, s in self._each(ins, outs, sems):
            j.start(i, o, s)

    def turn(self, ins, outs, sems):
        for j, i, o, s in self._each(ins, outs, sems):
            if getattr(j, "two_phase", False):
                j.turn(i, o, s)

    def finish(self, ins, outs, sems):
        for j, i, o, s in self._each(ins, outs, sems):
            j.finish(i, o, s)

    def split(self, results):
        at, parts = 0, []
        for j in self.jobs:
            parts.append(results[at:at + len(j.outs)])
            at += len(j.outs)
        return parts


def _remote(src, dst, send, recv, idx, to):
    return pltpu.make_async_remote_copy(src_ref=src, dst_ref=dst, send_sem=send.at[idx], recv_sem=recv.at[idx],
                                        device_id=to, device_id_type=MESH)


def _spread_copies(ins, outs, send, recv, base=0):
    x, y, c, chips = _place()
    me = (x, y, c)
    peers = [(x, y, 1 - c)] + [(*chip, c) for chip in chips]
    started, awaited = [], []
    for i, (src, dst) in enumerate(zip(ins, outs)):
        for k, peer in enumerate(peers):
            started.append(_remote(src, dst.at[_slot(me)], send, recv, (base + i, k), peer))
            awaited.append(_remote(src, dst.at[_slot(peer)], send, recv, (base + i, k), peer))
    return started, awaited


def _forward_copies(ins, outs, send, recv, base=0):
    x, y, c, chips = _place()
    started, awaited = [], []
    for i, buf in enumerate(outs):
        for j, chip in enumerate(chips):
            mine, theirs = buf.at[_slot((*chip, c))], buf.at[_slot((*chip, 1 - c))]
            started.append(_remote(mine, mine, send, recv, (base + i, j), (x, y, 1 - c)))
            awaited.append(_remote(theirs, theirs, send, recv, (base + i, j), (x, y, 1 - c)))
    return started, awaited


def _own_block_copies(ins, outs, sems):
    x, y, c, _ = _place()
    return [pltpu.make_async_copy(src, dst.at[_slot((x, y, c))], sems.at[i])
            for i, (src, dst) in enumerate(zip(ins, outs))]


def gather_spread_job(shards):
    outs = [jax.ShapeDtypeStruct((N_DEV, *a.shape), a.dtype) for a in shards]
    return _Job(shards, outs, (len(shards), 4), _spread_copies, local=_own_block_copies)


def gather_forward_job(fulls):
    outs = [jax.ShapeDtypeStruct(a.shape, a.dtype) for a in fulls]
    return _Job(fulls, outs, (len(fulls), 3), _forward_copies, aliases={i: i for i in range(len(fulls))})


TURN_EIGHTHS = 6


class _GatherJob:
    two_phase = True

    def __init__(self, shards):
        self.ins = list(shards)
        self.outs = [jax.ShapeDtypeStruct((N_DEV, *a.shape), a.dtype) for a in shards]
        self.aliases = {}

    def scratch(self):
        n = len(self.ins)
        return [pltpu.SemaphoreType.DMA((n, 4)), pltpu.SemaphoreType.DMA((n, 4)),
                pltpu.SemaphoreType.DMA((n, 3)), pltpu.SemaphoreType.DMA((n, 3)), pltpu.SemaphoreType.DMA((n,))]

    def start(self, ins, outs, sems):
        for cp in _own_block_copies(ins, outs, sems[4]) + _spread_copies(ins, outs, sems[0], sems[1])[0]:
            cp.start()

    def turn(self, ins, outs, sems):
        for cp in _spread_copies(ins, outs, sems[0], sems[1])[1]:
            cp.wait_recv()
        for cp in _forward_copies(outs, outs, sems[2], sems[3])[0]:
            cp.start()

    def finish(self, ins, outs, sems):
        handed_on, arriving = _forward_copies(outs, outs, sems[2], sems[3])
        for cp in arriving:
            cp.wait_recv()
        for cp in _spread_copies(ins, outs, sems[0], sems[1])[0] + handed_on:
            cp.wait_send()
        for cp in _own_block_copies(ins, outs, sems[4]):
            cp.wait()


def swap_job(gs):
    def copies(ins, outs, send, recv):
        x, y, c, _ = _place()
        started, awaited = [], []
        for i, (g, r1) in enumerate(zip(ins, outs)):
            for q in range(N_CHIP):
                started.append(_remote(g.at[2 * q + (1 - c)], r1.at[q], send, recv, (i, q), (x, y, 1 - c)))
                awaited.append(_remote(g.at[2 * q + c], r1.at[q], send, recv, (i, q), (x, y, 1 - c)))
        return started, awaited

    outs = [jax.ShapeDtypeStruct((N_CHIP, *g.shape[1:]), g.dtype) for g in gs]
    return _Job(gs, outs, (len(gs), N_CHIP), copies)


def chip_exchange_job(ps, rows=None, into=None):
    n = len(ps)

    def copies(ins, outs, send, recv):
        x, y, c, chips = _place()
        started, awaited = [], []
        for i, (p, r2) in enumerate(zip(ins[:n], outs)):
            for k, chip in enumerate(chips):
                src, mine, dst = p.at[2 * chip[0] + chip[1]], p.at[2 * x + y], r2.at[k]
                if rows is not None:
                    src, mine, dst = (t.at[pl.ds(rows[0], rows[1])] for t in (src, mine, dst))
                started.append(_remote(src, dst, send, recv, (i, k), (*chip, c)))
                awaited.append(_remote(mine, dst, send, recv, (i, k), (*chip, c)))
        return started, awaited

    outs = [jax.ShapeDtypeStruct((3, *p.shape[1:]), p.dtype) for p in ps]
    if into is None:
        return _Job(ps, outs, (n, 3), copies)
    return _Job(list(ps) + list(into), outs, (n, 3), copies, aliases={n + i: i for i in range(n)})


def _call(body, job, *, name, grid, in_specs, out_specs, out_shape, args, scratch_shapes=(), vmem=VMEM_LIMIT):
    if job is None:
        res = pl.pallas_call(
            body, name=name, grid=grid, in_specs=in_specs, out_specs=out_specs, out_shape=out_shape,
            scratch_shapes=list(scratch_shapes), compiler_params=_params(("arbitrary",) * len(grid), vmem),
        )(*args)
        return res, []
    n_in, n_out, n_scr = len(in_specs), len(out_specs), len(scratch_shapes)
    j_in, j_out = len(job.ins), len(job.outs)

    def with_copies(*refs):
        at = 0
        ins = refs[at:at + n_in]; at += n_in
        jins = refs[at:at + j_in]; at += j_in
        outs = refs[at:at + n_out]; at += n_out
        jouts = refs[at:at + j_out]; at += j_out
        scr = refs[at:at + n_scr]; at += n_scr
        sems = refs[at:]
        ids = [pl.program_id(d) for d in range(len(grid))]
        first = functools.reduce(jnp.logical_and, [i == 0 for i in ids])
        last = functools.reduce(jnp.logical_and, [i == n - 1 for i, n in zip(ids, grid)])

        @pl.when(first)
        def _():
            job.start(jins, jouts, sems)

        if getattr(job, "two_phase", False):
            steps, at = 1, 0
            for i, n in zip(ids, grid):
                steps, at = steps * n, at * n + i

            @pl.when(at == (TURN_EIGHTHS * steps) // 8)
            def _():
                job.turn(jins, jouts, sems)

        body(*ins, *outs, *scr)

        @pl.when(last)
        def _():
            job.finish(jins, jouts, sems)

    any_spec = pl.BlockSpec(memory_space=pl.ANY)
    res = pl.pallas_call(
        with_copies, name=name, grid=grid,
        in_specs=list(in_specs) + [any_spec] * j_in, out_specs=list(out_specs) + [any_spec] * j_out,
        out_shape=list(out_shape) + list(job.outs),
        input_output_aliases={n_in + i: n_out + o for i, o in job.aliases.items()},
        scratch_shapes=list(scratch_shapes) + job.scratch(),
        compiler_params=_params(("arbitrary",) * len(grid), vmem),
    )(*args, *job.ins)
    return res[:n_out], res[n_out:]


def run_job(job, name):
    def body(*refs):
        j_in, j_out = len(job.ins), len(job.outs)
        ins, outs, sems = refs[:j_in], refs[j_in:j_in + j_out], refs[j_in + j_out:]
        job.start(ins, outs, sems)
        job.finish(ins, outs, sems)

    any_spec = pl.BlockSpec(memory_space=pl.ANY)
    return pl.pallas_call(
        body, name=name, in_specs=[any_spec] * len(job.ins), out_specs=[any_spec] * len(job.outs),
        out_shape=list(job.outs), input_output_aliases=dict(job.aliases), scratch_shapes=job.scratch(),
    )(*job.ins)


def pair_sum(core, g, r1, name):
    _, rows, cols = g.shape
    rb = next(cand for cand in range(min(rows, 512), 0, -16) if rows % cand == 0)

    def body(core_ref, g_ref, r1_ref, p_ref, own_ref):
        del core_ref
        x, y, _, _ = _place()
        s = g_ref[0].astype(F32) + r1_ref[0].astype(F32)
        p_ref[0] = s.astype(BF16)

        @pl.when(pl.program_id(1) == 2 * x + y)
        def _():
            own_ref[...] = s

    chunk = (1, rb, cols)
    return pl.pallas_call(
        body, name=name,
        grid_spec=pltpu.PrefetchScalarGridSpec(
            num_scalar_prefetch=1, grid=(rows // rb, N_CHIP),
            in_specs=[pl.BlockSpec(chunk, lambda i, q, core_ref: (2 * q + core_ref[0], i, 0)),
                      pl.BlockSpec(chunk, lambda i, q, core_ref: (q, i, 0))],
            out_specs=[pl.BlockSpec(chunk, lambda i, q, core_ref: (q, i, 0)),
                       pl.BlockSpec((rb, cols), lambda i, q, core_ref: (i, 0))]),
        out_shape=[jax.ShapeDtypeStruct((N_CHIP, rows, cols), BF16), jax.ShapeDtypeStruct((rows, cols), F32)],
        compiler_params=_params(("arbitrary", "arbitrary")),
    )(core, g, r1)


def sum_devices(a, name):
    def body(a_ref, o_ref):
        acc = a_ref[0]
        for d in range(1, N_DEV):
            acc = acc + a_ref[d]
        o_ref[...] = acc

    return pl.pallas_call(body, name=name, out_shape=jax.ShapeDtypeStruct(a.shape[1:], F32))(a)


def _adam_update(w, g, m, v):
    nm = ADAM_B1 * m + (1.0 - ADAM_B1) * g
    nv = ADAM_B2 * v + (1.0 - ADAM_B2) * (g * g)
    m_hat = nm / (1.0 - ADAM_B1 ** ADAM_STEP)
    v_hat = nv / (1.0 - ADAM_B2 ** ADAM_STEP)
    return -ADAM_LR * (m_hat / (jnp.sqrt(v_hat) + ADAM_EPS) + ADAM_WD * w), nm, nv


def adamw(w, g, m, v, name, others=None):
    rows, cols = w.shape
    rb = rows
    for cand in range(min(rows, 512), 7, -8):
        if rows % cand == 0 and cand % 8 == 0:
            rb = cand
            break

    def body(*refs):
        if others is None:
            w_ref, g_ref, m_ref, v_ref, d_ref, nm_ref, nv_ref = refs
            gg = g_ref[...]
        else:
            w_ref, g_ref, m_ref, v_ref, r2_ref, go_ref, d_ref, nm_ref, nv_ref = refs
            gg = g_ref[...]
            for k in range(3):
                gg = gg + r2_ref[k].astype(F32)
            go_ref[...] = gg
        d_ref[...], nm_ref[...], nv_ref[...] = _adam_update(w_ref[...], gg, m_ref[...], v_ref[...])

    spec = pl.BlockSpec((rb, cols), lambda i: (i, 0))
    out = jax.ShapeDtypeStruct((rows, cols), F32)
    in_specs, args = [spec] * 4, [w, g, m, v]
    if others is not None:
        in_specs.append(pl.BlockSpec((3, rb, cols), lambda i: (0, i, 0)))
        args.append(others)
    n_out = 3 if others is None else 4
    res = pl.pallas_call(
        body, name=name, grid=(rows // rb,), in_specs=in_specs, out_specs=[spec] * n_out,
        out_shape=[out] * n_out, compiler_params=_params(("parallel",)),
    )(*args)
    return (g, *res) if others is None else tuple(res)


def adamw_small(items, name):
    n = len(items)

    def body(*refs):
        ins, outs = refs[:4 * n], refs[4 * n:]
        for i in range(n):
            w_ref, g_ref, m_ref, v_ref = ins[4 * i:4 * i + 4]
            d_ref, nm_ref, nv_ref = outs[3 * i:3 * i + 3]
            d_ref[...], nm_ref[...], nv_ref[...] = _adam_update(w_ref[...], g_ref[...], m_ref[...], v_ref[...])

    res = pl.pallas_call(
        body, name=name,
        out_shape=[jax.ShapeDtypeStruct(w.shape, F32) for w, _, _, _ in items for _ in range(3)],
    )(*[t for item in items for t in item])
    return [tuple(res[3 * i:3 * i + 3]) for i in range(n)]


def ada_fwd(c_all, w_cols, b_cols, name):
    def body(c_ref, w_ref, b_ref, cond_ref, mod_ref):
        cc = c_ref[...]
        cond = (cc * _sigmoid(cc)).astype(BF16)
        cond_ref[...] = cond
        mod_ref[...] = _dot(cond, w_ref[...].astype(BF16)) + b_ref[...]

    n, cols = c_all.shape[0], w_cols.shape[1]
    return pl.pallas_call(
        body, name=name,
        out_shape=[jax.ShapeDtypeStruct(c_all.shape, BF16), jax.ShapeDtypeStruct((n, cols), F32)],
        compiler_params=_params(),
    )(c_all, w_cols, b_cols)


def ada_bwd(cond_all, dmod_cols, name):
    def body(c_ref, d_ref, gw_ref, gb_ref):
        d = d_ref[...]
        gw_ref[...] = _dot_tn(c_ref[...], d.astype(BF16))
        gb_ref[...] = jnp.sum(d, axis=0, keepdims=True)

    dm, cols = cond_all.shape[1], dmod_cols.shape[1]
    return pl.pallas_call(
        body, name=name,
        out_shape=[jax.ShapeDtypeStruct((dm, cols), F32), jax.ShapeDtypeStruct((1, cols), F32)],
        compiler_params=_params(),
    )(cond_all, dmod_cols)


MXU_COLS = 256
FFN_CHUNK = 4 * MXU_COLS


def _hidden_chunks(ff):
    assert ff % MXU_COLS == 0
    return [(at, min(FFN_CHUNK, ff - at)) for at in range(0, ff, FFN_CHUNK)]


def _mod_spec(tiles_per_seq, dm):
    return pl.BlockSpec((1, 1, dm), lambda i: (i // tiles_per_seq, 0, 0))


def ffn_loss(x, sh, sc, gt, wgu, wd, ln_g, ln_b, target, seq, name):
    tokens, dm = x.shape
    ff = wgu.shape[1]
    chunks = _hidden_chunks(ff)
    tm = min(FFN_WIDE_TILE, seq)
    tiles_per_seq = seq // tm

    def body(x_ref, sh_ref, sc_ref, gt_ref, wgu_ref, wd_ref, lg_ref, lb_ref, t_ref,
             dr_ref, df_ref, gu_ref, a_ref, h_ref, loss_ref, dln_ref, dgt_ref):
        i = pl.program_id(0)
        xx = x_ref[...]
        h = (xx * (1.0 + sc_ref[0]) + sh_ref[0]).astype(BF16)
        h_ref[...] = h
        acc = jnp.zeros((tm, dm), F32)
        for at, wdt in chunks:
            gk = _dot_nt(h, wgu_ref[0, at:at + wdt, :])
            uk = _dot_nt(h, wgu_ref[1, at:at + wdt, :])
            gu_ref[0, :, at:at + wdt] = gk.astype(BF16)
            gu_ref[1, :, at:at + wdt] = uk.astype(BF16)
            a = (gk * _sigmoid(gk) * uk).astype(BF16)
            a_ref[:, at:at + wdt] = a
            acc = acc + _dot(a, wd_ref[at:at + wdt, :])
        half_gate = 0.5 * (1.0 + gt_ref[0])
        xhat, rstd = _ln_stats(DN_ALPHA * xx + half_gate * acc)
        err = xhat * lg_ref[...] + lb_ref[...] - t_ref[...]
        dr, dgain, dbias = _ln_bwd_normalized(err * (1.0 / dm), xhat, rstd, lg_ref[...])
        dr_ref[...] = dr
        df_ref[...] = (half_gate * dr).astype(BF16)

        @pl.when(i == 0)
        def _():
            loss_ref[...] = jnp.zeros_like(loss_ref)
            dln_ref[...] = jnp.zeros_like(dln_ref)

        @pl.when(i % tiles_per_seq == 0)
        def _():
            dgt_ref[...] = jnp.zeros_like(dgt_ref)

        loss_ref[...] += jnp.full((1, 128), (0.5 / dm) * jnp.sum(err * err), F32)
        dln_ref[0:1, :] += dgain
        dln_ref[1:2, :] += dbias
        dgt_ref[0] += jnp.sum(dr * (0.5 * acc), axis=0, keepdims=True)

    tile = pl.BlockSpec((tm, dm), lambda i: (i, 0))
    mod = _mod_spec(tiles_per_seq, dm)
    res, _ = _call(
        body, None, name=name, grid=(tokens // tm,),
        in_specs=[tile, mod, mod, mod, _const_spec(wgu.shape), _const_spec(wd.shape),
                  _const_spec((1, dm)), _const_spec((1, dm)), tile],
        out_specs=[tile, tile, pl.BlockSpec((2, tm, ff), lambda i: (0, i, 0)), pl.BlockSpec((tm, ff), lambda i: (i, 0)),
                   tile, pl.BlockSpec((1, 128), lambda i: (0, 0)), pl.BlockSpec((2, dm), lambda i: (0, 0)), mod],
        out_shape=[jax.ShapeDtypeStruct((tokens, dm), F32), jax.ShapeDtypeStruct((tokens, dm), BF16),
                   jax.ShapeDtypeStruct((2, tokens, ff), BF16), jax.ShapeDtypeStruct((tokens, ff), BF16),
                   jax.ShapeDtypeStruct((tokens, dm), BF16), jax.ShapeDtypeStruct((1, 128), F32),
                   jax.ShapeDtypeStruct((2, dm), F32), jax.ShapeDtypeStruct((tokens // seq, 1, dm), F32)],
        args=(x, sh, sc, gt, wgu, wd, ln_g, ln_b, target), vmem=FFN_WIDE_VMEM)
    return res


def ffn_up(x, sh, sc, wgu, seq, name, job=None):
    tokens, dm = x.shape
    ff = wgu.shape[1]
    chunks = _hidden_chunks(ff)
    tm = min(FFN_FWD_TILE, seq)

    def body(x_ref, sh_ref, sc_ref, wgu_ref, gu_ref, a_ref, h_ref):
        h = (x_ref[...] * (1.0 + sc_ref[0]) + sh_ref[0]).astype(BF16)
        h_ref[...] = h
        for at, wdt in chunks:
            gk = _dot_nt(h, wgu_ref[0, at:at + wdt, :])
            uk = _dot_nt(h, wgu_ref[1, at:at + wdt, :])
            gu_ref[0, :, at:at + wdt] = gk.astype(BF16)
            gu_ref[1, :, at:at + wdt] = uk.astype(BF16)
            a_ref[:, at:at + wdt] = (gk * _sigmoid(gk) * uk).astype(BF16)

    tile = pl.BlockSpec((tm, dm), lambda i: (i, 0))
    mod = _mod_spec(seq // tm, dm)
    return _call(
        body, job, name=name, grid=(tokens // tm,),
        in_specs=[tile, mod, mod, _const_spec(wgu.shape)],
        out_specs=[pl.BlockSpec((2, tm, ff), lambda i: (0, i, 0)), pl.BlockSpec((tm, ff), lambda i: (i, 0)), tile],
        out_shape=[jax.ShapeDtypeStruct((2, tokens, ff), BF16), jax.ShapeDtypeStruct((tokens, ff), BF16),
                   jax.ShapeDtypeStruct((tokens, dm), BF16)],
        args=(x, sh, sc, wgu))


def ffn_down(x, a, gt, wd, ln_g, ln_b, seq, name, job=None):
    tokens, dm = x.shape
    ff = wd.shape[0]
    chunks = _hidden_chunks(ff)
    tm = min(FFN_FWD_TILE, seq)

    def body(x_ref, a_ref, gt_ref, wd_ref, lg_ref, lb_ref, xo_ref, r_ref, f_ref):
        acc = jnp.zeros((tm, dm), F32)
        for at, wdt in chunks:
            acc = acc + _dot(a_ref[:, at:at + wdt], wd_ref[at:at + wdt, :])
        f_ref[...] = acc.astype(BF16)
        r = DN_ALPHA * x_ref[...] + (0.5 * (1.0 + gt_ref[0])) * acc
        r_ref[...] = r
        xhat, _ = _ln_stats(r)
        xo_ref[...] = xhat * lg_ref[...] + lb_ref[...]

    tile = pl.BlockSpec((tm, dm), lambda i: (i, 0))
    return _call(
        body, job, name=name, grid=(tokens // tm,),
        in_specs=[tile, pl.BlockSpec((tm, ff), lambda i: (i, 0)), _mod_spec(seq // tm, dm), _const_spec(wd.shape),
                  _const_spec((1, dm)), _const_spec((1, dm))],
        out_specs=[tile, tile, tile],
        out_shape=[jax.ShapeDtypeStruct((tokens, dm), F32), jax.ShapeDtypeStruct((tokens, dm), F32),
                   jax.ShapeDtypeStruct((tokens, dm), BF16)],
        args=(x, a, gt, wd, ln_g, ln_b))


def ffn_bwd(dr, df, x, gu, sc, wgu, wd, seq, name, job=None):
    tokens, dm = x.shape
    ff = wgu.shape[1]
    chunks = _hidden_chunks(ff)
    tm = min(FFN_WIDE_TILE, seq)
    tiles_per_seq = seq // tm
    nseq = tokens // seq

    def body(dr_ref, df_ref, x_ref, gu_ref, sc_ref, wgu_ref, wd_ref, dx_ref, dgu_ref, dmod_ref):
        @pl.when(pl.program_id(0) % tiles_per_seq == 0)
        def _():
            dmod_ref[...] = jnp.zeros_like(dmod_ref)

        df = df_ref[...]
        dh = jnp.zeros((tm, dm), F32)
        for at, wdt in chunks:
            cols = slice(at, at + wdt)
            da = _dot_nt(df, wd_ref[cols, :])
            gk = gu_ref[0, :, cols].astype(F32)
            uk = gu_ref[1, :, cols].astype(F32)
            sg = _sigmoid(gk)
            sil = gk * sg
            du = (da * sil).astype(BF16)
            dg = (da * uk * (sg * (1.0 + gk * (1.0 - sg)))).astype(BF16)
            dgu_ref[0, :, cols] = dg
            dgu_ref[1, :, cols] = du
            dh = dh + _dot(dg, wgu_ref[0, cols, :]) + _dot(du, wgu_ref[1, cols, :])
        dx_ref[...] = DN_ALPHA * dr_ref[...] + dh * (1.0 + sc_ref[0])
        dmod_ref[0, 0:1, :] += jnp.sum(dh, axis=0, keepdims=True)
        dmod_ref[0, 1:2, :] += jnp.sum(dh * x_ref[...], axis=0, keepdims=True)

    tile = pl.BlockSpec((tm, dm), lambda i: (i, 0))
    gu_spec = pl.BlockSpec((2, tm, ff), lambda i: (0, i, 0))
    return _call(
        body, job, name=name, grid=(tokens // tm,),
        in_specs=[tile, tile, tile, gu_spec, _mod_spec(tiles_per_seq, dm), _const_spec(wgu.shape), _const_spec(wd.shape)],
        out_specs=[tile, gu_spec, pl.BlockSpec((1, 2, dm), lambda i: (i // tiles_per_seq, 0, 0))],
        out_shape=[jax.ShapeDtypeStruct((tokens, dm), F32), jax.ShapeDtypeStruct((2, tokens, ff), BF16),
                   jax.ShapeDtypeStruct((nseq, 2, dm), F32)],
        args=(dr, df, x, gu, sc, wgu, wd), vmem=FFN_WIDE_VMEM)


def tn_matmul(a, b, name, job=None, b_cols=None, a_width=None):
    na, tokens, k_all = a.shape
    kk = k_all if a_width is None else a_width
    nka = k_all // kk
    assert nka * kk == k_all
    nb, _, cc = b.shape
    col = 0
    if b_cols is not None:
        col, cc = b_cols
    whole = job is None and nb == 1 and 6 * tokens * kk + 2 * tokens * cc + 8 * kk * cc <= TN_WHOLE_BUDGET
    tt = tokens
    while not whole and 4 * tt * (kk + cc) + 8 * kk * cc > TN_VMEM_BUDGET and tt % 2 == 0 and tt > 256:
        tt //= 2
    steps = tokens // tt
    b_spec = pl.BlockSpec((1, tt, cc), lambda i, s, j, t: (j, t, col),
                          pipeline_mode=pl.Buffered(1 if whole else 2))

    def body(a_ref, b_ref, o_ref, *acc):
        if steps == 1:
            o_ref[0, 0, 0] = _dot_tn(a_ref[0], b_ref[0]).astype(BF16)
            return
        acc_ref, = acc
        t = pl.program_id(3)

        @pl.when(t == 0)
        def _():
            acc_ref[...] = jnp.zeros_like(acc_ref)

        acc_ref[...] += _dot_tn(a_ref[0], b_ref[0])

        @pl.when(t == steps - 1)
        def _():
            o_ref[0, 0, 0] = acc_ref[...].astype(BF16)

    return _call(
        body, job, name=name, grid=(na, nka, nb, steps),
        in_specs=[pl.BlockSpec((1, tt, kk), lambda i, s, j, t: (i, t, s)), b_spec],
        out_specs=[pl.BlockSpec((1, 1, 1, kk, cc), lambda i, s, j, t: (i, s, j, 0, 0))],
        out_shape=[jax.ShapeDtypeStruct((na, nka, nb, kk, cc), BF16)],
        scratch_shapes=[] if steps == 1 else [pltpu.VMEM((kk, cc), F32)], args=(a, b),
        vmem=FFN_WIDE_VMEM if whole else VMEM_LIMIT)


def proj_fwd(x1, sh, sc, w_in, seq, name, job=None):
    tokens, dm = x1.shape
    tm = min(MIX_TILE, seq)
    tiles_per_seq = seq // tm
    widths = [N_Q_HEADS * HEAD_DIM, N_KV_HEADS * HEAD_DIM, N_KV_HEADS * HEAD_DIM, 512, 512, 512]
    assert sum(widths) == w_in.shape[0]

    def body(x_ref, sh_ref, sc_ref, w_ref, *outs):
        h = (x_ref[...] * (1.0 + sc_ref[0]) + sh_ref[0]).astype(BF16)
        proj = _dot_nt(h, w_ref[...])
        at = 0
        for o_ref, wdt in zip(outs, widths):
            o_ref[...] = proj[:, at:at + wdt].astype(o_ref.dtype)
            at += wdt

    tile = pl.BlockSpec((tm, dm), lambda i: (i, 0))
    mod = _mod_spec(tiles_per_seq, dm)
    return _call(
        body, job, name=name, grid=(tokens // tm,),
        in_specs=[tile, mod, mod, _const_spec(w_in.shape)],
        out_specs=[pl.BlockSpec((tm, wdt), lambda i: (i, 0)) for wdt in widths],
        out_shape=[jax.ShapeDtypeStruct((tokens, wdt), F32 if i < 3 else BF16) for i, wdt in enumerate(widths)],
        args=(x1, sh, sc, w_in))


LANES = 2 * HEAD_DIM


def _head_lane(shape):
    return lax.broadcasted_iota(jnp.int32, shape, 1) % HEAD_DIM


def _lane_half(shape):
    return lax.broadcasted_iota(jnp.int32, shape, 1) // HEAD_DIM


def _swap_rot(v):
    lane = _head_lane(v.shape)
    half = ROT_DIM // 2
    return jnp.where(lane < half, pltpu.roll(v, LANES - half, 1),
                     jnp.where(lane < ROT_DIM, pltpu.roll(v, half, 1), 0.0))


def _rope(v, cos_t, sin_t):
    return v * cos_t + _swap_rot(v) * sin_t


def _unrope(dv, cos_t, sin_t):
    return dv * cos_t + _swap_rot(dv * sin_t)


def _both_halves(t, g):
    return jnp.where(_lane_half(t.shape) == g, t, pltpu.roll(t, HEAD_DIM, 1))


def _fold_halves(t, g):
    return jnp.where(_lane_half(t.shape) == g, t + pltpu.roll(t, HEAD_DIM, 1), 0.0)


def _stack_heads(blocks):
    rows = []
    for blk in blocks:
        half = _lane_half(blk.shape)
        rows += [jnp.where(half == 0, blk, 0.0), jnp.where(half == 1, blk, 0.0)]
    return jnp.concatenate(rows, axis=0)


def _unstack_heads(t, j):
    lo = t[(2 * j) * ATTN_BLOCK:(2 * j + 1) * ATTN_BLOCK]
    hi = t[(2 * j + 1) * ATTN_BLOCK:(2 * j + 2) * ATTN_BLOCK]
    return jnp.where(_lane_half(lo.shape) == 0, lo, hi)


def _band_mask(q0, w0):
    rows, cols = GQA_GROUP * ATTN_BLOCK, 2 * ATTN_BLOCK
    qi = lax.broadcasted_iota(jnp.int32, (rows, cols), 0) % ATTN_BLOCK + q0
    ki = lax.broadcasted_iota(jnp.int32, (rows, cols), 1) + w0
    diff = qi - ki
    return (diff >= 0) & (diff < ATTN_BLOCK)


def _attn_specs(seq):
    q_spec = pl.BlockSpec((seq, GQA_GROUP * HEAD_DIM), lambda b, g: (b, g))
    kv_spec = pl.BlockSpec((seq, LANES), lambda b, g: (b, 0))
    sink_spec = pl.BlockSpec((1, GQA_GROUP * ATTN_BLOCK, 1), lambda b, g: (g, 0, 0))
    return q_spec, kv_spec, sink_spec


def _block_starts(n):
    q0 = pl.multiple_of(n * ATTN_BLOCK, ATTN_BLOCK)
    w0 = pl.multiple_of(jnp.maximum(n - 1, 0) * ATTN_BLOCK, ATTN_BLOCK)
    return q0, w0


def _stacked_queries(ref, rows):
    return _stack_heads([ref[rows, j * LANES:(j + 1) * LANES] for j in range(2)]).astype(BF16)


def _sink_columns(sinks):
    return jnp.repeat(sinks.reshape(N_KV_HEADS, GQA_GROUP), ATTN_BLOCK, axis=1)[:, :, None]


def _probs_spec(nblk):
    return pl.BlockSpec((1, 1, nblk, GQA_GROUP * ATTN_BLOCK, 2 * ATTN_BLOCK), lambda b, g: (b, g, 0, 0, 0))


def _sink_probs_spec():
    return pl.BlockSpec((1, 1, GQA_GROUP * ATTN_BLOCK, LANES), lambda b, g: (b, g, 0, 0))


def attn_fwd(q, k, v, cos_t, sin_t, sinks, seq, name, job=None):
    tokens = q.shape[0]
    nblk = seq // ATTN_BLOCK
    assert nblk >= 2
    scale = HEAD_DIM ** -0.5

    nseq = tokens // seq
    rows_stacked = GQA_GROUP * ATTN_BLOCK
    assert nblk <= LANES

    def body(q_ref, k_ref, v_ref, cos_ref, sin_ref, sink_ref, o_ref, qr_ref, p_ref, ps_ref, kd_ref, vd_ref):
        g = pl.program_id(1)
        kd_ref[...] = _both_halves(_rope(k_ref[...].astype(F32), cos_ref[...], sin_ref[...]), g).astype(BF16)
        vd_ref[...] = _both_halves(v_ref[...].astype(F32), g).astype(BF16)
        sink = sink_ref[0]
        lane = lax.broadcasted_iota(jnp.int32, (rows_stacked, LANES), 1)

        ps_ref[...] = jnp.zeros_like(ps_ref)

        def block(n, carry):
            q0, w0 = _block_starts(n)
            rows, win = pl.ds(q0, ATTN_BLOCK), pl.ds(w0, 2 * ATTN_BLOCK)
            blocks = []
            for j in range(2):
                qr = _rope(q_ref[rows, j * LANES:(j + 1) * LANES].astype(F32), cos_ref[rows, :], sin_ref[rows, :]).astype(BF16)
                qr_ref[rows, j * LANES:(j + 1) * LANES] = qr
                blocks.append(qr)
            qs = _stack_heads(blocks)
            s = _dot_nt(qs, kd_ref[win, :]) * scale
            s = jnp.where(_band_mask(q0, w0), s, NEG_BIG)
            m = jnp.maximum(jnp.max(s, axis=-1, keepdims=True), sink)
            p = jnp.exp(s - m)
            e_sink = jnp.exp(sink - m)
            inv = pl.reciprocal(jnp.sum(p, axis=-1, keepdims=True) + e_sink, approx=True)
            pn = (p * inv).astype(BF16)
            p_ref[0, 0, n] = pn
            out = _dot(pn, vd_ref[win, :])
            for j in range(2):
                o_ref[rows, j * LANES:(j + 1) * LANES] = _unstack_heads(out, j).astype(o_ref.dtype)
            ps_ref[0, 0] = jnp.where(lane == n, e_sink * inv, ps_ref[0, 0])
            return carry

        lax.fori_loop(0, nblk, block, 0, unroll=2)

    q_spec, kv_spec, sink_spec = _attn_specs(seq)
    return _call(
        body, job, name=name, grid=(nseq, N_KV_HEADS),
        in_specs=[q_spec, kv_spec, kv_spec, kv_spec, kv_spec, sink_spec],
        out_specs=[q_spec, q_spec, _probs_spec(nblk), _sink_probs_spec()],
        out_shape=[jax.ShapeDtypeStruct(q.shape, BF16), jax.ShapeDtypeStruct(q.shape, BF16),
                   jax.ShapeDtypeStruct((nseq, N_KV_HEADS, nblk, rows_stacked, 2 * ATTN_BLOCK), BF16),
                   jax.ShapeDtypeStruct((nseq, N_KV_HEADS, rows_stacked, LANES), F32)],
        scratch_shapes=[pltpu.VMEM((seq, LANES), BF16), pltpu.VMEM((seq, LANES), BF16)],
        args=(q, k, v, cos_t, sin_t, _sink_columns(sinks)))


def attn_bwd(qr, k, v, do, probs, sink_probs, cos_t, sin_t, seq, name, job=None):
    tokens = qr.shape[0]
    nseq = tokens // seq
    nblk = seq // ATTN_BLOCK
    assert nblk >= 2
    rows_stacked = GQA_GROUP * ATTN_BLOCK
    scale = HEAD_DIM ** -0.5

    def body(q_ref, k_ref, v_ref, do_ref, p_ref, ps_ref, cos_ref, sin_ref, dq_ref, dk_ref, dv_ref, ds_ref,
             kd_ref, vd_ref, dkd_ref, dvd_ref, acc_ref):
        g = pl.program_id(1)
        kd_ref[...] = _both_halves(_rope(k_ref[...].astype(F32), cos_ref[...], sin_ref[...]), g).astype(BF16)
        vd_ref[...] = _both_halves(v_ref[...].astype(F32), g).astype(BF16)
        dkd_ref[...] = jnp.zeros_like(dkd_ref)
        dvd_ref[...] = jnp.zeros_like(dvd_ref)
        acc_ref[...] = jnp.zeros_like(acc_ref)
        lane = lax.broadcasted_iota(jnp.int32, (rows_stacked, LANES), 1)

        def block(n, carry):
            q0, w0 = _block_starts(n)
            rows, win = pl.ds(q0, ATTN_BLOCK), pl.ds(w0, 2 * ATTN_BLOCK)
            qs = _stacked_queries(q_ref, rows)
            dos = _stacked_queries(do_ref, rows)
            kw, vw = kd_ref[win, :], vd_ref[win, :]
            pn16 = p_ref[0, 0, n]
            pn = pn16.astype(F32)
            dvd_ref[win, :] += _dot_tn(pn16, dos)
            dp = _dot_nt(dos, vw)
            delta = jnp.sum(dp * pn, axis=-1, keepdims=True)
            ds = (pn * (dp - delta)).astype(BF16)
            dqs = _dot(ds, kw) * scale
            dkd_ref[win, :] += _dot_tn(ds, qs) * scale
            cos_b, sin_b = cos_ref[rows, :], sin_ref[rows, :]
            for j in range(2):
                dq_ref[rows, j * LANES:(j + 1) * LANES] = _unrope(_unstack_heads(dqs, j), cos_b, sin_b).astype(BF16)
            acc_ref[...] += jnp.where(lane == n, ps_ref[0, 0] * delta, 0.0)
            return carry

        lax.fori_loop(0, nblk // 2, lambda i, carry: block(2 * i + 1, block(2 * i, carry)), 0)
        ds_ref[0, 0] = -jnp.sum(acc_ref[...], axis=-1, keepdims=True)
        dk_g = _unrope(_fold_halves(dkd_ref[...], g), cos_ref[...], sin_ref[...])
        dv_g = _fold_halves(dvd_ref[...], g)

        @pl.when(g == 0)
        def _():
            dk_ref[...] = dk_g
            dv_ref[...] = dv_g

        @pl.when(g != 0)
        def _():
            dk_ref[...] += dk_g
            dv_ref[...] += dv_g

    q_spec, kv_spec, _ = _attn_specs(seq)
    return _call(
        body, job, name=name, grid=(nseq, N_KV_HEADS),
        in_specs=[q_spec, kv_spec, kv_spec, q_spec, _probs_spec(nblk), _sink_probs_spec(), kv_spec, kv_spec],
        out_specs=[q_spec, kv_spec, kv_spec, pl.BlockSpec((1, 1, rows_stacked, 1), lambda b, g: (b, g, 0, 0))],
        out_shape=[jax.ShapeDtypeStruct(qr.shape, BF16), jax.ShapeDtypeStruct(k.shape, F32),
                   jax.ShapeDtypeStruct(k.shape, F32), jax.ShapeDtypeStruct((nseq, N_KV_HEADS, rows_stacked, 1), F32)],
        scratch_shapes=[pltpu.VMEM((seq, LANES), BF16), pltpu.VMEM((seq, LANES), BF16),
                        pltpu.VMEM((seq, LANES), F32), pltpu.VMEM((seq, LANES), F32),
                        pltpu.VMEM((rows_stacked, LANES), F32)],
        args=(qr, k, v, do, probs, sink_probs, cos_t, sin_t))


CONV_COLS = 128


def _shift_down(z, by):
    t = lax.broadcasted_iota(jnp.int32, z.shape, 0)
    return jnp.where(t >= by, pltpu.roll(z, by, 0), 0.0)


def _shift_up(z, by):
    n = z.shape[0]
    t = lax.broadcasted_iota(jnp.int32, z.shape, 0)
    return jnp.where(t < n - by, pltpu.roll(z, n - by, 0), 0.0)


def conv_fwd(u, bg, cg, conv_w, seq, name):
    tokens, width = u.shape

    def body(u_ref, bg_ref, cg_ref, w_ref, o_ref):
        z = cg_ref[...].astype(F32) * u_ref[...].astype(F32)
        yy = w_ref[2:3, :] * z + w_ref[1:2, :] * _shift_down(z, 1) + w_ref[0:1, :] * _shift_down(z, 2)
        o_ref[...] = (bg_ref[...].astype(F32) * yy).astype(BF16)

    col = pl.BlockSpec((seq, CONV_COLS), lambda j, b: (b, j))
    return pl.pallas_call(
        body, name=name, grid=(width // CONV_COLS, tokens // seq),
        in_specs=[col, col, col, pl.BlockSpec((CONV_TAPS, CONV_COLS), lambda j, b: (0, j))],
        out_specs=col, out_shape=jax.ShapeDtypeStruct((tokens, width), BF16),
        compiler_params=_params(("parallel", "parallel")),
    )(u, bg, cg, conv_w)


def conv_bwd(dout, u, bg, cg, conv_w, seq, name):
    tokens, width = u.shape

    def body(do_ref, u_ref, bg_ref, cg_ref, w_ref, du_ref, dbg_ref, dcg_ref, dw_ref):
        uu, cg_v, do = u_ref[...].astype(F32), cg_ref[...].astype(F32), do_ref[...].astype(F32)
        z = cg_v * uu
        z1, z2 = _shift_down(z, 1), _shift_down(z, 2)
        yy = w_ref[2:3, :] * z + w_ref[1:2, :] * z1 + w_ref[0:1, :] * z2
        dbg_ref[...] = (do * yy).astype(BF16)
        dyy = do * bg_ref[...].astype(F32)
        dz = w_ref[2:3, :] * dyy + w_ref[1:2, :] * _shift_up(dyy, 1) + w_ref[0:1, :] * _shift_up(dyy, 2)
        du_ref[...] = (dz * cg_v).astype(BF16)
        dcg_ref[...] = (dz * uu).astype(BF16)

        @pl.when(pl.program_id(1) == 0)
        def _():
            dw_ref[...] = jnp.zeros_like(dw_ref)

        dw_ref[0:1, :] += jnp.sum(dyy * z2, axis=0, keepdims=True)
        dw_ref[1:2, :] += jnp.sum(dyy * z1, axis=0, keepdims=True)
        dw_ref[2:3, :] += jnp.sum(dyy * z, axis=0, keepdims=True)

    col = pl.BlockSpec((seq, CONV_COLS), lambda j, b: (b, j))
    w_spec = pl.BlockSpec((CONV_TAPS, CONV_COLS), lambda j, b: (0, j))
    act = jax.ShapeDtypeStruct((tokens, width), BF16)
    return pl.pallas_call(
        body, name=name, grid=(width // CONV_COLS, tokens // seq),
        in_specs=[col, col, col, col, w_spec], out_specs=[col, col, col, w_spec],
        out_shape=[act, act, act, jax.ShapeDtypeStruct((CONV_TAPS, width), F32)],
        compiler_params=_params(("parallel", "arbitrary")),
    )(dout, u, bg, cg, conv_w)


def out_fwd(x1, attn, conv, gt, w_out, ln_g, ln_b, seq, name, job=None):
    tokens, dm = x1.shape
    half = attn.shape[1]
    tm = min(MIX_TILE, seq)
    tiles_per_seq = seq // tm

    def body(x_ref, a_ref, c_ref, gt_ref, w_ref, lg_ref, lb_ref, xo_ref, r_ref, mi_ref, mix_ref):
        mixin = jnp.concatenate([a_ref[...], c_ref[...]], axis=1).astype(BF16)
        mi_ref[...] = mixin
        mix = _dot(mixin, w_ref[...])
        mix_ref[...] = mix.astype(BF16)
        r = DN_ALPHA * x_ref[...] + (1.0 + gt_ref[0]) * mix
        r_ref[...] = r
        xhat, _ = _ln_stats(r)
        xo_ref[...] = xhat * lg_ref[...] + lb_ref[...]

    tile = pl.BlockSpec((tm, dm), lambda i: (i, 0))
    htile = pl.BlockSpec((tm, half), lambda i: (i, 0))
    return _call(
        body, job, name=name, grid=(tokens // tm,),
        in_specs=[tile, htile, htile, _mod_spec(tiles_per_seq, dm), _const_spec(w_out.shape),
                  _const_spec((1, dm)), _const_spec((1, dm))],
        out_specs=[tile, tile, tile, tile],
        out_shape=[jax.ShapeDtypeStruct((tokens, dm), F32), jax.ShapeDtypeStruct((tokens, dm), F32),
                   jax.ShapeDtypeStruct((tokens, dm), BF16), jax.ShapeDtypeStruct((tokens, dm), BF16)],
        args=(x1, attn, conv, gt, w_out, ln_g, ln_b))


def out_bwd(dy, r, mix, gt, w_out, ln_g, seq, name, job=None):
    tokens, dm = r.shape
    half = dm // 2
    tm = min(MIX_TILE, seq)
    tiles_per_seq = seq // tm
    nseq = tokens // seq

    def body(dy_ref, r_ref, mix_ref, gt_ref, w_ref, lg_ref, dres_ref, da_ref, dc_ref, dmix_ref, dln_ref, dgt_ref):
        i = pl.program_id(0)
        dr, dgain, dbias = _ln_bwd(dy_ref[...], r_ref[...], lg_ref[...])

        @pl.when(i == 0)
        def _():
            dln_ref[...] = jnp.zeros_like(dln_ref)

        @pl.when(i % tiles_per_seq == 0)
        def _():
            dgt_ref[...] = jnp.zeros_like(dgt_ref)

        dln_ref[0:1, :] += dgain
        dln_ref[1:2, :] += dbias
        dgt_ref[0] += jnp.sum(dr * mix_ref[...].astype(F32), axis=0, keepdims=True)
        dres_ref[...] = DN_ALPHA * dr
        dmix = ((1.0 + gt_ref[0]) * dr).astype(BF16)
        dmix_ref[...] = dmix
        dmixin = _dot_nt(dmix, w_ref[...])
        da_ref[...] = dmixin[:, :half].astype(BF16)
        dc_ref[...] = dmixin[:, half:].astype(BF16)

    tile = pl.BlockSpec((tm, dm), lambda i: (i, 0))
    htile = pl.BlockSpec((tm, half), lambda i: (i, 0))
    return _call(
        body, job, name=name, grid=(tokens // tm,),
        in_specs=[tile, tile, tile, _mod_spec(tiles_per_seq, dm), _const_spec(w_out.shape), _const_spec((1, dm))],
        out_specs=[tile, htile, htile, tile, pl.BlockSpec((2, dm), lambda i: (0, 0)),
                   pl.BlockSpec((1, 1, dm), lambda i: (i // tiles_per_seq, 0, 0))],
        out_shape=[jax.ShapeDtypeStruct((tokens, dm), F32), jax.ShapeDtypeStruct((tokens, half), BF16),
                   jax.ShapeDtypeStruct((tokens, half), BF16), jax.ShapeDtypeStruct((tokens, dm), BF16),
                   jax.ShapeDtypeStruct((2, dm), F32), jax.ShapeDtypeStruct((nseq, 1, dm), F32)],
        args=(dy, r, mix, gt, w_out, ln_g))


def proj_bwd(parts, dres, x1, sh, sc, w_in, r_prev, f_prev, gt_prev, ln_g_prev, seq, name, job=None):
    tokens, dm = x1.shape
    tm = min(MIX_TILE, seq)
    tiles_per_seq = seq // tm
    nseq = tokens // seq
    widths = [p.shape[1] for p in parts]
    total = sum(widths)

    def body(*refs):
        part_refs = refs[:6]
        (dres_ref, x_ref, sh_ref, sc_ref, w_ref, r_ref, f_ref, gt_ref, lg_ref,
         dr_ref, df_ref, dproj_ref, h_ref, dmod_ref, dln_ref, dgt_ref) = refs[6:]
        i = pl.program_id(0)
        dproj = jnp.concatenate([p[...].astype(BF16) for p in part_refs], axis=1)
        dproj_ref[...] = dproj
        dh = _dot(dproj, w_ref[...])
        xx = x_ref[...]
        one_sc = 1.0 + sc_ref[0]
        h_ref[...] = (xx * one_sc + sh_ref[0]).astype(BF16)
        dr, dgain, dbias = _ln_bwd(dres_ref[...] + dh * one_sc, r_ref[...], lg_ref[...])
        dr_ref[...] = dr
        df_ref[...] = ((0.5 * (1.0 + gt_ref[0])) * dr).astype(BF16)

        @pl.when(i == 0)
        def _():
            dln_ref[...] = jnp.zeros_like(dln_ref)

        @pl.when(i % tiles_per_seq == 0)
        def _():
            dmod_ref[...] = jnp.zeros_like(dmod_ref)
            dgt_ref[...] = jnp.zeros_like(dgt_ref)

        dmod_ref[0, 0:1, :] += jnp.sum(dh, axis=0, keepdims=True)
        dmod_ref[0, 1:2, :] += jnp.sum(dh * xx, axis=0, keepdims=True)
        dln_ref[0:1, :] += dgain
        dln_ref[1:2, :] += dbias
        dgt_ref[0] += jnp.sum(dr * (0.5 * f_ref[...].astype(F32)), axis=0, keepdims=True)

    tile = pl.BlockSpec((tm, dm), lambda i: (i, 0))
    mod = _mod_spec(tiles_per_seq, dm)
    return _call(
        body, job, name=name, grid=(tokens // tm,),
        in_specs=[pl.BlockSpec((tm, wdt), lambda i: (i, 0)) for wdt in widths]
        + [tile, tile, mod, mod, _const_spec(w_in.shape), tile, tile, mod, _const_spec((1, dm))],
        out_specs=[tile, tile, pl.BlockSpec((tm, total), lambda i: (i, 0)), tile,
                   pl.BlockSpec((1, 2, dm), lambda i: (i // tiles_per_seq, 0, 0)),
                   pl.BlockSpec((2, dm), lambda i: (0, 0)), mod],
        out_shape=[jax.ShapeDtypeStruct((tokens, dm), F32), jax.ShapeDtypeStruct((tokens, dm), BF16),
                   jax.ShapeDtypeStruct((tokens, total), BF16), jax.ShapeDtypeStruct((tokens, dm), BF16),
                   jax.ShapeDtypeStruct((nseq, 2, dm), F32), jax.ShapeDtypeStruct((2, dm), F32),
                   jax.ShapeDtypeStruct((nseq, 1, dm), F32)],
        args=(*parts, dres, x1, sh, sc, w_in, r_prev, f_prev, gt_prev, ln_g_prev))


def _rope_tables(positions):
    half = ROT_DIM // 2
    inv_freq = jnp.power(jnp.float32(ROPE_THETA), -jnp.arange(0, ROT_DIM, 2, dtype=F32) / ROT_DIM)
    lane = jnp.arange(LANES) % HEAD_DIM
    freq = jnp.where(lane < ROT_DIM, inv_freq[lane % half], 0.0)
    sign = jnp.where(lane < half, -1.0, 1.0).astype(F32)
    ang = positions.astype(F32)[:, None] * freq[None, :]
    return jnp.cos(ang), sign[None, :] * jnp.sin(ang)


def kernel(x, c, positions, w_ada, b_ada, ffn1_w_gate_up, ffn1_w_down, ln1_g, ln1_b, w_in, conv_w, attn_sinks, w_out, ln2_g, ln2_b, ffn2_w_gate_up, ffn2_w_down, ln3_g, ln3_b, loss_target, m_w_ada, m_b_ada, m_ffn1_w_gate_up, m_ffn1_w_down, m_ln1_g, m_ln1_b, m_w_in, m_conv_w, m_attn_sinks, m_w_out, m_ln2_g, m_ln2_b, m_ffn2_w_gate_up, m_ffn2_w_down, m_ln3_g, m_ln3_b, v_w_ada, v_b_ada, v_ffn1_w_gate_up, v_ffn1_w_down, v_ln1_g, v_ln1_b, v_w_in, v_conv_w, v_attn_sinks, v_w_out, v_ln2_g, v_ln2_b, v_ffn2_w_gate_up, v_ffn2_w_down, v_ln3_g, v_ln3_b):
    nseq, seq, dm = x.shape
    tokens = nseq * seq
    dev = 4 * lax.axis_index("x") + 2 * lax.axis_index("y") + lax.axis_index("c")
    core = lax.axis_index("c").astype(jnp.int32).reshape(1)
    ada_cols = w_ada.shape[2]
    ff = ffn1_w_down.shape[1] * N_DEV
    fc = ff // 4
    in_cols = w_in.shape[2]
    conv_cols = conv_w.shape[2]

    def t_bf16(w):
        return w[0].T.astype(BF16)

    c_all, convw_all = all_gather([c, conv_w[0]], "gather_cond")
    c_all = c_all.reshape(N_DEV * nseq, dm)
    convw_full = convw_all.transpose(1, 0, 2).reshape(CONV_TAPS, N_DEV * conv_cols)

    b_cols = lax.dynamic_slice(b_ada, (0, dev * ada_cols), (1, ada_cols))
    cond_all, mod_cols = ada_fwd(c_all, w_ada[0], b_cols, "ada_fwd")
    wgu1, mod_all = all_gather([t_bf16(ffn1_w_gate_up), mod_cols], "gather_ffn1")
    wgu1 = wgu1.reshape(2, ff, dm)
    mod = lax.dynamic_slice(mod_all, (0, dev * nseq, 0), (N_DEV, nseq, ada_cols))
    mod = mod.transpose(1, 0, 2).reshape(nseq, 9, 1, dm)
    sh1, sc1, g1, sh2, sc2, g2, sh3, sc3, g3 = [mod[:, i] for i in range(9)]

    x0 = x.reshape(tokens, dm)
    (gu1, a1, h1), (wd1, wout) = ffn_up(x0, sh1, sc1, wgu1, seq, "ffn1_up",
                                        job=_GatherJob([ffn1_w_down[0].astype(BF16), w_out[0].astype(BF16)]))
    wd1, wout = wd1.reshape(ff, dm), wout.reshape(dm, dm)
    (x1, r1, f1), (win,) = ffn_down(x0, a1, g1, wd1, ln1_g, ln1_b, seq, "ffn1_down", job=_GatherJob([t_bf16(w_in)]))
    win = win.reshape(N_DEV * in_cols, dm)
    (q, k, v, u, bg, cg), wd2_spread = proj_fwd(x1, sh2, sc2, win, seq, "proj_fwd",
                                                job=gather_spread_job([ffn2_w_down[0].astype(BF16)]))
    cos_t, sin_t = _rope_tables(positions.reshape(tokens))
    sinks = attn_sinks[0]
    (attn, q_rot, probs, sink_probs), wgu2_spread = attn_fwd(q, k, v, cos_t, sin_t, sinks, seq, "attn_fwd",
                                                             job=gather_spread_job([t_bf16(ffn2_w_gate_up)]))
    conv = conv_fwd(u, bg, cg, convw_full, seq, "conv_fwd")
    (x2, r2, mixin, mix), (wd2, wgu2) = out_fwd(x1, attn, conv, g2, wout, ln2_g, ln2_b, seq, "out_fwd",
                                                job=gather_forward_job(wd2_spread + wgu2_spread))
    wd2, wgu2 = wd2.reshape(ff, dm), wgu2.reshape(2, ff, dm)
    target = loss_target.reshape(tokens, dm)
    dr3, df3, gu3, a3, h3, loss_part, dln3, dg3 = ffn_loss(x2, sh3, sc3, g3, wgu2, wd2, ln3_g, ln3_b, target, seq, "ffn2_fwd")

    (dx2, dgu3, dmod3), _ = ffn_bwd(dr3, df3, x2, gu3, sc3, wgu2, wd2, seq, "ffn2_bwd")
    pair = 2 * fc
    g_wd2 = tn_matmul(a3[None], df3[None], "ffn2_dwd", a_width=pair)[0][0].reshape(N_DEV, ff // N_DEV, dm)
    g_wgu2 = tn_matmul(dgu3, h3[None], "ffn2_dwgu", a_width=pair)[0][0].reshape(N_DEV, fc, dm)
    (dres2, dattn, dconv, dmix, dln2, dg2), swapped = out_bwd(dx2, r2, mix, g2, wout, ln2_g, seq, "out_bwd",
                                                              job=swap_job([g_wgu2, g_wd2]))
    p_wgu2, own_wgu2 = pair_sum(core, g_wgu2, swapped[0], "pair_wgu2")
    p_wd2, own_wd2 = pair_sum(core, g_wd2, swapped[1], "pair_wd2")
    du, dbg, dcg, dconvw = conv_bwd(dconv, u, bg, cg, convw_full, seq, "conv_bwd")
    (dq, dk, dv, dsink_rows), (far_wd2,) = attn_bwd(
        q_rot, k, v, dattn, probs, sink_probs, cos_t, sin_t, seq, "attn_bwd", job=chip_exchange_job([p_wd2]))
    parts = [dq, dk, dv, du, dbg, dcg]
    (dr1, df1, dproj, h2, dmod2, dln1, dg1), far_top = proj_bwd(
        parts, dres2, x1, sh2, sc2, win, r1, f1, g1, ln1_g, seq, "proj_bwd",
        job=chip_exchange_job([p_wgu2], rows=(0, fc // 2)))
    (dx0, dgu1, dmod1), _ = ffn_bwd(dr1, df1, x0, gu1, sc1, wgu1, wd1, seq, "ffn1_bwd")

    dmod = jnp.concatenate([dmod1, dg1, dmod2, dg2, dmod3, dg3], axis=1).reshape(nseq, 9 * dm)
    half = dm // 2
    jobs = _Jobs([gather_spread_job([dmod]),
                  chip_exchange_job([p_wgu2], rows=(fc // 2, fc // 2), into=far_top)])
    (g_wd1,), res = tn_matmul(a1[None], df1[None], "ffn1_dwd", job=jobs, a_width=pair)
    dmod_spread, (far_wgu2,) = jobs.split(res)
    g_wd1 = g_wd1.reshape(N_DEV, ff // N_DEV, dm)
    jobs = _Jobs([swap_job([g_wd1]), gather_forward_job(dmod_spread)])
    (g_l,), res = tn_matmul(dgu1, h1[None], "ffn1_dwgu_l", job=jobs, b_cols=(0, half), a_width=pair)
    (sw_wd1,), (dmod_all,) = jobs.split(res)
    g_l = g_l.reshape(N_DEV, fc, half)
    p_wd1, own_wd1 = pair_sum(core, g_wd1, sw_wd1, "pair_wd1")
    jobs = _Jobs([chip_exchange_job([p_wd1]), swap_job([g_l])])
    (g_r,), res = tn_matmul(dgu1, h1[None], "ffn1_dwgu_r", job=jobs, b_cols=(1, half), a_width=pair)
    (far_wd1,), (sw_l,) = jobs.split(res)
    g_r = g_r.reshape(N_DEV, fc, half)
    p_l, own_l = pair_sum(core, g_l, sw_l, "pair_wgu1_l")

    dmod_cols = lax.dynamic_slice(dmod_all.reshape(N_DEV * nseq, 9 * dm), (0, dev * ada_cols), (N_DEV * nseq, ada_cols))
    grad_w_ada, gb_cols = ada_bwd(cond_all, dmod_cols, "ada_bwd")
    dsinks = jnp.sum(dsink_rows.reshape(nseq, N_Q_HEADS, ATTN_BLOCK), axis=(0, 2))
    small = jnp.zeros((8, dm), F32)
    small = small.at[0:2].set(dln1).at[2:4].set(dln2).at[4:6].set(dln3)
    small = small.at[6, 0:N_Q_HEADS].set(dsinks).at[7, 0].set(loss_part[0, 0])

    jobs = _Jobs([chip_exchange_job([p_l]), swap_job([g_r]), gather_spread_job([small, dconvw, gb_cols])])
    (g_win,), res = tn_matmul(dproj[None], h2[None], "dwin", job=jobs)
    (far_l,), (sw_r,), small_spread = jobs.split(res)
    g_win = g_win.reshape(N_DEV, in_cols, dm)
    p_r, own_r = pair_sum(core, g_r, sw_r, "pair_wgu1_r")
    jobs = _Jobs([chip_exchange_job([p_r]), swap_job([g_win]), gather_forward_job(small_spread)])
    (g_wout,), res = tn_matmul(mixin[None], dmix[None], "dwout", job=jobs)
    (far_r,), (sw_win,), (small_all, dconvw_all, gb_all) = jobs.split(res)
    g_wout = g_wout.reshape(N_DEV, dm // N_DEV, dm)
    p_win, own_win = pair_sum(core, g_win, sw_win, "pair_win")

    given = dict(w_ada=(w_ada, m_w_ada, v_w_ada), b_ada=(b_ada, m_b_ada, v_b_ada),
                 ffn1_w_gate_up=(ffn1_w_gate_up, m_ffn1_w_gate_up, v_ffn1_w_gate_up),
                 ffn1_w_down=(ffn1_w_down, m_ffn1_w_down, v_ffn1_w_down),
                 ln1_g=(ln1_g, m_ln1_g, v_ln1_g), ln1_b=(ln1_b, m_ln1_b, v_ln1_b),
                 w_in=(w_in, m_w_in, v_w_in), conv_w=(conv_w, m_conv_w, v_conv_w),
                 attn_sinks=(attn_sinks, m_attn_sinks, v_attn_sinks), w_out=(w_out, m_w_out, v_w_out),
                 ln2_g=(ln2_g, m_ln2_g, v_ln2_g), ln2_b=(ln2_b, m_ln2_b, v_ln2_b),
                 ffn2_w_gate_up=(ffn2_w_gate_up, m_ffn2_w_gate_up, v_ffn2_w_gate_up),
                 ffn2_w_down=(ffn2_w_down, m_ffn2_w_down, v_ffn2_w_down),
                 ln3_g=(ln3_g, m_ln3_g, v_ln3_g), ln3_b=(ln3_b, m_ln3_b, v_ln3_b))
    transposed = ("ffn1_w_gate_up", "ffn2_w_gate_up", "w_in")

    def big_adamw(nm, grad, far=None):
        flip = nm in transposed
        w2, m2, v2 = [t[0].T if flip else t[0] for t in given[nm]]
        return [t.T[None] if flip else t[None] for t in adamw(w2, grad, m2, v2, "adamw_" + nm, others=far)]

    jobs = _Jobs([chip_exchange_job([p_win]), swap_job([g_wout])])
    (far_win,), (sw_wout,) = jobs.split(run_job(jobs, "rs_tail_win"))
    p_wout, own_wout = pair_sum(core, g_wout, sw_wout, "pair_wout")
    (far_wout,) = run_job(chip_exchange_job([p_wout]), "rs_tail_wout")

    grads = {
        "ffn1_w_gate_up": jnp.concatenate([own_l, own_r], axis=1), "ffn1_w_down": own_wd1,
        "w_in": own_win, "w_out": own_wout, "ffn2_w_gate_up": own_wgu2, "ffn2_w_down": own_wd2,
    }
    others = {"ffn1_w_gate_up": jnp.concatenate([far_l, far_r], axis=2), "ffn1_w_down": far_wd1,
              "w_in": far_win, "w_out": far_wout, "ffn2_w_gate_up": far_wgu2, "ffn2_w_down": far_wd2}
    results = {"w_ada": big_adamw("w_ada", grad_w_ada)}
    for nm in grads:
        results[nm] = big_adamw(nm, grads[nm], others[nm])

    small_sum = sum_devices(small_all, "sum_small")
    dconvw_sum = sum_devices(dconvw_all, "sum_convw")
    loss = small_sum[7, 0]
    grads["b_ada"] = gb_all.reshape(1, N_DEV * ada_cols)
    grads["conv_w"] = lax.dynamic_slice(dconvw_sum, (0, dev * conv_cols), (CONV_TAPS, conv_cols))
    grads["attn_sinks"] = small_sum[6:7, 0:N_Q_HEADS]
    for i, nm in enumerate(["ln1_g", "ln1_b", "ln2_g", "ln2_b", "ln3_g", "ln3_b"]):
        grads[nm] = small_sum[i:i + 1]

    order = ["w_ada", "b_ada", "ffn1_w_gate_up", "ffn1_w_down", "ln1_g", "ln1_b", "w_in", "conv_w", "attn_sinks",
             "w_out", "ln2_g", "ln2_b", "ffn2_w_gate_up", "ffn2_w_down", "ln3_g", "ln3_b"]
    small_names = [nm for nm in order if nm not in results]
    items = []
    for nm in small_names:
        shape = given[nm][0].shape
        two_d = (shape[-2], shape[-1])
        items.append((given[nm][0].reshape(two_d), grads[nm].reshape(two_d), *[t.reshape(two_d) for t in given[nm][1:]]))
    for nm, res in zip(small_names, adamw_small(items, "adamw_small")):
        shape = given[nm][0].shape
        results[nm] = [grads[nm].reshape(shape)] + [t.reshape(shape) for t in res]
    grad_x = dx0.reshape(nseq, seq, dm)
    return (loss, grad_x, *[results[nm][i] for i in range(4) for nm in order])
```

```python
import functools

import jax
import jax.numpy as jnp
import numpy as np
from jax import lax
from jax.experimental import pallas as pl
from jax.experimental.pallas import tpu as pltpu

F32 = jnp.float32
BF16 = jnp.bfloat16
MESH = pl.DeviceIdType.MESH

N_DEV = 8
N_CHIP = 4
HEAD_DIM = 64
N_Q_HEADS = 8
N_KV_HEADS = 2
GQA_GROUP = N_Q_HEADS // N_KV_HEADS
ATTN_BLOCK = 128
ROT_DIM = 16
ROPE_THETA = 500000.0
CONV_TAPS = 3
LN_EPS = 1e-5
DN_ALPHA = 2.0 ** 0.25
ADAM_LR = 0.001
ADAM_B1 = 0.9
ADAM_B2 = 0.999
ADAM_EPS = 1e-08
ADAM_WD = 0.01
ADAM_STEP = 10
NEG_BIG = -1e30

VMEM_LIMIT = 56 * 1024 * 1024
FFN_FWD_TILE = 512
MIX_TILE = 512
FFN_WIDE_TILE = 512
FFN_WIDE_VMEM = 62 * 1024 * 1024
TN_VMEM_BUDGET = 36 * 1024 * 1024


def _params(semantics=None, vmem=VMEM_LIMIT):
    return pltpu.CompilerParams(dimension_semantics=semantics, vmem_limit_bytes=vmem)


def _dot(a, b):
    return jnp.dot(a, b, preferred_element_type=F32)


def _dot_nt(a, b):
    return lax.dot_general(a, b, (((1,), (1,)), ((), ())), preferred_element_type=F32)


def _dot_tn(a, b):
    return lax.dot_general(a, b, (((0,), (0,)), ((), ())), preferred_element_type=F32)


def _sigmoid(x):
    return pl.reciprocal(1.0 + jnp.exp(-x), approx=True)


def _ln_stats(r):
    mu = jnp.mean(r, axis=-1, keepdims=True)
    d = r - mu
    var = jnp.mean(d * d, axis=-1, keepdims=True)
    rstd = lax.rsqrt(var + LN_EPS)
    return d * rstd, rstd


def _ln_bwd(dy, r, g):
    return _ln_bwd_normalized(dy, *_ln_stats(r), g)


def _ln_bwd_normalized(dy, xhat, rstd, g):
    dxhat = dy * g
    c1 = jnp.mean(dxhat, axis=-1, keepdims=True)
    c2 = jnp.mean(dxhat * xhat, axis=-1, keepdims=True)
    dr = rstd * (dxhat - c1 - xhat * c2)
    return dr, jnp.sum(dy * xhat, axis=0, keepdims=True), jnp.sum(dy, axis=0, keepdims=True)


def _const_spec(shape):
    nd = len(shape)
    return pl.BlockSpec(shape, lambda *_: (0,) * nd, pipeline_mode=pl.Buffered(1))


def all_gather(arrs, name):
    n = len(arrs)

    def body(*refs):
        ins, outs = refs[:n], refs[n:2 * n]
        send_sems, recv_sems, local_sems = refs[2 * n:]
        x, y, c = lax.axis_index("x"), lax.axis_index("y"), lax.axis_index("c")
        me, sibling = (x, y, c), (x, y, 1 - c)
        chips = [(1 - x, y), (x, 1 - y), (1 - x, 1 - y)]

        def slot(i, p):
            return outs[i].at[4 * p[0] + 2 * p[1] + p[2]]

        def copy(i, k, block, to, src=None):
            return pltpu.make_async_remote_copy(
                src_ref=slot(i, block) if src is None else src, dst_ref=slot(i, block),
                send_sem=send_sems.at[i, k], recv_sem=recv_sems.at[i, k],
                device_id=to, device_id_type=MESH)

        mine = [pltpu.make_async_copy(ins[i], slot(i, me), local_sems.at[i]) for i in range(n)]
        for cp in mine:
            cp.start()
        first = []
        for i in range(n):
            first.append(copy(i, 0, me, sibling, src=ins[i]))
            first += [copy(i, 1 + j, me, (*chip, c), src=ins[i]) for j, chip in enumerate(chips)]
        for cp in first:
            cp.start()
        passed = []
        for i in range(n):
            for j, chip in enumerate(chips):
                copy(i, 1 + j, (*chip, c), me).wait_recv()
                cp = copy(i, 4 + j, (*chip, c), sibling)
                cp.start()
                passed.append(cp)
        for i in range(n):
            copy(i, 0, sibling, me).wait_recv()
            for j, chip in enumerate(chips):
                copy(i, 4 + j, (*chip, 1 - c), me).wait_recv()
        for cp in first + passed:
            cp.wait_send()
        for cp in mine:
            cp.wait()

    any_spec = pl.BlockSpec(memory_space=pl.ANY)
    return pl.pallas_call(
        body, name=name,
        out_shape=[jax.ShapeDtypeStruct((N_DEV, *a.shape), a.dtype) for a in arrs],
        in_specs=[any_spec] * n, out_specs=[any_spec] * n,
        scratch_shapes=[pltpu.SemaphoreType.DMA((n, 7)), pltpu.SemaphoreType.DMA((n, 7)),
                        pltpu.SemaphoreType.DMA((n,))],
    )(*arrs)


def _place():
    x, y, c = lax.axis_index("x"), lax.axis_index("y"), lax.axis_index("c")
    return x, y, c, [(1 - x, y), (x, 1 - y), (1 - x, 1 - y)]


def _slot(p):
    return 4 * p[0] + 2 * p[1] + p[2]


class _Job:
    def __init__(self, ins, outs, nsem, copies, aliases=None, local=None):
        self.ins, self.outs, self.nsem, self.copies = list(ins), list(outs), nsem, copies
        self.aliases = aliases or {}
        self.local = local

    def scratch(self):
        s = [pltpu.SemaphoreType.DMA(self.nsem), pltpu.SemaphoreType.DMA(self.nsem)]
        if self.local is not None:
            s.append(pltpu.SemaphoreType.DMA((len(self.ins),)))
        return s

    def start(self, ins, outs, sems):
        if self.local is not None:
            for cp in self.local(ins, outs, sems[2]):
                cp.start()
        for cp in self.copies(ins, outs, sems[0], sems[1])[0]:
            cp.start()

    def finish(self, ins, outs, sems):
        started, awaited = self.copies(ins, outs, sems[0], sems[1])
        for cp in awaited:
            cp.wait_recv()
        for cp in started:
            cp.wait_send()
        if self.local is not None:
            for cp in self.local(ins, outs, sems[2]):
                cp.wait()


class _Jobs:
    def __init__(self, jobs):
        self.jobs = jobs
        self.ins = [a for j in jobs for a in j.ins]
        self.outs = [o for j in jobs for o in j.outs]
        self.two_phase = any(getattr(j, "two_phase", False) for j in jobs)
        self.aliases = {}
        at_in = at_out = 0
        for j in jobs:
            self.aliases.update({at_in + i: at_out + o for i, o in j.aliases.items()})
            at_in, at_out = at_in + len(j.ins), at_out + len(j.outs)

    def scratch(self):
        return [s for j in self.jobs for s in j.scratch()]

    def _each(self, ins, outs, sems):
        at_in = at_out = at_sem = 0
        for j in self.jobs:
            n_in, n_out, n_sem = len(j.ins), len(j.outs), len(j.scratch())
            yield j, ins[at_in:at_in + n_in], outs[at_out:at_out + n_out], sems[at_sem:at_sem + n_sem]
            at_in, at_out, at_sem = at_in + n_in, at_out + n_out, at_sem + n_sem

    def start(self, ins, outs, sems):
        for j, i, o, s in self._each(ins, outs, sems):
            j.start(i, o, s)

    def turn(self, ins, outs, sems):
        for j, i, o, s in self._each(ins, outs, sems):
            if getattr(j, "two_phase", False):
                j.turn(i, o, s)

    def finish(self, ins, outs, sems):
        for j, i, o, s in self._each(ins, outs, sems):
            j.finish(i, o, s)

    def split(self, results):
        at, parts = 0, []
        for j in self.jobs:
            parts.append(results[at:at + len(j.outs)])
            at += len(j.outs)
        return parts


def _remote(src, dst, send, recv, idx, to):
    return pltpu.make_async_remote_copy(src_ref=src, dst_ref=dst, send_sem=send.at[idx], recv_sem=recv.at[idx],
                                        device_id=to, device_id_type=MESH)


def _spread_copies(ins, outs, send, recv, base=0):
    x, y, c, chips = _place()
    me = (x, y, c)
    peers = [(x, y, 1 - c)] + [(*chip, c) for chip in chips]
    started, awaited = [], []
    for i, (src, dst) in enumerate(zip(ins, outs)):
        for k, peer in enumerate(peers):
            started.append(_remote(src, dst.at[_slot(me)], send, recv, (base + i, k), peer))
            awaited.append(_remote(src, dst.at[_slot(peer)], send, recv, (base + i, k), peer))
    return started, awaited


def _forward_copies(ins, outs, send, recv, base=0):
    x, y, c, chips = _place()
    started, awaited = [], []
    for i, buf in enumerate(outs):
        for j, chip in enumerate(chips):
            mine, theirs = buf.at[_slot((*chip, c))], buf.at[_slot((*chip, 1 - c))]
            started.append(_remote(mine, mine, send, recv, (base + i, j), (x, y, 1 - c)))
            awaited.append(_remote(theirs, theirs, send, recv, (base + i, j), (x, y, 1 - c)))
    return started, awaited


def _own_block_copies(ins, outs, sems):
    x, y, c, _ = _place()
    return [pltpu.make_async_copy(src, dst.at[_slot((x, y, c))], sems.at[i])
            for i, (src, dst) in enumerate(zip(ins, outs))]


def gather_spread_job(shards):
    outs = [jax.ShapeDtypeStruct((N_DEV, *a.shape), a.dtype) for a in shards]
    return _Job(shards, outs, (len(shards), 4), _spread_copies, local=_own_block_copies)


def gather_forward_job(fulls):
    outs = [jax.ShapeDtypeStruct(a.shape, a.dtype) for a in fulls]
    return _Job(fulls, outs, (len(fulls), 3), _forward_copies, aliases={i: i for i in range(len(fulls))})


TURN_EIGHTHS = 6


class _GatherJob:
    two_phase = True

    def __init__(self, shards):
        self.ins = list(shards)
        self.outs = [jax.ShapeDtypeStruct((N_DEV, *a.shape), a.dtype) for a in shards]
        self.aliases = {}

    def scratch(self):
        n = len(self.ins)
        return [pltpu.SemaphoreType.DMA((n, 4)), pltpu.SemaphoreType.DMA((n, 4)),
                pltpu.SemaphoreType.DMA((n, 3)), pltpu.SemaphoreType.DMA((n, 3)), pltpu.SemaphoreType.DMA((n,))]

    def start(self, ins, outs, sems):
        for cp in _own_block_copies(ins, outs, sems[4]) + _spread_copies(ins, outs, sems[0], sems[1])[0]:
            cp.start()

    def turn(self, ins, outs, sems):
        for cp in _spread_copies(ins, outs, sems[0], sems[1])[1]:
            cp.wait_recv()
        for cp in _forward_copies(outs, outs, sems[2], sems[3])[0]:
            cp.start()

    def finish(self, ins, outs, sems):
        handed_on, arriving = _forward_copies(outs, outs, sems[2], sems[3])
        for cp in arriving:
            cp.wait_recv()
        for cp in _spread_copies(ins, outs, sems[0], sems[1])[0] + handed_on:
            cp.wait_send()
        for cp in _own_block_copies(ins, outs, sems[4]):
            cp.wait()


def swap_job(gs):
    def copies(ins, outs, send, recv):
        x, y, c, _ = _place()
        started, awaited = [], []
        for i, (g, r1) in enumerate(zip(ins, outs)):
            for q in range(N_CHIP):
                started.append(_remote(g.at[2 * q + (1 - c)], r1.at[q], send, recv, (i, q), (x, y, 1 - c)))
                awaited.append(_remote(g.at[2 * q + c], r1.at[q], send, recv, (i, q), (x, y, 1 - c)))
        return started, awaited

    outs = [jax.ShapeDtypeStruct((N_CHIP, *g.shape[1:]), g.dtype) for g in gs]
    return _Job(gs, outs, (len(gs), N_CHIP), copies)


def chip_exchange_job(ps, rows=None, into=None):
    n = len(ps)

    def copies(ins, outs, send, recv):
        x, y, c, chips = _place()
        started, awaited = [], []
        for i, (p, r2) in enumerate(zip(ins[:n], outs)):
            for k, chip in enumerate(chips):
                src, mine, dst = p.at[2 * chip[0] + chip[1]], p.at[2 * x + y], r2.at[k]
                if rows is not None:
                    src, mine, dst = (t.at[pl.ds(rows[0], rows[1])] for t in (src, mine, dst))
                started.append(_remote(src, dst, send, recv, (i, k), (*chip, c)))
                awaited.append(_remote(mine, dst, send, recv, (i, k), (*chip, c)))
        return started, awaited

    outs = [jax.ShapeDtypeStruct((3, *p.shape[1:]), p.dtype) for p in ps]
    if into is None:
        return _Job(ps, outs, (n, 3), copies)
    return _Job(list(ps) + list(into), outs, (n, 3), copies, aliases={n + i: i for i in range(n)})


def _call(body, job, *, name, grid, in_specs, out_specs, out_shape, args, scratch_shapes=(), vmem=VMEM_LIMIT):
    if job is None:
        res = pl.pallas_call(
            body, name=name, grid=grid, in_specs=in_specs, out_specs=out_specs, out_shape=out_shape,
            scratch_shapes=list(scratch_shapes), compiler_params=_params(("arbitrary",) * len(grid), vmem),
        )(*args)
        return res, []
    n_in, n_out, n_scr = len(in_specs), len(out_specs), len(scratch_shapes)
    j_in, j_out = len(job.ins), len(job.outs)

    def with_copies(*refs):
        at = 0
        ins = refs[at:at + n_in]; at += n_in
        jins = refs[at:at + j_in]; at += j_in
        outs = refs[at:at + n_out]; at += n_out
        jouts = refs[at:at + j_out]; at += j_out
        scr = refs[at:at + n_scr]; at += n_scr
        sems = refs[at:]
        ids = [pl.program_id(d) for d in range(len(grid))]
        first = functools.reduce(jnp.logical_and, [i == 0 for i in ids])
        last = functools.reduce(jnp.logical_and, [i == n - 1 for i, n in zip(ids, grid)])

        @pl.when(first)
        def _():
            job.start(jins, jouts, sems)

        if getattr(job, "two_phase", False):
            steps, at = 1, 0
            for i, n in zip(ids, grid):
                steps, at = steps * n, at * n + i

            @pl.when(at == (TURN_EIGHTHS * steps) // 8)
            def _():
                job.turn(jins, jouts, sems)

        body(*ins, *outs, *scr)

        @pl.when(last)
        def _():
            job.finish(jins, jouts, sems)

    any_spec = pl.BlockSpec(memory_space=pl.ANY)
    res = pl.pallas_call(
        with_copies, name=name, grid=grid,
        in_specs=list(in_specs) + [any_spec] * j_in, out_specs=list(out_specs) + [any_spec] * j_out,
        out_shape=list(out_shape) + list(job.outs),
        input_output_aliases={n_in + i: n_out + o for i, o in job.aliases.items()},
        scratch_shapes=list(scratch_shapes) + job.scratch(),
        compiler_params=_params(("arbitrary",) * len(grid), vmem),
    )(*args, *job.ins)
    return res[:n_out], res[n_out:]


def run_job(job, name):
    def body(*refs):
        j_in, j_out = len(job.ins), len(job.outs)
        ins, outs, sems = refs[:j_in], refs[j_in:j_in + j_out], refs[j_in + j_out:]
        job.start(ins, outs, sems)
        job.finish(ins, outs, sems)

    any_spec = pl.BlockSpec(memory_space=pl.ANY)
    return pl.pallas_call(
        body, name=name, in_specs=[any_spec] * len(job.ins), out_specs=[any_spec] * len(job.outs),
        out_shape=list(job.outs), input_output_aliases=dict(job.aliases), scratch_shapes=job.scratch(),
    )(*job.ins)


def pair_sum(core, g, r1, name):
    _, rows, cols = g.shape
    rb = next(cand for cand in range(min(rows, 512), 0, -16) if rows % cand == 0)

    def body(core_ref, g_ref, r1_ref, p_ref, own_ref):
        del core_ref
        x, y, _, _ = _place()
        s = g_ref[0].astype(F32) + r1_ref[0].astype(F32)
        p_ref[0] = s.astype(BF16)

        @pl.when(pl.program_id(1) == 2 * x + y)
        def _():
            own_ref[...] = s

    chunk = (1, rb, cols)
    return pl.pallas_call(
        body, name=name,
        grid_spec=pltpu.PrefetchScalarGridSpec(
            num_scalar_prefetch=1, grid=(rows // rb, N_CHIP),
            in_specs=[pl.BlockSpec(chunk, lambda i, q, core_ref: (2 * q + core_ref[0], i, 0)),
                      pl.BlockSpec(chunk, lambda i, q, core_ref: (q, i, 0))],
            out_specs=[pl.BlockSpec(chunk, lambda i, q, core_ref: (q, i, 0)),
                       pl.BlockSpec((rb, cols), lambda i, q, core_ref: (i, 0))]),
        out_shape=[jax.ShapeDtypeStruct((N_CHIP, rows, cols), BF16), jax.ShapeDtypeStruct((rows, cols), F32)],
        compiler_params=_params(("arbitrary", "arbitrary")),
    )(core, g, r1)


def sum_devices(a, name):
    def body(a_ref, o_ref):
        acc = a_ref[0]
        for d in range(1, N_DEV):
            acc = acc + a_ref[d]
        o_ref[...] = acc

    return pl.pallas_call(body, name=name, out_shape=jax.ShapeDtypeStruct(a.shape[1:], F32))(a)


def _adam_update(w, g, m, v):
    nm = ADAM_B1 * m + (1.0 - ADAM_B1) * g
    nv = ADAM_B2 * v + (1.0 - ADAM_B2) * (g * g)
    m_hat = nm / (1.0 - ADAM_B1 ** ADAM_STEP)
    v_hat = nv / (1.0 - ADAM_B2 ** ADAM_STEP)
    return -ADAM_LR * (m_hat / (jnp.sqrt(v_hat) + ADAM_EPS) + ADAM_WD * w), nm, nv


def adamw(w, g, m, v, name, others=None):
    rows, cols = w.shape
    rb = rows
    for cand in range(min(rows, 512), 7, -8):
        if rows % cand == 0 and cand % 8 == 0:
            rb = cand
            break

    def body(*refs):
        if others is None:
            w_ref, g_ref, m_ref, v_ref, d_ref, nm_ref, nv_ref = refs
            gg = g_ref[...]
        else:
            w_ref, g_ref, m_ref, v_ref, r2_ref, go_ref, d_ref, nm_ref, nv_ref = refs
            gg = g_ref[...]
            for k in range(3):
                gg = gg + r2_ref[k].astype(F32)
            go_ref[...] = gg
        d_ref[...], nm_ref[...], nv_ref[...] = _adam_update(w_ref[...], gg, m_ref[...], v_ref[...])

    spec = pl.BlockSpec((rb, cols), lambda i: (i, 0))
    out = jax.ShapeDtypeStruct((rows, cols), F32)
    in_specs, args = [spec] * 4, [w, g, m, v]
    if others is not None:
        in_specs.append(pl.BlockSpec((3, rb, cols), lambda i: (0, i, 0)))
        args.append(others)
    n_out = 3 if others is None else 4
    res = pl.pallas_call(
        body, name=name, grid=(rows // rb,), in_specs=in_specs, out_specs=[spec] * n_out,
        out_shape=[out] * n_out, compiler_params=_params(("parallel",)),
    )(*args)
    return (g, *res) if others is None else tuple(res)


def adamw_small(items, name):
    n = len(items)

    def body(*refs):
        ins, outs = refs[:4 * n], refs[4 * n:]
        for i in range(n):
            w_ref, g_ref, m_ref, v_ref = ins[4 * i:4 * i + 4]
            d_ref, nm_ref, nv_ref = outs[3 * i:3 * i + 3]
            d_ref[...], nm_ref[...], nv_ref[...] = _adam_update(w_ref[...], g_ref[...], m_ref[...], v_ref[...])

    res = pl.pallas_call(
        body, name=name,
        out_shape=[jax.ShapeDtypeStruct(w.shape, F32) for w, _, _, _ in items for _ in range(3)],
    )(*[t for item in items for t in item])
    return [tuple(res[3 * i:3 * i + 3]) for i in range(n)]


def ada_fwd(c_all, w_cols, b_cols, name):
    def body(c_ref, w_ref, b_ref, cond_ref, mod_ref):
        cc = c_ref[...]
        cond = (cc * _sigmoid(cc)).astype(BF16)
        cond_ref[...] = cond
        mod_ref[...] = _dot(cond, w_ref[...].astype(BF16)) + b_ref[...]

    n, cols = c_all.shape[0], w_cols.shape[1]
    return pl.pallas_call(
        body, name=name,
        out_shape=[jax.ShapeDtypeStruct(c_all.shape, BF16), jax.ShapeDtypeStruct((n, cols), F32)],
        compiler_params=_params(),
    )(c_all, w_cols, b_cols)


def ada_bwd(cond_all, dmod_cols, name):
    def body(c_ref, d_ref, gw_ref, gb_ref):
        d = d_ref[...]
        gw_ref[...] = _dot_tn(c_ref[...], d.astype(BF16))
        gb_ref[...] = jnp.sum(d, axis=0, keepdims=True)

    dm, cols = cond_all.shape[1], dmod_cols.shape[1]
    return pl.pallas_call(
        body, name=name,
        out_shape=[jax.ShapeDtypeStruct((dm, cols), F32), jax.ShapeDtypeStruct((1, cols), F32)],
        compiler_params=_params(),
    )(cond_all, dmod_cols)


MXU_COLS = 256
FFN_CHUNK = 4 * MXU_COLS


def _hidden_chunks(ff):
    assert ff % MXU_COLS == 0
    return [(at, min(FFN_CHUNK, ff - at)) for at in range(0, ff, FFN_CHUNK)]


def _mod_spec(tiles_per_seq, dm):
    return pl.BlockSpec((1, 1, dm), lambda i: (i // tiles_per_seq, 0, 0))


def ffn_loss(x, sh, sc, gt, wgu, wd, ln_g, ln_b, target, seq, name):
    tokens, dm = x.shape
    ff = wgu.shape[1]
    chunks = _hidden_chunks(ff)
    tm = min(FFN_WIDE_TILE, seq)
    tiles_per_seq = seq // tm

    def body(x_ref, sh_ref, sc_ref, gt_ref, wgu_ref, wd_ref, lg_ref, lb_ref, t_ref,
             dr_ref, df_ref, gu_ref, a_ref, h_ref, loss_ref, dln_ref, dgt_ref):
        i = pl.program_id(0)
        xx = x_ref[...]
        h = (xx * (1.0 + sc_ref[0]) + sh_ref[0]).astype(BF16)
        h_ref[...] = h
        acc = jnp.zeros((tm, dm), F32)
        for at, wdt in chunks:
            gk = _dot_nt(h, wgu_ref[0, at:at + wdt, :])
            uk = _dot_nt(h, wgu_ref[1, at:at + wdt, :])
            gu_ref[0, :, at:at + wdt] = gk.astype(BF16)
            gu_ref[1, :, at:at + wdt] = uk.astype(BF16)
            a = (gk * _sigmoid(gk) * uk).astype(BF16)
            a_ref[:, at:at + wdt] = a
            acc = acc + _dot(a, wd_ref[at:at + wdt, :])
        half_gate = 0.5 * (1.0 + gt_ref[0])
        xhat, rstd = _ln_stats(DN_ALPHA * xx + half_gate * acc)
        err = xhat * lg_ref[...] + lb_ref[...] - t_ref[...]
        dr, dgain, dbias = _ln_bwd_normalized(err * (1.0 / dm), xhat, rstd, lg_ref[...])
        dr_ref[...] = dr
        df_ref[...] = (half_gate * dr).astype(BF16)

        @pl.when(i == 0)
        def _():
            loss_ref[...] = jnp.zeros_like(loss_ref)
            dln_ref[...] = jnp.zeros_like(dln_ref)

        @pl.when(i % tiles_per_seq == 0)
        def _():
            dgt_ref[...] = jnp.zeros_like(dgt_ref)

        loss_ref[...] += jnp.full((1, 128), (0.5 / dm) * jnp.sum(err * err), F32)
        dln_ref[0:1, :] += dgain
        dln_ref[1:2, :] += dbias
        dgt_ref[0] += jnp.sum(dr * (0.5 * acc), axis=0, keepdims=True)

    tile = pl.BlockSpec((tm, dm), lambda i: (i, 0))
    mod = _mod_spec(tiles_per_seq, dm)
    res, _ = _call(
        body, None, name=name, grid=(tokens // tm,),
        in_specs=[tile, mod, mod, mod, _const_spec(wgu.shape), _const_spec(wd.shape),
                  _const_spec((1, dm)), _const_spec((1, dm)), tile],
        out_specs=[tile, tile, pl.BlockSpec((2, tm, ff), lambda i: (0, i, 0)), pl.BlockSpec((tm, ff), lambda i: (i, 0)),
                   tile, pl.BlockSpec((1, 128), lambda i: (0, 0)), pl.BlockSpec((2, dm), lambda i: (0, 0)), mod],
        out_shape=[jax.ShapeDtypeStruct((tokens, dm), F32), jax.ShapeDtypeStruct((tokens, dm), BF16),
                   jax.ShapeDtypeStruct((2, tokens, ff), BF16), jax.ShapeDtypeStruct((tokens, ff), BF16),
                   jax.ShapeDtypeStruct((tokens, dm), BF16), jax.ShapeDtypeStruct((1, 128), F32),
                   jax.ShapeDtypeStruct((2, dm), F32), jax.ShapeDtypeStruct((tokens // seq, 1, dm), F32)],
        args=(x, sh, sc, gt, wgu, wd, ln_g, ln_b, target), vmem=FFN_WIDE_VMEM)
    return res


def ffn_up(x, sh, sc, wgu, seq, name, job=None):
    tokens, dm = x.shape
    ff = wgu.shape[1]
    chunks = _hidden_chunks(ff)
    tm = min(FFN_FWD_TILE, seq)

    def body(x_ref, sh_ref, sc_ref, wgu_ref, gu_ref, a_ref, h_ref):
        h = (x_ref[...] * (1.0 + sc_ref[0]) + sh_ref[0]).astype(BF16)
        h_ref[...] = h
        for at, wdt in chunks:
            gk = _dot_nt(h, wgu_ref[0, at:at + wdt, :])
            uk = _dot_nt(h, wgu_ref[1, at:at + wdt, :])
            gu_ref[0, :, at:at + wdt] = gk.astype(BF16)
            gu_ref[1, :, at:at + wdt] = uk.astype(BF16)
            a_ref[:, at:at + wdt] = (gk * _sigmoid(gk) * uk).astype(BF16)

    tile = pl.BlockSpec((tm, dm), lambda i: (i, 0))
    mod = _mod_spec(seq // tm, dm)
    return _call(
        body, job, name=name, grid=(tokens // tm,),
        in_specs=[tile, mod, mod, _const_spec(wgu.shape)],
        out_specs=[pl.BlockSpec((2, tm, ff), lambda i: (0, i, 0)), pl.BlockSpec((tm, ff), lambda i: (i, 0)), tile],
        out_shape=[jax.ShapeDtypeStruct((2, tokens, ff), BF16), jax.ShapeDtypeStruct((tokens, ff), BF16),
                   jax.ShapeDtypeStruct((tokens, dm), BF16)],
        args=(x, sh, sc, wgu))


def ffn_down(x, a, gt, wd, ln_g, ln_b, seq, name, job=None):
    tokens, dm = x.shape
    ff = wd.shape[0]
    chunks = _hidden_chunks(ff)
    tm = min(FFN_FWD_TILE, seq)

    def body(x_ref, a_ref, gt_ref, wd_ref, lg_ref, lb_ref, xo_ref, r_ref, f_ref):
        acc = jnp.zeros((tm, dm), F32)
        for at, wdt in chunks:
            acc = acc + _dot(a_ref[:, at:at + wdt], wd_ref[at:at + wdt, :])
        f_ref[...] = acc.astype(BF16)
        r = DN_ALPHA * x_ref[...] + (0.5 * (1.0 + gt_ref[0])) * acc
        r_ref[...] = r
        xhat, _ = _ln_stats(r)
        xo_ref[...] = xhat * lg_ref[...] + lb_ref[...]

    tile = pl.BlockSpec((tm, dm), lambda i: (i, 0))
    return _call(
        body, job, name=name, grid=(tokens // tm,),
        in_specs=[tile, pl.BlockSpec((tm, ff), lambda i: (i, 0)), _mod_spec(seq // tm, dm), _const_spec(wd.shape),
                  _const_spec((1, dm)), _const_spec((1, dm))],
        out_specs=[tile, tile, tile],
        out_shape=[jax.ShapeDtypeStruct((tokens, dm), F32), jax.ShapeDtypeStruct((tokens, dm), F32),
                   jax.ShapeDtypeStruct((tokens, dm), BF16)],
        args=(x, a, gt, wd, ln_g, ln_b))


def ffn_bwd(dr, df, x, gu, sc, wgu, wd, seq, name, job=None):
    tokens, dm = x.shape
    ff = wgu.shape[1]
    chunks = _hidden_chunks(ff)
    tm = min(FFN_WIDE_TILE, seq)
    tiles_per_seq = seq // tm
    nseq = tokens // seq

    def body(dr_ref, df_ref, x_ref, gu_ref, sc_ref, wgu_ref, wd_ref, dx_ref, dgu_ref, dmod_ref):
        @pl.when(pl.program_id(0) % tiles_per_seq == 0)
        def _():
            dmod_ref[...] = jnp.zeros_like(dmod_ref)

        df = df_ref[...]
        dh = jnp.zeros((tm, dm), F32)
        for at, wdt in chunks:
            cols = slice(at, at + wdt)
            da = _dot_nt(df, wd_ref[cols, :])
            gk = gu_ref[0, :, cols].astype(F32)
            uk = gu_ref[1, :, cols].astype(F32)
            sg = _sigmoid(gk)
            sil = gk * sg
            du = (da * sil).astype(BF16)
            dg = (da * uk * (sg * (1.0 + gk * (1.0 - sg)))).astype(BF16)
            dgu_ref[0, :, cols] = dg
            dgu_ref[1, :, cols] = du
            dh = dh + _dot(dg, wgu_ref[0, cols, :]) + _dot(du, wgu_ref[1, cols, :])
        dx_ref[...] = DN_ALPHA * dr_ref[...] + dh * (1.0 + sc_ref[0])
        dmod_ref[0, 0:1, :] += jnp.sum(dh, axis=0, keepdims=True)
        dmod_ref[0, 1:2, :] += jnp.sum(dh * x_ref[...], axis=0, keepdims=True)

    tile = pl.BlockSpec((tm, dm), lambda i: (i, 0))
    gu_spec = pl.BlockSpec((2, tm, ff), lambda i: (0, i, 0))
    return _call(
        body, job, name=name, grid=(tokens // tm,),
        in_specs=[tile, tile, tile, gu_spec, _mod_spec(tiles_per_seq, dm), _const_spec(wgu.shape), _const_spec(wd.shape)],
        out_specs=[tile, gu_spec, pl.BlockSpec((1, 2, dm), lambda i: (i // tiles_per_seq, 0, 0))],
        out_shape=[jax.ShapeDtypeStruct((tokens, dm), F32), jax.ShapeDtypeStruct((2, tokens, ff), BF16),
                   jax.ShapeDtypeStruct((nseq, 2, dm), F32)],
        args=(dr, df, x, gu, sc, wgu, wd), vmem=FFN_WIDE_VMEM)


def tn_matmul(a, b, name, job=None, b_cols=None, a_width=None):
    na, tokens, k_all = a.shape
    kk = k_all if a_width is None else a_width
    nka = k_all // kk
    assert nka * kk == k_all
    nb, _, cc = b.shape
    col = 0
    if b_cols is not None:
        col, cc = b_cols
    tt = tokens
    while 4 * tt * (kk + cc) + 8 * kk * cc > TN_VMEM_BUDGET and tt % 2 == 0 and tt > 256:
        tt //= 2
    steps = tokens // tt

    def body(a_ref, b_ref, o_ref, *acc):
        if steps == 1:
            o_ref[0, 0, 0] = _dot_tn(a_ref[0], b_ref[0]).astype(BF16)
            return
        acc_ref, = acc
        t = pl.program_id(3)

        @pl.when(t == 0)
        def _():
            acc_ref[...] = jnp.zeros_like(acc_ref)

        acc_ref[...] += _dot_tn(a_ref[0], b_ref[0])

        @pl.when(t == steps - 1)
        def _():
            o_ref[0, 0, 0] = acc_ref[...].astype(BF16)

    return _call(
        body, job, name=name, grid=(na, nka, nb, steps),
        in_specs=[pl.BlockSpec((1, tt, kk), lambda i, s, j, t: (i, t, s)),
                  pl.BlockSpec((1, tt, cc), lambda i, s, j, t: (j, t, col))],
        out_specs=[pl.BlockSpec((1, 1, 1, kk, cc), lambda i, s, j, t: (i, s, j, 0, 0))],
        out_shape=[jax.ShapeDtypeStruct((na, nka, nb, kk, cc), BF16)],
        scratch_shapes=[] if steps == 1 else [pltpu.VMEM((kk, cc), F32)], args=(a, b))


def proj_fwd(x1, sh, sc, w_in, seq, name, job=None):
    tokens, dm = x1.shape
    tm = min(MIX_TILE, seq)
    tiles_per_seq = seq // tm
    widths = [N_Q_HEADS * HEAD_DIM, N_KV_HEADS * HEAD_DIM, N_KV_HEADS * HEAD_DIM, 512, 512, 512]
    assert sum(widths) == w_in.shape[0]

    def body(x_ref, sh_ref, sc_ref, w_ref, *outs):
        h = (x_ref[...] * (1.0 + sc_ref[0]) + sh_ref[0]).astype(BF16)
        proj = _dot_nt(h, w_ref[...])
        at = 0
        for o_ref, wdt in zip(outs, widths):
            o_ref[...] = proj[:, at:at + wdt].astype(o_ref.dtype)
            at += wdt

    tile = pl.BlockSpec((tm, dm), lambda i: (i, 0))
    mod = _mod_spec(tiles_per_seq, dm)
    return _call(
        body, job, name=name, grid=(tokens // tm,),
        in_specs=[tile, mod, mod, _const_spec(w_in.shape)],
        out_specs=[pl.BlockSpec((tm, wdt), lambda i: (i, 0)) for wdt in widths],
        out_shape=[jax.ShapeDtypeStruct((tokens, wdt), F32 if i < 3 else BF16) for i, wdt in enumerate(widths)],
        args=(x1, sh, sc, w_in))


LANES = 2 * HEAD_DIM


def _head_lane(shape):
    return lax.broadcasted_iota(jnp.int32, shape, 1) % HEAD_DIM


def _lane_half(shape):
    return lax.broadcasted_iota(jnp.int32, shape, 1) // HEAD_DIM


def _swap_rot(v):
    lane = _head_lane(v.shape)
    half = ROT_DIM // 2
    return jnp.where(lane < half, pltpu.roll(v, LANES - half, 1),
                     jnp.where(lane < ROT_DIM, pltpu.roll(v, half, 1), 0.0))


def _rope(v, cos_t, sin_t):
    return v * cos_t + _swap_rot(v) * sin_t


def _unrope(dv, cos_t, sin_t):
    return dv * cos_t + _swap_rot(dv * sin_t)


def _both_halves(t, g):
    return jnp.where(_lane_half(t.shape) == g, t, pltpu.roll(t, HEAD_DIM, 1))


def _fold_halves(t, g):
    return jnp.where(_lane_half(t.shape) == g, t + pltpu.roll(t, HEAD_DIM, 1), 0.0)


def _stack_heads(blocks):
    rows = []
    for blk in blocks:
        half = _lane_half(blk.shape)
        rows += [jnp.where(half == 0, blk, 0.0), jnp.where(half == 1, blk, 0.0)]
    return jnp.concatenate(rows, axis=0)


def _unstack_heads(t, j):
    lo = t[(2 * j) * ATTN_BLOCK:(2 * j + 1) * ATTN_BLOCK]
    hi = t[(2 * j + 1) * ATTN_BLOCK:(2 * j + 2) * ATTN_BLOCK]
    return jnp.where(_lane_half(lo.shape) == 0, lo, hi)


def _band_mask(q0, w0):
    rows, cols = GQA_GROUP * ATTN_BLOCK, 2 * ATTN_BLOCK
    qi = lax.broadcasted_iota(jnp.int32, (rows, cols), 0) % ATTN_BLOCK + q0
    ki = lax.broadcasted_iota(jnp.int32, (rows, cols), 1) + w0
    diff = qi - ki
    return (diff >= 0) & (diff < ATTN_BLOCK)


def _attn_specs(seq):
    q_spec = pl.BlockSpec((seq, GQA_GROUP * HEAD_DIM), lambda b, g: (b, g))
    kv_spec = pl.BlockSpec((seq, LANES), lambda b, g: (b, 0))
    sink_spec = pl.BlockSpec((1, GQA_GROUP * ATTN_BLOCK, 1), lambda b, g: (g, 0, 0))
    return q_spec, kv_spec, sink_spec


def _block_starts(n):
    q0 = pl.multiple_of(n * ATTN_BLOCK, ATTN_BLOCK)
    w0 = pl.multiple_of(jnp.maximum(n - 1, 0) * ATTN_BLOCK, ATTN_BLOCK)
    return q0, w0


def _stacked_queries(ref, rows):
    return _stack_heads([ref[rows, j * LANES:(j + 1) * LANES] for j in range(2)]).astype(BF16)


def _sink_columns(sinks):
    return jnp.repeat(sinks.reshape(N_KV_HEADS, GQA_GROUP), ATTN_BLOCK, axis=1)[:, :, None]


def _probs_spec(nblk):
    return pl.BlockSpec((1, 1, nblk, GQA_GROUP * ATTN_BLOCK, 2 * ATTN_BLOCK), lambda b, g: (b, g, 0, 0, 0))


def _sink_probs_spec():
    return pl.BlockSpec((1, 1, GQA_GROUP * ATTN_BLOCK, LANES), lambda b, g: (b, g, 0, 0))


def attn_fwd(q, k, v, cos_t, sin_t, sinks, seq, name, job=None):
    tokens = q.shape[0]
    nblk = seq // ATTN_BLOCK
    assert nblk >= 2
    scale = HEAD_DIM ** -0.5

    nseq = tokens // seq
    rows_stacked = GQA_GROUP * ATTN_BLOCK
    assert nblk <= LANES

    def body(q_ref, k_ref, v_ref, cos_ref, sin_ref, sink_ref, o_ref, qr_ref, p_ref, ps_ref, kd_ref, vd_ref):
        g = pl.program_id(1)
        kd_ref[...] = _both_halves(_rope(k_ref[...].astype(F32), cos_ref[...], sin_ref[...]), g).astype(BF16)
        vd_ref[...] = _both_halves(v_ref[...].astype(F32), g).astype(BF16)
        sink = sink_ref[0]
        lane = lax.broadcasted_iota(jnp.int32, (rows_stacked, LANES), 1)

        ps_ref[...] = jnp.zeros_like(ps_ref)

        def block(n, carry):
            q0, w0 = _block_starts(n)
            rows, win = pl.ds(q0, ATTN_BLOCK), pl.ds(w0, 2 * ATTN_BLOCK)
            blocks = []
            for j in range(2):
                qr = _rope(q_ref[rows, j * LANES:(j + 1) * LANES].astype(F32), cos_ref[rows, :], sin_ref[rows, :]).astype(BF16)
                qr_ref[rows, j * LANES:(j + 1) * LANES] = qr
                blocks.append(qr)
            qs = _stack_heads(blocks)
            s = _dot_nt(qs, kd_ref[win, :]) * scale
            s = jnp.where(_band_mask(q0, w0), s, NEG_BIG)
            m = jnp.maximum(jnp.max(s, axis=-1, keepdims=True), sink)
            p = jnp.exp(s - m)
            e_sink = jnp.exp(sink - m)
            inv = pl.reciprocal(jnp.sum(p, axis=-1, keepdims=True) + e_sink, approx=True)
            pn = (p * inv).astype(BF16)
            p_ref[0, 0, n] = pn
            out = _dot(pn, vd_ref[win, :])
            for j in range(2):
                o_ref[rows, j * LANES:(j + 1) * LANES] = _unstack_heads(out, j).astype(o_ref.dtype)
            ps_ref[0, 0] = jnp.where(lane == n, e_sink * inv, ps_ref[0, 0])
            return carry

        lax.fori_loop(0, nblk, block, 0, unroll=2)

    q_spec, kv_spec, sink_spec = _attn_specs(seq)
    return _call(
        body, job, name=name, grid=(nseq, N_KV_HEADS),
        in_specs=[q_spec, kv_spec, kv_spec, kv_spec, kv_spec, sink_spec],
        out_specs=[q_spec, q_spec, _probs_spec(nblk), _sink_probs_spec()],
        out_shape=[jax.ShapeDtypeStruct(q.shape, BF16), jax.ShapeDtypeStruct(q.shape, BF16),
                   jax.ShapeDtypeStruct((nseq, N_KV_HEADS, nblk, rows_stacked, 2 * ATTN_BLOCK), BF16),
                   jax.ShapeDtypeStruct((nseq, N_KV_HEADS, rows_stacked, LANES), F32)],
        scratch_shapes=[pltpu.VMEM((seq, LANES), BF16), pltpu.VMEM((seq, LANES), BF16)],
        args=(q, k, v, cos_t, sin_t, _sink_columns(sinks)))


def attn_bwd(qr, k, v, do, probs, sink_probs, cos_t, sin_t, seq, name, job=None):
    tokens = qr.shape[0]
    nseq = tokens // seq
    nblk = seq // ATTN_BLOCK
    assert nblk >= 2
    rows_stacked = GQA_GROUP * ATTN_BLOCK
    scale = HEAD_DIM ** -0.5

    def body(q_ref, k_ref, v_ref, do_ref, p_ref, ps_ref, cos_ref, sin_ref, dq_ref, dk_ref, dv_ref, ds_ref,
             kd_ref, vd_ref, dkd_ref, dvd_ref, acc_ref):
        g = pl.program_id(1)
        kd_ref[...] = _both_halves(_rope(k_ref[...].astype(F32), cos_ref[...], sin_ref[...]), g).astype(BF16)
        vd_ref[...] = _both_halves(v_ref[...].astype(F32), g).astype(BF16)
        dkd_ref[...] = jnp.zeros_like(dkd_ref)
        dvd_ref[...] = jnp.zeros_like(dvd_ref)
        acc_ref[...] = jnp.zeros_like(acc_ref)
        lane = lax.broadcasted_iota(jnp.int32, (rows_stacked, LANES), 1)

        def block(n, carry):
            q0, w0 = _block_starts(n)
            rows, win = pl.ds(q0, ATTN_BLOCK), pl.ds(w0, 2 * ATTN_BLOCK)
            qs = _stacked_queries(q_ref, rows)
            dos = _stacked_queries(do_ref, rows)
            kw, vw = kd_ref[win, :], vd_ref[win, :]
            pn16 = p_ref[0, 0, n]
            pn = pn16.astype(F32)
            dvd_ref[win, :] += _dot_tn(pn16, dos)
            dp = _dot_nt(dos, vw)
            delta = jnp.sum(dp * pn, axis=-1, keepdims=True)
            ds = (pn * (dp - delta)).astype(BF16)
            dqs = _dot(ds, kw) * scale
            dkd_ref[win, :] += _dot_tn(ds, qs) * scale
            cos_b, sin_b = cos_ref[rows, :], sin_ref[rows, :]
            for j in range(2):
                dq_ref[rows, j * LANES:(j + 1) * LANES] = _unrope(_unstack_heads(dqs, j), cos_b, sin_b).astype(BF16)
            acc_ref[...] += jnp.where(lane == n, ps_ref[0, 0] * delta, 0.0)
            return carry

        lax.fori_loop(0, nblk // 2, lambda i, carry: block(2 * i + 1, block(2 * i, carry)), 0)
        ds_ref[0, 0] = -jnp.sum(acc_ref[...], axis=-1, keepdims=True)
        dk_g = _unrope(_fold_halves(dkd_ref[...], g), cos_ref[...], sin_ref[...])
        dv_g = _fold_halves(dvd_ref[...], g)

        @pl.when(g == 0)
        def _():
            dk_ref[...] = dk_g
            dv_ref[...] = dv_g

        @pl.when(g != 0)
        def _():
            dk_ref[...] += dk_g
            dv_ref[...] += dv_g

    q_spec, kv_spec, _ = _attn_specs(seq)
    return _call(
        body, job, name=name, grid=(nseq, N_KV_HEADS),
        in_specs=[q_spec, kv_spec, kv_spec, q_spec, _probs_spec(nblk), _sink_probs_spec(), kv_spec, kv_spec],
        out_specs=[q_spec, kv_spec, kv_spec, pl.BlockSpec((1, 1, rows_stacked, 1), lambda b, g: (b, g, 0, 0))],
        out_shape=[jax.ShapeDtypeStruct(qr.shape, BF16), jax.ShapeDtypeStruct(k.shape, F32),
                   jax.ShapeDtypeStruct(k.shape, F32), jax.ShapeDtypeStruct((nseq, N_KV_HEADS, rows_stacked, 1), F32)],
        scratch_shapes=[pltpu.VMEM((seq, LANES), BF16), pltpu.VMEM((seq, LANES), BF16),
                        pltpu.VMEM((seq, LANES), F32), pltpu.VMEM((seq, LANES), F32),
                        pltpu.VMEM((rows_stacked, LANES), F32)],
        args=(qr, k, v, do, probs, sink_probs, cos_t, sin_t))


CONV_COLS = 128


def _shift_down(z, by):
    t = lax.broadcasted_iota(jnp.int32, z.shape, 0)
    return jnp.where(t >= by, pltpu.roll(z, by, 0), 0.0)


def _shift_up(z, by):
    n = z.shape[0]
    t = lax.broadcasted_iota(jnp.int32, z.shape, 0)
    return jnp.where(t < n - by, pltpu.roll(z, n - by, 0), 0.0)


def conv_fwd(u, bg, cg, conv_w, seq, name):
    tokens, width = u.shape

    def body(u_ref, bg_ref, cg_ref, w_ref, o_ref):
        z = cg_ref[...].astype(F32) * u_ref[...].astype(F32)
        yy = w_ref[2:3, :] * z + w_ref[1:2, :] * _shift_down(z, 1) + w_ref[0:1, :] * _shift_down(z, 2)
        o_ref[...] = (bg_ref[...].astype(F32) * yy).astype(BF16)

    col = pl.BlockSpec((seq, CONV_COLS), lambda j, b: (b, j))
    return pl.pallas_call(
        body, name=name, grid=(width // CONV_COLS, tokens // seq),
        in_specs=[col, col, col, pl.BlockSpec((CONV_TAPS, CONV_COLS), lambda j, b: (0, j))],
        out_specs=col, out_shape=jax.ShapeDtypeStruct((tokens, width), BF16),
        compiler_params=_params(("parallel", "parallel")),
    )(u, bg, cg, conv_w)


def conv_bwd(dout, u, bg, cg, conv_w, seq, name):
    tokens, width = u.shape

    def body(do_ref, u_ref, bg_ref, cg_ref, w_ref, du_ref, dbg_ref, dcg_ref, dw_ref):
        uu, cg_v, do = u_ref[...].astype(F32), cg_ref[...].astype(F32), do_ref[...].astype(F32)
        z = cg_v * uu
        z1, z2 = _shift_down(z, 1), _shift_down(z, 2)
        yy = w_ref[2:3, :] * z + w_ref[1:2, :] * z1 + w_ref[0:1, :] * z2
        dbg_ref[...] = (do * yy).astype(BF16)
        dyy = do * bg_ref[...].astype(F32)
        dz = w_ref[2:3, :] * dyy + w_ref[1:2, :] * _shift_up(dyy, 1) + w_ref[0:1, :] * _shift_up(dyy, 2)
        du_ref[...] = (dz * cg_v).astype(BF16)
        dcg_ref[...] = (dz * uu).astype(BF16)

        @pl.when(pl.program_id(1) == 0)
        def _():
            dw_ref[...] = jnp.zeros_like(dw_ref)

        dw_ref[0:1, :] += jnp.sum(dyy * z2, axis=0, keepdims=True)
        dw_ref[1:2, :] += jnp.sum(dyy * z1, axis=0, keepdims=True)
        dw_ref[2:3, :] += jnp.sum(dyy * z, axis=0, keepdims=True)

    col = pl.BlockSpec((seq, CONV_COLS), lambda j, b: (b, j))
    w_spec = pl.BlockSpec((CONV_TAPS, CONV_COLS), lambda j, b: (0, j))
    act = jax.ShapeDtypeStruct((tokens, width), BF16)
    return pl.pallas_call(
        body, name=name, grid=(width // CONV_COLS, tokens // seq),
        in_specs=[col, col, col, col, w_spec], out_specs=[col, col, col, w_spec],
        out_shape=[act, act, act, jax.ShapeDtypeStruct((CONV_TAPS, width), F32)],
        compiler_params=_params(("parallel", "arbitrary")),
    )(dout, u, bg, cg, conv_w)


def out_fwd(x1, attn, conv, gt, w_out, ln_g, ln_b, seq, name, job=None):
    tokens, dm = x1.shape
    half = attn.shape[1]
    tm = min(MIX_TILE, seq)
    tiles_per_seq = seq // tm

    def body(x_ref, a_ref, c_ref, gt_ref, w_ref, lg_ref, lb_ref, xo_ref, r_ref, mi_ref, mix_ref):
        mixin = jnp.concatenate([a_ref[...], c_ref[...]], axis=1).astype(BF16)
        mi_ref[...] = mixin
        mix = _dot(mixin, w_ref[...])
        mix_ref[...] = mix.astype(BF16)
        r = DN_ALPHA * x_ref[...] + (1.0 + gt_ref[0]) * mix
        r_ref[...] = r
        xhat, _ = _ln_stats(r)
        xo_ref[...] = xhat * lg_ref[...] + lb_ref[...]

    tile = pl.BlockSpec((tm, dm), lambda i: (i, 0))
    htile = pl.BlockSpec((tm, half), lambda i: (i, 0))
    return _call(
        body, job, name=name, grid=(tokens // tm,),
        in_specs=[tile, htile, htile, _mod_spec(tiles_per_seq, dm), _const_spec(w_out.shape),
                  _const_spec((1, dm)), _const_spec((1, dm))],
        out_specs=[tile, tile, tile, tile],
        out_shape=[jax.ShapeDtypeStruct((tokens, dm), F32), jax.ShapeDtypeStruct((tokens, dm), F32),
                   jax.ShapeDtypeStruct((tokens, dm), BF16), jax.ShapeDtypeStruct((tokens, dm), BF16)],
        args=(x1, attn, conv, gt, w_out, ln_g, ln_b))


def out_bwd(dy, r, mix, gt, w_out, ln_g, seq, name, job=None):
    tokens, dm = r.shape
    half = dm // 2
    tm = min(MIX_TILE, seq)
    tiles_per_seq = seq // tm
    nseq = tokens // seq

    def body(dy_ref, r_ref, mix_ref, gt_ref, w_ref, lg_ref, dres_ref, da_ref, dc_ref, dmix_ref, dln_ref, dgt_ref):
        i = pl.program_id(0)
        dr, dgain, dbias = _ln_bwd(dy_ref[...], r_ref[...], lg_ref[...])

        @pl.when(i == 0)
        def _():
            dln_ref[...] = jnp.zeros_like(dln_ref)

        @pl.when(i % tiles_per_seq == 0)
        def _():
            dgt_ref[...] = jnp.zeros_like(dgt_ref)

        dln_ref[0:1, :] += dgain
        dln_ref[1:2, :] += dbias
        dgt_ref[0] += jnp.sum(dr * mix_ref[...].astype(F32), axis=0, keepdims=True)
        dres_ref[...] = DN_ALPHA * dr
        dmix = ((1.0 + gt_ref[0]) * dr).astype(BF16)
        dmix_ref[...] = dmix
        dmixin = _dot_nt(dmix, w_ref[...])
        da_ref[...] = dmixin[:, :half].astype(BF16)
        dc_ref[...] = dmixin[:, half:].astype(BF16)

    tile = pl.BlockSpec((tm, dm), lambda i: (i, 0))
    htile = pl.BlockSpec((tm, half), lambda i: (i, 0))
    return _call(
        body, job, name=name, grid=(tokens // tm,),
        in_specs=[tile, tile, tile, _mod_spec(tiles_per_seq, dm), _const_spec(w_out.shape), _const_spec((1, dm))],
        out_specs=[tile, htile, htile, tile, pl.BlockSpec((2, dm), lambda i: (0, 0)),
                   pl.BlockSpec((1, 1, dm), lambda i: (i // tiles_per_seq, 0, 0))],
        out_shape=[jax.ShapeDtypeStruct((tokens, dm), F32), jax.ShapeDtypeStruct((tokens, half), BF16),
                   jax.ShapeDtypeStruct((tokens, half), BF16), jax.ShapeDtypeStruct((tokens, dm), BF16),
                   jax.ShapeDtypeStruct((2, dm), F32), jax.ShapeDtypeStruct((nseq, 1, dm), F32)],
        args=(dy, r, mix, gt, w_out, ln_g))


def proj_bwd(parts, dres, x1, sh, sc, w_in, r_prev, f_prev, gt_prev, ln_g_prev, seq, name, job=None):
    tokens, dm = x1.shape
    tm = min(MIX_TILE, seq)
    tiles_per_seq = seq // tm
    nseq = tokens // seq
    widths = [p.shape[1] for p in parts]
    total = sum(widths)

    def body(*refs):
        part_refs = refs[:6]
        (dres_ref, x_ref, sh_ref, sc_ref, w_ref, r_ref, f_ref, gt_ref, lg_ref,
         dr_ref, df_ref, dproj_ref, h_ref, dmod_ref, dln_ref, dgt_ref) = refs[6:]
        i = pl.program_id(0)
        dproj = jnp.concatenate([p[...].astype(BF16) for p in part_refs], axis=1)
        dproj_ref[...] = dproj
        dh = _dot(dproj, w_ref[...])
        xx = x_ref[...]
        one_sc = 1.0 + sc_ref[0]
        h_ref[...] = (xx * one_sc + sh_ref[0]).astype(BF16)
        dr, dgain, dbias = _ln_bwd(dres_ref[...] + dh * one_sc, r_ref[...], lg_ref[...])
        dr_ref[...] = dr
        df_ref[...] = ((0.5 * (1.0 + gt_ref[0])) * dr).astype(BF16)

        @pl.when(i == 0)
        def _():
            dln_ref[...] = jnp.zeros_like(dln_ref)

        @pl.when(i % tiles_per_seq == 0)
        def _():
            dmod_ref[...] = jnp.zeros_like(dmod_ref)
            dgt_ref[...] = jnp.zeros_like(dgt_ref)

        dmod_ref[0, 0:1, :] += jnp.sum(dh, axis=0, keepdims=True)
        dmod_ref[0, 1:2, :] += jnp.sum(dh * xx, axis=0, keepdims=True)
        dln_ref[0:1, :] += dgain
        dln_ref[1:2, :] += dbias
        dgt_ref[0] += jnp.sum(dr * (0.5 * f_ref[...].astype(F32)), axis=0, keepdims=True)

    tile = pl.BlockSpec((tm, dm), lambda i: (i, 0))
    mod = _mod_spec(tiles_per_seq, dm)
    return _call(
        body, job, name=name, grid=(tokens // tm,),
        in_specs=[pl.BlockSpec((tm, wdt), lambda i: (i, 0)) for wdt in widths]
        + [tile, tile, mod, mod, _const_spec(w_in.shape), tile, tile, mod, _const_spec((1, dm))],
        out_specs=[tile, tile, pl.BlockSpec((tm, total), lambda i: (i, 0)), tile,
                   pl.BlockSpec((1, 2, dm), lambda i: (i // tiles_per_seq, 0, 0)),
                   pl.BlockSpec((2, dm), lambda i: (0, 0)), mod],
        out_shape=[jax.ShapeDtypeStruct((tokens, dm), F32), jax.ShapeDtypeStruct((tokens, dm), BF16),
                   jax.ShapeDtypeStruct((tokens, total), BF16), jax.ShapeDtypeStruct((tokens, dm), BF16),
                   jax.ShapeDtypeStruct((nseq, 2, dm), F32), jax.ShapeDtypeStruct((2, dm), F32),
                   jax.ShapeDtypeStruct((nseq, 1, dm), F32)],
        args=(*parts, dres, x1, sh, sc, w_in, r_prev, f_prev, gt_prev, ln_g_prev))


def _rope_tables(positions):
    half = ROT_DIM // 2
    inv_freq = np.power(np.float32(ROPE_THETA), -np.arange(0, ROT_DIM, 2, dtype=np.float32) / ROT_DIM)
    lane = np.arange(LANES) % HEAD_DIM
    freq = np.where(lane < ROT_DIM, inv_freq[lane % half], 0.0).astype(np.float32)
    sign = np.where(lane < half, -1.0, 1.0).astype(np.float32)
    ang = positions.astype(F32)[:, None] * freq[None, :]
    return jnp.cos(ang), sign[None, :] * jnp.sin(ang)


def kernel(x, c, positions, w_ada, b_ada, ffn1_w_gate_up, ffn1_w_down, ln1_g, ln1_b, w_in, conv_w, attn_sinks, w_out, ln2_g, ln2_b, ffn2_w_gate_up, ffn2_w_down, ln3_g, ln3_b, loss_target, m_w_ada, m_b_ada, m_ffn1_w_gate_up, m_ffn1_w_down, m_ln1_g, m_ln1_b, m_w_in, m_conv_w, m_attn_sinks, m_w_out, m_ln2_g, m_ln2_b, m_ffn2_w_gate_up, m_ffn2_w_down, m_ln3_g, m_ln3_b, v_w_ada, v_b_ada, v_ffn1_w_gate_up, v_ffn1_w_down, v_ln1_g, v_ln1_b, v_w_in, v_conv_w, v_attn_sinks, v_w_out, v_ln2_g, v_ln2_b, v_ffn2_w_gate_up, v_ffn2_w_down, v_ln3_g, v_ln3_b):
    nseq, seq, dm = x.shape
    tokens = nseq * seq
    dev = 4 * lax.axis_index("x") + 2 * lax.axis_index("y") + lax.axis_index("c")
    core = lax.axis_index("c").astype(jnp.int32).reshape(1)
    ada_cols = w_ada.shape[2]
    ff = ffn1_w_down.shape[1] * N_DEV
    fc = ff // 4
    in_cols = w_in.shape[2]
    conv_cols = conv_w.shape[2]

    def t_bf16(w):
        return w[0].T.astype(BF16)

    c_all, convw_all = all_gather([c, conv_w[0]], "gather_cond")
    c_all = c_all.reshape(N_DEV * nseq, dm)
    convw_full = convw_all.transpose(1, 0, 2).reshape(CONV_TAPS, N_DEV * conv_cols)

    b_cols = lax.dynamic_slice(b_ada, (0, dev * ada_cols), (1, ada_cols))
    cond_all, mod_cols = ada_fwd(c_all, w_ada[0], b_cols, "ada_fwd")
    wgu1, mod_all = all_gather([t_bf16(ffn1_w_gate_up), mod_cols], "gather_ffn1")
    wgu1 = wgu1.reshape(2, ff, dm)
    mod = lax.dynamic_slice(mod_all, (0, dev * nseq, 0), (N_DEV, nseq, ada_cols))
    mod = mod.transpose(1, 0, 2).reshape(nseq, 9, 1, dm)
    sh1, sc1, g1, sh2, sc2, g2, sh3, sc3, g3 = [mod[:, i] for i in range(9)]

    x0 = x.reshape(tokens, dm)
    (gu1, a1, h1), (wd1, wout) = ffn_up(x0, sh1, sc1, wgu1, seq, "ffn1_up",
                                        job=_GatherJob([ffn1_w_down[0].astype(BF16), w_out[0].astype(BF16)]))
    wd1, wout = wd1.reshape(ff, dm), wout.reshape(dm, dm)
    (x1, r1, f1), (win,) = ffn_down(x0, a1, g1, wd1, ln1_g, ln1_b, seq, "ffn1_down", job=_GatherJob([t_bf16(w_in)]))
    win = win.reshape(N_DEV * in_cols, dm)
    (q, k, v, u, bg, cg), wd2_spread = proj_fwd(x1, sh2, sc2, win, seq, "proj_fwd",
                                                job=gather_spread_job([ffn2_w_down[0].astype(BF16)]))
    cos_t, sin_t = _rope_tables(positions.reshape(tokens))
    sinks = attn_sinks[0]
    (attn, q_rot, probs, sink_probs), wgu2_spread = attn_fwd(q, k, v, cos_t, sin_t, sinks, seq, "attn_fwd",
                                                             job=gather_spread_job([t_bf16(ffn2_w_gate_up)]))
    conv = conv_fwd(u, bg, cg, convw_full, seq, "conv_fwd")
    (x2, r2, mixin, mix), (wd2, wgu2) = out_fwd(x1, attn, conv, g2, wout, ln2_g, ln2_b, seq, "out_fwd",
                                                job=gather_forward_job(wd2_spread + wgu2_spread))
    wd2, wgu2 = wd2.reshape(ff, dm), wgu2.reshape(2, ff, dm)
    target = loss_target.reshape(tokens, dm)
    dr3, df3, gu3, a3, h3, loss_part, dln3, dg3 = ffn_loss(x2, sh3, sc3, g3, wgu2, wd2, ln3_g, ln3_b, target, seq, "ffn2_fwd")

    (dx2, dgu3, dmod3), _ = ffn_bwd(dr3, df3, x2, gu3, sc3, wgu2, wd2, seq, "ffn2_bwd")
    pair = 2 * fc
    g_wd2 = tn_matmul(a3[None], df3[None], "ffn2_dwd", a_width=pair)[0][0].reshape(N_DEV, ff // N_DEV, dm)
    g_wgu2 = tn_matmul(dgu3, h3[None], "ffn2_dwgu", a_width=pair)[0][0].reshape(N_DEV, fc, dm)
    (dres2, dattn, dconv, dmix, dln2, dg2), swapped = out_bwd(dx2, r2, mix, g2, wout, ln2_g, seq, "out_bwd",
                                                              job=swap_job([g_wgu2, g_wd2]))
    p_wgu2, own_wgu2 = pair_sum(core, g_wgu2, swapped[0], "pair_wgu2")
    p_wd2, own_wd2 = pair_sum(core, g_wd2, swapped[1], "pair_wd2")
    du, dbg, dcg, dconvw = conv_bwd(dconv, u, bg, cg, convw_full, seq, "conv_bwd")
    (dq, dk, dv, dsink_rows), (far_wd2,) = attn_bwd(
        q_rot, k, v, dattn, probs, sink_probs, cos_t, sin_t, seq, "attn_bwd", job=chip_exchange_job([p_wd2]))
    parts = [dq, dk, dv, du, dbg, dcg]
    (dr1, df1, dproj, h2, dmod2, dln1, dg1), far_top = proj_bwd(
        parts, dres2, x1, sh2, sc2, win, r1, f1, g1, ln1_g, seq, "proj_bwd",
        job=chip_exchange_job([p_wgu2], rows=(0, fc // 2)))
    (dx0, dgu1, dmod1), _ = ffn_bwd(dr1, df1, x0, gu1, sc1, wgu1, wd1, seq, "ffn1_bwd")

    dmod = jnp.concatenate([dmod1, dg1, dmod2, dg2, dmod3, dg3], axis=1).reshape(nseq, 9 * dm)
    half = dm // 2
    jobs = _Jobs([gather_spread_job([dmod]),
                  chip_exchange_job([p_wgu2], rows=(fc // 2, fc // 2), into=far_top)])
    (g_wd1,), res = tn_matmul(a1[None], df1[None], "ffn1_dwd", job=jobs, a_width=pair)
    dmod_spread, (far_wgu2,) = jobs.split(res)
    g_wd1 = g_wd1.reshape(N_DEV, ff // N_DEV, dm)
    jobs = _Jobs([swap_job([g_wd1]), gather_forward_job(dmod_spread)])
    (g_l,), res = tn_matmul(dgu1, h1[None], "ffn1_dwgu_l", job=jobs, b_cols=(0, half), a_width=pair)
    (sw_wd1,), (dmod_all,) = jobs.split(res)
    g_l = g_l.reshape(N_DEV, fc, half)
    p_wd1, own_wd1 = pair_sum(core, g_wd1, sw_wd1, "pair_wd1")
    jobs = _Jobs([chip_exchange_job([p_wd1]), swap_job([g_l])])
    (g_r,), res = tn_matmul(dgu1, h1[None], "ffn1_dwgu_r", job=jobs, b_cols=(1, half), a_width=pair)
    (far_wd1,), (sw_l,) = jobs.split(res)
    g_r = g_r.reshape(N_DEV, fc, half)
    p_l, own_l = pair_sum(core, g_l, sw_l, "pair_wgu1_l")

    dmod_cols = lax.dynamic_slice(dmod_all.reshape(N_DEV * nseq, 9 * dm), (0, dev * ada_cols), (N_DEV * nseq, ada_cols))
    grad_w_ada, gb_cols = ada_bwd(cond_all, dmod_cols, "ada_bwd")
    dsinks = jnp.sum(dsink_rows.reshape(nseq, N_Q_HEADS, ATTN_BLOCK), axis=(0, 2))
    small = jnp.zeros((8, dm), F32)
    small = small.at[0:2].set(dln1).at[2:4].set(dln2).at[4:6].set(dln3)
    small = small.at[6, 0:N_Q_HEADS].set(dsinks).at[7, 0].set(loss_part[0, 0])

    jobs = _Jobs([chip_exchange_job([p_l]), swap_job([g_r]), gather_spread_job([small, dconvw, gb_cols])])
    (g_win,), res = tn_matmul(dproj[None], h2[None], "dwin", job=jobs)
    (far_l,), (sw_r,), small_spread = jobs.split(res)
    g_win = g_win.reshape(N_DEV, in_cols, dm)
    p_r, own_r = pair_sum(core, g_r, sw_r, "pair_wgu1_r")
    jobs = _Jobs([chip_exchange_job([p_r]), swap_job([g_win]), gather_forward_job(small_spread)])
    (g_wout,), res = tn_matmul(mixin[None], dmix[None], "dwout", job=jobs)
    (far_r,), (sw_win,), (small_all, dconvw_all, gb_all) = jobs.split(res)
    g_wout = g_wout.reshape(N_DEV, dm // N_DEV, dm)
    p_win, own_win = pair_sum(core, g_win, sw_win, "pair_win")

    given = dict(w_ada=(w_ada, m_w_ada, v_w_ada), b_ada=(b_ada, m_b_ada, v_b_ada),
                 ffn1_w_gate_up=(ffn1_w_gate_up, m_ffn1_w_gate_up, v_ffn1_w_gate_up),
                 ffn1_w_down=(ffn1_w_down, m_ffn1_w_down, v_ffn1_w_down),
                 ln1_g=(ln1_g, m_ln1_g, v_ln1_g), ln1_b=(ln1_b, m_ln1_b, v_ln1_b),
                 w_in=(w_in, m_w_in, v_w_in), conv_w=(conv_w, m_conv_w, v_conv_w),
                 attn_sinks=(attn_sinks, m_attn_sinks, v_attn_sinks), w_out=(w_out, m_w_out, v_w_out),
                 ln2_g=(ln2_g, m_ln2_g, v_ln2_g), ln2_b=(ln2_b, m_ln2_b, v_ln2_b),
                 ffn2_w_gate_up=(ffn2_w_gate_up, m_ffn2_w_gate_up, v_ffn2_w_gate_up),
                 ffn2_w_down=(ffn2_w_down, m_ffn2_w_down, v_ffn2_w_down),
                 ln3_g=(ln3_g, m_ln3_g, v_ln3_g), ln3_b=(ln3_b, m_ln3_b, v_ln3_b))
    transposed = ("ffn1_w_gate_up", "ffn2_w_gate_up", "w_in")

    def big_adamw(nm, grad, far=None):
        flip = nm in transposed
        w2, m2, v2 = [t[0].T if flip else t[0] for t in given[nm]]
        return [t.T[None] if flip else t[None] for t in adamw(w2, grad, m2, v2, "adamw_" + nm, others=far)]

    jobs = _Jobs([chip_exchange_job([p_win]), swap_job([g_wout])])
    (far_win,), (sw_wout,) = jobs.split(run_job(jobs, "rs_tail_win"))
    p_wout, own_wout = pair_sum(core, g_wout, sw_wout, "pair_wout")
    (far_wout,) = run_job(chip_exchange_job([p_wout]), "rs_tail_wout")

    grads = {
        "ffn1_w_gate_up": jnp.concatenate([own_l, own_r], axis=1), "ffn1_w_down": own_wd1,
        "w_in": own_win, "w_out": own_wout, "ffn2_w_gate_up": own_wgu2, "ffn2_w_down": own_wd2,
    }
    others = {"ffn1_w_gate_up": jnp.concatenate([far_l, far_r], axis=2), "ffn1_w_down": far_wd1,
              "w_in": far_win, "w_out": far_wout, "ffn2_w_gate_up": far_wgu2, "ffn2_w_down": far_wd2}
    results = {"w_ada": big_adamw("w_ada", grad_w_ada)}
    for nm in grads:
        results[nm] = big_adamw(nm, grads[nm], others[nm])

    small_sum = sum_devices(small_all, "sum_small")
    dconvw_sum = sum_devices(dconvw_all, "sum_convw")
    loss = small_sum[7, 0]
    grads["b_ada"] = gb_all.reshape(1, N_DEV * ada_cols)
    grads["conv_w"] = lax.dynamic_slice(dconvw_sum, (0, dev * conv_cols), (CONV_TAPS, conv_cols))
    grads["attn_sinks"] = small_sum[6:7, 0:N_Q_HEADS]
    for i, nm in enumerate(["ln1_g", "ln1_b", "ln2_g", "ln2_b", "ln3_g", "ln3_b"]):
        grads[nm] = small_sum[i:i + 1]

    order = ["w_ada", "b_ada", "ffn1_w_gate_up", "ffn1_w_down", "ln1_g", "ln1_b", "w_in", "conv_w", "attn_sinks",
             "w_out", "ln2_g", "ln2_b", "ffn2_w_gate_up", "ffn2_w_down", "ln3_g", "ln3_b"]
    small_names = [nm for nm in order if nm not in results]
    items = []
    for nm in small_names:
        shape = given[nm][0].shape
        two_d = (shape[-2], shape[-1])
        items.append((given[nm][0].reshape(two_d), grads[nm].reshape(two_d), *[t.reshape(two_d) for t in given[nm][1:]]))
    for nm, res in zip(small_names, adamw_small(items, "adamw_small")):
        shape = given[nm][0].shape
        results[nm] = [grads[nm].reshape(shape)] + [t.reshape(shape) for t in res]
    grad_x = dx0.reshape(nseq, seq, dm)
    return (loss, grad_x, *[results[nm][i] for i in range(4) for nm in order])
```

```python
import functools

import jax
import jax.numpy as jnp
import numpy as np
from jax import lax
from jax.experimental import pallas as pl
from jax.experimental.pallas import tpu as pltpu

F32 = jnp.float32
BF16 = jnp.bfloat16
MESH = pl.DeviceIdType.MESH

N_DEV = 8
N_CHIP = 4
HEAD_DIM = 64
N_Q_HEADS = 8
N_KV_HEADS = 2
GQA_GROUP = N_Q_HEADS // N_KV_HEADS
ATTN_BLOCK = 128
ROT_DIM = 16
ROPE_THETA = 500000.0
CONV_TAPS = 3
LN_EPS = 1e-5
DN_ALPHA = 2.0 ** 0.25
ADAM_LR = 0.001
ADAM_B1 = 0.9
ADAM_B2 = 0.999
ADAM_EPS = 1e-08
ADAM_WD = 0.01
ADAM_STEP = 10
NEG_BIG = -1e30

VMEM_LIMIT = 56 * 1024 * 1024
FFN_FWD_TILE = 512
MIX_TILE = 512
FFN_WIDE_TILE = 512
FFN_WIDE_VMEM = 62 * 1024 * 1024
TN_VMEM_BUDGET = 36 * 1024 * 1024


def _params(semantics=None, vmem=VMEM_LIMIT):
    return pltpu.CompilerParams(dimension_semantics=semantics, vmem_limit_bytes=vmem)


def _dot(a, b):
    return jnp.dot(a, b, preferred_element_type=F32)


def _dot_nt(a, b):
    return lax.dot_general(a, b, (((1,), (1,)), ((), ())), preferred_element_type=F32)


def _dot_tn(a, b):
    return lax.dot_general(a, b, (((0,), (0,)), ((), ())), preferred_element_type=F32)


def _sigmoid(x):
    return pl.reciprocal(1.0 + jnp.exp(-x), approx=True)


def _ln_stats(r):
    mu = jnp.mean(r, axis=-1, keepdims=True)
    d = r - mu
    var = jnp.mean(d * d, axis=-1, keepdims=True)
    rstd = lax.rsqrt(var + LN_EPS)
    return d * rstd, rstd


def _ln_bwd(dy, r, g):
    return _ln_bwd_normalized(dy, *_ln_stats(r), g)


def _ln_bwd_normalized(dy, xhat, rstd, g):
    dxhat = dy * g
    c1 = jnp.mean(dxhat, axis=-1, keepdims=True)
    c2 = jnp.mean(dxhat * xhat, axis=-1, keepdims=True)
    dr = rstd * (dxhat - c1 - xhat * c2)
    return dr, jnp.sum(dy * xhat, axis=0, keepdims=True), jnp.sum(dy, axis=0, keepdims=True)


def _const_spec(shape):
    nd = len(shape)
    return pl.BlockSpec(shape, lambda *_: (0,) * nd, pipeline_mode=pl.Buffered(1))


def all_gather(arrs, name):
    n = len(arrs)

    def body(*refs):
        ins, outs = refs[:n], refs[n:2 * n]
        send_sems, recv_sems, local_sems = refs[2 * n:]
        x, y, c = lax.axis_index("x"), lax.axis_index("y"), lax.axis_index("c")
        me, sibling = (x, y, c), (x, y, 1 - c)
        chips = [(1 - x, y), (x, 1 - y), (1 - x, 1 - y)]

        def slot(i, p):
            return outs[i].at[4 * p[0] + 2 * p[1] + p[2]]

        def copy(i, k, block, to, src=None):
            return pltpu.make_async_remote_copy(
                src_ref=slot(i, block) if src is None else src, dst_ref=slot(i, block),
                send_sem=send_sems.at[i, k], recv_sem=recv_sems.at[i, k],
                device_id=to, device_id_type=MESH)

        mine = [pltpu.make_async_copy(ins[i], slot(i, me), local_sems.at[i]) for i in range(n)]
        for cp in mine:
            cp.start()
        first = []
        for i in range(n):
            first.append(copy(i, 0, me, sibling, src=ins[i]))
            first += [copy(i, 1 + j, me, (*chip, c), src=ins[i]) for j, chip in enumerate(chips)]
        for cp in first:
            cp.start()
        passed = []
        for i in range(n):
            for j, chip in enumerate(chips):
                copy(i, 1 + j, (*chip, c), me).wait_recv()
                cp = copy(i, 4 + j, (*chip, c), sibling)
                cp.start()
                passed.append(cp)
        for i in range(n):
            copy(i, 0, sibling, me).wait_recv()
            for j, chip in enumerate(chips):
                copy(i, 4 + j, (*chip, 1 - c), me).wait_recv()
        for cp in first + passed:
            cp.wait_send()
        for cp in mine:
            cp.wait()

    any_spec = pl.BlockSpec(memory_space=pl.ANY)
    return pl.pallas_call(
        body, name=name,
        out_shape=[jax.ShapeDtypeStruct((N_DEV, *a.shape), a.dtype) for a in arrs],
        in_specs=[any_spec] * n, out_specs=[any_spec] * n,
        scratch_shapes=[pltpu.SemaphoreType.DMA((n, 7)), pltpu.SemaphoreType.DMA((n, 7)),
                        pltpu.SemaphoreType.DMA((n,))],
    )(*arrs)


def _place():
    x, y, c = lax.axis_index("x"), lax.axis_index("y"), lax.axis_index("c")
    return x, y, c, [(1 - x, y), (x, 1 - y), (1 - x, 1 - y)]


def _slot(p):
    return 4 * p[0] + 2 * p[1] + p[2]


class _Job:
    def __init__(self, ins, outs, nsem, copies, aliases=None, local=None):
        self.ins, self.outs, self.nsem, self.copies = list(ins), list(outs), nsem, copies
        self.aliases = aliases or {}
        self.local = local

    def scratch(self):
        s = [pltpu.SemaphoreType.DMA(self.nsem), pltpu.SemaphoreType.DMA(self.nsem)]
        if self.local is not None:
            s.append(pltpu.SemaphoreType.DMA((len(self.ins),)))
        return s

    def start(self, ins, outs, sems):
        if self.local is not None:
            for cp in self.local(ins, outs, sems[2]):
                cp.start()
        for cp in self.copies(ins, outs, sems[0], sems[1])[0]:
            cp.start()

    def finish(self, ins, outs, sems):
        started, awaited = self.copies(ins, outs, sems[0], sems[1])
        for cp in awaited:
            cp.wait_recv()
        for cp in started:
            cp.wait_send()
        if self.local is not None:
            for cp in self.local(ins, outs, sems[2]):
                cp.wait()


class _Jobs:
    def __init__(self, jobs):
        self.jobs = jobs
        self.ins = [a for j in jobs for a in j.ins]
        self.outs = [o for j in jobs for o in j.outs]
        self.two_phase = any(getattr(j, "two_phase", False) for j in jobs)
        self.aliases = {}
        at_in = at_out = 0
        for j in jobs:
            self.aliases.update({at_in + i: at_out + o for i, o in j.aliases.items()})
            at_in, at_out = at_in + len(j.ins), at_out + len(j.outs)

    def scratch(self):
        return [s for j in self.jobs for s in j.scratch()]

    def _each(self, ins, outs, sems):
        at_in = at_out = at_sem = 0
        for j in self.jobs:
            n_in, n_out, n_sem = len(j.ins), len(j.outs), len(j.scratch())
            yield j, ins[at_in:at_in + n_in], outs[at_out:at_out + n_out], sems[at_sem:at_sem + n_sem]
            at_in, at_out, at_sem = at_in + n_in, at_out + n_out, at_sem + n_sem

    def start(self, ins, outs, sems):
        for j, i, o, s in self._each(ins, outs, sems):
            j.start(i, o, s)

    def turn(self, ins, outs, sems):
        for j, i, o, s in self._each(ins, outs, sems):
            if getattr(j, "two_phase", False):
                j.turn(i, o, s)

    def finish(self, ins, outs, sems):
        for j, i, o, s in self._each(ins, outs, sems):
            j.finish(i, o, s)

    def split(self, results):
        at, parts = 0, []
        for j in self.jobs:
            parts.append(results[at:at + len(j.outs)])
            at += len(j.outs)
        return parts


def _remote(src, dst, send, recv, idx, to):
    return pltpu.make_async_remote_copy(src_ref=src, dst_ref=dst, send_sem=send.at[idx], recv_sem=recv.at[idx],
                                        device_id=to, device_id_type=MESH)


def _spread_copies(ins, outs, send, recv, base=0):
    x, y, c, chips = _place()
    me = (x, y, c)
    peers = [(x, y, 1 - c)] + [(*chip, c) for chip in chips]
    started, awaited = [], []
    for i, (src, dst) in enumerate(zip(ins, outs)):
        for k, peer in enumerate(peers):
            started.append(_remote(src, dst.at[_slot(me)], send, recv, (base + i, k), peer))
            awaited.append(_remote(src, dst.at[_slot(peer)], send, recv, (base + i, k), peer))
    return started, awaited


def _forward_copies(ins, outs, send, recv, base=0):
    x, y, c, chips = _place()
    started, awaited = [], []
    for i, buf in enumerate(outs):
        for j, chip in enumerate(chips):
            mine, theirs = buf.at[_slot((*chip, c))], buf.at[_slot((*chip, 1 - c))]
            started.append(_remote(mine, mine, send, recv, (base + i, j), (x, y, 1 - c)))
            awaited.append(_remote(theirs, theirs, send, recv, (base + i, j), (x, y, 1 - c)))
    return started, awaited


def _own_block_copies(ins, outs, sems):
    x, y, c, _ = _place()
    return [pltpu.make_async_copy(src, dst.at[_slot((x, y, c))], sems.at[i])
            for i, (src, dst) in enumerate(zip(ins, outs))]


def gather_spread_job(shards):
    outs = [jax.ShapeDtypeStruct((N_DEV, *a.shape), a.dtype) for a in shards]
    return _Job(shards, outs, (len(shards), 4), _spread_copies, local=_own_block_copies)


def gather_forward_job(fulls):
    outs = [jax.ShapeDtypeStruct(a.shape, a.dtype) for a in fulls]
    return _Job(fulls, outs, (len(fulls), 3), _forward_copies, aliases={i: i for i in range(len(fulls))})


TURN_EIGHTHS = 6


class _GatherJob:
    two_phase = True

    def __init__(self, shards):
        self.ins = list(shards)
        self.outs = [jax.ShapeDtypeStruct((N_DEV, *a.shape), a.dtype) for a in shards]
        self.aliases = {}

    def scratch(self):
        n = len(self.ins)
        return [pltpu.SemaphoreType.DMA((n, 4)), pltpu.SemaphoreType.DMA((n, 4)),
                pltpu.SemaphoreType.DMA((n, 3)), pltpu.SemaphoreType.DMA((n, 3)), pltpu.SemaphoreType.DMA((n,))]

    def start(self, ins, outs, sems):
        for cp in _own_block_copies(ins, outs, sems[4]) + _spread_copies(ins, outs, sems[0], sems[1])[0]:
            cp.start()

    def turn(self, ins, outs, sems):
        for cp in _spread_copies(ins, outs, sems[0], sems[1])[1]:
            cp.wait_recv()
        for cp in _forward_copies(outs, outs, sems[2], sems[3])[0]:
            cp.start()

    def finish(self, ins, outs, sems):
        handed_on, arriving = _forward_copies(outs, outs, sems[2], sems[3])
        for cp in arriving:
            cp.wait_recv()
        for cp in _spread_copies(ins, outs, sems[0], sems[1])[0] + handed_on:
            cp.wait_send()
        for cp in _own_block_copies(ins, outs, sems[4]):
            cp.wait()


def swap_job(gs):
    def copies(ins, outs, send, recv):
        x, y, c, _ = _place()
        started, awaited = [], []
        for i, (g, r1) in enumerate(zip(ins, outs)):
            for q in range(N_CHIP):
                started.append(_remote(g.at[2 * q + (1 - c)], r1.at[q], send, recv, (i, q), (x, y, 1 - c)))
                awaited.append(_remote(g.at[2 * q + c], r1.at[q], send, recv, (i, q), (x, y, 1 - c)))
        return started, awaited

    outs = [jax.ShapeDtypeStruct((N_CHIP, *g.shape[1:]), g.dtype) for g in gs]
    return _Job(gs, outs, (len(gs), N_CHIP), copies)


def chip_exchange_job(ps, rows=None, into=None):
    n = len(ps)

    def copies(ins, outs, send, recv):
        x, y, c, chips = _place()
        started, awaited = [], []
        for i, (p, r2) in enumerate(zip(ins[:n], outs)):
            for k, chip in enumerate(chips):
                src, mine, dst = p.at[2 * chip[0] + chip[1]], p.at[2 * x + y], r2.at[k]
                if rows is not None:
                    src, mine, dst = (t.at[pl.ds(rows[0], rows[1])] for t in (src, mine, dst))
                started.append(_remote(src, dst, send, recv, (i, k), (*chip, c)))
                awaited.append(_remote(mine, dst, send, recv, (i, k), (*chip, c)))
        return started, awaited

    outs = [jax.ShapeDtypeStruct((3, *p.shape[1:]), p.dtype) for p in ps]
    if into is None:
        return _Job(ps, outs, (n, 3), copies)
    return _Job(list(ps) + list(into), outs, (n, 3), copies, aliases={n + i: i for i in range(n)})


def _call(body, job, *, name, grid, in_specs, out_specs, out_shape, args, scratch_shapes=(), vmem=VMEM_LIMIT):
    if job is None:
        res = pl.pallas_call(
            body, name=name, grid=grid, in_specs=in_specs, out_specs=out_specs, out_shape=out_shape,
            scratch_shapes=list(scratch_shapes), compiler_params=_params(("arbitrary",) * len(grid), vmem),
        )(*args)
        return res, []
    n_in, n_out, n_scr = len(in_specs), len(out_specs), len(scratch_shapes)
    j_in, j_out = len(job.ins), len(job.outs)

    def with_copies(*refs):
        at = 0
        ins = refs[at:at + n_in]; at += n_in
        jins = refs[at:at + j_in]; at += j_in
        outs = refs[at:at + n_out]; at += n_out
        jouts = refs[at:at + j_out]; at += j_out
        scr = refs[at:at + n_scr]; at += n_scr
        sems = refs[at:]
        ids = [pl.program_id(d) for d in range(len(grid))]
        first = functools.reduce(jnp.logical_and, [i == 0 for i in ids])
        last = functools.reduce(jnp.logical_and, [i == n - 1 for i, n in zip(ids, grid)])

        @pl.when(first)
        def _():
            job.start(jins, jouts, sems)

        if getattr(job, "two_phase", False):
            steps, at = 1, 0
            for i, n in zip(ids, grid):
                steps, at = steps * n, at * n + i

            @pl.when(at == (TURN_EIGHTHS * steps) // 8)
            def _():
                job.turn(jins, jouts, sems)

        body(*ins, *outs, *scr)

        @pl.when(last)
        def _():
            job.finish(jins, jouts, sems)

    any_spec = pl.BlockSpec(memory_space=pl.ANY)
    res = pl.pallas_call(
        with_copies, name=name, grid=grid,
        in_specs=list(in_specs) + [any_spec] * j_in, out_specs=list(out_specs) + [any_spec] * j_out,
        out_shape=list(out_shape) + list(job.outs),
        input_output_aliases={n_in + i: n_out + o for i, o in job.aliases.items()},
        scratch_shapes=list(scratch_shapes) + job.scratch(),
        compiler_params=_params(("arbitrary",) * len(grid), vmem),
    )(*args, *job.ins)
    return res[:n_out], res[n_out:]


def run_job(job, name):
    def body(*refs):
        j_in, j_out = len(job.ins), len(job.outs)
        ins, outs, sems = refs[:j_in], refs[j_in:j_in + j_out], refs[j_in + j_out:]
        job.start(ins, outs, sems)
        job.finish(ins, outs, sems)

    any_spec = pl.BlockSpec(memory_space=pl.ANY)
    return pl.pallas_call(
        body, name=name, in_specs=[any_spec] * len(job.ins), out_specs=[any_spec] * len(job.outs),
        out_shape=list(job.outs), input_output_aliases=dict(job.aliases), scratch_shapes=job.scratch(),
    )(*job.ins)


def pair_sum(core, g, r1, name):
    _, rows, cols = g.shape
    rb = next(cand for cand in range(min(rows, 512), 0, -16) if rows % cand == 0)

    def body(core_ref, g_ref, r1_ref, p_ref, own_ref):
        del core_ref
        x, y, _, _ = _place()
        s = g_ref[0].astype(F32) + r1_ref[0].astype(F32)
        p_ref[0] = s.astype(BF16)

        @pl.when(pl.program_id(1) == 2 * x + y)
        def _():
            own_ref[...] = s

    chunk = (1, rb, cols)
    return pl.pallas_call(
        body, name=name,
        grid_spec=pltpu.PrefetchScalarGridSpec(
            num_scalar_prefetch=1, grid=(rows // rb, N_CHIP),
            in_specs=[pl.BlockSpec(chunk, lambda i, q, core_ref: (2 * q + core_ref[0], i, 0)),
                      pl.BlockSpec(chunk, lambda i, q, core_ref: (q, i, 0))],
            out_specs=[pl.BlockSpec(chunk, lambda i, q, core_ref: (q, i, 0)),
                       pl.BlockSpec((rb, cols), lambda i, q, core_ref: (i, 0))]),
        out_shape=[jax.ShapeDtypeStruct((N_CHIP, rows, cols), BF16), jax.ShapeDtypeStruct((rows, cols), F32)],
        compiler_params=_params(("arbitrary", "arbitrary")),
    )(core, g, r1)


def sum_devices(a, name):
    def body(a_ref, o_ref):
        acc = a_ref[0]
        for d in range(1, N_DEV):
            acc = acc + a_ref[d]
        o_ref[...] = acc

    return pl.pallas_call(body, name=name, out_shape=jax.ShapeDtypeStruct(a.shape[1:], F32))(a)


def _adam_update(w, g, m, v):
    nm = ADAM_B1 * m + (1.0 - ADAM_B1) * g
    nv = ADAM_B2 * v + (1.0 - ADAM_B2) * (g * g)
    m_hat = nm / (1.0 - ADAM_B1 ** ADAM_STEP)
    v_hat = nv / (1.0 - ADAM_B2 ** ADAM_STEP)
    return -ADAM_LR * (m_hat / (jnp.sqrt(v_hat) + ADAM_EPS) + ADAM_WD * w), nm, nv


def adamw(w, g, m, v, name, others=None):
    rows, cols = w.shape
    g_parts = list(g) if isinstance(g, (list, tuple)) else [g]
    other_parts = [] if others is None else list(others) if isinstance(others, (list, tuple)) else [others]
    n_parts = len(g_parts)
    assert others is not None or n_parts == 1
    assert others is None or len(other_parts) == n_parts
    assert sum(p.shape[1] for p in g_parts) == cols
    rb = rows
    for cand in range(min(rows, 512), 7, -8):
        if rows % cand == 0 and cand % 8 == 0:
            rb = cand
            break

    def body(*refs):
        w_ref, m_ref, v_ref = refs[:3]
        g_refs = refs[3:3 + n_parts]
        if others is None:
            d_ref, nm_ref, nv_ref = refs[3 + n_parts:]
            gg = g_refs[0][...]
        else:
            r2_refs = refs[3 + n_parts:3 + 2 * n_parts]
            go_ref, d_ref, nm_ref, nv_ref = refs[3 + 2 * n_parts:]
            blocks = []
            for g_ref, r2_ref in zip(g_refs, r2_refs):
                part = g_ref[...]
                for k in range(3):
                    part = part + r2_ref[k].astype(F32)
                blocks.append(part)
            gg = blocks[0] if n_parts == 1 else jnp.concatenate(blocks, axis=1)
            go_ref[...] = gg
        d_ref[...], nm_ref[...], nv_ref[...] = _adam_update(w_ref[...], gg, m_ref[...], v_ref[...])

    spec = pl.BlockSpec((rb, cols), lambda i: (i, 0))
    out = jax.ShapeDtypeStruct((rows, cols), F32)
    in_specs, args = [spec] * 3, [w, m, v]
    in_specs += [pl.BlockSpec((rb, p.shape[1]), lambda i: (i, 0)) for p in g_parts]
    args += g_parts
    in_specs += [pl.BlockSpec((3, rb, p.shape[2]), lambda i: (0, i, 0)) for p in other_parts]
    args += other_parts
    n_out = 3 if others is None else 4
    res = pl.pallas_call(
        body, name=name, grid=(rows // rb,), in_specs=in_specs, out_specs=[spec] * n_out,
        out_shape=[out] * n_out, compiler_params=_params(("parallel",)),
    )(*args)
    return (g, *res) if others is None else tuple(res)


def adamw_small(items, name):
    n = len(items)

    def body(*refs):
        ins, outs = refs[:4 * n], refs[4 * n:]
        for i in range(n):
            w_ref, g_ref, m_ref, v_ref = ins[4 * i:4 * i + 4]
            d_ref, nm_ref, nv_ref = outs[3 * i:3 * i + 3]
            d_ref[...], nm_ref[...], nv_ref[...] = _adam_update(w_ref[...], g_ref[...], m_ref[...], v_ref[...])

    res = pl.pallas_call(
        body, name=name,
        out_shape=[jax.ShapeDtypeStruct(w.shape, F32) for w, _, _, _ in items for _ in range(3)],
    )(*[t for item in items for t in item])
    return [tuple(res[3 * i:3 * i + 3]) for i in range(n)]


def ada_fwd(c_all, w_cols, b_cols, name):
    def body(c_ref, w_ref, b_ref, cond_ref, mod_ref):
        cc = c_ref[...]
        cond = (cc * _sigmoid(cc)).astype(BF16)
        cond_ref[...] = cond
        mod_ref[...] = _dot(cond, w_ref[...].astype(BF16)) + b_ref[...]

    n, cols = c_all.shape[0], w_cols.shape[1]
    return pl.pallas_call(
        body, name=name,
        out_shape=[jax.ShapeDtypeStruct(c_all.shape, BF16), jax.ShapeDtypeStruct((n, cols), F32)],
        compiler_params=_params(),
    )(c_all, w_cols, b_cols)


def ada_bwd(cond_all, dmod_cols, name):
    def body(c_ref, d_ref, gw_ref, gb_ref):
        d = d_ref[...]
        gw_ref[...] = _dot_tn(c_ref[...], d.astype(BF16))
        gb_ref[...] = jnp.sum(d, axis=0, keepdims=True)

    dm, cols = cond_all.shape[1], dmod_cols.shape[1]
    return pl.pallas_call(
        body, name=name,
        out_shape=[jax.ShapeDtypeStruct((dm, cols), F32), jax.ShapeDtypeStruct((1, cols), F32)],
        compiler_params=_params(),
    )(cond_all, dmod_cols)


MXU_COLS = 256
FFN_CHUNK = 4 * MXU_COLS


def _hidden_chunks(ff):
    assert ff % MXU_COLS == 0
    return [(at, min(FFN_CHUNK, ff - at)) for at in range(0, ff, FFN_CHUNK)]


def _mod_spec(tiles_per_seq, dm):
    return pl.BlockSpec((1, 1, dm), lambda i: (i // tiles_per_seq, 0, 0))


def ffn_loss(x, sh, sc, gt, wgu, wd, ln_g, ln_b, target, seq, name):
    tokens, dm = x.shape
    ff = wgu.shape[1]
    chunks = _hidden_chunks(ff)
    tm = min(FFN_WIDE_TILE, seq)
    tiles_per_seq = seq // tm

    def body(x_ref, sh_ref, sc_ref, gt_ref, wgu_ref, wd_ref, lg_ref, lb_ref, t_ref,
             dr_ref, df_ref, gu_ref, a_ref, h_ref, loss_ref, dln_ref, dgt_ref):
        i = pl.program_id(0)
        xx = x_ref[...]
        h = (xx * (1.0 + sc_ref[0]) + sh_ref[0]).astype(BF16)
        h_ref[...] = h
        acc = jnp.zeros((tm, dm), F32)
        for at, wdt in chunks:
            gk = _dot_nt(h, wgu_ref[0, at:at + wdt, :])
            uk = _dot_nt(h, wgu_ref[1, at:at + wdt, :])
            gu_ref[0, :, at:at + wdt] = gk.astype(BF16)
            gu_ref[1, :, at:at + wdt] = uk.astype(BF16)
            a = (gk * _sigmoid(gk) * uk).astype(BF16)
            a_ref[:, at:at + wdt] = a
            acc = acc + _dot(a, wd_ref[at:at + wdt, :])
        half_gate = 0.5 * (1.0 + gt_ref[0])
        xhat, rstd = _ln_stats(DN_ALPHA * xx + half_gate * acc)
        err = xhat * lg_ref[...] + lb_ref[...] - t_ref[...]
        dr, dgain, dbias = _ln_bwd_normalized(err * (1.0 / dm), xhat, rstd, lg_ref[...])
        dr_ref[...] = dr
        df_ref[...] = (half_gate * dr).astype(BF16)

        @pl.when(i == 0)
        def _():
            loss_ref[...] = jnp.zeros_like(loss_ref)
            dln_ref[...] = jnp.zeros_like(dln_ref)

        @pl.when(i % tiles_per_seq == 0)
        def _():
            dgt_ref[...] = jnp.zeros_like(dgt_ref)

        loss_ref[...] += jnp.full((1, 128), (0.5 / dm) * jnp.sum(err * err), F32)
        dln_ref[0:1, :] += dgain
        dln_ref[1:2, :] += dbias
        dgt_ref[0] += jnp.sum(dr * (0.5 * acc), axis=0, keepdims=True)

    tile = pl.BlockSpec((tm, dm), lambda i: (i, 0))
    mod = _mod_spec(tiles_per_seq, dm)
    res, _ = _call(
        body, None, name=name, grid=(tokens // tm,),
        in_specs=[tile, mod, mod, mod, _const_spec(wgu.shape), _const_spec(wd.shape),
                  _const_spec((1, dm)), _const_spec((1, dm)), tile],
        out_specs=[tile, tile, pl.BlockSpec((2, tm, ff), lambda i: (0, i, 0)), pl.BlockSpec((tm, ff), lambda i: (i, 0)),
                   tile, pl.BlockSpec((1, 128), lambda i: (0, 0)), pl.BlockSpec((2, dm), lambda i: (0, 0)), mod],
        out_shape=[jax.ShapeDtypeStruct((tokens, dm), F32), jax.ShapeDtypeStruct((tokens, dm), BF16),
                   jax.ShapeDtypeStruct((2, tokens, ff), BF16), jax.ShapeDtypeStruct((tokens, ff), BF16),
                   jax.ShapeDtypeStruct((tokens, dm), BF16), jax.ShapeDtypeStruct((1, 128), F32),
                   jax.ShapeDtypeStruct((2, dm), F32), jax.ShapeDtypeStruct((tokens // seq, 1, dm), F32)],
        args=(x, sh, sc, gt, wgu, wd, ln_g, ln_b, target), vmem=FFN_WIDE_VMEM)
    return res


def ffn_up(x, sh, sc, wgu, seq, name, job=None):
    tokens, dm = x.shape
    ff = wgu.shape[1]
    chunks = _hidden_chunks(ff)
    tm = min(FFN_FWD_TILE, seq)

    def body(x_ref, sh_ref, sc_ref, wgu_ref, gu_ref, a_ref, h_ref):
        h = (x_ref[...] * (1.0 + sc_ref[0]) + sh_ref[0]).astype(BF16)
        h_ref[...] = h
        for at, wdt in chunks:
            gk = _dot_nt(h, wgu_ref[0, at:at + wdt, :])
            uk = _dot_nt(h, wgu_ref[1, at:at + wdt, :])
            gu_ref[0, :, at:at + wdt] = gk.astype(BF16)
            gu_ref[1, :, at:at + wdt] = uk.astype(BF16)
            a_ref[:, at:at + wdt] = (gk * _sigmoid(gk) * uk).astype(BF16)

    tile = pl.BlockSpec((tm, dm), lambda i: (i, 0))
    mod = _mod_spec(seq // tm, dm)
    return _call(
        body, job, name=name, grid=(tokens // tm,),
        in_specs=[tile, mod, mod, _const_spec(wgu.shape)],
        out_specs=[pl.BlockSpec((2, tm, ff), lambda i: (0, i, 0)), pl.BlockSpec((tm, ff), lambda i: (i, 0)), tile],
        out_shape=[jax.ShapeDtypeStruct((2, tokens, ff), BF16), jax.ShapeDtypeStruct((tokens, ff), BF16),
                   jax.ShapeDtypeStruct((tokens, dm), BF16)],
        args=(x, sh, sc, wgu))


def ffn_down(x, a, gt, wd, ln_g, ln_b, seq, name, job=None):
    tokens, dm = x.shape
    ff = wd.shape[0]
    chunks = _hidden_chunks(ff)
    tm = min(FFN_FWD_TILE, seq)

    def body(x_ref, a_ref, gt_ref, wd_ref, lg_ref, lb_ref, xo_ref, r_ref, f_ref):
        acc = jnp.zeros((tm, dm), F32)
        for at, wdt in chunks:
            acc = acc + _dot(a_ref[:, at:at + wdt], wd_ref[at:at + wdt, :])
        f_ref[...] = acc.astype(BF16)
        r = DN_ALPHA * x_ref[...] + (0.5 * (1.0 + gt_ref[0])) * acc
        r_ref[...] = r
        xhat, _ = _ln_stats(r)
        xo_ref[...] = xhat * lg_ref[...] + lb_ref[...]

    tile = pl.BlockSpec((tm, dm), lambda i: (i, 0))
    return _call(
        body, job, name=name, grid=(tokens // tm,),
        in_specs=[tile, pl.BlockSpec((tm, ff), lambda i: (i, 0)), _mod_spec(seq // tm, dm), _const_spec(wd.shape),
                  _const_spec((1, dm)), _const_spec((1, dm))],
        out_specs=[tile, tile, tile],
        out_shape=[jax.ShapeDtypeStruct((tokens, dm), F32), jax.ShapeDtypeStruct((tokens, dm), F32),
                   jax.ShapeDtypeStruct((tokens, dm), BF16)],
        args=(x, a, gt, wd, ln_g, ln_b))


def ffn_bwd(dr, df, x, gu, sc, wgu, wd, seq, name, job=None):
    tokens, dm = x.shape
    ff = wgu.shape[1]
    chunks = _hidden_chunks(ff)
    tm = min(FFN_WIDE_TILE, seq)
    tiles_per_seq = seq // tm
    nseq = tokens // seq

    def body(dr_ref, df_ref, x_ref, gu_ref, sc_ref, wgu_ref, wd_ref, dx_ref, dgu_ref, dmod_ref):
        @pl.when(pl.program_id(0) % tiles_per_seq == 0)
        def _():
            dmod_ref[...] = jnp.zeros_like(dmod_ref)

        df = df_ref[...]
        dh = jnp.zeros((tm, dm), F32)
        for at, wdt in chunks:
            cols = slice(at, at + wdt)
            da = _dot_nt(df, wd_ref[cols, :])
            gk = gu_ref[0, :, cols].astype(F32)
            uk = gu_ref[1, :, cols].astype(F32)
            sg = _sigmoid(gk)
            sil = gk * sg
            du = (da * sil).astype(BF16)
            dg = (da * uk * (sg * (1.0 + gk * (1.0 - sg)))).astype(BF16)
            dgu_ref[0, :, cols] = dg
            dgu_ref[1, :, cols] = du
            dh = dh + _dot(dg, wgu_ref[0, cols, :]) + _dot(du, wgu_ref[1, cols, :])
        dx_ref[...] = DN_ALPHA * dr_ref[...] + dh * (1.0 + sc_ref[0])
        dmod_ref[0, 0:1, :] += jnp.sum(dh, axis=0, keepdims=True)
        dmod_ref[0, 1:2, :] += jnp.sum(dh * x_ref[...], axis=0, keepdims=True)

    tile = pl.BlockSpec((tm, dm), lambda i: (i, 0))
    gu_spec = pl.BlockSpec((2, tm, ff), lambda i: (0, i, 0))
    return _call(
        body, job, name=name, grid=(tokens // tm,),
        in_specs=[tile, tile, tile, gu_spec, _mod_spec(tiles_per_seq, dm), _const_spec(wgu.shape), _const_spec(wd.shape)],
        out_specs=[tile, gu_spec, pl.BlockSpec((1, 2, dm), lambda i: (i // tiles_per_seq, 0, 0))],
        out_shape=[jax.ShapeDtypeStruct((tokens, dm), F32), jax.ShapeDtypeStruct((2, tokens, ff), BF16),
                   jax.ShapeDtypeStruct((nseq, 2, dm), F32)],
        args=(dr, df, x, gu, sc, wgu, wd), vmem=FFN_WIDE_VMEM)


def tn_matmul(a, b, name, job=None, b_cols=None, a_width=None):
    na, tokens, k_all = a.shape
    kk = k_all if a_width is None else a_width
    nka = k_all // kk
    assert nka * kk == k_all
    nb, _, cc = b.shape
    col = 0
    if b_cols is not None:
        col, cc = b_cols
    tt = tokens
    while 4 * tt * (kk + cc) + 8 * kk * cc > TN_VMEM_BUDGET and tt % 2 == 0 and tt > 256:
        tt //= 2
    steps = tokens // tt

    def body(a_ref, b_ref, o_ref, *acc):
        if steps == 1:
            o_ref[0, 0, 0] = _dot_tn(a_ref[0], b_ref[0]).astype(BF16)
            return
        acc_ref, = acc
        t = pl.program_id(3)

        @pl.when(t == 0)
        def _():
            acc_ref[...] = jnp.zeros_like(acc_ref)

        acc_ref[...] += _dot_tn(a_ref[0], b_ref[0])

        @pl.when(t == steps - 1)
        def _():
            o_ref[0, 0, 0] = acc_ref[...].astype(BF16)

    return _call(
        body, job, name=name, grid=(na, nka, nb, steps),
        in_specs=[pl.BlockSpec((1, tt, kk), lambda i, s, j, t: (i, t, s)),
                  pl.BlockSpec((1, tt, cc), lambda i, s, j, t: (j, t, col))],
        out_specs=[pl.BlockSpec((1, 1, 1, kk, cc), lambda i, s, j, t: (i, s, j, 0, 0))],
        out_shape=[jax.ShapeDtypeStruct((na, nka, nb, kk, cc), BF16)],
        scratch_shapes=[] if steps == 1 else [pltpu.VMEM((kk, cc), F32)], args=(a, b))


def proj_fwd(x1, sh, sc, w_in, seq, name, job=None):
    tokens, dm = x1.shape
    tm = min(MIX_TILE, seq)
    tiles_per_seq = seq // tm
    widths = [N_Q_HEADS * HEAD_DIM, N_KV_HEADS * HEAD_DIM, N_KV_HEADS * HEAD_DIM, 512, 512, 512]
    assert sum(widths) == w_in.shape[0]

    def body(x_ref, sh_ref, sc_ref, w_ref, *outs):
        h = (x_ref[...] * (1.0 + sc_ref[0]) + sh_ref[0]).astype(BF16)
        proj = _dot_nt(h, w_ref[...])
        at = 0
        for o_ref, wdt in zip(outs, widths):
            o_ref[...] = proj[:, at:at + wdt].astype(o_ref.dtype)
            at += wdt

    tile = pl.BlockSpec((tm, dm), lambda i: (i, 0))
    mod = _mod_spec(tiles_per_seq, dm)
    return _call(
        body, job, name=name, grid=(tokens // tm,),
        in_specs=[tile, mod, mod, _const_spec(w_in.shape)],
        out_specs=[pl.BlockSpec((tm, wdt), lambda i: (i, 0)) for wdt in widths],
        out_shape=[jax.ShapeDtypeStruct((tokens, wdt), F32 if i < 3 else BF16) for i, wdt in enumerate(widths)],
        args=(x1, sh, sc, w_in))


LANES = 2 * HEAD_DIM


def _head_lane(shape):
    return lax.broadcasted_iota(jnp.int32, shape, 1) % HEAD_DIM


def _lane_half(shape):
    return lax.broadcasted_iota(jnp.int32, shape, 1) // HEAD_DIM


def _swap_rot(v):
    lane = _head_lane(v.shape)
    half = ROT_DIM // 2
    return jnp.where(lane < half, pltpu.roll(v, LANES - half, 1),
                     jnp.where(lane < ROT_DIM, pltpu.roll(v, half, 1), 0.0))


def _rope(v, cos_t, sin_t):
    return v * cos_t + _swap_rot(v) * sin_t


def _unrope(dv, cos_t, sin_t):
    return dv * cos_t + _swap_rot(dv * sin_t)


def _both_halves(t, g):
    return jnp.where(_lane_half(t.shape) == g, t, pltpu.roll(t, HEAD_DIM, 1))


def _fold_halves(t, g):
    return jnp.where(_lane_half(t.shape) == g, t + pltpu.roll(t, HEAD_DIM, 1), 0.0)


def _stack_heads(blocks):
    rows = []
    for blk in blocks:
        half = _lane_half(blk.shape)
        rows += [jnp.where(half == 0, blk, 0.0), jnp.where(half == 1, blk, 0.0)]
    return jnp.concatenate(rows, axis=0)


def _unstack_heads(t, j):
    lo = t[(2 * j) * ATTN_BLOCK:(2 * j + 1) * ATTN_BLOCK]
    hi = t[(2 * j + 1) * ATTN_BLOCK:(2 * j + 2) * ATTN_BLOCK]
    return jnp.where(_lane_half(lo.shape) == 0, lo, hi)


def _band_mask(q0, w0):
    rows, cols = GQA_GROUP * ATTN_BLOCK, 2 * ATTN_BLOCK
    qi = lax.broadcasted_iota(jnp.int32, (rows, cols), 0) % ATTN_BLOCK + q0
    ki = lax.broadcasted_iota(jnp.int32, (rows, cols), 1) + w0
    diff = qi - ki
    return (diff >= 0) & (diff < ATTN_BLOCK)


def _attn_specs(seq):
    q_spec = pl.BlockSpec((seq, GQA_GROUP * HEAD_DIM), lambda b, g: (b, g))
    kv_spec = pl.BlockSpec((seq, LANES), lambda b, g: (b, 0))
    sink_spec = pl.BlockSpec((1, GQA_GROUP * ATTN_BLOCK, 1), lambda b, g: (g, 0, 0))
    return q_spec, kv_spec, sink_spec


def _block_starts(n):
    q0 = pl.multiple_of(n * ATTN_BLOCK, ATTN_BLOCK)
    w0 = pl.multiple_of(jnp.maximum(n - 1, 0) * ATTN_BLOCK, ATTN_BLOCK)
    return q0, w0


def _stacked_queries(ref, rows):
    return _stack_heads([ref[rows, j * LANES:(j + 1) * LANES] for j in range(2)]).astype(BF16)


def _sink_columns(sinks):
    return jnp.repeat(sinks.reshape(N_KV_HEADS, GQA_GROUP), ATTN_BLOCK, axis=1)[:, :, None]


def _probs_spec(nblk):
    return pl.BlockSpec((1, 1, nblk, GQA_GROUP * ATTN_BLOCK, 2 * ATTN_BLOCK), lambda b, g: (b, g, 0, 0, 0))


def _sink_probs_spec():
    return pl.BlockSpec((1, 1, GQA_GROUP * ATTN_BLOCK, LANES), lambda b, g: (b, g, 0, 0))


def attn_fwd(q, k, v, cos_t, sin_t, sinks, seq, name, job=None):
    tokens = q.shape[0]
    nblk = seq // ATTN_BLOCK
    assert nblk >= 2
    scale = HEAD_DIM ** -0.5

    nseq = tokens // seq
    rows_stacked = GQA_GROUP * ATTN_BLOCK
    assert nblk <= LANES

    def body(q_ref, k_ref, v_ref, cos_ref, sin_ref, sink_ref, o_ref, qr_ref, p_ref, ps_ref, kd_ref, vd_ref):
        g = pl.program_id(1)
        kd_ref[...] = _both_halves(_rope(k_ref[...].astype(F32), cos_ref[...], sin_ref[...]), g).astype(BF16)
        vd_ref[...] = _both_halves(v_ref[...].astype(F32), g).astype(BF16)
        sink = sink_ref[0]
        lane = lax.broadcasted_iota(jnp.int32, (rows_stacked, LANES), 1)

        ps_ref[...] = jnp.zeros_like(ps_ref)

        def block(n, carry):
            q0, w0 = _block_starts(n)
            rows, win = pl.ds(q0, ATTN_BLOCK), pl.ds(w0, 2 * ATTN_BLOCK)
            blocks = []
            for j in range(2):
                qr = _rope(q_ref[rows, j * LANES:(j + 1) * LANES].astype(F32), cos_ref[rows, :], sin_ref[rows, :]).astype(BF16)
                qr_ref[rows, j * LANES:(j + 1) * LANES] = qr
                blocks.append(qr)
            qs = _stack_heads(blocks)
            s = _dot_nt(qs, kd_ref[win, :]) * scale
            s = jnp.where(_band_mask(q0, w0), s, NEG_BIG)
            m = jnp.maximum(jnp.max(s, axis=-1, keepdims=True), sink)
            p = jnp.exp(s - m)
            e_sink = jnp.exp(sink - m)
            inv = pl.reciprocal(jnp.sum(p, axis=-1, keepdims=True) + e_sink, approx=True)
            pn = (p * inv).astype(BF16)
            p_ref[0, 0, n] = pn
            out = _dot(pn, vd_ref[win, :])
            for j in range(2):
                o_ref[rows, j * LANES:(j + 1) * LANES] = _unstack_heads(out, j).astype(o_ref.dtype)
            ps_ref[0, 0] = jnp.where(lane == n, e_sink * inv, ps_ref[0, 0])
            return carry

        lax.fori_loop(0, nblk, block, 0, unroll=2)

    q_spec, kv_spec, sink_spec = _attn_specs(seq)
    return _call(
        body, job, name=name, grid=(nseq, N_KV_HEADS),
        in_specs=[q_spec, kv_spec, kv_spec, kv_spec, kv_spec, sink_spec],
        out_specs=[q_spec, q_spec, _probs_spec(nblk), _sink_probs_spec()],
        out_shape=[jax.ShapeDtypeStruct(q.shape, BF16), jax.ShapeDtypeStruct(q.shape, BF16),
                   jax.ShapeDtypeStruct((nseq, N_KV_HEADS, nblk, rows_stacked, 2 * ATTN_BLOCK), BF16),
                   jax.ShapeDtypeStruct((nseq, N_KV_HEADS, rows_stacked, LANES), F32)],
        scratch_shapes=[pltpu.VMEM((seq, LANES), BF16), pltpu.VMEM((seq, LANES), BF16)],
        args=(q, k, v, cos_t, sin_t, _sink_columns(sinks)))


def attn_bwd(qr, k, v, do, probs, sink_probs, cos_t, sin_t, seq, name, job=None):
    tokens = qr.shape[0]
    nseq = tokens // seq
    nblk = seq // ATTN_BLOCK
    assert nblk >= 2
    rows_stacked = GQA_GROUP * ATTN_BLOCK
    scale = HEAD_DIM ** -0.5

    def body(q_ref, k_ref, v_ref, do_ref, p_ref, ps_ref, cos_ref, sin_ref, dq_ref, dk_ref, dv_ref, ds_ref,
             kd_ref, vd_ref, dkd_ref, dvd_ref, acc_ref):
        g = pl.program_id(1)
        kd_ref[...] = _both_halves(_rope(k_ref[...].astype(F32), cos_ref[...], sin_ref[...]), g).astype(BF16)
        vd_ref[...] = _both_halves(v_ref[...].astype(F32), g).astype(BF16)
        dkd_ref[...] = jnp.zeros_like(dkd_ref)
        dvd_ref[...] = jnp.zeros_like(dvd_ref)
        acc_ref[...] = jnp.zeros_like(acc_ref)
        lane = lax.broadcasted_iota(jnp.int32, (rows_stacked, LANES), 1)

        def block(n, carry):
            q0, w0 = _block_starts(n)
            rows, win = pl.ds(q0, ATTN_BLOCK), pl.ds(w0, 2 * ATTN_BLOCK)
            qs = _stacked_queries(q_ref, rows)
            dos = _stacked_queries(do_ref, rows)
            kw, vw = kd_ref[win, :], vd_ref[win, :]
            pn16 = p_ref[0, 0, n]
            pn = pn16.astype(F32)
            dvd_ref[win, :] += _dot_tn(pn16, dos)
            dp = _dot_nt(dos, vw)
            delta = jnp.sum(dp * pn, axis=-1, keepdims=True)
            ds = (pn * (dp - delta)).astype(BF16)
            dqs = _dot(ds, kw) * scale
            dkd_ref[win, :] += _dot_tn(ds, qs) * scale
            cos_b, sin_b = cos_ref[rows, :], sin_ref[rows, :]
            for j in range(2):
                dq_ref[rows, j * LANES:(j + 1) * LANES] = _unrope(_unstack_heads(dqs, j), cos_b, sin_b).astype(BF16)
            acc_ref[...] += jnp.where(lane == n, ps_ref[0, 0] * delta, 0.0)
            return carry

        lax.fori_loop(0, nblk // 2, lambda i, carry: block(2 * i + 1, block(2 * i, carry)), 0)
        ds_ref[0, 0] = -jnp.sum(acc_ref[...], axis=-1, keepdims=True)
        dk_g = _unrope(_fold_halves(dkd_ref[...], g), cos_ref[...], sin_ref[...])
        dv_g = _fold_halves(dvd_ref[...], g)

        @pl.when(g == 0)
        def _():
            dk_ref[...] = dk_g
            dv_ref[...] = dv_g

        @pl.when(g != 0)
        def _():
            dk_ref[...] += dk_g
            dv_ref[...] += dv_g

    q_spec, kv_spec, _ = _attn_specs(seq)
    return _call(
        body, job, name=name, grid=(nseq, N_KV_HEADS),
        in_specs=[q_spec, kv_spec, kv_spec, q_spec, _probs_spec(nblk), _sink_probs_spec(), kv_spec, kv_spec],
        out_specs=[q_spec, kv_spec, kv_spec, pl.BlockSpec((1, 1, rows_stacked, 1), lambda b, g: (b, g, 0, 0))],
        out_shape=[jax.ShapeDtypeStruct(qr.shape, BF16), jax.ShapeDtypeStruct(k.shape, F32),
                   jax.ShapeDtypeStruct(k.shape, F32), jax.ShapeDtypeStruct((nseq, N_KV_HEADS, rows_stacked, 1), F32)],
        scratch_shapes=[pltpu.VMEM((seq, LANES), BF16), pltpu.VMEM((seq, LANES), BF16),
                        pltpu.VMEM((seq, LANES), F32), pltpu.VMEM((seq, LANES), F32),
                        pltpu.VMEM((rows_stacked, LANES), F32)],
        args=(qr, k, v, do, probs, sink_probs, cos_t, sin_t))


CONV_COLS = 128


def _shift_down(z, by):
    t = lax.broadcasted_iota(jnp.int32, z.shape, 0)
    return jnp.where(t >= by, pltpu.roll(z, by, 0), 0.0)


def _shift_up(z, by):
    n = z.shape[0]
    t = lax.broadcasted_iota(jnp.int32, z.shape, 0)
    return jnp.where(t < n - by, pltpu.roll(z, n - by, 0), 0.0)


def conv_fwd(u, bg, cg, conv_w, seq, name):
    tokens, width = u.shape

    def body(u_ref, bg_ref, cg_ref, w_ref, o_ref):
        z = cg_ref[...].astype(F32) * u_ref[...].astype(F32)
        yy = w_ref[2:3, :] * z + w_ref[1:2, :] * _shift_down(z, 1) + w_ref[0:1, :] * _shift_down(z, 2)
        o_ref[...] = (bg_ref[...].astype(F32) * yy).astype(BF16)

    col = pl.BlockSpec((seq, CONV_COLS), lambda j, b: (b, j))
    return pl.pallas_call(
        body, name=name, grid=(width // CONV_COLS, tokens // seq),
        in_specs=[col, col, col, pl.BlockSpec((CONV_TAPS, CONV_COLS), lambda j, b: (0, j))],
        out_specs=col, out_shape=jax.ShapeDtypeStruct((tokens, width), BF16),
        compiler_params=_params(("parallel", "parallel")),
    )(u, bg, cg, conv_w)


def conv_bwd(dout, u, bg, cg, conv_w, seq, name):
    tokens, width = u.shape

    def body(do_ref, u_ref, bg_ref, cg_ref, w_ref, du_ref, dbg_ref, dcg_ref, dw_ref):
        uu, cg_v, do = u_ref[...].astype(F32), cg_ref[...].astype(F32), do_ref[...].astype(F32)
        z = cg_v * uu
        z1, z2 = _shift_down(z, 1), _shift_down(z, 2)
        yy = w_ref[2:3, :] * z + w_ref[1:2, :] * z1 + w_ref[0:1, :] * z2
        dbg_ref[...] = (do * yy).astype(BF16)
        dyy = do * bg_ref[...].astype(F32)
        dz = w_ref[2:3, :] * dyy + w_ref[1:2, :] * _shift_up(dyy, 1) + w_ref[0:1, :] * _shift_up(dyy, 2)
        du_ref[...] = (dz * cg_v).astype(BF16)
        dcg_ref[...] = (dz * uu).astype(BF16)

        @pl.when(pl.program_id(1) == 0)
        def _():
            dw_ref[...] = jnp.zeros_like(dw_ref)

        dw_ref[0:1, :] += jnp.sum(dyy * z2, axis=0, keepdims=True)
        dw_ref[1:2, :] += jnp.sum(dyy * z1, axis=0, keepdims=True)
        dw_ref[2:3, :] += jnp.sum(dyy * z, axis=0, keepdims=True)

    col = pl.BlockSpec((seq, CONV_COLS), lambda j, b: (b, j))
    w_spec = pl.BlockSpec((CONV_TAPS, CONV_COLS), lambda j, b: (0, j))
    act = jax.ShapeDtypeStruct((tokens, width), BF16)
    return pl.pallas_call(
        body, name=name, grid=(width // CONV_COLS, tokens // seq),
        in_specs=[col, col, col, col, w_spec], out_specs=[col, col, col, w_spec],
        out_shape=[act, act, act, jax.ShapeDtypeStruct((CONV_TAPS, width), F32)],
        compiler_params=_params(("parallel", "arbitrary")),
    )(dout, u, bg, cg, conv_w)


def out_fwd(x1, attn, conv, gt, w_out, ln_g, ln_b, seq, name, job=None):
    tokens, dm = x1.shape
    half = attn.shape[1]
    tm = min(MIX_TILE, seq)
    tiles_per_seq = seq // tm

    def body(x_ref, a_ref, c_ref, gt_ref, w_ref, lg_ref, lb_ref, xo_ref, r_ref, mi_ref, mix_ref):
        mixin = jnp.concatenate([a_ref[...], c_ref[...]], axis=1).astype(BF16)
        mi_ref[...] = mixin
        mix = _dot(mixin, w_ref[...])
        mix_ref[...] = mix.astype(BF16)
        r = DN_ALPHA * x_ref[...] + (1.0 + gt_ref[0]) * mix
        r_ref[...] = r
        xhat, _ = _ln_stats(r)
        xo_ref[...] = xhat * lg_ref[...] + lb_ref[...]

    tile = pl.BlockSpec((tm, dm), lambda i: (i, 0))
    htile = pl.BlockSpec((tm, half), lambda i: (i, 0))
    return _call(
        body, job, name=name, grid=(tokens // tm,),
        in_specs=[tile, htile, htile, _mod_spec(tiles_per_seq, dm), _const_spec(w_out.shape),
                  _const_spec((1, dm)), _const_spec((1, dm))],
        out_specs=[tile, tile, tile, tile],
        out_shape=[jax.ShapeDtypeStruct((tokens, dm), F32), jax.ShapeDtypeStruct((tokens, dm), F32),
                   jax.ShapeDtypeStruct((tokens, dm), BF16), jax.ShapeDtypeStruct((tokens, dm), BF16)],
        args=(x1, attn, conv, gt, w_out, ln_g, ln_b))


def out_bwd(dy, r, mix, gt, w_out, ln_g, seq, name, job=None):
    tokens, dm = r.shape
    half = dm // 2
    tm = min(MIX_TILE, seq)
    tiles_per_seq = seq // tm
    nseq = tokens // seq

    def body(dy_ref, r_ref, mix_ref, gt_ref, w_ref, lg_ref, dres_ref, da_ref, dc_ref, dmix_ref, dln_ref, dgt_ref):
        i = pl.program_id(0)
        dr, dgain, dbias = _ln_bwd(dy_ref[...], r_ref[...], lg_ref[...])

        @pl.when(i == 0)
        def _():
            dln_ref[...] = jnp.zeros_like(dln_ref)

        @pl.when(i % tiles_per_seq == 0)
        def _():
            dgt_ref[...] = jnp.zeros_like(dgt_ref)

        dln_ref[0:1, :] += dgain
        dln_ref[1:2, :] += dbias
        dgt_ref[0] += jnp.sum(dr * mix_ref[...].astype(F32), axis=0, keepdims=True)
        dres_ref[...] = DN_ALPHA * dr
        dmix = ((1.0 + gt_ref[0]) * dr).astype(BF16)
        dmix_ref[...] = dmix
        dmixin = _dot_nt(dmix, w_ref[...])
        da_ref[...] = dmixin[:, :half].astype(BF16)
        dc_ref[...] = dmixin[:, half:].astype(BF16)

    tile = pl.BlockSpec((tm, dm), lambda i: (i, 0))
    htile = pl.BlockSpec((tm, half), lambda i: (i, 0))
    return _call(
        body, job, name=name, grid=(tokens // tm,),
        in_specs=[tile, tile, tile, _mod_spec(tiles_per_seq, dm), _const_spec(w_out.shape), _const_spec((1, dm))],
        out_specs=[tile, htile, htile, tile, pl.BlockSpec((2, dm), lambda i: (0, 0)),
                   pl.BlockSpec((1, 1, dm), lambda i: (i // tiles_per_seq, 0, 0))],
        out_shape=[jax.ShapeDtypeStruct((tokens, dm), F32), jax.ShapeDtypeStruct((tokens, half), BF16),
                   jax.ShapeDtypeStruct((tokens, half), BF16), jax.ShapeDtypeStruct((tokens, dm), BF16),
                   jax.ShapeDtypeStruct((2, dm), F32), jax.ShapeDtypeStruct((nseq, 1, dm), F32)],
        args=(dy, r, mix, gt, w_out, ln_g))


def proj_bwd(parts, dres, x1, sh, sc, w_in, r_prev, f_prev, gt_prev, ln_g_prev, seq, name, job=None):
    tokens, dm = x1.shape
    tm = min(MIX_TILE, seq)
    tiles_per_seq = seq // tm
    nseq = tokens // seq
    widths = [p.shape[1] for p in parts]
    total = sum(widths)

    def body(*refs):
        part_refs = refs[:6]
        (dres_ref, x_ref, sh_ref, sc_ref, w_ref, r_ref, f_ref, gt_ref, lg_ref,
         dr_ref, df_ref, dproj_ref, h_ref, dmod_ref, dln_ref, dgt_ref) = refs[6:]
        i = pl.program_id(0)
        dproj = jnp.concatenate([p[...].astype(BF16) for p in part_refs], axis=1)
        dproj_ref[...] = dproj
        dh = _dot(dproj, w_ref[...])
        xx = x_ref[...]
        one_sc = 1.0 + sc_ref[0]
        h_ref[...] = (xx * one_sc + sh_ref[0]).astype(BF16)
        dr, dgain, dbias = _ln_bwd(dres_ref[...] + dh * one_sc, r_ref[...], lg_ref[...])
        dr_ref[...] = dr
        df_ref[...] = ((0.5 * (1.0 + gt_ref[0])) * dr).astype(BF16)

        @pl.when(i == 0)
        def _():
            dln_ref[...] = jnp.zeros_like(dln_ref)

        @pl.when(i % tiles_per_seq == 0)
        def _():
            dmod_ref[...] = jnp.zeros_like(dmod_ref)
            dgt_ref[...] = jnp.zeros_like(dgt_ref)

        dmod_ref[0, 0:1, :] += jnp.sum(dh, axis=0, keepdims=True)
        dmod_ref[0, 1:2, :] += jnp.sum(dh * xx, axis=0, keepdims=True)
        dln_ref[0:1, :] += dgain
        dln_ref[1:2, :] += dbias
        dgt_ref[0] += jnp.sum(dr * (0.5 * f_ref[...].astype(F32)), axis=0, keepdims=True)

    tile = pl.BlockSpec((tm, dm), lambda i: (i, 0))
    mod = _mod_spec(tiles_per_seq, dm)
    return _call(
        body, job, name=name, grid=(tokens // tm,),
        in_specs=[pl.BlockSpec((tm, wdt), lambda i: (i, 0)) for wdt in widths]
        + [tile, tile, mod, mod, _const_spec(w_in.shape), tile, tile, mod, _const_spec((1, dm))],
        out_specs=[tile, tile, pl.BlockSpec((tm, total), lambda i: (i, 0)), tile,
                   pl.BlockSpec((1, 2, dm), lambda i: (i // tiles_per_seq, 0, 0)),
                   pl.BlockSpec((2, dm), lambda i: (0, 0)), mod],
        out_shape=[jax.ShapeDtypeStruct((tokens, dm), F32), jax.ShapeDtypeStruct((tokens, dm), BF16),
                   jax.ShapeDtypeStruct((tokens, total), BF16), jax.ShapeDtypeStruct((tokens, dm), BF16),
                   jax.ShapeDtypeStruct((nseq, 2, dm), F32), jax.ShapeDtypeStruct((2, dm), F32),
                   jax.ShapeDtypeStruct((nseq, 1, dm), F32)],
        args=(*parts, dres, x1, sh, sc, w_in, r_prev, f_prev, gt_prev, ln_g_prev))


def _rope_tables(positions):
    half = ROT_DIM // 2
    inv_freq = np.power(np.float32(ROPE_THETA), -np.arange(0, ROT_DIM, 2, dtype=np.float32) / ROT_DIM)
    lane = np.arange(LANES) % HEAD_DIM
    freq = np.where(lane < ROT_DIM, inv_freq[lane % half], 0.0).astype(np.float32)
    sign = np.where(lane < half, -1.0, 1.0).astype(np.float32)
    ang = positions.astype(F32)[:, None] * freq[None, :]
    return jnp.cos(ang), sign[None, :] * jnp.sin(ang)


def kernel(x, c, positions, w_ada, b_ada, ffn1_w_gate_up, ffn1_w_down, ln1_g, ln1_b, w_in, conv_w, attn_sinks, w_out, ln2_g, ln2_b, ffn2_w_gate_up, ffn2_w_down, ln3_g, ln3_b, loss_target, m_w_ada, m_b_ada, m_ffn1_w_gate_up, m_ffn1_w_down, m_ln1_g, m_ln1_b, m_w_in, m_conv_w, m_attn_sinks, m_w_out, m_ln2_g, m_ln2_b, m_ffn2_w_gate_up, m_ffn2_w_down, m_ln3_g, m_ln3_b, v_w_ada, v_b_ada, v_ffn1_w_gate_up, v_ffn1_w_down, v_ln1_g, v_ln1_b, v_w_in, v_conv_w, v_attn_sinks, v_w_out, v_ln2_g, v_ln2_b, v_ffn2_w_gate_up, v_ffn2_w_down, v_ln3_g, v_ln3_b):
    nseq, seq, dm = x.shape
    tokens = nseq * seq
    dev = 4 * lax.axis_index("x") + 2 * lax.axis_index("y") + lax.axis_index("c")
    core = lax.axis_index("c").astype(jnp.int32).reshape(1)
    ada_cols = w_ada.shape[2]
    ff = ffn1_w_down.shape[1] * N_DEV
    fc = ff // 4
    in_cols = w_in.shape[2]
    conv_cols = conv_w.shape[2]

    def t_bf16(w):
        return w[0].T.astype(BF16)

    c_all, convw_all = all_gather([c, conv_w[0]], "gather_cond")
    c_all = c_all.reshape(N_DEV * nseq, dm)
    convw_full = convw_all.transpose(1, 0, 2).reshape(CONV_TAPS, N_DEV * conv_cols)

    b_cols = lax.dynamic_slice(b_ada, (0, dev * ada_cols), (1, ada_cols))
    cond_all, mod_cols = ada_fwd(c_all, w_ada[0], b_cols, "ada_fwd")
    wgu1, mod_all = all_gather([t_bf16(ffn1_w_gate_up), mod_cols], "gather_ffn1")
    wgu1 = wgu1.reshape(2, ff, dm)
    mod = lax.dynamic_slice(mod_all, (0, dev * nseq, 0), (N_DEV, nseq, ada_cols))
    mod = mod.transpose(1, 0, 2).reshape(nseq, 9, 1, dm)
    sh1, sc1, g1, sh2, sc2, g2, sh3, sc3, g3 = [mod[:, i] for i in range(9)]

    x0 = x.reshape(tokens, dm)
    (gu1, a1, h1), (wd1, wout) = ffn_up(x0, sh1, sc1, wgu1, seq, "ffn1_up",
                                        job=_GatherJob([ffn1_w_down[0].astype(BF16), w_out[0].astype(BF16)]))
    wd1, wout = wd1.reshape(ff, dm), wout.reshape(dm, dm)
    (x1, r1, f1), (win,) = ffn_down(x0, a1, g1, wd1, ln1_g, ln1_b, seq, "ffn1_down", job=_GatherJob([t_bf16(w_in)]))
    win = win.reshape(N_DEV * in_cols, dm)
    (q, k, v, u, bg, cg), wd2_spread = proj_fwd(x1, sh2, sc2, win, seq, "proj_fwd",
                                                job=gather_spread_job([ffn2_w_down[0].astype(BF16)]))
    cos_t, sin_t = _rope_tables(positions.reshape(tokens))
    sinks = attn_sinks[0]
    (attn, q_rot, probs, sink_probs), wgu2_spread = attn_fwd(q, k, v, cos_t, sin_t, sinks, seq, "attn_fwd",
                                                             job=gather_spread_job([t_bf16(ffn2_w_gate_up)]))
    conv = conv_fwd(u, bg, cg, convw_full, seq, "conv_fwd")
    (x2, r2, mixin, mix), (wd2, wgu2) = out_fwd(x1, attn, conv, g2, wout, ln2_g, ln2_b, seq, "out_fwd",
                                                job=gather_forward_job(wd2_spread + wgu2_spread))
    wd2, wgu2 = wd2.reshape(ff, dm), wgu2.reshape(2, ff, dm)
    target = loss_target.reshape(tokens, dm)
    dr3, df3, gu3, a3, h3, loss_part, dln3, dg3 = ffn_loss(x2, sh3, sc3, g3, wgu2, wd2, ln3_g, ln3_b, target, seq, "ffn2_fwd")

    (dx2, dgu3, dmod3), _ = ffn_bwd(dr3, df3, x2, gu3, sc3, wgu2, wd2, seq, "ffn2_bwd")
    pair = 2 * fc
    g_wd2 = tn_matmul(a3[None], df3[None], "ffn2_dwd", a_width=pair)[0][0].reshape(N_DEV, ff // N_DEV, dm)
    g_wgu2 = tn_matmul(dgu3, h3[None], "ffn2_dwgu", a_width=pair)[0][0].reshape(N_DEV, fc, dm)
    (dres2, dattn, dconv, dmix, dln2, dg2), swapped = out_bwd(dx2, r2, mix, g2, wout, ln2_g, seq, "out_bwd",
                                                              job=swap_job([g_wgu2, g_wd2]))
    p_wgu2, own_wgu2 = pair_sum(core, g_wgu2, swapped[0], "pair_wgu2")
    p_wd2, own_wd2 = pair_sum(core, g_wd2, swapped[1], "pair_wd2")
    du, dbg, dcg, dconvw = conv_bwd(dconv, u, bg, cg, convw_full, seq, "conv_bwd")
    (dq, dk, dv, dsink_rows), (far_wd2,) = attn_bwd(
        q_rot, k, v, dattn, probs, sink_probs, cos_t, sin_t, seq, "attn_bwd", job=chip_exchange_job([p_wd2]))
    parts = [dq, dk, dv, du, dbg, dcg]
    (dr1, df1, dproj, h2, dmod2, dln1, dg1), far_top = proj_bwd(
        parts, dres2, x1, sh2, sc2, win, r1, f1, g1, ln1_g, seq, "proj_bwd",
        job=chip_exchange_job([p_wgu2], rows=(0, fc // 2)))
    (dx0, dgu1, dmod1), _ = ffn_bwd(dr1, df1, x0, gu1, sc1, wgu1, wd1, seq, "ffn1_bwd")

    dmod = jnp.concatenate([dmod1, dg1, dmod2, dg2, dmod3, dg3], axis=1).reshape(nseq, 9 * dm)
    half = dm // 2
    jobs = _Jobs([gather_spread_job([dmod]),
                  chip_exchange_job([p_wgu2], rows=(fc // 2, fc // 2), into=far_top)])
    (g_wd1,), res = tn_matmul(a1[None], df1[None], "ffn1_dwd", job=jobs, a_width=pair)
    dmod_spread, (far_wgu2,) = jobs.split(res)
    g_wd1 = g_wd1.reshape(N_DEV, ff // N_DEV, dm)
    jobs = _Jobs([swap_job([g_wd1]), gather_forward_job(dmod_spread)])
    (g_l,), res = tn_matmul(dgu1, h1[None], "ffn1_dwgu_l", job=jobs, b_cols=(0, half), a_width=pair)
    (sw_wd1,), (dmod_all,) = jobs.split(res)
    g_l = g_l.reshape(N_DEV, fc, half)
    p_wd1, own_wd1 = pair_sum(core, g_wd1, sw_wd1, "pair_wd1")
    jobs = _Jobs([chip_exchange_job([p_wd1]), swap_job([g_l])])
    (g_r,), res = tn_matmul(dgu1, h1[None], "ffn1_dwgu_r", job=jobs, b_cols=(1, half), a_width=pair)
    (far_wd1,), (sw_l,) = jobs.split(res)
    g_r = g_r.reshape(N_DEV, fc, half)
    p_l, own_l = pair_sum(core, g_l, sw_l, "pair_wgu1_l")

    dmod_cols = lax.dynamic_slice(dmod_all.reshape(N_DEV * nseq, 9 * dm), (0, dev * ada_cols), (N_DEV * nseq, ada_cols))
    grad_w_ada, gb_cols = ada_bwd(cond_all, dmod_cols, "ada_bwd")
    dsinks = jnp.sum(dsink_rows.reshape(nseq, N_Q_HEADS, ATTN_BLOCK), axis=(0, 2))
    small = jnp.zeros((8, dm), F32)
    small = small.at[0:2].set(dln1).at[2:4].set(dln2).at[4:6].set(dln3)
    small = small.at[6, 0:N_Q_HEADS].set(dsinks).at[7, 0].set(loss_part[0, 0])

    jobs = _Jobs([chip_exchange_job([p_l]), swap_job([g_r]), gather_spread_job([small, dconvw, gb_cols])])
    (g_win,), res = tn_matmul(dproj[None], h2[None], "dwin", job=jobs)
    (far_l,), (sw_r,), small_spread = jobs.split(res)
    g_win = g_win.reshape(N_DEV, in_cols, dm)
    p_r, own_r = pair_sum(core, g_r, sw_r, "pair_wgu1_r")
    jobs = _Jobs([chip_exchange_job([p_r]), swap_job([g_win]), gather_forward_job(small_spread)])
    (g_wout,), res = tn_matmul(mixin[None], dmix[None], "dwout", job=jobs)
    (far_r,), (sw_win,), (small_all, dconvw_all, gb_all) = jobs.split(res)
    g_wout = g_wout.reshape(N_DEV, dm // N_DEV, dm)
    p_win, own_win = pair_sum(core, g_win, sw_win, "pair_win")

    given = dict(w_ada=(w_ada, m_w_ada, v_w_ada), b_ada=(b_ada, m_b_ada, v_b_ada),
                 ffn1_w_gate_up=(ffn1_w_gate_up, m_ffn1_w_gate_up, v_ffn1_w_gate_up),
                 ffn1_w_down=(ffn1_w_down, m_ffn1_w_down, v_ffn1_w_down),
                 ln1_g=(ln1_g, m_ln1_g, v_ln1_g), ln1_b=(ln1_b, m_ln1_b, v_ln1_b),
                 w_in=(w_in, m_w_in, v_w_in), conv_w=(conv_w, m_conv_w, v_conv_w),
                 attn_sinks=(attn_sinks, m_attn_sinks, v_attn_sinks), w_out=(w_out, m_w_out, v_w_out),
                 ln2_g=(ln2_g, m_ln2_g, v_ln2_g), ln2_b=(ln2_b, m_ln2_b, v_ln2_b),
                 ffn2_w_gate_up=(ffn2_w_gate_up, m_ffn2_w_gate_up, v_ffn2_w_gate_up),
                 ffn2_w_down=(ffn2_w_down, m_ffn2_w_down, v_ffn2_w_down),
                 ln3_g=(ln3_g, m_ln3_g, v_ln3_g), ln3_b=(ln3_b, m_ln3_b, v_ln3_b))
    transposed = ("ffn1_w_gate_up", "ffn2_w_gate_up", "w_in")

    def big_adamw(nm, grad, far=None):
        flip = nm in transposed
        w2, m2, v2 = [t[0].T if flip else t[0] for t in given[nm]]
        return [t.T[None] if flip else t[None] for t in adamw(w2, grad, m2, v2, "adamw_" + nm, others=far)]

    jobs = _Jobs([chip_exchange_job([p_win]), swap_job([g_wout])])
    (far_win,), (sw_wout,) = jobs.split(run_job(jobs, "rs_tail_win"))
    p_wout, own_wout = pair_sum(core, g_wout, sw_wout, "pair_wout")
    (far_wout,) = run_job(chip_exchange_job([p_wout]), "rs_tail_wout")

    grads = {
        "ffn1_w_gate_up": [own_l, own_r], "ffn1_w_down": own_wd1,
        "w_in": own_win, "w_out": own_wout, "ffn2_w_gate_up": own_wgu2, "ffn2_w_down": own_wd2,
    }
    others = {"ffn1_w_gate_up": [far_l, far_r], "ffn1_w_down": far_wd1,
              "w_in": far_win, "w_out": far_wout, "ffn2_w_gate_up": far_wgu2, "ffn2_w_down": far_wd2}
    results = {"w_ada": big_adamw("w_ada", grad_w_ada)}
    for nm in grads:
        results[nm] = big_adamw(nm, grads[nm], others[nm])

    small_sum = sum_devices(small_all, "sum_small")
    dconvw_sum = sum_devices(dconvw_all, "sum_convw")
    loss = small_sum[7, 0]
    grads["b_ada"] = gb_all.reshape(1, N_DEV * ada_cols)
    grads["conv_w"] = lax.dynamic_slice(dconvw_sum, (0, dev * conv_cols), (CONV_TAPS, conv_cols))
    grads["attn_sinks"] = small_sum[6:7, 0:N_Q_HEADS]
    for i, nm in enumerate(["ln1_g", "ln1_b", "ln2_g", "ln2_b", "ln3_g", "ln3_b"]):
        grads[nm] = small_sum[i:i + 1]

    order = ["w_ada", "b_ada", "ffn1_w_gate_up", "ffn1_w_down", "ln1_g", "ln1_b", "w_in", "conv_w", "attn_sinks",
             "w_out", "ln2_g", "ln2_b", "ffn2_w_gate_up", "ffn2_w_down", "ln3_g", "ln3_b"]
    small_names = [nm for nm in order if nm not in results]
    items = []
    for nm in small_names:
        shape = given[nm][0].shape
        two_d = (shape[-2], shape[-1])
        items.append((given[nm][0].reshape(two_d), grads[nm].reshape(two_d), *[t.reshape(two_d) for t in given[nm][1:]]))
    for nm, res in zip(small_names, adamw_small(items, "adamw_small")):
        shape = given[nm][0].shape
        results[nm] = [grads[nm].reshape(shape)] + [t.reshape(shape) for t in res]
    grad_x = dx0.reshape(nseq, seq, dm)
    return (loss, grad_x, *[results[nm][i] for i in range(4) for nm in order])
```

```python
import functools

import jax
import jax.numpy as jnp
import numpy as np
from jax import lax
from jax.experimental import pallas as pl
from jax.experimental.pallas import tpu as pltpu

F32 = jnp.float32
BF16 = jnp.bfloat16
MESH = pl.DeviceIdType.MESH

N_DEV = 8
N_CHIP = 4
HEAD_DIM = 64
N_Q_HEADS = 8
N_KV_HEADS = 2
GQA_GROUP = N_Q_HEADS // N_KV_HEADS
ATTN_BLOCK = 128
ROT_DIM = 16
ROPE_THETA = 500000.0
CONV_TAPS = 3
LN_EPS = 1e-5
DN_ALPHA = 2.0 ** 0.25
ADAM_LR = 0.001
ADAM_B1 = 0.9
ADAM_B2 = 0.999
ADAM_EPS = 1e-08
ADAM_WD = 0.01
ADAM_STEP = 10
NEG_BIG = -1e30

VMEM_LIMIT = 56 * 1024 * 1024
FFN_FWD_TILE = 512
MIX_TILE = 512
FFN_WIDE_TILE = 512
FFN_WIDE_VMEM = 62 * 1024 * 1024
TN_VMEM_BUDGET = 36 * 1024 * 1024


def _params(semantics=None, vmem=VMEM_LIMIT):
    return pltpu.CompilerParams(dimension_semantics=semantics, vmem_limit_bytes=vmem)


def _dot(a, b):
    return jnp.dot(a, b, preferred_element_type=F32)


def _dot_nt(a, b):
    return lax.dot_general(a, b, (((1,), (1,)), ((), ())), preferred_element_type=F32)


def _dot_tn(a, b):
    return lax.dot_general(a, b, (((0,), (0,)), ((), ())), preferred_element_type=F32)


def _sigmoid(x):
    return pl.reciprocal(1.0 + jnp.exp(-x), approx=True)


def _ln_stats(r):
    mu = jnp.mean(r, axis=-1, keepdims=True)
    d = r - mu
    var = jnp.mean(d * d, axis=-1, keepdims=True)
    rstd = lax.rsqrt(var + LN_EPS)
    return d * rstd, rstd


def _ln_bwd(dy, r, g):
    return _ln_bwd_normalized(dy, *_ln_stats(r), g)


def _ln_bwd_normalized(dy, xhat, rstd, g):
    dxhat = dy * g
    c1 = jnp.mean(dxhat, axis=-1, keepdims=True)
    c2 = jnp.mean(dxhat * xhat, axis=-1, keepdims=True)
    dr = rstd * (dxhat - c1 - xhat * c2)
    return dr, jnp.sum(dy * xhat, axis=0, keepdims=True), jnp.sum(dy, axis=0, keepdims=True)


def _const_spec(shape):
    nd = len(shape)
    return pl.BlockSpec(shape, lambda *_: (0,) * nd, pipeline_mode=pl.Buffered(1))


def all_gather(arrs, name):
    n = len(arrs)

    def body(*refs):
        ins, outs = refs[:n], refs[n:2 * n]
        send_sems, recv_sems, local_sems = refs[2 * n:]
        x, y, c = lax.axis_index("x"), lax.axis_index("y"), lax.axis_index("c")
        me, sibling = (x, y, c), (x, y, 1 - c)
        chips = [(1 - x, y), (x, 1 - y), (1 - x, 1 - y)]

        def slot(i, p):
            return outs[i].at[4 * p[0] + 2 * p[1] + p[2]]

        def copy(i, k, block, to, src=None):
            return pltpu.make_async_remote_copy(
                src_ref=slot(i, block) if src is None else src, dst_ref=slot(i, block),
                send_sem=send_sems.at[i, k], recv_sem=recv_sems.at[i, k],
                device_id=to, device_id_type=MESH)

        mine = [pltpu.make_async_copy(ins[i], slot(i, me), local_sems.at[i]) for i in range(n)]
        for cp in mine:
            cp.start()
        first = []
        for i in range(n):
            first.append(copy(i, 0, me, sibling, src=ins[i]))
            first += [copy(i, 1 + j, me, (*chip, c), src=ins[i]) for j, chip in enumerate(chips)]
        for cp in first:
            cp.start()
        passed = []
        for i in range(n):
            for j, chip in enumerate(chips):
                copy(i, 1 + j, (*chip, c), me).wait_recv()
                cp = copy(i, 4 + j, (*chip, c), sibling)
                cp.start()
                passed.append(cp)
        for i in range(n):
            copy(i, 0, sibling, me).wait_recv()
            for j, chip in enumerate(chips):
                copy(i, 4 + j, (*chip, 1 - c), me).wait_recv()
        for cp in first + passed:
            cp.wait_send()
        for cp in mine:
            cp.wait()

    any_spec = pl.BlockSpec(memory_space=pl.ANY)
    return pl.pallas_call(
        body, name=name,
        out_shape=[jax.ShapeDtypeStruct((N_DEV, *a.shape), a.dtype) for a in arrs],
        in_specs=[any_spec] * n, out_specs=[any_spec] * n,
        scratch_shapes=[pltpu.SemaphoreType.DMA((n, 7)), pltpu.SemaphoreType.DMA((n, 7)),
                        pltpu.SemaphoreType.DMA((n,))],
    )(*arrs)


def _place():
    x, y, c = lax.axis_index("x"), lax.axis_index("y"), lax.axis_index("c")
    return x, y, c, [(1 - x, y), (x, 1 - y), (1 - x, 1 - y)]


def _slot(p):
    return 4 * p[0] + 2 * p[1] + p[2]


class _Job:
    def __init__(self, ins, outs, nsem, copies, aliases=None, local=None):
        self.ins, self.outs, self.nsem, self.copies = list(ins), list(outs), nsem, copies
        self.aliases = aliases or {}
        self.local = local

    def scratch(self):
        s = [pltpu.SemaphoreType.DMA(self.nsem), pltpu.SemaphoreType.DMA(self.nsem)]
        if self.local is not None:
            s.append(pltpu.SemaphoreType.DMA((len(self.ins),)))
        return s

    def start(self, ins, outs, sems):
        if self.local is not None:
            for cp in self.local(ins, outs, sems[2]):
                cp.start()
        for cp in self.copies(ins, outs, sems[0], sems[1])[0]:
            cp.start()

    def finish(self, ins, outs, sems):
        started, awaited = self.copies(ins, outs, sems[0], sems[1])
        for cp in awaited:
            cp.wait_recv()
        for cp in started:
            cp.wait_send()
        if self.local is not None:
            for cp in self.local(ins, outs, sems[2]):
                cp.wait()


class _Jobs:
    def __init__(self, jobs):
        self.jobs = jobs
        self.ins = [a for j in jobs for a in j.ins]
        self.outs = [o for j in jobs for o in j.outs]
        self.two_phase = any(getattr(j, "two_phase", False) for j in jobs)
        self.aliases = {}
        at_in = at_out = 0
        for j in jobs:
            self.aliases.update({at_in + i: at_out + o for i, o in j.aliases.items()})
            at_in, at_out = at_in + len(j.ins), at_out + len(j.outs)

    def scratch(self):
        return [s for j in self.jobs for s in j.scratch()]

    def _each(self, ins, outs, sems):
        at_in = at_out = at_sem = 0
        for j in self.jobs:
            n_in, n_out, n_sem = len(j.ins), len(j.outs), len(j.scratch())
            yield j, ins[at_in:at_in + n_in], outs[at_out:at_out + n_out], sems[at_sem:at_sem + n_sem]
            at_in, at_out, at_sem = at_in + n_in, at_out + n_out, at_sem + n_sem

    def start(self, ins, outs, sems):
        for j, i, o, s in self._each(ins, outs, sems):
            j.start(i, o, s)

    def turn(self, ins, outs, sems):
        for j, i, o, s in self._each(ins, outs, sems):
            if getattr(j, "two_phase", False):
                j.turn(i, o, s)

    def finish(self, ins, outs, sems):
        for j, i, o, s in self._each(ins, outs, sems):
            j.finish(i, o, s)

    def split(self, results):
        at, parts = 0, []
        for j in self.jobs:
            parts.append(results[at:at + len(j.outs)])
            at += len(j.outs)
        return parts


def _remote(src, dst, send, recv, idx, to):
    return pltpu.make_async_remote_copy(src_ref=src, dst_ref=dst, send_sem=send.at[idx], recv_sem=recv.at[idx],
                                        device_id=to, device_id_type=MESH)


def _spread_copies(ins, outs, send, recv, base=0):
    x, y, c, chips = _place()
    me = (x, y, c)
    peers = [(x, y, 1 - c)] + [(*chip, c) for chip in chips]
    started, awaited = [], []
    for i, (src, dst) in enumerate(zip(ins, outs)):
        for k, peer in enumerate(peers):
            started.append(_remote(src, dst.at[_slot(me)], send, recv, (base + i, k), peer))
            awaited.append(_remote(src, dst.at[_slot(peer)], send, recv, (base + i, k), peer))
    return started, awaited


def _forward_copies(ins, outs, send, recv, base=0):
    x, y, c, chips = _place()
    started, awaited = [], []
    for i, buf in enumerate(outs):
        for j, chip in enumerate(chips):
            mine, theirs = buf.at[_slot((*chip, c))], buf.at[_slot((*chip, 1 - c))]
            started.append(_remote(mine, mine, send, recv, (base + i, j), (x, y, 1 - c)))
            awaited.append(_remote(theirs, theirs, send, recv, (base + i, j), (x, y, 1 - c)))
    return started, awaited


def _own_block_copies(ins, outs, sems):
    x, y, c, _ = _place()
    return [pltpu.make_async_copy(src, dst.at[_slot((x, y, c))], sems.at[i])
            for i, (src, dst) in enumerate(zip(ins, outs))]


def gather_spread_job(shards):
    outs = [jax.ShapeDtypeStruct((N_DEV, *a.shape), a.dtype) for a in shards]
    return _Job(shards, outs, (len(shards), 4), _spread_copies, local=_own_block_copies)


def gather_forward_job(fulls):
    outs = [jax.ShapeDtypeStruct(a.shape, a.dtype) for a in fulls]
    return _Job(fulls, outs, (len(fulls), 3), _forward_copies, aliases={i: i for i in range(len(fulls))})


TURN_EIGHTHS = 6


class _GatherJob:
    two_phase = True

    def __init__(self, shards):
        self.ins = list(shards)
        self.outs = [jax.ShapeDtypeStruct((N_DEV, *a.shape), a.dtype) for a in shards]
        self.aliases = {}

    def scratch(self):
        n = len(self.ins)
        return [pltpu.SemaphoreType.DMA((n, 4)), pltpu.SemaphoreType.DMA((n, 4)),
                pltpu.SemaphoreType.DMA((n, 3)), pltpu.SemaphoreType.DMA((n, 3)), pltpu.SemaphoreType.DMA((n,))]

    def start(self, ins, outs, sems):
        for cp in _own_block_copies(ins, outs, sems[4]) + _spread_copies(ins, outs, sems[0], sems[1])[0]:
            cp.start()

    def turn(self, ins, outs, sems):
        for cp in _spread_copies(ins, outs, sems[0], sems[1])[1]:
            cp.wait_recv()
        for cp in _forward_copies(outs, outs, sems[2], sems[3])[0]:
            cp.start()

    def finish(self, ins, outs, sems):
        handed_on, arriving = _forward_copies(outs, outs, sems[2], sems[3])
        for cp in arriving:
            cp.wait_recv()
        for cp in _spread_copies(ins, outs, sems[0], sems[1])[0] + handed_on:
            cp.wait_send()
        for cp in _own_block_copies(ins, outs, sems[4]):
            cp.wait()


def swap_job(gs):
    def copies(ins, outs, send, recv):
        x, y, c, _ = _place()
        started, awaited = [], []
        for i, (g, r1) in enumerate(zip(ins, outs)):
            for q in range(N_CHIP):
                started.append(_remote(g.at[2 * q + (1 - c)], r1.at[q], send, recv, (i, q), (x, y, 1 - c)))
                awaited.append(_remote(g.at[2 * q + c], r1.at[q], send, recv, (i, q), (x, y, 1 - c)))
        return started, awaited

    outs = [jax.ShapeDtypeStruct((N_CHIP, *g.shape[1:]), g.dtype) for g in gs]
    return _Job(gs, outs, (len(gs), N_CHIP), copies)


def chip_exchange_job(ps, rows=None, into=None):
    n = len(ps)

    def copies(ins, outs, send, recv):
        x, y, c, chips = _place()
        started, awaited = [], []
        for i, (p, r2) in enumerate(zip(ins[:n], outs)):
            for k, chip in enumerate(chips):
                src, mine, dst = p.at[2 * chip[0] + chip[1]], p.at[2 * x + y], r2.at[k]
                if rows is not None:
                    src, mine, dst = (t.at[pl.ds(rows[0], rows[1])] for t in (src, mine, dst))
                started.append(_remote(src, dst, send, recv, (i, k), (*chip, c)))
                awaited.append(_remote(mine, dst, send, recv, (i, k), (*chip, c)))
        return started, awaited

    outs = [jax.ShapeDtypeStruct((3, *p.shape[1:]), p.dtype) for p in ps]
    if into is None:
        return _Job(ps, outs, (n, 3), copies)
    return _Job(list(ps) + list(into), outs, (n, 3), copies, aliases={n + i: i for i in range(n)})


def _call(body, job, *, name, grid, in_specs, out_specs, out_shape, args, scratch_shapes=(), vmem=VMEM_LIMIT):
    if job is None:
        res = pl.pallas_call(
            body, name=name, grid=grid, in_specs=in_specs, out_specs=out_specs, out_shape=out_shape,
            scratch_shapes=list(scratch_shapes), compiler_params=_params(("arbitrary",) * len(grid), vmem),
        )(*args)
        return res, []
    n_in, n_out, n_scr = len(in_specs), len(out_specs), len(scratch_shapes)
    j_in, j_out = len(job.ins), len(job.outs)

    def with_copies(*refs):
        at = 0
        ins = refs[at:at + n_in]; at += n_in
        jins = refs[at:at + j_in]; at += j_in
        outs = refs[at:at + n_out]; at += n_out
        jouts = refs[at:at + j_out]; at += j_out
        scr = refs[at:at + n_scr]; at += n_scr
        sems = refs[at:]
        ids = [pl.program_id(d) for d in range(len(grid))]
        first = functools.reduce(jnp.logical_and, [i == 0 for i in ids])
        last = functools.reduce(jnp.logical_and, [i == n - 1 for i, n in zip(ids, grid)])

        @pl.when(first)
        def _():
            job.start(jins, jouts, sems)

        if getattr(job, "two_phase", False):
            steps, at = 1, 0
            for i, n in zip(ids, grid):
                steps, at = steps * n, at * n + i

            @pl.when(at == (TURN_EIGHTHS * steps) // 8)
            def _():
                job.turn(jins, jouts, sems)

        body(*ins, *outs, *scr)

        @pl.when(last)
        def _():
            job.finish(jins, jouts, sems)

    any_spec = pl.BlockSpec(memory_space=pl.ANY)
    res = pl.pallas_call(
        with_copies, name=name, grid=grid,
        in_specs=list(in_specs) + [any_spec] * j_in, out_specs=list(out_specs) + [any_spec] * j_out,
        out_shape=list(out_shape) + list(job.outs),
        input_output_aliases={n_in + i: n_out + o for i, o in job.aliases.items()},
        scratch_shapes=list(scratch_shapes) + job.scratch(),
        compiler_params=_params(("arbitrary",) * len(grid), vmem),
    )(*args, *job.ins)
    return res[:n_out], res[n_out:]


def run_job(job, name):
    def body(*refs):
        j_in, j_out = len(job.ins), len(job.outs)
        ins, outs, sems = refs[:j_in], refs[j_in:j_in + j_out], refs[j_in + j_out:]
        job.start(ins, outs, sems)
        job.finish(ins, outs, sems)

    any_spec = pl.BlockSpec(memory_space=pl.ANY)
    return pl.pallas_call(
        body, name=name, in_specs=[any_spec] * len(job.ins), out_specs=[any_spec] * len(job.outs),
        out_shape=list(job.outs), input_output_aliases=dict(job.aliases), scratch_shapes=job.scratch(),
    )(*job.ins)


def pair_sum(core, g, r1, name):
    _, rows, cols = g.shape
    rb = next(cand for cand in range(min(rows, 512), 0, -16) if rows % cand == 0)

    def body(core_ref, g_ref, r1_ref, p_ref, own_ref):
        del core_ref
        x, y, _, _ = _place()
        s = g_ref[0].astype(F32) + r1_ref[0].astype(F32)
        p_ref[0] = s.astype(BF16)

        @pl.when(pl.program_id(1) == 2 * x + y)
        def _():
            own_ref[...] = s

    chunk = (1, rb, cols)
    return pl.pallas_call(
        body, name=name,
        grid_spec=pltpu.PrefetchScalarGridSpec(
            num_scalar_prefetch=1, grid=(rows // rb, N_CHIP),
            in_specs=[pl.BlockSpec(chunk, lambda i, q, core_ref: (2 * q + core_ref[0], i, 0)),
                      pl.BlockSpec(chunk, lambda i, q, core_ref: (q, i, 0))],
            out_specs=[pl.BlockSpec(chunk, lambda i, q, core_ref: (q, i, 0)),
                       pl.BlockSpec((rb, cols), lambda i, q, core_ref: (i, 0))]),
        out_shape=[jax.ShapeDtypeStruct((N_CHIP, rows, cols), BF16), jax.ShapeDtypeStruct((rows, cols), F32)],
        compiler_params=_params(("arbitrary", "arbitrary")),
    )(core, g, r1)


def sum_devices(a, name, row_widths=None):
    def body(a_ref, *o_refs):
        acc = a_ref[0]
        for d in range(1, N_DEV):
            acc = acc + a_ref[d]
        if row_widths is None:
            o_refs[0][...] = acc
            return
        for r, (o_ref, width) in enumerate(zip(o_refs, row_widths)):
            o_ref[...] = acc[r:r + 1, 0:width]

    if row_widths is None:
        return pl.pallas_call(body, name=name, out_shape=jax.ShapeDtypeStruct(a.shape[1:], F32))(a)
    assert len(row_widths) == a.shape[1]
    return pl.pallas_call(body, name=name, out_shape=[jax.ShapeDtypeStruct((1, width), F32) for width in row_widths])(a)


def _adam_update(w, g, m, v):
    nm = ADAM_B1 * m + (1.0 - ADAM_B1) * g
    nv = ADAM_B2 * v + (1.0 - ADAM_B2) * (g * g)
    m_hat = nm / (1.0 - ADAM_B1 ** ADAM_STEP)
    v_hat = nv / (1.0 - ADAM_B2 ** ADAM_STEP)
    return -ADAM_LR * (m_hat / (jnp.sqrt(v_hat) + ADAM_EPS) + ADAM_WD * w), nm, nv


def adamw(w, g, m, v, name, others=None):
    rows, cols = w.shape
    g_parts = list(g) if isinstance(g, (list, tuple)) else [g]
    other_parts = [] if others is None else list(others) if isinstance(others, (list, tuple)) else [others]
    n_parts = len(g_parts)
    assert others is not None or n_parts == 1
    assert others is None or len(other_parts) == n_parts
    assert sum(p.shape[1] for p in g_parts) == cols
    rb = rows
    for cand in range(min(rows, 512), 7, -8):
        if rows % cand == 0 and cand % 8 == 0:
            rb = cand
            break

    def body(*refs):
        w_ref, m_ref, v_ref = refs[:3]
        g_refs = refs[3:3 + n_parts]
        if others is None:
            d_ref, nm_ref, nv_ref = refs[3 + n_parts:]
            gg = g_refs[0][...]
        else:
            r2_refs = refs[3 + n_parts:3 + 2 * n_parts]
            go_ref, d_ref, nm_ref, nv_ref = refs[3 + 2 * n_parts:]
            blocks = []
            for g_ref, r2_ref in zip(g_refs, r2_refs):
                part = g_ref[...]
                for k in range(3):
                    part = part + r2_ref[k].astype(F32)
                blocks.append(part)
            gg = blocks[0] if n_parts == 1 else jnp.concatenate(blocks, axis=1)
            go_ref[...] = gg
        d_ref[...], nm_ref[...], nv_ref[...] = _adam_update(w_ref[...], gg, m_ref[...], v_ref[...])

    spec = pl.BlockSpec((rb, cols), lambda i: (i, 0))
    out = jax.ShapeDtypeStruct((rows, cols), F32)
    in_specs, args = [spec] * 3, [w, m, v]
    in_specs += [pl.BlockSpec((rb, p.shape[1]), lambda i: (i, 0)) for p in g_parts]
    args += g_parts
    in_specs += [pl.BlockSpec((3, rb, p.shape[2]), lambda i: (0, i, 0)) for p in other_parts]
    args += other_parts
    n_out = 3 if others is None else 4
    res = pl.pallas_call(
        body, name=name, grid=(rows // rb,), in_specs=in_specs, out_specs=[spec] * n_out,
        out_shape=[out] * n_out, compiler_params=_params(("parallel",)),
    )(*args)
    return (g, *res) if others is None else tuple(res)


def adamw_small(items, name):
    n = len(items)

    def body(*refs):
        ins, outs = refs[:4 * n], refs[4 * n:]
        for i in range(n):
            w_ref, g_ref, m_ref, v_ref = ins[4 * i:4 * i + 4]
            d_ref, nm_ref, nv_ref = outs[3 * i:3 * i + 3]
            d_ref[...], nm_ref[...], nv_ref[...] = _adam_update(w_ref[...], g_ref[...], m_ref[...], v_ref[...])

    res = pl.pallas_call(
        body, name=name,
        out_shape=[jax.ShapeDtypeStruct(w.shape, F32) for w, _, _, _ in items for _ in range(3)],
    )(*[t for item in items for t in item])
    return [tuple(res[3 * i:3 * i + 3]) for i in range(n)]


def ada_fwd(c_all, w_cols, b_cols, name):
    def body(c_ref, w_ref, b_ref, cond_ref, mod_ref):
        cc = c_ref[...]
        cond = (cc * _sigmoid(cc)).astype(BF16)
        cond_ref[...] = cond
        mod_ref[...] = _dot(cond, w_ref[...].astype(BF16)) + b_ref[...]

    n, cols = c_all.shape[0], w_cols.shape[1]
    return pl.pallas_call(
        body, name=name,
        out_shape=[jax.ShapeDtypeStruct(c_all.shape, BF16), jax.ShapeDtypeStruct((n, cols), F32)],
        compiler_params=_params(),
    )(c_all, w_cols, b_cols)


def ada_bwd(cond_all, dmod_cols, name):
    def body(c_ref, d_ref, gw_ref, gb_ref):
        d = d_ref[...]
        gw_ref[...] = _dot_tn(c_ref[...], d.astype(BF16))
        gb_ref[...] = jnp.sum(d, axis=0, keepdims=True)

    dm, cols = cond_all.shape[1], dmod_cols.shape[1]
    return pl.pallas_call(
        body, name=name,
        out_shape=[jax.ShapeDtypeStruct((dm, cols), F32), jax.ShapeDtypeStruct((1, cols), F32)],
        compiler_params=_params(),
    )(cond_all, dmod_cols)


MXU_COLS = 256
FFN_CHUNK = 4 * MXU_COLS


def _hidden_chunks(ff):
    assert ff % MXU_COLS == 0
    return [(at, min(FFN_CHUNK, ff - at)) for at in range(0, ff, FFN_CHUNK)]


def _mod_spec(tiles_per_seq, dm):
    return pl.BlockSpec((1, 1, dm), lambda i: (i // tiles_per_seq, 0, 0))


def ffn_loss(x, sh, sc, gt, wgu, wd, ln_g, ln_b, target, seq, name):
    tokens, dm = x.shape
    ff = wgu.shape[1]
    chunks = _hidden_chunks(ff)
    tm = min(FFN_WIDE_TILE, seq)
    tiles_per_seq = seq // tm

    def body(x_ref, sh_ref, sc_ref, gt_ref, wgu_ref, wd_ref, lg_ref, lb_ref, t_ref,
             dr_ref, df_ref, gu_ref, a_ref, h_ref, loss_ref, dln_ref, dgt_ref):
        i = pl.program_id(0)
        xx = x_ref[...]
        h = (xx * (1.0 + sc_ref[0]) + sh_ref[0]).astype(BF16)
        h_ref[...] = h
        acc = jnp.zeros((tm, dm), F32)
        for at, wdt in chunks:
            gk = _dot_nt(h, wgu_ref[0, at:at + wdt, :])
            uk = _dot_nt(h, wgu_ref[1, at:at + wdt, :])
            gu_ref[0, :, at:at + wdt] = gk.astype(BF16)
            gu_ref[1, :, at:at + wdt] = uk.astype(BF16)
            a = (gk * _sigmoid(gk) * uk).astype(BF16)
            a_ref[:, at:at + wdt] = a
            acc = acc + _dot(a, wd_ref[at:at + wdt, :])
        half_gate = 0.5 * (1.0 + gt_ref[0])
        xhat, rstd = _ln_stats(DN_ALPHA * xx + half_gate * acc)
        err = xhat * lg_ref[...] + lb_ref[...] - t_ref[...]
        dr, dgain, dbias = _ln_bwd_normalized(err * (1.0 / dm), xhat, rstd, lg_ref[...])
        dr_ref[...] = dr
        df_ref[...] = (half_gate * dr).astype(BF16)

        @pl.when(i == 0)
        def _():
            loss_ref[...] = jnp.zeros_like(loss_ref)
            dln_ref[...] = jnp.zeros_like(dln_ref)

        @pl.when(i % tiles_per_seq == 0)
        def _():
            dgt_ref[...] = jnp.zeros_like(dgt_ref)

        loss_ref[...] += jnp.full((1, 128), (0.5 / dm) * jnp.sum(err * err), F32)
        dln_ref[0:1, :] += dgain
        dln_ref[1:2, :] += dbias
        dgt_ref[0] += jnp.sum(dr * (0.5 * acc), axis=0, keepdims=True)

    tile = pl.BlockSpec((tm, dm), lambda i: (i, 0))
    mod = _mod_spec(tiles_per_seq, dm)
    res, _ = _call(
        body, None, name=name, grid=(tokens // tm,),
        in_specs=[tile, mod, mod, mod, _const_spec(wgu.shape), _const_spec(wd.shape),
                  _const_spec((1, dm)), _const_spec((1, dm)), tile],
        out_specs=[tile, tile, pl.BlockSpec((2, tm, ff), lambda i: (0, i, 0)), pl.BlockSpec((tm, ff), lambda i: (i, 0)),
                   tile, pl.BlockSpec((1, 128), lambda i: (0, 0)), pl.BlockSpec((2, dm), lambda i: (0, 0)), mod],
        out_shape=[jax.ShapeDtypeStruct((tokens, dm), F32), jax.ShapeDtypeStruct((tokens, dm), BF16),
                   jax.ShapeDtypeStruct((2, tokens, ff), BF16), jax.ShapeDtypeStruct((tokens, ff), BF16),
                   jax.ShapeDtypeStruct((tokens, dm), BF16), jax.ShapeDtypeStruct((1, 128), F32),
                   jax.ShapeDtypeStruct((2, dm), F32), jax.ShapeDtypeStruct((tokens // seq, 1, dm), F32)],
        args=(x, sh, sc, gt, wgu, wd, ln_g, ln_b, target), vmem=FFN_WIDE_VMEM)
    return res


def ffn_up(x, sh, sc, wgu, seq, name, job=None):
    tokens, dm = x.shape
    ff = wgu.shape[1]
    chunks = _hidden_chunks(ff)
    tm = min(FFN_FWD_TILE, seq)

    def body(x_ref, sh_ref, sc_ref, wgu_ref, gu_ref, a_ref, h_ref):
        h = (x_ref[...] * (1.0 + sc_ref[0]) + sh_ref[0]).astype(BF16)
        h_ref[...] = h
        for at, wdt in chunks:
            gk = _dot_nt(h, wgu_ref[0, at:at + wdt, :])
            uk = _dot_nt(h, wgu_ref[1, at:at + wdt, :])
            gu_ref[0, :, at:at + wdt] = gk.astype(BF16)
            gu_ref[1, :, at:at + wdt] = uk.astype(BF16)
            a_ref[:, at:at + wdt] = (gk * _sigmoid(gk) * uk).astype(BF16)

    tile = pl.BlockSpec((tm, dm), lambda i: (i, 0))
    mod = _mod_spec(seq // tm, dm)
    return _call(
        body, job, name=name, grid=(tokens // tm,),
        in_specs=[tile, mod, mod, _const_spec(wgu.shape)],
        out_specs=[pl.BlockSpec((2, tm, ff), lambda i: (0, i, 0)), pl.BlockSpec((tm, ff), lambda i: (i, 0)), tile],
        out_shape=[jax.ShapeDtypeStruct((2, tokens, ff), BF16), jax.ShapeDtypeStruct((tokens, ff), BF16),
                   jax.ShapeDtypeStruct((tokens, dm), BF16)],
        args=(x, sh, sc, wgu))


def ffn_down(x, a, gt, wd, ln_g, ln_b, seq, name, job=None):
    tokens, dm = x.shape
    ff = wd.shape[0]
    chunks = _hidden_chunks(ff)
    tm = min(FFN_FWD_TILE, seq)

    def body(x_ref, a_ref, gt_ref, wd_ref, lg_ref, lb_ref, xo_ref, r_ref, f_ref):
        acc = jnp.zeros((tm, dm), F32)
        for at, wdt in chunks:
            acc = acc + _dot(a_ref[:, at:at + wdt], wd_ref[at:at + wdt, :])
        f_ref[...] = acc.astype(BF16)
        r = DN_ALPHA * x_ref[...] + (0.5 * (1.0 + gt_ref[0])) * acc
        r_ref[...] = r
        xhat, _ = _ln_stats(r)
        xo_ref[...] = xhat * lg_ref[...] + lb_ref[...]

    tile = pl.BlockSpec((tm, dm), lambda i: (i, 0))
    return _call(
        body, job, name=name, grid=(tokens // tm,),
        in_specs=[tile, pl.BlockSpec((tm, ff), lambda i: (i, 0)), _mod_spec(seq // tm, dm), _const_spec(wd.shape),
                  _const_spec((1, dm)), _const_spec((1, dm))],
        out_specs=[tile, tile, tile],
        out_shape=[jax.ShapeDtypeStruct((tokens, dm), F32), jax.ShapeDtypeStruct((tokens, dm), F32),
                   jax.ShapeDtypeStruct((tokens, dm), BF16)],
        args=(x, a, gt, wd, ln_g, ln_b))


def ffn_bwd(dr, df, x, gu, sc, wgu, wd, seq, name, job=None):
    tokens, dm = x.shape
    ff = wgu.shape[1]
    chunks = _hidden_chunks(ff)
    tm = min(FFN_WIDE_TILE, seq)
    tiles_per_seq = seq // tm
    nseq = tokens // seq

    def body(dr_ref, df_ref, x_ref, gu_ref, sc_ref, wgu_ref, wd_ref, dx_ref, dgu_ref, dmod_ref):
        @pl.when(pl.program_id(0) % tiles_per_seq == 0)
        def _():
            dmod_ref[...] = jnp.zeros_like(dmod_ref)

        df = df_ref[...]
        dh = jnp.zeros((tm, dm), F32)
        for at, wdt in chunks:
            cols = slice(at, at + wdt)
            da = _dot_nt(df, wd_ref[cols, :])
            gk = gu_ref[0, :, cols].astype(F32)
            uk = gu_ref[1, :, cols].astype(F32)
            sg = _sigmoid(gk)
            sil = gk * sg
            du = (da * sil).astype(BF16)
            dg = (da * uk * (sg * (1.0 + gk * (1.0 - sg)))).astype(BF16)
            dgu_ref[0, :, cols] = dg
            dgu_ref[1, :, cols] = du
            dh = dh + _dot(dg, wgu_ref[0, cols, :]) + _dot(du, wgu_ref[1, cols, :])
        dx_ref[...] = DN_ALPHA * dr_ref[...] + dh * (1.0 + sc_ref[0])
        dmod_ref[0, 0:1, :] += jnp.sum(dh, axis=0, keepdims=True)
        dmod_ref[0, 1:2, :] += jnp.sum(dh * x_ref[...], axis=0, keepdims=True)

    tile = pl.BlockSpec((tm, dm), lambda i: (i, 0))
    gu_spec = pl.BlockSpec((2, tm, ff), lambda i: (0, i, 0))
    return _call(
        body, job, name=name, grid=(tokens // tm,),
        in_specs=[tile, tile, tile, gu_spec, _mod_spec(tiles_per_seq, dm), _const_spec(wgu.shape), _const_spec(wd.shape)],
        out_specs=[tile, gu_spec, pl.BlockSpec((1, 2, dm), lambda i: (i // tiles_per_seq, 0, 0))],
        out_shape=[jax.ShapeDtypeStruct((tokens, dm), F32), jax.ShapeDtypeStruct((2, tokens, ff), BF16),
                   jax.ShapeDtypeStruct((nseq, 2, dm), F32)],
        args=(dr, df, x, gu, sc, wgu, wd), vmem=FFN_WIDE_VMEM)


def tn_matmul(a, b, name, job=None, b_cols=None, a_width=None):
    na, tokens, k_all = a.shape
    kk = k_all if a_width is None else a_width
    nka = k_all // kk
    assert nka * kk == k_all
    nb, _, cc = b.shape
    col = 0
    if b_cols is not None:
        col, cc = b_cols
    tt = tokens
    while 4 * tt * (kk + cc) + 8 * kk * cc > TN_VMEM_BUDGET and tt % 2 == 0 and tt > 256:
        tt //= 2
    steps = tokens // tt

    def body(a_ref, b_ref, o_ref, *acc):
        if steps == 1:
            o_ref[0, 0, 0] = _dot_tn(a_ref[0], b_ref[0]).astype(BF16)
            return
        acc_ref, = acc
        t = pl.program_id(3)

        @pl.when(t == 0)
        def _():
            acc_ref[...] = jnp.zeros_like(acc_ref)

        acc_ref[...] += _dot_tn(a_ref[0], b_ref[0])

        @pl.when(t == steps - 1)
        def _():
            o_ref[0, 0, 0] = acc_ref[...].astype(BF16)

    return _call(
        body, job, name=name, grid=(na, nka, nb, steps),
        in_specs=[pl.BlockSpec((1, tt, kk), lambda i, s, j, t: (i, t, s)),
                  pl.BlockSpec((1, tt, cc), lambda i, s, j, t: (j, t, col))],
        out_specs=[pl.BlockSpec((1, 1, 1, kk, cc), lambda i, s, j, t: (i, s, j, 0, 0))],
        out_shape=[jax.ShapeDtypeStruct((na, nka, nb, kk, cc), BF16)],
        scratch_shapes=[] if steps == 1 else [pltpu.VMEM((kk, cc), F32)], args=(a, b))


def proj_fwd(x1, sh, sc, w_in, seq, name, job=None):
    tokens, dm = x1.shape
    tm = min(MIX_TILE, seq)
    tiles_per_seq = seq // tm
    widths = [N_Q_HEADS * HEAD_DIM, N_KV_HEADS * HEAD_DIM, N_KV_HEADS * HEAD_DIM, 512, 512, 512]
    assert sum(widths) == w_in.shape[0]

    def body(x_ref, sh_ref, sc_ref, w_ref, *outs):
        h = (x_ref[...] * (1.0 + sc_ref[0]) + sh_ref[0]).astype(BF16)
        proj = _dot_nt(h, w_ref[...])
        at = 0
        for o_ref, wdt in zip(outs, widths):
            o_ref[...] = proj[:, at:at + wdt].astype(o_ref.dtype)
            at += wdt

    tile = pl.BlockSpec((tm, dm), lambda i: (i, 0))
    mod = _mod_spec(tiles_per_seq, dm)
    return _call(
        body, job, name=name, grid=(tokens // tm,),
        in_specs=[tile, mod, mod, _const_spec(w_in.shape)],
        out_specs=[pl.BlockSpec((tm, wdt), lambda i: (i, 0)) for wdt in widths],
        out_shape=[jax.ShapeDtypeStruct((tokens, wdt), F32 if i < 3 else BF16) for i, wdt in enumerate(widths)],
        args=(x1, sh, sc, w_in))


LANES = 2 * HEAD_DIM


def _head_lane(shape):
    return lax.broadcasted_iota(jnp.int32, shape, 1) % HEAD_DIM


def _lane_half(shape):
    return lax.broadcasted_iota(jnp.int32, shape, 1) // HEAD_DIM


def _swap_rot(v):
    lane = _head_lane(v.shape)
    half = ROT_DIM // 2
    return jnp.where(lane < half, pltpu.roll(v, LANES - half, 1),
                     jnp.where(lane < ROT_DIM, pltpu.roll(v, half, 1), 0.0))


def _rope(v, cos_t, sin_t):
    return v * cos_t + _swap_rot(v) * sin_t


def _unrope(dv, cos_t, sin_t):
    return dv * cos_t + _swap_rot(dv * sin_t)


def _both_halves(t, g):
    return jnp.where(_lane_half(t.shape) == g, t, pltpu.roll(t, HEAD_DIM, 1))


def _fold_halves(t, g):
    return jnp.where(_lane_half(t.shape) == g, t + pltpu.roll(t, HEAD_DIM, 1), 0.0)


def _stack_heads(blocks):
    rows = []
    for blk in blocks:
        half = _lane_half(blk.shape)
        rows += [jnp.where(half == 0, blk, 0.0), jnp.where(half == 1, blk, 0.0)]
    return jnp.concatenate(rows, axis=0)


def _unstack_heads(t, j):
    lo = t[(2 * j) * ATTN_BLOCK:(2 * j + 1) * ATTN_BLOCK]
    hi = t[(2 * j + 1) * ATTN_BLOCK:(2 * j + 2) * ATTN_BLOCK]
    return jnp.where(_lane_half(lo.shape) == 0, lo, hi)


def _band_mask(q0, w0):
    rows, cols = GQA_GROUP * ATTN_BLOCK, 2 * ATTN_BLOCK
    qi = lax.broadcasted_iota(jnp.int32, (rows, cols), 0) % ATTN_BLOCK + q0
    ki = lax.broadcasted_iota(jnp.int32, (rows, cols), 1) + w0
    diff = qi - ki
    return (diff >= 0) & (diff < ATTN_BLOCK)


def _attn_specs(seq):
    q_spec = pl.BlockSpec((seq, GQA_GROUP * HEAD_DIM), lambda b, g: (b, g))
    kv_spec = pl.BlockSpec((seq, LANES), lambda b, g: (b, 0))
    sink_spec = pl.BlockSpec((1, GQA_GROUP * ATTN_BLOCK, 1), lambda b, g: (g, 0, 0))
    return q_spec, kv_spec, sink_spec


def _block_starts(n):
    q0 = pl.multiple_of(n * ATTN_BLOCK, ATTN_BLOCK)
    w0 = pl.multiple_of(jnp.maximum(n - 1, 0) * ATTN_BLOCK, ATTN_BLOCK)
    return q0, w0


def _stacked_queries(ref, rows):
    return _stack_heads([ref[rows, j * LANES:(j + 1) * LANES] for j in range(2)]).astype(BF16)


def _sink_columns(sinks):
    return jnp.repeat(sinks.reshape(N_KV_HEADS, GQA_GROUP), ATTN_BLOCK, axis=1)[:, :, None]


def _probs_spec(nblk):
    return pl.BlockSpec((1, 1, nblk, GQA_GROUP * ATTN_BLOCK, 2 * ATTN_BLOCK), lambda b, g: (b, g, 0, 0, 0))


def _sink_probs_spec():
    return pl.BlockSpec((1, 1, GQA_GROUP * ATTN_BLOCK, LANES), lambda b, g: (b, g, 0, 0))


def attn_fwd(q, k, v, cos_t, sin_t, sinks, seq, name, job=None):
    tokens = q.shape[0]
    nblk = seq // ATTN_BLOCK
    assert nblk >= 2
    scale = HEAD_DIM ** -0.5

    nseq = tokens // seq
    rows_stacked = GQA_GROUP * ATTN_BLOCK
    assert nblk <= LANES

    def body(q_ref, k_ref, v_ref, cos_ref, sin_ref, sink_ref, o_ref, qr_ref, p_ref, ps_ref, kd_ref, vd_ref):
        g = pl.program_id(1)
        kd_ref[...] = _both_halves(_rope(k_ref[...].astype(F32), cos_ref[...], sin_ref[...]), g).astype(BF16)
        vd_ref[...] = _both_halves(v_ref[...].astype(F32), g).astype(BF16)
        sink = sink_ref[0]
        lane = lax.broadcasted_iota(jnp.int32, (rows_stacked, LANES), 1)

        ps_ref[...] = jnp.zeros_like(ps_ref)

        def block(n, carry):
            q0, w0 = _block_starts(n)
            rows, win = pl.ds(q0, ATTN_BLOCK), pl.ds(w0, 2 * ATTN_BLOCK)
            blocks = []
            for j in range(2):
                qr = _rope(q_ref[rows, j * LANES:(j + 1) * LANES].astype(F32), cos_ref[rows, :], sin_ref[rows, :]).astype(BF16)
                qr_ref[rows, j * LANES:(j + 1) * LANES] = qr
                blocks.append(qr)
            qs = _stack_heads(blocks)
            s = _dot_nt(qs, kd_ref[win, :]) * scale
            s = jnp.where(_band_mask(q0, w0), s, NEG_BIG)
            m = jnp.maximum(jnp.max(s, axis=-1, keepdims=True), sink)
            p = jnp.exp(s - m)
            e_sink = jnp.exp(sink - m)
            inv = pl.reciprocal(jnp.sum(p, axis=-1, keepdims=True) + e_sink, approx=True)
            pn = (p * inv).astype(BF16)
            p_ref[0, 0, n] = pn
            out = _dot(pn, vd_ref[win, :])
            for j in range(2):
                o_ref[rows, j * LANES:(j + 1) * LANES] = _unstack_heads(out, j).astype(o_ref.dtype)
            ps_ref[0, 0] = jnp.where(lane == n, e_sink * inv, ps_ref[0, 0])
            return carry

        lax.fori_loop(0, nblk, block, 0, unroll=2)

    q_spec, kv_spec, sink_spec = _attn_specs(seq)
    return _call(
        body, job, name=name, grid=(nseq, N_KV_HEADS),
        in_specs=[q_spec, kv_spec, kv_spec, kv_spec, kv_spec, sink_spec],
        out_specs=[q_spec, q_spec, _probs_spec(nblk), _sink_probs_spec()],
        out_shape=[jax.ShapeDtypeStruct(q.shape, BF16), jax.ShapeDtypeStruct(q.shape, BF16),
                   jax.ShapeDtypeStruct((nseq, N_KV_HEADS, nblk, rows_stacked, 2 * ATTN_BLOCK), BF16),
                   jax.ShapeDtypeStruct((nseq, N_KV_HEADS, rows_stacked, LANES), F32)],
        scratch_shapes=[pltpu.VMEM((seq, LANES), BF16), pltpu.VMEM((seq, LANES), BF16)],
        args=(q, k, v, cos_t, sin_t, _sink_columns(sinks)))


def attn_bwd(qr, k, v, do, probs, sink_probs, cos_t, sin_t, seq, name, job=None):
    tokens = qr.shape[0]
    nseq = tokens // seq
    nblk = seq // ATTN_BLOCK
    assert nblk >= 2
    rows_stacked = GQA_GROUP * ATTN_BLOCK
    scale = HEAD_DIM ** -0.5

    def body(q_ref, k_ref, v_ref, do_ref, p_ref, ps_ref, cos_ref, sin_ref, dq_ref, dk_ref, dv_ref, ds_ref,
             kd_ref, vd_ref, dkd_ref, dvd_ref, acc_ref):
        g = pl.program_id(1)
        kd_ref[...] = _both_halves(_rope(k_ref[...].astype(F32), cos_ref[...], sin_ref[...]), g).astype(BF16)
        vd_ref[...] = _both_halves(v_ref[...].astype(F32), g).astype(BF16)
        dkd_ref[...] = jnp.zeros_like(dkd_ref)
        dvd_ref[...] = jnp.zeros_like(dvd_ref)
        acc_ref[...] = jnp.zeros_like(acc_ref)
        lane = lax.broadcasted_iota(jnp.int32, (rows_stacked, LANES), 1)

        def block(n, carry):
            q0, w0 = _block_starts(n)
            rows, win = pl.ds(q0, ATTN_BLOCK), pl.ds(w0, 2 * ATTN_BLOCK)
            qs = _stacked_queries(q_ref, rows)
            dos = _stacked_queries(do_ref, rows)
            kw, vw = kd_ref[win, :], vd_ref[win, :]
            pn16 = p_ref[0, 0, n]
            pn = pn16.astype(F32)
            dvd_ref[win, :] += _dot_tn(pn16, dos)
            dp = _dot_nt(dos, vw)
            delta = jnp.sum(dp * pn, axis=-1, keepdims=True)
            ds = (pn * (dp - delta)).astype(BF16)
            dqs = _dot(ds, kw) * scale
            dkd_ref[win, :] += _dot_tn(ds, qs) * scale
            cos_b, sin_b = cos_ref[rows, :], sin_ref[rows, :]
            for j in range(2):
                dq_ref[rows, j * LANES:(j + 1) * LANES] = _unrope(_unstack_heads(dqs, j), cos_b, sin_b).astype(BF16)
            acc_ref[...] += jnp.where(lane == n, ps_ref[0, 0] * delta, 0.0)
            return carry

        lax.fori_loop(0, nblk // 2, lambda i, carry: block(2 * i + 1, block(2 * i, carry)), 0)
        ds_ref[0, 0] = -jnp.sum(acc_ref[...], axis=-1, keepdims=True)
        dk_g = _unrope(_fold_halves(dkd_ref[...], g), cos_ref[...], sin_ref[...])
        dv_g = _fold_halves(dvd_ref[...], g)

        @pl.when(g == 0)
        def _():
            dk_ref[...] = dk_g
            dv_ref[...] = dv_g

        @pl.when(g != 0)
        def _():
            dk_ref[...] += dk_g
            dv_ref[...] += dv_g

    q_spec, kv_spec, _ = _attn_specs(seq)
    return _call(
        body, job, name=name, grid=(nseq, N_KV_HEADS),
        in_specs=[q_spec, kv_spec, kv_spec, q_spec, _probs_spec(nblk), _sink_probs_spec(), kv_spec, kv_spec],
        out_specs=[q_spec, kv_spec, kv_spec, pl.BlockSpec((1, 1, rows_stacked, 1), lambda b, g: (b, g, 0, 0))],
        out_shape=[jax.ShapeDtypeStruct(qr.shape, BF16), jax.ShapeDtypeStruct(k.shape, F32),
                   jax.ShapeDtypeStruct(k.shape, F32), jax.ShapeDtypeStruct((nseq, N_KV_HEADS, rows_stacked, 1), F32)],
        scratch_shapes=[pltpu.VMEM((seq, LANES), BF16), pltpu.VMEM((seq, LANES), BF16),
                        pltpu.VMEM((seq, LANES), F32), pltpu.VMEM((seq, LANES), F32),
                        pltpu.VMEM((rows_stacked, LANES), F32)],
        args=(qr, k, v, do, probs, sink_probs, cos_t, sin_t))


CONV_COLS = 128


def _shift_down(z, by):
    t = lax.broadcasted_iota(jnp.int32, z.shape, 0)
    return jnp.where(t >= by, pltpu.roll(z, by, 0), 0.0)


def _shift_up(z, by):
    n = z.shape[0]
    t = lax.broadcasted_iota(jnp.int32, z.shape, 0)
    return jnp.where(t < n - by, pltpu.roll(z, n - by, 0), 0.0)


def conv_fwd(u, bg, cg, conv_w, seq, name):
    tokens, width = u.shape

    def body(u_ref, bg_ref, cg_ref, w_ref, o_ref):
        z = cg_ref[...].astype(F32) * u_ref[...].astype(F32)
        yy = w_ref[2:3, :] * z + w_ref[1:2, :] * _shift_down(z, 1) + w_ref[0:1, :] * _shift_down(z, 2)
        o_ref[...] = (bg_ref[...].astype(F32) * yy).astype(BF16)

    col = pl.BlockSpec((seq, CONV_COLS), lambda j, b: (b, j))
    return pl.pallas_call(
        body, name=name, grid=(width // CONV_COLS, tokens // seq),
        in_specs=[col, col, col, pl.BlockSpec((CONV_TAPS, CONV_COLS), lambda j, b: (0, j))],
        out_specs=col, out_shape=jax.ShapeDtypeStruct((tokens, width), BF16),
        compiler_params=_params(("parallel", "parallel")),
    )(u, bg, cg, conv_w)


def conv_bwd(dout, u, bg, cg, conv_w, seq, name):
    tokens, width = u.shape

    def body(do_ref, u_ref, bg_ref, cg_ref, w_ref, du_ref, dbg_ref, dcg_ref, dw_ref):
        uu, cg_v, do = u_ref[...].astype(F32), cg_ref[...].astype(F32), do_ref[...].astype(F32)
        z = cg_v * uu
        z1, z2 = _shift_down(z, 1), _shift_down(z, 2)
        yy = w_ref[2:3, :] * z + w_ref[1:2, :] * z1 + w_ref[0:1, :] * z2
        dbg_ref[...] = (do * yy).astype(BF16)
        dyy = do * bg_ref[...].astype(F32)
        dz = w_ref[2:3, :] * dyy + w_ref[1:2, :] * _shift_up(dyy, 1) + w_ref[0:1, :] * _shift_up(dyy, 2)
        du_ref[...] = (dz * cg_v).astype(BF16)
        dcg_ref[...] = (dz * uu).astype(BF16)

        @pl.when(pl.program_id(1) == 0)
        def _():
            dw_ref[...] = jnp.zeros_like(dw_ref)

        dw_ref[0:1, :] += jnp.sum(dyy * z2, axis=0, keepdims=True)
        dw_ref[1:2, :] += jnp.sum(dyy * z1, axis=0, keepdims=True)
        dw_ref[2:3, :] += jnp.sum(dyy * z, axis=0, keepdims=True)

    col = pl.BlockSpec((seq, CONV_COLS), lambda j, b: (b, j))
    w_spec = pl.BlockSpec((CONV_TAPS, CONV_COLS), lambda j, b: (0, j))
    act = jax.ShapeDtypeStruct((tokens, width), BF16)
    return pl.pallas_call(
        body, name=name, grid=(width // CONV_COLS, tokens // seq),
        in_specs=[col, col, col, col, w_spec], out_specs=[col, col, col, w_spec],
        out_shape=[act, act, act, jax.ShapeDtypeStruct((CONV_TAPS, width), F32)],
        compiler_params=_params(("parallel", "arbitrary")),
    )(dout, u, bg, cg, conv_w)


def out_fwd(x1, attn, conv, gt, w_out, ln_g, ln_b, seq, name, job=None):
    tokens, dm = x1.shape
    half = attn.shape[1]
    tm = min(MIX_TILE, seq)
    tiles_per_seq = seq // tm

    def body(x_ref, a_ref, c_ref, gt_ref, w_ref, lg_ref, lb_ref, xo_ref, r_ref, mi_ref, mix_ref):
        mixin = jnp.concatenate([a_ref[...], c_ref[...]], axis=1).astype(BF16)
        mi_ref[...] = mixin
        mix = _dot(mixin, w_ref[...])
        mix_ref[...] = mix.astype(BF16)
        r = DN_ALPHA * x_ref[...] + (1.0 + gt_ref[0]) * mix
        r_ref[...] = r
        xhat, _ = _ln_stats(r)
        xo_ref[...] = xhat * lg_ref[...] + lb_ref[...]

    tile = pl.BlockSpec((tm, dm), lambda i: (i, 0))
    htile = pl.BlockSpec((tm, half), lambda i: (i, 0))
    return _call(
        body, job, name=name, grid=(tokens // tm,),
        in_specs=[tile, htile, htile, _mod_spec(tiles_per_seq, dm), _const_spec(w_out.shape),
                  _const_spec((1, dm)), _const_spec((1, dm))],
        out_specs=[tile, tile, tile, tile],
        out_shape=[jax.ShapeDtypeStruct((tokens, dm), F32), jax.ShapeDtypeStruct((tokens, dm), F32),
                   jax.ShapeDtypeStruct((tokens, dm), BF16), jax.ShapeDtypeStruct((tokens, dm), BF16)],
        args=(x1, attn, conv, gt, w_out, ln_g, ln_b))


def out_bwd(dy, r, mix, gt, w_out, ln_g, seq, name, job=None):
    tokens, dm = r.shape
    half = dm // 2
    tm = min(MIX_TILE, seq)
    tiles_per_seq = seq // tm
    nseq = tokens // seq

    def body(dy_ref, r_ref, mix_ref, gt_ref, w_ref, lg_ref, dres_ref, da_ref, dc_ref, dmix_ref, dln_ref, dgt_ref):
        i = pl.program_id(0)
        dr, dgain, dbias = _ln_bwd(dy_ref[...], r_ref[...], lg_ref[...])

        @pl.when(i == 0)
        def _():
            dln_ref[...] = jnp.zeros_like(dln_ref)

        @pl.when(i % tiles_per_seq == 0)
        def _():
            dgt_ref[...] = jnp.zeros_like(dgt_ref)

        dln_ref[0:1, :] += dgain
        dln_ref[1:2, :] += dbias
        dgt_ref[0] += jnp.sum(dr * mix_ref[...].astype(F32), axis=0, keepdims=True)
        dres_ref[...] = DN_ALPHA * dr
        dmix = ((1.0 + gt_ref[0]) * dr).astype(BF16)
        dmix_ref[...] = dmix
        dmixin = _dot_nt(dmix, w_ref[...])
        da_ref[...] = dmixin[:, :half].astype(BF16)
        dc_ref[...] = dmixin[:, half:].astype(BF16)

    tile = pl.BlockSpec((tm, dm), lambda i: (i, 0))
    htile = pl.BlockSpec((tm, half), lambda i: (i, 0))
    return _call(
        body, job, name=name, grid=(tokens // tm,),
        in_specs=[tile, tile, tile, _mod_spec(tiles_per_seq, dm), _const_spec(w_out.shape), _const_spec((1, dm))],
        out_specs=[tile, htile, htile, tile, pl.BlockSpec((2, dm), lambda i: (0, 0)),
                   pl.BlockSpec((1, 1, dm), lambda i: (i // tiles_per_seq, 0, 0))],
        out_shape=[jax.ShapeDtypeStruct((tokens, dm), F32), jax.ShapeDtypeStruct((tokens, half), BF16),
                   jax.ShapeDtypeStruct((tokens, half), BF16), jax.ShapeDtypeStruct((tokens, dm), BF16),
                   jax.ShapeDtypeStruct((2, dm), F32), jax.ShapeDtypeStruct((nseq, 1, dm), F32)],
        args=(dy, r, mix, gt, w_out, ln_g))


def proj_bwd(parts, dres, x1, sh, sc, w_in, r_prev, f_prev, gt_prev, ln_g_prev, seq, name, job=None):
    tokens, dm = x1.shape
    tm = min(MIX_TILE, seq)
    tiles_per_seq = seq // tm
    nseq = tokens // seq
    widths = [p.shape[1] for p in parts]
    total = sum(widths)

    def body(*refs):
        part_refs = refs[:6]
        (dres_ref, x_ref, sh_ref, sc_ref, w_ref, r_ref, f_ref, gt_ref, lg_ref,
         dr_ref, df_ref, dproj_ref, h_ref, dmod_ref, dln_ref, dgt_ref) = refs[6:]
        i = pl.program_id(0)
        dproj = jnp.concatenate([p[...].astype(BF16) for p in part_refs], axis=1)
        dproj_ref[...] = dproj
        dh = _dot(dproj, w_ref[...])
        xx = x_ref[...]
        one_sc = 1.0 + sc_ref[0]
        h_ref[...] = (xx * one_sc + sh_ref[0]).astype(BF16)
        dr, dgain, dbias = _ln_bwd(dres_ref[...] + dh * one_sc, r_ref[...], lg_ref[...])
        dr_ref[...] = dr
        df_ref[...] = ((0.5 * (1.0 + gt_ref[0])) * dr).astype(BF16)

        @pl.when(i == 0)
        def _():
            dln_ref[...] = jnp.zeros_like(dln_ref)

        @pl.when(i % tiles_per_seq == 0)
        def _():
            dmod_ref[...] = jnp.zeros_like(dmod_ref)
            dgt_ref[...] = jnp.zeros_like(dgt_ref)

        dmod_ref[0, 0:1, :] += jnp.sum(dh, axis=0, keepdims=True)
        dmod_ref[0, 1:2, :] += jnp.sum(dh * xx, axis=0, keepdims=True)
        dln_ref[0:1, :] += dgain
        dln_ref[1:2, :] += dbias
        dgt_ref[0] += jnp.sum(dr * (0.5 * f_ref[...].astype(F32)), axis=0, keepdims=True)

    tile = pl.BlockSpec((tm, dm), lambda i: (i, 0))
    mod = _mod_spec(tiles_per_seq, dm)
    return _call(
        body, job, name=name, grid=(tokens // tm,),
        in_specs=[pl.BlockSpec((tm, wdt), lambda i: (i, 0)) for wdt in widths]
        + [tile, tile, mod, mod, _const_spec(w_in.shape), tile, tile, mod, _const_spec((1, dm))],
        out_specs=[tile, tile, pl.BlockSpec((tm, total), lambda i: (i, 0)), tile,
                   pl.BlockSpec((1, 2, dm), lambda i: (i // tiles_per_seq, 0, 0)),
                   pl.BlockSpec((2, dm), lambda i: (0, 0)), mod],
        out_shape=[jax.ShapeDtypeStruct((tokens, dm), F32), jax.ShapeDtypeStruct((tokens, dm), BF16),
                   jax.ShapeDtypeStruct((tokens, total), BF16), jax.ShapeDtypeStruct((tokens, dm), BF16),
                   jax.ShapeDtypeStruct((nseq, 2, dm), F32), jax.ShapeDtypeStruct((2, dm), F32),
                   jax.ShapeDtypeStruct((nseq, 1, dm), F32)],
        args=(*parts, dres, x1, sh, sc, w_in, r_prev, f_prev, gt_prev, ln_g_prev))


def _rope_tables(positions):
    half = ROT_DIM // 2
    inv_freq = np.power(np.float32(ROPE_THETA), -np.arange(0, ROT_DIM, 2, dtype=np.float32) / ROT_DIM)
    lane = np.arange(LANES) % HEAD_DIM
    freq = np.where(lane < ROT_DIM, inv_freq[lane % half], 0.0).astype(np.float32)
    sign = np.where(lane < half, -1.0, 1.0).astype(np.float32)
    ang = positions.astype(F32)[:, None] * freq[None, :]
    return jnp.cos(ang), sign[None, :] * jnp.sin(ang)


def kernel(x, c, positions, w_ada, b_ada, ffn1_w_gate_up, ffn1_w_down, ln1_g, ln1_b, w_in, conv_w, attn_sinks, w_out, ln2_g, ln2_b, ffn2_w_gate_up, ffn2_w_down, ln3_g, ln3_b, loss_target, m_w_ada, m_b_ada, m_ffn1_w_gate_up, m_ffn1_w_down, m_ln1_g, m_ln1_b, m_w_in, m_conv_w, m_attn_sinks, m_w_out, m_ln2_g, m_ln2_b, m_ffn2_w_gate_up, m_ffn2_w_down, m_ln3_g, m_ln3_b, v_w_ada, v_b_ada, v_ffn1_w_gate_up, v_ffn1_w_down, v_ln1_g, v_ln1_b, v_w_in, v_conv_w, v_attn_sinks, v_w_out, v_ln2_g, v_ln2_b, v_ffn2_w_gate_up, v_ffn2_w_down, v_ln3_g, v_ln3_b):
    nseq, seq, dm = x.shape
    tokens = nseq * seq
    dev = 4 * lax.axis_index("x") + 2 * lax.axis_index("y") + lax.axis_index("c")
    core = lax.axis_index("c").astype(jnp.int32).reshape(1)
    ada_cols = w_ada.shape[2]
    ff = ffn1_w_down.shape[1] * N_DEV
    fc = ff // 4
    in_cols = w_in.shape[2]
    conv_cols = conv_w.shape[2]

    def t_bf16(w):
        return w[0].T.astype(BF16)

    c_all, convw_all = all_gather([c, conv_w[0]], "gather_cond")
    c_all = c_all.reshape(N_DEV * nseq, dm)
    convw_full = convw_all.transpose(1, 0, 2).reshape(CONV_TAPS, N_DEV * conv_cols)

    b_cols = lax.dynamic_slice(b_ada, (0, dev * ada_cols), (1, ada_cols))
    cond_all, mod_cols = ada_fwd(c_all, w_ada[0], b_cols, "ada_fwd")
    wgu1, mod_all = all_gather([t_bf16(ffn1_w_gate_up), mod_cols], "gather_ffn1")
    wgu1 = wgu1.reshape(2, ff, dm)
    mod = lax.dynamic_slice(mod_all, (0, dev * nseq, 0), (N_DEV, nseq, ada_cols))
    mod = mod.transpose(1, 0, 2).reshape(nseq, 9, 1, dm)
    sh1, sc1, g1, sh2, sc2, g2, sh3, sc3, g3 = [mod[:, i] for i in range(9)]

    x0 = x.reshape(tokens, dm)
    (gu1, a1, h1), (wd1, wout) = ffn_up(x0, sh1, sc1, wgu1, seq, "ffn1_up",
                                        job=_GatherJob([ffn1_w_down[0].astype(BF16), w_out[0].astype(BF16)]))
    wd1, wout = wd1.reshape(ff, dm), wout.reshape(dm, dm)
    (x1, r1, f1), (win,) = ffn_down(x0, a1, g1, wd1, ln1_g, ln1_b, seq, "ffn1_down", job=_GatherJob([t_bf16(w_in)]))
    win = win.reshape(N_DEV * in_cols, dm)
    (q, k, v, u, bg, cg), wd2_spread = proj_fwd(x1, sh2, sc2, win, seq, "proj_fwd",
                                                job=gather_spread_job([ffn2_w_down[0].astype(BF16)]))
    cos_t, sin_t = _rope_tables(positions.reshape(tokens))
    sinks = attn_sinks[0]
    (attn, q_rot, probs, sink_probs), wgu2_spread = attn_fwd(q, k, v, cos_t, sin_t, sinks, seq, "attn_fwd",
                                                             job=gather_spread_job([t_bf16(ffn2_w_gate_up)]))
    conv = conv_fwd(u, bg, cg, convw_full, seq, "conv_fwd")
    (x2, r2, mixin, mix), (wd2, wgu2) = out_fwd(x1, attn, conv, g2, wout, ln2_g, ln2_b, seq, "out_fwd",
                                                job=gather_forward_job(wd2_spread + wgu2_spread))
    wd2, wgu2 = wd2.reshape(ff, dm), wgu2.reshape(2, ff, dm)
    target = loss_target.reshape(tokens, dm)
    dr3, df3, gu3, a3, h3, loss_part, dln3, dg3 = ffn_loss(x2, sh3, sc3, g3, wgu2, wd2, ln3_g, ln3_b, target, seq, "ffn2_fwd")

    (dx2, dgu3, dmod3), _ = ffn_bwd(dr3, df3, x2, gu3, sc3, wgu2, wd2, seq, "ffn2_bwd")
    pair = 2 * fc
    g_wd2 = tn_matmul(a3[None], df3[None], "ffn2_dwd", a_width=pair)[0][0].reshape(N_DEV, ff // N_DEV, dm)
    g_wgu2 = tn_matmul(dgu3, h3[None], "ffn2_dwgu", a_width=pair)[0][0].reshape(N_DEV, fc, dm)
    (dres2, dattn, dconv, dmix, dln2, dg2), swapped = out_bwd(dx2, r2, mix, g2, wout, ln2_g, seq, "out_bwd",
                                                              job=swap_job([g_wgu2, g_wd2]))
    p_wgu2, own_wgu2 = pair_sum(core, g_wgu2, swapped[0], "pair_wgu2")
    p_wd2, own_wd2 = pair_sum(core, g_wd2, swapped[1], "pair_wd2")
    du, dbg, dcg, dconvw = conv_bwd(dconv, u, bg, cg, convw_full, seq, "conv_bwd")
    (dq, dk, dv, dsink_rows), (far_wd2,) = attn_bwd(
        q_rot, k, v, dattn, probs, sink_probs, cos_t, sin_t, seq, "attn_bwd", job=chip_exchange_job([p_wd2]))
    parts = [dq, dk, dv, du, dbg, dcg]
    (dr1, df1, dproj, h2, dmod2, dln1, dg1), far_top = proj_bwd(
        parts, dres2, x1, sh2, sc2, win, r1, f1, g1, ln1_g, seq, "proj_bwd",
        job=chip_exchange_job([p_wgu2], rows=(0, fc // 2)))
    (dx0, dgu1, dmod1), _ = ffn_bwd(dr1, df1, x0, gu1, sc1, wgu1, wd1, seq, "ffn1_bwd")

    dmod = jnp.concatenate([dmod1, dg1, dmod2, dg2, dmod3, dg3], axis=1).reshape(nseq, 9 * dm)
    half = dm // 2
    jobs = _Jobs([gather_spread_job([dmod]),
                  chip_exchange_job([p_wgu2], rows=(fc // 2, fc // 2), into=far_top)])
    (g_wd1,), res = tn_matmul(a1[None], df1[None], "ffn1_dwd", job=jobs, a_width=pair)
    dmod_spread, (far_wgu2,) = jobs.split(res)
    g_wd1 = g_wd1.reshape(N_DEV, ff // N_DEV, dm)
    jobs = _Jobs([swap_job([g_wd1]), gather_forward_job(dmod_spread)])
    (g_l,), res = tn_matmul(dgu1, h1[None], "ffn1_dwgu_l", job=jobs, b_cols=(0, half), a_width=pair)
    (sw_wd1,), (dmod_all,) = jobs.split(res)
    g_l = g_l.reshape(N_DEV, fc, half)
    p_wd1, own_wd1 = pair_sum(core, g_wd1, sw_wd1, "pair_wd1")
    jobs = _Jobs([chip_exchange_job([p_wd1]), swap_job([g_l])])
    (g_r,), res = tn_matmul(dgu1, h1[None], "ffn1_dwgu_r", job=jobs, b_cols=(1, half), a_width=pair)
    (far_wd1,), (sw_l,) = jobs.split(res)
    g_r = g_r.reshape(N_DEV, fc, half)
    p_l, own_l = pair_sum(core, g_l, sw_l, "pair_wgu1_l")

    dmod_cols = lax.dynamic_slice(dmod_all.reshape(N_DEV * nseq, 9 * dm), (0, dev * ada_cols), (N_DEV * nseq, ada_cols))
    grad_w_ada, gb_cols = ada_bwd(cond_all, dmod_cols, "ada_bwd")
    dsinks = jnp.sum(dsink_rows.reshape(nseq, N_Q_HEADS, ATTN_BLOCK), axis=(0, 2))
    small = jnp.zeros((8, dm), F32)
    small = small.at[0:2].set(dln1).at[2:4].set(dln2).at[4:6].set(dln3)
    small = small.at[6, 0:N_Q_HEADS].set(dsinks).at[7, 0].set(loss_part[0, 0])

    jobs = _Jobs([chip_exchange_job([p_l]), swap_job([g_r]), gather_spread_job([small, dconvw, gb_cols])])
    (g_win,), res = tn_matmul(dproj[None], h2[None], "dwin", job=jobs)
    (far_l,), (sw_r,), small_spread = jobs.split(res)
    g_win = g_win.reshape(N_DEV, in_cols, dm)
    p_r, own_r = pair_sum(core, g_r, sw_r, "pair_wgu1_r")
    jobs = _Jobs([chip_exchange_job([p_r]), swap_job([g_win]), gather_forward_job(small_spread)])
    (g_wout,), res = tn_matmul(mixin[None], dmix[None], "dwout", job=jobs)
    (far_r,), (sw_win,), (small_all, dconvw_all, gb_all) = jobs.split(res)
    g_wout = g_wout.reshape(N_DEV, dm // N_DEV, dm)
    p_win, own_win = pair_sum(core, g_win, sw_win, "pair_win")

    given = dict(w_ada=(w_ada, m_w_ada, v_w_ada), b_ada=(b_ada, m_b_ada, v_b_ada),
                 ffn1_w_gate_up=(ffn1_w_gate_up, m_ffn1_w_gate_up, v_ffn1_w_gate_up),
                 ffn1_w_down=(ffn1_w_down, m_ffn1_w_down, v_ffn1_w_down),
                 ln1_g=(ln1_g, m_ln1_g, v_ln1_g), ln1_b=(ln1_b, m_ln1_b, v_ln1_b),
                 w_in=(w_in, m_w_in, v_w_in), conv_w=(conv_w, m_conv_w, v_conv_w),
                 attn_sinks=(attn_sinks, m_attn_sinks, v_attn_sinks), w_out=(w_out, m_w_out, v_w_out),
                 ln2_g=(ln2_g, m_ln2_g, v_ln2_g), ln2_b=(ln2_b, m_ln2_b, v_ln2_b),
                 ffn2_w_gate_up=(ffn2_w_gate_up, m_ffn2_w_gate_up, v_ffn2_w_gate_up),
                 ffn2_w_down=(ffn2_w_down, m_ffn2_w_down, v_ffn2_w_down),
                 ln3_g=(ln3_g, m_ln3_g, v_ln3_g), ln3_b=(ln3_b, m_ln3_b, v_ln3_b))
    transposed = ("ffn1_w_gate_up", "ffn2_w_gate_up", "w_in")

    def big_adamw(nm, grad, far=None):
        flip = nm in transposed
        w2, m2, v2 = [t[0].T if flip else t[0] for t in given[nm]]
        return [t.T[None] if flip else t[None] for t in adamw(w2, grad, m2, v2, "adamw_" + nm, others=far)]

    jobs = _Jobs([chip_exchange_job([p_win]), swap_job([g_wout])])
    (far_win,), (sw_wout,) = jobs.split(run_job(jobs, "rs_tail_win"))
    p_wout, own_wout = pair_sum(core, g_wout, sw_wout, "pair_wout")
    (far_wout,) = run_job(chip_exchange_job([p_wout]), "rs_tail_wout")

    grads = {
        "ffn1_w_gate_up": [own_l, own_r], "ffn1_w_down": own_wd1,
        "w_in": own_win, "w_out": own_wout, "ffn2_w_gate_up": own_wgu2, "ffn2_w_down": own_wd2,
    }
    others = {"ffn1_w_gate_up": [far_l, far_r], "ffn1_w_down": far_wd1,
              "w_in": far_win, "w_out": far_wout, "ffn2_w_gate_up": far_wgu2, "ffn2_w_down": far_wd2}
    results = {"w_ada": big_adamw("w_ada", grad_w_ada)}
    for nm in grads:
        results[nm] = big_adamw(nm, grads[nm], others[nm])

    small_rows = ["ln1_g", "ln1_b", "ln2_g", "ln2_b", "ln3_g", "ln3_b", "attn_sinks", "loss"]
    small_sums = sum_devices(small_all, "sum_small", row_widths=[dm] * 6 + [N_Q_HEADS, 1])
    dconvw_sum = sum_devices(dconvw_all, "sum_convw")
    grads.update(zip(small_rows, small_sums))
    loss = grads.pop("loss").reshape(())
    grads["b_ada"] = gb_all.reshape(1, N_DEV * ada_cols)
    grads["conv_w"] = lax.dynamic_slice(dconvw_sum, (0, dev * conv_cols), (CONV_TAPS, conv_cols))

    order = ["w_ada", "b_ada", "ffn1_w_gate_up", "ffn1_w_down", "ln1_g", "ln1_b", "w_in", "conv_w", "attn_sinks",
             "w_out", "ln2_g", "ln2_b", "ffn2_w_gate_up", "ffn2_w_down", "ln3_g", "ln3_b"]
    small_names = [nm for nm in order if nm not in results]
    items = []
    for nm in small_names:
        shape = given[nm][0].shape
        two_d = (shape[-2], shape[-1])
        items.append((given[nm][0].reshape(two_d), grads[nm].reshape(two_d), *[t.reshape(two_d) for t in given[nm][1:]]))
    for nm, res in zip(small_names, adamw_small(items, "adamw_small")):
        shape = given[nm][0].shape
        results[nm] = [grads[nm].reshape(shape)] + [t.reshape(shape) for t in res]
    grad_x = dx0.reshape(nseq, seq, dm)
    return (loss, grad_x, *[results[nm][i] for i in range(4) for nm in order])
```

```python
import functools

import jax
import jax.numpy as jnp
import numpy as np
from jax import lax
from jax.experimental import pallas as pl
from jax.experimental.pallas import tpu as pltpu

F32 = jnp.float32
BF16 = jnp.bfloat16
MESH = pl.DeviceIdType.MESH

N_DEV = 8
N_CHIP = 4
HEAD_DIM = 64
N_Q_HEADS = 8
N_KV_HEADS = 2
GQA_GROUP = N_Q_HEADS // N_KV_HEADS
ATTN_BLOCK = 128
ROT_DIM = 16
ROPE_THETA = 500000.0
CONV_TAPS = 3
LN_EPS = 1e-5
DN_ALPHA = 2.0 ** 0.25
ADAM_LR = 0.001
ADAM_B1 = 0.9
ADAM_B2 = 0.999
ADAM_EPS = 1e-08
ADAM_WD = 0.01
ADAM_STEP = 10
NEG_BIG = -1e30

VMEM_LIMIT = 56 * 1024 * 1024
FFN_FWD_TILE = 512
MIX_TILE = 512
FFN_WIDE_TILE = 512
FFN_WIDE_VMEM = 62 * 1024 * 1024
TN_VMEM_BUDGET = 36 * 1024 * 1024


def _params(semantics=None, vmem=VMEM_LIMIT):
    return pltpu.CompilerParams(dimension_semantics=semantics, vmem_limit_bytes=vmem)


def _dot(a, b):
    return jnp.dot(a, b, preferred_element_type=F32)


def _dot_nt(a, b):
    return lax.dot_general(a, b, (((1,), (1,)), ((), ())), preferred_element_type=F32)


def _dot_tn(a, b):
    return lax.dot_general(a, b, (((0,), (0,)), ((), ())), preferred_element_type=F32)


def _sigmoid(x):
    return pl.reciprocal(1.0 + jnp.exp(-x), approx=True)


def _ln_stats(r):
    mu = jnp.mean(r, axis=-1, keepdims=True)
    d = r - mu
    var = jnp.mean(d * d, axis=-1, keepdims=True)
    rstd = lax.rsqrt(var + LN_EPS)
    return d * rstd, rstd


def _ln_bwd(dy, r, g):
    return _ln_bwd_normalized(dy, *_ln_stats(r), g)


def _ln_bwd_normalized(dy, xhat, rstd, g):
    dxhat = dy * g
    c1 = jnp.mean(dxhat, axis=-1, keepdims=True)
    c2 = jnp.mean(dxhat * xhat, axis=-1, keepdims=True)
    dr = rstd * (dxhat - c1 - xhat * c2)
    return dr, jnp.sum(dy * xhat, axis=0, keepdims=True), jnp.sum(dy, axis=0, keepdims=True)


def _const_spec(shape):
    nd = len(shape)
    return pl.BlockSpec(shape, lambda *_: (0,) * nd, pipeline_mode=pl.Buffered(1))


def all_gather(arrs, name):
    n = len(arrs)

    def body(*refs):
        ins, outs = refs[:n], refs[n:2 * n]
        send_sems, recv_sems, local_sems = refs[2 * n:]
        x, y, c = lax.axis_index("x"), lax.axis_index("y"), lax.axis_index("c")
        me, sibling = (x, y, c), (x, y, 1 - c)
        chips = [(1 - x, y), (x, 1 - y), (1 - x, 1 - y)]

        def slot(i, p):
            return outs[i].at[4 * p[0] + 2 * p[1] + p[2]]

        def copy(i, k, block, to, src=None):
            return pltpu.make_async_remote_copy(
                src_ref=slot(i, block) if src is None else src, dst_ref=slot(i, block),
                send_sem=send_sems.at[i, k], recv_sem=recv_sems.at[i, k],
                device_id=to, device_id_type=MESH)

        mine = [pltpu.make_async_copy(ins[i], slot(i, me), local_sems.at[i]) for i in range(n)]
        for cp in mine:
            cp.start()
        first = []
        for i in range(n):
            first.append(copy(i, 0, me, sibling, src=ins[i]))
            first += [copy(i, 1 + j, me, (*chip, c), src=ins[i]) for j, chip in enumerate(chips)]
        for cp in first:
            cp.start()
        passed = []
        for i in range(n):
            for j, chip in enumerate(chips):
                copy(i, 1 + j, (*chip, c), me).wait_recv()
                cp = copy(i, 4 + j, (*chip, c), sibling)
                cp.start()
                passed.append(cp)
        for i in range(n):
            copy(i, 0, sibling, me).wait_recv()
            for j, chip in enumerate(chips):
                copy(i, 4 + j, (*chip, 1 - c), me).wait_recv()
        for cp in first + passed:
            cp.wait_send()
        for cp in mine:
            cp.wait()

    any_spec = pl.BlockSpec(memory_space=pl.ANY)
    return pl.pallas_call(
        body, name=name,
        out_shape=[jax.ShapeDtypeStruct((N_DEV, *a.shape), a.dtype) for a in arrs],
        in_specs=[any_spec] * n, out_specs=[any_spec] * n,
        scratch_shapes=[pltpu.SemaphoreType.DMA((n, 7)), pltpu.SemaphoreType.DMA((n, 7)),
                        pltpu.SemaphoreType.DMA((n,))],
    )(*arrs)


def _place():
    x, y, c = lax.axis_index("x"), lax.axis_index("y"), lax.axis_index("c")
    return x, y, c, [(1 - x, y), (x, 1 - y), (1 - x, 1 - y)]


def _slot(p):
    return 4 * p[0] + 2 * p[1] + p[2]


class _Job:
    def __init__(self, ins, outs, nsem, copies, aliases=None, local=None):
        self.ins, self.outs, self.nsem, self.copies = list(ins), list(outs), nsem, copies
        self.aliases = aliases or {}
        self.local = local

    def scratch(self):
        s = [pltpu.SemaphoreType.DMA(self.nsem), pltpu.SemaphoreType.DMA(self.nsem)]
        if self.local is not None:
            s.append(pltpu.SemaphoreType.DMA((len(self.ins),)))
        return s

    def start(self, ins, outs, sems):
        if self.local is not None:
            for cp in self.local(ins, outs, sems[2]):
                cp.start()
        for cp in self.copies(ins, outs, sems[0], sems[1])[0]:
            cp.start()

    def finish(self, ins, outs, sems):
        started, awaited = self.copies(ins, outs, sems[0], sems[1])
        for cp in awaited:
            cp.wait_recv()
        for cp in started:
            cp.wait_send()
        if self.local is not None:
            for cp in self.local(ins, outs, sems[2]):
                cp.wait()


class _Jobs:
    def __init__(self, jobs):
        self.jobs = jobs
        self.ins = [a for j in jobs for a in j.ins]
        self.outs = [o for j in jobs for o in j.outs]
        self.two_phase = any(getattr(j, "two_phase", False) for j in jobs)
        self.aliases = {}
        at_in = at_out = 0
        for j in jobs:
            self.aliases.update({at_in + i: at_out + o for i, o in j.aliases.items()})
            at_in, at_out = at_in + len(j.ins), at_out + len(j.outs)

    def scratch(self):
        return [s for j in self.jobs for s in j.scratch()]

    def _each(self, ins, outs, sems):
        at_in = at_out = at_sem = 0
        for j in self.jobs:
            n_in, n_out, n_sem = len(j.ins), len(j.outs), len(j.scratch())
            yield j, ins[at_in:at_in + n_in], outs[at_out:at_out + n_out], sems[at_sem:at_sem + n_sem]
            at_in, at_out, at_sem = at_in + n_in, at_out + n_out, at_sem + n_sem

    def start(self, ins, outs, sems):
        for j, i, o, s in self._each(ins, outs, sems):
            j.start(i, o, s)

    def turn(self, ins, outs, sems):
        for j, i, o, s in self._each(ins, outs, sems):
            if getattr(j, "two_phase", False):
                j.turn(i, o, s)

    def finish(self, ins, outs, sems):
        for j, i, o, s in self._each(ins, outs, sems):
            j.finish(i, o, s)

    def split(self, results):
        at, parts = 0, []
        for j in self.jobs:
            parts.append(results[at:at + len(j.outs)])
            at += len(j.outs)
        return parts


def _remote(src, dst, send, recv, idx, to):
    return pltpu.make_async_remote_copy(src_ref=src, dst_ref=dst, send_sem=send.at[idx], recv_sem=recv.at[idx],
                                        device_id=to, device_id_type=MESH)


def _spread_copies(ins, outs, send, recv, base=0):
    x, y, c, chips = _place()
    me = (x, y, c)
    peers = [(x, y, 1 - c)] + [(*chip, c) for chip in chips]
    started, awaited = [], []
    for i, (src, dst) in enumerate(zip(ins, outs)):
        for k, peer in enumerate(peers):
            started.append(_remote(src, dst.at[_slot(me)], send, recv, (base + i, k), peer))
            awaited.append(_remote(src, dst.at[_slot(peer)], send, recv, (base + i, k), peer))
    return started, awaited


def _forward_copies(ins, outs, send, recv, base=0):
    x, y, c, chips = _place()
    started, awaited = [], []
    for i, buf in enumerate(outs):
        for j, chip in enumerate(chips):
            mine, theirs = buf.at[_slot((*chip, c))], buf.at[_slot((*chip, 1 - c))]
            started.append(_remote(mine, mine, send, recv, (base + i, j), (x, y, 1 - c)))
            awaited.append(_remote(theirs, theirs, send, recv, (base + i, j), (x, y, 1 - c)))
    return started, awaited


def _own_block_copies(ins, outs, sems):
    x, y, c, _ = _place()
    return [pltpu.make_async_copy(src, dst.at[_slot((x, y, c))], sems.at[i])
            for i, (src, dst) in enumerate(zip(ins, outs))]


def gather_spread_job(shards):
    outs = [jax.ShapeDtypeStruct((N_DEV, *a.shape), a.dtype) for a in shards]
    return _Job(shards, outs, (len(shards), 4), _spread_copies, local=_own_block_copies)


def gather_forward_job(fulls):
    outs = [jax.ShapeDtypeStruct(a.shape, a.dtype) for a in fulls]
    return _Job(fulls, outs, (len(fulls), 3), _forward_copies, aliases={i: i for i in range(len(fulls))})


TURN_EIGHTHS = 6


class _GatherJob:
    two_phase = True

    def __init__(self, shards):
        self.ins = list(shards)
        self.outs = [jax.ShapeDtypeStruct((N_DEV, *a.shape), a.dtype) for a in shards]
        self.aliases = {}

    def scratch(self):
        n = len(self.ins)
        return [pltpu.SemaphoreType.DMA((n, 4)), pltpu.SemaphoreType.DMA((n, 4)),
                pltpu.SemaphoreType.DMA((n, 3)), pltpu.SemaphoreType.DMA((n, 3)), pltpu.SemaphoreType.DMA((n,))]

    def start(self, ins, outs, sems):
        for cp in _own_block_copies(ins, outs, sems[4]) + _spread_copies(ins, outs, sems[0], sems[1])[0]:
            cp.start()

    def turn(self, ins, outs, sems):
        for cp in _spread_copies(ins, outs, sems[0], sems[1])[1]:
            cp.wait_recv()
        for cp in _forward_copies(outs, outs, sems[2], sems[3])[0]:
            cp.start()

    def finish(self, ins, outs, sems):
        handed_on, arriving = _forward_copies(outs, outs, sems[2], sems[3])
        for cp in arriving:
            cp.wait_recv()
        for cp in _spread_copies(ins, outs, sems[0], sems[1])[0] + handed_on:
            cp.wait_send()
        for cp in _own_block_copies(ins, outs, sems[4]):
            cp.wait()


def swap_job(gs):
    def copies(ins, outs, send, recv):
        x, y, c, _ = _place()
        started, awaited = [], []
        for i, (g, r1) in enumerate(zip(ins, outs)):
            for q in range(N_CHIP):
                started.append(_remote(g.at[2 * q + (1 - c)], r1.at[q], send, recv, (i, q), (x, y, 1 - c)))
                awaited.append(_remote(g.at[2 * q + c], r1.at[q], send, recv, (i, q), (x, y, 1 - c)))
        return started, awaited

    outs = [jax.ShapeDtypeStruct((N_CHIP, *g.shape[1:]), g.dtype) for g in gs]
    return _Job(gs, outs, (len(gs), N_CHIP), copies)


def chip_exchange_job(ps, rows=None, into=None):
    n = len(ps)

    def copies(ins, outs, send, recv):
        x, y, c, chips = _place()
        started, awaited = [], []
        for i, (p, r2) in enumerate(zip(ins[:n], outs)):
            for k, chip in enumerate(chips):
                src, mine, dst = p.at[2 * chip[0] + chip[1]], p.at[2 * x + y], r2.at[k]
                if rows is not None:
                    src, mine, dst = (t.at[pl.ds(rows[0], rows[1])] for t in (src, mine, dst))
                started.append(_remote(src, dst, send, recv, (i, k), (*chip, c)))
                awaited.append(_remote(mine, dst, send, recv, (i, k), (*chip, c)))
        return started, awaited

    outs = [jax.ShapeDtypeStruct((3, *p.shape[1:]), p.dtype) for p in ps]
    if into is None:
        return _Job(ps, outs, (n, 3), copies)
    return _Job(list(ps) + list(into), outs, (n, 3), copies, aliases={n + i: i for i in range(n)})


def _call(body, job, *, name, grid, in_specs, out_specs, out_shape, args, scratch_shapes=(), vmem=VMEM_LIMIT):
    if job is None:
        res = pl.pallas_call(
            body, name=name, grid=grid, in_specs=in_specs, out_specs=out_specs, out_shape=out_shape,
            scratch_shapes=list(scratch_shapes), compiler_params=_params(("arbitrary",) * len(grid), vmem),
        )(*args)
        return res, []
    n_in, n_out, n_scr = len(in_specs), len(out_specs), len(scratch_shapes)
    j_in, j_out = len(job.ins), len(job.outs)

    def with_copies(*refs):
        at = 0
        ins = refs[at:at + n_in]; at += n_in
        jins = refs[at:at + j_in]; at += j_in
        outs = refs[at:at + n_out]; at += n_out
        jouts = refs[at:at + j_out]; at += j_out
        scr = refs[at:at + n_scr]; at += n_scr
        sems = refs[at:]
        ids = [pl.program_id(d) for d in range(len(grid))]
        first = functools.reduce(jnp.logical_and, [i == 0 for i in ids])
        last = functools.reduce(jnp.logical_and, [i == n - 1 for i, n in zip(ids, grid)])

        @pl.when(first)
        def _():
            job.start(jins, jouts, sems)

        if getattr(job, "two_phase", False):
            steps, at = 1, 0
            for i, n in zip(ids, grid):
                steps, at = steps * n, at * n + i

            @pl.when(at == (TURN_EIGHTHS * steps) // 8)
            def _():
                job.turn(jins, jouts, sems)

        body(*ins, *outs, *scr)

        @pl.when(last)
        def _():
            job.finish(jins, jouts, sems)

    any_spec = pl.BlockSpec(memory_space=pl.ANY)
    res = pl.pallas_call(
        with_copies, name=name, grid=grid,
        in_specs=list(in_specs) + [any_spec] * j_in, out_specs=list(out_specs) + [any_spec] * j_out,
        out_shape=list(out_shape) + list(job.outs),
        input_output_aliases={n_in + i: n_out + o for i, o in job.aliases.items()},
        scratch_shapes=list(scratch_shapes) + job.scratch(),
        compiler_params=_params(("arbitrary",) * len(grid), vmem),
    )(*args, *job.ins)
    return res[:n_out], res[n_out:]


def run_job(job, name):
    def body(*refs):
        j_in, j_out = len(job.ins), len(job.outs)
        ins, outs, sems = refs[:j_in], refs[j_in:j_in + j_out], refs[j_in + j_out:]
        job.start(ins, outs, sems)
        job.finish(ins, outs, sems)

    any_spec = pl.BlockSpec(memory_space=pl.ANY)
    return pl.pallas_call(
        body, name=name, in_specs=[any_spec] * len(job.ins), out_specs=[any_spec] * len(job.outs),
        out_shape=list(job.outs), input_output_aliases=dict(job.aliases), scratch_shapes=job.scratch(),
    )(*job.ins)


def pair_sum(core, g, r1, name):
    _, rows, cols = g.shape
    rb = next(cand for cand in range(min(rows, 512), 0, -16) if rows % cand == 0)

    def body(core_ref, g_ref, r1_ref, p_ref, own_ref):
        del core_ref
        x, y, _, _ = _place()
        s = g_ref[0].astype(F32) + r1_ref[0].astype(F32)
        p_ref[0] = s.astype(BF16)

        @pl.when(pl.program_id(1) == 2 * x + y)
        def _():
            own_ref[...] = s

    chunk = (1, rb, cols)
    return pl.pallas_call(
        body, name=name,
        grid_spec=pltpu.PrefetchScalarGridSpec(
            num_scalar_prefetch=1, grid=(rows // rb, N_CHIP),
            in_specs=[pl.BlockSpec(chunk, lambda i, q, core_ref: (2 * q + core_ref[0], i, 0)),
                      pl.BlockSpec(chunk, lambda i, q, core_ref: (q, i, 0))],
            out_specs=[pl.BlockSpec(chunk, lambda i, q, core_ref: (q, i, 0)),
                       pl.BlockSpec((rb, cols), lambda i, q, core_ref: (i, 0))]),
        out_shape=[jax.ShapeDtypeStruct((N_CHIP, rows, cols), BF16), jax.ShapeDtypeStruct((rows, cols), F32)],
        compiler_params=_params(("arbitrary", "arbitrary")),
    )(core, g, r1)


def sum_devices(a, name, row_widths=None):
    def body(a_ref, *o_refs):
        acc = a_ref[0]
        for d in range(1, N_DEV):
            acc = acc + a_ref[d]
        if row_widths is None:
            o_refs[0][...] = acc
            return
        for r, (o_ref, width) in enumerate(zip(o_refs, row_widths)):
            o_ref[...] = acc[r:r + 1, 0:width]

    if row_widths is None:
        return pl.pallas_call(body, name=name, out_shape=jax.ShapeDtypeStruct(a.shape[1:], F32))(a)
    assert len(row_widths) == a.shape[1]
    return pl.pallas_call(body, name=name, out_shape=[jax.ShapeDtypeStruct((1, width), F32) for width in row_widths])(a)


def _adam_update(w, g, m, v):
    nm = ADAM_B1 * m + (1.0 - ADAM_B1) * g
    nv = ADAM_B2 * v + (1.0 - ADAM_B2) * (g * g)
    m_hat = nm / (1.0 - ADAM_B1 ** ADAM_STEP)
    v_hat = nv / (1.0 - ADAM_B2 ** ADAM_STEP)
    return -ADAM_LR * (m_hat / (jnp.sqrt(v_hat) + ADAM_EPS) + ADAM_WD * w), nm, nv


def adamw(w, g, m, v, name, others=None):
    rows, cols = w.shape
    g_parts = list(g) if isinstance(g, (list, tuple)) else [g]
    other_parts = [] if others is None else list(others) if isinstance(others, (list, tuple)) else [others]
    n_parts = len(g_parts)
    assert others is not None or n_parts == 1
    assert others is None or len(other_parts) == n_parts
    assert sum(p.shape[1] for p in g_parts) == cols
    rb = rows
    for cand in range(min(rows, 512), 7, -8):
        if rows % cand == 0 and cand % 8 == 0:
            rb = cand
            break

    def body(*refs):
        w_ref, m_ref, v_ref = refs[:3]
        g_refs = refs[3:3 + n_parts]
        if others is None:
            d_ref, nm_ref, nv_ref = refs[3 + n_parts:]
            gg = g_refs[0][...]
        else:
            r2_refs = refs[3 + n_parts:3 + 2 * n_parts]
            go_ref, d_ref, nm_ref, nv_ref = refs[3 + 2 * n_parts:]
            blocks = []
            for g_ref, r2_ref in zip(g_refs, r2_refs):
                part = g_ref[...]
                for k in range(3):
                    part = part + r2_ref[k].astype(F32)
                blocks.append(part)
            gg = blocks[0] if n_parts == 1 else jnp.concatenate(blocks, axis=1)
            go_ref[...] = gg
        d_ref[...], nm_ref[...], nv_ref[...] = _adam_update(w_ref[...], gg, m_ref[...], v_ref[...])

    spec = pl.BlockSpec((rb, cols), lambda i: (i, 0))
    out = jax.ShapeDtypeStruct((rows, cols), F32)
    in_specs, args = [spec] * 3, [w, m, v]
    in_specs += [pl.BlockSpec((rb, p.shape[1]), lambda i: (i, 0)) for p in g_parts]
    args += g_parts
    in_specs += [pl.BlockSpec((3, rb, p.shape[2]), lambda i: (0, i, 0)) for p in other_parts]
    args += other_parts
    n_out = 3 if others is None else 4
    res = pl.pallas_call(
        body, name=name, grid=(rows // rb,), in_specs=in_specs, out_specs=[spec] * n_out,
        out_shape=[out] * n_out, compiler_params=_params(("parallel",)),
    )(*args)
    return (g, *res) if others is None else tuple(res)


def adamw_small(items, name):
    n = len(items)

    def body(*refs):
        ins, outs = refs[:4 * n], refs[4 * n:]
        for i in range(n):
            w_ref, g_ref, m_ref, v_ref = ins[4 * i:4 * i + 4]
            d_ref, nm_ref, nv_ref = outs[3 * i:3 * i + 3]
            d_ref[...], nm_ref[...], nv_ref[...] = _adam_update(w_ref[...], g_ref[...], m_ref[...], v_ref[...])

    res = pl.pallas_call(
        body, name=name,
        out_shape=[jax.ShapeDtypeStruct(w.shape, F32) for w, _, _, _ in items for _ in range(3)],
    )(*[t for item in items for t in item])
    return [tuple(res[3 * i:3 * i + 3]) for i in range(n)]


def ada_fwd(c_all, w_cols, b_cols, name):
    def body(c_ref, w_ref, b_ref, cond_ref, mod_ref):
        cc = c_ref[...]
        cond = (cc * _sigmoid(cc)).astype(BF16)
        cond_ref[...] = cond
        mod_ref[...] = _dot(cond, w_ref[...].astype(BF16)) + b_ref[...]

    n, cols = c_all.shape[0], w_cols.shape[1]
    return pl.pallas_call(
        body, name=name,
        out_shape=[jax.ShapeDtypeStruct(c_all.shape, BF16), jax.ShapeDtypeStruct((n, cols), F32)],
        compiler_params=_params(),
    )(c_all, w_cols, b_cols)


def ada_bwd(cond_all, dmod_cols, name):
    def body(c_ref, d_ref, gw_ref, gb_ref):
        d = d_ref[...]
        gw_ref[...] = _dot_tn(c_ref[...], d.astype(BF16))
        gb_ref[...] = jnp.sum(d, axis=0, keepdims=True)

    dm, cols = cond_all.shape[1], dmod_cols.shape[1]
    return pl.pallas_call(
        body, name=name,
        out_shape=[jax.ShapeDtypeStruct((dm, cols), F32), jax.ShapeDtypeStruct((1, cols), F32)],
        compiler_params=_params(),
    )(cond_all, dmod_cols)


MXU_COLS = 256
FFN_CHUNK = 4 * MXU_COLS


def _hidden_chunks(ff):
    assert ff % MXU_COLS == 0
    return [(at, min(FFN_CHUNK, ff - at)) for at in range(0, ff, FFN_CHUNK)]


def _mod_spec(tiles_per_seq, dm):
    return pl.BlockSpec((1, 1, dm), lambda i: (i // tiles_per_seq, 0, 0))


def ffn_loss(x, sh, sc, gt, wgu, wd, ln_g, ln_b, target, seq, name):
    tokens, dm = x.shape
    ff = wgu.shape[1]
    chunks = _hidden_chunks(ff)
    tm = min(FFN_WIDE_TILE, seq)
    tiles_per_seq = seq // tm

    def body(x_ref, sh_ref, sc_ref, gt_ref, wgu_ref, wd_ref, lg_ref, lb_ref, t_ref,
             dr_ref, df_ref, gu_ref, a_ref, h_ref, loss_ref, dln_ref, dgt_ref):
        i = pl.program_id(0)
        xx = x_ref[...]
        h = (xx * (1.0 + sc_ref[0]) + sh_ref[0]).astype(BF16)
        h_ref[...] = h
        acc = jnp.zeros((tm, dm), F32)
        for at, wdt in chunks:
            gk = _dot_nt(h, wgu_ref[0, at:at + wdt, :])
            uk = _dot_nt(h, wgu_ref[1, at:at + wdt, :])
            gu_ref[0, :, at:at + wdt] = gk.astype(BF16)
            gu_ref[1, :, at:at + wdt] = uk.astype(BF16)
            a = (gk * _sigmoid(gk) * uk).astype(BF16)
            a_ref[:, at:at + wdt] = a
            acc = acc + _dot(a, wd_ref[at:at + wdt, :])
        half_gate = 0.5 * (1.0 + gt_ref[0])
        xhat, rstd = _ln_stats(DN_ALPHA * xx + half_gate * acc)
        err = xhat * lg_ref[...] + lb_ref[...] - t_ref[...]
        dr, dgain, dbias = _ln_bwd_normalized(err * (1.0 / dm), xhat, rstd, lg_ref[...])
        dr_ref[...] = dr
        df_ref[...] = (half_gate * dr).astype(BF16)

        @pl.when(i == 0)
        def _():
            loss_ref[...] = jnp.zeros_like(loss_ref)
            dln_ref[...] = jnp.zeros_like(dln_ref)

        @pl.when(i % tiles_per_seq == 0)
        def _():
            dgt_ref[...] = jnp.zeros_like(dgt_ref)

        loss_ref[...] += jnp.full((1, 128), (0.5 / dm) * jnp.sum(err * err), F32)
        dln_ref[0:1, :] += dgain
        dln_ref[1:2, :] += dbias
        dgt_ref[0] += jnp.sum(dr * (0.5 * acc), axis=0, keepdims=True)

    tile = pl.BlockSpec((tm, dm), lambda i: (i, 0))
    mod = _mod_spec(tiles_per_seq, dm)
    res, _ = _call(
        body, None, name=name, grid=(tokens // tm,),
        in_specs=[tile, mod, mod, mod, _const_spec(wgu.shape), _const_spec(wd.shape),
                  _const_spec((1, dm)), _const_spec((1, dm)), tile],
        out_specs=[tile, tile, pl.BlockSpec((2, tm, ff), lambda i: (0, i, 0)), pl.BlockSpec((tm, ff), lambda i: (i, 0)),
                   tile, pl.BlockSpec((1, 128), lambda i: (0, 0)), pl.BlockSpec((2, dm), lambda i: (0, 0)), mod],
        out_shape=[jax.ShapeDtypeStruct((tokens, dm), F32), jax.ShapeDtypeStruct((tokens, dm), BF16),
                   jax.ShapeDtypeStruct((2, tokens, ff), BF16), jax.ShapeDtypeStruct((tokens, ff), BF16),
                   jax.ShapeDtypeStruct((tokens, dm), BF16), jax.ShapeDtypeStruct((1, 128), F32),
                   jax.ShapeDtypeStruct((2, dm), F32), jax.ShapeDtypeStruct((tokens // seq, 1, dm), F32)],
        args=(x, sh, sc, gt, wgu, wd, ln_g, ln_b, target), vmem=FFN_WIDE_VMEM)
    return res


def ffn_up(x, sh, sc, wgu, seq, name, job=None):
    tokens, dm = x.shape
    ff = wgu.shape[1]
    chunks = _hidden_chunks(ff)
    tm = min(FFN_FWD_TILE, seq)

    def body(x_ref, sh_ref, sc_ref, wgu_ref, gu_ref, a_ref, h_ref):
        h = (x_ref[...] * (1.0 + sc_ref[0]) + sh_ref[0]).astype(BF16)
        h_ref[...] = h
        for at, wdt in chunks:
            gk = _dot_nt(h, wgu_ref[0, at:at + wdt, :])
            uk = _dot_nt(h, wgu_ref[1, at:at + wdt, :])
            gu_ref[0, :, at:at + wdt] = gk.astype(BF16)
            gu_ref[1, :, at:at + wdt] = uk.astype(BF16)
            a_ref[:, at:at + wdt] = (gk * _sigmoid(gk) * uk).astype(BF16)

    tile = pl.BlockSpec((tm, dm), lambda i: (i, 0))
    mod = _mod_spec(seq // tm, dm)
    return _call(
        body, job, name=name, grid=(tokens // tm,),
        in_specs=[tile, mod, mod, _const_spec(wgu.shape)],
        out_specs=[pl.BlockSpec((2, tm, ff), lambda i: (0, i, 0)), pl.BlockSpec((tm, ff), lambda i: (i, 0)), tile],
        out_shape=[jax.ShapeDtypeStruct((2, tokens, ff), BF16), jax.ShapeDtypeStruct((tokens, ff), BF16),
                   jax.ShapeDtypeStruct((tokens, dm), BF16)],
        args=(x, sh, sc, wgu))


def ffn_down(x, a, gt, wd, ln_g, ln_b, seq, name, job=None):
    tokens, dm = x.shape
    ff = wd.shape[0]
    chunks = _hidden_chunks(ff)
    tm = min(FFN_FWD_TILE, seq)

    def body(x_ref, a_ref, gt_ref, wd_ref, lg_ref, lb_ref, xo_ref, r_ref, f_ref):
        acc = jnp.zeros((tm, dm), F32)
        for at, wdt in chunks:
            acc = acc + _dot(a_ref[:, at:at + wdt], wd_ref[at:at + wdt, :])
        f_ref[...] = acc.astype(BF16)
        r = DN_ALPHA * x_ref[...] + (0.5 * (1.0 + gt_ref[0])) * acc
        r_ref[...] = r
        xhat, _ = _ln_stats(r)
        xo_ref[...] = xhat * lg_ref[...] + lb_ref[...]

    tile = pl.BlockSpec((tm, dm), lambda i: (i, 0))
    return _call(
        body, job, name=name, grid=(tokens // tm,),
        in_specs=[tile, pl.BlockSpec((tm, ff), lambda i: (i, 0)), _mod_spec(seq // tm, dm), _const_spec(wd.shape),
                  _const_spec((1, dm)), _const_spec((1, dm))],
        out_specs=[tile, tile, tile],
        out_shape=[jax.ShapeDtypeStruct((tokens, dm), F32), jax.ShapeDtypeStruct((tokens, dm), F32),
                   jax.ShapeDtypeStruct((tokens, dm), BF16)],
        args=(x, a, gt, wd, ln_g, ln_b))


def ffn_bwd(dr, df, x, gu, sc, wgu, wd, seq, name, job=None):
    tokens, dm = x.shape
    ff = wgu.shape[1]
    chunks = _hidden_chunks(ff)
    tm = min(FFN_WIDE_TILE, seq)
    tiles_per_seq = seq // tm
    nseq = tokens // seq

    def body(dr_ref, df_ref, x_ref, gu_ref, sc_ref, wgu_ref, wd_ref, dx_ref, dgu_ref, dmod_ref):
        @pl.when(pl.program_id(0) % tiles_per_seq == 0)
        def _():
            dmod_ref[...] = jnp.zeros_like(dmod_ref)

        df = df_ref[...]
        dh = jnp.zeros((tm, dm), F32)
        for at, wdt in chunks:
            cols = slice(at, at + wdt)
            da = _dot_nt(df, wd_ref[cols, :])
            gk = gu_ref[0, :, cols].astype(F32)
            uk = gu_ref[1, :, cols].astype(F32)
            sg = _sigmoid(gk)
            sil = gk * sg
            du = (da * sil).astype(BF16)
            dg = (da * uk * (sg * (1.0 + gk * (1.0 - sg)))).astype(BF16)
            dgu_ref[0, :, cols] = dg
            dgu_ref[1, :, cols] = du
            dh = dh + _dot(dg, wgu_ref[0, cols, :]) + _dot(du, wgu_ref[1, cols, :])
        dx_ref[...] = DN_ALPHA * dr_ref[...] + dh * (1.0 + sc_ref[0])
        dmod_ref[0, 0:1, :] += jnp.sum(dh, axis=0, keepdims=True)
        dmod_ref[0, 1:2, :] += jnp.sum(dh * x_ref[...], axis=0, keepdims=True)

    tile = pl.BlockSpec((tm, dm), lambda i: (i, 0))
    gu_spec = pl.BlockSpec((2, tm, ff), lambda i: (0, i, 0))
    return _call(
        body, job, name=name, grid=(tokens // tm,),
        in_specs=[tile, tile, tile, gu_spec, _mod_spec(tiles_per_seq, dm), _const_spec(wgu.shape), _const_spec(wd.shape)],
        out_specs=[tile, gu_spec, pl.BlockSpec((1, 2, dm), lambda i: (i // tiles_per_seq, 0, 0))],
        out_shape=[jax.ShapeDtypeStruct((tokens, dm), F32), jax.ShapeDtypeStruct((2, tokens, ff), BF16),
                   jax.ShapeDtypeStruct((nseq, 2, dm), F32)],
        args=(dr, df, x, gu, sc, wgu, wd), vmem=FFN_WIDE_VMEM)


def tn_matmul(a, b, name, job=None, b_cols=None, a_width=None):
    na, tokens, k_all = a.shape
    kk = k_all if a_width is None else a_width
    nka = k_all // kk
    assert nka * kk == k_all
    nb, _, cc = b.shape
    col = 0
    if b_cols is not None:
        col, cc = b_cols
    tt = tokens
    while 4 * tt * (kk + cc) + 8 * kk * cc > TN_VMEM_BUDGET and tt % 2 == 0 and tt > 256:
        tt //= 2
    steps = tokens // tt

    def body(a_ref, b_ref, o_ref, *acc):
        if steps == 1:
            o_ref[0, 0, 0] = _dot_tn(a_ref[0], b_ref[0]).astype(BF16)
            return
        acc_ref, = acc
        t = pl.program_id(3)

        @pl.when(t == 0)
        def _():
            acc_ref[...] = jnp.zeros_like(acc_ref)

        acc_ref[...] += _dot_tn(a_ref[0], b_ref[0])

        @pl.when(t == steps - 1)
        def _():
            o_ref[0, 0, 0] = acc_ref[...].astype(BF16)

    return _call(
        body, job, name=name, grid=(na, nka, nb, steps),
        in_specs=[pl.BlockSpec((1, tt, kk), lambda i, s, j, t: (i, t, s)),
                  pl.BlockSpec((1, tt, cc), lambda i, s, j, t: (j, t, col))],
        out_specs=[pl.BlockSpec((1, 1, 1, kk, cc), lambda i, s, j, t: (i, s, j, 0, 0))],
        out_shape=[jax.ShapeDtypeStruct((na, nka, nb, kk, cc), BF16)],
        scratch_shapes=[] if steps == 1 else [pltpu.VMEM((kk, cc), F32)], args=(a, b))


def proj_fwd(x1, sh, sc, w_in, seq, name, job=None):
    tokens, dm = x1.shape
    tm = min(MIX_TILE, seq)
    tiles_per_seq = seq // tm
    widths = [N_Q_HEADS * HEAD_DIM, N_KV_HEADS * HEAD_DIM, N_KV_HEADS * HEAD_DIM, 512, 512, 512]
    assert sum(widths) == w_in.shape[0]

    def body(x_ref, sh_ref, sc_ref, w_ref, *outs):
        h = (x_ref[...] * (1.0 + sc_ref[0]) + sh_ref[0]).astype(BF16)
        proj = _dot_nt(h, w_ref[...])
        at = 0
        for o_ref, wdt in zip(outs, widths):
            o_ref[...] = proj[:, at:at + wdt].astype(o_ref.dtype)
            at += wdt

    tile = pl.BlockSpec((tm, dm), lambda i: (i, 0))
    mod = _mod_spec(tiles_per_seq, dm)
    return _call(
        body, job, name=name, grid=(tokens // tm,),
        in_specs=[tile, mod, mod, _const_spec(w_in.shape)],
        out_specs=[pl.BlockSpec((tm, wdt), lambda i: (i, 0)) for wdt in widths],
        out_shape=[jax.ShapeDtypeStruct((tokens, wdt), F32 if i < 3 else BF16) for i, wdt in enumerate(widths)],
        args=(x1, sh, sc, w_in))


LANES = 2 * HEAD_DIM


def _head_lane(shape):
    return lax.broadcasted_iota(jnp.int32, shape, 1) % HEAD_DIM


def _lane_half(shape):
    return lax.broadcasted_iota(jnp.int32, shape, 1) // HEAD_DIM


def _swap_rot(v):
    lane = _head_lane(v.shape)
    half = ROT_DIM // 2
    return jnp.where(lane < half, pltpu.roll(v, LANES - half, 1),
                     jnp.where(lane < ROT_DIM, pltpu.roll(v, half, 1), 0.0))


def _rope(v, cos_t, sin_t):
    return v * cos_t + _swap_rot(v) * sin_t


def _unrope(dv, cos_t, sin_t):
    return dv * cos_t + _swap_rot(dv * sin_t)


def _both_halves(t, g):
    return jnp.where(_lane_half(t.shape) == g, t, pltpu.roll(t, HEAD_DIM, 1))


def _fold_halves(t, g):
    return jnp.where(_lane_half(t.shape) == g, t + pltpu.roll(t, HEAD_DIM, 1), 0.0)


def _stack_heads(blocks):
    rows = []
    for blk in blocks:
        half = _lane_half(blk.shape)
        rows += [jnp.where(half == 0, blk, 0.0), jnp.where(half == 1, blk, 0.0)]
    return jnp.concatenate(rows, axis=0)


def _unstack_heads(t, j):
    lo = t[(2 * j) * ATTN_BLOCK:(2 * j + 1) * ATTN_BLOCK]
    hi = t[(2 * j + 1) * ATTN_BLOCK:(2 * j + 2) * ATTN_BLOCK]
    return jnp.where(_lane_half(lo.shape) == 0, lo, hi)


def _band_mask(q0, w0):
    rows, cols = GQA_GROUP * ATTN_BLOCK, 2 * ATTN_BLOCK
    qi = lax.broadcasted_iota(jnp.int32, (rows, cols), 0) % ATTN_BLOCK + q0
    ki = lax.broadcasted_iota(jnp.int32, (rows, cols), 1) + w0
    diff = qi - ki
    return (diff >= 0) & (diff < ATTN_BLOCK)


def _attn_specs(seq):
    q_spec = pl.BlockSpec((seq, GQA_GROUP * HEAD_DIM), lambda b, g: (b, g))
    kv_spec = pl.BlockSpec((seq, LANES), lambda b, g: (b, 0))
    sink_spec = pl.BlockSpec((1, GQA_GROUP * ATTN_BLOCK, 1), lambda b, g: (g, 0, 0))
    return q_spec, kv_spec, sink_spec


def _block_starts(n):
    q0 = pl.multiple_of(n * ATTN_BLOCK, ATTN_BLOCK)
    w0 = pl.multiple_of(jnp.maximum(n - 1, 0) * ATTN_BLOCK, ATTN_BLOCK)
    return q0, w0


def _stacked_queries(ref, rows):
    return _stack_heads([ref[rows, j * LANES:(j + 1) * LANES] for j in range(2)]).astype(BF16)


def _sink_columns(sinks):
    return jnp.repeat(sinks.reshape(N_KV_HEADS, GQA_GROUP), ATTN_BLOCK, axis=1)[:, :, None]


def _probs_spec(nblk):
    return pl.BlockSpec((1, 1, nblk, GQA_GROUP * ATTN_BLOCK, 2 * ATTN_BLOCK), lambda b, g: (b, g, 0, 0, 0))


def _sink_probs_spec():
    return pl.BlockSpec((1, 1, GQA_GROUP * ATTN_BLOCK, LANES), lambda b, g: (b, g, 0, 0))


def attn_fwd(q, k, v, cos_t, sin_t, sinks, seq, name, job=None):
    tokens = q.shape[0]
    nblk = seq // ATTN_BLOCK
    assert nblk >= 2
    scale = HEAD_DIM ** -0.5

    nseq = tokens // seq
    rows_stacked = GQA_GROUP * ATTN_BLOCK
    assert nblk <= LANES

    def body(q_ref, k_ref, v_ref, cos_ref, sin_ref, sink_ref, o_ref, qr_ref, p_ref, ps_ref, kd_ref, vd_ref):
        g = pl.program_id(1)
        kd_ref[...] = _both_halves(_rope(k_ref[...].astype(F32), cos_ref[...], sin_ref[...]), g).astype(BF16)
        vd_ref[...] = _both_halves(v_ref[...].astype(F32), g).astype(BF16)
        sink = sink_ref[0]
        lane = lax.broadcasted_iota(jnp.int32, (rows_stacked, LANES), 1)

        ps_ref[...] = jnp.zeros_like(ps_ref)

        def block(n, carry):
            q0, w0 = _block_starts(n)
            rows, win = pl.ds(q0, ATTN_BLOCK), pl.ds(w0, 2 * ATTN_BLOCK)
            blocks = []
            for j in range(2):
                qr = _rope(q_ref[rows, j * LANES:(j + 1) * LANES].astype(F32), cos_ref[rows, :], sin_ref[rows, :]).astype(BF16)
                qr_ref[rows, j * LANES:(j + 1) * LANES] = qr
                blocks.append(qr)
            qs = _stack_heads(blocks)
            s = _dot_nt(qs, kd_ref[win, :]) * scale
            s = jnp.where(_band_mask(q0, w0), s, NEG_BIG)
            m = jnp.maximum(jnp.max(s, axis=-1, keepdims=True), sink)
            p = jnp.exp(s - m)
            e_sink = jnp.exp(sink - m)
            inv = pl.reciprocal(jnp.sum(p, axis=-1, keepdims=True) + e_sink, approx=True)
            pn = (p * inv).astype(BF16)
            p_ref[0, 0, n] = pn
            out = _dot(pn, vd_ref[win, :])
            for j in range(2):
                o_ref[rows, j * LANES:(j + 1) * LANES] = _unstack_heads(out, j).astype(o_ref.dtype)
            ps_ref[0, 0] = jnp.where(lane == n, e_sink * inv, ps_ref[0, 0])
            return carry

        lax.fori_loop(0, nblk, block, 0, unroll=2)

    q_spec, kv_spec, sink_spec = _attn_specs(seq)
    return _call(
        body, job, name=name, grid=(nseq, N_KV_HEADS),
        in_specs=[q_spec, kv_spec, kv_spec, kv_spec, kv_spec, sink_spec],
        out_specs=[q_spec, q_spec, _probs_spec(nblk), _sink_probs_spec()],
        out_shape=[jax.ShapeDtypeStruct(q.shape, BF16), jax.ShapeDtypeStruct(q.shape, BF16),
                   jax.ShapeDtypeStruct((nseq, N_KV_HEADS, nblk, rows_stacked, 2 * ATTN_BLOCK), BF16),
                   jax.ShapeDtypeStruct((nseq, N_KV_HEADS, rows_stacked, LANES), F32)],
        scratch_shapes=[pltpu.VMEM((seq, LANES), BF16), pltpu.VMEM((seq, LANES), BF16)],
        args=(q, k, v, cos_t, sin_t, _sink_columns(sinks)))


def attn_bwd(qr, k, v, do, probs, sink_probs, cos_t, sin_t, seq, name, job=None):
    tokens = qr.shape[0]
    nseq = tokens // seq
    nblk = seq // ATTN_BLOCK
    assert nblk >= 2
    rows_stacked = GQA_GROUP * ATTN_BLOCK
    scale = HEAD_DIM ** -0.5

    def body(q_ref, k_ref, v_ref, do_ref, p_ref, ps_ref, cos_ref, sin_ref, dq_ref, dk_ref, dv_ref, ds_ref,
             kd_ref, vd_ref, dkd_ref, dvd_ref, acc_ref):
        g = pl.program_id(1)
        kd_ref[...] = _both_halves(_rope(k_ref[...].astype(F32), cos_ref[...], sin_ref[...]), g).astype(BF16)
        vd_ref[...] = _both_halves(v_ref[...].astype(F32), g).astype(BF16)
        dkd_ref[...] = jnp.zeros_like(dkd_ref)
        dvd_ref[...] = jnp.zeros_like(dvd_ref)
        acc_ref[...] = jnp.zeros_like(acc_ref)
        lane = lax.broadcasted_iota(jnp.int32, (rows_stacked, LANES), 1)

        def block(n, carry):
            q0, w0 = _block_starts(n)
            rows, win = pl.ds(q0, ATTN_BLOCK), pl.ds(w0, 2 * ATTN_BLOCK)
            qs = _stacked_queries(q_ref, rows)
            dos = _stacked_queries(do_ref, rows)
            kw, vw = kd_ref[win, :], vd_ref[win, :]
            pn16 = p_ref[0, 0, n]
            pn = pn16.astype(F32)
            dvd_ref[win, :] += _dot_tn(pn16, dos)
            dp = _dot_nt(dos, vw)
            delta = jnp.sum(dp * pn, axis=-1, keepdims=True)
            ds = (pn * (dp - delta)).astype(BF16)
            dqs = _dot(ds, kw) * scale
            dkd_ref[win, :] += _dot_tn(ds, qs) * scale
            cos_b, sin_b = cos_ref[rows, :], sin_ref[rows, :]
            for j in range(2):
                dq_ref[rows, j * LANES:(j + 1) * LANES] = _unrope(_unstack_heads(dqs, j), cos_b, sin_b).astype(BF16)
            acc_ref[...] += jnp.where(lane == n, ps_ref[0, 0] * delta, 0.0)
            return carry

        lax.fori_loop(0, nblk // 2, lambda i, carry: block(2 * i + 1, block(2 * i, carry)), 0)
        ds_ref[0, 0] = -jnp.sum(acc_ref[...], axis=-1, keepdims=True)
        dk_g = _unrope(_fold_halves(dkd_ref[...], g), cos_ref[...], sin_ref[...])
        dv_g = _fold_halves(dvd_ref[...], g)

        @pl.when(g == 0)
        def _():
            dk_ref[...] = dk_g
            dv_ref[...] = dv_g

        @pl.when(g != 0)
        def _():
            dk_ref[...] += dk_g
            dv_ref[...] += dv_g

    q_spec, kv_spec, _ = _attn_specs(seq)
    return _call(
        body, job, name=name, grid=(nseq, N_KV_HEADS),
        in_specs=[q_spec, kv_spec, kv_spec, q_spec, _probs_spec(nblk), _sink_probs_spec(), kv_spec, kv_spec],
        out_specs=[q_spec, kv_spec, kv_spec, pl.BlockSpec((1, 1, rows_stacked, 1), lambda b, g: (b, g, 0, 0))],
        out_shape=[jax.ShapeDtypeStruct(qr.shape, BF16), jax.ShapeDtypeStruct(k.shape, F32),
                   jax.ShapeDtypeStruct(k.shape, F32), jax.ShapeDtypeStruct((nseq, N_KV_HEADS, rows_stacked, 1), F32)],
        scratch_shapes=[pltpu.VMEM((seq, LANES), BF16), pltpu.VMEM((seq, LANES), BF16),
                        pltpu.VMEM((seq, LANES), F32), pltpu.VMEM((seq, LANES), F32),
                        pltpu.VMEM((rows_stacked, LANES), F32)],
        args=(qr, k, v, do, probs, sink_probs, cos_t, sin_t))


CONV_COLS = 128


def _shift_down(z, by):
    t = lax.broadcasted_iota(jnp.int32, z.shape, 0)
    return jnp.where(t >= by, pltpu.roll(z, by, 0), 0.0)


def _shift_up(z, by):
    n = z.shape[0]
    t = lax.broadcasted_iota(jnp.int32, z.shape, 0)
    return jnp.where(t < n - by, pltpu.roll(z, n - by, 0), 0.0)


def conv_fwd(u, bg, cg, conv_w, seq, name):
    tokens, width = u.shape

    def body(u_ref, bg_ref, cg_ref, w_ref, o_ref):
        z = cg_ref[...].astype(F32) * u_ref[...].astype(F32)
        yy = w_ref[2:3, :] * z + w_ref[1:2, :] * _shift_down(z, 1) + w_ref[0:1, :] * _shift_down(z, 2)
        o_ref[...] = (bg_ref[...].astype(F32) * yy).astype(BF16)

    col = pl.BlockSpec((seq, CONV_COLS), lambda j, b: (b, j))
    return pl.pallas_call(
        body, name=name, grid=(width // CONV_COLS, tokens // seq),
        in_specs=[col, col, col, pl.BlockSpec((CONV_TAPS, CONV_COLS), lambda j, b: (0, j))],
        out_specs=col, out_shape=jax.ShapeDtypeStruct((tokens, width), BF16),
        compiler_params=_params(("parallel", "parallel")),
    )(u, bg, cg, conv_w)


def conv_bwd(dout, u, bg, cg, conv_w, seq, name):
    tokens, width = u.shape

    def body(do_ref, u_ref, bg_ref, cg_ref, w_ref, du_ref, dbg_ref, dcg_ref, dw_ref):
        uu, cg_v, do = u_ref[...].astype(F32), cg_ref[...].astype(F32), do_ref[...].astype(F32)
        z = cg_v * uu
        z1, z2 = _shift_down(z, 1), _shift_down(z, 2)
        yy = w_ref[2:3, :] * z + w_ref[1:2, :] * z1 + w_ref[0:1, :] * z2
        dbg_ref[...] = (do * yy).astype(BF16)
        dyy = do * bg_ref[...].astype(F32)
        dz = w_ref[2:3, :] * dyy + w_ref[1:2, :] * _shift_up(dyy, 1) + w_ref[0:1, :] * _shift_up(dyy, 2)
        du_ref[...] = (dz * cg_v).astype(BF16)
        dcg_ref[...] = (dz * uu).astype(BF16)

        @pl.when(pl.program_id(1) == 0)
        def _():
            dw_ref[...] = jnp.zeros_like(dw_ref)

        dw_ref[0:1, :] += jnp.sum(dyy * z2, axis=0, keepdims=True)
        dw_ref[1:2, :] += jnp.sum(dyy * z1, axis=0, keepdims=True)
        dw_ref[2:3, :] += jnp.sum(dyy * z, axis=0, keepdims=True)

    col = pl.BlockSpec((seq, CONV_COLS), lambda j, b: (b, j))
    w_spec = pl.BlockSpec((CONV_TAPS, CONV_COLS), lambda j, b: (0, j))
    act = jax.ShapeDtypeStruct((tokens, width), BF16)
    return pl.pallas_call(
        body, name=name, grid=(width // CONV_COLS, tokens // seq),
        in_specs=[col, col, col, col, w_spec], out_specs=[col, col, col, w_spec],
        out_shape=[act, act, act, jax.ShapeDtypeStruct((CONV_TAPS, width), F32)],
        compiler_params=_params(("parallel", "arbitrary")),
    )(dout, u, bg, cg, conv_w)


def out_fwd(x1, attn, conv, gt, w_out, ln_g, ln_b, seq, name, job=None):
    tokens, dm = x1.shape
    half = attn.shape[1]
    tm = min(MIX_TILE, seq)
    tiles_per_seq = seq // tm

    def body(x_ref, a_ref, c_ref, gt_ref, w_ref, lg_ref, lb_ref, xo_ref, r_ref, mi_ref, mix_ref):
        mixin = jnp.concatenate([a_ref[...], c_ref[...]], axis=1).astype(BF16)
        mi_ref[...] = mixin
        mix = _dot(mixin, w_ref[...])
        mix_ref[...] = mix.astype(BF16)
        r = DN_ALPHA * x_ref[...] + (1.0 + gt_ref[0]) * mix
        r_ref[...] = r
        xhat, _ = _ln_stats(r)
        xo_ref[...] = xhat * lg_ref[...] + lb_ref[...]

    tile = pl.BlockSpec((tm, dm), lambda i: (i, 0))
    htile = pl.BlockSpec((tm, half), lambda i: (i, 0))
    return _call(
        body, job, name=name, grid=(tokens // tm,),
        in_specs=[tile, htile, htile, _mod_spec(tiles_per_seq, dm), _const_spec(w_out.shape),
                  _const_spec((1, dm)), _const_spec((1, dm))],
        out_specs=[tile, tile, tile, tile],
        out_shape=[jax.ShapeDtypeStruct((tokens, dm), F32), jax.ShapeDtypeStruct((tokens, dm), F32),
                   jax.ShapeDtypeStruct((tokens, dm), BF16), jax.ShapeDtypeStruct((tokens, dm), BF16)],
        args=(x1, attn, conv, gt, w_out, ln_g, ln_b))


def out_bwd(dy, r, mix, gt, w_out, ln_g, seq, name, job=None):
    tokens, dm = r.shape
    half = dm // 2
    tm = min(MIX_TILE, seq)
    tiles_per_seq = seq // tm
    nseq = tokens // seq

    def body(dy_ref, r_ref, mix_ref, gt_ref, w_ref, lg_ref, dres_ref, da_ref, dc_ref, dmix_ref, dln_ref, dgt_ref):
        i = pl.program_id(0)
        dr, dgain, dbias = _ln_bwd(dy_ref[...], r_ref[...], lg_ref[...])

        @pl.when(i == 0)
        def _():
            dln_ref[...] = jnp.zeros_like(dln_ref)

        @pl.when(i % tiles_per_seq == 0)
        def _():
            dgt_ref[...] = jnp.zeros_like(dgt_ref)

        dln_ref[0:1, :] += dgain
        dln_ref[1:2, :] += dbias
        dgt_ref[0] += jnp.sum(dr * mix_ref[...].astype(F32), axis=0, keepdims=True)
        dres_ref[...] = DN_ALPHA * dr
        dmix = ((1.0 + gt_ref[0]) * dr).astype(BF16)
        dmix_ref[...] = dmix
        dmixin = _dot_nt(dmix, w_ref[...])
        da_ref[...] = dmixin[:, :half].astype(BF16)
        dc_ref[...] = dmixin[:, half:].astype(BF16)

    tile = pl.BlockSpec((tm, dm), lambda i: (i, 0))
    htile = pl.BlockSpec((tm, half), lambda i: (i, 0))
    return _call(
        body, job, name=name, grid=(tokens // tm,),
        in_specs=[tile, tile, tile, _mod_spec(tiles_per_seq, dm), _const_spec(w_out.shape), _const_spec((1, dm))],
        out_specs=[tile, htile, htile, tile, pl.BlockSpec((2, dm), lambda i: (0, 0)),
                   pl.BlockSpec((1, 1, dm), lambda i: (i // tiles_per_seq, 0, 0))],
        out_shape=[jax.ShapeDtypeStruct((tokens, dm), F32), jax.ShapeDtypeStruct((tokens, half), BF16),
                   jax.ShapeDtypeStruct((tokens, half), BF16), jax.ShapeDtypeStruct((tokens, dm), BF16),
                   jax.ShapeDtypeStruct((2, dm), F32), jax.ShapeDtypeStruct((nseq, 1, dm), F32)],
        args=(dy, r, mix, gt, w_out, ln_g))


def proj_bwd(parts, dres, x1, sh, sc, w_in, r_prev, f_prev, gt_prev, ln_g_prev, seq, name, job=None):
    tokens, dm = x1.shape
    tm = min(MIX_TILE, seq)
    tiles_per_seq = seq // tm
    nseq = tokens // seq
    widths = [p.shape[1] for p in parts]
    total = sum(widths)

    def body(*refs):
        part_refs = refs[:6]
        (dres_ref, x_ref, sh_ref, sc_ref, w_ref, r_ref, f_ref, gt_ref, lg_ref,
         dr_ref, df_ref, dproj_ref, h_ref, dmod_ref, dln_ref, dgt_ref) = refs[6:]
        i = pl.program_id(0)
        dproj = jnp.concatenate([p[...].astype(BF16) for p in part_refs], axis=1)
        dproj_ref[...] = dproj
        dh = _dot(dproj, w_ref[...])
        xx = x_ref[...]
        one_sc = 1.0 + sc_ref[0]
        h_ref[...] = (xx * one_sc + sh_ref[0]).astype(BF16)
        dr, dgain, dbias = _ln_bwd(dres_ref[...] + dh * one_sc, r_ref[...], lg_ref[...])
        dr_ref[...] = dr
        df_ref[...] = ((0.5 * (1.0 + gt_ref[0])) * dr).astype(BF16)

        @pl.when(i == 0)
        def _():
            dln_ref[...] = jnp.zeros_like(dln_ref)

        @pl.when(i % tiles_per_seq == 0)
        def _():
            dmod_ref[...] = jnp.zeros_like(dmod_ref)
            dgt_ref[...] = jnp.zeros_like(dgt_ref)

        dmod_ref[0, 0:1, :] += jnp.sum(dh, axis=0, keepdims=True)
        dmod_ref[0, 1:2, :] += jnp.sum(dh * xx, axis=0, keepdims=True)
        dln_ref[0:1, :] += dgain
        dln_ref[1:2, :] += dbias
        dgt_ref[0] += jnp.sum(dr * (0.5 * f_ref[...].astype(F32)), axis=0, keepdims=True)

    tile = pl.BlockSpec((tm, dm), lambda i: (i, 0))
    mod = _mod_spec(tiles_per_seq, dm)
    return _call(
        body, job, name=name, grid=(tokens // tm,),
        in_specs=[pl.BlockSpec((tm, wdt), lambda i: (i, 0)) for wdt in widths]
        + [tile, tile, mod, mod, _const_spec(w_in.shape), tile, tile, mod, _const_spec((1, dm))],
        out_specs=[tile, tile, pl.BlockSpec((tm, total), lambda i: (i, 0)), tile,
                   pl.BlockSpec((1, 2, dm), lambda i: (i // tiles_per_seq, 0, 0)),
                   pl.BlockSpec((2, dm), lambda i: (0, 0)), mod],
        out_shape=[jax.ShapeDtypeStruct((tokens, dm), F32), jax.ShapeDtypeStruct((tokens, dm), BF16),
                   jax.ShapeDtypeStruct((tokens, total), BF16), jax.ShapeDtypeStruct((tokens, dm), BF16),
                   jax.ShapeDtypeStruct((nseq, 2, dm), F32), jax.ShapeDtypeStruct((2, dm), F32),
                   jax.ShapeDtypeStruct((nseq, 1, dm), F32)],
        args=(*parts, dres, x1, sh, sc, w_in, r_prev, f_prev, gt_prev, ln_g_prev))


def _rope_tables(positions):
    half = ROT_DIM // 2
    inv_freq = np.power(np.float32(ROPE_THETA), -np.arange(0, ROT_DIM, 2, dtype=np.float32) / ROT_DIM)
    lane = np.arange(LANES) % HEAD_DIM
    freq = np.where(lane < ROT_DIM, inv_freq[lane % half], 0.0).astype(np.float32)
    sign = np.where(lane < half, -1.0, 1.0).astype(np.float32)
    ang = positions.astype(F32)[:, None] * freq[None, :]
    return jnp.cos(ang), sign[None, :] * jnp.sin(ang)


def kernel(x, c, positions, w_ada, b_ada, ffn1_w_gate_up, ffn1_w_down, ln1_g, ln1_b, w_in, conv_w, attn_sinks, w_out, ln2_g, ln2_b, ffn2_w_gate_up, ffn2_w_down, ln3_g, ln3_b, loss_target, m_w_ada, m_b_ada, m_ffn1_w_gate_up, m_ffn1_w_down, m_ln1_g, m_ln1_b, m_w_in, m_conv_w, m_attn_sinks, m_w_out, m_ln2_g, m_ln2_b, m_ffn2_w_gate_up, m_ffn2_w_down, m_ln3_g, m_ln3_b, v_w_ada, v_b_ada, v_ffn1_w_gate_up, v_ffn1_w_down, v_ln1_g, v_ln1_b, v_w_in, v_conv_w, v_attn_sinks, v_w_out, v_ln2_g, v_ln2_b, v_ffn2_w_gate_up, v_ffn2_w_down, v_ln3_g, v_ln3_b):
    nseq, seq, dm = x.shape
    tokens = nseq * seq
    dev = 4 * lax.axis_index("x") + 2 * lax.axis_index("y") + lax.axis_index("c")
    core = lax.axis_index("c").astype(jnp.int32).reshape(1)
    ada_cols = w_ada.shape[2]
    ff = ffn1_w_down.shape[1] * N_DEV
    fc = ff // 4
    in_cols = w_in.shape[2]
    conv_cols = conv_w.shape[2]

    def t_bf16(w):
        return w[0].T.astype(BF16)

    c_all, convw_all = all_gather([c, conv_w[0]], "gather_cond")
    c_all = c_all.reshape(N_DEV * nseq, dm)
    convw_full = convw_all.transpose(1, 0, 2).reshape(CONV_TAPS, N_DEV * conv_cols)

    b_cols = lax.dynamic_slice(b_ada, (0, dev * ada_cols), (1, ada_cols))
    cond_all, mod_cols = ada_fwd(c_all, w_ada[0], b_cols, "ada_fwd")
    wgu1, mod_all = all_gather([t_bf16(ffn1_w_gate_up), mod_cols], "gather_ffn1")
    wgu1 = wgu1.reshape(2, ff, dm)
    mod = lax.dynamic_slice(mod_all, (0, dev * nseq, 0), (N_DEV, nseq, ada_cols))
    mod = mod.transpose(1, 0, 2).reshape(nseq, 9, 1, dm)
    sh1, sc1, g1, sh2, sc2, g2, sh3, sc3, g3 = [mod[:, i] for i in range(9)]

    x0 = x.reshape(tokens, dm)
    (gu1, a1, h1), (wd1, wout) = ffn_up(x0, sh1, sc1, wgu1, seq, "ffn1_up",
                                        job=_GatherJob([ffn1_w_down[0].astype(BF16), w_out[0].astype(BF16)]))
    wd1, wout = wd1.reshape(ff, dm), wout.reshape(dm, dm)
    (x1, r1, f1), (win,) = ffn_down(x0, a1, g1, wd1, ln1_g, ln1_b, seq, "ffn1_down", job=_GatherJob([t_bf16(w_in)]))
    win = win.reshape(N_DEV * in_cols, dm)
    (q, k, v, u, bg, cg), wd2_spread = proj_fwd(x1, sh2, sc2, win, seq, "proj_fwd",
                                                job=gather_spread_job([ffn2_w_down[0].astype(BF16)]))
    cos_t, sin_t = _rope_tables(positions.reshape(tokens))
    sinks = attn_sinks[0]
    (attn, q_rot, probs, sink_probs), wgu2_spread = attn_fwd(q, k, v, cos_t, sin_t, sinks, seq, "attn_fwd",
                                                             job=gather_spread_job([t_bf16(ffn2_w_gate_up)]))
    conv = conv_fwd(u, bg, cg, convw_full, seq, "conv_fwd")
    (x2, r2, mixin, mix), (wd2, wgu2) = out_fwd(x1, attn, conv, g2, wout, ln2_g, ln2_b, seq, "out_fwd",
                                                job=gather_forward_job(wd2_spread + wgu2_spread))
    wd2, wgu2 = wd2.reshape(ff, dm), wgu2.reshape(2, ff, dm)
    target = loss_target.reshape(tokens, dm)
    dr3, df3, gu3, a3, h3, loss_part, dln3, dg3 = ffn_loss(x2, sh3, sc3, g3, wgu2, wd2, ln3_g, ln3_b, target, seq, "ffn2_fwd")

    (dx2, dgu3, dmod3), _ = ffn_bwd(dr3, df3, x2, gu3, sc3, wgu2, wd2, seq, "ffn2_bwd")
    pair = 2 * fc
    g_wd2 = tn_matmul(a3[None], df3[None], "ffn2_dwd", a_width=pair)[0][0].reshape(N_DEV, ff // N_DEV, dm)
    g_wgu2 = tn_matmul(dgu3, h3[None], "ffn2_dwgu", a_width=pair)[0][0].reshape(N_DEV, fc, dm)
    (dres2, dattn, dconv, dmix, dln2, dg2), swapped = out_bwd(dx2, r2, mix, g2, wout, ln2_g, seq, "out_bwd",
                                                              job=swap_job([g_wgu2, g_wd2]))
    p_wgu2, own_wgu2 = pair_sum(core, g_wgu2, swapped[0], "pair_wgu2")
    p_wd2, own_wd2 = pair_sum(core, g_wd2, swapped[1], "pair_wd2")
    du, dbg, dcg, dconvw = conv_bwd(dconv, u, bg, cg, convw_full, seq, "conv_bwd")
    (dq, dk, dv, dsink_rows), (far_wd2,) = attn_bwd(
        q_rot, k, v, dattn, probs, sink_probs, cos_t, sin_t, seq, "attn_bwd", job=chip_exchange_job([p_wd2]))
    parts = [dq, dk, dv, du, dbg, dcg]
    (dr1, df1, dproj, h2, dmod2, dln1, dg1), far_top = proj_bwd(
        parts, dres2, x1, sh2, sc2, win, r1, f1, g1, ln1_g, seq, "proj_bwd",
        job=chip_exchange_job([p_wgu2], rows=(0, fc // 2)))
    (dx0, dgu1, dmod1), _ = ffn_bwd(dr1, df1, x0, gu1, sc1, wgu1, wd1, seq, "ffn1_bwd")

    dmod = jnp.concatenate([dmod1, dg1, dmod2, dg2, dmod3, dg3], axis=1).reshape(nseq, 9 * dm)
    half = dm // 2
    jobs = _Jobs([gather_spread_job([dmod]),
                  chip_exchange_job([p_wgu2], rows=(fc // 2, fc // 2), into=far_top)])
    (g_wd1,), res = tn_matmul(a1[None], df1[None], "ffn1_dwd", job=jobs, a_width=pair)
    dmod_spread, (far_wgu2,) = jobs.split(res)
    g_wd1 = g_wd1.reshape(N_DEV, ff // N_DEV, dm)
    jobs = _Jobs([swap_job([g_wd1]), gather_forward_job(dmod_spread)])
    (g_l,), res = tn_matmul(dgu1, h1[None], "ffn1_dwgu_l", job=jobs, b_cols=(0, half), a_width=pair)
    (sw_wd1,), (dmod_all,) = jobs.split(res)
    g_l = g_l.reshape(N_DEV, fc, half)
    p_wd1, own_wd1 = pair_sum(core, g_wd1, sw_wd1, "pair_wd1")
    jobs = _Jobs([chip_exchange_job([p_wd1]), swap_job([g_l])])
    (g_r,), res = tn_matmul(dgu1, h1[None], "ffn1_dwgu_r", job=jobs, b_cols=(1, half), a_width=pair)
    (far_wd1,), (sw_l,) = jobs.split(res)
    g_r = g_r.reshape(N_DEV, fc, half)
    p_l, own_l = pair_sum(core, g_l, sw_l, "pair_wgu1_l")

    dmod_cols = lax.dynamic_slice(dmod_all.reshape(N_DEV * nseq, 9 * dm), (0, dev * ada_cols), (N_DEV * nseq, ada_cols))
    grad_w_ada, gb_cols = ada_bwd(cond_all, dmod_cols, "ada_bwd")
    dsinks = jnp.sum(dsink_rows.reshape(nseq, N_Q_HEADS, ATTN_BLOCK), axis=(0, 2))
    small = jnp.zeros((8, dm), F32)
    small = small.at[0:2].set(dln1).at[2:4].set(dln2).at[4:6].set(dln3)
    small = small.at[6, 0:N_Q_HEADS].set(dsinks).at[7, 0].set(loss_part[0, 0])

    jobs = _Jobs([chip_exchange_job([p_l]), swap_job([g_r]), gather_spread_job([small, dconvw, gb_cols])])
    (g_win,), res = tn_matmul(dproj[None], h2[None], "dwin", job=jobs)
    (far_l,), (sw_r,), small_spread = jobs.split(res)
    g_win = g_win.reshape(N_DEV, in_cols, dm)
    p_r, own_r = pair_sum(core, g_r, sw_r, "pair_wgu1_r")
    jobs = _Jobs([chip_exchange_job([p_r]), swap_job([g_win]), gather_forward_job(small_spread)])
    (g_wout,), res = tn_matmul(mixin[None], dmix[None], "dwout", job=jobs)
    (far_r,), (sw_win,), (small_all, dconvw_all, gb_all) = jobs.split(res)
    g_wout = g_wout.reshape(N_DEV, dm // N_DEV, dm)
    p_win, own_win = pair_sum(core, g_win, sw_win, "pair_win")

    given = dict(w_ada=(w_ada, m_w_ada, v_w_ada), b_ada=(b_ada, m_b_ada, v_b_ada),
                 ffn1_w_gate_up=(ffn1_w_gate_up, m_ffn1_w_gate_up, v_ffn1_w_gate_up),
                 ffn1_w_down=(ffn1_w_down, m_ffn1_w_down, v_ffn1_w_down),
                 ln1_g=(ln1_g, m_ln1_g, v_ln1_g), ln1_b=(ln1_b, m_ln1_b, v_ln1_b),
                 w_in=(w_in, m_w_in, v_w_in), conv_w=(conv_w, m_conv_w, v_conv_w),
                 attn_sinks=(attn_sinks, m_attn_sinks, v_attn_sinks), w_out=(w_out, m_w_out, v_w_out),
                 ln2_g=(ln2_g, m_ln2_g, v_ln2_g), ln2_b=(ln2_b, m_ln2_b, v_ln2_b),
                 ffn2_w_gate_up=(ffn2_w_gate_up, m_ffn2_w_gate_up, v_ffn2_w_gate_up),
                 ffn2_w_down=(ffn2_w_down, m_ffn2_w_down, v_ffn2_w_down),
                 ln3_g=(ln3_g, m_ln3_g, v_ln3_g), ln3_b=(ln3_b, m_ln3_b, v_ln3_b))
    transposed = ("ffn1_w_gate_up", "ffn2_w_gate_up", "w_in")

    def big_adamw(nm, grad, far=None):
        flip = nm in transposed
        w2, m2, v2 = [t[0].T if flip else t[0] for t in given[nm]]
        return [t.T[None] if flip else t[None] for t in adamw(w2, grad, m2, v2, "adamw_" + nm, others=far)]

    jobs = _Jobs([chip_exchange_job([p_win]), swap_job([g_wout])])
    (far_win,), (sw_wout,) = jobs.split(run_job(jobs, "rs_tail_win"))
    p_wout, own_wout = pair_sum(core, g_wout, sw_wout, "pair_wout")
    (far_wout,) = run_job(chip_exchange_job([p_wout]), "rs_tail_wout")

    grads = {
        "ffn1_w_gate_up": [own_l, own_r], "ffn1_w_down": own_wd1,
        "w_in": own_win, "w_out": own_wout, "ffn2_w_gate_up": own_wgu2, "ffn2_w_down": own_wd2,
    }
    others = {"ffn1_w_gate_up": [far_l, far_r], "ffn1_w_down": far_wd1,
              "w_in": far_win, "w_out": far_wout, "ffn2_w_gate_up": far_wgu2, "ffn2_w_down": far_wd2}
    results = {"w_ada": big_adamw("w_ada", grad_w_ada)}
    for nm in grads:
        results[nm] = big_adamw(nm, grads[nm], others[nm])

    small_rows = ["ln1_g", "ln1_b", "ln2_g", "ln2_b", "ln3_g", "ln3_b", "attn_sinks", "loss"]
    small_sums = sum_devices(small_all, "sum_small", row_widths=[dm] * 6 + [N_Q_HEADS, 1])
    dconvw_sum = sum_devices(dconvw_all, "sum_convw")
    grads.update(zip(small_rows, small_sums))
    loss = grads.pop("loss").reshape(())
    grads["b_ada"] = gb_all.reshape(1, N_DEV * ada_cols)
    grads["conv_w"] = lax.dynamic_slice(dconvw_sum, (0, dev * conv_cols), (CONV_TAPS, conv_cols))

    order = ["w_ada", "b_ada", "ffn1_w_gate_up", "ffn1_w_down", "ln1_g", "ln1_b", "w_in", "conv_w", "attn_sinks",
             "w_out", "ln2_g", "ln2_b", "ffn2_w_gate_up", "ffn2_w_down", "ln3_g", "ln3_b"]
    small_names = [nm for nm in order if nm not in results]
    def rows_first(t):
        return t.transpose(1, 0, 2) if t.ndim == 3 else t

    items = []
    for nm in small_names:
        w, m, v = given[nm]
        grad = grads[nm].reshape(w.shape)
        items.append((rows_first(w), rows_first(grad), rows_first(m), rows_first(v)))
    for nm, res in zip(small_names, adamw_small(items, "adamw_small")):
        results[nm] = [grads[nm].reshape(given[nm][0].shape)] + [rows_first(t) for t in res]
    grad_x = dx0.reshape(nseq, seq, dm)
    return (loss, grad_x, *[results[nm][i] for i in range(4) for nm in order])
```

```python
import functools

import jax
import jax.numpy as jnp
import numpy as np
from jax import lax
from jax.experimental import pallas as pl
from jax.experimental.pallas import tpu as pltpu

F32 = jnp.float32
BF16 = jnp.bfloat16
MESH = pl.DeviceIdType.MESH

N_DEV = 8
N_CHIP = 4
HEAD_DIM = 64
N_Q_HEADS = 8
N_KV_HEADS = 2
GQA_GROUP = N_Q_HEADS // N_KV_HEADS
ATTN_BLOCK = 128
ROT_DIM = 16
ROPE_THETA = 500000.0
CONV_TAPS = 3
LN_EPS = 1e-5
DN_ALPHA = 2.0 ** 0.25
ADAM_LR = 0.001
ADAM_B1 = 0.9
ADAM_B2 = 0.999
ADAM_EPS = 1e-08
ADAM_WD = 0.01
ADAM_STEP = 10
NEG_BIG = -1e30

VMEM_LIMIT = 56 * 1024 * 1024
FFN_FWD_TILE = 512
MIX_TILE = 512
FFN_WIDE_TILE = 512
FFN_WIDE_VMEM = 62 * 1024 * 1024
TN_VMEM_BUDGET = 36 * 1024 * 1024


def _params(semantics=None, vmem=VMEM_LIMIT):
    return pltpu.CompilerParams(dimension_semantics=semantics, vmem_limit_bytes=vmem)


def _dot(a, b):
    return jnp.dot(a, b, preferred_element_type=F32)


def _dot_nt(a, b):
    return lax.dot_general(a, b, (((1,), (1,)), ((), ())), preferred_element_type=F32)


def _dot_tn(a, b):
    return lax.dot_general(a, b, (((0,), (0,)), ((), ())), preferred_element_type=F32)


def _sigmoid(x):
    return pl.reciprocal(1.0 + jnp.exp(-x), approx=True)


def _ln_stats(r):
    mu = jnp.mean(r, axis=-1, keepdims=True)
    d = r - mu
    var = jnp.mean(d * d, axis=-1, keepdims=True)
    rstd = lax.rsqrt(var + LN_EPS)
    return d * rstd, rstd


def _ln_bwd(dy, r, g):
    return _ln_bwd_normalized(dy, *_ln_stats(r), g)


def _ln_bwd_normalized(dy, xhat, rstd, g):
    dxhat = dy * g
    c1 = jnp.mean(dxhat, axis=-1, keepdims=True)
    c2 = jnp.mean(dxhat * xhat, axis=-1, keepdims=True)
    dr = rstd * (dxhat - c1 - xhat * c2)
    return dr, jnp.sum(dy * xhat, axis=0, keepdims=True), jnp.sum(dy, axis=0, keepdims=True)


def _const_spec(shape):
    nd = len(shape)
    return pl.BlockSpec(shape, lambda *_: (0,) * nd, pipeline_mode=pl.Buffered(1))


def all_gather(arrs, name):
    n = len(arrs)

    def body(*refs):
        ins, outs = refs[:n], refs[n:2 * n]
        send_sems, recv_sems, local_sems = refs[2 * n:]
        x, y, c = lax.axis_index("x"), lax.axis_index("y"), lax.axis_index("c")
        me, sibling = (x, y, c), (x, y, 1 - c)
        chips = [(1 - x, y), (x, 1 - y), (1 - x, 1 - y)]

        def slot(i, p):
            return outs[i].at[4 * p[0] + 2 * p[1] + p[2]]

        def copy(i, k, block, to, src=None):
            return pltpu.make_async_remote_copy(
                src_ref=slot(i, block) if src is None else src, dst_ref=slot(i, block),
                send_sem=send_sems.at[i, k], recv_sem=recv_sems.at[i, k],
                device_id=to, device_id_type=MESH)

        mine = [pltpu.make_async_copy(ins[i], slot(i, me), local_sems.at[i]) for i in range(n)]
        for cp in mine:
            cp.start()
        first = []
        for i in range(n):
            first.append(copy(i, 0, me, sibling, src=ins[i]))
            first += [copy(i, 1 + j, me, (*chip, c), src=ins[i]) for j, chip in enumerate(chips)]
        for cp in first:
            cp.start()
        passed = []
        for i in range(n):
            for j, chip in enumerate(chips):
                copy(i, 1 + j, (*chip, c), me).wait_recv()
                cp = copy(i, 4 + j, (*chip, c), sibling)
                cp.start()
                passed.append(cp)
        for i in range(n):
            copy(i, 0, sibling, me).wait_recv()
            for j, chip in enumerate(chips):
                copy(i, 4 + j, (*chip, 1 - c), me).wait_recv()
        for cp in first + passed:
            cp.wait_send()
        for cp in mine:
            cp.wait()

    any_spec = pl.BlockSpec(memory_space=pl.ANY)
    return pl.pallas_call(
        body, name=name,
        out_shape=[jax.ShapeDtypeStruct((N_DEV, *a.shape), a.dtype) for a in arrs],
        in_specs=[any_spec] * n, out_specs=[any_spec] * n,
        scratch_shapes=[pltpu.SemaphoreType.DMA((n, 7)), pltpu.SemaphoreType.DMA((n, 7)),
                        pltpu.SemaphoreType.DMA((n,))],
    )(*arrs)


def _place():
    x, y, c = lax.axis_index("x"), lax.axis_index("y"), lax.axis_index("c")
    return x, y, c, [(1 - x, y), (x, 1 - y), (1 - x, 1 - y)]


def _slot(p):
    return 4 * p[0] + 2 * p[1] + p[2]


class _Job:
    def __init__(self, ins, outs, nsem, copies, aliases=None, local=None):
        self.ins, self.outs, self.nsem, self.copies = list(ins), list(outs), nsem, copies
        self.aliases = aliases or {}
        self.local = local

    def scratch(self):
        s = [pltpu.SemaphoreType.DMA(self.nsem), pltpu.SemaphoreType.DMA(self.nsem)]
        if self.local is not None:
            s.append(pltpu.SemaphoreType.DMA((len(self.ins),)))
        return s

    def start(self, ins, outs, sems):
        if self.local is not None:
            for cp in self.local(ins, outs, sems[2]):
                cp.start()
        for cp in self.copies(ins, outs, sems[0], sems[1])[0]:
            cp.start()

    def finish(self, ins, outs, sems):
        started, awaited = self.copies(ins, outs, sems[0], sems[1])
        for cp in awaited:
            cp.wait_recv()
        for cp in started:
            cp.wait_send()
        if self.local is not None:
            for cp in self.local(ins, outs, sems[2]):
                cp.wait()


class _Jobs:
    def __init__(self, jobs):
        self.jobs = jobs
        self.ins = [a for j in jobs for a in j.ins]
        self.outs = [o for j in jobs for o in j.outs]
        self.two_phase = any(getattr(j, "two_phase", False) for j in jobs)
        self.aliases = {}
        at_in = at_out = 0
        for j in jobs:
            self.aliases.update({at_in + i: at_out + o for i, o in j.aliases.items()})
            at_in, at_out = at_in + len(j.ins), at_out + len(j.outs)

    def scratch(self):
        return [s for j in self.jobs for s in j.scratch()]

    def _each(self, ins, outs, sems):
        at_in = at_out = at_sem = 0
        for j in self.jobs:
            n_in, n_out, n_sem = len(j.ins), len(j.outs), len(j.scratch())
            yield j, ins[at_in:at_in + n_in], outs[at_out:at_out + n_out], sems[at_sem:at_sem + n_sem]
            at_in, at_out, at_sem = at_in + n_in, at_out + n_out, at_sem + n_sem

    def start(self, ins, outs, sems):
        for j, i, o, s in self._each(ins, outs, sems):
            j.start(i, o, s)

    def turn(self, ins, outs, sems):
        for j, i, o, s in self._each(ins, outs, sems):
            if getattr(j, "two_phase", False):
                j.turn(i, o, s)

    def finish(self, ins, outs, sems):
        for j, i, o, s in self._each(ins, outs, sems):
            j.finish(i, o, s)

    def split(self, results):
        at, parts = 0, []
        for j in self.jobs:
            parts.append(results[at:at + len(j.outs)])
            at += len(j.outs)
        return parts


def _remote(src, dst, send, recv, idx, to):
    return pltpu.make_async_remote_copy(src_ref=src, dst_ref=dst, send_sem=send.at[idx], recv_sem=recv.at[idx],
                                        device_id=to, device_id_type=MESH)


def _spread_copies(ins, outs, send, recv, base=0):
    x, y, c, chips = _place()
    me = (x, y, c)
    peers = [(x, y, 1 - c)] + [(*chip, c) for chip in chips]
    started, awaited = [], []
    for i, (src, dst) in enumerate(zip(ins, outs)):
        for k, peer in enumerate(peers):
            started.append(_remote(src, dst.at[_slot(me)], send, recv, (base + i, k), peer))
            awaited.append(_remote(src, dst.at[_slot(peer)], send, recv, (base + i, k), peer))
    return started, awaited


def _forward_copies(ins, outs, send, recv, base=0):
    x, y, c, chips = _place()
    started, awaited = [], []
    for i, buf in enumerate(outs):
        for j, chip in enumerate(chips):
            mine, theirs = buf.at[_slot((*chip, c))], buf.at[_slot((*chip, 1 - c))]
            started.append(_remote(mine, mine, send, recv, (base + i, j), (x, y, 1 - c)))
            awaited.append(_remote(theirs, theirs, send, recv, (base + i, j), (x, y, 1 - c)))
    return started, awaited


def _own_block_copies(ins, outs, sems):
    x, y, c, _ = _place()
    return [pltpu.make_async_copy(src, dst.at[_slot((x, y, c))], sems.at[i])
            for i, (src, dst) in enumerate(zip(ins, outs))]


def gather_spread_job(shards):
    outs = [jax.ShapeDtypeStruct((N_DEV, *a.shape), a.dtype) for a in shards]
    return _Job(shards, outs, (len(shards), 4), _spread_copies, local=_own_block_copies)


def gather_forward_job(fulls):
    outs = [jax.ShapeDtypeStruct(a.shape, a.dtype) for a in fulls]
    return _Job(fulls, outs, (len(fulls), 3), _forward_copies, aliases={i: i for i in range(len(fulls))})


TURN_EIGHTHS = 6


class _GatherJob:
    two_phase = True

    def __init__(self, shards):
        self.ins = list(shards)
        self.outs = [jax.ShapeDtypeStruct((N_DEV, *a.shape), a.dtype) for a in shards]
        self.aliases = {}

    def scratch(self):
        n = len(self.ins)
        return [pltpu.SemaphoreType.DMA((n, 4)), pltpu.SemaphoreType.DMA((n, 4)),
                pltpu.SemaphoreType.DMA((n, 3)), pltpu.SemaphoreType.DMA((n, 3)), pltpu.SemaphoreType.DMA((n,))]

    def start(self, ins, outs, sems):
        for cp in _own_block_copies(ins, outs, sems[4]) + _spread_copies(ins, outs, sems[0], sems[1])[0]:
            cp.start()

    def turn(self, ins, outs, sems):
        for cp in _spread_copies(ins, outs, sems[0], sems[1])[1]:
            cp.wait_recv()
        for cp in _forward_copies(outs, outs, sems[2], sems[3])[0]:
            cp.start()

    def finish(self, ins, outs, sems):
        handed_on, arriving = _forward_copies(outs, outs, sems[2], sems[3])
        for cp in arriving:
            cp.wait_recv()
        for cp in _spread_copies(ins, outs, sems[0], sems[1])[0] + handed_on:
            cp.wait_send()
        for cp in _own_block_copies(ins, outs, sems[4]):
            cp.wait()


def swap_job(gs):
    def copies(ins, outs, send, recv):
        x, y, c, _ = _place()
        started, awaited = [], []
        for i, (g, r1) in enumerate(zip(ins, outs)):
            for q in range(N_CHIP):
                started.append(_remote(g.at[2 * q + (1 - c)], r1.at[q], send, recv, (i, q), (x, y, 1 - c)))
                awaited.append(_remote(g.at[2 * q + c], r1.at[q], send, recv, (i, q), (x, y, 1 - c)))
        return started, awaited

    outs = [jax.ShapeDtypeStruct((N_CHIP, *g.shape[1:]), g.dtype) for g in gs]
    return _Job(gs, outs, (len(gs), N_CHIP), copies)


def chip_exchange_job(ps, rows=None, into=None):
    n = len(ps)

    def copies(ins, outs, send, recv):
        x, y, c, chips = _place()
        started, awaited = [], []
        for i, (p, r2) in enumerate(zip(ins[:n], outs)):
            for k, chip in enumerate(chips):
                src, mine, dst = p.at[2 * chip[0] + chip[1]], p.at[2 * x + y], r2.at[k]
                if rows is not None:
                    src, mine, dst = (t.at[pl.ds(rows[0], rows[1])] for t in (src, mine, dst))
                started.append(_remote(src, dst, send, recv, (i, k), (*chip, c)))
                awaited.append(_remote(mine, dst, send, recv, (i, k), (*chip, c)))
        return started, awaited

    outs = [jax.ShapeDtypeStruct((3, *p.shape[1:]), p.dtype) for p in ps]
    if into is None:
        return _Job(ps, outs, (n, 3), copies)
    return _Job(list(ps) + list(into), outs, (n, 3), copies, aliases={n + i: i for i in range(n)})


def _call(body, job, *, name, grid, in_specs, out_specs, out_shape, args, scratch_shapes=(), vmem=VMEM_LIMIT):
    if job is None:
        res = pl.pallas_call(
            body, name=name, grid=grid, in_specs=in_specs, out_specs=out_specs, out_shape=out_shape,
            scratch_shapes=list(scratch_shapes), compiler_params=_params(("arbitrary",) * len(grid), vmem),
        )(*args)
        return res, []
    n_in, n_out, n_scr = len(in_specs), len(out_specs), len(scratch_shapes)
    j_in, j_out = len(job.ins), len(job.outs)

    def with_copies(*refs):
        at = 0
        ins = refs[at:at + n_in]; at += n_in
        jins = refs[at:at + j_in]; at += j_in
        outs = refs[at:at + n_out]; at += n_out
        jouts = refs[at:at + j_out]; at += j_out
        scr = refs[at:at + n_scr]; at += n_scr
        sems = refs[at:]
        ids = [pl.program_id(d) for d in range(len(grid))]
        first = functools.reduce(jnp.logical_and, [i == 0 for i in ids])
        last = functools.reduce(jnp.logical_and, [i == n - 1 for i, n in zip(ids, grid)])

        @pl.when(first)
        def _():
            job.start(jins, jouts, sems)

        if getattr(job, "two_phase", False):
            steps, at = 1, 0
            for i, n in zip(ids, grid):
                steps, at = steps * n, at * n + i

            @pl.when(at == (TURN_EIGHTHS * steps) // 8)
            def _():
                job.turn(jins, jouts, sems)

        body(*ins, *outs, *scr)

        @pl.when(last)
        def _():
            job.finish(jins, jouts, sems)

    any_spec = pl.BlockSpec(memory_space=pl.ANY)
    res = pl.pallas_call(
        with_copies, name=name, grid=grid,
        in_specs=list(in_specs) + [any_spec] * j_in, out_specs=list(out_specs) + [any_spec] * j_out,
        out_shape=list(out_shape) + list(job.outs),
        input_output_aliases={n_in + i: n_out + o for i, o in job.aliases.items()},
        scratch_shapes=list(scratch_shapes) + job.scratch(),
        compiler_params=_params(("arbitrary",) * len(grid), vmem),
    )(*args, *job.ins)
    return res[:n_out], res[n_out:]


def run_job(job, name):
    def body(*refs):
        j_in, j_out = len(job.ins), len(job.outs)
        ins, outs, sems = refs[:j_in], refs[j_in:j_in + j_out], refs[j_in + j_out:]
        job.start(ins, outs, sems)
        job.finish(ins, outs, sems)

    any_spec = pl.BlockSpec(memory_space=pl.ANY)
    return pl.pallas_call(
        body, name=name, in_specs=[any_spec] * len(job.ins), out_specs=[any_spec] * len(job.outs),
        out_shape=list(job.outs), input_output_aliases=dict(job.aliases), scratch_shapes=job.scratch(),
    )(*job.ins)


def pair_sum(core, g, r1, name):
    _, rows, cols = g.shape
    rb = next(cand for cand in range(min(rows, 512), 0, -16) if rows % cand == 0)

    def body(core_ref, g_ref, r1_ref, p_ref, own_ref):
        del core_ref
        x, y, _, _ = _place()
        s = g_ref[0].astype(F32) + r1_ref[0].astype(F32)
        p_ref[0] = s.astype(BF16)

        @pl.when(pl.program_id(1) == 2 * x + y)
        def _():
            own_ref[...] = s

    chunk = (1, rb, cols)
    return pl.pallas_call(
        body, name=name,
        grid_spec=pltpu.PrefetchScalarGridSpec(
            num_scalar_prefetch=1, grid=(rows // rb, N_CHIP),
            in_specs=[pl.BlockSpec(chunk, lambda i, q, core_ref: (2 * q + core_ref[0], i, 0)),
                      pl.BlockSpec(chunk, lambda i, q, core_ref: (q, i, 0))],
            out_specs=[pl.BlockSpec(chunk, lambda i, q, core_ref: (q, i, 0)),
                       pl.BlockSpec((rb, cols), lambda i, q, core_ref: (i, 0))]),
        out_shape=[jax.ShapeDtypeStruct((N_CHIP, rows, cols), BF16), jax.ShapeDtypeStruct((rows, cols), F32)],
        compiler_params=_params(("arbitrary", "arbitrary")),
    )(core, g, r1)


def sum_devices(a, name, row_widths=None):
    def body(a_ref, *o_refs):
        acc = a_ref[0]
        for d in range(1, N_DEV):
            acc = acc + a_ref[d]
        if row_widths is None:
            o_refs[0][...] = acc
            return
        for r, (o_ref, width) in enumerate(zip(o_refs, row_widths)):
            o_ref[...] = acc[r:r + 1, 0:width]

    if row_widths is None:
        return pl.pallas_call(body, name=name, out_shape=jax.ShapeDtypeStruct(a.shape[1:], F32))(a)
    assert len(row_widths) == a.shape[1]
    return pl.pallas_call(body, name=name, out_shape=[jax.ShapeDtypeStruct((1, width), F32) for width in row_widths])(a)


def _adam_update(w, g, m, v):
    nm = ADAM_B1 * m + (1.0 - ADAM_B1) * g
    nv = ADAM_B2 * v + (1.0 - ADAM_B2) * (g * g)
    m_hat = nm / (1.0 - ADAM_B1 ** ADAM_STEP)
    v_hat = nv / (1.0 - ADAM_B2 ** ADAM_STEP)
    return -ADAM_LR * (m_hat / (jnp.sqrt(v_hat) + ADAM_EPS) + ADAM_WD * w), nm, nv


def adamw(w, g, m, v, name, others=None):
    rows, cols = w.shape
    g_parts = list(g) if isinstance(g, (list, tuple)) else [g]
    other_parts = [] if others is None else list(others) if isinstance(others, (list, tuple)) else [others]
    n_parts = len(g_parts)
    assert others is not None or n_parts == 1
    assert others is None or len(other_parts) == n_parts
    assert sum(p.shape[1] for p in g_parts) == cols
    rb = rows
    for cand in range(min(rows, 512), 7, -8):
        if rows % cand == 0 and cand % 8 == 0:
            rb = cand
            break

    def body(*refs):
        w_ref, m_ref, v_ref = refs[:3]
        g_refs = refs[3:3 + n_parts]
        if others is None:
            d_ref, nm_ref, nv_ref = refs[3 + n_parts:]
            gg = g_refs[0][...]
        else:
            r2_refs = refs[3 + n_parts:3 + 2 * n_parts]
            go_ref, d_ref, nm_ref, nv_ref = refs[3 + 2 * n_parts:]
            blocks = []
            for g_ref, r2_ref in zip(g_refs, r2_refs):
                part = g_ref[...]
                for k in range(3):
                    part = part + r2_ref[k].astype(F32)
                blocks.append(part)
            gg = blocks[0] if n_parts == 1 else jnp.concatenate(blocks, axis=1)
            go_ref[...] = gg
        d_ref[...], nm_ref[...], nv_ref[...] = _adam_update(w_ref[...], gg, m_ref[...], v_ref[...])

    spec = pl.BlockSpec((rb, cols), lambda i: (i, 0))
    out = jax.ShapeDtypeStruct((rows, cols), F32)
    in_specs, args = [spec] * 3, [w, m, v]
    in_specs += [pl.BlockSpec((rb, p.shape[1]), lambda i: (i, 0)) for p in g_parts]
    args += g_parts
    in_specs += [pl.BlockSpec((3, rb, p.shape[2]), lambda i: (0, i, 0)) for p in other_parts]
    args += other_parts
    n_out = 3 if others is None else 4
    res = pl.pallas_call(
        body, name=name, grid=(rows // rb,), in_specs=in_specs, out_specs=[spec] * n_out,
        out_shape=[out] * n_out, compiler_params=_params(("parallel",)),
    )(*args)
    return (g, *res) if others is None else tuple(res)


def adamw_small(items, name):
    n = len(items)

    def body(*refs):
        ins, outs = refs[:4 * n], refs[4 * n:]
        for i in range(n):
            w_ref, g_ref, m_ref, v_ref = ins[4 * i:4 * i + 4]
            d_ref, nm_ref, nv_ref = outs[3 * i:3 * i + 3]
            d_ref[...], nm_ref[...], nv_ref[...] = _adam_update(w_ref[...], g_ref[...], m_ref[...], v_ref[...])

    res = pl.pallas_call(
        body, name=name,
        out_shape=[jax.ShapeDtypeStruct(w.shape, F32) for w, _, _, _ in items for _ in range(3)],
    )(*[t for item in items for t in item])
    return [tuple(res[3 * i:3 * i + 3]) for i in range(n)]


def ada_fwd(dev, c_all, w_cols, b_all, name):
    def body(dev_ref, c_ref, w_ref, b_ref, cond_ref, mod_ref):
        del dev_ref
        cc = c_ref[...]
        cond = (cc * _sigmoid(cc)).astype(BF16)
        cond_ref[...] = cond
        mod_ref[...] = _dot(cond, w_ref[...].astype(BF16)) + b_ref[...]

    n, cols = c_all.shape[0], w_cols.shape[1]
    return pl.pallas_call(
        body, name=name,
        grid_spec=pltpu.PrefetchScalarGridSpec(
            num_scalar_prefetch=1, grid=(1,),
            in_specs=[pl.BlockSpec(c_all.shape, lambda i, dev_ref: (0, 0)),
                      pl.BlockSpec(w_cols.shape, lambda i, dev_ref: (0, 0)),
                      pl.BlockSpec((1, cols), lambda i, dev_ref: (0, dev_ref[0]))],
            out_specs=[pl.BlockSpec(c_all.shape, lambda i, dev_ref: (0, 0)),
                       pl.BlockSpec((n, cols), lambda i, dev_ref: (0, 0))]),
        out_shape=[jax.ShapeDtypeStruct(c_all.shape, BF16), jax.ShapeDtypeStruct((n, cols), F32)],
        compiler_params=_params(("arbitrary",)),
    )(dev, c_all, w_cols, b_all)


def ada_bwd(cond_all, dmod_cols, name):
    def body(c_ref, d_ref, gw_ref, gb_ref):
        d = d_ref[...]
        gw_ref[...] = _dot_tn(c_ref[...], d.astype(BF16))
        gb_ref[...] = jnp.sum(d, axis=0, keepdims=True)

    dm, cols = cond_all.shape[1], dmod_cols.shape[1]
    return pl.pallas_call(
        body, name=name,
        out_shape=[jax.ShapeDtypeStruct((dm, cols), F32), jax.ShapeDtypeStruct((1, cols), F32)],
        compiler_params=_params(),
    )(cond_all, dmod_cols)


MXU_COLS = 256
FFN_CHUNK = 4 * MXU_COLS


def _hidden_chunks(ff):
    assert ff % MXU_COLS == 0
    return [(at, min(FFN_CHUNK, ff - at)) for at in range(0, ff, FFN_CHUNK)]


def _mod_spec(tiles_per_seq, dm):
    return pl.BlockSpec((1, 1, dm), lambda i: (i // tiles_per_seq, 0, 0))


def ffn_loss(x, sh, sc, gt, wgu, wd, ln_g, ln_b, target, seq, name):
    tokens, dm = x.shape
    ff = wgu.shape[1]
    chunks = _hidden_chunks(ff)
    tm = min(FFN_WIDE_TILE, seq)
    tiles_per_seq = seq // tm

    def body(x_ref, sh_ref, sc_ref, gt_ref, wgu_ref, wd_ref, lg_ref, lb_ref, t_ref,
             dr_ref, df_ref, gu_ref, a_ref, h_ref, loss_ref, dln_ref, dgt_ref):
        i = pl.program_id(0)
        xx = x_ref[...]
        h = (xx * (1.0 + sc_ref[0]) + sh_ref[0]).astype(BF16)
        h_ref[...] = h
        acc = jnp.zeros((tm, dm), F32)
        for at, wdt in chunks:
            gk = _dot_nt(h, wgu_ref[0, at:at + wdt, :])
            uk = _dot_nt(h, wgu_ref[1, at:at + wdt, :])
            gu_ref[0, :, at:at + wdt] = gk.astype(BF16)
            gu_ref[1, :, at:at + wdt] = uk.astype(BF16)
            a = (gk * _sigmoid(gk) * uk).astype(BF16)
            a_ref[:, at:at + wdt] = a
            acc = acc + _dot(a, wd_ref[at:at + wdt, :])
        half_gate = 0.5 * (1.0 + gt_ref[0])
        xhat, rstd = _ln_stats(DN_ALPHA * xx + half_gate * acc)
        err = xhat * lg_ref[...] + lb_ref[...] - t_ref[...]
        dr, dgain, dbias = _ln_bwd_normalized(err * (1.0 / dm), xhat, rstd, lg_ref[...])
        dr_ref[...] = dr
        df_ref[...] = (half_gate * dr).astype(BF16)

        @pl.when(i == 0)
        def _():
            loss_ref[...] = jnp.zeros_like(loss_ref)
            dln_ref[...] = jnp.zeros_like(dln_ref)

        @pl.when(i % tiles_per_seq == 0)
        def _():
            dgt_ref[...] = jnp.zeros_like(dgt_ref)

        loss_ref[...] += jnp.full((1, 128), (0.5 / dm) * jnp.sum(err * err), F32)
        dln_ref[0:1, :] += dgain
        dln_ref[1:2, :] += dbias
        dgt_ref[0] += jnp.sum(dr * (0.5 * acc), axis=0, keepdims=True)

    tile = pl.BlockSpec((tm, dm), lambda i: (i, 0))
    mod = _mod_spec(tiles_per_seq, dm)
    res, _ = _call(
        body, None, name=name, grid=(tokens // tm,),
        in_specs=[tile, mod, mod, mod, _const_spec(wgu.shape), _const_spec(wd.shape),
                  _const_spec((1, dm)), _const_spec((1, dm)), tile],
        out_specs=[tile, tile, pl.BlockSpec((2, tm, ff), lambda i: (0, i, 0)), pl.BlockSpec((tm, ff), lambda i: (i, 0)),
                   tile, pl.BlockSpec((1, 128), lambda i: (0, 0)), pl.BlockSpec((2, dm), lambda i: (0, 0)), mod],
        out_shape=[jax.ShapeDtypeStruct((tokens, dm), F32), jax.ShapeDtypeStruct((tokens, dm), BF16),
                   jax.ShapeDtypeStruct((2, tokens, ff), BF16), jax.ShapeDtypeStruct((tokens, ff), BF16),
                   jax.ShapeDtypeStruct((tokens, dm), BF16), jax.ShapeDtypeStruct((1, 128), F32),
                   jax.ShapeDtypeStruct((2, dm), F32), jax.ShapeDtypeStruct((tokens // seq, 1, dm), F32)],
        args=(x, sh, sc, gt, wgu, wd, ln_g, ln_b, target), vmem=FFN_WIDE_VMEM)
    return res


def ffn_up(x, sh, sc, wgu, seq, name, job=None):
    tokens, dm = x.shape
    ff = wgu.shape[1]
    chunks = _hidden_chunks(ff)
    tm = min(FFN_FWD_TILE, seq)

    def body(x_ref, sh_ref, sc_ref, wgu_ref, gu_ref, a_ref, h_ref):
        h = (x_ref[...] * (1.0 + sc_ref[0]) + sh_ref[0]).astype(BF16)
        h_ref[...] = h
        for at, wdt in chunks:
            gk = _dot_nt(h, wgu_ref[0, at:at + wdt, :])
            uk = _dot_nt(h, wgu_ref[1, at:at + wdt, :])
            gu_ref[0, :, at:at + wdt] = gk.astype(BF16)
            gu_ref[1, :, at:at + wdt] = uk.astype(BF16)
            a_ref[:, at:at + wdt] = (gk * _sigmoid(gk) * uk).astype(BF16)

    tile = pl.BlockSpec((tm, dm), lambda i: (i, 0))
    mod = _mod_spec(seq // tm, dm)
    return _call(
        body, job, name=name, grid=(tokens // tm,),
        in_specs=[tile, mod, mod, _const_spec(wgu.shape)],
        out_specs=[pl.BlockSpec((2, tm, ff), lambda i: (0, i, 0)), pl.BlockSpec((tm, ff), lambda i: (i, 0)), tile],
        out_shape=[jax.ShapeDtypeStruct((2, tokens, ff), BF16), jax.ShapeDtypeStruct((tokens, ff), BF16),
                   jax.ShapeDtypeStruct((tokens, dm), BF16)],
        args=(x, sh, sc, wgu))


def ffn_down(x, a, gt, wd, ln_g, ln_b, seq, name, job=None):
    tokens, dm = x.shape
    ff = wd.shape[0]
    chunks = _hidden_chunks(ff)
    tm = min(FFN_FWD_TILE, seq)

    def body(x_ref, a_ref, gt_ref, wd_ref, lg_ref, lb_ref, xo_ref, r_ref, f_ref):
        acc = jnp.zeros((tm, dm), F32)
        for at, wdt in chunks:
            acc = acc + _dot(a_ref[:, at:at + wdt], wd_ref[at:at + wdt, :])
        f_ref[...] = acc.astype(BF16)
        r = DN_ALPHA * x_ref[...] + (0.5 * (1.0 + gt_ref[0])) * acc
        r_ref[...] = r
        xhat, _ = _ln_stats(r)
        xo_ref[...] = xhat * lg_ref[...] + lb_ref[...]

    tile = pl.BlockSpec((tm, dm), lambda i: (i, 0))
    return _call(
        body, job, name=name, grid=(tokens // tm,),
        in_specs=[tile, pl.BlockSpec((tm, ff), lambda i: (i, 0)), _mod_spec(seq // tm, dm), _const_spec(wd.shape),
                  _const_spec((1, dm)), _const_spec((1, dm))],
        out_specs=[tile, tile, tile],
        out_shape=[jax.ShapeDtypeStruct((tokens, dm), F32), jax.ShapeDtypeStruct((tokens, dm), F32),
                   jax.ShapeDtypeStruct((tokens, dm), BF16)],
        args=(x, a, gt, wd, ln_g, ln_b))


def ffn_bwd(dr, df, x, gu, sc, wgu, wd, seq, name, job=None):
    tokens, dm = x.shape
    ff = wgu.shape[1]
    chunks = _hidden_chunks(ff)
    tm = min(FFN_WIDE_TILE, seq)
    tiles_per_seq = seq // tm
    nseq = tokens // seq

    def body(dr_ref, df_ref, x_ref, gu_ref, sc_ref, wgu_ref, wd_ref, dx_ref, dgu_ref, dmod_ref):
        @pl.when(pl.program_id(0) % tiles_per_seq == 0)
        def _():
            dmod_ref[...] = jnp.zeros_like(dmod_ref)

        df = df_ref[...]
        dh = jnp.zeros((tm, dm), F32)
        for at, wdt in chunks:
            cols = slice(at, at + wdt)
            da = _dot_nt(df, wd_ref[cols, :])
            gk = gu_ref[0, :, cols].astype(F32)
            uk = gu_ref[1, :, cols].astype(F32)
            sg = _sigmoid(gk)
            sil = gk * sg
            du = (da * sil).astype(BF16)
            dg = (da * uk * (sg * (1.0 + gk * (1.0 - sg)))).astype(BF16)
            dgu_ref[0, :, cols] = dg
            dgu_ref[1, :, cols] = du
            dh = dh + _dot(dg, wgu_ref[0, cols, :]) + _dot(du, wgu_ref[1, cols, :])
        dx_ref[...] = DN_ALPHA * dr_ref[...] + dh * (1.0 + sc_ref[0])
        dmod_ref[0, 0:1, :] += jnp.sum(dh, axis=0, keepdims=True)
        dmod_ref[0, 1:2, :] += jnp.sum(dh * x_ref[...], axis=0, keepdims=True)

    tile = pl.BlockSpec((tm, dm), lambda i: (i, 0))
    gu_spec = pl.BlockSpec((2, tm, ff), lambda i: (0, i, 0))
    return _call(
        body, job, name=name, grid=(tokens // tm,),
        in_specs=[tile, tile, tile, gu_spec, _mod_spec(tiles_per_seq, dm), _const_spec(wgu.shape), _const_spec(wd.shape)],
        out_specs=[tile, gu_spec, pl.BlockSpec((1, 2, dm), lambda i: (i // tiles_per_seq, 0, 0))],
        out_shape=[jax.ShapeDtypeStruct((tokens, dm), F32), jax.ShapeDtypeStruct((2, tokens, ff), BF16),
                   jax.ShapeDtypeStruct((nseq, 2, dm), F32)],
        args=(dr, df, x, gu, sc, wgu, wd), vmem=FFN_WIDE_VMEM)


def tn_matmul(a, b, name, job=None, b_cols=None, a_width=None):
    na, tokens, k_all = a.shape
    kk = k_all if a_width is None else a_width
    nka = k_all // kk
    assert nka * kk == k_all
    nb, _, cc = b.shape
    col = 0
    if b_cols is not None:
        col, cc = b_cols
    tt = tokens
    while 4 * tt * (kk + cc) + 8 * kk * cc > TN_VMEM_BUDGET and tt % 2 == 0 and tt > 256:
        tt //= 2
    steps = tokens // tt

    def body(a_ref, b_ref, o_ref, *acc):
        if steps == 1:
            o_ref[0, 0, 0] = _dot_tn(a_ref[0], b_ref[0]).astype(BF16)
            return
        acc_ref, = acc
        t = pl.program_id(3)

        @pl.when(t == 0)
        def _():
            acc_ref[...] = jnp.zeros_like(acc_ref)

        acc_ref[...] += _dot_tn(a_ref[0], b_ref[0])

        @pl.when(t == steps - 1)
        def _():
            o_ref[0, 0, 0] = acc_ref[...].astype(BF16)

    return _call(
        body, job, name=name, grid=(na, nka, nb, steps),
        in_specs=[pl.BlockSpec((1, tt, kk), lambda i, s, j, t: (i, t, s)),
                  pl.BlockSpec((1, tt, cc), lambda i, s, j, t: (j, t, col))],
        out_specs=[pl.BlockSpec((1, 1, 1, kk, cc), lambda i, s, j, t: (i, s, j, 0, 0))],
        out_shape=[jax.ShapeDtypeStruct((na, nka, nb, kk, cc), BF16)],
        scratch_shapes=[] if steps == 1 else [pltpu.VMEM((kk, cc), F32)], args=(a, b))


def proj_fwd(x1, sh, sc, w_in, seq, name, job=None):
    tokens, dm = x1.shape
    tm = min(MIX_TILE, seq)
    tiles_per_seq = seq // tm
    widths = [N_Q_HEADS * HEAD_DIM, N_KV_HEADS * HEAD_DIM, N_KV_HEADS * HEAD_DIM, 512, 512, 512]
    assert sum(widths) == w_in.shape[0]

    def body(x_ref, sh_ref, sc_ref, w_ref, *outs):
        h = (x_ref[...] * (1.0 + sc_ref[0]) + sh_ref[0]).astype(BF16)
        proj = _dot_nt(h, w_ref[...])
        at = 0
        for o_ref, wdt in zip(outs, widths):
            o_ref[...] = proj[:, at:at + wdt].astype(o_ref.dtype)
            at += wdt

    tile = pl.BlockSpec((tm, dm), lambda i: (i, 0))
    mod = _mod_spec(tiles_per_seq, dm)
    return _call(
        body, job, name=name, grid=(tokens // tm,),
        in_specs=[tile, mod, mod, _const_spec(w_in.shape)],
        out_specs=[pl.BlockSpec((tm, wdt), lambda i: (i, 0)) for wdt in widths],
        out_shape=[jax.ShapeDtypeStruct((tokens, wdt), F32 if i < 3 else BF16) for i, wdt in enumerate(widths)],
        args=(x1, sh, sc, w_in))


LANES = 2 * HEAD_DIM


def _head_lane(shape):
    return lax.broadcasted_iota(jnp.int32, shape, 1) % HEAD_DIM


def _lane_half(shape):
    return lax.broadcasted_iota(jnp.int32, shape, 1) // HEAD_DIM


def _swap_rot(v):
    lane = _head_lane(v.shape)
    half = ROT_DIM // 2
    return jnp.where(lane < half, pltpu.roll(v, LANES - half, 1),
                     jnp.where(lane < ROT_DIM, pltpu.roll(v, half, 1), 0.0))


def _rope(v, cos_t, sin_t):
    return v * cos_t + _swap_rot(v) * sin_t


def _unrope(dv, cos_t, sin_t):
    return dv * cos_t + _swap_rot(dv * sin_t)


def _both_halves(t, g):
    return jnp.where(_lane_half(t.shape) == g, t, pltpu.roll(t, HEAD_DIM, 1))


def _fold_halves(t, g):
    return jnp.where(_lane_half(t.shape) == g, t + pltpu.roll(t, HEAD_DIM, 1), 0.0)


def _stack_heads(blocks):
    rows = []
    for blk in blocks:
        half = _lane_half(blk.shape)
        rows += [jnp.where(half == 0, blk, 0.0), jnp.where(half == 1, blk, 0.0)]
    return jnp.concatenate(rows, axis=0)


def _unstack_heads(t, j):
    lo = t[(2 * j) * ATTN_BLOCK:(2 * j + 1) * ATTN_BLOCK]
    hi = t[(2 * j + 1) * ATTN_BLOCK:(2 * j + 2) * ATTN_BLOCK]
    return jnp.where(_lane_half(lo.shape) == 0, lo, hi)


def _band_mask(q0, w0):
    rows, cols = GQA_GROUP * ATTN_BLOCK, 2 * ATTN_BLOCK
    qi = lax.broadcasted_iota(jnp.int32, (rows, cols), 0) % ATTN_BLOCK + q0
    ki = lax.broadcasted_iota(jnp.int32, (rows, cols), 1) + w0
    diff = qi - ki
    return (diff >= 0) & (diff < ATTN_BLOCK)


def _attn_specs(seq):
    q_spec = pl.BlockSpec((seq, GQA_GROUP * HEAD_DIM), lambda b, g: (b, g))
    kv_spec = pl.BlockSpec((seq, LANES), lambda b, g: (b, 0))
    sink_spec = pl.BlockSpec((1, GQA_GROUP * ATTN_BLOCK, 1), lambda b, g: (g, 0, 0))
    return q_spec, kv_spec, sink_spec


def _block_starts(n):
    q0 = pl.multiple_of(n * ATTN_BLOCK, ATTN_BLOCK)
    w0 = pl.multiple_of(jnp.maximum(n - 1, 0) * ATTN_BLOCK, ATTN_BLOCK)
    return q0, w0


def _stacked_queries(ref, rows):
    return _stack_heads([ref[rows, j * LANES:(j + 1) * LANES] for j in range(2)]).astype(BF16)


def _sink_columns(sinks):
    return jnp.repeat(sinks.reshape(N_KV_HEADS, GQA_GROUP), ATTN_BLOCK, axis=1)[:, :, None]


def _probs_spec(nblk):
    return pl.BlockSpec((1, 1, nblk, GQA_GROUP * ATTN_BLOCK, 2 * ATTN_BLOCK), lambda b, g: (b, g, 0, 0, 0))


def _sink_probs_spec():
    return pl.BlockSpec((1, 1, GQA_GROUP * ATTN_BLOCK, LANES), lambda b, g: (b, g, 0, 0))


def attn_fwd(q, k, v, cos_t, sin_t, sinks, seq, name, job=None):
    tokens = q.shape[0]
    nblk = seq // ATTN_BLOCK
    assert nblk >= 2
    scale = HEAD_DIM ** -0.5

    nseq = tokens // seq
    rows_stacked = GQA_GROUP * ATTN_BLOCK
    assert nblk <= LANES

    def body(q_ref, k_ref, v_ref, cos_ref, sin_ref, sink_ref, o_ref, qr_ref, p_ref, ps_ref, kd_ref, vd_ref):
        g = pl.program_id(1)
        kd_ref[...] = _both_halves(_rope(k_ref[...].astype(F32), cos_ref[...], sin_ref[...]), g).astype(BF16)
        vd_ref[...] = _both_halves(v_ref[...].astype(F32), g).astype(BF16)
        sink = sink_ref[0]
        lane = lax.broadcasted_iota(jnp.int32, (rows_stacked, LANES), 1)

        ps_ref[...] = jnp.zeros_like(ps_ref)

        def block(n, carry):
            q0, w0 = _block_starts(n)
            rows, win = pl.ds(q0, ATTN_BLOCK), pl.ds(w0, 2 * ATTN_BLOCK)
            blocks = []
            for j in range(2):
                qr = _rope(q_ref[rows, j * LANES:(j + 1) * LANES].astype(F32), cos_ref[rows, :], sin_ref[rows, :]).astype(BF16)
                qr_ref[rows, j * LANES:(j + 1) * LANES] = qr
                blocks.append(qr)
            qs = _stack_heads(blocks)
            s = _dot_nt(qs, kd_ref[win, :]) * scale
            s = jnp.where(_band_mask(q0, w0), s, NEG_BIG)
            m = jnp.maximum(jnp.max(s, axis=-1, keepdims=True), sink)
            p = jnp.exp(s - m)
            e_sink = jnp.exp(sink - m)
            inv = pl.reciprocal(jnp.sum(p, axis=-1, keepdims=True) + e_sink, approx=True)
            pn = (p * inv).astype(BF16)
            p_ref[0, 0, n] = pn
            out = _dot(pn, vd_ref[win, :])
            for j in range(2):
                o_ref[rows, j * LANES:(j + 1) * LANES] = _unstack_heads(out, j).astype(o_ref.dtype)
            ps_ref[0, 0] = jnp.where(lane == n, e_sink * inv, ps_ref[0, 0])
            return carry

        lax.fori_loop(0, nblk, block, 0, unroll=2)

    q_spec, kv_spec, sink_spec = _attn_specs(seq)
    return _call(
        body, job, name=name, grid=(nseq, N_KV_HEADS),
        in_specs=[q_spec, kv_spec, kv_spec, kv_spec, kv_spec, sink_spec],
        out_specs=[q_spec, q_spec, _probs_spec(nblk), _sink_probs_spec()],
        out_shape=[jax.ShapeDtypeStruct(q.shape, BF16), jax.ShapeDtypeStruct(q.shape, BF16),
                   jax.ShapeDtypeStruct((nseq, N_KV_HEADS, nblk, rows_stacked, 2 * ATTN_BLOCK), BF16),
                   jax.ShapeDtypeStruct((nseq, N_KV_HEADS, rows_stacked, LANES), F32)],
        scratch_shapes=[pltpu.VMEM((seq, LANES), BF16), pltpu.VMEM((seq, LANES), BF16)],
        args=(q, k, v, cos_t, sin_t, _sink_columns(sinks)))


def attn_bwd(qr, k, v, do, probs, sink_probs, cos_t, sin_t, seq, name, job=None):
    tokens = qr.shape[0]
    nseq = tokens // seq
    nblk = seq // ATTN_BLOCK
    assert nblk >= 2
    rows_stacked = GQA_GROUP * ATTN_BLOCK
    scale = HEAD_DIM ** -0.5

    def body(q_ref, k_ref, v_ref, do_ref, p_ref, ps_ref, cos_ref, sin_ref, dq_ref, dk_ref, dv_ref, ds_ref,
             kd_ref, vd_ref, dkd_ref, dvd_ref, acc_ref):
        g = pl.program_id(1)
        kd_ref[...] = _both_halves(_rope(k_ref[...].astype(F32), cos_ref[...], sin_ref[...]), g).astype(BF16)
        vd_ref[...] = _both_halves(v_ref[...].astype(F32), g).astype(BF16)
        dkd_ref[...] = jnp.zeros_like(dkd_ref)
        dvd_ref[...] = jnp.zeros_like(dvd_ref)
        acc_ref[...] = jnp.zeros_like(acc_ref)
        lane = lax.broadcasted_iota(jnp.int32, (rows_stacked, LANES), 1)

        def block(n, carry):
            q0, w0 = _block_starts(n)
            rows, win = pl.ds(q0, ATTN_BLOCK), pl.ds(w0, 2 * ATTN_BLOCK)
            qs = _stacked_queries(q_ref, rows)
            dos = _stacked_queries(do_ref, rows)
            kw, vw = kd_ref[win, :], vd_ref[win, :]
            pn16 = p_ref[0, 0, n]
            pn = pn16.astype(F32)
            dvd_ref[win, :] += _dot_tn(pn16, dos)
            dp = _dot_nt(dos, vw)
            delta = jnp.sum(dp * pn, axis=-1, keepdims=True)
            ds = (pn * (dp - delta)).astype(BF16)
            dqs = _dot(ds, kw) * scale
            dkd_ref[win, :] += _dot_tn(ds, qs) * scale
            cos_b, sin_b = cos_ref[rows, :], sin_ref[rows, :]
            for j in range(2):
                dq_ref[rows, j * LANES:(j + 1) * LANES] = _unrope(_unstack_heads(dqs, j), cos_b, sin_b).astype(BF16)
            acc_ref[...] += jnp.where(lane == n, ps_ref[0, 0] * delta, 0.0)
            return carry

        lax.fori_loop(0, nblk // 2, lambda i, carry: block(2 * i + 1, block(2 * i, carry)), 0)
        ds_ref[0, 0] = -jnp.sum(acc_ref[...], axis=-1, keepdims=True)
        dk_g = _unrope(_fold_halves(dkd_ref[...], g), cos_ref[...], sin_ref[...])
        dv_g = _fold_halves(dvd_ref[...], g)

        @pl.when(g == 0)
        def _():
            dk_ref[...] = dk_g
            dv_ref[...] = dv_g

        @pl.when(g != 0)
        def _():
            dk_ref[...] += dk_g
            dv_ref[...] += dv_g

    q_spec, kv_spec, _ = _attn_specs(seq)
    return _call(
        body, job, name=name, grid=(nseq, N_KV_HEADS),
        in_specs=[q_spec, kv_spec, kv_spec, q_spec, _probs_spec(nblk), _sink_probs_spec(), kv_spec, kv_spec],
        out_specs=[q_spec, kv_spec, kv_spec, pl.BlockSpec((1, 1, rows_stacked, 1), lambda b, g: (b, g, 0, 0))],
        out_shape=[jax.ShapeDtypeStruct(qr.shape, BF16), jax.ShapeDtypeStruct(k.shape, F32),
                   jax.ShapeDtypeStruct(k.shape, F32), jax.ShapeDtypeStruct((nseq, N_KV_HEADS, rows_stacked, 1), F32)],
        scratch_shapes=[pltpu.VMEM((seq, LANES), BF16), pltpu.VMEM((seq, LANES), BF16),
                        pltpu.VMEM((seq, LANES), F32), pltpu.VMEM((seq, LANES), F32),
                        pltpu.VMEM((rows_stacked, LANES), F32)],
        args=(qr, k, v, do, probs, sink_probs, cos_t, sin_t))


CONV_COLS = 128


def _shift_down(z, by):
    t = lax.broadcasted_iota(jnp.int32, z.shape, 0)
    return jnp.where(t >= by, pltpu.roll(z, by, 0), 0.0)


def _shift_up(z, by):
    n = z.shape[0]
    t = lax.broadcasted_iota(jnp.int32, z.shape, 0)
    return jnp.where(t < n - by, pltpu.roll(z, n - by, 0), 0.0)


def conv_fwd(u, bg, cg, conv_w, seq, name):
    tokens, width = u.shape

    def body(u_ref, bg_ref, cg_ref, w_ref, o_ref):
        z = cg_ref[...].astype(F32) * u_ref[...].astype(F32)
        yy = w_ref[2:3, :] * z + w_ref[1:2, :] * _shift_down(z, 1) + w_ref[0:1, :] * _shift_down(z, 2)
        o_ref[...] = (bg_ref[...].astype(F32) * yy).astype(BF16)

    col = pl.BlockSpec((seq, CONV_COLS), lambda j, b: (b, j))
    return pl.pallas_call(
        body, name=name, grid=(width // CONV_COLS, tokens // seq),
        in_specs=[col, col, col, pl.BlockSpec((CONV_TAPS, CONV_COLS), lambda j, b: (0, j))],
        out_specs=col, out_shape=jax.ShapeDtypeStruct((tokens, width), BF16),
        compiler_params=_params(("parallel", "parallel")),
    )(u, bg, cg, conv_w)


def conv_bwd(dout, u, bg, cg, conv_w, seq, name):
    tokens, width = u.shape

    def body(do_ref, u_ref, bg_ref, cg_ref, w_ref, du_ref, dbg_ref, dcg_ref, dw_ref):
        uu, cg_v, do = u_ref[...].astype(F32), cg_ref[...].astype(F32), do_ref[...].astype(F32)
        z = cg_v * uu
        z1, z2 = _shift_down(z, 1), _shift_down(z, 2)
        yy = w_ref[2:3, :] * z + w_ref[1:2, :] * z1 + w_ref[0:1, :] * z2
        dbg_ref[...] = (do * yy).astype(BF16)
        dyy = do * bg_ref[...].astype(F32)
        dz = w_ref[2:3, :] * dyy + w_ref[1:2, :] * _shift_up(dyy, 1) + w_ref[0:1, :] * _shift_up(dyy, 2)
        du_ref[...] = (dz * cg_v).astype(BF16)
        dcg_ref[...] = (dz * uu).astype(BF16)

        @pl.when(pl.program_id(1) == 0)
        def _():
            dw_ref[...] = jnp.zeros_like(dw_ref)

        dw_ref[0:1, :] += jnp.sum(dyy * z2, axis=0, keepdims=True)
        dw_ref[1:2, :] += jnp.sum(dyy * z1, axis=0, keepdims=True)
        dw_ref[2:3, :] += jnp.sum(dyy * z, axis=0, keepdims=True)

    col = pl.BlockSpec((seq, CONV_COLS), lambda j, b: (b, j))
    w_spec = pl.BlockSpec((CONV_TAPS, CONV_COLS), lambda j, b: (0, j))
    act = jax.ShapeDtypeStruct((tokens, width), BF16)
    return pl.pallas_call(
        body, name=name, grid=(width // CONV_COLS, tokens // seq),
        in_specs=[col, col, col, col, w_spec], out_specs=[col, col, col, w_spec],
        out_shape=[act, act, act, jax.ShapeDtypeStruct((CONV_TAPS, width), F32)],
        compiler_params=_params(("parallel", "arbitrary")),
    )(dout, u, bg, cg, conv_w)


def out_fwd(x1, attn, conv, gt, w_out, ln_g, ln_b, seq, name, job=None):
    tokens, dm = x1.shape
    half = attn.shape[1]
    tm = min(MIX_TILE, seq)
    tiles_per_seq = seq // tm

    def body(x_ref, a_ref, c_ref, gt_ref, w_ref, lg_ref, lb_ref, xo_ref, r_ref, mi_ref, mix_ref):
        mixin = jnp.concatenate([a_ref[...], c_ref[...]], axis=1).astype(BF16)
        mi_ref[...] = mixin
        mix = _dot(mixin, w_ref[...])
        mix_ref[...] = mix.astype(BF16)
        r = DN_ALPHA * x_ref[...] + (1.0 + gt_ref[0]) * mix
        r_ref[...] = r
        xhat, _ = _ln_stats(r)
        xo_ref[...] = xhat * lg_ref[...] + lb_ref[...]

    tile = pl.BlockSpec((tm, dm), lambda i: (i, 0))
    htile = pl.BlockSpec((tm, half), lambda i: (i, 0))
    return _call(
        body, job, name=name, grid=(tokens // tm,),
        in_specs=[tile, htile, htile, _mod_spec(tiles_per_seq, dm), _const_spec(w_out.shape),
                  _const_spec((1, dm)), _const_spec((1, dm))],
        out_specs=[tile, tile, tile, tile],
        out_shape=[jax.ShapeDtypeStruct((tokens, dm), F32), jax.ShapeDtypeStruct((tokens, dm), F32),
                   jax.ShapeDtypeStruct((tokens, dm), BF16), jax.ShapeDtypeStruct((tokens, dm), BF16)],
        args=(x1, attn, conv, gt, w_out, ln_g, ln_b))


def out_bwd(dy, r, mix, gt, w_out, ln_g, seq, name, job=None):
    tokens, dm = r.shape
    half = dm // 2
    tm = min(MIX_TILE, seq)
    tiles_per_seq = seq // tm
    nseq = tokens // seq

    def body(dy_ref, r_ref, mix_ref, gt_ref, w_ref, lg_ref, dres_ref, da_ref, dc_ref, dmix_ref, dln_ref, dgt_ref):
        i = pl.program_id(0)
        dr, dgain, dbias = _ln_bwd(dy_ref[...], r_ref[...], lg_ref[...])

        @pl.when(i == 0)
        def _():
            dln_ref[...] = jnp.zeros_like(dln_ref)

        @pl.when(i % tiles_per_seq == 0)
        def _():
            dgt_ref[...] = jnp.zeros_like(dgt_ref)

        dln_ref[0:1, :] += dgain
        dln_ref[1:2, :] += dbias
        dgt_ref[0] += jnp.sum(dr * mix_ref[...].astype(F32), axis=0, keepdims=True)
        dres_ref[...] = DN_ALPHA * dr
        dmix = ((1.0 + gt_ref[0]) * dr).astype(BF16)
        dmix_ref[...] = dmix
        dmixin = _dot_nt(dmix, w_ref[...])
        da_ref[...] = dmixin[:, :half].astype(BF16)
        dc_ref[...] = dmixin[:, half:].astype(BF16)

    tile = pl.BlockSpec((tm, dm), lambda i: (i, 0))
    htile = pl.BlockSpec((tm, half), lambda i: (i, 0))
    return _call(
        body, job, name=name, grid=(tokens // tm,),
        in_specs=[tile, tile, tile, _mod_spec(tiles_per_seq, dm), _const_spec(w_out.shape), _const_spec((1, dm))],
        out_specs=[tile, htile, htile, tile, pl.BlockSpec((2, dm), lambda i: (0, 0)),
                   pl.BlockSpec((1, 1, dm), lambda i: (i // tiles_per_seq, 0, 0))],
        out_shape=[jax.ShapeDtypeStruct((tokens, dm), F32), jax.ShapeDtypeStruct((tokens, half), BF16),
                   jax.ShapeDtypeStruct((tokens, half), BF16), jax.ShapeDtypeStruct((tokens, dm), BF16),
                   jax.ShapeDtypeStruct((2, dm), F32), jax.ShapeDtypeStruct((nseq, 1, dm), F32)],
        args=(dy, r, mix, gt, w_out, ln_g))


def proj_bwd(parts, dres, x1, sh, sc, w_in, r_prev, f_prev, gt_prev, ln_g_prev, seq, name, job=None):
    tokens, dm = x1.shape
    tm = min(MIX_TILE, seq)
    tiles_per_seq = seq // tm
    nseq = tokens // seq
    widths = [p.shape[1] for p in parts]
    total = sum(widths)

    def body(*refs):
        part_refs = refs[:6]
        (dres_ref, x_ref, sh_ref, sc_ref, w_ref, r_ref, f_ref, gt_ref, lg_ref,
         dr_ref, df_ref, dproj_ref, h_ref, dmod_ref, dln_ref, dgt_ref) = refs[6:]
        i = pl.program_id(0)
        dproj = jnp.concatenate([p[...].astype(BF16) for p in part_refs], axis=1)
        dproj_ref[...] = dproj
        dh = _dot(dproj, w_ref[...])
        xx = x_ref[...]
        one_sc = 1.0 + sc_ref[0]
        h_ref[...] = (xx * one_sc + sh_ref[0]).astype(BF16)
        dr, dgain, dbias = _ln_bwd(dres_ref[...] + dh * one_sc, r_ref[...], lg_ref[...])
        dr_ref[...] = dr
        df_ref[...] = ((0.5 * (1.0 + gt_ref[0])) * dr).astype(BF16)

        @pl.when(i == 0)
        def _():
            dln_ref[...] = jnp.zeros_like(dln_ref)

        @pl.when(i % tiles_per_seq == 0)
        def _():
            dmod_ref[...] = jnp.zeros_like(dmod_ref)
            dgt_ref[...] = jnp.zeros_like(dgt_ref)

        dmod_ref[0, 0:1, :] += jnp.sum(dh, axis=0, keepdims=True)
        dmod_ref[0, 1:2, :] += jnp.sum(dh * xx, axis=0, keepdims=True)
        dln_ref[0:1, :] += dgain
        dln_ref[1:2, :] += dbias
        dgt_ref[0] += jnp.sum(dr * (0.5 * f_ref[...].astype(F32)), axis=0, keepdims=True)

    tile = pl.BlockSpec((tm, dm), lambda i: (i, 0))
    mod = _mod_spec(tiles_per_seq, dm)
    return _call(
        body, job, name=name, grid=(tokens // tm,),
        in_specs=[pl.BlockSpec((tm, wdt), lambda i: (i, 0)) for wdt in widths]
        + [tile, tile, mod, mod, _const_spec(w_in.shape), tile, tile, mod, _const_spec((1, dm))],
        out_specs=[tile, tile, pl.BlockSpec((tm, total), lambda i: (i, 0)), tile,
                   pl.BlockSpec((1, 2, dm), lambda i: (i // tiles_per_seq, 0, 0)),
                   pl.BlockSpec((2, dm), lambda i: (0, 0)), mod],
        out_shape=[jax.ShapeDtypeStruct((tokens, dm), F32), jax.ShapeDtypeStruct((tokens, dm), BF16),
                   jax.ShapeDtypeStruct((tokens, total), BF16), jax.ShapeDtypeStruct((tokens, dm), BF16),
                   jax.ShapeDtypeStruct((nseq, 2, dm), F32), jax.ShapeDtypeStruct((2, dm), F32),
                   jax.ShapeDtypeStruct((nseq, 1, dm), F32)],
        args=(*parts, dres, x1, sh, sc, w_in, r_prev, f_prev, gt_prev, ln_g_prev))


def _rope_tables(positions):
    half = ROT_DIM // 2
    inv_freq = np.power(np.float32(ROPE_THETA), -np.arange(0, ROT_DIM, 2, dtype=np.float32) / ROT_DIM)
    lane = np.arange(LANES) % HEAD_DIM
    freq = np.where(lane < ROT_DIM, inv_freq[lane % half], 0.0).astype(np.float32)
    sign = np.where(lane < half, -1.0, 1.0).astype(np.float32)
    ang = positions.astype(F32)[:, None] * freq[None, :]
    return jnp.cos(ang), sign[None, :] * jnp.sin(ang)


def kernel(x, c, positions, w_ada, b_ada, ffn1_w_gate_up, ffn1_w_down, ln1_g, ln1_b, w_in, conv_w, attn_sinks, w_out, ln2_g, ln2_b, ffn2_w_gate_up, ffn2_w_down, ln3_g, ln3_b, loss_target, m_w_ada, m_b_ada, m_ffn1_w_gate_up, m_ffn1_w_down, m_ln1_g, m_ln1_b, m_w_in, m_conv_w, m_attn_sinks, m_w_out, m_ln2_g, m_ln2_b, m_ffn2_w_gate_up, m_ffn2_w_down, m_ln3_g, m_ln3_b, v_w_ada, v_b_ada, v_ffn1_w_gate_up, v_ffn1_w_down, v_ln1_g, v_ln1_b, v_w_in, v_conv_w, v_attn_sinks, v_w_out, v_ln2_g, v_ln2_b, v_ffn2_w_gate_up, v_ffn2_w_down, v_ln3_g, v_ln3_b):
    nseq, seq, dm = x.shape
    tokens = nseq * seq
    dev = 4 * lax.axis_index("x") + 2 * lax.axis_index("y") + lax.axis_index("c")
    core = lax.axis_index("c").astype(jnp.int32).reshape(1)
    ada_cols = w_ada.shape[2]
    ff = ffn1_w_down.shape[1] * N_DEV
    fc = ff // 4
    in_cols = w_in.shape[2]
    conv_cols = conv_w.shape[2]

    def t_bf16(w):
        return w[0].T.astype(BF16)

    c_all, convw_all = all_gather([c, conv_w[0]], "gather_cond")
    c_all = c_all.reshape(N_DEV * nseq, dm)
    convw_full = convw_all.transpose(1, 0, 2).reshape(CONV_TAPS, N_DEV * conv_cols)

    cond_all, mod_cols = ada_fwd(dev.astype(jnp.int32).reshape(1), c_all, w_ada[0], b_ada, "ada_fwd")
    wgu1, mod_all = all_gather([t_bf16(ffn1_w_gate_up), mod_cols], "gather_ffn1")
    wgu1 = wgu1.reshape(2, ff, dm)
    mod = lax.dynamic_slice(mod_all, (0, dev * nseq, 0), (N_DEV, nseq, ada_cols))
    mod = mod.transpose(1, 0, 2).reshape(nseq, 9, 1, dm)
    sh1, sc1, g1, sh2, sc2, g2, sh3, sc3, g3 = [mod[:, i] for i in range(9)]

    x0 = x.reshape(tokens, dm)
    (gu1, a1, h1), (wd1, wout) = ffn_up(x0, sh1, sc1, wgu1, seq, "ffn1_up",
                                        job=_GatherJob([ffn1_w_down[0].astype(BF16), w_out[0].astype(BF16)]))
    wd1, wout = wd1.reshape(ff, dm), wout.reshape(dm, dm)
    (x1, r1, f1), (win,) = ffn_down(x0, a1, g1, wd1, ln1_g, ln1_b, seq, "ffn1_down", job=_GatherJob([t_bf16(w_in)]))
    win = win.reshape(N_DEV * in_cols, dm)
    (q, k, v, u, bg, cg), wd2_spread = proj_fwd(x1, sh2, sc2, win, seq, "proj_fwd",
                                                job=gather_spread_job([ffn2_w_down[0].astype(BF16)]))
    cos_t, sin_t = _rope_tables(positions.reshape(tokens))
    sinks = attn_sinks[0]
    (attn, q_rot, probs, sink_probs), wgu2_spread = attn_fwd(q, k, v, cos_t, sin_t, sinks, seq, "attn_fwd",
                                                             job=gather_spread_job([t_bf16(ffn2_w_gate_up)]))
    conv = conv_fwd(u, bg, cg, convw_full, seq, "conv_fwd")
    (x2, r2, mixin, mix), (wd2, wgu2) = out_fwd(x1, attn, conv, g2, wout, ln2_g, ln2_b, seq, "out_fwd",
                                                job=gather_forward_job(wd2_spread + wgu2_spread))
    wd2, wgu2 = wd2.reshape(ff, dm), wgu2.reshape(2, ff, dm)
    target = loss_target.reshape(tokens, dm)
    dr3, df3, gu3, a3, h3, loss_part, dln3, dg3 = ffn_loss(x2, sh3, sc3, g3, wgu2, wd2, ln3_g, ln3_b, target, seq, "ffn2_fwd")

    (dx2, dgu3, dmod3), _ = ffn_bwd(dr3, df3, x2, gu3, sc3, wgu2, wd2, seq, "ffn2_bwd")
    pair = 2 * fc
    g_wd2 = tn_matmul(a3[None], df3[None], "ffn2_dwd", a_width=pair)[0][0].reshape(N_DEV, ff // N_DEV, dm)
    g_wgu2 = tn_matmul(dgu3, h3[None], "ffn2_dwgu", a_width=pair)[0][0].reshape(N_DEV, fc, dm)
    (dres2, dattn, dconv, dmix, dln2, dg2), swapped = out_bwd(dx2, r2, mix, g2, wout, ln2_g, seq, "out_bwd",
                                                              job=swap_job([g_wgu2, g_wd2]))
    p_wgu2, own_wgu2 = pair_sum(core, g_wgu2, swapped[0], "pair_wgu2")
    p_wd2, own_wd2 = pair_sum(core, g_wd2, swapped[1], "pair_wd2")
    du, dbg, dcg, dconvw = conv_bwd(dconv, u, bg, cg, convw_full, seq, "conv_bwd")
    (dq, dk, dv, dsink_rows), (far_wd2,) = attn_bwd(
        q_rot, k, v, dattn, probs, sink_probs, cos_t, sin_t, seq, "attn_bwd", job=chip_exchange_job([p_wd2]))
    parts = [dq, dk, dv, du, dbg, dcg]
    (dr1, df1, dproj, h2, dmod2, dln1, dg1), far_top = proj_bwd(
        parts, dres2, x1, sh2, sc2, win, r1, f1, g1, ln1_g, seq, "proj_bwd",
        job=chip_exchange_job([p_wgu2], rows=(0, fc // 2)))
    (dx0, dgu1, dmod1), _ = ffn_bwd(dr1, df1, x0, gu1, sc1, wgu1, wd1, seq, "ffn1_bwd")

    dmod = jnp.concatenate([dmod1, dg1, dmod2, dg2, dmod3, dg3], axis=1).reshape(nseq, 9 * dm)
    half = dm // 2
    jobs = _Jobs([gather_spread_job([dmod]),
                  chip_exchange_job([p_wgu2], rows=(fc // 2, fc // 2), into=far_top)])
    (g_wd1,), res = tn_matmul(a1[None], df1[None], "ffn1_dwd", job=jobs, a_width=pair)
    dmod_spread, (far_wgu2,) = jobs.split(res)
    g_wd1 = g_wd1.reshape(N_DEV, ff // N_DEV, dm)
    jobs = _Jobs([swap_job([g_wd1]), gather_forward_job(dmod_spread)])
    (g_l,), res = tn_matmul(dgu1, h1[None], "ffn1_dwgu_l", job=jobs, b_cols=(0, half), a_width=pair)
    (sw_wd1,), (dmod_all,) = jobs.split(res)
    g_l = g_l.reshape(N_DEV, fc, half)
    p_wd1, own_wd1 = pair_sum(core, g_wd1, sw_wd1, "pair_wd1")
    jobs = _Jobs([chip_exchange_job([p_wd1]), swap_job([g_l])])
    (g_r,), res = tn_matmul(dgu1, h1[None], "ffn1_dwgu_r", job=jobs, b_cols=(1, half), a_width=pair)
    (far_wd1,), (sw_l,) = jobs.split(res)
    g_r = g_r.reshape(N_DEV, fc, half)
    p_l, own_l = pair_sum(core, g_l, sw_l, "pair_wgu1_l")

    dmod_cols = lax.dynamic_slice(dmod_all.reshape(N_DEV * nseq, 9 * dm), (0, dev * ada_cols), (N_DEV * nseq, ada_cols))
    grad_w_ada, gb_cols = ada_bwd(cond_all, dmod_cols, "ada_bwd")
    dsinks = jnp.sum(dsink_rows.reshape(nseq, N_Q_HEADS, ATTN_BLOCK), axis=(0, 2))
    small = jnp.zeros((8, dm), F32)
    small = small.at[0:2].set(dln1).at[2:4].set(dln2).at[4:6].set(dln3)
    small = small.at[6, 0:N_Q_HEADS].set(dsinks).at[7, 0].set(loss_part[0, 0])

    jobs = _Jobs([chip_exchange_job([p_l]), swap_job([g_r]), gather_spread_job([small, dconvw, gb_cols])])
    (g_win,), res = tn_matmul(dproj[None], h2[None], "dwin", job=jobs)
    (far_l,), (sw_r,), small_spread = jobs.split(res)
    g_win = g_win.reshape(N_DEV, in_cols, dm)
    p_r, own_r = pair_sum(core, g_r, sw_r, "pair_wgu1_r")
    jobs = _Jobs([chip_exchange_job([p_r]), swap_job([g_win]), gather_forward_job(small_spread)])
    (g_wout,), res = tn_matmul(mixin[None], dmix[None], "dwout", job=jobs)
    (far_r,), (sw_win,), (small_all, dconvw_all, gb_all) = jobs.split(res)
    g_wout = g_wout.reshape(N_DEV, dm // N_DEV, dm)
    p_win, own_win = pair_sum(core, g_win, sw_win, "pair_win")

    given = dict(w_ada=(w_ada, m_w_ada, v_w_ada), b_ada=(b_ada, m_b_ada, v_b_ada),
                 ffn1_w_gate_up=(ffn1_w_gate_up, m_ffn1_w_gate_up, v_ffn1_w_gate_up),
                 ffn1_w_down=(ffn1_w_down, m_ffn1_w_down, v_ffn1_w_down),
                 ln1_g=(ln1_g, m_ln1_g, v_ln1_g), ln1_b=(ln1_b, m_ln1_b, v_ln1_b),
                 w_in=(w_in, m_w_in, v_w_in), conv_w=(conv_w, m_conv_w, v_conv_w),
                 attn_sinks=(attn_sinks, m_attn_sinks, v_attn_sinks), w_out=(w_out, m_w_out, v_w_out),
                 ln2_g=(ln2_g, m_ln2_g, v_ln2_g), ln2_b=(ln2_b, m_ln2_b, v_ln2_b),
                 ffn2_w_gate_up=(ffn2_w_gate_up, m_ffn2_w_gate_up, v_ffn2_w_gate_up),
                 ffn2_w_down=(ffn2_w_down, m_ffn2_w_down, v_ffn2_w_down),
                 ln3_g=(ln3_g, m_ln3_g, v_ln3_g), ln3_b=(ln3_b, m_ln3_b, v_ln3_b))
    transposed = ("ffn1_w_gate_up", "ffn2_w_gate_up", "w_in")

    def big_adamw(nm, grad, far=None):
        flip = nm in transposed
        w2, m2, v2 = [t[0].T if flip else t[0] for t in given[nm]]
        return [t.T[None] if flip else t[None] for t in adamw(w2, grad, m2, v2, "adamw_" + nm, others=far)]

    jobs = _Jobs([chip_exchange_job([p_win]), swap_job([g_wout])])
    (far_win,), (sw_wout,) = jobs.split(run_job(jobs, "rs_tail_win"))
    p_wout, own_wout = pair_sum(core, g_wout, sw_wout, "pair_wout")
    (far_wout,) = run_job(chip_exchange_job([p_wout]), "rs_tail_wout")

    grads = {
        "ffn1_w_gate_up": [own_l, own_r], "ffn1_w_down": own_wd1,
        "w_in": own_win, "w_out": own_wout, "ffn2_w_gate_up": own_wgu2, "ffn2_w_down": own_wd2,
    }
    others = {"ffn1_w_gate_up": [far_l, far_r], "ffn1_w_down": far_wd1,
              "w_in": far_win, "w_out": far_wout, "ffn2_w_gate_up": far_wgu2, "ffn2_w_down": far_wd2}
    results = {"w_ada": big_adamw("w_ada", grad_w_ada)}
    for nm in grads:
        results[nm] = big_adamw(nm, grads[nm], others[nm])

    small_rows = ["ln1_g", "ln1_b", "ln2_g", "ln2_b", "ln3_g", "ln3_b", "attn_sinks", "loss"]
    small_sums = sum_devices(small_all, "sum_small", row_widths=[dm] * 6 + [N_Q_HEADS, 1])
    dconvw_sum = sum_devices(dconvw_all, "sum_convw")
    grads.update(zip(small_rows, small_sums))
    loss = grads.pop("loss").reshape(())
    grads["b_ada"] = gb_all.reshape(1, N_DEV * ada_cols)
    grads["conv_w"] = lax.dynamic_slice(dconvw_sum, (0, dev * conv_cols), (CONV_TAPS, conv_cols))

    order = ["w_ada", "b_ada", "ffn1_w_gate_up", "ffn1_w_down", "ln1_g", "ln1_b", "w_in", "conv_w", "attn_sinks",
             "w_out", "ln2_g", "ln2_b", "ffn2_w_gate_up", "ffn2_w_down", "ln3_g", "ln3_b"]
    small_names = [nm for nm in order if nm not in results]
    def rows_first(t):
        return t.transpose(1, 0, 2) if t.ndim == 3 else t

    items = []
    for nm in small_names:
        w, m, v = given[nm]
        grad = grads[nm].reshape(w.shape)
        items.append((rows_first(w), rows_first(grad), rows_first(m), rows_first(v)))
    for nm, res in zip(small_names, adamw_small(items, "adamw_small")):
        results[nm] = [grads[nm].reshape(given[nm][0].shape)] + [rows_first(t) for t in res]
    grad_x = dx0.reshape(nseq, seq, dm)
    return (loss, grad_x, *[results[nm][i] for i in range(4) for nm in order])
```
